```python
import math
import jax, jax.numpy as jnp
from jax import lax
import numpy as np

D_MODEL = 1024
BATCH = 16
SEQ = 2048
DEPTH = 2

EPS = 1e-6
PLE_DIM = 256
D_FF = 2816
CONV_W = 4
HEAD_DIM = 64
LRU_WIDTH = D_MODEL // 4
LRU_BLOCKS = LRU_WIDTH // HEAD_DIM
LRU_BLOCK = LRU_WIDTH // LRU_BLOCKS
LRU_C = 8.0
ATT_WIDTH = D_MODEL // 2
ATT_HEADS = ATT_WIDTH // HEAD_DIM
ATT_KV_HEADS = 2
ATT_GROUP = ATT_HEADS // ATT_KV_HEADS
KV_WIDTH = ATT_KV_HEADS * HEAD_DIM
WINDOW = 128
BLOCK_Q = 128
REL_BUCKETS = 32
REL_MAX_DIST = 128
DN_WIDTH = D_MODEL // 4
DN_HEADS = DN_WIDTH // HEAD_DIM
DN_DK = HEAD_DIM
DN_DV = HEAD_DIM
DN_QK = DN_HEADS * DN_DK
DN_CHUNK = 64
D_MIX = LRU_WIDTH + ATT_WIDTH + DN_WIDTH
IN_SPLITS = (LRU_WIDTH, LRU_WIDTH,
             ATT_WIDTH, KV_WIDTH, KV_WIDTH,
             DN_QK, DN_QK, DN_WIDTH, DN_WIDTH,
             DN_HEADS, DN_HEADS)
D_IN = sum(IN_SPLITS)

kernel_name = "hymba_style_lru_swa_deltanet_macaron"


def rms_norm(x, g):
    xf = x.astype(jnp.float32)
    y = xf * lax.rsqrt(jnp.mean(xf * xf, axis=-1, keepdims=True) + EPS)
    return (y * g.astype(jnp.float32)).astype(x.dtype)


def swiglu(x, w_gate, w_up, w_down):
    return (jax.nn.silu(x @ w_gate) * (x @ w_up)) @ w_down


def causal_dwconv(x, w, b=None):
    K = w.shape[0]
    S = x.shape[1]
    xp = jnp.pad(x, ((0, 0), (K - 1, 0), (0, 0)))
    y = xp[:, 0:S] * w[0]
    for k in range(1, K):
        y = y + xp[:, k:k + S] * w[k]
    if b is not None:
        y = y + b
    return y


def split_points():
    return np.cumsum(np.array(IN_SPLITS))[:-1].tolist()


def rg_lru(x, w_a, b_a, w_x, b_x, lam):
    B, S, _ = x.shape
    xb = x.reshape(B, S, LRU_BLOCKS, LRU_BLOCK)
    r = jax.nn.sigmoid(jnp.einsum('bshi,hij->bshj', xb, w_a).reshape(B, S, LRU_WIDTH) + b_a)
    i = jax.nn.sigmoid(jnp.einsum('bshi,hij->bshj', xb, w_x).reshape(B, S, LRU_WIDTH) + b_x)
    log_a = -LRU_C * r.astype(jnp.float32) * jax.nn.softplus(-lam.astype(jnp.float32))
    a = jnp.exp(log_a)
    u = jnp.sqrt(-jnp.expm1(2.0 * log_a)) * (i * x).astype(jnp.float32)

    def combine(left, right):
        a1, b1 = left
        a2, b2 = right
        return a1 * a2, a2 * b1 + b2

    _, h = lax.associative_scan(combine, (a, u), axis=1)
    return h.astype(x.dtype)


def rel_bucket(dist):
    max_exact = REL_BUCKETS // 2
    large = max_exact + (jnp.log(jnp.maximum(dist, 1).astype(jnp.float32) / max_exact)
                         / math.log(REL_MAX_DIST / max_exact)
                         * (REL_BUCKETS - max_exact)).astype(jnp.int32)
    large = jnp.minimum(large, REL_BUCKETS - 1)
    return jnp.where(dist < max_exact, dist, large)


def swa_attention(q, k, v, sinks, rel_bias):
    B, S = q.shape[:2]
    NB = S // BLOCK_Q
    qb = q.reshape(B, NB, BLOCK_Q, ATT_KV_HEADS, ATT_GROUP, HEAD_DIM)

    def with_prev(t):
        tb = t.reshape(B, NB, BLOCK_Q, ATT_KV_HEADS, HEAD_DIM)
        prev = jnp.pad(tb, ((0, 0), (1, 0), (0, 0), (0, 0), (0, 0)))[:, :-1]
        return jnp.concatenate([prev, tb], axis=2)

    kb, vb = with_prev(k), with_prev(v)
    qi = jnp.arange(BLOCK_Q)[:, None]
    kj = jnp.arange(2 * BLOCK_Q)[None, :]
    dist = BLOCK_Q + qi - kj
    band = (dist >= 0) & (dist < WINDOW)
    blk = jnp.arange(NB)[:, None, None]
    valid = band[None] & ((blk > 0) | (kj[None] >= BLOCK_Q))
    bias = rel_bias.astype(jnp.float32)[rel_bucket(jnp.maximum(dist, 0))]
    bias = bias.transpose(2, 0, 1).reshape(ATT_KV_HEADS, ATT_GROUP, BLOCK_Q, 2 * BLOCK_Q)
    s = jnp.einsum('bnikgd,bnjkd->bnkgij', qb, kb).astype(jnp.float32) * (HEAD_DIM ** -0.5) + bias
    s = jnp.where(valid[None, :, None, None], s, -jnp.inf)
    sink = sinks.astype(jnp.float32).reshape(ATT_KV_HEADS, ATT_GROUP)[:, :, None, None]
    m = jnp.maximum(jnp.max(s, axis=-1, keepdims=True), sink)
    e = jnp.exp(s - m)
    probs = e / (jnp.sum(e, axis=-1, keepdims=True) + jnp.exp(sink - m))
    o = jnp.einsum('bnkgij,bnjkd->bnikgd', probs.astype(v.dtype), vb)
    return o.reshape(B, S, ATT_WIDTH)


def l2norm(t):
    return t * lax.rsqrt(jnp.sum(t * t, axis=-1, keepdims=True) + EPS)


def gated_delta_rule(q, k, v, g, beta):
    B, S, H, DK = k.shape
    DV = v.shape[-1]
    C = DN_CHUNK
    NC = S // C
    f32 = jnp.float32
    q = l2norm(q.astype(f32)) * (DK ** -0.5)
    k = l2norm(k.astype(f32))

    def chunks(t):
        return t.reshape(B, NC, C, H, -1).transpose(1, 0, 3, 2, 4)

    qc, kc, vc = chunks(q), chunks(k), chunks(v.astype(f32))
    gc = g.astype(f32).reshape(B, NC, C, H).transpose(1, 0, 3, 2)
    bc = beta.astype(f32).reshape(B, NC, C, H).transpose(1, 0, 3, 2)
    gcum = jnp.cumsum(gc, axis=-1)
    tril = jnp.tril(jnp.ones((C, C), dtype=bool))
    strict = jnp.tril(jnp.ones((C, C), dtype=bool), -1)
    decay = jnp.exp(jnp.where(tril, gcum[..., :, None] - gcum[..., None, :], -jnp.inf))
    k_beta = kc * bc[..., None]
    v_beta = vc * bc[..., None]
    Lmat = jnp.where(strict, jnp.einsum('...id,...jd->...ij', k_beta, kc) * decay, 0.0)
    eye = jnp.eye(C, dtype=f32)
    T = lax.linalg.triangular_solve(Lmat + eye, jnp.broadcast_to(eye, Lmat.shape),
                                    left_side=True, lower=True, unit_diagonal=True)
    u = jnp.einsum('...ij,...jd->...id', T, v_beta)
    w = jnp.einsum('...ij,...jd->...id', T, k_beta * jnp.exp(gcum)[..., None])

    def step(state, xs):
        q_i, k_i, u_i, w_i, g_i, dec_i = xs
        attn = jnp.einsum('bhid,bhjd->bhij', q_i, k_i) * dec_i
        v_new = u_i - jnp.einsum('bhcd,bhde->bhce', w_i, state)
        o = (jnp.einsum('bhcd,bhde->bhce', q_i * jnp.exp(g_i)[..., None], state)
             + jnp.einsum('bhij,bhje->bhie', attn, v_new))
        g_last = g_i[..., -1]
        k_dec = k_i * jnp.exp(g_last[..., None] - g_i)[..., None]
        state = state * jnp.exp(g_last)[..., None, None] + jnp.einsum('bhcd,bhce->bhde', k_dec, v_new)
        return state, o

    state0 = jnp.zeros((B, H, DK, DV), f32)
    _, o = lax.scan(step, state0, (qc, kc, u, w, gcum, decay))
    return o.transpose(1, 0, 3, 2, 4).reshape(B, S, H, DV)


def hybrid_mixer(xn, w_in, lru_conv_w, lru_conv_b, lru_w_a, lru_b_a, lru_w_x, lru_b_x, lru_lambda,
                 attn_sinks, rel_bias, dn_conv_w, dn_a_log, dn_dt_bias, dn_norm, w_out):
    B, S, _ = xn.shape
    u = xn @ w_in
    (lru_x, lru_gate, att_q, att_k, att_v, dn_q, dn_k, dn_v, dn_z, dn_b, dn_a) = jnp.split(
        u, split_points(), axis=-1)
    xr = causal_dwconv(lru_x, lru_conv_w, lru_conv_b)
    y_lru = jax.nn.gelu(lru_gate) * rg_lru(xr, lru_w_a, lru_b_a, lru_w_x, lru_b_x, lru_lambda)
    y_att = swa_attention(att_q.reshape(B, S, ATT_HEADS, HEAD_DIM),
                          att_k.reshape(B, S, ATT_KV_HEADS, HEAD_DIM),
                          att_v.reshape(B, S, ATT_KV_HEADS, HEAD_DIM),
                          attn_sinks, rel_bias)
    qkv = jax.nn.silu(causal_dwconv(jnp.concatenate([dn_q, dn_k, dn_v], axis=-1), dn_conv_w))
    q, k, v = jnp.split(qkv, [DN_QK, 2 * DN_QK], axis=-1)
    beta = jax.nn.sigmoid(dn_b.astype(jnp.float32))
    g = -jnp.exp(dn_a_log.astype(jnp.float32)) * jax.nn.softplus(
        dn_a.astype(jnp.float32) + dn_dt_bias.astype(jnp.float32))
    o = gated_delta_rule(q.reshape(B, S, DN_HEADS, DN_DK), k.reshape(B, S, DN_HEADS, DN_DK),
                         v.reshape(B, S, DN_HEADS, DN_DV), g, beta)
    z = dn_z.reshape(B, S, DN_HEADS, DN_DV).astype(jnp.float32)
    o = (o * lax.rsqrt(jnp.mean(o * o, axis=-1, keepdims=True) + EPS)
         * dn_norm.astype(jnp.float32) * jax.nn.silu(z))
    y_dn = o.reshape(B, S, DN_WIDTH).astype(xn.dtype)
    return jnp.concatenate([y_lru, y_att, y_dn], axis=-1) @ w_out


def _fwd_setup_inputs(seed: int = 0) -> dict:
    key = jax.random.key(seed)
    ks = list(jax.random.split(key, 48))

    def nrm(shape, scale):
        return scale * jax.random.normal(ks.pop(), shape, jnp.float32)

    def gain(shape):
        return 1.0 + 0.02 * jax.random.normal(ks.pop(), shape, jnp.float32)

    L, D = DEPTH, D_MODEL
    x = nrm((BATCH, SEQ, D), 1.0)
    p = nrm((DEPTH, BATCH, SEQ, PLE_DIM), 1.0)
    a_c = jax.random.uniform(ks.pop(), (L, LRU_WIDTH), jnp.float32, 0.9, 0.999)
    s = a_c ** (1.0 / LRU_C)
    lru_lambda = jnp.log(s) - jnp.log1p(-s)
    dn_a_log = jnp.log(jax.random.uniform(ks.pop(), (L, DN_HEADS), jnp.float32, 1.0, 16.0))
    dt = jnp.exp(jax.random.uniform(ks.pop(), (L, DN_HEADS), jnp.float32,
                                    math.log(1e-3), math.log(1e-1)))
    dn_dt_bias = dt + jnp.log(-jnp.expm1(-dt))
    return {
        "x": x,
        "p": p,
        "ffn1_norm": gain((L, D)),
        "ffn1_w_gate": nrm((L, D, D_FF), D ** -0.5),
        "ffn1_w_up": nrm((L, D, D_FF), D ** -0.5),
        "ffn1_w_down": nrm((L, D_FF, D), D_FF ** -0.5),
        "mix_norm": gain((L, D)),
        "w_in": nrm((L, D, D_IN), D ** -0.5),
        "lru_conv_w": nrm((L, CONV_W, LRU_WIDTH), CONV_W ** -0.5),
        "lru_conv_b": nrm((L, LRU_WIDTH), 0.01),
        "lru_w_a": nrm((L, LRU_BLOCKS, LRU_BLOCK, LRU_BLOCK), LRU_BLOCK ** -0.5),
        "lru_b_a": nrm((L, LRU_WIDTH), 0.01),
        "lru_w_x": nrm((L, LRU_BLOCKS, LRU_BLOCK, LRU_BLOCK), LRU_BLOCK ** -0.5),
        "lru_b_x": nrm((L, LRU_WIDTH), 0.01),
        "lru_lambda": lru_lambda,
        "attn_sinks": nrm((L, ATT_HEADS), 0.5),
        "rel_bias": nrm((REL_BUCKETS, ATT_HEADS), 0.5),
        "dn_conv_w": nrm((L, CONV_W, 2 * DN_QK + DN_WIDTH), CONV_W ** -0.5),
        "dn_a_log": dn_a_log,
        "dn_dt_bias": dn_dt_bias,
        "dn_norm": gain((L, DN_DV)),
        "w_out": nrm((L, D_MIX, D), D_MIX ** -0.5),
        "ffn2_norm": gain((L, D)),
        "ffn2_w_gate": nrm((L, D, D_FF), D ** -0.5),
        "ffn2_w_up": nrm((L, D, D_FF), D ** -0.5),
        "ffn2_w_down": nrm((L, D_FF, D), D_FF ** -0.5),
        "ple_norm": gain((L, D)),
        "ple_w_gate": nrm((L, D, D), D ** -0.5),
        "ple_w_proj": nrm((L, PLE_DIM, D), PLE_DIM ** -0.5),
        "final_norm": gain((D,)),
    }


def _fwd_reference(x, p, ffn1_norm, ffn1_w_gate, ffn1_w_up, ffn1_w_down, mix_norm, w_in,
              lru_conv_w, lru_conv_b, lru_w_a, lru_b_a, lru_w_x, lru_b_x, lru_lambda,
              attn_sinks, rel_bias, dn_conv_w, dn_a_log, dn_dt_bias, dn_norm, w_out,
              ffn2_norm, ffn2_w_gate, ffn2_w_up, ffn2_w_down, ple_norm, ple_w_gate, ple_w_proj,
              final_norm):
    h = x
    for l in range(DEPTH):
        h = h + 0.5 * swiglu(rms_norm(h, ffn1_norm[l]), ffn1_w_gate[l], ffn1_w_up[l], ffn1_w_down[l])
        h = h + hybrid_mixer(rms_norm(h, mix_norm[l]), w_in[l],
                             lru_conv_w[l], lru_conv_b[l], lru_w_a[l], lru_b_a[l],
                             lru_w_x[l], lru_b_x[l], lru_lambda[l],
                             attn_sinks[l], rel_bias,
                             dn_conv_w[l], dn_a_log[l], dn_dt_bias[l], dn_norm[l], w_out[l])
        h = h + 0.5 * swiglu(rms_norm(h, ffn2_norm[l]), ffn2_w_gate[l], ffn2_w_up[l], ffn2_w_down[l])
        gate = jax.nn.sigmoid(rms_norm(h, ple_norm[l]) @ ple_w_gate[l])
        h = h + gate * (p[l] @ ple_w_proj[l])
    return rms_norm(h, final_norm)


import jax as _jax
import jax.numpy as _jnp

TWIN_FORMAT = 'train_step'
FWD_PARAMS = ['x', 'p', 'ffn1_norm', 'ffn1_w_gate', 'ffn1_w_up', 'ffn1_w_down', 'mix_norm', 'w_in', 'lru_conv_w', 'lru_conv_b', 'lru_w_a', 'lru_b_a', 'lru_w_x', 'lru_b_x', 'lru_lambda', 'attn_sinks', 'rel_bias', 'dn_conv_w', 'dn_a_log', 'dn_dt_bias', 'dn_norm', 'w_out', 'ffn2_norm', 'ffn2_w_gate', 'ffn2_w_up', 'ffn2_w_down', 'ple_norm', 'ple_w_gate', 'ple_w_proj', 'final_norm']
TWIN_WEIGHTS = ['ffn1_norm', 'ffn1_w_gate', 'ffn1_w_up', 'ffn1_w_down', 'mix_norm', 'w_in', 'lru_conv_w', 'lru_conv_b', 'lru_w_a', 'lru_b_a', 'lru_w_x', 'lru_b_x', 'lru_lambda', 'attn_sinks', 'rel_bias', 'dn_conv_w', 'dn_a_log', 'dn_dt_bias', 'dn_norm', 'w_out', 'ffn2_norm', 'ffn2_w_gate', 'ffn2_w_up', 'ffn2_w_down', 'ple_norm', 'ple_w_gate', 'ple_w_proj', 'final_norm']
TWIN_DIFF_INPUT = 'x'
TWIN_INPUTS = ['x', 'p', 'ffn1_norm', 'ffn1_w_gate', 'ffn1_w_up', 'ffn1_w_down', 'mix_norm', 'w_in', 'lru_conv_w', 'lru_conv_b', 'lru_w_a', 'lru_b_a', 'lru_w_x', 'lru_b_x', 'lru_lambda', 'attn_sinks', 'rel_bias', 'dn_conv_w', 'dn_a_log', 'dn_dt_bias', 'dn_norm', 'w_out', 'ffn2_norm', 'ffn2_w_gate', 'ffn2_w_up', 'ffn2_w_down', 'ple_norm', 'ple_w_gate', 'ple_w_proj', 'final_norm', 'loss_target', 'm_ffn1_norm', 'm_ffn1_w_gate', 'm_ffn1_w_up', 'm_ffn1_w_down', 'm_mix_norm', 'm_w_in', 'm_lru_conv_w', 'm_lru_conv_b', 'm_lru_w_a', 'm_lru_b_a', 'm_lru_w_x', 'm_lru_b_x', 'm_lru_lambda', 'm_attn_sinks', 'm_rel_bias', 'm_dn_conv_w', 'm_dn_a_log', 'm_dn_dt_bias', 'm_dn_norm', 'm_w_out', 'm_ffn2_norm', 'm_ffn2_w_gate', 'm_ffn2_w_up', 'm_ffn2_w_down', 'm_ple_norm', 'm_ple_w_gate', 'm_ple_w_proj', 'm_final_norm', 'v_ffn1_norm', 'v_ffn1_w_gate', 'v_ffn1_w_up', 'v_ffn1_w_down', 'v_mix_norm', 'v_w_in', 'v_lru_conv_w', 'v_lru_conv_b', 'v_lru_w_a', 'v_lru_b_a', 'v_lru_w_x', 'v_lru_b_x', 'v_lru_lambda', 'v_attn_sinks', 'v_rel_bias', 'v_dn_conv_w', 'v_dn_a_log', 'v_dn_dt_bias', 'v_dn_norm', 'v_w_out', 'v_ffn2_norm', 'v_ffn2_w_gate', 'v_ffn2_w_up', 'v_ffn2_w_down', 'v_ple_norm', 'v_ple_w_gate', 'v_ple_w_proj', 'v_final_norm']
TWIN_OUTPUTS = ['loss', 'grad_x', 'grad_ffn1_norm', 'grad_ffn1_w_gate', 'grad_ffn1_w_up', 'grad_ffn1_w_down', 'grad_mix_norm', 'grad_w_in', 'grad_lru_conv_w', 'grad_lru_conv_b', 'grad_lru_w_a', 'grad_lru_b_a', 'grad_lru_w_x', 'grad_lru_b_x', 'grad_lru_lambda', 'grad_attn_sinks', 'grad_rel_bias', 'grad_dn_conv_w', 'grad_dn_a_log', 'grad_dn_dt_bias', 'grad_dn_norm', 'grad_w_out', 'grad_ffn2_norm', 'grad_ffn2_w_gate', 'grad_ffn2_w_up', 'grad_ffn2_w_down', 'grad_ple_norm', 'grad_ple_w_gate', 'grad_ple_w_proj', 'grad_final_norm', 'delta_ffn1_norm', 'delta_ffn1_w_gate', 'delta_ffn1_w_up', 'delta_ffn1_w_down', 'delta_mix_norm', 'delta_w_in', 'delta_lru_conv_w', 'delta_lru_conv_b', 'delta_lru_w_a', 'delta_lru_b_a', 'delta_lru_w_x', 'delta_lru_b_x', 'delta_lru_lambda', 'delta_attn_sinks', 'delta_rel_bias', 'delta_dn_conv_w', 'delta_dn_a_log', 'delta_dn_dt_bias', 'delta_dn_norm', 'delta_w_out', 'delta_ffn2_norm', 'delta_ffn2_w_gate', 'delta_ffn2_w_up', 'delta_ffn2_w_down', 'delta_ple_norm', 'delta_ple_w_gate', 'delta_ple_w_proj', 'delta_final_norm', 'new_m_ffn1_norm', 'new_m_ffn1_w_gate', 'new_m_ffn1_w_up', 'new_m_ffn1_w_down', 'new_m_mix_norm', 'new_m_w_in', 'new_m_lru_conv_w', 'new_m_lru_conv_b', 'new_m_lru_w_a', 'new_m_lru_b_a', 'new_m_lru_w_x', 'new_m_lru_b_x', 'new_m_lru_lambda', 'new_m_attn_sinks', 'new_m_rel_bias', 'new_m_dn_conv_w', 'new_m_dn_a_log', 'new_m_dn_dt_bias', 'new_m_dn_norm', 'new_m_w_out', 'new_m_ffn2_norm', 'new_m_ffn2_w_gate', 'new_m_ffn2_w_up', 'new_m_ffn2_w_down', 'new_m_ple_norm', 'new_m_ple_w_gate', 'new_m_ple_w_proj', 'new_m_final_norm', 'new_v_ffn1_norm', 'new_v_ffn1_w_gate', 'new_v_ffn1_w_up', 'new_v_ffn1_w_down', 'new_v_mix_norm', 'new_v_w_in', 'new_v_lru_conv_w', 'new_v_lru_conv_b', 'new_v_lru_w_a', 'new_v_lru_b_a', 'new_v_lru_w_x', 'new_v_lru_b_x', 'new_v_lru_lambda', 'new_v_attn_sinks', 'new_v_rel_bias', 'new_v_dn_conv_w', 'new_v_dn_a_log', 'new_v_dn_dt_bias', 'new_v_dn_norm', 'new_v_w_out', 'new_v_ffn2_norm', 'new_v_ffn2_w_gate', 'new_v_ffn2_w_up', 'new_v_ffn2_w_down', 'new_v_ple_norm', 'new_v_ple_w_gate', 'new_v_ple_w_proj', 'new_v_final_norm']
TWIN_LEAF_KINDS = {'loss': 'loss', 'grad_x': 'grad_x', 'grad_ffn1_norm': 'grad_w', 'grad_ffn1_w_gate': 'grad_w', 'grad_ffn1_w_up': 'grad_w', 'grad_ffn1_w_down': 'grad_w', 'grad_mix_norm': 'grad_w', 'grad_w_in': 'grad_w', 'grad_lru_conv_w': 'grad_w', 'grad_lru_conv_b': 'grad_w', 'grad_lru_w_a': 'grad_w', 'grad_lru_b_a': 'grad_w', 'grad_lru_w_x': 'grad_w', 'grad_lru_b_x': 'grad_w', 'grad_lru_lambda': 'grad_w', 'grad_attn_sinks': 'grad_w', 'grad_rel_bias': 'grad_w', 'grad_dn_conv_w': 'grad_w', 'grad_dn_a_log': 'grad_w', 'grad_dn_dt_bias': 'grad_w', 'grad_dn_norm': 'grad_w', 'grad_w_out': 'grad_w', 'grad_ffn2_norm': 'grad_w', 'grad_ffn2_w_gate': 'grad_w', 'grad_ffn2_w_up': 'grad_w', 'grad_ffn2_w_down': 'grad_w', 'grad_ple_norm': 'grad_w', 'grad_ple_w_gate': 'grad_w', 'grad_ple_w_proj': 'grad_w', 'grad_final_norm': 'grad_w', 'delta_ffn1_norm': 'delta_w', 'delta_ffn1_w_gate': 'delta_w', 'delta_ffn1_w_up': 'delta_w', 'delta_ffn1_w_down': 'delta_w', 'delta_mix_norm': 'delta_w', 'delta_w_in': 'delta_w', 'delta_lru_conv_w': 'delta_w', 'delta_lru_conv_b': 'delta_w', 'delta_lru_w_a': 'delta_w', 'delta_lru_b_a': 'delta_w', 'delta_lru_w_x': 'delta_w', 'delta_lru_b_x': 'delta_w', 'delta_lru_lambda': 'delta_w', 'delta_attn_sinks': 'delta_w', 'delta_rel_bias': 'delta_w', 'delta_dn_conv_w': 'delta_w', 'delta_dn_a_log': 'delta_w', 'delta_dn_dt_bias': 'delta_w', 'delta_dn_norm': 'delta_w', 'delta_w_out': 'delta_w', 'delta_ffn2_norm': 'delta_w', 'delta_ffn2_w_gate': 'delta_w', 'delta_ffn2_w_up': 'delta_w', 'delta_ffn2_w_down': 'delta_w', 'delta_ple_norm': 'delta_w', 'delta_ple_w_gate': 'delta_w', 'delta_ple_w_proj': 'delta_w', 'delta_final_norm': 'delta_w', 'new_m_ffn1_norm': 'new_m', 'new_m_ffn1_w_gate': 'new_m', 'new_m_ffn1_w_up': 'new_m', 'new_m_ffn1_w_down': 'new_m', 'new_m_mix_norm': 'new_m', 'new_m_w_in': 'new_m', 'new_m_lru_conv_w': 'new_m', 'new_m_lru_conv_b': 'new_m', 'new_m_lru_w_a': 'new_m', 'new_m_lru_b_a': 'new_m', 'new_m_lru_w_x': 'new_m', 'new_m_lru_b_x': 'new_m', 'new_m_lru_lambda': 'new_m', 'new_m_attn_sinks': 'new_m', 'new_m_rel_bias': 'new_m', 'new_m_dn_conv_w': 'new_m', 'new_m_dn_a_log': 'new_m', 'new_m_dn_dt_bias': 'new_m', 'new_m_dn_norm': 'new_m', 'new_m_w_out': 'new_m', 'new_m_ffn2_norm': 'new_m', 'new_m_ffn2_w_gate': 'new_m', 'new_m_ffn2_w_up': 'new_m', 'new_m_ffn2_w_down': 'new_m', 'new_m_ple_norm': 'new_m', 'new_m_ple_w_gate': 'new_m', 'new_m_ple_w_proj': 'new_m', 'new_m_final_norm': 'new_m', 'new_v_ffn1_norm': 'new_v', 'new_v_ffn1_w_gate': 'new_v', 'new_v_ffn1_w_up': 'new_v', 'new_v_ffn1_w_down': 'new_v', 'new_v_mix_norm': 'new_v', 'new_v_w_in': 'new_v', 'new_v_lru_conv_w': 'new_v', 'new_v_lru_conv_b': 'new_v', 'new_v_lru_w_a': 'new_v', 'new_v_lru_b_a': 'new_v', 'new_v_lru_w_x': 'new_v', 'new_v_lru_b_x': 'new_v', 'new_v_lru_lambda': 'new_v', 'new_v_attn_sinks': 'new_v', 'new_v_rel_bias': 'new_v', 'new_v_dn_conv_w': 'new_v', 'new_v_dn_a_log': 'new_v', 'new_v_dn_dt_bias': 'new_v', 'new_v_dn_norm': 'new_v', 'new_v_w_out': 'new_v', 'new_v_ffn2_norm': 'new_v', 'new_v_ffn2_w_gate': 'new_v', 'new_v_ffn2_w_up': 'new_v', 'new_v_ffn2_w_down': 'new_v', 'new_v_ple_norm': 'new_v', 'new_v_ple_w_gate': 'new_v', 'new_v_ple_w_proj': 'new_v', 'new_v_final_norm': 'new_v'}


def _forward(args):
    return _fwd_reference(*[args[k] for k in FWD_PARAMS])


def _output_shape():
    out = _jax.eval_shape(lambda: _forward(_fwd_setup_inputs(0)))
    return out.shape, out.dtype

N_MICROBATCH = 1
ADAM_LR = 0.001
ADAM_B1 = 0.9
ADAM_B2 = 0.999
ADAM_EPS = 1e-08
ADAM_WD = 0.01
ADAM_STEP = 10
PER_EXAMPLE_BATCH_AXIS = {'x': 0, 'p': 1, 'loss_target': 0}
SHARED_INPUTS = []
_WEIGHT_DTYPES = {'ffn1_norm': _jnp.float32, 'ffn1_w_gate': _jnp.float32, 'ffn1_w_up': _jnp.float32, 'ffn1_w_down': _jnp.float32, 'mix_norm': _jnp.float32, 'w_in': _jnp.float32, 'lru_conv_w': _jnp.float32, 'lru_conv_b': _jnp.float32, 'lru_w_a': _jnp.float32, 'lru_b_a': _jnp.float32, 'lru_w_x': _jnp.float32, 'lru_b_x': _jnp.float32, 'lru_lambda': _jnp.float32, 'attn_sinks': _jnp.float32, 'rel_bias': _jnp.float32, 'dn_conv_w': _jnp.float32, 'dn_a_log': _jnp.float32, 'dn_dt_bias': _jnp.float32, 'dn_norm': _jnp.float32, 'w_out': _jnp.float32, 'ffn2_norm': _jnp.float32, 'ffn2_w_gate': _jnp.float32, 'ffn2_w_up': _jnp.float32, 'ffn2_w_down': _jnp.float32, 'ple_norm': _jnp.float32, 'ple_w_gate': _jnp.float32, 'ple_w_proj': _jnp.float32, 'final_norm': _jnp.float32}
MOMENT_SCALE = {'ffn1_norm': 7.406570e-02, 'ffn1_w_gate': 3.175897e-02, 'ffn1_w_up': 3.071869e-02, 'ffn1_w_down': 5.095428e-02, 'mix_norm': 1.000675e-01, 'w_in': 6.714543e-02, 'lru_conv_w': 8.299577e-02, 'lru_conv_b': 8.356510e-01, 'lru_w_a': 2.985011e-02, 'lru_b_a': 2.079460e-02, 'lru_w_x': 5.296160e-02, 'lru_b_x': 3.839966e-02, 'lru_lambda': 4.390070e-02, 'attn_sinks': 2.671501e-02, 'rel_bias': 5.653413e-02, 'dn_conv_w': 8.308936e-02, 'dn_a_log': 6.872039e-01, 'dn_dt_bias': 6.510312e-01, 'dn_norm': 1.724802e-01, 'w_out': 6.101913e-02, 'ffn2_norm': 5.837109e-02, 'ffn2_w_gate': 2.529739e-02, 'ffn2_w_up': 2.457052e-02, 'ffn2_w_down': 4.071785e-02, 'ple_norm': 2.793247e-02, 'ple_w_gate': 2.844663e-02, 'ple_w_proj': 7.275663e-02, 'final_norm': 3.203705e+01}


def _to_microbatches(a, axis):
    t = _jnp.moveaxis(a, axis, 0)
    t = t.reshape((N_MICROBATCH, t.shape[0] // N_MICROBATCH) + t.shape[1:])
    return _jnp.moveaxis(t, 1, axis + 1)


def setup_inputs(seed: int = 0) -> dict:
    inp = _fwd_setup_inputs(seed)
    key = _jax.random.fold_in(_jax.random.key(seed), 7919)
    shape, _ = _output_shape()
    out = dict(inp)
    out["loss_target"] = _jax.random.normal(_jax.random.fold_in(key, 0), shape, _jnp.float32)
    for i, name in enumerate(TWIN_WEIGHTS):
        w = inp[name].astype(_jnp.float32)
        if MOMENT_SCALE is None:
            s = _jnp.sqrt(_jnp.mean(_jnp.square(w)) + 1e-30)
        else:
            s = MOMENT_SCALE[name]
        km, kv = _jax.random.split(_jax.random.fold_in(key, i + 1))
        out[name] = w
        out["m_" + name] = s * _jax.random.normal(km, w.shape, _jnp.float32)
        out["v_" + name] = (s * s) * _jax.random.uniform(kv, w.shape, _jnp.float32, 0.5, 1.5)
    if N_MICROBATCH > 1:
        for name, axis in PER_EXAMPLE_BATCH_AXIS.items():
            out[name] = _to_microbatches(out[name], axis)
    return {'x': out['x'], 'p': out['p'], 'ffn1_norm': out['ffn1_norm'], 'ffn1_w_gate': out['ffn1_w_gate'], 'ffn1_w_up': out['ffn1_w_up'], 'ffn1_w_down': out['ffn1_w_down'], 'mix_norm': out['mix_norm'], 'w_in': out['w_in'], 'lru_conv_w': out['lru_conv_w'], 'lru_conv_b': out['lru_conv_b'], 'lru_w_a': out['lru_w_a'], 'lru_b_a': out['lru_b_a'], 'lru_w_x': out['lru_w_x'], 'lru_b_x': out['lru_b_x'], 'lru_lambda': out['lru_lambda'], 'attn_sinks': out['attn_sinks'], 'rel_bias': out['rel_bias'], 'dn_conv_w': out['dn_conv_w'], 'dn_a_log': out['dn_a_log'], 'dn_dt_bias': out['dn_dt_bias'], 'dn_norm': out['dn_norm'], 'w_out': out['w_out'], 'ffn2_norm': out['ffn2_norm'], 'ffn2_w_gate': out['ffn2_w_gate'], 'ffn2_w_up': out['ffn2_w_up'], 'ffn2_w_down': out['ffn2_w_down'], 'ple_norm': out['ple_norm'], 'ple_w_gate': out['ple_w_gate'], 'ple_w_proj': out['ple_w_proj'], 'final_norm': out['final_norm'], 'loss_target': out['loss_target'], 'm_ffn1_norm': out['m_ffn1_norm'], 'm_ffn1_w_gate': out['m_ffn1_w_gate'], 'm_ffn1_w_up': out['m_ffn1_w_up'], 'm_ffn1_w_down': out['m_ffn1_w_down'], 'm_mix_norm': out['m_mix_norm'], 'm_w_in': out['m_w_in'], 'm_lru_conv_w': out['m_lru_conv_w'], 'm_lru_conv_b': out['m_lru_conv_b'], 'm_lru_w_a': out['m_lru_w_a'], 'm_lru_b_a': out['m_lru_b_a'], 'm_lru_w_x': out['m_lru_w_x'], 'm_lru_b_x': out['m_lru_b_x'], 'm_lru_lambda': out['m_lru_lambda'], 'm_attn_sinks': out['m_attn_sinks'], 'm_rel_bias': out['m_rel_bias'], 'm_dn_conv_w': out['m_dn_conv_w'], 'm_dn_a_log': out['m_dn_a_log'], 'm_dn_dt_bias': out['m_dn_dt_bias'], 'm_dn_norm': out['m_dn_norm'], 'm_w_out': out['m_w_out'], 'm_ffn2_norm': out['m_ffn2_norm'], 'm_ffn2_w_gate': out['m_ffn2_w_gate'], 'm_ffn2_w_up': out['m_ffn2_w_up'], 'm_ffn2_w_down': out['m_ffn2_w_down'], 'm_ple_norm': out['m_ple_norm'], 'm_ple_w_gate': out['m_ple_w_gate'], 'm_ple_w_proj': out['m_ple_w_proj'], 'm_final_norm': out['m_final_norm'], 'v_ffn1_norm': out['v_ffn1_norm'], 'v_ffn1_w_gate': out['v_ffn1_w_gate'], 'v_ffn1_w_up': out['v_ffn1_w_up'], 'v_ffn1_w_down': out['v_ffn1_w_down'], 'v_mix_norm': out['v_mix_norm'], 'v_w_in': out['v_w_in'], 'v_lru_conv_w': out['v_lru_conv_w'], 'v_lru_conv_b': out['v_lru_conv_b'], 'v_lru_w_a': out['v_lru_w_a'], 'v_lru_b_a': out['v_lru_b_a'], 'v_lru_w_x': out['v_lru_w_x'], 'v_lru_b_x': out['v_lru_b_x'], 'v_lru_lambda': out['v_lru_lambda'], 'v_attn_sinks': out['v_attn_sinks'], 'v_rel_bias': out['v_rel_bias'], 'v_dn_conv_w': out['v_dn_conv_w'], 'v_dn_a_log': out['v_dn_a_log'], 'v_dn_dt_bias': out['v_dn_dt_bias'], 'v_dn_norm': out['v_dn_norm'], 'v_w_out': out['v_w_out'], 'v_ffn2_norm': out['v_ffn2_norm'], 'v_ffn2_w_gate': out['v_ffn2_w_gate'], 'v_ffn2_w_up': out['v_ffn2_w_up'], 'v_ffn2_w_down': out['v_ffn2_w_down'], 'v_ple_norm': out['v_ple_norm'], 'v_ple_w_gate': out['v_ple_w_gate'], 'v_ple_w_proj': out['v_ple_w_proj'], 'v_final_norm': out['v_final_norm']}


def _loss(weights, diff, rest, loss_target):
    with _jax.named_scope("forward"):
        args = {**rest, TWIN_DIFF_INPUT: diff, **{k: w.astype(_WEIGHT_DTYPES[k]) for k, w in weights.items()}}
        y = _forward(args)
    with _jax.named_scope("loss_head"):
        err = _jnp.square(y.astype(_jnp.float32) - loss_target)
        return 0.5 * _jnp.sum(_jnp.mean(err, axis=-1)) if err.ndim else 0.5 * err


def _adamw(w, g, m, v):
    m = ADAM_B1 * m + (1.0 - ADAM_B1) * g
    v = ADAM_B2 * v + (1.0 - ADAM_B2) * _jnp.square(g)
    m_hat = m / (1.0 - ADAM_B1 ** ADAM_STEP)
    v_hat = v / (1.0 - ADAM_B2 ** ADAM_STEP)
    delta = -ADAM_LR * (m_hat / (_jnp.sqrt(v_hat) + ADAM_EPS) + ADAM_WD * w)
    return delta, m, v


def reference(x, p, ffn1_norm, ffn1_w_gate, ffn1_w_up, ffn1_w_down, mix_norm, w_in, lru_conv_w, lru_conv_b, lru_w_a, lru_b_a, lru_w_x, lru_b_x, lru_lambda, attn_sinks, rel_bias, dn_conv_w, dn_a_log, dn_dt_bias, dn_norm, w_out, ffn2_norm, ffn2_w_gate, ffn2_w_up, ffn2_w_down, ple_norm, ple_w_gate, ple_w_proj, final_norm, loss_target, m_ffn1_norm, m_ffn1_w_gate, m_ffn1_w_up, m_ffn1_w_down, m_mix_norm, m_w_in, m_lru_conv_w, m_lru_conv_b, m_lru_w_a, m_lru_b_a, m_lru_w_x, m_lru_b_x, m_lru_lambda, m_attn_sinks, m_rel_bias, m_dn_conv_w, m_dn_a_log, m_dn_dt_bias, m_dn_norm, m_w_out, m_ffn2_norm, m_ffn2_w_gate, m_ffn2_w_up, m_ffn2_w_down, m_ple_norm, m_ple_w_gate, m_ple_w_proj, m_final_norm, v_ffn1_norm, v_ffn1_w_gate, v_ffn1_w_up, v_ffn1_w_down, v_mix_norm, v_w_in, v_lru_conv_w, v_lru_conv_b, v_lru_w_a, v_lru_b_a, v_lru_w_x, v_lru_b_x, v_lru_lambda, v_attn_sinks, v_rel_bias, v_dn_conv_w, v_dn_a_log, v_dn_dt_bias, v_dn_norm, v_w_out, v_ffn2_norm, v_ffn2_w_gate, v_ffn2_w_up, v_ffn2_w_down, v_ple_norm, v_ple_w_gate, v_ple_w_proj, v_final_norm):
    given = dict(x=x, p=p, ffn1_norm=ffn1_norm, ffn1_w_gate=ffn1_w_gate, ffn1_w_up=ffn1_w_up, ffn1_w_down=ffn1_w_down, mix_norm=mix_norm, w_in=w_in, lru_conv_w=lru_conv_w, lru_conv_b=lru_conv_b, lru_w_a=lru_w_a, lru_b_a=lru_b_a, lru_w_x=lru_w_x, lru_b_x=lru_b_x, lru_lambda=lru_lambda, attn_sinks=attn_sinks, rel_bias=rel_bias, dn_conv_w=dn_conv_w, dn_a_log=dn_a_log, dn_dt_bias=dn_dt_bias, dn_norm=dn_norm, w_out=w_out, ffn2_norm=ffn2_norm, ffn2_w_gate=ffn2_w_gate, ffn2_w_up=ffn2_w_up, ffn2_w_down=ffn2_w_down, ple_norm=ple_norm, ple_w_gate=ple_w_gate, ple_w_proj=ple_w_proj, final_norm=final_norm, loss_target=loss_target, m_ffn1_norm=m_ffn1_norm, m_ffn1_w_gate=m_ffn1_w_gate, m_ffn1_w_up=m_ffn1_w_up, m_ffn1_w_down=m_ffn1_w_down, m_mix_norm=m_mix_norm, m_w_in=m_w_in, m_lru_conv_w=m_lru_conv_w, m_lru_conv_b=m_lru_conv_b, m_lru_w_a=m_lru_w_a, m_lru_b_a=m_lru_b_a, m_lru_w_x=m_lru_w_x, m_lru_b_x=m_lru_b_x, m_lru_lambda=m_lru_lambda, m_attn_sinks=m_attn_sinks, m_rel_bias=m_rel_bias, m_dn_conv_w=m_dn_conv_w, m_dn_a_log=m_dn_a_log, m_dn_dt_bias=m_dn_dt_bias, m_dn_norm=m_dn_norm, m_w_out=m_w_out, m_ffn2_norm=m_ffn2_norm, m_ffn2_w_gate=m_ffn2_w_gate, m_ffn2_w_up=m_ffn2_w_up, m_ffn2_w_down=m_ffn2_w_down, m_ple_norm=m_ple_norm, m_ple_w_gate=m_ple_w_gate, m_ple_w_proj=m_ple_w_proj, m_final_norm=m_final_norm, v_ffn1_norm=v_ffn1_norm, v_ffn1_w_gate=v_ffn1_w_gate, v_ffn1_w_up=v_ffn1_w_up, v_ffn1_w_down=v_ffn1_w_down, v_mix_norm=v_mix_norm, v_w_in=v_w_in, v_lru_conv_w=v_lru_conv_w, v_lru_conv_b=v_lru_conv_b, v_lru_w_a=v_lru_w_a, v_lru_b_a=v_lru_b_a, v_lru_w_x=v_lru_w_x, v_lru_b_x=v_lru_b_x, v_lru_lambda=v_lru_lambda, v_attn_sinks=v_attn_sinks, v_rel_bias=v_rel_bias, v_dn_conv_w=v_dn_conv_w, v_dn_a_log=v_dn_a_log, v_dn_dt_bias=v_dn_dt_bias, v_dn_norm=v_dn_norm, v_w_out=v_w_out, v_ffn2_norm=v_ffn2_norm, v_ffn2_w_gate=v_ffn2_w_gate, v_ffn2_w_up=v_ffn2_w_up, v_ffn2_w_down=v_ffn2_w_down, v_ple_norm=v_ple_norm, v_ple_w_gate=v_ple_w_gate, v_ple_w_proj=v_ple_w_proj, v_final_norm=v_final_norm)
    weights = {n: given[n] for n in TWIN_WEIGHTS}
    shared = {n: given[n] for n in SHARED_INPUTS}
    per_example = {n: given[n] for n in ['x', 'p']}
    grad_fn = _jax.value_and_grad(_loss, argnums=(0, 1))

    def one_microbatch(ex, loss_target):
        ex = dict(ex)
        diff = ex.pop(TWIN_DIFF_INPUT)
        return grad_fn(weights, diff, {**shared, **ex}, loss_target)

    if N_MICROBATCH == 1:
        loss, (grad_w, grad_x) = one_microbatch(per_example, given["loss_target"])
    else:
        def body(carry, xs):
            loss_sum, grad_sum = carry
            l_k, (gw_k, gx_k) = one_microbatch(xs[0], xs[1])
            with _jax.named_scope("update"):
                return (loss_sum + l_k, _jax.tree.map(_jnp.add, grad_sum, gw_k)), gx_k

        init = (_jnp.zeros((), _jnp.float32), _jax.tree.map(_jnp.zeros_like, weights))
        (loss, grad_w), grad_x = _jax.lax.scan(body, init, (per_example, given["loss_target"]))
    with _jax.named_scope("update"):
        delta_w, new_m, new_v = {}, {}, {}
        for n in TWIN_WEIGHTS:
            delta_w[n], new_m[n], new_v[n] = _adamw(weights[n], grad_w[n], given["m_" + n], given["v_" + n])
    return (loss, grad_x, *[grad_w[n] for n in TWIN_WEIGHTS], *[delta_w[n] for n in TWIN_WEIGHTS],
            *[new_m[n] for n in TWIN_WEIGHTS], *[new_v[n] for n in TWIN_WEIGHTS])
```

```python
import functools
import math

import numpy as np
import jax
import jax.numpy as jnp
from jax import lax
from jax.experimental import pallas as pl
from jax.experimental.pallas import tpu as pltpu

F32 = jnp.float32
BF16 = jnp.bfloat16

EPS = 1e-6
D_MODEL = 1024
D_FF = 2816
N_CHIP = 4
FF_BLK = D_FF // N_CHIP
HEAD = 64
LRU_W = 256
ATT_W = 512
ATT_HEADS = 8
KV_HEADS = 2
ATT_GROUP = 4
BLOCK_Q = 128
DN_HEADS = 4
DN_CHUNK = 64
D_IN = 2312
D_IN_PAD = 2560
PLE_DIM = 256
REL_BUCKETS = 32
LRU_C = 8.0
N_LAYER = 2

ADAM_LR, ADAM_B1, ADAM_B2, ADAM_EPS, ADAM_WD, ADAM_STEP = 0.001, 0.9, 0.999, 1e-08, 0.01, 10

VMEM_LIMIT = 56 << 20
MESH = pl.DeviceIdType.MESH
SDS = jax.ShapeDtypeStruct


def _dot(a, b, ca=1, cb=0, hi=False):
    dims = (((ca,), (cb,)), ((), ()))
    if hi:
        return lax.dot_general(a.astype(F32), b.astype(F32), dims, precision=lax.Precision.HIGHEST,
                               preferred_element_type=F32)
    return lax.dot_general(a.astype(BF16), b.astype(BF16), dims, preferred_element_type=F32)


def _nn(a, b, hi=False):
    return _dot(a, b, 1, 0, hi)


def _nt(a, b, hi=False):
    return _dot(a, b, 1, 1, hi)


def _tn(a, b, hi=False):
    return _dot(a, b, 0, 0, hi)


def _sigmoid(x):
    return jax.nn.sigmoid(x)


def _softplus(x):
    return jnp.maximum(x, 0.0) + jnp.log1p(jnp.exp(-jnp.abs(x)))


def _neg_expm1(z):
    series = -z * (1.0 + z * (0.5 + z * (1.0 / 6.0 + z * (1.0 / 24.0 + z * (1.0 / 120.0)))))
    return jnp.where(z > -0.05, series, 1.0 - jnp.exp(z))


_GELU_C = math.sqrt(2.0 / math.pi)


def _gelu(x):
    t = jnp.tanh(_GELU_C * (x + 0.044715 * x * x * x))
    return 0.5 * x * (1.0 + t), t


def _gelu_grad(x, t):
    return 0.5 * (1.0 + t) + 0.5 * x * (1.0 - t * t) * _GELU_C * (1.0 + 3.0 * 0.044715 * x * x)


def _rms_fwd(h, g):
    r = lax.rsqrt(jnp.mean(h * h, axis=-1, keepdims=True) + EPS)
    xh = h * r
    return xh * g, xh, r


def _rms_bwd(dn, xh, r, g):
    dxh = dn * g
    dh = r * (dxh - xh * jnp.mean(dxh * xh, axis=-1, keepdims=True))
    return dh, jnp.sum(dn * xh, axis=0, keepdims=True)


def _shift_down(x, d, fill=0.0):
    row = lax.broadcasted_iota(jnp.int32, x.shape, 0)
    return jnp.where(row >= d, pltpu.roll(x, d, 0), fill)


def _shift_up(x, d, fill=0.0):
    n = x.shape[0]
    row = lax.broadcasted_iota(jnp.int32, x.shape, 0)
    return jnp.where(row < n - d, pltpu.roll(x, n - d, 0), fill)


def _conv_fwd(x, w):
    y = x * w[3]
    for k in range(3):
        y = y + _shift_down(x, 3 - k) * w[k]
    return y


def _conv_bwd(dy, x, w):
    dx = dy * w[3]
    rows = [None] * 4
    rows[3] = jnp.sum(dy * x, axis=0, keepdims=True)
    for k in range(3):
        dx = dx + _shift_up(dy, 3 - k) * w[k]
        rows[k] = jnp.sum(dy * _shift_down(x, 3 - k), axis=0, keepdims=True)
    r4 = lax.broadcasted_iota(jnp.int32, (4, x.shape[1]), 0)
    dw = jnp.zeros((4, x.shape[1]), F32)
    for k in range(4):
        dw = jnp.where(r4 == k, rows[k], dw)
    return dx, dw


def _params(sem=None, vmem=VMEM_LIMIT):
    return pltpu.CompilerParams(dimension_semantics=sem, vmem_limit_bytes=vmem)


def _whole(shape):
    nd = len(shape)
    return pl.BlockSpec(shape, lambda *_: (0,) * nd)


def matmul(a, b, *, name, ta=False, tb=False, residual=None, out_dtype=F32, tm=512, tn=512, tk=512):
    m, k = (a.shape[1], a.shape[0]) if ta else a.shape
    n = b.shape[0] if tb else b.shape[1]
    tm, tn, tk = min(tm, m), min(tn, n), min(tk, k)
    assert m % tm == 0 and n % tn == 0 and k % tk == 0, (m, n, k, tm, tn, tk)
    nk = k // tk

    def body(*refs):
        if residual is None:
            a_ref, b_ref, o_ref, acc = refs
        else:
            a_ref, b_ref, r_ref, o_ref, acc = refs
        kk = pl.program_id(2)

        @pl.when(kk == 0)
        def _():
            acc[...] = jnp.zeros_like(acc)

        acc[...] += _dot(a_ref[...], b_ref[...], 0 if ta else 1, 1 if tb else 0)

        @pl.when(kk == nk - 1)
        def _():
            out = acc[...]
            if residual is not None:
                out = out + r_ref[...]
            o_ref[...] = out.astype(out_dtype)

    a_spec = pl.BlockSpec((tk, tm), lambda i, j, kk: (kk, i)) if ta else pl.BlockSpec((tm, tk), lambda i, j, kk: (i, kk))
    b_spec = pl.BlockSpec((tn, tk), lambda i, j, kk: (j, kk)) if tb else pl.BlockSpec((tk, tn), lambda i, j, kk: (kk, j))
    o_spec = pl.BlockSpec((tm, tn), lambda i, j, kk: (i, j))
    in_specs, args = [a_spec, b_spec], [a, b]
    if residual is not None:
        in_specs.append(o_spec)
        args.append(residual)
    return pl.pallas_call(
        body, name=name, grid=(m // tm, n // tn, nk), in_specs=in_specs, out_specs=o_spec,
        out_shape=SDS((m, n), out_dtype), scratch_shapes=[pltpu.VMEM((tm, tn), F32)],
        compiler_params=_params(("parallel", "parallel", "arbitrary")))(*args)


def norm_matmul(h, gain, w, *, name, tm=512, tn=512):
    t, d = h.shape
    tm = min(tm, t)
    n = w.shape[1]
    assert t % tm == 0 and n % tn == 0

    def body(h_ref, g_ref, w_ref, u_ref, n_ref):
        @pl.when(pl.program_id(1) == 0)
        def _():
            n_ref[...] = _rms_fwd(h_ref[...], g_ref[...])[0].astype(BF16)

        u_ref[...] = _nn(n_ref[...], w_ref[...])

    return pl.pallas_call(
        body, name=name, grid=(t // tm, n // tn),
        in_specs=[pl.BlockSpec((tm, d), lambda i, j: (i, 0)), _whole((1, d)), pl.BlockSpec((d, tn), lambda i, j: (0, j))],
        out_specs=[pl.BlockSpec((tm, tn), lambda i, j: (i, j)), pl.BlockSpec((tm, d), lambda i, j: (i, 0))],
        out_shape=[SDS((t, n), F32), SDS((t, d), BF16)],
        compiler_params=_params(("parallel", "arbitrary")))(h, gain, w)


def rms_bwd(h, gain, dn, dres, *, name, tm=512):
    t, d = h.shape
    tm = min(tm, t)

    def body(h_ref, g_ref, dn_ref, dr_ref, dh_ref, dg_ref):
        @pl.when(pl.program_id(0) == 0)
        def _():
            dg_ref[...] = jnp.zeros_like(dg_ref)

        g = g_ref[...]
        _, xh, r = _rms_fwd(h_ref[...], g)
        dh, dg = _rms_bwd(dn_ref[...], xh, r, g)
        dh_ref[...] = dr_ref[...] + dh
        dg_ref[...] += dg

    row = pl.BlockSpec((tm, d), lambda i: (i, 0))
    return pl.pallas_call(
        body, name=name, grid=(t // tm,), in_specs=[row, _whole((1, d)), row, row],
        out_specs=[row, _whole((1, d))], out_shape=[SDS((t, d), F32), SDS((1, d), F32)],
        compiler_params=_params(("arbitrary",)))(h, gain, dn, dres)


def ffn_fwd(h, gain, wg, wu, wd, *, name, tm=512):
    t, d = h.shape
    tm = min(tm, t)

    def body(h_ref, g_ref, wg_ref, wu_ref, wd_ref, o_ref, n_sc, acc):
        j = pl.program_id(1)

        @pl.when(j == 0)
        def _():
            n_sc[...] = _rms_fwd(h_ref[...], g_ref[...])[0].astype(BF16)
            acc[...] = jnp.zeros_like(acc)

        n = n_sc[...]
        a = _nn(n, wg_ref[...])
        b = _nn(n, wu_ref[...])
        acc[...] += _nn(a * _sigmoid(a) * b, wd_ref[...])

        @pl.when(j == N_CHIP - 1)
        def _():
            o_ref[...] = h_ref[...] + 0.5 * acc[...]

    row = pl.BlockSpec((tm, d), lambda i, j: (i, 0))
    return pl.pallas_call(
        body, name=name, grid=(t // tm, N_CHIP),
        in_specs=[row, _whole((1, d)),
                  pl.BlockSpec((None, d, FF_BLK), lambda i, j: (j, 0, 0)),
                  pl.BlockSpec((None, d, FF_BLK), lambda i, j: (j, 0, 0)),
                  pl.BlockSpec((None, FF_BLK, d), lambda i, j: (j, 0, 0))],
        out_specs=row, out_shape=SDS((t, d), F32),
        scratch_shapes=[pltpu.VMEM((tm, d), BF16), pltpu.VMEM((tm, d), F32)],
        compiler_params=_params(("parallel", "arbitrary")))(h, gain, wg, wu, wd)


def ffn_bwd_act(h, gain, dout, wg, wu, wd, *, name, tm=512):
    t, d = h.shape
    tm = min(tm, t)

    def body(h_ref, g_ref, do_ref, wg_ref, wu_ref, wd_ref, dh_ref, n_ref, da_ref, db_ref, s_ref, dg_ref, dn_acc):
        i, j = pl.program_id(0), pl.program_id(1)

        @pl.when((i == 0) & (j == 0))
        def _():
            dg_ref[...] = jnp.zeros_like(dg_ref)

        @pl.when(j == 0)
        def _():
            n_ref[...] = _rms_fwd(h_ref[...], g_ref[...])[0].astype(BF16)
            dn_acc[...] = jnp.zeros_like(dn_acc)

        n = n_ref[...]
        a = _nn(n, wg_ref[...])
        b = _nn(n, wu_ref[...])
        sig = _sigmoid(a)
        sa = a * sig
        ds = _nt(0.5 * do_ref[...], wd_ref[...])
        db = ds * sa
        da = ds * b * (sig * (1.0 + a * (1.0 - sig)))
        s_ref[...] = (sa * b).astype(BF16)
        da_ref[...] = da.astype(BF16)
        db_ref[...] = db.astype(BF16)
        dn_acc[...] += _nt(da, wg_ref[...]) + _nt(db, wu_ref[...])

        @pl.when(j == N_CHIP - 1)
        def _():
            g = g_ref[...]
            _, xh, r = _rms_fwd(h_ref[...], g)
            dh, dg = _rms_bwd(dn_acc[...], xh, r, g)
            dh_ref[...] = do_ref[...] + dh
            dg_ref[...] += dg

    row = pl.BlockSpec((tm, d), lambda i, j: (i, 0))
    blk = pl.BlockSpec((None, tm, FF_BLK), lambda i, j: (j, i, 0))
    act = SDS((N_CHIP, t, FF_BLK), BF16)
    return pl.pallas_call(
        body, name=name, grid=(t // tm, N_CHIP),
        in_specs=[row, _whole((1, d)), row,
                  pl.BlockSpec((None, d, FF_BLK), lambda i, j: (j, 0, 0)),
                  pl.BlockSpec((None, d, FF_BLK), lambda i, j: (j, 0, 0)),
                  pl.BlockSpec((None, FF_BLK, d), lambda i, j: (j, 0, 0))],
        out_specs=[row, row, blk, blk, blk, _whole((1, d))],
        out_shape=[SDS((t, d), F32), SDS((t, d), BF16), act, act, act, SDS((1, d), F32)],
        scratch_shapes=[pltpu.VMEM((tm, d), F32)],
        compiler_params=_params(("arbitrary", "arbitrary")))(h, gain, dout, wg, wu, wd)


def ffn_bwd_w(n, da, db, s, dout, *, name, tk=512):
    t, d = n.shape
    tk = min(tk, t)

    def body(n_ref, da_ref, db_ref, s_ref, do_ref, dwg_ref, dwu_ref, dwd_ref):
        @pl.when(pl.program_id(1) == 0)
        def _():
            dwg_ref[...] = jnp.zeros_like(dwg_ref)
            dwu_ref[...] = jnp.zeros_like(dwu_ref)
            dwd_ref[...] = jnp.zeros_like(dwd_ref)

        nn = n_ref[...]
        dwg_ref[...] += _tn(nn, da_ref[...])
        dwu_ref[...] += _tn(nn, db_ref[...])
        dwd_ref[...] += _tn(s_ref[...], 0.5 * do_ref[...])

    row = pl.BlockSpec((tk, d), lambda j, kk: (kk, 0))
    blk = pl.BlockSpec((None, tk, FF_BLK), lambda j, kk: (j, kk, 0))
    return pl.pallas_call(
        body, name=name, grid=(N_CHIP, t // tk), in_specs=[row, blk, blk, blk, row],
        out_specs=[pl.BlockSpec((None, d, FF_BLK), lambda j, kk: (j, 0, 0)),
                   pl.BlockSpec((None, d, FF_BLK), lambda j, kk: (j, 0, 0)),
                   pl.BlockSpec((None, FF_BLK, d), lambda j, kk: (j, 0, 0))],
        out_shape=[SDS((N_CHIP, d, FF_BLK), F32), SDS((N_CHIP, d, FF_BLK), F32), SDS((N_CHIP, FF_BLK, d), F32)],
        compiler_params=_params(("parallel", "arbitrary")))(n, da, db, s, dout)


def ple_fwd(h, gain, wpg, pl_in, wpp, *, name, tm=512):
    t, d = h.shape
    tm = min(tm, t)
    pd = pl_in.shape[1]

    def body(h_ref, g_ref, wpg_ref, p_ref, wpp_ref, o_ref):
        hh = h_ref[...]
        n = _rms_fwd(hh, g_ref[...])[0]
        gate = _sigmoid(_nn(n, wpg_ref[...]))
        o_ref[...] = hh + gate * _nn(p_ref[...], wpp_ref[...])

    row = pl.BlockSpec((tm, d), lambda i: (i, 0))
    return pl.pallas_call(
        body, name=name, grid=(t // tm,),
        in_specs=[row, _whole((1, d)), _whole((d, d)), pl.BlockSpec((tm, pd), lambda i: (i, 0)), _whole((pd, d))],
        out_specs=row, out_shape=SDS((t, d), F32), compiler_params=_params(("parallel",)))(h, gain, wpg, pl_in, wpp)


def ple_bwd(h, gain, wpg, pl_in, wpp, dout, *, name, tm=512):
    t, d = h.shape
    tm = min(tm, t)
    pd = pl_in.shape[1]

    def body(h_ref, g_ref, wpg_ref, p_ref, wpp_ref, do_ref, dh_ref, n_ref, dga_ref, dpp_ref, dg_ref):
        @pl.when(pl.program_id(0) == 0)
        def _():
            dg_ref[...] = jnp.zeros_like(dg_ref)

        g = g_ref[...]
        n, xh, r = _rms_fwd(h_ref[...], g)
        gate = _sigmoid(_nn(n, wpg_ref[...]))
        pp = _nn(p_ref[...], wpp_ref[...])
        do = do_ref[...]
        dga = do * pp * gate * (1.0 - gate)
        dh, dg = _rms_bwd(_nt(dga, wpg_ref[...]), xh, r, g)
        dh_ref[...] = do + dh
        n_ref[...] = n.astype(BF16)
        dga_ref[...] = dga.astype(BF16)
        dpp_ref[...] = (do * gate).astype(BF16)
        dg_ref[...] += dg

    row = pl.BlockSpec((tm, d), lambda i: (i, 0))
    return pl.pallas_call(
        body, name=name, grid=(t // tm,),
        in_specs=[row, _whole((1, d)), _whole((d, d)), pl.BlockSpec((tm, pd), lambda i: (i, 0)), _whole((pd, d)), row],
        out_specs=[row, row, row, row, _whole((1, d))],
        out_shape=[SDS((t, d), F32), SDS((t, d), BF16), SDS((t, d), BF16), SDS((t, d), BF16), SDS((1, d), F32)],
        compiler_params=_params(("arbitrary",)))(h, gain, wpg, pl_in, wpp, dout)


def loss_head(h, gain, target, *, name, tm=512):
    t, d = h.shape
    tm = min(tm, t)

    def body(h_ref, g_ref, t_ref, dh_ref, dg_ref, l_ref):
        @pl.when(pl.program_id(0) == 0)
        def _():
            dg_ref[...] = jnp.zeros_like(dg_ref)
            l_ref[...] = jnp.zeros_like(l_ref)

        g = g_ref[...]
        y, xh, r = _rms_fwd(h_ref[...], g)
        err = y - t_ref[...]
        l_ref[...] += 0.5 * jnp.sum(jnp.mean(err * err, axis=-1, keepdims=True), axis=0, keepdims=True)
        dh, dg = _rms_bwd(err * (1.0 / d), xh, r, g)
        dh_ref[...] = dh
        dg_ref[...] += dg

    row = pl.BlockSpec((tm, d), lambda i: (i, 0))
    return pl.pallas_call(
        body, name=name, grid=(t // tm,), in_specs=[row, _whole((1, d)), row],
        out_specs=[row, _whole((1, d)), _whole((1, 1))],
        out_shape=[SDS((t, d), F32), SDS((1, d), F32), SDS((1, 1), F32)],
        compiler_params=_params(("arbitrary",)))(h, gain, target)


def adamw(w, g, m, v, *, name):
    r, c = w.shape
    tr = r
    for cand in (512, 256, 128, 64, 32, 16, 8):
        if r % cand == 0:
            tr = cand
            break

    def body(w_ref, g_ref, m_ref, v_ref, d_ref, nm_ref, nv_ref):
        gg = g_ref[...]
        mm = ADAM_B1 * m_ref[...] + (1.0 - ADAM_B1) * gg
        vv = ADAM_B2 * v_ref[...] + (1.0 - ADAM_B2) * (gg * gg)
        m_hat = mm / (1.0 - ADAM_B1 ** ADAM_STEP)
        v_hat = vv / (1.0 - ADAM_B2 ** ADAM_STEP)
        d_ref[...] = -ADAM_LR * (m_hat / (jnp.sqrt(v_hat) + ADAM_EPS) + ADAM_WD * w_ref[...])
        nm_ref[...] = mm
        nv_ref[...] = vv

    blk = pl.BlockSpec((tr, c), lambda i: (i, 0))
    out = SDS((r, c), F32)
    return pl.pallas_call(body, name=name, grid=(r // tr,), in_specs=[blk] * 4, out_specs=[blk] * 3,
                          out_shape=[out, out, out], compiler_params=_params(("parallel",)))(w, g, m, v)


def _scan_fwd(a, b):
    d = 1
    while d < a.shape[0]:
        b = a * _shift_down(b, d, 0.0) + b
        a = a * _shift_down(a, d, 1.0)
        d *= 2
    return b


def _scan_rev(a, b):
    d = 1
    while d < a.shape[0]:
        b = a * _shift_up(b, d, 0.0) + b
        a = a * _shift_up(a, d, 1.0)
        d *= 2
    return b


LRU_HALF = 128


def _lru_in_specs(seq):
    half = LRU_W // LRU_HALF
    vec = pl.BlockSpec((1, LRU_HALF), lambda j, b: (0, j))
    mat = pl.BlockSpec((LRU_HALF, LRU_HALF), lambda j, b: (j, j))
    return [pl.BlockSpec((seq, LRU_HALF), lambda j, b: (b, j)), pl.BlockSpec((seq, LRU_HALF), lambda j, b: (b, half + j)),
            pl.BlockSpec((4, LRU_HALF), lambda j, b: (0, j)), vec, mat, vec, mat, vec, vec]


def _lru_math(x_ref, gate_ref, cw_ref, cb_ref, wa_ref, ba_ref, wx_ref, bx_ref, lam_ref):
    x = x_ref[...]
    gate = gate_ref[...]
    cw =[cw_ref[k:k + 1, :] for k in range(4)]
    xr = _conv_fwd(x, cw) + cb_ref[...]
    r = _sigmoid(_nn(xr, wa_ref[...]) + ba_ref[...])
    i = _sigmoid(_nn(xr, wx_ref[...]) + bx_ref[...])
    sp = _softplus(-lam_ref[...])
    log_a = -LRU_C * r * sp
    a = jnp.exp(log_a)
    mult = jnp.sqrt(_neg_expm1(2.0 * log_a))
    gi = i * xr
    h = _scan_fwd(a, mult * gi)
    gl, tg = _gelu(gate)
    return dict(x=x, gate=gate, cw=cw, xr=xr, r=r, i=i, sp=sp, a=a, mult=mult, gi=gi, h=h, gl=gl, tg=tg)


def lru_fwd(u, cw, cb, wa, ba, wx, bx, lam, *, seq, name):
    t = u.shape[0]

    def body(x_ref, gate_ref, cw_ref, cb_ref, wa_ref, ba_ref, wx_ref, bx_ref, lam_ref, y_ref):
        f = _lru_math(x_ref, gate_ref, cw_ref, cb_ref, wa_ref, ba_ref, wx_ref, bx_ref, lam_ref)
        y_ref[...] = f["gl"] * f["h"]

    return pl.pallas_call(
        body, name=name, grid=(LRU_W // LRU_HALF, t // seq), in_specs=_lru_in_specs(seq),
        out_specs=pl.BlockSpec((seq, LRU_HALF), lambda j, b: (b, j)), out_shape=SDS((t, LRU_W), F32),
        compiler_params=_params(("parallel", "parallel")))(u, u, cw, cb, wa, ba, wx, bx, lam)


def lru_bwd(u, cw, cb, wa, ba, wx, bx, lam, dy, *, seq, name):
    t = u.shape[0]

    def body(x_ref, gate_ref, cw_ref, cb_ref, wa_ref, ba_ref, wx_ref, bx_ref, lam_ref, dy_ref,
             dx_ref, dgate_ref, dcw_ref, dwa_ref, dwx_ref, dv_ref):
        @pl.when(pl.program_id(1) == 0)
        def _():
            dcw_ref[...] = jnp.zeros_like(dcw_ref)
            dwa_ref[...] = jnp.zeros_like(dwa_ref)
            dwx_ref[...] = jnp.zeros_like(dwx_ref)
            dv_ref[...] = jnp.zeros_like(dv_ref)

        f = _lru_math(x_ref, gate_ref, cw_ref, cb_ref, wa_ref, ba_ref, wx_ref, bx_ref, lam_ref)
        dy = dy_ref[...]
        a, h, xr, r, i, mult, gi, sp = f["a"], f["h"], f["xr"], f["r"], f["i"], f["mult"], f["gi"], f["sp"]
        dgate_ref[...] = dy * h * _gelu_grad(f["gate"], f["tg"])
        lamb = _scan_rev(_shift_up(a, 1, 0.0), dy * f["gl"])
        da = lamb * _shift_down(h, 1)
        dlog_a = da * a - (lamb * gi) * (a * a) / mult
        dgi = lamb * mult
        dra = dlog_a * (-LRU_C * sp) * r * (1.0 - r)
        dia = dgi * xr * i * (1.0 - i)
        dsp = jnp.sum(dlog_a * (-LRU_C * r), axis=0, keepdims=True)
        dlam = -dsp * _sigmoid(-lam_ref[...])
        dxr = dgi * i + _nt(dra, wa_ref[...]) + _nt(dia, wx_ref[...])
        dx, dcw = _conv_bwd(dxr, f["x"], f["cw"])
        dx_ref[...] = dx
        dcw_ref[...] += dcw
        dwa_ref[...] += _tn(xr, dra)
        dwx_ref[...] += _tn(xr, dia)
        rows = [jnp.sum(dxr, axis=0, keepdims=True), jnp.sum(dra, axis=0, keepdims=True),
                jnp.sum(dia, axis=0, keepdims=True), dlam]
        r8 = lax.broadcasted_iota(jnp.int32, (8, LRU_HALF), 0)
        acc = jnp.zeros((8, LRU_HALF), F32)
        for k, row in enumerate(rows):
            acc = jnp.where(r8 == k, row, acc)
        dv_ref[...] += acc

    nhalf = LRU_W // LRU_HALF
    col = pl.BlockSpec((seq, LRU_HALF), lambda j, b: (b, j))
    mat = pl.BlockSpec((None, LRU_HALF, LRU_HALF), lambda j, b: (j, 0, 0))
    return pl.pallas_call(
        body, name=name, grid=(nhalf, t // seq), in_specs=_lru_in_specs(seq) + [col],
        out_specs=[col, col, pl.BlockSpec((4, LRU_HALF), lambda j, b: (0, j)), mat, mat,
                   pl.BlockSpec((8, LRU_HALF), lambda j, b: (0, j))],
        out_shape=[SDS((t, LRU_W), F32), SDS((t, LRU_W), F32), SDS((4, LRU_W), F32),
                   SDS((nhalf, LRU_HALF, LRU_HALF), F32), SDS((nhalf, LRU_HALF, LRU_HALF), F32), SDS((8, LRU_W), F32)],
        compiler_params=_params(("arbitrary", "arbitrary")))(u, u, cw, cb, wa, ba, wx, bx, lam, dy)


NEG = -1e30


def _rel_bucket_map():
    dist = (np.arange(BLOCK_Q)[:, None] - np.arange(BLOCK_Q)[None, :]) % BLOCK_Q
    max_exact = REL_BUCKETS // 2
    large = max_exact + (np.log(np.maximum(dist, 1).astype(np.float32) / max_exact)
                         / math.log(BLOCK_Q / max_exact) * (REL_BUCKETS - max_exact)).astype(np.int32)
    large = np.minimum(large, REL_BUCKETS - 1)
    return np.where(dist < max_exact, dist, large).astype(np.int32)


def relbias_fwd(rel_bias, bmap, *, name):
    def body(rb_ref, bm_ref, o_ref):
        bm = bm_ref[...]
        for h in range(ATT_HEADS):
            acc = jnp.zeros((BLOCK_Q, BLOCK_Q), F32)
            for b in range(REL_BUCKETS):
                acc = jnp.where(bm == b, rb_ref[b, h], acc)
            o_ref[h] = acc

    return pl.pallas_call(
        body, name=name, in_specs=[pl.BlockSpec(memory_space=pltpu.SMEM), pl.BlockSpec(memory_space=pltpu.VMEM)],
        out_specs=pl.BlockSpec(memory_space=pltpu.VMEM), out_shape=SDS((ATT_HEADS, BLOCK_Q, BLOCK_Q), F32))(rel_bias, bmap)


def relbias_bwd(dbias, bmap, *, name):
    def body(db_ref, bm_ref, o_ref):
        bm = bm_ref[...]
        row = lax.broadcasted_iota(jnp.int32, (REL_BUCKETS, 128), 0)
        col = lax.broadcasted_iota(jnp.int32, (REL_BUCKETS, 128), 1)
        acc = jnp.zeros((REL_BUCKETS, 128), F32)
        for h in range(ATT_HEADS):
            d = db_ref[h]
            for b in range(REL_BUCKETS):
                s = jnp.sum(jnp.sum(jnp.where(bm == b, d, 0.0), axis=1, keepdims=True), axis=0, keepdims=True)
                acc = jnp.where((row == b) & (col == h), s, acc)
        o_ref[...] = acc

    return pl.pallas_call(body, name=name, out_shape=SDS((REL_BUCKETS, 128), F32))(dbias, bmap)


def _attn_probs(q_ref, k_ref, v_ref, b_ref, s_ref, n):
    rows = ATT_GROUP * BLOCK_Q
    qs = q_ref[...].reshape(rows, HEAD) * (HEAD ** -0.5)
    prev = pl.multiple_of(jnp.maximum(n - 1, 0) * BLOCK_Q, BLOCK_Q)
    cur = pl.multiple_of(n * BLOCK_Q, BLOCK_Q)
    kp, kc = k_ref[pl.ds(prev, BLOCK_Q), :], k_ref[pl.ds(cur, BLOCK_Q), :]
    vp, vc = v_ref[pl.ds(prev, BLOCK_Q), :], v_ref[pl.ds(cur, BLOCK_Q), :]
    bias = b_ref[...].reshape(rows, BLOCK_Q)
    i = lax.broadcasted_iota(jnp.int32, (rows, BLOCK_Q), 0) & (BLOCK_Q - 1)
    j = lax.broadcasted_iota(jnp.int32, (rows, BLOCK_Q), 1)
    s_p = jnp.where((j > i) & (n > 0), _nt(qs, kp) + bias, NEG)
    s_c = jnp.where(j <= i, _nt(qs, kc) + bias, NEG)
    sink = s_ref[...]
    m = jnp.maximum(jnp.maximum(jnp.max(s_p, axis=-1, keepdims=True), jnp.max(s_c, axis=-1, keepdims=True)), sink)
    e_p, e_c, e_s = jnp.exp(s_p - m), jnp.exp(s_c - m), jnp.exp(sink - m)
    inv = 1.0 / (jnp.sum(e_p, axis=-1, keepdims=True) + jnp.sum(e_c, axis=-1, keepdims=True) + e_s)
    return e_p * inv, e_c * inv, e_s * inv, qs, kp, kc, vp, vc, prev, cur


def _attn_specs(seq):
    qspec = pl.BlockSpec((None, ATT_GROUP, BLOCK_Q, HEAD), lambda g, b, n: (b, g, n, 0))
    kvspec = pl.BlockSpec((None, None, seq, HEAD), lambda g, b, n: (b, g, 0, 0))
    bspec = pl.BlockSpec((ATT_GROUP, BLOCK_Q, BLOCK_Q), lambda g, b, n: (g, 0, 0))
    sspec = pl.BlockSpec((ATT_GROUP * BLOCK_Q, 1), lambda g, b, n: (g, 0))
    return qspec, kvspec, bspec, sspec


def attn_fwd(q, k, v, bias, sink_rows, *, name):
    nb, _, seq, _ = q.shape

    def body(q_ref, k_ref, v_ref, b_ref, s_ref, o_ref):
        p_p, p_c, _, _, _, _, vp, vc, _, _ = _attn_probs(q_ref, k_ref, v_ref, b_ref, s_ref, pl.program_id(2))
        o_ref[...] = (_nn(p_p, vp) + _nn(p_c, vc)).reshape(ATT_GROUP, BLOCK_Q, HEAD)

    qspec, kvspec, bspec, sspec = _attn_specs(seq)
    return pl.pallas_call(
        body, name=name, grid=(KV_HEADS, nb, seq // BLOCK_Q), in_specs=[qspec, kvspec, kvspec, bspec, sspec],
        out_specs=qspec, out_shape=SDS(q.shape, F32),
        compiler_params=_params(("parallel", "parallel", "arbitrary")))(q, k, v, bias, sink_rows)


def attn_bwd(q, k, v, bias, sink_rows, do, *, name):
    nb, _, seq, _ = q.shape

    def body(q_ref, k_ref, v_ref, b_ref, s_ref, do_ref, dq_ref, dk_ref, dv_ref, db_ref, ds_ref):
        b, n = pl.program_id(1), pl.program_id(2)

        @pl.when((b == 0) & (n == 0))
        def _():
            db_ref[...] = jnp.zeros_like(db_ref)
            ds_ref[...] = jnp.zeros_like(ds_ref)

        @pl.when(n == 0)
        def _():
            dk_ref[...] = jnp.zeros_like(dk_ref)
            dv_ref[...] = jnp.zeros_like(dv_ref)

        p_p, p_c, p_s, qs, kp, kc, vp, vc, prev, cur = _attn_probs(q_ref, k_ref, v_ref, b_ref, s_ref, n)
        do = do_ref[...].reshape(ATT_GROUP * BLOCK_Q, HEAD)
        dp_p, dp_c = _nt(do, vp), _nt(do, vc)
        delta = jnp.sum(p_p * dp_p, axis=-1, keepdims=True) + jnp.sum(p_c * dp_c, axis=-1, keepdims=True)
        ds_p, ds_c = p_p * (dp_p - delta), p_c * (dp_c - delta)
        dq_ref[...] = ((_nn(ds_p, kp) + _nn(ds_c, kc)) * (HEAD ** -0.5)).reshape(ATT_GROUP, BLOCK_Q, HEAD)
        dk_ref[pl.ds(prev, BLOCK_Q), :] += _tn(ds_p, qs)
        dk_ref[pl.ds(cur, BLOCK_Q), :] += _tn(ds_c, qs)
        dv_ref[pl.ds(prev, BLOCK_Q), :] += _tn(p_p, do)
        dv_ref[pl.ds(cur, BLOCK_Q), :] += _tn(p_c, do)
        db_ref[...] += (ds_p + ds_c).reshape(ATT_GROUP, BLOCK_Q, BLOCK_Q)
        ds_ref[...] += -p_s * delta

    qspec, kvspec, bspec, sspec = _attn_specs(seq)
    return pl.pallas_call(
        body, name=name, grid=(KV_HEADS, nb, seq // BLOCK_Q), in_specs=[qspec, kvspec, kvspec, bspec, sspec, qspec],
        out_specs=[qspec, kvspec, kvspec, bspec, sspec],
        out_shape=[SDS(q.shape, F32), SDS(k.shape, F32), SDS(v.shape, F32),
                   SDS((ATT_HEADS, BLOCK_Q, BLOCK_Q), F32), SDS((ATT_HEADS * BLOCK_Q, 1), F32)],
        compiler_params=_params(("arbitrary", "arbitrary", "arbitrary")))(q, k, v, bias, sink_rows, do)


def _iota2(shape, axis):
    return lax.broadcasted_iota(jnp.int32, shape, axis)


def _col_to_row(col):
    c = col.shape[0]
    eye = _iota2((c, c), 0) == _iota2((c, c), 1)
    return jnp.sum(jnp.where(eye, jnp.broadcast_to(col, (c, c)), 0.0), axis=0, keepdims=True)


def _row_to_col(row):
    c = row.shape[1]
    eye = _iota2((c, c), 0) == _iota2((c, c), 1)
    return jnp.sum(jnp.where(eye, jnp.broadcast_to(row, (c, c)), 0.0), axis=1, keepdims=True)


def _last_row(col):
    c = col.shape[0]
    return jnp.sum(jnp.where(_iota2((c, 1), 0) == c - 1, col, 0.0), axis=0, keepdims=True)


def _chunk_cumsum(x):
    pos = _iota2(x.shape, 0) & (DN_CHUNK - 1)
    d = 1
    while d < DN_CHUNK:
        x = x + jnp.where(pos >= d, pltpu.roll(x, d, 0), 0.0)
        d *= 2
    return x


def _chunk_rev_cumsum(x):
    n = x.shape[0]
    pos = _iota2(x.shape, 0) & (DN_CHUNK - 1)
    d = 1
    while d < DN_CHUNK:
        x = x + jnp.where(pos < DN_CHUNK - d, pltpu.roll(x, n - d, 0), 0.0)
        d *= 2
    return x


def _tri_inv(low):
    c = low.shape[0]
    eye = (_iota2((c, c), 0) == _iota2((c, c), 1)).astype(F32)
    m = -low
    p = eye + m
    steps = int(math.log2(c)) - 1
    for _ in range(steps):
        m = _nn(m, m, hi=True)
        p = p + _nn(p, m, hi=True)
    return p


_DN_SCALE = (HEAD ** -0.5, 1.0, None)


def _dn_act(c, scale):
    sig = _sigmoid(c)
    a = c * sig
    if scale is None:
        return a, sig, None, None
    r = lax.rsqrt(jnp.sum(a * a, axis=-1, keepdims=True) + EPS)
    return a * r * scale, sig, a * r, r


def _dn_gates(ba_ref, hs_ref):
    beta = _sigmoid(ba_ref[0])
    sp_arg = ba_ref[1] + hs_ref[1]
    a_exp = jnp.exp(hs_ref[0])
    g = -a_exp * _softplus(sp_arg)
    return beta, g, sp_arg, a_exp


def _dn_inputs(pre_ref, cw_ref, ba_ref, hs_ref, act_sc, b_sc, gc_sc, c_sc=None):
    for idx in range(3):
        c = _conv_fwd(pre_ref[idx], [cw_ref[idx, k:k + 1, :] for k in range(4)])
        if c_sc is not None:
            c_sc[idx] = c
        act_sc[idx] = _dn_act(c, _DN_SCALE[idx])[0]
    beta, g, _, _ = _dn_gates(ba_ref, hs_ref)
    b_sc[...] = beta
    gc_sc[...] = _chunk_cumsum(g)


def _dn_chunk_math(q, k, v, b, gcc):
    c = q.shape[0]
    tril = _iota2((c, c), 0) >= _iota2((c, c), 1)
    strict = _iota2((c, c), 0) > _iota2((c, c), 1)
    eg = jnp.exp(gcc)
    kb, vb = k * b, v * b
    kbg = kb * eg
    dm = jnp.exp(jnp.where(tril, jnp.broadcast_to(gcc, (c, c)) - _col_to_row(gcc), NEG))
    kk = _nt(kb, k)
    t = _tri_inv(jnp.where(strict, kk * dm, 0.0))
    glast = _last_row(gcc)
    ekd = jnp.exp(glast - gcc)
    qk = _nt(q, k)
    return dict(tril=tril, strict=strict, eg=eg, kb=kb, vb=vb, kbg=kbg, dm=dm, kk=kk, t=t, glast=glast, ekd=ekd,
                kd=k * ekd, qk=qk, amat=jnp.where(tril, qk * dm, 0.0), qg=q * eg,
                egl=jnp.broadcast_to(jnp.exp(glast), (c, 1)))


def _dn_specs(seq):
    s64 = lambda lead: pl.BlockSpec((lead, None, None, seq, HEAD), lambda b, h: (0, b, h, 0, 0))
    s1 = lambda lead: pl.BlockSpec((lead, None, None, seq, 1), lambda b, h: (0, b, h, 0, 0))
    one64 = pl.BlockSpec((None, None, seq, HEAD), lambda b, h: (b, h, 0, 0))
    one1 = pl.BlockSpec((None, None, seq, 1), lambda b, h: (b, h, 0, 0))
    cw = pl.BlockSpec((None, 3, 4, HEAD), lambda b, h: (h, 0, 0, 0))
    hs = pl.BlockSpec((None, 2, 1, 1), lambda b, h: (h, 0, 0, 0))
    return s64, s1, one64, one1, cw, hs


def dn_prep(pre, cw, ba, hs, *, name):
    _, nb, nh, seq, _ = pre.shape
    nc = seq // DN_CHUNK

    def body(pre_ref, cw_ref, ba_ref, hs_ref, loc_ref, egl_ref, act_sc, b_sc, gc_sc):
        _dn_inputs(pre_ref, cw_ref, ba_ref, hs_ref, act_sc, b_sc, gc_sc)

        def step(c, carry):
            rows = pl.ds(pl.multiple_of(c * DN_CHUNK, DN_CHUNK), DN_CHUNK)
            m = _dn_chunk_math(act_sc[0, rows, :], act_sc[1, rows, :], act_sc[2, rows, :], b_sc[rows, :], gc_sc[rows, :])
            loc_ref[0, rows, :] = m["qg"]
            loc_ref[1, rows, :] = m["kd"]
            loc_ref[2, rows, :] = _nn(m["t"], m["vb"])
            loc_ref[3, rows, :] = _nn(m["t"], m["kbg"])
            loc_ref[4, rows, :] = m["amat"]
            egl_ref[rows, :] = m["egl"]
            return carry

        lax.fori_loop(0, nc, step, 0)

    s64, s1, one64, one1, cwspec, hsspec = _dn_specs(seq)
    return pl.pallas_call(
        body, name=name, grid=(nb, nh), in_specs=[s64(3), cwspec, s1(2), hsspec], out_specs=[s64(5), one1],
        out_shape=[SDS((5, nb, nh, seq, HEAD), F32), SDS((nb, nh, seq, 1), F32)],
        scratch_shapes=[pltpu.VMEM((3, seq, HEAD), F32)] + [pltpu.VMEM((seq, 1), F32)] * 2,
        compiler_params=_params(("parallel", "parallel")))(pre, cw, ba, hs)


def _gated_norm(o, z, gn):
    r = lax.rsqrt(jnp.mean(o * o, axis=-1, keepdims=True) + EPS)
    sig = _sigmoid(z)
    return o * r, sig, r


def dn_scan(loc, egl, z, gn, *, name):
    _, nb, nh, seq, _ = loc.shape
    nc = seq // DN_CHUNK

    def body(loc_ref, egl_ref, z_ref, gn_ref, y_ref, o_ref, vn_ref, st_ref):
        gn = gn_ref[...]

        def step(c, state):
            rows = pl.ds(pl.multiple_of(c * DN_CHUNK, DN_CHUNK), DN_CHUNK)
            st_ref[rows, :] = state
            vn = loc_ref[2, rows, :] - _nn(loc_ref[3, rows, :], state)
            o = _nn(loc_ref[0, rows, :], state) + _nn(loc_ref[4, rows, :], vn)
            vn_ref[rows, :] = vn
            o_ref[rows, :] = o
            zz = z_ref[rows, :]
            on, sig, _ = _gated_norm(o, zz, gn)
            y_ref[rows, :] = on * gn * (zz * sig)
            return state * egl_ref[rows, :] + _tn(loc_ref[1, rows, :], vn)

        lax.fori_loop(0, nc, step, jnp.zeros((HEAD, HEAD), F32))

    s64, s1, one64, one1, cwspec, hsspec = _dn_specs(seq)
    out = SDS((nb, nh, seq, HEAD), F32)
    return pl.pallas_call(
        body, name=name, grid=(nb, nh), in_specs=[s64(5), one1, one64, _whole((1, HEAD))],
        out_specs=[one64] * 4, out_shape=[out] * 4,
        compiler_params=_params(("parallel", "parallel")))(loc, egl, z, gn)


def dn_scan_bwd(loc, egl, z, gn, o, vn, states, dy, *, name):
    _, nb, nh, seq, _ = loc.shape
    nc = seq // DN_CHUNK

    def body(loc_ref, egl_ref, z_ref, gn_ref, o_ref, vn_ref, st_ref, dy_ref, dloc_ref, degl_ref, dz_ref, dgn_ref):
        @pl.when((pl.program_id(0) == 0) & (pl.program_id(1) == 0))
        def _():
            dgn_ref[...] = jnp.zeros_like(dgn_ref)

        gn = gn_ref[...]
        tril = _iota2((DN_CHUNK, DN_CHUNK), 0) >= _iota2((DN_CHUNK, DN_CHUNK), 1)

        def step(i, carry):
            ds, dgn = carry
            rows = pl.ds(pl.multiple_of((nc - 1 - i) * DN_CHUNK, DN_CHUNK), DN_CHUNK)
            dy, zz, oo = dy_ref[rows, :], z_ref[rows, :], o_ref[rows, :]
            on, sig, r = _gated_norm(oo, zz, gn)
            sz = zz * sig
            dz_ref[rows, :] = dy * on * gn * (sig * (1.0 + zz * (1.0 - sig)))
            dgn = dgn + jnp.sum(dy * on * sz, axis=0, keepdims=True)
            don = dy * gn * sz
            do = r * (don - on * jnp.mean(don * on, axis=-1, keepdims=True))
            state, vnew = st_ref[rows, :], vn_ref[rows, :]
            qg, kd, w, amat = loc_ref[0, rows, :], loc_ref[1, rows, :], loc_ref[3, rows, :], loc_ref[4, rows, :]
            dvn = _tn(amat, do) + _nn(kd, ds)
            dloc_ref[0, rows, :] = _nt(do, state)
            dloc_ref[1, rows, :] = _nt(vnew, ds)
            dloc_ref[2, rows, :] = dvn
            dloc_ref[3, rows, :] = -_nt(dvn, state)
            dloc_ref[4, rows, :] = jnp.where(tril, _nt(do, vnew), 0.0)
            degl = jnp.sum(jnp.sum(state * ds, axis=1, keepdims=True), axis=0, keepdims=True)
            degl_ref[rows, :] = jnp.broadcast_to(degl, (DN_CHUNK, 1))
            return ds * egl_ref[rows, :] + _tn(qg, do) - _tn(w, dvn), dgn

        _, dgn = lax.fori_loop(0, nc, step, (jnp.zeros((HEAD, HEAD), F32), jnp.zeros((1, HEAD), F32)))
        dgn_ref[...] += dgn

    s64, s1, one64, one1, cwspec, hsspec = _dn_specs(seq)
    return pl.pallas_call(
        body, name=name, grid=(nb, nh),
        in_specs=[s64(5), one1, one64, _whole((1, HEAD)), one64, one64, one64, one64],
        out_specs=[s64(5), one1, one64, _whole((1, HEAD))],
        out_shape=[SDS((5, nb, nh, seq, HEAD), F32), SDS((nb, nh, seq, 1), F32), SDS((nb, nh, seq, HEAD), F32),
                   SDS((1, HEAD), F32)],
        compiler_params=_params(("arbitrary", "arbitrary")))(loc, egl, z, gn, o, vn, states, dy)


def dn_prep_bwd(pre, cw, ba, hs, dloc, degl, *, name):
    _, nb, nh, seq, _ = pre.shape
    nc = seq // DN_CHUNK

    def body(pre_ref, cw_ref, ba_ref, hs_ref, dloc_ref, degl_ref, dpre_ref, dba_ref, dcw_ref, dhs_ref,
             act_sc, b_sc, gc_sc, c_sc):
        @pl.when(pl.program_id(1) == 0)
        def _():
            dcw_ref[...] = jnp.zeros_like(dcw_ref)
            dhs_ref[...] = jnp.zeros_like(dhs_ref)

        _dn_inputs(pre_ref, cw_ref, ba_ref, hs_ref, act_sc, b_sc, gc_sc, c_sc)

        def step(c, carry):
            rows = pl.ds(pl.multiple_of(c * DN_CHUNK, DN_CHUNK), DN_CHUNK)
            q, k, v, b, gcc = act_sc[0, rows, :], act_sc[1, rows, :], act_sc[2, rows, :], b_sc[rows, :], gc_sc[rows, :]
            m = _dn_chunk_math(q, k, v, b, gcc)
            dqg, dkd, du, dw, da = (dloc_ref[x, rows, :] for x in range(5))
            t, dm, eg = m["t"], m["dm"], m["eg"]
            dt = _nt(du, m["vb"]) + _nt(dw, m["kbg"])
            dvb, dkbg = _tn(t, du), _tn(t, dw)
            dl = jnp.where(m["strict"], -_tn(t, _nt(dt, t, hi=True), hi=True), 0.0)
            dkk = dl * dm
            dqk = da * dm
            dd = dl * m["kk"] + da * m["qk"]
            dkb = _nn(dkk, k) + dkbg * eg
            dq = _nn(dqk, k) + dqg * eg
            dk = _tn(dkk, m["kb"]) + _tn(dqk, q) + dkd * m["ekd"] + dkb * b
            db = jnp.sum(dkb * k, axis=-1, keepdims=True) + jnp.sum(dvb * v, axis=-1, keepdims=True)
            mx = jnp.where(m["tril"], dd * dm, 0.0)
            tk = jnp.sum(dkd * m["kd"], axis=-1, keepdims=True)
            dgc = (jnp.sum(mx, axis=-1, keepdims=True) - _row_to_col(jnp.sum(mx, axis=0, keepdims=True))
                   + jnp.sum(dqg * m["qg"], axis=-1, keepdims=True) + jnp.sum(dkbg * m["kbg"], axis=-1, keepdims=True) - tk)
            dglast = jnp.sum(tk, axis=0, keepdims=True) + _last_row(degl_ref[rows, :]) * jnp.exp(m["glast"])
            act_sc[0, rows, :] = dq
            act_sc[1, rows, :] = dk
            act_sc[2, rows, :] = dvb * b
            b_sc[rows, :] = db
            gc_sc[rows, :] = dgc + jnp.where(_iota2((DN_CHUNK, 1), 0) == DN_CHUNK - 1, dglast, 0.0)
            return carry

        lax.fori_loop(0, nc, step, 0)

        beta, g, sp_arg, a_exp = _dn_gates(ba_ref, hs_ref)
        dg = _chunk_rev_cumsum(gc_sc[...])
        dal = dg * (-a_exp) * _sigmoid(sp_arg)
        dba_ref[0] = b_sc[...] * beta * (1.0 - beta)
        dba_ref[1] = dal
        dhs_ref[0] += jnp.sum(dg * g, axis=0, keepdims=True)
        dhs_ref[1] += jnp.sum(dal, axis=0, keepdims=True)
        for idx in range(3):
            c = c_sc[idx]
            _, sig, hat, r = _dn_act(c, _DN_SCALE[idx])
            da_ = act_sc[idx]
            if _DN_SCALE[idx] is not None:
                da_ = da_ * _DN_SCALE[idx]
                da_ = r * (da_ - hat * jnp.sum(da_ * hat, axis=-1, keepdims=True))
            dx, dcw = _conv_bwd(da_ * (sig * (1.0 + c * (1.0 - sig))), pre_ref[idx],
                                [cw_ref[idx, k:k + 1, :] for k in range(4)])
            dpre_ref[idx] = dx
            dcw_ref[idx] += dcw

    s64, s1, one64, one1, cwspec, hsspec = _dn_specs(seq)
    swap = lambda spec: pl.BlockSpec(spec.block_shape, lambda h, b, _f=spec.index_map: _f(b, h))
    return pl.pallas_call(
        body, name=name, grid=(nh, nb),
        in_specs=[swap(s64(3)), swap(cwspec), swap(s1(2)), swap(hsspec), swap(s64(5)), swap(one1)],
        out_specs=[swap(s64(3)), swap(s1(2)), swap(cwspec), swap(hsspec)],
        out_shape=[SDS((3, nb, nh, seq, HEAD), F32), SDS((2, nb, nh, seq, 1), F32), SDS((nh, 3, 4, HEAD), F32),
                   SDS((nh, 2, 1, 1), F32)],
        scratch_shapes=[pltpu.VMEM((3, seq, HEAD), F32)] + [pltpu.VMEM((seq, 1), F32)] * 2 + [pltpu.VMEM((3, seq, HEAD), F32)],
        compiler_params=_params(("arbitrary", "arbitrary")))(pre, cw, ba, hs, dloc, degl)


def _block_diag(w):
    out = jnp.zeros((LRU_W, LRU_W), w.dtype)
    for h in range(LRU_W // HEAD):
        out = lax.dynamic_update_slice(out, w[h], (h * HEAD, h * HEAD))
    return out


def _diag_blocks(w):
    per = LRU_HALF // HEAD
    return jnp.stack([w[h // per, (h % per) * HEAD:(h % per + 1) * HEAD, (h % per) * HEAD:(h % per + 1) * HEAD]
                      for h in range(LRU_W // HEAD)])


def layer_params(w, l, bias):
    row = lambda a: a[l].reshape(1, -1)
    return dict(
        ffn1_norm=row(w["ffn1_norm"]), ffn1=(w["ffn1_w_gate"][:, l], w["ffn1_w_up"][:, l], w["ffn1_w_down"][:, l]),
        mix_norm=row(w["mix_norm"]), w_in=w["w_in"][l],
        lru=(w["lru_conv_w"][l], row(w["lru_conv_b"]), _block_diag(w["lru_w_a"][l]), row(w["lru_b_a"]),
             _block_diag(w["lru_w_x"][l]), row(w["lru_b_x"]), row(w["lru_lambda"])),
        bias=bias, sink_rows=jnp.repeat(w["attn_sinks"][l], BLOCK_Q).reshape(ATT_HEADS * BLOCK_Q, 1),
        dn_cw=w["dn_conv_w"][l].reshape(4, 3, DN_HEADS, HEAD).transpose(2, 1, 0, 3),
        dn_hs=jnp.stack([w["dn_a_log"][l], w["dn_dt_bias"][l]], axis=1).reshape(DN_HEADS, 2, 1, 1),
        dn_norm=row(w["dn_norm"]), w_out=w["w_out"][l],
        ffn2_norm=row(w["ffn2_norm"]), ffn2=(w["ffn2_w_gate"][:, l], w["ffn2_w_up"][:, l], w["ffn2_w_down"][:, l]),
        ple_norm=row(w["ple_norm"]), ple_w_gate=w["ple_w_gate"][l], ple_w_proj=w["ple_w_proj"][l])


def _to_heads(a, nb, seq, nh):
    return a.reshape(nb, seq, nh, HEAD).transpose(0, 2, 1, 3)


def _from_heads(a):
    nb, nh, seq, _ = a.shape
    return a.transpose(0, 2, 1, 3).reshape(nb * seq, nh * HEAD)


def mixer_fwd(h, p, nb, seq, tag):
    u, n = norm_matmul(h, p["mix_norm"], p["w_in"], name=f"mix_in_{tag}")
    y_lru = lru_fwd(u, *p["lru"], seq=seq, name=f"lru_fwd_{tag}")
    q = _to_heads(u[:, 512:1024], nb, seq, ATT_HEADS)
    k = _to_heads(u[:, 1024:1152], nb, seq, KV_HEADS)
    v = _to_heads(u[:, 1152:1280], nb, seq, KV_HEADS)
    o = attn_fwd(q, k, v, p["bias"], p["sink_rows"], name=f"attn_fwd_{tag}")
    dn4 = u[:, 1280:2304].reshape(nb, seq, 4, DN_HEADS, HEAD).transpose(2, 0, 3, 1, 4)
    pre, z = dn4[:3], dn4[3]
    ba = u[:, 2304:2312].reshape(nb, seq, 2, DN_HEADS).transpose(2, 0, 3, 1)[..., None]
    loc, egl = dn_prep(pre, p["dn_cw"], ba, p["dn_hs"], name=f"dn_prep_{tag}")
    y_dn, o_raw, vn, st = dn_scan(loc, egl, z, p["dn_norm"], name=f"dn_scan_{tag}")
    ycat = jnp.concatenate([y_lru, _from_heads(o), _from_heads(y_dn)], axis=-1)
    out = matmul(ycat, p["w_out"], residual=h, name=f"mix_out_{tag}")
    return out, dict(h=h, u=u, n=n, q=q, k=k, v=v, pre=pre, z=z, ba=ba, loc=loc, egl=egl, o_raw=o_raw, vn=vn, st=st,
                     ycat=ycat)


def mixer_bwd(dout, s, p, nb, seq, tag):
    dycat = matmul(dout, p["w_out"], tb=True, name=f"mix_out_dx_{tag}")
    g = {"w_out": matmul(s["ycat"], dout, ta=True, name=f"mix_out_dw_{tag}")}
    dx_lru, dgate_lru, dcw, dwa, dwx, dvec = lru_bwd(s["u"], *p["lru"], dycat[:, :LRU_W], seq=seq, name=f"lru_bwd_{tag}")
    g.update(lru_conv_w=dcw, lru_conv_b=dvec[0], lru_w_a=_diag_blocks(dwa), lru_b_a=dvec[1], lru_w_x=_diag_blocks(dwx),
             lru_b_x=dvec[2], lru_lambda=dvec[3])
    do = _to_heads(dycat[:, LRU_W:LRU_W + ATT_W], nb, seq, ATT_HEADS)
    dq, dk, dv, dbias, dsink = attn_bwd(s["q"], s["k"], s["v"], p["bias"], p["sink_rows"], do, name=f"attn_bwd_{tag}")
    g.update(attn_sinks=dsink.reshape(ATT_HEADS, BLOCK_Q).sum(axis=1), bias=dbias)
    dy_dn = _to_heads(dycat[:, LRU_W + ATT_W:], nb, seq, DN_HEADS)
    dloc, degl, dz, dgn = dn_scan_bwd(s["loc"], s["egl"], s["z"], p["dn_norm"], s["o_raw"], s["vn"], s["st"], dy_dn,
                                      name=f"dn_scan_bwd_{tag}")
    dpre, dba, dcwh, dhs = dn_prep_bwd(s["pre"], p["dn_cw"], s["ba"], p["dn_hs"], dloc, degl, name=f"dn_prep_bwd_{tag}")
    g.update(dn_conv_w=dcwh.transpose(2, 1, 0, 3).reshape(4, 3 * DN_HEADS * HEAD), dn_a_log=dhs[:, 0, 0, 0],
             dn_dt_bias=dhs[:, 1, 0, 0], dn_norm=dgn[0])
    du_dn = jnp.concatenate([dpre, dz[None]], axis=0).transpose(1, 3, 0, 2, 4).reshape(nb * seq, 4 * DN_HEADS * HEAD)
    du_ba = dba[..., 0].transpose(1, 3, 0, 2).reshape(nb * seq, 2 * DN_HEADS)
    du = jnp.concatenate([dx_lru, dgate_lru, _from_heads(dq), _from_heads(dk), _from_heads(dv), du_dn, du_ba,
                          jnp.zeros((nb * seq, D_IN_PAD - D_IN), F32)], axis=-1)
    dn = matmul(du, p["w_in"], tb=True, name=f"mix_in_dx_{tag}")
    g["w_in"] = matmul(s["n"], du, ta=True, name=f"mix_in_dw_{tag}")
    dh, dgain = rms_bwd(s["h"], p["mix_norm"], dn, dout, name=f"mix_norm_bwd_{tag}")
    g["mix_norm"] = dgain[0]
    return dh, g


SHARDED = ("ffn1_w_gate", "ffn1_w_up", "ffn1_w_down", "w_in", "w_out", "ffn2_w_gate", "ffn2_w_up", "ffn2_w_down",
           "ple_w_gate", "ple_w_proj")
PER_LAYER_SMALL = ("ffn1_norm", "mix_norm", "lru_conv_w", "lru_conv_b", "lru_w_a", "lru_b_a", "lru_w_x", "lru_b_x",
                   "lru_lambda", "attn_sinks", "dn_conv_w", "dn_a_log", "dn_dt_bias", "dn_norm", "ffn2_norm", "ple_norm")


def _col_shards(a):
    r, c = a.shape
    return a.reshape(r, N_CHIP, c // N_CHIP).transpose(1, 0, 2)


def local_step(x, p, target, w, bmap, nb, seq):
    bias = relbias_fwd(w["rel_bias"], bmap, name="relbias_fwd")
    h, saved = x, []
    for l in range(N_LAYER):
        pr = layer_params(w, l, bias)
        s = dict(h0=h)
        h = ffn_fwd(h, pr["ffn1_norm"], *pr["ffn1"], name=f"ffn1_fwd_{l}")
        h, s["mix"] = mixer_fwd(h, pr, nb, seq, l)
        s["h2"] = h
        h = ffn_fwd(h, pr["ffn2_norm"], *pr["ffn2"], name=f"ffn2_fwd_{l}")
        s["h3"] = h
        h = ple_fwd(h, pr["ple_norm"], pr["ple_w_gate"], p[l], pr["ple_w_proj"], name=f"ple_fwd_{l}")
        saved.append((pr, s))
    dh, dgf, loss = loss_head(h, w["final_norm"].reshape(1, -1), target, name="loss_head")

    per_layer, dbias = [None] * N_LAYER, None
    for l in reversed(range(N_LAYER)):
        pr, s = saved[l]
        g = {}
        dout = dh
        dh, n, dga, dpp, dg = ple_bwd(s["h3"], pr["ple_norm"], pr["ple_w_gate"], p[l], pr["ple_w_proj"], dout, name=f"ple_bwd_{l}")
        g["ple_norm"] = dg[0]
        g["ple_w_gate"] = matmul(n, dga, ta=True, name=f"ple_dwg_{l}").reshape(N_CHIP, -1, D_MODEL)
        g["ple_w_proj"] = _col_shards(matmul(p[l], dpp, ta=True, name=f"ple_dwp_{l}"))
        for nm, hin in (("ffn2", s["h2"]), ("ffn1", s["h0"])):
            if nm == "ffn1":
                dh, gm = mixer_bwd(dh, s["mix"], pr, nb, seq, l)
                dbias = gm.pop("bias") if dbias is None else dbias + gm.pop("bias")
                gm["w_in"] = _col_shards(gm["w_in"][:, :D_IN])
                gm["w_out"] = gm["w_out"].reshape(N_CHIP, -1, D_MODEL)
                g.update(gm)
            dout = dh
            dh, n, da, db, sact, dg = ffn_bwd_act(hin, pr[nm + "_norm"], dout, *pr[nm], name=f"{nm}_bwd_act_{l}")
            g[nm + "_norm"] = dg[0]
            g[nm + "_w_gate"], g[nm + "_w_up"], g[nm + "_w_down"] = ffn_bwd_w(n, da, db, sact, dout, name=f"{nm}_bwd_w_{l}")
        per_layer[l] = g
    grads = {k: jnp.stack([per_layer[l][k] for l in range(N_LAYER)]) for k in SHARDED + PER_LAYER_SMALL}
    grads["rel_bias"] = relbias_bwd(dbias, bmap, name="relbias_bwd")[:, :ATT_HEADS]
    grads["final_norm"] = dgf[0]
    return loss, dh, grads


HBM_SPEC = pl.BlockSpec(memory_space=pltpu.HBM)


def _place():
    x, y, c = lax.axis_index("x"), lax.axis_index("y"), lax.axis_index("c")
    chips = [(1 - x, y), (x, 1 - y), (1 - x, 1 - y)]
    return x, y, c, 2 * x + y, (x, y, 1 - c), chips, [2 * cx + cy for cx, cy in chips]


def _remote(src, dst, send_sem, recv_sem, to):
    return pltpu.make_async_remote_copy(src_ref=src, dst_ref=dst, send_sem=send_sem, recv_sem=recv_sem, device_id=to,
                                        device_id_type=MESH)


def allgather_shards(shards, *, name):
    n = len(shards)

    def body(*refs):
        ins, outs = refs[:n], refs[n:2 * n]
        send, recv, fsend, frecv, lsem = refs[2 * n:]
        x, y, c, me, sib, chips, cids = _place()
        local, first, passed = [], [], []
        for k in range(n):
            local.append(pltpu.make_async_copy(ins[k], outs[k].at[me], lsem.at[k]))
            local[-1].start()
            for j, chip in enumerate(chips):
                first.append(_remote(ins[k].at[c], outs[k].at[me, c], send.at[3 * k + j], recv.at[3 * k + j], (*chip, c)))
                first[-1].start()
        for k in range(n):
            for j in range(3):
                piece = outs[k].at[cids[j], c]
                _remote(piece, piece, send.at[3 * k + j], recv.at[3 * k + j], sib).wait_recv()
                passed.append(_remote(piece, piece, fsend.at[3 * k + j], frecv.at[3 * k + j], sib))
                passed[-1].start()
        for k in range(n):
            for j in range(3):
                piece = outs[k].at[cids[j], 1 - c]
                _remote(piece, piece, fsend.at[3 * k + j], frecv.at[3 * k + j], sib).wait_recv()
        for cp in first + passed:
            cp.wait_send()
        for cp in local:
            cp.wait()

    return pl.pallas_call(
        body, name=name, in_specs=[HBM_SPEC] * n, out_specs=[HBM_SPEC] * n,
        out_shape=[SDS((N_CHIP,) + s.shape, s.dtype) for s in shards],
        scratch_shapes=[pltpu.SemaphoreType.DMA((3 * n,))] * 4 + [pltpu.SemaphoreType.DMA((n,))])(*shards)


def exchange_layers(gs, *, name):
    n = len(gs)

    def body(*refs):
        ins, outs, (send, recv) = refs[:n], refs[n:2 * n], refs[2 * n:]
        x, y, c, me, sib, chips, cids = _place()
        cps = [_remote(ins[k].at[1 - c], outs[k], send.at[k], recv.at[k], sib) for k in range(n)]
        for cp in cps:
            cp.start()
        for cp in cps:
            cp.wait()

    return pl.pallas_call(
        body, name=name, in_specs=[HBM_SPEC] * n, out_specs=[HBM_SPEC] * n,
        out_shape=[SDS(g.shape[1:], g.dtype) for g in gs], scratch_shapes=[pltpu.SemaphoreType.DMA((n,))] * 2)(*gs)


def reduce_to_shards(ss, *, name):
    n = len(ss)

    def body(*refs):
        ins, outs, (send, recv, lsem) = refs[:n], refs[n:2 * n], refs[2 * n:]
        x, y, c, me, sib, chips, cids = _place()
        local, cps = [], []
        for k in range(n):
            local.append(pltpu.make_async_copy(ins[k].at[me], outs[k].at[me], lsem.at[k]))
            local[-1].start()
            for j, chip in enumerate(chips):
                cps.append(_remote(ins[k].at[cids[j]], outs[k].at[me], send.at[3 * k + j], recv.at[3 * k + j], (*chip, c)))
                cps[-1].start()
        for k in range(n):
            for j in range(3):
                slot = outs[k].at[cids[j]]
                _remote(slot, slot, send.at[3 * k + j], recv.at[3 * k + j], sib).wait_recv()
        for cp in cps:
            cp.wait_send()
        for cp in local:
            cp.wait()

    return pl.pallas_call(
        body, name=name, in_specs=[HBM_SPEC] * n, out_specs=[HBM_SPEC] * n, out_shape=[SDS(s.shape, s.dtype) for s in ss],
        scratch_shapes=[pltpu.SemaphoreType.DMA((3 * n,))] * 2 + [pltpu.SemaphoreType.DMA((n,))])(*ss)


def share_layers(fs, *, name):
    n = len(fs)

    def body(*refs):
        ins, outs, (send, recv, lsem) = refs[:n], refs[n:2 * n], refs[2 * n:]
        x, y, c, me, sib, chips, cids = _place()
        local = [pltpu.make_async_copy(ins[k], outs[k].at[c], lsem.at[k]) for k in range(n)]
        cps = [_remote(ins[k], outs[k].at[c], send.at[k], recv.at[k], sib) for k in range(n)]
        for cp in local + cps:
            cp.start()
        for k in range(n):
            theirs = outs[k].at[1 - c]
            _remote(theirs, theirs, send.at[k], recv.at[k], sib).wait_recv()
        for cp in cps:
            cp.wait_send()
        for cp in local:
            cp.wait()

    return pl.pallas_call(
        body, name=name, in_specs=[HBM_SPEC] * n, out_specs=[HBM_SPEC] * n,
        out_shape=[SDS((N_LAYER,) + f.shape, f.dtype) for f in fs], scratch_shapes=[pltpu.SemaphoreType.DMA((n,))] * 3)(*fs)


N_DEV = 8


def allreduce_small(buf, *, name):
    rows = buf.shape[0]

    def body(in_ref, out_ref, gath, send, recv):
        x, y, c = lax.axis_index("x"), lax.axis_index("y"), lax.axis_index("c")
        mine = 4 * x + 2 * y + c
        gath[mine] = in_ref[...]
        cps = []
        for k in range(1, N_DEV):
            to = (x ^ (k >> 2), y ^ ((k >> 1) & 1), c ^ (k & 1))
            cps.append(_remote(in_ref, gath.at[mine], send.at[k - 1], recv.at[k - 1], to))
            cps[-1].start()
        for k in range(1, N_DEV):
            theirs = gath.at[4 * (x ^ (k >> 2)) + 2 * (y ^ ((k >> 1) & 1)) + (c ^ (k & 1))]
            _remote(theirs, theirs, send.at[k - 1], recv.at[k - 1], (x, y, c)).wait_recv()
        for cp in cps:
            cp.wait_send()
        acc = gath[0]
        for d in range(1, N_DEV):
            acc = acc + gath[d]
        out_ref[...] = acc

    vm = pl.BlockSpec(memory_space=pltpu.VMEM)
    return pl.pallas_call(
        body, name=name, in_specs=[vm], out_specs=vm, out_shape=SDS(buf.shape, F32),
        scratch_shapes=[pltpu.VMEM((N_DEV, rows, 128), F32), pltpu.SemaphoreType.DMA((N_DEV - 1,)),
                        pltpu.SemaphoreType.DMA((N_DEV - 1,))])(buf)


def add_sibling(g, r, c_arr, *, name, tr=256):
    _, m, cdim = g.shape
    assert m % tr == 0

    def body(c_ref, g_ref, r_ref, o_ref):
        o_ref[...] = g_ref[...] + r_ref[...]

    return pl.pallas_call(
        body, name=name,
        grid_spec=pltpu.PrefetchScalarGridSpec(
            num_scalar_prefetch=1, grid=(m // tr,),
            in_specs=[pl.BlockSpec((None, tr, cdim), lambda i, c: (c[0], i, 0)), pl.BlockSpec((tr, cdim), lambda i, c: (i, 0))],
            out_specs=pl.BlockSpec((tr, cdim), lambda i, c: (i, 0))),
        out_shape=SDS((m, cdim), F32), compiler_params=_params(("parallel",)))(c_arr, g, r)


def sum_slots(r, *, name, tr=256):
    _, m, cdim = r.shape
    tr = next(cand for cand in (tr, 128, 64, 32, 16, 8) if m % cand == 0)

    def body(r_ref, o_ref):
        o_ref[...] = ((r_ref[0] + r_ref[1]) + r_ref[2]) + r_ref[3]

    return pl.pallas_call(
        body, name=name, grid=(m // tr,), in_specs=[pl.BlockSpec((N_CHIP, tr, cdim), lambda i: (0, i, 0))],
        out_specs=pl.BlockSpec((tr, cdim), lambda i: (i, 0)), out_shape=SDS((m, cdim), F32),
        compiler_params=_params(("parallel",)))(r)


WEIGHTS = ("ffn1_norm", "ffn1_w_gate", "ffn1_w_up", "ffn1_w_down", "mix_norm", "w_in", "lru_conv_w", "lru_conv_b", "lru_w_a",
           "lru_b_a", "lru_w_x", "lru_b_x", "lru_lambda", "attn_sinks", "rel_bias", "dn_conv_w", "dn_a_log", "dn_dt_bias",
           "dn_norm", "w_out", "ffn2_norm", "ffn2_w_gate", "ffn2_w_up", "ffn2_w_down", "ple_norm", "ple_w_gate",
           "ple_w_proj", "final_norm")
CONV_SHARDED = ("lru_conv_w", "dn_conv_w")
SMALL = tuple(k for k in WEIGHTS if k not in SHARDED)


def _pack(arrs):
    flat = []
    for a in arrs:
        v = a.reshape(-1)
        flat.append(jnp.pad(v, (0, -v.shape[0] % 128)))
    v = jnp.concatenate(flat)
    v = jnp.pad(v, (0, -v.shape[0] % 1024))
    return v.reshape(-1, 128)


def _unpack(buf, shapes):
    v, out, off = buf.reshape(-1), [], 0
    for s in shapes:
        n = int(np.prod(s))
        out.append(v[off:off + n].reshape(s))
        off += n + (-n % 128)
    return out


def _chip_cols(a):
    n, l, r, c = a.shape
    return a.transpose(1, 2, 0, 3).reshape(l, r, n * c)


def _chip_rows(a):
    n, l, r, c = a.shape
    return a.transpose(1, 0, 2, 3).reshape(l, n * r, c)


def kernel(x, p, ffn1_norm, ffn1_w_gate, ffn1_w_up, ffn1_w_down, mix_norm, w_in, lru_conv_w, lru_conv_b, lru_w_a, lru_b_a, lru_w_x, lru_b_x, lru_lambda, attn_sinks, rel_bias, dn_conv_w, dn_a_log, dn_dt_bias, dn_norm, w_out, ffn2_norm, ffn2_w_gate, ffn2_w_up, ffn2_w_down, ple_norm, ple_w_gate, ple_w_proj, final_norm, loss_target, m_ffn1_norm, m_ffn1_w_gate, m_ffn1_w_up, m_ffn1_w_down, m_mix_norm, m_w_in, m_lru_conv_w, m_lru_conv_b, m_lru_w_a, m_lru_b_a, m_lru_w_x, m_lru_b_x, m_lru_lambda, m_attn_sinks, m_rel_bias, m_dn_conv_w, m_dn_a_log, m_dn_dt_bias, m_dn_norm, m_w_out, m_ffn2_norm, m_ffn2_w_gate, m_ffn2_w_up, m_ffn2_w_down, m_ple_norm, m_ple_w_gate, m_ple_w_proj, m_final_norm, v_ffn1_norm, v_ffn1_w_gate, v_ffn1_w_up, v_ffn1_w_down, v_mix_norm, v_w_in, v_lru_conv_w, v_lru_conv_b, v_lru_w_a, v_lru_b_a, v_lru_w_x, v_lru_b_x, v_lru_lambda, v_attn_sinks, v_rel_bias, v_dn_conv_w, v_dn_a_log, v_dn_dt_bias, v_dn_norm, v_w_out, v_ffn2_norm, v_ffn2_w_gate, v_ffn2_w_up, v_ffn2_w_down, v_ple_norm, v_ple_w_gate, v_ple_w_proj, v_final_norm):
    given = dict(locals())
    ws = {k: given[k] for k in WEIGHTS}
    ms = {k: given["m_" + k] for k in WEIGHTS}
    vs = {k: given["v_" + k] for k in WEIGHTS}
    nb, seq, d = x.shape
    t = nb * seq
    cx, cy, cc = lax.axis_index("x"), lax.axis_index("y"), lax.axis_index("c")
    chip = 2 * cx + cy

    gathered = allgather_shards([ws[k].astype(BF16) for k in SHARDED] + [ws[k] for k in CONV_SHARDED], name="allgather_weights")
    full = dict(zip(SHARDED + CONV_SHARDED, gathered))
    for k in ("w_in", "ple_w_proj", "lru_conv_w", "dn_conv_w"):
        full[k] = _chip_cols(full[k])
    for k in ("w_out", "ple_w_gate"):
        full[k] = _chip_rows(full[k])
    full["w_in"] = jnp.pad(full["w_in"], ((0, 0), (0, 0), (0, D_IN_PAD - D_IN)))
    for k in SMALL:
        if k not in CONV_SHARDED:
            full[k] = ws[k]

    bmap = jnp.asarray(_rel_bucket_map())
    loss, gx, grads = local_step(x.reshape(t, d), p.reshape(N_LAYER, t, PLE_DIM), loss_target.reshape(t, d), full, bmap, nb, seq)

    gs = [grads[k] for k in SHARDED]
    flat = lambda a, lead: a.reshape(a.shape[:lead] + (-1, a.shape[-1]))
    theirs = exchange_layers(gs, name="rs_exchange_layers")
    c_arr = cc.astype(jnp.int32).reshape(1)
    sums = [add_sibling(flat(g, 1), flat(r, 0), c_arr, name=f"rs_add_{k}").reshape(r.shape)
            for k, g, r in zip(SHARDED, gs, theirs)]
    slots = reduce_to_shards(sums, name="rs_reduce_to_shards")
    mine = [sum_slots(flat(r, 1), name=f"rs_sum_{k}").reshape(r.shape[1:]) for k, r in zip(SHARDED, slots)]
    g_out = dict(zip(SHARDED, share_layers(mine, name="rs_share_layers")))

    small_shapes = [grads[k].shape for k in SMALL]
    g_small = dict(zip(SMALL, _unpack(allreduce_small(_pack([grads[k] for k in SMALL]), name="allreduce_small"), small_shapes)))
    for k in CONV_SHARDED:
        width = ws[k].shape[-1]
        g_small[k] = lax.dynamic_slice_in_dim(g_small[k], chip * width, width, axis=2)
    g_out.update(g_small)

    delta, new_m, new_v = {}, {}, {}
    for k in SHARDED:
        two_d = lambda a: a.reshape(-1, a.shape[-1])
        res = adamw(two_d(ws[k]), two_d(g_out[k]), two_d(ms[k]), two_d(vs[k]), name=f"adamw_{k}")
        delta[k], new_m[k], new_v[k] = (r.reshape(ws[k].shape) for r in res)
    shapes = [ws[k].shape for k in SMALL]
    res = adamw(*[_pack([src[k] for k in SMALL]) for src in (ws, g_out, ms, vs)], name="adamw_small")
    for dst, r in zip((delta, new_m, new_v), res):
        dst.update(zip(SMALL, _unpack(r, shapes)))

    total = lax.psum(loss[0, 0], ("x", "y", "c"))
    return (total, gx.reshape(nb, seq, d), *[g_out[k] for k in WEIGHTS], *[delta[k] for k in WEIGHTS],
            *[new_m[k] for k in WEIGHTS], *[new_v[k] for k in WEIGHTS])
```

```python
import functools
import math

import numpy as np
import jax
import jax.numpy as jnp
from jax import lax
from jax.experimental import pallas as pl
from jax.experimental.pallas import tpu as pltpu

F32 = jnp.float32
BF16 = jnp.bfloat16

EPS = 1e-6
D_MODEL = 1024
D_FF = 2816
N_CHIP = 4
FF_BLK = D_FF // N_CHIP
HEAD = 64
LRU_W = 256
ATT_W = 512
ATT_HEADS = 8
KV_HEADS = 2
ATT_GROUP = 4
BLOCK_Q = 128
DN_HEADS = 4
DN_CHUNK = 64
D_IN = 2312
D_IN_PAD = 2560
PLE_DIM = 256
REL_BUCKETS = 32
LRU_C = 8.0
N_LAYER = 2

ADAM_LR, ADAM_B1, ADAM_B2, ADAM_EPS, ADAM_WD, ADAM_STEP = 0.001, 0.9, 0.999, 1e-08, 0.01, 10

VMEM_LIMIT = 56 << 20
MESH = pl.DeviceIdType.MESH
SDS = jax.ShapeDtypeStruct


def _dot(a, b, ca=1, cb=0, hi=False):
    dims = (((ca,), (cb,)), ((), ()))
    one = lambda u, v: lax.dot_general(u, v, dims, preferred_element_type=F32)
    a_hi, b_hi = a.astype(BF16), b.astype(BF16)
    if not hi:
        return one(a_hi, b_hi)
    a_lo = (a - a_hi.astype(F32)).astype(BF16)
    b_lo = (b - b_hi.astype(F32)).astype(BF16)
    return one(a_hi, b_hi) + (one(a_hi, b_lo) + one(a_lo, b_hi))


def _nn(a, b, hi=False):
    return _dot(a, b, 1, 0, hi)


def _nt(a, b, hi=False):
    return _dot(a, b, 1, 1, hi)


def _tn(a, b, hi=False):
    return _dot(a, b, 0, 0, hi)


def _sigmoid(x):
    return jax.nn.sigmoid(x)


def _softplus(x):
    return jnp.maximum(x, 0.0) + jnp.log1p(jnp.exp(-jnp.abs(x)))


def _neg_expm1(z):
    series = -z * (1.0 + z * (0.5 + z * (1.0 / 6.0 + z * (1.0 / 24.0 + z * (1.0 / 120.0)))))
    return jnp.where(z > -0.05, series, 1.0 - jnp.exp(z))


_GELU_C = math.sqrt(2.0 / math.pi)


def _gelu(x):
    t = jnp.tanh(_GELU_C * (x + 0.044715 * x * x * x))
    return 0.5 * x * (1.0 + t), t


def _gelu_grad(x, t):
    return 0.5 * (1.0 + t) + 0.5 * x * (1.0 - t * t) * _GELU_C * (1.0 + 3.0 * 0.044715 * x * x)


def _rms_fwd(h, g):
    r = lax.rsqrt(jnp.mean(h * h, axis=-1, keepdims=True) + EPS)
    xh = h * r
    return xh * g, xh, r


def _rms_bwd(dn, xh, r, g):
    dxh = dn * g
    dh = r * (dxh - xh * jnp.mean(dxh * xh, axis=-1, keepdims=True))
    return dh, jnp.sum(dn * xh, axis=0, keepdims=True)


def _shift_down(x, d, fill=0.0):
    row = lax.broadcasted_iota(jnp.int32, x.shape, 0)
    return jnp.where(row >= d, pltpu.roll(x, d, 0), fill)


def _shift_up(x, d, fill=0.0):
    n = x.shape[0]
    row = lax.broadcasted_iota(jnp.int32, x.shape, 0)
    return jnp.where(row < n - d, pltpu.roll(x, n - d, 0), fill)


def _conv_fwd(x, w):
    y = x * w[3]
    for k in range(3):
        y = y + _shift_down(x, 3 - k) * w[k]
    return y


def _conv_bwd(dy, x, w):
    dx = dy * w[3]
    rows = [None] * 4
    rows[3] = jnp.sum(dy * x, axis=0, keepdims=True)
    for k in range(3):
        dx = dx + _shift_up(dy, 3 - k) * w[k]
        rows[k] = jnp.sum(dy * _shift_down(x, 3 - k), axis=0, keepdims=True)
    r4 = lax.broadcasted_iota(jnp.int32, (4, x.shape[1]), 0)
    dw = jnp.zeros((4, x.shape[1]), F32)
    for k in range(4):
        dw = jnp.where(r4 == k, rows[k], dw)
    return dx, dw


def _params(sem=None, vmem=VMEM_LIMIT):
    return pltpu.CompilerParams(dimension_semantics=sem, vmem_limit_bytes=vmem)


def _whole(shape):
    nd = len(shape)
    return pl.BlockSpec(shape, lambda *_: (0,) * nd)


def matmul(a, b, *, name, ta=False, tb=False, residual=None, out_dtype=F32, tm=512, tn=512, tk=512):
    m, k = (a.shape[1], a.shape[0]) if ta else a.shape
    n = b.shape[0] if tb else b.shape[1]
    tm, tn, tk = min(tm, m), min(tn, n), min(tk, k)
    assert m % tm == 0 and n % tn == 0 and k % tk == 0, (m, n, k, tm, tn, tk)
    nk = k // tk

    def body(*refs):
        if residual is None:
            a_ref, b_ref, o_ref, acc = refs
        else:
            a_ref, b_ref, r_ref, o_ref, acc = refs
        kk = pl.program_id(2)

        @pl.when(kk == 0)
        def _():
            acc[...] = jnp.zeros_like(acc)

        acc[...] += _dot(a_ref[...], b_ref[...], 0 if ta else 1, 1 if tb else 0)

        @pl.when(kk == nk - 1)
        def _():
            out = acc[...]
            if residual is not None:
                out = out + r_ref[...]
            o_ref[...] = out.astype(out_dtype)

    a_spec = pl.BlockSpec((tk, tm), lambda i, j, kk: (kk, i)) if ta else pl.BlockSpec((tm, tk), lambda i, j, kk: (i, kk))
    b_spec = pl.BlockSpec((tn, tk), lambda i, j, kk: (j, kk)) if tb else pl.BlockSpec((tk, tn), lambda i, j, kk: (kk, j))
    o_spec = pl.BlockSpec((tm, tn), lambda i, j, kk: (i, j))
    in_specs, args = [a_spec, b_spec], [a, b]
    if residual is not None:
        in_specs.append(o_spec)
        args.append(residual)
    return pl.pallas_call(
        body, name=name, grid=(m // tm, n // tn, nk), in_specs=in_specs, out_specs=o_spec,
        out_shape=SDS((m, n), out_dtype), scratch_shapes=[pltpu.VMEM((tm, tn), F32)],
        compiler_params=_params(("parallel", "parallel", "arbitrary")))(*args)


def norm_matmul(h, gain, w, *, name, tm=512, tn=512):
    t, d = h.shape
    tm = min(tm, t)
    n = w.shape[1]
    assert t % tm == 0 and n % tn == 0

    def body(h_ref, g_ref, w_ref, u_ref, n_ref):
        @pl.when(pl.program_id(1) == 0)
        def _():
            n_ref[...] = _rms_fwd(h_ref[...], g_ref[...])[0].astype(BF16)

        u_ref[...] = _nn(n_ref[...], w_ref[...])

    return pl.pallas_call(
        body, name=name, grid=(t // tm, n // tn),
        in_specs=[pl.BlockSpec((tm, d), lambda i, j: (i, 0)), _whole((1, d)), pl.BlockSpec((d, tn), lambda i, j: (0, j))],
        out_specs=[pl.BlockSpec((tm, tn), lambda i, j: (i, j)), pl.BlockSpec((tm, d), lambda i, j: (i, 0))],
        out_shape=[SDS((t, n), F32), SDS((t, d), BF16)],
        compiler_params=_params(("parallel", "arbitrary")))(h, gain, w)


def rms_bwd(h, gain, dn, dres, *, name, tm=512):
    t, d = h.shape
    tm = min(tm, t)

    def body(h_ref, g_ref, dn_ref, dr_ref, dh_ref, dg_ref):
        @pl.when(pl.program_id(0) == 0)
        def _():
            dg_ref[...] = jnp.zeros_like(dg_ref)

        g = g_ref[...]
        _, xh, r = _rms_fwd(h_ref[...], g)
        dh, dg = _rms_bwd(dn_ref[...], xh, r, g)
        dh_ref[...] = dr_ref[...] + dh
        dg_ref[...] += dg

    row = pl.BlockSpec((tm, d), lambda i: (i, 0))
    return pl.pallas_call(
        body, name=name, grid=(t // tm,), in_specs=[row, _whole((1, d)), row, row],
        out_specs=[row, _whole((1, d))], out_shape=[SDS((t, d), F32), SDS((1, d), F32)],
        compiler_params=_params(("arbitrary",)))(h, gain, dn, dres)


def ffn_fwd(h, gain, wg, wu, wd, *, name, tm=512):
    t, d = h.shape
    tm = min(tm, t)

    def body(h_ref, g_ref, wg_ref, wu_ref, wd_ref, o_ref, n_sc, acc):
        j = pl.program_id(1)

        @pl.when(j == 0)
        def _():
            n_sc[...] = _rms_fwd(h_ref[...], g_ref[...])[0].astype(BF16)
            acc[...] = jnp.zeros_like(acc)

        n = n_sc[...]
        a = _nn(n, wg_ref[...])
        b = _nn(n, wu_ref[...])
        acc[...] += _nn(a * _sigmoid(a) * b, wd_ref[...])

        @pl.when(j == N_CHIP - 1)
        def _():
            o_ref[...] = h_ref[...] + 0.5 * acc[...]

    row = pl.BlockSpec((tm, d), lambda i, j: (i, 0))
    return pl.pallas_call(
        body, name=name, grid=(t // tm, N_CHIP),
        in_specs=[row, _whole((1, d)),
                  pl.BlockSpec((None, d, FF_BLK), lambda i, j: (j, 0, 0)),
                  pl.BlockSpec((None, d, FF_BLK), lambda i, j: (j, 0, 0)),
                  pl.BlockSpec((None, FF_BLK, d), lambda i, j: (j, 0, 0))],
        out_specs=row, out_shape=SDS((t, d), F32),
        scratch_shapes=[pltpu.VMEM((tm, d), BF16), pltpu.VMEM((tm, d), F32)],
        compiler_params=_params(("parallel", "arbitrary")))(h, gain, wg, wu, wd)


def ffn_bwd_act(h, gain, dout, wg, wu, wd, *, name, tm=512):
    t, d = h.shape
    tm = min(tm, t)

    def body(h_ref, g_ref, do_ref, wg_ref, wu_ref, wd_ref, dh_ref, n_ref, da_ref, db_ref, s_ref, dg_ref, dn_acc):
        i, j = pl.program_id(0), pl.program_id(1)

        @pl.when((i == 0) & (j == 0))
        def _():
            dg_ref[...] = jnp.zeros_like(dg_ref)

        @pl.when(j == 0)
        def _():
            n_ref[...] = _rms_fwd(h_ref[...], g_ref[...])[0].astype(BF16)
            dn_acc[...] = jnp.zeros_like(dn_acc)

        n = n_ref[...]
        a = _nn(n, wg_ref[...])
        b = _nn(n, wu_ref[...])
        sig = _sigmoid(a)
        sa = a * sig
        ds = _nt(0.5 * do_ref[...], wd_ref[...])
        db = ds * sa
        da = ds * b * (sig * (1.0 + a * (1.0 - sig)))
        s_ref[...] = (sa * b).astype(BF16)
        da_ref[...] = da.astype(BF16)
        db_ref[...] = db.astype(BF16)
        dn_acc[...] += _nt(da, wg_ref[...]) + _nt(db, wu_ref[...])

        @pl.when(j == N_CHIP - 1)
        def _():
            g = g_ref[...]
            _, xh, r = _rms_fwd(h_ref[...], g)
            dh, dg = _rms_bwd(dn_acc[...], xh, r, g)
            dh_ref[...] = do_ref[...] + dh
            dg_ref[...] += dg

    row = pl.BlockSpec((tm, d), lambda i, j: (i, 0))
    blk = pl.BlockSpec((None, tm, FF_BLK), lambda i, j: (j, i, 0))
    act = SDS((N_CHIP, t, FF_BLK), BF16)
    return pl.pallas_call(
        body, name=name, grid=(t // tm, N_CHIP),
        in_specs=[row, _whole((1, d)), row,
                  pl.BlockSpec((None, d, FF_BLK), lambda i, j: (j, 0, 0)),
                  pl.BlockSpec((None, d, FF_BLK), lambda i, j: (j, 0, 0)),
                  pl.BlockSpec((None, FF_BLK, d), lambda i, j: (j, 0, 0))],
        out_specs=[row, row, blk, blk, blk, _whole((1, d))],
        out_shape=[SDS((t, d), F32), SDS((t, d), BF16), act, act, act, SDS((1, d), F32)],
        scratch_shapes=[pltpu.VMEM((tm, d), F32)],
        compiler_params=_params(("arbitrary", "arbitrary")))(h, gain, dout, wg, wu, wd)


def ffn_bwd_w(n, da, db, s, dout, *, name, tk=512):
    t, d = n.shape
    tk = min(tk, t)

    def body(n_ref, da_ref, db_ref, s_ref, do_ref, dwg_ref, dwu_ref, dwd_ref):
        @pl.when(pl.program_id(1) == 0)
        def _():
            dwg_ref[...] = jnp.zeros_like(dwg_ref)
            dwu_ref[...] = jnp.zeros_like(dwu_ref)
            dwd_ref[...] = jnp.zeros_like(dwd_ref)

        nn = n_ref[...]
        dwg_ref[...] += _tn(nn, da_ref[...])
        dwu_ref[...] += _tn(nn, db_ref[...])
        dwd_ref[...] += _tn(s_ref[...], 0.5 * do_ref[...])

    row = pl.BlockSpec((tk, d), lambda j, kk: (kk, 0))
    blk = pl.BlockSpec((None, tk, FF_BLK), lambda j, kk: (j, kk, 0))
    return pl.pallas_call(
        body, name=name, grid=(N_CHIP, t // tk), in_specs=[row, blk, blk, blk, row],
        out_specs=[pl.BlockSpec((None, d, FF_BLK), lambda j, kk: (j, 0, 0)),
                   pl.BlockSpec((None, d, FF_BLK), lambda j, kk: (j, 0, 0)),
                   pl.BlockSpec((None, FF_BLK, d), lambda j, kk: (j, 0, 0))],
        out_shape=[SDS((N_CHIP, d, FF_BLK), F32), SDS((N_CHIP, d, FF_BLK), F32), SDS((N_CHIP, FF_BLK, d), F32)],
        compiler_params=_params(("parallel", "arbitrary")))(n, da, db, s, dout)


def ple_fwd(h, gain, wpg, pl_in, wpp, *, name, tm=512):
    t, d = h.shape
    tm = min(tm, t)
    pd = pl_in.shape[1]

    def body(h_ref, g_ref, wpg_ref, p_ref, wpp_ref, o_ref):
        hh = h_ref[...]
        n = _rms_fwd(hh, g_ref[...])[0]
        gate = _sigmoid(_nn(n, wpg_ref[...]))
        o_ref[...] = hh + gate * _nn(p_ref[...], wpp_ref[...])

    row = pl.BlockSpec((tm, d), lambda i: (i, 0))
    return pl.pallas_call(
        body, name=name, grid=(t // tm,),
        in_specs=[row, _whole((1, d)), _whole((d, d)), pl.BlockSpec((tm, pd), lambda i: (i, 0)), _whole((pd, d))],
        out_specs=row, out_shape=SDS((t, d), F32), compiler_params=_params(("parallel",)))(h, gain, wpg, pl_in, wpp)


def ple_bwd(h, gain, wpg, pl_in, wpp, dout, *, name, tm=512):
    t, d = h.shape
    tm = min(tm, t)
    pd = pl_in.shape[1]

    def body(h_ref, g_ref, wpg_ref, p_ref, wpp_ref, do_ref, dh_ref, n_ref, dga_ref, dpp_ref, dg_ref):
        @pl.when(pl.program_id(0) == 0)
        def _():
            dg_ref[...] = jnp.zeros_like(dg_ref)

        g = g_ref[...]
        n, xh, r = _rms_fwd(h_ref[...], g)
        gate = _sigmoid(_nn(n, wpg_ref[...]))
        pp = _nn(p_ref[...], wpp_ref[...])
        do = do_ref[...]
        dga = do * pp * gate * (1.0 - gate)
        dh, dg = _rms_bwd(_nt(dga, wpg_ref[...]), xh, r, g)
        dh_ref[...] = do + dh
        n_ref[...] = n.astype(BF16)
        dga_ref[...] = dga.astype(BF16)
        dpp_ref[...] = (do * gate).astype(BF16)
        dg_ref[...] += dg

    row = pl.BlockSpec((tm, d), lambda i: (i, 0))
    return pl.pallas_call(
        body, name=name, grid=(t // tm,),
        in_specs=[row, _whole((1, d)), _whole((d, d)), pl.BlockSpec((tm, pd), lambda i: (i, 0)), _whole((pd, d)), row],
        out_specs=[row, row, row, row, _whole((1, d))],
        out_shape=[SDS((t, d), F32), SDS((t, d), BF16), SDS((t, d), BF16), SDS((t, d), BF16), SDS((1, d), F32)],
        compiler_params=_params(("arbitrary",)))(h, gain, wpg, pl_in, wpp, dout)


def loss_head(h, gain, target, *, name, tm=512):
    t, d = h.shape
    tm = min(tm, t)

    def body(h_ref, g_ref, t_ref, dh_ref, dg_ref, l_ref):
        @pl.when(pl.program_id(0) == 0)
        def _():
            dg_ref[...] = jnp.zeros_like(dg_ref)
            l_ref[...] = jnp.zeros_like(l_ref)

        g = g_ref[...]
        y, xh, r = _rms_fwd(h_ref[...], g)
        err = y - t_ref[...]
        l_ref[...] += 0.5 * jnp.sum(jnp.mean(err * err, axis=-1, keepdims=True), axis=0, keepdims=True)
        dh, dg = _rms_bwd(err * (1.0 / d), xh, r, g)
        dh_ref[...] = dh
        dg_ref[...] += dg

    row = pl.BlockSpec((tm, d), lambda i: (i, 0))
    return pl.pallas_call(
        body, name=name, grid=(t // tm,), in_specs=[row, _whole((1, d)), row],
        out_specs=[row, _whole((1, d)), _whole((1, 1))],
        out_shape=[SDS((t, d), F32), SDS((1, d), F32), SDS((1, 1), F32)],
        compiler_params=_params(("arbitrary",)))(h, gain, target)


def adamw(w, g, m, v, *, name):
    r, c = w.shape
    tr = r
    for cand in (512, 256, 128, 64, 32, 16, 8):
        if r % cand == 0:
            tr = cand
            break

    def body(w_ref, g_ref, m_ref, v_ref, d_ref, nm_ref, nv_ref):
        gg = g_ref[...]
        mm = ADAM_B1 * m_ref[...] + (1.0 - ADAM_B1) * gg
        vv = ADAM_B2 * v_ref[...] + (1.0 - ADAM_B2) * (gg * gg)
        m_hat = mm / (1.0 - ADAM_B1 ** ADAM_STEP)
        v_hat = vv / (1.0 - ADAM_B2 ** ADAM_STEP)
        d_ref[...] = -ADAM_LR * (m_hat / (jnp.sqrt(v_hat) + ADAM_EPS) + ADAM_WD * w_ref[...])
        nm_ref[...] = mm
        nv_ref[...] = vv

    blk = pl.BlockSpec((tr, c), lambda i: (i, 0))
    out = SDS((r, c), F32)
    return pl.pallas_call(body, name=name, grid=(r // tr,), in_specs=[blk] * 4, out_specs=[blk] * 3,
                          out_shape=[out, out, out], compiler_params=_params(("parallel",)))(w, g, m, v)


def _scan_fwd(a, b):
    d = 1
    while d < a.shape[0]:
        b = a * _shift_down(b, d, 0.0) + b
        a = a * _shift_down(a, d, 1.0)
        d *= 2
    return b


def _scan_rev(a, b):
    d = 1
    while d < a.shape[0]:
        b = a * _shift_up(b, d, 0.0) + b
        a = a * _shift_up(a, d, 1.0)
        d *= 2
    return b


LRU_HALF = 128


def _lru_in_specs(seq):
    half = LRU_W // LRU_HALF
    vec = pl.BlockSpec((1, LRU_HALF), lambda j, b: (0, j))
    mat = pl.BlockSpec((LRU_HALF, LRU_HALF), lambda j, b: (j, j))
    return [pl.BlockSpec((seq, LRU_HALF), lambda j, b: (b, j)), pl.BlockSpec((seq, LRU_HALF), lambda j, b: (b, half + j)),
            pl.BlockSpec((4, LRU_HALF), lambda j, b: (0, j)), vec, mat, vec, mat, vec, vec]


def _lru_math(x_ref, gate_ref, cw_ref, cb_ref, wa_ref, ba_ref, wx_ref, bx_ref, lam_ref):
    x = x_ref[...]
    gate = gate_ref[...]
    cw =[cw_ref[k:k + 1, :] for k in range(4)]
    xr = _conv_fwd(x, cw) + cb_ref[...]
    r = _sigmoid(_nn(xr, wa_ref[...]) + ba_ref[...])
    i = _sigmoid(_nn(xr, wx_ref[...]) + bx_ref[...])
    sp = _softplus(-lam_ref[...])
    log_a = -LRU_C * r * sp
    a = jnp.exp(log_a)
    mult = jnp.sqrt(_neg_expm1(2.0 * log_a))
    gi = i * xr
    h = _scan_fwd(a, mult * gi)
    gl, tg = _gelu(gate)
    return dict(x=x, gate=gate, cw=cw, xr=xr, r=r, i=i, sp=sp, a=a, mult=mult, gi=gi, h=h, gl=gl, tg=tg)


def lru_fwd(u, cw, cb, wa, ba, wx, bx, lam, *, seq, name):
    t = u.shape[0]

    def body(x_ref, gate_ref, cw_ref, cb_ref, wa_ref, ba_ref, wx_ref, bx_ref, lam_ref, y_ref):
        f = _lru_math(x_ref, gate_ref, cw_ref, cb_ref, wa_ref, ba_ref, wx_ref, bx_ref, lam_ref)
        y_ref[...] = f["gl"] * f["h"]

    return pl.pallas_call(
        body, name=name, grid=(LRU_W // LRU_HALF, t // seq), in_specs=_lru_in_specs(seq),
        out_specs=pl.BlockSpec((seq, LRU_HALF), lambda j, b: (b, j)), out_shape=SDS((t, LRU_W), F32),
        compiler_params=_params(("parallel", "parallel")))(u, u, cw, cb, wa, ba, wx, bx, lam)


def lru_bwd(u, cw, cb, wa, ba, wx, bx, lam, dy, *, seq, name):
    t = u.shape[0]

    def body(x_ref, gate_ref, cw_ref, cb_ref, wa_ref, ba_ref, wx_ref, bx_ref, lam_ref, dy_ref,
             dx_ref, dgate_ref, dcw_ref, dwa_ref, dwx_ref, dv_ref):
        @pl.when(pl.program_id(1) == 0)
        def _():
            dcw_ref[...] = jnp.zeros_like(dcw_ref)
            dwa_ref[...] = jnp.zeros_like(dwa_ref)
            dwx_ref[...] = jnp.zeros_like(dwx_ref)
            dv_ref[...] = jnp.zeros_like(dv_ref)

        f = _lru_math(x_ref, gate_ref, cw_ref, cb_ref, wa_ref, ba_ref, wx_ref, bx_ref, lam_ref)
        dy = dy_ref[...]
        a, h, xr, r, i, mult, gi, sp = f["a"], f["h"], f["xr"], f["r"], f["i"], f["mult"], f["gi"], f["sp"]
        dgate_ref[...] = dy * h * _gelu_grad(f["gate"], f["tg"])
        lamb = _scan_rev(_shift_up(a, 1, 0.0), dy * f["gl"])
        da = lamb * _shift_down(h, 1)
        dlog_a = da * a - (lamb * gi) * (a * a) / mult
        dgi = lamb * mult
        dra = dlog_a * (-LRU_C * sp) * r * (1.0 - r)
        dia = dgi * xr * i * (1.0 - i)
        dsp = jnp.sum(dlog_a * (-LRU_C * r), axis=0, keepdims=True)
        dlam = -dsp * _sigmoid(-lam_ref[...])
        dxr = dgi * i + _nt(dra, wa_ref[...]) + _nt(dia, wx_ref[...])
        dx, dcw = _conv_bwd(dxr, f["x"], f["cw"])
        dx_ref[...] = dx
        dcw_ref[...] += dcw
        dwa_ref[...] += _tn(xr, dra)
        dwx_ref[...] += _tn(xr, dia)
        rows = [jnp.sum(dxr, axis=0, keepdims=True), jnp.sum(dra, axis=0, keepdims=True),
                jnp.sum(dia, axis=0, keepdims=True), dlam]
        r8 = lax.broadcasted_iota(jnp.int32, (8, LRU_HALF), 0)
        acc = jnp.zeros((8, LRU_HALF), F32)
        for k, row in enumerate(rows):
            acc = jnp.where(r8 == k, row, acc)
        dv_ref[...] += acc

    nhalf = LRU_W // LRU_HALF
    col = pl.BlockSpec((seq, LRU_HALF), lambda j, b: (b, j))
    mat = pl.BlockSpec((None, LRU_HALF, LRU_HALF), lambda j, b: (j, 0, 0))
    return pl.pallas_call(
        body, name=name, grid=(nhalf, t // seq), in_specs=_lru_in_specs(seq) + [col],
        out_specs=[col, col, pl.BlockSpec((4, LRU_HALF), lambda j, b: (0, j)), mat, mat,
                   pl.BlockSpec((8, LRU_HALF), lambda j, b: (0, j))],
        out_shape=[SDS((t, LRU_W), F32), SDS((t, LRU_W), F32), SDS((4, LRU_W), F32),
                   SDS((nhalf, LRU_HALF, LRU_HALF), F32), SDS((nhalf, LRU_HALF, LRU_HALF), F32), SDS((8, LRU_W), F32)],
        compiler_params=_params(("arbitrary", "arbitrary")))(u, u, cw, cb, wa, ba, wx, bx, lam, dy)


NEG = -1e30


def _rel_bucket_map():
    dist = (np.arange(BLOCK_Q)[:, None] - np.arange(BLOCK_Q)[None, :]) % BLOCK_Q
    max_exact = REL_BUCKETS // 2
    large = max_exact + (np.log(np.maximum(dist, 1).astype(np.float32) / max_exact)
                         / math.log(BLOCK_Q / max_exact) * (REL_BUCKETS - max_exact)).astype(np.int32)
    large = np.minimum(large, REL_BUCKETS - 1)
    return np.where(dist < max_exact, dist, large).astype(np.int32)


def relbias_fwd(rel_bias, bmap, *, name):
    def body(rb_ref, bm_ref, o_ref):
        bm = bm_ref[...]
        for h in range(ATT_HEADS):
            acc = jnp.zeros((BLOCK_Q, BLOCK_Q), F32)
            for b in range(REL_BUCKETS):
                acc = jnp.where(bm == b, rb_ref[b, h], acc)
            o_ref[h] = acc

    return pl.pallas_call(
        body, name=name, in_specs=[pl.BlockSpec(memory_space=pltpu.SMEM), pl.BlockSpec(memory_space=pltpu.VMEM)],
        out_specs=pl.BlockSpec(memory_space=pltpu.VMEM), out_shape=SDS((ATT_HEADS, BLOCK_Q, BLOCK_Q), F32))(rel_bias, bmap)


def relbias_bwd(dbias, bmap, *, name):
    def body(db_ref, bm_ref, o_ref):
        bm = bm_ref[...]
        row = lax.broadcasted_iota(jnp.int32, (REL_BUCKETS, 128), 0)
        col = lax.broadcasted_iota(jnp.int32, (REL_BUCKETS, 128), 1)
        acc = jnp.zeros((REL_BUCKETS, 128), F32)
        for h in range(ATT_HEADS):
            d = db_ref[h]
            for b in range(REL_BUCKETS):
                s = jnp.sum(jnp.sum(jnp.where(bm == b, d, 0.0), axis=1, keepdims=True), axis=0, keepdims=True)
                acc = jnp.where((row == b) & (col == h), s, acc)
        o_ref[...] = acc

    return pl.pallas_call(body, name=name, out_shape=SDS((REL_BUCKETS, 128), F32))(dbias, bmap)


def _attn_probs(q_ref, k_ref, v_ref, b_ref, s_ref, n):
    rows = ATT_GROUP * BLOCK_Q
    qs = q_ref[...].reshape(rows, HEAD) * (HEAD ** -0.5)
    prev = pl.multiple_of(jnp.maximum(n - 1, 0) * BLOCK_Q, BLOCK_Q)
    cur = pl.multiple_of(n * BLOCK_Q, BLOCK_Q)
    kp, kc = k_ref[pl.ds(prev, BLOCK_Q), :], k_ref[pl.ds(cur, BLOCK_Q), :]
    vp, vc = v_ref[pl.ds(prev, BLOCK_Q), :], v_ref[pl.ds(cur, BLOCK_Q), :]
    bias = b_ref[...].reshape(rows, BLOCK_Q)
    i = lax.broadcasted_iota(jnp.int32, (rows, BLOCK_Q), 0) & (BLOCK_Q - 1)
    j = lax.broadcasted_iota(jnp.int32, (rows, BLOCK_Q), 1)
    s_p = jnp.where((j > i) & (n > 0), _nt(qs, kp) + bias, NEG)
    s_c = jnp.where(j <= i, _nt(qs, kc) + bias, NEG)
    sink = s_ref[...]
    m = jnp.maximum(jnp.maximum(jnp.max(s_p, axis=-1, keepdims=True), jnp.max(s_c, axis=-1, keepdims=True)), sink)
    e_p, e_c, e_s = jnp.exp(s_p - m), jnp.exp(s_c - m), jnp.exp(sink - m)
    inv = 1.0 / (jnp.sum(e_p, axis=-1, keepdims=True) + jnp.sum(e_c, axis=-1, keepdims=True) + e_s)
    return e_p * inv, e_c * inv, e_s * inv, qs, kp, kc, vp, vc, prev, cur


def _attn_specs(seq):
    qspec = pl.BlockSpec((None, ATT_GROUP, BLOCK_Q, HEAD), lambda g, b, n: (b, g, n, 0))
    kvspec = pl.BlockSpec((None, None, seq, HEAD), lambda g, b, n: (b, g, 0, 0))
    bspec = pl.BlockSpec((ATT_GROUP, BLOCK_Q, BLOCK_Q), lambda g, b, n: (g, 0, 0))
    sspec = pl.BlockSpec((ATT_GROUP * BLOCK_Q, 1), lambda g, b, n: (g, 0))
    return qspec, kvspec, bspec, sspec


def attn_fwd(q, k, v, bias, sink_rows, *, name):
    nb, _, seq, _ = q.shape

    def body(q_ref, k_ref, v_ref, b_ref, s_ref, o_ref):
        p_p, p_c, _, _, _, _, vp, vc, _, _ = _attn_probs(q_ref, k_ref, v_ref, b_ref, s_ref, pl.program_id(2))
        o_ref[...] = (_nn(p_p, vp) + _nn(p_c, vc)).reshape(ATT_GROUP, BLOCK_Q, HEAD)

    qspec, kvspec, bspec, sspec = _attn_specs(seq)
    return pl.pallas_call(
        body, name=name, grid=(KV_HEADS, nb, seq // BLOCK_Q), in_specs=[qspec, kvspec, kvspec, bspec, sspec],
        out_specs=qspec, out_shape=SDS(q.shape, F32),
        compiler_params=_params(("parallel", "parallel", "arbitrary")))(q, k, v, bias, sink_rows)


def attn_bwd(q, k, v, bias, sink_rows, do, *, name):
    nb, _, seq, _ = q.shape

    def body(q_ref, k_ref, v_ref, b_ref, s_ref, do_ref, dq_ref, dk_ref, dv_ref, db_ref, ds_ref):
        b, n = pl.program_id(1), pl.program_id(2)

        @pl.when((b == 0) & (n == 0))
        def _():
            db_ref[...] = jnp.zeros_like(db_ref)
            ds_ref[...] = jnp.zeros_like(ds_ref)

        @pl.when(n == 0)
        def _():
            dk_ref[...] = jnp.zeros_like(dk_ref)
            dv_ref[...] = jnp.zeros_like(dv_ref)

        p_p, p_c, p_s, qs, kp, kc, vp, vc, prev, cur = _attn_probs(q_ref, k_ref, v_ref, b_ref, s_ref, n)
        do = do_ref[...].reshape(ATT_GROUP * BLOCK_Q, HEAD)
        dp_p, dp_c = _nt(do, vp), _nt(do, vc)
        delta = jnp.sum(p_p * dp_p, axis=-1, keepdims=True) + jnp.sum(p_c * dp_c, axis=-1, keepdims=True)
        ds_p, ds_c = p_p * (dp_p - delta), p_c * (dp_c - delta)
        dq_ref[...] = ((_nn(ds_p, kp) + _nn(ds_c, kc)) * (HEAD ** -0.5)).reshape(ATT_GROUP, BLOCK_Q, HEAD)
        dk_ref[pl.ds(prev, BLOCK_Q), :] += _tn(ds_p, qs)
        dk_ref[pl.ds(cur, BLOCK_Q), :] += _tn(ds_c, qs)
        dv_ref[pl.ds(prev, BLOCK_Q), :] += _tn(p_p, do)
        dv_ref[pl.ds(cur, BLOCK_Q), :] += _tn(p_c, do)
        db_ref[...] += (ds_p + ds_c).reshape(ATT_GROUP, BLOCK_Q, BLOCK_Q)
        ds_ref[...] += -p_s * delta

    qspec, kvspec, bspec, sspec = _attn_specs(seq)
    return pl.pallas_call(
        body, name=name, grid=(KV_HEADS, nb, seq // BLOCK_Q), in_specs=[qspec, kvspec, kvspec, bspec, sspec, qspec],
        out_specs=[qspec, kvspec, kvspec, bspec, sspec],
        out_shape=[SDS(q.shape, F32), SDS(k.shape, F32), SDS(v.shape, F32),
                   SDS((ATT_HEADS, BLOCK_Q, BLOCK_Q), F32), SDS((ATT_HEADS * BLOCK_Q, 1), F32)],
        compiler_params=_params(("arbitrary", "arbitrary", "arbitrary")))(q, k, v, bias, sink_rows, do)


def _iota2(shape, axis):
    return lax.broadcasted_iota(jnp.int32, shape, axis)


def _col_to_row(col):
    c = col.shape[0]
    eye = _iota2((c, c), 0) == _iota2((c, c), 1)
    return jnp.sum(jnp.where(eye, jnp.broadcast_to(col, (c, c)), 0.0), axis=0, keepdims=True)


def _row_to_col(row):
    c = row.shape[1]
    eye = _iota2((c, c), 0) == _iota2((c, c), 1)
    return jnp.sum(jnp.where(eye, jnp.broadcast_to(row, (c, c)), 0.0), axis=1, keepdims=True)


def _last_row(col):
    c = col.shape[0]
    return jnp.sum(jnp.where(_iota2((c, 1), 0) == c - 1, col, 0.0), axis=0, keepdims=True)


def _chunk_cumsum(x):
    pos = _iota2(x.shape, 0) & (DN_CHUNK - 1)
    d = 1
    while d < DN_CHUNK:
        x = x + jnp.where(pos >= d, pltpu.roll(x, d, 0), 0.0)
        d *= 2
    return x


def _chunk_rev_cumsum(x):
    n = x.shape[0]
    pos = _iota2(x.shape, 0) & (DN_CHUNK - 1)
    d = 1
    while d < DN_CHUNK:
        x = x + jnp.where(pos < DN_CHUNK - d, pltpu.roll(x, n - d, 0), 0.0)
        d *= 2
    return x


def _tri_inv(low):
    c = low.shape[0]
    eye = (_iota2((c, c), 0) == _iota2((c, c), 1)).astype(F32)
    m = -low
    p = eye + m
    steps = int(math.log2(c)) - 1
    for _ in range(steps):
        m = _nn(m, m, hi=True)
        p = p + _nn(p, m, hi=True)
    return p


_DN_SCALE = (HEAD ** -0.5, 1.0, None)


def _dn_act(c, scale):
    sig = _sigmoid(c)
    a = c * sig
    if scale is None:
        return a, sig, None, None
    r = lax.rsqrt(jnp.sum(a * a, axis=-1, keepdims=True) + EPS)
    return a * r * scale, sig, a * r, r


def _dn_gates(ba_ref, hs_ref):
    beta = _sigmoid(ba_ref[0])
    sp_arg = ba_ref[1] + hs_ref[1]
    a_exp = jnp.exp(hs_ref[0])
    g = -a_exp * _softplus(sp_arg)
    return beta, g, sp_arg, a_exp


def _dn_inputs(pre_ref, cw_ref, ba_ref, hs_ref, act_sc, b_sc, gc_sc, c_sc=None):
    for idx in range(3):
        c = _conv_fwd(pre_ref[idx], [cw_ref[idx, k:k + 1, :] for k in range(4)])
        if c_sc is not None:
            c_sc[idx] = c
        act_sc[idx] = _dn_act(c, _DN_SCALE[idx])[0]
    beta, g, _, _ = _dn_gates(ba_ref, hs_ref)
    b_sc[...] = beta
    gc_sc[...] = _chunk_cumsum(g)


def _dn_chunk_math(q, k, v, b, gcc):
    c = q.shape[0]
    tril = _iota2((c, c), 0) >= _iota2((c, c), 1)
    strict = _iota2((c, c), 0) > _iota2((c, c), 1)
    eg = jnp.exp(gcc)
    kb, vb = k * b, v * b
    kbg = kb * eg
    dm = jnp.exp(jnp.where(tril, jnp.broadcast_to(gcc, (c, c)) - _col_to_row(gcc), NEG))
    kk = _nt(kb, k)
    t = _tri_inv(jnp.where(strict, kk * dm, 0.0))
    glast = _last_row(gcc)
    ekd = jnp.exp(glast - gcc)
    qk = _nt(q, k)
    return dict(tril=tril, strict=strict, eg=eg, kb=kb, vb=vb, kbg=kbg, dm=dm, kk=kk, t=t, glast=glast, ekd=ekd,
                kd=k * ekd, qk=qk, amat=jnp.where(tril, qk * dm, 0.0), qg=q * eg,
                egl=jnp.broadcast_to(jnp.exp(glast), (c, 1)))


DN_UNROLL = 4


def _chunk_loop(nc, chunk):
    u = math.gcd(nc, DN_UNROLL)

    def step(i, carry):
        for j in range(u):
            chunk(i * u + j)
        return carry

    lax.fori_loop(0, nc // u, step, 0)


def _dn_specs(seq):
    s64 = lambda lead: pl.BlockSpec((lead, None, None, seq, HEAD), lambda b, h: (0, b, h, 0, 0))
    s1 = lambda lead: pl.BlockSpec((lead, None, None, seq, 1), lambda b, h: (0, b, h, 0, 0))
    one64 = pl.BlockSpec((None, None, seq, HEAD), lambda b, h: (b, h, 0, 0))
    one1 = pl.BlockSpec((None, None, seq, 1), lambda b, h: (b, h, 0, 0))
    cw = pl.BlockSpec((None, 3, 4, HEAD), lambda b, h: (h, 0, 0, 0))
    hs = pl.BlockSpec((None, 2, 1, 1), lambda b, h: (h, 0, 0, 0))
    return s64, s1, one64, one1, cw, hs


def dn_prep(pre, cw, ba, hs, *, name):
    _, nb, nh, seq, _ = pre.shape
    nc = seq // DN_CHUNK

    def body(pre_ref, cw_ref, ba_ref, hs_ref, loc_ref, egl_ref, act_sc, b_sc, gc_sc):
        _dn_inputs(pre_ref, cw_ref, ba_ref, hs_ref, act_sc, b_sc, gc_sc)

        def chunk(c):
            rows = pl.ds(pl.multiple_of(c * DN_CHUNK, DN_CHUNK), DN_CHUNK)
            m = _dn_chunk_math(act_sc[0, rows, :], act_sc[1, rows, :], act_sc[2, rows, :], b_sc[rows, :], gc_sc[rows, :])
            loc_ref[0, rows, :] = m["qg"]
            loc_ref[1, rows, :] = m["kd"]
            loc_ref[2, rows, :] = _nn(m["t"], m["vb"])
            loc_ref[3, rows, :] = _nn(m["t"], m["kbg"])
            loc_ref[4, rows, :] = m["amat"]
            egl_ref[rows, :] = m["egl"]

        _chunk_loop(nc, chunk)

    s64, s1, one64, one1, cwspec, hsspec = _dn_specs(seq)
    return pl.pallas_call(
        body, name=name, grid=(nb, nh), in_specs=[s64(3), cwspec, s1(2), hsspec], out_specs=[s64(5), one1],
        out_shape=[SDS((5, nb, nh, seq, HEAD), F32), SDS((nb, nh, seq, 1), F32)],
        scratch_shapes=[pltpu.VMEM((3, seq, HEAD), F32)] + [pltpu.VMEM((seq, 1), F32)] * 2,
        compiler_params=_params(("parallel", "parallel")))(pre, cw, ba, hs)


def _gated_norm(o, z, gn):
    r = lax.rsqrt(jnp.mean(o * o, axis=-1, keepdims=True) + EPS)
    sig = _sigmoid(z)
    return o * r, sig, r


def dn_scan(loc, egl, z, gn, *, name):
    _, nb, nh, seq, _ = loc.shape
    nc = seq // DN_CHUNK

    def body(loc_ref, egl_ref, z_ref, gn_ref, y_ref, o_ref, vn_ref, st_ref):
        gn = gn_ref[...]

        def step(c, state):
            rows = pl.ds(pl.multiple_of(c * DN_CHUNK, DN_CHUNK), DN_CHUNK)
            st_ref[rows, :] = state
            vn = loc_ref[2, rows, :] - _nn(loc_ref[3, rows, :], state)
            o = _nn(loc_ref[0, rows, :], state) + _nn(loc_ref[4, rows, :], vn)
            vn_ref[rows, :] = vn
            o_ref[rows, :] = o
            zz = z_ref[rows, :]
            on, sig, _ = _gated_norm(o, zz, gn)
            y_ref[rows, :] = on * gn * (zz * sig)
            return state * egl_ref[rows, :] + _tn(loc_ref[1, rows, :], vn)

        lax.fori_loop(0, nc, step, jnp.zeros((HEAD, HEAD), F32))

    s64, s1, one64, one1, cwspec, hsspec = _dn_specs(seq)
    out = SDS((nb, nh, seq, HEAD), F32)
    return pl.pallas_call(
        body, name=name, grid=(nb, nh), in_specs=[s64(5), one1, one64, _whole((1, HEAD))],
        out_specs=[one64] * 4, out_shape=[out] * 4,
        compiler_params=_params(("parallel", "parallel")))(loc, egl, z, gn)


def dn_scan_bwd(loc, egl, z, gn, o, vn, states, dy, *, name):
    _, nb, nh, seq, _ = loc.shape
    nc = seq // DN_CHUNK

    def body(loc_ref, egl_ref, z_ref, gn_ref, o_ref, vn_ref, st_ref, dy_ref, dloc_ref, degl_ref, dz_ref, dgn_ref):
        @pl.when((pl.program_id(0) == 0) & (pl.program_id(1) == 0))
        def _():
            dgn_ref[...] = jnp.zeros_like(dgn_ref)

        gn = gn_ref[...]
        tril = _iota2((DN_CHUNK, DN_CHUNK), 0) >= _iota2((DN_CHUNK, DN_CHUNK), 1)

        def step(i, carry):
            ds, dgn = carry
            rows = pl.ds(pl.multiple_of((nc - 1 - i) * DN_CHUNK, DN_CHUNK), DN_CHUNK)
            dy, zz, oo = dy_ref[rows, :], z_ref[rows, :], o_ref[rows, :]
            on, sig, r = _gated_norm(oo, zz, gn)
            sz = zz * sig
            dz_ref[rows, :] = dy * on * gn * (sig * (1.0 + zz * (1.0 - sig)))
            dgn = dgn + jnp.sum(dy * on * sz, axis=0, keepdims=True)
            don = dy * gn * sz
            do = r * (don - on * jnp.mean(don * on, axis=-1, keepdims=True))
            state, vnew = st_ref[rows, :], vn_ref[rows, :]
            qg, kd, w, amat = loc_ref[0, rows, :], loc_ref[1, rows, :], loc_ref[3, rows, :], loc_ref[4, rows, :]
            dvn = _tn(amat, do) + _nn(kd, ds)
            dloc_ref[0, rows, :] = _nt(do, state)
            dloc_ref[1, rows, :] = _nt(vnew, ds)
            dloc_ref[2, rows, :] = dvn
            dloc_ref[3, rows, :] = -_nt(dvn, state)
            dloc_ref[4, rows, :] = jnp.where(tril, _nt(do, vnew), 0.0)
            degl = jnp.sum(jnp.sum(state * ds, axis=1, keepdims=True), axis=0, keepdims=True)
            degl_ref[rows, :] = jnp.broadcast_to(degl, (DN_CHUNK, 1))
            return ds * egl_ref[rows, :] + _tn(qg, do) - _tn(w, dvn), dgn

        _, dgn = lax.fori_loop(0, nc, step, (jnp.zeros((HEAD, HEAD), F32), jnp.zeros((1, HEAD), F32)))
        dgn_ref[...] += dgn

    s64, s1, one64, one1, cwspec, hsspec = _dn_specs(seq)
    return pl.pallas_call(
        body, name=name, grid=(nb, nh),
        in_specs=[s64(5), one1, one64, _whole((1, HEAD)), one64, one64, one64, one64],
        out_specs=[s64(5), one1, one64, _whole((1, HEAD))],
        out_shape=[SDS((5, nb, nh, seq, HEAD), F32), SDS((nb, nh, seq, 1), F32), SDS((nb, nh, seq, HEAD), F32),
                   SDS((1, HEAD), F32)],
        compiler_params=_params(("arbitrary", "arbitrary")))(loc, egl, z, gn, o, vn, states, dy)


def dn_prep_bwd(pre, cw, ba, hs, dloc, degl, *, name):
    _, nb, nh, seq, _ = pre.shape
    nc = seq // DN_CHUNK

    def body(pre_ref, cw_ref, ba_ref, hs_ref, dloc_ref, degl_ref, dpre_ref, dba_ref, dcw_ref, dhs_ref,
             act_sc, b_sc, gc_sc, c_sc):
        @pl.when(pl.program_id(1) == 0)
        def _():
            dcw_ref[...] = jnp.zeros_like(dcw_ref)
            dhs_ref[...] = jnp.zeros_like(dhs_ref)

        _dn_inputs(pre_ref, cw_ref, ba_ref, hs_ref, act_sc, b_sc, gc_sc, c_sc)

        def chunk(c):
            rows = pl.ds(pl.multiple_of(c * DN_CHUNK, DN_CHUNK), DN_CHUNK)
            q, k, v, b, gcc = act_sc[0, rows, :], act_sc[1, rows, :], act_sc[2, rows, :], b_sc[rows, :], gc_sc[rows, :]
            m = _dn_chunk_math(q, k, v, b, gcc)
            dqg, dkd, du, dw, da = (dloc_ref[x, rows, :] for x in range(5))
            t, dm, eg = m["t"], m["dm"], m["eg"]
            dt = _nt(du, m["vb"]) + _nt(dw, m["kbg"])
            dvb, dkbg = _tn(t, du), _tn(t, dw)
            dl = jnp.where(m["strict"], -_tn(t, _nt(dt, t, hi=True), hi=True), 0.0)
            dkk = dl * dm
            dqk = da * dm
            dd = dl * m["kk"] + da * m["qk"]
            dkb = _nn(dkk, k) + dkbg * eg
            dq = _nn(dqk, k) + dqg * eg
            dk = _tn(dkk, m["kb"]) + _tn(dqk, q) + dkd * m["ekd"] + dkb * b
            db = jnp.sum(dkb * k, axis=-1, keepdims=True) + jnp.sum(dvb * v, axis=-1, keepdims=True)
            mx = jnp.where(m["tril"], dd * dm, 0.0)
            tk = jnp.sum(dkd * m["kd"], axis=-1, keepdims=True)
            dgc = (jnp.sum(mx, axis=-1, keepdims=True) - _row_to_col(jnp.sum(mx, axis=0, keepdims=True))
                   + jnp.sum(dqg * m["qg"], axis=-1, keepdims=True) + jnp.sum(dkbg * m["kbg"], axis=-1, keepdims=True) - tk)
            dglast = jnp.sum(tk, axis=0, keepdims=True) + _last_row(degl_ref[rows, :]) * jnp.exp(m["glast"])
            act_sc[0, rows, :] = dq
            act_sc[1, rows, :] = dk
            act_sc[2, rows, :] = dvb * b
            b_sc[rows, :] = db
            gc_sc[rows, :] = dgc + jnp.where(_iota2((DN_CHUNK, 1), 0) == DN_CHUNK - 1, dglast, 0.0)

        _chunk_loop(nc, chunk)

        beta, g, sp_arg, a_exp = _dn_gates(ba_ref, hs_ref)
        dg = _chunk_rev_cumsum(gc_sc[...])
        dal = dg * (-a_exp) * _sigmoid(sp_arg)
        dba_ref[0] = b_sc[...] * beta * (1.0 - beta)
        dba_ref[1] = dal
        dhs_ref[0] += jnp.sum(dg * g, axis=0, keepdims=True)
        dhs_ref[1] += jnp.sum(dal, axis=0, keepdims=True)
        for idx in range(3):
            c = c_sc[idx]
            _, sig, hat, r = _dn_act(c, _DN_SCALE[idx])
            da_ = act_sc[idx]
            if _DN_SCALE[idx] is not None:
                da_ = da_ * _DN_SCALE[idx]
                da_ = r * (da_ - hat * jnp.sum(da_ * hat, axis=-1, keepdims=True))
            dx, dcw = _conv_bwd(da_ * (sig * (1.0 + c * (1.0 - sig))), pre_ref[idx],
                                [cw_ref[idx, k:k + 1, :] for k in range(4)])
            dpre_ref[idx] = dx
            dcw_ref[idx] += dcw

    s64, s1, one64, one1, cwspec, hsspec = _dn_specs(seq)
    swap = lambda spec: pl.BlockSpec(spec.block_shape, lambda h, b, _f=spec.index_map: _f(b, h))
    return pl.pallas_call(
        body, name=name, grid=(nh, nb),
        in_specs=[swap(s64(3)), swap(cwspec), swap(s1(2)), swap(hsspec), swap(s64(5)), swap(one1)],
        out_specs=[swap(s64(3)), swap(s1(2)), swap(cwspec), swap(hsspec)],
        out_shape=[SDS((3, nb, nh, seq, HEAD), F32), SDS((2, nb, nh, seq, 1), F32), SDS((nh, 3, 4, HEAD), F32),
                   SDS((nh, 2, 1, 1), F32)],
        scratch_shapes=[pltpu.VMEM((3, seq, HEAD), F32)] + [pltpu.VMEM((seq, 1), F32)] * 2 + [pltpu.VMEM((3, seq, HEAD), F32)],
        compiler_params=_params(("arbitrary", "arbitrary")))(pre, cw, ba, hs, dloc, degl)


def _block_diag(w):
    out = jnp.zeros((LRU_W, LRU_W), w.dtype)
    for h in range(LRU_W // HEAD):
        out = lax.dynamic_update_slice(out, w[h], (h * HEAD, h * HEAD))
    return out


def _diag_blocks(w):
    per = LRU_HALF // HEAD
    return jnp.stack([w[h // per, (h % per) * HEAD:(h % per + 1) * HEAD, (h % per) * HEAD:(h % per + 1) * HEAD]
                      for h in range(LRU_W // HEAD)])


def layer_params(w, l, bias):
    row = lambda a: a[l].reshape(1, -1)
    return dict(
        ffn1_norm=row(w["ffn1_norm"]), ffn1=(w["ffn1_w_gate"][:, l], w["ffn1_w_up"][:, l], w["ffn1_w_down"][:, l]),
        mix_norm=row(w["mix_norm"]), w_in=w["w_in"][l],
        lru=(w["lru_conv_w"][l], row(w["lru_conv_b"]), _block_diag(w["lru_w_a"][l]), row(w["lru_b_a"]),
             _block_diag(w["lru_w_x"][l]), row(w["lru_b_x"]), row(w["lru_lambda"])),
        bias=bias, sink_rows=jnp.repeat(w["attn_sinks"][l], BLOCK_Q).reshape(ATT_HEADS * BLOCK_Q, 1),
        dn_cw=w["dn_conv_w"][l].reshape(4, 3, DN_HEADS, HEAD).transpose(2, 1, 0, 3),
        dn_hs=jnp.stack([w["dn_a_log"][l], w["dn_dt_bias"][l]], axis=1).reshape(DN_HEADS, 2, 1, 1),
        dn_norm=row(w["dn_norm"]), w_out=w["w_out"][l],
        ffn2_norm=row(w["ffn2_norm"]), ffn2=(w["ffn2_w_gate"][:, l], w["ffn2_w_up"][:, l], w["ffn2_w_down"][:, l]),
        ple_norm=row(w["ple_norm"]), ple_w_gate=w["ple_w_gate"][l], ple_w_proj=w["ple_w_proj"][l])


def _to_heads(a, nb, seq, nh):
    return a.reshape(nb, seq, nh, HEAD).transpose(0, 2, 1, 3)


def _from_heads(a):
    nb, nh, seq, _ = a.shape
    return a.transpose(0, 2, 1, 3).reshape(nb * seq, nh * HEAD)


def mixer_fwd(h, p, nb, seq, tag):
    u, n = norm_matmul(h, p["mix_norm"], p["w_in"], name=f"mix_in_{tag}")
    y_lru = lru_fwd(u, *p["lru"], seq=seq, name=f"lru_fwd_{tag}")
    q = _to_heads(u[:, 512:1024], nb, seq, ATT_HEADS)
    k = _to_heads(u[:, 1024:1152], nb, seq, KV_HEADS)
    v = _to_heads(u[:, 1152:1280], nb, seq, KV_HEADS)
    o = attn_fwd(q, k, v, p["bias"], p["sink_rows"], name=f"attn_fwd_{tag}")
    dn4 = u[:, 1280:2304].reshape(nb, seq, 4, DN_HEADS, HEAD).transpose(2, 0, 3, 1, 4)
    pre, z = dn4[:3], dn4[3]
    ba = u[:, 2304:2312].reshape(nb, seq, 2, DN_HEADS).transpose(2, 0, 3, 1)[..., None]
    loc, egl = dn_prep(pre, p["dn_cw"], ba, p["dn_hs"], name=f"dn_prep_{tag}")
    y_dn, o_raw, vn, st = dn_scan(loc, egl, z, p["dn_norm"], name=f"dn_scan_{tag}")
    ycat = jnp.concatenate([y_lru, _from_heads(o), _from_heads(y_dn)], axis=-1)
    out = matmul(ycat, p["w_out"], residual=h, name=f"mix_out_{tag}")
    return out, dict(h=h, u=u, n=n, q=q, k=k, v=v, pre=pre, z=z, ba=ba, loc=loc, egl=egl, o_raw=o_raw, vn=vn, st=st,
                     ycat=ycat)


def mixer_bwd(dout, s, p, nb, seq, tag):
    dycat = matmul(dout, p["w_out"], tb=True, name=f"mix_out_dx_{tag}")
    g = {"w_out": matmul(s["ycat"], dout, ta=True, name=f"mix_out_dw_{tag}")}
    dx_lru, dgate_lru, dcw, dwa, dwx, dvec = lru_bwd(s["u"], *p["lru"], dycat[:, :LRU_W], seq=seq, name=f"lru_bwd_{tag}")
    g.update(lru_conv_w=dcw, lru_conv_b=dvec[0], lru_w_a=_diag_blocks(dwa), lru_b_a=dvec[1], lru_w_x=_diag_blocks(dwx),
             lru_b_x=dvec[2], lru_lambda=dvec[3])
    do = _to_heads(dycat[:, LRU_W:LRU_W + ATT_W], nb, seq, ATT_HEADS)
    dq, dk, dv, dbias, dsink = attn_bwd(s["q"], s["k"], s["v"], p["bias"], p["sink_rows"], do, name=f"attn_bwd_{tag}")
    g.update(attn_sinks=dsink.reshape(ATT_HEADS, BLOCK_Q).sum(axis=1), bias=dbias)
    dy_dn = _to_heads(dycat[:, LRU_W + ATT_W:], nb, seq, DN_HEADS)
    dloc, degl, dz, dgn = dn_scan_bwd(s["loc"], s["egl"], s["z"], p["dn_norm"], s["o_raw"], s["vn"], s["st"], dy_dn,
                                      name=f"dn_scan_bwd_{tag}")
    dpre, dba, dcwh, dhs = dn_prep_bwd(s["pre"], p["dn_cw"], s["ba"], p["dn_hs"], dloc, degl, name=f"dn_prep_bwd_{tag}")
    g.update(dn_conv_w=dcwh.transpose(2, 1, 0, 3).reshape(4, 3 * DN_HEADS * HEAD), dn_a_log=dhs[:, 0, 0, 0],
             dn_dt_bias=dhs[:, 1, 0, 0], dn_norm=dgn[0])
    du_dn = jnp.concatenate([dpre, dz[None]], axis=0).transpose(1, 3, 0, 2, 4).reshape(nb * seq, 4 * DN_HEADS * HEAD)
    du_ba = dba[..., 0].transpose(1, 3, 0, 2).reshape(nb * seq, 2 * DN_HEADS)
    du = jnp.concatenate([dx_lru, dgate_lru, _from_heads(dq), _from_heads(dk), _from_heads(dv), du_dn, du_ba,
                          jnp.zeros((nb * seq, D_IN_PAD - D_IN), F32)], axis=-1)
    dn = matmul(du, p["w_in"], tb=True, name=f"mix_in_dx_{tag}")
    g["w_in"] = matmul(s["n"], du, ta=True, name=f"mix_in_dw_{tag}")
    dh, dgain = rms_bwd(s["h"], p["mix_norm"], dn, dout, name=f"mix_norm_bwd_{tag}")
    g["mix_norm"] = dgain[0]
    return dh, g


SHARDED = ("ffn1_w_gate", "ffn1_w_up", "ffn1_w_down", "w_in", "w_out", "ffn2_w_gate", "ffn2_w_up", "ffn2_w_down",
           "ple_w_gate", "ple_w_proj")
PER_LAYER_SMALL = ("ffn1_norm", "mix_norm", "lru_conv_w", "lru_conv_b", "lru_w_a", "lru_b_a", "lru_w_x", "lru_b_x",
                   "lru_lambda", "attn_sinks", "dn_conv_w", "dn_a_log", "dn_dt_bias", "dn_norm", "ffn2_norm", "ple_norm")


def _col_shards(a):
    r, c = a.shape
    return a.reshape(r, N_CHIP, c // N_CHIP).transpose(1, 0, 2)


def local_step(x, p, target, w, bmap, nb, seq):
    bias = relbias_fwd(w["rel_bias"], bmap, name="relbias_fwd")
    h, saved = x, []
    for l in range(N_LAYER):
        pr = layer_params(w, l, bias)
        s = dict(h0=h)
        h = ffn_fwd(h, pr["ffn1_norm"], *pr["ffn1"], name=f"ffn1_fwd_{l}")
        h, s["mix"] = mixer_fwd(h, pr, nb, seq, l)
        s["h2"] = h
        h = ffn_fwd(h, pr["ffn2_norm"], *pr["ffn2"], name=f"ffn2_fwd_{l}")
        s["h3"] = h
        h = ple_fwd(h, pr["ple_norm"], pr["ple_w_gate"], p[l], pr["ple_w_proj"], name=f"ple_fwd_{l}")
        saved.append((pr, s))
    dh, dgf, loss = loss_head(h, w["final_norm"].reshape(1, -1), target, name="loss_head")

    per_layer, dbias = [None] * N_LAYER, None
    for l in reversed(range(N_LAYER)):
        pr, s = saved[l]
        g = {}
        dout = dh
        dh, n, dga, dpp, dg = ple_bwd(s["h3"], pr["ple_norm"], pr["ple_w_gate"], p[l], pr["ple_w_proj"], dout, name=f"ple_bwd_{l}")
        g["ple_norm"] = dg[0]
        g["ple_w_gate"] = matmul(n, dga, ta=True, name=f"ple_dwg_{l}").reshape(N_CHIP, -1, D_MODEL)
        g["ple_w_proj"] = _col_shards(matmul(p[l], dpp, ta=True, name=f"ple_dwp_{l}"))
        for nm, hin in (("ffn2", s["h2"]), ("ffn1", s["h0"])):
            if nm == "ffn1":
                dh, gm = mixer_bwd(dh, s["mix"], pr, nb, seq, l)
                dbias = gm.pop("bias") if dbias is None else dbias + gm.pop("bias")
                gm["w_in"] = _col_shards(gm["w_in"][:, :D_IN])
                gm["w_out"] = gm["w_out"].reshape(N_CHIP, -1, D_MODEL)
                g.update(gm)
            dout = dh
            dh, n, da, db, sact, dg = ffn_bwd_act(hin, pr[nm + "_norm"], dout, *pr[nm], name=f"{nm}_bwd_act_{l}")
            g[nm + "_norm"] = dg[0]
            g[nm + "_w_gate"], g[nm + "_w_up"], g[nm + "_w_down"] = ffn_bwd_w(n, da, db, sact, dout, name=f"{nm}_bwd_w_{l}")
        per_layer[l] = g
    grads = {k: jnp.stack([per_layer[l][k] for l in range(N_LAYER)]) for k in SHARDED + PER_LAYER_SMALL}
    grads["rel_bias"] = relbias_bwd(dbias, bmap, name="relbias_bwd")[:, :ATT_HEADS]
    grads["final_norm"] = dgf[0]
    return loss, dh, grads


HBM_SPEC = pl.BlockSpec(memory_space=pltpu.HBM)


def _place():
    x, y, c = lax.axis_index("x"), lax.axis_index("y"), lax.axis_index("c")
    chips = [(1 - x, y), (x, 1 - y), (1 - x, 1 - y)]
    return x, y, c, 2 * x + y, (x, y, 1 - c), chips, [2 * cx + cy for cx, cy in chips]


def _remote(src, dst, send_sem, recv_sem, to):
    return pltpu.make_async_remote_copy(src_ref=src, dst_ref=dst, send_sem=send_sem, recv_sem=recv_sem, device_id=to,
                                        device_id_type=MESH)


def allgather_shards(shards, *, name):
    n = len(shards)

    def body(*refs):
        ins, outs = refs[:n], refs[n:2 * n]
        send, recv, fsend, frecv, lsem = refs[2 * n:]
        x, y, c, me, sib, chips, cids = _place()
        local, first, passed = [], [], []
        for k in range(n):
            local.append(pltpu.make_async_copy(ins[k], outs[k].at[me], lsem.at[k]))
            local[-1].start()
            for j, chip in enumerate(chips):
                first.append(_remote(ins[k].at[c], outs[k].at[me, c], send.at[3 * k + j], recv.at[3 * k + j], (*chip, c)))
                first[-1].start()
        for k in range(n):
            for j in range(3):
                piece = outs[k].at[cids[j], c]
                _remote(piece, piece, send.at[3 * k + j], recv.at[3 * k + j], sib).wait_recv()
                passed.append(_remote(piece, piece, fsend.at[3 * k + j], frecv.at[3 * k + j], sib))
                passed[-1].start()
        for k in range(n):
            for j in range(3):
                piece = outs[k].at[cids[j], 1 - c]
                _remote(piece, piece, fsend.at[3 * k + j], frecv.at[3 * k + j], sib).wait_recv()
        for cp in first + passed:
            cp.wait_send()
        for cp in local:
            cp.wait()

    return pl.pallas_call(
        body, name=name, in_specs=[HBM_SPEC] * n, out_specs=[HBM_SPEC] * n,
        out_shape=[SDS((N_CHIP,) + s.shape, s.dtype) for s in shards],
        scratch_shapes=[pltpu.SemaphoreType.DMA((3 * n,))] * 4 + [pltpu.SemaphoreType.DMA((n,))])(*shards)


def exchange_layers(gs, *, name):
    n = len(gs)

    def body(*refs):
        ins, outs, (send, recv) = refs[:n], refs[n:2 * n], refs[2 * n:]
        x, y, c, me, sib, chips, cids = _place()
        cps = [_remote(ins[k].at[1 - c], outs[k], send.at[k], recv.at[k], sib) for k in range(n)]
        for cp in cps:
            cp.start()
        for cp in cps:
            cp.wait()

    return pl.pallas_call(
        body, name=name, in_specs=[HBM_SPEC] * n, out_specs=[HBM_SPEC] * n,
        out_shape=[SDS(g.shape[1:], g.dtype) for g in gs], scratch_shapes=[pltpu.SemaphoreType.DMA((n,))] * 2)(*gs)


def reduce_to_shards(ss, *, name):
    n = len(ss)

    def body(*refs):
        ins, outs, (send, recv, lsem) = refs[:n], refs[n:2 * n], refs[2 * n:]
        x, y, c, me, sib, chips, cids = _place()
        local, cps = [], []
        for k in range(n):
            local.append(pltpu.make_async_copy(ins[k].at[me], outs[k].at[me], lsem.at[k]))
            local[-1].start()
            for j, chip in enumerate(chips):
                cps.append(_remote(ins[k].at[cids[j]], outs[k].at[me], send.at[3 * k + j], recv.at[3 * k + j], (*chip, c)))
                cps[-1].start()
        for k in range(n):
            for j in range(3):
                slot = outs[k].at[cids[j]]
                _remote(slot, slot, send.at[3 * k + j], recv.at[3 * k + j], sib).wait_recv()
        for cp in cps:
            cp.wait_send()
        for cp in local:
            cp.wait()

    return pl.pallas_call(
        body, name=name, in_specs=[HBM_SPEC] * n, out_specs=[HBM_SPEC] * n, out_shape=[SDS(s.shape, s.dtype) for s in ss],
        scratch_shapes=[pltpu.SemaphoreType.DMA((3 * n,))] * 2 + [pltpu.SemaphoreType.DMA((n,))])(*ss)


def share_layers(fs, *, name):
    n = len(fs)

    def body(*refs):
        ins, outs, (send, recv, lsem) = refs[:n], refs[n:2 * n], refs[2 * n:]
        x, y, c, me, sib, chips, cids = _place()
        local = [pltpu.make_async_copy(ins[k], outs[k].at[c], lsem.at[k]) for k in range(n)]
        cps = [_remote(ins[k], outs[k].at[c], send.at[k], recv.at[k], sib) for k in range(n)]
        for cp in local + cps:
            cp.start()
        for k in range(n):
            theirs = outs[k].at[1 - c]
            _remote(theirs, theirs, send.at[k], recv.at[k], sib).wait_recv()
        for cp in cps:
            cp.wait_send()
        for cp in local:
            cp.wait()

    return pl.pallas_call(
        body, name=name, in_specs=[HBM_SPEC] * n, out_specs=[HBM_SPEC] * n,
        out_shape=[SDS((N_LAYER,) + f.shape, f.dtype) for f in fs], scratch_shapes=[pltpu.SemaphoreType.DMA((n,))] * 3)(*fs)


N_DEV = 8


def allreduce_small(buf, *, name):
    rows = buf.shape[0]

    def body(in_ref, out_ref, gath, send, recv):
        x, y, c = lax.axis_index("x"), lax.axis_index("y"), lax.axis_index("c")
        mine = 4 * x + 2 * y + c
        gath[mine] = in_ref[...]
        cps = []
        for k in range(1, N_DEV):
            to = (x ^ (k >> 2), y ^ ((k >> 1) & 1), c ^ (k & 1))
            cps.append(_remote(in_ref, gath.at[mine], send.at[k - 1], recv.at[k - 1], to))
            cps[-1].start()
        for k in range(1, N_DEV):
            theirs = gath.at[4 * (x ^ (k >> 2)) + 2 * (y ^ ((k >> 1) & 1)) + (c ^ (k & 1))]
            _remote(theirs, theirs, send.at[k - 1], recv.at[k - 1], (x, y, c)).wait_recv()
        for cp in cps:
            cp.wait_send()
        acc = gath[0]
        for d in range(1, N_DEV):
            acc = acc + gath[d]
        out_ref[...] = acc

    vm = pl.BlockSpec(memory_space=pltpu.VMEM)
    return pl.pallas_call(
        body, name=name, in_specs=[vm], out_specs=vm, out_shape=SDS(buf.shape, F32),
        scratch_shapes=[pltpu.VMEM((N_DEV, rows, 128), F32), pltpu.SemaphoreType.DMA((N_DEV - 1,)),
                        pltpu.SemaphoreType.DMA((N_DEV - 1,))])(buf)


def add_sibling(g, r, c_arr, *, name, tr=256):
    _, m, cdim = g.shape
    assert m % tr == 0

    def body(c_ref, g_ref, r_ref, o_ref):
        o_ref[...] = (g_ref[...] + r_ref[...]).astype(o_ref.dtype)

    return pl.pallas_call(
        body, name=name,
        grid_spec=pltpu.PrefetchScalarGridSpec(
            num_scalar_prefetch=1, grid=(m // tr,),
            in_specs=[pl.BlockSpec((None, tr, cdim), lambda i, c: (c[0], i, 0)), pl.BlockSpec((tr, cdim), lambda i, c: (i, 0))],
            out_specs=pl.BlockSpec((tr, cdim), lambda i, c: (i, 0))),
        out_shape=SDS((m, cdim), BF16), compiler_params=_params(("parallel",)))(c_arr, g, r)


def sum_slots(r, *, name, tr=256):
    _, m, cdim = r.shape
    tr = next(cand for cand in (tr, 128, 64, 32, 16, 8) if m % cand == 0)

    def body(r_ref, o_ref):
        o_ref[...] = ((r_ref[0].astype(F32) + r_ref[1].astype(F32)) + r_ref[2].astype(F32)) + r_ref[3].astype(F32)

    return pl.pallas_call(
        body, name=name, grid=(m // tr,), in_specs=[pl.BlockSpec((N_CHIP, tr, cdim), lambda i: (0, i, 0))],
        out_specs=pl.BlockSpec((tr, cdim), lambda i: (i, 0)), out_shape=SDS((m, cdim), F32),
        compiler_params=_params(("parallel",)))(r)


WEIGHTS = ("ffn1_norm", "ffn1_w_gate", "ffn1_w_up", "ffn1_w_down", "mix_norm", "w_in", "lru_conv_w", "lru_conv_b", "lru_w_a",
           "lru_b_a", "lru_w_x", "lru_b_x", "lru_lambda", "attn_sinks", "rel_bias", "dn_conv_w", "dn_a_log", "dn_dt_bias",
           "dn_norm", "w_out", "ffn2_norm", "ffn2_w_gate", "ffn2_w_up", "ffn2_w_down", "ple_norm", "ple_w_gate",
           "ple_w_proj", "final_norm")
CONV_SHARDED = ("lru_conv_w", "dn_conv_w")
SMALL = tuple(k for k in WEIGHTS if k not in SHARDED)


def _pack(arrs):
    flat = []
    for a in arrs:
        v = a.reshape(-1)
        flat.append(jnp.pad(v, (0, -v.shape[0] % 128)))
    v = jnp.concatenate(flat)
    v = jnp.pad(v, (0, -v.shape[0] % 1024))
    return v.reshape(-1, 128)


def _unpack(buf, shapes):
    v, out, off = buf.reshape(-1), [], 0
    for s in shapes:
        n = int(np.prod(s))
        out.append(v[off:off + n].reshape(s))
        off += n + (-n % 128)
    return out


def _chip_cols(a):
    n, l, r, c = a.shape
    return a.transpose(1, 2, 0, 3).reshape(l, r, n * c)


def _chip_rows(a):
    n, l, r, c = a.shape
    return a.transpose(1, 0, 2, 3).reshape(l, n * r, c)


def kernel(x, p, ffn1_norm, ffn1_w_gate, ffn1_w_up, ffn1_w_down, mix_norm, w_in, lru_conv_w, lru_conv_b, lru_w_a, lru_b_a, lru_w_x, lru_b_x, lru_lambda, attn_sinks, rel_bias, dn_conv_w, dn_a_log, dn_dt_bias, dn_norm, w_out, ffn2_norm, ffn2_w_gate, ffn2_w_up, ffn2_w_down, ple_norm, ple_w_gate, ple_w_proj, final_norm, loss_target, m_ffn1_norm, m_ffn1_w_gate, m_ffn1_w_up, m_ffn1_w_down, m_mix_norm, m_w_in, m_lru_conv_w, m_lru_conv_b, m_lru_w_a, m_lru_b_a, m_lru_w_x, m_lru_b_x, m_lru_lambda, m_attn_sinks, m_rel_bias, m_dn_conv_w, m_dn_a_log, m_dn_dt_bias, m_dn_norm, m_w_out, m_ffn2_norm, m_ffn2_w_gate, m_ffn2_w_up, m_ffn2_w_down, m_ple_norm, m_ple_w_gate, m_ple_w_proj, m_final_norm, v_ffn1_norm, v_ffn1_w_gate, v_ffn1_w_up, v_ffn1_w_down, v_mix_norm, v_w_in, v_lru_conv_w, v_lru_conv_b, v_lru_w_a, v_lru_b_a, v_lru_w_x, v_lru_b_x, v_lru_lambda, v_attn_sinks, v_rel_bias, v_dn_conv_w, v_dn_a_log, v_dn_dt_bias, v_dn_norm, v_w_out, v_ffn2_norm, v_ffn2_w_gate, v_ffn2_w_up, v_ffn2_w_down, v_ple_norm, v_ple_w_gate, v_ple_w_proj, v_final_norm):
    given = dict(locals())
    ws = {k: given[k] for k in WEIGHTS}
    ms = {k: given["m_" + k] for k in WEIGHTS}
    vs = {k: given["v_" + k] for k in WEIGHTS}
    nb, seq, d = x.shape
    t = nb * seq
    cx, cy, cc = lax.axis_index("x"), lax.axis_index("y"), lax.axis_index("c")
    chip = 2 * cx + cy

    gathered = allgather_shards([ws[k].astype(BF16) for k in SHARDED] + [ws[k] for k in CONV_SHARDED], name="allgather_weights")
    full = dict(zip(SHARDED + CONV_SHARDED, gathered))
    for k in ("w_in", "ple_w_proj", "lru_conv_w", "dn_conv_w"):
        full[k] = _chip_cols(full[k])
    for k in ("w_out", "ple_w_gate"):
        full[k] = _chip_rows(full[k])
    full["w_in"] = jnp.pad(full["w_in"], ((0, 0), (0, 0), (0, D_IN_PAD - D_IN)))
    for k in SMALL:
        if k not in CONV_SHARDED:
            full[k] = ws[k]

    bmap = jnp.asarray(_rel_bucket_map())
    loss, gx, grads = local_step(x.reshape(t, d), p.reshape(N_LAYER, t, PLE_DIM), loss_target.reshape(t, d), full, bmap, nb, seq)

    gs = [grads[k] for k in SHARDED]
    flat = lambda a, lead: a.reshape(a.shape[:lead] + (-1, a.shape[-1]))
    theirs = exchange_layers(gs, name="rs_exchange_layers")
    c_arr = cc.astype(jnp.int32).reshape(1)
    sums = [add_sibling(flat(g, 1), flat(r, 0), c_arr, name=f"rs_add_{k}").reshape(r.shape)
            for k, g, r in zip(SHARDED, gs, theirs)]
    slots = reduce_to_shards(sums, name="rs_reduce_to_shards")
    mine = [sum_slots(flat(r, 1), name=f"rs_sum_{k}").reshape(r.shape[1:]) for k, r in zip(SHARDED, slots)]
    g_out = dict(zip(SHARDED, share_layers(mine, name="rs_share_layers")))

    small_shapes = [grads[k].shape for k in SMALL]
    g_small = dict(zip(SMALL, _unpack(allreduce_small(_pack([grads[k] for k in SMALL]), name="allreduce_small"), small_shapes)))
    for k in CONV_SHARDED:
        width = ws[k].shape[-1]
        g_small[k] = lax.dynamic_slice_in_dim(g_small[k], chip * width, width, axis=2)
    g_out.update(g_small)

    delta, new_m, new_v = {}, {}, {}
    for k in SHARDED:
        two_d = lambda a: a.reshape(-1, a.shape[-1])
        res = adamw(two_d(ws[k]), two_d(g_out[k]), two_d(ms[k]), two_d(vs[k]), name=f"adamw_{k}")
        delta[k], new_m[k], new_v[k] = (r.reshape(ws[k].shape) for r in res)
    shapes = [ws[k].shape for k in SMALL]
    res = adamw(*[_pack([src[k] for k in SMALL]) for src in (ws, g_out, ms, vs)], name="adamw_small")
    for dst, r in zip((delta, new_m, new_v), res):
        dst.update(zip(SMALL, _unpack(r, shapes)))

    total = lax.psum(loss[0, 0], ("x", "y", "c"))
    return (total, gx.reshape(nb, seq, d), *[g_out[k] for k in WEIGHTS], *[delta[k] for k in WEIGHTS],
            *[new_m[k] for k in WEIGHTS], *[new_v[k] for k in WEIGHTS])
```

```python
import functools
import math

import numpy as np
import jax
import jax.numpy as jnp
from jax import lax
from jax.experimental import pallas as pl
from jax.experimental.pallas import tpu as pltpu

F32 = jnp.float32
BF16 = jnp.bfloat16

EPS = 1e-6
D_MODEL = 1024
D_FF = 2816
N_CHIP = 4
FF_BLK = D_FF // N_CHIP
HEAD = 64
LRU_W = 256
ATT_W = 512
ATT_HEADS = 8
KV_HEADS = 2
ATT_GROUP = 4
BLOCK_Q = 128
DN_HEADS = 4
DN_CHUNK = 64
D_IN = 2312
D_IN_PAD = 2560
PLE_DIM = 256
REL_BUCKETS = 32
LRU_C = 8.0
N_LAYER = 2

ADAM_LR, ADAM_B1, ADAM_B2, ADAM_EPS, ADAM_WD, ADAM_STEP = 0.001, 0.9, 0.999, 1e-08, 0.01, 10

VMEM_LIMIT = 56 << 20
MESH = pl.DeviceIdType.MESH
SDS = jax.ShapeDtypeStruct


def _dot(a, b, ca=1, cb=0, hi=False):
    dims = (((ca,), (cb,)), ((), ()))
    one = lambda u, v: lax.dot_general(u, v, dims, preferred_element_type=F32)
    a_hi, b_hi = a.astype(BF16), b.astype(BF16)
    if not hi:
        return one(a_hi, b_hi)
    a_lo = (a - a_hi.astype(F32)).astype(BF16)
    b_lo = (b - b_hi.astype(F32)).astype(BF16)
    return one(a_hi, b_hi) + (one(a_hi, b_lo) + one(a_lo, b_hi))


def _nn(a, b, hi=False):
    return _dot(a, b, 1, 0, hi)


def _nt(a, b, hi=False):
    return _dot(a, b, 1, 1, hi)


def _tn(a, b, hi=False):
    return _dot(a, b, 0, 0, hi)


def _sigmoid(x):
    return jax.nn.sigmoid(x)


def _softplus(x):
    return jnp.maximum(x, 0.0) + jnp.log1p(jnp.exp(-jnp.abs(x)))


def _neg_expm1(z):
    series = -z * (1.0 + z * (0.5 + z * (1.0 / 6.0 + z * (1.0 / 24.0 + z * (1.0 / 120.0)))))
    return jnp.where(z > -0.05, series, 1.0 - jnp.exp(z))


_GELU_C = math.sqrt(2.0 / math.pi)


def _gelu(x):
    t = jnp.tanh(_GELU_C * (x + 0.044715 * x * x * x))
    return 0.5 * x * (1.0 + t), t


def _gelu_grad(x, t):
    return 0.5 * (1.0 + t) + 0.5 * x * (1.0 - t * t) * _GELU_C * (1.0 + 3.0 * 0.044715 * x * x)


def _rms_fwd(h, g):
    r = lax.rsqrt(jnp.mean(h * h, axis=-1, keepdims=True) + EPS)
    xh = h * r
    return xh * g, xh, r


def _rms_bwd(dn, xh, r, g):
    dxh = dn * g
    dh = r * (dxh - xh * jnp.mean(dxh * xh, axis=-1, keepdims=True))
    return dh, jnp.sum(dn * xh, axis=0, keepdims=True)


def _shift_down(x, d, fill=0.0):
    row = lax.broadcasted_iota(jnp.int32, x.shape, 0)
    return jnp.where(row >= d, pltpu.roll(x, d, 0), fill)


def _shift_up(x, d, fill=0.0):
    n = x.shape[0]
    row = lax.broadcasted_iota(jnp.int32, x.shape, 0)
    return jnp.where(row < n - d, pltpu.roll(x, n - d, 0), fill)


def _conv_fwd(x, w):
    y = x * w[3]
    for k in range(3):
        y = y + _shift_down(x, 3 - k) * w[k]
    return y


def _conv_bwd(dy, x, w):
    dx = dy * w[3]
    rows = [None] * 4
    rows[3] = jnp.sum(dy * x, axis=0, keepdims=True)
    for k in range(3):
        dx = dx + _shift_up(dy, 3 - k) * w[k]
        rows[k] = jnp.sum(dy * _shift_down(x, 3 - k), axis=0, keepdims=True)
    r4 = lax.broadcasted_iota(jnp.int32, (4, x.shape[1]), 0)
    dw = jnp.zeros((4, x.shape[1]), F32)
    for k in range(4):
        dw = jnp.where(r4 == k, rows[k], dw)
    return dx, dw


def _params(sem=None, vmem=VMEM_LIMIT):
    return pltpu.CompilerParams(dimension_semantics=sem, vmem_limit_bytes=vmem)


def _whole(shape):
    nd = len(shape)
    return pl.BlockSpec(shape, lambda *_: (0,) * nd)


def matmul(a, b, *, name, ta=False, tb=False, residual=None, out_dtype=F32, tm=512, tn=512, tk=512):
    m, k = (a.shape[1], a.shape[0]) if ta else a.shape
    n = b.shape[0] if tb else b.shape[1]
    tm, tn, tk = min(tm, m), min(tn, n), min(tk, k)
    assert m % tm == 0 and n % tn == 0 and k % tk == 0, (m, n, k, tm, tn, tk)
    nk = k // tk

    def body(*refs):
        if residual is None:
            a_ref, b_ref, o_ref, acc = refs
        else:
            a_ref, b_ref, r_ref, o_ref, acc = refs
        kk = pl.program_id(2)

        @pl.when(kk == 0)
        def _():
            acc[...] = jnp.zeros_like(acc)

        acc[...] += _dot(a_ref[...], b_ref[...], 0 if ta else 1, 1 if tb else 0)

        @pl.when(kk == nk - 1)
        def _():
            out = acc[...]
            if residual is not None:
                out = out + r_ref[...]
            o_ref[...] = out.astype(out_dtype)

    a_spec = pl.BlockSpec((tk, tm), lambda i, j, kk: (kk, i)) if ta else pl.BlockSpec((tm, tk), lambda i, j, kk: (i, kk))
    b_spec = pl.BlockSpec((tn, tk), lambda i, j, kk: (j, kk)) if tb else pl.BlockSpec((tk, tn), lambda i, j, kk: (kk, j))
    o_spec = pl.BlockSpec((tm, tn), lambda i, j, kk: (i, j))
    in_specs, args = [a_spec, b_spec], [a, b]
    if residual is not None:
        in_specs.append(o_spec)
        args.append(residual)
    return pl.pallas_call(
        body, name=name, grid=(m // tm, n // tn, nk), in_specs=in_specs, out_specs=o_spec,
        out_shape=SDS((m, n), out_dtype), scratch_shapes=[pltpu.VMEM((tm, tn), F32)],
        compiler_params=_params(("parallel", "parallel", "arbitrary")))(*args)


def norm_matmul(h, gain, w, *, name, tm=512, tn=512):
    t, d = h.shape
    tm = min(tm, t)
    n = w.shape[1]
    assert t % tm == 0 and n % tn == 0

    def body(h_ref, g_ref, w_ref, u_ref, n_ref):
        @pl.when(pl.program_id(1) == 0)
        def _():
            n_ref[...] = _rms_fwd(h_ref[...], g_ref[...])[0].astype(BF16)

        u_ref[...] = _nn(n_ref[...], w_ref[...])

    return pl.pallas_call(
        body, name=name, grid=(t // tm, n // tn),
        in_specs=[pl.BlockSpec((tm, d), lambda i, j: (i, 0)), _whole((1, d)), pl.BlockSpec((d, tn), lambda i, j: (0, j))],
        out_specs=[pl.BlockSpec((tm, tn), lambda i, j: (i, j)), pl.BlockSpec((tm, d), lambda i, j: (i, 0))],
        out_shape=[SDS((t, n), F32), SDS((t, d), BF16)],
        compiler_params=_params(("parallel", "arbitrary")))(h, gain, w)


def rms_bwd(h, gain, dn, dres, *, name, tm=512):
    t, d = h.shape
    tm = min(tm, t)

    def body(h_ref, g_ref, dn_ref, dr_ref, dh_ref, dg_ref):
        @pl.when(pl.program_id(0) == 0)
        def _():
            dg_ref[...] = jnp.zeros_like(dg_ref)

        g = g_ref[...]
        _, xh, r = _rms_fwd(h_ref[...], g)
        dh, dg = _rms_bwd(dn_ref[...], xh, r, g)
        dh_ref[...] = dr_ref[...] + dh
        dg_ref[...] += dg

    row = pl.BlockSpec((tm, d), lambda i: (i, 0))
    return pl.pallas_call(
        body, name=name, grid=(t // tm,), in_specs=[row, _whole((1, d)), row, row],
        out_specs=[row, _whole((1, d))], out_shape=[SDS((t, d), F32), SDS((1, d), F32)],
        compiler_params=_params(("arbitrary",)))(h, gain, dn, dres)


def ffn_fwd(h, gain, wg, wu, wd, *, name, tm=512):
    t, d = h.shape
    tm = min(tm, t)

    def body(h_ref, g_ref, wg_ref, wu_ref, wd_ref, o_ref, n_sc, acc):
        j = pl.program_id(1)

        @pl.when(j == 0)
        def _():
            n_sc[...] = _rms_fwd(h_ref[...], g_ref[...])[0].astype(BF16)
            acc[...] = jnp.zeros_like(acc)

        n = n_sc[...]
        a = _nn(n, wg_ref[...])
        b = _nn(n, wu_ref[...])
        acc[...] += _nn(a * _sigmoid(a) * b, wd_ref[...])

        @pl.when(j == N_CHIP - 1)
        def _():
            o_ref[...] = h_ref[...] + 0.5 * acc[...]

    row = pl.BlockSpec((tm, d), lambda i, j: (i, 0))
    return pl.pallas_call(
        body, name=name, grid=(t // tm, N_CHIP),
        in_specs=[row, _whole((1, d)),
                  pl.BlockSpec((None, d, FF_BLK), lambda i, j: (j, 0, 0)),
                  pl.BlockSpec((None, d, FF_BLK), lambda i, j: (j, 0, 0)),
                  pl.BlockSpec((None, FF_BLK, d), lambda i, j: (j, 0, 0))],
        out_specs=row, out_shape=SDS((t, d), F32),
        scratch_shapes=[pltpu.VMEM((tm, d), BF16), pltpu.VMEM((tm, d), F32)],
        compiler_params=_params(("parallel", "arbitrary")))(h, gain, wg, wu, wd)


def ffn_bwd_act(h, gain, dout, wg, wu, wd, *, name, tm=512):
    t, d = h.shape
    tm = min(tm, t)

    def body(h_ref, g_ref, do_ref, wg_ref, wu_ref, wd_ref, dh_ref, n_ref, da_ref, db_ref, s_ref, dg_ref, dn_acc):
        i, j = pl.program_id(0), pl.program_id(1)

        @pl.when((i == 0) & (j == 0))
        def _():
            dg_ref[...] = jnp.zeros_like(dg_ref)

        @pl.when(j == 0)
        def _():
            n_ref[...] = _rms_fwd(h_ref[...], g_ref[...])[0].astype(BF16)
            dn_acc[...] = jnp.zeros_like(dn_acc)

        n = n_ref[...]
        a = _nn(n, wg_ref[...])
        b = _nn(n, wu_ref[...])
        sig = _sigmoid(a)
        sa = a * sig
        ds = _nt(0.5 * do_ref[...], wd_ref[...])
        db = ds * sa
        da = ds * b * (sig * (1.0 + a * (1.0 - sig)))
        s_ref[...] = (sa * b).astype(BF16)
        da_ref[...] = da.astype(BF16)
        db_ref[...] = db.astype(BF16)
        dn_acc[...] += _nt(da, wg_ref[...]) + _nt(db, wu_ref[...])

        @pl.when(j == N_CHIP - 1)
        def _():
            g = g_ref[...]
            _, xh, r = _rms_fwd(h_ref[...], g)
            dh, dg = _rms_bwd(dn_acc[...], xh, r, g)
            dh_ref[...] = do_ref[...] + dh
            dg_ref[...] += dg

    row = pl.BlockSpec((tm, d), lambda i, j: (i, 0))
    blk = pl.BlockSpec((None, tm, FF_BLK), lambda i, j: (j, i, 0))
    act = SDS((N_CHIP, t, FF_BLK), BF16)
    return pl.pallas_call(
        body, name=name, grid=(t // tm, N_CHIP),
        in_specs=[row, _whole((1, d)), row,
                  pl.BlockSpec((None, d, FF_BLK), lambda i, j: (j, 0, 0)),
                  pl.BlockSpec((None, d, FF_BLK), lambda i, j: (j, 0, 0)),
                  pl.BlockSpec((None, FF_BLK, d), lambda i, j: (j, 0, 0))],
        out_specs=[row, row, blk, blk, blk, _whole((1, d))],
        out_shape=[SDS((t, d), F32), SDS((t, d), BF16), act, act, act, SDS((1, d), F32)],
        scratch_shapes=[pltpu.VMEM((tm, d), F32)],
        compiler_params=_params(("arbitrary", "arbitrary")))(h, gain, dout, wg, wu, wd)


def ffn_bwd_w(n, da, db, s, dout, *, name, tk=512):
    t, d = n.shape
    tk = min(tk, t)

    def body(n_ref, da_ref, db_ref, s_ref, do_ref, dwg_ref, dwu_ref, dwd_ref):
        @pl.when(pl.program_id(1) == 0)
        def _():
            dwg_ref[...] = jnp.zeros_like(dwg_ref)
            dwu_ref[...] = jnp.zeros_like(dwu_ref)
            dwd_ref[...] = jnp.zeros_like(dwd_ref)

        nn = n_ref[...]
        dwg_ref[...] += _tn(nn, da_ref[...])
        dwu_ref[...] += _tn(nn, db_ref[...])
        dwd_ref[...] += _tn(s_ref[...], 0.5 * do_ref[...])

    row = pl.BlockSpec((tk, d), lambda j, kk: (kk, 0))
    blk = pl.BlockSpec((None, tk, FF_BLK), lambda j, kk: (j, kk, 0))
    return pl.pallas_call(
        body, name=name, grid=(N_CHIP, t // tk), in_specs=[row, blk, blk, blk, row],
        out_specs=[pl.BlockSpec((None, d, FF_BLK), lambda j, kk: (j, 0, 0)),
                   pl.BlockSpec((None, d, FF_BLK), lambda j, kk: (j, 0, 0)),
                   pl.BlockSpec((None, FF_BLK, d), lambda j, kk: (j, 0, 0))],
        out_shape=[SDS((N_CHIP, d, FF_BLK), F32), SDS((N_CHIP, d, FF_BLK), F32), SDS((N_CHIP, FF_BLK, d), F32)],
        compiler_params=_params(("parallel", "arbitrary")))(n, da, db, s, dout)


def ple_fwd(h, gain, wpg, pl_in, wpp, *, name, tm=512):
    t, d = h.shape
    tm = min(tm, t)
    pd = pl_in.shape[1]

    def body(h_ref, g_ref, wpg_ref, p_ref, wpp_ref, o_ref):
        hh = h_ref[...]
        n = _rms_fwd(hh, g_ref[...])[0]
        gate = _sigmoid(_nn(n, wpg_ref[...]))
        o_ref[...] = hh + gate * _nn(p_ref[...], wpp_ref[...])

    row = pl.BlockSpec((tm, d), lambda i: (i, 0))
    return pl.pallas_call(
        body, name=name, grid=(t // tm,),
        in_specs=[row, _whole((1, d)), _whole((d, d)), pl.BlockSpec((tm, pd), lambda i: (i, 0)), _whole((pd, d))],
        out_specs=row, out_shape=SDS((t, d), F32), compiler_params=_params(("parallel",)))(h, gain, wpg, pl_in, wpp)


def ple_bwd(h, gain, wpg, pl_in, wpp, dout, *, name, tm=512):
    t, d = h.shape
    tm = min(tm, t)
    pd = pl_in.shape[1]

    def body(h_ref, g_ref, wpg_ref, p_ref, wpp_ref, do_ref, dh_ref, n_ref, dga_ref, dpp_ref, dg_ref):
        @pl.when(pl.program_id(0) == 0)
        def _():
            dg_ref[...] = jnp.zeros_like(dg_ref)

        g = g_ref[...]
        n, xh, r = _rms_fwd(h_ref[...], g)
        gate = _sigmoid(_nn(n, wpg_ref[...]))
        pp = _nn(p_ref[...], wpp_ref[...])
        do = do_ref[...]
        dga = do * pp * gate * (1.0 - gate)
        dh, dg = _rms_bwd(_nt(dga, wpg_ref[...]), xh, r, g)
        dh_ref[...] = do + dh
        n_ref[...] = n.astype(BF16)
        dga_ref[...] = dga.astype(BF16)
        dpp_ref[...] = (do * gate).astype(BF16)
        dg_ref[...] += dg

    row = pl.BlockSpec((tm, d), lambda i: (i, 0))
    return pl.pallas_call(
        body, name=name, grid=(t // tm,),
        in_specs=[row, _whole((1, d)), _whole((d, d)), pl.BlockSpec((tm, pd), lambda i: (i, 0)), _whole((pd, d)), row],
        out_specs=[row, row, row, row, _whole((1, d))],
        out_shape=[SDS((t, d), F32), SDS((t, d), BF16), SDS((t, d), BF16), SDS((t, d), BF16), SDS((1, d), F32)],
        compiler_params=_params(("arbitrary",)))(h, gain, wpg, pl_in, wpp, dout)


def loss_head(h, gain, target, *, name, tm=512):
    t, d = h.shape
    tm = min(tm, t)

    def body(h_ref, g_ref, t_ref, dh_ref, dg_ref, l_ref):
        @pl.when(pl.program_id(0) == 0)
        def _():
            dg_ref[...] = jnp.zeros_like(dg_ref)
            l_ref[...] = jnp.zeros_like(l_ref)

        g = g_ref[...]
        y, xh, r = _rms_fwd(h_ref[...], g)
        err = y - t_ref[...]
        l_ref[...] += 0.5 * jnp.sum(jnp.mean(err * err, axis=-1, keepdims=True), axis=0, keepdims=True)
        dh, dg = _rms_bwd(err * (1.0 / d), xh, r, g)
        dh_ref[...] = dh
        dg_ref[...] += dg

    row = pl.BlockSpec((tm, d), lambda i: (i, 0))
    return pl.pallas_call(
        body, name=name, grid=(t // tm,), in_specs=[row, _whole((1, d)), row],
        out_specs=[row, _whole((1, d)), _whole((1, 1))],
        out_shape=[SDS((t, d), F32), SDS((1, d), F32), SDS((1, 1), F32)],
        compiler_params=_params(("arbitrary",)))(h, gain, target)


def adamw(w, g, m, v, *, name):
    r, c = w.shape
    tr = r
    for cand in (512, 256, 128, 64, 32, 16, 8):
        if r % cand == 0:
            tr = cand
            break

    def body(w_ref, g_ref, m_ref, v_ref, d_ref, nm_ref, nv_ref):
        gg = g_ref[...]
        mm = ADAM_B1 * m_ref[...] + (1.0 - ADAM_B1) * gg
        vv = ADAM_B2 * v_ref[...] + (1.0 - ADAM_B2) * (gg * gg)
        m_hat = mm / (1.0 - ADAM_B1 ** ADAM_STEP)
        v_hat = vv / (1.0 - ADAM_B2 ** ADAM_STEP)
        d_ref[...] = -ADAM_LR * (m_hat / (jnp.sqrt(v_hat) + ADAM_EPS) + ADAM_WD * w_ref[...])
        nm_ref[...] = mm
        nv_ref[...] = vv

    blk = pl.BlockSpec((tr, c), lambda i: (i, 0))
    out = SDS((r, c), F32)
    return pl.pallas_call(body, name=name, grid=(r // tr,), in_specs=[blk] * 4, out_specs=[blk] * 3,
                          out_shape=[out, out, out], compiler_params=_params(("parallel",)))(w, g, m, v)


def _scan_fwd(a, b):
    d = 1
    while d < a.shape[0]:
        b = a * _shift_down(b, d, 0.0) + b
        a = a * _shift_down(a, d, 1.0)
        d *= 2
    return b


def _scan_rev(a, b):
    d = 1
    while d < a.shape[0]:
        b = a * _shift_up(b, d, 0.0) + b
        a = a * _shift_up(a, d, 1.0)
        d *= 2
    return b


LRU_HALF = 128


def _lru_in_specs(seq):
    half = LRU_W // LRU_HALF
    vec = pl.BlockSpec((1, LRU_HALF), lambda j, b: (0, j))
    mat = pl.BlockSpec((LRU_HALF, LRU_HALF), lambda j, b: (j, j))
    return [pl.BlockSpec((seq, LRU_HALF), lambda j, b: (b, j)), pl.BlockSpec((seq, LRU_HALF), lambda j, b: (b, half + j)),
            pl.BlockSpec((4, LRU_HALF), lambda j, b: (0, j)), vec, mat, vec, mat, vec, vec]


def _lru_math(x_ref, gate_ref, cw_ref, cb_ref, wa_ref, ba_ref, wx_ref, bx_ref, lam_ref):
    x = x_ref[...]
    gate = gate_ref[...]
    cw =[cw_ref[k:k + 1, :] for k in range(4)]
    xr = _conv_fwd(x, cw) + cb_ref[...]
    r = _sigmoid(_nn(xr, wa_ref[...]) + ba_ref[...])
    i = _sigmoid(_nn(xr, wx_ref[...]) + bx_ref[...])
    sp = _softplus(-lam_ref[...])
    log_a = -LRU_C * r * sp
    a = jnp.exp(log_a)
    mult = jnp.sqrt(_neg_expm1(2.0 * log_a))
    gi = i * xr
    h = _scan_fwd(a, mult * gi)
    gl, tg = _gelu(gate)
    return dict(x=x, gate=gate, cw=cw, xr=xr, r=r, i=i, sp=sp, a=a, mult=mult, gi=gi, h=h, gl=gl, tg=tg)


def lru_fwd(u, cw, cb, wa, ba, wx, bx, lam, *, seq, name):
    t = u.shape[0]

    def body(x_ref, gate_ref, cw_ref, cb_ref, wa_ref, ba_ref, wx_ref, bx_ref, lam_ref, y_ref):
        f = _lru_math(x_ref, gate_ref, cw_ref, cb_ref, wa_ref, ba_ref, wx_ref, bx_ref, lam_ref)
        y_ref[...] = f["gl"] * f["h"]

    return pl.pallas_call(
        body, name=name, grid=(LRU_W // LRU_HALF, t // seq), in_specs=_lru_in_specs(seq),
        out_specs=pl.BlockSpec((seq, LRU_HALF), lambda j, b: (b, j)), out_shape=SDS((t, LRU_W), F32),
        compiler_params=_params(("parallel", "parallel")))(u, u, cw, cb, wa, ba, wx, bx, lam)


def lru_bwd(u, cw, cb, wa, ba, wx, bx, lam, dy, *, seq, name):
    t = u.shape[0]

    def body(x_ref, gate_ref, cw_ref, cb_ref, wa_ref, ba_ref, wx_ref, bx_ref, lam_ref, dy_ref,
             dx_ref, dgate_ref, dcw_ref, dwa_ref, dwx_ref, dv_ref):
        @pl.when(pl.program_id(1) == 0)
        def _():
            dcw_ref[...] = jnp.zeros_like(dcw_ref)
            dwa_ref[...] = jnp.zeros_like(dwa_ref)
            dwx_ref[...] = jnp.zeros_like(dwx_ref)
            dv_ref[...] = jnp.zeros_like(dv_ref)

        f = _lru_math(x_ref, gate_ref, cw_ref, cb_ref, wa_ref, ba_ref, wx_ref, bx_ref, lam_ref)
        dy = dy_ref[...]
        a, h, xr, r, i, mult, gi, sp = f["a"], f["h"], f["xr"], f["r"], f["i"], f["mult"], f["gi"], f["sp"]
        dgate_ref[...] = dy * h * _gelu_grad(f["gate"], f["tg"])
        lamb = _scan_rev(_shift_up(a, 1, 0.0), dy * f["gl"])
        da = lamb * _shift_down(h, 1)
        dlog_a = da * a - (lamb * gi) * (a * a) / mult
        dgi = lamb * mult
        dra = dlog_a * (-LRU_C * sp) * r * (1.0 - r)
        dia = dgi * xr * i * (1.0 - i)
        dsp = jnp.sum(dlog_a * (-LRU_C * r), axis=0, keepdims=True)
        dlam = -dsp * _sigmoid(-lam_ref[...])
        dxr = dgi * i + _nt(dra, wa_ref[...]) + _nt(dia, wx_ref[...])
        dx, dcw = _conv_bwd(dxr, f["x"], f["cw"])
        dx_ref[...] = dx
        dcw_ref[...] += dcw
        dwa_ref[...] += _tn(xr, dra)
        dwx_ref[...] += _tn(xr, dia)
        rows = [jnp.sum(dxr, axis=0, keepdims=True), jnp.sum(dra, axis=0, keepdims=True),
                jnp.sum(dia, axis=0, keepdims=True), dlam]
        r8 = lax.broadcasted_iota(jnp.int32, (8, LRU_HALF), 0)
        acc = jnp.zeros((8, LRU_HALF), F32)
        for k, row in enumerate(rows):
            acc = jnp.where(r8 == k, row, acc)
        dv_ref[...] += acc

    nhalf = LRU_W // LRU_HALF
    col = pl.BlockSpec((seq, LRU_HALF), lambda j, b: (b, j))
    mat = pl.BlockSpec((None, LRU_HALF, LRU_HALF), lambda j, b: (j, 0, 0))
    return pl.pallas_call(
        body, name=name, grid=(nhalf, t // seq), in_specs=_lru_in_specs(seq) + [col],
        out_specs=[col, col, pl.BlockSpec((4, LRU_HALF), lambda j, b: (0, j)), mat, mat,
                   pl.BlockSpec((8, LRU_HALF), lambda j, b: (0, j))],
        out_shape=[SDS((t, LRU_W), F32), SDS((t, LRU_W), F32), SDS((4, LRU_W), F32),
                   SDS((nhalf, LRU_HALF, LRU_HALF), F32), SDS((nhalf, LRU_HALF, LRU_HALF), F32), SDS((8, LRU_W), F32)],
        compiler_params=_params(("arbitrary", "arbitrary")))(u, u, cw, cb, wa, ba, wx, bx, lam, dy)


NEG = -1e30


def _rel_bucket_map():
    dist = (np.arange(BLOCK_Q)[:, None] - np.arange(BLOCK_Q)[None, :]) % BLOCK_Q
    max_exact = REL_BUCKETS // 2
    large = max_exact + (np.log(np.maximum(dist, 1).astype(np.float32) / max_exact)
                         / math.log(BLOCK_Q / max_exact) * (REL_BUCKETS - max_exact)).astype(np.int32)
    large = np.minimum(large, REL_BUCKETS - 1)
    return np.where(dist < max_exact, dist, large).astype(np.int32)


def relbias_fwd(rel_bias, bmap, *, name):
    def body(rb_ref, bm_ref, o_ref):
        bm = bm_ref[...]
        for h in range(ATT_HEADS):
            acc = jnp.zeros((BLOCK_Q, BLOCK_Q), F32)
            for b in range(REL_BUCKETS):
                acc = jnp.where(bm == b, rb_ref[b, h], acc)
            o_ref[h] = acc

    return pl.pallas_call(
        body, name=name, in_specs=[pl.BlockSpec(memory_space=pltpu.SMEM), pl.BlockSpec(memory_space=pltpu.VMEM)],
        out_specs=pl.BlockSpec(memory_space=pltpu.VMEM), out_shape=SDS((ATT_HEADS, BLOCK_Q, BLOCK_Q), F32))(rel_bias, bmap)


def relbias_bwd(dbias, bmap, *, name):
    def body(db_ref, bm_ref, o_ref):
        bm = bm_ref[...]
        row = lax.broadcasted_iota(jnp.int32, (REL_BUCKETS, 128), 0)
        col = lax.broadcasted_iota(jnp.int32, (REL_BUCKETS, 128), 1)
        acc = jnp.zeros((REL_BUCKETS, 128), F32)
        for h in range(ATT_HEADS):
            d = db_ref[h]
            for b in range(REL_BUCKETS):
                s = jnp.sum(jnp.sum(jnp.where(bm == b, d, 0.0), axis=1, keepdims=True), axis=0, keepdims=True)
                acc = jnp.where((row == b) & (col == h), s, acc)
        o_ref[...] = acc

    return pl.pallas_call(body, name=name, out_shape=SDS((REL_BUCKETS, 128), F32))(dbias, bmap)


def _attn_probs(q_ref, k_ref, v_ref, b_ref, s_ref, n):
    rows = ATT_GROUP * BLOCK_Q
    qs = q_ref[...].reshape(rows, HEAD) * (HEAD ** -0.5)
    prev = pl.multiple_of(jnp.maximum(n - 1, 0) * BLOCK_Q, BLOCK_Q)
    cur = pl.multiple_of(n * BLOCK_Q, BLOCK_Q)
    kp, kc = k_ref[pl.ds(prev, BLOCK_Q), :], k_ref[pl.ds(cur, BLOCK_Q), :]
    vp, vc = v_ref[pl.ds(prev, BLOCK_Q), :], v_ref[pl.ds(cur, BLOCK_Q), :]
    bias = b_ref[...].reshape(rows, BLOCK_Q)
    i = lax.broadcasted_iota(jnp.int32, (rows, BLOCK_Q), 0) & (BLOCK_Q - 1)
    j = lax.broadcasted_iota(jnp.int32, (rows, BLOCK_Q), 1)
    s_p = jnp.where((j > i) & (n > 0), _nt(qs, kp) + bias, NEG)
    s_c = jnp.where(j <= i, _nt(qs, kc) + bias, NEG)
    sink = s_ref[...]
    m = jnp.maximum(jnp.maximum(jnp.max(s_p, axis=-1, keepdims=True), jnp.max(s_c, axis=-1, keepdims=True)), sink)
    e_p, e_c, e_s = jnp.exp(s_p - m), jnp.exp(s_c - m), jnp.exp(sink - m)
    inv = 1.0 / (jnp.sum(e_p, axis=-1, keepdims=True) + jnp.sum(e_c, axis=-1, keepdims=True) + e_s)
    return e_p * inv, e_c * inv, e_s * inv, qs, kp, kc, vp, vc, prev, cur


def _attn_specs(seq):
    qspec = pl.BlockSpec((None, ATT_GROUP, BLOCK_Q, HEAD), lambda g, b, n: (b, g, n, 0))
    kvspec = pl.BlockSpec((None, None, seq, HEAD), lambda g, b, n: (b, g, 0, 0))
    bspec = pl.BlockSpec((ATT_GROUP, BLOCK_Q, BLOCK_Q), lambda g, b, n: (g, 0, 0))
    sspec = pl.BlockSpec((ATT_GROUP * BLOCK_Q, 1), lambda g, b, n: (g, 0))
    return qspec, kvspec, bspec, sspec


def attn_fwd(q, k, v, bias, sink_rows, *, name):
    nb, _, seq, _ = q.shape

    def body(q_ref, k_ref, v_ref, b_ref, s_ref, o_ref):
        p_p, p_c, _, _, _, _, vp, vc, _, _ = _attn_probs(q_ref, k_ref, v_ref, b_ref, s_ref, pl.program_id(2))
        o_ref[...] = (_nn(p_p, vp) + _nn(p_c, vc)).reshape(ATT_GROUP, BLOCK_Q, HEAD)

    qspec, kvspec, bspec, sspec = _attn_specs(seq)
    return pl.pallas_call(
        body, name=name, grid=(KV_HEADS, nb, seq // BLOCK_Q), in_specs=[qspec, kvspec, kvspec, bspec, sspec],
        out_specs=qspec, out_shape=SDS(q.shape, F32),
        compiler_params=_params(("parallel", "parallel", "arbitrary")))(q, k, v, bias, sink_rows)


def attn_bwd(q, k, v, bias, sink_rows, do, *, name):
    nb, _, seq, _ = q.shape

    def body(q_ref, k_ref, v_ref, b_ref, s_ref, do_ref, dq_ref, dk_ref, dv_ref, db_ref, ds_ref):
        b, n = pl.program_id(1), pl.program_id(2)

        @pl.when((b == 0) & (n == 0))
        def _():
            db_ref[...] = jnp.zeros_like(db_ref)
            ds_ref[...] = jnp.zeros_like(ds_ref)

        @pl.when(n == 0)
        def _():
            dk_ref[...] = jnp.zeros_like(dk_ref)
            dv_ref[...] = jnp.zeros_like(dv_ref)

        p_p, p_c, p_s, qs, kp, kc, vp, vc, prev, cur = _attn_probs(q_ref, k_ref, v_ref, b_ref, s_ref, n)
        do = do_ref[...].reshape(ATT_GROUP * BLOCK_Q, HEAD)
        dp_p, dp_c = _nt(do, vp), _nt(do, vc)
        delta = jnp.sum(p_p * dp_p, axis=-1, keepdims=True) + jnp.sum(p_c * dp_c, axis=-1, keepdims=True)
        ds_p, ds_c = p_p * (dp_p - delta), p_c * (dp_c - delta)
        dq_ref[...] = ((_nn(ds_p, kp) + _nn(ds_c, kc)) * (HEAD ** -0.5)).reshape(ATT_GROUP, BLOCK_Q, HEAD)
        dk_ref[pl.ds(prev, BLOCK_Q), :] += _tn(ds_p, qs)
        dk_ref[pl.ds(cur, BLOCK_Q), :] += _tn(ds_c, qs)
        dv_ref[pl.ds(prev, BLOCK_Q), :] += _tn(p_p, do)
        dv_ref[pl.ds(cur, BLOCK_Q), :] += _tn(p_c, do)
        db_ref[...] += (ds_p + ds_c).reshape(ATT_GROUP, BLOCK_Q, BLOCK_Q)
        ds_ref[...] += -p_s * delta

    qspec, kvspec, bspec, sspec = _attn_specs(seq)
    return pl.pallas_call(
        body, name=name, grid=(KV_HEADS, nb, seq // BLOCK_Q), in_specs=[qspec, kvspec, kvspec, bspec, sspec, qspec],
        out_specs=[qspec, kvspec, kvspec, bspec, sspec],
        out_shape=[SDS(q.shape, F32), SDS(k.shape, F32), SDS(v.shape, F32),
                   SDS((ATT_HEADS, BLOCK_Q, BLOCK_Q), F32), SDS((ATT_HEADS * BLOCK_Q, 1), F32)],
        compiler_params=_params(("arbitrary", "arbitrary", "arbitrary")))(q, k, v, bias, sink_rows, do)


def _iota2(shape, axis):
    return lax.broadcasted_iota(jnp.int32, shape, axis)


def _col_to_row(col):
    c = col.shape[0]
    eye = _iota2((c, c), 0) == _iota2((c, c), 1)
    return jnp.sum(jnp.where(eye, jnp.broadcast_to(col, (c, c)), 0.0), axis=0, keepdims=True)


def _row_to_col(row):
    c = row.shape[1]
    eye = _iota2((c, c), 0) == _iota2((c, c), 1)
    return jnp.sum(jnp.where(eye, jnp.broadcast_to(row, (c, c)), 0.0), axis=1, keepdims=True)


def _last_row(col):
    c = col.shape[0]
    return jnp.sum(jnp.where(_iota2((c, 1), 0) == c - 1, col, 0.0), axis=0, keepdims=True)


def _chunk_cumsum(x):
    pos = _iota2(x.shape, 0) & (DN_CHUNK - 1)
    d = 1
    while d < DN_CHUNK:
        x = x + jnp.where(pos >= d, pltpu.roll(x, d, 0), 0.0)
        d *= 2
    return x


def _chunk_rev_cumsum(x):
    n = x.shape[0]
    pos = _iota2(x.shape, 0) & (DN_CHUNK - 1)
    d = 1
    while d < DN_CHUNK:
        x = x + jnp.where(pos < DN_CHUNK - d, pltpu.roll(x, n - d, 0), 0.0)
        d *= 2
    return x


def _tri_inv(low):
    c = low.shape[0]
    eye = (_iota2((c, c), 0) == _iota2((c, c), 1)).astype(F32)
    m = -low
    p = eye + m
    steps = int(math.log2(c)) - 1
    for _ in range(steps):
        m = _nn(m, m, hi=True)
        p = p + _nn(p, m, hi=True)
    return p


_DN_SCALE = (HEAD ** -0.5, 1.0, None)


def _dn_act(c, scale):
    sig = _sigmoid(c)
    a = c * sig
    if scale is None:
        return a, sig, None, None
    r = lax.rsqrt(jnp.sum(a * a, axis=-1, keepdims=True) + EPS)
    return a * r * scale, sig, a * r, r


def _dn_gates(ba_ref, hs_ref):
    beta = _sigmoid(ba_ref[0])
    sp_arg = ba_ref[1] + hs_ref[1]
    a_exp = jnp.exp(hs_ref[0])
    g = -a_exp * _softplus(sp_arg)
    return beta, g, sp_arg, a_exp


def _dn_inputs(pre_ref, cw_ref, ba_ref, hs_ref, act_sc, b_sc, gc_sc, c_sc=None):
    for idx in range(3):
        c = _conv_fwd(pre_ref[idx], [cw_ref[idx, k:k + 1, :] for k in range(4)])
        if c_sc is not None:
            c_sc[idx] = c
        act_sc[idx] = _dn_act(c, _DN_SCALE[idx])[0]
    beta, g, _, _ = _dn_gates(ba_ref, hs_ref)
    b_sc[...] = beta
    gc_sc[...] = _chunk_cumsum(g)


def _dn_chunk_math(q, k, v, b, gcc):
    c = q.shape[0]
    tril = _iota2((c, c), 0) >= _iota2((c, c), 1)
    strict = _iota2((c, c), 0) > _iota2((c, c), 1)
    eg = jnp.exp(gcc)
    kb, vb = k * b, v * b
    kbg = kb * eg
    dm = jnp.exp(jnp.where(tril, jnp.broadcast_to(gcc, (c, c)) - _col_to_row(gcc), NEG))
    kk = _nt(kb, k)
    t = _tri_inv(jnp.where(strict, kk * dm, 0.0))
    glast = _last_row(gcc)
    ekd = jnp.exp(glast - gcc)
    qk = _nt(q, k)
    return dict(tril=tril, strict=strict, eg=eg, kb=kb, vb=vb, kbg=kbg, dm=dm, kk=kk, t=t, glast=glast, ekd=ekd,
                kd=k * ekd, qk=qk, amat=jnp.where(tril, qk * dm, 0.0), qg=q * eg,
                egl=jnp.broadcast_to(jnp.exp(glast), (c, 1)))


DN_UNROLL = 4


def _chunk_loop(nc, chunk):
    u = math.gcd(nc, DN_UNROLL)

    def step(i, carry):
        for j in range(u):
            chunk(i * u + j)
        return carry

    lax.fori_loop(0, nc // u, step, 0)


def _dn_specs(seq):
    s64 = lambda lead: pl.BlockSpec((lead, None, None, seq, HEAD), lambda b, h: (0, b, h, 0, 0))
    s1 = lambda lead: pl.BlockSpec((lead, None, None, seq, 1), lambda b, h: (0, b, h, 0, 0))
    one64 = pl.BlockSpec((None, None, seq, HEAD), lambda b, h: (b, h, 0, 0))
    one1 = pl.BlockSpec((None, None, seq, 1), lambda b, h: (b, h, 0, 0))
    cw = pl.BlockSpec((None, 3, 4, HEAD), lambda b, h: (h, 0, 0, 0))
    hs = pl.BlockSpec((None, 2, 1, 1), lambda b, h: (h, 0, 0, 0))
    return s64, s1, one64, one1, cw, hs


def dn_prep(pre, cw, ba, hs, *, name):
    _, nb, nh, seq, _ = pre.shape
    nc = seq // DN_CHUNK

    def body(pre_ref, cw_ref, ba_ref, hs_ref, loc_ref, egl_ref, act_sc, b_sc, gc_sc):
        _dn_inputs(pre_ref, cw_ref, ba_ref, hs_ref, act_sc, b_sc, gc_sc)

        def chunk(c):
            rows = pl.ds(pl.multiple_of(c * DN_CHUNK, DN_CHUNK), DN_CHUNK)
            m = _dn_chunk_math(act_sc[0, rows, :], act_sc[1, rows, :], act_sc[2, rows, :], b_sc[rows, :], gc_sc[rows, :])
            loc_ref[0, rows, :] = m["qg"]
            loc_ref[1, rows, :] = m["kd"]
            loc_ref[2, rows, :] = _nn(m["t"], m["vb"])
            loc_ref[3, rows, :] = _nn(m["t"], m["kbg"])
            loc_ref[4, rows, :] = m["amat"]
            egl_ref[rows, :] = m["egl"]

        _chunk_loop(nc, chunk)

    s64, s1, one64, one1, cwspec, hsspec = _dn_specs(seq)
    return pl.pallas_call(
        body, name=name, grid=(nb, nh), in_specs=[s64(3), cwspec, s1(2), hsspec], out_specs=[s64(5), one1],
        out_shape=[SDS((5, nb, nh, seq, HEAD), F32), SDS((nb, nh, seq, 1), F32)],
        scratch_shapes=[pltpu.VMEM((3, seq, HEAD), F32)] + [pltpu.VMEM((seq, 1), F32)] * 2,
        compiler_params=_params(("parallel", "parallel")))(pre, cw, ba, hs)


def _gated_norm(o, z, gn):
    r = lax.rsqrt(jnp.mean(o * o, axis=-1, keepdims=True) + EPS)
    sig = _sigmoid(z)
    return o * r, sig, r


def dn_scan(loc, egl, z, gn, *, name):
    _, nb, nh, seq, _ = loc.shape
    nc = seq // DN_CHUNK

    def body(loc_ref, egl_ref, z_ref, gn_ref, y_ref, o_ref, vn_ref, st_ref):
        gn = gn_ref[...]

        def step(c, state):
            rows = pl.ds(pl.multiple_of(c * DN_CHUNK, DN_CHUNK), DN_CHUNK)
            st_ref[rows, :] = state
            vn = loc_ref[2, rows, :] - _nn(loc_ref[3, rows, :], state)
            o = _nn(loc_ref[0, rows, :], state) + _nn(loc_ref[4, rows, :], vn)
            vn_ref[rows, :] = vn
            o_ref[rows, :] = o
            zz = z_ref[rows, :]
            on, sig, _ = _gated_norm(o, zz, gn)
            y_ref[rows, :] = on * gn * (zz * sig)
            return state * egl_ref[rows, :] + _tn(loc_ref[1, rows, :], vn)

        lax.fori_loop(0, nc, step, jnp.zeros((HEAD, HEAD), F32))

    s64, s1, one64, one1, cwspec, hsspec = _dn_specs(seq)
    out = SDS((nb, nh, seq, HEAD), F32)
    return pl.pallas_call(
        body, name=name, grid=(nb, nh), in_specs=[s64(5), one1, one64, _whole((1, HEAD))],
        out_specs=[one64] * 4, out_shape=[out] * 4,
        compiler_params=_params(("parallel", "parallel")))(loc, egl, z, gn)


def dn_scan_bwd(loc, egl, z, gn, o, vn, states, dy, *, name):
    _, nb, nh, seq, _ = loc.shape
    nc = seq // DN_CHUNK

    def body(loc_ref, egl_ref, z_ref, gn_ref, o_ref, vn_ref, st_ref, dy_ref, dloc_ref, degl_ref, dz_ref, dgn_ref):
        @pl.when((pl.program_id(0) == 0) & (pl.program_id(1) == 0))
        def _():
            dgn_ref[...] = jnp.zeros_like(dgn_ref)

        gn = gn_ref[...]
        tril = _iota2((DN_CHUNK, DN_CHUNK), 0) >= _iota2((DN_CHUNK, DN_CHUNK), 1)

        def step(i, carry):
            ds, dgn = carry
            rows = pl.ds(pl.multiple_of((nc - 1 - i) * DN_CHUNK, DN_CHUNK), DN_CHUNK)
            dy, zz, oo = dy_ref[rows, :], z_ref[rows, :], o_ref[rows, :]
            on, sig, r = _gated_norm(oo, zz, gn)
            sz = zz * sig
            dz_ref[rows, :] = dy * on * gn * (sig * (1.0 + zz * (1.0 - sig)))
            dgn = dgn + jnp.sum(dy * on * sz, axis=0, keepdims=True)
            don = dy * gn * sz
            do = r * (don - on * jnp.mean(don * on, axis=-1, keepdims=True))
            state, vnew = st_ref[rows, :], vn_ref[rows, :]
            qg, kd, w, amat = loc_ref[0, rows, :], loc_ref[1, rows, :], loc_ref[3, rows, :], loc_ref[4, rows, :]
            dvn = _tn(amat, do) + _nn(kd, ds)
            dloc_ref[0, rows, :] = _nt(do, state)
            dloc_ref[1, rows, :] = _nt(vnew, ds)
            dloc_ref[2, rows, :] = dvn
            dloc_ref[3, rows, :] = -_nt(dvn, state)
            dloc_ref[4, rows, :] = jnp.where(tril, _nt(do, vnew), 0.0)
            degl = jnp.sum(jnp.sum(state * ds, axis=1, keepdims=True), axis=0, keepdims=True)
            degl_ref[rows, :] = jnp.broadcast_to(degl, (DN_CHUNK, 1))
            return ds * egl_ref[rows, :] + _tn(qg, do) - _tn(w, dvn), dgn

        _, dgn = lax.fori_loop(0, nc, step, (jnp.zeros((HEAD, HEAD), F32), jnp.zeros((1, HEAD), F32)))
        dgn_ref[...] += dgn

    s64, s1, one64, one1, cwspec, hsspec = _dn_specs(seq)
    return pl.pallas_call(
        body, name=name, grid=(nb, nh),
        in_specs=[s64(5), one1, one64, _whole((1, HEAD)), one64, one64, one64, one64],
        out_specs=[s64(5), one1, one64, _whole((1, HEAD))],
        out_shape=[SDS((5, nb, nh, seq, HEAD), F32), SDS((nb, nh, seq, 1), F32), SDS((nb, nh, seq, HEAD), F32),
                   SDS((1, HEAD), F32)],
        compiler_params=_params(("arbitrary", "arbitrary")))(loc, egl, z, gn, o, vn, states, dy)


def dn_prep_bwd(pre, cw, ba, hs, dloc, degl, *, name):
    _, nb, nh, seq, _ = pre.shape
    nc = seq // DN_CHUNK

    def body(pre_ref, cw_ref, ba_ref, hs_ref, dloc_ref, degl_ref, dpre_ref, dba_ref, dcw_ref, dhs_ref,
             act_sc, b_sc, gc_sc, c_sc):
        @pl.when(pl.program_id(1) == 0)
        def _():
            dcw_ref[...] = jnp.zeros_like(dcw_ref)
            dhs_ref[...] = jnp.zeros_like(dhs_ref)

        _dn_inputs(pre_ref, cw_ref, ba_ref, hs_ref, act_sc, b_sc, gc_sc, c_sc)

        def chunk(c):
            rows = pl.ds(pl.multiple_of(c * DN_CHUNK, DN_CHUNK), DN_CHUNK)
            q, k, v, b, gcc = act_sc[0, rows, :], act_sc[1, rows, :], act_sc[2, rows, :], b_sc[rows, :], gc_sc[rows, :]
            m = _dn_chunk_math(q, k, v, b, gcc)
            dqg, dkd, du, dw, da = (dloc_ref[x, rows, :] for x in range(5))
            t, dm, eg = m["t"], m["dm"], m["eg"]
            dt = _nt(du, m["vb"]) + _nt(dw, m["kbg"])
            dvb, dkbg = _tn(t, du), _tn(t, dw)
            dl = jnp.where(m["strict"], -_tn(t, _nt(dt, t, hi=True), hi=True), 0.0)
            dkk = dl * dm
            dqk = da * dm
            dd = dl * m["kk"] + da * m["qk"]
            dkb = _nn(dkk, k) + dkbg * eg
            dq = _nn(dqk, k) + dqg * eg
            dk = _tn(dkk, m["kb"]) + _tn(dqk, q) + dkd * m["ekd"] + dkb * b
            db = jnp.sum(dkb * k, axis=-1, keepdims=True) + jnp.sum(dvb * v, axis=-1, keepdims=True)
            mx = jnp.where(m["tril"], dd * dm, 0.0)
            tk = jnp.sum(dkd * m["kd"], axis=-1, keepdims=True)
            dgc = (jnp.sum(mx, axis=-1, keepdims=True) - _row_to_col(jnp.sum(mx, axis=0, keepdims=True))
                   + jnp.sum(dqg * m["qg"], axis=-1, keepdims=True) + jnp.sum(dkbg * m["kbg"], axis=-1, keepdims=True) - tk)
            dglast = jnp.sum(tk, axis=0, keepdims=True) + _last_row(degl_ref[rows, :]) * jnp.exp(m["glast"])
            act_sc[0, rows, :] = dq
            act_sc[1, rows, :] = dk
            act_sc[2, rows, :] = dvb * b
            b_sc[rows, :] = db
            gc_sc[rows, :] = dgc + jnp.where(_iota2((DN_CHUNK, 1), 0) == DN_CHUNK - 1, dglast, 0.0)

        _chunk_loop(nc, chunk)

        beta, g, sp_arg, a_exp = _dn_gates(ba_ref, hs_ref)
        dg = _chunk_rev_cumsum(gc_sc[...])
        dal = dg * (-a_exp) * _sigmoid(sp_arg)
        dba_ref[0] = b_sc[...] * beta * (1.0 - beta)
        dba_ref[1] = dal
        dhs_ref[0] += jnp.sum(dg * g, axis=0, keepdims=True)
        dhs_ref[1] += jnp.sum(dal, axis=0, keepdims=True)
        for idx in range(3):
            c = c_sc[idx]
            _, sig, hat, r = _dn_act(c, _DN_SCALE[idx])
            da_ = act_sc[idx]
            if _DN_SCALE[idx] is not None:
                da_ = da_ * _DN_SCALE[idx]
                da_ = r * (da_ - hat * jnp.sum(da_ * hat, axis=-1, keepdims=True))
            dx, dcw = _conv_bwd(da_ * (sig * (1.0 + c * (1.0 - sig))), pre_ref[idx],
                                [cw_ref[idx, k:k + 1, :] for k in range(4)])
            dpre_ref[idx] = dx
            dcw_ref[idx] += dcw

    s64, s1, one64, one1, cwspec, hsspec = _dn_specs(seq)
    swap = lambda spec: pl.BlockSpec(spec.block_shape, lambda h, b, _f=spec.index_map: _f(b, h))
    return pl.pallas_call(
        body, name=name, grid=(nh, nb),
        in_specs=[swap(s64(3)), swap(cwspec), swap(s1(2)), swap(hsspec), swap(s64(5)), swap(one1)],
        out_specs=[swap(s64(3)), swap(s1(2)), swap(cwspec), swap(hsspec)],
        out_shape=[SDS((3, nb, nh, seq, HEAD), F32), SDS((2, nb, nh, seq, 1), F32), SDS((nh, 3, 4, HEAD), F32),
                   SDS((nh, 2, 1, 1), F32)],
        scratch_shapes=[pltpu.VMEM((3, seq, HEAD), F32)] + [pltpu.VMEM((seq, 1), F32)] * 2 + [pltpu.VMEM((3, seq, HEAD), F32)],
        compiler_params=_params(("arbitrary", "arbitrary")))(pre, cw, ba, hs, dloc, degl)


def _block_diag(w):
    out = jnp.zeros((LRU_W, LRU_W), w.dtype)
    for h in range(LRU_W // HEAD):
        out = lax.dynamic_update_slice(out, w[h], (h * HEAD, h * HEAD))
    return out


def _diag_blocks(w):
    per = LRU_HALF // HEAD
    return jnp.stack([w[h // per, (h % per) * HEAD:(h % per + 1) * HEAD, (h % per) * HEAD:(h % per + 1) * HEAD]
                      for h in range(LRU_W // HEAD)])


def layer_params(w, l, bias):
    row = lambda a: a[l].reshape(1, -1)
    return dict(
        ffn1_norm=row(w["ffn1_norm"]), ffn1=(w["ffn1_w_gate"][:, l], w["ffn1_w_up"][:, l], w["ffn1_w_down"][:, l]),
        mix_norm=row(w["mix_norm"]), w_in=w["w_in"][l],
        lru=(w["lru_conv_w"][l], row(w["lru_conv_b"]), _block_diag(w["lru_w_a"][l]), row(w["lru_b_a"]),
             _block_diag(w["lru_w_x"][l]), row(w["lru_b_x"]), row(w["lru_lambda"])),
        bias=bias, sink_rows=jnp.repeat(w["attn_sinks"][l], BLOCK_Q).reshape(ATT_HEADS * BLOCK_Q, 1),
        dn_cw=w["dn_conv_w"][l].reshape(4, 3, DN_HEADS, HEAD).transpose(2, 1, 0, 3),
        dn_hs=jnp.stack([w["dn_a_log"][l], w["dn_dt_bias"][l]], axis=1).reshape(DN_HEADS, 2, 1, 1),
        dn_norm=row(w["dn_norm"]), w_out=w["w_out"][l],
        ffn2_norm=row(w["ffn2_norm"]), ffn2=(w["ffn2_w_gate"][:, l], w["ffn2_w_up"][:, l], w["ffn2_w_down"][:, l]),
        ple_norm=row(w["ple_norm"]), ple_w_gate=w["ple_w_gate"][l], ple_w_proj=w["ple_w_proj"][l])


def _to_heads(a, nb, seq, nh):
    return a.reshape(nb, seq, nh, HEAD).transpose(0, 2, 1, 3)


def _from_heads(a):
    nb, nh, seq, _ = a.shape
    return a.transpose(0, 2, 1, 3).reshape(nb * seq, nh * HEAD)


def mixer_fwd(h, p, nb, seq, tag):
    u, n = norm_matmul(h, p["mix_norm"], p["w_in"], name=f"mix_in_{tag}")
    y_lru = lru_fwd(u, *p["lru"], seq=seq, name=f"lru_fwd_{tag}")
    q = _to_heads(u[:, 512:1024], nb, seq, ATT_HEADS)
    k = _to_heads(u[:, 1024:1152], nb, seq, KV_HEADS)
    v = _to_heads(u[:, 1152:1280], nb, seq, KV_HEADS)
    o = attn_fwd(q, k, v, p["bias"], p["sink_rows"], name=f"attn_fwd_{tag}")
    dn4 = u[:, 1280:2304].reshape(nb, seq, 4, DN_HEADS, HEAD).transpose(2, 0, 3, 1, 4)
    pre, z = dn4[:3], dn4[3]
    ba = u[:, 2304:2312].reshape(nb, seq, 2, DN_HEADS).transpose(2, 0, 3, 1)[..., None]
    loc, egl = dn_prep(pre, p["dn_cw"], ba, p["dn_hs"], name=f"dn_prep_{tag}")
    y_dn, o_raw, vn, st = dn_scan(loc, egl, z, p["dn_norm"], name=f"dn_scan_{tag}")
    ycat = jnp.concatenate([y_lru, _from_heads(o), _from_heads(y_dn)], axis=-1)
    out = matmul(ycat, p["w_out"], residual=h, name=f"mix_out_{tag}")
    return out, dict(h=h, u=u, n=n, q=q, k=k, v=v, pre=pre, z=z, ba=ba, loc=loc, egl=egl, o_raw=o_raw, vn=vn, st=st,
                     ycat=ycat)


def mixer_bwd(dout, s, p, nb, seq, tag):
    dycat = matmul(dout, p["w_out"], tb=True, name=f"mix_out_dx_{tag}")
    g = {"w_out": matmul(s["ycat"], dout, ta=True, name=f"mix_out_dw_{tag}")}
    dx_lru, dgate_lru, dcw, dwa, dwx, dvec = lru_bwd(s["u"], *p["lru"], dycat[:, :LRU_W], seq=seq, name=f"lru_bwd_{tag}")
    g.update(lru_conv_w=dcw, lru_conv_b=dvec[0], lru_w_a=_diag_blocks(dwa), lru_b_a=dvec[1], lru_w_x=_diag_blocks(dwx),
             lru_b_x=dvec[2], lru_lambda=dvec[3])
    do = _to_heads(dycat[:, LRU_W:LRU_W + ATT_W], nb, seq, ATT_HEADS)
    dq, dk, dv, dbias, dsink = attn_bwd(s["q"], s["k"], s["v"], p["bias"], p["sink_rows"], do, name=f"attn_bwd_{tag}")
    g.update(attn_sinks=dsink.reshape(ATT_HEADS, BLOCK_Q).sum(axis=1), bias=dbias)
    dy_dn = _to_heads(dycat[:, LRU_W + ATT_W:], nb, seq, DN_HEADS)
    dloc, degl, dz, dgn = dn_scan_bwd(s["loc"], s["egl"], s["z"], p["dn_norm"], s["o_raw"], s["vn"], s["st"], dy_dn,
                                      name=f"dn_scan_bwd_{tag}")
    dpre, dba, dcwh, dhs = dn_prep_bwd(s["pre"], p["dn_cw"], s["ba"], p["dn_hs"], dloc, degl, name=f"dn_prep_bwd_{tag}")
    g.update(dn_conv_w=dcwh.transpose(2, 1, 0, 3).reshape(4, 3 * DN_HEADS * HEAD), dn_a_log=dhs[:, 0, 0, 0],
             dn_dt_bias=dhs[:, 1, 0, 0], dn_norm=dgn[0])
    du_dn = jnp.concatenate([dpre, dz[None]], axis=0).transpose(1, 3, 0, 2, 4).reshape(nb * seq, 4 * DN_HEADS * HEAD)
    du_ba = dba[..., 0].transpose(1, 3, 0, 2).reshape(nb * seq, 2 * DN_HEADS)
    du = jnp.concatenate([dx_lru, dgate_lru, _from_heads(dq), _from_heads(dk), _from_heads(dv), du_dn, du_ba,
                          jnp.zeros((nb * seq, D_IN_PAD - D_IN), F32)], axis=-1)
    dn = matmul(du, p["w_in"], tb=True, name=f"mix_in_dx_{tag}")
    g["w_in"] = matmul(s["n"], du, ta=True, name=f"mix_in_dw_{tag}")
    dh, dgain = rms_bwd(s["h"], p["mix_norm"], dn, dout, name=f"mix_norm_bwd_{tag}")
    g["mix_norm"] = dgain[0]
    return dh, g


SHARDED = ("ffn1_w_gate", "ffn1_w_up", "ffn1_w_down", "w_in", "w_out", "ffn2_w_gate", "ffn2_w_up", "ffn2_w_down",
           "ple_w_gate", "ple_w_proj")
PER_LAYER_SMALL = ("ffn1_norm", "mix_norm", "lru_conv_w", "lru_conv_b", "lru_w_a", "lru_b_a", "lru_w_x", "lru_b_x",
                   "lru_lambda", "attn_sinks", "dn_conv_w", "dn_a_log", "dn_dt_bias", "dn_norm", "ffn2_norm", "ple_norm")


def _col_shards(a):
    r, c = a.shape
    return a.reshape(r, N_CHIP, c // N_CHIP).transpose(1, 0, 2)


def local_step(x, p, target, w, bmap, nb, seq):
    bias = relbias_fwd(w["rel_bias"], bmap, name="relbias_fwd")
    h, saved = x, []
    for l in range(N_LAYER):
        pr = layer_params(w, l, bias)
        s = dict(h0=h)
        h = ffn_fwd(h, pr["ffn1_norm"], *pr["ffn1"], name=f"ffn1_fwd_{l}")
        h, s["mix"] = mixer_fwd(h, pr, nb, seq, l)
        s["h2"] = h
        h = ffn_fwd(h, pr["ffn2_norm"], *pr["ffn2"], name=f"ffn2_fwd_{l}")
        s["h3"] = h
        h = ple_fwd(h, pr["ple_norm"], pr["ple_w_gate"], p[l], pr["ple_w_proj"], name=f"ple_fwd_{l}")
        saved.append((pr, s))
    dh, dgf, loss = loss_head(h, w["final_norm"].reshape(1, -1), target, name="loss_head")

    per_layer, dbias = [None] * N_LAYER, None
    for l in reversed(range(N_LAYER)):
        pr, s = saved[l]
        g = {}
        dout = dh
        dh, n, dga, dpp, dg = ple_bwd(s["h3"], pr["ple_norm"], pr["ple_w_gate"], p[l], pr["ple_w_proj"], dout, name=f"ple_bwd_{l}")
        g["ple_norm"] = dg[0]
        g["ple_w_gate"] = matmul(n, dga, ta=True, name=f"ple_dwg_{l}").reshape(N_CHIP, -1, D_MODEL)
        g["ple_w_proj"] = _col_shards(matmul(p[l], dpp, ta=True, name=f"ple_dwp_{l}"))
        for nm, hin in (("ffn2", s["h2"]), ("ffn1", s["h0"])):
            if nm == "ffn1":
                dh, gm = mixer_bwd(dh, s["mix"], pr, nb, seq, l)
                dbias = gm.pop("bias") if dbias is None else dbias + gm.pop("bias")
                gm["w_in"] = _col_shards(gm["w_in"][:, :D_IN])
                gm["w_out"] = gm["w_out"].reshape(N_CHIP, -1, D_MODEL)
                g.update(gm)
            dout = dh
            dh, n, da, db, sact, dg = ffn_bwd_act(hin, pr[nm + "_norm"], dout, *pr[nm], name=f"{nm}_bwd_act_{l}")
            g[nm + "_norm"] = dg[0]
            g[nm + "_w_gate"], g[nm + "_w_up"], g[nm + "_w_down"] = ffn_bwd_w(n, da, db, sact, dout, name=f"{nm}_bwd_w_{l}")
        per_layer[l] = g
    grads = {k: jnp.stack([per_layer[l][k] for l in range(N_LAYER)]) for k in SHARDED + PER_LAYER_SMALL}
    grads["rel_bias"] = relbias_bwd(dbias, bmap, name="relbias_bwd")[:, :ATT_HEADS]
    grads["final_norm"] = dgf[0]
    return loss, dh, grads


HBM_SPEC = pl.BlockSpec(memory_space=pltpu.HBM)


def _place():
    x, y, c = lax.axis_index("x"), lax.axis_index("y"), lax.axis_index("c")
    chips = [(1 - x, y), (x, 1 - y), (1 - x, 1 - y)]
    return x, y, c, 2 * x + y, (x, y, 1 - c), chips, [2 * cx + cy for cx, cy in chips]


def _remote(src, dst, send_sem, recv_sem, to):
    return pltpu.make_async_remote_copy(src_ref=src, dst_ref=dst, send_sem=send_sem, recv_sem=recv_sem, device_id=to,
                                        device_id_type=MESH)


def place_shard(w, chip_arr, dtype, *, name):
    nl, r, c = w.shape
    tr = next(cand for cand in (256, 128, 64, 32, 16, 8, r) if r % cand == 0)

    def body(chip_ref, w_ref, o_ref):
        o_ref[...] = w_ref[...].astype(dtype)

    return pl.pallas_call(
        body, name=name,
        grid_spec=pltpu.PrefetchScalarGridSpec(
            num_scalar_prefetch=1, grid=(nl, r // tr),
            in_specs=[pl.BlockSpec((None, tr, c), lambda l, i, chip: (l, i, 0))],
            out_specs=pl.BlockSpec((None, None, tr, c), lambda l, i, chip: (chip[0], l, i, 0))),
        out_shape=SDS((N_CHIP, nl, r, c), dtype), compiler_params=_params(("parallel", "parallel")))(chip_arr, w)


def allgather_shards(shards, *, name):
    n = len(shards)

    def body(*refs):
        outs = refs[n:2 * n]
        send, recv, fsend, frecv = refs[2 * n:]
        x, y, c, me, sib, chips, cids = _place()
        first, passed = [], []
        for k in range(n):
            for j, chip in enumerate(chips):
                mine = outs[k].at[me, c]
                first.append(_remote(mine, mine, send.at[3 * k + j], recv.at[3 * k + j], (*chip, c)))
                first[-1].start()
        for k in range(n):
            for j in range(3):
                piece = outs[k].at[cids[j], c]
                _remote(piece, piece, send.at[3 * k + j], recv.at[3 * k + j], sib).wait_recv()
                passed.append(_remote(piece, piece, fsend.at[3 * k + j], frecv.at[3 * k + j], sib))
                passed[-1].start()
        for k in range(n):
            for j in range(3):
                piece = outs[k].at[cids[j], 1 - c]
                _remote(piece, piece, fsend.at[3 * k + j], frecv.at[3 * k + j], sib).wait_recv()
        for cp in first + passed:
            cp.wait_send()

    return pl.pallas_call(
        body, name=name, in_specs=[HBM_SPEC] * n, out_specs=[HBM_SPEC] * n,
        out_shape=[SDS(s.shape, s.dtype) for s in shards], input_output_aliases={k: k for k in range(n)},
        scratch_shapes=[pltpu.SemaphoreType.DMA((3 * n,))] * 4)(*shards)


def exchange_layers(gs, *, name):
    n = len(gs)

    def body(*refs):
        ins, outs, (send, recv) = refs[:n], refs[n:2 * n], refs[2 * n:]
        x, y, c, me, sib, chips, cids = _place()
        cps = [_remote(ins[k].at[1 - c], outs[k], send.at[k], recv.at[k], sib) for k in range(n)]
        for cp in cps:
            cp.start()
        for cp in cps:
            cp.wait()

    return pl.pallas_call(
        body, name=name, in_specs=[HBM_SPEC] * n, out_specs=[HBM_SPEC] * n,
        out_shape=[SDS(g.shape[1:], g.dtype) for g in gs], scratch_shapes=[pltpu.SemaphoreType.DMA((n,))] * 2)(*gs)


def reduce_to_shards(ss, *, name):
    n = len(ss)

    def body(*refs):
        ins, outs, (send, recv) = refs[:n], refs[n:2 * n], refs[2 * n:]
        x, y, c, me, sib, chips, cids = _place()
        cps = []
        for k in range(n):
            for j, chip in enumerate(chips):
                cps.append(_remote(ins[k].at[cids[j]], outs[k].at[j], send.at[3 * k + j], recv.at[3 * k + j], (*chip, c)))
                cps[-1].start()
        for k in range(n):
            for j in range(3):
                slot = outs[k].at[j]
                _remote(slot, slot, send.at[3 * k + j], recv.at[3 * k + j], sib).wait_recv()
        for cp in cps:
            cp.wait_send()

    return pl.pallas_call(
        body, name=name, in_specs=[HBM_SPEC] * n, out_specs=[HBM_SPEC] * n,
        out_shape=[SDS((N_CHIP - 1,) + s.shape[1:], s.dtype) for s in ss],
        scratch_shapes=[pltpu.SemaphoreType.DMA((3 * n,))] * 2)(*ss)


def share_layers(fs, *, name):
    n = len(fs)

    def body(*refs):
        outs, (send, recv) = refs[n:2 * n], refs[2 * n:]
        x, y, c, me, sib, chips, cids = _place()
        cps = [_remote(outs[k].at[c], outs[k].at[c], send.at[k], recv.at[k], sib) for k in range(n)]
        for cp in cps:
            cp.start()
        for k in range(n):
            theirs = outs[k].at[1 - c]
            _remote(theirs, theirs, send.at[k], recv.at[k], sib).wait_recv()
        for cp in cps:
            cp.wait_send()

    return pl.pallas_call(
        body, name=name, in_specs=[HBM_SPEC] * n, out_specs=[HBM_SPEC] * n, out_shape=[SDS(f.shape, f.dtype) for f in fs],
        input_output_aliases={k: k for k in range(n)}, scratch_shapes=[pltpu.SemaphoreType.DMA((n,))] * 2)(*fs)


N_DEV = 8


def allreduce_small(buf, *, name):
    rows = buf.shape[0]

    def body(in_ref, out_ref, gath, send, recv):
        x, y, c = lax.axis_index("x"), lax.axis_index("y"), lax.axis_index("c")
        mine = 4 * x + 2 * y + c
        gath[mine] = in_ref[...]
        cps = []
        for k in range(1, N_DEV):
            to = (x ^ (k >> 2), y ^ ((k >> 1) & 1), c ^ (k & 1))
            cps.append(_remote(in_ref, gath.at[mine], send.at[k - 1], recv.at[k - 1], to))
            cps[-1].start()
        for k in range(1, N_DEV):
            theirs = gath.at[4 * (x ^ (k >> 2)) + 2 * (y ^ ((k >> 1) & 1)) + (c ^ (k & 1))]
            _remote(theirs, theirs, send.at[k - 1], recv.at[k - 1], (x, y, c)).wait_recv()
        for cp in cps:
            cp.wait_send()
        acc = gath[0]
        for d in range(1, N_DEV):
            acc = acc + gath[d]
        out_ref[...] = acc

    vm = pl.BlockSpec(memory_space=pltpu.VMEM)
    return pl.pallas_call(
        body, name=name, in_specs=[vm], out_specs=vm, out_shape=SDS(buf.shape, F32),
        scratch_shapes=[pltpu.VMEM((N_DEV, rows, 128), F32), pltpu.SemaphoreType.DMA((N_DEV - 1,)),
                        pltpu.SemaphoreType.DMA((N_DEV - 1,))])(buf)


def add_sibling(g, r, c_arr, *, name, tr=256):
    _, m, cdim = g.shape
    assert m % tr == 0

    def body(c_ref, g_ref, r_ref, o_ref):
        o_ref[...] = (g_ref[...] + r_ref[...]).astype(o_ref.dtype)

    return pl.pallas_call(
        body, name=name,
        grid_spec=pltpu.PrefetchScalarGridSpec(
            num_scalar_prefetch=1, grid=(m // tr,),
            in_specs=[pl.BlockSpec((None, tr, cdim), lambda i, c: (c[0], i, 0)), pl.BlockSpec((tr, cdim), lambda i, c: (i, 0))],
            out_specs=pl.BlockSpec((tr, cdim), lambda i, c: (i, 0))),
        out_shape=SDS((m, cdim), BF16), compiler_params=_params(("parallel",)))(c_arr, g, r)


def sum_slots(own, r, place_arr, *, name, tr=256):
    _, m, cdim = r.shape
    tr = next(cand for cand in (tr, 128, 64, 32, 16, 8) if m % cand == 0)

    def body(p_ref, own_ref, r_ref, o_ref):
        o_ref[...] = ((own_ref[...].astype(F32) + r_ref[0].astype(F32)) + r_ref[1].astype(F32)) + r_ref[2].astype(F32)

    return pl.pallas_call(
        body, name=name,
        grid_spec=pltpu.PrefetchScalarGridSpec(
            num_scalar_prefetch=1, grid=(m // tr,),
            in_specs=[pl.BlockSpec((None, tr, cdim), lambda i, p: (p[0], i, 0)),
                      pl.BlockSpec((N_CHIP - 1, tr, cdim), lambda i, p: (0, i, 0))],
            out_specs=pl.BlockSpec((None, tr, cdim), lambda i, p: (p[1], i, 0))),
        out_shape=SDS((N_LAYER, m, cdim), F32), compiler_params=_params(("parallel",)))(place_arr, own, r)


WEIGHTS = ("ffn1_norm", "ffn1_w_gate", "ffn1_w_up", "ffn1_w_down", "mix_norm", "w_in", "lru_conv_w", "lru_conv_b", "lru_w_a",
           "lru_b_a", "lru_w_x", "lru_b_x", "lru_lambda", "attn_sinks", "rel_bias", "dn_conv_w", "dn_a_log", "dn_dt_bias",
           "dn_norm", "w_out", "ffn2_norm", "ffn2_w_gate", "ffn2_w_up", "ffn2_w_down", "ple_norm", "ple_w_gate",
           "ple_w_proj", "final_norm")
CONV_SHARDED = ("lru_conv_w", "dn_conv_w")
SMALL = tuple(k for k in WEIGHTS if k not in SHARDED)


def _pack(arrs):
    flat = []
    for a in arrs:
        v = a.reshape(-1)
        flat.append(jnp.pad(v, (0, -v.shape[0] % 128)))
    v = jnp.concatenate(flat)
    v = jnp.pad(v, (0, -v.shape[0] % 1024))
    return v.reshape(-1, 128)


def _unpack(buf, shapes):
    v, out, off = buf.reshape(-1), [], 0
    for s in shapes:
        n = int(np.prod(s))
        out.append(v[off:off + n].reshape(s))
        off += n + (-n % 128)
    return out


def _chip_cols(a):
    n, l, r, c = a.shape
    return a.transpose(1, 2, 0, 3).reshape(l, r, n * c)


def _chip_rows(a):
    n, l, r, c = a.shape
    return a.transpose(1, 0, 2, 3).reshape(l, n * r, c)


def kernel(x, p, ffn1_norm, ffn1_w_gate, ffn1_w_up, ffn1_w_down, mix_norm, w_in, lru_conv_w, lru_conv_b, lru_w_a, lru_b_a, lru_w_x, lru_b_x, lru_lambda, attn_sinks, rel_bias, dn_conv_w, dn_a_log, dn_dt_bias, dn_norm, w_out, ffn2_norm, ffn2_w_gate, ffn2_w_up, ffn2_w_down, ple_norm, ple_w_gate, ple_w_proj, final_norm, loss_target, m_ffn1_norm, m_ffn1_w_gate, m_ffn1_w_up, m_ffn1_w_down, m_mix_norm, m_w_in, m_lru_conv_w, m_lru_conv_b, m_lru_w_a, m_lru_b_a, m_lru_w_x, m_lru_b_x, m_lru_lambda, m_attn_sinks, m_rel_bias, m_dn_conv_w, m_dn_a_log, m_dn_dt_bias, m_dn_norm, m_w_out, m_ffn2_norm, m_ffn2_w_gate, m_ffn2_w_up, m_ffn2_w_down, m_ple_norm, m_ple_w_gate, m_ple_w_proj, m_final_norm, v_ffn1_norm, v_ffn1_w_gate, v_ffn1_w_up, v_ffn1_w_down, v_mix_norm, v_w_in, v_lru_conv_w, v_lru_conv_b, v_lru_w_a, v_lru_b_a, v_lru_w_x, v_lru_b_x, v_lru_lambda, v_attn_sinks, v_rel_bias, v_dn_conv_w, v_dn_a_log, v_dn_dt_bias, v_dn_norm, v_w_out, v_ffn2_norm, v_ffn2_w_gate, v_ffn2_w_up, v_ffn2_w_down, v_ple_norm, v_ple_w_gate, v_ple_w_proj, v_final_norm):
    given = dict(locals())
    ws = {k: given[k] for k in WEIGHTS}
    ms = {k: given["m_" + k] for k in WEIGHTS}
    vs = {k: given["v_" + k] for k in WEIGHTS}
    nb, seq, d = x.shape
    t = nb * seq
    cx, cy, cc = lax.axis_index("x"), lax.axis_index("y"), lax.axis_index("c")
    chip = 2 * cx + cy

    chip_arr = chip.astype(jnp.int32).reshape(1)
    placed = [place_shard(ws[k], chip_arr, F32 if k in CONV_SHARDED else BF16, name=f"place_{k}")
              for k in SHARDED + CONV_SHARDED]
    gathered = allgather_shards(placed, name="allgather_weights")
    full = dict(zip(SHARDED + CONV_SHARDED, gathered))
    for k in ("w_in", "ple_w_proj", "lru_conv_w", "dn_conv_w"):
        full[k] = _chip_cols(full[k])
    for k in ("w_out", "ple_w_gate"):
        full[k] = _chip_rows(full[k])
    full["w_in"] = jnp.pad(full["w_in"], ((0, 0), (0, 0), (0, D_IN_PAD - D_IN)))
    for k in SMALL:
        if k not in CONV_SHARDED:
            full[k] = ws[k]

    bmap = jnp.asarray(_rel_bucket_map())
    loss, gx, grads = local_step(x.reshape(t, d), p.reshape(N_LAYER, t, PLE_DIM), loss_target.reshape(t, d), full, bmap, nb, seq)

    gs = [grads[k] for k in SHARDED]
    flat = lambda a, lead: a.reshape(a.shape[:lead] + (-1, a.shape[-1]))
    theirs = exchange_layers(gs, name="rs_exchange_layers")
    c_arr = cc.astype(jnp.int32).reshape(1)
    sums = [add_sibling(flat(g, 1), flat(r, 0), c_arr, name=f"rs_add_{k}").reshape(r.shape)
            for k, g, r in zip(SHARDED, gs, theirs)]
    slots = reduce_to_shards(sums, name="rs_reduce_to_shards")
    place_arr = jnp.stack([chip, cc]).astype(jnp.int32)
    mine = [sum_slots(flat(s, 1), flat(r, 1), place_arr, name=f"rs_sum_{k}").reshape((N_LAYER,) + r.shape[1:])
            for k, s, r in zip(SHARDED, sums, slots)]
    g_out = dict(zip(SHARDED, share_layers(mine, name="rs_share_layers")))

    small_shapes = [grads[k].shape for k in SMALL]
    g_small = dict(zip(SMALL, _unpack(allreduce_small(_pack([grads[k] for k in SMALL]), name="allreduce_small"), small_shapes)))
    for k in CONV_SHARDED:
        width = ws[k].shape[-1]
        g_small[k] = lax.dynamic_slice_in_dim(g_small[k], chip * width, width, axis=2)
    g_out.update(g_small)

    delta, new_m, new_v = {}, {}, {}
    for k in SHARDED:
        two_d = lambda a: a.reshape(-1, a.shape[-1])
        res = adamw(two_d(ws[k]), two_d(g_out[k]), two_d(ms[k]), two_d(vs[k]), name=f"adamw_{k}")
        delta[k], new_m[k], new_v[k] = (r.reshape(ws[k].shape) for r in res)
    shapes = [ws[k].shape for k in SMALL]
    res = adamw(*[_pack([src[k] for k in SMALL]) for src in (ws, g_out, ms, vs)], name="adamw_small")
    for dst, r in zip((delta, new_m, new_v), res):
        dst.update(zip(SMALL, _unpack(r, shapes)))

    total = lax.psum(loss[0, 0], ("x", "y", "c"))
    return (total, gx.reshape(nb, seq, d), *[g_out[k] for k in WEIGHTS], *[delta[k] for k in WEIGHTS],
            *[new_m[k] for k in WEIGHTS], *[new_v[k] for k in WEIGHTS])
```

```python
import functools
import math

import numpy as np
import jax
import jax.numpy as jnp
from jax import lax
from jax.experimental import pallas as pl
from jax.experimental.pallas import tpu as pltpu

F32 = jnp.float32
BF16 = jnp.bfloat16

EPS = 1e-6
D_MODEL = 1024
D_FF = 2816
N_CHIP = 4
FF_BLK = D_FF // N_CHIP
HEAD = 64
LRU_W = 256
ATT_W = 512
ATT_HEADS = 8
KV_HEADS = 2
ATT_GROUP = 4
BLOCK_Q = 128
DN_HEADS = 4
DN_CHUNK = 64
D_IN = 2312
D_IN_PAD = 2560
PLE_DIM = 256
REL_BUCKETS = 32
LRU_C = 8.0
N_LAYER = 2

ADAM_LR, ADAM_B1, ADAM_B2, ADAM_EPS, ADAM_WD, ADAM_STEP = 0.001, 0.9, 0.999, 1e-08, 0.01, 10

VMEM_LIMIT = 56 << 20
MESH = pl.DeviceIdType.MESH
SDS = jax.ShapeDtypeStruct


def _dot(a, b, ca=1, cb=0, hi=False):
    dims = (((ca,), (cb,)), ((), ()))
    one = lambda u, v: lax.dot_general(u, v, dims, preferred_element_type=F32)
    a_hi, b_hi = a.astype(BF16), b.astype(BF16)
    if not hi:
        return one(a_hi, b_hi)
    a_lo = (a - a_hi.astype(F32)).astype(BF16)
    b_lo = (b - b_hi.astype(F32)).astype(BF16)
    return one(a_hi, b_hi) + (one(a_hi, b_lo) + one(a_lo, b_hi))


def _nn(a, b, hi=False):
    return _dot(a, b, 1, 0, hi)


def _nt(a, b, hi=False):
    return _dot(a, b, 1, 1, hi)


def _tn(a, b, hi=False):
    return _dot(a, b, 0, 0, hi)


def _sigmoid(x):
    return jax.nn.sigmoid(x)


def _softplus(x):
    return jnp.maximum(x, 0.0) + jnp.log1p(jnp.exp(-jnp.abs(x)))


def _neg_expm1(z):
    series = -z * (1.0 + z * (0.5 + z * (1.0 / 6.0 + z * (1.0 / 24.0 + z * (1.0 / 120.0)))))
    return jnp.where(z > -0.05, series, 1.0 - jnp.exp(z))


_GELU_C = math.sqrt(2.0 / math.pi)


def _gelu(x):
    t = jnp.tanh(_GELU_C * (x + 0.044715 * x * x * x))
    return 0.5 * x * (1.0 + t), t


def _gelu_grad(x, t):
    return 0.5 * (1.0 + t) + 0.5 * x * (1.0 - t * t) * _GELU_C * (1.0 + 3.0 * 0.044715 * x * x)


def _rms_fwd(h, g):
    r = lax.rsqrt(jnp.mean(h * h, axis=-1, keepdims=True) + EPS)
    xh = h * r
    return xh * g, xh, r


def _rms_bwd(dn, xh, r, g):
    dxh = dn * g
    dh = r * (dxh - xh * jnp.mean(dxh * xh, axis=-1, keepdims=True))
    return dh, jnp.sum(dn * xh, axis=0, keepdims=True)


def _shift_down(x, d, fill=0.0):
    row = lax.broadcasted_iota(jnp.int32, x.shape, 0)
    return jnp.where(row >= d, pltpu.roll(x, d, 0), fill)


def _shift_up(x, d, fill=0.0):
    n = x.shape[0]
    row = lax.broadcasted_iota(jnp.int32, x.shape, 0)
    return jnp.where(row < n - d, pltpu.roll(x, n - d, 0), fill)


def _conv_fwd(x, w):
    y = x * w[3]
    for k in range(3):
        y = y + _shift_down(x, 3 - k) * w[k]
    return y


def _conv_bwd(dy, x, w):
    dx = dy * w[3]
    rows = [None] * 4
    rows[3] = jnp.sum(dy * x, axis=0, keepdims=True)
    for k in range(3):
        dx = dx + _shift_up(dy, 3 - k) * w[k]
        rows[k] = jnp.sum(dy * _shift_down(x, 3 - k), axis=0, keepdims=True)
    r4 = lax.broadcasted_iota(jnp.int32, (4, x.shape[1]), 0)
    dw = jnp.zeros((4, x.shape[1]), F32)
    for k in range(4):
        dw = jnp.where(r4 == k, rows[k], dw)
    return dx, dw


def _params(sem=None, vmem=VMEM_LIMIT):
    return pltpu.CompilerParams(dimension_semantics=sem, vmem_limit_bytes=vmem)


def _whole(shape):
    nd = len(shape)
    return pl.BlockSpec(shape, lambda *_: (0,) * nd)


def matmul(a, b, *, name, ta=False, tb=False, residual=None, out_dtype=F32, tm=512, tn=512, tk=512):
    m, k = (a.shape[1], a.shape[0]) if ta else a.shape
    n = b.shape[0] if tb else b.shape[1]
    tm, tn, tk = min(tm, m), min(tn, n), min(tk, k)
    assert m % tm == 0 and n % tn == 0 and k % tk == 0, (m, n, k, tm, tn, tk)
    nk = k // tk

    def body(*refs):
        if residual is None:
            a_ref, b_ref, o_ref, acc = refs
        else:
            a_ref, b_ref, r_ref, o_ref, acc = refs
        kk = pl.program_id(2)

        @pl.when(kk == 0)
        def _():
            acc[...] = jnp.zeros_like(acc)

        acc[...] += _dot(a_ref[...], b_ref[...], 0 if ta else 1, 1 if tb else 0)

        @pl.when(kk == nk - 1)
        def _():
            out = acc[...]
            if residual is not None:
                out = out + r_ref[...]
            o_ref[...] = out.astype(out_dtype)

    a_spec = pl.BlockSpec((tk, tm), lambda i, j, kk: (kk, i)) if ta else pl.BlockSpec((tm, tk), lambda i, j, kk: (i, kk))
    b_spec = pl.BlockSpec((tn, tk), lambda i, j, kk: (j, kk)) if tb else pl.BlockSpec((tk, tn), lambda i, j, kk: (kk, j))
    o_spec = pl.BlockSpec((tm, tn), lambda i, j, kk: (i, j))
    in_specs, args = [a_spec, b_spec], [a, b]
    if residual is not None:
        in_specs.append(o_spec)
        args.append(residual)
    return pl.pallas_call(
        body, name=name, grid=(m // tm, n // tn, nk), in_specs=in_specs, out_specs=o_spec,
        out_shape=SDS((m, n), out_dtype), scratch_shapes=[pltpu.VMEM((tm, tn), F32)],
        compiler_params=_params(("parallel", "parallel", "arbitrary")))(*args)


def norm_matmul(h, gain, w, *, name, tm=512, tn=512):
    t, d = h.shape
    tm = min(tm, t)
    n = w.shape[1]
    assert t % tm == 0 and n % tn == 0

    def body(h_ref, g_ref, w_ref, u_ref, n_ref):
        @pl.when(pl.program_id(1) == 0)
        def _():
            n_ref[...] = _rms_fwd(h_ref[...], g_ref[...])[0].astype(BF16)

        u_ref[...] = _nn(n_ref[...], w_ref[...])

    return pl.pallas_call(
        body, name=name, grid=(t // tm, n // tn),
        in_specs=[pl.BlockSpec((tm, d), lambda i, j: (i, 0)), _whole((1, d)), pl.BlockSpec((d, tn), lambda i, j: (0, j))],
        out_specs=[pl.BlockSpec((tm, tn), lambda i, j: (i, j)), pl.BlockSpec((tm, d), lambda i, j: (i, 0))],
        out_shape=[SDS((t, n), F32), SDS((t, d), BF16)],
        compiler_params=_params(("parallel", "arbitrary")))(h, gain, w)


def rms_bwd(h, gain, dn, dres, *, name, tm=512):
    t, d = h.shape
    tm = min(tm, t)

    def body(h_ref, g_ref, dn_ref, dr_ref, dh_ref, dg_ref):
        @pl.when(pl.program_id(0) == 0)
        def _():
            dg_ref[...] = jnp.zeros_like(dg_ref)

        g = g_ref[...]
        _, xh, r = _rms_fwd(h_ref[...], g)
        dh, dg = _rms_bwd(dn_ref[...], xh, r, g)
        dh_ref[...] = dr_ref[...] + dh
        dg_ref[...] += dg

    row = pl.BlockSpec((tm, d), lambda i: (i, 0))
    return pl.pallas_call(
        body, name=name, grid=(t // tm,), in_specs=[row, _whole((1, d)), row, row],
        out_specs=[row, _whole((1, d))], out_shape=[SDS((t, d), F32), SDS((1, d), F32)],
        compiler_params=_params(("arbitrary",)))(h, gain, dn, dres)


def ffn_fwd(h, gain, wg, wu, wd, *, name, tm=512):
    t, d = h.shape
    tm = min(tm, t)

    def body(h_ref, g_ref, wg_ref, wu_ref, wd_ref, o_ref, n_sc, acc):
        j = pl.program_id(1)

        @pl.when(j == 0)
        def _():
            n_sc[...] = _rms_fwd(h_ref[...], g_ref[...])[0].astype(BF16)
            acc[...] = jnp.zeros_like(acc)

        n = n_sc[...]
        a = _nn(n, wg_ref[...])
        b = _nn(n, wu_ref[...])
        acc[...] += _nn(a * _sigmoid(a) * b, wd_ref[...])

        @pl.when(j == N_CHIP - 1)
        def _():
            o_ref[...] = h_ref[...] + 0.5 * acc[...]

    row = pl.BlockSpec((tm, d), lambda i, j: (i, 0))
    return pl.pallas_call(
        body, name=name, grid=(t // tm, N_CHIP),
        in_specs=[row, _whole((1, d)),
                  pl.BlockSpec((None, d, FF_BLK), lambda i, j: (j, 0, 0)),
                  pl.BlockSpec((None, d, FF_BLK), lambda i, j: (j, 0, 0)),
                  pl.BlockSpec((None, FF_BLK, d), lambda i, j: (j, 0, 0))],
        out_specs=row, out_shape=SDS((t, d), F32),
        scratch_shapes=[pltpu.VMEM((tm, d), BF16), pltpu.VMEM((tm, d), F32)],
        compiler_params=_params(("parallel", "arbitrary")))(h, gain, wg, wu, wd)


def ffn_bwd_act(h, gain, dout, wg, wu, wd, *, name, tm=512):
    t, d = h.shape
    tm = min(tm, t)

    def body(h_ref, g_ref, do_ref, wg_ref, wu_ref, wd_ref, dh_ref, n_ref, da_ref, db_ref, s_ref, dg_ref, dn_acc):
        i, j = pl.program_id(0), pl.program_id(1)

        @pl.when((i == 0) & (j == 0))
        def _():
            dg_ref[...] = jnp.zeros_like(dg_ref)

        @pl.when(j == 0)
        def _():
            n_ref[...] = _rms_fwd(h_ref[...], g_ref[...])[0].astype(BF16)
            dn_acc[...] = jnp.zeros_like(dn_acc)

        n = n_ref[...]
        a = _nn(n, wg_ref[...])
        b = _nn(n, wu_ref[...])
        sig = _sigmoid(a)
        sa = a * sig
        ds = _nt(0.5 * do_ref[...], wd_ref[...])
        db = ds * sa
        da = ds * b * (sig * (1.0 + a * (1.0 - sig)))
        s_ref[...] = (sa * b).astype(BF16)
        da_ref[...] = da.astype(BF16)
        db_ref[...] = db.astype(BF16)
        dn_acc[...] += _nt(da, wg_ref[...]) + _nt(db, wu_ref[...])

        @pl.when(j == N_CHIP - 1)
        def _():
            g = g_ref[...]
            _, xh, r = _rms_fwd(h_ref[...], g)
            dh, dg = _rms_bwd(dn_acc[...], xh, r, g)
            dh_ref[...] = do_ref[...] + dh
            dg_ref[...] += dg

    row = pl.BlockSpec((tm, d), lambda i, j: (i, 0))
    blk = pl.BlockSpec((None, tm, FF_BLK), lambda i, j: (j, i, 0))
    act = SDS((N_CHIP, t, FF_BLK), BF16)
    return pl.pallas_call(
        body, name=name, grid=(t // tm, N_CHIP),
        in_specs=[row, _whole((1, d)), row,
                  pl.BlockSpec((None, d, FF_BLK), lambda i, j: (j, 0, 0)),
                  pl.BlockSpec((None, d, FF_BLK), lambda i, j: (j, 0, 0)),
                  pl.BlockSpec((None, FF_BLK, d), lambda i, j: (j, 0, 0))],
        out_specs=[row, row, blk, blk, blk, _whole((1, d))],
        out_shape=[SDS((t, d), F32), SDS((t, d), BF16), act, act, act, SDS((1, d), F32)],
        scratch_shapes=[pltpu.VMEM((tm, d), F32)],
        compiler_params=_params(("arbitrary", "arbitrary")))(h, gain, dout, wg, wu, wd)


def ffn_bwd_w(n, da, db, s, dout, *, name, tk=512):
    t, d = n.shape
    tk = min(tk, t)

    def body(n_ref, da_ref, db_ref, s_ref, do_ref, dwg_ref, dwu_ref, dwd_ref):
        @pl.when(pl.program_id(1) == 0)
        def _():
            dwg_ref[...] = jnp.zeros_like(dwg_ref)
            dwu_ref[...] = jnp.zeros_like(dwu_ref)
            dwd_ref[...] = jnp.zeros_like(dwd_ref)

        nn = n_ref[...]
        dwg_ref[...] += _tn(nn, da_ref[...])
        dwu_ref[...] += _tn(nn, db_ref[...])
        dwd_ref[...] += _tn(s_ref[...], 0.5 * do_ref[...])

    row = pl.BlockSpec((tk, d), lambda j, kk: (kk, 0))
    blk = pl.BlockSpec((None, tk, FF_BLK), lambda j, kk: (j, kk, 0))
    return pl.pallas_call(
        body, name=name, grid=(N_CHIP, t // tk), in_specs=[row, blk, blk, blk, row],
        out_specs=[pl.BlockSpec((None, d, FF_BLK), lambda j, kk: (j, 0, 0)),
                   pl.BlockSpec((None, d, FF_BLK), lambda j, kk: (j, 0, 0)),
                   pl.BlockSpec((None, FF_BLK, d), lambda j, kk: (j, 0, 0))],
        out_shape=[SDS((N_CHIP, d, FF_BLK), F32), SDS((N_CHIP, d, FF_BLK), F32), SDS((N_CHIP, FF_BLK, d), F32)],
        compiler_params=_params(("parallel", "arbitrary")))(n, da, db, s, dout)


def ple_fwd(h, gain, wpg, pl_in, wpp, *, name, tm=512):
    t, d = h.shape
    tm = min(tm, t)
    pd = pl_in.shape[1]

    def body(h_ref, g_ref, wpg_ref, p_ref, wpp_ref, o_ref):
        hh = h_ref[...]
        n = _rms_fwd(hh, g_ref[...])[0]
        gate = _sigmoid(_nn(n, wpg_ref[...]))
        o_ref[...] = hh + gate * _nn(p_ref[...], wpp_ref[...])

    row = pl.BlockSpec((tm, d), lambda i: (i, 0))
    return pl.pallas_call(
        body, name=name, grid=(t // tm,),
        in_specs=[row, _whole((1, d)), _whole((d, d)), pl.BlockSpec((tm, pd), lambda i: (i, 0)), _whole((pd, d))],
        out_specs=row, out_shape=SDS((t, d), F32), compiler_params=_params(("parallel",)))(h, gain, wpg, pl_in, wpp)


def ple_bwd(h, gain, wpg, pl_in, wpp, dout, *, name, tm=512):
    t, d = h.shape
    tm = min(tm, t)
    pd = pl_in.shape[1]

    def body(h_ref, g_ref, wpg_ref, p_ref, wpp_ref, do_ref, dh_ref, n_ref, dga_ref, dpp_ref, dg_ref):
        @pl.when(pl.program_id(0) == 0)
        def _():
            dg_ref[...] = jnp.zeros_like(dg_ref)

        g = g_ref[...]
        n, xh, r = _rms_fwd(h_ref[...], g)
        gate = _sigmoid(_nn(n, wpg_ref[...]))
        pp = _nn(p_ref[...], wpp_ref[...])
        do = do_ref[...]
        dga = do * pp * gate * (1.0 - gate)
        dh, dg = _rms_bwd(_nt(dga, wpg_ref[...]), xh, r, g)
        dh_ref[...] = do + dh
        n_ref[...] = n.astype(BF16)
        dga_ref[...] = dga.astype(BF16)
        dpp_ref[...] = (do * gate).astype(BF16)
        dg_ref[...] += dg

    row = pl.BlockSpec((tm, d), lambda i: (i, 0))
    return pl.pallas_call(
        body, name=name, grid=(t // tm,),
        in_specs=[row, _whole((1, d)), _whole((d, d)), pl.BlockSpec((tm, pd), lambda i: (i, 0)), _whole((pd, d)), row],
        out_specs=[row, row, row, row, _whole((1, d))],
        out_shape=[SDS((t, d), F32), SDS((t, d), BF16), SDS((t, d), BF16), SDS((t, d), BF16), SDS((1, d), F32)],
        compiler_params=_params(("arbitrary",)))(h, gain, wpg, pl_in, wpp, dout)


def loss_head(h, gain, target, *, name, tm=512):
    t, d = h.shape
    tm = min(tm, t)

    def body(h_ref, g_ref, t_ref, dh_ref, dg_ref, l_ref):
        @pl.when(pl.program_id(0) == 0)
        def _():
            dg_ref[...] = jnp.zeros_like(dg_ref)
            l_ref[...] = jnp.zeros_like(l_ref)

        g = g_ref[...]
        y, xh, r = _rms_fwd(h_ref[...], g)
        err = y - t_ref[...]
        l_ref[...] += 0.5 * jnp.sum(jnp.mean(err * err, axis=-1, keepdims=True), axis=0, keepdims=True)
        dh, dg = _rms_bwd(err * (1.0 / d), xh, r, g)
        dh_ref[...] = dh
        dg_ref[...] += dg

    row = pl.BlockSpec((tm, d), lambda i: (i, 0))
    return pl.pallas_call(
        body, name=name, grid=(t // tm,), in_specs=[row, _whole((1, d)), row],
        out_specs=[row, _whole((1, d)), _whole((1, 1))],
        out_shape=[SDS((t, d), F32), SDS((1, d), F32), SDS((1, 1), F32)],
        compiler_params=_params(("arbitrary",)))(h, gain, target)


def adamw(w, g, m, v, *, name):
    r, c = w.shape
    tr = r
    for cand in (512, 256, 128, 64, 32, 16, 8):
        if r % cand == 0:
            tr = cand
            break

    def body(w_ref, g_ref, m_ref, v_ref, d_ref, nm_ref, nv_ref):
        gg = g_ref[...]
        mm = ADAM_B1 * m_ref[...] + (1.0 - ADAM_B1) * gg
        vv = ADAM_B2 * v_ref[...] + (1.0 - ADAM_B2) * (gg * gg)
        m_hat = mm / (1.0 - ADAM_B1 ** ADAM_STEP)
        v_hat = vv / (1.0 - ADAM_B2 ** ADAM_STEP)
        d_ref[...] = -ADAM_LR * (m_hat / (jnp.sqrt(v_hat) + ADAM_EPS) + ADAM_WD * w_ref[...])
        nm_ref[...] = mm
        nv_ref[...] = vv

    blk = pl.BlockSpec((tr, c), lambda i: (i, 0))
    out = SDS((r, c), F32)
    return pl.pallas_call(body, name=name, grid=(r // tr,), in_specs=[blk] * 4, out_specs=[blk] * 3,
                          out_shape=[out, out, out], compiler_params=_params(("parallel",)))(w, g, m, v)


def _scan_fwd(a, b):
    d = 1
    while d < a.shape[0]:
        b = a * _shift_down(b, d, 0.0) + b
        a = a * _shift_down(a, d, 1.0)
        d *= 2
    return b


def _scan_rev(a, b):
    d = 1
    while d < a.shape[0]:
        b = a * _shift_up(b, d, 0.0) + b
        a = a * _shift_up(a, d, 1.0)
        d *= 2
    return b


LRU_HALF = 128


def _lru_in_specs(seq):
    half = LRU_W // LRU_HALF
    vec = pl.BlockSpec((1, LRU_HALF), lambda j, b: (0, j))
    mat = pl.BlockSpec((LRU_HALF, LRU_HALF), lambda j, b: (j, j))
    return [pl.BlockSpec((seq, LRU_HALF), lambda j, b: (b, j)), pl.BlockSpec((seq, LRU_HALF), lambda j, b: (b, half + j)),
            pl.BlockSpec((4, LRU_HALF), lambda j, b: (0, j)), vec, mat, vec, mat, vec, vec]


def _lru_math(x_ref, gate_ref, cw_ref, cb_ref, wa_ref, ba_ref, wx_ref, bx_ref, lam_ref):
    x = x_ref[...]
    gate = gate_ref[...]
    cw =[cw_ref[k:k + 1, :] for k in range(4)]
    xr = _conv_fwd(x, cw) + cb_ref[...]
    r = _sigmoid(_nn(xr, wa_ref[...]) + ba_ref[...])
    i = _sigmoid(_nn(xr, wx_ref[...]) + bx_ref[...])
    sp = _softplus(-lam_ref[...])
    log_a = -LRU_C * r * sp
    a = jnp.exp(log_a)
    mult = jnp.sqrt(_neg_expm1(2.0 * log_a))
    gi = i * xr
    h = _scan_fwd(a, mult * gi)
    gl, tg = _gelu(gate)
    return dict(x=x, gate=gate, cw=cw, xr=xr, r=r, i=i, sp=sp, a=a, mult=mult, gi=gi, h=h, gl=gl, tg=tg)


def lru_fwd(u, cw, cb, wa, ba, wx, bx, lam, *, seq, name):
    t = u.shape[0]

    def body(x_ref, gate_ref, cw_ref, cb_ref, wa_ref, ba_ref, wx_ref, bx_ref, lam_ref, y_ref):
        f = _lru_math(x_ref, gate_ref, cw_ref, cb_ref, wa_ref, ba_ref, wx_ref, bx_ref, lam_ref)
        y_ref[...] = f["gl"] * f["h"]

    return pl.pallas_call(
        body, name=name, grid=(LRU_W // LRU_HALF, t // seq), in_specs=_lru_in_specs(seq),
        out_specs=pl.BlockSpec((seq, LRU_HALF), lambda j, b: (b, j)), out_shape=SDS((t, LRU_W), F32),
        compiler_params=_params(("parallel", "parallel")))(u, u, cw, cb, wa, ba, wx, bx, lam)


def lru_bwd(u, cw, cb, wa, ba, wx, bx, lam, dy, *, seq, name):
    t = u.shape[0]

    def body(x_ref, gate_ref, cw_ref, cb_ref, wa_ref, ba_ref, wx_ref, bx_ref, lam_ref, dy_ref,
             dx_ref, dgate_ref, dcw_ref, dwa_ref, dwx_ref, dv_ref):
        @pl.when(pl.program_id(1) == 0)
        def _():
            dcw_ref[...] = jnp.zeros_like(dcw_ref)
            dwa_ref[...] = jnp.zeros_like(dwa_ref)
            dwx_ref[...] = jnp.zeros_like(dwx_ref)
            dv_ref[...] = jnp.zeros_like(dv_ref)

        f = _lru_math(x_ref, gate_ref, cw_ref, cb_ref, wa_ref, ba_ref, wx_ref, bx_ref, lam_ref)
        dy = dy_ref[...]
        a, h, xr, r, i, mult, gi, sp = f["a"], f["h"], f["xr"], f["r"], f["i"], f["mult"], f["gi"], f["sp"]
        dgate_ref[...] = dy * h * _gelu_grad(f["gate"], f["tg"])
        lamb = _scan_rev(_shift_up(a, 1, 0.0), dy * f["gl"])
        da = lamb * _shift_down(h, 1)
        dlog_a = da * a - (lamb * gi) * (a * a) / mult
        dgi = lamb * mult
        dra = dlog_a * (-LRU_C * sp) * r * (1.0 - r)
        dia = dgi * xr * i * (1.0 - i)
        dsp = jnp.sum(dlog_a * (-LRU_C * r), axis=0, keepdims=True)
        dlam = -dsp * _sigmoid(-lam_ref[...])
        dxr = dgi * i + _nt(dra, wa_ref[...]) + _nt(dia, wx_ref[...])
        dx, dcw = _conv_bwd(dxr, f["x"], f["cw"])
        dx_ref[...] = dx
        dcw_ref[...] += dcw
        dwa_ref[...] += _tn(xr, dra)
        dwx_ref[...] += _tn(xr, dia)
        rows = [jnp.sum(dxr, axis=0, keepdims=True), jnp.sum(dra, axis=0, keepdims=True),
                jnp.sum(dia, axis=0, keepdims=True), dlam]
        r8 = lax.broadcasted_iota(jnp.int32, (8, LRU_HALF), 0)
        acc = jnp.zeros((8, LRU_HALF), F32)
        for k, row in enumerate(rows):
            acc = jnp.where(r8 == k, row, acc)
        dv_ref[...] += acc

    nhalf = LRU_W // LRU_HALF
    col = pl.BlockSpec((seq, LRU_HALF), lambda j, b: (b, j))
    mat = pl.BlockSpec((None, LRU_HALF, LRU_HALF), lambda j, b: (j, 0, 0))
    return pl.pallas_call(
        body, name=name, grid=(nhalf, t // seq), in_specs=_lru_in_specs(seq) + [col],
        out_specs=[col, col, pl.BlockSpec((4, LRU_HALF), lambda j, b: (0, j)), mat, mat,
                   pl.BlockSpec((8, LRU_HALF), lambda j, b: (0, j))],
        out_shape=[SDS((t, LRU_W), F32), SDS((t, LRU_W), F32), SDS((4, LRU_W), F32),
                   SDS((nhalf, LRU_HALF, LRU_HALF), F32), SDS((nhalf, LRU_HALF, LRU_HALF), F32), SDS((8, LRU_W), F32)],
        compiler_params=_params(("arbitrary", "arbitrary")))(u, u, cw, cb, wa, ba, wx, bx, lam, dy)


NEG = -1e30


def _rel_bucket_map():
    dist = (np.arange(BLOCK_Q)[:, None] - np.arange(BLOCK_Q)[None, :]) % BLOCK_Q
    max_exact = REL_BUCKETS // 2
    large = max_exact + (np.log(np.maximum(dist, 1).astype(np.float32) / max_exact)
                         / math.log(BLOCK_Q / max_exact) * (REL_BUCKETS - max_exact)).astype(np.int32)
    large = np.minimum(large, REL_BUCKETS - 1)
    return np.where(dist < max_exact, dist, large).astype(np.int32)


def relbias_fwd(rel_bias, bmap, *, name):
    def body(rb_ref, bm_ref, o_ref):
        bm = bm_ref[...]
        for h in range(ATT_HEADS):
            acc = jnp.zeros((BLOCK_Q, BLOCK_Q), F32)
            for b in range(REL_BUCKETS):
                acc = jnp.where(bm == b, rb_ref[b, h], acc)
            o_ref[h] = acc

    return pl.pallas_call(
        body, name=name, in_specs=[pl.BlockSpec(memory_space=pltpu.SMEM), pl.BlockSpec(memory_space=pltpu.VMEM)],
        out_specs=pl.BlockSpec(memory_space=pltpu.VMEM), out_shape=SDS((ATT_HEADS, BLOCK_Q, BLOCK_Q), F32))(rel_bias, bmap)


def relbias_bwd(dbias, bmap, *, name):
    def body(db_ref, bm_ref, o_ref):
        bm = bm_ref[...]
        row = lax.broadcasted_iota(jnp.int32, (REL_BUCKETS, 128), 0)
        col = lax.broadcasted_iota(jnp.int32, (REL_BUCKETS, 128), 1)
        acc = jnp.zeros((REL_BUCKETS, 128), F32)
        for h in range(ATT_HEADS):
            d = db_ref[h]
            for b in range(REL_BUCKETS):
                s = jnp.sum(jnp.sum(jnp.where(bm == b, d, 0.0), axis=1, keepdims=True), axis=0, keepdims=True)
                acc = jnp.where((row == b) & (col == h), s, acc)
        o_ref[...] = acc

    return pl.pallas_call(body, name=name, out_shape=SDS((REL_BUCKETS, 128), F32))(dbias, bmap)


def _attn_probs(q_ref, k_ref, v_ref, b_ref, s_ref, n):
    rows = ATT_GROUP * BLOCK_Q
    qs = q_ref[...].reshape(rows, HEAD) * (HEAD ** -0.5)
    prev = pl.multiple_of(jnp.maximum(n - 1, 0) * BLOCK_Q, BLOCK_Q)
    cur = pl.multiple_of(n * BLOCK_Q, BLOCK_Q)
    kp, kc = k_ref[pl.ds(prev, BLOCK_Q), :], k_ref[pl.ds(cur, BLOCK_Q), :]
    vp, vc = v_ref[pl.ds(prev, BLOCK_Q), :], v_ref[pl.ds(cur, BLOCK_Q), :]
    bias = b_ref[...].reshape(rows, BLOCK_Q)
    i = lax.broadcasted_iota(jnp.int32, (rows, BLOCK_Q), 0) & (BLOCK_Q - 1)
    j = lax.broadcasted_iota(jnp.int32, (rows, BLOCK_Q), 1)
    s_p = jnp.where((j > i) & (n > 0), _nt(qs, kp) + bias, NEG)
    s_c = jnp.where(j <= i, _nt(qs, kc) + bias, NEG)
    sink = s_ref[...]
    m = jnp.maximum(jnp.maximum(jnp.max(s_p, axis=-1, keepdims=True), jnp.max(s_c, axis=-1, keepdims=True)), sink)
    e_p, e_c, e_s = jnp.exp(s_p - m), jnp.exp(s_c - m), jnp.exp(sink - m)
    inv = 1.0 / (jnp.sum(e_p, axis=-1, keepdims=True) + jnp.sum(e_c, axis=-1, keepdims=True) + e_s)
    return e_p * inv, e_c * inv, e_s * inv, qs, kp, kc, vp, vc, prev, cur


def _attn_specs(seq):
    qspec = pl.BlockSpec((None, ATT_GROUP, BLOCK_Q, HEAD), lambda g, b, n: (b, g, n, 0))
    kvspec = pl.BlockSpec((None, None, seq, HEAD), lambda g, b, n: (b, g, 0, 0))
    bspec = pl.BlockSpec((ATT_GROUP, BLOCK_Q, BLOCK_Q), lambda g, b, n: (g, 0, 0))
    sspec = pl.BlockSpec((ATT_GROUP * BLOCK_Q, 1), lambda g, b, n: (g, 0))
    return qspec, kvspec, bspec, sspec


def attn_fwd(q, k, v, bias, sink_rows, *, name):
    nb, _, seq, _ = q.shape

    def body(q_ref, k_ref, v_ref, b_ref, s_ref, o_ref):
        p_p, p_c, _, _, _, _, vp, vc, _, _ = _attn_probs(q_ref, k_ref, v_ref, b_ref, s_ref, pl.program_id(2))
        o_ref[...] = (_nn(p_p, vp) + _nn(p_c, vc)).reshape(ATT_GROUP, BLOCK_Q, HEAD)

    qspec, kvspec, bspec, sspec = _attn_specs(seq)
    return pl.pallas_call(
        body, name=name, grid=(KV_HEADS, nb, seq // BLOCK_Q), in_specs=[qspec, kvspec, kvspec, bspec, sspec],
        out_specs=qspec, out_shape=SDS(q.shape, F32),
        compiler_params=_params(("parallel", "parallel", "arbitrary")))(q, k, v, bias, sink_rows)


def attn_bwd(q, k, v, bias, sink_rows, do, *, name):
    nb, _, seq, _ = q.shape

    def body(q_ref, k_ref, v_ref, b_ref, s_ref, do_ref, dq_ref, dk_ref, dv_ref, db_ref, ds_ref):
        b, n = pl.program_id(1), pl.program_id(2)

        @pl.when((b == 0) & (n == 0))
        def _():
            db_ref[...] = jnp.zeros_like(db_ref)
            ds_ref[...] = jnp.zeros_like(ds_ref)

        @pl.when(n == 0)
        def _():
            dk_ref[...] = jnp.zeros_like(dk_ref)
            dv_ref[...] = jnp.zeros_like(dv_ref)

        p_p, p_c, p_s, qs, kp, kc, vp, vc, prev, cur = _attn_probs(q_ref, k_ref, v_ref, b_ref, s_ref, n)
        do = do_ref[...].reshape(ATT_GROUP * BLOCK_Q, HEAD)
        dp_p, dp_c = _nt(do, vp), _nt(do, vc)
        delta = jnp.sum(p_p * dp_p, axis=-1, keepdims=True) + jnp.sum(p_c * dp_c, axis=-1, keepdims=True)
        ds_p, ds_c = p_p * (dp_p - delta), p_c * (dp_c - delta)
        dq_ref[...] = ((_nn(ds_p, kp) + _nn(ds_c, kc)) * (HEAD ** -0.5)).reshape(ATT_GROUP, BLOCK_Q, HEAD)
        dk_ref[pl.ds(prev, BLOCK_Q), :] += _tn(ds_p, qs)
        dk_ref[pl.ds(cur, BLOCK_Q), :] += _tn(ds_c, qs)
        dv_ref[pl.ds(prev, BLOCK_Q), :] += _tn(p_p, do)
        dv_ref[pl.ds(cur, BLOCK_Q), :] += _tn(p_c, do)
        db_ref[...] += (ds_p + ds_c).reshape(ATT_GROUP, BLOCK_Q, BLOCK_Q)
        ds_ref[...] += -p_s * delta

    qspec, kvspec, bspec, sspec = _attn_specs(seq)
    return pl.pallas_call(
        body, name=name, grid=(KV_HEADS, nb, seq // BLOCK_Q), in_specs=[qspec, kvspec, kvspec, bspec, sspec, qspec],
        out_specs=[qspec, kvspec, kvspec, bspec, sspec],
        out_shape=[SDS(q.shape, F32), SDS(k.shape, F32), SDS(v.shape, F32),
                   SDS((ATT_HEADS, BLOCK_Q, BLOCK_Q), F32), SDS((ATT_HEADS * BLOCK_Q, 1), F32)],
        compiler_params=_params(("arbitrary", "arbitrary", "arbitrary")))(q, k, v, bias, sink_rows, do)


def _iota2(shape, axis):
    return lax.broadcasted_iota(jnp.int32, shape, axis)


def _col_to_row(col):
    c = col.shape[0]
    eye = _iota2((c, c), 0) == _iota2((c, c), 1)
    return jnp.sum(jnp.where(eye, jnp.broadcast_to(col, (c, c)), 0.0), axis=0, keepdims=True)


def _row_to_col(row):
    c = row.shape[1]
    eye = _iota2((c, c), 0) == _iota2((c, c), 1)
    return jnp.sum(jnp.where(eye, jnp.broadcast_to(row, (c, c)), 0.0), axis=1, keepdims=True)


def _last_row(col):
    c = col.shape[0]
    return jnp.sum(jnp.where(_iota2((c, 1), 0) == c - 1, col, 0.0), axis=0, keepdims=True)


def _chunk_cumsum(x):
    pos = _iota2(x.shape, 0) & (DN_CHUNK - 1)
    d = 1
    while d < DN_CHUNK:
        x = x + jnp.where(pos >= d, pltpu.roll(x, d, 0), 0.0)
        d *= 2
    return x


def _chunk_rev_cumsum(x):
    n = x.shape[0]
    pos = _iota2(x.shape, 0) & (DN_CHUNK - 1)
    d = 1
    while d < DN_CHUNK:
        x = x + jnp.where(pos < DN_CHUNK - d, pltpu.roll(x, n - d, 0), 0.0)
        d *= 2
    return x


def _tri_inv(low):
    c = low.shape[0]
    eye = (_iota2((c, c), 0) == _iota2((c, c), 1)).astype(F32)
    m = -low
    p = eye + m
    steps = int(math.log2(c)) - 1
    for _ in range(steps):
        m = _nn(m, m, hi=True)
        p = p + _nn(p, m, hi=True)
    return p


_DN_SCALE = (HEAD ** -0.5, 1.0, None)


def _dn_act(c, scale):
    sig = _sigmoid(c)
    a = c * sig
    if scale is None:
        return a, sig, None, None
    r = lax.rsqrt(jnp.sum(a * a, axis=-1, keepdims=True) + EPS)
    return a * r * scale, sig, a * r, r


def _dn_gates(ba_ref, hs_ref):
    beta = _sigmoid(ba_ref[0])
    sp_arg = ba_ref[1] + hs_ref[1]
    a_exp = jnp.exp(hs_ref[0])
    g = -a_exp * _softplus(sp_arg)
    return beta, g, sp_arg, a_exp


def _dn_inputs(pre_ref, cw_ref, ba_ref, hs_ref, act_sc, b_sc, gc_sc, c_sc=None):
    for idx in range(3):
        c = _conv_fwd(pre_ref[idx], [cw_ref[idx, k:k + 1, :] for k in range(4)])
        if c_sc is not None:
            c_sc[idx] = c
        act_sc[idx] = _dn_act(c, _DN_SCALE[idx])[0]
    beta, g, _, _ = _dn_gates(ba_ref, hs_ref)
    b_sc[...] = beta
    gc_sc[...] = _chunk_cumsum(g)


def _dn_chunk_math(q, k, v, b, gcc):
    c = q.shape[0]
    tril = _iota2((c, c), 0) >= _iota2((c, c), 1)
    strict = _iota2((c, c), 0) > _iota2((c, c), 1)
    eg = jnp.exp(gcc)
    kb, vb = k * b, v * b
    kbg = kb * eg
    dm = jnp.exp(jnp.where(tril, jnp.broadcast_to(gcc, (c, c)) - _col_to_row(gcc), NEG))
    kk = _nt(kb, k)
    t = _tri_inv(jnp.where(strict, kk * dm, 0.0))
    glast = _last_row(gcc)
    ekd = jnp.exp(glast - gcc)
    qk = _nt(q, k)
    return dict(tril=tril, strict=strict, eg=eg, kb=kb, vb=vb, kbg=kbg, dm=dm, kk=kk, t=t, glast=glast, ekd=ekd,
                kd=k * ekd, qk=qk, amat=jnp.where(tril, qk * dm, 0.0), qg=q * eg,
                egl=jnp.broadcast_to(jnp.exp(glast), (c, 1)))


DN_UNROLL = 4


def _chunk_loop(nc, chunk):
    u = math.gcd(nc, DN_UNROLL)

    def step(i, carry):
        for j in range(u):
            chunk(i * u + j)
        return carry

    lax.fori_loop(0, nc // u, step, 0)


def _dn_specs(seq):
    s64 = lambda lead: pl.BlockSpec((lead, None, None, seq, HEAD), lambda b, h: (0, b, h, 0, 0))
    s1 = lambda lead: pl.BlockSpec((lead, None, None, seq, 1), lambda b, h: (0, b, h, 0, 0))
    one64 = pl.BlockSpec((None, None, seq, HEAD), lambda b, h: (b, h, 0, 0))
    one1 = pl.BlockSpec((None, None, seq, 1), lambda b, h: (b, h, 0, 0))
    cw = pl.BlockSpec((None, 3, 4, HEAD), lambda b, h: (h, 0, 0, 0))
    hs = pl.BlockSpec((None, 2, 1, 1), lambda b, h: (h, 0, 0, 0))
    return s64, s1, one64, one1, cw, hs


def dn_prep(pre, cw, ba, hs, *, name):
    _, nb, nh, seq, _ = pre.shape
    nc = seq // DN_CHUNK

    def body(pre_ref, cw_ref, ba_ref, hs_ref, loc_ref, egl_ref, act_sc, b_sc, gc_sc):
        _dn_inputs(pre_ref, cw_ref, ba_ref, hs_ref, act_sc, b_sc, gc_sc)

        def chunk(c):
            rows = pl.ds(pl.multiple_of(c * DN_CHUNK, DN_CHUNK), DN_CHUNK)
            m = _dn_chunk_math(act_sc[0, rows, :], act_sc[1, rows, :], act_sc[2, rows, :], b_sc[rows, :], gc_sc[rows, :])
            loc_ref[0, rows, :] = m["qg"]
            loc_ref[1, rows, :] = m["kd"]
            loc_ref[2, rows, :] = _nn(m["t"], m["vb"])
            loc_ref[3, rows, :] = _nn(m["t"], m["kbg"])
            loc_ref[4, rows, :] = m["amat"]
            egl_ref[rows, :] = m["egl"]

        _chunk_loop(nc, chunk)

    s64, s1, one64, one1, cwspec, hsspec = _dn_specs(seq)
    return pl.pallas_call(
        body, name=name, grid=(nb, nh), in_specs=[s64(3), cwspec, s1(2), hsspec], out_specs=[s64(5), one1],
        out_shape=[SDS((5, nb, nh, seq, HEAD), F32), SDS((nb, nh, seq, 1), F32)],
        scratch_shapes=[pltpu.VMEM((3, seq, HEAD), F32)] + [pltpu.VMEM((seq, 1), F32)] * 2,
        compiler_params=_params(("parallel", "parallel")))(pre, cw, ba, hs)


def _gated_norm(o, z, gn):
    r = lax.rsqrt(jnp.mean(o * o, axis=-1, keepdims=True) + EPS)
    sig = _sigmoid(z)
    return o * r, sig, r


def dn_scan(loc, egl, z, gn, *, name):
    _, nb, nh, seq, _ = loc.shape
    nc = seq // DN_CHUNK

    def body(loc_ref, egl_ref, z_ref, gn_ref, y_ref, o_ref, vn_ref, st_ref):
        gn = gn_ref[...]

        def step(c, state):
            rows = pl.ds(pl.multiple_of(c * DN_CHUNK, DN_CHUNK), DN_CHUNK)
            st_ref[rows, :] = state
            vn = loc_ref[2, rows, :] - _nn(loc_ref[3, rows, :], state)
            o = _nn(loc_ref[0, rows, :], state) + _nn(loc_ref[4, rows, :], vn)
            vn_ref[rows, :] = vn
            o_ref[rows, :] = o
            zz = z_ref[rows, :]
            on, sig, _ = _gated_norm(o, zz, gn)
            y_ref[rows, :] = on * gn * (zz * sig)
            return state * egl_ref[rows, :] + _tn(loc_ref[1, rows, :], vn)

        lax.fori_loop(0, nc, step, jnp.zeros((HEAD, HEAD), F32))

    s64, s1, one64, one1, cwspec, hsspec = _dn_specs(seq)
    out = SDS((nb, nh, seq, HEAD), F32)
    return pl.pallas_call(
        body, name=name, grid=(nb, nh), in_specs=[s64(5), one1, one64, _whole((1, HEAD))],
        out_specs=[one64] * 4, out_shape=[out] * 4,
        compiler_params=_params(("parallel", "parallel")))(loc, egl, z, gn)


def dn_scan_bwd(loc, egl, z, gn, o, vn, states, dy, *, name):
    _, nb, nh, seq, _ = loc.shape
    nc = seq // DN_CHUNK

    def body(loc_ref, egl_ref, z_ref, gn_ref, o_ref, vn_ref, st_ref, dy_ref, dloc_ref, degl_ref, dz_ref, dgn_ref):
        @pl.when((pl.program_id(0) == 0) & (pl.program_id(1) == 0))
        def _():
            dgn_ref[...] = jnp.zeros_like(dgn_ref)

        gn = gn_ref[...]
        tril = _iota2((DN_CHUNK, DN_CHUNK), 0) >= _iota2((DN_CHUNK, DN_CHUNK), 1)

        def step(i, carry):
            ds, dgn = carry
            rows = pl.ds(pl.multiple_of((nc - 1 - i) * DN_CHUNK, DN_CHUNK), DN_CHUNK)
            dy, zz, oo = dy_ref[rows, :], z_ref[rows, :], o_ref[rows, :]
            on, sig, r = _gated_norm(oo, zz, gn)
            sz = zz * sig
            dz_ref[rows, :] = dy * on * gn * (sig * (1.0 + zz * (1.0 - sig)))
            dgn = dgn + jnp.sum(dy * on * sz, axis=0, keepdims=True)
            don = dy * gn * sz
            do = r * (don - on * jnp.mean(don * on, axis=-1, keepdims=True))
            state, vnew = st_ref[rows, :], vn_ref[rows, :]
            qg, kd, w, amat = loc_ref[0, rows, :], loc_ref[1, rows, :], loc_ref[3, rows, :], loc_ref[4, rows, :]
            dvn = _tn(amat, do) + _nn(kd, ds)
            dloc_ref[0, rows, :] = _nt(do, state)
            dloc_ref[1, rows, :] = _nt(vnew, ds)
            dloc_ref[2, rows, :] = dvn
            dloc_ref[3, rows, :] = -_nt(dvn, state)
            dloc_ref[4, rows, :] = jnp.where(tril, _nt(do, vnew), 0.0)
            degl = jnp.sum(jnp.sum(state * ds, axis=1, keepdims=True), axis=0, keepdims=True)
            degl_ref[rows, :] = jnp.broadcast_to(degl, (DN_CHUNK, 1))
            return ds * egl_ref[rows, :] + _tn(qg, do) - _tn(w, dvn), dgn

        _, dgn = lax.fori_loop(0, nc, step, (jnp.zeros((HEAD, HEAD), F32), jnp.zeros((1, HEAD), F32)))
        dgn_ref[...] += dgn

    s64, s1, one64, one1, cwspec, hsspec = _dn_specs(seq)
    return pl.pallas_call(
        body, name=name, grid=(nb, nh),
        in_specs=[s64(5), one1, one64, _whole((1, HEAD)), one64, one64, one64, one64],
        out_specs=[s64(5), one1, one64, _whole((1, HEAD))],
        out_shape=[SDS((5, nb, nh, seq, HEAD), F32), SDS((nb, nh, seq, 1), F32), SDS((nb, nh, seq, HEAD), F32),
                   SDS((1, HEAD), F32)],
        compiler_params=_params(("arbitrary", "arbitrary")))(loc, egl, z, gn, o, vn, states, dy)


def dn_prep_bwd(pre, cw, ba, hs, dloc, degl, *, name):
    _, nb, nh, seq, _ = pre.shape
    nc = seq // DN_CHUNK

    def body(pre_ref, cw_ref, ba_ref, hs_ref, dloc_ref, degl_ref, dpre_ref, dba_ref, dcw_ref, dhs_ref,
             act_sc, b_sc, gc_sc, c_sc):
        @pl.when(pl.program_id(1) == 0)
        def _():
            dcw_ref[...] = jnp.zeros_like(dcw_ref)
            dhs_ref[...] = jnp.zeros_like(dhs_ref)

        _dn_inputs(pre_ref, cw_ref, ba_ref, hs_ref, act_sc, b_sc, gc_sc, c_sc)

        def chunk(c):
            rows = pl.ds(pl.multiple_of(c * DN_CHUNK, DN_CHUNK), DN_CHUNK)
            q, k, v, b, gcc = act_sc[0, rows, :], act_sc[1, rows, :], act_sc[2, rows, :], b_sc[rows, :], gc_sc[rows, :]
            m = _dn_chunk_math(q, k, v, b, gcc)
            dqg, dkd, du, dw, da = (dloc_ref[x, rows, :] for x in range(5))
            t, dm, eg = m["t"], m["dm"], m["eg"]
            dt = _nt(du, m["vb"]) + _nt(dw, m["kbg"])
            dvb, dkbg = _tn(t, du), _tn(t, dw)
            dl = jnp.where(m["strict"], -_tn(t, _nt(dt, t, hi=True), hi=True), 0.0)
            dkk = dl * dm
            dqk = da * dm
            dd = dl * m["kk"] + da * m["qk"]
            dkb = _nn(dkk, k) + dkbg * eg
            dq = _nn(dqk, k) + dqg * eg
            dk = _tn(dkk, m["kb"]) + _tn(dqk, q) + dkd * m["ekd"] + dkb * b
            db = jnp.sum(dkb * k, axis=-1, keepdims=True) + jnp.sum(dvb * v, axis=-1, keepdims=True)
            mx = jnp.where(m["tril"], dd * dm, 0.0)
            tk = jnp.sum(dkd * m["kd"], axis=-1, keepdims=True)
            dgc = (jnp.sum(mx, axis=-1, keepdims=True) - _row_to_col(jnp.sum(mx, axis=0, keepdims=True))
                   + jnp.sum(dqg * m["qg"], axis=-1, keepdims=True) + jnp.sum(dkbg * m["kbg"], axis=-1, keepdims=True) - tk)
            dglast = jnp.sum(tk, axis=0, keepdims=True) + _last_row(degl_ref[rows, :]) * jnp.exp(m["glast"])
            act_sc[0, rows, :] = dq
            act_sc[1, rows, :] = dk
            act_sc[2, rows, :] = dvb * b
            b_sc[rows, :] = db
            gc_sc[rows, :] = dgc + jnp.where(_iota2((DN_CHUNK, 1), 0) == DN_CHUNK - 1, dglast, 0.0)

        _chunk_loop(nc, chunk)

        beta, g, sp_arg, a_exp = _dn_gates(ba_ref, hs_ref)
        dg = _chunk_rev_cumsum(gc_sc[...])
        dal = dg * (-a_exp) * _sigmoid(sp_arg)
        dba_ref[0] = b_sc[...] * beta * (1.0 - beta)
        dba_ref[1] = dal
        dhs_ref[0] += jnp.sum(dg * g, axis=0, keepdims=True)
        dhs_ref[1] += jnp.sum(dal, axis=0, keepdims=True)
        for idx in range(3):
            c = c_sc[idx]
            _, sig, hat, r = _dn_act(c, _DN_SCALE[idx])
            da_ = act_sc[idx]
            if _DN_SCALE[idx] is not None:
                da_ = da_ * _DN_SCALE[idx]
                da_ = r * (da_ - hat * jnp.sum(da_ * hat, axis=-1, keepdims=True))
            dx, dcw = _conv_bwd(da_ * (sig * (1.0 + c * (1.0 - sig))), pre_ref[idx],
                                [cw_ref[idx, k:k + 1, :] for k in range(4)])
            dpre_ref[idx] = dx
            dcw_ref[idx] += dcw

    s64, s1, one64, one1, cwspec, hsspec = _dn_specs(seq)
    swap = lambda spec: pl.BlockSpec(spec.block_shape, lambda h, b, _f=spec.index_map: _f(b, h))
    return pl.pallas_call(
        body, name=name, grid=(nh, nb),
        in_specs=[swap(s64(3)), swap(cwspec), swap(s1(2)), swap(hsspec), swap(s64(5)), swap(one1)],
        out_specs=[swap(s64(3)), swap(s1(2)), swap(cwspec), swap(hsspec)],
        out_shape=[SDS((3, nb, nh, seq, HEAD), F32), SDS((2, nb, nh, seq, 1), F32), SDS((nh, 3, 4, HEAD), F32),
                   SDS((nh, 2, 1, 1), F32)],
        scratch_shapes=[pltpu.VMEM((3, seq, HEAD), F32)] + [pltpu.VMEM((seq, 1), F32)] * 2 + [pltpu.VMEM((3, seq, HEAD), F32)],
        compiler_params=_params(("arbitrary", "arbitrary")))(pre, cw, ba, hs, dloc, degl)


COL_Q, COL_K, COL_V = 512 // 128, 1024 // 128, 1152 // 128
COL_DNQ, COL_DNK, COL_DNV, COL_DNZ, COL_BA = 1280 // 128, 1536 // 128, 1792 // 128, 2048 // 128, 2304 // 128


def _lane_a(shape):
    return _iota2(shape, 1) < HEAD


def _bd(x):
    la = _lane_a(x.shape)
    return jnp.concatenate([jnp.where(la, x, 0.0), jnp.where(la, 0.0, x)], axis=0)


def _fold(m):
    return m[:HEAD] + m[HEAD:]


def _bd_mask():
    return (_iota2((2 * HEAD, 2 * HEAD), 0) < HEAD) == (_iota2((2 * HEAD, 2 * HEAD), 1) < HEAD)


def _pk_nn(x, y, hi=False):
    return _nn(x, _bd(y), hi)


def _pk_nt(u, v, hi=False):
    return _nt(u, _bd(v), hi)


def _pk_tn(x, y, hi=False):
    return _fold(jnp.where(_bd_mask(), _tn(x, y, hi), 0.0))


def _half_sum(x):
    la = _lane_a(x.shape)
    return jnp.where(la, jnp.sum(jnp.where(la, x, 0.0), axis=-1, keepdims=True),
                     jnp.sum(jnp.where(la, 0.0, x), axis=-1, keepdims=True))


def _lane_col(x, idx):
    return jnp.sum(jnp.where(_iota2(x.shape, 1) == idx, x, 0.0), axis=-1, keepdims=True)


def _row0(x):
    return jnp.max(x, axis=0, keepdims=True)


def _dup_kv(x, g):
    la = _lane_a(x.shape)
    rolled = pltpu.roll(x, HEAD, 1)
    return jnp.where(la, x, rolled) if g == 0 else jnp.where(la, rolled, x)


def _stack_heads(ref, g):
    la = _lane_a((BLOCK_Q, 2 * HEAD))
    parts = []
    for hh in range(ATT_GROUP):
        pair = ref[:, pl.ds(2 * HEAD * (2 * g + hh // 2), 2 * HEAD)]
        parts.append(jnp.where(la if hh % 2 == 0 else ~la, pair, 0.0))
    return jnp.concatenate(parts, axis=0)


def _unstack_heads(stack, ref, g):
    la = _lane_a((BLOCK_Q, 2 * HEAD))
    for j in range(2):
        top = stack[2 * j * BLOCK_Q:(2 * j + 1) * BLOCK_Q]
        bot = stack[(2 * j + 1) * BLOCK_Q:(2 * j + 2) * BLOCK_Q]
        ref[:, pl.ds(2 * HEAD * (2 * g + j), 2 * HEAD)] = jnp.where(la, top, bot)


def _swa_probs(q_ref, k_ref, v_ref, b_ref, s_ref, n, g):
    rows = ATT_GROUP * BLOCK_Q
    prev = pl.multiple_of(jnp.maximum(n - 1, 0) * BLOCK_Q, BLOCK_Q)
    cur = pl.multiple_of(n * BLOCK_Q, BLOCK_Q)
    kp, kc = _dup_kv(k_ref[pl.ds(prev, BLOCK_Q), :], g), _dup_kv(k_ref[pl.ds(cur, BLOCK_Q), :], g)
    vp, vc = _dup_kv(v_ref[pl.ds(prev, BLOCK_Q), :], g), _dup_kv(v_ref[pl.ds(cur, BLOCK_Q), :], g)
    qs = _stack_heads(q_ref, g) * (HEAD ** -0.5)
    bias = b_ref[pl.ds(ATT_GROUP * g, ATT_GROUP)].reshape(rows, BLOCK_Q)
    i = _iota2((rows, BLOCK_Q), 0) & (BLOCK_Q - 1)
    j = _iota2((rows, BLOCK_Q), 1)
    s_p = jnp.where((j > i) & (n > 0), _nt(qs, kp) + bias, NEG)
    s_c = jnp.where(j <= i, _nt(qs, kc) + bias, NEG)
    sink = s_ref[pl.ds(rows * g, rows), :]
    m = jnp.maximum(jnp.maximum(jnp.max(s_p, axis=-1, keepdims=True), jnp.max(s_c, axis=-1, keepdims=True)), sink)
    e_p, e_c, e_s = jnp.exp(s_p - m), jnp.exp(s_c - m), jnp.exp(sink - m)
    inv = 1.0 / (jnp.sum(e_p, axis=-1, keepdims=True) + jnp.sum(e_c, axis=-1, keepdims=True) + e_s)
    return e_p * inv, e_c * inv, e_s * inv, qs, kp, kc, vp, vc, prev, cur


def _swa_specs(seq):
    nblk = seq // BLOCK_Q
    qspec = pl.BlockSpec((BLOCK_Q, ATT_W), lambda b, n: (b * nblk + n, COL_Q * 128 // ATT_W))
    kspec = pl.BlockSpec((seq, 2 * HEAD), lambda b, n: (b, COL_K))
    vspec = pl.BlockSpec((seq, 2 * HEAD), lambda b, n: (b, COL_V))
    ospec = pl.BlockSpec((BLOCK_Q, ATT_W), lambda b, n: (b * nblk + n, 0))
    kvout = pl.BlockSpec((seq, 2 * HEAD), lambda b, n: (b, 0))
    return qspec, kspec, vspec, ospec, kvout, _whole((ATT_HEADS, BLOCK_Q, BLOCK_Q)), _whole((ATT_HEADS * BLOCK_Q, 1))


def swa_fwd(u, bias, sink_rows, *, seq, name):
    t = u.shape[0]

    def body(q_ref, k_ref, v_ref, b_ref, s_ref, o_ref):
        for g in range(KV_HEADS):
            p_p, p_c, _, _, _, _, vp, vc, _, _ = _swa_probs(q_ref, k_ref, v_ref, b_ref, s_ref, pl.program_id(1), g)
            _unstack_heads(_nn(p_p, vp) + _nn(p_c, vc), o_ref, g)

    qspec, kspec, vspec, ospec, kvout, bspec, sspec = _swa_specs(seq)
    return pl.pallas_call(
        body, name=name, grid=(t // seq, seq // BLOCK_Q), in_specs=[qspec, kspec, vspec, bspec, sspec], out_specs=ospec,
        out_shape=SDS((t, ATT_W), F32), compiler_params=_params(("parallel", "arbitrary")))(u, u, u, bias, sink_rows)


def swa_bwd(u, bias, sink_rows, do, *, seq, name):
    t = u.shape[0]

    def body(q_ref, k_ref, v_ref, b_ref, s_ref, do_ref, dq_ref, dk_ref, dv_ref, db_ref, ds_ref):
        b, n = pl.program_id(0), pl.program_id(1)

        @pl.when((b == 0) & (n == 0))
        def _():
            db_ref[...] = jnp.zeros_like(db_ref)
            ds_ref[...] = jnp.zeros_like(ds_ref)

        @pl.when(n == 0)
        def _():
            dk_ref[...] = jnp.zeros_like(dk_ref)
            dv_ref[...] = jnp.zeros_like(dv_ref)

        la = _lane_a((BLOCK_Q, 2 * HEAD))
        for g in range(KV_HEADS):
            p_p, p_c, p_s, qs, kp, kc, vp, vc, prev, cur = _swa_probs(q_ref, k_ref, v_ref, b_ref, s_ref, n, g)
            do = _stack_heads(do_ref, g)
            dp_p, dp_c = _nt(do, vp), _nt(do, vc)
            delta = jnp.sum(p_p * dp_p, axis=-1, keepdims=True) + jnp.sum(p_c * dp_c, axis=-1, keepdims=True)
            ds_p, ds_c = p_p * (dp_p - delta), p_c * (dp_c - delta)
            _unstack_heads((_nn(ds_p, kp) + _nn(ds_c, kc)) * (HEAD ** -0.5), dq_ref, g)
            mine = la if g == 0 else ~la

            def to_head(x):
                return jnp.where(mine, x + pltpu.roll(x, HEAD, 1), 0.0)

            dk_ref[pl.ds(prev, BLOCK_Q), :] += to_head(_tn(ds_p, qs))
            dk_ref[pl.ds(cur, BLOCK_Q), :] += to_head(_tn(ds_c, qs))
            dv_ref[pl.ds(prev, BLOCK_Q), :] += to_head(_tn(p_p, do))
            dv_ref[pl.ds(cur, BLOCK_Q), :] += to_head(_tn(p_c, do))
            db_ref[pl.ds(ATT_GROUP * g, ATT_GROUP)] += (ds_p + ds_c).reshape(ATT_GROUP, BLOCK_Q, BLOCK_Q)
            rows = ATT_GROUP * BLOCK_Q
            ds_ref[pl.ds(rows * g, rows), :] += -p_s * delta

    qspec, kspec, vspec, ospec, kvout, bspec, sspec = _swa_specs(seq)
    return pl.pallas_call(
        body, name=name, grid=(t // seq, seq // BLOCK_Q), in_specs=[qspec, kspec, vspec, bspec, sspec, ospec],
        out_specs=[ospec, kvout, kvout, bspec, sspec],
        out_shape=[SDS((t, ATT_W), F32), SDS((t, 2 * HEAD), F32), SDS((t, 2 * HEAD), F32),
                   SDS((ATT_HEADS, BLOCK_Q, BLOCK_Q), F32), SDS((ATT_HEADS * BLOCK_Q, 1), F32)],
        compiler_params=_params(("arbitrary", "arbitrary")))(u, u, u, bias, sink_rows, do)


def _gdn_gates(ba_ref, alog_ref, dt_ref, hp):
    blk = ba_ref[...]
    beta_blk = _sigmoid(blk)
    sp_arg = blk + dt_ref[...]
    a_exp = jnp.exp(alog_ref[...])
    g_blk = -a_exp * _softplus(sp_arg)
    la = _lane_a(blk.shape)
    ha = 2 * hp
    beta = jnp.where(la, _lane_col(beta_blk, ha), _lane_col(beta_blk, ha + 1))
    g = jnp.where(la, _lane_col(g_blk, DN_HEADS + ha), _lane_col(g_blk, DN_HEADS + ha + 1))
    return beta, g, beta_blk, sp_arg, a_exp, g_blk


def _gdn_act(c, scale):
    sig = _sigmoid(c)
    a = c * sig
    if scale is None:
        return a, sig, None, None
    r = lax.rsqrt(_half_sum(a * a) + EPS)
    return a * r * scale, sig, a * r, r


def _gdn_inputs(pre_refs, cw_refs, ba_ref, alog_ref, dt_ref, hp, act_sc, b_sc, gc_sc, c_sc=None):
    for idx in range(3):
        c = _conv_fwd(pre_refs[idx][...], [cw_refs[idx][k:k + 1, :] for k in range(4)])
        if c_sc is not None:
            c_sc[idx] = c
        act_sc[idx] = _gdn_act(c, _DN_SCALE[idx])[0]
    beta, g = _gdn_gates(ba_ref, alog_ref, dt_ref, hp)[:2]
    b_sc[...] = beta
    gc_sc[...] = _chunk_cumsum(g)


def _gdn_chunk(q, k, v, b, gcc):
    shape = q.shape
    row, lm = _iota2(shape, 0), _iota2(shape, 1) & (HEAD - 1)
    tril, strict, eye = row >= lm, row > lm, row == lm
    eg = jnp.exp(gcc)
    kb, vb = k * b, v * b
    kbg = kb * eg
    grow = jnp.sum(jnp.where(eye, gcc, 0.0), axis=0, keepdims=True)
    dm = jnp.exp(jnp.where(tril, gcc - grow, NEG))
    kk = _pk_nt(kb, k)
    m = -jnp.where(strict, kk * dm, 0.0)
    t = eye.astype(F32) + m
    for _ in range(int(math.log2(HEAD)) - 1):
        m = _pk_nn(m, m, hi=True)
        t = t + _pk_nn(t, m, hi=True)
    glast = jnp.sum(jnp.where(row == DN_CHUNK - 1, gcc, 0.0), axis=0, keepdims=True)
    ekd = jnp.exp(glast - gcc)
    qk = _pk_nt(q, k)
    return dict(tril=tril, strict=strict, eye=eye, row=row, eg=eg, kb=kb, vb=vb, kbg=kbg, dm=dm, kk=kk, t=t, glast=glast,
                ekd=ekd, kd=k * ekd, qk=qk, amat=jnp.where(tril, qk * dm, 0.0), qg=q * eg,
                egl=jnp.broadcast_to(jnp.exp(glast), shape))


def _gdn_in_specs(seq):
    u_at = lambda col: pl.BlockSpec((seq, 2 * HEAD), lambda b, hp, _c=col: (b, _c + hp))
    cw_at = lambda col: pl.BlockSpec((4, 2 * HEAD), lambda b, hp, _c=col: (0, _c + hp))
    row = pl.BlockSpec((1, 2 * HEAD), lambda b, hp: (0, 0))
    ba = pl.BlockSpec((seq, 2 * HEAD), lambda b, hp: (b, COL_BA))
    return [u_at(COL_DNQ), u_at(COL_DNK), u_at(COL_DNV), ba, cw_at(0), cw_at(2), cw_at(4), row, row]


def _pair(seq, lead=None):
    if lead is None:
        return pl.BlockSpec((seq, 2 * HEAD), lambda b, hp: (b, hp))
    return pl.BlockSpec((lead, seq, 2 * HEAD), lambda b, hp: (0, b, hp))


def _swap(spec):
    return pl.BlockSpec(spec.block_shape, lambda hp, b, _f=spec.index_map: _f(b, hp))


def gdn_prep(u, cw, alog_row, dt_row, *, seq, name):
    t = u.shape[0]
    nc = seq // DN_CHUNK

    def body(q_ref, k_ref, v_ref, ba_ref, cq_ref, ck_ref, cv_ref, alog_ref, dt_ref, loc_ref, egl_ref, act_sc, b_sc, gc_sc):
        _gdn_inputs((q_ref, k_ref, v_ref), (cq_ref, ck_ref, cv_ref), ba_ref, alog_ref, dt_ref, pl.program_id(1),
                    act_sc, b_sc, gc_sc)

        def chunk(c):
            rows = pl.ds(pl.multiple_of(c * DN_CHUNK, DN_CHUNK), DN_CHUNK)
            m = _gdn_chunk(act_sc[0, rows, :], act_sc[1, rows, :], act_sc[2, rows, :], b_sc[rows, :], gc_sc[rows, :])
            loc_ref[0, rows, :] = m["qg"]
            loc_ref[1, rows, :] = m["kd"]
            loc_ref[2, rows, :] = _pk_nn(m["t"], m["vb"])
            loc_ref[3, rows, :] = _pk_nn(m["t"], m["kbg"])
            loc_ref[4, rows, :] = m["amat"]
            egl_ref[rows, :] = m["egl"]

        _chunk_loop(nc, chunk)

    return pl.pallas_call(
        body, name=name, grid=(t // seq, DN_HEADS // 2), in_specs=_gdn_in_specs(seq), out_specs=[_pair(seq, 5), _pair(seq)],
        out_shape=[SDS((5, t, DN_HEADS * HEAD), F32), SDS((t, DN_HEADS * HEAD), F32)],
        scratch_shapes=[pltpu.VMEM((3, seq, 2 * HEAD), F32)] + [pltpu.VMEM((seq, 2 * HEAD), F32)] * 2,
        compiler_params=_params(("parallel", "parallel")))(u, u, u, u, cw, cw, cw, alog_row, dt_row)


def _gated_norm2(o, z, gn):
    r = lax.rsqrt(_half_sum(o * o) * (1.0 / HEAD) + EPS)
    return o * r, _sigmoid(z), r


def gdn_scan(loc, egl, u, gn, *, seq, name):
    t = u.shape[0]
    nc = seq // DN_CHUNK

    def body(loc_ref, egl_ref, z_ref, gn_ref, y_ref, o_ref, vn_ref, st_ref):
        gn = gn_ref[...]
        bdm = _bd_mask()

        def step(c, state):
            rows = pl.ds(pl.multiple_of(c * DN_CHUNK, DN_CHUNK), DN_CHUNK)
            st_ref[rows, :] = _fold(state)
            vn = loc_ref[2, rows, :] - _nn(loc_ref[3, rows, :], state)
            o = _nn(loc_ref[0, rows, :], state) + _pk_nn(loc_ref[4, rows, :], vn)
            vn_ref[rows, :] = vn
            o_ref[rows, :] = o
            zz = z_ref[rows, :]
            on, sig, _ = _gated_norm2(o, zz, gn)
            y_ref[rows, :] = on * gn * (zz * sig)
            return state * _row0(egl_ref[rows, :]) + jnp.where(bdm, _tn(loc_ref[1, rows, :], vn), 0.0)

        lax.fori_loop(0, nc, step, jnp.zeros((2 * HEAD, 2 * HEAD), F32))

    zspec = pl.BlockSpec((seq, 2 * HEAD), lambda b, hp: (b, COL_DNZ + hp))
    out = SDS((t, DN_HEADS * HEAD), F32)
    return pl.pallas_call(
        body, name=name, grid=(t // seq, DN_HEADS // 2), in_specs=[_pair(seq, 5), _pair(seq), zspec, _whole((1, 2 * HEAD))],
        out_specs=[_pair(seq)] * 4, out_shape=[out] * 4,
        compiler_params=_params(("parallel", "parallel")))(loc, egl, u, gn)


def gdn_scan_bwd(loc, egl, u, gn, o, vn, states, dy, *, seq, name):
    t = u.shape[0]
    nc = seq // DN_CHUNK

    def body(loc_ref, egl_ref, z_ref, gn_ref, o_ref, vn_ref, st_ref, dy_ref, dloc_ref, degl_ref, dz_ref, dgn_ref):
        @pl.when((pl.program_id(0) == 0) & (pl.program_id(1) == 0))
        def _():
            dgn_ref[...] = jnp.zeros_like(dgn_ref)

        gn = gn_ref[...]
        bdm = _bd_mask()
        shape = (DN_CHUNK, 2 * HEAD)
        tril = _iota2(shape, 0) >= (_iota2(shape, 1) & (HEAD - 1))

        def step(i, carry):
            ds, dgn = carry
            rows = pl.ds(pl.multiple_of((nc - 1 - i) * DN_CHUNK, DN_CHUNK), DN_CHUNK)
            dy, zz, oo = dy_ref[rows, :], z_ref[rows, :], o_ref[rows, :]
            on, sig, r = _gated_norm2(oo, zz, gn)
            sz = zz * sig
            dz_ref[rows, :] = dy * on * gn * (sig * (1.0 + zz * (1.0 - sig)))
            dgn = dgn + jnp.sum(dy * on * sz, axis=0, keepdims=True)
            don = dy * gn * sz
            do = r * (don - on * _half_sum(don * on) * (1.0 / HEAD))
            state, vnew = _bd(st_ref[rows, :]), vn_ref[rows, :]
            qg, kd, w, amat = loc_ref[0, rows, :], loc_ref[1, rows, :], loc_ref[3, rows, :], loc_ref[4, rows, :]
            dvn = _pk_tn(amat, do) + _nn(kd, ds)
            dloc_ref[0, rows, :] = _nt(do, state)
            dloc_ref[1, rows, :] = _nt(vnew, ds)
            dloc_ref[2, rows, :] = dvn
            dloc_ref[3, rows, :] = -_nt(dvn, state)
            dloc_ref[4, rows, :] = jnp.where(tril, _pk_nt(do, vnew), 0.0)
            degl = _half_sum(jnp.sum(state * ds, axis=0, keepdims=True))
            degl_ref[rows, :] = jnp.broadcast_to(degl, shape)
            grow = jnp.where(bdm, _tn(qg, do) - _tn(w, dvn), 0.0)
            return ds * _row0(egl_ref[rows, :]) + grow, dgn

        _, dgn = lax.fori_loop(0, nc, step, (jnp.zeros((2 * HEAD, 2 * HEAD), F32), jnp.zeros((1, 2 * HEAD), F32)))
        dgn_ref[...] += dgn

    zspec = pl.BlockSpec((seq, 2 * HEAD), lambda b, hp: (b, COL_DNZ + hp))
    one = _pair(seq)
    out = SDS((t, DN_HEADS * HEAD), F32)
    return pl.pallas_call(
        body, name=name, grid=(t // seq, DN_HEADS // 2),
        in_specs=[_pair(seq, 5), one, zspec, _whole((1, 2 * HEAD)), one, one, one, one],
        out_specs=[_pair(seq, 5), one, one, _whole((1, 2 * HEAD))],
        out_shape=[SDS((5, t, DN_HEADS * HEAD), F32), out, out, SDS((1, 2 * HEAD), F32)],
        compiler_params=_params(("arbitrary", "arbitrary")))(loc, egl, u, gn, o, vn, states, dy)


def gdn_prep_bwd(u, cw, alog_row, dt_row, dloc, degl, *, seq, name):
    t = u.shape[0]
    nc = seq // DN_CHUNK

    def body(q_ref, k_ref, v_ref, ba_ref, cq_ref, ck_ref, cv_ref, alog_ref, dt_ref, dloc_ref, degl_ref,
             dqkv_ref, dba_ref, dcw_ref, dhs_ref, act_sc, b_sc, gc_sc, c_sc):
        hp = pl.program_id(0)

        @pl.when(pl.program_id(1) == 0)
        def _():
            dcw_ref[...] = jnp.zeros_like(dcw_ref)
            dhs_ref[...] = jnp.zeros_like(dhs_ref)

        pre_refs, cw_refs = (q_ref, k_ref, v_ref), (cq_ref, ck_ref, cv_ref)
        _gdn_inputs(pre_refs, cw_refs, ba_ref, alog_ref, dt_ref, hp, act_sc, b_sc, gc_sc, c_sc)

        def chunk(c):
            rows = pl.ds(pl.multiple_of(c * DN_CHUNK, DN_CHUNK), DN_CHUNK)
            q, k, v, b, gcc = act_sc[0, rows, :], act_sc[1, rows, :], act_sc[2, rows, :], b_sc[rows, :], gc_sc[rows, :]
            m = _gdn_chunk(q, k, v, b, gcc)
            dqg, dkd, du, dw, da = (dloc_ref[x, rows, :] for x in range(5))
            tt, dm, eg = m["t"], m["dm"], m["eg"]
            dt = _pk_nt(du, m["vb"]) + _pk_nt(dw, m["kbg"])
            dvb, dkbg = _pk_tn(tt, du), _pk_tn(tt, dw)
            dl = jnp.where(m["strict"], -_pk_tn(tt, _pk_nt(dt, tt, hi=True), hi=True), 0.0)
            dkk = dl * dm
            dqk = da * dm
            dd = dl * m["kk"] + da * m["qk"]
            dkb = _pk_nn(dkk, k) + dkbg * eg
            dq = _pk_nn(dqk, k) + dqg * eg
            dk = _pk_tn(dkk, m["kb"]) + _pk_tn(dqk, q) + dkd * m["ekd"] + dkb * b
            db = _half_sum(dkb * k + dvb * v)
            mx = jnp.where(m["tril"], dd * dm, 0.0)
            tk = _half_sum(dkd * m["kd"])
            colsum = jnp.where(m["eye"], jnp.broadcast_to(jnp.sum(mx, axis=0, keepdims=True), mx.shape), 0.0)
            dgc = _half_sum(mx) - _half_sum(colsum) + _half_sum(dqg * m["qg"] + dkbg * m["kbg"]) - tk
            dglast = jnp.sum(tk, axis=0, keepdims=True) + _row0(degl_ref[rows, :]) * jnp.exp(m["glast"])
            act_sc[0, rows, :] = dq
            act_sc[1, rows, :] = dk
            act_sc[2, rows, :] = dvb * b
            b_sc[rows, :] = db
            gc_sc[rows, :] = dgc + jnp.where(m["row"] == DN_CHUNK - 1, dglast, 0.0)

        _chunk_loop(nc, chunk)

        beta, g, beta_blk, sp_arg, a_exp, g_blk = _gdn_gates(ba_ref, alog_ref, dt_ref, hp)
        dg = _chunk_rev_cumsum(gc_sc[...])
        lane = _iota2(beta_blk.shape, 1)
        ha = 2 * hp
        db = b_sc[...]
        at = lambda idx, x_a, x_b: (jnp.where(lane == idx, _lane_col(x_a, 0), 0.0)
                                    + jnp.where(lane == idx + 1, _lane_col(x_b, HEAD), 0.0))
        dg_blk = at(DN_HEADS + ha, dg, dg)
        dal = dg_blk * (-a_exp) * _sigmoid(sp_arg)
        dba_ref[...] = at(ha, db, db) * beta_blk * (1.0 - beta_blk) + dal
        dhs_ref[0:1, :] += jnp.sum(dg_blk * g_blk, axis=0, keepdims=True)
        dhs_ref[1:2, :] += jnp.sum(dal, axis=0, keepdims=True)
        for idx in range(3):
            c = c_sc[idx]
            _, sig, hat, r = _gdn_act(c, _DN_SCALE[idx])
            da_ = act_sc[idx]
            if _DN_SCALE[idx] is not None:
                da_ = da_ * _DN_SCALE[idx]
                da_ = r * (da_ - hat * _half_sum(da_ * hat))
            dx, dcw = _conv_bwd(da_ * (sig * (1.0 + c * (1.0 - sig))), pre_refs[idx][...],
                                [cw_refs[idx][k:k + 1, :] for k in range(4)])
            dqkv_ref[idx] = dx
            dcw_ref[idx] += dcw

    pair = DN_HEADS // 2
    in_specs = [_swap(s) for s in _gdn_in_specs(seq)] + [_swap(_pair(seq, 5)), _swap(_pair(seq))]
    return pl.pallas_call(
        body, name=name, grid=(pair, t // seq), in_specs=in_specs,
        out_specs=[_swap(_pair(seq, 3)), pl.BlockSpec((None, seq, 2 * HEAD), lambda hp, b: (hp, b, 0)),
                   pl.BlockSpec((3, 4, 2 * HEAD), lambda hp, b: (0, 0, hp)),
                   pl.BlockSpec((None, 2, 2 * HEAD), lambda hp, b: (hp, 0, 0))],
        out_shape=[SDS((3, t, DN_HEADS * HEAD), F32), SDS((pair, t, 2 * HEAD), F32), SDS((3, 4, DN_HEADS * HEAD), F32),
                   SDS((pair, 2, 2 * HEAD), F32)],
        scratch_shapes=[pltpu.VMEM((3, seq, 2 * HEAD), F32)] + [pltpu.VMEM((seq, 2 * HEAD), F32)] * 2
        + [pltpu.VMEM((3, seq, 2 * HEAD), F32)],
        compiler_params=_params(("arbitrary", "arbitrary")))(u, u, u, u, cw, cw, cw, alog_row, dt_row, dloc, degl)


def mix_out(y_lru, o, y_dn, w_out, h, *, name, tm=512):
    t, d = h.shape
    tm = min(tm, t)

    def body(a_ref, b_ref, c_ref, w_ref, h_ref, o_ref, y_ref):
        y_ref[:, 0:LRU_W] = a_ref[...].astype(BF16)
        y_ref[:, LRU_W:LRU_W + ATT_W] = b_ref[...].astype(BF16)
        y_ref[:, LRU_W + ATT_W:] = c_ref[...].astype(BF16)
        o_ref[...] = h_ref[...] + _nn(y_ref[...], w_ref[...])

    rows = lambda width: pl.BlockSpec((tm, width), lambda i: (i, 0))
    return pl.pallas_call(
        body, name=name, grid=(t // tm,), in_specs=[rows(LRU_W), rows(ATT_W), rows(LRU_W), _whole((d, d)), rows(d)],
        out_specs=[rows(d), rows(d)], out_shape=[SDS((t, d), F32), SDS((t, d), BF16)],
        compiler_params=_params(("parallel",)))(y_lru, o, y_dn, w_out, h)


def mix_out_bwd(dout, w_out, *, name, tm=512):
    t, d = dout.shape
    tm = min(tm, t)

    def body(d_ref, w_ref, a_ref, b_ref, c_ref):
        dy = _nt(d_ref[...], w_ref[...])
        a_ref[...] = dy[:, 0:LRU_W]
        b_ref[...] = dy[:, LRU_W:LRU_W + ATT_W]
        c_ref[...] = dy[:, LRU_W + ATT_W:]

    rows = lambda width: pl.BlockSpec((tm, width), lambda i: (i, 0))
    return pl.pallas_call(
        body, name=name, grid=(t // tm,), in_specs=[rows(d), _whole((d, d))], out_specs=[rows(LRU_W), rows(ATT_W), rows(LRU_W)],
        out_shape=[SDS((t, LRU_W), F32), SDS((t, ATT_W), F32), SDS((t, LRU_W), F32)],
        compiler_params=_params(("parallel",)))(dout, w_out)


def mix_in_bwd(h, gain, dout, w_in, dx, dgate, dq, dk, dv, dqkv, dz, dba, *, name, tm=512):
    t, d = h.shape
    tm = min(tm, t)

    def body(h_ref, g_ref, do_ref, w_ref, dx_ref, dgate_ref, dq_ref, dk_ref, dv_ref, dqkv_ref, dz_ref, dba_ref,
             dh_ref, dg_ref, du_ref):
        @pl.when(pl.program_id(0) == 0)
        def _():
            dg_ref[...] = jnp.zeros_like(dg_ref)

        off = 0
        for piece in (dx_ref[...], dgate_ref[...], dq_ref[...], dk_ref[...], dv_ref[...], dqkv_ref[0], dqkv_ref[1],
                      dqkv_ref[2], dz_ref[...], dba_ref[0] + dba_ref[1]):
            du_ref[:, off:off + piece.shape[1]] = piece.astype(BF16)
            off += piece.shape[1]
        du_ref[:, off:] = jnp.zeros((tm, D_IN_PAD - off), BF16)
        g = g_ref[...]
        _, xh, r = _rms_fwd(h_ref[...], g)
        dh, dg = _rms_bwd(_nt(du_ref[...], w_ref[...]), xh, r, g)
        dh_ref[...] = do_ref[...] + dh
        dg_ref[...] += dg

    rows = lambda width: pl.BlockSpec((tm, width), lambda i: (i, 0))
    return pl.pallas_call(
        body, name=name, grid=(t // tm,),
        in_specs=[rows(d), _whole((1, d)), rows(d), _whole((d, D_IN_PAD)), rows(LRU_W), rows(LRU_W), rows(ATT_W),
                  rows(2 * HEAD), rows(2 * HEAD), pl.BlockSpec((3, tm, DN_HEADS * HEAD), lambda i: (0, i, 0)),
                  rows(DN_HEADS * HEAD), pl.BlockSpec((2, tm, 2 * HEAD), lambda i: (0, i, 0))],
        out_specs=[rows(d), _whole((1, d)), rows(D_IN_PAD)],
        out_shape=[SDS((t, d), F32), SDS((1, d), F32), SDS((t, D_IN_PAD), BF16)],
        compiler_params=_params(("arbitrary",)))(h, gain, dout, w_in, dx, dgate, dq, dk, dv, dqkv, dz, dba)


def _block_diag(w):
    out = jnp.zeros((LRU_W, LRU_W), w.dtype)
    for h in range(LRU_W // HEAD):
        out = lax.dynamic_update_slice(out, w[h], (h * HEAD, h * HEAD))
    return out


def _diag_blocks(w):
    per = LRU_HALF // HEAD
    return jnp.stack([w[h // per, (h % per) * HEAD:(h % per + 1) * HEAD, (h % per) * HEAD:(h % per + 1) * HEAD]
                      for h in range(LRU_W // HEAD)])


def layer_params(w, l, bias):
    row = lambda a: a[l].reshape(1, -1)
    return dict(
        ffn1_norm=row(w["ffn1_norm"]), ffn1=(w["ffn1_w_gate"][:, l], w["ffn1_w_up"][:, l], w["ffn1_w_down"][:, l]),
        mix_norm=row(w["mix_norm"]), w_in=w["w_in"][l],
        lru=(w["lru_conv_w"][l], row(w["lru_conv_b"]), _block_diag(w["lru_w_a"][l]), row(w["lru_b_a"]),
             _block_diag(w["lru_w_x"][l]), row(w["lru_b_x"]), row(w["lru_lambda"])),
        bias=bias, sink_rows=jnp.repeat(w["attn_sinks"][l], BLOCK_Q).reshape(ATT_HEADS * BLOCK_Q, 1),
        dn_cw=w["dn_conv_w"][l], dn_alog=_ba_row(w["dn_a_log"][l]), dn_dt=_ba_row(w["dn_dt_bias"][l]),
        dn_norm=jnp.tile(row(w["dn_norm"]), (1, 2)), w_out=w["w_out"][l],
        ffn2_norm=row(w["ffn2_norm"]), ffn2=(w["ffn2_w_gate"][:, l], w["ffn2_w_up"][:, l], w["ffn2_w_down"][:, l]),
        ple_norm=row(w["ple_norm"]), ple_w_gate=w["ple_w_gate"][l], ple_w_proj=w["ple_w_proj"][l])


def _ba_row(per_head):
    return jnp.pad(per_head, (DN_HEADS, 2 * HEAD - 2 * DN_HEADS)).reshape(1, 2 * HEAD)


def mixer_fwd(h, p, nb, seq, tag):
    u, n = norm_matmul(h, p["mix_norm"], p["w_in"], name=f"mix_in_{tag}")
    y_lru = lru_fwd(u, *p["lru"], seq=seq, name=f"lru_fwd_{tag}")
    o = swa_fwd(u, p["bias"], p["sink_rows"], seq=seq, name=f"swa_fwd_{tag}")
    loc, egl = gdn_prep(u, p["dn_cw"], p["dn_alog"], p["dn_dt"], seq=seq, name=f"gdn_prep_{tag}")
    y_dn, o_raw, vn, st = gdn_scan(loc, egl, u, p["dn_norm"], seq=seq, name=f"gdn_scan_{tag}")
    out, ycat = mix_out(y_lru, o, y_dn, p["w_out"], h, name=f"mix_out_{tag}")
    return out, dict(h=h, u=u, n=n, loc=loc, egl=egl, o_raw=o_raw, vn=vn, st=st, ycat=ycat)


def mixer_bwd(dout, s, p, nb, seq, tag):
    u = s["u"]
    dy_lru, do, dy_dn = mix_out_bwd(dout, p["w_out"], name=f"mix_out_dx_{tag}")
    g = {"w_out": matmul(s["ycat"], dout, ta=True, name=f"mix_out_dw_{tag}")}
    dx, dgate, dcw, dwa, dwx, dvec = lru_bwd(u, *p["lru"], dy_lru, seq=seq, name=f"lru_bwd_{tag}")
    g.update(lru_conv_w=dcw, lru_conv_b=dvec[0], lru_w_a=_diag_blocks(dwa), lru_b_a=dvec[1], lru_w_x=_diag_blocks(dwx),
             lru_b_x=dvec[2], lru_lambda=dvec[3])
    dq, dk, dv, dbias, dsink = swa_bwd(u, p["bias"], p["sink_rows"], do, seq=seq, name=f"swa_bwd_{tag}")
    g.update(attn_sinks=dsink.reshape(ATT_HEADS, BLOCK_Q).sum(axis=1), bias=dbias)
    dloc, degl, dz, dgn = gdn_scan_bwd(s["loc"], s["egl"], u, p["dn_norm"], s["o_raw"], s["vn"], s["st"], dy_dn, seq=seq,
                                       name=f"gdn_scan_bwd_{tag}")
    dqkv, dba, dcw3, dhs = gdn_prep_bwd(u, p["dn_cw"], p["dn_alog"], p["dn_dt"], dloc, degl, seq=seq,
                                        name=f"gdn_prep_bwd_{tag}")
    dhs = dhs.sum(axis=0)[:, DN_HEADS:2 * DN_HEADS]
    g.update(dn_conv_w=dcw3.transpose(1, 0, 2).reshape(4, 3 * DN_HEADS * HEAD), dn_a_log=dhs[0], dn_dt_bias=dhs[1],
             dn_norm=dgn[0, :HEAD] + dgn[0, HEAD:])
    dh, dgain, du = mix_in_bwd(s["h"], p["mix_norm"], dout, p["w_in"], dx, dgate, dq, dk, dv, dqkv, dz, dba,
                               name=f"mix_in_bwd_{tag}")
    g["w_in"] = matmul(s["n"], du, ta=True, name=f"mix_in_dw_{tag}")
    g["mix_norm"] = dgain[0]
    return dh, g


SHARDED = ("ffn1_w_gate", "ffn1_w_up", "ffn1_w_down", "w_in", "w_out", "ffn2_w_gate", "ffn2_w_up", "ffn2_w_down",
           "ple_w_gate", "ple_w_proj")
PER_LAYER_SMALL = ("ffn1_norm", "mix_norm", "lru_conv_w", "lru_conv_b", "lru_w_a", "lru_b_a", "lru_w_x", "lru_b_x",
                   "lru_lambda", "attn_sinks", "dn_conv_w", "dn_a_log", "dn_dt_bias", "dn_norm", "ffn2_norm", "ple_norm")


def _col_shards(a):
    r, c = a.shape
    return a.reshape(r, N_CHIP, c // N_CHIP).transpose(1, 0, 2)


def local_step(x, p, target, w, bmap, nb, seq):
    bias = relbias_fwd(w["rel_bias"], bmap, name="relbias_fwd")
    h, saved = x, []
    for l in range(N_LAYER):
        pr = layer_params(w, l, bias)
        s = dict(h0=h)
        h = ffn_fwd(h, pr["ffn1_norm"], *pr["ffn1"], name=f"ffn1_fwd_{l}")
        h, s["mix"] = mixer_fwd(h, pr, nb, seq, l)
        s["h2"] = h
        h = ffn_fwd(h, pr["ffn2_norm"], *pr["ffn2"], name=f"ffn2_fwd_{l}")
        s["h3"] = h
        h = ple_fwd(h, pr["ple_norm"], pr["ple_w_gate"], p[l], pr["ple_w_proj"], name=f"ple_fwd_{l}")
        saved.append((pr, s))
    dh, dgf, loss = loss_head(h, w["final_norm"].reshape(1, -1), target, name="loss_head")

    per_layer, dbias = [None] * N_LAYER, None
    for l in reversed(range(N_LAYER)):
        pr, s = saved[l]
        g = {}
        dout = dh
        dh, n, dga, dpp, dg = ple_bwd(s["h3"], pr["ple_norm"], pr["ple_w_gate"], p[l], pr["ple_w_proj"], dout, name=f"ple_bwd_{l}")
        g["ple_norm"] = dg[0]
        g["ple_w_gate"] = matmul(n, dga, ta=True, name=f"ple_dwg_{l}").reshape(N_CHIP, -1, D_MODEL)
        g["ple_w_proj"] = _col_shards(matmul(p[l], dpp, ta=True, name=f"ple_dwp_{l}"))
        for nm, hin in (("ffn2", s["h2"]), ("ffn1", s["h0"])):
            if nm == "ffn1":
                dh, gm = mixer_bwd(dh, s["mix"], pr, nb, seq, l)
                dbias = gm.pop("bias") if dbias is None else dbias + gm.pop("bias")
                gm["w_in"] = _col_shards(gm["w_in"][:, :D_IN])
                gm["w_out"] = gm["w_out"].reshape(N_CHIP, -1, D_MODEL)
                g.update(gm)
            dout = dh
            dh, n, da, db, sact, dg = ffn_bwd_act(hin, pr[nm + "_norm"], dout, *pr[nm], name=f"{nm}_bwd_act_{l}")
            g[nm + "_norm"] = dg[0]
            g[nm + "_w_gate"], g[nm + "_w_up"], g[nm + "_w_down"] = ffn_bwd_w(n, da, db, sact, dout, name=f"{nm}_bwd_w_{l}")
        per_layer[l] = g
    grads = {k: jnp.stack([per_layer[l][k] for l in range(N_LAYER)]) for k in SHARDED + PER_LAYER_SMALL}
    grads["rel_bias"] = relbias_bwd(dbias, bmap, name="relbias_bwd")[:, :ATT_HEADS]
    grads["final_norm"] = dgf[0]
    return loss, dh, grads


HBM_SPEC = pl.BlockSpec(memory_space=pltpu.HBM)


def _place():
    x, y, c = lax.axis_index("x"), lax.axis_index("y"), lax.axis_index("c")
    chips = [(1 - x, y), (x, 1 - y), (1 - x, 1 - y)]
    return x, y, c, 2 * x + y, (x, y, 1 - c), chips, [2 * cx + cy for cx, cy in chips]


def _remote(src, dst, send_sem, recv_sem, to):
    return pltpu.make_async_remote_copy(src_ref=src, dst_ref=dst, send_sem=send_sem, recv_sem=recv_sem, device_id=to,
                                        device_id_type=MESH)


def place_shard(w, chip_arr, dtype, *, name):
    nl, r, c = w.shape
    tr = next(cand for cand in (256, 128, 64, 32, 16, 8, r) if r % cand == 0)

    def body(chip_ref, w_ref, o_ref):
        o_ref[...] = w_ref[...].astype(dtype)

    return pl.pallas_call(
        body, name=name,
        grid_spec=pltpu.PrefetchScalarGridSpec(
            num_scalar_prefetch=1, grid=(nl, r // tr),
            in_specs=[pl.BlockSpec((None, tr, c), lambda l, i, chip: (l, i, 0))],
            out_specs=pl.BlockSpec((None, None, tr, c), lambda l, i, chip: (chip[0], l, i, 0))),
        out_shape=SDS((N_CHIP, nl, r, c), dtype), compiler_params=_params(("parallel", "parallel")))(chip_arr, w)


def allgather_shards(shards, *, name):
    n = len(shards)

    def body(*refs):
        outs = refs[n:2 * n]
        send, recv, fsend, frecv = refs[2 * n:]
        x, y, c, me, sib, chips, cids = _place()
        first, passed = [], []
        for k in range(n):
            for j, chip in enumerate(chips):
                mine = outs[k].at[me, c]
                first.append(_remote(mine, mine, send.at[3 * k + j], recv.at[3 * k + j], (*chip, c)))
                first[-1].start()
        for k in range(n):
            for j in range(3):
                piece = outs[k].at[cids[j], c]
                _remote(piece, piece, send.at[3 * k + j], recv.at[3 * k + j], sib).wait_recv()
                passed.append(_remote(piece, piece, fsend.at[3 * k + j], frecv.at[3 * k + j], sib))
                passed[-1].start()
        for k in range(n):
            for j in range(3):
                piece = outs[k].at[cids[j], 1 - c]
                _remote(piece, piece, fsend.at[3 * k + j], frecv.at[3 * k + j], sib).wait_recv()
        for cp in first + passed:
            cp.wait_send()

    return pl.pallas_call(
        body, name=name, in_specs=[HBM_SPEC] * n, out_specs=[HBM_SPEC] * n,
        out_shape=[SDS(s.shape, s.dtype) for s in shards], input_output_aliases={k: k for k in range(n)},
        scratch_shapes=[pltpu.SemaphoreType.DMA((3 * n,))] * 4)(*shards)


def exchange_layers(gs, *, name):
    n = len(gs)

    def body(*refs):
        ins, outs, (send, recv) = refs[:n], refs[n:2 * n], refs[2 * n:]
        x, y, c, me, sib, chips, cids = _place()
        cps = [_remote(ins[k].at[1 - c], outs[k], send.at[k], recv.at[k], sib) for k in range(n)]
        for cp in cps:
            cp.start()
        for cp in cps:
            cp.wait()

    return pl.pallas_call(
        body, name=name, in_specs=[HBM_SPEC] * n, out_specs=[HBM_SPEC] * n,
        out_shape=[SDS(g.shape[1:], g.dtype) for g in gs], scratch_shapes=[pltpu.SemaphoreType.DMA((n,))] * 2)(*gs)


def reduce_to_shards(ss, *, name):
    n = len(ss)

    def body(*refs):
        ins, outs, (send, recv) = refs[:n], refs[n:2 * n], refs[2 * n:]
        x, y, c, me, sib, chips, cids = _place()
        cps = []
        for k in range(n):
            for j, chip in enumerate(chips):
                cps.append(_remote(ins[k].at[cids[j]], outs[k].at[j], send.at[3 * k + j], recv.at[3 * k + j], (*chip, c)))
                cps[-1].start()
        for k in range(n):
            for j in range(3):
                slot = outs[k].at[j]
                _remote(slot, slot, send.at[3 * k + j], recv.at[3 * k + j], sib).wait_recv()
        for cp in cps:
            cp.wait_send()

    return pl.pallas_call(
        body, name=name, in_specs=[HBM_SPEC] * n, out_specs=[HBM_SPEC] * n,
        out_shape=[SDS((N_CHIP - 1,) + s.shape[1:], s.dtype) for s in ss],
        scratch_shapes=[pltpu.SemaphoreType.DMA((3 * n,))] * 2)(*ss)


def share_layers(fs, *, name):
    n = len(fs)

    def body(*refs):
        outs, (send, recv) = refs[n:2 * n], refs[2 * n:]
        x, y, c, me, sib, chips, cids = _place()
        cps = [_remote(outs[k].at[c], outs[k].at[c], send.at[k], recv.at[k], sib) for k in range(n)]
        for cp in cps:
            cp.start()
        for k in range(n):
            theirs = outs[k].at[1 - c]
            _remote(theirs, theirs, send.at[k], recv.at[k], sib).wait_recv()
        for cp in cps:
            cp.wait_send()

    return pl.pallas_call(
        body, name=name, in_specs=[HBM_SPEC] * n, out_specs=[HBM_SPEC] * n, out_shape=[SDS(f.shape, f.dtype) for f in fs],
        input_output_aliases={k: k for k in range(n)}, scratch_shapes=[pltpu.SemaphoreType.DMA((n,))] * 2)(*fs)


N_DEV = 8


def allreduce_small(buf, *, name):
    rows = buf.shape[0]

    def body(in_ref, out_ref, gath, send, recv):
        x, y, c = lax.axis_index("x"), lax.axis_index("y"), lax.axis_index("c")
        mine = 4 * x + 2 * y + c
        gath[mine] = in_ref[...]
        cps = []
        for k in range(1, N_DEV):
            to = (x ^ (k >> 2), y ^ ((k >> 1) & 1), c ^ (k & 1))
            cps.append(_remote(in_ref, gath.at[mine], send.at[k - 1], recv.at[k - 1], to))
            cps[-1].start()
        for k in range(1, N_DEV):
            theirs = gath.at[4 * (x ^ (k >> 2)) + 2 * (y ^ ((k >> 1) & 1)) + (c ^ (k & 1))]
            _remote(theirs, theirs, send.at[k - 1], recv.at[k - 1], (x, y, c)).wait_recv()
        for cp in cps:
            cp.wait_send()
        acc = gath[0]
        for d in range(1, N_DEV):
            acc = acc + gath[d]
        out_ref[...] = acc

    vm = pl.BlockSpec(memory_space=pltpu.VMEM)
    return pl.pallas_call(
        body, name=name, in_specs=[vm], out_specs=vm, out_shape=SDS(buf.shape, F32),
        scratch_shapes=[pltpu.VMEM((N_DEV, rows, 128), F32), pltpu.SemaphoreType.DMA((N_DEV - 1,)),
                        pltpu.SemaphoreType.DMA((N_DEV - 1,))])(buf)


def add_sibling(g, r, c_arr, *, name, tr=256):
    _, m, cdim = g.shape
    assert m % tr == 0

    def body(c_ref, g_ref, r_ref, o_ref):
        o_ref[...] = (g_ref[...] + r_ref[...]).astype(o_ref.dtype)

    return pl.pallas_call(
        body, name=name,
        grid_spec=pltpu.PrefetchScalarGridSpec(
            num_scalar_prefetch=1, grid=(m // tr,),
            in_specs=[pl.BlockSpec((None, tr, cdim), lambda i, c: (c[0], i, 0)), pl.BlockSpec((tr, cdim), lambda i, c: (i, 0))],
            out_specs=pl.BlockSpec((tr, cdim), lambda i, c: (i, 0))),
        out_shape=SDS((m, cdim), BF16), compiler_params=_params(("parallel",)))(c_arr, g, r)


def sum_slots(own, r, place_arr, *, name, tr=256):
    _, m, cdim = r.shape
    tr = next(cand for cand in (tr, 128, 64, 32, 16, 8) if m % cand == 0)

    def body(p_ref, own_ref, r_ref, o_ref):
        o_ref[...] = ((own_ref[...].astype(F32) + r_ref[0].astype(F32)) + r_ref[1].astype(F32)) + r_ref[2].astype(F32)

    return pl.pallas_call(
        body, name=name,
        grid_spec=pltpu.PrefetchScalarGridSpec(
            num_scalar_prefetch=1, grid=(m // tr,),
            in_specs=[pl.BlockSpec((None, tr, cdim), lambda i, p: (p[0], i, 0)),
                      pl.BlockSpec((N_CHIP - 1, tr, cdim), lambda i, p: (0, i, 0))],
            out_specs=pl.BlockSpec((None, tr, cdim), lambda i, p: (p[1], i, 0))),
        out_shape=SDS((N_LAYER, m, cdim), F32), compiler_params=_params(("parallel",)))(place_arr, own, r)


WEIGHTS = ("ffn1_norm", "ffn1_w_gate", "ffn1_w_up", "ffn1_w_down", "mix_norm", "w_in", "lru_conv_w", "lru_conv_b", "lru_w_a",
           "lru_b_a", "lru_w_x", "lru_b_x", "lru_lambda", "attn_sinks", "rel_bias", "dn_conv_w", "dn_a_log", "dn_dt_bias",
           "dn_norm", "w_out", "ffn2_norm", "ffn2_w_gate", "ffn2_w_up", "ffn2_w_down", "ple_norm", "ple_w_gate",
           "ple_w_proj", "final_norm")
CONV_SHARDED = ("lru_conv_w", "dn_conv_w")
SMALL = tuple(k for k in WEIGHTS if k not in SHARDED)


def _pack(arrs):
    flat = []
    for a in arrs:
        v = a.reshape(-1)
        flat.append(jnp.pad(v, (0, -v.shape[0] % 128)))
    v = jnp.concatenate(flat)
    v = jnp.pad(v, (0, -v.shape[0] % 1024))
    return v.reshape(-1, 128)


def _unpack(buf, shapes):
    v, out, off = buf.reshape(-1), [], 0
    for s in shapes:
        n = int(np.prod(s))
        out.append(v[off:off + n].reshape(s))
        off += n + (-n % 128)
    return out


def _chip_cols(a):
    n, l, r, c = a.shape
    return a.transpose(1, 2, 0, 3).reshape(l, r, n * c)


def _chip_rows(a):
    n, l, r, c = a.shape
    return a.transpose(1, 0, 2, 3).reshape(l, n * r, c)


def kernel(x, p, ffn1_norm, ffn1_w_gate, ffn1_w_up, ffn1_w_down, mix_norm, w_in, lru_conv_w, lru_conv_b, lru_w_a, lru_b_a, lru_w_x, lru_b_x, lru_lambda, attn_sinks, rel_bias, dn_conv_w, dn_a_log, dn_dt_bias, dn_norm, w_out, ffn2_norm, ffn2_w_gate, ffn2_w_up, ffn2_w_down, ple_norm, ple_w_gate, ple_w_proj, final_norm, loss_target, m_ffn1_norm, m_ffn1_w_gate, m_ffn1_w_up, m_ffn1_w_down, m_mix_norm, m_w_in, m_lru_conv_w, m_lru_conv_b, m_lru_w_a, m_lru_b_a, m_lru_w_x, m_lru_b_x, m_lru_lambda, m_attn_sinks, m_rel_bias, m_dn_conv_w, m_dn_a_log, m_dn_dt_bias, m_dn_norm, m_w_out, m_ffn2_norm, m_ffn2_w_gate, m_ffn2_w_up, m_ffn2_w_down, m_ple_norm, m_ple_w_gate, m_ple_w_proj, m_final_norm, v_ffn1_norm, v_ffn1_w_gate, v_ffn1_w_up, v_ffn1_w_down, v_mix_norm, v_w_in, v_lru_conv_w, v_lru_conv_b, v_lru_w_a, v_lru_b_a, v_lru_w_x, v_lru_b_x, v_lru_lambda, v_attn_sinks, v_rel_bias, v_dn_conv_w, v_dn_a_log, v_dn_dt_bias, v_dn_norm, v_w_out, v_ffn2_norm, v_ffn2_w_gate, v_ffn2_w_up, v_ffn2_w_down, v_ple_norm, v_ple_w_gate, v_ple_w_proj, v_final_norm):
    given = dict(locals())
    ws = {k: given[k] for k in WEIGHTS}
    ms = {k: given["m_" + k] for k in WEIGHTS}
    vs = {k: given["v_" + k] for k in WEIGHTS}
    nb, seq, d = x.shape
    t = nb * seq
    cx, cy, cc = lax.axis_index("x"), lax.axis_index("y"), lax.axis_index("c")
    chip = 2 * cx + cy

    chip_arr = chip.astype(jnp.int32).reshape(1)
    placed = [place_shard(ws[k], chip_arr, F32 if k in CONV_SHARDED else BF16, name=f"place_{k}")
              for k in SHARDED + CONV_SHARDED]
    gathered = allgather_shards(placed, name="allgather_weights")
    full = dict(zip(SHARDED + CONV_SHARDED, gathered))
    for k in ("w_in", "ple_w_proj", "lru_conv_w", "dn_conv_w"):
        full[k] = _chip_cols(full[k])
    for k in ("w_out", "ple_w_gate"):
        full[k] = _chip_rows(full[k])
    full["w_in"] = jnp.pad(full["w_in"], ((0, 0), (0, 0), (0, D_IN_PAD - D_IN)))
    for k in SMALL:
        if k not in CONV_SHARDED:
            full[k] = ws[k]

    bmap = jnp.asarray(_rel_bucket_map())
    loss, gx, grads = local_step(x.reshape(t, d), p.reshape(N_LAYER, t, PLE_DIM), loss_target.reshape(t, d), full, bmap, nb, seq)

    gs = [grads[k] for k in SHARDED]
    flat = lambda a, lead: a.reshape(a.shape[:lead] + (-1, a.shape[-1]))
    theirs = exchange_layers(gs, name="rs_exchange_layers")
    c_arr = cc.astype(jnp.int32).reshape(1)
    sums = [add_sibling(flat(g, 1), flat(r, 0), c_arr, name=f"rs_add_{k}").reshape(r.shape)
            for k, g, r in zip(SHARDED, gs, theirs)]
    slots = reduce_to_shards(sums, name="rs_reduce_to_shards")
    place_arr = jnp.stack([chip, cc]).astype(jnp.int32)
    mine = [sum_slots(flat(s, 1), flat(r, 1), place_arr, name=f"rs_sum_{k}").reshape((N_LAYER,) + r.shape[1:])
            for k, s, r in zip(SHARDED, sums, slots)]
    g_out = dict(zip(SHARDED, share_layers(mine, name="rs_share_layers")))

    small_shapes = [grads[k].shape for k in SMALL]
    g_small = dict(zip(SMALL, _unpack(allreduce_small(_pack([grads[k] for k in SMALL]), name="allreduce_small"), small_shapes)))
    for k in CONV_SHARDED:
        width = ws[k].shape[-1]
        g_small[k] = lax.dynamic_slice_in_dim(g_small[k], chip * width, width, axis=2)
    g_out.update(g_small)

    delta, new_m, new_v = {}, {}, {}
    for k in SHARDED:
        two_d = lambda a: a.reshape(-1, a.shape[-1])
        res = adamw(two_d(ws[k]), two_d(g_out[k]), two_d(ms[k]), two_d(vs[k]), name=f"adamw_{k}")
        delta[k], new_m[k], new_v[k] = (r.reshape(ws[k].shape) for r in res)
    shapes = [ws[k].shape for k in SMALL]
    res = adamw(*[_pack([src[k] for k in SMALL]) for src in (ws, g_out, ms, vs)], name="adamw_small")
    for dst, r in zip((delta, new_m, new_v), res):
        dst.update(zip(SMALL, _unpack(r, shapes)))

    total = lax.psum(loss[0, 0], ("x", "y", "c"))
    return (total, gx.reshape(nb, seq, d), *[g_out[k] for k in WEIGHTS], *[delta[k] for k in WEIGHTS],
            *[new_m[k] for k in WEIGHTS], *[new_v[k] for k in WEIGHTS])
```

```python
import functools
import math

import numpy as np
import jax
import jax.numpy as jnp
from jax import lax
from jax.experimental import pallas as pl
from jax.experimental.pallas import tpu as pltpu

F32 = jnp.float32
BF16 = jnp.bfloat16

EPS = 1e-6
D_MODEL = 1024
D_FF = 2816
N_CHIP = 4
FF_BLK = D_FF // N_CHIP
HEAD = 64
LRU_W = 256
ATT_W = 512
ATT_HEADS = 8
KV_HEADS = 2
ATT_GROUP = 4
BLOCK_Q = 128
DN_HEADS = 4
DN_CHUNK = 64
D_IN = 2312
D_IN_PAD = 2560
PLE_DIM = 256
REL_BUCKETS = 32
LRU_C = 8.0
N_LAYER = 2

ADAM_LR, ADAM_B1, ADAM_B2, ADAM_EPS, ADAM_WD, ADAM_STEP = 0.001, 0.9, 0.999, 1e-08, 0.01, 10

VMEM_LIMIT = 56 << 20
MESH = pl.DeviceIdType.MESH
SDS = jax.ShapeDtypeStruct


def _dot(a, b, ca=1, cb=0, hi=False):
    dims = (((ca,), (cb,)), ((), ()))
    one = lambda u, v: lax.dot_general(u, v, dims, preferred_element_type=F32)
    a_hi, b_hi = a.astype(BF16), b.astype(BF16)
    if not hi:
        return one(a_hi, b_hi)
    a_lo = (a - a_hi.astype(F32)).astype(BF16)
    b_lo = (b - b_hi.astype(F32)).astype(BF16)
    return one(a_hi, b_hi) + (one(a_hi, b_lo) + one(a_lo, b_hi))


def _nn(a, b, hi=False):
    return _dot(a, b, 1, 0, hi)


def _nt(a, b, hi=False):
    return _dot(a, b, 1, 1, hi)


def _tn(a, b, hi=False):
    return _dot(a, b, 0, 0, hi)


def _sigmoid(x):
    return jax.nn.sigmoid(x)


def _softplus(x):
    return jnp.maximum(x, 0.0) + jnp.log1p(jnp.exp(-jnp.abs(x)))


def _neg_expm1(z):
    series = -z * (1.0 + z * (0.5 + z * (1.0 / 6.0 + z * (1.0 / 24.0 + z * (1.0 / 120.0)))))
    return jnp.where(z > -0.05, series, 1.0 - jnp.exp(z))


_GELU_C = math.sqrt(2.0 / math.pi)


def _gelu(x):
    t = jnp.tanh(_GELU_C * (x + 0.044715 * x * x * x))
    return 0.5 * x * (1.0 + t), t


def _gelu_grad(x, t):
    return 0.5 * (1.0 + t) + 0.5 * x * (1.0 - t * t) * _GELU_C * (1.0 + 3.0 * 0.044715 * x * x)


def _rms_fwd(h, g):
    r = lax.rsqrt(jnp.mean(h * h, axis=-1, keepdims=True) + EPS)
    xh = h * r
    return xh * g, xh, r


def _rms_bwd(dn, xh, r, g):
    dxh = dn * g
    dh = r * (dxh - xh * jnp.mean(dxh * xh, axis=-1, keepdims=True))
    return dh, jnp.sum(dn * xh, axis=0, keepdims=True)


def _shift_down(x, d, fill=0.0):
    row = lax.broadcasted_iota(jnp.int32, x.shape, 0)
    return jnp.where(row >= d, pltpu.roll(x, d, 0), fill)


def _shift_up(x, d, fill=0.0):
    n = x.shape[0]
    row = lax.broadcasted_iota(jnp.int32, x.shape, 0)
    return jnp.where(row < n - d, pltpu.roll(x, n - d, 0), fill)


def _conv_fwd(x, w):
    y = x * w[3]
    for k in range(3):
        y = y + _shift_down(x, 3 - k) * w[k]
    return y


def _conv_bwd(dy, x, w):
    dx = dy * w[3]
    rows = [None] * 4
    rows[3] = jnp.sum(dy * x, axis=0, keepdims=True)
    for k in range(3):
        dx = dx + _shift_up(dy, 3 - k) * w[k]
        rows[k] = jnp.sum(dy * _shift_down(x, 3 - k), axis=0, keepdims=True)
    r4 = lax.broadcasted_iota(jnp.int32, (4, x.shape[1]), 0)
    dw = jnp.zeros((4, x.shape[1]), F32)
    for k in range(4):
        dw = jnp.where(r4 == k, rows[k], dw)
    return dx, dw


def _params(sem=None, vmem=VMEM_LIMIT):
    return pltpu.CompilerParams(dimension_semantics=sem, vmem_limit_bytes=vmem)


def _whole(shape):
    nd = len(shape)
    return pl.BlockSpec(shape, lambda *_: (0,) * nd)


def matmul(a, b, *, name, ta=False, tb=False, residual=None, out_dtype=F32, tm=512, tn=512, tk=512):
    m, k = (a.shape[1], a.shape[0]) if ta else a.shape
    n = b.shape[0] if tb else b.shape[1]
    tm, tn, tk = min(tm, m), min(tn, n), min(tk, k)
    assert m % tm == 0 and n % tn == 0 and k % tk == 0, (m, n, k, tm, tn, tk)
    nk = k // tk

    def body(*refs):
        if residual is None:
            a_ref, b_ref, o_ref, acc = refs
        else:
            a_ref, b_ref, r_ref, o_ref, acc = refs
        kk = pl.program_id(2)

        @pl.when(kk == 0)
        def _():
            acc[...] = jnp.zeros_like(acc)

        acc[...] += _dot(a_ref[...], b_ref[...], 0 if ta else 1, 1 if tb else 0)

        @pl.when(kk == nk - 1)
        def _():
            out = acc[...]
            if residual is not None:
                out = out + r_ref[...]
            o_ref[...] = out.astype(out_dtype)

    a_spec = pl.BlockSpec((tk, tm), lambda i, j, kk: (kk, i)) if ta else pl.BlockSpec((tm, tk), lambda i, j, kk: (i, kk))
    b_spec = pl.BlockSpec((tn, tk), lambda i, j, kk: (j, kk)) if tb else pl.BlockSpec((tk, tn), lambda i, j, kk: (kk, j))
    o_spec = pl.BlockSpec((tm, tn), lambda i, j, kk: (i, j))
    in_specs, args = [a_spec, b_spec], [a, b]
    if residual is not None:
        in_specs.append(o_spec)
        args.append(residual)
    return pl.pallas_call(
        body, name=name, grid=(m // tm, n // tn, nk), in_specs=in_specs, out_specs=o_spec,
        out_shape=SDS((m, n), out_dtype), scratch_shapes=[pltpu.VMEM((tm, tn), F32)],
        compiler_params=_params(("parallel", "parallel", "arbitrary")))(*args)


def norm_matmul(h, gain, w, *, name, tm=512, tn=512):
    t, d = h.shape
    tm = min(tm, t)
    n = w.shape[1]
    assert t % tm == 0 and n % tn == 0

    def body(h_ref, g_ref, w_ref, u_ref, n_ref):
        @pl.when(pl.program_id(1) == 0)
        def _():
            n_ref[...] = _rms_fwd(h_ref[...], g_ref[...])[0].astype(BF16)

        u_ref[...] = _nn(n_ref[...], w_ref[...])

    return pl.pallas_call(
        body, name=name, grid=(t // tm, n // tn),
        in_specs=[pl.BlockSpec((tm, d), lambda i, j: (i, 0)), _whole((1, d)), pl.BlockSpec((d, tn), lambda i, j: (0, j))],
        out_specs=[pl.BlockSpec((tm, tn), lambda i, j: (i, j)), pl.BlockSpec((tm, d), lambda i, j: (i, 0))],
        out_shape=[SDS((t, n), F32), SDS((t, d), BF16)],
        compiler_params=_params(("parallel", "arbitrary")))(h, gain, w)


def rms_bwd(h, gain, dn, dres, *, name, tm=512):
    t, d = h.shape
    tm = min(tm, t)

    def body(h_ref, g_ref, dn_ref, dr_ref, dh_ref, dg_ref):
        @pl.when(pl.program_id(0) == 0)
        def _():
            dg_ref[...] = jnp.zeros_like(dg_ref)

        g = g_ref[...]
        _, xh, r = _rms_fwd(h_ref[...], g)
        dh, dg = _rms_bwd(dn_ref[...], xh, r, g)
        dh_ref[...] = dr_ref[...] + dh
        dg_ref[...] += dg

    row = pl.BlockSpec((tm, d), lambda i: (i, 0))
    return pl.pallas_call(
        body, name=name, grid=(t // tm,), in_specs=[row, _whole((1, d)), row, row],
        out_specs=[row, _whole((1, d))], out_shape=[SDS((t, d), F32), SDS((1, d), F32)],
        compiler_params=_params(("arbitrary",)))(h, gain, dn, dres)


def ffn_fwd(h, gain, wg, wu, wd, layer, *, name, tm=512):
    t, d = h.shape
    tm = min(tm, t)

    def body(h_ref, g_ref, wg_ref, wu_ref, wd_ref, o_ref, n_sc, acc):
        j = pl.program_id(1)

        @pl.when(j == 0)
        def _():
            n_sc[...] = _rms_fwd(h_ref[...], g_ref[...])[0].astype(BF16)
            acc[...] = jnp.zeros_like(acc)

        n = n_sc[...]
        a = _nn(n, wg_ref[...])
        b = _nn(n, wu_ref[...])
        acc[...] += _nn(a * _sigmoid(a) * b, wd_ref[...])

        @pl.when(j == N_CHIP - 1)
        def _():
            o_ref[...] = h_ref[...] + 0.5 * acc[...]

    row = pl.BlockSpec((tm, d), lambda i, j: (i, 0))
    return pl.pallas_call(
        body, name=name, grid=(t // tm, N_CHIP),
        in_specs=[row, _whole((1, d)),
                  pl.BlockSpec((None, None, d, FF_BLK), lambda i, j: (j, layer, 0, 0)),
                  pl.BlockSpec((None, None, d, FF_BLK), lambda i, j: (j, layer, 0, 0)),
                  pl.BlockSpec((None, None, FF_BLK, d), lambda i, j: (j, layer, 0, 0))],
        out_specs=row, out_shape=SDS((t, d), F32),
        scratch_shapes=[pltpu.VMEM((tm, d), BF16), pltpu.VMEM((tm, d), F32)],
        compiler_params=_params(("parallel", "arbitrary")))(h, gain, wg, wu, wd)


def ffn_bwd_act(h, gain, dout, wg, wu, wd, layer, *, name, tm=512):
    t, d = h.shape
    tm = min(tm, t)

    def body(h_ref, g_ref, do_ref, wg_ref, wu_ref, wd_ref, dh_ref, n_ref, da_ref, db_ref, s_ref, dg_ref, dn_acc):
        i, j = pl.program_id(0), pl.program_id(1)

        @pl.when((i == 0) & (j == 0))
        def _():
            dg_ref[...] = jnp.zeros_like(dg_ref)

        @pl.when(j == 0)
        def _():
            n_ref[...] = _rms_fwd(h_ref[...], g_ref[...])[0].astype(BF16)
            dn_acc[...] = jnp.zeros_like(dn_acc)

        n = n_ref[...]
        a = _nn(n, wg_ref[...])
        b = _nn(n, wu_ref[...])
        sig = _sigmoid(a)
        sa = a * sig
        ds = _nt(0.5 * do_ref[...], wd_ref[...])
        db = ds * sa
        da = ds * b * (sig * (1.0 + a * (1.0 - sig)))
        s_ref[...] = (sa * b).astype(BF16)
        da_ref[...] = da.astype(BF16)
        db_ref[...] = db.astype(BF16)
        dn_acc[...] += _nt(da, wg_ref[...]) + _nt(db, wu_ref[...])

        @pl.when(j == N_CHIP - 1)
        def _():
            g = g_ref[...]
            _, xh, r = _rms_fwd(h_ref[...], g)
            dh, dg = _rms_bwd(dn_acc[...], xh, r, g)
            dh_ref[...] = do_ref[...] + dh
            dg_ref[...] += dg

    row = pl.BlockSpec((tm, d), lambda i, j: (i, 0))
    blk = pl.BlockSpec((None, tm, FF_BLK), lambda i, j: (j, i, 0))
    act = SDS((N_CHIP, t, FF_BLK), BF16)
    return pl.pallas_call(
        body, name=name, grid=(t // tm, N_CHIP),
        in_specs=[row, _whole((1, d)), row,
                  pl.BlockSpec((None, None, d, FF_BLK), lambda i, j: (j, layer, 0, 0)),
                  pl.BlockSpec((None, None, d, FF_BLK), lambda i, j: (j, layer, 0, 0)),
                  pl.BlockSpec((None, None, FF_BLK, d), lambda i, j: (j, layer, 0, 0))],
        out_specs=[row, row, blk, blk, blk, _whole((1, d))],
        out_shape=[SDS((t, d), F32), SDS((t, d), BF16), act, act, act, SDS((1, d), F32)],
        scratch_shapes=[pltpu.VMEM((tm, d), F32)],
        compiler_params=_params(("arbitrary", "arbitrary")))(h, gain, dout, wg, wu, wd)


def ffn_bwd_w(n, da, db, s, dout, *, name, tk=512):
    t, d = n.shape
    tk = min(tk, t)

    def body(n_ref, da_ref, db_ref, s_ref, do_ref, dwg_ref, dwu_ref, dwd_ref):
        @pl.when(pl.program_id(1) == 0)
        def _():
            dwg_ref[...] = jnp.zeros_like(dwg_ref)
            dwu_ref[...] = jnp.zeros_like(dwu_ref)
            dwd_ref[...] = jnp.zeros_like(dwd_ref)

        nn = n_ref[...]
        dwg_ref[...] += _tn(nn, da_ref[...])
        dwu_ref[...] += _tn(nn, db_ref[...])
        dwd_ref[...] += _tn(s_ref[...], 0.5 * do_ref[...])

    row = pl.BlockSpec((tk, d), lambda j, kk: (kk, 0))
    blk = pl.BlockSpec((None, tk, FF_BLK), lambda j, kk: (j, kk, 0))
    return pl.pallas_call(
        body, name=name, grid=(N_CHIP, t // tk), in_specs=[row, blk, blk, blk, row],
        out_specs=[pl.BlockSpec((None, d, FF_BLK), lambda j, kk: (j, 0, 0)),
                   pl.BlockSpec((None, d, FF_BLK), lambda j, kk: (j, 0, 0)),
                   pl.BlockSpec((None, FF_BLK, d), lambda j, kk: (j, 0, 0))],
        out_shape=[SDS((N_CHIP, d, FF_BLK), F32), SDS((N_CHIP, d, FF_BLK), F32), SDS((N_CHIP, FF_BLK, d), F32)],
        compiler_params=_params(("parallel", "arbitrary")))(n, da, db, s, dout)


def ple_fwd(h, gain, wpg, pl_in, wpp, *, name, tm=512):
    t, d = h.shape
    tm = min(tm, t)
    pd = pl_in.shape[1]

    def body(h_ref, g_ref, wpg_ref, p_ref, wpp_ref, o_ref):
        hh = h_ref[...]
        n = _rms_fwd(hh, g_ref[...])[0]
        gate = _sigmoid(_nn(n, wpg_ref[...]))
        o_ref[...] = hh + gate * _nn(p_ref[...], wpp_ref[...])

    row = pl.BlockSpec((tm, d), lambda i: (i, 0))
    return pl.pallas_call(
        body, name=name, grid=(t // tm,),
        in_specs=[row, _whole((1, d)), _whole((d, d)), pl.BlockSpec((tm, pd), lambda i: (i, 0)), _whole((pd, d))],
        out_specs=row, out_shape=SDS((t, d), F32), compiler_params=_params(("parallel",)))(h, gain, wpg, pl_in, wpp)


def ple_bwd(h, gain, wpg, pl_in, wpp, dout, *, name, tm=512):
    t, d = h.shape
    tm = min(tm, t)
    pd = pl_in.shape[1]

    def body(h_ref, g_ref, wpg_ref, p_ref, wpp_ref, do_ref, dh_ref, n_ref, dga_ref, dpp_ref, dg_ref):
        @pl.when(pl.program_id(0) == 0)
        def _():
            dg_ref[...] = jnp.zeros_like(dg_ref)

        g = g_ref[...]
        n, xh, r = _rms_fwd(h_ref[...], g)
        gate = _sigmoid(_nn(n, wpg_ref[...]))
        pp = _nn(p_ref[...], wpp_ref[...])
        do = do_ref[...]
        dga = do * pp * gate * (1.0 - gate)
        dh, dg = _rms_bwd(_nt(dga, wpg_ref[...]), xh, r, g)
        dh_ref[...] = do + dh
        n_ref[...] = n.astype(BF16)
        dga_ref[...] = dga.astype(BF16)
        dpp_ref[...] = (do * gate).astype(BF16)
        dg_ref[...] += dg

    row = pl.BlockSpec((tm, d), lambda i: (i, 0))
    return pl.pallas_call(
        body, name=name, grid=(t // tm,),
        in_specs=[row, _whole((1, d)), _whole((d, d)), pl.BlockSpec((tm, pd), lambda i: (i, 0)), _whole((pd, d)), row],
        out_specs=[row, row, row, row, _whole((1, d))],
        out_shape=[SDS((t, d), F32), SDS((t, d), BF16), SDS((t, d), BF16), SDS((t, d), BF16), SDS((1, d), F32)],
        compiler_params=_params(("arbitrary",)))(h, gain, wpg, pl_in, wpp, dout)


def loss_head(h, gain, target, *, name, tm=512):
    t, d = h.shape
    tm = min(tm, t)

    def body(h_ref, g_ref, t_ref, dh_ref, dg_ref, l_ref):
        @pl.when(pl.program_id(0) == 0)
        def _():
            dg_ref[...] = jnp.zeros_like(dg_ref)
            l_ref[...] = jnp.zeros_like(l_ref)

        g = g_ref[...]
        y, xh, r = _rms_fwd(h_ref[...], g)
        err = y - t_ref[...]
        l_ref[...] += 0.5 * jnp.sum(jnp.mean(err * err, axis=-1, keepdims=True), axis=0, keepdims=True)
        dh, dg = _rms_bwd(err * (1.0 / d), xh, r, g)
        dh_ref[...] = dh
        dg_ref[...] += dg

    row = pl.BlockSpec((tm, d), lambda i: (i, 0))
    return pl.pallas_call(
        body, name=name, grid=(t // tm,), in_specs=[row, _whole((1, d)), row],
        out_specs=[row, _whole((1, d)), _whole((1, 1))],
        out_shape=[SDS((t, d), F32), SDS((1, d), F32), SDS((1, 1), F32)],
        compiler_params=_params(("arbitrary",)))(h, gain, target)


def adamw(w, g, m, v, *, name):
    r, c = w.shape
    tr = r
    for cand in (512, 256, 128, 64, 32, 16, 8):
        if r % cand == 0:
            tr = cand
            break

    def body(w_ref, g_ref, m_ref, v_ref, d_ref, nm_ref, nv_ref):
        gg = g_ref[...]
        mm = ADAM_B1 * m_ref[...] + (1.0 - ADAM_B1) * gg
        vv = ADAM_B2 * v_ref[...] + (1.0 - ADAM_B2) * (gg * gg)
        m_hat = mm / (1.0 - ADAM_B1 ** ADAM_STEP)
        v_hat = vv / (1.0 - ADAM_B2 ** ADAM_STEP)
        d_ref[...] = -ADAM_LR * (m_hat / (jnp.sqrt(v_hat) + ADAM_EPS) + ADAM_WD * w_ref[...])
        nm_ref[...] = mm
        nv_ref[...] = vv

    blk = pl.BlockSpec((tr, c), lambda i: (i, 0))
    out = SDS((r, c), F32)
    return pl.pallas_call(body, name=name, grid=(r // tr,), in_specs=[blk] * 4, out_specs=[blk] * 3,
                          out_shape=[out, out, out], compiler_params=_params(("parallel",)))(w, g, m, v)


def _scan_fwd(a, b):
    d = 1
    while d < a.shape[0]:
        b = a * _shift_down(b, d, 0.0) + b
        a = a * _shift_down(a, d, 1.0)
        d *= 2
    return b


def _scan_rev(a, b):
    d = 1
    while d < a.shape[0]:
        b = a * _shift_up(b, d, 0.0) + b
        a = a * _shift_up(a, d, 1.0)
        d *= 2
    return b


LRU_HALF = 128


def _lru_in_specs(seq):
    half = LRU_W // LRU_HALF
    vec = pl.BlockSpec((1, LRU_HALF), lambda j, b: (0, j))
    mat = pl.BlockSpec((LRU_HALF, LRU_HALF), lambda j, b: (j, j))
    return [pl.BlockSpec((seq, LRU_HALF), lambda j, b: (b, j)), pl.BlockSpec((seq, LRU_HALF), lambda j, b: (b, half + j)),
            pl.BlockSpec((4, LRU_HALF), lambda j, b: (0, j)), vec, mat, vec, mat, vec, vec]


def _lru_math(x_ref, gate_ref, cw_ref, cb_ref, wa_ref, ba_ref, wx_ref, bx_ref, lam_ref):
    x = x_ref[...]
    gate = gate_ref[...]
    cw =[cw_ref[k:k + 1, :] for k in range(4)]
    xr = _conv_fwd(x, cw) + cb_ref[...]
    r = _sigmoid(_nn(xr, wa_ref[...]) + ba_ref[...])
    i = _sigmoid(_nn(xr, wx_ref[...]) + bx_ref[...])
    sp = _softplus(-lam_ref[...])
    log_a = -LRU_C * r * sp
    a = jnp.exp(log_a)
    mult = jnp.sqrt(_neg_expm1(2.0 * log_a))
    gi = i * xr
    h = _scan_fwd(a, mult * gi)
    gl, tg = _gelu(gate)
    return dict(x=x, gate=gate, cw=cw, xr=xr, r=r, i=i, sp=sp, a=a, mult=mult, gi=gi, h=h, gl=gl, tg=tg)


def lru_fwd(u, cw, cb, wa, ba, wx, bx, lam, *, seq, name):
    t = u.shape[0]

    def body(x_ref, gate_ref, cw_ref, cb_ref, wa_ref, ba_ref, wx_ref, bx_ref, lam_ref, y_ref):
        f = _lru_math(x_ref, gate_ref, cw_ref, cb_ref, wa_ref, ba_ref, wx_ref, bx_ref, lam_ref)
        y_ref[...] = f["gl"] * f["h"]

    return pl.pallas_call(
        body, name=name, grid=(LRU_W // LRU_HALF, t // seq), in_specs=_lru_in_specs(seq),
        out_specs=pl.BlockSpec((seq, LRU_HALF), lambda j, b: (b, j)), out_shape=SDS((t, LRU_W), F32),
        compiler_params=_params(("parallel", "parallel")))(u, u, cw, cb, wa, ba, wx, bx, lam)


def lru_bwd(u, cw, cb, wa, ba, wx, bx, lam, dy, *, seq, name):
    t = u.shape[0]

    def body(x_ref, gate_ref, cw_ref, cb_ref, wa_ref, ba_ref, wx_ref, bx_ref, lam_ref, dy_ref,
             dx_ref, dgate_ref, dcw_ref, dwa_ref, dwx_ref, dv_ref):
        @pl.when(pl.program_id(1) == 0)
        def _():
            dcw_ref[...] = jnp.zeros_like(dcw_ref)
            dwa_ref[...] = jnp.zeros_like(dwa_ref)
            dwx_ref[...] = jnp.zeros_like(dwx_ref)
            dv_ref[...] = jnp.zeros_like(dv_ref)

        f = _lru_math(x_ref, gate_ref, cw_ref, cb_ref, wa_ref, ba_ref, wx_ref, bx_ref, lam_ref)
        dy = dy_ref[...]
        a, h, xr, r, i, mult, gi, sp = f["a"], f["h"], f["xr"], f["r"], f["i"], f["mult"], f["gi"], f["sp"]
        dgate_ref[...] = dy * h * _gelu_grad(f["gate"], f["tg"])
        lamb = _scan_rev(_shift_up(a, 1, 0.0), dy * f["gl"])
        da = lamb * _shift_down(h, 1)
        dlog_a = da * a - (lamb * gi) * (a * a) / mult
        dgi = lamb * mult
        dra = dlog_a * (-LRU_C * sp) * r * (1.0 - r)
        dia = dgi * xr * i * (1.0 - i)
        dsp = jnp.sum(dlog_a * (-LRU_C * r), axis=0, keepdims=True)
        dlam = -dsp * _sigmoid(-lam_ref[...])
        dxr = dgi * i + _nt(dra, wa_ref[...]) + _nt(dia, wx_ref[...])
        dx, dcw = _conv_bwd(dxr, f["x"], f["cw"])
        dx_ref[...] = dx
        dcw_ref[...] += dcw
        dwa_ref[...] += _tn(xr, dra)
        dwx_ref[...] += _tn(xr, dia)
        rows = [jnp.sum(dxr, axis=0, keepdims=True), jnp.sum(dra, axis=0, keepdims=True),
                jnp.sum(dia, axis=0, keepdims=True), dlam]
        r8 = lax.broadcasted_iota(jnp.int32, (8, LRU_HALF), 0)
        acc = jnp.zeros((8, LRU_HALF), F32)
        for k, row in enumerate(rows):
            acc = jnp.where(r8 == k, row, acc)
        dv_ref[...] += acc

    nhalf = LRU_W // LRU_HALF
    col = pl.BlockSpec((seq, LRU_HALF), lambda j, b: (b, j))
    mat = pl.BlockSpec((None, LRU_HALF, LRU_HALF), lambda j, b: (j, 0, 0))
    return pl.pallas_call(
        body, name=name, grid=(nhalf, t // seq), in_specs=_lru_in_specs(seq) + [col],
        out_specs=[col, col, pl.BlockSpec((4, LRU_HALF), lambda j, b: (0, j)), mat, mat,
                   pl.BlockSpec((8, LRU_HALF), lambda j, b: (0, j))],
        out_shape=[SDS((t, LRU_W), F32), SDS((t, LRU_W), F32), SDS((4, LRU_W), F32),
                   SDS((nhalf, LRU_HALF, LRU_HALF), F32), SDS((nhalf, LRU_HALF, LRU_HALF), F32), SDS((8, LRU_W), F32)],
        compiler_params=_params(("arbitrary", "arbitrary")))(u, u, cw, cb, wa, ba, wx, bx, lam, dy)


NEG = -1e30


def _rel_bucket_map():
    dist = (np.arange(BLOCK_Q)[:, None] - np.arange(BLOCK_Q)[None, :]) % BLOCK_Q
    max_exact = REL_BUCKETS // 2
    large = max_exact + (np.log(np.maximum(dist, 1).astype(np.float32) / max_exact)
                         / math.log(BLOCK_Q / max_exact) * (REL_BUCKETS - max_exact)).astype(np.int32)
    large = np.minimum(large, REL_BUCKETS - 1)
    return np.where(dist < max_exact, dist, large).astype(np.int32)


def relbias_fwd(rel_bias, bmap, *, name):
    def body(rb_ref, bm_ref, o_ref):
        bm = bm_ref[...]
        for h in range(ATT_HEADS):
            acc = jnp.zeros((BLOCK_Q, BLOCK_Q), F32)
            for b in range(REL_BUCKETS):
                acc = jnp.where(bm == b, rb_ref[b, h], acc)
            o_ref[h] = acc

    return pl.pallas_call(
        body, name=name, in_specs=[pl.BlockSpec(memory_space=pltpu.SMEM), pl.BlockSpec(memory_space=pltpu.VMEM)],
        out_specs=pl.BlockSpec(memory_space=pltpu.VMEM), out_shape=SDS((ATT_HEADS, BLOCK_Q, BLOCK_Q), F32))(rel_bias, bmap)


def relbias_bwd(dbias, bmap, *, name):
    def body(db_ref, bm_ref, o_ref):
        bm = bm_ref[...]
        row = lax.broadcasted_iota(jnp.int32, (REL_BUCKETS, 128), 0)
        col = lax.broadcasted_iota(jnp.int32, (REL_BUCKETS, 128), 1)
        acc = jnp.zeros((REL_BUCKETS, 128), F32)
        for h in range(ATT_HEADS):
            d = db_ref[h]
            for b in range(REL_BUCKETS):
                s = jnp.sum(jnp.sum(jnp.where(bm == b, d, 0.0), axis=1, keepdims=True), axis=0, keepdims=True)
                acc = jnp.where((row == b) & (col == h), s, acc)
        o_ref[...] = acc

    return pl.pallas_call(body, name=name, out_shape=SDS((REL_BUCKETS, 128), F32))(dbias, bmap)


def _attn_probs(q_ref, k_ref, v_ref, b_ref, s_ref, n):
    rows = ATT_GROUP * BLOCK_Q
    qs = q_ref[...].reshape(rows, HEAD) * (HEAD ** -0.5)
    prev = pl.multiple_of(jnp.maximum(n - 1, 0) * BLOCK_Q, BLOCK_Q)
    cur = pl.multiple_of(n * BLOCK_Q, BLOCK_Q)
    kp, kc = k_ref[pl.ds(prev, BLOCK_Q), :], k_ref[pl.ds(cur, BLOCK_Q), :]
    vp, vc = v_ref[pl.ds(prev, BLOCK_Q), :], v_ref[pl.ds(cur, BLOCK_Q), :]
    bias = b_ref[...].reshape(rows, BLOCK_Q)
    i = lax.broadcasted_iota(jnp.int32, (rows, BLOCK_Q), 0) & (BLOCK_Q - 1)
    j = lax.broadcasted_iota(jnp.int32, (rows, BLOCK_Q), 1)
    s_p = jnp.where((j > i) & (n > 0), _nt(qs, kp) + bias, NEG)
    s_c = jnp.where(j <= i, _nt(qs, kc) + bias, NEG)
    sink = s_ref[...]
    m = jnp.maximum(jnp.maximum(jnp.max(s_p, axis=-1, keepdims=True), jnp.max(s_c, axis=-1, keepdims=True)), sink)
    e_p, e_c, e_s = jnp.exp(s_p - m), jnp.exp(s_c - m), jnp.exp(sink - m)
    inv = 1.0 / (jnp.sum(e_p, axis=-1, keepdims=True) + jnp.sum(e_c, axis=-1, keepdims=True) + e_s)
    return e_p * inv, e_c * inv, e_s * inv, qs, kp, kc, vp, vc, prev, cur


def _attn_specs(seq):
    qspec = pl.BlockSpec((None, ATT_GROUP, BLOCK_Q, HEAD), lambda g, b, n: (b, g, n, 0))
    kvspec = pl.BlockSpec((None, None, seq, HEAD), lambda g, b, n: (b, g, 0, 0))
    bspec = pl.BlockSpec((ATT_GROUP, BLOCK_Q, BLOCK_Q), lambda g, b, n: (g, 0, 0))
    sspec = pl.BlockSpec((ATT_GROUP * BLOCK_Q, 1), lambda g, b, n: (g, 0))
    return qspec, kvspec, bspec, sspec


def attn_fwd(q, k, v, bias, sink_rows, *, name):
    nb, _, seq, _ = q.shape

    def body(q_ref, k_ref, v_ref, b_ref, s_ref, o_ref):
        p_p, p_c, _, _, _, _, vp, vc, _, _ = _attn_probs(q_ref, k_ref, v_ref, b_ref, s_ref, pl.program_id(2))
        o_ref[...] = (_nn(p_p, vp) + _nn(p_c, vc)).reshape(ATT_GROUP, BLOCK_Q, HEAD)

    qspec, kvspec, bspec, sspec = _attn_specs(seq)
    return pl.pallas_call(
        body, name=name, grid=(KV_HEADS, nb, seq // BLOCK_Q), in_specs=[qspec, kvspec, kvspec, bspec, sspec],
        out_specs=qspec, out_shape=SDS(q.shape, F32),
        compiler_params=_params(("parallel", "parallel", "arbitrary")))(q, k, v, bias, sink_rows)


def attn_bwd(q, k, v, bias, sink_rows, do, *, name):
    nb, _, seq, _ = q.shape

    def body(q_ref, k_ref, v_ref, b_ref, s_ref, do_ref, dq_ref, dk_ref, dv_ref, db_ref, ds_ref):
        b, n = pl.program_id(1), pl.program_id(2)

        @pl.when((b == 0) & (n == 0))
        def _():
            db_ref[...] = jnp.zeros_like(db_ref)
            ds_ref[...] = jnp.zeros_like(ds_ref)

        @pl.when(n == 0)
        def _():
            dk_ref[...] = jnp.zeros_like(dk_ref)
            dv_ref[...] = jnp.zeros_like(dv_ref)

        p_p, p_c, p_s, qs, kp, kc, vp, vc, prev, cur = _attn_probs(q_ref, k_ref, v_ref, b_ref, s_ref, n)
        do = do_ref[...].reshape(ATT_GROUP * BLOCK_Q, HEAD)
        dp_p, dp_c = _nt(do, vp), _nt(do, vc)
        delta = jnp.sum(p_p * dp_p, axis=-1, keepdims=True) + jnp.sum(p_c * dp_c, axis=-1, keepdims=True)
        ds_p, ds_c = p_p * (dp_p - delta), p_c * (dp_c - delta)
        dq_ref[...] = ((_nn(ds_p, kp) + _nn(ds_c, kc)) * (HEAD ** -0.5)).reshape(ATT_GROUP, BLOCK_Q, HEAD)
        dk_ref[pl.ds(prev, BLOCK_Q), :] += _tn(ds_p, qs)
        dk_ref[pl.ds(cur, BLOCK_Q), :] += _tn(ds_c, qs)
        dv_ref[pl.ds(prev, BLOCK_Q), :] += _tn(p_p, do)
        dv_ref[pl.ds(cur, BLOCK_Q), :] += _tn(p_c, do)
        db_ref[...] += (ds_p + ds_c).reshape(ATT_GROUP, BLOCK_Q, BLOCK_Q)
        ds_ref[...] += -p_s * delta

    qspec, kvspec, bspec, sspec = _attn_specs(seq)
    return pl.pallas_call(
        body, name=name, grid=(KV_HEADS, nb, seq // BLOCK_Q), in_specs=[qspec, kvspec, kvspec, bspec, sspec, qspec],
        out_specs=[qspec, kvspec, kvspec, bspec, sspec],
        out_shape=[SDS(q.shape, F32), SDS(k.shape, F32), SDS(v.shape, F32),
                   SDS((ATT_HEADS, BLOCK_Q, BLOCK_Q), F32), SDS((ATT_HEADS * BLOCK_Q, 1), F32)],
        compiler_params=_params(("arbitrary", "arbitrary", "arbitrary")))(q, k, v, bias, sink_rows, do)


def _iota2(shape, axis):
    return lax.broadcasted_iota(jnp.int32, shape, axis)


def _col_to_row(col):
    c = col.shape[0]
    eye = _iota2((c, c), 0) == _iota2((c, c), 1)
    return jnp.sum(jnp.where(eye, jnp.broadcast_to(col, (c, c)), 0.0), axis=0, keepdims=True)


def _row_to_col(row):
    c = row.shape[1]
    eye = _iota2((c, c), 0) == _iota2((c, c), 1)
    return jnp.sum(jnp.where(eye, jnp.broadcast_to(row, (c, c)), 0.0), axis=1, keepdims=True)


def _last_row(col):
    c = col.shape[0]
    return jnp.sum(jnp.where(_iota2((c, 1), 0) == c - 1, col, 0.0), axis=0, keepdims=True)


def _chunk_cumsum(x):
    pos = _iota2(x.shape, 0) & (DN_CHUNK - 1)
    d = 1
    while d < DN_CHUNK:
        x = x + jnp.where(pos >= d, pltpu.roll(x, d, 0), 0.0)
        d *= 2
    return x


def _chunk_rev_cumsum(x):
    n = x.shape[0]
    pos = _iota2(x.shape, 0) & (DN_CHUNK - 1)
    d = 1
    while d < DN_CHUNK:
        x = x + jnp.where(pos < DN_CHUNK - d, pltpu.roll(x, n - d, 0), 0.0)
        d *= 2
    return x


def _tri_inv(low):
    c = low.shape[0]
    eye = (_iota2((c, c), 0) == _iota2((c, c), 1)).astype(F32)
    m = -low
    p = eye + m
    steps = int(math.log2(c)) - 1
    for _ in range(steps):
        m = _nn(m, m, hi=True)
        p = p + _nn(p, m, hi=True)
    return p


_DN_SCALE = (HEAD ** -0.5, 1.0, None)


def _dn_act(c, scale):
    sig = _sigmoid(c)
    a = c * sig
    if scale is None:
        return a, sig, None, None
    r = lax.rsqrt(jnp.sum(a * a, axis=-1, keepdims=True) + EPS)
    return a * r * scale, sig, a * r, r


def _dn_gates(ba_ref, hs_ref):
    beta = _sigmoid(ba_ref[0])
    sp_arg = ba_ref[1] + hs_ref[1]
    a_exp = jnp.exp(hs_ref[0])
    g = -a_exp * _softplus(sp_arg)
    return beta, g, sp_arg, a_exp


def _dn_inputs(pre_ref, cw_ref, ba_ref, hs_ref, act_sc, b_sc, gc_sc, c_sc=None):
    for idx in range(3):
        c = _conv_fwd(pre_ref[idx], [cw_ref[idx, k:k + 1, :] for k in range(4)])
        if c_sc is not None:
            c_sc[idx] = c
        act_sc[idx] = _dn_act(c, _DN_SCALE[idx])[0]
    beta, g, _, _ = _dn_gates(ba_ref, hs_ref)
    b_sc[...] = beta
    gc_sc[...] = _chunk_cumsum(g)


def _dn_chunk_math(q, k, v, b, gcc):
    c = q.shape[0]
    tril = _iota2((c, c), 0) >= _iota2((c, c), 1)
    strict = _iota2((c, c), 0) > _iota2((c, c), 1)
    eg = jnp.exp(gcc)
    kb, vb = k * b, v * b
    kbg = kb * eg
    dm = jnp.exp(jnp.where(tril, jnp.broadcast_to(gcc, (c, c)) - _col_to_row(gcc), NEG))
    kk = _nt(kb, k)
    t = _tri_inv(jnp.where(strict, kk * dm, 0.0))
    glast = _last_row(gcc)
    ekd = jnp.exp(glast - gcc)
    qk = _nt(q, k)
    return dict(tril=tril, strict=strict, eg=eg, kb=kb, vb=vb, kbg=kbg, dm=dm, kk=kk, t=t, glast=glast, ekd=ekd,
                kd=k * ekd, qk=qk, amat=jnp.where(tril, qk * dm, 0.0), qg=q * eg,
                egl=jnp.broadcast_to(jnp.exp(glast), (c, 1)))


DN_UNROLL = 4


def _chunk_loop(nc, chunk):
    u = math.gcd(nc, DN_UNROLL)

    def step(i, carry):
        for j in range(u):
            chunk(i * u + j)
        return carry

    lax.fori_loop(0, nc // u, step, 0)


def _dn_specs(seq):
    s64 = lambda lead: pl.BlockSpec((lead, None, None, seq, HEAD), lambda b, h: (0, b, h, 0, 0))
    s1 = lambda lead: pl.BlockSpec((lead, None, None, seq, 1), lambda b, h: (0, b, h, 0, 0))
    one64 = pl.BlockSpec((None, None, seq, HEAD), lambda b, h: (b, h, 0, 0))
    one1 = pl.BlockSpec((None, None, seq, 1), lambda b, h: (b, h, 0, 0))
    cw = pl.BlockSpec((None, 3, 4, HEAD), lambda b, h: (h, 0, 0, 0))
    hs = pl.BlockSpec((None, 2, 1, 1), lambda b, h: (h, 0, 0, 0))
    return s64, s1, one64, one1, cw, hs


def dn_prep(pre, cw, ba, hs, *, name):
    _, nb, nh, seq, _ = pre.shape
    nc = seq // DN_CHUNK

    def body(pre_ref, cw_ref, ba_ref, hs_ref, loc_ref, egl_ref, act_sc, b_sc, gc_sc):
        _dn_inputs(pre_ref, cw_ref, ba_ref, hs_ref, act_sc, b_sc, gc_sc)

        def chunk(c):
            rows = pl.ds(pl.multiple_of(c * DN_CHUNK, DN_CHUNK), DN_CHUNK)
            m = _dn_chunk_math(act_sc[0, rows, :], act_sc[1, rows, :], act_sc[2, rows, :], b_sc[rows, :], gc_sc[rows, :])
            loc_ref[0, rows, :] = m["qg"]
            loc_ref[1, rows, :] = m["kd"]
            loc_ref[2, rows, :] = _nn(m["t"], m["vb"])
            loc_ref[3, rows, :] = _nn(m["t"], m["kbg"])
            loc_ref[4, rows, :] = m["amat"]
            egl_ref[rows, :] = m["egl"]

        _chunk_loop(nc, chunk)

    s64, s1, one64, one1, cwspec, hsspec = _dn_specs(seq)
    return pl.pallas_call(
        body, name=name, grid=(nb, nh), in_specs=[s64(3), cwspec, s1(2), hsspec], out_specs=[s64(5), one1],
        out_shape=[SDS((5, nb, nh, seq, HEAD), F32), SDS((nb, nh, seq, 1), F32)],
        scratch_shapes=[pltpu.VMEM((3, seq, HEAD), F32)] + [pltpu.VMEM((seq, 1), F32)] * 2,
        compiler_params=_params(("parallel", "parallel")))(pre, cw, ba, hs)


def _gated_norm(o, z, gn):
    r = lax.rsqrt(jnp.mean(o * o, axis=-1, keepdims=True) + EPS)
    sig = _sigmoid(z)
    return o * r, sig, r


def dn_scan(loc, egl, z, gn, *, name):
    _, nb, nh, seq, _ = loc.shape
    nc = seq // DN_CHUNK

    def body(loc_ref, egl_ref, z_ref, gn_ref, y_ref, o_ref, vn_ref, st_ref):
        gn = gn_ref[...]

        def step(c, state):
            rows = pl.ds(pl.multiple_of(c * DN_CHUNK, DN_CHUNK), DN_CHUNK)
            st_ref[rows, :] = state
            vn = loc_ref[2, rows, :] - _nn(loc_ref[3, rows, :], state)
            o = _nn(loc_ref[0, rows, :], state) + _nn(loc_ref[4, rows, :], vn)
            vn_ref[rows, :] = vn
            o_ref[rows, :] = o
            zz = z_ref[rows, :]
            on, sig, _ = _gated_norm(o, zz, gn)
            y_ref[rows, :] = on * gn * (zz * sig)
            return state * egl_ref[rows, :] + _tn(loc_ref[1, rows, :], vn)

        lax.fori_loop(0, nc, step, jnp.zeros((HEAD, HEAD), F32))

    s64, s1, one64, one1, cwspec, hsspec = _dn_specs(seq)
    out = SDS((nb, nh, seq, HEAD), F32)
    return pl.pallas_call(
        body, name=name, grid=(nb, nh), in_specs=[s64(5), one1, one64, _whole((1, HEAD))],
        out_specs=[one64] * 4, out_shape=[out] * 4,
        compiler_params=_params(("parallel", "parallel")))(loc, egl, z, gn)


def dn_scan_bwd(loc, egl, z, gn, o, vn, states, dy, *, name):
    _, nb, nh, seq, _ = loc.shape
    nc = seq // DN_CHUNK

    def body(loc_ref, egl_ref, z_ref, gn_ref, o_ref, vn_ref, st_ref, dy_ref, dloc_ref, degl_ref, dz_ref, dgn_ref):
        @pl.when((pl.program_id(0) == 0) & (pl.program_id(1) == 0))
        def _():
            dgn_ref[...] = jnp.zeros_like(dgn_ref)

        gn = gn_ref[...]
        tril = _iota2((DN_CHUNK, DN_CHUNK), 0) >= _iota2((DN_CHUNK, DN_CHUNK), 1)

        def step(i, carry):
            ds, dgn = carry
            rows = pl.ds(pl.multiple_of((nc - 1 - i) * DN_CHUNK, DN_CHUNK), DN_CHUNK)
            dy, zz, oo = dy_ref[rows, :], z_ref[rows, :], o_ref[rows, :]
            on, sig, r = _gated_norm(oo, zz, gn)
            sz = zz * sig
            dz_ref[rows, :] = dy * on * gn * (sig * (1.0 + zz * (1.0 - sig)))
            dgn = dgn + jnp.sum(dy * on * sz, axis=0, keepdims=True)
            don = dy * gn * sz
            do = r * (don - on * jnp.mean(don * on, axis=-1, keepdims=True))
            state, vnew = st_ref[rows, :], vn_ref[rows, :]
            qg, kd, w, amat = loc_ref[0, rows, :], loc_ref[1, rows, :], loc_ref[3, rows, :], loc_ref[4, rows, :]
            dvn = _tn(amat, do) + _nn(kd, ds)
            dloc_ref[0, rows, :] = _nt(do, state)
            dloc_ref[1, rows, :] = _nt(vnew, ds)
            dloc_ref[2, rows, :] = dvn
            dloc_ref[3, rows, :] = -_nt(dvn, state)
            dloc_ref[4, rows, :] = jnp.where(tril, _nt(do, vnew), 0.0)
            degl = jnp.sum(jnp.sum(state * ds, axis=1, keepdims=True), axis=0, keepdims=True)
            degl_ref[rows, :] = jnp.broadcast_to(degl, (DN_CHUNK, 1))
            return ds * egl_ref[rows, :] + _tn(qg, do) - _tn(w, dvn), dgn

        _, dgn = lax.fori_loop(0, nc, step, (jnp.zeros((HEAD, HEAD), F32), jnp.zeros((1, HEAD), F32)))
        dgn_ref[...] += dgn

    s64, s1, one64, one1, cwspec, hsspec = _dn_specs(seq)
    return pl.pallas_call(
        body, name=name, grid=(nb, nh),
        in_specs=[s64(5), one1, one64, _whole((1, HEAD)), one64, one64, one64, one64],
        out_specs=[s64(5), one1, one64, _whole((1, HEAD))],
        out_shape=[SDS((5, nb, nh, seq, HEAD), F32), SDS((nb, nh, seq, 1), F32), SDS((nb, nh, seq, HEAD), F32),
                   SDS((1, HEAD), F32)],
        compiler_params=_params(("arbitrary", "arbitrary")))(loc, egl, z, gn, o, vn, states, dy)


def dn_prep_bwd(pre, cw, ba, hs, dloc, degl, *, name):
    _, nb, nh, seq, _ = pre.shape
    nc = seq // DN_CHUNK

    def body(pre_ref, cw_ref, ba_ref, hs_ref, dloc_ref, degl_ref, dpre_ref, dba_ref, dcw_ref, dhs_ref,
             act_sc, b_sc, gc_sc, c_sc):
        @pl.when(pl.program_id(1) == 0)
        def _():
            dcw_ref[...] = jnp.zeros_like(dcw_ref)
            dhs_ref[...] = jnp.zeros_like(dhs_ref)

        _dn_inputs(pre_ref, cw_ref, ba_ref, hs_ref, act_sc, b_sc, gc_sc, c_sc)

        def chunk(c):
            rows = pl.ds(pl.multiple_of(c * DN_CHUNK, DN_CHUNK), DN_CHUNK)
            q, k, v, b, gcc = act_sc[0, rows, :], act_sc[1, rows, :], act_sc[2, rows, :], b_sc[rows, :], gc_sc[rows, :]
            m = _dn_chunk_math(q, k, v, b, gcc)
            dqg, dkd, du, dw, da = (dloc_ref[x, rows, :] for x in range(5))
            t, dm, eg = m["t"], m["dm"], m["eg"]
            dt = _nt(du, m["vb"]) + _nt(dw, m["kbg"])
            dvb, dkbg = _tn(t, du), _tn(t, dw)
            dl = jnp.where(m["strict"], -_tn(t, _nt(dt, t, hi=True), hi=True), 0.0)
            dkk = dl * dm
            dqk = da * dm
            dd = dl * m["kk"] + da * m["qk"]
            dkb = _nn(dkk, k) + dkbg * eg
            dq = _nn(dqk, k) + dqg * eg
            dk = _tn(dkk, m["kb"]) + _tn(dqk, q) + dkd * m["ekd"] + dkb * b
            db = jnp.sum(dkb * k, axis=-1, keepdims=True) + jnp.sum(dvb * v, axis=-1, keepdims=True)
            mx = jnp.where(m["tril"], dd * dm, 0.0)
            tk = jnp.sum(dkd * m["kd"], axis=-1, keepdims=True)
            dgc = (jnp.sum(mx, axis=-1, keepdims=True) - _row_to_col(jnp.sum(mx, axis=0, keepdims=True))
                   + jnp.sum(dqg * m["qg"], axis=-1, keepdims=True) + jnp.sum(dkbg * m["kbg"], axis=-1, keepdims=True) - tk)
            dglast = jnp.sum(tk, axis=0, keepdims=True) + _last_row(degl_ref[rows, :]) * jnp.exp(m["glast"])
            act_sc[0, rows, :] = dq
            act_sc[1, rows, :] = dk
            act_sc[2, rows, :] = dvb * b
            b_sc[rows, :] = db
            gc_sc[rows, :] = dgc + jnp.where(_iota2((DN_CHUNK, 1), 0) == DN_CHUNK - 1, dglast, 0.0)

        _chunk_loop(nc, chunk)

        beta, g, sp_arg, a_exp = _dn_gates(ba_ref, hs_ref)
        dg = _chunk_rev_cumsum(gc_sc[...])
        dal = dg * (-a_exp) * _sigmoid(sp_arg)
        dba_ref[0] = b_sc[...] * beta * (1.0 - beta)
        dba_ref[1] = dal
        dhs_ref[0] += jnp.sum(dg * g, axis=0, keepdims=True)
        dhs_ref[1] += jnp.sum(dal, axis=0, keepdims=True)
        for idx in range(3):
            c = c_sc[idx]
            _, sig, hat, r = _dn_act(c, _DN_SCALE[idx])
            da_ = act_sc[idx]
            if _DN_SCALE[idx] is not None:
                da_ = da_ * _DN_SCALE[idx]
                da_ = r * (da_ - hat * jnp.sum(da_ * hat, axis=-1, keepdims=True))
            dx, dcw = _conv_bwd(da_ * (sig * (1.0 + c * (1.0 - sig))), pre_ref[idx],
                                [cw_ref[idx, k:k + 1, :] for k in range(4)])
            dpre_ref[idx] = dx
            dcw_ref[idx] += dcw

    s64, s1, one64, one1, cwspec, hsspec = _dn_specs(seq)
    swap = lambda spec: pl.BlockSpec(spec.block_shape, lambda h, b, _f=spec.index_map: _f(b, h))
    return pl.pallas_call(
        body, name=name, grid=(nh, nb),
        in_specs=[swap(s64(3)), swap(cwspec), swap(s1(2)), swap(hsspec), swap(s64(5)), swap(one1)],
        out_specs=[swap(s64(3)), swap(s1(2)), swap(cwspec), swap(hsspec)],
        out_shape=[SDS((3, nb, nh, seq, HEAD), F32), SDS((2, nb, nh, seq, 1), F32), SDS((nh, 3, 4, HEAD), F32),
                   SDS((nh, 2, 1, 1), F32)],
        scratch_shapes=[pltpu.VMEM((3, seq, HEAD), F32)] + [pltpu.VMEM((seq, 1), F32)] * 2 + [pltpu.VMEM((3, seq, HEAD), F32)],
        compiler_params=_params(("arbitrary", "arbitrary")))(pre, cw, ba, hs, dloc, degl)


COL_Q, COL_K, COL_V = 512 // 128, 1024 // 128, 1152 // 128
COL_DNQ, COL_DNK, COL_DNV, COL_DNZ, COL_BA = 1280 // 128, 1536 // 128, 1792 // 128, 2048 // 128, 2304 // 128


def _lane_a(shape):
    return _iota2(shape, 1) < HEAD


def _bd(x):
    la = _lane_a(x.shape)
    return jnp.concatenate([jnp.where(la, x, 0.0), jnp.where(la, 0.0, x)], axis=0)


def _fold(m):
    return m[:HEAD] + m[HEAD:]


def _bd_mask():
    return (_iota2((2 * HEAD, 2 * HEAD), 0) < HEAD) == (_iota2((2 * HEAD, 2 * HEAD), 1) < HEAD)


def _pk_nn(x, y, hi=False):
    return _nn(x, _bd(y), hi)


def _pk_nt(u, v, hi=False):
    return _nt(u, _bd(v), hi)


def _pk_tn(x, y, hi=False):
    return _fold(jnp.where(_bd_mask(), _tn(x, y, hi), 0.0))


def _half_sum(x):
    la = _lane_a(x.shape)
    return jnp.where(la, jnp.sum(jnp.where(la, x, 0.0), axis=-1, keepdims=True),
                     jnp.sum(jnp.where(la, 0.0, x), axis=-1, keepdims=True))


def _lane_col(x, idx):
    return jnp.sum(jnp.where(_iota2(x.shape, 1) == idx, x, 0.0), axis=-1, keepdims=True)


def _row0(x):
    return jnp.max(x, axis=0, keepdims=True)


def _dup_kv(x, g):
    la = _lane_a(x.shape)
    rolled = pltpu.roll(x, HEAD, 1)
    return jnp.where(la, x, rolled) if g == 0 else jnp.where(la, rolled, x)


def _stack_heads(ref, g):
    la = _lane_a((BLOCK_Q, 2 * HEAD))
    parts = []
    for hh in range(ATT_GROUP):
        pair = ref[:, pl.ds(2 * HEAD * (2 * g + hh // 2), 2 * HEAD)]
        parts.append(jnp.where(la if hh % 2 == 0 else ~la, pair, 0.0))
    return jnp.concatenate(parts, axis=0)


def _unstack_heads(stack, ref, g):
    la = _lane_a((BLOCK_Q, 2 * HEAD))
    for j in range(2):
        top = stack[2 * j * BLOCK_Q:(2 * j + 1) * BLOCK_Q]
        bot = stack[(2 * j + 1) * BLOCK_Q:(2 * j + 2) * BLOCK_Q]
        ref[:, pl.ds(2 * HEAD * (2 * g + j), 2 * HEAD)] = jnp.where(la, top, bot)


def _swa_probs(q_ref, k_ref, v_ref, b_ref, s_ref, n, g):
    rows = ATT_GROUP * BLOCK_Q
    prev = pl.multiple_of(jnp.maximum(n - 1, 0) * BLOCK_Q, BLOCK_Q)
    cur = pl.multiple_of(n * BLOCK_Q, BLOCK_Q)
    kp, kc = _dup_kv(k_ref[pl.ds(prev, BLOCK_Q), :], g), _dup_kv(k_ref[pl.ds(cur, BLOCK_Q), :], g)
    vp, vc = _dup_kv(v_ref[pl.ds(prev, BLOCK_Q), :], g), _dup_kv(v_ref[pl.ds(cur, BLOCK_Q), :], g)
    qs = _stack_heads(q_ref, g) * (HEAD ** -0.5)
    bias = b_ref[pl.ds(ATT_GROUP * g, ATT_GROUP)].reshape(rows, BLOCK_Q)
    i = _iota2((rows, BLOCK_Q), 0) & (BLOCK_Q - 1)
    j = _iota2((rows, BLOCK_Q), 1)
    s_p = jnp.where((j > i) & (n > 0), _nt(qs, kp) + bias, NEG)
    s_c = jnp.where(j <= i, _nt(qs, kc) + bias, NEG)
    sink = s_ref[pl.ds(rows * g, rows), :]
    m = jnp.maximum(jnp.maximum(jnp.max(s_p, axis=-1, keepdims=True), jnp.max(s_c, axis=-1, keepdims=True)), sink)
    e_p, e_c, e_s = jnp.exp(s_p - m), jnp.exp(s_c - m), jnp.exp(sink - m)
    inv = 1.0 / (jnp.sum(e_p, axis=-1, keepdims=True) + jnp.sum(e_c, axis=-1, keepdims=True) + e_s)
    return e_p * inv, e_c * inv, e_s * inv, qs, kp, kc, vp, vc, prev, cur


def _swa_specs(seq):
    nblk = seq // BLOCK_Q
    qspec = pl.BlockSpec((BLOCK_Q, ATT_W), lambda b, n: (b * nblk + n, COL_Q * 128 // ATT_W))
    kspec = pl.BlockSpec((seq, 2 * HEAD), lambda b, n: (b, COL_K))
    vspec = pl.BlockSpec((seq, 2 * HEAD), lambda b, n: (b, COL_V))
    ospec = pl.BlockSpec((BLOCK_Q, ATT_W), lambda b, n: (b * nblk + n, 0))
    kvout = pl.BlockSpec((seq, 2 * HEAD), lambda b, n: (b, 0))
    return qspec, kspec, vspec, ospec, kvout, _whole((ATT_HEADS, BLOCK_Q, BLOCK_Q)), _whole((ATT_HEADS * BLOCK_Q, 1))


def swa_fwd(u, bias, sink_rows, *, seq, name):
    t = u.shape[0]

    def body(q_ref, k_ref, v_ref, b_ref, s_ref, o_ref):
        for g in range(KV_HEADS):
            p_p, p_c, _, _, _, _, vp, vc, _, _ = _swa_probs(q_ref, k_ref, v_ref, b_ref, s_ref, pl.program_id(1), g)
            _unstack_heads(_nn(p_p, vp) + _nn(p_c, vc), o_ref, g)

    qspec, kspec, vspec, ospec, kvout, bspec, sspec = _swa_specs(seq)
    return pl.pallas_call(
        body, name=name, grid=(t // seq, seq // BLOCK_Q), in_specs=[qspec, kspec, vspec, bspec, sspec], out_specs=ospec,
        out_shape=SDS((t, ATT_W), F32), compiler_params=_params(("parallel", "arbitrary")))(u, u, u, bias, sink_rows)


def swa_bwd(u, bias, sink_rows, do, *, seq, name):
    t = u.shape[0]

    def body(q_ref, k_ref, v_ref, b_ref, s_ref, do_ref, dq_ref, dk_ref, dv_ref, db_ref, ds_ref):
        b, n = pl.program_id(0), pl.program_id(1)

        @pl.when((b == 0) & (n == 0))
        def _():
            db_ref[...] = jnp.zeros_like(db_ref)
            ds_ref[...] = jnp.zeros_like(ds_ref)

        @pl.when(n == 0)
        def _():
            dk_ref[...] = jnp.zeros_like(dk_ref)
            dv_ref[...] = jnp.zeros_like(dv_ref)

        la = _lane_a((BLOCK_Q, 2 * HEAD))
        for g in range(KV_HEADS):
            p_p, p_c, p_s, qs, kp, kc, vp, vc, prev, cur = _swa_probs(q_ref, k_ref, v_ref, b_ref, s_ref, n, g)
            do = _stack_heads(do_ref, g)
            dp_p, dp_c = _nt(do, vp), _nt(do, vc)
            delta = jnp.sum(p_p * dp_p, axis=-1, keepdims=True) + jnp.sum(p_c * dp_c, axis=-1, keepdims=True)
            ds_p, ds_c = p_p * (dp_p - delta), p_c * (dp_c - delta)
            _unstack_heads((_nn(ds_p, kp) + _nn(ds_c, kc)) * (HEAD ** -0.5), dq_ref, g)
            mine = la if g == 0 else ~la

            def to_head(x):
                return jnp.where(mine, x + pltpu.roll(x, HEAD, 1), 0.0)

            dk_ref[pl.ds(prev, BLOCK_Q), :] += to_head(_tn(ds_p, qs))
            dk_ref[pl.ds(cur, BLOCK_Q), :] += to_head(_tn(ds_c, qs))
            dv_ref[pl.ds(prev, BLOCK_Q), :] += to_head(_tn(p_p, do))
            dv_ref[pl.ds(cur, BLOCK_Q), :] += to_head(_tn(p_c, do))
            db_ref[pl.ds(ATT_GROUP * g, ATT_GROUP)] += (ds_p + ds_c).reshape(ATT_GROUP, BLOCK_Q, BLOCK_Q)
            rows = ATT_GROUP * BLOCK_Q
            ds_ref[pl.ds(rows * g, rows), :] += -p_s * delta

    qspec, kspec, vspec, ospec, kvout, bspec, sspec = _swa_specs(seq)
    return pl.pallas_call(
        body, name=name, grid=(t // seq, seq // BLOCK_Q), in_specs=[qspec, kspec, vspec, bspec, sspec, ospec],
        out_specs=[ospec, kvout, kvout, bspec, sspec],
        out_shape=[SDS((t, ATT_W), F32), SDS((t, 2 * HEAD), F32), SDS((t, 2 * HEAD), F32),
                   SDS((ATT_HEADS, BLOCK_Q, BLOCK_Q), F32), SDS((ATT_HEADS * BLOCK_Q, 1), F32)],
        compiler_params=_params(("arbitrary", "arbitrary")))(u, u, u, bias, sink_rows, do)


def _gdn_gates(ba_ref, alog_ref, dt_ref, hp):
    blk = ba_ref[...]
    beta_blk = _sigmoid(blk)
    sp_arg = blk + dt_ref[...]
    a_exp = jnp.exp(alog_ref[...])
    g_blk = -a_exp * _softplus(sp_arg)
    la = _lane_a(blk.shape)
    ha = 2 * hp
    beta = jnp.where(la, _lane_col(beta_blk, ha), _lane_col(beta_blk, ha + 1))
    g = jnp.where(la, _lane_col(g_blk, DN_HEADS + ha), _lane_col(g_blk, DN_HEADS + ha + 1))
    return beta, g, beta_blk, sp_arg, a_exp, g_blk


def _gdn_act(c, scale):
    sig = _sigmoid(c)
    a = c * sig
    if scale is None:
        return a, sig, None, None
    r = lax.rsqrt(_half_sum(a * a) + EPS)
    return a * r * scale, sig, a * r, r


def _gdn_inputs(pre_refs, cw_refs, ba_ref, alog_ref, dt_ref, hp, act_sc, b_sc, gc_sc, c_sc=None):
    for idx in range(3):
        c = _conv_fwd(pre_refs[idx][...], [cw_refs[idx][k:k + 1, :] for k in range(4)])
        if c_sc is not None:
            c_sc[idx] = c
        act_sc[idx] = _gdn_act(c, _DN_SCALE[idx])[0]
    beta, g = _gdn_gates(ba_ref, alog_ref, dt_ref, hp)[:2]
    b_sc[...] = beta
    gc_sc[...] = _chunk_cumsum(g)


def _gdn_chunk(q, k, v, b, gcc):
    shape = q.shape
    row, lm = _iota2(shape, 0), _iota2(shape, 1) & (HEAD - 1)
    tril, strict, eye = row >= lm, row > lm, row == lm
    eg = jnp.exp(gcc)
    kb, vb = k * b, v * b
    kbg = kb * eg
    grow = jnp.sum(jnp.where(eye, gcc, 0.0), axis=0, keepdims=True)
    dm = jnp.exp(jnp.where(tril, gcc - grow, NEG))
    kk = _pk_nt(kb, k)
    glast = jnp.sum(jnp.where(row == DN_CHUNK - 1, gcc, 0.0), axis=0, keepdims=True)
    ekd = jnp.exp(glast - gcc)
    qk = _pk_nt(q, k)
    return dict(q=q, k=k, v=v, b=b, tril=tril, strict=strict, eye=eye, row=row, eg=eg, kb=kb, vb=vb, kbg=kbg, dm=dm, kk=kk,
                low=jnp.where(strict, kk * dm, 0.0), glast=glast, ekd=ekd, kd=k * ekd, qk=qk,
                amat=jnp.where(tril, qk * dm, 0.0), qg=q * eg, egl=jnp.broadcast_to(jnp.exp(glast), shape))


def _tri_inv_many(chunks):
    ms = [-m["low"] for m in chunks]
    ts = [m["eye"].astype(F32) + x for m, x in zip(chunks, ms)]
    for _ in range(int(math.log2(HEAD)) - 1):
        ms = [_pk_nn(x, x, hi=True) for x in ms]
        ts = [t + _pk_nn(t, x, hi=True) for t, x in zip(ts, ms)]
    return ts


def _gdn_chunk_loop(nc, act_sc, b_sc, gc_sc, finish):
    u = math.gcd(nc, DN_UNROLL)

    def step(i, carry):
        rows = [pl.ds(pl.multiple_of((i * u + j) * DN_CHUNK, DN_CHUNK), DN_CHUNK) for j in range(u)]
        chunks = [_gdn_chunk(act_sc[0, r, :], act_sc[1, r, :], act_sc[2, r, :], b_sc[r, :], gc_sc[r, :]) for r in rows]
        pending = [finish(r, m, t) for r, m, t in zip(rows, chunks, _tri_inv_many(chunks))]
        pending = [g for g in pending if g is not None]
        while pending:
            for g in list(pending):
                if next(g, StopIteration) is StopIteration:
                    pending.remove(g)
        return carry

    lax.fori_loop(0, nc // u, step, 0)


def _gdn_in_specs(seq):
    u_at = lambda col: pl.BlockSpec((seq, 2 * HEAD), lambda b, hp, _c=col: (b, _c + hp))
    cw_at = lambda col: pl.BlockSpec((4, 2 * HEAD), lambda b, hp, _c=col: (0, _c + hp))
    row = pl.BlockSpec((1, 2 * HEAD), lambda b, hp: (0, 0))
    ba = pl.BlockSpec((seq, 2 * HEAD), lambda b, hp: (b, COL_BA))
    return [u_at(COL_DNQ), u_at(COL_DNK), u_at(COL_DNV), ba, cw_at(0), cw_at(2), cw_at(4), row, row]


def _pair(seq, lead=None):
    if lead is None:
        return pl.BlockSpec((seq, 2 * HEAD), lambda b, hp: (b, hp))
    return pl.BlockSpec((lead, seq, 2 * HEAD), lambda b, hp: (0, b, hp))


def _swap(spec):
    return pl.BlockSpec(spec.block_shape, lambda hp, b, _f=spec.index_map: _f(b, hp))


def gdn_prep(u, cw, alog_row, dt_row, *, seq, name):
    t = u.shape[0]
    nc = seq // DN_CHUNK

    def body(q_ref, k_ref, v_ref, ba_ref, cq_ref, ck_ref, cv_ref, alog_ref, dt_ref, loc_ref, egl_ref, act_sc, b_sc, gc_sc):
        _gdn_inputs((q_ref, k_ref, v_ref), (cq_ref, ck_ref, cv_ref), ba_ref, alog_ref, dt_ref, pl.program_id(1),
                    act_sc, b_sc, gc_sc)

        def finish(rows, m, t):
            loc_ref[0, rows, :] = m["qg"]
            loc_ref[1, rows, :] = m["kd"]
            loc_ref[2, rows, :] = _pk_nn(t, m["vb"])
            loc_ref[3, rows, :] = _pk_nn(t, m["kbg"])
            loc_ref[4, rows, :] = m["amat"]
            egl_ref[rows, :] = m["egl"]

        _gdn_chunk_loop(nc, act_sc, b_sc, gc_sc, finish)

    return pl.pallas_call(
        body, name=name, grid=(t // seq, DN_HEADS // 2), in_specs=_gdn_in_specs(seq), out_specs=[_pair(seq, 5), _pair(seq)],
        out_shape=[SDS((5, t, DN_HEADS * HEAD), F32), SDS((t, DN_HEADS * HEAD), F32)],
        scratch_shapes=[pltpu.VMEM((3, seq, 2 * HEAD), F32)] + [pltpu.VMEM((seq, 2 * HEAD), F32)] * 2,
        compiler_params=_params(("parallel", "parallel")))(u, u, u, u, cw, cw, cw, alog_row, dt_row)


def _gated_norm2(o, z, gn):
    r = lax.rsqrt(_half_sum(o * o) * (1.0 / HEAD) + EPS)
    return o * r, _sigmoid(z), r


def gdn_scan(loc, egl, u, gn, *, seq, name):
    t = u.shape[0]
    nc = seq // DN_CHUNK

    def body(loc_ref, egl_ref, z_ref, gn_ref, y_ref, o_ref, vn_ref, st_ref):
        gn = gn_ref[...]
        bdm = _bd_mask()

        def step(c, state):
            rows = pl.ds(pl.multiple_of(c * DN_CHUNK, DN_CHUNK), DN_CHUNK)
            st_ref[rows, :] = _fold(state)
            vn = loc_ref[2, rows, :] - _nn(loc_ref[3, rows, :], state)
            o = _nn(loc_ref[0, rows, :], state) + _pk_nn(loc_ref[4, rows, :], vn)
            vn_ref[rows, :] = vn
            o_ref[rows, :] = o
            zz = z_ref[rows, :]
            on, sig, _ = _gated_norm2(o, zz, gn)
            y_ref[rows, :] = on * gn * (zz * sig)
            return state * _row0(egl_ref[rows, :]) + jnp.where(bdm, _tn(loc_ref[1, rows, :], vn), 0.0)

        lax.fori_loop(0, nc, step, jnp.zeros((2 * HEAD, 2 * HEAD), F32))

    zspec = pl.BlockSpec((seq, 2 * HEAD), lambda b, hp: (b, COL_DNZ + hp))
    out = SDS((t, DN_HEADS * HEAD), F32)
    return pl.pallas_call(
        body, name=name, grid=(t // seq, DN_HEADS // 2), in_specs=[_pair(seq, 5), _pair(seq), zspec, _whole((1, 2 * HEAD))],
        out_specs=[_pair(seq)] * 4, out_shape=[out] * 4,
        compiler_params=_params(("parallel", "parallel")))(loc, egl, u, gn)


def gdn_scan_bwd(loc, egl, u, gn, o, vn, states, dy, *, seq, name):
    t = u.shape[0]
    nc = seq // DN_CHUNK

    def body(loc_ref, egl_ref, z_ref, gn_ref, o_ref, vn_ref, st_ref, dy_ref, dloc_ref, degl_ref, dz_ref, dgn_ref):
        @pl.when((pl.program_id(0) == 0) & (pl.program_id(1) == 0))
        def _():
            dgn_ref[...] = jnp.zeros_like(dgn_ref)

        gn = gn_ref[...]
        bdm = _bd_mask()
        shape = (DN_CHUNK, 2 * HEAD)
        tril = _iota2(shape, 0) >= (_iota2(shape, 1) & (HEAD - 1))

        def step(i, carry):
            ds, dgn = carry
            rows = pl.ds(pl.multiple_of((nc - 1 - i) * DN_CHUNK, DN_CHUNK), DN_CHUNK)
            dy, zz, oo = dy_ref[rows, :], z_ref[rows, :], o_ref[rows, :]
            on, sig, r = _gated_norm2(oo, zz, gn)
            sz = zz * sig
            dz_ref[rows, :] = dy * on * gn * (sig * (1.0 + zz * (1.0 - sig)))
            dgn = dgn + jnp.sum(dy * on * sz, axis=0, keepdims=True)
            don = dy * gn * sz
            do = r * (don - on * _half_sum(don * on) * (1.0 / HEAD))
            state, vnew = _bd(st_ref[rows, :]), vn_ref[rows, :]
            qg, kd, w, amat = loc_ref[0, rows, :], loc_ref[1, rows, :], loc_ref[3, rows, :], loc_ref[4, rows, :]
            dvn = _pk_tn(amat, do) + _nn(kd, ds)
            dloc_ref[0, rows, :] = _nt(do, state)
            dloc_ref[1, rows, :] = _nt(vnew, ds)
            dloc_ref[2, rows, :] = dvn
            dloc_ref[3, rows, :] = -_nt(dvn, state)
            dloc_ref[4, rows, :] = jnp.where(tril, _pk_nt(do, vnew), 0.0)
            degl = _half_sum(jnp.sum(state * ds, axis=0, keepdims=True))
            degl_ref[rows, :] = jnp.broadcast_to(degl, shape)
            grow = jnp.where(bdm, _tn(qg, do) - _tn(w, dvn), 0.0)
            return ds * _row0(egl_ref[rows, :]) + grow, dgn

        _, dgn = lax.fori_loop(0, nc, step, (jnp.zeros((2 * HEAD, 2 * HEAD), F32), jnp.zeros((1, 2 * HEAD), F32)))
        dgn_ref[...] += dgn

    zspec = pl.BlockSpec((seq, 2 * HEAD), lambda b, hp: (b, COL_DNZ + hp))
    one = _pair(seq)
    out = SDS((t, DN_HEADS * HEAD), F32)
    return pl.pallas_call(
        body, name=name, grid=(t // seq, DN_HEADS // 2),
        in_specs=[_pair(seq, 5), one, zspec, _whole((1, 2 * HEAD)), one, one, one, one],
        out_specs=[_pair(seq, 5), one, one, _whole((1, 2 * HEAD))],
        out_shape=[SDS((5, t, DN_HEADS * HEAD), F32), out, out, SDS((1, 2 * HEAD), F32)],
        compiler_params=_params(("arbitrary", "arbitrary")))(loc, egl, u, gn, o, vn, states, dy)


def gdn_prep_bwd(u, cw, alog_row, dt_row, dloc, degl, *, seq, name):
    t = u.shape[0]
    nc = seq // DN_CHUNK

    def body(q_ref, k_ref, v_ref, ba_ref, cq_ref, ck_ref, cv_ref, alog_ref, dt_ref, dloc_ref, degl_ref,
             dqkv_ref, dba_ref, dcw_ref, dhs_ref, act_sc, b_sc, gc_sc, c_sc):
        hp = pl.program_id(0)

        @pl.when(pl.program_id(1) == 0)
        def _():
            dcw_ref[...] = jnp.zeros_like(dcw_ref)
            dhs_ref[...] = jnp.zeros_like(dhs_ref)

        pre_refs, cw_refs = (q_ref, k_ref, v_ref), (cq_ref, ck_ref, cv_ref)
        _gdn_inputs(pre_refs, cw_refs, ba_ref, alog_ref, dt_ref, hp, act_sc, b_sc, gc_sc, c_sc)

        def finish(rows, m, tt):
            q, k, v, b = m["q"], m["k"], m["v"], m["b"]
            dqg, dkd, du, dw, da = (dloc_ref[x, rows, :] for x in range(5))
            dm, eg = m["dm"], m["eg"]
            dt = _pk_nt(du, m["vb"]) + _pk_nt(dw, m["kbg"])
            dvb, dkbg = _pk_tn(tt, du), _pk_tn(tt, dw)
            yield
            dtt = _pk_nt(dt, tt, hi=True)
            yield
            dl = jnp.where(m["strict"], -_pk_tn(tt, dtt, hi=True), 0.0)
            yield
            dkk = dl * dm
            dqk = da * dm
            dd = dl * m["kk"] + da * m["qk"]
            dkb = _pk_nn(dkk, k) + dkbg * eg
            dq = _pk_nn(dqk, k) + dqg * eg
            yield
            dk = _pk_tn(dkk, m["kb"]) + _pk_tn(dqk, q) + dkd * m["ekd"] + dkb * b
            db = _half_sum(dkb * k + dvb * v)
            yield
            mx = jnp.where(m["tril"], dd * dm, 0.0)
            tk = _half_sum(dkd * m["kd"])
            colsum = jnp.where(m["eye"], jnp.broadcast_to(jnp.sum(mx, axis=0, keepdims=True), mx.shape), 0.0)
            dgc = _half_sum(mx) - _half_sum(colsum) + _half_sum(dqg * m["qg"] + dkbg * m["kbg"]) - tk
            dglast = jnp.sum(tk, axis=0, keepdims=True) + _row0(degl_ref[rows, :]) * jnp.exp(m["glast"])
            act_sc[0, rows, :] = dq
            act_sc[1, rows, :] = dk
            act_sc[2, rows, :] = dvb * b
            b_sc[rows, :] = db
            gc_sc[rows, :] = dgc + jnp.where(m["row"] == DN_CHUNK - 1, dglast, 0.0)

        _gdn_chunk_loop(nc, act_sc, b_sc, gc_sc, finish)

        beta, g, beta_blk, sp_arg, a_exp, g_blk = _gdn_gates(ba_ref, alog_ref, dt_ref, hp)
        dg = _chunk_rev_cumsum(gc_sc[...])
        lane = _iota2(beta_blk.shape, 1)
        ha = 2 * hp
        db = b_sc[...]
        at = lambda idx, x_a, x_b: (jnp.where(lane == idx, _lane_col(x_a, 0), 0.0)
                                    + jnp.where(lane == idx + 1, _lane_col(x_b, HEAD), 0.0))
        dg_blk = at(DN_HEADS + ha, dg, dg)
        dal = dg_blk * (-a_exp) * _sigmoid(sp_arg)
        dba_ref[...] = at(ha, db, db) * beta_blk * (1.0 - beta_blk) + dal
        dhs_ref[0:1, :] += jnp.sum(dg_blk * g_blk, axis=0, keepdims=True)
        dhs_ref[1:2, :] += jnp.sum(dal, axis=0, keepdims=True)
        for idx in range(3):
            c = c_sc[idx]
            _, sig, hat, r = _gdn_act(c, _DN_SCALE[idx])
            da_ = act_sc[idx]
            if _DN_SCALE[idx] is not None:
                da_ = da_ * _DN_SCALE[idx]
                da_ = r * (da_ - hat * _half_sum(da_ * hat))
            dx, dcw = _conv_bwd(da_ * (sig * (1.0 + c * (1.0 - sig))), pre_refs[idx][...],
                                [cw_refs[idx][k:k + 1, :] for k in range(4)])
            dqkv_ref[idx] = dx
            dcw_ref[idx] += dcw

    pair = DN_HEADS // 2
    in_specs = [_swap(s) for s in _gdn_in_specs(seq)] + [_swap(_pair(seq, 5)), _swap(_pair(seq))]
    return pl.pallas_call(
        body, name=name, grid=(pair, t // seq), in_specs=in_specs,
        out_specs=[_swap(_pair(seq, 3)), pl.BlockSpec((None, seq, 2 * HEAD), lambda hp, b: (hp, b, 0)),
                   pl.BlockSpec((3, 4, 2 * HEAD), lambda hp, b: (0, 0, hp)),
                   pl.BlockSpec((None, 2, 2 * HEAD), lambda hp, b: (hp, 0, 0))],
        out_shape=[SDS((3, t, DN_HEADS * HEAD), F32), SDS((pair, t, 2 * HEAD), F32), SDS((3, 4, DN_HEADS * HEAD), F32),
                   SDS((pair, 2, 2 * HEAD), F32)],
        scratch_shapes=[pltpu.VMEM((3, seq, 2 * HEAD), F32)] + [pltpu.VMEM((seq, 2 * HEAD), F32)] * 2
        + [pltpu.VMEM((3, seq, 2 * HEAD), F32)],
        compiler_params=_params(("arbitrary", "arbitrary")))(u, u, u, u, cw, cw, cw, alog_row, dt_row, dloc, degl)


def mix_out(y_lru, o, y_dn, w_out, h, *, name, tm=512):
    t, d = h.shape
    tm = min(tm, t)

    def body(a_ref, b_ref, c_ref, w_ref, h_ref, o_ref, y_ref):
        y_ref[:, 0:LRU_W] = a_ref[...].astype(BF16)
        y_ref[:, LRU_W:LRU_W + ATT_W] = b_ref[...].astype(BF16)
        y_ref[:, LRU_W + ATT_W:] = c_ref[...].astype(BF16)
        o_ref[...] = h_ref[...] + _nn(y_ref[...], w_ref[...])

    rows = lambda width: pl.BlockSpec((tm, width), lambda i: (i, 0))
    return pl.pallas_call(
        body, name=name, grid=(t // tm,), in_specs=[rows(LRU_W), rows(ATT_W), rows(LRU_W), _whole((d, d)), rows(d)],
        out_specs=[rows(d), rows(d)], out_shape=[SDS((t, d), F32), SDS((t, d), BF16)],
        compiler_params=_params(("parallel",)))(y_lru, o, y_dn, w_out, h)


def mix_out_bwd(dout, w_out, *, name, tm=512):
    t, d = dout.shape
    tm = min(tm, t)

    def body(d_ref, w_ref, a_ref, b_ref, c_ref):
        dy = _nt(d_ref[...], w_ref[...])
        a_ref[...] = dy[:, 0:LRU_W]
        b_ref[...] = dy[:, LRU_W:LRU_W + ATT_W]
        c_ref[...] = dy[:, LRU_W + ATT_W:]

    rows = lambda width: pl.BlockSpec((tm, width), lambda i: (i, 0))
    return pl.pallas_call(
        body, name=name, grid=(t // tm,), in_specs=[rows(d), _whole((d, d))], out_specs=[rows(LRU_W), rows(ATT_W), rows(LRU_W)],
        out_shape=[SDS((t, LRU_W), F32), SDS((t, ATT_W), F32), SDS((t, LRU_W), F32)],
        compiler_params=_params(("parallel",)))(dout, w_out)


def mix_in_bwd(h, gain, dout, w_in, dx, dgate, dq, dk, dv, dqkv, dz, dba, *, name, tm=512):
    t, d = h.shape
    tm = min(tm, t)

    def body(h_ref, g_ref, do_ref, w_ref, dx_ref, dgate_ref, dq_ref, dk_ref, dv_ref, dqkv_ref, dz_ref, dba_ref,
             dh_ref, dg_ref, du_ref):
        @pl.when(pl.program_id(0) == 0)
        def _():
            dg_ref[...] = jnp.zeros_like(dg_ref)

        off = 0
        for piece in (dx_ref[...], dgate_ref[...], dq_ref[...], dk_ref[...], dv_ref[...], dqkv_ref[0], dqkv_ref[1],
                      dqkv_ref[2], dz_ref[...], dba_ref[0] + dba_ref[1]):
            du_ref[:, off:off + piece.shape[1]] = piece.astype(BF16)
            off += piece.shape[1]
        du_ref[:, off:] = jnp.zeros((tm, D_IN_PAD - off), BF16)
        g = g_ref[...]
        _, xh, r = _rms_fwd(h_ref[...], g)
        dh, dg = _rms_bwd(_nt(du_ref[...], w_ref[...]), xh, r, g)
        dh_ref[...] = do_ref[...] + dh
        dg_ref[...] += dg

    rows = lambda width: pl.BlockSpec((tm, width), lambda i: (i, 0))
    return pl.pallas_call(
        body, name=name, grid=(t // tm,),
        in_specs=[rows(d), _whole((1, d)), rows(d), _whole((d, D_IN_PAD)), rows(LRU_W), rows(LRU_W), rows(ATT_W),
                  rows(2 * HEAD), rows(2 * HEAD), pl.BlockSpec((3, tm, DN_HEADS * HEAD), lambda i: (0, i, 0)),
                  rows(DN_HEADS * HEAD), pl.BlockSpec((2, tm, 2 * HEAD), lambda i: (0, i, 0))],
        out_specs=[rows(d), _whole((1, d)), rows(D_IN_PAD)],
        out_shape=[SDS((t, d), F32), SDS((1, d), F32), SDS((t, D_IN_PAD), BF16)],
        compiler_params=_params(("arbitrary",)))(h, gain, dout, w_in, dx, dgate, dq, dk, dv, dqkv, dz, dba)


def _block_diag(w):
    out = jnp.zeros((LRU_W, LRU_W), w.dtype)
    for h in range(LRU_W // HEAD):
        out = lax.dynamic_update_slice(out, w[h], (h * HEAD, h * HEAD))
    return out


def _diag_blocks(w):
    per = LRU_HALF // HEAD
    return jnp.stack([w[h // per, (h % per) * HEAD:(h % per + 1) * HEAD, (h % per) * HEAD:(h % per + 1) * HEAD]
                      for h in range(LRU_W // HEAD)])


def layer_params(w, l, bias):
    row = lambda a: a[l].reshape(1, -1)
    return dict(
        ffn1_norm=row(w["ffn1_norm"]), ffn1=(w["ffn1_w_gate"], w["ffn1_w_up"], w["ffn1_w_down"], l),
        mix_norm=row(w["mix_norm"]), w_in=w["w_in"][l],
        lru=(w["lru_conv_w"][l], row(w["lru_conv_b"]), _block_diag(w["lru_w_a"][l]), row(w["lru_b_a"]),
             _block_diag(w["lru_w_x"][l]), row(w["lru_b_x"]), row(w["lru_lambda"])),
        bias=bias, sink_rows=jnp.repeat(w["attn_sinks"][l], BLOCK_Q).reshape(ATT_HEADS * BLOCK_Q, 1),
        dn_cw=w["dn_conv_w"][l], dn_alog=_ba_row(w["dn_a_log"][l]), dn_dt=_ba_row(w["dn_dt_bias"][l]),
        dn_norm=jnp.tile(row(w["dn_norm"]), (1, 2)), w_out=w["w_out"][l],
        ffn2_norm=row(w["ffn2_norm"]), ffn2=(w["ffn2_w_gate"], w["ffn2_w_up"], w["ffn2_w_down"], l),
        ple_norm=row(w["ple_norm"]), ple_w_gate=w["ple_w_gate"][l], ple_w_proj=w["ple_w_proj"][l])


def _ba_row(per_head):
    return jnp.pad(per_head, (DN_HEADS, 2 * HEAD - 2 * DN_HEADS)).reshape(1, 2 * HEAD)


def mixer_fwd(h, p, nb, seq, tag):
    u, n = norm_matmul(h, p["mix_norm"], p["w_in"], name=f"mix_in_{tag}")
    y_lru = lru_fwd(u, *p["lru"], seq=seq, name=f"lru_fwd_{tag}")
    o = swa_fwd(u, p["bias"], p["sink_rows"], seq=seq, name=f"swa_fwd_{tag}")
    loc, egl = gdn_prep(u, p["dn_cw"], p["dn_alog"], p["dn_dt"], seq=seq, name=f"gdn_prep_{tag}")
    y_dn, o_raw, vn, st = gdn_scan(loc, egl, u, p["dn_norm"], seq=seq, name=f"gdn_scan_{tag}")
    out, ycat = mix_out(y_lru, o, y_dn, p["w_out"], h, name=f"mix_out_{tag}")
    return out, dict(h=h, u=u, n=n, loc=loc, egl=egl, o_raw=o_raw, vn=vn, st=st, ycat=ycat)


def mixer_bwd(dout, s, p, nb, seq, tag):
    u = s["u"]
    dy_lru, do, dy_dn = mix_out_bwd(dout, p["w_out"], name=f"mix_out_dx_{tag}")
    g = {"w_out": matmul(s["ycat"], dout, ta=True, name=f"mix_out_dw_{tag}")}
    dx, dgate, dcw, dwa, dwx, dvec = lru_bwd(u, *p["lru"], dy_lru, seq=seq, name=f"lru_bwd_{tag}")
    g.update(lru_conv_w=dcw, lru_conv_b=dvec[0], lru_w_a=_diag_blocks(dwa), lru_b_a=dvec[1], lru_w_x=_diag_blocks(dwx),
             lru_b_x=dvec[2], lru_lambda=dvec[3])
    dq, dk, dv, dbias, dsink = swa_bwd(u, p["bias"], p["sink_rows"], do, seq=seq, name=f"swa_bwd_{tag}")
    g.update(attn_sinks=dsink.reshape(ATT_HEADS, BLOCK_Q).sum(axis=1), bias=dbias)
    dloc, degl, dz, dgn = gdn_scan_bwd(s["loc"], s["egl"], u, p["dn_norm"], s["o_raw"], s["vn"], s["st"], dy_dn, seq=seq,
                                       name=f"gdn_scan_bwd_{tag}")
    dqkv, dba, dcw3, dhs = gdn_prep_bwd(u, p["dn_cw"], p["dn_alog"], p["dn_dt"], dloc, degl, seq=seq,
                                        name=f"gdn_prep_bwd_{tag}")
    dhs = dhs.sum(axis=0)[:, DN_HEADS:2 * DN_HEADS]
    g.update(dn_conv_w=dcw3.transpose(1, 0, 2).reshape(4, 3 * DN_HEADS * HEAD), dn_a_log=dhs[0], dn_dt_bias=dhs[1],
             dn_norm=dgn[0, :HEAD] + dgn[0, HEAD:])
    dh, dgain, du = mix_in_bwd(s["h"], p["mix_norm"], dout, p["w_in"], dx, dgate, dq, dk, dv, dqkv, dz, dba,
                               name=f"mix_in_bwd_{tag}")
    g["w_in"] = matmul(s["n"], du, ta=True, name=f"mix_in_dw_{tag}")
    g["mix_norm"] = dgain[0]
    return dh, g


SHARDED = ("ffn1_w_gate", "ffn1_w_up", "ffn1_w_down", "w_in", "w_out", "ffn2_w_gate", "ffn2_w_up", "ffn2_w_down",
           "ple_w_gate", "ple_w_proj")
PER_LAYER_SMALL = ("ffn1_norm", "mix_norm", "lru_conv_w", "lru_conv_b", "lru_w_a", "lru_b_a", "lru_w_x", "lru_b_x",
                   "lru_lambda", "attn_sinks", "dn_conv_w", "dn_a_log", "dn_dt_bias", "dn_norm", "ffn2_norm", "ple_norm")


def _col_shards(a):
    r, c = a.shape
    return a.reshape(r, N_CHIP, c // N_CHIP).transpose(1, 0, 2)


def local_step(x, p, target, w, bmap, nb, seq):
    bias = relbias_fwd(w["rel_bias"], bmap, name="relbias_fwd")
    h, saved = x, []
    for l in range(N_LAYER):
        pr = layer_params(w, l, bias)
        s = dict(h0=h)
        h = ffn_fwd(h, pr["ffn1_norm"], *pr["ffn1"], name=f"ffn1_fwd_{l}")
        h, s["mix"] = mixer_fwd(h, pr, nb, seq, l)
        s["h2"] = h
        h = ffn_fwd(h, pr["ffn2_norm"], *pr["ffn2"], name=f"ffn2_fwd_{l}")
        s["h3"] = h
        h = ple_fwd(h, pr["ple_norm"], pr["ple_w_gate"], p[l], pr["ple_w_proj"], name=f"ple_fwd_{l}")
        saved.append((pr, s))
    dh, dgf, loss = loss_head(h, w["final_norm"].reshape(1, -1), target, name="loss_head")

    per_layer, dbias = [None] * N_LAYER, None
    for l in reversed(range(N_LAYER)):
        pr, s = saved[l]
        g = {}
        dout = dh
        dh, n, dga, dpp, dg = ple_bwd(s["h3"], pr["ple_norm"], pr["ple_w_gate"], p[l], pr["ple_w_proj"], dout, name=f"ple_bwd_{l}")
        g["ple_norm"] = dg[0]
        g["ple_w_gate"] = matmul(n, dga, ta=True, name=f"ple_dwg_{l}").reshape(N_CHIP, -1, D_MODEL)
        g["ple_w_proj"] = _col_shards(matmul(p[l], dpp, ta=True, name=f"ple_dwp_{l}"))
        for nm, hin in (("ffn2", s["h2"]), ("ffn1", s["h0"])):
            if nm == "ffn1":
                dh, gm = mixer_bwd(dh, s["mix"], pr, nb, seq, l)
                dbias = gm.pop("bias") if dbias is None else dbias + gm.pop("bias")
                gm["w_in"] = _col_shards(gm["w_in"][:, :D_IN])
                gm["w_out"] = gm["w_out"].reshape(N_CHIP, -1, D_MODEL)
                g.update(gm)
            dout = dh
            dh, n, da, db, sact, dg = ffn_bwd_act(hin, pr[nm + "_norm"], dout, *pr[nm], name=f"{nm}_bwd_act_{l}")
            g[nm + "_norm"] = dg[0]
            g[nm + "_w_gate"], g[nm + "_w_up"], g[nm + "_w_down"] = ffn_bwd_w(n, da, db, sact, dout, name=f"{nm}_bwd_w_{l}")
        per_layer[l] = g
    grads = {k: jnp.stack([per_layer[l][k] for l in range(N_LAYER)]) for k in SHARDED + PER_LAYER_SMALL}
    grads["rel_bias"] = relbias_bwd(dbias, bmap, name="relbias_bwd")[:, :ATT_HEADS]
    grads["final_norm"] = dgf[0]
    return loss, dh, grads


HBM_SPEC = pl.BlockSpec(memory_space=pltpu.HBM)


def _place():
    x, y, c = lax.axis_index("x"), lax.axis_index("y"), lax.axis_index("c")
    chips = [(1 - x, y), (x, 1 - y), (1 - x, 1 - y)]
    return x, y, c, 2 * x + y, (x, y, 1 - c), chips, [2 * cx + cy for cx, cy in chips]


def _remote(src, dst, send_sem, recv_sem, to):
    return pltpu.make_async_remote_copy(src_ref=src, dst_ref=dst, send_sem=send_sem, recv_sem=recv_sem, device_id=to,
                                        device_id_type=MESH)


def place_shard(w, chip_arr, dtype, *, name):
    nl, r, c = w.shape
    tr = next(cand for cand in (256, 128, 64, 32, 16, 8, r) if r % cand == 0)

    def body(chip_ref, w_ref, o_ref):
        o_ref[...] = w_ref[...].astype(dtype)

    return pl.pallas_call(
        body, name=name,
        grid_spec=pltpu.PrefetchScalarGridSpec(
            num_scalar_prefetch=1, grid=(nl, r // tr),
            in_specs=[pl.BlockSpec((None, tr, c), lambda l, i, chip: (l, i, 0))],
            out_specs=pl.BlockSpec((None, None, tr, c), lambda l, i, chip: (chip[0], l, i, 0))),
        out_shape=SDS((N_CHIP, nl, r, c), dtype), compiler_params=_params(("parallel", "parallel")))(chip_arr, w)


def allgather_shards(shards, *, name):
    n = len(shards)

    def body(*refs):
        outs = refs[n:2 * n]
        send, recv, fsend, frecv = refs[2 * n:]
        x, y, c, me, sib, chips, cids = _place()
        first, passed = [], []
        for k in range(n):
            for j, chip in enumerate(chips):
                mine = outs[k].at[me, c]
                first.append(_remote(mine, mine, send.at[3 * k + j], recv.at[3 * k + j], (*chip, c)))
                first[-1].start()
        for k in range(n):
            for j in range(3):
                piece = outs[k].at[cids[j], c]
                _remote(piece, piece, send.at[3 * k + j], recv.at[3 * k + j], sib).wait_recv()
                passed.append(_remote(piece, piece, fsend.at[3 * k + j], frecv.at[3 * k + j], sib))
                passed[-1].start()
        for k in range(n):
            for j in range(3):
                piece = outs[k].at[cids[j], 1 - c]
                _remote(piece, piece, fsend.at[3 * k + j], frecv.at[3 * k + j], sib).wait_recv()
        for cp in first + passed:
            cp.wait_send()

    return pl.pallas_call(
        body, name=name, in_specs=[HBM_SPEC] * n, out_specs=[HBM_SPEC] * n,
        out_shape=[SDS(s.shape, s.dtype) for s in shards], input_output_aliases={k: k for k in range(n)},
        scratch_shapes=[pltpu.SemaphoreType.DMA((3 * n,))] * 4)(*shards)


def exchange_layers(gs, *, name):
    n = len(gs)

    def body(*refs):
        ins, outs, (send, recv) = refs[:n], refs[n:2 * n], refs[2 * n:]
        x, y, c, me, sib, chips, cids = _place()
        cps = [_remote(ins[k].at[1 - c], outs[k], send.at[k], recv.at[k], sib) for k in range(n)]
        for cp in cps:
            cp.start()
        for cp in cps:
            cp.wait()

    return pl.pallas_call(
        body, name=name, in_specs=[HBM_SPEC] * n, out_specs=[HBM_SPEC] * n,
        out_shape=[SDS(g.shape[1:], g.dtype) for g in gs], scratch_shapes=[pltpu.SemaphoreType.DMA((n,))] * 2)(*gs)


def reduce_to_shards(ss, *, name):
    n = len(ss)

    def body(*refs):
        ins, outs, (send, recv) = refs[:n], refs[n:2 * n], refs[2 * n:]
        x, y, c, me, sib, chips, cids = _place()
        cps = []
        for k in range(n):
            for j, chip in enumerate(chips):
                cps.append(_remote(ins[k].at[cids[j]], outs[k].at[j], send.at[3 * k + j], recv.at[3 * k + j], (*chip, c)))
                cps[-1].start()
        for k in range(n):
            for j in range(3):
                slot = outs[k].at[j]
                _remote(slot, slot, send.at[3 * k + j], recv.at[3 * k + j], sib).wait_recv()
        for cp in cps:
            cp.wait_send()

    return pl.pallas_call(
        body, name=name, in_specs=[HBM_SPEC] * n, out_specs=[HBM_SPEC] * n,
        out_shape=[SDS((N_CHIP - 1,) + s.shape[1:], s.dtype) for s in ss],
        scratch_shapes=[pltpu.SemaphoreType.DMA((3 * n,))] * 2)(*ss)


def share_layers(fs, *, name):
    n = len(fs)

    def body(*refs):
        outs, (send, recv) = refs[n:2 * n], refs[2 * n:]
        x, y, c, me, sib, chips, cids = _place()
        cps = [_remote(outs[k].at[c], outs[k].at[c], send.at[k], recv.at[k], sib) for k in range(n)]
        for cp in cps:
            cp.start()
        for k in range(n):
            theirs = outs[k].at[1 - c]
            _remote(theirs, theirs, send.at[k], recv.at[k], sib).wait_recv()
        for cp in cps:
            cp.wait_send()

    return pl.pallas_call(
        body, name=name, in_specs=[HBM_SPEC] * n, out_specs=[HBM_SPEC] * n, out_shape=[SDS(f.shape, f.dtype) for f in fs],
        input_output_aliases={k: k for k in range(n)}, scratch_shapes=[pltpu.SemaphoreType.DMA((n,))] * 2)(*fs)


N_DEV = 8


def allreduce_small(buf, *, name):
    rows = buf.shape[0]

    def body(in_ref, out_ref, gath, send, recv):
        x, y, c = lax.axis_index("x"), lax.axis_index("y"), lax.axis_index("c")
        mine = 4 * x + 2 * y + c
        gath[mine] = in_ref[...]
        cps = []
        for k in range(1, N_DEV):
            to = (x ^ (k >> 2), y ^ ((k >> 1) & 1), c ^ (k & 1))
            cps.append(_remote(in_ref, gath.at[mine], send.at[k - 1], recv.at[k - 1], to))
            cps[-1].start()
        for k in range(1, N_DEV):
            theirs = gath.at[4 * (x ^ (k >> 2)) + 2 * (y ^ ((k >> 1) & 1)) + (c ^ (k & 1))]
            _remote(theirs, theirs, send.at[k - 1], recv.at[k - 1], (x, y, c)).wait_recv()
        for cp in cps:
            cp.wait_send()
        acc = gath[0]
        for d in range(1, N_DEV):
            acc = acc + gath[d]
        out_ref[...] = acc

    vm = pl.BlockSpec(memory_space=pltpu.VMEM)
    return pl.pallas_call(
        body, name=name, in_specs=[vm], out_specs=vm, out_shape=SDS(buf.shape, F32),
        scratch_shapes=[pltpu.VMEM((N_DEV, rows, 128), F32), pltpu.SemaphoreType.DMA((N_DEV - 1,)),
                        pltpu.SemaphoreType.DMA((N_DEV - 1,))])(buf)


def add_sibling(g, r, c_arr, *, name, tr=256):
    _, m, cdim = g.shape
    assert m % tr == 0

    def body(c_ref, g_ref, r_ref, o_ref):
        o_ref[...] = (g_ref[...] + r_ref[...]).astype(o_ref.dtype)

    return pl.pallas_call(
        body, name=name,
        grid_spec=pltpu.PrefetchScalarGridSpec(
            num_scalar_prefetch=1, grid=(m // tr,),
            in_specs=[pl.BlockSpec((None, tr, cdim), lambda i, c: (c[0], i, 0)), pl.BlockSpec((tr, cdim), lambda i, c: (i, 0))],
            out_specs=pl.BlockSpec((tr, cdim), lambda i, c: (i, 0))),
        out_shape=SDS((m, cdim), BF16), compiler_params=_params(("parallel",)))(c_arr, g, r)


def sum_slots(own, r, place_arr, *, name, tr=256):
    _, m, cdim = r.shape
    tr = next(cand for cand in (tr, 128, 64, 32, 16, 8) if m % cand == 0)

    def body(p_ref, own_ref, r_ref, o_ref):
        o_ref[...] = ((own_ref[...].astype(F32) + r_ref[0].astype(F32)) + r_ref[1].astype(F32)) + r_ref[2].astype(F32)

    return pl.pallas_call(
        body, name=name,
        grid_spec=pltpu.PrefetchScalarGridSpec(
            num_scalar_prefetch=1, grid=(m // tr,),
            in_specs=[pl.BlockSpec((None, tr, cdim), lambda i, p: (p[0], i, 0)),
                      pl.BlockSpec((N_CHIP - 1, tr, cdim), lambda i, p: (0, i, 0))],
            out_specs=pl.BlockSpec((None, tr, cdim), lambda i, p: (p[1], i, 0))),
        out_shape=SDS((N_LAYER, m, cdim), F32), compiler_params=_params(("parallel",)))(place_arr, own, r)


WEIGHTS = ("ffn1_norm", "ffn1_w_gate", "ffn1_w_up", "ffn1_w_down", "mix_norm", "w_in", "lru_conv_w", "lru_conv_b", "lru_w_a",
           "lru_b_a", "lru_w_x", "lru_b_x", "lru_lambda", "attn_sinks", "rel_bias", "dn_conv_w", "dn_a_log", "dn_dt_bias",
           "dn_norm", "w_out", "ffn2_norm", "ffn2_w_gate", "ffn2_w_up", "ffn2_w_down", "ple_norm", "ple_w_gate",
           "ple_w_proj", "final_norm")
CONV_SHARDED = ("lru_conv_w", "dn_conv_w")
SMALL = tuple(k for k in WEIGHTS if k not in SHARDED)


def _pack(arrs):
    flat = []
    for a in arrs:
        v = a.reshape(-1)
        flat.append(jnp.pad(v, (0, -v.shape[0] % 128)))
    v = jnp.concatenate(flat)
    v = jnp.pad(v, (0, -v.shape[0] % 1024))
    return v.reshape(-1, 128)


def _unpack(buf, shapes):
    v, out, off = buf.reshape(-1), [], 0
    for s in shapes:
        n = int(np.prod(s))
        out.append(v[off:off + n].reshape(s))
        off += n + (-n % 128)
    return out


def _chip_cols(a):
    n, l, r, c = a.shape
    return a.transpose(1, 2, 0, 3).reshape(l, r, n * c)


def _chip_rows(a):
    n, l, r, c = a.shape
    return a.transpose(1, 0, 2, 3).reshape(l, n * r, c)


def kernel(x, p, ffn1_norm, ffn1_w_gate, ffn1_w_up, ffn1_w_down, mix_norm, w_in, lru_conv_w, lru_conv_b, lru_w_a, lru_b_a, lru_w_x, lru_b_x, lru_lambda, attn_sinks, rel_bias, dn_conv_w, dn_a_log, dn_dt_bias, dn_norm, w_out, ffn2_norm, ffn2_w_gate, ffn2_w_up, ffn2_w_down, ple_norm, ple_w_gate, ple_w_proj, final_norm, loss_target, m_ffn1_norm, m_ffn1_w_gate, m_ffn1_w_up, m_ffn1_w_down, m_mix_norm, m_w_in, m_lru_conv_w, m_lru_conv_b, m_lru_w_a, m_lru_b_a, m_lru_w_x, m_lru_b_x, m_lru_lambda, m_attn_sinks, m_rel_bias, m_dn_conv_w, m_dn_a_log, m_dn_dt_bias, m_dn_norm, m_w_out, m_ffn2_norm, m_ffn2_w_gate, m_ffn2_w_up, m_ffn2_w_down, m_ple_norm, m_ple_w_gate, m_ple_w_proj, m_final_norm, v_ffn1_norm, v_ffn1_w_gate, v_ffn1_w_up, v_ffn1_w_down, v_mix_norm, v_w_in, v_lru_conv_w, v_lru_conv_b, v_lru_w_a, v_lru_b_a, v_lru_w_x, v_lru_b_x, v_lru_lambda, v_attn_sinks, v_rel_bias, v_dn_conv_w, v_dn_a_log, v_dn_dt_bias, v_dn_norm, v_w_out, v_ffn2_norm, v_ffn2_w_gate, v_ffn2_w_up, v_ffn2_w_down, v_ple_norm, v_ple_w_gate, v_ple_w_proj, v_final_norm):
    given = dict(locals())
    ws = {k: given[k] for k in WEIGHTS}
    ms = {k: given["m_" + k] for k in WEIGHTS}
    vs = {k: given["v_" + k] for k in WEIGHTS}
    nb, seq, d = x.shape
    t = nb * seq
    cx, cy, cc = lax.axis_index("x"), lax.axis_index("y"), lax.axis_index("c")
    chip = 2 * cx + cy

    chip_arr = chip.astype(jnp.int32).reshape(1)
    placed = [place_shard(ws[k], chip_arr, F32 if k in CONV_SHARDED else BF16, name=f"place_{k}")
              for k in SHARDED + CONV_SHARDED]
    gathered = allgather_shards(placed, name="allgather_weights")
    full = dict(zip(SHARDED + CONV_SHARDED, gathered))
    for k in ("w_in", "ple_w_proj", "lru_conv_w", "dn_conv_w"):
        full[k] = _chip_cols(full[k])
    for k in ("w_out", "ple_w_gate"):
        full[k] = _chip_rows(full[k])
    full["w_in"] = jnp.pad(full["w_in"], ((0, 0), (0, 0), (0, D_IN_PAD - D_IN)))
    for k in SMALL:
        if k not in CONV_SHARDED:
            full[k] = ws[k]

    bmap = jnp.asarray(_rel_bucket_map())
    loss, gx, grads = local_step(x.reshape(t, d), p.reshape(N_LAYER, t, PLE_DIM), loss_target.reshape(t, d), full, bmap, nb, seq)

    gs = [grads[k] for k in SHARDED]
    flat = lambda a, lead: a.reshape(a.shape[:lead] + (-1, a.shape[-1]))
    theirs = exchange_layers(gs, name="rs_exchange_layers")
    c_arr = cc.astype(jnp.int32).reshape(1)
    sums = [add_sibling(flat(g, 1), flat(r, 0), c_arr, name=f"rs_add_{k}").reshape(r.shape)
            for k, g, r in zip(SHARDED, gs, theirs)]
    slots = reduce_to_shards(sums, name="rs_reduce_to_shards")
    place_arr = jnp.stack([chip, cc]).astype(jnp.int32)
    mine = [sum_slots(flat(s, 1), flat(r, 1), place_arr, name=f"rs_sum_{k}").reshape((N_LAYER,) + r.shape[1:])
            for k, s, r in zip(SHARDED, sums, slots)]
    g_out = dict(zip(SHARDED, share_layers(mine, name="rs_share_layers")))

    small_shapes = [grads[k].shape for k in SMALL]
    g_small = dict(zip(SMALL, _unpack(allreduce_small(_pack([grads[k] for k in SMALL]), name="allreduce_small"), small_shapes)))
    for k in CONV_SHARDED:
        width = ws[k].shape[-1]
        g_small[k] = lax.dynamic_slice_in_dim(g_small[k], chip * width, width, axis=2)
    g_out.update(g_small)

    delta, new_m, new_v = {}, {}, {}
    for k in SHARDED:
        two_d = lambda a: a.reshape(-1, a.shape[-1])
        res = adamw(two_d(ws[k]), two_d(g_out[k]), two_d(ms[k]), two_d(vs[k]), name=f"adamw_{k}")
        delta[k], new_m[k], new_v[k] = (r.reshape(ws[k].shape) for r in res)
    shapes = [ws[k].shape for k in SMALL]
    res = adamw(*[_pack([src[k] for k in SMALL]) for src in (ws, g_out, ms, vs)], name="adamw_small")
    for dst, r in zip((delta, new_m, new_v), res):
        dst.update(zip(SMALL, _unpack(r, shapes)))

    total = lax.psum(loss[0, 0], ("x", "y", "c"))
    return (total, gx.reshape(nb, seq, d), *[g_out[k] for k in WEIGHTS], *[delta[k] for k in WEIGHTS],
            *[new_m[k] for k in WEIGHTS], *[new_v[k] for k in WEIGHTS])
```

```python
import functools
import math

import numpy as np
import jax
import jax.numpy as jnp
from jax import lax
from jax.experimental import pallas as pl
from jax.experimental.pallas import tpu as pltpu

F32 = jnp.float32
BF16 = jnp.bfloat16

EPS = 1e-6
D_MODEL = 1024
D_FF = 2816
N_CHIP = 4
FF_BLK = D_FF // N_CHIP
HEAD = 64
LRU_W = 256
ATT_W = 512
ATT_HEADS = 8
KV_HEADS = 2
ATT_GROUP = 4
BLOCK_Q = 128
DN_HEADS = 4
DN_CHUNK = 64
D_IN = 2312
D_IN_PAD = 2560
PLE_DIM = 256
REL_BUCKETS = 32
LRU_C = 8.0
N_LAYER = 2

ADAM_LR, ADAM_B1, ADAM_B2, ADAM_EPS, ADAM_WD, ADAM_STEP = 0.001, 0.9, 0.999, 1e-08, 0.01, 10

VMEM_LIMIT = 56 << 20
MESH = pl.DeviceIdType.MESH
SDS = jax.ShapeDtypeStruct


def _dot(a, b, ca=1, cb=0, hi=False):
    dims = (((ca,), (cb,)), ((), ()))
    one = lambda u, v: lax.dot_general(u, v, dims, preferred_element_type=F32)
    a_hi, b_hi = a.astype(BF16), b.astype(BF16)
    if not hi:
        return one(a_hi, b_hi)
    a_lo = (a - a_hi.astype(F32)).astype(BF16)
    b_lo = (b - b_hi.astype(F32)).astype(BF16)
    return one(a_hi, b_hi) + (one(a_hi, b_lo) + one(a_lo, b_hi))


def _nn(a, b, hi=False):
    return _dot(a, b, 1, 0, hi)


def _nt(a, b, hi=False):
    return _dot(a, b, 1, 1, hi)


def _tn(a, b, hi=False):
    return _dot(a, b, 0, 0, hi)


def _sigmoid(x):
    return jax.nn.sigmoid(x)


def _softplus(x):
    return jnp.maximum(x, 0.0) + jnp.log1p(jnp.exp(-jnp.abs(x)))


def _neg_expm1(z):
    series = -z * (1.0 + z * (0.5 + z * (1.0 / 6.0 + z * (1.0 / 24.0 + z * (1.0 / 120.0)))))
    return jnp.where(z > -0.05, series, 1.0 - jnp.exp(z))


_GELU_C = math.sqrt(2.0 / math.pi)


def _gelu(x):
    t = jnp.tanh(_GELU_C * (x + 0.044715 * x * x * x))
    return 0.5 * x * (1.0 + t), t


def _gelu_grad(x, t):
    return 0.5 * (1.0 + t) + 0.5 * x * (1.0 - t * t) * _GELU_C * (1.0 + 3.0 * 0.044715 * x * x)


def _rms_fwd(h, g):
    r = lax.rsqrt(jnp.mean(h * h, axis=-1, keepdims=True) + EPS)
    xh = h * r
    return xh * g, xh, r


def _rms_bwd(dn, xh, r, g):
    dxh = dn * g
    dh = r * (dxh - xh * jnp.mean(dxh * xh, axis=-1, keepdims=True))
    return dh, jnp.sum(dn * xh, axis=0, keepdims=True)


def _shift_down(x, d, fill=0.0):
    row = lax.broadcasted_iota(jnp.int32, x.shape, 0)
    return jnp.where(row >= d, pltpu.roll(x, d, 0), fill)


def _shift_up(x, d, fill=0.0):
    n = x.shape[0]
    row = lax.broadcasted_iota(jnp.int32, x.shape, 0)
    return jnp.where(row < n - d, pltpu.roll(x, n - d, 0), fill)


def _conv_fwd(x, w):
    y = x * w[3]
    for k in range(3):
        y = y + _shift_down(x, 3 - k) * w[k]
    return y


def _conv_bwd(dy, x, w):
    dx = dy * w[3]
    rows = [None] * 4
    rows[3] = jnp.sum(dy * x, axis=0, keepdims=True)
    for k in range(3):
        dx = dx + _shift_up(dy, 3 - k) * w[k]
        rows[k] = jnp.sum(dy * _shift_down(x, 3 - k), axis=0, keepdims=True)
    r4 = lax.broadcasted_iota(jnp.int32, (4, x.shape[1]), 0)
    dw = jnp.zeros((4, x.shape[1]), F32)
    for k in range(4):
        dw = jnp.where(r4 == k, rows[k], dw)
    return dx, dw


def _params(sem=None, vmem=VMEM_LIMIT):
    return pltpu.CompilerParams(dimension_semantics=sem, vmem_limit_bytes=vmem)


def _whole(shape):
    nd = len(shape)
    return pl.BlockSpec(shape, lambda *_: (0,) * nd)


def matmul(a, b, *, name, ta=False, tb=False, residual=None, out_dtype=F32, tm=512, tn=512, tk=512):
    m, k = (a.shape[1], a.shape[0]) if ta else a.shape
    n = b.shape[0] if tb else b.shape[1]
    tm, tn, tk = min(tm, m), min(tn, n), min(tk, k)
    assert m % tm == 0 and n % tn == 0 and k % tk == 0, (m, n, k, tm, tn, tk)
    nk = k // tk

    def body(*refs):
        if residual is None:
            a_ref, b_ref, o_ref, acc = refs
        else:
            a_ref, b_ref, r_ref, o_ref, acc = refs
        kk = pl.program_id(2)

        @pl.when(kk == 0)
        def _():
            acc[...] = jnp.zeros_like(acc)

        acc[...] += _dot(a_ref[...], b_ref[...], 0 if ta else 1, 1 if tb else 0)

        @pl.when(kk == nk - 1)
        def _():
            out = acc[...]
            if residual is not None:
                out = out + r_ref[...]
            o_ref[...] = out.astype(out_dtype)

    a_spec = pl.BlockSpec((tk, tm), lambda i, j, kk: (kk, i)) if ta else pl.BlockSpec((tm, tk), lambda i, j, kk: (i, kk))
    b_spec = pl.BlockSpec((tn, tk), lambda i, j, kk: (j, kk)) if tb else pl.BlockSpec((tk, tn), lambda i, j, kk: (kk, j))
    o_spec = pl.BlockSpec((tm, tn), lambda i, j, kk: (i, j))
    in_specs, args = [a_spec, b_spec], [a, b]
    if residual is not None:
        in_specs.append(o_spec)
        args.append(residual)
    return pl.pallas_call(
        body, name=name, grid=(m // tm, n // tn, nk), in_specs=in_specs, out_specs=o_spec,
        out_shape=SDS((m, n), out_dtype), scratch_shapes=[pltpu.VMEM((tm, tn), F32)],
        compiler_params=_params(("parallel", "parallel", "arbitrary")))(*args)


def norm_matmul(h, gain, w, *, name, tm=512, tn=512):
    t, d = h.shape
    tm = min(tm, t)
    n = w.shape[1]
    assert t % tm == 0 and n % tn == 0

    def body(h_ref, g_ref, w_ref, u_ref, n_ref):
        @pl.when(pl.program_id(1) == 0)
        def _():
            n_ref[...] = _rms_fwd(h_ref[...], g_ref[...])[0].astype(BF16)

        u_ref[...] = _nn(n_ref[...], w_ref[...])

    return pl.pallas_call(
        body, name=name, grid=(t // tm, n // tn),
        in_specs=[pl.BlockSpec((tm, d), lambda i, j: (i, 0)), _whole((1, d)), pl.BlockSpec((d, tn), lambda i, j: (0, j))],
        out_specs=[pl.BlockSpec((tm, tn), lambda i, j: (i, j)), pl.BlockSpec((tm, d), lambda i, j: (i, 0))],
        out_shape=[SDS((t, n), F32), SDS((t, d), BF16)],
        compiler_params=_params(("parallel", "arbitrary")))(h, gain, w)


def rms_bwd(h, gain, dn, dres, *, name, tm=512):
    t, d = h.shape
    tm = min(tm, t)

    def body(h_ref, g_ref, dn_ref, dr_ref, dh_ref, dg_ref):
        @pl.when(pl.program_id(0) == 0)
        def _():
            dg_ref[...] = jnp.zeros_like(dg_ref)

        g = g_ref[...]
        _, xh, r = _rms_fwd(h_ref[...], g)
        dh, dg = _rms_bwd(dn_ref[...], xh, r, g)
        dh_ref[...] = dr_ref[...] + dh
        dg_ref[...] += dg

    row = pl.BlockSpec((tm, d), lambda i: (i, 0))
    return pl.pallas_call(
        body, name=name, grid=(t // tm,), in_specs=[row, _whole((1, d)), row, row],
        out_specs=[row, _whole((1, d))], out_shape=[SDS((t, d), F32), SDS((1, d), F32)],
        compiler_params=_params(("arbitrary",)))(h, gain, dn, dres)


def ffn_fwd(h, gain, wg, wu, wd, *, name, tm=512):
    t, d = h.shape
    tm = min(tm, t)

    def body(h_ref, g_ref, wg_ref, wu_ref, wd_ref, o_ref, n_sc, acc):
        j = pl.program_id(1)

        @pl.when(j == 0)
        def _():
            n_sc[...] = _rms_fwd(h_ref[...], g_ref[...])[0].astype(BF16)
            acc[...] = jnp.zeros_like(acc)

        n = n_sc[...]
        a = _nn(n, wg_ref[...])
        b = _nn(n, wu_ref[...])
        acc[...] += _nn(a * _sigmoid(a) * b, wd_ref[...])

        @pl.when(j == N_CHIP - 1)
        def _():
            o_ref[...] = h_ref[...] + 0.5 * acc[...]

    row = pl.BlockSpec((tm, d), lambda i, j: (i, 0))
    return pl.pallas_call(
        body, name=name, grid=(t // tm, N_CHIP),
        in_specs=[row, _whole((1, d)),
                  pl.BlockSpec((None, d, FF_BLK), lambda i, j: (j, 0, 0)),
                  pl.BlockSpec((None, d, FF_BLK), lambda i, j: (j, 0, 0)),
                  pl.BlockSpec((None, FF_BLK, d), lambda i, j: (j, 0, 0))],
        out_specs=row, out_shape=SDS((t, d), F32),
        scratch_shapes=[pltpu.VMEM((tm, d), BF16), pltpu.VMEM((tm, d), F32)],
        compiler_params=_params(("parallel", "arbitrary")))(h, gain, wg, wu, wd)


def ffn_bwd_act(h, gain, dout, wg, wu, wd, *, name, tm=512):
    t, d = h.shape
    tm = min(tm, t)

    def body(h_ref, g_ref, do_ref, wg_ref, wu_ref, wd_ref, dh_ref, n_ref, da_ref, db_ref, s_ref, dg_ref, dn_acc):
        i, j = pl.program_id(0), pl.program_id(1)

        @pl.when((i == 0) & (j == 0))
        def _():
            dg_ref[...] = jnp.zeros_like(dg_ref)

        @pl.when(j == 0)
        def _():
            n_ref[...] = _rms_fwd(h_ref[...], g_ref[...])[0].astype(BF16)
            dn_acc[...] = jnp.zeros_like(dn_acc)

        n = n_ref[...]
        a = _nn(n, wg_ref[...])
        b = _nn(n, wu_ref[...])
        sig = _sigmoid(a)
        sa = a * sig
        ds = _nt(0.5 * do_ref[...], wd_ref[...])
        db = ds * sa
        da = ds * b * (sig * (1.0 + a * (1.0 - sig)))
        s_ref[...] = (sa * b).astype(BF16)
        da_ref[...] = da.astype(BF16)
        db_ref[...] = db.astype(BF16)
        dn_acc[...] += _nt(da, wg_ref[...]) + _nt(db, wu_ref[...])

        @pl.when(j == N_CHIP - 1)
        def _():
            g = g_ref[...]
            _, xh, r = _rms_fwd(h_ref[...], g)
            dh, dg = _rms_bwd(dn_acc[...], xh, r, g)
            dh_ref[...] = do_ref[...] + dh
            dg_ref[...] += dg

    row = pl.BlockSpec((tm, d), lambda i, j: (i, 0))
    blk = pl.BlockSpec((None, tm, FF_BLK), lambda i, j: (j, i, 0))
    act = SDS((N_CHIP, t, FF_BLK), BF16)
    return pl.pallas_call(
        body, name=name, grid=(t // tm, N_CHIP),
        in_specs=[row, _whole((1, d)), row,
                  pl.BlockSpec((None, d, FF_BLK), lambda i, j: (j, 0, 0)),
                  pl.BlockSpec((None, d, FF_BLK), lambda i, j: (j, 0, 0)),
                  pl.BlockSpec((None, FF_BLK, d), lambda i, j: (j, 0, 0))],
        out_specs=[row, row, blk, blk, blk, _whole((1, d))],
        out_shape=[SDS((t, d), F32), SDS((t, d), BF16), act, act, act, SDS((1, d), F32)],
        scratch_shapes=[pltpu.VMEM((tm, d), F32)],
        compiler_params=_params(("arbitrary", "arbitrary")))(h, gain, dout, wg, wu, wd)


def ffn_bwd_w(n, da, db, s, dout, *, name, tk=512):
    t, d = n.shape
    tk = min(tk, t)

    def body(n_ref, da_ref, db_ref, s_ref, do_ref, dwg_ref, dwu_ref, dwd_ref):
        @pl.when(pl.program_id(1) == 0)
        def _():
            dwg_ref[...] = jnp.zeros_like(dwg_ref)
            dwu_ref[...] = jnp.zeros_like(dwu_ref)
            dwd_ref[...] = jnp.zeros_like(dwd_ref)

        nn = n_ref[...]
        dwg_ref[...] += _tn(nn, da_ref[...])
        dwu_ref[...] += _tn(nn, db_ref[...])
        dwd_ref[...] += _tn(s_ref[...], 0.5 * do_ref[...])

    row = pl.BlockSpec((tk, d), lambda j, kk: (kk, 0))
    blk = pl.BlockSpec((None, tk, FF_BLK), lambda j, kk: (j, kk, 0))
    return pl.pallas_call(
        body, name=name, grid=(N_CHIP, t // tk), in_specs=[row, blk, blk, blk, row],
        out_specs=[pl.BlockSpec((None, d, FF_BLK), lambda j, kk: (j, 0, 0)),
                   pl.BlockSpec((None, d, FF_BLK), lambda j, kk: (j, 0, 0)),
                   pl.BlockSpec((None, FF_BLK, d), lambda j, kk: (j, 0, 0))],
        out_shape=[SDS((N_CHIP, d, FF_BLK), F32), SDS((N_CHIP, d, FF_BLK), F32), SDS((N_CHIP, FF_BLK, d), F32)],
        compiler_params=_params(("parallel", "arbitrary")))(n, da, db, s, dout)


def ple_fwd(h, gain, wpg, pl_in, wpp, *, name, tm=512):
    t, d = h.shape
    tm = min(tm, t)
    pd = pl_in.shape[1]

    def body(h_ref, g_ref, wpg_ref, p_ref, wpp_ref, o_ref):
        hh = h_ref[...]
        n = _rms_fwd(hh, g_ref[...])[0]
        gate = _sigmoid(_nn(n, wpg_ref[...]))
        o_ref[...] = hh + gate * _nn(p_ref[...], wpp_ref[...])

    row = pl.BlockSpec((tm, d), lambda i: (i, 0))
    return pl.pallas_call(
        body, name=name, grid=(t // tm,),
        in_specs=[row, _whole((1, d)), _whole((d, d)), pl.BlockSpec((tm, pd), lambda i: (i, 0)), _whole((pd, d))],
        out_specs=row, out_shape=SDS((t, d), F32), compiler_params=_params(("parallel",)))(h, gain, wpg, pl_in, wpp)


def ple_bwd(h, gain, wpg, pl_in, wpp, dout, *, name, tm=512):
    t, d = h.shape
    tm = min(tm, t)
    pd = pl_in.shape[1]

    def body(h_ref, g_ref, wpg_ref, p_ref, wpp_ref, do_ref, dh_ref, n_ref, dga_ref, dpp_ref, dg_ref):
        @pl.when(pl.program_id(0) == 0)
        def _():
            dg_ref[...] = jnp.zeros_like(dg_ref)

        g = g_ref[...]
        n, xh, r = _rms_fwd(h_ref[...], g)
        gate = _sigmoid(_nn(n, wpg_ref[...]))
        pp = _nn(p_ref[...], wpp_ref[...])
        do = do_ref[...]
        dga = do * pp * gate * (1.0 - gate)
        dh, dg = _rms_bwd(_nt(dga, wpg_ref[...]), xh, r, g)
        dh_ref[...] = do + dh
        n_ref[...] = n.astype(BF16)
        dga_ref[...] = dga.astype(BF16)
        dpp_ref[...] = (do * gate).astype(BF16)
        dg_ref[...] += dg

    row = pl.BlockSpec((tm, d), lambda i: (i, 0))
    return pl.pallas_call(
        body, name=name, grid=(t // tm,),
        in_specs=[row, _whole((1, d)), _whole((d, d)), pl.BlockSpec((tm, pd), lambda i: (i, 0)), _whole((pd, d)), row],
        out_specs=[row, row, row, row, _whole((1, d))],
        out_shape=[SDS((t, d), F32), SDS((t, d), BF16), SDS((t, d), BF16), SDS((t, d), BF16), SDS((1, d), F32)],
        compiler_params=_params(("arbitrary",)))(h, gain, wpg, pl_in, wpp, dout)


def loss_head(h, gain, target, *, name, tm=512):
    t, d = h.shape
    tm = min(tm, t)

    def body(h_ref, g_ref, t_ref, dh_ref, dg_ref, l_ref):
        @pl.when(pl.program_id(0) == 0)
        def _():
            dg_ref[...] = jnp.zeros_like(dg_ref)
            l_ref[...] = jnp.zeros_like(l_ref)

        g = g_ref[...]
        y, xh, r = _rms_fwd(h_ref[...], g)
        err = y - t_ref[...]
        l_ref[...] += 0.5 * jnp.sum(jnp.mean(err * err, axis=-1, keepdims=True), axis=0, keepdims=True)
        dh, dg = _rms_bwd(err * (1.0 / d), xh, r, g)
        dh_ref[...] = dh
        dg_ref[...] += dg

    row = pl.BlockSpec((tm, d), lambda i: (i, 0))
    return pl.pallas_call(
        body, name=name, grid=(t // tm,), in_specs=[row, _whole((1, d)), row],
        out_specs=[row, _whole((1, d)), _whole((1, 1))],
        out_shape=[SDS((t, d), F32), SDS((1, d), F32), SDS((1, 1), F32)],
        compiler_params=_params(("arbitrary",)))(h, gain, target)


def adamw(w, g, m, v, *, name):
    r, c = w.shape
    tr = r
    for cand in (512, 256, 128, 64, 32, 16, 8):
        if r % cand == 0:
            tr = cand
            break

    def body(w_ref, g_ref, m_ref, v_ref, d_ref, nm_ref, nv_ref):
        gg = g_ref[...]
        mm = ADAM_B1 * m_ref[...] + (1.0 - ADAM_B1) * gg
        vv = ADAM_B2 * v_ref[...] + (1.0 - ADAM_B2) * (gg * gg)
        m_hat = mm / (1.0 - ADAM_B1 ** ADAM_STEP)
        v_hat = vv / (1.0 - ADAM_B2 ** ADAM_STEP)
        d_ref[...] = -ADAM_LR * (m_hat / (jnp.sqrt(v_hat) + ADAM_EPS) + ADAM_WD * w_ref[...])
        nm_ref[...] = mm
        nv_ref[...] = vv

    blk = pl.BlockSpec((tr, c), lambda i: (i, 0))
    out = SDS((r, c), F32)
    return pl.pallas_call(body, name=name, grid=(r // tr,), in_specs=[blk] * 4, out_specs=[blk] * 3,
                          out_shape=[out, out, out], compiler_params=_params(("parallel",)))(w, g, m, v)


def _scan_fwd(a, b):
    d = 1
    while d < a.shape[0]:
        b = a * _shift_down(b, d, 0.0) + b
        a = a * _shift_down(a, d, 1.0)
        d *= 2
    return b


def _scan_rev(a, b):
    d = 1
    while d < a.shape[0]:
        b = a * _shift_up(b, d, 0.0) + b
        a = a * _shift_up(a, d, 1.0)
        d *= 2
    return b


LRU_HALF = 128


def _lru_in_specs(seq):
    half = LRU_W // LRU_HALF
    vec = pl.BlockSpec((1, LRU_HALF), lambda j, b: (0, j))
    mat = pl.BlockSpec((LRU_HALF, LRU_HALF), lambda j, b: (j, j))
    return [pl.BlockSpec((seq, LRU_HALF), lambda j, b: (b, j)), pl.BlockSpec((seq, LRU_HALF), lambda j, b: (b, half + j)),
            pl.BlockSpec((4, LRU_HALF), lambda j, b: (0, j)), vec, mat, vec, mat, vec, vec]


def _lru_math(x_ref, gate_ref, cw_ref, cb_ref, wa_ref, ba_ref, wx_ref, bx_ref, lam_ref):
    x = x_ref[...]
    gate = gate_ref[...]
    cw =[cw_ref[k:k + 1, :] for k in range(4)]
    xr = _conv_fwd(x, cw) + cb_ref[...]
    r = _sigmoid(_nn(xr, wa_ref[...]) + ba_ref[...])
    i = _sigmoid(_nn(xr, wx_ref[...]) + bx_ref[...])
    sp = _softplus(-lam_ref[...])
    log_a = -LRU_C * r * sp
    a = jnp.exp(log_a)
    mult = jnp.sqrt(_neg_expm1(2.0 * log_a))
    gi = i * xr
    h = _scan_fwd(a, mult * gi)
    gl, tg = _gelu(gate)
    return dict(x=x, gate=gate, cw=cw, xr=xr, r=r, i=i, sp=sp, a=a, mult=mult, gi=gi, h=h, gl=gl, tg=tg)


def lru_fwd(u, cw, cb, wa, ba, wx, bx, lam, *, seq, name):
    t = u.shape[0]

    def body(x_ref, gate_ref, cw_ref, cb_ref, wa_ref, ba_ref, wx_ref, bx_ref, lam_ref, y_ref):
        f = _lru_math(x_ref, gate_ref, cw_ref, cb_ref, wa_ref, ba_ref, wx_ref, bx_ref, lam_ref)
        y_ref[...] = f["gl"] * f["h"]

    return pl.pallas_call(
        body, name=name, grid=(LRU_W // LRU_HALF, t // seq), in_specs=_lru_in_specs(seq),
        out_specs=pl.BlockSpec((seq, LRU_HALF), lambda j, b: (b, j)), out_shape=SDS((t, LRU_W), F32),
        compiler_params=_params(("parallel", "parallel")))(u, u, cw, cb, wa, ba, wx, bx, lam)


def lru_bwd(u, cw, cb, wa, ba, wx, bx, lam, dy, *, seq, name):
    t = u.shape[0]

    def body(x_ref, gate_ref, cw_ref, cb_ref, wa_ref, ba_ref, wx_ref, bx_ref, lam_ref, dy_ref,
             dx_ref, dgate_ref, dcw_ref, dwa_ref, dwx_ref, dv_ref):
        @pl.when(pl.program_id(1) == 0)
        def _():
            dcw_ref[...] = jnp.zeros_like(dcw_ref)
            dwa_ref[...] = jnp.zeros_like(dwa_ref)
            dwx_ref[...] = jnp.zeros_like(dwx_ref)
            dv_ref[...] = jnp.zeros_like(dv_ref)

        f = _lru_math(x_ref, gate_ref, cw_ref, cb_ref, wa_ref, ba_ref, wx_ref, bx_ref, lam_ref)
        dy = dy_ref[...]
        a, h, xr, r, i, mult, gi, sp = f["a"], f["h"], f["xr"], f["r"], f["i"], f["mult"], f["gi"], f["sp"]
        dgate_ref[...] = dy * h * _gelu_grad(f["gate"], f["tg"])
        lamb = _scan_rev(_shift_up(a, 1, 0.0), dy * f["gl"])
        da = lamb * _shift_down(h, 1)
        dlog_a = da * a - (lamb * gi) * (a * a) / mult
        dgi = lamb * mult
        dra = dlog_a * (-LRU_C * sp) * r * (1.0 - r)
        dia = dgi * xr * i * (1.0 - i)
        dsp = jnp.sum(dlog_a * (-LRU_C * r), axis=0, keepdims=True)
        dlam = -dsp * _sigmoid(-lam_ref[...])
        dxr = dgi * i + _nt(dra, wa_ref[...]) + _nt(dia, wx_ref[...])
        dx, dcw = _conv_bwd(dxr, f["x"], f["cw"])
        dx_ref[...] = dx
        dcw_ref[...] += dcw
        dwa_ref[...] += _tn(xr, dra)
        dwx_ref[...] += _tn(xr, dia)
        rows = [jnp.sum(dxr, axis=0, keepdims=True), jnp.sum(dra, axis=0, keepdims=True),
                jnp.sum(dia, axis=0, keepdims=True), dlam]
        r8 = lax.broadcasted_iota(jnp.int32, (8, LRU_HALF), 0)
        acc = jnp.zeros((8, LRU_HALF), F32)
        for k, row in enumerate(rows):
            acc = jnp.where(r8 == k, row, acc)
        dv_ref[...] += acc

    nhalf = LRU_W // LRU_HALF
    col = pl.BlockSpec((seq, LRU_HALF), lambda j, b: (b, j))
    mat = pl.BlockSpec((None, LRU_HALF, LRU_HALF), lambda j, b: (j, 0, 0))
    return pl.pallas_call(
        body, name=name, grid=(nhalf, t // seq), in_specs=_lru_in_specs(seq) + [col],
        out_specs=[col, col, pl.BlockSpec((4, LRU_HALF), lambda j, b: (0, j)), mat, mat,
                   pl.BlockSpec((8, LRU_HALF), lambda j, b: (0, j))],
        out_shape=[SDS((t, LRU_W), F32), SDS((t, LRU_W), F32), SDS((4, LRU_W), F32),
                   SDS((nhalf, LRU_HALF, LRU_HALF), F32), SDS((nhalf, LRU_HALF, LRU_HALF), F32), SDS((8, LRU_W), F32)],
        compiler_params=_params(("arbitrary", "arbitrary")))(u, u, cw, cb, wa, ba, wx, bx, lam, dy)


NEG = -1e30


def _rel_bucket_map():
    dist = (np.arange(BLOCK_Q)[:, None] - np.arange(BLOCK_Q)[None, :]) % BLOCK_Q
    max_exact = REL_BUCKETS // 2
    large = max_exact + (np.log(np.maximum(dist, 1).astype(np.float32) / max_exact)
                         / math.log(BLOCK_Q / max_exact) * (REL_BUCKETS - max_exact)).astype(np.int32)
    large = np.minimum(large, REL_BUCKETS - 1)
    return np.where(dist < max_exact, dist, large).astype(np.int32)


def relbias_fwd(rel_bias, bmap, *, name):
    def body(rb_ref, bm_ref, o_ref):
        bm = bm_ref[...]
        for h in range(ATT_HEADS):
            acc = jnp.zeros((BLOCK_Q, BLOCK_Q), F32)
            for b in range(REL_BUCKETS):
                acc = jnp.where(bm == b, rb_ref[b, h], acc)
            o_ref[h] = acc

    return pl.pallas_call(
        body, name=name, in_specs=[pl.BlockSpec(memory_space=pltpu.SMEM), pl.BlockSpec(memory_space=pltpu.VMEM)],
        out_specs=pl.BlockSpec(memory_space=pltpu.VMEM), out_shape=SDS((ATT_HEADS, BLOCK_Q, BLOCK_Q), F32))(rel_bias, bmap)


def relbias_bwd(dbias, bmap, *, name):
    def body(db_ref, bm_ref, o_ref):
        bm = bm_ref[...]
        row = lax.broadcasted_iota(jnp.int32, (REL_BUCKETS, 128), 0)
        col = lax.broadcasted_iota(jnp.int32, (REL_BUCKETS, 128), 1)
        acc = jnp.zeros((REL_BUCKETS, 128), F32)
        for h in range(ATT_HEADS):
            d = db_ref[h]
            for b in range(REL_BUCKETS):
                s = jnp.sum(jnp.sum(jnp.where(bm == b, d, 0.0), axis=1, keepdims=True), axis=0, keepdims=True)
                acc = jnp.where((row == b) & (col == h), s, acc)
        o_ref[...] = acc

    return pl.pallas_call(body, name=name, out_shape=SDS((REL_BUCKETS, 128), F32))(dbias, bmap)


def _attn_probs(q_ref, k_ref, v_ref, b_ref, s_ref, n):
    rows = ATT_GROUP * BLOCK_Q
    qs = q_ref[...].reshape(rows, HEAD) * (HEAD ** -0.5)
    prev = pl.multiple_of(jnp.maximum(n - 1, 0) * BLOCK_Q, BLOCK_Q)
    cur = pl.multiple_of(n * BLOCK_Q, BLOCK_Q)
    kp, kc = k_ref[pl.ds(prev, BLOCK_Q), :], k_ref[pl.ds(cur, BLOCK_Q), :]
    vp, vc = v_ref[pl.ds(prev, BLOCK_Q), :], v_ref[pl.ds(cur, BLOCK_Q), :]
    bias = b_ref[...].reshape(rows, BLOCK_Q)
    i = lax.broadcasted_iota(jnp.int32, (rows, BLOCK_Q), 0) & (BLOCK_Q - 1)
    j = lax.broadcasted_iota(jnp.int32, (rows, BLOCK_Q), 1)
    s_p = jnp.where((j > i) & (n > 0), _nt(qs, kp) + bias, NEG)
    s_c = jnp.where(j <= i, _nt(qs, kc) + bias, NEG)
    sink = s_ref[...]
    m = jnp.maximum(jnp.maximum(jnp.max(s_p, axis=-1, keepdims=True), jnp.max(s_c, axis=-1, keepdims=True)), sink)
    e_p, e_c, e_s = jnp.exp(s_p - m), jnp.exp(s_c - m), jnp.exp(sink - m)
    inv = 1.0 / (jnp.sum(e_p, axis=-1, keepdims=True) + jnp.sum(e_c, axis=-1, keepdims=True) + e_s)
    return e_p * inv, e_c * inv, e_s * inv, qs, kp, kc, vp, vc, prev, cur


def _attn_specs(seq):
    qspec = pl.BlockSpec((None, ATT_GROUP, BLOCK_Q, HEAD), lambda g, b, n: (b, g, n, 0))
    kvspec = pl.BlockSpec((None, None, seq, HEAD), lambda g, b, n: (b, g, 0, 0))
    bspec = pl.BlockSpec((ATT_GROUP, BLOCK_Q, BLOCK_Q), lambda g, b, n: (g, 0, 0))
    sspec = pl.BlockSpec((ATT_GROUP * BLOCK_Q, 1), lambda g, b, n: (g, 0))
    return qspec, kvspec, bspec, sspec


def attn_fwd(q, k, v, bias, sink_rows, *, name):
    nb, _, seq, _ = q.shape

    def body(q_ref, k_ref, v_ref, b_ref, s_ref, o_ref):
        p_p, p_c, _, _, _, _, vp, vc, _, _ = _attn_probs(q_ref, k_ref, v_ref, b_ref, s_ref, pl.program_id(2))
        o_ref[...] = (_nn(p_p, vp) + _nn(p_c, vc)).reshape(ATT_GROUP, BLOCK_Q, HEAD)

    qspec, kvspec, bspec, sspec = _attn_specs(seq)
    return pl.pallas_call(
        body, name=name, grid=(KV_HEADS, nb, seq // BLOCK_Q), in_specs=[qspec, kvspec, kvspec, bspec, sspec],
        out_specs=qspec, out_shape=SDS(q.shape, F32),
        compiler_params=_params(("parallel", "parallel", "arbitrary")))(q, k, v, bias, sink_rows)


def attn_bwd(q, k, v, bias, sink_rows, do, *, name):
    nb, _, seq, _ = q.shape

    def body(q_ref, k_ref, v_ref, b_ref, s_ref, do_ref, dq_ref, dk_ref, dv_ref, db_ref, ds_ref):
        b, n = pl.program_id(1), pl.program_id(2)

        @pl.when((b == 0) & (n == 0))
        def _():
            db_ref[...] = jnp.zeros_like(db_ref)
            ds_ref[...] = jnp.zeros_like(ds_ref)

        @pl.when(n == 0)
        def _():
            dk_ref[...] = jnp.zeros_like(dk_ref)
            dv_ref[...] = jnp.zeros_like(dv_ref)

        p_p, p_c, p_s, qs, kp, kc, vp, vc, prev, cur = _attn_probs(q_ref, k_ref, v_ref, b_ref, s_ref, n)
        do = do_ref[...].reshape(ATT_GROUP * BLOCK_Q, HEAD)
        dp_p, dp_c = _nt(do, vp), _nt(do, vc)
        delta = jnp.sum(p_p * dp_p, axis=-1, keepdims=True) + jnp.sum(p_c * dp_c, axis=-1, keepdims=True)
        ds_p, ds_c = p_p * (dp_p - delta), p_c * (dp_c - delta)
        dq_ref[...] = ((_nn(ds_p, kp) + _nn(ds_c, kc)) * (HEAD ** -0.5)).reshape(ATT_GROUP, BLOCK_Q, HEAD)
        dk_ref[pl.ds(prev, BLOCK_Q), :] += _tn(ds_p, qs)
        dk_ref[pl.ds(cur, BLOCK_Q), :] += _tn(ds_c, qs)
        dv_ref[pl.ds(prev, BLOCK_Q), :] += _tn(p_p, do)
        dv_ref[pl.ds(cur, BLOCK_Q), :] += _tn(p_c, do)
        db_ref[...] += (ds_p + ds_c).reshape(ATT_GROUP, BLOCK_Q, BLOCK_Q)
        ds_ref[...] += -p_s * delta

    qspec, kvspec, bspec, sspec = _attn_specs(seq)
    return pl.pallas_call(
        body, name=name, grid=(KV_HEADS, nb, seq // BLOCK_Q), in_specs=[qspec, kvspec, kvspec, bspec, sspec, qspec],
        out_specs=[qspec, kvspec, kvspec, bspec, sspec],
        out_shape=[SDS(q.shape, F32), SDS(k.shape, F32), SDS(v.shape, F32),
                   SDS((ATT_HEADS, BLOCK_Q, BLOCK_Q), F32), SDS((ATT_HEADS * BLOCK_Q, 1), F32)],
        compiler_params=_params(("arbitrary", "arbitrary", "arbitrary")))(q, k, v, bias, sink_rows, do)


def _iota2(shape, axis):
    return lax.broadcasted_iota(jnp.int32, shape, axis)


def _col_to_row(col):
    c = col.shape[0]
    eye = _iota2((c, c), 0) == _iota2((c, c), 1)
    return jnp.sum(jnp.where(eye, jnp.broadcast_to(col, (c, c)), 0.0), axis=0, keepdims=True)


def _row_to_col(row):
    c = row.shape[1]
    eye = _iota2((c, c), 0) == _iota2((c, c), 1)
    return jnp.sum(jnp.where(eye, jnp.broadcast_to(row, (c, c)), 0.0), axis=1, keepdims=True)


def _last_row(col):
    c = col.shape[0]
    return jnp.sum(jnp.where(_iota2((c, 1), 0) == c - 1, col, 0.0), axis=0, keepdims=True)


def _chunk_cumsum(x):
    pos = _iota2(x.shape, 0) & (DN_CHUNK - 1)
    d = 1
    while d < DN_CHUNK:
        x = x + jnp.where(pos >= d, pltpu.roll(x, d, 0), 0.0)
        d *= 2
    return x


def _chunk_rev_cumsum(x):
    n = x.shape[0]
    pos = _iota2(x.shape, 0) & (DN_CHUNK - 1)
    d = 1
    while d < DN_CHUNK:
        x = x + jnp.where(pos < DN_CHUNK - d, pltpu.roll(x, n - d, 0), 0.0)
        d *= 2
    return x


def _tri_inv(low):
    c = low.shape[0]
    eye = (_iota2((c, c), 0) == _iota2((c, c), 1)).astype(F32)
    m = -low
    p = eye + m
    steps = int(math.log2(c)) - 1
    for _ in range(steps):
        m = _nn(m, m, hi=True)
        p = p + _nn(p, m, hi=True)
    return p


_DN_SCALE = (HEAD ** -0.5, 1.0, None)


def _dn_act(c, scale):
    sig = _sigmoid(c)
    a = c * sig
    if scale is None:
        return a, sig, None, None
    r = lax.rsqrt(jnp.sum(a * a, axis=-1, keepdims=True) + EPS)
    return a * r * scale, sig, a * r, r


def _dn_gates(ba_ref, hs_ref):
    beta = _sigmoid(ba_ref[0])
    sp_arg = ba_ref[1] + hs_ref[1]
    a_exp = jnp.exp(hs_ref[0])
    g = -a_exp * _softplus(sp_arg)
    return beta, g, sp_arg, a_exp


def _dn_inputs(pre_ref, cw_ref, ba_ref, hs_ref, act_sc, b_sc, gc_sc, c_sc=None):
    for idx in range(3):
        c = _conv_fwd(pre_ref[idx], [cw_ref[idx, k:k + 1, :] for k in range(4)])
        if c_sc is not None:
            c_sc[idx] = c
        act_sc[idx] = _dn_act(c, _DN_SCALE[idx])[0]
    beta, g, _, _ = _dn_gates(ba_ref, hs_ref)
    b_sc[...] = beta
    gc_sc[...] = _chunk_cumsum(g)


def _dn_chunk_math(q, k, v, b, gcc):
    c = q.shape[0]
    tril = _iota2((c, c), 0) >= _iota2((c, c), 1)
    strict = _iota2((c, c), 0) > _iota2((c, c), 1)
    eg = jnp.exp(gcc)
    kb, vb = k * b, v * b
    kbg = kb * eg
    dm = jnp.exp(jnp.where(tril, jnp.broadcast_to(gcc, (c, c)) - _col_to_row(gcc), NEG))
    kk = _nt(kb, k)
    t = _tri_inv(jnp.where(strict, kk * dm, 0.0))
    glast = _last_row(gcc)
    ekd = jnp.exp(glast - gcc)
    qk = _nt(q, k)
    return dict(tril=tril, strict=strict, eg=eg, kb=kb, vb=vb, kbg=kbg, dm=dm, kk=kk, t=t, glast=glast, ekd=ekd,
                kd=k * ekd, qk=qk, amat=jnp.where(tril, qk * dm, 0.0), qg=q * eg,
                egl=jnp.broadcast_to(jnp.exp(glast), (c, 1)))


DN_UNROLL = 4


def _chunk_loop(nc, chunk):
    u = math.gcd(nc, DN_UNROLL)

    def step(i, carry):
        for j in range(u):
            chunk(i * u + j)
        return carry

    lax.fori_loop(0, nc // u, step, 0)


def _dn_specs(seq):
    s64 = lambda lead: pl.BlockSpec((lead, None, None, seq, HEAD), lambda b, h: (0, b, h, 0, 0))
    s1 = lambda lead: pl.BlockSpec((lead, None, None, seq, 1), lambda b, h: (0, b, h, 0, 0))
    one64 = pl.BlockSpec((None, None, seq, HEAD), lambda b, h: (b, h, 0, 0))
    one1 = pl.BlockSpec((None, None, seq, 1), lambda b, h: (b, h, 0, 0))
    cw = pl.BlockSpec((None, 3, 4, HEAD), lambda b, h: (h, 0, 0, 0))
    hs = pl.BlockSpec((None, 2, 1, 1), lambda b, h: (h, 0, 0, 0))
    return s64, s1, one64, one1, cw, hs


def dn_prep(pre, cw, ba, hs, *, name):
    _, nb, nh, seq, _ = pre.shape
    nc = seq // DN_CHUNK

    def body(pre_ref, cw_ref, ba_ref, hs_ref, loc_ref, egl_ref, act_sc, b_sc, gc_sc):
        _dn_inputs(pre_ref, cw_ref, ba_ref, hs_ref, act_sc, b_sc, gc_sc)

        def chunk(c):
            rows = pl.ds(pl.multiple_of(c * DN_CHUNK, DN_CHUNK), DN_CHUNK)
            m = _dn_chunk_math(act_sc[0, rows, :], act_sc[1, rows, :], act_sc[2, rows, :], b_sc[rows, :], gc_sc[rows, :])
            loc_ref[0, rows, :] = m["qg"]
            loc_ref[1, rows, :] = m["kd"]
            loc_ref[2, rows, :] = _nn(m["t"], m["vb"])
            loc_ref[3, rows, :] = _nn(m["t"], m["kbg"])
            loc_ref[4, rows, :] = m["amat"]
            egl_ref[rows, :] = m["egl"]

        _chunk_loop(nc, chunk)

    s64, s1, one64, one1, cwspec, hsspec = _dn_specs(seq)
    return pl.pallas_call(
        body, name=name, grid=(nb, nh), in_specs=[s64(3), cwspec, s1(2), hsspec], out_specs=[s64(5), one1],
        out_shape=[SDS((5, nb, nh, seq, HEAD), F32), SDS((nb, nh, seq, 1), F32)],
        scratch_shapes=[pltpu.VMEM((3, seq, HEAD), F32)] + [pltpu.VMEM((seq, 1), F32)] * 2,
        compiler_params=_params(("parallel", "parallel")))(pre, cw, ba, hs)


def _gated_norm(o, z, gn):
    r = lax.rsqrt(jnp.mean(o * o, axis=-1, keepdims=True) + EPS)
    sig = _sigmoid(z)
    return o * r, sig, r


def dn_scan(loc, egl, z, gn, *, name):
    _, nb, nh, seq, _ = loc.shape
    nc = seq // DN_CHUNK

    def body(loc_ref, egl_ref, z_ref, gn_ref, y_ref, o_ref, vn_ref, st_ref):
        gn = gn_ref[...]

        def step(c, state):
            rows = pl.ds(pl.multiple_of(c * DN_CHUNK, DN_CHUNK), DN_CHUNK)
            st_ref[rows, :] = state
            vn = loc_ref[2, rows, :] - _nn(loc_ref[3, rows, :], state)
            o = _nn(loc_ref[0, rows, :], state) + _nn(loc_ref[4, rows, :], vn)
            vn_ref[rows, :] = vn
            o_ref[rows, :] = o
            zz = z_ref[rows, :]
            on, sig, _ = _gated_norm(o, zz, gn)
            y_ref[rows, :] = on * gn * (zz * sig)
            return state * egl_ref[rows, :] + _tn(loc_ref[1, rows, :], vn)

        lax.fori_loop(0, nc, step, jnp.zeros((HEAD, HEAD), F32))

    s64, s1, one64, one1, cwspec, hsspec = _dn_specs(seq)
    out = SDS((nb, nh, seq, HEAD), F32)
    return pl.pallas_call(
        body, name=name, grid=(nb, nh), in_specs=[s64(5), one1, one64, _whole((1, HEAD))],
        out_specs=[one64] * 4, out_shape=[out] * 4,
        compiler_params=_params(("parallel", "parallel")))(loc, egl, z, gn)


def dn_scan_bwd(loc, egl, z, gn, o, vn, states, dy, *, name):
    _, nb, nh, seq, _ = loc.shape
    nc = seq // DN_CHUNK

    def body(loc_ref, egl_ref, z_ref, gn_ref, o_ref, vn_ref, st_ref, dy_ref, dloc_ref, degl_ref, dz_ref, dgn_ref):
        @pl.when((pl.program_id(0) == 0) & (pl.program_id(1) == 0))
        def _():
            dgn_ref[...] = jnp.zeros_like(dgn_ref)

        gn = gn_ref[...]
        tril = _iota2((DN_CHUNK, DN_CHUNK), 0) >= _iota2((DN_CHUNK, DN_CHUNK), 1)

        def step(i, carry):
            ds, dgn = carry
            rows = pl.ds(pl.multiple_of((nc - 1 - i) * DN_CHUNK, DN_CHUNK), DN_CHUNK)
            dy, zz, oo = dy_ref[rows, :], z_ref[rows, :], o_ref[rows, :]
            on, sig, r = _gated_norm(oo, zz, gn)
            sz = zz * sig
            dz_ref[rows, :] = dy * on * gn * (sig * (1.0 + zz * (1.0 - sig)))
            dgn = dgn + jnp.sum(dy * on * sz, axis=0, keepdims=True)
            don = dy * gn * sz
            do = r * (don - on * jnp.mean(don * on, axis=-1, keepdims=True))
            state, vnew = st_ref[rows, :], vn_ref[rows, :]
            qg, kd, w, amat = loc_ref[0, rows, :], loc_ref[1, rows, :], loc_ref[3, rows, :], loc_ref[4, rows, :]
            dvn = _tn(amat, do) + _nn(kd, ds)
            dloc_ref[0, rows, :] = _nt(do, state)
            dloc_ref[1, rows, :] = _nt(vnew, ds)
            dloc_ref[2, rows, :] = dvn
            dloc_ref[3, rows, :] = -_nt(dvn, state)
            dloc_ref[4, rows, :] = jnp.where(tril, _nt(do, vnew), 0.0)
            degl = jnp.sum(jnp.sum(state * ds, axis=1, keepdims=True), axis=0, keepdims=True)
            degl_ref[rows, :] = jnp.broadcast_to(degl, (DN_CHUNK, 1))
            return ds * egl_ref[rows, :] + _tn(qg, do) - _tn(w, dvn), dgn

        _, dgn = lax.fori_loop(0, nc, step, (jnp.zeros((HEAD, HEAD), F32), jnp.zeros((1, HEAD), F32)))
        dgn_ref[...] += dgn

    s64, s1, one64, one1, cwspec, hsspec = _dn_specs(seq)
    return pl.pallas_call(
        body, name=name, grid=(nb, nh),
        in_specs=[s64(5), one1, one64, _whole((1, HEAD)), one64, one64, one64, one64],
        out_specs=[s64(5), one1, one64, _whole((1, HEAD))],
        out_shape=[SDS((5, nb, nh, seq, HEAD), F32), SDS((nb, nh, seq, 1), F32), SDS((nb, nh, seq, HEAD), F32),
                   SDS((1, HEAD), F32)],
        compiler_params=_params(("arbitrary", "arbitrary")))(loc, egl, z, gn, o, vn, states, dy)


def dn_prep_bwd(pre, cw, ba, hs, dloc, degl, *, name):
    _, nb, nh, seq, _ = pre.shape
    nc = seq // DN_CHUNK

    def body(pre_ref, cw_ref, ba_ref, hs_ref, dloc_ref, degl_ref, dpre_ref, dba_ref, dcw_ref, dhs_ref,
             act_sc, b_sc, gc_sc, c_sc):
        @pl.when(pl.program_id(1) == 0)
        def _():
            dcw_ref[...] = jnp.zeros_like(dcw_ref)
            dhs_ref[...] = jnp.zeros_like(dhs_ref)

        _dn_inputs(pre_ref, cw_ref, ba_ref, hs_ref, act_sc, b_sc, gc_sc, c_sc)

        def chunk(c):
            rows = pl.ds(pl.multiple_of(c * DN_CHUNK, DN_CHUNK), DN_CHUNK)
            q, k, v, b, gcc = act_sc[0, rows, :], act_sc[1, rows, :], act_sc[2, rows, :], b_sc[rows, :], gc_sc[rows, :]
            m = _dn_chunk_math(q, k, v, b, gcc)
            dqg, dkd, du, dw, da = (dloc_ref[x, rows, :] for x in range(5))
            t, dm, eg = m["t"], m["dm"], m["eg"]
            dt = _nt(du, m["vb"]) + _nt(dw, m["kbg"])
            dvb, dkbg = _tn(t, du), _tn(t, dw)
            dl = jnp.where(m["strict"], -_tn(t, _nt(dt, t, hi=True), hi=True), 0.0)
            dkk = dl * dm
            dqk = da * dm
            dd = dl * m["kk"] + da * m["qk"]
            dkb = _nn(dkk, k) + dkbg * eg
            dq = _nn(dqk, k) + dqg * eg
            dk = _tn(dkk, m["kb"]) + _tn(dqk, q) + dkd * m["ekd"] + dkb * b
            db = jnp.sum(dkb * k, axis=-1, keepdims=True) + jnp.sum(dvb * v, axis=-1, keepdims=True)
            mx = jnp.where(m["tril"], dd * dm, 0.0)
            tk = jnp.sum(dkd * m["kd"], axis=-1, keepdims=True)
            dgc = (jnp.sum(mx, axis=-1, keepdims=True) - _row_to_col(jnp.sum(mx, axis=0, keepdims=True))
                   + jnp.sum(dqg * m["qg"], axis=-1, keepdims=True) + jnp.sum(dkbg * m["kbg"], axis=-1, keepdims=True) - tk)
            dglast = jnp.sum(tk, axis=0, keepdims=True) + _last_row(degl_ref[rows, :]) * jnp.exp(m["glast"])
            act_sc[0, rows, :] = dq
            act_sc[1, rows, :] = dk
            act_sc[2, rows, :] = dvb * b
            b_sc[rows, :] = db
            gc_sc[rows, :] = dgc + jnp.where(_iota2((DN_CHUNK, 1), 0) == DN_CHUNK - 1, dglast, 0.0)

        _chunk_loop(nc, chunk)

        beta, g, sp_arg, a_exp = _dn_gates(ba_ref, hs_ref)
        dg = _chunk_rev_cumsum(gc_sc[...])
        dal = dg * (-a_exp) * _sigmoid(sp_arg)
        dba_ref[0] = b_sc[...] * beta * (1.0 - beta)
        dba_ref[1] = dal
        dhs_ref[0] += jnp.sum(dg * g, axis=0, keepdims=True)
        dhs_ref[1] += jnp.sum(dal, axis=0, keepdims=True)
        for idx in range(3):
            c = c_sc[idx]
            _, sig, hat, r = _dn_act(c, _DN_SCALE[idx])
            da_ = act_sc[idx]
            if _DN_SCALE[idx] is not None:
                da_ = da_ * _DN_SCALE[idx]
                da_ = r * (da_ - hat * jnp.sum(da_ * hat, axis=-1, keepdims=True))
            dx, dcw = _conv_bwd(da_ * (sig * (1.0 + c * (1.0 - sig))), pre_ref[idx],
                                [cw_ref[idx, k:k + 1, :] for k in range(4)])
            dpre_ref[idx] = dx
            dcw_ref[idx] += dcw

    s64, s1, one64, one1, cwspec, hsspec = _dn_specs(seq)
    swap = lambda spec: pl.BlockSpec(spec.block_shape, lambda h, b, _f=spec.index_map: _f(b, h))
    return pl.pallas_call(
        body, name=name, grid=(nh, nb),
        in_specs=[swap(s64(3)), swap(cwspec), swap(s1(2)), swap(hsspec), swap(s64(5)), swap(one1)],
        out_specs=[swap(s64(3)), swap(s1(2)), swap(cwspec), swap(hsspec)],
        out_shape=[SDS((3, nb, nh, seq, HEAD), F32), SDS((2, nb, nh, seq, 1), F32), SDS((nh, 3, 4, HEAD), F32),
                   SDS((nh, 2, 1, 1), F32)],
        scratch_shapes=[pltpu.VMEM((3, seq, HEAD), F32)] + [pltpu.VMEM((seq, 1), F32)] * 2 + [pltpu.VMEM((3, seq, HEAD), F32)],
        compiler_params=_params(("arbitrary", "arbitrary")))(pre, cw, ba, hs, dloc, degl)


COL_Q, COL_K, COL_V = 512 // 128, 1024 // 128, 1152 // 128
COL_DNQ, COL_DNK, COL_DNV, COL_DNZ, COL_BA = 1280 // 128, 1536 // 128, 1792 // 128, 2048 // 128, 2304 // 128


def _lane_a(shape):
    return _iota2(shape, 1) < HEAD


def _bd(x):
    la = _lane_a(x.shape)
    return jnp.concatenate([jnp.where(la, x, 0.0), jnp.where(la, 0.0, x)], axis=0)


def _fold(m):
    return m[:HEAD] + m[HEAD:]


def _bd_mask():
    return (_iota2((2 * HEAD, 2 * HEAD), 0) < HEAD) == (_iota2((2 * HEAD, 2 * HEAD), 1) < HEAD)


def _pk_nn(x, y, hi=False):
    return _nn(x, _bd(y), hi)


def _pk_nt(u, v, hi=False):
    return _nt(u, _bd(v), hi)


def _pk_tn(x, y, hi=False):
    return _fold(jnp.where(_bd_mask(), _tn(x, y, hi), 0.0))


def _half_sum(x):
    la = _lane_a(x.shape)
    return jnp.where(la, jnp.sum(jnp.where(la, x, 0.0), axis=-1, keepdims=True),
                     jnp.sum(jnp.where(la, 0.0, x), axis=-1, keepdims=True))


def _lane_col(x, idx):
    return jnp.sum(jnp.where(_iota2(x.shape, 1) == idx, x, 0.0), axis=-1, keepdims=True)


def _row0(x):
    return jnp.max(x, axis=0, keepdims=True)


def _dup_kv(x, g):
    la = _lane_a(x.shape)
    rolled = pltpu.roll(x, HEAD, 1)
    return jnp.where(la, x, rolled) if g == 0 else jnp.where(la, rolled, x)


def _stack_heads(ref, g):
    la = _lane_a((BLOCK_Q, 2 * HEAD))
    parts = []
    for hh in range(ATT_GROUP):
        pair = ref[:, pl.ds(2 * HEAD * (2 * g + hh // 2), 2 * HEAD)]
        parts.append(jnp.where(la if hh % 2 == 0 else ~la, pair, 0.0))
    return jnp.concatenate(parts, axis=0)


def _unstack_heads(stack, ref, g):
    la = _lane_a((BLOCK_Q, 2 * HEAD))
    for j in range(2):
        top = stack[2 * j * BLOCK_Q:(2 * j + 1) * BLOCK_Q]
        bot = stack[(2 * j + 1) * BLOCK_Q:(2 * j + 2) * BLOCK_Q]
        ref[:, pl.ds(2 * HEAD * (2 * g + j), 2 * HEAD)] = jnp.where(la, top, bot)


def _swa_probs(q_ref, k_ref, v_ref, b_ref, s_ref, n, g):
    rows = ATT_GROUP * BLOCK_Q
    prev = pl.multiple_of(jnp.maximum(n - 1, 0) * BLOCK_Q, BLOCK_Q)
    cur = pl.multiple_of(n * BLOCK_Q, BLOCK_Q)
    kp, kc = _dup_kv(k_ref[pl.ds(prev, BLOCK_Q), :], g), _dup_kv(k_ref[pl.ds(cur, BLOCK_Q), :], g)
    vp, vc = _dup_kv(v_ref[pl.ds(prev, BLOCK_Q), :], g), _dup_kv(v_ref[pl.ds(cur, BLOCK_Q), :], g)
    qs = _stack_heads(q_ref, g) * (HEAD ** -0.5)
    bias = b_ref[pl.ds(ATT_GROUP * g, ATT_GROUP)].reshape(rows, BLOCK_Q)
    i = _iota2((rows, BLOCK_Q), 0) & (BLOCK_Q - 1)
    j = _iota2((rows, BLOCK_Q), 1)
    s_p = jnp.where((j > i) & (n > 0), _nt(qs, kp) + bias, NEG)
    s_c = jnp.where(j <= i, _nt(qs, kc) + bias, NEG)
    sink = s_ref[pl.ds(rows * g, rows), :]
    m = jnp.maximum(jnp.maximum(jnp.max(s_p, axis=-1, keepdims=True), jnp.max(s_c, axis=-1, keepdims=True)), sink)
    e_p, e_c, e_s = jnp.exp(s_p - m), jnp.exp(s_c - m), jnp.exp(sink - m)
    inv = 1.0 / (jnp.sum(e_p, axis=-1, keepdims=True) + jnp.sum(e_c, axis=-1, keepdims=True) + e_s)
    return e_p * inv, e_c * inv, e_s * inv, qs, kp, kc, vp, vc, prev, cur


def _swa_specs(seq):
    nblk = seq // BLOCK_Q
    qspec = pl.BlockSpec((BLOCK_Q, ATT_W), lambda b, n: (b * nblk + n, COL_Q * 128 // ATT_W))
    kspec = pl.BlockSpec((seq, 2 * HEAD), lambda b, n: (b, COL_K))
    vspec = pl.BlockSpec((seq, 2 * HEAD), lambda b, n: (b, COL_V))
    ospec = pl.BlockSpec((BLOCK_Q, ATT_W), lambda b, n: (b * nblk + n, 0))
    kvout = pl.BlockSpec((seq, 2 * HEAD), lambda b, n: (b, 0))
    return qspec, kspec, vspec, ospec, kvout, _whole((ATT_HEADS, BLOCK_Q, BLOCK_Q)), _whole((ATT_HEADS * BLOCK_Q, 1))


def swa_fwd(u, bias, sink_rows, *, seq, name):
    t = u.shape[0]

    def body(q_ref, k_ref, v_ref, b_ref, s_ref, o_ref):
        for g in range(KV_HEADS):
            p_p, p_c, _, _, _, _, vp, vc, _, _ = _swa_probs(q_ref, k_ref, v_ref, b_ref, s_ref, pl.program_id(1), g)
            _unstack_heads(_nn(p_p, vp) + _nn(p_c, vc), o_ref, g)

    qspec, kspec, vspec, ospec, kvout, bspec, sspec = _swa_specs(seq)
    return pl.pallas_call(
        body, name=name, grid=(t // seq, seq // BLOCK_Q), in_specs=[qspec, kspec, vspec, bspec, sspec], out_specs=ospec,
        out_shape=SDS((t, ATT_W), F32), compiler_params=_params(("parallel", "arbitrary")))(u, u, u, bias, sink_rows)


def swa_bwd(u, bias, sink_rows, do, *, seq, name):
    t = u.shape[0]

    def body(q_ref, k_ref, v_ref, b_ref, s_ref, do_ref, dq_ref, dk_ref, dv_ref, db_ref, ds_ref):
        b, n = pl.program_id(0), pl.program_id(1)

        @pl.when((b == 0) & (n == 0))
        def _():
            db_ref[...] = jnp.zeros_like(db_ref)
            ds_ref[...] = jnp.zeros_like(ds_ref)

        @pl.when(n == 0)
        def _():
            dk_ref[...] = jnp.zeros_like(dk_ref)
            dv_ref[...] = jnp.zeros_like(dv_ref)

        la = _lane_a((BLOCK_Q, 2 * HEAD))
        for g in range(KV_HEADS):
            p_p, p_c, p_s, qs, kp, kc, vp, vc, prev, cur = _swa_probs(q_ref, k_ref, v_ref, b_ref, s_ref, n, g)
            do = _stack_heads(do_ref, g)
            dp_p, dp_c = _nt(do, vp), _nt(do, vc)
            delta = jnp.sum(p_p * dp_p, axis=-1, keepdims=True) + jnp.sum(p_c * dp_c, axis=-1, keepdims=True)
            ds_p, ds_c = p_p * (dp_p - delta), p_c * (dp_c - delta)
            _unstack_heads((_nn(ds_p, kp) + _nn(ds_c, kc)) * (HEAD ** -0.5), dq_ref, g)
            mine = la if g == 0 else ~la

            def to_head(x):
                return jnp.where(mine, x + pltpu.roll(x, HEAD, 1), 0.0)

            dk_ref[pl.ds(prev, BLOCK_Q), :] += to_head(_tn(ds_p, qs))
            dk_ref[pl.ds(cur, BLOCK_Q), :] += to_head(_tn(ds_c, qs))
            dv_ref[pl.ds(prev, BLOCK_Q), :] += to_head(_tn(p_p, do))
            dv_ref[pl.ds(cur, BLOCK_Q), :] += to_head(_tn(p_c, do))
            db_ref[pl.ds(ATT_GROUP * g, ATT_GROUP)] += (ds_p + ds_c).reshape(ATT_GROUP, BLOCK_Q, BLOCK_Q)
            rows = ATT_GROUP * BLOCK_Q
            ds_ref[pl.ds(rows * g, rows), :] += -p_s * delta

    qspec, kspec, vspec, ospec, kvout, bspec, sspec = _swa_specs(seq)
    return pl.pallas_call(
        body, name=name, grid=(t // seq, seq // BLOCK_Q), in_specs=[qspec, kspec, vspec, bspec, sspec, ospec],
        out_specs=[ospec, kvout, kvout, bspec, sspec],
        out_shape=[SDS((t, ATT_W), F32), SDS((t, 2 * HEAD), F32), SDS((t, 2 * HEAD), F32),
                   SDS((ATT_HEADS, BLOCK_Q, BLOCK_Q), F32), SDS((ATT_HEADS * BLOCK_Q, 1), F32)],
        compiler_params=_params(("arbitrary", "arbitrary")))(u, u, u, bias, sink_rows, do)


def _gdn_gates(ba_ref, alog_ref, dt_ref, hp):
    blk = ba_ref[...]
    beta_blk = _sigmoid(blk)
    sp_arg = blk + dt_ref[...]
    a_exp = jnp.exp(alog_ref[...])
    g_blk = -a_exp * _softplus(sp_arg)
    la = _lane_a(blk.shape)
    ha = 2 * hp
    beta = jnp.where(la, _lane_col(beta_blk, ha), _lane_col(beta_blk, ha + 1))
    g = jnp.where(la, _lane_col(g_blk, DN_HEADS + ha), _lane_col(g_blk, DN_HEADS + ha + 1))
    return beta, g, beta_blk, sp_arg, a_exp, g_blk


def _gdn_act(c, scale):
    sig = _sigmoid(c)
    a = c * sig
    if scale is None:
        return a, sig, None, None
    r = lax.rsqrt(_half_sum(a * a) + EPS)
    return a * r * scale, sig, a * r, r


def _gdn_inputs(pre_refs, cw_refs, ba_ref, alog_ref, dt_ref, hp, act_sc, b_sc, gc_sc, c_sc=None):
    for idx in range(3):
        c = _conv_fwd(pre_refs[idx][...], [cw_refs[idx][k:k + 1, :] for k in range(4)])
        if c_sc is not None:
            c_sc[idx] = c
        act_sc[idx] = _gdn_act(c, _DN_SCALE[idx])[0]
    beta, g = _gdn_gates(ba_ref, alog_ref, dt_ref, hp)[:2]
    b_sc[...] = beta
    gc_sc[...] = _chunk_cumsum(g)


def _gdn_chunk(q, k, v, b, gcc):
    shape = q.shape
    row, lm = _iota2(shape, 0), _iota2(shape, 1) & (HEAD - 1)
    tril, strict, eye = row >= lm, row > lm, row == lm
    eg = jnp.exp(gcc)
    kb, vb = k * b, v * b
    kbg = kb * eg
    grow = jnp.sum(jnp.where(eye, gcc, 0.0), axis=0, keepdims=True)
    dm = jnp.exp(jnp.where(tril, gcc - grow, NEG))
    kk = _pk_nt(kb, k)
    glast = jnp.sum(jnp.where(row == DN_CHUNK - 1, gcc, 0.0), axis=0, keepdims=True)
    ekd = jnp.exp(glast - gcc)
    qk = _pk_nt(q, k)
    return dict(q=q, k=k, v=v, b=b, tril=tril, strict=strict, eye=eye, row=row, eg=eg, kb=kb, vb=vb, kbg=kbg, dm=dm, kk=kk,
                low=jnp.where(strict, kk * dm, 0.0), glast=glast, ekd=ekd, kd=k * ekd, qk=qk,
                amat=jnp.where(tril, qk * dm, 0.0), qg=q * eg, egl=jnp.broadcast_to(jnp.exp(glast), shape))


def _tri_inv_many(chunks):
    ms = [-m["low"] for m in chunks]
    ts = [m["eye"].astype(F32) + x for m, x in zip(chunks, ms)]
    for _ in range(int(math.log2(HEAD)) - 1):
        ms = [_pk_nn(x, x, hi=True) for x in ms]
        ts = [t + _pk_nn(t, x, hi=True) for t, x in zip(ts, ms)]
    return ts


def _gdn_chunk_loop(nc, act_sc, b_sc, gc_sc, finish):
    u = math.gcd(nc, DN_UNROLL)

    def step(i, carry):
        rows = [pl.ds(pl.multiple_of((i * u + j) * DN_CHUNK, DN_CHUNK), DN_CHUNK) for j in range(u)]
        chunks = [_gdn_chunk(act_sc[0, r, :], act_sc[1, r, :], act_sc[2, r, :], b_sc[r, :], gc_sc[r, :]) for r in rows]
        pending = [finish(r, m, t) for r, m, t in zip(rows, chunks, _tri_inv_many(chunks))]
        pending = [g for g in pending if g is not None]
        while pending:
            for g in list(pending):
                if next(g, StopIteration) is StopIteration:
                    pending.remove(g)
        return carry

    lax.fori_loop(0, nc // u, step, 0)


def _gdn_in_specs(seq):
    u_at = lambda col: pl.BlockSpec((seq, 2 * HEAD), lambda b, hp, _c=col: (b, _c + hp))
    cw_at = lambda col: pl.BlockSpec((4, 2 * HEAD), lambda b, hp, _c=col: (0, _c + hp))
    row = pl.BlockSpec((1, 2 * HEAD), lambda b, hp: (0, 0))
    ba = pl.BlockSpec((seq, 2 * HEAD), lambda b, hp: (b, COL_BA))
    return [u_at(COL_DNQ), u_at(COL_DNK), u_at(COL_DNV), ba, cw_at(0), cw_at(2), cw_at(4), row, row]


def _pair(seq, lead=None):
    if lead is None:
        return pl.BlockSpec((seq, 2 * HEAD), lambda b, hp: (b, hp))
    return pl.BlockSpec((lead, seq, 2 * HEAD), lambda b, hp: (0, b, hp))


def _swap(spec):
    return pl.BlockSpec(spec.block_shape, lambda hp, b, _f=spec.index_map: _f(b, hp))


def gdn_prep(u, cw, alog_row, dt_row, *, seq, name):
    t = u.shape[0]
    nc = seq // DN_CHUNK

    def body(q_ref, k_ref, v_ref, ba_ref, cq_ref, ck_ref, cv_ref, alog_ref, dt_ref, loc_ref, egl_ref, act_sc, b_sc, gc_sc):
        _gdn_inputs((q_ref, k_ref, v_ref), (cq_ref, ck_ref, cv_ref), ba_ref, alog_ref, dt_ref, pl.program_id(1),
                    act_sc, b_sc, gc_sc)

        def finish(rows, m, t):
            loc_ref[0, rows, :] = m["qg"]
            loc_ref[1, rows, :] = m["kd"]
            loc_ref[2, rows, :] = _pk_nn(t, m["vb"])
            loc_ref[3, rows, :] = _pk_nn(t, m["kbg"])
            loc_ref[4, rows, :] = m["amat"]
            egl_ref[rows, :] = m["egl"]

        _gdn_chunk_loop(nc, act_sc, b_sc, gc_sc, finish)

    return pl.pallas_call(
        body, name=name, grid=(t // seq, DN_HEADS // 2), in_specs=_gdn_in_specs(seq), out_specs=[_pair(seq, 5), _pair(seq)],
        out_shape=[SDS((5, t, DN_HEADS * HEAD), F32), SDS((t, DN_HEADS * HEAD), F32)],
        scratch_shapes=[pltpu.VMEM((3, seq, 2 * HEAD), F32)] + [pltpu.VMEM((seq, 2 * HEAD), F32)] * 2,
        compiler_params=_params(("parallel", "parallel")))(u, u, u, u, cw, cw, cw, alog_row, dt_row)


def _gated_norm2(o, z, gn):
    r = lax.rsqrt(_half_sum(o * o) * (1.0 / HEAD) + EPS)
    return o * r, _sigmoid(z), r


def gdn_scan(loc, egl, u, gn, *, seq, name):
    t = u.shape[0]
    nc = seq // DN_CHUNK

    def body(loc_ref, egl_ref, z_ref, gn_ref, y_ref, o_ref, vn_ref, st_ref):
        gn = gn_ref[...]
        bdm = _bd_mask()

        def step(c, state):
            rows = pl.ds(pl.multiple_of(c * DN_CHUNK, DN_CHUNK), DN_CHUNK)
            st_ref[rows, :] = _fold(state)
            vn = loc_ref[2, rows, :] - _nn(loc_ref[3, rows, :], state)
            o = _nn(loc_ref[0, rows, :], state) + _pk_nn(loc_ref[4, rows, :], vn)
            vn_ref[rows, :] = vn
            o_ref[rows, :] = o
            zz = z_ref[rows, :]
            on, sig, _ = _gated_norm2(o, zz, gn)
            y_ref[rows, :] = on * gn * (zz * sig)
            return state * _row0(egl_ref[rows, :]) + jnp.where(bdm, _tn(loc_ref[1, rows, :], vn), 0.0)

        lax.fori_loop(0, nc, step, jnp.zeros((2 * HEAD, 2 * HEAD), F32))

    zspec = pl.BlockSpec((seq, 2 * HEAD), lambda b, hp: (b, COL_DNZ + hp))
    out = SDS((t, DN_HEADS * HEAD), F32)
    return pl.pallas_call(
        body, name=name, grid=(t // seq, DN_HEADS // 2), in_specs=[_pair(seq, 5), _pair(seq), zspec, _whole((1, 2 * HEAD))],
        out_specs=[_pair(seq)] * 4, out_shape=[out] * 4,
        compiler_params=_params(("parallel", "parallel")))(loc, egl, u, gn)


def gdn_scan_bwd(loc, egl, u, gn, o, vn, states, dy, *, seq, name):
    t = u.shape[0]
    nc = seq // DN_CHUNK

    def body(loc_ref, egl_ref, z_ref, gn_ref, o_ref, vn_ref, st_ref, dy_ref, dloc_ref, degl_ref, dz_ref, dgn_ref):
        @pl.when((pl.program_id(0) == 0) & (pl.program_id(1) == 0))
        def _():
            dgn_ref[...] = jnp.zeros_like(dgn_ref)

        gn = gn_ref[...]
        bdm = _bd_mask()
        shape = (DN_CHUNK, 2 * HEAD)
        tril = _iota2(shape, 0) >= (_iota2(shape, 1) & (HEAD - 1))

        def step(i, carry):
            ds, dgn = carry
            rows = pl.ds(pl.multiple_of((nc - 1 - i) * DN_CHUNK, DN_CHUNK), DN_CHUNK)
            dy, zz, oo = dy_ref[rows, :], z_ref[rows, :], o_ref[rows, :]
            on, sig, r = _gated_norm2(oo, zz, gn)
            sz = zz * sig
            dz_ref[rows, :] = dy * on * gn * (sig * (1.0 + zz * (1.0 - sig)))
            dgn = dgn + jnp.sum(dy * on * sz, axis=0, keepdims=True)
            don = dy * gn * sz
            do = r * (don - on * _half_sum(don * on) * (1.0 / HEAD))
            state, vnew = _bd(st_ref[rows, :]), vn_ref[rows, :]
            qg, kd, w, amat = loc_ref[0, rows, :], loc_ref[1, rows, :], loc_ref[3, rows, :], loc_ref[4, rows, :]
            dvn = _pk_tn(amat, do) + _nn(kd, ds)
            dloc_ref[0, rows, :] = _nt(do, state)
            dloc_ref[1, rows, :] = _nt(vnew, ds)
            dloc_ref[2, rows, :] = dvn
            dloc_ref[3, rows, :] = -_nt(dvn, state)
            dloc_ref[4, rows, :] = jnp.where(tril, _pk_nt(do, vnew), 0.0)
            degl = _half_sum(jnp.sum(state * ds, axis=0, keepdims=True))
            degl_ref[rows, :] = jnp.broadcast_to(degl, shape)
            grow = jnp.where(bdm, _tn(qg, do) - _tn(w, dvn), 0.0)
            return ds * _row0(egl_ref[rows, :]) + grow, dgn

        _, dgn = lax.fori_loop(0, nc, step, (jnp.zeros((2 * HEAD, 2 * HEAD), F32), jnp.zeros((1, 2 * HEAD), F32)))
        dgn_ref[...] += dgn

    zspec = pl.BlockSpec((seq, 2 * HEAD), lambda b, hp: (b, COL_DNZ + hp))
    one = _pair(seq)
    out = SDS((t, DN_HEADS * HEAD), F32)
    return pl.pallas_call(
        body, name=name, grid=(t // seq, DN_HEADS // 2),
        in_specs=[_pair(seq, 5), one, zspec, _whole((1, 2 * HEAD)), one, one, one, one],
        out_specs=[_pair(seq, 5), one, one, _whole((1, 2 * HEAD))],
        out_shape=[SDS((5, t, DN_HEADS * HEAD), F32), out, out, SDS((1, 2 * HEAD), F32)],
        compiler_params=_params(("arbitrary", "arbitrary")))(loc, egl, u, gn, o, vn, states, dy)


def gdn_prep_bwd(u, cw, alog_row, dt_row, dloc, degl, *, seq, name):
    t = u.shape[0]
    nc = seq // DN_CHUNK

    def body(q_ref, k_ref, v_ref, ba_ref, cq_ref, ck_ref, cv_ref, alog_ref, dt_ref, dloc_ref, degl_ref,
             dqkv_ref, dba_ref, dcw_ref, dhs_ref, act_sc, b_sc, gc_sc, c_sc):
        hp = pl.program_id(0)

        @pl.when(pl.program_id(1) == 0)
        def _():
            dcw_ref[...] = jnp.zeros_like(dcw_ref)
            dhs_ref[...] = jnp.zeros_like(dhs_ref)

        pre_refs, cw_refs = (q_ref, k_ref, v_ref), (cq_ref, ck_ref, cv_ref)
        _gdn_inputs(pre_refs, cw_refs, ba_ref, alog_ref, dt_ref, hp, act_sc, b_sc, gc_sc, c_sc)

        def finish(rows, m, tt):
            q, k, v, b = m["q"], m["k"], m["v"], m["b"]
            dqg, dkd, du, dw, da = (dloc_ref[x, rows, :] for x in range(5))
            dm, eg = m["dm"], m["eg"]
            dt = _pk_nt(du, m["vb"]) + _pk_nt(dw, m["kbg"])
            dvb, dkbg = _pk_tn(tt, du), _pk_tn(tt, dw)
            yield
            dtt = _pk_nt(dt, tt, hi=True)
            yield
            dl = jnp.where(m["strict"], -_pk_tn(tt, dtt, hi=True), 0.0)
            yield
            dkk = dl * dm
            dqk = da * dm
            dd = dl * m["kk"] + da * m["qk"]
            dkb = _pk_nn(dkk, k) + dkbg * eg
            dq = _pk_nn(dqk, k) + dqg * eg
            yield
            dk = _pk_tn(dkk, m["kb"]) + _pk_tn(dqk, q) + dkd * m["ekd"] + dkb * b
            db = _half_sum(dkb * k + dvb * v)
            yield
            mx = jnp.where(m["tril"], dd * dm, 0.0)
            tk = _half_sum(dkd * m["kd"])
            colsum = jnp.where(m["eye"], jnp.broadcast_to(jnp.sum(mx, axis=0, keepdims=True), mx.shape), 0.0)
            dgc = _half_sum(mx) - _half_sum(colsum) + _half_sum(dqg * m["qg"] + dkbg * m["kbg"]) - tk
            dglast = jnp.sum(tk, axis=0, keepdims=True) + _row0(degl_ref[rows, :]) * jnp.exp(m["glast"])
            act_sc[0, rows, :] = dq
            act_sc[1, rows, :] = dk
            act_sc[2, rows, :] = dvb * b
            b_sc[rows, :] = db
            gc_sc[rows, :] = dgc + jnp.where(m["row"] == DN_CHUNK - 1, dglast, 0.0)

        _gdn_chunk_loop(nc, act_sc, b_sc, gc_sc, finish)

        beta, g, beta_blk, sp_arg, a_exp, g_blk = _gdn_gates(ba_ref, alog_ref, dt_ref, hp)
        dg = _chunk_rev_cumsum(gc_sc[...])
        lane = _iota2(beta_blk.shape, 1)
        ha = 2 * hp
        db = b_sc[...]
        at = lambda idx, x_a, x_b: (jnp.where(lane == idx, _lane_col(x_a, 0), 0.0)
                                    + jnp.where(lane == idx + 1, _lane_col(x_b, HEAD), 0.0))
        dg_blk = at(DN_HEADS + ha, dg, dg)
        dal = dg_blk * (-a_exp) * _sigmoid(sp_arg)
        dba_ref[...] = at(ha, db, db) * beta_blk * (1.0 - beta_blk) + dal
        dhs_ref[0:1, :] += jnp.sum(dg_blk * g_blk, axis=0, keepdims=True)
        dhs_ref[1:2, :] += jnp.sum(dal, axis=0, keepdims=True)
        for idx in range(3):
            c = c_sc[idx]
            _, sig, hat, r = _gdn_act(c, _DN_SCALE[idx])
            da_ = act_sc[idx]
            if _DN_SCALE[idx] is not None:
                da_ = da_ * _DN_SCALE[idx]
                da_ = r * (da_ - hat * _half_sum(da_ * hat))
            dx, dcw = _conv_bwd(da_ * (sig * (1.0 + c * (1.0 - sig))), pre_refs[idx][...],
                                [cw_refs[idx][k:k + 1, :] for k in range(4)])
            dqkv_ref[idx] = dx
            dcw_ref[idx] += dcw

    pair = DN_HEADS // 2
    in_specs = [_swap(s) for s in _gdn_in_specs(seq)] + [_swap(_pair(seq, 5)), _swap(_pair(seq))]
    return pl.pallas_call(
        body, name=name, grid=(pair, t // seq), in_specs=in_specs,
        out_specs=[_swap(_pair(seq, 3)), pl.BlockSpec((None, seq, 2 * HEAD), lambda hp, b: (hp, b, 0)),
                   pl.BlockSpec((3, 4, 2 * HEAD), lambda hp, b: (0, 0, hp)),
                   pl.BlockSpec((None, 2, 2 * HEAD), lambda hp, b: (hp, 0, 0))],
        out_shape=[SDS((3, t, DN_HEADS * HEAD), F32), SDS((pair, t, 2 * HEAD), F32), SDS((3, 4, DN_HEADS * HEAD), F32),
                   SDS((pair, 2, 2 * HEAD), F32)],
        scratch_shapes=[pltpu.VMEM((3, seq, 2 * HEAD), F32)] + [pltpu.VMEM((seq, 2 * HEAD), F32)] * 2
        + [pltpu.VMEM((3, seq, 2 * HEAD), F32)],
        compiler_params=_params(("arbitrary", "arbitrary")))(u, u, u, u, cw, cw, cw, alog_row, dt_row, dloc, degl)


def mix_out(y_lru, o, y_dn, w_out, h, *, name, tm=512):
    t, d = h.shape
    tm = min(tm, t)

    def body(a_ref, b_ref, c_ref, w_ref, h_ref, o_ref, y_ref):
        y_ref[:, 0:LRU_W] = a_ref[...].astype(BF16)
        y_ref[:, LRU_W:LRU_W + ATT_W] = b_ref[...].astype(BF16)
        y_ref[:, LRU_W + ATT_W:] = c_ref[...].astype(BF16)
        o_ref[...] = h_ref[...] + _nn(y_ref[...], w_ref[...])

    rows = lambda width: pl.BlockSpec((tm, width), lambda i: (i, 0))
    return pl.pallas_call(
        body, name=name, grid=(t // tm,), in_specs=[rows(LRU_W), rows(ATT_W), rows(LRU_W), _whole((d, d)), rows(d)],
        out_specs=[rows(d), rows(d)], out_shape=[SDS((t, d), F32), SDS((t, d), BF16)],
        compiler_params=_params(("parallel",)))(y_lru, o, y_dn, w_out, h)


def mix_out_bwd(dout, w_out, *, name, tm=512):
    t, d = dout.shape
    tm = min(tm, t)

    def body(d_ref, w_ref, a_ref, b_ref, c_ref):
        dy = _nt(d_ref[...], w_ref[...])
        a_ref[...] = dy[:, 0:LRU_W]
        b_ref[...] = dy[:, LRU_W:LRU_W + ATT_W]
        c_ref[...] = dy[:, LRU_W + ATT_W:]

    rows = lambda width: pl.BlockSpec((tm, width), lambda i: (i, 0))
    return pl.pallas_call(
        body, name=name, grid=(t // tm,), in_specs=[rows(d), _whole((d, d))], out_specs=[rows(LRU_W), rows(ATT_W), rows(LRU_W)],
        out_shape=[SDS((t, LRU_W), F32), SDS((t, ATT_W), F32), SDS((t, LRU_W), F32)],
        compiler_params=_params(("parallel",)))(dout, w_out)


def mix_in_bwd(h, gain, dout, w_in, dx, dgate, dq, dk, dv, dqkv, dz, dba, *, name, tm=512):
    t, d = h.shape
    tm = min(tm, t)

    def body(h_ref, g_ref, do_ref, w_ref, dx_ref, dgate_ref, dq_ref, dk_ref, dv_ref, dqkv_ref, dz_ref, dba_ref,
             dh_ref, dg_ref, du_ref):
        @pl.when(pl.program_id(0) == 0)
        def _():
            dg_ref[...] = jnp.zeros_like(dg_ref)

        off = 0
        for piece in (dx_ref[...], dgate_ref[...], dq_ref[...], dk_ref[...], dv_ref[...], dqkv_ref[0], dqkv_ref[1],
                      dqkv_ref[2], dz_ref[...], dba_ref[0] + dba_ref[1]):
            du_ref[:, off:off + piece.shape[1]] = piece.astype(BF16)
            off += piece.shape[1]
        du_ref[:, off:] = jnp.zeros((tm, D_IN_PAD - off), BF16)
        g = g_ref[...]
        _, xh, r = _rms_fwd(h_ref[...], g)
        dh, dg = _rms_bwd(_nt(du_ref[...], w_ref[...]), xh, r, g)
        dh_ref[...] = do_ref[...] + dh
        dg_ref[...] += dg

    rows = lambda width: pl.BlockSpec((tm, width), lambda i: (i, 0))
    return pl.pallas_call(
        body, name=name, grid=(t // tm,),
        in_specs=[rows(d), _whole((1, d)), rows(d), _whole((d, D_IN_PAD)), rows(LRU_W), rows(LRU_W), rows(ATT_W),
                  rows(2 * HEAD), rows(2 * HEAD), pl.BlockSpec((3, tm, DN_HEADS * HEAD), lambda i: (0, i, 0)),
                  rows(DN_HEADS * HEAD), pl.BlockSpec((2, tm, 2 * HEAD), lambda i: (0, i, 0))],
        out_specs=[rows(d), _whole((1, d)), rows(D_IN_PAD)],
        out_shape=[SDS((t, d), F32), SDS((1, d), F32), SDS((t, D_IN_PAD), BF16)],
        compiler_params=_params(("arbitrary",)))(h, gain, dout, w_in, dx, dgate, dq, dk, dv, dqkv, dz, dba)


def _block_diag(w):
    out = jnp.zeros((LRU_W, LRU_W), w.dtype)
    for h in range(LRU_W // HEAD):
        out = lax.dynamic_update_slice(out, w[h], (h * HEAD, h * HEAD))
    return out


def _diag_blocks(w):
    per = LRU_HALF // HEAD
    return jnp.stack([w[h // per, (h % per) * HEAD:(h % per + 1) * HEAD, (h % per) * HEAD:(h % per + 1) * HEAD]
                      for h in range(LRU_W // HEAD)])


def layer_params(w, wl, l, bias):
    row = lambda a: a[l].reshape(1, -1)
    return dict(
        ffn1_norm=row(w["ffn1_norm"]), ffn1=(wl["ffn1_w_gate"], wl["ffn1_w_up"], wl["ffn1_w_down"]),
        mix_norm=row(w["mix_norm"]), w_in=wl["w_in"],
        lru=(wl["lru_conv_w"], row(w["lru_conv_b"]), _block_diag(w["lru_w_a"][l]), row(w["lru_b_a"]),
             _block_diag(w["lru_w_x"][l]), row(w["lru_b_x"]), row(w["lru_lambda"])),
        bias=bias, sink_rows=jnp.repeat(w["attn_sinks"][l], BLOCK_Q).reshape(ATT_HEADS * BLOCK_Q, 1),
        dn_cw=wl["dn_conv_w"], dn_alog=_ba_row(w["dn_a_log"][l]), dn_dt=_ba_row(w["dn_dt_bias"][l]),
        dn_norm=jnp.tile(row(w["dn_norm"]), (1, 2)), w_out=wl["w_out"],
        ffn2_norm=row(w["ffn2_norm"]), ffn2=(wl["ffn2_w_gate"], wl["ffn2_w_up"], wl["ffn2_w_down"]),
        ple_norm=row(w["ple_norm"]), ple_w_gate=wl["ple_w_gate"], ple_w_proj=wl["ple_w_proj"])


def _ba_row(per_head):
    return jnp.pad(per_head, (DN_HEADS, 2 * HEAD - 2 * DN_HEADS)).reshape(1, 2 * HEAD)


def mixer_fwd(h, p, nb, seq, tag):
    u, n = norm_matmul(h, p["mix_norm"], p["w_in"], name=f"mix_in_{tag}")
    y_lru = lru_fwd(u, *p["lru"], seq=seq, name=f"lru_fwd_{tag}")
    o = swa_fwd(u, p["bias"], p["sink_rows"], seq=seq, name=f"swa_fwd_{tag}")
    loc, egl = gdn_prep(u, p["dn_cw"], p["dn_alog"], p["dn_dt"], seq=seq, name=f"gdn_prep_{tag}")
    y_dn, o_raw, vn, st = gdn_scan(loc, egl, u, p["dn_norm"], seq=seq, name=f"gdn_scan_{tag}")
    out, ycat = mix_out(y_lru, o, y_dn, p["w_out"], h, name=f"mix_out_{tag}")
    return out, dict(h=h, u=u, n=n, loc=loc, egl=egl, o_raw=o_raw, vn=vn, st=st, ycat=ycat)


def mixer_bwd(dout, s, p, nb, seq, tag):
    u = s["u"]
    dy_lru, do, dy_dn = mix_out_bwd(dout, p["w_out"], name=f"mix_out_dx_{tag}")
    g = {"w_out": matmul(s["ycat"], dout, ta=True, name=f"mix_out_dw_{tag}")}
    dx, dgate, dcw, dwa, dwx, dvec = lru_bwd(u, *p["lru"], dy_lru, seq=seq, name=f"lru_bwd_{tag}")
    g.update(lru_conv_w=dcw, lru_conv_b=dvec[0], lru_w_a=_diag_blocks(dwa), lru_b_a=dvec[1], lru_w_x=_diag_blocks(dwx),
             lru_b_x=dvec[2], lru_lambda=dvec[3])
    dq, dk, dv, dbias, dsink = swa_bwd(u, p["bias"], p["sink_rows"], do, seq=seq, name=f"swa_bwd_{tag}")
    g.update(attn_sinks=dsink.reshape(ATT_HEADS, BLOCK_Q).sum(axis=1), bias=dbias)
    dloc, degl, dz, dgn = gdn_scan_bwd(s["loc"], s["egl"], u, p["dn_norm"], s["o_raw"], s["vn"], s["st"], dy_dn, seq=seq,
                                       name=f"gdn_scan_bwd_{tag}")
    dqkv, dba, dcw3, dhs = gdn_prep_bwd(u, p["dn_cw"], p["dn_alog"], p["dn_dt"], dloc, degl, seq=seq,
                                        name=f"gdn_prep_bwd_{tag}")
    dhs = dhs.sum(axis=0)[:, DN_HEADS:2 * DN_HEADS]
    g.update(dn_conv_w=dcw3.transpose(1, 0, 2).reshape(4, 3 * DN_HEADS * HEAD), dn_a_log=dhs[0], dn_dt_bias=dhs[1],
             dn_norm=dgn[0, :HEAD] + dgn[0, HEAD:])
    dh, dgain, du = mix_in_bwd(s["h"], p["mix_norm"], dout, p["w_in"], dx, dgate, dq, dk, dv, dqkv, dz, dba,
                               name=f"mix_in_bwd_{tag}")
    g["w_in"] = matmul(s["n"], du, ta=True, name=f"mix_in_dw_{tag}")
    g["mix_norm"] = dgain[0]
    return dh, g


SHARDED = ("ffn1_w_gate", "ffn1_w_up", "ffn1_w_down", "w_in", "w_out", "ffn2_w_gate", "ffn2_w_up", "ffn2_w_down",
           "ple_w_gate", "ple_w_proj")
PER_LAYER_SMALL = ("ffn1_norm", "mix_norm", "lru_conv_w", "lru_conv_b", "lru_w_a", "lru_b_a", "lru_w_x", "lru_b_x",
                   "lru_lambda", "attn_sinks", "dn_conv_w", "dn_a_log", "dn_dt_bias", "dn_norm", "ffn2_norm", "ple_norm")


def _col_shards(a):
    r, c = a.shape
    return a.reshape(r, N_CHIP, c // N_CHIP).transpose(1, 0, 2)


def local_step(x, p, target, w, layer_weights, layer_grads, bmap, nb, seq):
    bias = relbias_fwd(w["rel_bias"], bmap, name="relbias_fwd")
    h, saved = x, []
    for l in range(N_LAYER):
        pr = layer_params(w, layer_weights(l, h), l, bias)
        s = dict(h0=h)
        h = ffn_fwd(h, pr["ffn1_norm"], *pr["ffn1"], name=f"ffn1_fwd_{l}")
        h, s["mix"] = mixer_fwd(h, pr, nb, seq, l)
        s["h2"] = h
        h = ffn_fwd(h, pr["ffn2_norm"], *pr["ffn2"], name=f"ffn2_fwd_{l}")
        s["h3"] = h
        h = ple_fwd(h, pr["ple_norm"], pr["ple_w_gate"], p[l], pr["ple_w_proj"], name=f"ple_fwd_{l}")
        saved.append((pr, s))
    dh, dgf, loss = loss_head(h, w["final_norm"].reshape(1, -1), target, name="loss_head")

    per_layer, dbias, token = [None] * N_LAYER, None, None
    for l in reversed(range(N_LAYER)):
        pr, s = saved[l]
        g = {}
        dout = dh
        ple_norm = pr["ple_norm"] if token is None else pr["ple_norm"] + token[0:1, 0:1]
        dh, n, dga, dpp, dg = ple_bwd(s["h3"], ple_norm, pr["ple_w_gate"], p[l], pr["ple_w_proj"], dout, name=f"ple_bwd_{l}")
        g["ple_norm"] = dg[0]
        g["ple_w_gate"] = matmul(n, dga, ta=True, name=f"ple_dwg_{l}").reshape(N_CHIP, -1, D_MODEL)
        g["ple_w_proj"] = _col_shards(matmul(p[l], dpp, ta=True, name=f"ple_dwp_{l}"))
        for nm, hin in (("ffn2", s["h2"]), ("ffn1", s["h0"])):
            if nm == "ffn1":
                dh, gm = mixer_bwd(dh, s["mix"], pr, nb, seq, l)
                dbias = gm.pop("bias") if dbias is None else dbias + gm.pop("bias")
                gm["w_in"] = _col_shards(gm["w_in"][:, :D_IN])
                gm["w_out"] = gm["w_out"].reshape(N_CHIP, -1, D_MODEL)
                g.update(gm)
            dout = dh
            dh, n, da, db, sact, dg = ffn_bwd_act(hin, pr[nm + "_norm"], dout, *pr[nm], name=f"{nm}_bwd_act_{l}")
            g[nm + "_norm"] = dg[0]
            g[nm + "_w_gate"], g[nm + "_w_up"], g[nm + "_w_down"] = ffn_bwd_w(n, da, db, sact, dout, name=f"{nm}_bwd_w_{l}")
        token = layer_grads(l, {k: g.pop(k) for k in SHARDED}, dh)
        per_layer[l] = g
    grads = {k: jnp.stack([per_layer[l][k] for l in range(N_LAYER)]) for k in PER_LAYER_SMALL}
    grads["rel_bias"] = relbias_bwd(dbias, bmap, name="relbias_bwd")[:, :ATT_HEADS]
    grads["final_norm"] = dgf[0]
    return loss, dh, grads


HBM_SPEC = pl.BlockSpec(memory_space=pltpu.HBM)


def _place():
    x, y, c = lax.axis_index("x"), lax.axis_index("y"), lax.axis_index("c")
    chips = [(1 - x, y), (x, 1 - y), (1 - x, 1 - y)]
    return x, y, c, 2 * x + y, (x, y, 1 - c), chips, [2 * cx + cy for cx, cy in chips]


def _remote(src, dst, send_sem, recv_sem, to):
    return pltpu.make_async_remote_copy(src_ref=src, dst_ref=dst, send_sem=send_sem, recv_sem=recv_sem, device_id=to,
                                        device_id_type=MESH)


def place_shard(w, chip_arr, dtype, *, name):
    nl, r, c = w.shape
    tr = next(cand for cand in (256, 128, 64, 32, 16, 8, r) if r % cand == 0)

    def body(chip_ref, w_ref, o_ref):
        o_ref[...] = w_ref[...].astype(dtype)

    return pl.pallas_call(
        body, name=name,
        grid_spec=pltpu.PrefetchScalarGridSpec(
            num_scalar_prefetch=1, grid=(nl, r // tr),
            in_specs=[pl.BlockSpec((None, tr, c), lambda l, i, chip: (l, i, 0))],
            out_specs=pl.BlockSpec((None, None, tr, c), lambda l, i, chip: (chip[0], l, i, 0))),
        out_shape=SDS((N_CHIP, nl, r, c), dtype), compiler_params=_params(("parallel", "parallel")))(chip_arr, w)


def allgather_shards(shards, *, name):
    n = len(shards)

    def body(*refs):
        outs = refs[n:2 * n]
        send, recv, fsend, frecv = refs[2 * n:]
        x, y, c, me, sib, chips, cids = _place()
        first, passed = [], []
        for k in range(n):
            for j, chip in enumerate(chips):
                mine = outs[k].at[me, c]
                first.append(_remote(mine, mine, send.at[3 * k + j], recv.at[3 * k + j], (*chip, c)))
                first[-1].start()
        for k in range(n):
            for j in range(3):
                piece = outs[k].at[cids[j], c]
                _remote(piece, piece, send.at[3 * k + j], recv.at[3 * k + j], sib).wait_recv()
                passed.append(_remote(piece, piece, fsend.at[3 * k + j], frecv.at[3 * k + j], sib))
                passed[-1].start()
        for k in range(n):
            for j in range(3):
                piece = outs[k].at[cids[j], 1 - c]
                _remote(piece, piece, fsend.at[3 * k + j], frecv.at[3 * k + j], sib).wait_recv()
        for cp in first + passed:
            cp.wait_send()

    return pl.pallas_call(
        body, name=name, in_specs=[HBM_SPEC] * n, out_specs=[HBM_SPEC] * n,
        out_shape=[SDS(s.shape, s.dtype) for s in shards], input_output_aliases={k: k for k in range(n)},
        scratch_shapes=[pltpu.SemaphoreType.DMA((3 * n,))] * 4)(*shards)


def exchange_layers(gs, *, name):
    n = len(gs)

    def body(*refs):
        ins, outs, (send, recv) = refs[:n], refs[n:2 * n], refs[2 * n:]
        x, y, c, me, sib, chips, cids = _place()
        cps = [_remote(ins[k].at[1 - c], outs[k], send.at[k], recv.at[k], sib) for k in range(n)]
        for cp in cps:
            cp.start()
        for cp in cps:
            cp.wait()

    return pl.pallas_call(
        body, name=name, in_specs=[HBM_SPEC] * n, out_specs=[HBM_SPEC] * n,
        out_shape=[SDS(g.shape[1:], g.dtype) for g in gs], scratch_shapes=[pltpu.SemaphoreType.DMA((n,))] * 2)(*gs)


def reduce_to_shards(ss, *, name):
    n = len(ss)

    def body(*refs):
        ins, outs, (send, recv) = refs[:n], refs[n:2 * n], refs[2 * n:]
        x, y, c, me, sib, chips, cids = _place()
        cps = []
        for k in range(n):
            for j, chip in enumerate(chips):
                cps.append(_remote(ins[k].at[cids[j]], outs[k].at[j], send.at[3 * k + j], recv.at[3 * k + j], (*chip, c)))
                cps[-1].start()
        for k in range(n):
            for j in range(3):
                slot = outs[k].at[j]
                _remote(slot, slot, send.at[3 * k + j], recv.at[3 * k + j], sib).wait_recv()
        for cp in cps:
            cp.wait_send()

    return pl.pallas_call(
        body, name=name, in_specs=[HBM_SPEC] * n, out_specs=[HBM_SPEC] * n,
        out_shape=[SDS((N_CHIP - 1,) + s.shape[1:], s.dtype) for s in ss],
        scratch_shapes=[pltpu.SemaphoreType.DMA((3 * n,))] * 2)(*ss)


def share_layers(fs, *, name):
    n = len(fs)

    def body(*refs):
        outs, (send, recv) = refs[n:2 * n], refs[2 * n:]
        x, y, c, me, sib, chips, cids = _place()
        cps = [_remote(outs[k].at[c], outs[k].at[c], send.at[k], recv.at[k], sib) for k in range(n)]
        for cp in cps:
            cp.start()
        for k in range(n):
            theirs = outs[k].at[1 - c]
            _remote(theirs, theirs, send.at[k], recv.at[k], sib).wait_recv()
        for cp in cps:
            cp.wait_send()

    return pl.pallas_call(
        body, name=name, in_specs=[HBM_SPEC] * n, out_specs=[HBM_SPEC] * n, out_shape=[SDS(f.shape, f.dtype) for f in fs],
        input_output_aliases={k: k for k in range(n)}, scratch_shapes=[pltpu.SemaphoreType.DMA((n,))] * 2)(*fs)


N_DEV = 8


def allreduce_small(buf, *, name):
    rows = buf.shape[0]

    def body(in_ref, out_ref, gath, send, recv):
        x, y, c = lax.axis_index("x"), lax.axis_index("y"), lax.axis_index("c")
        mine = 4 * x + 2 * y + c
        gath[mine] = in_ref[...]
        cps = []
        for k in range(1, N_DEV):
            to = (x ^ (k >> 2), y ^ ((k >> 1) & 1), c ^ (k & 1))
            cps.append(_remote(in_ref, gath.at[mine], send.at[k - 1], recv.at[k - 1], to))
            cps[-1].start()
        for k in range(1, N_DEV):
            theirs = gath.at[4 * (x ^ (k >> 2)) + 2 * (y ^ ((k >> 1) & 1)) + (c ^ (k & 1))]
            _remote(theirs, theirs, send.at[k - 1], recv.at[k - 1], (x, y, c)).wait_recv()
        for cp in cps:
            cp.wait_send()
        acc = gath[0]
        for d in range(1, N_DEV):
            acc = acc + gath[d]
        out_ref[...] = acc

    vm = pl.BlockSpec(memory_space=pltpu.VMEM)
    return pl.pallas_call(
        body, name=name, in_specs=[vm], out_specs=vm, out_shape=SDS(buf.shape, F32),
        scratch_shapes=[pltpu.VMEM((N_DEV, rows, 128), F32), pltpu.SemaphoreType.DMA((N_DEV - 1,)),
                        pltpu.SemaphoreType.DMA((N_DEV - 1,))])(buf)


def add_sibling(g, r, c_arr, *, name, tr=256):
    _, m, cdim = g.shape
    assert m % tr == 0

    def body(c_ref, g_ref, r_ref, o_ref):
        o_ref[...] = (g_ref[...] + r_ref[...]).astype(o_ref.dtype)

    return pl.pallas_call(
        body, name=name,
        grid_spec=pltpu.PrefetchScalarGridSpec(
            num_scalar_prefetch=1, grid=(m // tr,),
            in_specs=[pl.BlockSpec((None, tr, cdim), lambda i, c: (c[0], i, 0)), pl.BlockSpec((tr, cdim), lambda i, c: (i, 0))],
            out_specs=pl.BlockSpec((tr, cdim), lambda i, c: (i, 0))),
        out_shape=SDS((m, cdim), BF16), compiler_params=_params(("parallel",)))(c_arr, g, r)


def sum_slots(own, r, place_arr, *, name, tr=256):
    _, m, cdim = r.shape
    tr = next(cand for cand in (tr, 128, 64, 32, 16, 8) if m % cand == 0)

    def body(p_ref, own_ref, r_ref, o_ref):
        o_ref[...] = ((own_ref[...].astype(F32) + r_ref[0].astype(F32)) + r_ref[1].astype(F32)) + r_ref[2].astype(F32)

    return pl.pallas_call(
        body, name=name,
        grid_spec=pltpu.PrefetchScalarGridSpec(
            num_scalar_prefetch=1, grid=(m // tr,),
            in_specs=[pl.BlockSpec((None, tr, cdim), lambda i, p: (p[0], i, 0)),
                      pl.BlockSpec((N_CHIP - 1, tr, cdim), lambda i, p: (0, i, 0))],
            out_specs=pl.BlockSpec((None, tr, cdim), lambda i, p: (p[1], i, 0))),
        out_shape=SDS((N_LAYER, m, cdim), F32), compiler_params=_params(("parallel",)))(place_arr, own, r)


SEM_SPEC = pl.BlockSpec(memory_space=pltpu.SEMAPHORE)
ANY_SPEC = pl.BlockSpec(memory_space=pl.ANY)
DATAFLOW = pltpu.SideEffectType.DATAFLOW_SIDE_EFFECTING


def _in_hbm(a):
    return pltpu.with_memory_space_constraint(a, pltpu.HBM)


def _my_rows(ref_rows, c, mine=True):
    half = ref_rows // 2
    start = (c if mine else 1 - c) * half
    return pl.ds(pl.multiple_of(start, 8), half)


def place_layer_shard(w, layer, chip_arr, dtype, *, name):
    _, r, c = w.shape
    tr = next(cand for cand in (256, 128, 64, 32, 16, 8, r) if r % cand == 0)

    def body(chip_ref, w_ref, o_ref):
        o_ref[...] = w_ref[...].astype(dtype)

    return pl.pallas_call(
        body, name=name,
        grid_spec=pltpu.PrefetchScalarGridSpec(
            num_scalar_prefetch=1, grid=(r // tr,),
            in_specs=[pl.BlockSpec((None, tr, c), lambda i, chip: (layer, i, 0))],
            out_specs=pl.BlockSpec((None, tr, c), lambda i, chip: (chip[0], i, 0))),
        out_shape=SDS((N_CHIP, r, c), dtype), compiler_params=_params(("parallel",)))(chip_arr, w)


def _gather_pieces(refs, n_split, c, me, cids):
    mine, theirs = [], []
    for k, ref in enumerate(refs):
        if k < n_split:
            rows = _my_rows(ref.shape[1], c)
            mine.append(ref.at[me, rows])
            theirs.append([ref.at[cid, rows] for cid in cids])
        else:
            mine.append(ref.at[me])
            theirs.append([ref.at[cid] for cid in cids])
    return mine, theirs


def gather_start(bufs, n_split, *, name):
    n = len(bufs)

    def body(*refs):
        ins, send, recv, token = refs[:n], refs[n], refs[n + 1], refs[-1]
        x, y, c, me, sib, chips, cids = _place()
        mine, _ = _gather_pieces(ins, n_split, c, me, cids)
        for k in range(n):
            for j, chip in enumerate(chips):
                _remote(mine[k], mine[k], send.at[3 * k + j], recv.at[3 * k + j], (*chip, c)).start()
        token[...] = jnp.zeros_like(token)

    out = pl.pallas_call(
        body, name=name, in_specs=[HBM_SPEC] * n,
        out_specs=[SEM_SPEC, SEM_SPEC] + [HBM_SPEC] * n + [pl.BlockSpec(memory_space=pltpu.VMEM)],
        out_shape=[pltpu.SemaphoreType.DMA((3 * n,)), pltpu.SemaphoreType.DMA((3 * n,))]
        + [pltpu.HBM(b.shape, b.dtype) for b in bufs] + [SDS((8, 128), F32)],
        input_output_aliases={k: k + 2 for k in range(n)},
        compiler_params=pltpu.CompilerParams(has_side_effects=DATAFLOW))(*[_in_hbm(b) for b in bufs])
    return out[0], out[1], list(out[2:2 + n]), out[-1]


def gather_wait(send, recv, bufs, n_split, after, *, name):
    n = len(bufs)

    def body(*refs):
        ins, send_ref, recv_ref = refs[:n], refs[n], refs[n + 1]
        x, y, c, me, sib, chips, cids = _place()
        mine, theirs = _gather_pieces(ins, n_split, c, me, cids)
        for k in range(n):
            for j in range(3):
                _remote(mine[k], mine[k], send_ref.at[3 * k + j], recv_ref.at[3 * k + j], sib).wait_send()
                _remote(theirs[k][j], theirs[k][j], send_ref.at[3 * k + j], recv_ref.at[3 * k + j], sib).wait_recv()

    return list(pl.pallas_call(
        body, name=name, in_specs=[HBM_SPEC] * n + [SEM_SPEC, SEM_SPEC, ANY_SPEC], out_specs=[HBM_SPEC] * n,
        out_shape=[pltpu.HBM(b.shape, b.dtype) for b in bufs], input_output_aliases={k: k for k in range(n)},
        compiler_params=pltpu.CompilerParams(has_side_effects=DATAFLOW))(*bufs, send, recv, after))


def gather_forward(bufs, *, name):
    n = len(bufs)

    def body(*refs):
        outs, (send, recv) = refs[n:2 * n], refs[2 * n:]
        x, y, c, me, sib, chips, cids = _place()
        cps = []
        for k in range(n):
            for j in range(3):
                piece = outs[k].at[cids[j], _my_rows(outs[k].shape[1], c)]
                cps.append(_remote(piece, piece, send.at[3 * k + j], recv.at[3 * k + j], sib))
                cps[-1].start()
        for k in range(n):
            for j in range(3):
                piece = outs[k].at[cids[j], _my_rows(outs[k].shape[1], c, mine=False)]
                _remote(piece, piece, send.at[3 * k + j], recv.at[3 * k + j], sib).wait_recv()
        for cp in cps:
            cp.wait_send()

    return list(pl.pallas_call(
        body, name=name, in_specs=[HBM_SPEC] * n, out_specs=[HBM_SPEC] * n, out_shape=[SDS(b.shape, b.dtype) for b in bufs],
        input_output_aliases={k: k for k in range(n)}, scratch_shapes=[pltpu.SemaphoreType.DMA((3 * n,))] * 2)(*bufs))


def reduce_exchange(gs, *, name):
    n = len(gs)

    def body(*refs):
        ins, outs, (send, recv) = refs[:n], refs[n:2 * n], refs[2 * n:]
        x, y, c, me, sib, chips, cids = _place()
        cps = [_remote(ins[k].at[pl.ds(0, N_CHIP), _my_rows(ins[k].shape[1], c, mine=False)], outs[k], send.at[k],
                       recv.at[k], sib) for k in range(n)]
        for cp in cps:
            cp.start()
        for cp in cps:
            cp.wait()

    return list(pl.pallas_call(
        body, name=name, in_specs=[HBM_SPEC] * n, out_specs=[HBM_SPEC] * n,
        out_shape=[SDS((N_CHIP, g.shape[1] // 2, g.shape[2]), g.dtype) for g in gs],
        scratch_shapes=[pltpu.SemaphoreType.DMA((n,))] * 2)(*gs))


def _half_tile(half):
    return next(cand for cand in (256, 176, 128, 64, 32, 16) if half % cand == 0)


def reduce_add(g, r, c_arr, *, name):
    _, rows, cdim = g.shape
    half = rows // 2
    tr = _half_tile(half)

    def body(c_ref, g_ref, r_ref, o_ref):
        o_ref[...] = (g_ref[...] + r_ref[...]).astype(o_ref.dtype)

    return pl.pallas_call(
        body, name=name,
        grid_spec=pltpu.PrefetchScalarGridSpec(
            num_scalar_prefetch=1, grid=(N_CHIP, half // tr),
            in_specs=[pl.BlockSpec((None, tr, cdim), lambda j, i, c: (j, c[0] * (half // tr) + i, 0)),
                      pl.BlockSpec((None, tr, cdim), lambda j, i, c: (j, i, 0))],
            out_specs=pl.BlockSpec((None, tr, cdim), lambda j, i, c: (j, i, 0))),
        out_shape=SDS((N_CHIP, half, cdim), BF16), compiler_params=_params(("parallel", "parallel")))(c_arr, g, r)


def reduce_start(ss, *, name):
    n = len(ss)

    def body(*refs):
        ins, lands, send, recv, token = refs[:n], refs[n:2 * n], refs[2 * n], refs[2 * n + 1], refs[-1]
        x, y, c, me, sib, chips, cids = _place()
        for k in range(n):
            for j, chip in enumerate(chips):
                _remote(ins[k].at[cids[j]], lands[k].at[j], send.at[3 * k + j], recv.at[3 * k + j], (*chip, c)).start()
        token[...] = jnp.zeros_like(token)

    lands = [_in_hbm(lax.empty((N_CHIP - 1,) + s.shape[1:], s.dtype)) for s in ss]
    out = pl.pallas_call(
        body, name=name, in_specs=[HBM_SPEC] * (2 * n),
        out_specs=[SEM_SPEC, SEM_SPEC] + [HBM_SPEC] * (2 * n) + [pl.BlockSpec(memory_space=pltpu.VMEM)],
        out_shape=[pltpu.SemaphoreType.DMA((3 * n,)), pltpu.SemaphoreType.DMA((3 * n,))]
        + [pltpu.HBM(b.shape, b.dtype) for b in list(ss) + lands] + [SDS((8, 128), F32)],
        input_output_aliases={k: k + 2 for k in range(2 * n)},
        compiler_params=pltpu.CompilerParams(has_side_effects=DATAFLOW))(*[_in_hbm(s) for s in ss], *lands)
    return out[0], out[1], list(out[2:2 + n]), list(out[2 + n:2 + 2 * n]), out[-1]


def reduce_wait(send, recv, ss, lands, after, *, name):
    n = len(ss)

    def body(*refs):
        ins, land_refs, send_ref, recv_ref = refs[:n], refs[n:2 * n], refs[2 * n], refs[2 * n + 1]
        x, y, c, me, sib, chips, cids = _place()
        for k in range(n):
            for j in range(3):
                _remote(ins[k].at[cids[j]], land_refs[k].at[j], send_ref.at[3 * k + j], recv_ref.at[3 * k + j],
                        sib).wait_send()
                _remote(ins[k].at[cids[j]], land_refs[k].at[j], send_ref.at[3 * k + j], recv_ref.at[3 * k + j],
                        sib).wait_recv()

    out = pl.pallas_call(
        body, name=name, in_specs=[HBM_SPEC] * (2 * n) + [SEM_SPEC, SEM_SPEC, ANY_SPEC], out_specs=[HBM_SPEC] * (2 * n),
        out_shape=[pltpu.HBM(b.shape, b.dtype) for b in list(ss) + list(lands)],
        input_output_aliases={k: k for k in range(2 * n)},
        compiler_params=pltpu.CompilerParams(has_side_effects=DATAFLOW))(*ss, *lands, send, recv, after)
    return list(out[:n]), list(out[n:])


def reduce_sum(own, land, place_arr, layer, acc, *, name):
    _, half, cdim = land.shape
    tr = _half_tile(half)

    def body(p_ref, own_ref, land_ref, *rest):
        o_ref = rest[-1]
        o_ref[...] = ((own_ref[...].astype(F32) + land_ref[0].astype(F32)) + land_ref[1].astype(F32)) + land_ref[2].astype(F32)

    in_specs = [pl.BlockSpec((None, tr, cdim), lambda i, p: (p[0], i, 0)),
                pl.BlockSpec((N_CHIP - 1, tr, cdim), lambda i, p: (0, i, 0))]
    args = [place_arr, own, land]
    if acc is not None:
        in_specs.append(ANY_SPEC)
        args.append(acc)
    return pl.pallas_call(
        body, name=name,
        grid_spec=pltpu.PrefetchScalarGridSpec(
            num_scalar_prefetch=1, grid=(half // tr,), in_specs=in_specs,
            out_specs=pl.BlockSpec((None, tr, cdim), lambda i, p: (layer, p[1] * (half // tr) + i, 0))),
        out_shape=SDS((N_LAYER, 2 * half, cdim), F32), input_output_aliases={} if acc is None else {3: 0},
        compiler_params=_params(("parallel",)))(*args)


def reduce_share(fs, *, name):
    n = len(fs)

    def body(*refs):
        outs, (send, recv) = refs[n:2 * n], refs[2 * n:]
        x, y, c, me, sib, chips, cids = _place()
        cps = []
        for k in range(n):
            piece = outs[k].at[pl.ds(0, N_LAYER), _my_rows(outs[k].shape[1], c)]
            cps.append(_remote(piece, piece, send.at[k], recv.at[k], sib))
            cps[-1].start()
        for k in range(n):
            theirs = outs[k].at[pl.ds(0, N_LAYER), _my_rows(outs[k].shape[1], c, mine=False)]
            _remote(theirs, theirs, send.at[k], recv.at[k], sib).wait_recv()
        for cp in cps:
            cp.wait_send()

    return list(pl.pallas_call(
        body, name=name, in_specs=[HBM_SPEC] * n, out_specs=[HBM_SPEC] * n, out_shape=[SDS(f.shape, f.dtype) for f in fs],
        input_output_aliases={k: k for k in range(n)}, scratch_shapes=[pltpu.SemaphoreType.DMA((n,))] * 2)(*fs))


WEIGHTS = ("ffn1_norm", "ffn1_w_gate", "ffn1_w_up", "ffn1_w_down", "mix_norm", "w_in", "lru_conv_w", "lru_conv_b", "lru_w_a",
           "lru_b_a", "lru_w_x", "lru_b_x", "lru_lambda", "attn_sinks", "rel_bias", "dn_conv_w", "dn_a_log", "dn_dt_bias",
           "dn_norm", "w_out", "ffn2_norm", "ffn2_w_gate", "ffn2_w_up", "ffn2_w_down", "ple_norm", "ple_w_gate",
           "ple_w_proj", "final_norm")
CONV_SHARDED = ("lru_conv_w", "dn_conv_w")
SMALL = tuple(k for k in WEIGHTS if k not in SHARDED)


def _pack(arrs):
    flat = []
    for a in arrs:
        v = a.reshape(-1)
        flat.append(jnp.pad(v, (0, -v.shape[0] % 128)))
    v = jnp.concatenate(flat)
    v = jnp.pad(v, (0, -v.shape[0] % 1024))
    return v.reshape(-1, 128)


def _unpack(buf, shapes):
    v, out, off = buf.reshape(-1), [], 0
    for s in shapes:
        n = int(np.prod(s))
        out.append(v[off:off + n].reshape(s))
        off += n + (-n % 128)
    return out


def _chip_cols(a):
    n, l, r, c = a.shape
    return a.transpose(1, 2, 0, 3).reshape(l, r, n * c)


def _chip_rows(a):
    n, l, r, c = a.shape
    return a.transpose(1, 0, 2, 3).reshape(l, n * r, c)


def kernel(x, p, ffn1_norm, ffn1_w_gate, ffn1_w_up, ffn1_w_down, mix_norm, w_in, lru_conv_w, lru_conv_b, lru_w_a, lru_b_a, lru_w_x, lru_b_x, lru_lambda, attn_sinks, rel_bias, dn_conv_w, dn_a_log, dn_dt_bias, dn_norm, w_out, ffn2_norm, ffn2_w_gate, ffn2_w_up, ffn2_w_down, ple_norm, ple_w_gate, ple_w_proj, final_norm, loss_target, m_ffn1_norm, m_ffn1_w_gate, m_ffn1_w_up, m_ffn1_w_down, m_mix_norm, m_w_in, m_lru_conv_w, m_lru_conv_b, m_lru_w_a, m_lru_b_a, m_lru_w_x, m_lru_b_x, m_lru_lambda, m_attn_sinks, m_rel_bias, m_dn_conv_w, m_dn_a_log, m_dn_dt_bias, m_dn_norm, m_w_out, m_ffn2_norm, m_ffn2_w_gate, m_ffn2_w_up, m_ffn2_w_down, m_ple_norm, m_ple_w_gate, m_ple_w_proj, m_final_norm, v_ffn1_norm, v_ffn1_w_gate, v_ffn1_w_up, v_ffn1_w_down, v_mix_norm, v_w_in, v_lru_conv_w, v_lru_conv_b, v_lru_w_a, v_lru_b_a, v_lru_w_x, v_lru_b_x, v_lru_lambda, v_attn_sinks, v_rel_bias, v_dn_conv_w, v_dn_a_log, v_dn_dt_bias, v_dn_norm, v_w_out, v_ffn2_norm, v_ffn2_w_gate, v_ffn2_w_up, v_ffn2_w_down, v_ple_norm, v_ple_w_gate, v_ple_w_proj, v_final_norm):
    given = dict(locals())
    ws = {k: given[k] for k in WEIGHTS}
    ms = {k: given["m_" + k] for k in WEIGHTS}
    vs = {k: given["v_" + k] for k in WEIGHTS}
    nb, seq, d = x.shape
    t = nb * seq
    cx, cy, cc = lax.axis_index("x"), lax.axis_index("y"), lax.axis_index("c")
    chip = 2 * cx + cy

    chip_arr = chip.astype(jnp.int32).reshape(1)
    c_arr = cc.astype(jnp.int32).reshape(1)
    place_arr = jnp.stack([chip, cc]).astype(jnp.int32)
    names = SHARDED + CONV_SHARDED
    n_big = len(SHARDED)

    started = []
    for l in range(N_LAYER):
        placed = [place_layer_shard(ws[k], l, chip_arr, F32 if k in CONV_SHARDED else BF16, name=f"place_{k}_{l}")
                  for k in names]
        started.append(gather_start(placed, n_big, name=f"gather_start_{l}"))

    def layer_weights(l, h):
        send, recv, bufs, _ = started[l]
        after = started[-1][3] if l == 0 else h
        bufs = gather_wait(send, recv, bufs, n_big, after, name=f"gather_wait_{l}")
        wl = dict(zip(names, gather_forward(bufs[:n_big], name=f"gather_forward_{l}") + bufs[n_big:]))
        for k in ("w_in", "ple_w_proj", "lru_conv_w", "dn_conv_w"):
            wl[k] = wl[k].transpose(1, 0, 2).reshape(wl[k].shape[1], -1)
        for k in ("w_out", "ple_w_gate"):
            wl[k] = wl[k].reshape(-1, wl[k].shape[-1])
        wl["w_in"] = jnp.pad(wl["w_in"], ((0, 0), (0, D_IN_PAD - D_IN)))
        return wl

    pending, finished = [], [None] * n_big

    def finish_reduce(after):
        send, recv, sums, lands, l = pending.pop()
        sums, lands = reduce_wait(send, recv, sums, lands, after, name=f"reduce_wait_{l}")
        for i, k in enumerate(SHARDED):
            finished[i] = reduce_sum(sums[i], lands[i], place_arr, l, finished[i], name=f"reduce_sum_{k}_{l}")

    def layer_grads(l, g, dh):
        gs = [g[k] for k in SHARDED]
        theirs = reduce_exchange(gs, name=f"reduce_exchange_{l}")
        sums = [reduce_add(a, b, c_arr, name=f"reduce_add_{k}_{l}") for k, a, b in zip(SHARDED, gs, theirs)]
        send, recv, sums, lands, token = reduce_start(sums, name=f"reduce_start_{l}")
        if pending:
            finish_reduce(dh)
        pending.append((send, recv, sums, lands, l))
        return token

    small_w = {k: ws[k] for k in SMALL if k not in CONV_SHARDED}
    bmap = jnp.asarray(_rel_bucket_map())
    loss, gx, grads = local_step(x.reshape(t, d), p.reshape(N_LAYER, t, PLE_DIM), loss_target.reshape(t, d), small_w,
                                 layer_weights, layer_grads, bmap, nb, seq)
    finish_reduce(gx)
    g_out = dict(zip(SHARDED, reduce_share(finished, name="reduce_share")))

    small_shapes = [grads[k].shape for k in SMALL]
    g_small = dict(zip(SMALL, _unpack(allreduce_small(_pack([grads[k] for k in SMALL]), name="allreduce_small"), small_shapes)))
    for k in CONV_SHARDED:
        width = ws[k].shape[-1]
        g_small[k] = lax.dynamic_slice_in_dim(g_small[k], chip * width, width, axis=2)
    g_out.update(g_small)

    delta, new_m, new_v = {}, {}, {}
    for k in SHARDED:
        two_d = lambda a: a.reshape(-1, a.shape[-1])
        res = adamw(two_d(ws[k]), two_d(g_out[k]), two_d(ms[k]), two_d(vs[k]), name=f"adamw_{k}")
        delta[k], new_m[k], new_v[k] = (r.reshape(ws[k].shape) for r in res)
    shapes = [ws[k].shape for k in SMALL]
    res = adamw(*[_pack([src[k] for k in SMALL]) for src in (ws, g_out, ms, vs)], name="adamw_small")
    for dst, r in zip((delta, new_m, new_v), res):
        dst.update(zip(SMALL, _unpack(r, shapes)))

    total = lax.psum(loss[0, 0], ("x", "y", "c"))
    return (total, gx.reshape(nb, seq, d), *[g_out[k] for k in WEIGHTS], *[delta[k] for k in WEIGHTS],
            *[new_m[k] for k in WEIGHTS], *[new_v[k] for k in WEIGHTS])
```

```python
import functools
import math

import numpy as np
import jax
import jax.numpy as jnp
from jax import lax
from jax.experimental import pallas as pl
from jax.experimental.pallas import tpu as pltpu

F32 = jnp.float32
BF16 = jnp.bfloat16

EPS = 1e-6
D_MODEL = 1024
D_FF = 2816
N_CHIP = 4
FF_BLK = D_FF // N_CHIP
HEAD = 64
LRU_W = 256
ATT_W = 512
ATT_HEADS = 8
KV_HEADS = 2
ATT_GROUP = 4
BLOCK_Q = 128
DN_HEADS = 4
DN_CHUNK = 64
D_IN = 2312
D_IN_PAD = 2560
PLE_DIM = 256
REL_BUCKETS = 32
LRU_C = 8.0
N_LAYER = 2

ADAM_LR, ADAM_B1, ADAM_B2, ADAM_EPS, ADAM_WD, ADAM_STEP = 0.001, 0.9, 0.999, 1e-08, 0.01, 10

VMEM_LIMIT = 56 << 20
MESH = pl.DeviceIdType.MESH
SDS = jax.ShapeDtypeStruct


def _dot(a, b, ca=1, cb=0, hi=False):
    dims = (((ca,), (cb,)), ((), ()))
    one = lambda u, v: lax.dot_general(u, v, dims, preferred_element_type=F32)
    a_hi, b_hi = a.astype(BF16), b.astype(BF16)
    if not hi:
        return one(a_hi, b_hi)
    a_lo = (a - a_hi.astype(F32)).astype(BF16)
    b_lo = (b - b_hi.astype(F32)).astype(BF16)
    return one(a_hi, b_hi) + (one(a_hi, b_lo) + one(a_lo, b_hi))


def _nn(a, b, hi=False):
    return _dot(a, b, 1, 0, hi)


def _nt(a, b, hi=False):
    return _dot(a, b, 1, 1, hi)


def _tn(a, b, hi=False):
    return _dot(a, b, 0, 0, hi)


def _sigmoid(x):
    return jax.nn.sigmoid(x)


def _softplus(x):
    return jnp.maximum(x, 0.0) + jnp.log1p(jnp.exp(-jnp.abs(x)))


def _neg_expm1(z):
    series = -z * (1.0 + z * (0.5 + z * (1.0 / 6.0 + z * (1.0 / 24.0 + z * (1.0 / 120.0)))))
    return jnp.where(z > -0.05, series, 1.0 - jnp.exp(z))


_GELU_C = math.sqrt(2.0 / math.pi)


def _gelu(x):
    t = jnp.tanh(_GELU_C * (x + 0.044715 * x * x * x))
    return 0.5 * x * (1.0 + t), t


def _gelu_grad(x, t):
    return 0.5 * (1.0 + t) + 0.5 * x * (1.0 - t * t) * _GELU_C * (1.0 + 3.0 * 0.044715 * x * x)


def _rms_fwd(h, g):
    r = lax.rsqrt(jnp.mean(h * h, axis=-1, keepdims=True) + EPS)
    xh = h * r
    return xh * g, xh, r


def _rms_bwd(dn, xh, r, g):
    dxh = dn * g
    dh = r * (dxh - xh * jnp.mean(dxh * xh, axis=-1, keepdims=True))
    return dh, jnp.sum(dn * xh, axis=0, keepdims=True)


def _shift_down(x, d, fill=0.0):
    row = lax.broadcasted_iota(jnp.int32, x.shape, 0)
    return jnp.where(row >= d, pltpu.roll(x, d, 0), fill)


def _shift_up(x, d, fill=0.0):
    n = x.shape[0]
    row = lax.broadcasted_iota(jnp.int32, x.shape, 0)
    return jnp.where(row < n - d, pltpu.roll(x, n - d, 0), fill)


def _conv_fwd(x, w):
    y = x * w[3]
    for k in range(3):
        y = y + _shift_down(x, 3 - k) * w[k]
    return y


def _conv_bwd(dy, x, w):
    dx = dy * w[3]
    rows = [None] * 4
    rows[3] = jnp.sum(dy * x, axis=0, keepdims=True)
    for k in range(3):
        dx = dx + _shift_up(dy, 3 - k) * w[k]
        rows[k] = jnp.sum(dy * _shift_down(x, 3 - k), axis=0, keepdims=True)
    r4 = lax.broadcasted_iota(jnp.int32, (4, x.shape[1]), 0)
    dw = jnp.zeros((4, x.shape[1]), F32)
    for k in range(4):
        dw = jnp.where(r4 == k, rows[k], dw)
    return dx, dw


def _params(sem=None, vmem=VMEM_LIMIT):
    return pltpu.CompilerParams(dimension_semantics=sem, vmem_limit_bytes=vmem)


def _whole(shape):
    nd = len(shape)
    return pl.BlockSpec(shape, lambda *_: (0,) * nd)


def matmul(a, b, *, name, ta=False, tb=False, residual=None, out_dtype=F32, tm=512, tn=512, tk=512):
    m, k = (a.shape[1], a.shape[0]) if ta else a.shape
    n = b.shape[0] if tb else b.shape[1]
    tm, tn, tk = min(tm, m), min(tn, n), min(tk, k)
    assert m % tm == 0 and n % tn == 0 and k % tk == 0, (m, n, k, tm, tn, tk)
    nk = k // tk

    def body(*refs):
        if residual is None:
            a_ref, b_ref, o_ref, acc = refs
        else:
            a_ref, b_ref, r_ref, o_ref, acc = refs
        kk = pl.program_id(2)

        @pl.when(kk == 0)
        def _():
            acc[...] = jnp.zeros_like(acc)

        acc[...] += _dot(a_ref[...], b_ref[...], 0 if ta else 1, 1 if tb else 0)

        @pl.when(kk == nk - 1)
        def _():
            out = acc[...]
            if residual is not None:
                out = out + r_ref[...]
            o_ref[...] = out.astype(out_dtype)

    a_spec = pl.BlockSpec((tk, tm), lambda i, j, kk: (kk, i)) if ta else pl.BlockSpec((tm, tk), lambda i, j, kk: (i, kk))
    b_spec = pl.BlockSpec((tn, tk), lambda i, j, kk: (j, kk)) if tb else pl.BlockSpec((tk, tn), lambda i, j, kk: (kk, j))
    o_spec = pl.BlockSpec((tm, tn), lambda i, j, kk: (i, j))
    in_specs, args = [a_spec, b_spec], [a, b]
    if residual is not None:
        in_specs.append(o_spec)
        args.append(residual)
    return pl.pallas_call(
        body, name=name, grid=(m // tm, n // tn, nk), in_specs=in_specs, out_specs=o_spec,
        out_shape=SDS((m, n), out_dtype), scratch_shapes=[pltpu.VMEM((tm, tn), F32)],
        compiler_params=_params(("parallel", "parallel", "arbitrary")))(*args)


def norm_matmul(h, gain, w, *, name, tm=512, tn=512):
    t, d = h.shape
    tm = min(tm, t)
    n = w.shape[1]
    assert t % tm == 0 and n % tn == 0

    def body(h_ref, g_ref, w_ref, u_ref, n_ref):
        @pl.when(pl.program_id(1) == 0)
        def _():
            n_ref[...] = _rms_fwd(h_ref[...], g_ref[...])[0].astype(BF16)

        u_ref[...] = _nn(n_ref[...], w_ref[...])

    return pl.pallas_call(
        body, name=name, grid=(t // tm, n // tn),
        in_specs=[pl.BlockSpec((tm, d), lambda i, j: (i, 0)), _whole((1, d)), pl.BlockSpec((d, tn), lambda i, j: (0, j))],
        out_specs=[pl.BlockSpec((tm, tn), lambda i, j: (i, j)), pl.BlockSpec((tm, d), lambda i, j: (i, 0))],
        out_shape=[SDS((t, n), F32), SDS((t, d), BF16)],
        compiler_params=_params(("parallel", "arbitrary")))(h, gain, w)


def rms_bwd(h, gain, dn, dres, *, name, tm=512):
    t, d = h.shape
    tm = min(tm, t)

    def body(h_ref, g_ref, dn_ref, dr_ref, dh_ref, dg_ref):
        @pl.when(pl.program_id(0) == 0)
        def _():
            dg_ref[...] = jnp.zeros_like(dg_ref)

        g = g_ref[...]
        _, xh, r = _rms_fwd(h_ref[...], g)
        dh, dg = _rms_bwd(dn_ref[...], xh, r, g)
        dh_ref[...] = dr_ref[...] + dh
        dg_ref[...] += dg

    row = pl.BlockSpec((tm, d), lambda i: (i, 0))
    return pl.pallas_call(
        body, name=name, grid=(t // tm,), in_specs=[row, _whole((1, d)), row, row],
        out_specs=[row, _whole((1, d))], out_shape=[SDS((t, d), F32), SDS((1, d), F32)],
        compiler_params=_params(("arbitrary",)))(h, gain, dn, dres)


def ffn_fwd(h, gain, wg, wu, wd, *, name, tm=512):
    t, d = h.shape
    tm = min(tm, t)

    def body(h_ref, g_ref, wg_ref, wu_ref, wd_ref, o_ref, n_sc, acc):
        j = pl.program_id(1)

        @pl.when(j == 0)
        def _():
            n_sc[...] = _rms_fwd(h_ref[...], g_ref[...])[0].astype(BF16)
            acc[...] = jnp.zeros_like(acc)

        n = n_sc[...]
        a = _nn(n, wg_ref[...])
        b = _nn(n, wu_ref[...])
        acc[...] += _nn(a * _sigmoid(a) * b, wd_ref[...])

        @pl.when(j == N_CHIP - 1)
        def _():
            o_ref[...] = h_ref[...] + 0.5 * acc[...]

    row = pl.BlockSpec((tm, d), lambda i, j: (i, 0))
    return pl.pallas_call(
        body, name=name, grid=(t // tm, N_CHIP),
        in_specs=[row, _whole((1, d)),
                  pl.BlockSpec((None, d, FF_BLK), lambda i, j: (j, 0, 0)),
                  pl.BlockSpec((None, d, FF_BLK), lambda i, j: (j, 0, 0)),
                  pl.BlockSpec((None, FF_BLK, d), lambda i, j: (j, 0, 0))],
        out_specs=row, out_shape=SDS((t, d), F32),
        scratch_shapes=[pltpu.VMEM((tm, d), BF16), pltpu.VMEM((tm, d), F32)],
        compiler_params=_params(("parallel", "arbitrary")))(h, gain, wg, wu, wd)


def ffn_bwd_act(h, gain, dout, wg, wu, wd, *, name, tm=512):
    t, d = h.shape
    tm = min(tm, t)

    def body(h_ref, g_ref, do_ref, wg_ref, wu_ref, wd_ref, dh_ref, n_ref, da_ref, db_ref, s_ref, dg_ref, dn_acc):
        i, j = pl.program_id(0), pl.program_id(1)

        @pl.when((i == 0) & (j == 0))
        def _():
            dg_ref[...] = jnp.zeros_like(dg_ref)

        @pl.when(j == 0)
        def _():
            n_ref[...] = _rms_fwd(h_ref[...], g_ref[...])[0].astype(BF16)
            dn_acc[...] = jnp.zeros_like(dn_acc)

        n = n_ref[...]
        a = _nn(n, wg_ref[...])
        b = _nn(n, wu_ref[...])
        sig = _sigmoid(a)
        sa = a * sig
        ds = _nt(0.5 * do_ref[...], wd_ref[...])
        db = ds * sa
        da = ds * b * (sig * (1.0 + a * (1.0 - sig)))
        s_ref[...] = (sa * b).astype(BF16)
        da_ref[...] = da.astype(BF16)
        db_ref[...] = db.astype(BF16)
        dn_acc[...] += _nt(da, wg_ref[...]) + _nt(db, wu_ref[...])

        @pl.when(j == N_CHIP - 1)
        def _():
            g = g_ref[...]
            _, xh, r = _rms_fwd(h_ref[...], g)
            dh, dg = _rms_bwd(dn_acc[...], xh, r, g)
            dh_ref[...] = do_ref[...] + dh
            dg_ref[...] += dg

    row = pl.BlockSpec((tm, d), lambda i, j: (i, 0))
    blk = pl.BlockSpec((None, tm, FF_BLK), lambda i, j: (j, i, 0))
    act = SDS((N_CHIP, t, FF_BLK), BF16)
    return pl.pallas_call(
        body, name=name, grid=(t // tm, N_CHIP),
        in_specs=[row, _whole((1, d)), row,
                  pl.BlockSpec((None, d, FF_BLK), lambda i, j: (j, 0, 0)),
                  pl.BlockSpec((None, d, FF_BLK), lambda i, j: (j, 0, 0)),
                  pl.BlockSpec((None, FF_BLK, d), lambda i, j: (j, 0, 0))],
        out_specs=[row, row, blk, blk, blk, _whole((1, d))],
        out_shape=[SDS((t, d), F32), SDS((t, d), BF16), act, act, act, SDS((1, d), F32)],
        scratch_shapes=[pltpu.VMEM((tm, d), F32)],
        compiler_params=_params(("arbitrary", "arbitrary")))(h, gain, dout, wg, wu, wd)


def ffn_bwd_w(n, da, db, s, dout, *, name, tk=512):
    t, d = n.shape
    tk = min(tk, t)

    def body(n_ref, da_ref, db_ref, s_ref, do_ref, dwg_ref, dwu_ref, dwd_ref):
        @pl.when(pl.program_id(1) == 0)
        def _():
            dwg_ref[...] = jnp.zeros_like(dwg_ref)
            dwu_ref[...] = jnp.zeros_like(dwu_ref)
            dwd_ref[...] = jnp.zeros_like(dwd_ref)

        nn = n_ref[...]
        dwg_ref[...] += _tn(nn, da_ref[...])
        dwu_ref[...] += _tn(nn, db_ref[...])
        dwd_ref[...] += _tn(s_ref[...], 0.5 * do_ref[...])

    row = pl.BlockSpec((tk, d), lambda j, kk: (kk, 0))
    blk = pl.BlockSpec((None, tk, FF_BLK), lambda j, kk: (j, kk, 0))
    return pl.pallas_call(
        body, name=name, grid=(N_CHIP, t // tk), in_specs=[row, blk, blk, blk, row],
        out_specs=[pl.BlockSpec((None, d, FF_BLK), lambda j, kk: (j, 0, 0)),
                   pl.BlockSpec((None, d, FF_BLK), lambda j, kk: (j, 0, 0)),
                   pl.BlockSpec((None, FF_BLK, d), lambda j, kk: (j, 0, 0))],
        out_shape=[SDS((N_CHIP, d, FF_BLK), F32), SDS((N_CHIP, d, FF_BLK), F32), SDS((N_CHIP, FF_BLK, d), F32)],
        compiler_params=_params(("parallel", "arbitrary")))(n, da, db, s, dout)


def ple_fwd(h, gain, wpg, pl_in, wpp, *, name, tm=512):
    t, d = h.shape
    tm = min(tm, t)
    pd = pl_in.shape[1]

    def body(h_ref, g_ref, wpg_ref, p_ref, wpp_ref, o_ref):
        hh = h_ref[...]
        n = _rms_fwd(hh, g_ref[...])[0]
        gate = _sigmoid(_nn(n, wpg_ref[...]))
        o_ref[...] = hh + gate * _nn(p_ref[...], wpp_ref[...])

    row = pl.BlockSpec((tm, d), lambda i: (i, 0))
    return pl.pallas_call(
        body, name=name, grid=(t // tm,),
        in_specs=[row, _whole((1, d)), _whole((d, d)), pl.BlockSpec((tm, pd), lambda i: (i, 0)), _whole((pd, d))],
        out_specs=row, out_shape=SDS((t, d), F32), compiler_params=_params(("parallel",)))(h, gain, wpg, pl_in, wpp)


def ple_bwd(h, gain, wpg, pl_in, wpp, dout, *, name, tm=512):
    t, d = h.shape
    tm = min(tm, t)
    pd = pl_in.shape[1]

    def body(h_ref, g_ref, wpg_ref, p_ref, wpp_ref, do_ref, dh_ref, n_ref, dga_ref, dpp_ref, dg_ref):
        @pl.when(pl.program_id(0) == 0)
        def _():
            dg_ref[...] = jnp.zeros_like(dg_ref)

        g = g_ref[...]
        n, xh, r = _rms_fwd(h_ref[...], g)
        gate = _sigmoid(_nn(n, wpg_ref[...]))
        pp = _nn(p_ref[...], wpp_ref[...])
        do = do_ref[...]
        dga = do * pp * gate * (1.0 - gate)
        dh, dg = _rms_bwd(_nt(dga, wpg_ref[...]), xh, r, g)
        dh_ref[...] = do + dh
        n_ref[...] = n.astype(BF16)
        dga_ref[...] = dga.astype(BF16)
        dpp_ref[...] = (do * gate).astype(BF16)
        dg_ref[...] += dg

    row = pl.BlockSpec((tm, d), lambda i: (i, 0))
    return pl.pallas_call(
        body, name=name, grid=(t // tm,),
        in_specs=[row, _whole((1, d)), _whole((d, d)), pl.BlockSpec((tm, pd), lambda i: (i, 0)), _whole((pd, d)), row],
        out_specs=[row, row, row, row, _whole((1, d))],
        out_shape=[SDS((t, d), F32), SDS((t, d), BF16), SDS((t, d), BF16), SDS((t, d), BF16), SDS((1, d), F32)],
        compiler_params=_params(("arbitrary",)))(h, gain, wpg, pl_in, wpp, dout)


def loss_head(h, gain, target, *, name, tm=512):
    t, d = h.shape
    tm = min(tm, t)

    def body(h_ref, g_ref, t_ref, dh_ref, dg_ref, l_ref):
        @pl.when(pl.program_id(0) == 0)
        def _():
            dg_ref[...] = jnp.zeros_like(dg_ref)
            l_ref[...] = jnp.zeros_like(l_ref)

        g = g_ref[...]
        y, xh, r = _rms_fwd(h_ref[...], g)
        err = y - t_ref[...]
        l_ref[...] += 0.5 * jnp.sum(jnp.mean(err * err, axis=-1, keepdims=True), axis=0, keepdims=True)
        dh, dg = _rms_bwd(err * (1.0 / d), xh, r, g)
        dh_ref[...] = dh
        dg_ref[...] += dg

    row = pl.BlockSpec((tm, d), lambda i: (i, 0))
    return pl.pallas_call(
        body, name=name, grid=(t // tm,), in_specs=[row, _whole((1, d)), row],
        out_specs=[row, _whole((1, d)), _whole((1, 1))],
        out_shape=[SDS((t, d), F32), SDS((1, d), F32), SDS((1, 1), F32)],
        compiler_params=_params(("arbitrary",)))(h, gain, target)


def adamw(w, g, m, v, *, name):
    r, c = w.shape
    tr = r
    for cand in (512, 256, 128, 64, 32, 16, 8):
        if r % cand == 0:
            tr = cand
            break

    def body(w_ref, g_ref, m_ref, v_ref, d_ref, nm_ref, nv_ref):
        gg = g_ref[...]
        mm = ADAM_B1 * m_ref[...] + (1.0 - ADAM_B1) * gg
        vv = ADAM_B2 * v_ref[...] + (1.0 - ADAM_B2) * (gg * gg)
        m_hat = mm / (1.0 - ADAM_B1 ** ADAM_STEP)
        v_hat = vv / (1.0 - ADAM_B2 ** ADAM_STEP)
        d_ref[...] = -ADAM_LR * (m_hat / (jnp.sqrt(v_hat) + ADAM_EPS) + ADAM_WD * w_ref[...])
        nm_ref[...] = mm
        nv_ref[...] = vv

    blk = pl.BlockSpec((tr, c), lambda i: (i, 0))
    out = SDS((r, c), F32)
    return pl.pallas_call(body, name=name, grid=(r // tr,), in_specs=[blk] * 4, out_specs=[blk] * 3,
                          out_shape=[out, out, out], compiler_params=_params(("parallel",)))(w, g, m, v)


def _scan_fwd(a, b):
    d = 1
    while d < a.shape[0]:
        b = a * _shift_down(b, d, 0.0) + b
        a = a * _shift_down(a, d, 1.0)
        d *= 2
    return b


def _scan_rev(a, b):
    d = 1
    while d < a.shape[0]:
        b = a * _shift_up(b, d, 0.0) + b
        a = a * _shift_up(a, d, 1.0)
        d *= 2
    return b


LRU_HALF = 128


def _lru_in_specs(seq):
    half = LRU_W // LRU_HALF
    vec = pl.BlockSpec((1, LRU_HALF), lambda j, b: (0, j))
    mat = pl.BlockSpec((LRU_HALF, LRU_HALF), lambda j, b: (j, j))
    return [pl.BlockSpec((seq, LRU_HALF), lambda j, b: (b, j)), pl.BlockSpec((seq, LRU_HALF), lambda j, b: (b, half + j)),
            pl.BlockSpec((4, LRU_HALF), lambda j, b: (0, j)), vec, mat, vec, mat, vec, vec]


def _lru_math(x_ref, gate_ref, cw_ref, cb_ref, wa_ref, ba_ref, wx_ref, bx_ref, lam_ref):
    x = x_ref[...]
    gate = gate_ref[...]
    cw =[cw_ref[k:k + 1, :] for k in range(4)]
    xr = _conv_fwd(x, cw) + cb_ref[...]
    r = _sigmoid(_nn(xr, wa_ref[...]) + ba_ref[...])
    i = _sigmoid(_nn(xr, wx_ref[...]) + bx_ref[...])
    sp = _softplus(-lam_ref[...])
    log_a = -LRU_C * r * sp
    a = jnp.exp(log_a)
    mult = jnp.sqrt(_neg_expm1(2.0 * log_a))
    gi = i * xr
    h = _scan_fwd(a, mult * gi)
    gl, tg = _gelu(gate)
    return dict(x=x, gate=gate, cw=cw, xr=xr, r=r, i=i, sp=sp, a=a, mult=mult, gi=gi, h=h, gl=gl, tg=tg)


def lru_fwd(u, cw, cb, wa, ba, wx, bx, lam, *, seq, name):
    t = u.shape[0]

    def body(x_ref, gate_ref, cw_ref, cb_ref, wa_ref, ba_ref, wx_ref, bx_ref, lam_ref, y_ref):
        f = _lru_math(x_ref, gate_ref, cw_ref, cb_ref, wa_ref, ba_ref, wx_ref, bx_ref, lam_ref)
        y_ref[...] = f["gl"] * f["h"]

    return pl.pallas_call(
        body, name=name, grid=(LRU_W // LRU_HALF, t // seq), in_specs=_lru_in_specs(seq),
        out_specs=pl.BlockSpec((seq, LRU_HALF), lambda j, b: (b, j)), out_shape=SDS((t, LRU_W), F32),
        compiler_params=_params(("parallel", "parallel")))(u, u, cw, cb, wa, ba, wx, bx, lam)


def lru_bwd(u, cw, cb, wa, ba, wx, bx, lam, dy, *, seq, name):
    t = u.shape[0]

    def body(x_ref, gate_ref, cw_ref, cb_ref, wa_ref, ba_ref, wx_ref, bx_ref, lam_ref, dy_ref,
             dx_ref, dgate_ref, dcw_ref, dwa_ref, dwx_ref, dv_ref):
        @pl.when(pl.program_id(1) == 0)
        def _():
            dcw_ref[...] = jnp.zeros_like(dcw_ref)
            dwa_ref[...] = jnp.zeros_like(dwa_ref)
            dwx_ref[...] = jnp.zeros_like(dwx_ref)
            dv_ref[...] = jnp.zeros_like(dv_ref)

        f = _lru_math(x_ref, gate_ref, cw_ref, cb_ref, wa_ref, ba_ref, wx_ref, bx_ref, lam_ref)
        dy = dy_ref[...]
        a, h, xr, r, i, mult, gi, sp = f["a"], f["h"], f["xr"], f["r"], f["i"], f["mult"], f["gi"], f["sp"]
        dgate_ref[...] = dy * h * _gelu_grad(f["gate"], f["tg"])
        lamb = _scan_rev(_shift_up(a, 1, 0.0), dy * f["gl"])
        da = lamb * _shift_down(h, 1)
        dlog_a = da * a - (lamb * gi) * (a * a) / mult
        dgi = lamb * mult
        dra = dlog_a * (-LRU_C * sp) * r * (1.0 - r)
        dia = dgi * xr * i * (1.0 - i)
        dsp = jnp.sum(dlog_a * (-LRU_C * r), axis=0, keepdims=True)
        dlam = -dsp * _sigmoid(-lam_ref[...])
        dxr = dgi * i + _nt(dra, wa_ref[...]) + _nt(dia, wx_ref[...])
        dx, dcw = _conv_bwd(dxr, f["x"], f["cw"])
        dx_ref[...] = dx
        dcw_ref[...] += dcw
        dwa_ref[...] += _tn(xr, dra)
        dwx_ref[...] += _tn(xr, dia)
        rows = [jnp.sum(dxr, axis=0, keepdims=True), jnp.sum(dra, axis=0, keepdims=True),
                jnp.sum(dia, axis=0, keepdims=True), dlam]
        r8 = lax.broadcasted_iota(jnp.int32, (8, LRU_HALF), 0)
        acc = jnp.zeros((8, LRU_HALF), F32)
        for k, row in enumerate(rows):
            acc = jnp.where(r8 == k, row, acc)
        dv_ref[...] += acc

    nhalf = LRU_W // LRU_HALF
    col = pl.BlockSpec((seq, LRU_HALF), lambda j, b: (b, j))
    mat = pl.BlockSpec((None, LRU_HALF, LRU_HALF), lambda j, b: (j, 0, 0))
    return pl.pallas_call(
        body, name=name, grid=(nhalf, t // seq), in_specs=_lru_in_specs(seq) + [col],
        out_specs=[col, col, pl.BlockSpec((4, LRU_HALF), lambda j, b: (0, j)), mat, mat,
                   pl.BlockSpec((8, LRU_HALF), lambda j, b: (0, j))],
        out_shape=[SDS((t, LRU_W), F32), SDS((t, LRU_W), F32), SDS((4, LRU_W), F32),
                   SDS((nhalf, LRU_HALF, LRU_HALF), F32), SDS((nhalf, LRU_HALF, LRU_HALF), F32), SDS((8, LRU_W), F32)],
        compiler_params=_params(("arbitrary", "arbitrary")))(u, u, cw, cb, wa, ba, wx, bx, lam, dy)


NEG = -1e30


def _rel_bucket_map():
    dist = (np.arange(BLOCK_Q)[:, None] - np.arange(BLOCK_Q)[None, :]) % BLOCK_Q
    max_exact = REL_BUCKETS // 2
    large = max_exact + (np.log(np.maximum(dist, 1).astype(np.float32) / max_exact)
                         / math.log(BLOCK_Q / max_exact) * (REL_BUCKETS - max_exact)).astype(np.int32)
    large = np.minimum(large, REL_BUCKETS - 1)
    return np.where(dist < max_exact, dist, large).astype(np.int32)


def relbias_fwd(rel_bias, bmap, *, name):
    def body(rb_ref, bm_ref, o_ref):
        bm = bm_ref[...]
        for h in range(ATT_HEADS):
            acc = jnp.zeros((BLOCK_Q, BLOCK_Q), F32)
            for b in range(REL_BUCKETS):
                acc = jnp.where(bm == b, rb_ref[b, h], acc)
            o_ref[h] = acc

    return pl.pallas_call(
        body, name=name, in_specs=[pl.BlockSpec(memory_space=pltpu.SMEM), pl.BlockSpec(memory_space=pltpu.VMEM)],
        out_specs=pl.BlockSpec(memory_space=pltpu.VMEM), out_shape=SDS((ATT_HEADS, BLOCK_Q, BLOCK_Q), F32))(rel_bias, bmap)


def relbias_bwd(dbias, bmap, *, name):
    def body(db_ref, bm_ref, o_ref):
        bm = bm_ref[...]
        row = lax.broadcasted_iota(jnp.int32, (REL_BUCKETS, 128), 0)
        col = lax.broadcasted_iota(jnp.int32, (REL_BUCKETS, 128), 1)
        acc = jnp.zeros((REL_BUCKETS, 128), F32)
        for h in range(ATT_HEADS):
            d = db_ref[h]
            for b in range(REL_BUCKETS):
                s = jnp.sum(jnp.sum(jnp.where(bm == b, d, 0.0), axis=1, keepdims=True), axis=0, keepdims=True)
                acc = jnp.where((row == b) & (col == h), s, acc)
        o_ref[...] = acc

    return pl.pallas_call(body, name=name, out_shape=SDS((REL_BUCKETS, 128), F32))(dbias, bmap)


def _attn_probs(q_ref, k_ref, v_ref, b_ref, s_ref, n):
    rows = ATT_GROUP * BLOCK_Q
    qs = q_ref[...].reshape(rows, HEAD) * (HEAD ** -0.5)
    prev = pl.multiple_of(jnp.maximum(n - 1, 0) * BLOCK_Q, BLOCK_Q)
    cur = pl.multiple_of(n * BLOCK_Q, BLOCK_Q)
    kp, kc = k_ref[pl.ds(prev, BLOCK_Q), :], k_ref[pl.ds(cur, BLOCK_Q), :]
    vp, vc = v_ref[pl.ds(prev, BLOCK_Q), :], v_ref[pl.ds(cur, BLOCK_Q), :]
    bias = b_ref[...].reshape(rows, BLOCK_Q)
    i = lax.broadcasted_iota(jnp.int32, (rows, BLOCK_Q), 0) & (BLOCK_Q - 1)
    j = lax.broadcasted_iota(jnp.int32, (rows, BLOCK_Q), 1)
    s_p = jnp.where((j > i) & (n > 0), _nt(qs, kp) + bias, NEG)
    s_c = jnp.where(j <= i, _nt(qs, kc) + bias, NEG)
    sink = s_ref[...]
    m = jnp.maximum(jnp.maximum(jnp.max(s_p, axis=-1, keepdims=True), jnp.max(s_c, axis=-1, keepdims=True)), sink)
    e_p, e_c, e_s = jnp.exp(s_p - m), jnp.exp(s_c - m), jnp.exp(sink - m)
    inv = 1.0 / (jnp.sum(e_p, axis=-1, keepdims=True) + jnp.sum(e_c, axis=-1, keepdims=True) + e_s)
    return e_p * inv, e_c * inv, e_s * inv, qs, kp, kc, vp, vc, prev, cur


def _attn_specs(seq):
    qspec = pl.BlockSpec((None, ATT_GROUP, BLOCK_Q, HEAD), lambda g, b, n: (b, g, n, 0))
    kvspec = pl.BlockSpec((None, None, seq, HEAD), lambda g, b, n: (b, g, 0, 0))
    bspec = pl.BlockSpec((ATT_GROUP, BLOCK_Q, BLOCK_Q), lambda g, b, n: (g, 0, 0))
    sspec = pl.BlockSpec((ATT_GROUP * BLOCK_Q, 1), lambda g, b, n: (g, 0))
    return qspec, kvspec, bspec, sspec


def attn_fwd(q, k, v, bias, sink_rows, *, name):
    nb, _, seq, _ = q.shape

    def body(q_ref, k_ref, v_ref, b_ref, s_ref, o_ref):
        p_p, p_c, _, _, _, _, vp, vc, _, _ = _attn_probs(q_ref, k_ref, v_ref, b_ref, s_ref, pl.program_id(2))
        o_ref[...] = (_nn(p_p, vp) + _nn(p_c, vc)).reshape(ATT_GROUP, BLOCK_Q, HEAD)

    qspec, kvspec, bspec, sspec = _attn_specs(seq)
    return pl.pallas_call(
        body, name=name, grid=(KV_HEADS, nb, seq // BLOCK_Q), in_specs=[qspec, kvspec, kvspec, bspec, sspec],
        out_specs=qspec, out_shape=SDS(q.shape, F32),
        compiler_params=_params(("parallel", "parallel", "arbitrary")))(q, k, v, bias, sink_rows)


def attn_bwd(q, k, v, bias, sink_rows, do, *, name):
    nb, _, seq, _ = q.shape

    def body(q_ref, k_ref, v_ref, b_ref, s_ref, do_ref, dq_ref, dk_ref, dv_ref, db_ref, ds_ref):
        b, n = pl.program_id(1), pl.program_id(2)

        @pl.when((b == 0) & (n == 0))
        def _():
            db_ref[...] = jnp.zeros_like(db_ref)
            ds_ref[...] = jnp.zeros_like(ds_ref)

        @pl.when(n == 0)
        def _():
            dk_ref[...] = jnp.zeros_like(dk_ref)
            dv_ref[...] = jnp.zeros_like(dv_ref)

        p_p, p_c, p_s, qs, kp, kc, vp, vc, prev, cur = _attn_probs(q_ref, k_ref, v_ref, b_ref, s_ref, n)
        do = do_ref[...].reshape(ATT_GROUP * BLOCK_Q, HEAD)
        dp_p, dp_c = _nt(do, vp), _nt(do, vc)
        delta = jnp.sum(p_p * dp_p, axis=-1, keepdims=True) + jnp.sum(p_c * dp_c, axis=-1, keepdims=True)
        ds_p, ds_c = p_p * (dp_p - delta), p_c * (dp_c - delta)
        dq_ref[...] = ((_nn(ds_p, kp) + _nn(ds_c, kc)) * (HEAD ** -0.5)).reshape(ATT_GROUP, BLOCK_Q, HEAD)
        dk_ref[pl.ds(prev, BLOCK_Q), :] += _tn(ds_p, qs)
        dk_ref[pl.ds(cur, BLOCK_Q), :] += _tn(ds_c, qs)
        dv_ref[pl.ds(prev, BLOCK_Q), :] += _tn(p_p, do)
        dv_ref[pl.ds(cur, BLOCK_Q), :] += _tn(p_c, do)
        db_ref[...] += (ds_p + ds_c).reshape(ATT_GROUP, BLOCK_Q, BLOCK_Q)
        ds_ref[...] += -p_s * delta

    qspec, kvspec, bspec, sspec = _attn_specs(seq)
    return pl.pallas_call(
        body, name=name, grid=(KV_HEADS, nb, seq // BLOCK_Q), in_specs=[qspec, kvspec, kvspec, bspec, sspec, qspec],
        out_specs=[qspec, kvspec, kvspec, bspec, sspec],
        out_shape=[SDS(q.shape, F32), SDS(k.shape, F32), SDS(v.shape, F32),
                   SDS((ATT_HEADS, BLOCK_Q, BLOCK_Q), F32), SDS((ATT_HEADS * BLOCK_Q, 1), F32)],
        compiler_params=_params(("arbitrary", "arbitrary", "arbitrary")))(q, k, v, bias, sink_rows, do)


def _iota2(shape, axis):
    return lax.broadcasted_iota(jnp.int32, shape, axis)


def _col_to_row(col):
    c = col.shape[0]
    eye = _iota2((c, c), 0) == _iota2((c, c), 1)
    return jnp.sum(jnp.where(eye, jnp.broadcast_to(col, (c, c)), 0.0), axis=0, keepdims=True)


def _row_to_col(row):
    c = row.shape[1]
    eye = _iota2((c, c), 0) == _iota2((c, c), 1)
    return jnp.sum(jnp.where(eye, jnp.broadcast_to(row, (c, c)), 0.0), axis=1, keepdims=True)


def _last_row(col):
    c = col.shape[0]
    return jnp.sum(jnp.where(_iota2((c, 1), 0) == c - 1, col, 0.0), axis=0, keepdims=True)


def _chunk_cumsum(x):
    pos = _iota2(x.shape, 0) & (DN_CHUNK - 1)
    d = 1
    while d < DN_CHUNK:
        x = x + jnp.where(pos >= d, pltpu.roll(x, d, 0), 0.0)
        d *= 2
    return x


def _chunk_rev_cumsum(x):
    n = x.shape[0]
    pos = _iota2(x.shape, 0) & (DN_CHUNK - 1)
    d = 1
    while d < DN_CHUNK:
        x = x + jnp.where(pos < DN_CHUNK - d, pltpu.roll(x, n - d, 0), 0.0)
        d *= 2
    return x


def _tri_inv(low):
    c = low.shape[0]
    eye = (_iota2((c, c), 0) == _iota2((c, c), 1)).astype(F32)
    m = -low
    p = eye + m
    steps = int(math.log2(c)) - 1
    for _ in range(steps):
        m = _nn(m, m, hi=True)
        p = p + _nn(p, m, hi=True)
    return p


_DN_SCALE = (HEAD ** -0.5, 1.0, None)


def _dn_act(c, scale):
    sig = _sigmoid(c)
    a = c * sig
    if scale is None:
        return a, sig, None, None
    r = lax.rsqrt(jnp.sum(a * a, axis=-1, keepdims=True) + EPS)
    return a * r * scale, sig, a * r, r


def _dn_gates(ba_ref, hs_ref):
    beta = _sigmoid(ba_ref[0])
    sp_arg = ba_ref[1] + hs_ref[1]
    a_exp = jnp.exp(hs_ref[0])
    g = -a_exp * _softplus(sp_arg)
    return beta, g, sp_arg, a_exp


def _dn_inputs(pre_ref, cw_ref, ba_ref, hs_ref, act_sc, b_sc, gc_sc, c_sc=None):
    for idx in range(3):
        c = _conv_fwd(pre_ref[idx], [cw_ref[idx, k:k + 1, :] for k in range(4)])
        if c_sc is not None:
            c_sc[idx] = c
        act_sc[idx] = _dn_act(c, _DN_SCALE[idx])[0]
    beta, g, _, _ = _dn_gates(ba_ref, hs_ref)
    b_sc[...] = beta
    gc_sc[...] = _chunk_cumsum(g)


def _dn_chunk_math(q, k, v, b, gcc):
    c = q.shape[0]
    tril = _iota2((c, c), 0) >= _iota2((c, c), 1)
    strict = _iota2((c, c), 0) > _iota2((c, c), 1)
    eg = jnp.exp(gcc)
    kb, vb = k * b, v * b
    kbg = kb * eg
    dm = jnp.exp(jnp.where(tril, jnp.broadcast_to(gcc, (c, c)) - _col_to_row(gcc), NEG))
    kk = _nt(kb, k)
    t = _tri_inv(jnp.where(strict, kk * dm, 0.0))
    glast = _last_row(gcc)
    ekd = jnp.exp(glast - gcc)
    qk = _nt(q, k)
    return dict(tril=tril, strict=strict, eg=eg, kb=kb, vb=vb, kbg=kbg, dm=dm, kk=kk, t=t, glast=glast, ekd=ekd,
                kd=k * ekd, qk=qk, amat=jnp.where(tril, qk * dm, 0.0), qg=q * eg,
                egl=jnp.broadcast_to(jnp.exp(glast), (c, 1)))


DN_UNROLL = 4


def _chunk_loop(nc, chunk):
    u = math.gcd(nc, DN_UNROLL)

    def step(i, carry):
        for j in range(u):
            chunk(i * u + j)
        return carry

    lax.fori_loop(0, nc // u, step, 0)


def _dn_specs(seq):
    s64 = lambda lead: pl.BlockSpec((lead, None, None, seq, HEAD), lambda b, h: (0, b, h, 0, 0))
    s1 = lambda lead: pl.BlockSpec((lead, None, None, seq, 1), lambda b, h: (0, b, h, 0, 0))
    one64 = pl.BlockSpec((None, None, seq, HEAD), lambda b, h: (b, h, 0, 0))
    one1 = pl.BlockSpec((None, None, seq, 1), lambda b, h: (b, h, 0, 0))
    cw = pl.BlockSpec((None, 3, 4, HEAD), lambda b, h: (h, 0, 0, 0))
    hs = pl.BlockSpec((None, 2, 1, 1), lambda b, h: (h, 0, 0, 0))
    return s64, s1, one64, one1, cw, hs


def dn_prep(pre, cw, ba, hs, *, name):
    _, nb, nh, seq, _ = pre.shape
    nc = seq // DN_CHUNK

    def body(pre_ref, cw_ref, ba_ref, hs_ref, loc_ref, egl_ref, act_sc, b_sc, gc_sc):
        _dn_inputs(pre_ref, cw_ref, ba_ref, hs_ref, act_sc, b_sc, gc_sc)

        def chunk(c):
            rows = pl.ds(pl.multiple_of(c * DN_CHUNK, DN_CHUNK), DN_CHUNK)
            m = _dn_chunk_math(act_sc[0, rows, :], act_sc[1, rows, :], act_sc[2, rows, :], b_sc[rows, :], gc_sc[rows, :])
            loc_ref[0, rows, :] = m["qg"]
            loc_ref[1, rows, :] = m["kd"]
            loc_ref[2, rows, :] = _nn(m["t"], m["vb"])
            loc_ref[3, rows, :] = _nn(m["t"], m["kbg"])
            loc_ref[4, rows, :] = m["amat"]
            egl_ref[rows, :] = m["egl"]

        _chunk_loop(nc, chunk)

    s64, s1, one64, one1, cwspec, hsspec = _dn_specs(seq)
    return pl.pallas_call(
        body, name=name, grid=(nb, nh), in_specs=[s64(3), cwspec, s1(2), hsspec], out_specs=[s64(5), one1],
        out_shape=[SDS((5, nb, nh, seq, HEAD), F32), SDS((nb, nh, seq, 1), F32)],
        scratch_shapes=[pltpu.VMEM((3, seq, HEAD), F32)] + [pltpu.VMEM((seq, 1), F32)] * 2,
        compiler_params=_params(("parallel", "parallel")))(pre, cw, ba, hs)


def _gated_norm(o, z, gn):
    r = lax.rsqrt(jnp.mean(o * o, axis=-1, keepdims=True) + EPS)
    sig = _sigmoid(z)
    return o * r, sig, r


def dn_scan(loc, egl, z, gn, *, name):
    _, nb, nh, seq, _ = loc.shape
    nc = seq // DN_CHUNK

    def body(loc_ref, egl_ref, z_ref, gn_ref, y_ref, o_ref, vn_ref, st_ref):
        gn = gn_ref[...]

        def step(c, state):
            rows = pl.ds(pl.multiple_of(c * DN_CHUNK, DN_CHUNK), DN_CHUNK)
            st_ref[rows, :] = state
            vn = loc_ref[2, rows, :] - _nn(loc_ref[3, rows, :], state)
            o = _nn(loc_ref[0, rows, :], state) + _nn(loc_ref[4, rows, :], vn)
            vn_ref[rows, :] = vn
            o_ref[rows, :] = o
            zz = z_ref[rows, :]
            on, sig, _ = _gated_norm(o, zz, gn)
            y_ref[rows, :] = on * gn * (zz * sig)
            return state * egl_ref[rows, :] + _tn(loc_ref[1, rows, :], vn)

        lax.fori_loop(0, nc, step, jnp.zeros((HEAD, HEAD), F32))

    s64, s1, one64, one1, cwspec, hsspec = _dn_specs(seq)
    out = SDS((nb, nh, seq, HEAD), F32)
    return pl.pallas_call(
        body, name=name, grid=(nb, nh), in_specs=[s64(5), one1, one64, _whole((1, HEAD))],
        out_specs=[one64] * 4, out_shape=[out] * 4,
        compiler_params=_params(("parallel", "parallel")))(loc, egl, z, gn)


def dn_scan_bwd(loc, egl, z, gn, o, vn, states, dy, *, name):
    _, nb, nh, seq, _ = loc.shape
    nc = seq // DN_CHUNK

    def body(loc_ref, egl_ref, z_ref, gn_ref, o_ref, vn_ref, st_ref, dy_ref, dloc_ref, degl_ref, dz_ref, dgn_ref):
        @pl.when((pl.program_id(0) == 0) & (pl.program_id(1) == 0))
        def _():
            dgn_ref[...] = jnp.zeros_like(dgn_ref)

        gn = gn_ref[...]
        tril = _iota2((DN_CHUNK, DN_CHUNK), 0) >= _iota2((DN_CHUNK, DN_CHUNK), 1)

        def step(i, carry):
            ds, dgn = carry
            rows = pl.ds(pl.multiple_of((nc - 1 - i) * DN_CHUNK, DN_CHUNK), DN_CHUNK)
            dy, zz, oo = dy_ref[rows, :], z_ref[rows, :], o_ref[rows, :]
            on, sig, r = _gated_norm(oo, zz, gn)
            sz = zz * sig
            dz_ref[rows, :] = dy * on * gn * (sig * (1.0 + zz * (1.0 - sig)))
            dgn = dgn + jnp.sum(dy * on * sz, axis=0, keepdims=True)
            don = dy * gn * sz
            do = r * (don - on * jnp.mean(don * on, axis=-1, keepdims=True))
            state, vnew = st_ref[rows, :], vn_ref[rows, :]
            qg, kd, w, amat = loc_ref[0, rows, :], loc_ref[1, rows, :], loc_ref[3, rows, :], loc_ref[4, rows, :]
            dvn = _tn(amat, do) + _nn(kd, ds)
            dloc_ref[0, rows, :] = _nt(do, state)
            dloc_ref[1, rows, :] = _nt(vnew, ds)
            dloc_ref[2, rows, :] = dvn
            dloc_ref[3, rows, :] = -_nt(dvn, state)
            dloc_ref[4, rows, :] = jnp.where(tril, _nt(do, vnew), 0.0)
            degl = jnp.sum(jnp.sum(state * ds, axis=1, keepdims=True), axis=0, keepdims=True)
            degl_ref[rows, :] = jnp.broadcast_to(degl, (DN_CHUNK, 1))
            return ds * egl_ref[rows, :] + _tn(qg, do) - _tn(w, dvn), dgn

        _, dgn = lax.fori_loop(0, nc, step, (jnp.zeros((HEAD, HEAD), F32), jnp.zeros((1, HEAD), F32)))
        dgn_ref[...] += dgn

    s64, s1, one64, one1, cwspec, hsspec = _dn_specs(seq)
    return pl.pallas_call(
        body, name=name, grid=(nb, nh),
        in_specs=[s64(5), one1, one64, _whole((1, HEAD)), one64, one64, one64, one64],
        out_specs=[s64(5), one1, one64, _whole((1, HEAD))],
        out_shape=[SDS((5, nb, nh, seq, HEAD), F32), SDS((nb, nh, seq, 1), F32), SDS((nb, nh, seq, HEAD), F32),
                   SDS((1, HEAD), F32)],
        compiler_params=_params(("arbitrary", "arbitrary")))(loc, egl, z, gn, o, vn, states, dy)


def dn_prep_bwd(pre, cw, ba, hs, dloc, degl, *, name):
    _, nb, nh, seq, _ = pre.shape
    nc = seq // DN_CHUNK

    def body(pre_ref, cw_ref, ba_ref, hs_ref, dloc_ref, degl_ref, dpre_ref, dba_ref, dcw_ref, dhs_ref,
             act_sc, b_sc, gc_sc, c_sc):
        @pl.when(pl.program_id(1) == 0)
        def _():
            dcw_ref[...] = jnp.zeros_like(dcw_ref)
            dhs_ref[...] = jnp.zeros_like(dhs_ref)

        _dn_inputs(pre_ref, cw_ref, ba_ref, hs_ref, act_sc, b_sc, gc_sc, c_sc)

        def chunk(c):
            rows = pl.ds(pl.multiple_of(c * DN_CHUNK, DN_CHUNK), DN_CHUNK)
            q, k, v, b, gcc = act_sc[0, rows, :], act_sc[1, rows, :], act_sc[2, rows, :], b_sc[rows, :], gc_sc[rows, :]
            m = _dn_chunk_math(q, k, v, b, gcc)
            dqg, dkd, du, dw, da = (dloc_ref[x, rows, :] for x in range(5))
            t, dm, eg = m["t"], m["dm"], m["eg"]
            dt = _nt(du, m["vb"]) + _nt(dw, m["kbg"])
            dvb, dkbg = _tn(t, du), _tn(t, dw)
            dl = jnp.where(m["strict"], -_tn(t, _nt(dt, t, hi=True), hi=True), 0.0)
            dkk = dl * dm
            dqk = da * dm
            dd = dl * m["kk"] + da * m["qk"]
            dkb = _nn(dkk, k) + dkbg * eg
            dq = _nn(dqk, k) + dqg * eg
            dk = _tn(dkk, m["kb"]) + _tn(dqk, q) + dkd * m["ekd"] + dkb * b
            db = jnp.sum(dkb * k, axis=-1, keepdims=True) + jnp.sum(dvb * v, axis=-1, keepdims=True)
            mx = jnp.where(m["tril"], dd * dm, 0.0)
            tk = jnp.sum(dkd * m["kd"], axis=-1, keepdims=True)
            dgc = (jnp.sum(mx, axis=-1, keepdims=True) - _row_to_col(jnp.sum(mx, axis=0, keepdims=True))
                   + jnp.sum(dqg * m["qg"], axis=-1, keepdims=True) + jnp.sum(dkbg * m["kbg"], axis=-1, keepdims=True) - tk)
            dglast = jnp.sum(tk, axis=0, keepdims=True) + _last_row(degl_ref[rows, :]) * jnp.exp(m["glast"])
            act_sc[0, rows, :] = dq
            act_sc[1, rows, :] = dk
            act_sc[2, rows, :] = dvb * b
            b_sc[rows, :] = db
            gc_sc[rows, :] = dgc + jnp.where(_iota2((DN_CHUNK, 1), 0) == DN_CHUNK - 1, dglast, 0.0)

        _chunk_loop(nc, chunk)

        beta, g, sp_arg, a_exp = _dn_gates(ba_ref, hs_ref)
        dg = _chunk_rev_cumsum(gc_sc[...])
        dal = dg * (-a_exp) * _sigmoid(sp_arg)
        dba_ref[0] = b_sc[...] * beta * (1.0 - beta)
        dba_ref[1] = dal
        dhs_ref[0] += jnp.sum(dg * g, axis=0, keepdims=True)
        dhs_ref[1] += jnp.sum(dal, axis=0, keepdims=True)
        for idx in range(3):
            c = c_sc[idx]
            _, sig, hat, r = _dn_act(c, _DN_SCALE[idx])
            da_ = act_sc[idx]
            if _DN_SCALE[idx] is not None:
                da_ = da_ * _DN_SCALE[idx]
                da_ = r * (da_ - hat * jnp.sum(da_ * hat, axis=-1, keepdims=True))
            dx, dcw = _conv_bwd(da_ * (sig * (1.0 + c * (1.0 - sig))), pre_ref[idx],
                                [cw_ref[idx, k:k + 1, :] for k in range(4)])
            dpre_ref[idx] = dx
            dcw_ref[idx] += dcw

    s64, s1, one64, one1, cwspec, hsspec = _dn_specs(seq)
    swap = lambda spec: pl.BlockSpec(spec.block_shape, lambda h, b, _f=spec.index_map: _f(b, h))
    return pl.pallas_call(
        body, name=name, grid=(nh, nb),
        in_specs=[swap(s64(3)), swap(cwspec), swap(s1(2)), swap(hsspec), swap(s64(5)), swap(one1)],
        out_specs=[swap(s64(3)), swap(s1(2)), swap(cwspec), swap(hsspec)],
        out_shape=[SDS((3, nb, nh, seq, HEAD), F32), SDS((2, nb, nh, seq, 1), F32), SDS((nh, 3, 4, HEAD), F32),
                   SDS((nh, 2, 1, 1), F32)],
        scratch_shapes=[pltpu.VMEM((3, seq, HEAD), F32)] + [pltpu.VMEM((seq, 1), F32)] * 2 + [pltpu.VMEM((3, seq, HEAD), F32)],
        compiler_params=_params(("arbitrary", "arbitrary")))(pre, cw, ba, hs, dloc, degl)


COL_Q, COL_K, COL_V = 512 // 128, 1024 // 128, 1152 // 128
COL_DNQ, COL_DNK, COL_DNV, COL_DNZ, COL_BA = 1280 // 128, 1536 // 128, 1792 // 128, 2048 // 128, 2304 // 128


def _lane_a(shape):
    return _iota2(shape, 1) < HEAD


def _bd(x):
    la = _lane_a(x.shape)
    return jnp.concatenate([jnp.where(la, x, 0.0), jnp.where(la, 0.0, x)], axis=0)


def _fold(m):
    return m[:HEAD] + m[HEAD:]


def _bd_mask():
    return (_iota2((2 * HEAD, 2 * HEAD), 0) < HEAD) == (_iota2((2 * HEAD, 2 * HEAD), 1) < HEAD)


def _pk_nn(x, y, hi=False):
    return _nn(x, _bd(y), hi)


def _pk_nt(u, v, hi=False):
    return _nt(u, _bd(v), hi)


def _pk_tn(x, y, hi=False):
    return _fold(jnp.where(_bd_mask(), _tn(x, y, hi), 0.0))


def _half_sum(x):
    la = _lane_a(x.shape)
    return jnp.where(la, jnp.sum(jnp.where(la, x, 0.0), axis=-1, keepdims=True),
                     jnp.sum(jnp.where(la, 0.0, x), axis=-1, keepdims=True))


def _lane_col(x, idx):
    return jnp.sum(jnp.where(_iota2(x.shape, 1) == idx, x, 0.0), axis=-1, keepdims=True)


def _row0(x):
    return jnp.max(x, axis=0, keepdims=True)


def _dup_kv(x, g):
    la = _lane_a(x.shape)
    rolled = pltpu.roll(x, HEAD, 1)
    return jnp.where(la, x, rolled) if g == 0 else jnp.where(la, rolled, x)


def _stack_heads(ref, g):
    la = _lane_a((BLOCK_Q, 2 * HEAD))
    parts = []
    for hh in range(ATT_GROUP):
        pair = ref[:, pl.ds(2 * HEAD * (2 * g + hh // 2), 2 * HEAD)]
        parts.append(jnp.where(la if hh % 2 == 0 else ~la, pair, 0.0))
    return jnp.concatenate(parts, axis=0)


def _unstack_heads(stack, ref, g):
    la = _lane_a((BLOCK_Q, 2 * HEAD))
    for j in range(2):
        top = stack[2 * j * BLOCK_Q:(2 * j + 1) * BLOCK_Q]
        bot = stack[(2 * j + 1) * BLOCK_Q:(2 * j + 2) * BLOCK_Q]
        ref[:, pl.ds(2 * HEAD * (2 * g + j), 2 * HEAD)] = jnp.where(la, top, bot)


def _swa_probs(q_ref, k_ref, v_ref, b_ref, s_ref, n, g):
    rows = ATT_GROUP * BLOCK_Q
    prev = pl.multiple_of(jnp.maximum(n - 1, 0) * BLOCK_Q, BLOCK_Q)
    cur = pl.multiple_of(n * BLOCK_Q, BLOCK_Q)
    kp, kc = _dup_kv(k_ref[pl.ds(prev, BLOCK_Q), :], g), _dup_kv(k_ref[pl.ds(cur, BLOCK_Q), :], g)
    vp, vc = _dup_kv(v_ref[pl.ds(prev, BLOCK_Q), :], g), _dup_kv(v_ref[pl.ds(cur, BLOCK_Q), :], g)
    qs = _stack_heads(q_ref, g) * (HEAD ** -0.5)
    bias = b_ref[pl.ds(ATT_GROUP * g, ATT_GROUP)].reshape(rows, BLOCK_Q)
    i = _iota2((rows, BLOCK_Q), 0) & (BLOCK_Q - 1)
    j = _iota2((rows, BLOCK_Q), 1)
    s_p = jnp.where((j > i) & (n > 0), _nt(qs, kp) + bias, NEG)
    s_c = jnp.where(j <= i, _nt(qs, kc) + bias, NEG)
    sink = s_ref[pl.ds(rows * g, rows), :]
    m = jnp.maximum(jnp.maximum(jnp.max(s_p, axis=-1, keepdims=True), jnp.max(s_c, axis=-1, keepdims=True)), sink)
    e_p, e_c, e_s = jnp.exp(s_p - m), jnp.exp(s_c - m), jnp.exp(sink - m)
    inv = 1.0 / (jnp.sum(e_p, axis=-1, keepdims=True) + jnp.sum(e_c, axis=-1, keepdims=True) + e_s)
    return e_p * inv, e_c * inv, e_s * inv, qs, kp, kc, vp, vc, prev, cur


def _swa_specs(seq):
    nblk = seq // BLOCK_Q
    qspec = pl.BlockSpec((BLOCK_Q, ATT_W), lambda b, n: (b * nblk + n, COL_Q * 128 // ATT_W))
    kspec = pl.BlockSpec((seq, 2 * HEAD), lambda b, n: (b, COL_K))
    vspec = pl.BlockSpec((seq, 2 * HEAD), lambda b, n: (b, COL_V))
    ospec = pl.BlockSpec((BLOCK_Q, ATT_W), lambda b, n: (b * nblk + n, 0))
    kvout = pl.BlockSpec((seq, 2 * HEAD), lambda b, n: (b, 0))
    return qspec, kspec, vspec, ospec, kvout, _whole((ATT_HEADS, BLOCK_Q, BLOCK_Q)), _whole((ATT_HEADS * BLOCK_Q, 1))


def swa_fwd(u, bias, sink_rows, *, seq, name):
    t = u.shape[0]

    def body(q_ref, k_ref, v_ref, b_ref, s_ref, o_ref):
        for g in range(KV_HEADS):
            p_p, p_c, _, _, _, _, vp, vc, _, _ = _swa_probs(q_ref, k_ref, v_ref, b_ref, s_ref, pl.program_id(1), g)
            _unstack_heads(_nn(p_p, vp) + _nn(p_c, vc), o_ref, g)

    qspec, kspec, vspec, ospec, kvout, bspec, sspec = _swa_specs(seq)
    return pl.pallas_call(
        body, name=name, grid=(t // seq, seq // BLOCK_Q), in_specs=[qspec, kspec, vspec, bspec, sspec], out_specs=ospec,
        out_shape=SDS((t, ATT_W), F32), compiler_params=_params(("parallel", "arbitrary")))(u, u, u, bias, sink_rows)


def swa_bwd(u, bias, sink_rows, do, *, seq, name):
    t = u.shape[0]

    def body(q_ref, k_ref, v_ref, b_ref, s_ref, do_ref, dq_ref, dk_ref, dv_ref, db_ref, ds_ref):
        b, n = pl.program_id(0), pl.program_id(1)

        @pl.when((b == 0) & (n == 0))
        def _():
            db_ref[...] = jnp.zeros_like(db_ref)
            ds_ref[...] = jnp.zeros_like(ds_ref)

        @pl.when(n == 0)
        def _():
            dk_ref[...] = jnp.zeros_like(dk_ref)
            dv_ref[...] = jnp.zeros_like(dv_ref)

        la = _lane_a((BLOCK_Q, 2 * HEAD))
        for g in range(KV_HEADS):
            p_p, p_c, p_s, qs, kp, kc, vp, vc, prev, cur = _swa_probs(q_ref, k_ref, v_ref, b_ref, s_ref, n, g)
            do = _stack_heads(do_ref, g)
            dp_p, dp_c = _nt(do, vp), _nt(do, vc)
            delta = jnp.sum(p_p * dp_p, axis=-1, keepdims=True) + jnp.sum(p_c * dp_c, axis=-1, keepdims=True)
            ds_p, ds_c = p_p * (dp_p - delta), p_c * (dp_c - delta)
            _unstack_heads((_nn(ds_p, kp) + _nn(ds_c, kc)) * (HEAD ** -0.5), dq_ref, g)
            mine = la if g == 0 else ~la

            def to_head(x):
                return jnp.where(mine, x + pltpu.roll(x, HEAD, 1), 0.0)

            dk_ref[pl.ds(prev, BLOCK_Q), :] += to_head(_tn(ds_p, qs))
            dk_ref[pl.ds(cur, BLOCK_Q), :] += to_head(_tn(ds_c, qs))
            dv_ref[pl.ds(prev, BLOCK_Q), :] += to_head(_tn(p_p, do))
            dv_ref[pl.ds(cur, BLOCK_Q), :] += to_head(_tn(p_c, do))
            db_ref[pl.ds(ATT_GROUP * g, ATT_GROUP)] += (ds_p + ds_c).reshape(ATT_GROUP, BLOCK_Q, BLOCK_Q)
            rows = ATT_GROUP * BLOCK_Q
            ds_ref[pl.ds(rows * g, rows), :] += -p_s * delta

    qspec, kspec, vspec, ospec, kvout, bspec, sspec = _swa_specs(seq)
    return pl.pallas_call(
        body, name=name, grid=(t // seq, seq // BLOCK_Q), in_specs=[qspec, kspec, vspec, bspec, sspec, ospec],
        out_specs=[ospec, kvout, kvout, bspec, sspec],
        out_shape=[SDS((t, ATT_W), F32), SDS((t, 2 * HEAD), F32), SDS((t, 2 * HEAD), F32),
                   SDS((ATT_HEADS, BLOCK_Q, BLOCK_Q), F32), SDS((ATT_HEADS * BLOCK_Q, 1), F32)],
        compiler_params=_params(("arbitrary", "arbitrary")))(u, u, u, bias, sink_rows, do)


def _gdn_gates(ba_ref, alog_ref, dt_ref, hp):
    blk = ba_ref[...]
    beta_blk = _sigmoid(blk)
    sp_arg = blk + dt_ref[...]
    a_exp = jnp.exp(alog_ref[...])
    g_blk = -a_exp * _softplus(sp_arg)
    la = _lane_a(blk.shape)
    ha = 2 * hp
    beta = jnp.where(la, _lane_col(beta_blk, ha), _lane_col(beta_blk, ha + 1))
    g = jnp.where(la, _lane_col(g_blk, DN_HEADS + ha), _lane_col(g_blk, DN_HEADS + ha + 1))
    return beta, g, beta_blk, sp_arg, a_exp, g_blk


def _gdn_act(c, scale):
    sig = _sigmoid(c)
    a = c * sig
    if scale is None:
        return a, sig, None, None
    r = lax.rsqrt(_half_sum(a * a) + EPS)
    return a * r * scale, sig, a * r, r


def _gdn_inputs(pre_refs, cw_refs, ba_ref, alog_ref, dt_ref, hp, act_sc, b_sc, gc_sc, c_sc=None):
    for idx in range(3):
        c = _conv_fwd(pre_refs[idx][...], [cw_refs[idx][k:k + 1, :] for k in range(4)])
        if c_sc is not None:
            c_sc[idx] = c
        act_sc[idx] = _gdn_act(c, _DN_SCALE[idx])[0]
    beta, g = _gdn_gates(ba_ref, alog_ref, dt_ref, hp)[:2]
    b_sc[...] = beta
    gc_sc[...] = _chunk_cumsum(g)


def _gdn_chunk(q, k, v, b, gcc):
    shape = q.shape
    row, lm = _iota2(shape, 0), _iota2(shape, 1) & (HEAD - 1)
    tril, strict, eye = row >= lm, row > lm, row == lm
    eg = jnp.exp(gcc)
    kb, vb = k * b, v * b
    kbg = kb * eg
    grow = jnp.sum(jnp.where(eye, gcc, 0.0), axis=0, keepdims=True)
    dm = jnp.exp(jnp.where(tril, gcc - grow, NEG))
    kk = _pk_nt(kb, k)
    glast = jnp.sum(jnp.where(row == DN_CHUNK - 1, gcc, 0.0), axis=0, keepdims=True)
    ekd = jnp.exp(glast - gcc)
    qk = _pk_nt(q, k)
    return dict(q=q, k=k, v=v, b=b, tril=tril, strict=strict, eye=eye, row=row, eg=eg, kb=kb, vb=vb, kbg=kbg, dm=dm, kk=kk,
                low=jnp.where(strict, kk * dm, 0.0), glast=glast, ekd=ekd, kd=k * ekd, qk=qk,
                amat=jnp.where(tril, qk * dm, 0.0), qg=q * eg, egl=jnp.broadcast_to(jnp.exp(glast), shape))


def _tri_inv_many(chunks):
    ms = [-m["low"] for m in chunks]
    ts = [m["eye"].astype(F32) + x for m, x in zip(chunks, ms)]
    for _ in range(int(math.log2(HEAD)) - 1):
        ms = [_pk_nn(x, x, hi=True) for x in ms]
        ts = [t + _pk_nn(t, x, hi=True) for t, x in zip(ts, ms)]
    return ts


def _gdn_chunk_loop(nc, act_sc, b_sc, gc_sc, finish):
    u = math.gcd(nc, DN_UNROLL)

    def step(i, carry):
        rows = [pl.ds(pl.multiple_of((i * u + j) * DN_CHUNK, DN_CHUNK), DN_CHUNK) for j in range(u)]
        chunks = [_gdn_chunk(act_sc[0, r, :], act_sc[1, r, :], act_sc[2, r, :], b_sc[r, :], gc_sc[r, :]) for r in rows]
        pending = [finish(r, m, t) for r, m, t in zip(rows, chunks, _tri_inv_many(chunks))]
        pending = [g for g in pending if g is not None]
        while pending:
            for g in list(pending):
                if next(g, StopIteration) is StopIteration:
                    pending.remove(g)
        return carry

    lax.fori_loop(0, nc // u, step, 0)


def _gdn_in_specs(seq):
    u_at = lambda col: pl.BlockSpec((seq, 2 * HEAD), lambda b, hp, _c=col: (b, _c + hp))
    cw_at = lambda col: pl.BlockSpec((4, 2 * HEAD), lambda b, hp, _c=col: (0, _c + hp))
    row = pl.BlockSpec((1, 2 * HEAD), lambda b, hp: (0, 0))
    ba = pl.BlockSpec((seq, 2 * HEAD), lambda b, hp: (b, COL_BA))
    return [u_at(COL_DNQ), u_at(COL_DNK), u_at(COL_DNV), ba, cw_at(0), cw_at(2), cw_at(4), row, row]


def _pair(seq, lead=None):
    if lead is None:
        return pl.BlockSpec((seq, 2 * HEAD), lambda b, hp: (b, hp))
    return pl.BlockSpec((lead, seq, 2 * HEAD), lambda b, hp: (0, b, hp))


def _swap(spec):
    return pl.BlockSpec(spec.block_shape, lambda hp, b, _f=spec.index_map: _f(b, hp))


def gdn_prep(u, cw, alog_row, dt_row, *, seq, name):
    t = u.shape[0]
    nc = seq // DN_CHUNK

    def body(q_ref, k_ref, v_ref, ba_ref, cq_ref, ck_ref, cv_ref, alog_ref, dt_ref, loc_ref, egl_ref, act_sc, b_sc, gc_sc):
        _gdn_inputs((q_ref, k_ref, v_ref), (cq_ref, ck_ref, cv_ref), ba_ref, alog_ref, dt_ref, pl.program_id(1),
                    act_sc, b_sc, gc_sc)

        def finish(rows, m, t):
            loc_ref[0, rows, :] = m["qg"]
            loc_ref[1, rows, :] = m["kd"]
            loc_ref[2, rows, :] = _pk_nn(t, m["vb"])
            loc_ref[3, rows, :] = _pk_nn(t, m["kbg"])
            loc_ref[4, rows, :] = m["amat"]
            egl_ref[rows, :] = m["egl"]

        _gdn_chunk_loop(nc, act_sc, b_sc, gc_sc, finish)

    return pl.pallas_call(
        body, name=name, grid=(t // seq, DN_HEADS // 2), in_specs=_gdn_in_specs(seq), out_specs=[_pair(seq, 5), _pair(seq)],
        out_shape=[SDS((5, t, DN_HEADS * HEAD), F32), SDS((t, DN_HEADS * HEAD), F32)],
        scratch_shapes=[pltpu.VMEM((3, seq, 2 * HEAD), F32)] + [pltpu.VMEM((seq, 2 * HEAD), F32)] * 2,
        compiler_params=_params(("parallel", "parallel")))(u, u, u, u, cw, cw, cw, alog_row, dt_row)


def _gated_norm2(o, z, gn):
    r = lax.rsqrt(_half_sum(o * o) * (1.0 / HEAD) + EPS)
    return o * r, _sigmoid(z), r


def gdn_scan(loc, egl, u, gn, *, seq, name):
    t = u.shape[0]
    nc = seq // DN_CHUNK

    def body(loc_ref, egl_ref, z_ref, gn_ref, y_ref, o_ref, vn_ref, st_ref):
        gn = gn_ref[...]
        bdm = _bd_mask()

        def step(c, state):
            rows = pl.ds(pl.multiple_of(c * DN_CHUNK, DN_CHUNK), DN_CHUNK)
            st_ref[rows, :] = _fold(state)
            vn = loc_ref[2, rows, :] - _nn(loc_ref[3, rows, :], state)
            o = _nn(loc_ref[0, rows, :], state) + _pk_nn(loc_ref[4, rows, :], vn)
            vn_ref[rows, :] = vn
            o_ref[rows, :] = o
            zz = z_ref[rows, :]
            on, sig, _ = _gated_norm2(o, zz, gn)
            y_ref[rows, :] = on * gn * (zz * sig)
            return state * _row0(egl_ref[rows, :]) + jnp.where(bdm, _tn(loc_ref[1, rows, :], vn), 0.0)

        lax.fori_loop(0, nc, step, jnp.zeros((2 * HEAD, 2 * HEAD), F32))

    zspec = pl.BlockSpec((seq, 2 * HEAD), lambda b, hp: (b, COL_DNZ + hp))
    out = SDS((t, DN_HEADS * HEAD), F32)
    return pl.pallas_call(
        body, name=name, grid=(t // seq, DN_HEADS // 2), in_specs=[_pair(seq, 5), _pair(seq), zspec, _whole((1, 2 * HEAD))],
        out_specs=[_pair(seq)] * 4, out_shape=[out] * 4,
        compiler_params=_params(("parallel", "parallel")))(loc, egl, u, gn)


def gdn_scan_bwd(loc, egl, u, gn, o, vn, states, dy, *, seq, name):
    t = u.shape[0]
    nc = seq // DN_CHUNK

    def body(loc_ref, egl_ref, z_ref, gn_ref, o_ref, vn_ref, st_ref, dy_ref, dloc_ref, degl_ref, dz_ref, dgn_ref):
        @pl.when((pl.program_id(0) == 0) & (pl.program_id(1) == 0))
        def _():
            dgn_ref[...] = jnp.zeros_like(dgn_ref)

        gn = gn_ref[...]
        bdm = _bd_mask()
        shape = (DN_CHUNK, 2 * HEAD)
        tril = _iota2(shape, 0) >= (_iota2(shape, 1) & (HEAD - 1))

        def step(i, carry):
            ds, dgn = carry
            rows = pl.ds(pl.multiple_of((nc - 1 - i) * DN_CHUNK, DN_CHUNK), DN_CHUNK)
            dy, zz, oo = dy_ref[rows, :], z_ref[rows, :], o_ref[rows, :]
            on, sig, r = _gated_norm2(oo, zz, gn)
            sz = zz * sig
            dz_ref[rows, :] = dy * on * gn * (sig * (1.0 + zz * (1.0 - sig)))
            dgn = dgn + jnp.sum(dy * on * sz, axis=0, keepdims=True)
            don = dy * gn * sz
            do = r * (don - on * _half_sum(don * on) * (1.0 / HEAD))
            state, vnew = _bd(st_ref[rows, :]), vn_ref[rows, :]
            qg, kd, w, amat = loc_ref[0, rows, :], loc_ref[1, rows, :], loc_ref[3, rows, :], loc_ref[4, rows, :]
            dvn = _pk_tn(amat, do) + _nn(kd, ds)
            dloc_ref[0, rows, :] = _nt(do, state)
            dloc_ref[1, rows, :] = _nt(vnew, ds)
            dloc_ref[2, rows, :] = dvn
            dloc_ref[3, rows, :] = -_nt(dvn, state)
            dloc_ref[4, rows, :] = jnp.where(tril, _pk_nt(do, vnew), 0.0)
            degl = _half_sum(jnp.sum(state * ds, axis=0, keepdims=True))
            degl_ref[rows, :] = jnp.broadcast_to(degl, shape)
            grow = jnp.where(bdm, _tn(qg, do) - _tn(w, dvn), 0.0)
            return ds * _row0(egl_ref[rows, :]) + grow, dgn

        _, dgn = lax.fori_loop(0, nc, step, (jnp.zeros((2 * HEAD, 2 * HEAD), F32), jnp.zeros((1, 2 * HEAD), F32)))
        dgn_ref[...] += dgn

    zspec = pl.BlockSpec((seq, 2 * HEAD), lambda b, hp: (b, COL_DNZ + hp))
    one = _pair(seq)
    out = SDS((t, DN_HEADS * HEAD), F32)
    return pl.pallas_call(
        body, name=name, grid=(t // seq, DN_HEADS // 2),
        in_specs=[_pair(seq, 5), one, zspec, _whole((1, 2 * HEAD)), one, one, one, one],
        out_specs=[_pair(seq, 5), one, one, _whole((1, 2 * HEAD))],
        out_shape=[SDS((5, t, DN_HEADS * HEAD), F32), out, out, SDS((1, 2 * HEAD), F32)],
        compiler_params=_params(("arbitrary", "arbitrary")))(loc, egl, u, gn, o, vn, states, dy)


def gdn_prep_bwd(u, cw, alog_row, dt_row, dloc, degl, *, seq, name):
    t = u.shape[0]
    nc = seq // DN_CHUNK

    def body(q_ref, k_ref, v_ref, ba_ref, cq_ref, ck_ref, cv_ref, alog_ref, dt_ref, dloc_ref, degl_ref,
             dqkv_ref, dba_ref, dcw_ref, dhs_ref, act_sc, b_sc, gc_sc, c_sc):
        hp = pl.program_id(0)

        @pl.when(pl.program_id(1) == 0)
        def _():
            dcw_ref[...] = jnp.zeros_like(dcw_ref)
            dhs_ref[...] = jnp.zeros_like(dhs_ref)

        pre_refs, cw_refs = (q_ref, k_ref, v_ref), (cq_ref, ck_ref, cv_ref)
        _gdn_inputs(pre_refs, cw_refs, ba_ref, alog_ref, dt_ref, hp, act_sc, b_sc, gc_sc, c_sc)

        def finish(rows, m, tt):
            q, k, v, b = m["q"], m["k"], m["v"], m["b"]
            dqg, dkd, du, dw, da = (dloc_ref[x, rows, :] for x in range(5))
            dm, eg = m["dm"], m["eg"]
            dt = _pk_nt(du, m["vb"]) + _pk_nt(dw, m["kbg"])
            dvb, dkbg = _pk_tn(tt, du), _pk_tn(tt, dw)
            yield
            dtt = _pk_nt(dt, tt, hi=True)
            yield
            dl = jnp.where(m["strict"], -_pk_tn(tt, dtt, hi=True), 0.0)
            yield
            dkk = dl * dm
            dqk = da * dm
            dd = dl * m["kk"] + da * m["qk"]
            dkb = _pk_nn(dkk, k) + dkbg * eg
            dq = _pk_nn(dqk, k) + dqg * eg
            yield
            dk = _pk_tn(dkk, m["kb"]) + _pk_tn(dqk, q) + dkd * m["ekd"] + dkb * b
            db = _half_sum(dkb * k + dvb * v)
            yield
            mx = jnp.where(m["tril"], dd * dm, 0.0)
            tk = _half_sum(dkd * m["kd"])
            colsum = jnp.where(m["eye"], jnp.broadcast_to(jnp.sum(mx, axis=0, keepdims=True), mx.shape), 0.0)
            dgc = _half_sum(mx) - _half_sum(colsum) + _half_sum(dqg * m["qg"] + dkbg * m["kbg"]) - tk
            dglast = jnp.sum(tk, axis=0, keepdims=True) + _row0(degl_ref[rows, :]) * jnp.exp(m["glast"])
            act_sc[0, rows, :] = dq
            act_sc[1, rows, :] = dk
            act_sc[2, rows, :] = dvb * b
            b_sc[rows, :] = db
            gc_sc[rows, :] = dgc + jnp.where(m["row"] == DN_CHUNK - 1, dglast, 0.0)

        _gdn_chunk_loop(nc, act_sc, b_sc, gc_sc, finish)

        beta, g, beta_blk, sp_arg, a_exp, g_blk = _gdn_gates(ba_ref, alog_ref, dt_ref, hp)
        dg = _chunk_rev_cumsum(gc_sc[...])
        lane = _iota2(beta_blk.shape, 1)
        ha = 2 * hp
        db = b_sc[...]
        at = lambda idx, x_a, x_b: (jnp.where(lane == idx, _lane_col(x_a, 0), 0.0)
                                    + jnp.where(lane == idx + 1, _lane_col(x_b, HEAD), 0.0))
        dg_blk = at(DN_HEADS + ha, dg, dg)
        dal = dg_blk * (-a_exp) * _sigmoid(sp_arg)
        dba_ref[...] = at(ha, db, db) * beta_blk * (1.0 - beta_blk) + dal
        dhs_ref[0:1, :] += jnp.sum(dg_blk * g_blk, axis=0, keepdims=True)
        dhs_ref[1:2, :] += jnp.sum(dal, axis=0, keepdims=True)
        for idx in range(3):
            c = c_sc[idx]
            _, sig, hat, r = _gdn_act(c, _DN_SCALE[idx])
            da_ = act_sc[idx]
            if _DN_SCALE[idx] is not None:
                da_ = da_ * _DN_SCALE[idx]
                da_ = r * (da_ - hat * _half_sum(da_ * hat))
            dx, dcw = _conv_bwd(da_ * (sig * (1.0 + c * (1.0 - sig))), pre_refs[idx][...],
                                [cw_refs[idx][k:k + 1, :] for k in range(4)])
            dqkv_ref[idx] = dx
            dcw_ref[idx] += dcw

    pair = DN_HEADS // 2
    in_specs = [_swap(s) for s in _gdn_in_specs(seq)] + [_swap(_pair(seq, 5)), _swap(_pair(seq))]
    return pl.pallas_call(
        body, name=name, grid=(pair, t // seq), in_specs=in_specs,
        out_specs=[_swap(_pair(seq, 3)), pl.BlockSpec((None, seq, 2 * HEAD), lambda hp, b: (hp, b, 0)),
                   pl.BlockSpec((3, 4, 2 * HEAD), lambda hp, b: (0, 0, hp)),
                   pl.BlockSpec((None, 2, 2 * HEAD), lambda hp, b: (hp, 0, 0))],
        out_shape=[SDS((3, t, DN_HEADS * HEAD), F32), SDS((pair, t, 2 * HEAD), F32), SDS((3, 4, DN_HEADS * HEAD), F32),
                   SDS((pair, 2, 2 * HEAD), F32)],
        scratch_shapes=[pltpu.VMEM((3, seq, 2 * HEAD), F32)] + [pltpu.VMEM((seq, 2 * HEAD), F32)] * 2
        + [pltpu.VMEM((3, seq, 2 * HEAD), F32)],
        compiler_params=_params(("arbitrary", "arbitrary")))(u, u, u, u, cw, cw, cw, alog_row, dt_row, dloc, degl)


def mix_out(y_lru, o, y_dn, w_out, h, *, name, tm=512):
    t, d = h.shape
    tm = min(tm, t)

    def body(a_ref, b_ref, c_ref, w_ref, h_ref, o_ref, y_ref):
        y_ref[:, 0:LRU_W] = a_ref[...].astype(BF16)
        y_ref[:, LRU_W:LRU_W + ATT_W] = b_ref[...].astype(BF16)
        y_ref[:, LRU_W + ATT_W:] = c_ref[...].astype(BF16)
        o_ref[...] = h_ref[...] + _nn(y_ref[...], w_ref[...])

    rows = lambda width: pl.BlockSpec((tm, width), lambda i: (i, 0))
    return pl.pallas_call(
        body, name=name, grid=(t // tm,), in_specs=[rows(LRU_W), rows(ATT_W), rows(LRU_W), _whole((d, d)), rows(d)],
        out_specs=[rows(d), rows(d)], out_shape=[SDS((t, d), F32), SDS((t, d), BF16)],
        compiler_params=_params(("parallel",)))(y_lru, o, y_dn, w_out, h)


def mix_out_bwd(dout, w_out, *, name, tm=512):
    t, d = dout.shape
    tm = min(tm, t)

    def body(d_ref, w_ref, a_ref, b_ref, c_ref):
        dy = _nt(d_ref[...], w_ref[...])
        a_ref[...] = dy[:, 0:LRU_W]
        b_ref[...] = dy[:, LRU_W:LRU_W + ATT_W]
        c_ref[...] = dy[:, LRU_W + ATT_W:]

    rows = lambda width: pl.BlockSpec((tm, width), lambda i: (i, 0))
    return pl.pallas_call(
        body, name=name, grid=(t // tm,), in_specs=[rows(d), _whole((d, d))], out_specs=[rows(LRU_W), rows(ATT_W), rows(LRU_W)],
        out_shape=[SDS((t, LRU_W), F32), SDS((t, ATT_W), F32), SDS((t, LRU_W), F32)],
        compiler_params=_params(("parallel",)))(dout, w_out)


def mix_in_bwd(h, gain, dout, w_in, dx, dgate, dq, dk, dv, dqkv, dz, dba, *, name, tm=512):
    t, d = h.shape
    tm = min(tm, t)

    def body(h_ref, g_ref, do_ref, w_ref, dx_ref, dgate_ref, dq_ref, dk_ref, dv_ref, dqkv_ref, dz_ref, dba_ref,
             dh_ref, dg_ref, du_ref):
        @pl.when(pl.program_id(0) == 0)
        def _():
            dg_ref[...] = jnp.zeros_like(dg_ref)

        off = 0
        for piece in (dx_ref[...], dgate_ref[...], dq_ref[...], dk_ref[...], dv_ref[...], dqkv_ref[0], dqkv_ref[1],
                      dqkv_ref[2], dz_ref[...], dba_ref[0] + dba_ref[1]):
            du_ref[:, off:off + piece.shape[1]] = piece.astype(BF16)
            off += piece.shape[1]
        du_ref[:, off:] = jnp.zeros((tm, D_IN_PAD - off), BF16)
        g = g_ref[...]
        _, xh, r = _rms_fwd(h_ref[...], g)
        dh, dg = _rms_bwd(_nt(du_ref[...], w_ref[...]), xh, r, g)
        dh_ref[...] = do_ref[...] + dh
        dg_ref[...] += dg

    rows = lambda width: pl.BlockSpec((tm, width), lambda i: (i, 0))
    return pl.pallas_call(
        body, name=name, grid=(t // tm,),
        in_specs=[rows(d), _whole((1, d)), rows(d), _whole((d, D_IN_PAD)), rows(LRU_W), rows(LRU_W), rows(ATT_W),
                  rows(2 * HEAD), rows(2 * HEAD), pl.BlockSpec((3, tm, DN_HEADS * HEAD), lambda i: (0, i, 0)),
                  rows(DN_HEADS * HEAD), pl.BlockSpec((2, tm, 2 * HEAD), lambda i: (0, i, 0))],
        out_specs=[rows(d), _whole((1, d)), rows(D_IN_PAD)],
        out_shape=[SDS((t, d), F32), SDS((1, d), F32), SDS((t, D_IN_PAD), BF16)],
        compiler_params=_params(("arbitrary",)))(h, gain, dout, w_in, dx, dgate, dq, dk, dv, dqkv, dz, dba)


def _block_diag(w):
    out = jnp.zeros((LRU_W, LRU_W), w.dtype)
    for h in range(LRU_W // HEAD):
        out = lax.dynamic_update_slice(out, w[h], (h * HEAD, h * HEAD))
    return out


def _diag_blocks(w):
    per = LRU_HALF // HEAD
    return jnp.stack([w[h // per, (h % per) * HEAD:(h % per + 1) * HEAD, (h % per) * HEAD:(h % per + 1) * HEAD]
                      for h in range(LRU_W // HEAD)])


def layer_params(w, wl, l, bias):
    row = lambda a: a[l].reshape(1, -1)
    return dict(
        ffn1_norm=row(w["ffn1_norm"]), ffn1=(wl["ffn1_w_gate"], wl["ffn1_w_up"], wl["ffn1_w_down"]),
        mix_norm=row(w["mix_norm"]) + wl["tie1"][0:1, 0:1], w_in=wl["w_in"],
        lru=(wl["lru_conv_w"], row(w["lru_conv_b"]), _block_diag(w["lru_w_a"][l]), row(w["lru_b_a"]),
             _block_diag(w["lru_w_x"][l]), row(w["lru_b_x"]), row(w["lru_lambda"])),
        bias=bias, sink_rows=jnp.repeat(w["attn_sinks"][l], BLOCK_Q).reshape(ATT_HEADS * BLOCK_Q, 1),
        dn_cw=wl["dn_conv_w"], dn_alog=_ba_row(w["dn_a_log"][l]), dn_dt=_ba_row(w["dn_dt_bias"][l]),
        dn_norm=jnp.tile(row(w["dn_norm"]), (1, 2)), w_out=wl["w_out"],
        ffn2_norm=row(w["ffn2_norm"]), ffn2=(wl["ffn2_w_gate"], wl["ffn2_w_up"], wl["ffn2_w_down"]),
        ple_norm=row(w["ple_norm"]), ple_w_gate=wl["ple_w_gate"], ple_w_proj=wl["ple_w_proj"])


def _ba_row(per_head):
    return jnp.pad(per_head, (DN_HEADS, 2 * HEAD - 2 * DN_HEADS)).reshape(1, 2 * HEAD)


def mixer_fwd(h, p, nb, seq, tag):
    u, n = norm_matmul(h, p["mix_norm"], p["w_in"], name=f"mix_in_{tag}")
    y_lru = lru_fwd(u, *p["lru"], seq=seq, name=f"lru_fwd_{tag}")
    o = swa_fwd(u, p["bias"], p["sink_rows"], seq=seq, name=f"swa_fwd_{tag}")
    loc, egl = gdn_prep(u, p["dn_cw"], p["dn_alog"], p["dn_dt"], seq=seq, name=f"gdn_prep_{tag}")
    y_dn, o_raw, vn, st = gdn_scan(loc, egl, u, p["dn_norm"], seq=seq, name=f"gdn_scan_{tag}")
    out, ycat = mix_out(y_lru, o, y_dn, p["w_out"], h, name=f"mix_out_{tag}")
    return out, dict(h=h, u=u, n=n, loc=loc, egl=egl, o_raw=o_raw, vn=vn, st=st, ycat=ycat)


def mixer_bwd(dout, s, p, nb, seq, tag):
    u = s["u"]
    dy_lru, do, dy_dn = mix_out_bwd(dout, p["w_out"], name=f"mix_out_dx_{tag}")
    g = {"w_out": matmul(s["ycat"], dout, ta=True, name=f"mix_out_dw_{tag}")}
    dx, dgate, dcw, dwa, dwx, dvec = lru_bwd(u, *p["lru"], dy_lru, seq=seq, name=f"lru_bwd_{tag}")
    g.update(lru_conv_w=dcw, lru_conv_b=dvec[0], lru_w_a=_diag_blocks(dwa), lru_b_a=dvec[1], lru_w_x=_diag_blocks(dwx),
             lru_b_x=dvec[2], lru_lambda=dvec[3])
    dq, dk, dv, dbias, dsink = swa_bwd(u, p["bias"], p["sink_rows"], do, seq=seq, name=f"swa_bwd_{tag}")
    g.update(attn_sinks=dsink.reshape(ATT_HEADS, BLOCK_Q).sum(axis=1), bias=dbias)
    dloc, degl, dz, dgn = gdn_scan_bwd(s["loc"], s["egl"], u, p["dn_norm"], s["o_raw"], s["vn"], s["st"], dy_dn, seq=seq,
                                       name=f"gdn_scan_bwd_{tag}")
    dqkv, dba, dcw3, dhs = gdn_prep_bwd(u, p["dn_cw"], p["dn_alog"], p["dn_dt"], dloc, degl, seq=seq,
                                        name=f"gdn_prep_bwd_{tag}")
    dhs = dhs.sum(axis=0)[:, DN_HEADS:2 * DN_HEADS]
    g.update(dn_conv_w=dcw3.transpose(1, 0, 2).reshape(4, 3 * DN_HEADS * HEAD), dn_a_log=dhs[0], dn_dt_bias=dhs[1],
             dn_norm=dgn[0, :HEAD] + dgn[0, HEAD:])
    dh, dgain, du = mix_in_bwd(s["h"], p["mix_norm"], dout, p["w_in"], dx, dgate, dq, dk, dv, dqkv, dz, dba,
                               name=f"mix_in_bwd_{tag}")
    g["w_in"] = matmul(s["n"], du, ta=True, name=f"mix_in_dw_{tag}")
    g["mix_norm"] = dgain[0]
    return dh, g


SHARDED = ("ffn1_w_gate", "ffn1_w_up", "ffn1_w_down", "w_in", "w_out", "ffn2_w_gate", "ffn2_w_up", "ffn2_w_down",
           "ple_w_gate", "ple_w_proj")
PER_LAYER_SMALL = ("ffn1_norm", "mix_norm", "lru_conv_w", "lru_conv_b", "lru_w_a", "lru_b_a", "lru_w_x", "lru_b_x",
                   "lru_lambda", "attn_sinks", "dn_conv_w", "dn_a_log", "dn_dt_bias", "dn_norm", "ffn2_norm", "ple_norm")


GRAD_PARTS = (("ple_w_gate", "ple_w_proj", "ffn2_w_gate", "ffn2_w_up", "ffn2_w_down"),
              ("ffn1_w_gate", "ffn1_w_up", "ffn1_w_down", "w_in", "w_out"))
WEIGHT_PARTS = (("ffn1_w_gate", "ffn1_w_up", "ffn1_w_down"),
                ("w_in", "w_out", "ffn2_w_gate", "ffn2_w_up", "ffn2_w_down", "ple_w_gate", "ple_w_proj", "lru_conv_w",
                 "dn_conv_w"))


def _col_shards(a):
    r, c = a.shape
    return a.reshape(r, N_CHIP, c // N_CHIP).transpose(1, 0, 2)


def local_step(x, p, target, w, layer_weights, layer_grads, bmap, nb, seq):
    bias = relbias_fwd(w["rel_bias"], bmap, name="relbias_fwd")
    h, saved = x, []
    for l in range(N_LAYER):
        wl = layer_weights(l, 0, h)
        s = dict(h0=h)
        h = ffn_fwd(h, w["ffn1_norm"][l].reshape(1, -1) + wl["tie0"][0:1, 0:1], wl["ffn1_w_gate"], wl["ffn1_w_up"],
                    wl["ffn1_w_down"], name=f"ffn1_fwd_{l}")
        wl.update(layer_weights(l, 1, h))
        pr = layer_params(w, wl, l, bias)
        h, s["mix"] = mixer_fwd(h, pr, nb, seq, l)
        s["h2"] = h
        h = ffn_fwd(h, pr["ffn2_norm"], *pr["ffn2"], name=f"ffn2_fwd_{l}")
        s["h3"] = h
        h = ple_fwd(h, pr["ple_norm"], pr["ple_w_gate"], p[l], pr["ple_w_proj"], name=f"ple_fwd_{l}")
        saved.append((pr, s))
    dh, dgf, loss = loss_head(h, w["final_norm"].reshape(1, -1), target, name="loss_head")

    per_layer, dbias, token = [None] * N_LAYER, None, None
    for l in reversed(range(N_LAYER)):
        pr, s = saved[l]
        g = {}
        dout = dh
        ple_norm = pr["ple_norm"] if token is None else pr["ple_norm"] + token[0:1, 0:1]
        dh, n, dga, dpp, dg = ple_bwd(s["h3"], ple_norm, pr["ple_w_gate"], p[l], pr["ple_w_proj"], dout, name=f"ple_bwd_{l}")
        g["ple_norm"] = dg[0]
        g["ple_w_gate"] = matmul(n, dga, ta=True, name=f"ple_dwg_{l}").reshape(N_CHIP, -1, D_MODEL)
        g["ple_w_proj"] = _col_shards(matmul(p[l], dpp, ta=True, name=f"ple_dwp_{l}"))
        for part, (nm, hin) in enumerate((("ffn2", s["h2"]), ("ffn1", s["h0"]))):
            if nm == "ffn1":
                lru = list(pr["lru"])
                lru[1] = lru[1] + token[0:1, 0:1]
                dh, gm = mixer_bwd(dh, s["mix"], dict(pr, lru=tuple(lru)), nb, seq, l)
                dbias = gm.pop("bias") if dbias is None else dbias + gm.pop("bias")
                gm["w_in"] = _col_shards(gm["w_in"][:, :D_IN])
                gm["w_out"] = gm["w_out"].reshape(N_CHIP, -1, D_MODEL)
                g.update(gm)
            dout = dh
            dh, n, da, db, sact, dg = ffn_bwd_act(hin, pr[nm + "_norm"], dout, *pr[nm], name=f"{nm}_bwd_act_{l}")
            g[nm + "_norm"] = dg[0]
            g[nm + "_w_gate"], g[nm + "_w_up"], g[nm + "_w_down"] = ffn_bwd_w(n, da, db, sact, dout, name=f"{nm}_bwd_w_{l}")
            token = layer_grads(l, part, {k: g.pop(k) for k in GRAD_PARTS[part]}, dh)
        per_layer[l] = g
    grads = {k: jnp.stack([per_layer[l][k] for l in range(N_LAYER)]) for k in PER_LAYER_SMALL}
    grads["rel_bias"] = relbias_bwd(dbias, bmap, name="relbias_bwd")[:, :ATT_HEADS]
    grads["final_norm"] = dgf[0]
    return loss, dh, grads


HBM_SPEC = pl.BlockSpec(memory_space=pltpu.HBM)


def _place():
    x, y, c = lax.axis_index("x"), lax.axis_index("y"), lax.axis_index("c")
    chips = [(1 - x, y), (x, 1 - y), (1 - x, 1 - y)]
    return x, y, c, 2 * x + y, (x, y, 1 - c), chips, [2 * cx + cy for cx, cy in chips]


def _remote(src, dst, send_sem, recv_sem, to):
    return pltpu.make_async_remote_copy(src_ref=src, dst_ref=dst, send_sem=send_sem, recv_sem=recv_sem, device_id=to,
                                        device_id_type=MESH)


def place_shard(w, chip_arr, dtype, *, name):
    nl, r, c = w.shape
    tr = next(cand for cand in (256, 128, 64, 32, 16, 8, r) if r % cand == 0)

    def body(chip_ref, w_ref, o_ref):
        o_ref[...] = w_ref[...].astype(dtype)

    return pl.pallas_call(
        body, name=name,
        grid_spec=pltpu.PrefetchScalarGridSpec(
            num_scalar_prefetch=1, grid=(nl, r // tr),
            in_specs=[pl.BlockSpec((None, tr, c), lambda l, i, chip: (l, i, 0))],
            out_specs=pl.BlockSpec((None, None, tr, c), lambda l, i, chip: (chip[0], l, i, 0))),
        out_shape=SDS((N_CHIP, nl, r, c), dtype), compiler_params=_params(("parallel", "parallel")))(chip_arr, w)


def allgather_shards(shards, *, name):
    n = len(shards)

    def body(*refs):
        outs = refs[n:2 * n]
        send, recv, fsend, frecv = refs[2 * n:]
        x, y, c, me, sib, chips, cids = _place()
        first, passed = [], []
        for k in range(n):
            for j, chip in enumerate(chips):
                mine = outs[k].at[me, c]
                first.append(_remote(mine, mine, send.at[3 * k + j], recv.at[3 * k + j], (*chip, c)))
                first[-1].start()
        for k in range(n):
            for j in range(3):
                piece = outs[k].at[cids[j], c]
                _remote(piece, piece, send.at[3 * k + j], recv.at[3 * k + j], sib).wait_recv()
                passed.append(_remote(piece, piece, fsend.at[3 * k + j], frecv.at[3 * k + j], sib))
                passed[-1].start()
        for k in range(n):
            for j in range(3):
                piece = outs[k].at[cids[j], 1 - c]
                _remote(piece, piece, fsend.at[3 * k + j], frecv.at[3 * k + j], sib).wait_recv()
        for cp in first + passed:
            cp.wait_send()

    return pl.pallas_call(
        body, name=name, in_specs=[HBM_SPEC] * n, out_specs=[HBM_SPEC] * n,
        out_shape=[SDS(s.shape, s.dtype) for s in shards], input_output_aliases={k: k for k in range(n)},
        scratch_shapes=[pltpu.SemaphoreType.DMA((3 * n,))] * 4)(*shards)


def exchange_layers(gs, *, name):
    n = len(gs)

    def body(*refs):
        ins, outs, (send, recv) = refs[:n], refs[n:2 * n], refs[2 * n:]
        x, y, c, me, sib, chips, cids = _place()
        cps = [_remote(ins[k].at[1 - c], outs[k], send.at[k], recv.at[k], sib) for k in range(n)]
        for cp in cps:
            cp.start()
        for cp in cps:
            cp.wait()

    return pl.pallas_call(
        body, name=name, in_specs=[HBM_SPEC] * n, out_specs=[HBM_SPEC] * n,
        out_shape=[SDS(g.shape[1:], g.dtype) for g in gs], scratch_shapes=[pltpu.SemaphoreType.DMA((n,))] * 2)(*gs)


def reduce_to_shards(ss, *, name):
    n = len(ss)

    def body(*refs):
        ins, outs, (send, recv) = refs[:n], refs[n:2 * n], refs[2 * n:]
        x, y, c, me, sib, chips, cids = _place()
        cps = []
        for k in range(n):
            for j, chip in enumerate(chips):
                cps.append(_remote(ins[k].at[cids[j]], outs[k].at[j], send.at[3 * k + j], recv.at[3 * k + j], (*chip, c)))
                cps[-1].start()
        for k in range(n):
            for j in range(3):
                slot = outs[k].at[j]
                _remote(slot, slot, send.at[3 * k + j], recv.at[3 * k + j], sib).wait_recv()
        for cp in cps:
            cp.wait_send()

    return pl.pallas_call(
        body, name=name, in_specs=[HBM_SPEC] * n, out_specs=[HBM_SPEC] * n,
        out_shape=[SDS((N_CHIP - 1,) + s.shape[1:], s.dtype) for s in ss],
        scratch_shapes=[pltpu.SemaphoreType.DMA((3 * n,))] * 2)(*ss)


def share_layers(fs, *, name):
    n = len(fs)

    def body(*refs):
        outs, (send, recv) = refs[n:2 * n], refs[2 * n:]
        x, y, c, me, sib, chips, cids = _place()
        cps = [_remote(outs[k].at[c], outs[k].at[c], send.at[k], recv.at[k], sib) for k in range(n)]
        for cp in cps:
            cp.start()
        for k in range(n):
            theirs = outs[k].at[1 - c]
            _remote(theirs, theirs, send.at[k], recv.at[k], sib).wait_recv()
        for cp in cps:
            cp.wait_send()

    return pl.pallas_call(
        body, name=name, in_specs=[HBM_SPEC] * n, out_specs=[HBM_SPEC] * n, out_shape=[SDS(f.shape, f.dtype) for f in fs],
        input_output_aliases={k: k for k in range(n)}, scratch_shapes=[pltpu.SemaphoreType.DMA((n,))] * 2)(*fs)


N_DEV = 8


def allreduce_small(buf, *, name):
    rows = buf.shape[0]

    def body(in_ref, out_ref, gath, send, recv):
        x, y, c = lax.axis_index("x"), lax.axis_index("y"), lax.axis_index("c")
        mine = 4 * x + 2 * y + c
        gath[mine] = in_ref[...]
        cps = []
        for k in range(1, N_DEV):
            to = (x ^ (k >> 2), y ^ ((k >> 1) & 1), c ^ (k & 1))
            cps.append(_remote(in_ref, gath.at[mine], send.at[k - 1], recv.at[k - 1], to))
            cps[-1].start()
        for k in range(1, N_DEV):
            theirs = gath.at[4 * (x ^ (k >> 2)) + 2 * (y ^ ((k >> 1) & 1)) + (c ^ (k & 1))]
            _remote(theirs, theirs, send.at[k - 1], recv.at[k - 1], (x, y, c)).wait_recv()
        for cp in cps:
            cp.wait_send()
        acc = gath[0]
        for d in range(1, N_DEV):
            acc = acc + gath[d]
        out_ref[...] = acc

    vm = pl.BlockSpec(memory_space=pltpu.VMEM)
    return pl.pallas_call(
        body, name=name, in_specs=[vm], out_specs=vm, out_shape=SDS(buf.shape, F32),
        scratch_shapes=[pltpu.VMEM((N_DEV, rows, 128), F32), pltpu.SemaphoreType.DMA((N_DEV - 1,)),
                        pltpu.SemaphoreType.DMA((N_DEV - 1,))])(buf)


def add_sibling(g, r, c_arr, *, name, tr=256):
    _, m, cdim = g.shape
    assert m % tr == 0

    def body(c_ref, g_ref, r_ref, o_ref):
        o_ref[...] = (g_ref[...] + r_ref[...]).astype(o_ref.dtype)

    return pl.pallas_call(
        body, name=name,
        grid_spec=pltpu.PrefetchScalarGridSpec(
            num_scalar_prefetch=1, grid=(m // tr,),
            in_specs=[pl.BlockSpec((None, tr, cdim), lambda i, c: (c[0], i, 0)), pl.BlockSpec((tr, cdim), lambda i, c: (i, 0))],
            out_specs=pl.BlockSpec((tr, cdim), lambda i, c: (i, 0))),
        out_shape=SDS((m, cdim), BF16), compiler_params=_params(("parallel",)))(c_arr, g, r)


def sum_slots(own, r, place_arr, *, name, tr=256):
    _, m, cdim = r.shape
    tr = next(cand for cand in (tr, 128, 64, 32, 16, 8) if m % cand == 0)

    def body(p_ref, own_ref, r_ref, o_ref):
        o_ref[...] = ((own_ref[...].astype(F32) + r_ref[0].astype(F32)) + r_ref[1].astype(F32)) + r_ref[2].astype(F32)

    return pl.pallas_call(
        body, name=name,
        grid_spec=pltpu.PrefetchScalarGridSpec(
            num_scalar_prefetch=1, grid=(m // tr,),
            in_specs=[pl.BlockSpec((None, tr, cdim), lambda i, p: (p[0], i, 0)),
                      pl.BlockSpec((N_CHIP - 1, tr, cdim), lambda i, p: (0, i, 0))],
            out_specs=pl.BlockSpec((None, tr, cdim), lambda i, p: (p[1], i, 0))),
        out_shape=SDS((N_LAYER, m, cdim), F32), compiler_params=_params(("parallel",)))(place_arr, own, r)


SEM_SPEC = pl.BlockSpec(memory_space=pltpu.SEMAPHORE)
ANY_SPEC = pl.BlockSpec(memory_space=pl.ANY)
DATAFLOW = pltpu.SideEffectType.DATAFLOW_SIDE_EFFECTING


def _in_hbm(a):
    return pltpu.with_memory_space_constraint(a, pltpu.HBM)


def _my_rows(ref_rows, c, mine=True):
    half = ref_rows // 2
    start = (c if mine else 1 - c) * half
    return pl.ds(pl.multiple_of(start, 8), half)


def place_layer_shard(w, layer, chip_arr, dtype, *, name):
    _, r, c = w.shape
    tr = next(cand for cand in (256, 128, 64, 32, 16, 8, r) if r % cand == 0)

    def body(chip_ref, w_ref, o_ref):
        o_ref[...] = w_ref[...].astype(dtype)

    return pl.pallas_call(
        body, name=name,
        grid_spec=pltpu.PrefetchScalarGridSpec(
            num_scalar_prefetch=1, grid=(r // tr,),
            in_specs=[pl.BlockSpec((None, tr, c), lambda i, chip: (layer, i, 0))],
            out_specs=pl.BlockSpec((None, tr, c), lambda i, chip: (chip[0], i, 0))),
        out_shape=SDS((N_CHIP, r, c), dtype), compiler_params=_params(("parallel",)))(chip_arr, w)


def _gather_pieces(refs, n_split, c, me, cids):
    mine, theirs = [], []
    for k, ref in enumerate(refs):
        if k < n_split:
            rows = _my_rows(ref.shape[1], c)
            mine.append(ref.at[me, rows])
            theirs.append([ref.at[cid, rows] for cid in cids])
        else:
            mine.append(ref.at[me])
            theirs.append([ref.at[cid] for cid in cids])
    return mine, theirs


def gather_start(bufs, n_split, after, *, name):
    n = len(bufs)

    def body(*refs):
        ins, send, recv, token = refs[:n], refs[n + 1], refs[n + 2], refs[-1]
        x, y, c, me, sib, chips, cids = _place()
        mine, _ = _gather_pieces(ins, n_split, c, me, cids)
        for k in range(n):
            for j, chip in enumerate(chips):
                _remote(mine[k], mine[k], send.at[3 * k + j], recv.at[3 * k + j], (*chip, c)).start()
        token[...] = jnp.zeros_like(token)

    out = pl.pallas_call(
        body, name=name, in_specs=[HBM_SPEC] * n + [ANY_SPEC],
        out_specs=[SEM_SPEC, SEM_SPEC] + [HBM_SPEC] * n + [pl.BlockSpec(memory_space=pltpu.VMEM)],
        out_shape=[pltpu.SemaphoreType.DMA((3 * n,)), pltpu.SemaphoreType.DMA((3 * n,))]
        + [pltpu.HBM(b.shape, b.dtype) for b in bufs] + [SDS((8, 128), F32)],
        input_output_aliases={k: k + 2 for k in range(n)},
        compiler_params=pltpu.CompilerParams(has_side_effects=DATAFLOW))(*[_in_hbm(b) for b in bufs], after)
    return out[0], out[1], list(out[2:2 + n]), out[-1]


def gather_wait(send, recv, bufs, n_split, after, *, name):
    n = len(bufs)

    def body(*refs):
        ins, send_ref, recv_ref = refs[:n], refs[n], refs[n + 1]
        x, y, c, me, sib, chips, cids = _place()
        mine, theirs = _gather_pieces(ins, n_split, c, me, cids)
        for k in range(n):
            for j in range(3):
                _remote(mine[k], mine[k], send_ref.at[3 * k + j], recv_ref.at[3 * k + j], sib).wait_send()
                _remote(theirs[k][j], theirs[k][j], send_ref.at[3 * k + j], recv_ref.at[3 * k + j], sib).wait_recv()

    return list(pl.pallas_call(
        body, name=name, in_specs=[HBM_SPEC] * n + [SEM_SPEC, SEM_SPEC, ANY_SPEC], out_specs=[HBM_SPEC] * n,
        out_shape=[pltpu.HBM(b.shape, b.dtype) for b in bufs], input_output_aliases={k: k for k in range(n)},
        compiler_params=pltpu.CompilerParams(has_side_effects=DATAFLOW))(*bufs, send, recv, after))


def gather_forward(bufs, *, name):
    n = len(bufs)

    def body(*refs):
        outs, (send, recv) = refs[n:2 * n], refs[2 * n:]
        x, y, c, me, sib, chips, cids = _place()
        cps = []
        for k in range(n):
            for j in range(3):
                piece = outs[k].at[cids[j], _my_rows(outs[k].shape[1], c)]
                cps.append(_remote(piece, piece, send.at[3 * k + j], recv.at[3 * k + j], sib))
                cps[-1].start()
        for k in range(n):
            for j in range(3):
                piece = outs[k].at[cids[j], _my_rows(outs[k].shape[1], c, mine=False)]
                _remote(piece, piece, send.at[3 * k + j], recv.at[3 * k + j], sib).wait_recv()
        for cp in cps:
            cp.wait_send()

    return list(pl.pallas_call(
        body, name=name, in_specs=[HBM_SPEC] * n, out_specs=[HBM_SPEC] * n, out_shape=[SDS(b.shape, b.dtype) for b in bufs],
        input_output_aliases={k: k for k in range(n)}, scratch_shapes=[pltpu.SemaphoreType.DMA((3 * n,))] * 2)(*bufs))


def reduce_exchange(gs, *, name):
    n = len(gs)

    def body(*refs):
        ins, outs, (send, recv) = refs[:n], refs[n:2 * n], refs[2 * n:]
        x, y, c, me, sib, chips, cids = _place()
        cps = [_remote(ins[k].at[pl.ds(0, N_CHIP), _my_rows(ins[k].shape[1], c, mine=False)], outs[k], send.at[k],
                       recv.at[k], sib) for k in range(n)]
        for cp in cps:
            cp.start()
        for cp in cps:
            cp.wait()

    return list(pl.pallas_call(
        body, name=name, in_specs=[HBM_SPEC] * n, out_specs=[HBM_SPEC] * n,
        out_shape=[SDS((N_CHIP, g.shape[1] // 2, g.shape[2]), g.dtype) for g in gs],
        scratch_shapes=[pltpu.SemaphoreType.DMA((n,))] * 2)(*gs))


def _half_tile(half):
    return next(cand for cand in (256, 176, 128, 64, 32, 16) if half % cand == 0)


def reduce_add(g, r, c_arr, *, name):
    _, rows, cdim = g.shape
    half = rows // 2
    tr = _half_tile(half)

    def body(c_ref, g_ref, r_ref, o_ref):
        o_ref[...] = (g_ref[...] + r_ref[...]).astype(o_ref.dtype)

    return pl.pallas_call(
        body, name=name,
        grid_spec=pltpu.PrefetchScalarGridSpec(
            num_scalar_prefetch=1, grid=(N_CHIP, half // tr),
            in_specs=[pl.BlockSpec((None, tr, cdim), lambda j, i, c: (j, c[0] * (half // tr) + i, 0)),
                      pl.BlockSpec((None, tr, cdim), lambda j, i, c: (j, i, 0))],
            out_specs=pl.BlockSpec((None, tr, cdim), lambda j, i, c: (j, i, 0))),
        out_shape=SDS((N_CHIP, half, cdim), BF16), compiler_params=_params(("parallel", "parallel")))(c_arr, g, r)


def reduce_start(ss, *, name):
    n = len(ss)

    def body(*refs):
        ins, lands, send, recv, token = refs[:n], refs[n:2 * n], refs[2 * n], refs[2 * n + 1], refs[-1]
        x, y, c, me, sib, chips, cids = _place()
        for k in range(n):
            for j, chip in enumerate(chips):
                _remote(ins[k].at[cids[j]], lands[k].at[j], send.at[3 * k + j], recv.at[3 * k + j], (*chip, c)).start()
        token[...] = jnp.zeros_like(token)

    lands = [_in_hbm(lax.empty((N_CHIP - 1,) + s.shape[1:], s.dtype)) for s in ss]
    out = pl.pallas_call(
        body, name=name, in_specs=[HBM_SPEC] * (2 * n),
        out_specs=[SEM_SPEC, SEM_SPEC] + [HBM_SPEC] * (2 * n) + [pl.BlockSpec(memory_space=pltpu.VMEM)],
        out_shape=[pltpu.SemaphoreType.DMA((3 * n,)), pltpu.SemaphoreType.DMA((3 * n,))]
        + [pltpu.HBM(b.shape, b.dtype) for b in list(ss) + lands] + [SDS((8, 128), F32)],
        input_output_aliases={k: k + 2 for k in range(2 * n)},
        compiler_params=pltpu.CompilerParams(has_side_effects=DATAFLOW))(*[_in_hbm(s) for s in ss], *lands)
    return out[0], out[1], list(out[2:2 + n]), list(out[2 + n:2 + 2 * n]), out[-1]


def reduce_wait(send, recv, ss, lands, after, *, name):
    n = len(ss)

    def body(*refs):
        ins, land_refs, send_ref, recv_ref = refs[:n], refs[n:2 * n], refs[2 * n], refs[2 * n + 1]
        x, y, c, me, sib, chips, cids = _place()
        for k in range(n):
            for j in range(3):
                _remote(ins[k].at[cids[j]], land_refs[k].at[j], send_ref.at[3 * k + j], recv_ref.at[3 * k + j],
                        sib).wait_send()
                _remote(ins[k].at[cids[j]], land_refs[k].at[j], send_ref.at[3 * k + j], recv_ref.at[3 * k + j],
                        sib).wait_recv()

    out = pl.pallas_call(
        body, name=name, in_specs=[HBM_SPEC] * (2 * n) + [SEM_SPEC, SEM_SPEC, ANY_SPEC], out_specs=[HBM_SPEC] * (2 * n),
        out_shape=[pltpu.HBM(b.shape, b.dtype) for b in list(ss) + list(lands)],
        input_output_aliases={k: k for k in range(2 * n)},
        compiler_params=pltpu.CompilerParams(has_side_effects=DATAFLOW))(*ss, *lands, send, recv, after)
    return list(out[:n]), list(out[n:])


def reduce_sum(own, land, place_arr, layer, acc, *, name):
    _, half, cdim = land.shape
    tr = _half_tile(half)

    def body(p_ref, own_ref, land_ref, *rest):
        o_ref = rest[-1]
        o_ref[...] = ((own_ref[...].astype(F32) + land_ref[0].astype(F32)) + land_ref[1].astype(F32)) + land_ref[2].astype(F32)

    in_specs = [pl.BlockSpec((None, tr, cdim), lambda i, p: (p[0], i, 0)),
                pl.BlockSpec((N_CHIP - 1, tr, cdim), lambda i, p: (0, i, 0))]
    args = [place_arr, own, land]
    if acc is not None:
        in_specs.append(ANY_SPEC)
        args.append(acc)
    return pl.pallas_call(
        body, name=name,
        grid_spec=pltpu.PrefetchScalarGridSpec(
            num_scalar_prefetch=1, grid=(half // tr,), in_specs=in_specs,
            out_specs=pl.BlockSpec((None, tr, cdim), lambda i, p: (layer, p[1] * (half // tr) + i, 0))),
        out_shape=SDS((N_LAYER, 2 * half, cdim), F32), input_output_aliases={} if acc is None else {3: 0},
        compiler_params=_params(("parallel",)))(*args)


def reduce_share(fs, *, name):
    n = len(fs)

    def body(*refs):
        outs, (send, recv) = refs[n:2 * n], refs[2 * n:]
        x, y, c, me, sib, chips, cids = _place()
        cps = []
        for k in range(n):
            piece = outs[k].at[pl.ds(0, N_LAYER), _my_rows(outs[k].shape[1], c)]
            cps.append(_remote(piece, piece, send.at[k], recv.at[k], sib))
            cps[-1].start()
        for k in range(n):
            theirs = outs[k].at[pl.ds(0, N_LAYER), _my_rows(outs[k].shape[1], c, mine=False)]
            _remote(theirs, theirs, send.at[k], recv.at[k], sib).wait_recv()
        for cp in cps:
            cp.wait_send()

    return list(pl.pallas_call(
        body, name=name, in_specs=[HBM_SPEC] * n, out_specs=[HBM_SPEC] * n, out_shape=[SDS(f.shape, f.dtype) for f in fs],
        input_output_aliases={k: k for k in range(n)}, scratch_shapes=[pltpu.SemaphoreType.DMA((n,))] * 2)(*fs))


WEIGHTS = ("ffn1_norm", "ffn1_w_gate", "ffn1_w_up", "ffn1_w_down", "mix_norm", "w_in", "lru_conv_w", "lru_conv_b", "lru_w_a",
           "lru_b_a", "lru_w_x", "lru_b_x", "lru_lambda", "attn_sinks", "rel_bias", "dn_conv_w", "dn_a_log", "dn_dt_bias",
           "dn_norm", "w_out", "ffn2_norm", "ffn2_w_gate", "ffn2_w_up", "ffn2_w_down", "ple_norm", "ple_w_gate",
           "ple_w_proj", "final_norm")
CONV_SHARDED = ("lru_conv_w", "dn_conv_w")
SMALL = tuple(k for k in WEIGHTS if k not in SHARDED)


def _pack(arrs):
    flat = []
    for a in arrs:
        v = a.reshape(-1)
        flat.append(jnp.pad(v, (0, -v.shape[0] % 128)))
    v = jnp.concatenate(flat)
    v = jnp.pad(v, (0, -v.shape[0] % 1024))
    return v.reshape(-1, 128)


def _unpack(buf, shapes):
    v, out, off = buf.reshape(-1), [], 0
    for s in shapes:
        n = int(np.prod(s))
        out.append(v[off:off + n].reshape(s))
        off += n + (-n % 128)
    return out


def _chip_cols(a):
    n, l, r, c = a.shape
    return a.transpose(1, 2, 0, 3).reshape(l, r, n * c)


def _chip_rows(a):
    n, l, r, c = a.shape
    return a.transpose(1, 0, 2, 3).reshape(l, n * r, c)


def kernel(x, p, ffn1_norm, ffn1_w_gate, ffn1_w_up, ffn1_w_down, mix_norm, w_in, lru_conv_w, lru_conv_b, lru_w_a, lru_b_a, lru_w_x, lru_b_x, lru_lambda, attn_sinks, rel_bias, dn_conv_w, dn_a_log, dn_dt_bias, dn_norm, w_out, ffn2_norm, ffn2_w_gate, ffn2_w_up, ffn2_w_down, ple_norm, ple_w_gate, ple_w_proj, final_norm, loss_target, m_ffn1_norm, m_ffn1_w_gate, m_ffn1_w_up, m_ffn1_w_down, m_mix_norm, m_w_in, m_lru_conv_w, m_lru_conv_b, m_lru_w_a, m_lru_b_a, m_lru_w_x, m_lru_b_x, m_lru_lambda, m_attn_sinks, m_rel_bias, m_dn_conv_w, m_dn_a_log, m_dn_dt_bias, m_dn_norm, m_w_out, m_ffn2_norm, m_ffn2_w_gate, m_ffn2_w_up, m_ffn2_w_down, m_ple_norm, m_ple_w_gate, m_ple_w_proj, m_final_norm, v_ffn1_norm, v_ffn1_w_gate, v_ffn1_w_up, v_ffn1_w_down, v_mix_norm, v_w_in, v_lru_conv_w, v_lru_conv_b, v_lru_w_a, v_lru_b_a, v_lru_w_x, v_lru_b_x, v_lru_lambda, v_attn_sinks, v_rel_bias, v_dn_conv_w, v_dn_a_log, v_dn_dt_bias, v_dn_norm, v_w_out, v_ffn2_norm, v_ffn2_w_gate, v_ffn2_w_up, v_ffn2_w_down, v_ple_norm, v_ple_w_gate, v_ple_w_proj, v_final_norm):
    given = dict(locals())
    ws = {k: given[k] for k in WEIGHTS}
    ms = {k: given["m_" + k] for k in WEIGHTS}
    vs = {k: given["v_" + k] for k in WEIGHTS}
    nb, seq, d = x.shape
    t = nb * seq
    cx, cy, cc = lax.axis_index("x"), lax.axis_index("y"), lax.axis_index("c")
    chip = 2 * cx + cy

    chip_arr = chip.astype(jnp.int32).reshape(1)
    c_arr = cc.astype(jnp.int32).reshape(1)
    place_arr = jnp.stack([chip, cc]).astype(jnp.int32)
    groups = [(l, part) for l in range(N_LAYER) for part in range(len(WEIGHT_PARTS))]
    placed = {(l, k): place_layer_shard(ws[k], l, chip_arr, F32 if k in CONV_SHARDED else BF16, name=f"place_{k}_{l}")
              for l in range(N_LAYER) for k in SHARDED + CONV_SHARDED}
    started = {}

    def start_group(i, after):
        l, part = groups[i]
        ks = WEIGHT_PARTS[part]
        n_split = sum(k in SHARDED for k in ks)
        started[i] = (ks, n_split) + gather_start([placed[l, k] for k in ks], n_split, after, name=f"gather_start_{l}_{part}")

    start_group(0, jnp.zeros((8, 128), F32))

    def layer_weights(l, part, h):
        i = groups.index((l, part))
        ks, n_split, send, recv, bufs, _ = started[i]
        bufs = gather_wait(send, recv, bufs, n_split, h, name=f"gather_wait_{l}_{part}")
        tie = jnp.zeros((8, 128), F32)
        for nxt in [j for j in range(i + 1, len(groups)) if j not in started and groups[j][0] == groups[min(i + 1, len(groups) - 1)][0]]:
            start_group(nxt, bufs[0] if nxt == i + 1 else started[nxt - 1][-1])
            tie = started[nxt][-1]
        wl = dict(zip(ks, gather_forward(bufs[:n_split], name=f"gather_forward_{l}_{part}") + bufs[n_split:]))
        for k in ("w_in", "ple_w_proj", "lru_conv_w", "dn_conv_w"):
            if k in wl:
                wl[k] = wl[k].transpose(1, 0, 2).reshape(wl[k].shape[1], -1)
        for k in ("w_out", "ple_w_gate"):
            if k in wl:
                wl[k] = wl[k].reshape(-1, wl[k].shape[-1])
        if "w_in" in wl:
            wl["w_in"] = jnp.pad(wl["w_in"], ((0, 0), (0, D_IN_PAD - D_IN)))
        wl[f"tie{part}"] = tie
        return wl

    pending, finished = [], {k: None for k in SHARDED}

    def finish_reduce(after):
        ks, send, recv, sums, lands, l, part = pending.pop(0)
        sums, lands = reduce_wait(send, recv, sums, lands, after, name=f"reduce_wait_{l}_{part}")
        for k, s, land in zip(ks, sums, lands):
            finished[k] = reduce_sum(s, land, place_arr, l, finished[k], name=f"reduce_sum_{k}_{l}")

    def layer_grads(l, part, g, dh):
        ks = GRAD_PARTS[part]
        gs = [g[k] for k in ks]
        theirs = reduce_exchange(gs, name=f"reduce_exchange_{l}_{part}")
        sums = [reduce_add(a, b, c_arr, name=f"reduce_add_{k}_{l}") for k, a, b in zip(ks, gs, theirs)]
        send, recv, sums, lands, token = reduce_start(sums, name=f"reduce_start_{l}_{part}")
        pending.append((ks, send, recv, sums, lands, l, part))
        while len(pending) > 2:
            finish_reduce(dh)
        return token

    small_w = {k: ws[k] for k in SMALL if k not in CONV_SHARDED}
    bmap = jnp.asarray(_rel_bucket_map())
    loss, gx, grads = local_step(x.reshape(t, d), p.reshape(N_LAYER, t, PLE_DIM), loss_target.reshape(t, d), small_w,
                                 layer_weights, layer_grads, bmap, nb, seq)
    while pending:
        finish_reduce(gx)
    g_out = dict(zip(SHARDED, reduce_share([finished[k] for k in SHARDED], name="reduce_share")))

    small_shapes = [grads[k].shape for k in SMALL]
    g_small = dict(zip(SMALL, _unpack(allreduce_small(_pack([grads[k] for k in SMALL]), name="allreduce_small"), small_shapes)))
    for k in CONV_SHARDED:
        width = ws[k].shape[-1]
        g_small[k] = lax.dynamic_slice_in_dim(g_small[k], chip * width, width, axis=2)
    g_out.update(g_small)

    delta, new_m, new_v = {}, {}, {}
    for k in SHARDED:
        two_d = lambda a: a.reshape(-1, a.shape[-1])
        res = adamw(two_d(ws[k]), two_d(g_out[k]), two_d(ms[k]), two_d(vs[k]), name=f"adamw_{k}")
        delta[k], new_m[k], new_v[k] = (r.reshape(ws[k].shape) for r in res)
    shapes = [ws[k].shape for k in SMALL]
    res = adamw(*[_pack([src[k] for k in SMALL]) for src in (ws, g_out, ms, vs)], name="adamw_small")
    for dst, r in zip((delta, new_m, new_v), res):
        dst.update(zip(SMALL, _unpack(r, shapes)))

    total = lax.psum(loss[0, 0], ("x", "y", "c"))
    return (total, gx.reshape(nb, seq, d), *[g_out[k] for k in WEIGHTS], *[delta[k] for k in WEIGHTS],
            *[new_m[k] for k in WEIGHTS], *[new_v[k] for k in WEIGHTS])
```

```python
import functools
import math

import numpy as np
import jax
import jax.numpy as jnp
from jax import lax
from jax.experimental import pallas as pl
from jax.experimental.pallas import tpu as pltpu

F32 = jnp.float32
BF16 = jnp.bfloat16

EPS = 1e-6
D_MODEL = 1024
D_FF = 2816
N_CHIP = 4
FF_BLK = D_FF // N_CHIP
HEAD = 64
LRU_W = 256
ATT_W = 512
ATT_HEADS = 8
KV_HEADS = 2
ATT_GROUP = 4
BLOCK_Q = 128
DN_HEADS = 4
DN_CHUNK = 64
D_IN = 2312
D_IN_PAD = 2560
PLE_DIM = 256
REL_BUCKETS = 32
LRU_C = 8.0
N_LAYER = 2

ADAM_LR, ADAM_B1, ADAM_B2, ADAM_EPS, ADAM_WD, ADAM_STEP = 0.001, 0.9, 0.999, 1e-08, 0.01, 10

VMEM_LIMIT = 56 << 20
MESH = pl.DeviceIdType.MESH
SDS = jax.ShapeDtypeStruct


def _dot(a, b, ca=1, cb=0, hi=False):
    dims = (((ca,), (cb,)), ((), ()))
    one = lambda u, v: lax.dot_general(u, v, dims, preferred_element_type=F32)
    a_hi, b_hi = a.astype(BF16), b.astype(BF16)
    if not hi:
        return one(a_hi, b_hi)
    a_lo = (a - a_hi.astype(F32)).astype(BF16)
    b_lo = (b - b_hi.astype(F32)).astype(BF16)
    return one(a_hi, b_hi) + (one(a_hi, b_lo) + one(a_lo, b_hi))


def _nn(a, b, hi=False):
    return _dot(a, b, 1, 0, hi)


def _nt(a, b, hi=False):
    return _dot(a, b, 1, 1, hi)


def _tn(a, b, hi=False):
    return _dot(a, b, 0, 0, hi)


def _sigmoid(x):
    return jax.nn.sigmoid(x)


def _softplus(x):
    return jnp.maximum(x, 0.0) + jnp.log1p(jnp.exp(-jnp.abs(x)))


def _neg_expm1(z):
    series = -z * (1.0 + z * (0.5 + z * (1.0 / 6.0 + z * (1.0 / 24.0 + z * (1.0 / 120.0)))))
    return jnp.where(z > -0.05, series, 1.0 - jnp.exp(z))


_GELU_C = math.sqrt(2.0 / math.pi)


def _gelu(x):
    t = jnp.tanh(_GELU_C * (x + 0.044715 * x * x * x))
    return 0.5 * x * (1.0 + t), t


def _gelu_grad(x, t):
    return 0.5 * (1.0 + t) + 0.5 * x * (1.0 - t * t) * _GELU_C * (1.0 + 3.0 * 0.044715 * x * x)


def _rms_fwd(h, g):
    r = lax.rsqrt(jnp.mean(h * h, axis=-1, keepdims=True) + EPS)
    xh = h * r
    return xh * g, xh, r


def _rms_bwd(dn, xh, r, g):
    dxh = dn * g
    dh = r * (dxh - xh * jnp.mean(dxh * xh, axis=-1, keepdims=True))
    return dh, jnp.sum(dn * xh, axis=0, keepdims=True)


def _shift_down(x, d, fill=0.0):
    row = lax.broadcasted_iota(jnp.int32, x.shape, 0)
    return jnp.where(row >= d, pltpu.roll(x, d, 0), fill)


def _shift_up(x, d, fill=0.0):
    n = x.shape[0]
    row = lax.broadcasted_iota(jnp.int32, x.shape, 0)
    return jnp.where(row < n - d, pltpu.roll(x, n - d, 0), fill)


def _conv_fwd(x, w):
    y = x * w[3]
    for k in range(3):
        y = y + _shift_down(x, 3 - k) * w[k]
    return y


def _conv_bwd(dy, x, w):
    dx = dy * w[3]
    rows = [None] * 4
    rows[3] = jnp.sum(dy * x, axis=0, keepdims=True)
    for k in range(3):
        dx = dx + _shift_up(dy, 3 - k) * w[k]
        rows[k] = jnp.sum(dy * _shift_down(x, 3 - k), axis=0, keepdims=True)
    r4 = lax.broadcasted_iota(jnp.int32, (4, x.shape[1]), 0)
    dw = jnp.zeros((4, x.shape[1]), F32)
    for k in range(4):
        dw = jnp.where(r4 == k, rows[k], dw)
    return dx, dw


FFN_SPLIT = 2


def _interleave(gens):
    pending = list(gens)
    while pending:
        for g in list(pending):
            if next(g, StopIteration) is StopIteration:
                pending.remove(g)


def _params(sem=None, vmem=VMEM_LIMIT):
    return pltpu.CompilerParams(dimension_semantics=sem, vmem_limit_bytes=vmem)


def _whole(shape):
    nd = len(shape)
    return pl.BlockSpec(shape, lambda *_: (0,) * nd)


def matmul(a, b, *, name, ta=False, tb=False, residual=None, out_dtype=F32, tm=512, tn=512, tk=512):
    m, k = (a.shape[1], a.shape[0]) if ta else a.shape
    n = b.shape[0] if tb else b.shape[1]
    tm, tn, tk = min(tm, m), min(tn, n), min(tk, k)
    assert m % tm == 0 and n % tn == 0 and k % tk == 0, (m, n, k, tm, tn, tk)
    nk = k // tk

    def body(*refs):
        if residual is None:
            a_ref, b_ref, o_ref, acc = refs
        else:
            a_ref, b_ref, r_ref, o_ref, acc = refs
        kk = pl.program_id(2)

        @pl.when(kk == 0)
        def _():
            acc[...] = jnp.zeros_like(acc)

        acc[...] += _dot(a_ref[...], b_ref[...], 0 if ta else 1, 1 if tb else 0)

        @pl.when(kk == nk - 1)
        def _():
            out = acc[...]
            if residual is not None:
                out = out + r_ref[...]
            o_ref[...] = out.astype(out_dtype)

    a_spec = pl.BlockSpec((tk, tm), lambda i, j, kk: (kk, i)) if ta else pl.BlockSpec((tm, tk), lambda i, j, kk: (i, kk))
    b_spec = pl.BlockSpec((tn, tk), lambda i, j, kk: (j, kk)) if tb else pl.BlockSpec((tk, tn), lambda i, j, kk: (kk, j))
    o_spec = pl.BlockSpec((tm, tn), lambda i, j, kk: (i, j))
    in_specs, args = [a_spec, b_spec], [a, b]
    if residual is not None:
        in_specs.append(o_spec)
        args.append(residual)
    return pl.pallas_call(
        body, name=name, grid=(m // tm, n // tn, nk), in_specs=in_specs, out_specs=o_spec,
        out_shape=SDS((m, n), out_dtype), scratch_shapes=[pltpu.VMEM((tm, tn), F32)],
        compiler_params=_params(("parallel", "parallel", "arbitrary")))(*args)


def norm_matmul(h, gain, w, *, name, tm=512, tn=512):
    t, d = h.shape
    tm = min(tm, t)
    n = w.shape[1]
    assert t % tm == 0 and n % tn == 0

    def body(h_ref, g_ref, w_ref, u_ref, n_ref):
        @pl.when(pl.program_id(1) == 0)
        def _():
            n_ref[...] = _rms_fwd(h_ref[...], g_ref[...])[0].astype(BF16)

        u_ref[...] = _nn(n_ref[...], w_ref[...])

    return pl.pallas_call(
        body, name=name, grid=(t // tm, n // tn),
        in_specs=[pl.BlockSpec((tm, d), lambda i, j: (i, 0)), _whole((1, d)), pl.BlockSpec((d, tn), lambda i, j: (0, j))],
        out_specs=[pl.BlockSpec((tm, tn), lambda i, j: (i, j)), pl.BlockSpec((tm, d), lambda i, j: (i, 0))],
        out_shape=[SDS((t, n), F32), SDS((t, d), BF16)],
        compiler_params=_params(("parallel", "arbitrary")))(h, gain, w)


def rms_bwd(h, gain, dn, dres, *, name, tm=512):
    t, d = h.shape
    tm = min(tm, t)

    def body(h_ref, g_ref, dn_ref, dr_ref, dh_ref, dg_ref):
        @pl.when(pl.program_id(0) == 0)
        def _():
            dg_ref[...] = jnp.zeros_like(dg_ref)

        g = g_ref[...]
        _, xh, r = _rms_fwd(h_ref[...], g)
        dh, dg = _rms_bwd(dn_ref[...], xh, r, g)
        dh_ref[...] = dr_ref[...] + dh
        dg_ref[...] += dg

    row = pl.BlockSpec((tm, d), lambda i: (i, 0))
    return pl.pallas_call(
        body, name=name, grid=(t // tm,), in_specs=[row, _whole((1, d)), row, row],
        out_specs=[row, _whole((1, d))], out_shape=[SDS((t, d), F32), SDS((1, d), F32)],
        compiler_params=_params(("arbitrary",)))(h, gain, dn, dres)


def ffn_fwd(h, gain, wg, wu, wd, *, name, tm=512):
    t, d = h.shape
    tm = min(tm, t)

    def body(h_ref, g_ref, wg_ref, wu_ref, wd_ref, o_ref, n_sc, acc):
        j = pl.program_id(1)

        @pl.when(j == 0)
        def _():
            n_sc[...] = _rms_fwd(h_ref[...], g_ref[...])[0].astype(BF16)
            acc[...] = jnp.zeros_like(acc)

        def part(rows):
            n = n_sc[rows, :]
            a = _nn(n, wg_ref[...])
            b = _nn(n, wu_ref[...])
            yield
            acc[rows, :] += _nn(a * _sigmoid(a) * b, wd_ref[...])

        _interleave([part(pl.ds(k * (tm // FFN_SPLIT), tm // FFN_SPLIT)) for k in range(FFN_SPLIT)])

        @pl.when(j == N_CHIP - 1)
        def _():
            o_ref[...] = h_ref[...] + 0.5 * acc[...]

    row = pl.BlockSpec((tm, d), lambda i, j: (i, 0))
    return pl.pallas_call(
        body, name=name, grid=(t // tm, N_CHIP),
        in_specs=[row, _whole((1, d)),
                  pl.BlockSpec((None, d, FF_BLK), lambda i, j: (j, 0, 0)),
                  pl.BlockSpec((None, d, FF_BLK), lambda i, j: (j, 0, 0)),
                  pl.BlockSpec((None, FF_BLK, d), lambda i, j: (j, 0, 0))],
        out_specs=row, out_shape=SDS((t, d), F32),
        scratch_shapes=[pltpu.VMEM((tm, d), BF16), pltpu.VMEM((tm, d), F32)],
        compiler_params=_params(("parallel", "arbitrary")))(h, gain, wg, wu, wd)


def ffn_bwd_act(h, gain, dout, wg, wu, wd, *, name, tm=512):
    t, d = h.shape
    tm = min(tm, t)

    def body(h_ref, g_ref, do_ref, wg_ref, wu_ref, wd_ref, dh_ref, n_ref, da_ref, db_ref, s_ref, dg_ref, dn_acc):
        i, j = pl.program_id(0), pl.program_id(1)

        @pl.when((i == 0) & (j == 0))
        def _():
            dg_ref[...] = jnp.zeros_like(dg_ref)

        @pl.when(j == 0)
        def _():
            n_ref[...] = _rms_fwd(h_ref[...], g_ref[...])[0].astype(BF16)
            dn_acc[...] = jnp.zeros_like(dn_acc)

        def part(rows):
            n = n_ref[rows, :]
            a = _nn(n, wg_ref[...])
            b = _nn(n, wu_ref[...])
            ds = _nt(0.5 * do_ref[rows, :], wd_ref[...])
            yield
            sig = _sigmoid(a)
            sa = a * sig
            db = ds * sa
            da = ds * b * (sig * (1.0 + a * (1.0 - sig)))
            s_ref[rows, :] = (sa * b).astype(BF16)
            da_ref[rows, :] = da.astype(BF16)
            db_ref[rows, :] = db.astype(BF16)
            yield
            dn_acc[rows, :] += _nt(da, wg_ref[...]) + _nt(db, wu_ref[...])

        _interleave([part(pl.ds(k * (tm // FFN_SPLIT), tm // FFN_SPLIT)) for k in range(FFN_SPLIT)])

        @pl.when(j == N_CHIP - 1)
        def _():
            g = g_ref[...]
            _, xh, r = _rms_fwd(h_ref[...], g)
            dh, dg = _rms_bwd(dn_acc[...], xh, r, g)
            dh_ref[...] = do_ref[...] + dh
            dg_ref[...] += dg

    row = pl.BlockSpec((tm, d), lambda i, j: (i, 0))
    blk = pl.BlockSpec((None, tm, FF_BLK), lambda i, j: (j, i, 0))
    act = SDS((N_CHIP, t, FF_BLK), BF16)
    return pl.pallas_call(
        body, name=name, grid=(t // tm, N_CHIP),
        in_specs=[row, _whole((1, d)), row,
                  pl.BlockSpec((None, d, FF_BLK), lambda i, j: (j, 0, 0)),
                  pl.BlockSpec((None, d, FF_BLK), lambda i, j: (j, 0, 0)),
                  pl.BlockSpec((None, FF_BLK, d), lambda i, j: (j, 0, 0))],
        out_specs=[row, row, blk, blk, blk, _whole((1, d))],
        out_shape=[SDS((t, d), F32), SDS((t, d), BF16), act, act, act, SDS((1, d), F32)],
        scratch_shapes=[pltpu.VMEM((tm, d), F32)],
        compiler_params=_params(("arbitrary", "arbitrary")))(h, gain, dout, wg, wu, wd)


def ffn_bwd_w(n, da, db, s, dout, *, name, tk=512):
    t, d = n.shape
    tk = min(tk, t)

    def body(n_ref, da_ref, db_ref, s_ref, do_ref, dwg_ref, dwu_ref, dwd_ref):
        @pl.when(pl.program_id(1) == 0)
        def _():
            dwg_ref[...] = jnp.zeros_like(dwg_ref)
            dwu_ref[...] = jnp.zeros_like(dwu_ref)
            dwd_ref[...] = jnp.zeros_like(dwd_ref)

        nn = n_ref[...]
        dwg_ref[...] += _tn(nn, da_ref[...])
        dwu_ref[...] += _tn(nn, db_ref[...])
        dwd_ref[...] += _tn(s_ref[...], 0.5 * do_ref[...])

    row = pl.BlockSpec((tk, d), lambda j, kk: (kk, 0))
    blk = pl.BlockSpec((None, tk, FF_BLK), lambda j, kk: (j, kk, 0))
    return pl.pallas_call(
        body, name=name, grid=(N_CHIP, t // tk), in_specs=[row, blk, blk, blk, row],
        out_specs=[pl.BlockSpec((None, d, FF_BLK), lambda j, kk: (j, 0, 0)),
                   pl.BlockSpec((None, d, FF_BLK), lambda j, kk: (j, 0, 0)),
                   pl.BlockSpec((None, FF_BLK, d), lambda j, kk: (j, 0, 0))],
        out_shape=[SDS((N_CHIP, d, FF_BLK), F32), SDS((N_CHIP, d, FF_BLK), F32), SDS((N_CHIP, FF_BLK, d), F32)],
        compiler_params=_params(("parallel", "arbitrary")))(n, da, db, s, dout)


def ple_fwd(h, gain, wpg, pl_in, wpp, *, name, tm=512):
    t, d = h.shape
    tm = min(tm, t)
    pd = pl_in.shape[1]

    def body(h_ref, g_ref, wpg_ref, p_ref, wpp_ref, o_ref):
        hh = h_ref[...]
        n = _rms_fwd(hh, g_ref[...])[0]
        gate = _sigmoid(_nn(n, wpg_ref[...]))
        o_ref[...] = hh + gate * _nn(p_ref[...], wpp_ref[...])

    row = pl.BlockSpec((tm, d), lambda i: (i, 0))
    return pl.pallas_call(
        body, name=name, grid=(t // tm,),
        in_specs=[row, _whole((1, d)), _whole((d, d)), pl.BlockSpec((tm, pd), lambda i: (i, 0)), _whole((pd, d))],
        out_specs=row, out_shape=SDS((t, d), F32), compiler_params=_params(("parallel",)))(h, gain, wpg, pl_in, wpp)


def ple_bwd(h, gain, wpg, pl_in, wpp, dout, *, name, tm=512):
    t, d = h.shape
    tm = min(tm, t)
    pd = pl_in.shape[1]

    def body(h_ref, g_ref, wpg_ref, p_ref, wpp_ref, do_ref, dh_ref, n_ref, dga_ref, dpp_ref, dg_ref):
        @pl.when(pl.program_id(0) == 0)
        def _():
            dg_ref[...] = jnp.zeros_like(dg_ref)

        g = g_ref[...]
        n, xh, r = _rms_fwd(h_ref[...], g)
        gate = _sigmoid(_nn(n, wpg_ref[...]))
        pp = _nn(p_ref[...], wpp_ref[...])
        do = do_ref[...]
        dga = do * pp * gate * (1.0 - gate)
        dh, dg = _rms_bwd(_nt(dga, wpg_ref[...]), xh, r, g)
        dh_ref[...] = do + dh
        n_ref[...] = n.astype(BF16)
        dga_ref[...] = dga.astype(BF16)
        dpp_ref[...] = (do * gate).astype(BF16)
        dg_ref[...] += dg

    row = pl.BlockSpec((tm, d), lambda i: (i, 0))
    return pl.pallas_call(
        body, name=name, grid=(t // tm,),
        in_specs=[row, _whole((1, d)), _whole((d, d)), pl.BlockSpec((tm, pd), lambda i: (i, 0)), _whole((pd, d)), row],
        out_specs=[row, row, row, row, _whole((1, d))],
        out_shape=[SDS((t, d), F32), SDS((t, d), BF16), SDS((t, d), BF16), SDS((t, d), BF16), SDS((1, d), F32)],
        compiler_params=_params(("arbitrary",)))(h, gain, wpg, pl_in, wpp, dout)


def loss_head(h, gain, target, *, name, tm=512):
    t, d = h.shape
    tm = min(tm, t)

    def body(h_ref, g_ref, t_ref, dh_ref, dg_ref, l_ref):
        @pl.when(pl.program_id(0) == 0)
        def _():
            dg_ref[...] = jnp.zeros_like(dg_ref)
            l_ref[...] = jnp.zeros_like(l_ref)

        g = g_ref[...]
        y, xh, r = _rms_fwd(h_ref[...], g)
        err = y - t_ref[...]
        l_ref[...] += 0.5 * jnp.sum(jnp.mean(err * err, axis=-1, keepdims=True), axis=0, keepdims=True)
        dh, dg = _rms_bwd(err * (1.0 / d), xh, r, g)
        dh_ref[...] = dh
        dg_ref[...] += dg

    row = pl.BlockSpec((tm, d), lambda i: (i, 0))
    return pl.pallas_call(
        body, name=name, grid=(t // tm,), in_specs=[row, _whole((1, d)), row],
        out_specs=[row, _whole((1, d)), _whole((1, 1))],
        out_shape=[SDS((t, d), F32), SDS((1, d), F32), SDS((1, 1), F32)],
        compiler_params=_params(("arbitrary",)))(h, gain, target)


def adamw(w, g, m, v, *, name):
    r, c = w.shape
    tr = r
    for cand in (512, 256, 128, 64, 32, 16, 8):
        if r % cand == 0:
            tr = cand
            break

    def body(w_ref, g_ref, m_ref, v_ref, d_ref, nm_ref, nv_ref):
        gg = g_ref[...]
        mm = ADAM_B1 * m_ref[...] + (1.0 - ADAM_B1) * gg
        vv = ADAM_B2 * v_ref[...] + (1.0 - ADAM_B2) * (gg * gg)
        m_hat = mm / (1.0 - ADAM_B1 ** ADAM_STEP)
        v_hat = vv / (1.0 - ADAM_B2 ** ADAM_STEP)
        d_ref[...] = -ADAM_LR * (m_hat / (jnp.sqrt(v_hat) + ADAM_EPS) + ADAM_WD * w_ref[...])
        nm_ref[...] = mm
        nv_ref[...] = vv

    blk = pl.BlockSpec((tr, c), lambda i: (i, 0))
    out = SDS((r, c), F32)
    return pl.pallas_call(body, name=name, grid=(r // tr,), in_specs=[blk] * 4, out_specs=[blk] * 3,
                          out_shape=[out, out, out], compiler_params=_params(("parallel",)))(w, g, m, v)


def _scan_fwd(a, b):
    d = 1
    while d < a.shape[0]:
        b = a * _shift_down(b, d, 0.0) + b
        a = a * _shift_down(a, d, 1.0)
        d *= 2
    return b


def _scan_rev(a, b):
    d = 1
    while d < a.shape[0]:
        b = a * _shift_up(b, d, 0.0) + b
        a = a * _shift_up(a, d, 1.0)
        d *= 2
    return b


LRU_HALF = 128


def _lru_in_specs(seq):
    half = LRU_W // LRU_HALF
    vec = pl.BlockSpec((1, LRU_HALF), lambda j, b: (0, j))
    mat = pl.BlockSpec((LRU_HALF, LRU_HALF), lambda j, b: (j, j))
    return [pl.BlockSpec((seq, LRU_HALF), lambda j, b: (b, j)), pl.BlockSpec((seq, LRU_HALF), lambda j, b: (b, half + j)),
            pl.BlockSpec((4, LRU_HALF), lambda j, b: (0, j)), vec, mat, vec, mat, vec, vec]


def _lru_math(x_ref, gate_ref, cw_ref, cb_ref, wa_ref, ba_ref, wx_ref, bx_ref, lam_ref):
    x = x_ref[...]
    gate = gate_ref[...]
    cw =[cw_ref[k:k + 1, :] for k in range(4)]
    xr = _conv_fwd(x, cw) + cb_ref[...]
    r = _sigmoid(_nn(xr, wa_ref[...]) + ba_ref[...])
    i = _sigmoid(_nn(xr, wx_ref[...]) + bx_ref[...])
    sp = _softplus(-lam_ref[...])
    log_a = -LRU_C * r * sp
    a = jnp.exp(log_a)
    mult = jnp.sqrt(_neg_expm1(2.0 * log_a))
    gi = i * xr
    h = _scan_fwd(a, mult * gi)
    gl, tg = _gelu(gate)
    return dict(x=x, gate=gate, cw=cw, xr=xr, r=r, i=i, sp=sp, a=a, mult=mult, gi=gi, h=h, gl=gl, tg=tg)


def lru_fwd(u, cw, cb, wa, ba, wx, bx, lam, *, seq, name):
    t = u.shape[0]

    def body(x_ref, gate_ref, cw_ref, cb_ref, wa_ref, ba_ref, wx_ref, bx_ref, lam_ref, y_ref):
        f = _lru_math(x_ref, gate_ref, cw_ref, cb_ref, wa_ref, ba_ref, wx_ref, bx_ref, lam_ref)
        y_ref[...] = f["gl"] * f["h"]

    return pl.pallas_call(
        body, name=name, grid=(LRU_W // LRU_HALF, t // seq), in_specs=_lru_in_specs(seq),
        out_specs=pl.BlockSpec((seq, LRU_HALF), lambda j, b: (b, j)), out_shape=SDS((t, LRU_W), F32),
        compiler_params=_params(("parallel", "parallel")))(u, u, cw, cb, wa, ba, wx, bx, lam)


def lru_bwd(u, cw, cb, wa, ba, wx, bx, lam, dy, *, seq, name):
    t = u.shape[0]

    def body(x_ref, gate_ref, cw_ref, cb_ref, wa_ref, ba_ref, wx_ref, bx_ref, lam_ref, dy_ref,
             dx_ref, dgate_ref, dcw_ref, dwa_ref, dwx_ref, dv_ref):
        @pl.when(pl.program_id(1) == 0)
        def _():
            dcw_ref[...] = jnp.zeros_like(dcw_ref)
            dwa_ref[...] = jnp.zeros_like(dwa_ref)
            dwx_ref[...] = jnp.zeros_like(dwx_ref)
            dv_ref[...] = jnp.zeros_like(dv_ref)

        f = _lru_math(x_ref, gate_ref, cw_ref, cb_ref, wa_ref, ba_ref, wx_ref, bx_ref, lam_ref)
        dy = dy_ref[...]
        a, h, xr, r, i, mult, gi, sp = f["a"], f["h"], f["xr"], f["r"], f["i"], f["mult"], f["gi"], f["sp"]
        dgate_ref[...] = dy * h * _gelu_grad(f["gate"], f["tg"])
        lamb = _scan_rev(_shift_up(a, 1, 0.0), dy * f["gl"])
        da = lamb * _shift_down(h, 1)
        dlog_a = da * a - (lamb * gi) * (a * a) / mult
        dgi = lamb * mult
        dra = dlog_a * (-LRU_C * sp) * r * (1.0 - r)
        dia = dgi * xr * i * (1.0 - i)
        dsp = jnp.sum(dlog_a * (-LRU_C * r), axis=0, keepdims=True)
        dlam = -dsp * _sigmoid(-lam_ref[...])
        dxr = dgi * i + _nt(dra, wa_ref[...]) + _nt(dia, wx_ref[...])
        dx, dcw = _conv_bwd(dxr, f["x"], f["cw"])
        dx_ref[...] = dx
        dcw_ref[...] += dcw
        dwa_ref[...] += _tn(xr, dra)
        dwx_ref[...] += _tn(xr, dia)
        rows = [jnp.sum(dxr, axis=0, keepdims=True), jnp.sum(dra, axis=0, keepdims=True),
                jnp.sum(dia, axis=0, keepdims=True), dlam]
        r8 = lax.broadcasted_iota(jnp.int32, (8, LRU_HALF), 0)
        acc = jnp.zeros((8, LRU_HALF), F32)
        for k, row in enumerate(rows):
            acc = jnp.where(r8 == k, row, acc)
        dv_ref[...] += acc

    nhalf = LRU_W // LRU_HALF
    col = pl.BlockSpec((seq, LRU_HALF), lambda j, b: (b, j))
    mat = pl.BlockSpec((None, LRU_HALF, LRU_HALF), lambda j, b: (j, 0, 0))
    return pl.pallas_call(
        body, name=name, grid=(nhalf, t // seq), in_specs=_lru_in_specs(seq) + [col],
        out_specs=[col, col, pl.BlockSpec((4, LRU_HALF), lambda j, b: (0, j)), mat, mat,
                   pl.BlockSpec((8, LRU_HALF), lambda j, b: (0, j))],
        out_shape=[SDS((t, LRU_W), F32), SDS((t, LRU_W), F32), SDS((4, LRU_W), F32),
                   SDS((nhalf, LRU_HALF, LRU_HALF), F32), SDS((nhalf, LRU_HALF, LRU_HALF), F32), SDS((8, LRU_W), F32)],
        compiler_params=_params(("arbitrary", "arbitrary")))(u, u, cw, cb, wa, ba, wx, bx, lam, dy)


NEG = -1e30


def _rel_bucket_map():
    dist = (np.arange(BLOCK_Q)[:, None] - np.arange(BLOCK_Q)[None, :]) % BLOCK_Q
    max_exact = REL_BUCKETS // 2
    large = max_exact + (np.log(np.maximum(dist, 1).astype(np.float32) / max_exact)
                         / math.log(BLOCK_Q / max_exact) * (REL_BUCKETS - max_exact)).astype(np.int32)
    large = np.minimum(large, REL_BUCKETS - 1)
    return np.where(dist < max_exact, dist, large).astype(np.int32)


def relbias_fwd(rel_bias, bmap, *, name):
    def body(rb_ref, bm_ref, o_ref):
        bm = bm_ref[...]
        for h in range(ATT_HEADS):
            acc = jnp.zeros((BLOCK_Q, BLOCK_Q), F32)
            for b in range(REL_BUCKETS):
                acc = jnp.where(bm == b, rb_ref[b, h], acc)
            o_ref[h] = acc

    return pl.pallas_call(
        body, name=name, in_specs=[pl.BlockSpec(memory_space=pltpu.SMEM), pl.BlockSpec(memory_space=pltpu.VMEM)],
        out_specs=pl.BlockSpec(memory_space=pltpu.VMEM), out_shape=SDS((ATT_HEADS, BLOCK_Q, BLOCK_Q), F32))(rel_bias, bmap)


def relbias_bwd(dbias, bmap, *, name):
    def body(db_ref, bm_ref, o_ref):
        bm = bm_ref[...]
        row = lax.broadcasted_iota(jnp.int32, (REL_BUCKETS, 128), 0)
        col = lax.broadcasted_iota(jnp.int32, (REL_BUCKETS, 128), 1)
        acc = jnp.zeros((REL_BUCKETS, 128), F32)
        for h in range(ATT_HEADS):
            d = db_ref[h]
            for b in range(REL_BUCKETS):
                s = jnp.sum(jnp.sum(jnp.where(bm == b, d, 0.0), axis=1, keepdims=True), axis=0, keepdims=True)
                acc = jnp.where((row == b) & (col == h), s, acc)
        o_ref[...] = acc

    return pl.pallas_call(body, name=name, out_shape=SDS((REL_BUCKETS, 128), F32))(dbias, bmap)


def _attn_probs(q_ref, k_ref, v_ref, b_ref, s_ref, n):
    rows = ATT_GROUP * BLOCK_Q
    qs = q_ref[...].reshape(rows, HEAD) * (HEAD ** -0.5)
    prev = pl.multiple_of(jnp.maximum(n - 1, 0) * BLOCK_Q, BLOCK_Q)
    cur = pl.multiple_of(n * BLOCK_Q, BLOCK_Q)
    kp, kc = k_ref[pl.ds(prev, BLOCK_Q), :], k_ref[pl.ds(cur, BLOCK_Q), :]
    vp, vc = v_ref[pl.ds(prev, BLOCK_Q), :], v_ref[pl.ds(cur, BLOCK_Q), :]
    bias = b_ref[...].reshape(rows, BLOCK_Q)
    i = lax.broadcasted_iota(jnp.int32, (rows, BLOCK_Q), 0) & (BLOCK_Q - 1)
    j = lax.broadcasted_iota(jnp.int32, (rows, BLOCK_Q), 1)
    s_p = jnp.where((j > i) & (n > 0), _nt(qs, kp) + bias, NEG)
    s_c = jnp.where(j <= i, _nt(qs, kc) + bias, NEG)
    sink = s_ref[...]
    m = jnp.maximum(jnp.maximum(jnp.max(s_p, axis=-1, keepdims=True), jnp.max(s_c, axis=-1, keepdims=True)), sink)
    e_p, e_c, e_s = jnp.exp(s_p - m), jnp.exp(s_c - m), jnp.exp(sink - m)
    inv = 1.0 / (jnp.sum(e_p, axis=-1, keepdims=True) + jnp.sum(e_c, axis=-1, keepdims=True) + e_s)
    return e_p * inv, e_c * inv, e_s * inv, qs, kp, kc, vp, vc, prev, cur


def _attn_specs(seq):
    qspec = pl.BlockSpec((None, ATT_GROUP, BLOCK_Q, HEAD), lambda g, b, n: (b, g, n, 0))
    kvspec = pl.BlockSpec((None, None, seq, HEAD), lambda g, b, n: (b, g, 0, 0))
    bspec = pl.BlockSpec((ATT_GROUP, BLOCK_Q, BLOCK_Q), lambda g, b, n: (g, 0, 0))
    sspec = pl.BlockSpec((ATT_GROUP * BLOCK_Q, 1), lambda g, b, n: (g, 0))
    return qspec, kvspec, bspec, sspec


def attn_fwd(q, k, v, bias, sink_rows, *, name):
    nb, _, seq, _ = q.shape

    def body(q_ref, k_ref, v_ref, b_ref, s_ref, o_ref):
        p_p, p_c, _, _, _, _, vp, vc, _, _ = _attn_probs(q_ref, k_ref, v_ref, b_ref, s_ref, pl.program_id(2))
        o_ref[...] = (_nn(p_p, vp) + _nn(p_c, vc)).reshape(ATT_GROUP, BLOCK_Q, HEAD)

    qspec, kvspec, bspec, sspec = _attn_specs(seq)
    return pl.pallas_call(
        body, name=name, grid=(KV_HEADS, nb, seq // BLOCK_Q), in_specs=[qspec, kvspec, kvspec, bspec, sspec],
        out_specs=qspec, out_shape=SDS(q.shape, F32),
        compiler_params=_params(("parallel", "parallel", "arbitrary")))(q, k, v, bias, sink_rows)


def attn_bwd(q, k, v, bias, sink_rows, do, *, name):
    nb, _, seq, _ = q.shape

    def body(q_ref, k_ref, v_ref, b_ref, s_ref, do_ref, dq_ref, dk_ref, dv_ref, db_ref, ds_ref):
        b, n = pl.program_id(1), pl.program_id(2)

        @pl.when((b == 0) & (n == 0))
        def _():
            db_ref[...] = jnp.zeros_like(db_ref)
            ds_ref[...] = jnp.zeros_like(ds_ref)

        @pl.when(n == 0)
        def _():
            dk_ref[...] = jnp.zeros_like(dk_ref)
            dv_ref[...] = jnp.zeros_like(dv_ref)

        p_p, p_c, p_s, qs, kp, kc, vp, vc, prev, cur = _attn_probs(q_ref, k_ref, v_ref, b_ref, s_ref, n)
        do = do_ref[...].reshape(ATT_GROUP * BLOCK_Q, HEAD)
        dp_p, dp_c = _nt(do, vp), _nt(do, vc)
        delta = jnp.sum(p_p * dp_p, axis=-1, keepdims=True) + jnp.sum(p_c * dp_c, axis=-1, keepdims=True)
        ds_p, ds_c = p_p * (dp_p - delta), p_c * (dp_c - delta)
        dq_ref[...] = ((_nn(ds_p, kp) + _nn(ds_c, kc)) * (HEAD ** -0.5)).reshape(ATT_GROUP, BLOCK_Q, HEAD)
        dk_ref[pl.ds(prev, BLOCK_Q), :] += _tn(ds_p, qs)
        dk_ref[pl.ds(cur, BLOCK_Q), :] += _tn(ds_c, qs)
        dv_ref[pl.ds(prev, BLOCK_Q), :] += _tn(p_p, do)
        dv_ref[pl.ds(cur, BLOCK_Q), :] += _tn(p_c, do)
        db_ref[...] += (ds_p + ds_c).reshape(ATT_GROUP, BLOCK_Q, BLOCK_Q)
        ds_ref[...] += -p_s * delta

    qspec, kvspec, bspec, sspec = _attn_specs(seq)
    return pl.pallas_call(
        body, name=name, grid=(KV_HEADS, nb, seq // BLOCK_Q), in_specs=[qspec, kvspec, kvspec, bspec, sspec, qspec],
        out_specs=[qspec, kvspec, kvspec, bspec, sspec],
        out_shape=[SDS(q.shape, F32), SDS(k.shape, F32), SDS(v.shape, F32),
                   SDS((ATT_HEADS, BLOCK_Q, BLOCK_Q), F32), SDS((ATT_HEADS * BLOCK_Q, 1), F32)],
        compiler_params=_params(("arbitrary", "arbitrary", "arbitrary")))(q, k, v, bias, sink_rows, do)


def _iota2(shape, axis):
    return lax.broadcasted_iota(jnp.int32, shape, axis)


def _col_to_row(col):
    c = col.shape[0]
    eye = _iota2((c, c), 0) == _iota2((c, c), 1)
    return jnp.sum(jnp.where(eye, jnp.broadcast_to(col, (c, c)), 0.0), axis=0, keepdims=True)


def _row_to_col(row):
    c = row.shape[1]
    eye = _iota2((c, c), 0) == _iota2((c, c), 1)
    return jnp.sum(jnp.where(eye, jnp.broadcast_to(row, (c, c)), 0.0), axis=1, keepdims=True)


def _last_row(col):
    c = col.shape[0]
    return jnp.sum(jnp.where(_iota2((c, 1), 0) == c - 1, col, 0.0), axis=0, keepdims=True)


def _chunk_cumsum(x):
    pos = _iota2(x.shape, 0) & (DN_CHUNK - 1)
    d = 1
    while d < DN_CHUNK:
        x = x + jnp.where(pos >= d, pltpu.roll(x, d, 0), 0.0)
        d *= 2
    return x


def _chunk_rev_cumsum(x):
    n = x.shape[0]
    pos = _iota2(x.shape, 0) & (DN_CHUNK - 1)
    d = 1
    while d < DN_CHUNK:
        x = x + jnp.where(pos < DN_CHUNK - d, pltpu.roll(x, n - d, 0), 0.0)
        d *= 2
    return x


def _tri_inv(low):
    c = low.shape[0]
    eye = (_iota2((c, c), 0) == _iota2((c, c), 1)).astype(F32)
    m = -low
    p = eye + m
    steps = int(math.log2(c)) - 1
    for _ in range(steps):
        m = _nn(m, m, hi=True)
        p = p + _nn(p, m, hi=True)
    return p


_DN_SCALE = (HEAD ** -0.5, 1.0, None)


def _dn_act(c, scale):
    sig = _sigmoid(c)
    a = c * sig
    if scale is None:
        return a, sig, None, None
    r = lax.rsqrt(jnp.sum(a * a, axis=-1, keepdims=True) + EPS)
    return a * r * scale, sig, a * r, r


def _dn_gates(ba_ref, hs_ref):
    beta = _sigmoid(ba_ref[0])
    sp_arg = ba_ref[1] + hs_ref[1]
    a_exp = jnp.exp(hs_ref[0])
    g = -a_exp * _softplus(sp_arg)
    return beta, g, sp_arg, a_exp


def _dn_inputs(pre_ref, cw_ref, ba_ref, hs_ref, act_sc, b_sc, gc_sc, c_sc=None):
    for idx in range(3):
        c = _conv_fwd(pre_ref[idx], [cw_ref[idx, k:k + 1, :] for k in range(4)])
        if c_sc is not None:
            c_sc[idx] = c
        act_sc[idx] = _dn_act(c, _DN_SCALE[idx])[0]
    beta, g, _, _ = _dn_gates(ba_ref, hs_ref)
    b_sc[...] = beta
    gc_sc[...] = _chunk_cumsum(g)


def _dn_chunk_math(q, k, v, b, gcc):
    c = q.shape[0]
    tril = _iota2((c, c), 0) >= _iota2((c, c), 1)
    strict = _iota2((c, c), 0) > _iota2((c, c), 1)
    eg = jnp.exp(gcc)
    kb, vb = k * b, v * b
    kbg = kb * eg
    dm = jnp.exp(jnp.where(tril, jnp.broadcast_to(gcc, (c, c)) - _col_to_row(gcc), NEG))
    kk = _nt(kb, k)
    t = _tri_inv(jnp.where(strict, kk * dm, 0.0))
    glast = _last_row(gcc)
    ekd = jnp.exp(glast - gcc)
    qk = _nt(q, k)
    return dict(tril=tril, strict=strict, eg=eg, kb=kb, vb=vb, kbg=kbg, dm=dm, kk=kk, t=t, glast=glast, ekd=ekd,
                kd=k * ekd, qk=qk, amat=jnp.where(tril, qk * dm, 0.0), qg=q * eg,
                egl=jnp.broadcast_to(jnp.exp(glast), (c, 1)))


DN_UNROLL = 4


def _chunk_loop(nc, chunk):
    u = math.gcd(nc, DN_UNROLL)

    def step(i, carry):
        for j in range(u):
            chunk(i * u + j)
        return carry

    lax.fori_loop(0, nc // u, step, 0)


def _dn_specs(seq):
    s64 = lambda lead: pl.BlockSpec((lead, None, None, seq, HEAD), lambda b, h: (0, b, h, 0, 0))
    s1 = lambda lead: pl.BlockSpec((lead, None, None, seq, 1), lambda b, h: (0, b, h, 0, 0))
    one64 = pl.BlockSpec((None, None, seq, HEAD), lambda b, h: (b, h, 0, 0))
    one1 = pl.BlockSpec((None, None, seq, 1), lambda b, h: (b, h, 0, 0))
    cw = pl.BlockSpec((None, 3, 4, HEAD), lambda b, h: (h, 0, 0, 0))
    hs = pl.BlockSpec((None, 2, 1, 1), lambda b, h: (h, 0, 0, 0))
    return s64, s1, one64, one1, cw, hs


def dn_prep(pre, cw, ba, hs, *, name):
    _, nb, nh, seq, _ = pre.shape
    nc = seq // DN_CHUNK

    def body(pre_ref, cw_ref, ba_ref, hs_ref, loc_ref, egl_ref, act_sc, b_sc, gc_sc):
        _dn_inputs(pre_ref, cw_ref, ba_ref, hs_ref, act_sc, b_sc, gc_sc)

        def chunk(c):
            rows = pl.ds(pl.multiple_of(c * DN_CHUNK, DN_CHUNK), DN_CHUNK)
            m = _dn_chunk_math(act_sc[0, rows, :], act_sc[1, rows, :], act_sc[2, rows, :], b_sc[rows, :], gc_sc[rows, :])
            loc_ref[0, rows, :] = m["qg"]
            loc_ref[1, rows, :] = m["kd"]
            loc_ref[2, rows, :] = _nn(m["t"], m["vb"])
            loc_ref[3, rows, :] = _nn(m["t"], m["kbg"])
            loc_ref[4, rows, :] = m["amat"]
            egl_ref[rows, :] = m["egl"]

        _chunk_loop(nc, chunk)

    s64, s1, one64, one1, cwspec, hsspec = _dn_specs(seq)
    return pl.pallas_call(
        body, name=name, grid=(nb, nh), in_specs=[s64(3), cwspec, s1(2), hsspec], out_specs=[s64(5), one1],
        out_shape=[SDS((5, nb, nh, seq, HEAD), F32), SDS((nb, nh, seq, 1), F32)],
        scratch_shapes=[pltpu.VMEM((3, seq, HEAD), F32)] + [pltpu.VMEM((seq, 1), F32)] * 2,
        compiler_params=_params(("parallel", "parallel")))(pre, cw, ba, hs)


def _gated_norm(o, z, gn):
    r = lax.rsqrt(jnp.mean(o * o, axis=-1, keepdims=True) + EPS)
    sig = _sigmoid(z)
    return o * r, sig, r


def dn_scan(loc, egl, z, gn, *, name):
    _, nb, nh, seq, _ = loc.shape
    nc = seq // DN_CHUNK

    def body(loc_ref, egl_ref, z_ref, gn_ref, y_ref, o_ref, vn_ref, st_ref):
        gn = gn_ref[...]

        def step(c, state):
            rows = pl.ds(pl.multiple_of(c * DN_CHUNK, DN_CHUNK), DN_CHUNK)
            st_ref[rows, :] = state
            vn = loc_ref[2, rows, :] - _nn(loc_ref[3, rows, :], state)
            o = _nn(loc_ref[0, rows, :], state) + _nn(loc_ref[4, rows, :], vn)
            vn_ref[rows, :] = vn
            o_ref[rows, :] = o
            zz = z_ref[rows, :]
            on, sig, _ = _gated_norm(o, zz, gn)
            y_ref[rows, :] = on * gn * (zz * sig)
            return state * egl_ref[rows, :] + _tn(loc_ref[1, rows, :], vn)

        lax.fori_loop(0, nc, step, jnp.zeros((HEAD, HEAD), F32))

    s64, s1, one64, one1, cwspec, hsspec = _dn_specs(seq)
    out = SDS((nb, nh, seq, HEAD), F32)
    return pl.pallas_call(
        body, name=name, grid=(nb, nh), in_specs=[s64(5), one1, one64, _whole((1, HEAD))],
        out_specs=[one64] * 4, out_shape=[out] * 4,
        compiler_params=_params(("parallel", "parallel")))(loc, egl, z, gn)


def dn_scan_bwd(loc, egl, z, gn, o, vn, states, dy, *, name):
    _, nb, nh, seq, _ = loc.shape
    nc = seq // DN_CHUNK

    def body(loc_ref, egl_ref, z_ref, gn_ref, o_ref, vn_ref, st_ref, dy_ref, dloc_ref, degl_ref, dz_ref, dgn_ref):
        @pl.when((pl.program_id(0) == 0) & (pl.program_id(1) == 0))
        def _():
            dgn_ref[...] = jnp.zeros_like(dgn_ref)

        gn = gn_ref[...]
        tril = _iota2((DN_CHUNK, DN_CHUNK), 0) >= _iota2((DN_CHUNK, DN_CHUNK), 1)

        def step(i, carry):
            ds, dgn = carry
            rows = pl.ds(pl.multiple_of((nc - 1 - i) * DN_CHUNK, DN_CHUNK), DN_CHUNK)
            dy, zz, oo = dy_ref[rows, :], z_ref[rows, :], o_ref[rows, :]
            on, sig, r = _gated_norm(oo, zz, gn)
            sz = zz * sig
            dz_ref[rows, :] = dy * on * gn * (sig * (1.0 + zz * (1.0 - sig)))
            dgn = dgn + jnp.sum(dy * on * sz, axis=0, keepdims=True)
            don = dy * gn * sz
            do = r * (don - on * jnp.mean(don * on, axis=-1, keepdims=True))
            state, vnew = st_ref[rows, :], vn_ref[rows, :]
            qg, kd, w, amat = loc_ref[0, rows, :], loc_ref[1, rows, :], loc_ref[3, rows, :], loc_ref[4, rows, :]
            dvn = _tn(amat, do) + _nn(kd, ds)
            dloc_ref[0, rows, :] = _nt(do, state)
            dloc_ref[1, rows, :] = _nt(vnew, ds)
            dloc_ref[2, rows, :] = dvn
            dloc_ref[3, rows, :] = -_nt(dvn, state)
            dloc_ref[4, rows, :] = jnp.where(tril, _nt(do, vnew), 0.0)
            degl = jnp.sum(jnp.sum(state * ds, axis=1, keepdims=True), axis=0, keepdims=True)
            degl_ref[rows, :] = jnp.broadcast_to(degl, (DN_CHUNK, 1))
            return ds * egl_ref[rows, :] + _tn(qg, do) - _tn(w, dvn), dgn

        _, dgn = lax.fori_loop(0, nc, step, (jnp.zeros((HEAD, HEAD), F32), jnp.zeros((1, HEAD), F32)))
        dgn_ref[...] += dgn

    s64, s1, one64, one1, cwspec, hsspec = _dn_specs(seq)
    return pl.pallas_call(
        body, name=name, grid=(nb, nh),
        in_specs=[s64(5), one1, one64, _whole((1, HEAD)), one64, one64, one64, one64],
        out_specs=[s64(5), one1, one64, _whole((1, HEAD))],
        out_shape=[SDS((5, nb, nh, seq, HEAD), F32), SDS((nb, nh, seq, 1), F32), SDS((nb, nh, seq, HEAD), F32),
                   SDS((1, HEAD), F32)],
        compiler_params=_params(("arbitrary", "arbitrary")))(loc, egl, z, gn, o, vn, states, dy)


def dn_prep_bwd(pre, cw, ba, hs, dloc, degl, *, name):
    _, nb, nh, seq, _ = pre.shape
    nc = seq // DN_CHUNK

    def body(pre_ref, cw_ref, ba_ref, hs_ref, dloc_ref, degl_ref, dpre_ref, dba_ref, dcw_ref, dhs_ref,
             act_sc, b_sc, gc_sc, c_sc):
        @pl.when(pl.program_id(1) == 0)
        def _():
            dcw_ref[...] = jnp.zeros_like(dcw_ref)
            dhs_ref[...] = jnp.zeros_like(dhs_ref)

        _dn_inputs(pre_ref, cw_ref, ba_ref, hs_ref, act_sc, b_sc, gc_sc, c_sc)

        def chunk(c):
            rows = pl.ds(pl.multiple_of(c * DN_CHUNK, DN_CHUNK), DN_CHUNK)
            q, k, v, b, gcc = act_sc[0, rows, :], act_sc[1, rows, :], act_sc[2, rows, :], b_sc[rows, :], gc_sc[rows, :]
            m = _dn_chunk_math(q, k, v, b, gcc)
            dqg, dkd, du, dw, da = (dloc_ref[x, rows, :] for x in range(5))
            t, dm, eg = m["t"], m["dm"], m["eg"]
            dt = _nt(du, m["vb"]) + _nt(dw, m["kbg"])
            dvb, dkbg = _tn(t, du), _tn(t, dw)
            dl = jnp.where(m["strict"], -_tn(t, _nt(dt, t, hi=True), hi=True), 0.0)
            dkk = dl * dm
            dqk = da * dm
            dd = dl * m["kk"] + da * m["qk"]
            dkb = _nn(dkk, k) + dkbg * eg
            dq = _nn(dqk, k) + dqg * eg
            dk = _tn(dkk, m["kb"]) + _tn(dqk, q) + dkd * m["ekd"] + dkb * b
            db = jnp.sum(dkb * k, axis=-1, keepdims=True) + jnp.sum(dvb * v, axis=-1, keepdims=True)
            mx = jnp.where(m["tril"], dd * dm, 0.0)
            tk = jnp.sum(dkd * m["kd"], axis=-1, keepdims=True)
            dgc = (jnp.sum(mx, axis=-1, keepdims=True) - _row_to_col(jnp.sum(mx, axis=0, keepdims=True))
                   + jnp.sum(dqg * m["qg"], axis=-1, keepdims=True) + jnp.sum(dkbg * m["kbg"], axis=-1, keepdims=True) - tk)
            dglast = jnp.sum(tk, axis=0, keepdims=True) + _last_row(degl_ref[rows, :]) * jnp.exp(m["glast"])
            act_sc[0, rows, :] = dq
            act_sc[1, rows, :] = dk
            act_sc[2, rows, :] = dvb * b
            b_sc[rows, :] = db
            gc_sc[rows, :] = dgc + jnp.where(_iota2((DN_CHUNK, 1), 0) == DN_CHUNK - 1, dglast, 0.0)

        _chunk_loop(nc, chunk)

        beta, g, sp_arg, a_exp = _dn_gates(ba_ref, hs_ref)
        dg = _chunk_rev_cumsum(gc_sc[...])
        dal = dg * (-a_exp) * _sigmoid(sp_arg)
        dba_ref[0] = b_sc[...] * beta * (1.0 - beta)
        dba_ref[1] = dal
        dhs_ref[0] += jnp.sum(dg * g, axis=0, keepdims=True)
        dhs_ref[1] += jnp.sum(dal, axis=0, keepdims=True)
        for idx in range(3):
            c = c_sc[idx]
            _, sig, hat, r = _dn_act(c, _DN_SCALE[idx])
            da_ = act_sc[idx]
            if _DN_SCALE[idx] is not None:
                da_ = da_ * _DN_SCALE[idx]
                da_ = r * (da_ - hat * jnp.sum(da_ * hat, axis=-1, keepdims=True))
            dx, dcw = _conv_bwd(da_ * (sig * (1.0 + c * (1.0 - sig))), pre_ref[idx],
                                [cw_ref[idx, k:k + 1, :] for k in range(4)])
            dpre_ref[idx] = dx
            dcw_ref[idx] += dcw

    s64, s1, one64, one1, cwspec, hsspec = _dn_specs(seq)
    swap = lambda spec: pl.BlockSpec(spec.block_shape, lambda h, b, _f=spec.index_map: _f(b, h))
    return pl.pallas_call(
        body, name=name, grid=(nh, nb),
        in_specs=[swap(s64(3)), swap(cwspec), swap(s1(2)), swap(hsspec), swap(s64(5)), swap(one1)],
        out_specs=[swap(s64(3)), swap(s1(2)), swap(cwspec), swap(hsspec)],
        out_shape=[SDS((3, nb, nh, seq, HEAD), F32), SDS((2, nb, nh, seq, 1), F32), SDS((nh, 3, 4, HEAD), F32),
                   SDS((nh, 2, 1, 1), F32)],
        scratch_shapes=[pltpu.VMEM((3, seq, HEAD), F32)] + [pltpu.VMEM((seq, 1), F32)] * 2 + [pltpu.VMEM((3, seq, HEAD), F32)],
        compiler_params=_params(("arbitrary", "arbitrary")))(pre, cw, ba, hs, dloc, degl)


COL_Q, COL_K, COL_V = 512 // 128, 1024 // 128, 1152 // 128
COL_DNQ, COL_DNK, COL_DNV, COL_DNZ, COL_BA = 1280 // 128, 1536 // 128, 1792 // 128, 2048 // 128, 2304 // 128


def _lane_a(shape):
    return _iota2(shape, 1) < HEAD


def _bd(x):
    la = _lane_a(x.shape)
    return jnp.concatenate([jnp.where(la, x, 0.0), jnp.where(la, 0.0, x)], axis=0)


def _fold(m):
    return m[:HEAD] + m[HEAD:]


def _bd_mask():
    return (_iota2((2 * HEAD, 2 * HEAD), 0) < HEAD) == (_iota2((2 * HEAD, 2 * HEAD), 1) < HEAD)


def _pk_nn(x, y, hi=False):
    return _nn(x, _bd(y), hi)


def _pk_nt(u, v, hi=False):
    return _nt(u, _bd(v), hi)


def _pk_tn(x, y, hi=False):
    return _fold(jnp.where(_bd_mask(), _tn(x, y, hi), 0.0))


def _half_sum(x):
    la = _lane_a(x.shape)
    return jnp.where(la, jnp.sum(jnp.where(la, x, 0.0), axis=-1, keepdims=True),
                     jnp.sum(jnp.where(la, 0.0, x), axis=-1, keepdims=True))


def _lane_col(x, idx):
    return jnp.sum(jnp.where(_iota2(x.shape, 1) == idx, x, 0.0), axis=-1, keepdims=True)


def _row0(x):
    return jnp.max(x, axis=0, keepdims=True)


def _dup_kv(x, g):
    la = _lane_a(x.shape)
    rolled = pltpu.roll(x, HEAD, 1)
    return jnp.where(la, x, rolled) if g == 0 else jnp.where(la, rolled, x)


def _stack_heads(ref, g):
    la = _lane_a((BLOCK_Q, 2 * HEAD))
    parts = []
    for hh in range(ATT_GROUP):
        pair = ref[:, pl.ds(2 * HEAD * (2 * g + hh // 2), 2 * HEAD)]
        parts.append(jnp.where(la if hh % 2 == 0 else ~la, pair, 0.0))
    return jnp.concatenate(parts, axis=0)


def _unstack_heads(stack, ref, g):
    la = _lane_a((BLOCK_Q, 2 * HEAD))
    for j in range(2):
        top = stack[2 * j * BLOCK_Q:(2 * j + 1) * BLOCK_Q]
        bot = stack[(2 * j + 1) * BLOCK_Q:(2 * j + 2) * BLOCK_Q]
        ref[:, pl.ds(2 * HEAD * (2 * g + j), 2 * HEAD)] = jnp.where(la, top, bot)


def _swa_probs(q_ref, k_ref, v_ref, b_ref, s_ref, n, g):
    rows = ATT_GROUP * BLOCK_Q
    prev = pl.multiple_of(jnp.maximum(n - 1, 0) * BLOCK_Q, BLOCK_Q)
    cur = pl.multiple_of(n * BLOCK_Q, BLOCK_Q)
    kp, kc = _dup_kv(k_ref[pl.ds(prev, BLOCK_Q), :], g), _dup_kv(k_ref[pl.ds(cur, BLOCK_Q), :], g)
    vp, vc = _dup_kv(v_ref[pl.ds(prev, BLOCK_Q), :], g), _dup_kv(v_ref[pl.ds(cur, BLOCK_Q), :], g)
    qs = _stack_heads(q_ref, g) * (HEAD ** -0.5)
    bias = b_ref[pl.ds(ATT_GROUP * g, ATT_GROUP)].reshape(rows, BLOCK_Q)
    i = _iota2((rows, BLOCK_Q), 0) & (BLOCK_Q - 1)
    j = _iota2((rows, BLOCK_Q), 1)
    s_p = jnp.where((j > i) & (n > 0), _nt(qs, kp) + bias, NEG)
    s_c = jnp.where(j <= i, _nt(qs, kc) + bias, NEG)
    sink = s_ref[pl.ds(rows * g, rows), :]
    m = jnp.maximum(jnp.maximum(jnp.max(s_p, axis=-1, keepdims=True), jnp.max(s_c, axis=-1, keepdims=True)), sink)
    e_p, e_c, e_s = jnp.exp(s_p - m), jnp.exp(s_c - m), jnp.exp(sink - m)
    inv = 1.0 / (jnp.sum(e_p, axis=-1, keepdims=True) + jnp.sum(e_c, axis=-1, keepdims=True) + e_s)
    return e_p * inv, e_c * inv, e_s * inv, qs, kp, kc, vp, vc, prev, cur


def _swa_specs(seq):
    nblk = seq // BLOCK_Q
    qspec = pl.BlockSpec((BLOCK_Q, ATT_W), lambda b, n: (b * nblk + n, COL_Q * 128 // ATT_W))
    kspec = pl.BlockSpec((seq, 2 * HEAD), lambda b, n: (b, COL_K))
    vspec = pl.BlockSpec((seq, 2 * HEAD), lambda b, n: (b, COL_V))
    ospec = pl.BlockSpec((BLOCK_Q, ATT_W), lambda b, n: (b * nblk + n, 0))
    kvout = pl.BlockSpec((seq, 2 * HEAD), lambda b, n: (b, 0))
    return qspec, kspec, vspec, ospec, kvout, _whole((ATT_HEADS, BLOCK_Q, BLOCK_Q)), _whole((ATT_HEADS * BLOCK_Q, 1))


def swa_fwd(u, bias, sink_rows, *, seq, name):
    t = u.shape[0]

    def body(q_ref, k_ref, v_ref, b_ref, s_ref, o_ref):
        for g in range(KV_HEADS):
            p_p, p_c, _, _, _, _, vp, vc, _, _ = _swa_probs(q_ref, k_ref, v_ref, b_ref, s_ref, pl.program_id(1), g)
            _unstack_heads(_nn(p_p, vp) + _nn(p_c, vc), o_ref, g)

    qspec, kspec, vspec, ospec, kvout, bspec, sspec = _swa_specs(seq)
    return pl.pallas_call(
        body, name=name, grid=(t // seq, seq // BLOCK_Q), in_specs=[qspec, kspec, vspec, bspec, sspec], out_specs=ospec,
        out_shape=SDS((t, ATT_W), F32), compiler_params=_params(("parallel", "arbitrary")))(u, u, u, bias, sink_rows)


def swa_bwd(u, bias, sink_rows, do, *, seq, name):
    t = u.shape[0]

    def body(q_ref, k_ref, v_ref, b_ref, s_ref, do_ref, dq_ref, dk_ref, dv_ref, db_ref, ds_ref):
        b, n = pl.program_id(0), pl.program_id(1)

        @pl.when((b == 0) & (n == 0))
        def _():
            db_ref[...] = jnp.zeros_like(db_ref)
            ds_ref[...] = jnp.zeros_like(ds_ref)

        @pl.when(n == 0)
        def _():
            dk_ref[...] = jnp.zeros_like(dk_ref)
            dv_ref[...] = jnp.zeros_like(dv_ref)

        la = _lane_a((BLOCK_Q, 2 * HEAD))
        for g in range(KV_HEADS):
            p_p, p_c, p_s, qs, kp, kc, vp, vc, prev, cur = _swa_probs(q_ref, k_ref, v_ref, b_ref, s_ref, n, g)
            do = _stack_heads(do_ref, g)
            dp_p, dp_c = _nt(do, vp), _nt(do, vc)
            delta = jnp.sum(p_p * dp_p, axis=-1, keepdims=True) + jnp.sum(p_c * dp_c, axis=-1, keepdims=True)
            ds_p, ds_c = p_p * (dp_p - delta), p_c * (dp_c - delta)
            _unstack_heads((_nn(ds_p, kp) + _nn(ds_c, kc)) * (HEAD ** -0.5), dq_ref, g)
            mine = la if g == 0 else ~la

            def to_head(x):
                return jnp.where(mine, x + pltpu.roll(x, HEAD, 1), 0.0)

            dk_ref[pl.ds(prev, BLOCK_Q), :] += to_head(_tn(ds_p, qs))
            dk_ref[pl.ds(cur, BLOCK_Q), :] += to_head(_tn(ds_c, qs))
            dv_ref[pl.ds(prev, BLOCK_Q), :] += to_head(_tn(p_p, do))
            dv_ref[pl.ds(cur, BLOCK_Q), :] += to_head(_tn(p_c, do))
            db_ref[pl.ds(ATT_GROUP * g, ATT_GROUP)] += (ds_p + ds_c).reshape(ATT_GROUP, BLOCK_Q, BLOCK_Q)
            rows = ATT_GROUP * BLOCK_Q
            ds_ref[pl.ds(rows * g, rows), :] += -p_s * delta

    qspec, kspec, vspec, ospec, kvout, bspec, sspec = _swa_specs(seq)
    return pl.pallas_call(
        body, name=name, grid=(t // seq, seq // BLOCK_Q), in_specs=[qspec, kspec, vspec, bspec, sspec, ospec],
        out_specs=[ospec, kvout, kvout, bspec, sspec],
        out_shape=[SDS((t, ATT_W), F32), SDS((t, 2 * HEAD), F32), SDS((t, 2 * HEAD), F32),
                   SDS((ATT_HEADS, BLOCK_Q, BLOCK_Q), F32), SDS((ATT_HEADS * BLOCK_Q, 1), F32)],
        compiler_params=_params(("arbitrary", "arbitrary")))(u, u, u, bias, sink_rows, do)


def _gdn_gates(ba_ref, alog_ref, dt_ref, hp):
    blk = ba_ref[...]
    beta_blk = _sigmoid(blk)
    sp_arg = blk + dt_ref[...]
    a_exp = jnp.exp(alog_ref[...])
    g_blk = -a_exp * _softplus(sp_arg)
    la = _lane_a(blk.shape)
    ha = 2 * hp
    beta = jnp.where(la, _lane_col(beta_blk, ha), _lane_col(beta_blk, ha + 1))
    g = jnp.where(la, _lane_col(g_blk, DN_HEADS + ha), _lane_col(g_blk, DN_HEADS + ha + 1))
    return beta, g, beta_blk, sp_arg, a_exp, g_blk


def _gdn_act(c, scale):
    sig = _sigmoid(c)
    a = c * sig
    if scale is None:
        return a, sig, None, None
    r = lax.rsqrt(_half_sum(a * a) + EPS)
    return a * r * scale, sig, a * r, r


def _gdn_inputs(pre_refs, cw_refs, ba_ref, alog_ref, dt_ref, hp, act_sc, b_sc, gc_sc, c_sc=None):
    for idx in range(3):
        c = _conv_fwd(pre_refs[idx][...], [cw_refs[idx][k:k + 1, :] for k in range(4)])
        if c_sc is not None:
            c_sc[idx] = c
        act_sc[idx] = _gdn_act(c, _DN_SCALE[idx])[0]
    beta, g = _gdn_gates(ba_ref, alog_ref, dt_ref, hp)[:2]
    b_sc[...] = beta
    gc_sc[...] = _chunk_cumsum(g)


def _gdn_chunk(q, k, v, b, gcc):
    shape = q.shape
    row, lm = _iota2(shape, 0), _iota2(shape, 1) & (HEAD - 1)
    tril, strict, eye = row >= lm, row > lm, row == lm
    eg = jnp.exp(gcc)
    kb, vb = k * b, v * b
    kbg = kb * eg
    grow = jnp.sum(jnp.where(eye, gcc, 0.0), axis=0, keepdims=True)
    dm = jnp.exp(jnp.where(tril, gcc - grow, NEG))
    kk = _pk_nt(kb, k)
    glast = jnp.sum(jnp.where(row == DN_CHUNK - 1, gcc, 0.0), axis=0, keepdims=True)
    ekd = jnp.exp(glast - gcc)
    qk = _pk_nt(q, k)
    return dict(q=q, k=k, v=v, b=b, tril=tril, strict=strict, eye=eye, row=row, eg=eg, kb=kb, vb=vb, kbg=kbg, dm=dm, kk=kk,
                low=jnp.where(strict, kk * dm, 0.0), glast=glast, ekd=ekd, kd=k * ekd, qk=qk,
                amat=jnp.where(tril, qk * dm, 0.0), qg=q * eg, egl=jnp.broadcast_to(jnp.exp(glast), shape))


def _tri_inv_many(chunks):
    ms = [-m["low"] for m in chunks]
    ts = [m["eye"].astype(F32) + x for m, x in zip(chunks, ms)]
    for _ in range(int(math.log2(HEAD)) - 1):
        ms = [_pk_nn(x, x, hi=True) for x in ms]
        ts = [t + _pk_nn(t, x, hi=True) for t, x in zip(ts, ms)]
    return ts


def _gdn_chunk_loop(nc, act_sc, b_sc, gc_sc, finish):
    u = math.gcd(nc, DN_UNROLL)

    def step(i, carry):
        rows = [pl.ds(pl.multiple_of((i * u + j) * DN_CHUNK, DN_CHUNK), DN_CHUNK) for j in range(u)]
        chunks = [_gdn_chunk(act_sc[0, r, :], act_sc[1, r, :], act_sc[2, r, :], b_sc[r, :], gc_sc[r, :]) for r in rows]
        pending = [finish(r, m, t) for r, m, t in zip(rows, chunks, _tri_inv_many(chunks))]
        pending = [g for g in pending if g is not None]
        while pending:
            for g in list(pending):
                if next(g, StopIteration) is StopIteration:
                    pending.remove(g)
        return carry

    lax.fori_loop(0, nc // u, step, 0)


def _gdn_in_specs(seq):
    u_at = lambda col: pl.BlockSpec((seq, 2 * HEAD), lambda b, hp, _c=col: (b, _c + hp))
    cw_at = lambda col: pl.BlockSpec((4, 2 * HEAD), lambda b, hp, _c=col: (0, _c + hp))
    row = pl.BlockSpec((1, 2 * HEAD), lambda b, hp: (0, 0))
    ba = pl.BlockSpec((seq, 2 * HEAD), lambda b, hp: (b, COL_BA))
    return [u_at(COL_DNQ), u_at(COL_DNK), u_at(COL_DNV), ba, cw_at(0), cw_at(2), cw_at(4), row, row]


def _pair(seq, lead=None):
    if lead is None:
        return pl.BlockSpec((seq, 2 * HEAD), lambda b, hp: (b, hp))
    return pl.BlockSpec((lead, seq, 2 * HEAD), lambda b, hp: (0, b, hp))


def _swap(spec):
    return pl.BlockSpec(spec.block_shape, lambda hp, b, _f=spec.index_map: _f(b, hp))


def gdn_prep(u, cw, alog_row, dt_row, *, seq, name):
    t = u.shape[0]
    nc = seq // DN_CHUNK

    def body(q_ref, k_ref, v_ref, ba_ref, cq_ref, ck_ref, cv_ref, alog_ref, dt_ref, loc_ref, egl_ref, act_sc, b_sc, gc_sc):
        _gdn_inputs((q_ref, k_ref, v_ref), (cq_ref, ck_ref, cv_ref), ba_ref, alog_ref, dt_ref, pl.program_id(1),
                    act_sc, b_sc, gc_sc)

        def finish(rows, m, t):
            loc_ref[0, rows, :] = m["qg"]
            loc_ref[1, rows, :] = m["kd"]
            loc_ref[2, rows, :] = _pk_nn(t, m["vb"])
            loc_ref[3, rows, :] = _pk_nn(t, m["kbg"])
            loc_ref[4, rows, :] = m["amat"]
            egl_ref[rows, :] = m["egl"]

        _gdn_chunk_loop(nc, act_sc, b_sc, gc_sc, finish)

    return pl.pallas_call(
        body, name=name, grid=(t // seq, DN_HEADS // 2), in_specs=_gdn_in_specs(seq), out_specs=[_pair(seq, 5), _pair(seq)],
        out_shape=[SDS((5, t, DN_HEADS * HEAD), F32), SDS((t, DN_HEADS * HEAD), F32)],
        scratch_shapes=[pltpu.VMEM((3, seq, 2 * HEAD), F32)] + [pltpu.VMEM((seq, 2 * HEAD), F32)] * 2,
        compiler_params=_params(("parallel", "parallel")))(u, u, u, u, cw, cw, cw, alog_row, dt_row)


def _gated_norm2(o, z, gn):
    r = lax.rsqrt(_half_sum(o * o) * (1.0 / HEAD) + EPS)
    return o * r, _sigmoid(z), r


def gdn_scan(loc, egl, u, gn, *, seq, name):
    t = u.shape[0]
    nc = seq // DN_CHUNK

    def body(loc_ref, egl_ref, z_ref, gn_ref, y_ref, o_ref, vn_ref, st_ref):
        gn = gn_ref[...]
        bdm = _bd_mask()

        def step(c, state):
            rows = pl.ds(pl.multiple_of(c * DN_CHUNK, DN_CHUNK), DN_CHUNK)
            st_ref[rows, :] = _fold(state)
            vn = loc_ref[2, rows, :] - _nn(loc_ref[3, rows, :], state)
            o = _nn(loc_ref[0, rows, :], state) + _pk_nn(loc_ref[4, rows, :], vn)
            vn_ref[rows, :] = vn
            o_ref[rows, :] = o
            zz = z_ref[rows, :]
            on, sig, _ = _gated_norm2(o, zz, gn)
            y_ref[rows, :] = on * gn * (zz * sig)
            return state * _row0(egl_ref[rows, :]) + jnp.where(bdm, _tn(loc_ref[1, rows, :], vn), 0.0)

        lax.fori_loop(0, nc, step, jnp.zeros((2 * HEAD, 2 * HEAD), F32))

    zspec = pl.BlockSpec((seq, 2 * HEAD), lambda b, hp: (b, COL_DNZ + hp))
    out = SDS((t, DN_HEADS * HEAD), F32)
    return pl.pallas_call(
        body, name=name, grid=(t // seq, DN_HEADS // 2), in_specs=[_pair(seq, 5), _pair(seq), zspec, _whole((1, 2 * HEAD))],
        out_specs=[_pair(seq)] * 4, out_shape=[out] * 4,
        compiler_params=_params(("parallel", "parallel")))(loc, egl, u, gn)


def gdn_scan_bwd(loc, egl, u, gn, o, vn, states, dy, *, seq, name):
    t = u.shape[0]
    nc = seq // DN_CHUNK

    def body(loc_ref, egl_ref, z_ref, gn_ref, o_ref, vn_ref, st_ref, dy_ref, dloc_ref, degl_ref, dz_ref, dgn_ref):
        @pl.when((pl.program_id(0) == 0) & (pl.program_id(1) == 0))
        def _():
            dgn_ref[...] = jnp.zeros_like(dgn_ref)

        gn = gn_ref[...]
        bdm = _bd_mask()
        shape = (DN_CHUNK, 2 * HEAD)
        tril = _iota2(shape, 0) >= (_iota2(shape, 1) & (HEAD - 1))

        def step(i, carry):
            ds, dgn = carry
            rows = pl.ds(pl.multiple_of((nc - 1 - i) * DN_CHUNK, DN_CHUNK), DN_CHUNK)
            dy, zz, oo = dy_ref[rows, :], z_ref[rows, :], o_ref[rows, :]
            on, sig, r = _gated_norm2(oo, zz, gn)
            sz = zz * sig
            dz_ref[rows, :] = dy * on * gn * (sig * (1.0 + zz * (1.0 - sig)))
            dgn = dgn + jnp.sum(dy * on * sz, axis=0, keepdims=True)
            don = dy * gn * sz
            do = r * (don - on * _half_sum(don * on) * (1.0 / HEAD))
            state, vnew = _bd(st_ref[rows, :]), vn_ref[rows, :]
            qg, kd, w, amat = loc_ref[0, rows, :], loc_ref[1, rows, :], loc_ref[3, rows, :], loc_ref[4, rows, :]
            dvn = _pk_tn(amat, do) + _nn(kd, ds)
            dloc_ref[0, rows, :] = _nt(do, state)
            dloc_ref[1, rows, :] = _nt(vnew, ds)
            dloc_ref[2, rows, :] = dvn
            dloc_ref[3, rows, :] = -_nt(dvn, state)
            dloc_ref[4, rows, :] = jnp.where(tril, _pk_nt(do, vnew), 0.0)
            degl = _half_sum(jnp.sum(state * ds, axis=0, keepdims=True))
            degl_ref[rows, :] = jnp.broadcast_to(degl, shape)
            grow = jnp.where(bdm, _tn(qg, do) - _tn(w, dvn), 0.0)
            return ds * _row0(egl_ref[rows, :]) + grow, dgn

        _, dgn = lax.fori_loop(0, nc, step, (jnp.zeros((2 * HEAD, 2 * HEAD), F32), jnp.zeros((1, 2 * HEAD), F32)))
        dgn_ref[...] += dgn

    zspec = pl.BlockSpec((seq, 2 * HEAD), lambda b, hp: (b, COL_DNZ + hp))
    one = _pair(seq)
    out = SDS((t, DN_HEADS * HEAD), F32)
    return pl.pallas_call(
        body, name=name, grid=(t // seq, DN_HEADS // 2),
        in_specs=[_pair(seq, 5), one, zspec, _whole((1, 2 * HEAD)), one, one, one, one],
        out_specs=[_pair(seq, 5), one, one, _whole((1, 2 * HEAD))],
        out_shape=[SDS((5, t, DN_HEADS * HEAD), F32), out, out, SDS((1, 2 * HEAD), F32)],
        compiler_params=_params(("arbitrary", "arbitrary")))(loc, egl, u, gn, o, vn, states, dy)


def gdn_prep_bwd(u, cw, alog_row, dt_row, dloc, degl, *, seq, name):
    t = u.shape[0]
    nc = seq // DN_CHUNK

    def body(q_ref, k_ref, v_ref, ba_ref, cq_ref, ck_ref, cv_ref, alog_ref, dt_ref, dloc_ref, degl_ref,
             dqkv_ref, dba_ref, dcw_ref, dhs_ref, act_sc, b_sc, gc_sc, c_sc):
        hp = pl.program_id(0)

        @pl.when(pl.program_id(1) == 0)
        def _():
            dcw_ref[...] = jnp.zeros_like(dcw_ref)
            dhs_ref[...] = jnp.zeros_like(dhs_ref)

        pre_refs, cw_refs = (q_ref, k_ref, v_ref), (cq_ref, ck_ref, cv_ref)
        _gdn_inputs(pre_refs, cw_refs, ba_ref, alog_ref, dt_ref, hp, act_sc, b_sc, gc_sc, c_sc)

        def finish(rows, m, tt):
            q, k, v, b = m["q"], m["k"], m["v"], m["b"]
            dqg, dkd, du, dw, da = (dloc_ref[x, rows, :] for x in range(5))
            dm, eg = m["dm"], m["eg"]
            dt = _pk_nt(du, m["vb"]) + _pk_nt(dw, m["kbg"])
            dvb, dkbg = _pk_tn(tt, du), _pk_tn(tt, dw)
            yield
            dtt = _pk_nt(dt, tt, hi=True)
            yield
            dl = jnp.where(m["strict"], -_pk_tn(tt, dtt, hi=True), 0.0)
            yield
            dkk = dl * dm
            dqk = da * dm
            dd = dl * m["kk"] + da * m["qk"]
            dkb = _pk_nn(dkk, k) + dkbg * eg
            dq = _pk_nn(dqk, k) + dqg * eg
            yield
            dk = _pk_tn(dkk, m["kb"]) + _pk_tn(dqk, q) + dkd * m["ekd"] + dkb * b
            db = _half_sum(dkb * k + dvb * v)
            yield
            mx = jnp.where(m["tril"], dd * dm, 0.0)
            tk = _half_sum(dkd * m["kd"])
            colsum = jnp.where(m["eye"], jnp.broadcast_to(jnp.sum(mx, axis=0, keepdims=True), mx.shape), 0.0)
            dgc = _half_sum(mx) - _half_sum(colsum) + _half_sum(dqg * m["qg"] + dkbg * m["kbg"]) - tk
            dglast = jnp.sum(tk, axis=0, keepdims=True) + _row0(degl_ref[rows, :]) * jnp.exp(m["glast"])
            act_sc[0, rows, :] = dq
            act_sc[1, rows, :] = dk
            act_sc[2, rows, :] = dvb * b
            b_sc[rows, :] = db
            gc_sc[rows, :] = dgc + jnp.where(m["row"] == DN_CHUNK - 1, dglast, 0.0)

        _gdn_chunk_loop(nc, act_sc, b_sc, gc_sc, finish)

        beta, g, beta_blk, sp_arg, a_exp, g_blk = _gdn_gates(ba_ref, alog_ref, dt_ref, hp)
        dg = _chunk_rev_cumsum(gc_sc[...])
        lane = _iota2(beta_blk.shape, 1)
        ha = 2 * hp
        db = b_sc[...]
        at = lambda idx, x_a, x_b: (jnp.where(lane == idx, _lane_col(x_a, 0), 0.0)
                                    + jnp.where(lane == idx + 1, _lane_col(x_b, HEAD), 0.0))
        dg_blk = at(DN_HEADS + ha, dg, dg)
        dal = dg_blk * (-a_exp) * _sigmoid(sp_arg)
        dba_ref[...] = at(ha, db, db) * beta_blk * (1.0 - beta_blk) + dal
        dhs_ref[0:1, :] += jnp.sum(dg_blk * g_blk, axis=0, keepdims=True)
        dhs_ref[1:2, :] += jnp.sum(dal, axis=0, keepdims=True)
        for idx in range(3):
            c = c_sc[idx]
            _, sig, hat, r = _gdn_act(c, _DN_SCALE[idx])
            da_ = act_sc[idx]
            if _DN_SCALE[idx] is not None:
                da_ = da_ * _DN_SCALE[idx]
                da_ = r * (da_ - hat * _half_sum(da_ * hat))
            dx, dcw = _conv_bwd(da_ * (sig * (1.0 + c * (1.0 - sig))), pre_refs[idx][...],
                                [cw_refs[idx][k:k + 1, :] for k in range(4)])
            dqkv_ref[idx] = dx
            dcw_ref[idx] += dcw

    pair = DN_HEADS // 2
    in_specs = [_swap(s) for s in _gdn_in_specs(seq)] + [_swap(_pair(seq, 5)), _swap(_pair(seq))]
    return pl.pallas_call(
        body, name=name, grid=(pair, t // seq), in_specs=in_specs,
        out_specs=[_swap(_pair(seq, 3)), pl.BlockSpec((None, seq, 2 * HEAD), lambda hp, b: (hp, b, 0)),
                   pl.BlockSpec((3, 4, 2 * HEAD), lambda hp, b: (0, 0, hp)),
                   pl.BlockSpec((None, 2, 2 * HEAD), lambda hp, b: (hp, 0, 0))],
        out_shape=[SDS((3, t, DN_HEADS * HEAD), F32), SDS((pair, t, 2 * HEAD), F32), SDS((3, 4, DN_HEADS * HEAD), F32),
                   SDS((pair, 2, 2 * HEAD), F32)],
        scratch_shapes=[pltpu.VMEM((3, seq, 2 * HEAD), F32)] + [pltpu.VMEM((seq, 2 * HEAD), F32)] * 2
        + [pltpu.VMEM((3, seq, 2 * HEAD), F32)],
        compiler_params=_params(("arbitrary", "arbitrary")))(u, u, u, u, cw, cw, cw, alog_row, dt_row, dloc, degl)


def mix_out(y_lru, o, y_dn, w_out, h, *, name, tm=512):
    t, d = h.shape
    tm = min(tm, t)

    def body(a_ref, b_ref, c_ref, w_ref, h_ref, o_ref, y_ref):
        y_ref[:, 0:LRU_W] = a_ref[...].astype(BF16)
        y_ref[:, LRU_W:LRU_W + ATT_W] = b_ref[...].astype(BF16)
        y_ref[:, LRU_W + ATT_W:] = c_ref[...].astype(BF16)
        o_ref[...] = h_ref[...] + _nn(y_ref[...], w_ref[...])

    rows = lambda width: pl.BlockSpec((tm, width), lambda i: (i, 0))
    return pl.pallas_call(
        body, name=name, grid=(t // tm,), in_specs=[rows(LRU_W), rows(ATT_W), rows(LRU_W), _whole((d, d)), rows(d)],
        out_specs=[rows(d), rows(d)], out_shape=[SDS((t, d), F32), SDS((t, d), BF16)],
        compiler_params=_params(("parallel",)))(y_lru, o, y_dn, w_out, h)


def mix_out_bwd(dout, w_out, *, name, tm=512):
    t, d = dout.shape
    tm = min(tm, t)

    def body(d_ref, w_ref, a_ref, b_ref, c_ref):
        dy = _nt(d_ref[...], w_ref[...])
        a_ref[...] = dy[:, 0:LRU_W]
        b_ref[...] = dy[:, LRU_W:LRU_W + ATT_W]
        c_ref[...] = dy[:, LRU_W + ATT_W:]

    rows = lambda width: pl.BlockSpec((tm, width), lambda i: (i, 0))
    return pl.pallas_call(
        body, name=name, grid=(t // tm,), in_specs=[rows(d), _whole((d, d))], out_specs=[rows(LRU_W), rows(ATT_W), rows(LRU_W)],
        out_shape=[SDS((t, LRU_W), F32), SDS((t, ATT_W), F32), SDS((t, LRU_W), F32)],
        compiler_params=_params(("parallel",)))(dout, w_out)


def mix_in_bwd(h, gain, dout, w_in, dx, dgate, dq, dk, dv, dqkv, dz, dba, *, name, tm=512):
    t, d = h.shape
    tm = min(tm, t)

    def body(h_ref, g_ref, do_ref, w_ref, dx_ref, dgate_ref, dq_ref, dk_ref, dv_ref, dqkv_ref, dz_ref, dba_ref,
             dh_ref, dg_ref, du_ref):
        @pl.when(pl.program_id(0) == 0)
        def _():
            dg_ref[...] = jnp.zeros_like(dg_ref)

        off = 0
        for piece in (dx_ref[...], dgate_ref[...], dq_ref[...], dk_ref[...], dv_ref[...], dqkv_ref[0], dqkv_ref[1],
                      dqkv_ref[2], dz_ref[...], dba_ref[0] + dba_ref[1]):
            du_ref[:, off:off + piece.shape[1]] = piece.astype(BF16)
            off += piece.shape[1]
        du_ref[:, off:] = jnp.zeros((tm, D_IN_PAD - off), BF16)
        g = g_ref[...]
        _, xh, r = _rms_fwd(h_ref[...], g)
        dh, dg = _rms_bwd(_nt(du_ref[...], w_ref[...]), xh, r, g)
        dh_ref[...] = do_ref[...] + dh
        dg_ref[...] += dg

    rows = lambda width: pl.BlockSpec((tm, width), lambda i: (i, 0))
    return pl.pallas_call(
        body, name=name, grid=(t // tm,),
        in_specs=[rows(d), _whole((1, d)), rows(d), _whole((d, D_IN_PAD)), rows(LRU_W), rows(LRU_W), rows(ATT_W),
                  rows(2 * HEAD), rows(2 * HEAD), pl.BlockSpec((3, tm, DN_HEADS * HEAD), lambda i: (0, i, 0)),
                  rows(DN_HEADS * HEAD), pl.BlockSpec((2, tm, 2 * HEAD), lambda i: (0, i, 0))],
        out_specs=[rows(d), _whole((1, d)), rows(D_IN_PAD)],
        out_shape=[SDS((t, d), F32), SDS((1, d), F32), SDS((t, D_IN_PAD), BF16)],
        compiler_params=_params(("arbitrary",)))(h, gain, dout, w_in, dx, dgate, dq, dk, dv, dqkv, dz, dba)


def _block_diag(w):
    out = jnp.zeros((LRU_W, LRU_W), w.dtype)
    for h in range(LRU_W // HEAD):
        out = lax.dynamic_update_slice(out, w[h], (h * HEAD, h * HEAD))
    return out


def _diag_blocks(w):
    per = LRU_HALF // HEAD
    return jnp.stack([w[h // per, (h % per) * HEAD:(h % per + 1) * HEAD, (h % per) * HEAD:(h % per + 1) * HEAD]
                      for h in range(LRU_W // HEAD)])


def layer_params(w, wl, l, bias):
    row = lambda a: a[l].reshape(1, -1)
    return dict(
        ffn1_norm=row(w["ffn1_norm"]), ffn1=(wl["ffn1_w_gate"], wl["ffn1_w_up"], wl["ffn1_w_down"]),
        mix_norm=row(w["mix_norm"]) + wl["tie1"][0:1, 0:1], w_in=wl["w_in"],
        lru=(wl["lru_conv_w"], row(w["lru_conv_b"]), _block_diag(w["lru_w_a"][l]), row(w["lru_b_a"]),
             _block_diag(w["lru_w_x"][l]), row(w["lru_b_x"]), row(w["lru_lambda"])),
        bias=bias, sink_rows=jnp.repeat(w["attn_sinks"][l], BLOCK_Q).reshape(ATT_HEADS * BLOCK_Q, 1),
        dn_cw=wl["dn_conv_w"], dn_alog=_ba_row(w["dn_a_log"][l]), dn_dt=_ba_row(w["dn_dt_bias"][l]),
        dn_norm=jnp.tile(row(w["dn_norm"]), (1, 2)), w_out=wl["w_out"],
        ffn2_norm=row(w["ffn2_norm"]), ffn2=(wl["ffn2_w_gate"], wl["ffn2_w_up"], wl["ffn2_w_down"]),
        ple_norm=row(w["ple_norm"]), ple_w_gate=wl["ple_w_gate"], ple_w_proj=wl["ple_w_proj"])


def _ba_row(per_head):
    return jnp.pad(per_head, (DN_HEADS, 2 * HEAD - 2 * DN_HEADS)).reshape(1, 2 * HEAD)


def mixer_fwd(h, p, nb, seq, tag):
    u, n = norm_matmul(h, p["mix_norm"], p["w_in"], name=f"mix_in_{tag}")
    y_lru = lru_fwd(u, *p["lru"], seq=seq, name=f"lru_fwd_{tag}")
    o = swa_fwd(u, p["bias"], p["sink_rows"], seq=seq, name=f"swa_fwd_{tag}")
    loc, egl = gdn_prep(u, p["dn_cw"], p["dn_alog"], p["dn_dt"], seq=seq, name=f"gdn_prep_{tag}")
    y_dn, o_raw, vn, st = gdn_scan(loc, egl, u, p["dn_norm"], seq=seq, name=f"gdn_scan_{tag}")
    out, ycat = mix_out(y_lru, o, y_dn, p["w_out"], h, name=f"mix_out_{tag}")
    return out, dict(h=h, u=u, n=n, loc=loc, egl=egl, o_raw=o_raw, vn=vn, st=st, ycat=ycat)


def mixer_bwd(dout, s, p, nb, seq, tag):
    u = s["u"]
    dy_lru, do, dy_dn = mix_out_bwd(dout, p["w_out"], name=f"mix_out_dx_{tag}")
    g = {"w_out": matmul(s["ycat"], dout, ta=True, name=f"mix_out_dw_{tag}")}
    dx, dgate, dcw, dwa, dwx, dvec = lru_bwd(u, *p["lru"], dy_lru, seq=seq, name=f"lru_bwd_{tag}")
    g.update(lru_conv_w=dcw, lru_conv_b=dvec[0], lru_w_a=_diag_blocks(dwa), lru_b_a=dvec[1], lru_w_x=_diag_blocks(dwx),
             lru_b_x=dvec[2], lru_lambda=dvec[3])
    dq, dk, dv, dbias, dsink = swa_bwd(u, p["bias"], p["sink_rows"], do, seq=seq, name=f"swa_bwd_{tag}")
    g.update(attn_sinks=dsink.reshape(ATT_HEADS, BLOCK_Q).sum(axis=1), bias=dbias)
    dloc, degl, dz, dgn = gdn_scan_bwd(s["loc"], s["egl"], u, p["dn_norm"], s["o_raw"], s["vn"], s["st"], dy_dn, seq=seq,
                                       name=f"gdn_scan_bwd_{tag}")
    dqkv, dba, dcw3, dhs = gdn_prep_bwd(u, p["dn_cw"], p["dn_alog"], p["dn_dt"], dloc, degl, seq=seq,
                                        name=f"gdn_prep_bwd_{tag}")
    dhs = dhs.sum(axis=0)[:, DN_HEADS:2 * DN_HEADS]
    g.update(dn_conv_w=dcw3.transpose(1, 0, 2).reshape(4, 3 * DN_HEADS * HEAD), dn_a_log=dhs[0], dn_dt_bias=dhs[1],
             dn_norm=dgn[0, :HEAD] + dgn[0, HEAD:])
    dh, dgain, du = mix_in_bwd(s["h"], p["mix_norm"], dout, p["w_in"], dx, dgate, dq, dk, dv, dqkv, dz, dba,
                               name=f"mix_in_bwd_{tag}")
    g["w_in"] = matmul(s["n"], du, ta=True, name=f"mix_in_dw_{tag}")
    g["mix_norm"] = dgain[0]
    return dh, g


SHARDED = ("ffn1_w_gate", "ffn1_w_up", "ffn1_w_down", "w_in", "w_out", "ffn2_w_gate", "ffn2_w_up", "ffn2_w_down",
           "ple_w_gate", "ple_w_proj")
PER_LAYER_SMALL = ("ffn1_norm", "mix_norm", "lru_conv_w", "lru_conv_b", "lru_w_a", "lru_b_a", "lru_w_x", "lru_b_x",
                   "lru_lambda", "attn_sinks", "dn_conv_w", "dn_a_log", "dn_dt_bias", "dn_norm", "ffn2_norm", "ple_norm")


GRAD_PARTS = (("ple_w_gate", "ple_w_proj", "ffn2_w_gate", "ffn2_w_up", "ffn2_w_down"),
              ("ffn1_w_gate", "ffn1_w_up", "ffn1_w_down", "w_in", "w_out"))
WEIGHT_PARTS = (("ffn1_w_gate", "ffn1_w_up", "ffn1_w_down"),
                ("w_in", "w_out", "ffn2_w_gate", "ffn2_w_up", "ffn2_w_down", "ple_w_gate", "ple_w_proj", "lru_conv_w",
                 "dn_conv_w"))


def _col_shards(a):
    r, c = a.shape
    return a.reshape(r, N_CHIP, c // N_CHIP).transpose(1, 0, 2)


def local_step(x, p, target, w, layer_weights, layer_grads, bmap, nb, seq):
    bias = relbias_fwd(w["rel_bias"], bmap, name="relbias_fwd")
    h, saved = x, []
    for l in range(N_LAYER):
        wl = layer_weights(l, 0, h)
        s = dict(h0=h)
        h = ffn_fwd(h, w["ffn1_norm"][l].reshape(1, -1) + wl["tie0"][0:1, 0:1], wl["ffn1_w_gate"], wl["ffn1_w_up"],
                    wl["ffn1_w_down"], name=f"ffn1_fwd_{l}")
        wl.update(layer_weights(l, 1, h))
        pr = layer_params(w, wl, l, bias)
        h, s["mix"] = mixer_fwd(h, pr, nb, seq, l)
        s["h2"] = h
        h = ffn_fwd(h, pr["ffn2_norm"], *pr["ffn2"], name=f"ffn2_fwd_{l}")
        s["h3"] = h
        h = ple_fwd(h, pr["ple_norm"], pr["ple_w_gate"], p[l], pr["ple_w_proj"], name=f"ple_fwd_{l}")
        saved.append((pr, s))
    dh, dgf, loss = loss_head(h, w["final_norm"].reshape(1, -1), target, name="loss_head")

    per_layer, dbias, token = [None] * N_LAYER, None, None
    for l in reversed(range(N_LAYER)):
        pr, s = saved[l]
        g = {}
        dout = dh
        ple_norm = pr["ple_norm"] if token is None else pr["ple_norm"] + token[0:1, 0:1]
        dh, n, dga, dpp, dg = ple_bwd(s["h3"], ple_norm, pr["ple_w_gate"], p[l], pr["ple_w_proj"], dout, name=f"ple_bwd_{l}")
        g["ple_norm"] = dg[0]
        g["ple_w_gate"] = matmul(n, dga, ta=True, name=f"ple_dwg_{l}").reshape(N_CHIP, -1, D_MODEL)
        g["ple_w_proj"] = _col_shards(matmul(p[l], dpp, ta=True, name=f"ple_dwp_{l}"))
        for part, (nm, hin) in enumerate((("ffn2", s["h2"]), ("ffn1", s["h0"]))):
            if nm == "ffn1":
                lru = list(pr["lru"])
                lru[1] = lru[1] + token[0:1, 0:1]
                dh, gm = mixer_bwd(dh, s["mix"], dict(pr, lru=tuple(lru)), nb, seq, l)
                dbias = gm.pop("bias") if dbias is None else dbias + gm.pop("bias")
                gm["w_in"] = _col_shards(gm["w_in"][:, :D_IN])
                gm["w_out"] = gm["w_out"].reshape(N_CHIP, -1, D_MODEL)
                g.update(gm)
            dout = dh
            dh, n, da, db, sact, dg = ffn_bwd_act(hin, pr[nm + "_norm"], dout, *pr[nm], name=f"{nm}_bwd_act_{l}")
            g[nm + "_norm"] = dg[0]
            g[nm + "_w_gate"], g[nm + "_w_up"], g[nm + "_w_down"] = ffn_bwd_w(n, da, db, sact, dout, name=f"{nm}_bwd_w_{l}")
            token = layer_grads(l, part, {k: g.pop(k) for k in GRAD_PARTS[part]}, dh)
        per_layer[l] = g
    grads = {k: jnp.stack([per_layer[l][k] for l in range(N_LAYER)]) for k in PER_LAYER_SMALL}
    grads["rel_bias"] = relbias_bwd(dbias, bmap, name="relbias_bwd")[:, :ATT_HEADS]
    grads["final_norm"] = dgf[0]
    return loss, dh, grads


HBM_SPEC = pl.BlockSpec(memory_space=pltpu.HBM)


def _place():
    x, y, c = lax.axis_index("x"), lax.axis_index("y"), lax.axis_index("c")
    chips = [(1 - x, y), (x, 1 - y), (1 - x, 1 - y)]
    return x, y, c, 2 * x + y, (x, y, 1 - c), chips, [2 * cx + cy for cx, cy in chips]


def _remote(src, dst, send_sem, recv_sem, to):
    return pltpu.make_async_remote_copy(src_ref=src, dst_ref=dst, send_sem=send_sem, recv_sem=recv_sem, device_id=to,
                                        device_id_type=MESH)


def place_shard(w, chip_arr, dtype, *, name):
    nl, r, c = w.shape
    tr = next(cand for cand in (256, 128, 64, 32, 16, 8, r) if r % cand == 0)

    def body(chip_ref, w_ref, o_ref):
        o_ref[...] = w_ref[...].astype(dtype)

    return pl.pallas_call(
        body, name=name,
        grid_spec=pltpu.PrefetchScalarGridSpec(
            num_scalar_prefetch=1, grid=(nl, r // tr),
            in_specs=[pl.BlockSpec((None, tr, c), lambda l, i, chip: (l, i, 0))],
            out_specs=pl.BlockSpec((None, None, tr, c), lambda l, i, chip: (chip[0], l, i, 0))),
        out_shape=SDS((N_CHIP, nl, r, c), dtype), compiler_params=_params(("parallel", "parallel")))(chip_arr, w)


def allgather_shards(shards, *, name):
    n = len(shards)

    def body(*refs):
        outs = refs[n:2 * n]
        send, recv, fsend, frecv = refs[2 * n:]
        x, y, c, me, sib, chips, cids = _place()
        first, passed = [], []
        for k in range(n):
            for j, chip in enumerate(chips):
                mine = outs[k].at[me, c]
                first.append(_remote(mine, mine, send.at[3 * k + j], recv.at[3 * k + j], (*chip, c)))
                first[-1].start()
        for k in range(n):
            for j in range(3):
                piece = outs[k].at[cids[j], c]
                _remote(piece, piece, send.at[3 * k + j], recv.at[3 * k + j], sib).wait_recv()
                passed.append(_remote(piece, piece, fsend.at[3 * k + j], frecv.at[3 * k + j], sib))
                passed[-1].start()
        for k in range(n):
            for j in range(3):
                piece = outs[k].at[cids[j], 1 - c]
                _remote(piece, piece, fsend.at[3 * k + j], frecv.at[3 * k + j], sib).wait_recv()
        for cp in first + passed:
            cp.wait_send()

    return pl.pallas_call(
        body, name=name, in_specs=[HBM_SPEC] * n, out_specs=[HBM_SPEC] * n,
        out_shape=[SDS(s.shape, s.dtype) for s in shards], input_output_aliases={k: k for k in range(n)},
        scratch_shapes=[pltpu.SemaphoreType.DMA((3 * n,))] * 4)(*shards)


def exchange_layers(gs, *, name):
    n = len(gs)

    def body(*refs):
        ins, outs, (send, recv) = refs[:n], refs[n:2 * n], refs[2 * n:]
        x, y, c, me, sib, chips, cids = _place()
        cps = [_remote(ins[k].at[1 - c], outs[k], send.at[k], recv.at[k], sib) for k in range(n)]
        for cp in cps:
            cp.start()
        for cp in cps:
            cp.wait()

    return pl.pallas_call(
        body, name=name, in_specs=[HBM_SPEC] * n, out_specs=[HBM_SPEC] * n,
        out_shape=[SDS(g.shape[1:], g.dtype) for g in gs], scratch_shapes=[pltpu.SemaphoreType.DMA((n,))] * 2)(*gs)


def reduce_to_shards(ss, *, name):
    n = len(ss)

    def body(*refs):
        ins, outs, (send, recv) = refs[:n], refs[n:2 * n], refs[2 * n:]
        x, y, c, me, sib, chips, cids = _place()
        cps = []
        for k in range(n):
            for j, chip in enumerate(chips):
                cps.append(_remote(ins[k].at[cids[j]], outs[k].at[j], send.at[3 * k + j], recv.at[3 * k + j], (*chip, c)))
                cps[-1].start()
        for k in range(n):
            for j in range(3):
                slot = outs[k].at[j]
                _remote(slot, slot, send.at[3 * k + j], recv.at[3 * k + j], sib).wait_recv()
        for cp in cps:
            cp.wait_send()

    return pl.pallas_call(
        body, name=name, in_specs=[HBM_SPEC] * n, out_specs=[HBM_SPEC] * n,
        out_shape=[SDS((N_CHIP - 1,) + s.shape[1:], s.dtype) for s in ss],
        scratch_shapes=[pltpu.SemaphoreType.DMA((3 * n,))] * 2)(*ss)


def share_layers(fs, *, name):
    n = len(fs)

    def body(*refs):
        outs, (send, recv) = refs[n:2 * n], refs[2 * n:]
        x, y, c, me, sib, chips, cids = _place()
        cps = [_remote(outs[k].at[c], outs[k].at[c], send.at[k], recv.at[k], sib) for k in range(n)]
        for cp in cps:
            cp.start()
        for k in range(n):
            theirs = outs[k].at[1 - c]
            _remote(theirs, theirs, send.at[k], recv.at[k], sib).wait_recv()
        for cp in cps:
            cp.wait_send()

    return pl.pallas_call(
        body, name=name, in_specs=[HBM_SPEC] * n, out_specs=[HBM_SPEC] * n, out_shape=[SDS(f.shape, f.dtype) for f in fs],
        input_output_aliases={k: k for k in range(n)}, scratch_shapes=[pltpu.SemaphoreType.DMA((n,))] * 2)(*fs)


N_DEV = 8


def allreduce_small(buf, *, name):
    rows = buf.shape[0]

    def body(in_ref, out_ref, gath, send, recv):
        x, y, c = lax.axis_index("x"), lax.axis_index("y"), lax.axis_index("c")
        mine = 4 * x + 2 * y + c
        gath[mine] = in_ref[...]
        cps = []
        for k in range(1, N_DEV):
            to = (x ^ (k >> 2), y ^ ((k >> 1) & 1), c ^ (k & 1))
            cps.append(_remote(in_ref, gath.at[mine], send.at[k - 1], recv.at[k - 1], to))
            cps[-1].start()
        for k in range(1, N_DEV):
            theirs = gath.at[4 * (x ^ (k >> 2)) + 2 * (y ^ ((k >> 1) & 1)) + (c ^ (k & 1))]
            _remote(theirs, theirs, send.at[k - 1], recv.at[k - 1], (x, y, c)).wait_recv()
        for cp in cps:
            cp.wait_send()
        acc = gath[0]
        for d in range(1, N_DEV):
            acc = acc + gath[d]
        out_ref[...] = acc

    vm = pl.BlockSpec(memory_space=pltpu.VMEM)
    return pl.pallas_call(
        body, name=name, in_specs=[vm], out_specs=vm, out_shape=SDS(buf.shape, F32),
        scratch_shapes=[pltpu.VMEM((N_DEV, rows, 128), F32), pltpu.SemaphoreType.DMA((N_DEV - 1,)),
                        pltpu.SemaphoreType.DMA((N_DEV - 1,))])(buf)


def add_sibling(g, r, c_arr, *, name, tr=256):
    _, m, cdim = g.shape
    assert m % tr == 0

    def body(c_ref, g_ref, r_ref, o_ref):
        o_ref[...] = (g_ref[...] + r_ref[...]).astype(o_ref.dtype)

    return pl.pallas_call(
        body, name=name,
        grid_spec=pltpu.PrefetchScalarGridSpec(
            num_scalar_prefetch=1, grid=(m // tr,),
            in_specs=[pl.BlockSpec((None, tr, cdim), lambda i, c: (c[0], i, 0)), pl.BlockSpec((tr, cdim), lambda i, c: (i, 0))],
            out_specs=pl.BlockSpec((tr, cdim), lambda i, c: (i, 0))),
        out_shape=SDS((m, cdim), BF16), compiler_params=_params(("parallel",)))(c_arr, g, r)


def sum_slots(own, r, place_arr, *, name, tr=256):
    _, m, cdim = r.shape
    tr = next(cand for cand in (tr, 128, 64, 32, 16, 8) if m % cand == 0)

    def body(p_ref, own_ref, r_ref, o_ref):
        o_ref[...] = ((own_ref[...].astype(F32) + r_ref[0].astype(F32)) + r_ref[1].astype(F32)) + r_ref[2].astype(F32)

    return pl.pallas_call(
        body, name=name,
        grid_spec=pltpu.PrefetchScalarGridSpec(
            num_scalar_prefetch=1, grid=(m // tr,),
            in_specs=[pl.BlockSpec((None, tr, cdim), lambda i, p: (p[0], i, 0)),
                      pl.BlockSpec((N_CHIP - 1, tr, cdim), lambda i, p: (0, i, 0))],
            out_specs=pl.BlockSpec((None, tr, cdim), lambda i, p: (p[1], i, 0))),
        out_shape=SDS((N_LAYER, m, cdim), F32), compiler_params=_params(("parallel",)))(place_arr, own, r)


SEM_SPEC = pl.BlockSpec(memory_space=pltpu.SEMAPHORE)
ANY_SPEC = pl.BlockSpec(memory_space=pl.ANY)
DATAFLOW = pltpu.SideEffectType.DATAFLOW_SIDE_EFFECTING


def _in_hbm(a):
    return pltpu.with_memory_space_constraint(a, pltpu.HBM)


def _my_rows(ref_rows, c, mine=True):
    half = ref_rows // 2
    start = (c if mine else 1 - c) * half
    return pl.ds(pl.multiple_of(start, 8), half)


def place_layer_shard(w, layer, chip_arr, dtype, after, *, name):
    _, r, c = w.shape
    tr = next(cand for cand in (256, 128, 64, 32, 16, 8, r) if r % cand == 0)

    def body(chip_ref, w_ref, after_ref, o_ref):
        o_ref[...] = w_ref[...].astype(dtype)

    return pl.pallas_call(
        body, name=name,
        grid_spec=pltpu.PrefetchScalarGridSpec(
            num_scalar_prefetch=1, grid=(r // tr,),
            in_specs=[pl.BlockSpec((None, tr, c), lambda i, chip: (layer, i, 0)), ANY_SPEC],
            out_specs=pl.BlockSpec((None, tr, c), lambda i, chip: (chip[0], i, 0))),
        out_shape=SDS((N_CHIP, r, c), dtype), compiler_params=_params(("parallel",)))(chip_arr, w, after)


def _gather_pieces(refs, n_split, c, me, cids):
    mine, theirs = [], []
    for k, ref in enumerate(refs):
        if k < n_split:
            rows = _my_rows(ref.shape[1], c)
            mine.append(ref.at[me, rows])
            theirs.append([ref.at[cid, rows] for cid in cids])
        else:
            mine.append(ref.at[me])
            theirs.append([ref.at[cid] for cid in cids])
    return mine, theirs


def gather_start(bufs, n_split, after, *, name):
    n = len(bufs)

    def body(*refs):
        ins, send, recv, token = refs[:n], refs[n + 1], refs[n + 2], refs[-1]
        x, y, c, me, sib, chips, cids = _place()
        mine, _ = _gather_pieces(ins, n_split, c, me, cids)
        for k in range(n):
            for j, chip in enumerate(chips):
                _remote(mine[k], mine[k], send.at[3 * k + j], recv.at[3 * k + j], (*chip, c)).start()
        token[...] = jnp.zeros_like(token)

    out = pl.pallas_call(
        body, name=name, in_specs=[HBM_SPEC] * n + [ANY_SPEC],
        out_specs=[SEM_SPEC, SEM_SPEC] + [HBM_SPEC] * n + [pl.BlockSpec(memory_space=pltpu.VMEM)],
        out_shape=[pltpu.SemaphoreType.DMA((3 * n,)), pltpu.SemaphoreType.DMA((3 * n,))]
        + [pltpu.HBM(b.shape, b.dtype) for b in bufs] + [SDS((8, 128), F32)],
        input_output_aliases={k: k + 2 for k in range(n)},
        compiler_params=pltpu.CompilerParams(has_side_effects=DATAFLOW))(*[_in_hbm(b) for b in bufs], after)
    return out[0], out[1], list(out[2:2 + n]), out[-1]


def gather_wait(send, recv, bufs, n_split, after, *, name):
    n = len(bufs)

    def body(*refs):
        ins, send_ref, recv_ref = refs[:n], refs[n], refs[n + 1]
        x, y, c, me, sib, chips, cids = _place()
        mine, theirs = _gather_pieces(ins, n_split, c, me, cids)
        for k in range(n):
            for j in range(3):
                _remote(mine[k], mine[k], send_ref.at[3 * k + j], recv_ref.at[3 * k + j], sib).wait_send()
                _remote(theirs[k][j], theirs[k][j], send_ref.at[3 * k + j], recv_ref.at[3 * k + j], sib).wait_recv()

    return list(pl.pallas_call(
        body, name=name, in_specs=[HBM_SPEC] * n + [SEM_SPEC, SEM_SPEC, ANY_SPEC], out_specs=[HBM_SPEC] * n,
        out_shape=[pltpu.HBM(b.shape, b.dtype) for b in bufs], input_output_aliases={k: k for k in range(n)},
        compiler_params=pltpu.CompilerParams(has_side_effects=DATAFLOW))(*bufs, send, recv, after))


def gather_forward(bufs, *, name):
    n = len(bufs)

    def body(*refs):
        outs, (send, recv) = refs[n:2 * n], refs[2 * n:]
        x, y, c, me, sib, chips, cids = _place()
        cps = []
        for k in range(n):
            for j in range(3):
                piece = outs[k].at[cids[j], _my_rows(outs[k].shape[1], c)]
                cps.append(_remote(piece, piece, send.at[3 * k + j], recv.at[3 * k + j], sib))
                cps[-1].start()
        for k in range(n):
            for j in range(3):
                piece = outs[k].at[cids[j], _my_rows(outs[k].shape[1], c, mine=False)]
                _remote(piece, piece, send.at[3 * k + j], recv.at[3 * k + j], sib).wait_recv()
        for cp in cps:
            cp.wait_send()

    return list(pl.pallas_call(
        body, name=name, in_specs=[HBM_SPEC] * n, out_specs=[HBM_SPEC] * n, out_shape=[SDS(b.shape, b.dtype) for b in bufs],
        input_output_aliases={k: k for k in range(n)}, scratch_shapes=[pltpu.SemaphoreType.DMA((3 * n,))] * 2)(*bufs))


def reduce_exchange(gs, *, name):
    n = len(gs)

    def body(*refs):
        ins, outs, (send, recv) = refs[:n], refs[n:2 * n], refs[2 * n:]
        x, y, c, me, sib, chips, cids = _place()
        cps = [_remote(ins[k].at[pl.ds(0, N_CHIP), _my_rows(ins[k].shape[1], c, mine=False)], outs[k], send.at[k],
                       recv.at[k], sib) for k in range(n)]
        for cp in cps:
            cp.start()
        for cp in cps:
            cp.wait()

    return list(pl.pallas_call(
        body, name=name, in_specs=[HBM_SPEC] * n, out_specs=[HBM_SPEC] * n,
        out_shape=[SDS((N_CHIP, g.shape[1] // 2, g.shape[2]), g.dtype) for g in gs],
        scratch_shapes=[pltpu.SemaphoreType.DMA((n,))] * 2)(*gs))


def _half_tile(half):
    return next(cand for cand in (256, 176, 128, 64, 32, 16) if half % cand == 0)


def reduce_add(g, r, c_arr, *, name):
    _, rows, cdim = g.shape
    half = rows // 2
    tr = _half_tile(half)

    def body(c_ref, g_ref, r_ref, o_ref):
        o_ref[...] = (g_ref[...] + r_ref[...]).astype(o_ref.dtype)

    return pl.pallas_call(
        body, name=name,
        grid_spec=pltpu.PrefetchScalarGridSpec(
            num_scalar_prefetch=1, grid=(N_CHIP, half // tr),
            in_specs=[pl.BlockSpec((None, tr, cdim), lambda j, i, c: (j, c[0] * (half // tr) + i, 0)),
                      pl.BlockSpec((None, tr, cdim), lambda j, i, c: (j, i, 0))],
            out_specs=pl.BlockSpec((None, tr, cdim), lambda j, i, c: (j, i, 0))),
        out_shape=SDS((N_CHIP, half, cdim), BF16), compiler_params=_params(("parallel", "parallel")))(c_arr, g, r)


def reduce_start(ss, *, name):
    n = len(ss)

    def body(*refs):
        ins, lands, send, recv, token = refs[:n], refs[n:2 * n], refs[2 * n], refs[2 * n + 1], refs[-1]
        x, y, c, me, sib, chips, cids = _place()
        for k in range(n):
            for j, chip in enumerate(chips):
                _remote(ins[k].at[cids[j]], lands[k].at[j], send.at[3 * k + j], recv.at[3 * k + j], (*chip, c)).start()
        token[...] = jnp.zeros_like(token)

    lands = [_in_hbm(lax.empty((N_CHIP - 1,) + s.shape[1:], s.dtype)) for s in ss]
    out = pl.pallas_call(
        body, name=name, in_specs=[HBM_SPEC] * (2 * n),
        out_specs=[SEM_SPEC, SEM_SPEC] + [HBM_SPEC] * (2 * n) + [pl.BlockSpec(memory_space=pltpu.VMEM)],
        out_shape=[pltpu.SemaphoreType.DMA((3 * n,)), pltpu.SemaphoreType.DMA((3 * n,))]
        + [pltpu.HBM(b.shape, b.dtype) for b in list(ss) + lands] + [SDS((8, 128), F32)],
        input_output_aliases={k: k + 2 for k in range(2 * n)},
        compiler_params=pltpu.CompilerParams(has_side_effects=DATAFLOW))(*[_in_hbm(s) for s in ss], *lands)
    return out[0], out[1], list(out[2:2 + n]), list(out[2 + n:2 + 2 * n]), out[-1]


def reduce_wait(send, recv, ss, lands, after, *, name):
    n = len(ss)

    def body(*refs):
        ins, land_refs, send_ref, recv_ref = refs[:n], refs[n:2 * n], refs[2 * n], refs[2 * n + 1]
        x, y, c, me, sib, chips, cids = _place()
        for k in range(n):
            for j in range(3):
                _remote(ins[k].at[cids[j]], land_refs[k].at[j], send_ref.at[3 * k + j], recv_ref.at[3 * k + j],
                        sib).wait_send()
                _remote(ins[k].at[cids[j]], land_refs[k].at[j], send_ref.at[3 * k + j], recv_ref.at[3 * k + j],
                        sib).wait_recv()

    out = pl.pallas_call(
        body, name=name, in_specs=[HBM_SPEC] * (2 * n) + [SEM_SPEC, SEM_SPEC, ANY_SPEC], out_specs=[HBM_SPEC] * (2 * n),
        out_shape=[pltpu.HBM(b.shape, b.dtype) for b in list(ss) + list(lands)],
        input_output_aliases={k: k for k in range(2 * n)},
        compiler_params=pltpu.CompilerParams(has_side_effects=DATAFLOW))(*ss, *lands, send, recv, after)
    return list(out[:n]), list(out[n:])


def reduce_sum(own, land, place_arr, layer, acc, *, name):
    _, half, cdim = land.shape
    tr = _half_tile(half)

    def body(p_ref, own_ref, land_ref, *rest):
        o_ref = rest[-1]
        o_ref[...] = ((own_ref[...].astype(F32) + land_ref[0].astype(F32)) + land_ref[1].astype(F32)) + land_ref[2].astype(F32)

    in_specs = [pl.BlockSpec((None, tr, cdim), lambda i, p: (p[0], i, 0)),
                pl.BlockSpec((N_CHIP - 1, tr, cdim), lambda i, p: (0, i, 0))]
    args = [place_arr, own, land]
    if acc is not None:
        in_specs.append(ANY_SPEC)
        args.append(acc)
    return pl.pallas_call(
        body, name=name,
        grid_spec=pltpu.PrefetchScalarGridSpec(
            num_scalar_prefetch=1, grid=(half // tr,), in_specs=in_specs,
            out_specs=pl.BlockSpec((None, tr, cdim), lambda i, p: (layer, p[1] * (half // tr) + i, 0))),
        out_shape=SDS((N_LAYER, 2 * half, cdim), F32), input_output_aliases={} if acc is None else {3: 0},
        compiler_params=_params(("parallel",)))(*args)


def reduce_share(fs, *, name):
    n = len(fs)

    def body(*refs):
        outs, (send, recv) = refs[n:2 * n], refs[2 * n:]
        x, y, c, me, sib, chips, cids = _place()
        cps = []
        for k in range(n):
            piece = outs[k].at[pl.ds(0, N_LAYER), _my_rows(outs[k].shape[1], c)]
            cps.append(_remote(piece, piece, send.at[k], recv.at[k], sib))
            cps[-1].start()
        for k in range(n):
            theirs = outs[k].at[pl.ds(0, N_LAYER), _my_rows(outs[k].shape[1], c, mine=False)]
            _remote(theirs, theirs, send.at[k], recv.at[k], sib).wait_recv()
        for cp in cps:
            cp.wait_send()

    return list(pl.pallas_call(
        body, name=name, in_specs=[HBM_SPEC] * n, out_specs=[HBM_SPEC] * n, out_shape=[SDS(f.shape, f.dtype) for f in fs],
        input_output_aliases={k: k for k in range(n)}, scratch_shapes=[pltpu.SemaphoreType.DMA((n,))] * 2)(*fs))


WEIGHTS = ("ffn1_norm", "ffn1_w_gate", "ffn1_w_up", "ffn1_w_down", "mix_norm", "w_in", "lru_conv_w", "lru_conv_b", "lru_w_a",
           "lru_b_a", "lru_w_x", "lru_b_x", "lru_lambda", "attn_sinks", "rel_bias", "dn_conv_w", "dn_a_log", "dn_dt_bias",
           "dn_norm", "w_out", "ffn2_norm", "ffn2_w_gate", "ffn2_w_up", "ffn2_w_down", "ple_norm", "ple_w_gate",
           "ple_w_proj", "final_norm")
CONV_SHARDED = ("lru_conv_w", "dn_conv_w")
SMALL = tuple(k for k in WEIGHTS if k not in SHARDED)


def _pack(arrs):
    flat = []
    for a in arrs:
        v = a.reshape(-1)
        flat.append(jnp.pad(v, (0, -v.shape[0] % 128)))
    v = jnp.concatenate(flat)
    v = jnp.pad(v, (0, -v.shape[0] % 1024))
    return v.reshape(-1, 128)


def _unpack(buf, shapes):
    v, out, off = buf.reshape(-1), [], 0
    for s in shapes:
        n = int(np.prod(s))
        out.append(v[off:off + n].reshape(s))
        off += n + (-n % 128)
    return out


def _chip_cols(a):
    n, l, r, c = a.shape
    return a.transpose(1, 2, 0, 3).reshape(l, r, n * c)


def _chip_rows(a):
    n, l, r, c = a.shape
    return a.transpose(1, 0, 2, 3).reshape(l, n * r, c)


def kernel(x, p, ffn1_norm, ffn1_w_gate, ffn1_w_up, ffn1_w_down, mix_norm, w_in, lru_conv_w, lru_conv_b, lru_w_a, lru_b_a, lru_w_x, lru_b_x, lru_lambda, attn_sinks, rel_bias, dn_conv_w, dn_a_log, dn_dt_bias, dn_norm, w_out, ffn2_norm, ffn2_w_gate, ffn2_w_up, ffn2_w_down, ple_norm, ple_w_gate, ple_w_proj, final_norm, loss_target, m_ffn1_norm, m_ffn1_w_gate, m_ffn1_w_up, m_ffn1_w_down, m_mix_norm, m_w_in, m_lru_conv_w, m_lru_conv_b, m_lru_w_a, m_lru_b_a, m_lru_w_x, m_lru_b_x, m_lru_lambda, m_attn_sinks, m_rel_bias, m_dn_conv_w, m_dn_a_log, m_dn_dt_bias, m_dn_norm, m_w_out, m_ffn2_norm, m_ffn2_w_gate, m_ffn2_w_up, m_ffn2_w_down, m_ple_norm, m_ple_w_gate, m_ple_w_proj, m_final_norm, v_ffn1_norm, v_ffn1_w_gate, v_ffn1_w_up, v_ffn1_w_down, v_mix_norm, v_w_in, v_lru_conv_w, v_lru_conv_b, v_lru_w_a, v_lru_b_a, v_lru_w_x, v_lru_b_x, v_lru_lambda, v_attn_sinks, v_rel_bias, v_dn_conv_w, v_dn_a_log, v_dn_dt_bias, v_dn_norm, v_w_out, v_ffn2_norm, v_ffn2_w_gate, v_ffn2_w_up, v_ffn2_w_down, v_ple_norm, v_ple_w_gate, v_ple_w_proj, v_final_norm):
    given = dict(locals())
    ws = {k: given[k] for k in WEIGHTS}
    ms = {k: given["m_" + k] for k in WEIGHTS}
    vs = {k: given["v_" + k] for k in WEIGHTS}
    nb, seq, d = x.shape
    t = nb * seq
    cx, cy, cc = lax.axis_index("x"), lax.axis_index("y"), lax.axis_index("c")
    chip = 2 * cx + cy

    chip_arr = chip.astype(jnp.int32).reshape(1)
    c_arr = cc.astype(jnp.int32).reshape(1)
    place_arr = jnp.stack([chip, cc]).astype(jnp.int32)
    groups = [(l, part) for l in range(N_LAYER) for part in range(len(WEIGHT_PARTS))]
    placed, started = {}, {}

    def place_group(i, after):
        l, part = groups[i]
        for k in WEIGHT_PARTS[part]:
            placed[l, k] = place_layer_shard(ws[k], l, chip_arr, F32 if k in CONV_SHARDED else BF16, after,
                                             name=f"place_{k}_{l}")

    def start_group(i, after):
        l, part = groups[i]
        ks = WEIGHT_PARTS[part]
        n_split = sum(k in SHARDED for k in ks)
        started[i] = (ks, n_split) + gather_start([placed[l, k] for k in ks], n_split, after, name=f"gather_start_{l}_{part}")

    place_group(0, jnp.zeros((8, 128), F32))
    start_group(0, jnp.zeros((8, 128), F32))
    for i in range(1, len(groups)):
        place_group(i, started[0][-1])

    def layer_weights(l, part, h):
        i = groups.index((l, part))
        ks, n_split, send, recv, bufs, _ = started[i]
        bufs = gather_wait(send, recv, bufs, n_split, h, name=f"gather_wait_{l}_{part}")
        tie = jnp.zeros((8, 128), F32)
        for nxt in [j for j in range(i + 1, len(groups)) if j not in started and groups[j][0] == groups[min(i + 1, len(groups) - 1)][0]]:
            start_group(nxt, bufs[0] if nxt == i + 1 else started[nxt - 1][-1])
            tie = started[nxt][-1]
        wl = dict(zip(ks, gather_forward(bufs[:n_split], name=f"gather_forward_{l}_{part}") + bufs[n_split:]))
        for k in ("w_in", "ple_w_proj", "lru_conv_w", "dn_conv_w"):
            if k in wl:
                wl[k] = wl[k].transpose(1, 0, 2).reshape(wl[k].shape[1], -1)
        for k in ("w_out", "ple_w_gate"):
            if k in wl:
                wl[k] = wl[k].reshape(-1, wl[k].shape[-1])
        if "w_in" in wl:
            wl["w_in"] = jnp.pad(wl["w_in"], ((0, 0), (0, D_IN_PAD - D_IN)))
        wl[f"tie{part}"] = tie
        return wl

    pending, finished = [], {k: None for k in SHARDED}

    def finish_reduce(after):
        ks, send, recv, sums, lands, l, part = pending.pop(0)
        sums, lands = reduce_wait(send, recv, sums, lands, after, name=f"reduce_wait_{l}_{part}")
        for k, s, land in zip(ks, sums, lands):
            finished[k] = reduce_sum(s, land, place_arr, l, finished[k], name=f"reduce_sum_{k}_{l}")

    def layer_grads(l, part, g, dh):
        ks = GRAD_PARTS[part]
        gs = [g[k] for k in ks]
        theirs = reduce_exchange(gs, name=f"reduce_exchange_{l}_{part}")
        sums = [reduce_add(a, b, c_arr, name=f"reduce_add_{k}_{l}") for k, a, b in zip(ks, gs, theirs)]
        send, recv, sums, lands, token = reduce_start(sums, name=f"reduce_start_{l}_{part}")
        pending.append((ks, send, recv, sums, lands, l, part))
        while len(pending) > 2:
            finish_reduce(dh)
        return token

    small_w = {k: ws[k] for k in SMALL if k not in CONV_SHARDED}
    bmap = jnp.asarray(_rel_bucket_map())
    loss, gx, grads = local_step(x.reshape(t, d), p.reshape(N_LAYER, t, PLE_DIM), loss_target.reshape(t, d), small_w,
                                 layer_weights, layer_grads, bmap, nb, seq)
    g_out, delta, new_m, new_v = {}, {}, {}, {}

    small_shapes = [grads[k].shape for k in SMALL]
    g_small = dict(zip(SMALL, _unpack(allreduce_small(_pack([grads[k] for k in SMALL]), name="allreduce_small"), small_shapes)))
    for k in CONV_SHARDED:
        width = ws[k].shape[-1]
        g_small[k] = lax.dynamic_slice_in_dim(g_small[k], chip * width, width, axis=2)
    g_out.update(g_small)
    shapes = [ws[k].shape for k in SMALL]
    res = adamw(*[_pack([src[k] for k in SMALL]) for src in (ws, g_out, ms, vs)], name="adamw_small")
    for dst, r in zip((delta, new_m, new_v), res):
        dst.update(zip(SMALL, _unpack(r, shapes)))

    after = res[0]
    for part, ks in enumerate(GRAD_PARTS):
        while pending and pending[0][0] == ks:
            finish_reduce(after)
        g_out.update(zip(ks, reduce_share([finished[k] for k in ks], name=f"reduce_share_{part}")))
        for k in ks:
            two_d = lambda a: a.reshape(-1, a.shape[-1])
            res = adamw(two_d(ws[k]), two_d(g_out[k]), two_d(ms[k]), two_d(vs[k]), name=f"adamw_{k}")
            delta[k], new_m[k], new_v[k] = (r.reshape(ws[k].shape) for r in res)
        after = res[0]

    total = lax.psum(loss[0, 0], ("x", "y", "c"))
    return (total, gx.reshape(nb, seq, d), *[g_out[k] for k in WEIGHTS], *[delta[k] for k in WEIGHTS],
            *[new_m[k] for k in WEIGHTS], *[new_v[k] for k in WEIGHTS])
```

```python
import functools
import math

import numpy as np
import jax
import jax.numpy as jnp
from jax import lax
from jax.experimental import pallas as pl
from jax.experimental.pallas import tpu as pltpu

F32 = jnp.float32
BF16 = jnp.bfloat16

EPS = 1e-6
D_MODEL = 1024
D_FF = 2816
N_CHIP = 4
FF_BLK = D_FF // N_CHIP
HEAD = 64
LRU_W = 256
ATT_W = 512
ATT_HEADS = 8
KV_HEADS = 2
ATT_GROUP = 4
BLOCK_Q = 128
DN_HEADS = 4
DN_CHUNK = 64
D_IN = 2312
D_IN_PAD = 2560
PLE_DIM = 256
REL_BUCKETS = 32
LRU_C = 8.0
N_LAYER = 2

ADAM_LR, ADAM_B1, ADAM_B2, ADAM_EPS, ADAM_WD, ADAM_STEP = 0.001, 0.9, 0.999, 1e-08, 0.01, 10

VMEM_LIMIT = 56 << 20
MESH = pl.DeviceIdType.MESH
SDS = jax.ShapeDtypeStruct


def _dot(a, b, ca=1, cb=0, hi=False):
    dims = (((ca,), (cb,)), ((), ()))
    one = lambda u, v: lax.dot_general(u, v, dims, preferred_element_type=F32)
    a_hi, b_hi = a.astype(BF16), b.astype(BF16)
    if not hi:
        return one(a_hi, b_hi)
    a_lo = (a - a_hi.astype(F32)).astype(BF16)
    b_lo = (b - b_hi.astype(F32)).astype(BF16)
    return one(a_hi, b_hi) + (one(a_hi, b_lo) + one(a_lo, b_hi))


def _nn(a, b, hi=False):
    return _dot(a, b, 1, 0, hi)


def _nt(a, b, hi=False):
    return _dot(a, b, 1, 1, hi)


def _tn(a, b, hi=False):
    return _dot(a, b, 0, 0, hi)


def _sigmoid(x):
    return jax.nn.sigmoid(x)


def _softplus(x):
    return jnp.maximum(x, 0.0) + jnp.log1p(jnp.exp(-jnp.abs(x)))


def _neg_expm1(z):
    series = -z * (1.0 + z * (0.5 + z * (1.0 / 6.0 + z * (1.0 / 24.0 + z * (1.0 / 120.0)))))
    return jnp.where(z > -0.05, series, 1.0 - jnp.exp(z))


_GELU_C = math.sqrt(2.0 / math.pi)


def _gelu(x):
    t = jnp.tanh(_GELU_C * (x + 0.044715 * x * x * x))
    return 0.5 * x * (1.0 + t), t


def _gelu_grad(x, t):
    return 0.5 * (1.0 + t) + 0.5 * x * (1.0 - t * t) * _GELU_C * (1.0 + 3.0 * 0.044715 * x * x)


def _rms_fwd(h, g):
    r = lax.rsqrt(jnp.mean(h * h, axis=-1, keepdims=True) + EPS)
    xh = h * r
    return xh * g, xh, r


def _rms_bwd(dn, xh, r, g):
    dxh = dn * g
    dh = r * (dxh - xh * jnp.mean(dxh * xh, axis=-1, keepdims=True))
    return dh, jnp.sum(dn * xh, axis=0, keepdims=True)


def _shift_down(x, d, fill=0.0):
    row = lax.broadcasted_iota(jnp.int32, x.shape, 0)
    return jnp.where(row >= d, pltpu.roll(x, d, 0), fill)


def _shift_up(x, d, fill=0.0):
    n = x.shape[0]
    row = lax.broadcasted_iota(jnp.int32, x.shape, 0)
    return jnp.where(row < n - d, pltpu.roll(x, n - d, 0), fill)


def _conv_fwd(x, w):
    y = x * w[3]
    for k in range(3):
        y = y + _shift_down(x, 3 - k) * w[k]
    return y


def _conv_bwd(dy, x, w):
    dx = dy * w[3]
    rows = [None] * 4
    rows[3] = jnp.sum(dy * x, axis=0, keepdims=True)
    for k in range(3):
        dx = dx + _shift_up(dy, 3 - k) * w[k]
        rows[k] = jnp.sum(dy * _shift_down(x, 3 - k), axis=0, keepdims=True)
    r4 = lax.broadcasted_iota(jnp.int32, (4, x.shape[1]), 0)
    dw = jnp.zeros((4, x.shape[1]), F32)
    for k in range(4):
        dw = jnp.where(r4 == k, rows[k], dw)
    return dx, dw


FFN_SPLIT = 2


def _interleave(gens):
    pending = list(gens)
    while pending:
        for g in list(pending):
            if next(g, StopIteration) is StopIteration:
                pending.remove(g)


def _params(sem=None, vmem=VMEM_LIMIT):
    return pltpu.CompilerParams(dimension_semantics=sem, vmem_limit_bytes=vmem)


def _whole(shape):
    nd = len(shape)
    return pl.BlockSpec(shape, lambda *_: (0,) * nd)


def matmul(a, b, *, name, ta=False, tb=False, residual=None, out_dtype=F32, tm=512, tn=512, tk=512):
    m, k = (a.shape[1], a.shape[0]) if ta else a.shape
    n = b.shape[0] if tb else b.shape[1]
    tm, tn, tk = min(tm, m), min(tn, n), min(tk, k)
    assert m % tm == 0 and n % tn == 0 and k % tk == 0, (m, n, k, tm, tn, tk)
    nk = k // tk

    def body(*refs):
        if residual is None:
            a_ref, b_ref, o_ref, acc = refs
        else:
            a_ref, b_ref, r_ref, o_ref, acc = refs
        kk = pl.program_id(2)

        @pl.when(kk == 0)
        def _():
            acc[...] = jnp.zeros_like(acc)

        acc[...] += _dot(a_ref[...], b_ref[...], 0 if ta else 1, 1 if tb else 0)

        @pl.when(kk == nk - 1)
        def _():
            out = acc[...]
            if residual is not None:
                out = out + r_ref[...]
            o_ref[...] = out.astype(out_dtype)

    a_spec = pl.BlockSpec((tk, tm), lambda i, j, kk: (kk, i)) if ta else pl.BlockSpec((tm, tk), lambda i, j, kk: (i, kk))
    b_spec = pl.BlockSpec((tn, tk), lambda i, j, kk: (j, kk)) if tb else pl.BlockSpec((tk, tn), lambda i, j, kk: (kk, j))
    o_spec = pl.BlockSpec((tm, tn), lambda i, j, kk: (i, j))
    in_specs, args = [a_spec, b_spec], [a, b]
    if residual is not None:
        in_specs.append(o_spec)
        args.append(residual)
    return pl.pallas_call(
        body, name=name, grid=(m // tm, n // tn, nk), in_specs=in_specs, out_specs=o_spec,
        out_shape=SDS((m, n), out_dtype), scratch_shapes=[pltpu.VMEM((tm, tn), F32)],
        compiler_params=_params(("parallel", "parallel", "arbitrary")))(*args)


def norm_matmul(h, gain, w, *, name, tm=512, tn=512):
    t, d = h.shape
    tm = min(tm, t)
    n = w.shape[1]
    assert t % tm == 0 and n % tn == 0

    def body(h_ref, g_ref, w_ref, u_ref, n_ref):
        @pl.when(pl.program_id(1) == 0)
        def _():
            n_ref[...] = _rms_fwd(h_ref[...], g_ref[...])[0].astype(BF16)

        u_ref[...] = _nn(n_ref[...], w_ref[...])

    return pl.pallas_call(
        body, name=name, grid=(t // tm, n // tn),
        in_specs=[pl.BlockSpec((tm, d), lambda i, j: (i, 0)), _whole((1, d)), pl.BlockSpec((d, tn), lambda i, j: (0, j))],
        out_specs=[pl.BlockSpec((tm, tn), lambda i, j: (i, j)), pl.BlockSpec((tm, d), lambda i, j: (i, 0))],
        out_shape=[SDS((t, n), F32), SDS((t, d), BF16)],
        compiler_params=_params(("parallel", "arbitrary")))(h, gain, w)


def rms_bwd(h, gain, dn, dres, *, name, tm=512):
    t, d = h.shape
    tm = min(tm, t)

    def body(h_ref, g_ref, dn_ref, dr_ref, dh_ref, dg_ref):
        @pl.when(pl.program_id(0) == 0)
        def _():
            dg_ref[...] = jnp.zeros_like(dg_ref)

        g = g_ref[...]
        _, xh, r = _rms_fwd(h_ref[...], g)
        dh, dg = _rms_bwd(dn_ref[...], xh, r, g)
        dh_ref[...] = dr_ref[...] + dh
        dg_ref[...] += dg

    row = pl.BlockSpec((tm, d), lambda i: (i, 0))
    return pl.pallas_call(
        body, name=name, grid=(t // tm,), in_specs=[row, _whole((1, d)), row, row],
        out_specs=[row, _whole((1, d))], out_shape=[SDS((t, d), F32), SDS((1, d), F32)],
        compiler_params=_params(("arbitrary",)))(h, gain, dn, dres)


def ffn_fwd(h, gain, wg, wu, wd, *, name, tm=512):
    t, d = h.shape
    tm = min(tm, t)

    def body(h_ref, g_ref, wg_ref, wu_ref, wd_ref, o_ref, n_sc, acc):
        j = pl.program_id(1)

        @pl.when(j == 0)
        def _():
            n_sc[...] = _rms_fwd(h_ref[...], g_ref[...])[0].astype(BF16)
            acc[...] = jnp.zeros_like(acc)

        def part(rows):
            n = n_sc[rows, :]
            a = _nt(n, wg_ref[...])
            b = _nt(n, wu_ref[...])
            yield
            acc[rows, :] += _nn(a * _sigmoid(a) * b, wd_ref[...])

        _interleave([part(pl.ds(k * (tm // FFN_SPLIT), tm // FFN_SPLIT)) for k in range(FFN_SPLIT)])

        @pl.when(j == N_CHIP - 1)
        def _():
            o_ref[...] = h_ref[...] + 0.5 * acc[...]

    row = pl.BlockSpec((tm, d), lambda i, j: (i, 0))
    return pl.pallas_call(
        body, name=name, grid=(t // tm, N_CHIP),
        in_specs=[row, _whole((1, d)),
                  pl.BlockSpec((None, FF_BLK, d), lambda i, j: (j, 0, 0)),
                  pl.BlockSpec((None, FF_BLK, d), lambda i, j: (j, 0, 0)),
                  pl.BlockSpec((None, FF_BLK, d), lambda i, j: (j, 0, 0))],
        out_specs=row, out_shape=SDS((t, d), F32),
        scratch_shapes=[pltpu.VMEM((tm, d), BF16), pltpu.VMEM((tm, d), F32)],
        compiler_params=_params(("parallel", "arbitrary")))(h, gain, wg, wu, wd)


def ffn_bwd_act(h, gain, dout, wg, wu, wd, *, name, tm=512):
    t, d = h.shape
    tm = min(tm, t)

    def body(h_ref, g_ref, do_ref, wg_ref, wu_ref, wd_ref, dh_ref, n_ref, da_ref, db_ref, s_ref, dg_ref, dn_acc):
        i, j = pl.program_id(0), pl.program_id(1)

        @pl.when((i == 0) & (j == 0))
        def _():
            dg_ref[...] = jnp.zeros_like(dg_ref)

        @pl.when(j == 0)
        def _():
            n_ref[...] = _rms_fwd(h_ref[...], g_ref[...])[0].astype(BF16)
            dn_acc[...] = jnp.zeros_like(dn_acc)

        def part(rows):
            n = n_ref[rows, :]
            a = _nt(n, wg_ref[...])
            b = _nt(n, wu_ref[...])
            ds = _nt(0.5 * do_ref[rows, :], wd_ref[...])
            yield
            sig = _sigmoid(a)
            sa = a * sig
            db = ds * sa
            da = ds * b * (sig * (1.0 + a * (1.0 - sig)))
            s_ref[rows, :] = (sa * b).astype(BF16)
            da_ref[rows, :] = da.astype(BF16)
            db_ref[rows, :] = db.astype(BF16)
            yield
            dn_acc[rows, :] += _nn(da, wg_ref[...]) + _nn(db, wu_ref[...])

        _interleave([part(pl.ds(k * (tm // FFN_SPLIT), tm // FFN_SPLIT)) for k in range(FFN_SPLIT)])

        @pl.when(j == N_CHIP - 1)
        def _():
            g = g_ref[...]
            _, xh, r = _rms_fwd(h_ref[...], g)
            dh, dg = _rms_bwd(dn_acc[...], xh, r, g)
            dh_ref[...] = do_ref[...] + dh
            dg_ref[...] += dg

    row = pl.BlockSpec((tm, d), lambda i, j: (i, 0))
    blk = pl.BlockSpec((None, tm, FF_BLK), lambda i, j: (j, i, 0))
    act = SDS((N_CHIP, t, FF_BLK), BF16)
    return pl.pallas_call(
        body, name=name, grid=(t // tm, N_CHIP),
        in_specs=[row, _whole((1, d)), row,
                  pl.BlockSpec((None, FF_BLK, d), lambda i, j: (j, 0, 0)),
                  pl.BlockSpec((None, FF_BLK, d), lambda i, j: (j, 0, 0)),
                  pl.BlockSpec((None, FF_BLK, d), lambda i, j: (j, 0, 0))],
        out_specs=[row, row, blk, blk, blk, _whole((1, d))],
        out_shape=[SDS((t, d), F32), SDS((t, d), BF16), act, act, act, SDS((1, d), F32)],
        scratch_shapes=[pltpu.VMEM((tm, d), F32)],
        compiler_params=_params(("arbitrary", "arbitrary")))(h, gain, dout, wg, wu, wd)


def ffn_bwd_w(n, da, db, s, dout, *, name, tk=512):
    t, d = n.shape
    tk = min(tk, t)

    def body(n_ref, da_ref, db_ref, s_ref, do_ref, dwg_ref, dwu_ref, dwd_ref):
        @pl.when(pl.program_id(1) == 0)
        def _():
            dwg_ref[...] = jnp.zeros_like(dwg_ref)
            dwu_ref[...] = jnp.zeros_like(dwu_ref)
            dwd_ref[...] = jnp.zeros_like(dwd_ref)

        nn = n_ref[...]
        dwg_ref[...] += _tn(da_ref[...], nn)
        dwu_ref[...] += _tn(db_ref[...], nn)
        dwd_ref[...] += _tn(s_ref[...], 0.5 * do_ref[...])

    row = pl.BlockSpec((tk, d), lambda j, kk: (kk, 0))
    blk = pl.BlockSpec((None, tk, FF_BLK), lambda j, kk: (j, kk, 0))
    return pl.pallas_call(
        body, name=name, grid=(N_CHIP, t // tk), in_specs=[row, blk, blk, blk, row],
        out_specs=[pl.BlockSpec((None, FF_BLK, d), lambda j, kk: (j, 0, 0)),
                   pl.BlockSpec((None, FF_BLK, d), lambda j, kk: (j, 0, 0)),
                   pl.BlockSpec((None, FF_BLK, d), lambda j, kk: (j, 0, 0))],
        out_shape=[SDS((N_CHIP, FF_BLK, d), F32)] * 3,
        compiler_params=_params(("parallel", "arbitrary")))(n, da, db, s, dout)


def ple_fwd(h, gain, wpg, pl_in, wpp, *, name, tm=512):
    t, d = h.shape
    tm = min(tm, t)
    pd = pl_in.shape[1]

    def body(h_ref, g_ref, wpg_ref, p_ref, wpp_ref, o_ref):
        hh = h_ref[...]
        n = _rms_fwd(hh, g_ref[...])[0]
        gate = _sigmoid(_nn(n, wpg_ref[...]))
        o_ref[...] = hh + gate * _nn(p_ref[...], wpp_ref[...])

    row = pl.BlockSpec((tm, d), lambda i: (i, 0))
    return pl.pallas_call(
        body, name=name, grid=(t // tm,),
        in_specs=[row, _whole((1, d)), _whole((d, d)), pl.BlockSpec((tm, pd), lambda i: (i, 0)), _whole((pd, d))],
        out_specs=row, out_shape=SDS((t, d), F32), compiler_params=_params(("parallel",)))(h, gain, wpg, pl_in, wpp)


def ple_bwd(h, gain, wpg, pl_in, wpp, dout, *, name, tm=512):
    t, d = h.shape
    tm = min(tm, t)
    pd = pl_in.shape[1]

    def body(h_ref, g_ref, wpg_ref, p_ref, wpp_ref, do_ref, dh_ref, n_ref, dga_ref, dpp_ref, dg_ref):
        @pl.when(pl.program_id(0) == 0)
        def _():
            dg_ref[...] = jnp.zeros_like(dg_ref)

        g = g_ref[...]
        n, xh, r = _rms_fwd(h_ref[...], g)
        gate = _sigmoid(_nn(n, wpg_ref[...]))
        pp = _nn(p_ref[...], wpp_ref[...])
        do = do_ref[...]
        dga = do * pp * gate * (1.0 - gate)
        dh, dg = _rms_bwd(_nt(dga, wpg_ref[...]), xh, r, g)
        dh_ref[...] = do + dh
        n_ref[...] = n.astype(BF16)
        dga_ref[...] = dga.astype(BF16)
        dpp_ref[...] = (do * gate).astype(BF16)
        dg_ref[...] += dg

    row = pl.BlockSpec((tm, d), lambda i: (i, 0))
    return pl.pallas_call(
        body, name=name, grid=(t // tm,),
        in_specs=[row, _whole((1, d)), _whole((d, d)), pl.BlockSpec((tm, pd), lambda i: (i, 0)), _whole((pd, d)), row],
        out_specs=[row, row, row, row, _whole((1, d))],
        out_shape=[SDS((t, d), F32), SDS((t, d), BF16), SDS((t, d), BF16), SDS((t, d), BF16), SDS((1, d), F32)],
        compiler_params=_params(("arbitrary",)))(h, gain, wpg, pl_in, wpp, dout)


def loss_head(h, gain, target, *, name, tm=512):
    t, d = h.shape
    tm = min(tm, t)

    def body(h_ref, g_ref, t_ref, dh_ref, dg_ref, l_ref):
        @pl.when(pl.program_id(0) == 0)
        def _():
            dg_ref[...] = jnp.zeros_like(dg_ref)
            l_ref[...] = jnp.zeros_like(l_ref)

        g = g_ref[...]
        y, xh, r = _rms_fwd(h_ref[...], g)
        err = y - t_ref[...]
        l_ref[...] += 0.5 * jnp.sum(jnp.mean(err * err, axis=-1, keepdims=True), axis=0, keepdims=True)
        dh, dg = _rms_bwd(err * (1.0 / d), xh, r, g)
        dh_ref[...] = dh
        dg_ref[...] += dg

    row = pl.BlockSpec((tm, d), lambda i: (i, 0))
    return pl.pallas_call(
        body, name=name, grid=(t // tm,), in_specs=[row, _whole((1, d)), row],
        out_specs=[row, _whole((1, d)), _whole((1, 1))],
        out_shape=[SDS((t, d), F32), SDS((1, d), F32), SDS((1, 1), F32)],
        compiler_params=_params(("arbitrary",)))(h, gain, target)


def adamw(w, g, m, v, *, name):
    r, c = w.shape
    tr = r
    for cand in (512, 256, 128, 64, 32, 16, 8):
        if r % cand == 0:
            tr = cand
            break

    def body(w_ref, g_ref, m_ref, v_ref, d_ref, nm_ref, nv_ref):
        gg = g_ref[...]
        mm = ADAM_B1 * m_ref[...] + (1.0 - ADAM_B1) * gg
        vv = ADAM_B2 * v_ref[...] + (1.0 - ADAM_B2) * (gg * gg)
        m_hat = mm / (1.0 - ADAM_B1 ** ADAM_STEP)
        v_hat = vv / (1.0 - ADAM_B2 ** ADAM_STEP)
        d_ref[...] = -ADAM_LR * (m_hat / (jnp.sqrt(v_hat) + ADAM_EPS) + ADAM_WD * w_ref[...])
        nm_ref[...] = mm
        nv_ref[...] = vv

    blk = pl.BlockSpec((tr, c), lambda i: (i, 0))
    out = SDS((r, c), F32)
    return pl.pallas_call(body, name=name, grid=(r // tr,), in_specs=[blk] * 4, out_specs=[blk] * 3,
                          out_shape=[out, out, out], compiler_params=_params(("parallel",)))(w, g, m, v)


def _scan_fwd(a, b):
    d = 1
    while d < a.shape[0]:
        b = a * _shift_down(b, d, 0.0) + b
        a = a * _shift_down(a, d, 1.0)
        d *= 2
    return b


def _scan_rev(a, b):
    d = 1
    while d < a.shape[0]:
        b = a * _shift_up(b, d, 0.0) + b
        a = a * _shift_up(a, d, 1.0)
        d *= 2
    return b


LRU_HALF = 128


def _lru_in_specs(seq):
    half = LRU_W // LRU_HALF
    vec = pl.BlockSpec((1, LRU_HALF), lambda j, b: (0, j))
    mat = pl.BlockSpec((LRU_HALF, LRU_HALF), lambda j, b: (j, j))
    return [pl.BlockSpec((seq, LRU_HALF), lambda j, b: (b, j)), pl.BlockSpec((seq, LRU_HALF), lambda j, b: (b, half + j)),
            pl.BlockSpec((4, LRU_HALF), lambda j, b: (0, j)), vec, mat, vec, mat, vec, vec]


def _lru_math(x_ref, gate_ref, cw_ref, cb_ref, wa_ref, ba_ref, wx_ref, bx_ref, lam_ref):
    x = x_ref[...]
    gate = gate_ref[...]
    cw =[cw_ref[k:k + 1, :] for k in range(4)]
    xr = _conv_fwd(x, cw) + cb_ref[...]
    r = _sigmoid(_nn(xr, wa_ref[...]) + ba_ref[...])
    i = _sigmoid(_nn(xr, wx_ref[...]) + bx_ref[...])
    sp = _softplus(-lam_ref[...])
    log_a = -LRU_C * r * sp
    a = jnp.exp(log_a)
    mult = jnp.sqrt(_neg_expm1(2.0 * log_a))
    gi = i * xr
    h = _scan_fwd(a, mult * gi)
    gl, tg = _gelu(gate)
    return dict(x=x, gate=gate, cw=cw, xr=xr, r=r, i=i, sp=sp, a=a, mult=mult, gi=gi, h=h, gl=gl, tg=tg)


def lru_fwd(u, cw, cb, wa, ba, wx, bx, lam, *, seq, name):
    t = u.shape[0]

    def body(x_ref, gate_ref, cw_ref, cb_ref, wa_ref, ba_ref, wx_ref, bx_ref, lam_ref, y_ref):
        f = _lru_math(x_ref, gate_ref, cw_ref, cb_ref, wa_ref, ba_ref, wx_ref, bx_ref, lam_ref)
        y_ref[...] = f["gl"] * f["h"]

    return pl.pallas_call(
        body, name=name, grid=(LRU_W // LRU_HALF, t // seq), in_specs=_lru_in_specs(seq),
        out_specs=pl.BlockSpec((seq, LRU_HALF), lambda j, b: (b, j)), out_shape=SDS((t, LRU_W), F32),
        compiler_params=_params(("parallel", "parallel")))(u, u, cw, cb, wa, ba, wx, bx, lam)


def lru_bwd(u, cw, cb, wa, ba, wx, bx, lam, dy, *, seq, name):
    t = u.shape[0]

    def body(x_ref, gate_ref, cw_ref, cb_ref, wa_ref, ba_ref, wx_ref, bx_ref, lam_ref, dy_ref,
             dx_ref, dgate_ref, dcw_ref, dwa_ref, dwx_ref, dv_ref):
        @pl.when(pl.program_id(1) == 0)
        def _():
            dcw_ref[...] = jnp.zeros_like(dcw_ref)
            dwa_ref[...] = jnp.zeros_like(dwa_ref)
            dwx_ref[...] = jnp.zeros_like(dwx_ref)
            dv_ref[...] = jnp.zeros_like(dv_ref)

        f = _lru_math(x_ref, gate_ref, cw_ref, cb_ref, wa_ref, ba_ref, wx_ref, bx_ref, lam_ref)
        dy = dy_ref[...]
        a, h, xr, r, i, mult, gi, sp = f["a"], f["h"], f["xr"], f["r"], f["i"], f["mult"], f["gi"], f["sp"]
        dgate_ref[...] = dy * h * _gelu_grad(f["gate"], f["tg"])
        lamb = _scan_rev(_shift_up(a, 1, 0.0), dy * f["gl"])
        da = lamb * _shift_down(h, 1)
        dlog_a = da * a - (lamb * gi) * (a * a) / mult
        dgi = lamb * mult
        dra = dlog_a * (-LRU_C * sp) * r * (1.0 - r)
        dia = dgi * xr * i * (1.0 - i)
        dsp = jnp.sum(dlog_a * (-LRU_C * r), axis=0, keepdims=True)
        dlam = -dsp * _sigmoid(-lam_ref[...])
        dxr = dgi * i + _nt(dra, wa_ref[...]) + _nt(dia, wx_ref[...])
        dx, dcw = _conv_bwd(dxr, f["x"], f["cw"])
        dx_ref[...] = dx
        dcw_ref[...] += dcw
        dwa_ref[...] += _tn(xr, dra)
        dwx_ref[...] += _tn(xr, dia)
        rows = [jnp.sum(dxr, axis=0, keepdims=True), jnp.sum(dra, axis=0, keepdims=True),
                jnp.sum(dia, axis=0, keepdims=True), dlam]
        r8 = lax.broadcasted_iota(jnp.int32, (8, LRU_HALF), 0)
        acc = jnp.zeros((8, LRU_HALF), F32)
        for k, row in enumerate(rows):
            acc = jnp.where(r8 == k, row, acc)
        dv_ref[...] += acc

    nhalf = LRU_W // LRU_HALF
    col = pl.BlockSpec((seq, LRU_HALF), lambda j, b: (b, j))
    mat = pl.BlockSpec((None, LRU_HALF, LRU_HALF), lambda j, b: (j, 0, 0))
    return pl.pallas_call(
        body, name=name, grid=(nhalf, t // seq), in_specs=_lru_in_specs(seq) + [col],
        out_specs=[col, col, pl.BlockSpec((4, LRU_HALF), lambda j, b: (0, j)), mat, mat,
                   pl.BlockSpec((8, LRU_HALF), lambda j, b: (0, j))],
        out_shape=[SDS((t, LRU_W), F32), SDS((t, LRU_W), F32), SDS((4, LRU_W), F32),
                   SDS((nhalf, LRU_HALF, LRU_HALF), F32), SDS((nhalf, LRU_HALF, LRU_HALF), F32), SDS((8, LRU_W), F32)],
        compiler_params=_params(("arbitrary", "arbitrary")))(u, u, cw, cb, wa, ba, wx, bx, lam, dy)


NEG = -1e30


def _rel_bucket_map():
    dist = (np.arange(BLOCK_Q)[:, None] - np.arange(BLOCK_Q)[None, :]) % BLOCK_Q
    max_exact = REL_BUCKETS // 2
    large = max_exact + (np.log(np.maximum(dist, 1).astype(np.float32) / max_exact)
                         / math.log(BLOCK_Q / max_exact) * (REL_BUCKETS - max_exact)).astype(np.int32)
    large = np.minimum(large, REL_BUCKETS - 1)
    return np.where(dist < max_exact, dist, large).astype(np.int32)


def relbias_fwd(rel_bias, bmap, *, name):
    def body(rb_ref, bm_ref, o_ref):
        bm = bm_ref[...]
        for h in range(ATT_HEADS):
            acc = jnp.zeros((BLOCK_Q, BLOCK_Q), F32)
            for b in range(REL_BUCKETS):
                acc = jnp.where(bm == b, rb_ref[b, h], acc)
            o_ref[h] = acc

    return pl.pallas_call(
        body, name=name, in_specs=[pl.BlockSpec(memory_space=pltpu.SMEM), pl.BlockSpec(memory_space=pltpu.VMEM)],
        out_specs=pl.BlockSpec(memory_space=pltpu.VMEM), out_shape=SDS((ATT_HEADS, BLOCK_Q, BLOCK_Q), F32))(rel_bias, bmap)


def relbias_bwd(dbias, bmap, *, name):
    def body(db_ref, bm_ref, o_ref):
        bm = bm_ref[...]
        row = lax.broadcasted_iota(jnp.int32, (REL_BUCKETS, 128), 0)
        col = lax.broadcasted_iota(jnp.int32, (REL_BUCKETS, 128), 1)
        acc = jnp.zeros((REL_BUCKETS, 128), F32)
        for h in range(ATT_HEADS):
            d = db_ref[h]
            for b in range(REL_BUCKETS):
                s = jnp.sum(jnp.sum(jnp.where(bm == b, d, 0.0), axis=1, keepdims=True), axis=0, keepdims=True)
                acc = jnp.where((row == b) & (col == h), s, acc)
        o_ref[...] = acc

    return pl.pallas_call(body, name=name, out_shape=SDS((REL_BUCKETS, 128), F32))(dbias, bmap)


def _attn_probs(q_ref, k_ref, v_ref, b_ref, s_ref, n):
    rows = ATT_GROUP * BLOCK_Q
    qs = q_ref[...].reshape(rows, HEAD) * (HEAD ** -0.5)
    prev = pl.multiple_of(jnp.maximum(n - 1, 0) * BLOCK_Q, BLOCK_Q)
    cur = pl.multiple_of(n * BLOCK_Q, BLOCK_Q)
    kp, kc = k_ref[pl.ds(prev, BLOCK_Q), :], k_ref[pl.ds(cur, BLOCK_Q), :]
    vp, vc = v_ref[pl.ds(prev, BLOCK_Q), :], v_ref[pl.ds(cur, BLOCK_Q), :]
    bias = b_ref[...].reshape(rows, BLOCK_Q)
    i = lax.broadcasted_iota(jnp.int32, (rows, BLOCK_Q), 0) & (BLOCK_Q - 1)
    j = lax.broadcasted_iota(jnp.int32, (rows, BLOCK_Q), 1)
    s_p = jnp.where((j > i) & (n > 0), _nt(qs, kp) + bias, NEG)
    s_c = jnp.where(j <= i, _nt(qs, kc) + bias, NEG)
    sink = s_ref[...]
    m = jnp.maximum(jnp.maximum(jnp.max(s_p, axis=-1, keepdims=True), jnp.max(s_c, axis=-1, keepdims=True)), sink)
    e_p, e_c, e_s = jnp.exp(s_p - m), jnp.exp(s_c - m), jnp.exp(sink - m)
    inv = 1.0 / (jnp.sum(e_p, axis=-1, keepdims=True) + jnp.sum(e_c, axis=-1, keepdims=True) + e_s)
    return e_p * inv, e_c * inv, e_s * inv, qs, kp, kc, vp, vc, prev, cur


def _attn_specs(seq):
    qspec = pl.BlockSpec((None, ATT_GROUP, BLOCK_Q, HEAD), lambda g, b, n: (b, g, n, 0))
    kvspec = pl.BlockSpec((None, None, seq, HEAD), lambda g, b, n: (b, g, 0, 0))
    bspec = pl.BlockSpec((ATT_GROUP, BLOCK_Q, BLOCK_Q), lambda g, b, n: (g, 0, 0))
    sspec = pl.BlockSpec((ATT_GROUP * BLOCK_Q, 1), lambda g, b, n: (g, 0))
    return qspec, kvspec, bspec, sspec


def attn_fwd(q, k, v, bias, sink_rows, *, name):
    nb, _, seq, _ = q.shape

    def body(q_ref, k_ref, v_ref, b_ref, s_ref, o_ref):
        p_p, p_c, _, _, _, _, vp, vc, _, _ = _attn_probs(q_ref, k_ref, v_ref, b_ref, s_ref, pl.program_id(2))
        o_ref[...] = (_nn(p_p, vp) + _nn(p_c, vc)).reshape(ATT_GROUP, BLOCK_Q, HEAD)

    qspec, kvspec, bspec, sspec = _attn_specs(seq)
    return pl.pallas_call(
        body, name=name, grid=(KV_HEADS, nb, seq // BLOCK_Q), in_specs=[qspec, kvspec, kvspec, bspec, sspec],
        out_specs=qspec, out_shape=SDS(q.shape, F32),
        compiler_params=_params(("parallel", "parallel", "arbitrary")))(q, k, v, bias, sink_rows)


def attn_bwd(q, k, v, bias, sink_rows, do, *, name):
    nb, _, seq, _ = q.shape

    def body(q_ref, k_ref, v_ref, b_ref, s_ref, do_ref, dq_ref, dk_ref, dv_ref, db_ref, ds_ref):
        b, n = pl.program_id(1), pl.program_id(2)

        @pl.when((b == 0) & (n == 0))
        def _():
            db_ref[...] = jnp.zeros_like(db_ref)
            ds_ref[...] = jnp.zeros_like(ds_ref)

        @pl.when(n == 0)
        def _():
            dk_ref[...] = jnp.zeros_like(dk_ref)
            dv_ref[...] = jnp.zeros_like(dv_ref)

        p_p, p_c, p_s, qs, kp, kc, vp, vc, prev, cur = _attn_probs(q_ref, k_ref, v_ref, b_ref, s_ref, n)
        do = do_ref[...].reshape(ATT_GROUP * BLOCK_Q, HEAD)
        dp_p, dp_c = _nt(do, vp), _nt(do, vc)
        delta = jnp.sum(p_p * dp_p, axis=-1, keepdims=True) + jnp.sum(p_c * dp_c, axis=-1, keepdims=True)
        ds_p, ds_c = p_p * (dp_p - delta), p_c * (dp_c - delta)
        dq_ref[...] = ((_nn(ds_p, kp) + _nn(ds_c, kc)) * (HEAD ** -0.5)).reshape(ATT_GROUP, BLOCK_Q, HEAD)
        dk_ref[pl.ds(prev, BLOCK_Q), :] += _tn(ds_p, qs)
        dk_ref[pl.ds(cur, BLOCK_Q), :] += _tn(ds_c, qs)
        dv_ref[pl.ds(prev, BLOCK_Q), :] += _tn(p_p, do)
        dv_ref[pl.ds(cur, BLOCK_Q), :] += _tn(p_c, do)
        db_ref[...] += (ds_p + ds_c).reshape(ATT_GROUP, BLOCK_Q, BLOCK_Q)
        ds_ref[...] += -p_s * delta

    qspec, kvspec, bspec, sspec = _attn_specs(seq)
    return pl.pallas_call(
        body, name=name, grid=(KV_HEADS, nb, seq // BLOCK_Q), in_specs=[qspec, kvspec, kvspec, bspec, sspec, qspec],
        out_specs=[qspec, kvspec, kvspec, bspec, sspec],
        out_shape=[SDS(q.shape, F32), SDS(k.shape, F32), SDS(v.shape, F32),
                   SDS((ATT_HEADS, BLOCK_Q, BLOCK_Q), F32), SDS((ATT_HEADS * BLOCK_Q, 1), F32)],
        compiler_params=_params(("arbitrary", "arbitrary", "arbitrary")))(q, k, v, bias, sink_rows, do)


def _iota2(shape, axis):
    return lax.broadcasted_iota(jnp.int32, shape, axis)


def _col_to_row(col):
    c = col.shape[0]
    eye = _iota2((c, c), 0) == _iota2((c, c), 1)
    return jnp.sum(jnp.where(eye, jnp.broadcast_to(col, (c, c)), 0.0), axis=0, keepdims=True)


def _row_to_col(row):
    c = row.shape[1]
    eye = _iota2((c, c), 0) == _iota2((c, c), 1)
    return jnp.sum(jnp.where(eye, jnp.broadcast_to(row, (c, c)), 0.0), axis=1, keepdims=True)


def _last_row(col):
    c = col.shape[0]
    return jnp.sum(jnp.where(_iota2((c, 1), 0) == c - 1, col, 0.0), axis=0, keepdims=True)


def _chunk_cumsum(x):
    pos = _iota2(x.shape, 0) & (DN_CHUNK - 1)
    d = 1
    while d < DN_CHUNK:
        x = x + jnp.where(pos >= d, pltpu.roll(x, d, 0), 0.0)
        d *= 2
    return x


def _chunk_rev_cumsum(x):
    n = x.shape[0]
    pos = _iota2(x.shape, 0) & (DN_CHUNK - 1)
    d = 1
    while d < DN_CHUNK:
        x = x + jnp.where(pos < DN_CHUNK - d, pltpu.roll(x, n - d, 0), 0.0)
        d *= 2
    return x


def _tri_inv(low):
    c = low.shape[0]
    eye = (_iota2((c, c), 0) == _iota2((c, c), 1)).astype(F32)
    m = -low
    p = eye + m
    steps = int(math.log2(c)) - 1
    for _ in range(steps):
        m = _nn(m, m, hi=True)
        p = p + _nn(p, m, hi=True)
    return p


_DN_SCALE = (HEAD ** -0.5, 1.0, None)


def _dn_act(c, scale):
    sig = _sigmoid(c)
    a = c * sig
    if scale is None:
        return a, sig, None, None
    r = lax.rsqrt(jnp.sum(a * a, axis=-1, keepdims=True) + EPS)
    return a * r * scale, sig, a * r, r


def _dn_gates(ba_ref, hs_ref):
    beta = _sigmoid(ba_ref[0])
    sp_arg = ba_ref[1] + hs_ref[1]
    a_exp = jnp.exp(hs_ref[0])
    g = -a_exp * _softplus(sp_arg)
    return beta, g, sp_arg, a_exp


def _dn_inputs(pre_ref, cw_ref, ba_ref, hs_ref, act_sc, b_sc, gc_sc, c_sc=None):
    for idx in range(3):
        c = _conv_fwd(pre_ref[idx], [cw_ref[idx, k:k + 1, :] for k in range(4)])
        if c_sc is not None:
            c_sc[idx] = c
        act_sc[idx] = _dn_act(c, _DN_SCALE[idx])[0]
    beta, g, _, _ = _dn_gates(ba_ref, hs_ref)
    b_sc[...] = beta
    gc_sc[...] = _chunk_cumsum(g)


def _dn_chunk_math(q, k, v, b, gcc):
    c = q.shape[0]
    tril = _iota2((c, c), 0) >= _iota2((c, c), 1)
    strict = _iota2((c, c), 0) > _iota2((c, c), 1)
    eg = jnp.exp(gcc)
    kb, vb = k * b, v * b
    kbg = kb * eg
    dm = jnp.exp(jnp.where(tril, jnp.broadcast_to(gcc, (c, c)) - _col_to_row(gcc), NEG))
    kk = _nt(kb, k)
    t = _tri_inv(jnp.where(strict, kk * dm, 0.0))
    glast = _last_row(gcc)
    ekd = jnp.exp(glast - gcc)
    qk = _nt(q, k)
    return dict(tril=tril, strict=strict, eg=eg, kb=kb, vb=vb, kbg=kbg, dm=dm, kk=kk, t=t, glast=glast, ekd=ekd,
                kd=k * ekd, qk=qk, amat=jnp.where(tril, qk * dm, 0.0), qg=q * eg,
                egl=jnp.broadcast_to(jnp.exp(glast), (c, 1)))


DN_UNROLL = 4


def _chunk_loop(nc, chunk):
    u = math.gcd(nc, DN_UNROLL)

    def step(i, carry):
        for j in range(u):
            chunk(i * u + j)
        return carry

    lax.fori_loop(0, nc // u, step, 0)


def _dn_specs(seq):
    s64 = lambda lead: pl.BlockSpec((lead, None, None, seq, HEAD), lambda b, h: (0, b, h, 0, 0))
    s1 = lambda lead: pl.BlockSpec((lead, None, None, seq, 1), lambda b, h: (0, b, h, 0, 0))
    one64 = pl.BlockSpec((None, None, seq, HEAD), lambda b, h: (b, h, 0, 0))
    one1 = pl.BlockSpec((None, None, seq, 1), lambda b, h: (b, h, 0, 0))
    cw = pl.BlockSpec((None, 3, 4, HEAD), lambda b, h: (h, 0, 0, 0))
    hs = pl.BlockSpec((None, 2, 1, 1), lambda b, h: (h, 0, 0, 0))
    return s64, s1, one64, one1, cw, hs


def dn_prep(pre, cw, ba, hs, *, name):
    _, nb, nh, seq, _ = pre.shape
    nc = seq // DN_CHUNK

    def body(pre_ref, cw_ref, ba_ref, hs_ref, loc_ref, egl_ref, act_sc, b_sc, gc_sc):
        _dn_inputs(pre_ref, cw_ref, ba_ref, hs_ref, act_sc, b_sc, gc_sc)

        def chunk(c):
            rows = pl.ds(pl.multiple_of(c * DN_CHUNK, DN_CHUNK), DN_CHUNK)
            m = _dn_chunk_math(act_sc[0, rows, :], act_sc[1, rows, :], act_sc[2, rows, :], b_sc[rows, :], gc_sc[rows, :])
            loc_ref[0, rows, :] = m["qg"]
            loc_ref[1, rows, :] = m["kd"]
            loc_ref[2, rows, :] = _nn(m["t"], m["vb"])
            loc_ref[3, rows, :] = _nn(m["t"], m["kbg"])
            loc_ref[4, rows, :] = m["amat"]
            egl_ref[rows, :] = m["egl"]

        _chunk_loop(nc, chunk)

    s64, s1, one64, one1, cwspec, hsspec = _dn_specs(seq)
    return pl.pallas_call(
        body, name=name, grid=(nb, nh), in_specs=[s64(3), cwspec, s1(2), hsspec], out_specs=[s64(5), one1],
        out_shape=[SDS((5, nb, nh, seq, HEAD), F32), SDS((nb, nh, seq, 1), F32)],
        scratch_shapes=[pltpu.VMEM((3, seq, HEAD), F32)] + [pltpu.VMEM((seq, 1), F32)] * 2,
        compiler_params=_params(("parallel", "parallel")))(pre, cw, ba, hs)


def _gated_norm(o, z, gn):
    r = lax.rsqrt(jnp.mean(o * o, axis=-1, keepdims=True) + EPS)
    sig = _sigmoid(z)
    return o * r, sig, r


def dn_scan(loc, egl, z, gn, *, name):
    _, nb, nh, seq, _ = loc.shape
    nc = seq // DN_CHUNK

    def body(loc_ref, egl_ref, z_ref, gn_ref, y_ref, o_ref, vn_ref, st_ref):
        gn = gn_ref[...]

        def step(c, state):
            rows = pl.ds(pl.multiple_of(c * DN_CHUNK, DN_CHUNK), DN_CHUNK)
            st_ref[rows, :] = state
            vn = loc_ref[2, rows, :] - _nn(loc_ref[3, rows, :], state)
            o = _nn(loc_ref[0, rows, :], state) + _nn(loc_ref[4, rows, :], vn)
            vn_ref[rows, :] = vn
            o_ref[rows, :] = o
            zz = z_ref[rows, :]
            on, sig, _ = _gated_norm(o, zz, gn)
            y_ref[rows, :] = on * gn * (zz * sig)
            return state * egl_ref[rows, :] + _tn(loc_ref[1, rows, :], vn)

        lax.fori_loop(0, nc, step, jnp.zeros((HEAD, HEAD), F32))

    s64, s1, one64, one1, cwspec, hsspec = _dn_specs(seq)
    out = SDS((nb, nh, seq, HEAD), F32)
    return pl.pallas_call(
        body, name=name, grid=(nb, nh), in_specs=[s64(5), one1, one64, _whole((1, HEAD))],
        out_specs=[one64] * 4, out_shape=[out] * 4,
        compiler_params=_params(("parallel", "parallel")))(loc, egl, z, gn)


def dn_scan_bwd(loc, egl, z, gn, o, vn, states, dy, *, name):
    _, nb, nh, seq, _ = loc.shape
    nc = seq // DN_CHUNK

    def body(loc_ref, egl_ref, z_ref, gn_ref, o_ref, vn_ref, st_ref, dy_ref, dloc_ref, degl_ref, dz_ref, dgn_ref):
        @pl.when((pl.program_id(0) == 0) & (pl.program_id(1) == 0))
        def _():
            dgn_ref[...] = jnp.zeros_like(dgn_ref)

        gn = gn_ref[...]
        tril = _iota2((DN_CHUNK, DN_CHUNK), 0) >= _iota2((DN_CHUNK, DN_CHUNK), 1)

        def step(i, carry):
            ds, dgn = carry
            rows = pl.ds(pl.multiple_of((nc - 1 - i) * DN_CHUNK, DN_CHUNK), DN_CHUNK)
            dy, zz, oo = dy_ref[rows, :], z_ref[rows, :], o_ref[rows, :]
            on, sig, r = _gated_norm(oo, zz, gn)
            sz = zz * sig
            dz_ref[rows, :] = dy * on * gn * (sig * (1.0 + zz * (1.0 - sig)))
            dgn = dgn + jnp.sum(dy * on * sz, axis=0, keepdims=True)
            don = dy * gn * sz
            do = r * (don - on * jnp.mean(don * on, axis=-1, keepdims=True))
            state, vnew = st_ref[rows, :], vn_ref[rows, :]
            qg, kd, w, amat = loc_ref[0, rows, :], loc_ref[1, rows, :], loc_ref[3, rows, :], loc_ref[4, rows, :]
            dvn = _tn(amat, do) + _nn(kd, ds)
            dloc_ref[0, rows, :] = _nt(do, state)
            dloc_ref[1, rows, :] = _nt(vnew, ds)
            dloc_ref[2, rows, :] = dvn
            dloc_ref[3, rows, :] = -_nt(dvn, state)
            dloc_ref[4, rows, :] = jnp.where(tril, _nt(do, vnew), 0.0)
            degl = jnp.sum(jnp.sum(state * ds, axis=1, keepdims=True), axis=0, keepdims=True)
            degl_ref[rows, :] = jnp.broadcast_to(degl, (DN_CHUNK, 1))
            return ds * egl_ref[rows, :] + _tn(qg, do) - _tn(w, dvn), dgn

        _, dgn = lax.fori_loop(0, nc, step, (jnp.zeros((HEAD, HEAD), F32), jnp.zeros((1, HEAD), F32)))
        dgn_ref[...] += dgn

    s64, s1, one64, one1, cwspec, hsspec = _dn_specs(seq)
    return pl.pallas_call(
        body, name=name, grid=(nb, nh),
        in_specs=[s64(5), one1, one64, _whole((1, HEAD)), one64, one64, one64, one64],
        out_specs=[s64(5), one1, one64, _whole((1, HEAD))],
        out_shape=[SDS((5, nb, nh, seq, HEAD), F32), SDS((nb, nh, seq, 1), F32), SDS((nb, nh, seq, HEAD), F32),
                   SDS((1, HEAD), F32)],
        compiler_params=_params(("arbitrary", "arbitrary")))(loc, egl, z, gn, o, vn, states, dy)


def dn_prep_bwd(pre, cw, ba, hs, dloc, degl, *, name):
    _, nb, nh, seq, _ = pre.shape
    nc = seq // DN_CHUNK

    def body(pre_ref, cw_ref, ba_ref, hs_ref, dloc_ref, degl_ref, dpre_ref, dba_ref, dcw_ref, dhs_ref,
             act_sc, b_sc, gc_sc, c_sc):
        @pl.when(pl.program_id(1) == 0)
        def _():
            dcw_ref[...] = jnp.zeros_like(dcw_ref)
            dhs_ref[...] = jnp.zeros_like(dhs_ref)

        _dn_inputs(pre_ref, cw_ref, ba_ref, hs_ref, act_sc, b_sc, gc_sc, c_sc)

        def chunk(c):
            rows = pl.ds(pl.multiple_of(c * DN_CHUNK, DN_CHUNK), DN_CHUNK)
            q, k, v, b, gcc = act_sc[0, rows, :], act_sc[1, rows, :], act_sc[2, rows, :], b_sc[rows, :], gc_sc[rows, :]
            m = _dn_chunk_math(q, k, v, b, gcc)
            dqg, dkd, du, dw, da = (dloc_ref[x, rows, :] for x in range(5))
            t, dm, eg = m["t"], m["dm"], m["eg"]
            dt = _nt(du, m["vb"]) + _nt(dw, m["kbg"])
            dvb, dkbg = _tn(t, du), _tn(t, dw)
            dl = jnp.where(m["strict"], -_tn(t, _nt(dt, t, hi=True), hi=True), 0.0)
            dkk = dl * dm
            dqk = da * dm
            dd = dl * m["kk"] + da * m["qk"]
            dkb = _nn(dkk, k) + dkbg * eg
            dq = _nn(dqk, k) + dqg * eg
            dk = _tn(dkk, m["kb"]) + _tn(dqk, q) + dkd * m["ekd"] + dkb * b
            db = jnp.sum(dkb * k, axis=-1, keepdims=True) + jnp.sum(dvb * v, axis=-1, keepdims=True)
            mx = jnp.where(m["tril"], dd * dm, 0.0)
            tk = jnp.sum(dkd * m["kd"], axis=-1, keepdims=True)
            dgc = (jnp.sum(mx, axis=-1, keepdims=True) - _row_to_col(jnp.sum(mx, axis=0, keepdims=True))
                   + jnp.sum(dqg * m["qg"], axis=-1, keepdims=True) + jnp.sum(dkbg * m["kbg"], axis=-1, keepdims=True) - tk)
            dglast = jnp.sum(tk, axis=0, keepdims=True) + _last_row(degl_ref[rows, :]) * jnp.exp(m["glast"])
            act_sc[0, rows, :] = dq
            act_sc[1, rows, :] = dk
            act_sc[2, rows, :] = dvb * b
            b_sc[rows, :] = db
            gc_sc[rows, :] = dgc + jnp.where(_iota2((DN_CHUNK, 1), 0) == DN_CHUNK - 1, dglast, 0.0)

        _chunk_loop(nc, chunk)

        beta, g, sp_arg, a_exp = _dn_gates(ba_ref, hs_ref)
        dg = _chunk_rev_cumsum(gc_sc[...])
        dal = dg * (-a_exp) * _sigmoid(sp_arg)
        dba_ref[0] = b_sc[...] * beta * (1.0 - beta)
        dba_ref[1] = dal
        dhs_ref[0] += jnp.sum(dg * g, axis=0, keepdims=True)
        dhs_ref[1] += jnp.sum(dal, axis=0, keepdims=True)
        for idx in range(3):
            c = c_sc[idx]
            _, sig, hat, r = _dn_act(c, _DN_SCALE[idx])
            da_ = act_sc[idx]
            if _DN_SCALE[idx] is not None:
                da_ = da_ * _DN_SCALE[idx]
                da_ = r * (da_ - hat * jnp.sum(da_ * hat, axis=-1, keepdims=True))
            dx, dcw = _conv_bwd(da_ * (sig * (1.0 + c * (1.0 - sig))), pre_ref[idx],
                                [cw_ref[idx, k:k + 1, :] for k in range(4)])
            dpre_ref[idx] = dx
            dcw_ref[idx] += dcw

    s64, s1, one64, one1, cwspec, hsspec = _dn_specs(seq)
    swap = lambda spec: pl.BlockSpec(spec.block_shape, lambda h, b, _f=spec.index_map: _f(b, h))
    return pl.pallas_call(
        body, name=name, grid=(nh, nb),
        in_specs=[swap(s64(3)), swap(cwspec), swap(s1(2)), swap(hsspec), swap(s64(5)), swap(one1)],
        out_specs=[swap(s64(3)), swap(s1(2)), swap(cwspec), swap(hsspec)],
        out_shape=[SDS((3, nb, nh, seq, HEAD), F32), SDS((2, nb, nh, seq, 1), F32), SDS((nh, 3, 4, HEAD), F32),
                   SDS((nh, 2, 1, 1), F32)],
        scratch_shapes=[pltpu.VMEM((3, seq, HEAD), F32)] + [pltpu.VMEM((seq, 1), F32)] * 2 + [pltpu.VMEM((3, seq, HEAD), F32)],
        compiler_params=_params(("arbitrary", "arbitrary")))(pre, cw, ba, hs, dloc, degl)


COL_Q, COL_K, COL_V = 512 // 128, 1024 // 128, 1152 // 128
COL_DNQ, COL_DNK, COL_DNV, COL_DNZ, COL_BA = 1280 // 128, 1536 // 128, 1792 // 128, 2048 // 128, 2304 // 128


def _lane_a(shape):
    return _iota2(shape, 1) < HEAD


def _bd(x):
    la = _lane_a(x.shape)
    return jnp.concatenate([jnp.where(la, x, 0.0), jnp.where(la, 0.0, x)], axis=0)


def _fold(m):
    return m[:HEAD] + m[HEAD:]


def _bd_mask():
    return (_iota2((2 * HEAD, 2 * HEAD), 0) < HEAD) == (_iota2((2 * HEAD, 2 * HEAD), 1) < HEAD)


def _pk_nn(x, y, hi=False):
    return _nn(x, _bd(y), hi)


def _pk_nt(u, v, hi=False):
    return _nt(u, _bd(v), hi)


def _pk_tn(x, y, hi=False):
    return _fold(jnp.where(_bd_mask(), _tn(x, y, hi), 0.0))


def _half_sum(x):
    la = _lane_a(x.shape)
    return jnp.where(la, jnp.sum(jnp.where(la, x, 0.0), axis=-1, keepdims=True),
                     jnp.sum(jnp.where(la, 0.0, x), axis=-1, keepdims=True))


def _lane_col(x, idx):
    return jnp.sum(jnp.where(_iota2(x.shape, 1) == idx, x, 0.0), axis=-1, keepdims=True)


def _row0(x):
    return jnp.max(x, axis=0, keepdims=True)


def _dup_kv(x, g):
    la = _lane_a(x.shape)
    rolled = pltpu.roll(x, HEAD, 1)
    return jnp.where(la, x, rolled) if g == 0 else jnp.where(la, rolled, x)


def _stack_heads(ref, g):
    la = _lane_a((BLOCK_Q, 2 * HEAD))
    parts = []
    for hh in range(ATT_GROUP):
        pair = ref[:, pl.ds(2 * HEAD * (2 * g + hh // 2), 2 * HEAD)]
        parts.append(jnp.where(la if hh % 2 == 0 else ~la, pair, 0.0))
    return jnp.concatenate(parts, axis=0)


def _unstack_heads(stack, ref, g):
    la = _lane_a((BLOCK_Q, 2 * HEAD))
    for j in range(2):
        top = stack[2 * j * BLOCK_Q:(2 * j + 1) * BLOCK_Q]
        bot = stack[(2 * j + 1) * BLOCK_Q:(2 * j + 2) * BLOCK_Q]
        ref[:, pl.ds(2 * HEAD * (2 * g + j), 2 * HEAD)] = jnp.where(la, top, bot)


def _swa_probs(q_ref, k_ref, v_ref, b_ref, s_ref, n, g):
    rows = ATT_GROUP * BLOCK_Q
    prev = pl.multiple_of(jnp.maximum(n - 1, 0) * BLOCK_Q, BLOCK_Q)
    cur = pl.multiple_of(n * BLOCK_Q, BLOCK_Q)
    kp, kc = _dup_kv(k_ref[pl.ds(prev, BLOCK_Q), :], g), _dup_kv(k_ref[pl.ds(cur, BLOCK_Q), :], g)
    vp, vc = _dup_kv(v_ref[pl.ds(prev, BLOCK_Q), :], g), _dup_kv(v_ref[pl.ds(cur, BLOCK_Q), :], g)
    qs = _stack_heads(q_ref, g) * (HEAD ** -0.5)
    bias = b_ref[pl.ds(ATT_GROUP * g, ATT_GROUP)].reshape(rows, BLOCK_Q)
    i = _iota2((rows, BLOCK_Q), 0) & (BLOCK_Q - 1)
    j = _iota2((rows, BLOCK_Q), 1)
    s_p = jnp.where((j > i) & (n > 0), _nt(qs, kp) + bias, NEG)
    s_c = jnp.where(j <= i, _nt(qs, kc) + bias, NEG)
    sink = s_ref[pl.ds(rows * g, rows), :]
    m = jnp.maximum(jnp.maximum(jnp.max(s_p, axis=-1, keepdims=True), jnp.max(s_c, axis=-1, keepdims=True)), sink)
    e_p, e_c, e_s = jnp.exp(s_p - m), jnp.exp(s_c - m), jnp.exp(sink - m)
    inv = 1.0 / (jnp.sum(e_p, axis=-1, keepdims=True) + jnp.sum(e_c, axis=-1, keepdims=True) + e_s)
    return e_p * inv, e_c * inv, e_s * inv, qs, kp, kc, vp, vc, prev, cur


def _swa_specs(seq):
    nblk = seq // BLOCK_Q
    qspec = pl.BlockSpec((BLOCK_Q, ATT_W), lambda b, n: (b * nblk + n, COL_Q * 128 // ATT_W))
    kspec = pl.BlockSpec((seq, 2 * HEAD), lambda b, n: (b, COL_K))
    vspec = pl.BlockSpec((seq, 2 * HEAD), lambda b, n: (b, COL_V))
    ospec = pl.BlockSpec((BLOCK_Q, ATT_W), lambda b, n: (b * nblk + n, 0))
    kvout = pl.BlockSpec((seq, 2 * HEAD), lambda b, n: (b, 0))
    return qspec, kspec, vspec, ospec, kvout, _whole((ATT_HEADS, BLOCK_Q, BLOCK_Q)), _whole((ATT_HEADS * BLOCK_Q, 1))


def swa_fwd(u, bias, sink_rows, *, seq, name):
    t = u.shape[0]

    def body(q_ref, k_ref, v_ref, b_ref, s_ref, o_ref):
        for g in range(KV_HEADS):
            p_p, p_c, _, _, _, _, vp, vc, _, _ = _swa_probs(q_ref, k_ref, v_ref, b_ref, s_ref, pl.program_id(1), g)
            _unstack_heads(_nn(p_p, vp) + _nn(p_c, vc), o_ref, g)

    qspec, kspec, vspec, ospec, kvout, bspec, sspec = _swa_specs(seq)
    return pl.pallas_call(
        body, name=name, grid=(t // seq, seq // BLOCK_Q), in_specs=[qspec, kspec, vspec, bspec, sspec], out_specs=ospec,
        out_shape=SDS((t, ATT_W), F32), compiler_params=_params(("parallel", "arbitrary")))(u, u, u, bias, sink_rows)


def swa_bwd(u, bias, sink_rows, do, *, seq, name):
    t = u.shape[0]

    def body(q_ref, k_ref, v_ref, b_ref, s_ref, do_ref, dq_ref, dk_ref, dv_ref, db_ref, ds_ref):
        b, n = pl.program_id(0), pl.program_id(1)

        @pl.when((b == 0) & (n == 0))
        def _():
            db_ref[...] = jnp.zeros_like(db_ref)
            ds_ref[...] = jnp.zeros_like(ds_ref)

        @pl.when(n == 0)
        def _():
            dk_ref[...] = jnp.zeros_like(dk_ref)
            dv_ref[...] = jnp.zeros_like(dv_ref)

        la = _lane_a((BLOCK_Q, 2 * HEAD))
        for g in range(KV_HEADS):
            p_p, p_c, p_s, qs, kp, kc, vp, vc, prev, cur = _swa_probs(q_ref, k_ref, v_ref, b_ref, s_ref, n, g)
            do = _stack_heads(do_ref, g)
            dp_p, dp_c = _nt(do, vp), _nt(do, vc)
            delta = jnp.sum(p_p * dp_p, axis=-1, keepdims=True) + jnp.sum(p_c * dp_c, axis=-1, keepdims=True)
            ds_p, ds_c = p_p * (dp_p - delta), p_c * (dp_c - delta)
            _unstack_heads((_nn(ds_p, kp) + _nn(ds_c, kc)) * (HEAD ** -0.5), dq_ref, g)
            mine = la if g == 0 else ~la

            def to_head(x):
                return jnp.where(mine, x + pltpu.roll(x, HEAD, 1), 0.0)

            dk_ref[pl.ds(prev, BLOCK_Q), :] += to_head(_tn(ds_p, qs))
            dk_ref[pl.ds(cur, BLOCK_Q), :] += to_head(_tn(ds_c, qs))
            dv_ref[pl.ds(prev, BLOCK_Q), :] += to_head(_tn(p_p, do))
            dv_ref[pl.ds(cur, BLOCK_Q), :] += to_head(_tn(p_c, do))
            db_ref[pl.ds(ATT_GROUP * g, ATT_GROUP)] += (ds_p + ds_c).reshape(ATT_GROUP, BLOCK_Q, BLOCK_Q)
            rows = ATT_GROUP * BLOCK_Q
            ds_ref[pl.ds(rows * g, rows), :] += -p_s * delta

    qspec, kspec, vspec, ospec, kvout, bspec, sspec = _swa_specs(seq)
    return pl.pallas_call(
        body, name=name, grid=(t // seq, seq // BLOCK_Q), in_specs=[qspec, kspec, vspec, bspec, sspec, ospec],
        out_specs=[ospec, kvout, kvout, bspec, sspec],
        out_shape=[SDS((t, ATT_W), F32), SDS((t, 2 * HEAD), F32), SDS((t, 2 * HEAD), F32),
                   SDS((ATT_HEADS, BLOCK_Q, BLOCK_Q), F32), SDS((ATT_HEADS * BLOCK_Q, 1), F32)],
        compiler_params=_params(("arbitrary", "arbitrary")))(u, u, u, bias, sink_rows, do)


def _gdn_gates(ba_ref, alog_ref, dt_ref, hp):
    blk = ba_ref[...]
    beta_blk = _sigmoid(blk)
    sp_arg = blk + dt_ref[...]
    a_exp = jnp.exp(alog_ref[...])
    g_blk = -a_exp * _softplus(sp_arg)
    la = _lane_a(blk.shape)
    ha = 2 * hp
    beta = jnp.where(la, _lane_col(beta_blk, ha), _lane_col(beta_blk, ha + 1))
    g = jnp.where(la, _lane_col(g_blk, DN_HEADS + ha), _lane_col(g_blk, DN_HEADS + ha + 1))
    return beta, g, beta_blk, sp_arg, a_exp, g_blk


def _gdn_act(c, scale):
    sig = _sigmoid(c)
    a = c * sig
    if scale is None:
        return a, sig, None, None
    r = lax.rsqrt(_half_sum(a * a) + EPS)
    return a * r * scale, sig, a * r, r


def _gdn_inputs(pre_refs, cw_refs, ba_ref, alog_ref, dt_ref, hp, act_sc, b_sc, gc_sc, c_sc=None):
    for idx in range(3):
        c = _conv_fwd(pre_refs[idx][...], [cw_refs[idx][k:k + 1, :] for k in range(4)])
        if c_sc is not None:
            c_sc[idx] = c
        act_sc[idx] = _gdn_act(c, _DN_SCALE[idx])[0]
    beta, g = _gdn_gates(ba_ref, alog_ref, dt_ref, hp)[:2]
    b_sc[...] = beta
    gc_sc[...] = _chunk_cumsum(g)


def _gdn_chunk(q, k, v, b, gcc):
    shape = q.shape
    row, lm = _iota2(shape, 0), _iota2(shape, 1) & (HEAD - 1)
    tril, strict, eye = row >= lm, row > lm, row == lm
    eg = jnp.exp(gcc)
    kb, vb = k * b, v * b
    kbg = kb * eg
    grow = jnp.sum(jnp.where(eye, gcc, 0.0), axis=0, keepdims=True)
    dm = jnp.exp(jnp.where(tril, gcc - grow, NEG))
    kk = _pk_nt(kb, k)
    glast = jnp.sum(jnp.where(row == DN_CHUNK - 1, gcc, 0.0), axis=0, keepdims=True)
    ekd = jnp.exp(glast - gcc)
    qk = _pk_nt(q, k)
    return dict(q=q, k=k, v=v, b=b, tril=tril, strict=strict, eye=eye, row=row, eg=eg, kb=kb, vb=vb, kbg=kbg, dm=dm, kk=kk,
                low=jnp.where(strict, kk * dm, 0.0), glast=glast, ekd=ekd, kd=k * ekd, qk=qk,
                amat=jnp.where(tril, qk * dm, 0.0), qg=q * eg, egl=jnp.broadcast_to(jnp.exp(glast), shape))


def _tri_inv_many(chunks):
    ms = [-m["low"] for m in chunks]
    ts = [m["eye"].astype(F32) + x for m, x in zip(chunks, ms)]
    for _ in range(int(math.log2(HEAD)) - 1):
        ms = [_pk_nn(x, x, hi=True) for x in ms]
        ts = [t + _pk_nn(t, x, hi=True) for t, x in zip(ts, ms)]
    return ts


def _gdn_chunk_loop(nc, act_sc, b_sc, gc_sc, finish):
    u = math.gcd(nc, DN_UNROLL)

    def step(i, carry):
        rows = [pl.ds(pl.multiple_of((i * u + j) * DN_CHUNK, DN_CHUNK), DN_CHUNK) for j in range(u)]
        chunks = [_gdn_chunk(act_sc[0, r, :], act_sc[1, r, :], act_sc[2, r, :], b_sc[r, :], gc_sc[r, :]) for r in rows]
        pending = [finish(r, m, t) for r, m, t in zip(rows, chunks, _tri_inv_many(chunks))]
        pending = [g for g in pending if g is not None]
        while pending:
            for g in list(pending):
                if next(g, StopIteration) is StopIteration:
                    pending.remove(g)
        return carry

    lax.fori_loop(0, nc // u, step, 0)


def _gdn_in_specs(seq):
    u_at = lambda col: pl.BlockSpec((seq, 2 * HEAD), lambda b, hp, _c=col: (b, _c + hp))
    cw_at = lambda col: pl.BlockSpec((4, 2 * HEAD), lambda b, hp, _c=col: (0, _c + hp))
    row = pl.BlockSpec((1, 2 * HEAD), lambda b, hp: (0, 0))
    ba = pl.BlockSpec((seq, 2 * HEAD), lambda b, hp: (b, COL_BA))
    return [u_at(COL_DNQ), u_at(COL_DNK), u_at(COL_DNV), ba, cw_at(0), cw_at(2), cw_at(4), row, row]


def _pair(seq, lead=None):
    if lead is None:
        return pl.BlockSpec((seq, 2 * HEAD), lambda b, hp: (b, hp))
    return pl.BlockSpec((lead, seq, 2 * HEAD), lambda b, hp: (0, b, hp))


def _swap(spec):
    return pl.BlockSpec(spec.block_shape, lambda hp, b, _f=spec.index_map: _f(b, hp))


def gdn_prep(u, cw, alog_row, dt_row, *, seq, name):
    t = u.shape[0]
    nc = seq // DN_CHUNK

    def body(q_ref, k_ref, v_ref, ba_ref, cq_ref, ck_ref, cv_ref, alog_ref, dt_ref, loc_ref, egl_ref, act_sc, b_sc, gc_sc):
        _gdn_inputs((q_ref, k_ref, v_ref), (cq_ref, ck_ref, cv_ref), ba_ref, alog_ref, dt_ref, pl.program_id(1),
                    act_sc, b_sc, gc_sc)

        def finish(rows, m, t):
            loc_ref[0, rows, :] = m["qg"]
            loc_ref[1, rows, :] = m["kd"]
            loc_ref[2, rows, :] = _pk_nn(t, m["vb"])
            loc_ref[3, rows, :] = _pk_nn(t, m["kbg"])
            loc_ref[4, rows, :] = m["amat"]
            egl_ref[rows, :] = m["egl"]

        _gdn_chunk_loop(nc, act_sc, b_sc, gc_sc, finish)

    return pl.pallas_call(
        body, name=name, grid=(t // seq, DN_HEADS // 2), in_specs=_gdn_in_specs(seq), out_specs=[_pair(seq, 5), _pair(seq)],
        out_shape=[SDS((5, t, DN_HEADS * HEAD), F32), SDS((t, DN_HEADS * HEAD), F32)],
        scratch_shapes=[pltpu.VMEM((3, seq, 2 * HEAD), F32)] + [pltpu.VMEM((seq, 2 * HEAD), F32)] * 2,
        compiler_params=_params(("parallel", "parallel")))(u, u, u, u, cw, cw, cw, alog_row, dt_row)


def _gated_norm2(o, z, gn):
    r = lax.rsqrt(_half_sum(o * o) * (1.0 / HEAD) + EPS)
    return o * r, _sigmoid(z), r


def gdn_scan(loc, egl, u, gn, *, seq, name):
    t = u.shape[0]
    nc = seq // DN_CHUNK

    def body(loc_ref, egl_ref, z_ref, gn_ref, y_ref, o_ref, vn_ref, st_ref):
        gn = gn_ref[...]
        bdm = _bd_mask()

        def step(c, state):
            rows = pl.ds(pl.multiple_of(c * DN_CHUNK, DN_CHUNK), DN_CHUNK)
            st_ref[rows, :] = _fold(state)
            vn = loc_ref[2, rows, :] - _nn(loc_ref[3, rows, :], state)
            o = _nn(loc_ref[0, rows, :], state) + _pk_nn(loc_ref[4, rows, :], vn)
            vn_ref[rows, :] = vn
            o_ref[rows, :] = o
            zz = z_ref[rows, :]
            on, sig, _ = _gated_norm2(o, zz, gn)
            y_ref[rows, :] = on * gn * (zz * sig)
            return state * _row0(egl_ref[rows, :]) + jnp.where(bdm, _tn(loc_ref[1, rows, :], vn), 0.0)

        lax.fori_loop(0, nc, step, jnp.zeros((2 * HEAD, 2 * HEAD), F32))

    zspec = pl.BlockSpec((seq, 2 * HEAD), lambda b, hp: (b, COL_DNZ + hp))
    out = SDS((t, DN_HEADS * HEAD), F32)
    return pl.pallas_call(
        body, name=name, grid=(t // seq, DN_HEADS // 2), in_specs=[_pair(seq, 5), _pair(seq), zspec, _whole((1, 2 * HEAD))],
        out_specs=[_pair(seq)] * 4, out_shape=[out] * 4,
        compiler_params=_params(("parallel", "parallel")))(loc, egl, u, gn)


def gdn_scan_bwd(loc, egl, u, gn, o, vn, states, dy, *, seq, name):
    t = u.shape[0]
    nc = seq // DN_CHUNK

    def body(loc_ref, egl_ref, z_ref, gn_ref, o_ref, vn_ref, st_ref, dy_ref, dloc_ref, degl_ref, dz_ref, dgn_ref):
        @pl.when((pl.program_id(0) == 0) & (pl.program_id(1) == 0))
        def _():
            dgn_ref[...] = jnp.zeros_like(dgn_ref)

        gn = gn_ref[...]
        bdm = _bd_mask()
        shape = (DN_CHUNK, 2 * HEAD)
        tril = _iota2(shape, 0) >= (_iota2(shape, 1) & (HEAD - 1))

        def step(i, carry):
            ds, dgn = carry
            rows = pl.ds(pl.multiple_of((nc - 1 - i) * DN_CHUNK, DN_CHUNK), DN_CHUNK)
            dy, zz, oo = dy_ref[rows, :], z_ref[rows, :], o_ref[rows, :]
            on, sig, r = _gated_norm2(oo, zz, gn)
            sz = zz * sig
            dz_ref[rows, :] = dy * on * gn * (sig * (1.0 + zz * (1.0 - sig)))
            dgn = dgn + jnp.sum(dy * on * sz, axis=0, keepdims=True)
            don = dy * gn * sz
            do = r * (don - on * _half_sum(don * on) * (1.0 / HEAD))
            state, vnew = _bd(st_ref[rows, :]), vn_ref[rows, :]
            qg, kd, w, amat = loc_ref[0, rows, :], loc_ref[1, rows, :], loc_ref[3, rows, :], loc_ref[4, rows, :]
            dvn = _pk_tn(amat, do) + _nn(kd, ds)
            dloc_ref[0, rows, :] = _nt(do, state)
            dloc_ref[1, rows, :] = _nt(vnew, ds)
            dloc_ref[2, rows, :] = dvn
            dloc_ref[3, rows, :] = -_nt(dvn, state)
            dloc_ref[4, rows, :] = jnp.where(tril, _pk_nt(do, vnew), 0.0)
            degl = _half_sum(jnp.sum(state * ds, axis=0, keepdims=True))
            degl_ref[rows, :] = jnp.broadcast_to(degl, shape)
            grow = jnp.where(bdm, _tn(qg, do) - _tn(w, dvn), 0.0)
            return ds * _row0(egl_ref[rows, :]) + grow, dgn

        _, dgn = lax.fori_loop(0, nc, step, (jnp.zeros((2 * HEAD, 2 * HEAD), F32), jnp.zeros((1, 2 * HEAD), F32)))
        dgn_ref[...] += dgn

    zspec = pl.BlockSpec((seq, 2 * HEAD), lambda b, hp: (b, COL_DNZ + hp))
    one = _pair(seq)
    out = SDS((t, DN_HEADS * HEAD), F32)
    return pl.pallas_call(
        body, name=name, grid=(t // seq, DN_HEADS // 2),
        in_specs=[_pair(seq, 5), one, zspec, _whole((1, 2 * HEAD)), one, one, one, one],
        out_specs=[_pair(seq, 5), one, one, _whole((1, 2 * HEAD))],
        out_shape=[SDS((5, t, DN_HEADS * HEAD), F32), out, out, SDS((1, 2 * HEAD), F32)],
        compiler_params=_params(("arbitrary", "arbitrary")))(loc, egl, u, gn, o, vn, states, dy)


def gdn_prep_bwd(u, cw, alog_row, dt_row, dloc, degl, *, seq, name):
    t = u.shape[0]
    nc = seq // DN_CHUNK

    def body(q_ref, k_ref, v_ref, ba_ref, cq_ref, ck_ref, cv_ref, alog_ref, dt_ref, dloc_ref, degl_ref,
             dqkv_ref, dba_ref, dcw_ref, dhs_ref, act_sc, b_sc, gc_sc, c_sc):
        hp = pl.program_id(0)

        @pl.when(pl.program_id(1) == 0)
        def _():
            dcw_ref[...] = jnp.zeros_like(dcw_ref)
            dhs_ref[...] = jnp.zeros_like(dhs_ref)

        pre_refs, cw_refs = (q_ref, k_ref, v_ref), (cq_ref, ck_ref, cv_ref)
        _gdn_inputs(pre_refs, cw_refs, ba_ref, alog_ref, dt_ref, hp, act_sc, b_sc, gc_sc, c_sc)

        def finish(rows, m, tt):
            q, k, v, b = m["q"], m["k"], m["v"], m["b"]
            dqg, dkd, du, dw, da = (dloc_ref[x, rows, :] for x in range(5))
            dm, eg = m["dm"], m["eg"]
            dt = _pk_nt(du, m["vb"]) + _pk_nt(dw, m["kbg"])
            dvb, dkbg = _pk_tn(tt, du), _pk_tn(tt, dw)
            yield
            dtt = _pk_nt(dt, tt, hi=True)
            yield
            dl = jnp.where(m["strict"], -_pk_tn(tt, dtt, hi=True), 0.0)
            yield
            dkk = dl * dm
            dqk = da * dm
            dd = dl * m["kk"] + da * m["qk"]
            dkb = _pk_nn(dkk, k) + dkbg * eg
            dq = _pk_nn(dqk, k) + dqg * eg
            yield
            dk = _pk_tn(dkk, m["kb"]) + _pk_tn(dqk, q) + dkd * m["ekd"] + dkb * b
            db = _half_sum(dkb * k + dvb * v)
            yield
            mx = jnp.where(m["tril"], dd * dm, 0.0)
            tk = _half_sum(dkd * m["kd"])
            colsum = jnp.where(m["eye"], jnp.broadcast_to(jnp.sum(mx, axis=0, keepdims=True), mx.shape), 0.0)
            dgc = _half_sum(mx) - _half_sum(colsum) + _half_sum(dqg * m["qg"] + dkbg * m["kbg"]) - tk
            dglast = jnp.sum(tk, axis=0, keepdims=True) + _row0(degl_ref[rows, :]) * jnp.exp(m["glast"])
            act_sc[0, rows, :] = dq
            act_sc[1, rows, :] = dk
            act_sc[2, rows, :] = dvb * b
            b_sc[rows, :] = db
            gc_sc[rows, :] = dgc + jnp.where(m["row"] == DN_CHUNK - 1, dglast, 0.0)

        _gdn_chunk_loop(nc, act_sc, b_sc, gc_sc, finish)

        beta, g, beta_blk, sp_arg, a_exp, g_blk = _gdn_gates(ba_ref, alog_ref, dt_ref, hp)
        dg = _chunk_rev_cumsum(gc_sc[...])
        lane = _iota2(beta_blk.shape, 1)
        ha = 2 * hp
        db = b_sc[...]
        at = lambda idx, x_a, x_b: (jnp.where(lane == idx, _lane_col(x_a, 0), 0.0)
                                    + jnp.where(lane == idx + 1, _lane_col(x_b, HEAD), 0.0))
        dg_blk = at(DN_HEADS + ha, dg, dg)
        dal = dg_blk * (-a_exp) * _sigmoid(sp_arg)
        dba_ref[...] = at(ha, db, db) * beta_blk * (1.0 - beta_blk) + dal
        dhs_ref[0:1, :] += jnp.sum(dg_blk * g_blk, axis=0, keepdims=True)
        dhs_ref[1:2, :] += jnp.sum(dal, axis=0, keepdims=True)
        for idx in range(3):
            c = c_sc[idx]
            _, sig, hat, r = _gdn_act(c, _DN_SCALE[idx])
            da_ = act_sc[idx]
            if _DN_SCALE[idx] is not None:
                da_ = da_ * _DN_SCALE[idx]
                da_ = r * (da_ - hat * _half_sum(da_ * hat))
            dx, dcw = _conv_bwd(da_ * (sig * (1.0 + c * (1.0 - sig))), pre_refs[idx][...],
                                [cw_refs[idx][k:k + 1, :] for k in range(4)])
            dqkv_ref[idx] = dx
            dcw_ref[idx] += dcw

    pair = DN_HEADS // 2
    in_specs = [_swap(s) for s in _gdn_in_specs(seq)] + [_swap(_pair(seq, 5)), _swap(_pair(seq))]
    return pl.pallas_call(
        body, name=name, grid=(pair, t // seq), in_specs=in_specs,
        out_specs=[_swap(_pair(seq, 3)), pl.BlockSpec((None, seq, 2 * HEAD), lambda hp, b: (hp, b, 0)),
                   pl.BlockSpec((3, 4, 2 * HEAD), lambda hp, b: (0, 0, hp)),
                   pl.BlockSpec((None, 2, 2 * HEAD), lambda hp, b: (hp, 0, 0))],
        out_shape=[SDS((3, t, DN_HEADS * HEAD), F32), SDS((pair, t, 2 * HEAD), F32), SDS((3, 4, DN_HEADS * HEAD), F32),
                   SDS((pair, 2, 2 * HEAD), F32)],
        scratch_shapes=[pltpu.VMEM((3, seq, 2 * HEAD), F32)] + [pltpu.VMEM((seq, 2 * HEAD), F32)] * 2
        + [pltpu.VMEM((3, seq, 2 * HEAD), F32)],
        compiler_params=_params(("arbitrary", "arbitrary")))(u, u, u, u, cw, cw, cw, alog_row, dt_row, dloc, degl)


def mix_out(y_lru, o, y_dn, w_out, h, *, name, tm=512):
    t, d = h.shape
    tm = min(tm, t)

    def body(a_ref, b_ref, c_ref, w_ref, h_ref, o_ref, y_ref):
        y_ref[:, 0:LRU_W] = a_ref[...].astype(BF16)
        y_ref[:, LRU_W:LRU_W + ATT_W] = b_ref[...].astype(BF16)
        y_ref[:, LRU_W + ATT_W:] = c_ref[...].astype(BF16)
        o_ref[...] = h_ref[...] + _nn(y_ref[...], w_ref[...])

    rows = lambda width: pl.BlockSpec((tm, width), lambda i: (i, 0))
    return pl.pallas_call(
        body, name=name, grid=(t // tm,), in_specs=[rows(LRU_W), rows(ATT_W), rows(LRU_W), _whole((d, d)), rows(d)],
        out_specs=[rows(d), rows(d)], out_shape=[SDS((t, d), F32), SDS((t, d), BF16)],
        compiler_params=_params(("parallel",)))(y_lru, o, y_dn, w_out, h)


def mix_out_bwd(dout, w_out, *, name, tm=512):
    t, d = dout.shape
    tm = min(tm, t)

    def body(d_ref, w_ref, a_ref, b_ref, c_ref):
        dy = _nt(d_ref[...], w_ref[...])
        a_ref[...] = dy[:, 0:LRU_W]
        b_ref[...] = dy[:, LRU_W:LRU_W + ATT_W]
        c_ref[...] = dy[:, LRU_W + ATT_W:]

    rows = lambda width: pl.BlockSpec((tm, width), lambda i: (i, 0))
    return pl.pallas_call(
        body, name=name, grid=(t // tm,), in_specs=[rows(d), _whole((d, d))], out_specs=[rows(LRU_W), rows(ATT_W), rows(LRU_W)],
        out_shape=[SDS((t, LRU_W), F32), SDS((t, ATT_W), F32), SDS((t, LRU_W), F32)],
        compiler_params=_params(("parallel",)))(dout, w_out)


def mix_in_bwd(h, gain, dout, w_in, dx, dgate, dq, dk, dv, dqkv, dz, dba, *, name, tm=512):
    t, d = h.shape
    tm = min(tm, t)

    def body(h_ref, g_ref, do_ref, w_ref, dx_ref, dgate_ref, dq_ref, dk_ref, dv_ref, dqkv_ref, dz_ref, dba_ref,
             dh_ref, dg_ref, du_ref):
        @pl.when(pl.program_id(0) == 0)
        def _():
            dg_ref[...] = jnp.zeros_like(dg_ref)

        off = 0
        for piece in (dx_ref[...], dgate_ref[...], dq_ref[...], dk_ref[...], dv_ref[...], dqkv_ref[0], dqkv_ref[1],
                      dqkv_ref[2], dz_ref[...], dba_ref[0] + dba_ref[1]):
            du_ref[:, off:off + piece.shape[1]] = piece.astype(BF16)
            off += piece.shape[1]
        du_ref[:, off:] = jnp.zeros((tm, D_IN_PAD - off), BF16)
        g = g_ref[...]
        _, xh, r = _rms_fwd(h_ref[...], g)
        dh, dg = _rms_bwd(_nt(du_ref[...], w_ref[...]), xh, r, g)
        dh_ref[...] = do_ref[...] + dh
        dg_ref[...] += dg

    rows = lambda width: pl.BlockSpec((tm, width), lambda i: (i, 0))
    return pl.pallas_call(
        body, name=name, grid=(t // tm,),
        in_specs=[rows(d), _whole((1, d)), rows(d), _whole((d, D_IN_PAD)), rows(LRU_W), rows(LRU_W), rows(ATT_W),
                  rows(2 * HEAD), rows(2 * HEAD), pl.BlockSpec((3, tm, DN_HEADS * HEAD), lambda i: (0, i, 0)),
                  rows(DN_HEADS * HEAD), pl.BlockSpec((2, tm, 2 * HEAD), lambda i: (0, i, 0))],
        out_specs=[rows(d), _whole((1, d)), rows(D_IN_PAD)],
        out_shape=[SDS((t, d), F32), SDS((1, d), F32), SDS((t, D_IN_PAD), BF16)],
        compiler_params=_params(("arbitrary",)))(h, gain, dout, w_in, dx, dgate, dq, dk, dv, dqkv, dz, dba)


def _block_diag(w):
    out = jnp.zeros((LRU_W, LRU_W), w.dtype)
    for h in range(LRU_W // HEAD):
        out = lax.dynamic_update_slice(out, w[h], (h * HEAD, h * HEAD))
    return out


def _diag_blocks(w):
    per = LRU_HALF // HEAD
    return jnp.stack([w[h // per, (h % per) * HEAD:(h % per + 1) * HEAD, (h % per) * HEAD:(h % per + 1) * HEAD]
                      for h in range(LRU_W // HEAD)])


def layer_params(w, wl, l, bias):
    row = lambda a: a[l].reshape(1, -1)
    return dict(
        ffn1_norm=row(w["ffn1_norm"]), ffn1=(wl["ffn1_w_gate"], wl["ffn1_w_up"], wl["ffn1_w_down"]),
        mix_norm=row(w["mix_norm"]) + wl["tie1"][0:1, 0:1], w_in=wl["w_in"],
        lru=(wl["lru_conv_w"], row(w["lru_conv_b"]), _block_diag(w["lru_w_a"][l]), row(w["lru_b_a"]),
             _block_diag(w["lru_w_x"][l]), row(w["lru_b_x"]), row(w["lru_lambda"])),
        bias=bias, sink_rows=jnp.repeat(w["attn_sinks"][l], BLOCK_Q).reshape(ATT_HEADS * BLOCK_Q, 1),
        dn_cw=wl["dn_conv_w"], dn_alog=_ba_row(w["dn_a_log"][l]), dn_dt=_ba_row(w["dn_dt_bias"][l]),
        dn_norm=jnp.tile(row(w["dn_norm"]), (1, 2)), w_out=wl["w_out"],
        ffn2_norm=row(w["ffn2_norm"]), ffn2=(wl["ffn2_w_gate"], wl["ffn2_w_up"], wl["ffn2_w_down"]),
        ple_norm=row(w["ple_norm"]), ple_w_gate=wl["ple_w_gate"], ple_w_proj=wl["ple_w_proj"])


def _ba_row(per_head):
    return jnp.pad(per_head, (DN_HEADS, 2 * HEAD - 2 * DN_HEADS)).reshape(1, 2 * HEAD)


def mixer_fwd(h, p, nb, seq, tag):
    u, n = norm_matmul(h, p["mix_norm"], p["w_in"], name=f"mix_in_{tag}")
    y_lru = lru_fwd(u, *p["lru"], seq=seq, name=f"lru_fwd_{tag}")
    o = swa_fwd(u, p["bias"], p["sink_rows"], seq=seq, name=f"swa_fwd_{tag}")
    loc, egl = gdn_prep(u, p["dn_cw"], p["dn_alog"], p["dn_dt"], seq=seq, name=f"gdn_prep_{tag}")
    y_dn, o_raw, vn, st = gdn_scan(loc, egl, u, p["dn_norm"], seq=seq, name=f"gdn_scan_{tag}")
    out, ycat = mix_out(y_lru, o, y_dn, p["w_out"], h, name=f"mix_out_{tag}")
    return out, dict(h=h, u=u, n=n, loc=loc, egl=egl, o_raw=o_raw, vn=vn, st=st, ycat=ycat)


def mixer_bwd(dout, s, p, nb, seq, tag):
    u = s["u"]
    dy_lru, do, dy_dn = mix_out_bwd(dout, p["w_out"], name=f"mix_out_dx_{tag}")
    g = {"w_out": matmul(s["ycat"], dout, ta=True, name=f"mix_out_dw_{tag}")}
    dx, dgate, dcw, dwa, dwx, dvec = lru_bwd(u, *p["lru"], dy_lru, seq=seq, name=f"lru_bwd_{tag}")
    g.update(lru_conv_w=dcw, lru_conv_b=dvec[0], lru_w_a=_diag_blocks(dwa), lru_b_a=dvec[1], lru_w_x=_diag_blocks(dwx),
             lru_b_x=dvec[2], lru_lambda=dvec[3])
    dq, dk, dv, dbias, dsink = swa_bwd(u, p["bias"], p["sink_rows"], do, seq=seq, name=f"swa_bwd_{tag}")
    g.update(attn_sinks=dsink.reshape(ATT_HEADS, BLOCK_Q).sum(axis=1), bias=dbias)
    dloc, degl, dz, dgn = gdn_scan_bwd(s["loc"], s["egl"], u, p["dn_norm"], s["o_raw"], s["vn"], s["st"], dy_dn, seq=seq,
                                       name=f"gdn_scan_bwd_{tag}")
    dqkv, dba, dcw3, dhs = gdn_prep_bwd(u, p["dn_cw"], p["dn_alog"], p["dn_dt"], dloc, degl, seq=seq,
                                        name=f"gdn_prep_bwd_{tag}")
    dhs = dhs.sum(axis=0)[:, DN_HEADS:2 * DN_HEADS]
    g.update(dn_conv_w=dcw3.transpose(1, 0, 2).reshape(4, 3 * DN_HEADS * HEAD), dn_a_log=dhs[0], dn_dt_bias=dhs[1],
             dn_norm=dgn[0, :HEAD] + dgn[0, HEAD:])
    dh, dgain, du = mix_in_bwd(s["h"], p["mix_norm"], dout, p["w_in"], dx, dgate, dq, dk, dv, dqkv, dz, dba,
                               name=f"mix_in_bwd_{tag}")
    g["w_in"] = matmul(s["n"], du, ta=True, name=f"mix_in_dw_{tag}")
    g["mix_norm"] = dgain[0]
    return dh, g


SHARDED = ("ffn1_w_gate", "ffn1_w_up", "ffn1_w_down", "w_in", "w_out", "ffn2_w_gate", "ffn2_w_up", "ffn2_w_down",
           "ple_w_gate", "ple_w_proj")
PER_LAYER_SMALL = ("ffn1_norm", "mix_norm", "lru_conv_w", "lru_conv_b", "lru_w_a", "lru_b_a", "lru_w_x", "lru_b_x",
                   "lru_lambda", "attn_sinks", "dn_conv_w", "dn_a_log", "dn_dt_bias", "dn_norm", "ffn2_norm", "ple_norm")


GRAD_PARTS = (("ple_w_gate", "ple_w_proj", "ffn2_w_gate", "ffn2_w_up", "ffn2_w_down"),
              ("ffn1_w_gate", "ffn1_w_up", "ffn1_w_down", "w_in", "w_out"))
WEIGHT_PARTS = (("ffn1_w_gate", "ffn1_w_up", "ffn1_w_down"),
                ("w_in", "w_out", "ffn2_w_gate", "ffn2_w_up", "ffn2_w_down", "ple_w_gate", "ple_w_proj", "lru_conv_w",
                 "dn_conv_w"))


def _col_shards(a):
    r, c = a.shape
    return a.reshape(r, N_CHIP, c // N_CHIP).transpose(1, 0, 2)


def local_step(x, p, target, w, layer_weights, layer_grads, bmap, nb, seq):
    bias = relbias_fwd(w["rel_bias"], bmap, name="relbias_fwd")
    h, saved = x, []
    for l in range(N_LAYER):
        wl = layer_weights(l, 0, h)
        s = dict(h0=h)
        h = ffn_fwd(h, w["ffn1_norm"][l].reshape(1, -1) + wl["tie0"][0:1, 0:1], wl["ffn1_w_gate"], wl["ffn1_w_up"],
                    wl["ffn1_w_down"], name=f"ffn1_fwd_{l}")
        wl.update(layer_weights(l, 1, h))
        pr = layer_params(w, wl, l, bias)
        h, s["mix"] = mixer_fwd(h, pr, nb, seq, l)
        s["h2"] = h
        h = ffn_fwd(h, pr["ffn2_norm"], *pr["ffn2"], name=f"ffn2_fwd_{l}")
        s["h3"] = h
        h = ple_fwd(h, pr["ple_norm"], pr["ple_w_gate"], p[l], pr["ple_w_proj"], name=f"ple_fwd_{l}")
        saved.append((pr, s))
    dh, dgf, loss = loss_head(h, w["final_norm"].reshape(1, -1), target, name="loss_head")

    per_layer, dbias, token = [None] * N_LAYER, None, None
    for l in reversed(range(N_LAYER)):
        pr, s = saved[l]
        g = {}
        dout = dh
        ple_norm = pr["ple_norm"] if token is None else pr["ple_norm"] + token[0:1, 0:1]
        dh, n, dga, dpp, dg = ple_bwd(s["h3"], ple_norm, pr["ple_w_gate"], p[l], pr["ple_w_proj"], dout, name=f"ple_bwd_{l}")
        g["ple_norm"] = dg[0]
        g["ple_w_gate"] = matmul(n, dga, ta=True, name=f"ple_dwg_{l}").reshape(N_CHIP, -1, D_MODEL)
        g["ple_w_proj"] = _col_shards(matmul(p[l], dpp, ta=True, name=f"ple_dwp_{l}"))
        for part, (nm, hin) in enumerate((("ffn2", s["h2"]), ("ffn1", s["h0"]))):
            if nm == "ffn1":
                lru = list(pr["lru"])
                lru[1] = lru[1] + token[0:1, 0:1]
                dh, gm = mixer_bwd(dh, s["mix"], dict(pr, lru=tuple(lru)), nb, seq, l)
                dbias = gm.pop("bias") if dbias is None else dbias + gm.pop("bias")
                gm["w_in"] = _col_shards(gm["w_in"][:, :D_IN])
                gm["w_out"] = gm["w_out"].reshape(N_CHIP, -1, D_MODEL)
                g.update(gm)
            dout = dh
            dh, n, da, db, sact, dg = ffn_bwd_act(hin, pr[nm + "_norm"], dout, *pr[nm], name=f"{nm}_bwd_act_{l}")
            g[nm + "_norm"] = dg[0]
            g[nm + "_w_gate"], g[nm + "_w_up"], g[nm + "_w_down"] = ffn_bwd_w(n, da, db, sact, dout, name=f"{nm}_bwd_w_{l}")
            token = layer_grads(l, part, {k: g.pop(k) for k in GRAD_PARTS[part]}, dh)
        per_layer[l] = g
    grads = {k: jnp.stack([per_layer[l][k] for l in range(N_LAYER)]) for k in PER_LAYER_SMALL}
    grads["rel_bias"] = relbias_bwd(dbias, bmap, name="relbias_bwd")[:, :ATT_HEADS]
    grads["final_norm"] = dgf[0]
    return loss, dh, grads


HBM_SPEC = pl.BlockSpec(memory_space=pltpu.HBM)


def _place():
    x, y, c = lax.axis_index("x"), lax.axis_index("y"), lax.axis_index("c")
    chips = [(1 - x, y), (x, 1 - y), (1 - x, 1 - y)]
    return x, y, c, 2 * x + y, (x, y, 1 - c), chips, [2 * cx + cy for cx, cy in chips]


def _remote(src, dst, send_sem, recv_sem, to):
    return pltpu.make_async_remote_copy(src_ref=src, dst_ref=dst, send_sem=send_sem, recv_sem=recv_sem, device_id=to,
                                        device_id_type=MESH)


def place_shard(w, chip_arr, dtype, *, name):
    nl, r, c = w.shape
    tr = next(cand for cand in (256, 128, 64, 32, 16, 8, r) if r % cand == 0)

    def body(chip_ref, w_ref, o_ref):
        o_ref[...] = w_ref[...].astype(dtype)

    return pl.pallas_call(
        body, name=name,
        grid_spec=pltpu.PrefetchScalarGridSpec(
            num_scalar_prefetch=1, grid=(nl, r // tr),
            in_specs=[pl.BlockSpec((None, tr, c), lambda l, i, chip: (l, i, 0))],
            out_specs=pl.BlockSpec((None, None, tr, c), lambda l, i, chip: (chip[0], l, i, 0))),
        out_shape=SDS((N_CHIP, nl, r, c), dtype), compiler_params=_params(("parallel", "parallel")))(chip_arr, w)


def allgather_shards(shards, *, name):
    n = len(shards)

    def body(*refs):
        outs = refs[n:2 * n]
        send, recv, fsend, frecv = refs[2 * n:]
        x, y, c, me, sib, chips, cids = _place()
        first, passed = [], []
        for k in range(n):
            for j, chip in enumerate(chips):
                mine = outs[k].at[me, c]
                first.append(_remote(mine, mine, send.at[3 * k + j], recv.at[3 * k + j], (*chip, c)))
                first[-1].start()
        for k in range(n):
            for j in range(3):
                piece = outs[k].at[cids[j], c]
                _remote(piece, piece, send.at[3 * k + j], recv.at[3 * k + j], sib).wait_recv()
                passed.append(_remote(piece, piece, fsend.at[3 * k + j], frecv.at[3 * k + j], sib))
                passed[-1].start()
        for k in range(n):
            for j in range(3):
                piece = outs[k].at[cids[j], 1 - c]
                _remote(piece, piece, fsend.at[3 * k + j], frecv.at[3 * k + j], sib).wait_recv()
        for cp in first + passed:
            cp.wait_send()

    return pl.pallas_call(
        body, name=name, in_specs=[HBM_SPEC] * n, out_specs=[HBM_SPEC] * n,
        out_shape=[SDS(s.shape, s.dtype) for s in shards], input_output_aliases={k: k for k in range(n)},
        scratch_shapes=[pltpu.SemaphoreType.DMA((3 * n,))] * 4)(*shards)


def exchange_layers(gs, *, name):
    n = len(gs)

    def body(*refs):
        ins, outs, (send, recv) = refs[:n], refs[n:2 * n], refs[2 * n:]
        x, y, c, me, sib, chips, cids = _place()
        cps = [_remote(ins[k].at[1 - c], outs[k], send.at[k], recv.at[k], sib) for k in range(n)]
        for cp in cps:
            cp.start()
        for cp in cps:
            cp.wait()

    return pl.pallas_call(
        body, name=name, in_specs=[HBM_SPEC] * n, out_specs=[HBM_SPEC] * n,
        out_shape=[SDS(g.shape[1:], g.dtype) for g in gs], scratch_shapes=[pltpu.SemaphoreType.DMA((n,))] * 2)(*gs)


def reduce_to_shards(ss, *, name):
    n = len(ss)

    def body(*refs):
        ins, outs, (send, recv) = refs[:n], refs[n:2 * n], refs[2 * n:]
        x, y, c, me, sib, chips, cids = _place()
        cps = []
        for k in range(n):
            for j, chip in enumerate(chips):
                cps.append(_remote(ins[k].at[cids[j]], outs[k].at[j], send.at[3 * k + j], recv.at[3 * k + j], (*chip, c)))
                cps[-1].start()
        for k in range(n):
            for j in range(3):
                slot = outs[k].at[j]
                _remote(slot, slot, send.at[3 * k + j], recv.at[3 * k + j], sib).wait_recv()
        for cp in cps:
            cp.wait_send()

    return pl.pallas_call(
        body, name=name, in_specs=[HBM_SPEC] * n, out_specs=[HBM_SPEC] * n,
        out_shape=[SDS((N_CHIP - 1,) + s.shape[1:], s.dtype) for s in ss],
        scratch_shapes=[pltpu.SemaphoreType.DMA((3 * n,))] * 2)(*ss)


def share_layers(fs, *, name):
    n = len(fs)

    def body(*refs):
        outs, (send, recv) = refs[n:2 * n], refs[2 * n:]
        x, y, c, me, sib, chips, cids = _place()
        cps = [_remote(outs[k].at[c], outs[k].at[c], send.at[k], recv.at[k], sib) for k in range(n)]
        for cp in cps:
            cp.start()
        for k in range(n):
            theirs = outs[k].at[1 - c]
            _remote(theirs, theirs, send.at[k], recv.at[k], sib).wait_recv()
        for cp in cps:
            cp.wait_send()

    return pl.pallas_call(
        body, name=name, in_specs=[HBM_SPEC] * n, out_specs=[HBM_SPEC] * n, out_shape=[SDS(f.shape, f.dtype) for f in fs],
        input_output_aliases={k: k for k in range(n)}, scratch_shapes=[pltpu.SemaphoreType.DMA((n,))] * 2)(*fs)


N_DEV = 8


def allreduce_small(buf, *, name):
    rows = buf.shape[0]

    def body(in_ref, out_ref, gath, send, recv):
        x, y, c = lax.axis_index("x"), lax.axis_index("y"), lax.axis_index("c")
        mine = 4 * x + 2 * y + c
        gath[mine] = in_ref[...]
        cps = []
        for k in range(1, N_DEV):
            to = (x ^ (k >> 2), y ^ ((k >> 1) & 1), c ^ (k & 1))
            cps.append(_remote(in_ref, gath.at[mine], send.at[k - 1], recv.at[k - 1], to))
            cps[-1].start()
        for k in range(1, N_DEV):
            theirs = gath.at[4 * (x ^ (k >> 2)) + 2 * (y ^ ((k >> 1) & 1)) + (c ^ (k & 1))]
            _remote(theirs, theirs, send.at[k - 1], recv.at[k - 1], (x, y, c)).wait_recv()
        for cp in cps:
            cp.wait_send()
        acc = gath[0]
        for d in range(1, N_DEV):
            acc = acc + gath[d]
        out_ref[...] = acc

    vm = pl.BlockSpec(memory_space=pltpu.VMEM)
    return pl.pallas_call(
        body, name=name, in_specs=[vm], out_specs=vm, out_shape=SDS(buf.shape, F32),
        scratch_shapes=[pltpu.VMEM((N_DEV, rows, 128), F32), pltpu.SemaphoreType.DMA((N_DEV - 1,)),
                        pltpu.SemaphoreType.DMA((N_DEV - 1,))])(buf)


def add_sibling(g, r, c_arr, *, name, tr=256):
    _, m, cdim = g.shape
    assert m % tr == 0

    def body(c_ref, g_ref, r_ref, o_ref):
        o_ref[...] = (g_ref[...] + r_ref[...]).astype(o_ref.dtype)

    return pl.pallas_call(
        body, name=name,
        grid_spec=pltpu.PrefetchScalarGridSpec(
            num_scalar_prefetch=1, grid=(m // tr,),
            in_specs=[pl.BlockSpec((None, tr, cdim), lambda i, c: (c[0], i, 0)), pl.BlockSpec((tr, cdim), lambda i, c: (i, 0))],
            out_specs=pl.BlockSpec((tr, cdim), lambda i, c: (i, 0))),
        out_shape=SDS((m, cdim), BF16), compiler_params=_params(("parallel",)))(c_arr, g, r)


def sum_slots(own, r, place_arr, *, name, tr=256):
    _, m, cdim = r.shape
    tr = next(cand for cand in (tr, 128, 64, 32, 16, 8) if m % cand == 0)

    def body(p_ref, own_ref, r_ref, o_ref):
        o_ref[...] = ((own_ref[...].astype(F32) + r_ref[0].astype(F32)) + r_ref[1].astype(F32)) + r_ref[2].astype(F32)

    return pl.pallas_call(
        body, name=name,
        grid_spec=pltpu.PrefetchScalarGridSpec(
            num_scalar_prefetch=1, grid=(m // tr,),
            in_specs=[pl.BlockSpec((None, tr, cdim), lambda i, p: (p[0], i, 0)),
                      pl.BlockSpec((N_CHIP - 1, tr, cdim), lambda i, p: (0, i, 0))],
            out_specs=pl.BlockSpec((None, tr, cdim), lambda i, p: (p[1], i, 0))),
        out_shape=SDS((N_LAYER, m, cdim), F32), compiler_params=_params(("parallel",)))(place_arr, own, r)


SEM_SPEC = pl.BlockSpec(memory_space=pltpu.SEMAPHORE)
ANY_SPEC = pl.BlockSpec(memory_space=pl.ANY)
DATAFLOW = pltpu.SideEffectType.DATAFLOW_SIDE_EFFECTING


def _in_hbm(a):
    return pltpu.with_memory_space_constraint(a, pltpu.HBM)


def _my_rows(ref_rows, c, mine=True):
    half = ref_rows // 2
    start = (c if mine else 1 - c) * half
    return pl.ds(pl.multiple_of(start, 8), half)


def place_layer_shard(w, layer, chip_arr, dtype, after, *, name):
    _, r, c = w.shape
    tr = next(cand for cand in (256, 128, 64, 32, 16, 8, r) if r % cand == 0)

    def body(chip_ref, w_ref, after_ref, o_ref):
        o_ref[...] = w_ref[...].astype(dtype)

    return pl.pallas_call(
        body, name=name,
        grid_spec=pltpu.PrefetchScalarGridSpec(
            num_scalar_prefetch=1, grid=(r // tr,),
            in_specs=[pl.BlockSpec((None, tr, c), lambda i, chip: (layer, i, 0)), ANY_SPEC],
            out_specs=pl.BlockSpec((None, tr, c), lambda i, chip: (chip[0], i, 0))),
        out_shape=SDS((N_CHIP, r, c), dtype), compiler_params=_params(("parallel",)))(chip_arr, w, after)


def _gather_pieces(refs, n_split, c, me, cids):
    mine, theirs = [], []
    for k, ref in enumerate(refs):
        if k < n_split:
            rows = _my_rows(ref.shape[1], c)
            mine.append(ref.at[me, rows])
            theirs.append([ref.at[cid, rows] for cid in cids])
        else:
            mine.append(ref.at[me])
            theirs.append([ref.at[cid] for cid in cids])
    return mine, theirs


def gather_start(bufs, n_split, after, *, name):
    n = len(bufs)

    def body(*refs):
        ins, send, recv, token = refs[:n], refs[n + 1], refs[n + 2], refs[-1]
        x, y, c, me, sib, chips, cids = _place()
        mine, _ = _gather_pieces(ins, n_split, c, me, cids)
        for k in range(n):
            for j, chip in enumerate(chips):
                _remote(mine[k], mine[k], send.at[3 * k + j], recv.at[3 * k + j], (*chip, c)).start()
        token[...] = jnp.zeros_like(token)

    out = pl.pallas_call(
        body, name=name, in_specs=[HBM_SPEC] * n + [ANY_SPEC],
        out_specs=[SEM_SPEC, SEM_SPEC] + [HBM_SPEC] * n + [pl.BlockSpec(memory_space=pltpu.VMEM)],
        out_shape=[pltpu.SemaphoreType.DMA((3 * n,)), pltpu.SemaphoreType.DMA((3 * n,))]
        + [pltpu.HBM(b.shape, b.dtype) for b in bufs] + [SDS((8, 128), F32)],
        input_output_aliases={k: k + 2 for k in range(n)},
        compiler_params=pltpu.CompilerParams(has_side_effects=DATAFLOW))(*[_in_hbm(b) for b in bufs], after)
    return out[0], out[1], list(out[2:2 + n]), out[-1]


def gather_wait(send, recv, bufs, n_split, after, *, name):
    n = len(bufs)

    def body(*refs):
        ins, send_ref, recv_ref = refs[:n], refs[n], refs[n + 1]
        x, y, c, me, sib, chips, cids = _place()
        mine, theirs = _gather_pieces(ins, n_split, c, me, cids)
        for k in range(n):
            for j in range(3):
                _remote(mine[k], mine[k], send_ref.at[3 * k + j], recv_ref.at[3 * k + j], sib).wait_send()
                _remote(theirs[k][j], theirs[k][j], send_ref.at[3 * k + j], recv_ref.at[3 * k + j], sib).wait_recv()

    return list(pl.pallas_call(
        body, name=name, in_specs=[HBM_SPEC] * n + [SEM_SPEC, SEM_SPEC, ANY_SPEC], out_specs=[HBM_SPEC] * n,
        out_shape=[pltpu.HBM(b.shape, b.dtype) for b in bufs], input_output_aliases={k: k for k in range(n)},
        compiler_params=pltpu.CompilerParams(has_side_effects=DATAFLOW))(*bufs, send, recv, after))


def gather_forward(bufs, *, name):
    n = len(bufs)

    def body(*refs):
        outs, (send, recv) = refs[n:2 * n], refs[2 * n:]
        x, y, c, me, sib, chips, cids = _place()
        cps = []
        for k in range(n):
            for j in range(3):
                piece = outs[k].at[cids[j], _my_rows(outs[k].shape[1], c)]
                cps.append(_remote(piece, piece, send.at[3 * k + j], recv.at[3 * k + j], sib))
                cps[-1].start()
        for k in range(n):
            for j in range(3):
                piece = outs[k].at[cids[j], _my_rows(outs[k].shape[1], c, mine=False)]
                _remote(piece, piece, send.at[3 * k + j], recv.at[3 * k + j], sib).wait_recv()
        for cp in cps:
            cp.wait_send()

    return list(pl.pallas_call(
        body, name=name, in_specs=[HBM_SPEC] * n, out_specs=[HBM_SPEC] * n, out_shape=[SDS(b.shape, b.dtype) for b in bufs],
        input_output_aliases={k: k for k in range(n)}, scratch_shapes=[pltpu.SemaphoreType.DMA((3 * n,))] * 2)(*bufs))


def reduce_exchange(gs, *, name):
    n = len(gs)

    def body(*refs):
        ins, outs, (send, recv) = refs[:n], refs[n:2 * n], refs[2 * n:]
        x, y, c, me, sib, chips, cids = _place()
        cps = [_remote(ins[k].at[pl.ds(0, N_CHIP), _my_rows(ins[k].shape[1], c, mine=False)], outs[k], send.at[k],
                       recv.at[k], sib) for k in range(n)]
        for cp in cps:
            cp.start()
        for cp in cps:
            cp.wait()

    return list(pl.pallas_call(
        body, name=name, in_specs=[HBM_SPEC] * n, out_specs=[HBM_SPEC] * n,
        out_shape=[SDS((N_CHIP, g.shape[1] // 2, g.shape[2]), g.dtype) for g in gs],
        scratch_shapes=[pltpu.SemaphoreType.DMA((n,))] * 2)(*gs))


def _half_tile(half):
    return next(cand for cand in (256, 176, 128, 64, 32, 16) if half % cand == 0)


def reduce_add(g, r, c_arr, *, name):
    _, rows, cdim = g.shape
    half = rows // 2
    tr = _half_tile(half)

    def body(c_ref, g_ref, r_ref, o_ref):
        o_ref[...] = (g_ref[...] + r_ref[...]).astype(o_ref.dtype)

    return pl.pallas_call(
        body, name=name,
        grid_spec=pltpu.PrefetchScalarGridSpec(
            num_scalar_prefetch=1, grid=(N_CHIP, half // tr),
            in_specs=[pl.BlockSpec((None, tr, cdim), lambda j, i, c: (j, c[0] * (half // tr) + i, 0)),
                      pl.BlockSpec((None, tr, cdim), lambda j, i, c: (j, i, 0))],
            out_specs=pl.BlockSpec((None, tr, cdim), lambda j, i, c: (j, i, 0))),
        out_shape=SDS((N_CHIP, half, cdim), BF16), compiler_params=_params(("parallel", "parallel")))(c_arr, g, r)


def reduce_start(ss, *, name):
    n = len(ss)

    def body(*refs):
        ins, lands, send, recv, token = refs[:n], refs[n:2 * n], refs[2 * n], refs[2 * n + 1], refs[-1]
        x, y, c, me, sib, chips, cids = _place()
        for k in range(n):
            for j, chip in enumerate(chips):
                _remote(ins[k].at[cids[j]], lands[k].at[j], send.at[3 * k + j], recv.at[3 * k + j], (*chip, c)).start()
        token[...] = jnp.zeros_like(token)

    lands = [_in_hbm(lax.empty((N_CHIP - 1,) + s.shape[1:], s.dtype)) for s in ss]
    out = pl.pallas_call(
        body, name=name, in_specs=[HBM_SPEC] * (2 * n),
        out_specs=[SEM_SPEC, SEM_SPEC] + [HBM_SPEC] * (2 * n) + [pl.BlockSpec(memory_space=pltpu.VMEM)],
        out_shape=[pltpu.SemaphoreType.DMA((3 * n,)), pltpu.SemaphoreType.DMA((3 * n,))]
        + [pltpu.HBM(b.shape, b.dtype) for b in list(ss) + lands] + [SDS((8, 128), F32)],
        input_output_aliases={k: k + 2 for k in range(2 * n)},
        compiler_params=pltpu.CompilerParams(has_side_effects=DATAFLOW))(*[_in_hbm(s) for s in ss], *lands)
    return out[0], out[1], list(out[2:2 + n]), list(out[2 + n:2 + 2 * n]), out[-1]


def reduce_wait(send, recv, ss, lands, after, *, name):
    n = len(ss)

    def body(*refs):
        ins, land_refs, send_ref, recv_ref = refs[:n], refs[n:2 * n], refs[2 * n], refs[2 * n + 1]
        x, y, c, me, sib, chips, cids = _place()
        for k in range(n):
            for j in range(3):
                _remote(ins[k].at[cids[j]], land_refs[k].at[j], send_ref.at[3 * k + j], recv_ref.at[3 * k + j],
                        sib).wait_send()
                _remote(ins[k].at[cids[j]], land_refs[k].at[j], send_ref.at[3 * k + j], recv_ref.at[3 * k + j],
                        sib).wait_recv()

    out = pl.pallas_call(
        body, name=name, in_specs=[HBM_SPEC] * (2 * n) + [SEM_SPEC, SEM_SPEC, ANY_SPEC], out_specs=[HBM_SPEC] * (2 * n),
        out_shape=[pltpu.HBM(b.shape, b.dtype) for b in list(ss) + list(lands)],
        input_output_aliases={k: k for k in range(2 * n)},
        compiler_params=pltpu.CompilerParams(has_side_effects=DATAFLOW))(*ss, *lands, send, recv, after)
    return list(out[:n]), list(out[n:])


def reduce_sum(own, land, place_arr, layer, acc, *, name):
    _, half, cdim = land.shape
    tr = _half_tile(half)

    def body(p_ref, own_ref, land_ref, *rest):
        o_ref = rest[-1]
        o_ref[...] = ((own_ref[...].astype(F32) + land_ref[0].astype(F32)) + land_ref[1].astype(F32)) + land_ref[2].astype(F32)

    in_specs = [pl.BlockSpec((None, tr, cdim), lambda i, p: (p[0], i, 0)),
                pl.BlockSpec((N_CHIP - 1, tr, cdim), lambda i, p: (0, i, 0))]
    args = [place_arr, own, land]
    if acc is not None:
        in_specs.append(ANY_SPEC)
        args.append(acc)
    return pl.pallas_call(
        body, name=name,
        grid_spec=pltpu.PrefetchScalarGridSpec(
            num_scalar_prefetch=1, grid=(half // tr,), in_specs=in_specs,
            out_specs=pl.BlockSpec((None, tr, cdim), lambda i, p: (layer, p[1] * (half // tr) + i, 0))),
        out_shape=SDS((N_LAYER, 2 * half, cdim), F32), input_output_aliases={} if acc is None else {3: 0},
        compiler_params=_params(("parallel",)))(*args)


def reduce_share(fs, *, name):
    n = len(fs)

    def body(*refs):
        outs, (send, recv) = refs[n:2 * n], refs[2 * n:]
        x, y, c, me, sib, chips, cids = _place()
        cps = []
        for k in range(n):
            piece = outs[k].at[pl.ds(0, N_LAYER), _my_rows(outs[k].shape[1], c)]
            cps.append(_remote(piece, piece, send.at[k], recv.at[k], sib))
            cps[-1].start()
        for k in range(n):
            theirs = outs[k].at[pl.ds(0, N_LAYER), _my_rows(outs[k].shape[1], c, mine=False)]
            _remote(theirs, theirs, send.at[k], recv.at[k], sib).wait_recv()
        for cp in cps:
            cp.wait_send()

    return list(pl.pallas_call(
        body, name=name, in_specs=[HBM_SPEC] * n, out_specs=[HBM_SPEC] * n, out_shape=[SDS(f.shape, f.dtype) for f in fs],
        input_output_aliases={k: k for k in range(n)}, scratch_shapes=[pltpu.SemaphoreType.DMA((n,))] * 2)(*fs))


WEIGHTS = ("ffn1_norm", "ffn1_w_gate", "ffn1_w_up", "ffn1_w_down", "mix_norm", "w_in", "lru_conv_w", "lru_conv_b", "lru_w_a",
           "lru_b_a", "lru_w_x", "lru_b_x", "lru_lambda", "attn_sinks", "rel_bias", "dn_conv_w", "dn_a_log", "dn_dt_bias",
           "dn_norm", "w_out", "ffn2_norm", "ffn2_w_gate", "ffn2_w_up", "ffn2_w_down", "ple_norm", "ple_w_gate",
           "ple_w_proj", "final_norm")
CONV_SHARDED = ("lru_conv_w", "dn_conv_w")
FFN_TRANSPOSED = ("ffn1_w_gate", "ffn1_w_up", "ffn2_w_gate", "ffn2_w_up")
SMALL = tuple(k for k in WEIGHTS if k not in SHARDED)


def _pack(arrs):
    flat = []
    for a in arrs:
        v = a.reshape(-1)
        flat.append(jnp.pad(v, (0, -v.shape[0] % 128)))
    v = jnp.concatenate(flat)
    v = jnp.pad(v, (0, -v.shape[0] % 1024))
    return v.reshape(-1, 128)


def _unpack(buf, shapes):
    v, out, off = buf.reshape(-1), [], 0
    for s in shapes:
        n = int(np.prod(s))
        out.append(v[off:off + n].reshape(s))
        off += n + (-n % 128)
    return out


def _chip_cols(a):
    n, l, r, c = a.shape
    return a.transpose(1, 2, 0, 3).reshape(l, r, n * c)


def _chip_rows(a):
    n, l, r, c = a.shape
    return a.transpose(1, 0, 2, 3).reshape(l, n * r, c)


def kernel(x, p, ffn1_norm, ffn1_w_gate, ffn1_w_up, ffn1_w_down, mix_norm, w_in, lru_conv_w, lru_conv_b, lru_w_a, lru_b_a, lru_w_x, lru_b_x, lru_lambda, attn_sinks, rel_bias, dn_conv_w, dn_a_log, dn_dt_bias, dn_norm, w_out, ffn2_norm, ffn2_w_gate, ffn2_w_up, ffn2_w_down, ple_norm, ple_w_gate, ple_w_proj, final_norm, loss_target, m_ffn1_norm, m_ffn1_w_gate, m_ffn1_w_up, m_ffn1_w_down, m_mix_norm, m_w_in, m_lru_conv_w, m_lru_conv_b, m_lru_w_a, m_lru_b_a, m_lru_w_x, m_lru_b_x, m_lru_lambda, m_attn_sinks, m_rel_bias, m_dn_conv_w, m_dn_a_log, m_dn_dt_bias, m_dn_norm, m_w_out, m_ffn2_norm, m_ffn2_w_gate, m_ffn2_w_up, m_ffn2_w_down, m_ple_norm, m_ple_w_gate, m_ple_w_proj, m_final_norm, v_ffn1_norm, v_ffn1_w_gate, v_ffn1_w_up, v_ffn1_w_down, v_mix_norm, v_w_in, v_lru_conv_w, v_lru_conv_b, v_lru_w_a, v_lru_b_a, v_lru_w_x, v_lru_b_x, v_lru_lambda, v_attn_sinks, v_rel_bias, v_dn_conv_w, v_dn_a_log, v_dn_dt_bias, v_dn_norm, v_w_out, v_ffn2_norm, v_ffn2_w_gate, v_ffn2_w_up, v_ffn2_w_down, v_ple_norm, v_ple_w_gate, v_ple_w_proj, v_final_norm):
    given = dict(locals())
    stored = lambda k, a: jnp.swapaxes(a, 1, 2) if k in FFN_TRANSPOSED else a
    ws = {k: stored(k, given[k]) for k in WEIGHTS}
    ms = {k: stored(k, given["m_" + k]) for k in WEIGHTS}
    vs = {k: stored(k, given["v_" + k]) for k in WEIGHTS}
    nb, seq, d = x.shape
    t = nb * seq
    cx, cy, cc = lax.axis_index("x"), lax.axis_index("y"), lax.axis_index("c")
    chip = 2 * cx + cy

    chip_arr = chip.astype(jnp.int32).reshape(1)
    c_arr = cc.astype(jnp.int32).reshape(1)
    place_arr = jnp.stack([chip, cc]).astype(jnp.int32)
    groups = [(l, part) for l in range(N_LAYER) for part in range(len(WEIGHT_PARTS))]
    placed, started = {}, {}

    def place_group(i, after):
        l, part = groups[i]
        for k in WEIGHT_PARTS[part]:
            placed[l, k] = place_layer_shard(ws[k], l, chip_arr, F32 if k in CONV_SHARDED else BF16, after,
                                             name=f"place_{k}_{l}")

    def start_group(i, after):
        l, part = groups[i]
        ks = WEIGHT_PARTS[part]
        n_split = sum(k in SHARDED for k in ks)
        started[i] = (ks, n_split) + gather_start([placed[l, k] for k in ks], n_split, after, name=f"gather_start_{l}_{part}")

    place_group(0, jnp.zeros((8, 128), F32))
    start_group(0, jnp.zeros((8, 128), F32))
    for i in range(1, len(groups)):
        place_group(i, started[0][-1])

    def layer_weights(l, part, h):
        i = groups.index((l, part))
        ks, n_split, send, recv, bufs, _ = started[i]
        bufs = gather_wait(send, recv, bufs, n_split, h, name=f"gather_wait_{l}_{part}")
        tie = jnp.zeros((8, 128), F32)
        for nxt in [j for j in range(i + 1, len(groups)) if j not in started and groups[j][0] == groups[min(i + 1, len(groups) - 1)][0]]:
            start_group(nxt, bufs[0] if nxt == i + 1 else started[nxt - 1][-1])
            tie = started[nxt][-1]
        wl = dict(zip(ks, gather_forward(bufs[:n_split], name=f"gather_forward_{l}_{part}") + bufs[n_split:]))
        for k in ("w_in", "ple_w_proj", "lru_conv_w", "dn_conv_w"):
            if k in wl:
                wl[k] = wl[k].transpose(1, 0, 2).reshape(wl[k].shape[1], -1)
        for k in ("w_out", "ple_w_gate"):
            if k in wl:
                wl[k] = wl[k].reshape(-1, wl[k].shape[-1])
        if "w_in" in wl:
            wl["w_in"] = jnp.pad(wl["w_in"], ((0, 0), (0, D_IN_PAD - D_IN)))
        wl[f"tie{part}"] = tie
        return wl

    pending, finished, tokens = [], {k: None for k in SHARDED}, []

    def finish_reduce(after):
        ks, send, recv, sums, lands, l, part = pending.pop(0)
        sums, lands = reduce_wait(send, recv, sums, lands, after, name=f"reduce_wait_{l}_{part}")
        for k, s, land in zip(ks, sums, lands):
            finished[k] = reduce_sum(s, land, place_arr, l, finished[k], name=f"reduce_sum_{k}_{l}")

    def layer_grads(l, part, g, dh):
        ks = GRAD_PARTS[part]
        gs = [g[k] for k in ks]
        theirs = reduce_exchange(gs, name=f"reduce_exchange_{l}_{part}")
        sums = [reduce_add(a, b, c_arr, name=f"reduce_add_{k}_{l}") for k, a, b in zip(ks, gs, theirs)]
        send, recv, sums, lands, token = reduce_start(sums, name=f"reduce_start_{l}_{part}")
        pending.append((ks, send, recv, sums, lands, l, part))
        while len(pending) > 2:
            finish_reduce(dh)
        tokens.append(token)
        return token

    small_w = {k: ws[k] for k in SMALL if k not in CONV_SHARDED}
    bmap = jnp.asarray(_rel_bucket_map())
    loss, gx, grads = local_step(x.reshape(t, d), p.reshape(N_LAYER, t, PLE_DIM), loss_target.reshape(t, d), small_w,
                                 layer_weights, layer_grads, bmap, nb, seq)
    g_out, delta, new_m, new_v = {}, {}, {}, {}

    small_shapes = [grads[k].shape for k in SMALL]
    packed = _pack([grads[k] for k in SMALL]) + tokens[-1][0:1, 0:1]
    g_small = dict(zip(SMALL, _unpack(allreduce_small(packed, name="allreduce_small"), small_shapes)))
    for k in CONV_SHARDED:
        width = ws[k].shape[-1]
        g_small[k] = lax.dynamic_slice_in_dim(g_small[k], chip * width, width, axis=2)
    g_out.update(g_small)
    shapes = [ws[k].shape for k in SMALL]
    res = adamw(*[_pack([src[k] for k in SMALL]) for src in (ws, g_out, ms, vs)], name="adamw_small")
    for dst, r in zip((delta, new_m, new_v), res):
        dst.update(zip(SMALL, _unpack(r, shapes)))

    after = res[0]
    for part, ks in enumerate(GRAD_PARTS):
        while pending and pending[0][0] == ks:
            finish_reduce(after)
        g_out.update(zip(ks, reduce_share([finished[k] for k in ks], name=f"reduce_share_{part}")))
        for k in ks:
            two_d = lambda a: a.reshape(-1, a.shape[-1])
            res = adamw(two_d(ws[k]), two_d(g_out[k]), two_d(ms[k]), two_d(vs[k]), name=f"adamw_{k}")
            delta[k], new_m[k], new_v[k] = (r.reshape(ws[k].shape) for r in res)
        after = res[0]

    total = lax.psum(loss[0, 0], ("x", "y", "c"))
    return (total, gx.reshape(nb, seq, d), *[stored(k, out[k]) for out in (g_out, delta, new_m, new_v) for k in WEIGHTS])
```

```python
import functools
import math

import numpy as np
import jax
import jax.numpy as jnp
from jax import lax
from jax.experimental import pallas as pl
from jax.experimental.pallas import tpu as pltpu

F32 = jnp.float32
BF16 = jnp.bfloat16

EPS = 1e-6
D_MODEL = 1024
D_FF = 2816
N_CHIP = 4
FF_BLK = D_FF // N_CHIP
HEAD = 64
LRU_W = 256
ATT_W = 512
ATT_HEADS = 8
KV_HEADS = 2
ATT_GROUP = 4
BLOCK_Q = 128
DN_HEADS = 4
DN_CHUNK = 64
D_IN = 2312
D_IN_PAD = 2560
PLE_DIM = 256
REL_BUCKETS = 32
LRU_C = 8.0
N_LAYER = 2

ADAM_LR, ADAM_B1, ADAM_B2, ADAM_EPS, ADAM_WD, ADAM_STEP = 0.001, 0.9, 0.999, 1e-08, 0.01, 10

VMEM_LIMIT = 56 << 20
MESH = pl.DeviceIdType.MESH
SDS = jax.ShapeDtypeStruct


def _dot(a, b, ca=1, cb=0, hi=False):
    dims = (((ca,), (cb,)), ((), ()))
    one = lambda u, v: lax.dot_general(u, v, dims, preferred_element_type=F32)
    a_hi, b_hi = a.astype(BF16), b.astype(BF16)
    if not hi:
        return one(a_hi, b_hi)
    a_lo = (a - a_hi.astype(F32)).astype(BF16)
    b_lo = (b - b_hi.astype(F32)).astype(BF16)
    return one(a_hi, b_hi) + (one(a_hi, b_lo) + one(a_lo, b_hi))


def _nn(a, b, hi=False):
    return _dot(a, b, 1, 0, hi)


def _nt(a, b, hi=False):
    return _dot(a, b, 1, 1, hi)


def _tn(a, b, hi=False):
    return _dot(a, b, 0, 0, hi)


def _sigmoid(x):
    return jax.nn.sigmoid(x)


def _softplus(x):
    return jnp.maximum(x, 0.0) + jnp.log1p(jnp.exp(-jnp.abs(x)))


def _neg_expm1(z):
    series = -z * (1.0 + z * (0.5 + z * (1.0 / 6.0 + z * (1.0 / 24.0 + z * (1.0 / 120.0)))))
    return jnp.where(z > -0.05, series, 1.0 - jnp.exp(z))


_GELU_C = math.sqrt(2.0 / math.pi)


def _gelu(x):
    t = jnp.tanh(_GELU_C * (x + 0.044715 * x * x * x))
    return 0.5 * x * (1.0 + t), t


def _gelu_grad(x, t):
    return 0.5 * (1.0 + t) + 0.5 * x * (1.0 - t * t) * _GELU_C * (1.0 + 3.0 * 0.044715 * x * x)


def _rms_fwd(h, g):
    r = lax.rsqrt(jnp.mean(h * h, axis=-1, keepdims=True) + EPS)
    xh = h * r
    return xh * g, xh, r


def _rms_bwd(dn, xh, r, g):
    dxh = dn * g
    dh = r * (dxh - xh * jnp.mean(dxh * xh, axis=-1, keepdims=True))
    return dh, jnp.sum(dn * xh, axis=0, keepdims=True)


def _shift_down(x, d, fill=0.0):
    row = lax.broadcasted_iota(jnp.int32, x.shape, 0)
    return jnp.where(row >= d, pltpu.roll(x, d, 0), fill)


def _shift_up(x, d, fill=0.0):
    n = x.shape[0]
    row = lax.broadcasted_iota(jnp.int32, x.shape, 0)
    return jnp.where(row < n - d, pltpu.roll(x, n - d, 0), fill)


def _conv_fwd(x, w):
    y = x * w[3]
    for k in range(3):
        y = y + _shift_down(x, 3 - k) * w[k]
    return y


def _conv_bwd(dy, x, w):
    dx = dy * w[3]
    rows = [None] * 4
    rows[3] = jnp.sum(dy * x, axis=0, keepdims=True)
    for k in range(3):
        dx = dx + _shift_up(dy, 3 - k) * w[k]
        rows[k] = jnp.sum(dy * _shift_down(x, 3 - k), axis=0, keepdims=True)
    r4 = lax.broadcasted_iota(jnp.int32, (4, x.shape[1]), 0)
    dw = jnp.zeros((4, x.shape[1]), F32)
    for k in range(4):
        dw = jnp.where(r4 == k, rows[k], dw)
    return dx, dw


FFN_SPLIT = 2


def _interleave(gens):
    pending = list(gens)
    while pending:
        for g in list(pending):
            if next(g, StopIteration) is StopIteration:
                pending.remove(g)


def _params(sem=None, vmem=VMEM_LIMIT):
    return pltpu.CompilerParams(dimension_semantics=sem, vmem_limit_bytes=vmem)


def _whole(shape):
    nd = len(shape)
    return pl.BlockSpec(shape, lambda *_: (0,) * nd)


def matmul(a, b, *, name, ta=False, tb=False, residual=None, out_dtype=F32, tm=512, tn=512, tk=512):
    m, k = (a.shape[1], a.shape[0]) if ta else a.shape
    n = b.shape[0] if tb else b.shape[1]
    tm, tn, tk = min(tm, m), min(tn, n), min(tk, k)
    assert m % tm == 0 and n % tn == 0 and k % tk == 0, (m, n, k, tm, tn, tk)
    nk = k // tk

    def body(*refs):
        if residual is None:
            a_ref, b_ref, o_ref, acc = refs
        else:
            a_ref, b_ref, r_ref, o_ref, acc = refs
        kk = pl.program_id(2)

        @pl.when(kk == 0)
        def _():
            acc[...] = jnp.zeros_like(acc)

        acc[...] += _dot(a_ref[...], b_ref[...], 0 if ta else 1, 1 if tb else 0)

        @pl.when(kk == nk - 1)
        def _():
            out = acc[...]
            if residual is not None:
                out = out + r_ref[...]
            o_ref[...] = out.astype(out_dtype)

    a_spec = pl.BlockSpec((tk, tm), lambda i, j, kk: (kk, i)) if ta else pl.BlockSpec((tm, tk), lambda i, j, kk: (i, kk))
    b_spec = pl.BlockSpec((tn, tk), lambda i, j, kk: (j, kk)) if tb else pl.BlockSpec((tk, tn), lambda i, j, kk: (kk, j))
    o_spec = pl.BlockSpec((tm, tn), lambda i, j, kk: (i, j))
    in_specs, args = [a_spec, b_spec], [a, b]
    if residual is not None:
        in_specs.append(o_spec)
        args.append(residual)
    return pl.pallas_call(
        body, name=name, grid=(m // tm, n // tn, nk), in_specs=in_specs, out_specs=o_spec,
        out_shape=SDS((m, n), out_dtype), scratch_shapes=[pltpu.VMEM((tm, tn), F32)],
        compiler_params=_params(("parallel", "parallel", "arbitrary")))(*args)


def norm_matmul(h, gain, w, *, name, tm=512, tn=512):
    t, d = h.shape
    tm = min(tm, t)
    n = w.shape[1]
    assert t % tm == 0 and n % tn == 0

    def body(h_ref, g_ref, w_ref, u_ref, n_ref):
        @pl.when(pl.program_id(1) == 0)
        def _():
            n_ref[...] = _rms_fwd(h_ref[...], g_ref[...])[0].astype(BF16)

        u_ref[...] = _nn(n_ref[...], w_ref[...])

    return pl.pallas_call(
        body, name=name, grid=(t // tm, n // tn),
        in_specs=[pl.BlockSpec((tm, d), lambda i, j: (i, 0)), _whole((1, d)), pl.BlockSpec((d, tn), lambda i, j: (0, j))],
        out_specs=[pl.BlockSpec((tm, tn), lambda i, j: (i, j)), pl.BlockSpec((tm, d), lambda i, j: (i, 0))],
        out_shape=[SDS((t, n), F32), SDS((t, d), BF16)],
        compiler_params=_params(("parallel", "arbitrary")))(h, gain, w)


def rms_bwd(h, gain, dn, dres, *, name, tm=512):
    t, d = h.shape
    tm = min(tm, t)

    def body(h_ref, g_ref, dn_ref, dr_ref, dh_ref, dg_ref):
        @pl.when(pl.program_id(0) == 0)
        def _():
            dg_ref[...] = jnp.zeros_like(dg_ref)

        g = g_ref[...]
        _, xh, r = _rms_fwd(h_ref[...], g)
        dh, dg = _rms_bwd(dn_ref[...], xh, r, g)
        dh_ref[...] = dr_ref[...] + dh
        dg_ref[...] += dg

    row = pl.BlockSpec((tm, d), lambda i: (i, 0))
    return pl.pallas_call(
        body, name=name, grid=(t // tm,), in_specs=[row, _whole((1, d)), row, row],
        out_specs=[row, _whole((1, d))], out_shape=[SDS((t, d), F32), SDS((1, d), F32)],
        compiler_params=_params(("arbitrary",)))(h, gain, dn, dres)


def ffn_fwd(h, gain, wg, wu, wd, *, name, tm=512):
    t, d = h.shape
    tm = min(tm, t)

    def body(h_ref, g_ref, wg_ref, wu_ref, wd_ref, o_ref, n_sc, acc):
        j = pl.program_id(1)

        @pl.when(j == 0)
        def _():
            n_sc[...] = _rms_fwd(h_ref[...], g_ref[...])[0].astype(BF16)
            acc[...] = jnp.zeros_like(acc)

        def part(rows):
            n = n_sc[rows, :]
            a = _nt(n, wg_ref[...])
            b = _nt(n, wu_ref[...])
            yield
            acc[rows, :] += _nn(a * _sigmoid(a) * b, wd_ref[...])

        _interleave([part(pl.ds(k * (tm // FFN_SPLIT), tm // FFN_SPLIT)) for k in range(FFN_SPLIT)])

        @pl.when(j == N_CHIP - 1)
        def _():
            o_ref[...] = h_ref[...] + 0.5 * acc[...]

    row = pl.BlockSpec((tm, d), lambda i, j: (i, 0))
    return pl.pallas_call(
        body, name=name, grid=(t // tm, N_CHIP),
        in_specs=[row, _whole((1, d)),
                  pl.BlockSpec((None, FF_BLK, d), lambda i, j: (j, 0, 0)),
                  pl.BlockSpec((None, FF_BLK, d), lambda i, j: (j, 0, 0)),
                  pl.BlockSpec((None, FF_BLK, d), lambda i, j: (j, 0, 0))],
        out_specs=row, out_shape=SDS((t, d), F32),
        scratch_shapes=[pltpu.VMEM((tm, d), BF16), pltpu.VMEM((tm, d), F32)],
        compiler_params=_params(("parallel", "arbitrary")))(h, gain, wg, wu, wd)


def ffn_bwd_act(h, gain, dout, wg, wu, wd, *, name, tm=512):
    t, d = h.shape
    tm = min(tm, t)

    def body(h_ref, g_ref, do_ref, wg_ref, wu_ref, wd_ref, dh_ref, n_ref, da_ref, db_ref, s_ref, dg_ref, dn_acc):
        i, j = pl.program_id(0), pl.program_id(1)

        @pl.when((i == 0) & (j == 0))
        def _():
            dg_ref[...] = jnp.zeros_like(dg_ref)

        @pl.when(j == 0)
        def _():
            n_ref[...] = _rms_fwd(h_ref[...], g_ref[...])[0].astype(BF16)
            dn_acc[...] = jnp.zeros_like(dn_acc)

        def part(rows):
            n = n_ref[rows, :]
            a = _nt(n, wg_ref[...])
            b = _nt(n, wu_ref[...])
            ds = _nt(0.5 * do_ref[rows, :], wd_ref[...])
            yield
            sig = _sigmoid(a)
            sa = a * sig
            db = ds * sa
            da = ds * b * (sig * (1.0 + a * (1.0 - sig)))
            s_ref[rows, :] = (sa * b).astype(BF16)
            da_ref[rows, :] = da.astype(BF16)
            db_ref[rows, :] = db.astype(BF16)
            yield
            dn_acc[rows, :] += _nn(da, wg_ref[...]) + _nn(db, wu_ref[...])

        _interleave([part(pl.ds(k * (tm // FFN_SPLIT), tm // FFN_SPLIT)) for k in range(FFN_SPLIT)])

        @pl.when(j == N_CHIP - 1)
        def _():
            g = g_ref[...]
            _, xh, r = _rms_fwd(h_ref[...], g)
            dh, dg = _rms_bwd(dn_acc[...], xh, r, g)
            dh_ref[...] = do_ref[...] + dh
            dg_ref[...] += dg

    row = pl.BlockSpec((tm, d), lambda i, j: (i, 0))
    blk = pl.BlockSpec((None, tm, FF_BLK), lambda i, j: (j, i, 0))
    act = SDS((N_CHIP, t, FF_BLK), BF16)
    return pl.pallas_call(
        body, name=name, grid=(t // tm, N_CHIP),
        in_specs=[row, _whole((1, d)), row,
                  pl.BlockSpec((None, FF_BLK, d), lambda i, j: (j, 0, 0)),
                  pl.BlockSpec((None, FF_BLK, d), lambda i, j: (j, 0, 0)),
                  pl.BlockSpec((None, FF_BLK, d), lambda i, j: (j, 0, 0))],
        out_specs=[row, row, blk, blk, blk, _whole((1, d))],
        out_shape=[SDS((t, d), F32), SDS((t, d), BF16), act, act, act, SDS((1, d), F32)],
        scratch_shapes=[pltpu.VMEM((tm, d), F32)],
        compiler_params=_params(("arbitrary", "arbitrary")))(h, gain, dout, wg, wu, wd)


def ffn_bwd_w(n, da, db, s, dout, *, name, tk=512):
    t, d = n.shape
    tk = min(tk, t)

    def body(n_ref, da_ref, db_ref, s_ref, do_ref, dwg_ref, dwu_ref, dwd_ref):
        @pl.when(pl.program_id(1) == 0)
        def _():
            dwg_ref[...] = jnp.zeros_like(dwg_ref)
            dwu_ref[...] = jnp.zeros_like(dwu_ref)
            dwd_ref[...] = jnp.zeros_like(dwd_ref)

        nn = n_ref[...]
        dwg_ref[...] += _tn(da_ref[...], nn)
        dwu_ref[...] += _tn(db_ref[...], nn)
        dwd_ref[...] += _tn(s_ref[...], 0.5 * do_ref[...])

    row = pl.BlockSpec((tk, d), lambda j, kk: (kk, 0))
    blk = pl.BlockSpec((None, tk, FF_BLK), lambda j, kk: (j, kk, 0))
    return pl.pallas_call(
        body, name=name, grid=(N_CHIP, t // tk), in_specs=[row, blk, blk, blk, row],
        out_specs=[pl.BlockSpec((None, FF_BLK, d), lambda j, kk: (j, 0, 0)),
                   pl.BlockSpec((None, FF_BLK, d), lambda j, kk: (j, 0, 0)),
                   pl.BlockSpec((None, FF_BLK, d), lambda j, kk: (j, 0, 0))],
        out_shape=[SDS((N_CHIP, FF_BLK, d), F32)] * 3,
        compiler_params=_params(("parallel", "arbitrary")))(n, da, db, s, dout)


def ple_fwd(h, gain, wpg, pl_in, wpp, *, name, tm=512):
    t, d = h.shape
    tm = min(tm, t)
    pd = pl_in.shape[1]

    def body(h_ref, g_ref, wpg_ref, p_ref, wpp_ref, o_ref):
        hh = h_ref[...]
        n = _rms_fwd(hh, g_ref[...])[0]
        gate = _sigmoid(_nn(n, wpg_ref[...]))
        o_ref[...] = hh + gate * _nn(p_ref[...], wpp_ref[...])

    row = pl.BlockSpec((tm, d), lambda i: (i, 0))
    return pl.pallas_call(
        body, name=name, grid=(t // tm,),
        in_specs=[row, _whole((1, d)), _whole((d, d)), pl.BlockSpec((tm, pd), lambda i: (i, 0)), _whole((pd, d))],
        out_specs=row, out_shape=SDS((t, d), F32), compiler_params=_params(("parallel",)))(h, gain, wpg, pl_in, wpp)


def ple_bwd(h, gain, wpg, pl_in, wpp, dout, *, name, tm=512):
    t, d = h.shape
    tm = min(tm, t)
    pd = pl_in.shape[1]

    def body(h_ref, g_ref, wpg_ref, p_ref, wpp_ref, do_ref, dh_ref, n_ref, dga_ref, dpp_ref, dg_ref):
        @pl.when(pl.program_id(0) == 0)
        def _():
            dg_ref[...] = jnp.zeros_like(dg_ref)

        g = g_ref[...]
        n, xh, r = _rms_fwd(h_ref[...], g)
        gate = _sigmoid(_nn(n, wpg_ref[...]))
        pp = _nn(p_ref[...], wpp_ref[...])
        do = do_ref[...]
        dga = do * pp * gate * (1.0 - gate)
        dh, dg = _rms_bwd(_nt(dga, wpg_ref[...]), xh, r, g)
        dh_ref[...] = do + dh
        n_ref[...] = n.astype(BF16)
        dga_ref[...] = dga.astype(BF16)
        dpp_ref[...] = (do * gate).astype(BF16)
        dg_ref[...] += dg

    row = pl.BlockSpec((tm, d), lambda i: (i, 0))
    return pl.pallas_call(
        body, name=name, grid=(t // tm,),
        in_specs=[row, _whole((1, d)), _whole((d, d)), pl.BlockSpec((tm, pd), lambda i: (i, 0)), _whole((pd, d)), row],
        out_specs=[row, row, row, row, _whole((1, d))],
        out_shape=[SDS((t, d), F32), SDS((t, d), BF16), SDS((t, d), BF16), SDS((t, d), BF16), SDS((1, d), F32)],
        compiler_params=_params(("arbitrary",)))(h, gain, wpg, pl_in, wpp, dout)


def loss_head(h, gain, target, *, name, tm=512):
    t, d = h.shape
    tm = min(tm, t)

    def body(h_ref, g_ref, t_ref, dh_ref, dg_ref, l_ref):
        @pl.when(pl.program_id(0) == 0)
        def _():
            dg_ref[...] = jnp.zeros_like(dg_ref)
            l_ref[...] = jnp.zeros_like(l_ref)

        g = g_ref[...]
        y, xh, r = _rms_fwd(h_ref[...], g)
        err = y - t_ref[...]
        l_ref[...] += 0.5 * jnp.sum(jnp.mean(err * err, axis=-1, keepdims=True), axis=0, keepdims=True)
        dh, dg = _rms_bwd(err * (1.0 / d), xh, r, g)
        dh_ref[...] = dh
        dg_ref[...] += dg

    row = pl.BlockSpec((tm, d), lambda i: (i, 0))
    return pl.pallas_call(
        body, name=name, grid=(t // tm,), in_specs=[row, _whole((1, d)), row],
        out_specs=[row, _whole((1, d)), _whole((1, 1))],
        out_shape=[SDS((t, d), F32), SDS((1, d), F32), SDS((1, 1), F32)],
        compiler_params=_params(("arbitrary",)))(h, gain, target)


def adamw(w, g, m, v, *, name):
    r, c = w.shape
    tr = r
    for cand in (512, 256, 128, 64, 32, 16, 8):
        if r % cand == 0:
            tr = cand
            break

    def body(w_ref, g_ref, m_ref, v_ref, d_ref, nm_ref, nv_ref):
        gg = g_ref[...]
        mm = ADAM_B1 * m_ref[...] + (1.0 - ADAM_B1) * gg
        vv = ADAM_B2 * v_ref[...] + (1.0 - ADAM_B2) * (gg * gg)
        m_hat = mm / (1.0 - ADAM_B1 ** ADAM_STEP)
        v_hat = vv / (1.0 - ADAM_B2 ** ADAM_STEP)
        d_ref[...] = -ADAM_LR * (m_hat / (jnp.sqrt(v_hat) + ADAM_EPS) + ADAM_WD * w_ref[...])
        nm_ref[...] = mm
        nv_ref[...] = vv

    blk = pl.BlockSpec((tr, c), lambda i: (i, 0))
    out = SDS((r, c), F32)
    return pl.pallas_call(body, name=name, grid=(r // tr,), in_specs=[blk] * 4, out_specs=[blk] * 3,
                          out_shape=[out, out, out], compiler_params=_params(("parallel",)))(w, g, m, v)


def _scan_fwd(a, b):
    d = 1
    while d < a.shape[0]:
        b = a * _shift_down(b, d, 0.0) + b
        a = a * _shift_down(a, d, 1.0)
        d *= 2
    return b


def _scan_rev(a, b):
    d = 1
    while d < a.shape[0]:
        b = a * _shift_up(b, d, 0.0) + b
        a = a * _shift_up(a, d, 1.0)
        d *= 2
    return b


LRU_HALF = 128


def _lru_in_specs(seq):
    half = LRU_W // LRU_HALF
    vec = pl.BlockSpec((1, LRU_HALF), lambda j, b: (0, j))
    mat = pl.BlockSpec((LRU_HALF, LRU_HALF), lambda j, b: (j, j))
    return [pl.BlockSpec((seq, LRU_HALF), lambda j, b: (b, j)), pl.BlockSpec((seq, LRU_HALF), lambda j, b: (b, half + j)),
            pl.BlockSpec((4, LRU_HALF), lambda j, b: (0, j)), vec, mat, vec, mat, vec, vec]


def _lru_math(x_ref, gate_ref, cw_ref, cb_ref, wa_ref, ba_ref, wx_ref, bx_ref, lam_ref):
    x = x_ref[...]
    gate = gate_ref[...]
    cw =[cw_ref[k:k + 1, :] for k in range(4)]
    xr = _conv_fwd(x, cw) + cb_ref[...]
    r = _sigmoid(_nn(xr, wa_ref[...]) + ba_ref[...])
    i = _sigmoid(_nn(xr, wx_ref[...]) + bx_ref[...])
    sp = _softplus(-lam_ref[...])
    log_a = -LRU_C * r * sp
    a = jnp.exp(log_a)
    mult = jnp.sqrt(_neg_expm1(2.0 * log_a))
    gi = i * xr
    h = _scan_fwd(a, mult * gi)
    gl, tg = _gelu(gate)
    return dict(x=x, gate=gate, cw=cw, xr=xr, r=r, i=i, sp=sp, a=a, mult=mult, gi=gi, h=h, gl=gl, tg=tg)


def lru_fwd(u, cw, cb, wa, ba, wx, bx, lam, *, seq, name):
    t = u.shape[0]

    def body(x_ref, gate_ref, cw_ref, cb_ref, wa_ref, ba_ref, wx_ref, bx_ref, lam_ref, y_ref):
        f = _lru_math(x_ref, gate_ref, cw_ref, cb_ref, wa_ref, ba_ref, wx_ref, bx_ref, lam_ref)
        y_ref[...] = f["gl"] * f["h"]

    return pl.pallas_call(
        body, name=name, grid=(LRU_W // LRU_HALF, t // seq), in_specs=_lru_in_specs(seq),
        out_specs=pl.BlockSpec((seq, LRU_HALF), lambda j, b: (b, j)), out_shape=SDS((t, LRU_W), F32),
        compiler_params=_params(("parallel", "parallel")))(u, u, cw, cb, wa, ba, wx, bx, lam)


def lru_bwd(u, cw, cb, wa, ba, wx, bx, lam, dy, *, seq, name):
    t = u.shape[0]

    def body(x_ref, gate_ref, cw_ref, cb_ref, wa_ref, ba_ref, wx_ref, bx_ref, lam_ref, dy_ref,
             dx_ref, dgate_ref, dcw_ref, dwa_ref, dwx_ref, dv_ref):
        @pl.when(pl.program_id(1) == 0)
        def _():
            dcw_ref[...] = jnp.zeros_like(dcw_ref)
            dwa_ref[...] = jnp.zeros_like(dwa_ref)
            dwx_ref[...] = jnp.zeros_like(dwx_ref)
            dv_ref[...] = jnp.zeros_like(dv_ref)

        f = _lru_math(x_ref, gate_ref, cw_ref, cb_ref, wa_ref, ba_ref, wx_ref, bx_ref, lam_ref)
        dy = dy_ref[...]
        a, h, xr, r, i, mult, gi, sp = f["a"], f["h"], f["xr"], f["r"], f["i"], f["mult"], f["gi"], f["sp"]
        dgate_ref[...] = dy * h * _gelu_grad(f["gate"], f["tg"])
        lamb = _scan_rev(_shift_up(a, 1, 0.0), dy * f["gl"])
        da = lamb * _shift_down(h, 1)
        dlog_a = da * a - (lamb * gi) * (a * a) / mult
        dgi = lamb * mult
        dra = dlog_a * (-LRU_C * sp) * r * (1.0 - r)
        dia = dgi * xr * i * (1.0 - i)
        dsp = jnp.sum(dlog_a * (-LRU_C * r), axis=0, keepdims=True)
        dlam = -dsp * _sigmoid(-lam_ref[...])
        dxr = dgi * i + _nt(dra, wa_ref[...]) + _nt(dia, wx_ref[...])
        dx, dcw = _conv_bwd(dxr, f["x"], f["cw"])
        dx_ref[...] = dx
        dcw_ref[...] += dcw
        dwa_ref[...] += _tn(xr, dra)
        dwx_ref[...] += _tn(xr, dia)
        rows = [jnp.sum(dxr, axis=0, keepdims=True), jnp.sum(dra, axis=0, keepdims=True),
                jnp.sum(dia, axis=0, keepdims=True), dlam]
        r8 = lax.broadcasted_iota(jnp.int32, (8, LRU_HALF), 0)
        acc = jnp.zeros((8, LRU_HALF), F32)
        for k, row in enumerate(rows):
            acc = jnp.where(r8 == k, row, acc)
        dv_ref[...] += acc

    nhalf = LRU_W // LRU_HALF
    col = pl.BlockSpec((seq, LRU_HALF), lambda j, b: (b, j))
    mat = pl.BlockSpec((None, LRU_HALF, LRU_HALF), lambda j, b: (j, 0, 0))
    return pl.pallas_call(
        body, name=name, grid=(nhalf, t // seq), in_specs=_lru_in_specs(seq) + [col],
        out_specs=[col, col, pl.BlockSpec((4, LRU_HALF), lambda j, b: (0, j)), mat, mat,
                   pl.BlockSpec((8, LRU_HALF), lambda j, b: (0, j))],
        out_shape=[SDS((t, LRU_W), F32), SDS((t, LRU_W), F32), SDS((4, LRU_W), F32),
                   SDS((nhalf, LRU_HALF, LRU_HALF), F32), SDS((nhalf, LRU_HALF, LRU_HALF), F32), SDS((8, LRU_W), F32)],
        compiler_params=_params(("arbitrary", "arbitrary")))(u, u, cw, cb, wa, ba, wx, bx, lam, dy)


NEG = -1e30


def _rel_bucket_map():
    dist = (np.arange(BLOCK_Q)[:, None] - np.arange(BLOCK_Q)[None, :]) % BLOCK_Q
    max_exact = REL_BUCKETS // 2
    large = max_exact + (np.log(np.maximum(dist, 1).astype(np.float32) / max_exact)
                         / math.log(BLOCK_Q / max_exact) * (REL_BUCKETS - max_exact)).astype(np.int32)
    large = np.minimum(large, REL_BUCKETS - 1)
    return np.where(dist < max_exact, dist, large).astype(np.int32)


def relbias_fwd(rel_bias, bmap, *, name):
    def body(rb_ref, bm_ref, o_ref):
        bm = bm_ref[...]
        for h in range(ATT_HEADS):
            acc = jnp.zeros((BLOCK_Q, BLOCK_Q), F32)
            for b in range(REL_BUCKETS):
                acc = jnp.where(bm == b, rb_ref[b, h], acc)
            o_ref[h] = acc

    return pl.pallas_call(
        body, name=name, in_specs=[pl.BlockSpec(memory_space=pltpu.SMEM), pl.BlockSpec(memory_space=pltpu.VMEM)],
        out_specs=pl.BlockSpec(memory_space=pltpu.VMEM), out_shape=SDS((ATT_HEADS, BLOCK_Q, BLOCK_Q), F32))(rel_bias, bmap)


def relbias_bwd(dbias, bmap, *, name):
    def body(db_ref, bm_ref, o_ref):
        bm = bm_ref[...]
        row = lax.broadcasted_iota(jnp.int32, (REL_BUCKETS, 128), 0)
        col = lax.broadcasted_iota(jnp.int32, (REL_BUCKETS, 128), 1)
        acc = jnp.zeros((REL_BUCKETS, 128), F32)
        for h in range(ATT_HEADS):
            d = db_ref[h]
            for b in range(REL_BUCKETS):
                s = jnp.sum(jnp.sum(jnp.where(bm == b, d, 0.0), axis=1, keepdims=True), axis=0, keepdims=True)
                acc = jnp.where((row == b) & (col == h), s, acc)
        o_ref[...] = acc

    return pl.pallas_call(body, name=name, out_shape=SDS((REL_BUCKETS, 128), F32))(dbias, bmap)


def _attn_probs(q_ref, k_ref, v_ref, b_ref, s_ref, n):
    rows = ATT_GROUP * BLOCK_Q
    qs = q_ref[...].reshape(rows, HEAD) * (HEAD ** -0.5)
    prev = pl.multiple_of(jnp.maximum(n - 1, 0) * BLOCK_Q, BLOCK_Q)
    cur = pl.multiple_of(n * BLOCK_Q, BLOCK_Q)
    kp, kc = k_ref[pl.ds(prev, BLOCK_Q), :], k_ref[pl.ds(cur, BLOCK_Q), :]
    vp, vc = v_ref[pl.ds(prev, BLOCK_Q), :], v_ref[pl.ds(cur, BLOCK_Q), :]
    bias = b_ref[...].reshape(rows, BLOCK_Q)
    i = lax.broadcasted_iota(jnp.int32, (rows, BLOCK_Q), 0) & (BLOCK_Q - 1)
    j = lax.broadcasted_iota(jnp.int32, (rows, BLOCK_Q), 1)
    s_p = jnp.where((j > i) & (n > 0), _nt(qs, kp) + bias, NEG)
    s_c = jnp.where(j <= i, _nt(qs, kc) + bias, NEG)
    sink = s_ref[...]
    m = jnp.maximum(jnp.maximum(jnp.max(s_p, axis=-1, keepdims=True), jnp.max(s_c, axis=-1, keepdims=True)), sink)
    e_p, e_c, e_s = jnp.exp(s_p - m), jnp.exp(s_c - m), jnp.exp(sink - m)
    inv = 1.0 / (jnp.sum(e_p, axis=-1, keepdims=True) + jnp.sum(e_c, axis=-1, keepdims=True) + e_s)
    return e_p * inv, e_c * inv, e_s * inv, qs, kp, kc, vp, vc, prev, cur


def _attn_specs(seq):
    qspec = pl.BlockSpec((None, ATT_GROUP, BLOCK_Q, HEAD), lambda g, b, n: (b, g, n, 0))
    kvspec = pl.BlockSpec((None, None, seq, HEAD), lambda g, b, n: (b, g, 0, 0))
    bspec = pl.BlockSpec((ATT_GROUP, BLOCK_Q, BLOCK_Q), lambda g, b, n: (g, 0, 0))
    sspec = pl.BlockSpec((ATT_GROUP * BLOCK_Q, 1), lambda g, b, n: (g, 0))
    return qspec, kvspec, bspec, sspec


def attn_fwd(q, k, v, bias, sink_rows, *, name):
    nb, _, seq, _ = q.shape

    def body(q_ref, k_ref, v_ref, b_ref, s_ref, o_ref):
        p_p, p_c, _, _, _, _, vp, vc, _, _ = _attn_probs(q_ref, k_ref, v_ref, b_ref, s_ref, pl.program_id(2))
        o_ref[...] = (_nn(p_p, vp) + _nn(p_c, vc)).reshape(ATT_GROUP, BLOCK_Q, HEAD)

    qspec, kvspec, bspec, sspec = _attn_specs(seq)
    return pl.pallas_call(
        body, name=name, grid=(KV_HEADS, nb, seq // BLOCK_Q), in_specs=[qspec, kvspec, kvspec, bspec, sspec],
        out_specs=qspec, out_shape=SDS(q.shape, F32),
        compiler_params=_params(("parallel", "parallel", "arbitrary")))(q, k, v, bias, sink_rows)


def attn_bwd(q, k, v, bias, sink_rows, do, *, name):
    nb, _, seq, _ = q.shape

    def body(q_ref, k_ref, v_ref, b_ref, s_ref, do_ref, dq_ref, dk_ref, dv_ref, db_ref, ds_ref):
        b, n = pl.program_id(1), pl.program_id(2)

        @pl.when((b == 0) & (n == 0))
        def _():
            db_ref[...] = jnp.zeros_like(db_ref)
            ds_ref[...] = jnp.zeros_like(ds_ref)

        @pl.when(n == 0)
        def _():
            dk_ref[...] = jnp.zeros_like(dk_ref)
            dv_ref[...] = jnp.zeros_like(dv_ref)

        p_p, p_c, p_s, qs, kp, kc, vp, vc, prev, cur = _attn_probs(q_ref, k_ref, v_ref, b_ref, s_ref, n)
        do = do_ref[...].reshape(ATT_GROUP * BLOCK_Q, HEAD)
        dp_p, dp_c = _nt(do, vp), _nt(do, vc)
        delta = jnp.sum(p_p * dp_p, axis=-1, keepdims=True) + jnp.sum(p_c * dp_c, axis=-1, keepdims=True)
        ds_p, ds_c = p_p * (dp_p - delta), p_c * (dp_c - delta)
        dq_ref[...] = ((_nn(ds_p, kp) + _nn(ds_c, kc)) * (HEAD ** -0.5)).reshape(ATT_GROUP, BLOCK_Q, HEAD)
        dk_ref[pl.ds(prev, BLOCK_Q), :] += _tn(ds_p, qs)
        dk_ref[pl.ds(cur, BLOCK_Q), :] += _tn(ds_c, qs)
        dv_ref[pl.ds(prev, BLOCK_Q), :] += _tn(p_p, do)
        dv_ref[pl.ds(cur, BLOCK_Q), :] += _tn(p_c, do)
        db_ref[...] += (ds_p + ds_c).reshape(ATT_GROUP, BLOCK_Q, BLOCK_Q)
        ds_ref[...] += -p_s * delta

    qspec, kvspec, bspec, sspec = _attn_specs(seq)
    return pl.pallas_call(
        body, name=name, grid=(KV_HEADS, nb, seq // BLOCK_Q), in_specs=[qspec, kvspec, kvspec, bspec, sspec, qspec],
        out_specs=[qspec, kvspec, kvspec, bspec, sspec],
        out_shape=[SDS(q.shape, F32), SDS(k.shape, F32), SDS(v.shape, F32),
                   SDS((ATT_HEADS, BLOCK_Q, BLOCK_Q), F32), SDS((ATT_HEADS * BLOCK_Q, 1), F32)],
        compiler_params=_params(("arbitrary", "arbitrary", "arbitrary")))(q, k, v, bias, sink_rows, do)


def _iota2(shape, axis):
    return lax.broadcasted_iota(jnp.int32, shape, axis)


def _col_to_row(col):
    c = col.shape[0]
    eye = _iota2((c, c), 0) == _iota2((c, c), 1)
    return jnp.sum(jnp.where(eye, jnp.broadcast_to(col, (c, c)), 0.0), axis=0, keepdims=True)


def _row_to_col(row):
    c = row.shape[1]
    eye = _iota2((c, c), 0) == _iota2((c, c), 1)
    return jnp.sum(jnp.where(eye, jnp.broadcast_to(row, (c, c)), 0.0), axis=1, keepdims=True)


def _last_row(col):
    c = col.shape[0]
    return jnp.sum(jnp.where(_iota2((c, 1), 0) == c - 1, col, 0.0), axis=0, keepdims=True)


def _chunk_cumsum(x):
    pos = _iota2(x.shape, 0) & (DN_CHUNK - 1)
    d = 1
    while d < DN_CHUNK:
        x = x + jnp.where(pos >= d, pltpu.roll(x, d, 0), 0.0)
        d *= 2
    return x


def _chunk_rev_cumsum(x):
    n = x.shape[0]
    pos = _iota2(x.shape, 0) & (DN_CHUNK - 1)
    d = 1
    while d < DN_CHUNK:
        x = x + jnp.where(pos < DN_CHUNK - d, pltpu.roll(x, n - d, 0), 0.0)
        d *= 2
    return x


def _tri_inv(low):
    c = low.shape[0]
    eye = (_iota2((c, c), 0) == _iota2((c, c), 1)).astype(F32)
    m = -low
    p = eye + m
    steps = int(math.log2(c)) - 1
    for _ in range(steps):
        m = _nn(m, m, hi=True)
        p = p + _nn(p, m, hi=True)
    return p


_DN_SCALE = (HEAD ** -0.5, 1.0, None)


def _dn_act(c, scale):
    sig = _sigmoid(c)
    a = c * sig
    if scale is None:
        return a, sig, None, None
    r = lax.rsqrt(jnp.sum(a * a, axis=-1, keepdims=True) + EPS)
    return a * r * scale, sig, a * r, r


def _dn_gates(ba_ref, hs_ref):
    beta = _sigmoid(ba_ref[0])
    sp_arg = ba_ref[1] + hs_ref[1]
    a_exp = jnp.exp(hs_ref[0])
    g = -a_exp * _softplus(sp_arg)
    return beta, g, sp_arg, a_exp


def _dn_inputs(pre_ref, cw_ref, ba_ref, hs_ref, act_sc, b_sc, gc_sc, c_sc=None):
    for idx in range(3):
        c = _conv_fwd(pre_ref[idx], [cw_ref[idx, k:k + 1, :] for k in range(4)])
        if c_sc is not None:
            c_sc[idx] = c
        act_sc[idx] = _dn_act(c, _DN_SCALE[idx])[0]
    beta, g, _, _ = _dn_gates(ba_ref, hs_ref)
    b_sc[...] = beta
    gc_sc[...] = _chunk_cumsum(g)


def _dn_chunk_math(q, k, v, b, gcc):
    c = q.shape[0]
    tril = _iota2((c, c), 0) >= _iota2((c, c), 1)
    strict = _iota2((c, c), 0) > _iota2((c, c), 1)
    eg = jnp.exp(gcc)
    kb, vb = k * b, v * b
    kbg = kb * eg
    dm = jnp.exp(jnp.where(tril, jnp.broadcast_to(gcc, (c, c)) - _col_to_row(gcc), NEG))
    kk = _nt(kb, k)
    t = _tri_inv(jnp.where(strict, kk * dm, 0.0))
    glast = _last_row(gcc)
    ekd = jnp.exp(glast - gcc)
    qk = _nt(q, k)
    return dict(tril=tril, strict=strict, eg=eg, kb=kb, vb=vb, kbg=kbg, dm=dm, kk=kk, t=t, glast=glast, ekd=ekd,
                kd=k * ekd, qk=qk, amat=jnp.where(tril, qk * dm, 0.0), qg=q * eg,
                egl=jnp.broadcast_to(jnp.exp(glast), (c, 1)))


DN_UNROLL = 4


def _chunk_loop(nc, chunk):
    u = math.gcd(nc, DN_UNROLL)

    def step(i, carry):
        for j in range(u):
            chunk(i * u + j)
        return carry

    lax.fori_loop(0, nc // u, step, 0)


def _dn_specs(seq):
    s64 = lambda lead: pl.BlockSpec((lead, None, None, seq, HEAD), lambda b, h: (0, b, h, 0, 0))
    s1 = lambda lead: pl.BlockSpec((lead, None, None, seq, 1), lambda b, h: (0, b, h, 0, 0))
    one64 = pl.BlockSpec((None, None, seq, HEAD), lambda b, h: (b, h, 0, 0))
    one1 = pl.BlockSpec((None, None, seq, 1), lambda b, h: (b, h, 0, 0))
    cw = pl.BlockSpec((None, 3, 4, HEAD), lambda b, h: (h, 0, 0, 0))
    hs = pl.BlockSpec((None, 2, 1, 1), lambda b, h: (h, 0, 0, 0))
    return s64, s1, one64, one1, cw, hs


def dn_prep(pre, cw, ba, hs, *, name):
    _, nb, nh, seq, _ = pre.shape
    nc = seq // DN_CHUNK

    def body(pre_ref, cw_ref, ba_ref, hs_ref, loc_ref, egl_ref, act_sc, b_sc, gc_sc):
        _dn_inputs(pre_ref, cw_ref, ba_ref, hs_ref, act_sc, b_sc, gc_sc)

        def chunk(c):
            rows = pl.ds(pl.multiple_of(c * DN_CHUNK, DN_CHUNK), DN_CHUNK)
            m = _dn_chunk_math(act_sc[0, rows, :], act_sc[1, rows, :], act_sc[2, rows, :], b_sc[rows, :], gc_sc[rows, :])
            loc_ref[0, rows, :] = m["qg"]
            loc_ref[1, rows, :] = m["kd"]
            loc_ref[2, rows, :] = _nn(m["t"], m["vb"])
            loc_ref[3, rows, :] = _nn(m["t"], m["kbg"])
            loc_ref[4, rows, :] = m["amat"]
            egl_ref[rows, :] = m["egl"]

        _chunk_loop(nc, chunk)

    s64, s1, one64, one1, cwspec, hsspec = _dn_specs(seq)
    return pl.pallas_call(
        body, name=name, grid=(nb, nh), in_specs=[s64(3), cwspec, s1(2), hsspec], out_specs=[s64(5), one1],
        out_shape=[SDS((5, nb, nh, seq, HEAD), F32), SDS((nb, nh, seq, 1), F32)],
        scratch_shapes=[pltpu.VMEM((3, seq, HEAD), F32)] + [pltpu.VMEM((seq, 1), F32)] * 2,
        compiler_params=_params(("parallel", "parallel")))(pre, cw, ba, hs)


def _gated_norm(o, z, gn):
    r = lax.rsqrt(jnp.mean(o * o, axis=-1, keepdims=True) + EPS)
    sig = _sigmoid(z)
    return o * r, sig, r


def dn_scan(loc, egl, z, gn, *, name):
    _, nb, nh, seq, _ = loc.shape
    nc = seq // DN_CHUNK

    def body(loc_ref, egl_ref, z_ref, gn_ref, y_ref, o_ref, vn_ref, st_ref):
        gn = gn_ref[...]

        def step(c, state):
            rows = pl.ds(pl.multiple_of(c * DN_CHUNK, DN_CHUNK), DN_CHUNK)
            st_ref[rows, :] = state
            vn = loc_ref[2, rows, :] - _nn(loc_ref[3, rows, :], state)
            o = _nn(loc_ref[0, rows, :], state) + _nn(loc_ref[4, rows, :], vn)
            vn_ref[rows, :] = vn
            o_ref[rows, :] = o
            zz = z_ref[rows, :]
            on, sig, _ = _gated_norm(o, zz, gn)
            y_ref[rows, :] = on * gn * (zz * sig)
            return state * egl_ref[rows, :] + _tn(loc_ref[1, rows, :], vn)

        lax.fori_loop(0, nc, step, jnp.zeros((HEAD, HEAD), F32))

    s64, s1, one64, one1, cwspec, hsspec = _dn_specs(seq)
    out = SDS((nb, nh, seq, HEAD), F32)
    return pl.pallas_call(
        body, name=name, grid=(nb, nh), in_specs=[s64(5), one1, one64, _whole((1, HEAD))],
        out_specs=[one64] * 4, out_shape=[out] * 4,
        compiler_params=_params(("parallel", "parallel")))(loc, egl, z, gn)


def dn_scan_bwd(loc, egl, z, gn, o, vn, states, dy, *, name):
    _, nb, nh, seq, _ = loc.shape
    nc = seq // DN_CHUNK

    def body(loc_ref, egl_ref, z_ref, gn_ref, o_ref, vn_ref, st_ref, dy_ref, dloc_ref, degl_ref, dz_ref, dgn_ref):
        @pl.when((pl.program_id(0) == 0) & (pl.program_id(1) == 0))
        def _():
            dgn_ref[...] = jnp.zeros_like(dgn_ref)

        gn = gn_ref[...]
        tril = _iota2((DN_CHUNK, DN_CHUNK), 0) >= _iota2((DN_CHUNK, DN_CHUNK), 1)

        def step(i, carry):
            ds, dgn = carry
            rows = pl.ds(pl.multiple_of((nc - 1 - i) * DN_CHUNK, DN_CHUNK), DN_CHUNK)
            dy, zz, oo = dy_ref[rows, :], z_ref[rows, :], o_ref[rows, :]
            on, sig, r = _gated_norm(oo, zz, gn)
            sz = zz * sig
            dz_ref[rows, :] = dy * on * gn * (sig * (1.0 + zz * (1.0 - sig)))
            dgn = dgn + jnp.sum(dy * on * sz, axis=0, keepdims=True)
            don = dy * gn * sz
            do = r * (don - on * jnp.mean(don * on, axis=-1, keepdims=True))
            state, vnew = st_ref[rows, :], vn_ref[rows, :]
            qg, kd, w, amat = loc_ref[0, rows, :], loc_ref[1, rows, :], loc_ref[3, rows, :], loc_ref[4, rows, :]
            dvn = _tn(amat, do) + _nn(kd, ds)
            dloc_ref[0, rows, :] = _nt(do, state)
            dloc_ref[1, rows, :] = _nt(vnew, ds)
            dloc_ref[2, rows, :] = dvn
            dloc_ref[3, rows, :] = -_nt(dvn, state)
            dloc_ref[4, rows, :] = jnp.where(tril, _nt(do, vnew), 0.0)
            degl = jnp.sum(jnp.sum(state * ds, axis=1, keepdims=True), axis=0, keepdims=True)
            degl_ref[rows, :] = jnp.broadcast_to(degl, (DN_CHUNK, 1))
            return ds * egl_ref[rows, :] + _tn(qg, do) - _tn(w, dvn), dgn

        _, dgn = lax.fori_loop(0, nc, step, (jnp.zeros((HEAD, HEAD), F32), jnp.zeros((1, HEAD), F32)))
        dgn_ref[...] += dgn

    s64, s1, one64, one1, cwspec, hsspec = _dn_specs(seq)
    return pl.pallas_call(
        body, name=name, grid=(nb, nh),
        in_specs=[s64(5), one1, one64, _whole((1, HEAD)), one64, one64, one64, one64],
        out_specs=[s64(5), one1, one64, _whole((1, HEAD))],
        out_shape=[SDS((5, nb, nh, seq, HEAD), F32), SDS((nb, nh, seq, 1), F32), SDS((nb, nh, seq, HEAD), F32),
                   SDS((1, HEAD), F32)],
        compiler_params=_params(("arbitrary", "arbitrary")))(loc, egl, z, gn, o, vn, states, dy)


def dn_prep_bwd(pre, cw, ba, hs, dloc, degl, *, name):
    _, nb, nh, seq, _ = pre.shape
    nc = seq // DN_CHUNK

    def body(pre_ref, cw_ref, ba_ref, hs_ref, dloc_ref, degl_ref, dpre_ref, dba_ref, dcw_ref, dhs_ref,
             act_sc, b_sc, gc_sc, c_sc):
        @pl.when(pl.program_id(1) == 0)
        def _():
            dcw_ref[...] = jnp.zeros_like(dcw_ref)
            dhs_ref[...] = jnp.zeros_like(dhs_ref)

        _dn_inputs(pre_ref, cw_ref, ba_ref, hs_ref, act_sc, b_sc, gc_sc, c_sc)

        def chunk(c):
            rows = pl.ds(pl.multiple_of(c * DN_CHUNK, DN_CHUNK), DN_CHUNK)
            q, k, v, b, gcc = act_sc[0, rows, :], act_sc[1, rows, :], act_sc[2, rows, :], b_sc[rows, :], gc_sc[rows, :]
            m = _dn_chunk_math(q, k, v, b, gcc)
            dqg, dkd, du, dw, da = (dloc_ref[x, rows, :] for x in range(5))
            t, dm, eg = m["t"], m["dm"], m["eg"]
            dt = _nt(du, m["vb"]) + _nt(dw, m["kbg"])
            dvb, dkbg = _tn(t, du), _tn(t, dw)
            dl = jnp.where(m["strict"], -_tn(t, _nt(dt, t, hi=True), hi=True), 0.0)
            dkk = dl * dm
            dqk = da * dm
            dd = dl * m["kk"] + da * m["qk"]
            dkb = _nn(dkk, k) + dkbg * eg
            dq = _nn(dqk, k) + dqg * eg
            dk = _tn(dkk, m["kb"]) + _tn(dqk, q) + dkd * m["ekd"] + dkb * b
            db = jnp.sum(dkb * k, axis=-1, keepdims=True) + jnp.sum(dvb * v, axis=-1, keepdims=True)
            mx = jnp.where(m["tril"], dd * dm, 0.0)
            tk = jnp.sum(dkd * m["kd"], axis=-1, keepdims=True)
            dgc = (jnp.sum(mx, axis=-1, keepdims=True) - _row_to_col(jnp.sum(mx, axis=0, keepdims=True))
                   + jnp.sum(dqg * m["qg"], axis=-1, keepdims=True) + jnp.sum(dkbg * m["kbg"], axis=-1, keepdims=True) - tk)
            dglast = jnp.sum(tk, axis=0, keepdims=True) + _last_row(degl_ref[rows, :]) * jnp.exp(m["glast"])
            act_sc[0, rows, :] = dq
            act_sc[1, rows, :] = dk
            act_sc[2, rows, :] = dvb * b
            b_sc[rows, :] = db
            gc_sc[rows, :] = dgc + jnp.where(_iota2((DN_CHUNK, 1), 0) == DN_CHUNK - 1, dglast, 0.0)

        _chunk_loop(nc, chunk)

        beta, g, sp_arg, a_exp = _dn_gates(ba_ref, hs_ref)
        dg = _chunk_rev_cumsum(gc_sc[...])
        dal = dg * (-a_exp) * _sigmoid(sp_arg)
        dba_ref[0] = b_sc[...] * beta * (1.0 - beta)
        dba_ref[1] = dal
        dhs_ref[0] += jnp.sum(dg * g, axis=0, keepdims=True)
        dhs_ref[1] += jnp.sum(dal, axis=0, keepdims=True)
        for idx in range(3):
            c = c_sc[idx]
            _, sig, hat, r = _dn_act(c, _DN_SCALE[idx])
            da_ = act_sc[idx]
            if _DN_SCALE[idx] is not None:
                da_ = da_ * _DN_SCALE[idx]
                da_ = r * (da_ - hat * jnp.sum(da_ * hat, axis=-1, keepdims=True))
            dx, dcw = _conv_bwd(da_ * (sig * (1.0 + c * (1.0 - sig))), pre_ref[idx],
                                [cw_ref[idx, k:k + 1, :] for k in range(4)])
            dpre_ref[idx] = dx
            dcw_ref[idx] += dcw

    s64, s1, one64, one1, cwspec, hsspec = _dn_specs(seq)
    swap = lambda spec: pl.BlockSpec(spec.block_shape, lambda h, b, _f=spec.index_map: _f(b, h))
    return pl.pallas_call(
        body, name=name, grid=(nh, nb),
        in_specs=[swap(s64(3)), swap(cwspec), swap(s1(2)), swap(hsspec), swap(s64(5)), swap(one1)],
        out_specs=[swap(s64(3)), swap(s1(2)), swap(cwspec), swap(hsspec)],
        out_shape=[SDS((3, nb, nh, seq, HEAD), F32), SDS((2, nb, nh, seq, 1), F32), SDS((nh, 3, 4, HEAD), F32),
                   SDS((nh, 2, 1, 1), F32)],
        scratch_shapes=[pltpu.VMEM((3, seq, HEAD), F32)] + [pltpu.VMEM((seq, 1), F32)] * 2 + [pltpu.VMEM((3, seq, HEAD), F32)],
        compiler_params=_params(("arbitrary", "arbitrary")))(pre, cw, ba, hs, dloc, degl)


COL_Q, COL_K, COL_V = 512 // 128, 1024 // 128, 1152 // 128
COL_DNQ, COL_DNK, COL_DNV, COL_DNZ, COL_BA = 1280 // 128, 1536 // 128, 1792 // 128, 2048 // 128, 2304 // 128


def _lane_a(shape):
    return _iota2(shape, 1) < HEAD


def _bd(x):
    la = _lane_a(x.shape)
    return jnp.concatenate([jnp.where(la, x, 0.0), jnp.where(la, 0.0, x)], axis=0)


def _fold(m):
    return m[:HEAD] + m[HEAD:]


def _bd_mask():
    return (_iota2((2 * HEAD, 2 * HEAD), 0) < HEAD) == (_iota2((2 * HEAD, 2 * HEAD), 1) < HEAD)


def _pk_nn(x, y, hi=False):
    return _nn(x, _bd(y), hi)


def _pk_nt(u, v, hi=False):
    return _nt(u, _bd(v), hi)


def _pk_tn(x, y, hi=False):
    return _fold(jnp.where(_bd_mask(), _tn(x, y, hi), 0.0))


def _half_sum(x):
    la = _lane_a(x.shape)
    return jnp.where(la, jnp.sum(jnp.where(la, x, 0.0), axis=-1, keepdims=True),
                     jnp.sum(jnp.where(la, 0.0, x), axis=-1, keepdims=True))


def _lane_col(x, idx):
    return jnp.sum(jnp.where(_iota2(x.shape, 1) == idx, x, 0.0), axis=-1, keepdims=True)


def _row0(x):
    return jnp.max(x, axis=0, keepdims=True)


def _dup_kv(x, g):
    la = _lane_a(x.shape)
    rolled = pltpu.roll(x, HEAD, 1)
    return jnp.where(la, x, rolled) if g == 0 else jnp.where(la, rolled, x)


def _stack_heads(ref, g):
    la = _lane_a((BLOCK_Q, 2 * HEAD))
    parts = []
    for hh in range(ATT_GROUP):
        pair = ref[:, pl.ds(2 * HEAD * (2 * g + hh // 2), 2 * HEAD)]
        parts.append(jnp.where(la if hh % 2 == 0 else ~la, pair, 0.0))
    return jnp.concatenate(parts, axis=0)


def _unstack_heads(stack, ref, g):
    la = _lane_a((BLOCK_Q, 2 * HEAD))
    for j in range(2):
        top = stack[2 * j * BLOCK_Q:(2 * j + 1) * BLOCK_Q]
        bot = stack[(2 * j + 1) * BLOCK_Q:(2 * j + 2) * BLOCK_Q]
        ref[:, pl.ds(2 * HEAD * (2 * g + j), 2 * HEAD)] = jnp.where(la, top, bot)


def _swa_probs(q_ref, k_ref, v_ref, b_ref, s_ref, n, g):
    rows = ATT_GROUP * BLOCK_Q
    prev = pl.multiple_of(jnp.maximum(n - 1, 0) * BLOCK_Q, BLOCK_Q)
    cur = pl.multiple_of(n * BLOCK_Q, BLOCK_Q)
    kp, kc = _dup_kv(k_ref[pl.ds(prev, BLOCK_Q), :], g), _dup_kv(k_ref[pl.ds(cur, BLOCK_Q), :], g)
    vp, vc = _dup_kv(v_ref[pl.ds(prev, BLOCK_Q), :], g), _dup_kv(v_ref[pl.ds(cur, BLOCK_Q), :], g)
    qs = _stack_heads(q_ref, g) * (HEAD ** -0.5)
    bias = b_ref[pl.ds(ATT_GROUP * g, ATT_GROUP)].reshape(rows, BLOCK_Q)
    i = _iota2((rows, BLOCK_Q), 0) & (BLOCK_Q - 1)
    j = _iota2((rows, BLOCK_Q), 1)
    s_p = jnp.where((j > i) & (n > 0), _nt(qs, kp) + bias, NEG)
    s_c = jnp.where(j <= i, _nt(qs, kc) + bias, NEG)
    sink = s_ref[pl.ds(rows * g, rows), :]
    m = jnp.maximum(jnp.maximum(jnp.max(s_p, axis=-1, keepdims=True), jnp.max(s_c, axis=-1, keepdims=True)), sink)
    e_p, e_c, e_s = jnp.exp(s_p - m), jnp.exp(s_c - m), jnp.exp(sink - m)
    inv = 1.0 / (jnp.sum(e_p, axis=-1, keepdims=True) + jnp.sum(e_c, axis=-1, keepdims=True) + e_s)
    return e_p * inv, e_c * inv, e_s * inv, qs, kp, kc, vp, vc, prev, cur


def _swa_specs(seq):
    nblk = seq // BLOCK_Q
    qspec = pl.BlockSpec((BLOCK_Q, ATT_W), lambda b, n: (b * nblk + n, COL_Q * 128 // ATT_W))
    kspec = pl.BlockSpec((seq, 2 * HEAD), lambda b, n: (b, COL_K))
    vspec = pl.BlockSpec((seq, 2 * HEAD), lambda b, n: (b, COL_V))
    ospec = pl.BlockSpec((BLOCK_Q, ATT_W), lambda b, n: (b * nblk + n, 0))
    kvout = pl.BlockSpec((seq, 2 * HEAD), lambda b, n: (b, 0))
    return qspec, kspec, vspec, ospec, kvout, _whole((ATT_HEADS, BLOCK_Q, BLOCK_Q)), _whole((ATT_HEADS * BLOCK_Q, 1))


def swa_fwd(u, bias, sink_rows, *, seq, name):
    t = u.shape[0]

    def body(q_ref, k_ref, v_ref, b_ref, s_ref, o_ref):
        for g in range(KV_HEADS):
            p_p, p_c, _, _, _, _, vp, vc, _, _ = _swa_probs(q_ref, k_ref, v_ref, b_ref, s_ref, pl.program_id(1), g)
            _unstack_heads(_nn(p_p, vp) + _nn(p_c, vc), o_ref, g)

    qspec, kspec, vspec, ospec, kvout, bspec, sspec = _swa_specs(seq)
    return pl.pallas_call(
        body, name=name, grid=(t // seq, seq // BLOCK_Q), in_specs=[qspec, kspec, vspec, bspec, sspec], out_specs=ospec,
        out_shape=SDS((t, ATT_W), F32), compiler_params=_params(("parallel", "arbitrary")))(u, u, u, bias, sink_rows)


def swa_bwd(u, bias, sink_rows, do, *, seq, name):
    t = u.shape[0]

    def body(q_ref, k_ref, v_ref, b_ref, s_ref, do_ref, dq_ref, dk_ref, dv_ref, db_ref, ds_ref):
        b, n = pl.program_id(0), pl.program_id(1)

        @pl.when((b == 0) & (n == 0))
        def _():
            db_ref[...] = jnp.zeros_like(db_ref)
            ds_ref[...] = jnp.zeros_like(ds_ref)

        @pl.when(n == 0)
        def _():
            dk_ref[...] = jnp.zeros_like(dk_ref)
            dv_ref[...] = jnp.zeros_like(dv_ref)

        la = _lane_a((BLOCK_Q, 2 * HEAD))
        for g in range(KV_HEADS):
            p_p, p_c, p_s, qs, kp, kc, vp, vc, prev, cur = _swa_probs(q_ref, k_ref, v_ref, b_ref, s_ref, n, g)
            do = _stack_heads(do_ref, g)
            dp_p, dp_c = _nt(do, vp), _nt(do, vc)
            delta = jnp.sum(p_p * dp_p, axis=-1, keepdims=True) + jnp.sum(p_c * dp_c, axis=-1, keepdims=True)
            ds_p, ds_c = p_p * (dp_p - delta), p_c * (dp_c - delta)
            _unstack_heads((_nn(ds_p, kp) + _nn(ds_c, kc)) * (HEAD ** -0.5), dq_ref, g)
            mine = la if g == 0 else ~la

            def to_head(x):
                return jnp.where(mine, x + pltpu.roll(x, HEAD, 1), 0.0)

            dk_ref[pl.ds(prev, BLOCK_Q), :] += to_head(_tn(ds_p, qs))
            dk_ref[pl.ds(cur, BLOCK_Q), :] += to_head(_tn(ds_c, qs))
            dv_ref[pl.ds(prev, BLOCK_Q), :] += to_head(_tn(p_p, do))
            dv_ref[pl.ds(cur, BLOCK_Q), :] += to_head(_tn(p_c, do))
            db_ref[pl.ds(ATT_GROUP * g, ATT_GROUP)] += (ds_p + ds_c).reshape(ATT_GROUP, BLOCK_Q, BLOCK_Q)
            rows = ATT_GROUP * BLOCK_Q
            ds_ref[pl.ds(rows * g, rows), :] += -p_s * delta

    qspec, kspec, vspec, ospec, kvout, bspec, sspec = _swa_specs(seq)
    return pl.pallas_call(
        body, name=name, grid=(t // seq, seq // BLOCK_Q), in_specs=[qspec, kspec, vspec, bspec, sspec, ospec],
        out_specs=[ospec, kvout, kvout, bspec, sspec],
        out_shape=[SDS((t, ATT_W), F32), SDS((t, 2 * HEAD), F32), SDS((t, 2 * HEAD), F32),
                   SDS((ATT_HEADS, BLOCK_Q, BLOCK_Q), F32), SDS((ATT_HEADS * BLOCK_Q, 1), F32)],
        compiler_params=_params(("arbitrary", "arbitrary")))(u, u, u, bias, sink_rows, do)


def _gdn_gates(ba_ref, alog_ref, dt_ref, hp):
    blk = ba_ref[...]
    beta_blk = _sigmoid(blk)
    sp_arg = blk + dt_ref[...]
    a_exp = jnp.exp(alog_ref[...])
    g_blk = -a_exp * _softplus(sp_arg)
    la = _lane_a(blk.shape)
    ha = 2 * hp
    beta = jnp.where(la, _lane_col(beta_blk, ha), _lane_col(beta_blk, ha + 1))
    g = jnp.where(la, _lane_col(g_blk, DN_HEADS + ha), _lane_col(g_blk, DN_HEADS + ha + 1))
    return beta, g, beta_blk, sp_arg, a_exp, g_blk


def _gdn_act(c, scale):
    sig = _sigmoid(c)
    a = c * sig
    if scale is None:
        return a, sig, None, None
    r = lax.rsqrt(_half_sum(a * a) + EPS)
    return a * r * scale, sig, a * r, r


def _gdn_inputs(pre_refs, cw_refs, ba_ref, alog_ref, dt_ref, hp, act_sc, b_sc, gc_sc, c_sc=None):
    for idx in range(3):
        c = _conv_fwd(pre_refs[idx][...], [cw_refs[idx][k:k + 1, :] for k in range(4)])
        if c_sc is not None:
            c_sc[idx] = c
        act_sc[idx] = _gdn_act(c, _DN_SCALE[idx])[0]
    beta, g = _gdn_gates(ba_ref, alog_ref, dt_ref, hp)[:2]
    b_sc[...] = beta
    gc_sc[...] = _chunk_cumsum(g)


def _gdn_chunk(q, k, v, b, gcc):
    shape = q.shape
    row, lm = _iota2(shape, 0), _iota2(shape, 1) & (HEAD - 1)
    tril, strict, eye = row >= lm, row > lm, row == lm
    eg = jnp.exp(gcc)
    kb, vb = k * b, v * b
    kbg = kb * eg
    grow = jnp.sum(jnp.where(eye, gcc, 0.0), axis=0, keepdims=True)
    dm = jnp.exp(jnp.where(tril, gcc - grow, NEG))
    kk = _pk_nt(kb, k)
    glast = jnp.sum(jnp.where(row == DN_CHUNK - 1, gcc, 0.0), axis=0, keepdims=True)
    ekd = jnp.exp(glast - gcc)
    qk = _pk_nt(q, k)
    return dict(q=q, k=k, v=v, b=b, tril=tril, strict=strict, eye=eye, row=row, eg=eg, kb=kb, vb=vb, kbg=kbg, dm=dm, kk=kk,
                low=jnp.where(strict, kk * dm, 0.0), glast=glast, ekd=ekd, kd=k * ekd, qk=qk,
                amat=jnp.where(tril, qk * dm, 0.0), qg=q * eg, egl=jnp.broadcast_to(jnp.exp(glast), shape))


def _tri_inv_many(chunks):
    ms = [-m["low"] for m in chunks]
    ts = [m["eye"].astype(F32) + x for m, x in zip(chunks, ms)]
    for _ in range(int(math.log2(HEAD)) - 1):
        ms = [_pk_nn(x, x, hi=True) for x in ms]
        ts = [t + _pk_nn(t, x, hi=True) for t, x in zip(ts, ms)]
    return ts


def _gdn_chunk_loop(nc, act_sc, b_sc, gc_sc, finish):
    u = math.gcd(nc, DN_UNROLL)

    def step(i, carry):
        rows = [pl.ds(pl.multiple_of((i * u + j) * DN_CHUNK, DN_CHUNK), DN_CHUNK) for j in range(u)]
        chunks = [_gdn_chunk(act_sc[0, r, :], act_sc[1, r, :], act_sc[2, r, :], b_sc[r, :], gc_sc[r, :]) for r in rows]
        pending = [finish(r, m, t) for r, m, t in zip(rows, chunks, _tri_inv_many(chunks))]
        pending = [g for g in pending if g is not None]
        while pending:
            for g in list(pending):
                if next(g, StopIteration) is StopIteration:
                    pending.remove(g)
        return carry

    lax.fori_loop(0, nc // u, step, 0)


def _gdn_in_specs(seq):
    u_at = lambda col: pl.BlockSpec((seq, 2 * HEAD), lambda b, hp, _c=col: (b, _c + hp))
    cw_at = lambda col: pl.BlockSpec((4, 2 * HEAD), lambda b, hp, _c=col: (0, _c + hp))
    row = pl.BlockSpec((1, 2 * HEAD), lambda b, hp: (0, 0))
    ba = pl.BlockSpec((seq, 2 * HEAD), lambda b, hp: (b, COL_BA))
    return [u_at(COL_DNQ), u_at(COL_DNK), u_at(COL_DNV), ba, cw_at(0), cw_at(2), cw_at(4), row, row]


def _pair(seq, lead=None):
    if lead is None:
        return pl.BlockSpec((seq, 2 * HEAD), lambda b, hp: (b, hp))
    return pl.BlockSpec((lead, seq, 2 * HEAD), lambda b, hp: (0, b, hp))


def _swap(spec):
    return pl.BlockSpec(spec.block_shape, lambda hp, b, _f=spec.index_map: _f(b, hp))


def gdn_prep(u, cw, alog_row, dt_row, *, seq, name):
    t = u.shape[0]
    nc = seq // DN_CHUNK

    def body(q_ref, k_ref, v_ref, ba_ref, cq_ref, ck_ref, cv_ref, alog_ref, dt_ref, loc_ref, egl_ref, act_sc, b_sc, gc_sc):
        _gdn_inputs((q_ref, k_ref, v_ref), (cq_ref, ck_ref, cv_ref), ba_ref, alog_ref, dt_ref, pl.program_id(1),
                    act_sc, b_sc, gc_sc)

        def finish(rows, m, t):
            loc_ref[0, rows, :] = m["qg"]
            loc_ref[1, rows, :] = m["kd"]
            loc_ref[2, rows, :] = _pk_nn(t, m["vb"])
            loc_ref[3, rows, :] = _pk_nn(t, m["kbg"])
            loc_ref[4, rows, :] = m["amat"]
            egl_ref[rows, :] = m["egl"]

        _gdn_chunk_loop(nc, act_sc, b_sc, gc_sc, finish)

    return pl.pallas_call(
        body, name=name, grid=(t // seq, DN_HEADS // 2), in_specs=_gdn_in_specs(seq), out_specs=[_pair(seq, 5), _pair(seq)],
        out_shape=[SDS((5, t, DN_HEADS * HEAD), F32), SDS((t, DN_HEADS * HEAD), F32)],
        scratch_shapes=[pltpu.VMEM((3, seq, 2 * HEAD), F32)] + [pltpu.VMEM((seq, 2 * HEAD), F32)] * 2,
        compiler_params=_params(("parallel", "parallel")))(u, u, u, u, cw, cw, cw, alog_row, dt_row)


def _gated_norm2(o, z, gn):
    r = lax.rsqrt(_half_sum(o * o) * (1.0 / HEAD) + EPS)
    return o * r, _sigmoid(z), r


def gdn_scan(loc, egl, u, gn, *, seq, name):
    t = u.shape[0]
    nc = seq // DN_CHUNK

    def body(loc_ref, egl_ref, z_ref, gn_ref, y_ref, o_ref, vn_ref, st_ref):
        gn = gn_ref[...]
        bdm = _bd_mask()

        def step(c, state):
            rows = pl.ds(pl.multiple_of(c * DN_CHUNK, DN_CHUNK), DN_CHUNK)
            st_ref[rows, :] = _fold(state)
            vn = loc_ref[2, rows, :] - _nn(loc_ref[3, rows, :], state)
            o = _nn(loc_ref[0, rows, :], state) + _pk_nn(loc_ref[4, rows, :], vn)
            vn_ref[rows, :] = vn
            o_ref[rows, :] = o
            zz = z_ref[rows, :]
            on, sig, _ = _gated_norm2(o, zz, gn)
            y_ref[rows, :] = on * gn * (zz * sig)
            return state * _row0(egl_ref[rows, :]) + jnp.where(bdm, _tn(loc_ref[1, rows, :], vn), 0.0)

        lax.fori_loop(0, nc, step, jnp.zeros((2 * HEAD, 2 * HEAD), F32))

    zspec = pl.BlockSpec((seq, 2 * HEAD), lambda b, hp: (b, COL_DNZ + hp))
    out = SDS((t, DN_HEADS * HEAD), F32)
    return pl.pallas_call(
        body, name=name, grid=(t // seq, DN_HEADS // 2), in_specs=[_pair(seq, 5), _pair(seq), zspec, _whole((1, 2 * HEAD))],
        out_specs=[_pair(seq)] * 4, out_shape=[out] * 4,
        compiler_params=_params(("parallel", "parallel")))(loc, egl, u, gn)


def gdn_scan_bwd(loc, egl, u, gn, o, vn, states, dy, *, seq, name):
    t = u.shape[0]
    nc = seq // DN_CHUNK

    def body(loc_ref, egl_ref, z_ref, gn_ref, o_ref, vn_ref, st_ref, dy_ref, dloc_ref, degl_ref, dz_ref, dgn_ref):
        @pl.when((pl.program_id(0) == 0) & (pl.program_id(1) == 0))
        def _():
            dgn_ref[...] = jnp.zeros_like(dgn_ref)

        gn = gn_ref[...]
        bdm = _bd_mask()
        shape = (DN_CHUNK, 2 * HEAD)
        tril = _iota2(shape, 0) >= (_iota2(shape, 1) & (HEAD - 1))

        def step(i, carry):
            ds, dgn = carry
            rows = pl.ds(pl.multiple_of((nc - 1 - i) * DN_CHUNK, DN_CHUNK), DN_CHUNK)
            dy, zz, oo = dy_ref[rows, :], z_ref[rows, :], o_ref[rows, :]
            on, sig, r = _gated_norm2(oo, zz, gn)
            sz = zz * sig
            dz_ref[rows, :] = dy * on * gn * (sig * (1.0 + zz * (1.0 - sig)))
            dgn = dgn + jnp.sum(dy * on * sz, axis=0, keepdims=True)
            don = dy * gn * sz
            do = r * (don - on * _half_sum(don * on) * (1.0 / HEAD))
            state, vnew = _bd(st_ref[rows, :]), vn_ref[rows, :]
            qg, kd, w, amat = loc_ref[0, rows, :], loc_ref[1, rows, :], loc_ref[3, rows, :], loc_ref[4, rows, :]
            dvn = _pk_tn(amat, do) + _nn(kd, ds)
            dloc_ref[0, rows, :] = _nt(do, state)
            dloc_ref[1, rows, :] = _nt(vnew, ds)
            dloc_ref[2, rows, :] = dvn
            dloc_ref[3, rows, :] = -_nt(dvn, state)
            dloc_ref[4, rows, :] = jnp.where(tril, _pk_nt(do, vnew), 0.0)
            degl = _half_sum(jnp.sum(state * ds, axis=0, keepdims=True))
            degl_ref[rows, :] = jnp.broadcast_to(degl, shape)
            grow = jnp.where(bdm, _tn(qg, do) - _tn(w, dvn), 0.0)
            return ds * _row0(egl_ref[rows, :]) + grow, dgn

        _, dgn = lax.fori_loop(0, nc, step, (jnp.zeros((2 * HEAD, 2 * HEAD), F32), jnp.zeros((1, 2 * HEAD), F32)))
        dgn_ref[...] += dgn

    zspec = pl.BlockSpec((seq, 2 * HEAD), lambda b, hp: (b, COL_DNZ + hp))
    one = _pair(seq)
    out = SDS((t, DN_HEADS * HEAD), F32)
    return pl.pallas_call(
        body, name=name, grid=(t // seq, DN_HEADS // 2),
        in_specs=[_pair(seq, 5), one, zspec, _whole((1, 2 * HEAD)), one, one, one, one],
        out_specs=[_pair(seq, 5), one, one, _whole((1, 2 * HEAD))],
        out_shape=[SDS((5, t, DN_HEADS * HEAD), F32), out, out, SDS((1, 2 * HEAD), F32)],
        compiler_params=_params(("arbitrary", "arbitrary")))(loc, egl, u, gn, o, vn, states, dy)


def gdn_prep_bwd(u, cw, alog_row, dt_row, dloc, degl, *, seq, name):
    t = u.shape[0]
    nc = seq // DN_CHUNK

    def body(q_ref, k_ref, v_ref, ba_ref, cq_ref, ck_ref, cv_ref, alog_ref, dt_ref, dloc_ref, degl_ref,
             dqkv_ref, dba_ref, dcw_ref, dhs_ref, act_sc, b_sc, gc_sc, c_sc):
        hp = pl.program_id(0)

        @pl.when(pl.program_id(1) == 0)
        def _():
            dcw_ref[...] = jnp.zeros_like(dcw_ref)
            dhs_ref[...] = jnp.zeros_like(dhs_ref)

        pre_refs, cw_refs = (q_ref, k_ref, v_ref), (cq_ref, ck_ref, cv_ref)
        _gdn_inputs(pre_refs, cw_refs, ba_ref, alog_ref, dt_ref, hp, act_sc, b_sc, gc_sc, c_sc)

        def finish(rows, m, tt):
            q, k, v, b = m["q"], m["k"], m["v"], m["b"]
            dqg, dkd, du, dw, da = (dloc_ref[x, rows, :] for x in range(5))
            dm, eg = m["dm"], m["eg"]
            dt = _pk_nt(du, m["vb"]) + _pk_nt(dw, m["kbg"])
            dvb, dkbg = _pk_tn(tt, du), _pk_tn(tt, dw)
            yield
            dtt = _pk_nt(dt, tt, hi=True)
            yield
            dl = jnp.where(m["strict"], -_pk_tn(tt, dtt, hi=True), 0.0)
            yield
            dkk = dl * dm
            dqk = da * dm
            dd = dl * m["kk"] + da * m["qk"]
            dkb = _pk_nn(dkk, k) + dkbg * eg
            dq = _pk_nn(dqk, k) + dqg * eg
            yield
            dk = _pk_tn(dkk, m["kb"]) + _pk_tn(dqk, q) + dkd * m["ekd"] + dkb * b
            db = _half_sum(dkb * k + dvb * v)
            yield
            mx = jnp.where(m["tril"], dd * dm, 0.0)
            tk = _half_sum(dkd * m["kd"])
            colsum = jnp.where(m["eye"], jnp.broadcast_to(jnp.sum(mx, axis=0, keepdims=True), mx.shape), 0.0)
            dgc = _half_sum(mx) - _half_sum(colsum) + _half_sum(dqg * m["qg"] + dkbg * m["kbg"]) - tk
            dglast = jnp.sum(tk, axis=0, keepdims=True) + _row0(degl_ref[rows, :]) * jnp.exp(m["glast"])
            act_sc[0, rows, :] = dq
            act_sc[1, rows, :] = dk
            act_sc[2, rows, :] = dvb * b
            b_sc[rows, :] = db
            gc_sc[rows, :] = dgc + jnp.where(m["row"] == DN_CHUNK - 1, dglast, 0.0)

        _gdn_chunk_loop(nc, act_sc, b_sc, gc_sc, finish)

        beta, g, beta_blk, sp_arg, a_exp, g_blk = _gdn_gates(ba_ref, alog_ref, dt_ref, hp)
        dg = _chunk_rev_cumsum(gc_sc[...])
        lane = _iota2(beta_blk.shape, 1)
        ha = 2 * hp
        db = b_sc[...]
        at = lambda idx, x_a, x_b: (jnp.where(lane == idx, _lane_col(x_a, 0), 0.0)
                                    + jnp.where(lane == idx + 1, _lane_col(x_b, HEAD), 0.0))
        dg_blk = at(DN_HEADS + ha, dg, dg)
        dal = dg_blk * (-a_exp) * _sigmoid(sp_arg)
        dba_ref[...] = at(ha, db, db) * beta_blk * (1.0 - beta_blk) + dal
        dhs_ref[0:1, :] += jnp.sum(dg_blk * g_blk, axis=0, keepdims=True)
        dhs_ref[1:2, :] += jnp.sum(dal, axis=0, keepdims=True)
        for idx in range(3):
            c = c_sc[idx]
            _, sig, hat, r = _gdn_act(c, _DN_SCALE[idx])
            da_ = act_sc[idx]
            if _DN_SCALE[idx] is not None:
                da_ = da_ * _DN_SCALE[idx]
                da_ = r * (da_ - hat * _half_sum(da_ * hat))
            dx, dcw = _conv_bwd(da_ * (sig * (1.0 + c * (1.0 - sig))), pre_refs[idx][...],
                                [cw_refs[idx][k:k + 1, :] for k in range(4)])
            dqkv_ref[idx] = dx
            dcw_ref[idx] += dcw

    pair = DN_HEADS // 2
    in_specs = [_swap(s) for s in _gdn_in_specs(seq)] + [_swap(_pair(seq, 5)), _swap(_pair(seq))]
    return pl.pallas_call(
        body, name=name, grid=(pair, t // seq), in_specs=in_specs,
        out_specs=[_swap(_pair(seq, 3)), pl.BlockSpec((None, seq, 2 * HEAD), lambda hp, b: (hp, b, 0)),
                   pl.BlockSpec((3, 4, 2 * HEAD), lambda hp, b: (0, 0, hp)),
                   pl.BlockSpec((None, 2, 2 * HEAD), lambda hp, b: (hp, 0, 0))],
        out_shape=[SDS((3, t, DN_HEADS * HEAD), F32), SDS((pair, t, 2 * HEAD), F32), SDS((3, 4, DN_HEADS * HEAD), F32),
                   SDS((pair, 2, 2 * HEAD), F32)],
        scratch_shapes=[pltpu.VMEM((3, seq, 2 * HEAD), F32)] + [pltpu.VMEM((seq, 2 * HEAD), F32)] * 2
        + [pltpu.VMEM((3, seq, 2 * HEAD), F32)],
        compiler_params=_params(("arbitrary", "arbitrary")))(u, u, u, u, cw, cw, cw, alog_row, dt_row, dloc, degl)


def mix_out(y_lru, o, y_dn, w_out, h, *, name, tm=512):
    t, d = h.shape
    tm = min(tm, t)

    def body(a_ref, b_ref, c_ref, w_ref, h_ref, o_ref, y_ref):
        y_ref[:, 0:LRU_W] = a_ref[...].astype(BF16)
        y_ref[:, LRU_W:LRU_W + ATT_W] = b_ref[...].astype(BF16)
        y_ref[:, LRU_W + ATT_W:] = c_ref[...].astype(BF16)
        o_ref[...] = h_ref[...] + _nn(y_ref[...], w_ref[...])

    rows = lambda width: pl.BlockSpec((tm, width), lambda i: (i, 0))
    return pl.pallas_call(
        body, name=name, grid=(t // tm,), in_specs=[rows(LRU_W), rows(ATT_W), rows(LRU_W), _whole((d, d)), rows(d)],
        out_specs=[rows(d), rows(d)], out_shape=[SDS((t, d), F32), SDS((t, d), BF16)],
        compiler_params=_params(("parallel",)))(y_lru, o, y_dn, w_out, h)


def mix_out_bwd(dout, w_out, *, name, tm=512):
    t, d = dout.shape
    tm = min(tm, t)

    def body(d_ref, w_ref, a_ref, b_ref, c_ref):
        dy = _nt(d_ref[...], w_ref[...])
        a_ref[...] = dy[:, 0:LRU_W]
        b_ref[...] = dy[:, LRU_W:LRU_W + ATT_W]
        c_ref[...] = dy[:, LRU_W + ATT_W:]

    rows = lambda width: pl.BlockSpec((tm, width), lambda i: (i, 0))
    return pl.pallas_call(
        body, name=name, grid=(t // tm,), in_specs=[rows(d), _whole((d, d))], out_specs=[rows(LRU_W), rows(ATT_W), rows(LRU_W)],
        out_shape=[SDS((t, LRU_W), F32), SDS((t, ATT_W), F32), SDS((t, LRU_W), F32)],
        compiler_params=_params(("parallel",)))(dout, w_out)


def mix_in_bwd(h, gain, dout, w_in, dx, dgate, dq, dk, dv, dqkv, dz, dba, *, name, tm=512):
    t, d = h.shape
    tm = min(tm, t)

    def body(h_ref, g_ref, do_ref, w_ref, dx_ref, dgate_ref, dq_ref, dk_ref, dv_ref, dqkv_ref, dz_ref, dba_ref,
             dh_ref, dg_ref, du_ref):
        @pl.when(pl.program_id(0) == 0)
        def _():
            dg_ref[...] = jnp.zeros_like(dg_ref)

        off = 0
        for piece in (dx_ref[...], dgate_ref[...], dq_ref[...], dk_ref[...], dv_ref[...], dqkv_ref[0], dqkv_ref[1],
                      dqkv_ref[2], dz_ref[...], dba_ref[0] + dba_ref[1]):
            du_ref[:, off:off + piece.shape[1]] = piece.astype(BF16)
            off += piece.shape[1]
        du_ref[:, off:] = jnp.zeros((tm, D_IN_PAD - off), BF16)
        g = g_ref[...]
        _, xh, r = _rms_fwd(h_ref[...], g)
        dh, dg = _rms_bwd(_nt(du_ref[...], w_ref[...]), xh, r, g)
        dh_ref[...] = do_ref[...] + dh
        dg_ref[...] += dg

    rows = lambda width: pl.BlockSpec((tm, width), lambda i: (i, 0))
    return pl.pallas_call(
        body, name=name, grid=(t // tm,),
        in_specs=[rows(d), _whole((1, d)), rows(d), _whole((d, D_IN_PAD)), rows(LRU_W), rows(LRU_W), rows(ATT_W),
                  rows(2 * HEAD), rows(2 * HEAD), pl.BlockSpec((3, tm, DN_HEADS * HEAD), lambda i: (0, i, 0)),
                  rows(DN_HEADS * HEAD), pl.BlockSpec((2, tm, 2 * HEAD), lambda i: (0, i, 0))],
        out_specs=[rows(d), _whole((1, d)), rows(D_IN_PAD)],
        out_shape=[SDS((t, d), F32), SDS((1, d), F32), SDS((t, D_IN_PAD), BF16)],
        compiler_params=_params(("arbitrary",)))(h, gain, dout, w_in, dx, dgate, dq, dk, dv, dqkv, dz, dba)


def _block_diag(w):
    out = jnp.zeros((LRU_W, LRU_W), w.dtype)
    for h in range(LRU_W // HEAD):
        out = lax.dynamic_update_slice(out, w[h], (h * HEAD, h * HEAD))
    return out


def _diag_blocks(w):
    per = LRU_HALF // HEAD
    return jnp.stack([w[h // per, (h % per) * HEAD:(h % per + 1) * HEAD, (h % per) * HEAD:(h % per + 1) * HEAD]
                      for h in range(LRU_W // HEAD)])


def layer_params(w, wl, l, bias):
    row = lambda a: a[l].reshape(1, -1)
    return dict(
        ffn1_norm=row(w["ffn1_norm"]), ffn1=(wl["ffn1_w_gate"], wl["ffn1_w_up"], wl["ffn1_w_down"]),
        mix_norm=row(w["mix_norm"]) + wl["tie1"][0:1, 0:1], w_in=wl["w_in"],
        lru=(wl["lru_conv_w"], row(w["lru_conv_b"]), _block_diag(w["lru_w_a"][l]), row(w["lru_b_a"]),
             _block_diag(w["lru_w_x"][l]), row(w["lru_b_x"]), row(w["lru_lambda"])),
        bias=bias, sink_rows=jnp.repeat(w["attn_sinks"][l], BLOCK_Q).reshape(ATT_HEADS * BLOCK_Q, 1),
        dn_cw=wl["dn_conv_w"], dn_alog=_ba_row(w["dn_a_log"][l]), dn_dt=_ba_row(w["dn_dt_bias"][l]),
        dn_norm=jnp.tile(row(w["dn_norm"]), (1, 2)), w_out=wl["w_out"],
        ffn2_norm=row(w["ffn2_norm"]), ffn2=(wl["ffn2_w_gate"], wl["ffn2_w_up"], wl["ffn2_w_down"]),
        ple_norm=row(w["ple_norm"]), ple_w_gate=wl["ple_w_gate"], ple_w_proj=wl["ple_w_proj"])


def _ba_row(per_head):
    return jnp.pad(per_head, (DN_HEADS, 2 * HEAD - 2 * DN_HEADS)).reshape(1, 2 * HEAD)


def mixer_fwd(h, p, nb, seq, tag):
    u, n = norm_matmul(h, p["mix_norm"], p["w_in"], name=f"mix_in_{tag}")
    y_lru = lru_fwd(u, *p["lru"], seq=seq, name=f"lru_fwd_{tag}")
    o = swa_fwd(u, p["bias"], p["sink_rows"], seq=seq, name=f"swa_fwd_{tag}")
    loc, egl = gdn_prep(u, p["dn_cw"], p["dn_alog"], p["dn_dt"], seq=seq, name=f"gdn_prep_{tag}")
    y_dn, o_raw, vn, st = gdn_scan(loc, egl, u, p["dn_norm"], seq=seq, name=f"gdn_scan_{tag}")
    out, ycat = mix_out(y_lru, o, y_dn, p["w_out"], h, name=f"mix_out_{tag}")
    return out, dict(h=h, u=u, n=n, loc=loc, egl=egl, o_raw=o_raw, vn=vn, st=st, ycat=ycat)


def mixer_bwd(dout, s, p, nb, seq, tag):
    u = s["u"]
    dy_lru, do, dy_dn = mix_out_bwd(dout, p["w_out"], name=f"mix_out_dx_{tag}")
    g = {"w_out": matmul(s["ycat"], dout, ta=True, tm=1024, name=f"mix_out_dw_{tag}")}
    dx, dgate, dcw, dwa, dwx, dvec = lru_bwd(u, *p["lru"], dy_lru, seq=seq, name=f"lru_bwd_{tag}")
    g.update(lru_conv_w=dcw, lru_conv_b=dvec[0], lru_w_a=_diag_blocks(dwa), lru_b_a=dvec[1], lru_w_x=_diag_blocks(dwx),
             lru_b_x=dvec[2], lru_lambda=dvec[3])
    dq, dk, dv, dbias, dsink = swa_bwd(u, p["bias"], p["sink_rows"], do, seq=seq, name=f"swa_bwd_{tag}")
    g.update(attn_sinks=dsink.reshape(ATT_HEADS, BLOCK_Q).sum(axis=1), bias=dbias)
    dloc, degl, dz, dgn = gdn_scan_bwd(s["loc"], s["egl"], u, p["dn_norm"], s["o_raw"], s["vn"], s["st"], dy_dn, seq=seq,
                                       name=f"gdn_scan_bwd_{tag}")
    dqkv, dba, dcw3, dhs = gdn_prep_bwd(u, p["dn_cw"], p["dn_alog"], p["dn_dt"], dloc, degl, seq=seq,
                                        name=f"gdn_prep_bwd_{tag}")
    dhs = dhs.sum(axis=0)[:, DN_HEADS:2 * DN_HEADS]
    g.update(dn_conv_w=dcw3.transpose(1, 0, 2).reshape(4, 3 * DN_HEADS * HEAD), dn_a_log=dhs[0], dn_dt_bias=dhs[1],
             dn_norm=dgn[0, :HEAD] + dgn[0, HEAD:])
    dh, dgain, du = mix_in_bwd(s["h"], p["mix_norm"], dout, p["w_in"], dx, dgate, dq, dk, dv, dqkv, dz, dba,
                               name=f"mix_in_bwd_{tag}")
    g["w_in"] = matmul(s["n"], du, ta=True, tm=1024, tn=640, name=f"mix_in_dw_{tag}")
    g["mix_norm"] = dgain[0]
    return dh, g


SHARDED = ("ffn1_w_gate", "ffn1_w_up", "ffn1_w_down", "w_in", "w_out", "ffn2_w_gate", "ffn2_w_up", "ffn2_w_down",
           "ple_w_gate", "ple_w_proj")
PER_LAYER_SMALL = ("ffn1_norm", "mix_norm", "lru_conv_w", "lru_conv_b", "lru_w_a", "lru_b_a", "lru_w_x", "lru_b_x",
                   "lru_lambda", "attn_sinks", "dn_conv_w", "dn_a_log", "dn_dt_bias", "dn_norm", "ffn2_norm", "ple_norm")


GRAD_PARTS = (("ple_w_gate", "ple_w_proj", "ffn2_w_gate", "ffn2_w_up", "ffn2_w_down"), ("w_in", "w_out"),
              ("ffn1_w_gate", "ffn1_w_up", "ffn1_w_down"))
WEIGHT_PARTS = (("ffn1_w_gate", "ffn1_w_up", "ffn1_w_down"),
                ("w_in", "w_out", "ffn2_w_gate", "ffn2_w_up", "ffn2_w_down", "ple_w_gate", "ple_w_proj", "lru_conv_w",
                 "dn_conv_w"))


def _col_shards(a):
    r, c = a.shape
    return a.reshape(r, N_CHIP, c // N_CHIP).transpose(1, 0, 2)


def local_step(x, p, target, w, layer_weights, layer_grads, bmap, nb, seq):
    bias = relbias_fwd(w["rel_bias"], bmap, name="relbias_fwd")
    h, saved = x, []
    for l in range(N_LAYER):
        wl = layer_weights(l, 0, h)
        s = dict(h0=h)
        h = ffn_fwd(h, w["ffn1_norm"][l].reshape(1, -1) + wl["tie0"][0:1, 0:1], wl["ffn1_w_gate"], wl["ffn1_w_up"],
                    wl["ffn1_w_down"], name=f"ffn1_fwd_{l}")
        wl.update(layer_weights(l, 1, h))
        pr = layer_params(w, wl, l, bias)
        h, s["mix"] = mixer_fwd(h, pr, nb, seq, l)
        s["h2"] = h
        h = ffn_fwd(h, pr["ffn2_norm"], *pr["ffn2"], name=f"ffn2_fwd_{l}")
        s["h3"] = h
        h = ple_fwd(h, pr["ple_norm"], pr["ple_w_gate"], p[l], pr["ple_w_proj"], name=f"ple_fwd_{l}")
        saved.append((pr, s))
    dh, dgf, loss = loss_head(h, w["final_norm"].reshape(1, -1), target, name="loss_head")

    per_layer, dbias, token = [None] * N_LAYER, None, None
    for l in reversed(range(N_LAYER)):
        pr, s = saved[l]
        g = {}
        dout = dh
        ple_norm = pr["ple_norm"] if token is None else pr["ple_norm"] + token[0:1, 0:1]
        dh, n, dga, dpp, dg = ple_bwd(s["h3"], ple_norm, pr["ple_w_gate"], p[l], pr["ple_w_proj"], dout, name=f"ple_bwd_{l}")
        g["ple_norm"] = dg[0]
        g["ple_w_gate"] = matmul(n, dga, ta=True, tm=1024, name=f"ple_dwg_{l}").reshape(N_CHIP, -1, D_MODEL)
        g["ple_w_proj"] = _col_shards(matmul(p[l], dpp, ta=True, name=f"ple_dwp_{l}"))
        for nm, hin in (("ffn2", s["h2"]), ("ffn1", s["h0"])):
            if nm == "ffn1":
                lru = list(pr["lru"])
                lru[1] = lru[1] + token[0:1, 0:1]
                dh, gm = mixer_bwd(dh, s["mix"], dict(pr, lru=tuple(lru)), nb, seq, l)
                dbias = gm.pop("bias") if dbias is None else dbias + gm.pop("bias")
                gm["w_in"] = _col_shards(gm["w_in"][:, :D_IN])
                gm["w_out"] = gm["w_out"].reshape(N_CHIP, -1, D_MODEL)
                g.update(gm)
                token = layer_grads(l, 1, {k: g.pop(k) for k in GRAD_PARTS[1]}, dh)
            dout = dh
            dh, n, da, db, sact, dg = ffn_bwd_act(hin, pr[nm + "_norm"] + token[0:1, 0:1] if nm == "ffn1" else pr[nm + "_norm"],
                                                  dout, *pr[nm], name=f"{nm}_bwd_act_{l}")
            g[nm + "_norm"] = dg[0]
            g[nm + "_w_gate"], g[nm + "_w_up"], g[nm + "_w_down"] = ffn_bwd_w(n, da, db, sact, dout, name=f"{nm}_bwd_w_{l}")
            part = 0 if nm == "ffn2" else 2
            token = layer_grads(l, part, {k: g.pop(k) for k in GRAD_PARTS[part]}, dh)
        per_layer[l] = g
    grads = {k: jnp.stack([per_layer[l][k] for l in range(N_LAYER)]) for k in PER_LAYER_SMALL}
    grads["rel_bias"] = relbias_bwd(dbias, bmap, name="relbias_bwd")[:, :ATT_HEADS]
    grads["final_norm"] = dgf[0]
    return loss, dh, grads


HBM_SPEC = pl.BlockSpec(memory_space=pltpu.HBM)


def _place():
    x, y, c = lax.axis_index("x"), lax.axis_index("y"), lax.axis_index("c")
    chips = [(1 - x, y), (x, 1 - y), (1 - x, 1 - y)]
    return x, y, c, 2 * x + y, (x, y, 1 - c), chips, [2 * cx + cy for cx, cy in chips]


def _remote(src, dst, send_sem, recv_sem, to):
    return pltpu.make_async_remote_copy(src_ref=src, dst_ref=dst, send_sem=send_sem, recv_sem=recv_sem, device_id=to,
                                        device_id_type=MESH)


def place_shard(w, chip_arr, dtype, *, name):
    nl, r, c = w.shape
    tr = next(cand for cand in (256, 128, 64, 32, 16, 8, r) if r % cand == 0)

    def body(chip_ref, w_ref, o_ref):
        o_ref[...] = w_ref[...].astype(dtype)

    return pl.pallas_call(
        body, name=name,
        grid_spec=pltpu.PrefetchScalarGridSpec(
            num_scalar_prefetch=1, grid=(nl, r // tr),
            in_specs=[pl.BlockSpec((None, tr, c), lambda l, i, chip: (l, i, 0))],
            out_specs=pl.BlockSpec((None, None, tr, c), lambda l, i, chip: (chip[0], l, i, 0))),
        out_shape=SDS((N_CHIP, nl, r, c), dtype), compiler_params=_params(("parallel", "parallel")))(chip_arr, w)


def allgather_shards(shards, *, name):
    n = len(shards)

    def body(*refs):
        outs = refs[n:2 * n]
        send, recv, fsend, frecv = refs[2 * n:]
        x, y, c, me, sib, chips, cids = _place()
        first, passed = [], []
        for k in range(n):
            for j, chip in enumerate(chips):
                mine = outs[k].at[me, c]
                first.append(_remote(mine, mine, send.at[3 * k + j], recv.at[3 * k + j], (*chip, c)))
                first[-1].start()
        for k in range(n):
            for j in range(3):
                piece = outs[k].at[cids[j], c]
                _remote(piece, piece, send.at[3 * k + j], recv.at[3 * k + j], sib).wait_recv()
                passed.append(_remote(piece, piece, fsend.at[3 * k + j], frecv.at[3 * k + j], sib))
                passed[-1].start()
        for k in range(n):
            for j in range(3):
                piece = outs[k].at[cids[j], 1 - c]
                _remote(piece, piece, fsend.at[3 * k + j], frecv.at[3 * k + j], sib).wait_recv()
        for cp in first + passed:
            cp.wait_send()

    return pl.pallas_call(
        body, name=name, in_specs=[HBM_SPEC] * n, out_specs=[HBM_SPEC] * n,
        out_shape=[SDS(s.shape, s.dtype) for s in shards], input_output_aliases={k: k for k in range(n)},
        scratch_shapes=[pltpu.SemaphoreType.DMA((3 * n,))] * 4)(*shards)


def exchange_layers(gs, *, name):
    n = len(gs)

    def body(*refs):
        ins, outs, (send, recv) = refs[:n], refs[n:2 * n], refs[2 * n:]
        x, y, c, me, sib, chips, cids = _place()
        cps = [_remote(ins[k].at[1 - c], outs[k], send.at[k], recv.at[k], sib) for k in range(n)]
        for cp in cps:
            cp.start()
        for cp in cps:
            cp.wait()

    return pl.pallas_call(
        body, name=name, in_specs=[HBM_SPEC] * n, out_specs=[HBM_SPEC] * n,
        out_shape=[SDS(g.shape[1:], g.dtype) for g in gs], scratch_shapes=[pltpu.SemaphoreType.DMA((n,))] * 2)(*gs)


def reduce_to_shards(ss, *, name):
    n = len(ss)

    def body(*refs):
        ins, outs, (send, recv) = refs[:n], refs[n:2 * n], refs[2 * n:]
        x, y, c, me, sib, chips, cids = _place()
        cps = []
        for k in range(n):
            for j, chip in enumerate(chips):
                cps.append(_remote(ins[k].at[cids[j]], outs[k].at[j], send.at[3 * k + j], recv.at[3 * k + j], (*chip, c)))
                cps[-1].start()
        for k in range(n):
            for j in range(3):
                slot = outs[k].at[j]
                _remote(slot, slot, send.at[3 * k + j], recv.at[3 * k + j], sib).wait_recv()
        for cp in cps:
            cp.wait_send()

    return pl.pallas_call(
        body, name=name, in_specs=[HBM_SPEC] * n, out_specs=[HBM_SPEC] * n,
        out_shape=[SDS((N_CHIP - 1,) + s.shape[1:], s.dtype) for s in ss],
        scratch_shapes=[pltpu.SemaphoreType.DMA((3 * n,))] * 2)(*ss)


def share_layers(fs, *, name):
    n = len(fs)

    def body(*refs):
        outs, (send, recv) = refs[n:2 * n], refs[2 * n:]
        x, y, c, me, sib, chips, cids = _place()
        cps = [_remote(outs[k].at[c], outs[k].at[c], send.at[k], recv.at[k], sib) for k in range(n)]
        for cp in cps:
            cp.start()
        for k in range(n):
            theirs = outs[k].at[1 - c]
            _remote(theirs, theirs, send.at[k], recv.at[k], sib).wait_recv()
        for cp in cps:
            cp.wait_send()

    return pl.pallas_call(
        body, name=name, in_specs=[HBM_SPEC] * n, out_specs=[HBM_SPEC] * n, out_shape=[SDS(f.shape, f.dtype) for f in fs],
        input_output_aliases={k: k for k in range(n)}, scratch_shapes=[pltpu.SemaphoreType.DMA((n,))] * 2)(*fs)


N_DEV = 8


def allreduce_small(buf, *, name):
    rows = buf.shape[0]

    def body(in_ref, out_ref, gath, send, recv):
        x, y, c = lax.axis_index("x"), lax.axis_index("y"), lax.axis_index("c")
        mine = 4 * x + 2 * y + c
        gath[mine] = in_ref[...]
        cps = []
        for k in range(1, N_DEV):
            to = (x ^ (k >> 2), y ^ ((k >> 1) & 1), c ^ (k & 1))
            cps.append(_remote(in_ref, gath.at[mine], send.at[k - 1], recv.at[k - 1], to))
            cps[-1].start()
        for k in range(1, N_DEV):
            theirs = gath.at[4 * (x ^ (k >> 2)) + 2 * (y ^ ((k >> 1) & 1)) + (c ^ (k & 1))]
            _remote(theirs, theirs, send.at[k - 1], recv.at[k - 1], (x, y, c)).wait_recv()
        for cp in cps:
            cp.wait_send()
        acc = gath[0]
        for d in range(1, N_DEV):
            acc = acc + gath[d]
        out_ref[...] = acc

    vm = pl.BlockSpec(memory_space=pltpu.VMEM)
    return pl.pallas_call(
        body, name=name, in_specs=[vm], out_specs=vm, out_shape=SDS(buf.shape, F32),
        scratch_shapes=[pltpu.VMEM((N_DEV, rows, 128), F32), pltpu.SemaphoreType.DMA((N_DEV - 1,)),
                        pltpu.SemaphoreType.DMA((N_DEV - 1,))])(buf)


def add_sibling(g, r, c_arr, *, name, tr=256):
    _, m, cdim = g.shape
    assert m % tr == 0

    def body(c_ref, g_ref, r_ref, o_ref):
        o_ref[...] = (g_ref[...] + r_ref[...]).astype(o_ref.dtype)

    return pl.pallas_call(
        body, name=name,
        grid_spec=pltpu.PrefetchScalarGridSpec(
            num_scalar_prefetch=1, grid=(m // tr,),
            in_specs=[pl.BlockSpec((None, tr, cdim), lambda i, c: (c[0], i, 0)), pl.BlockSpec((tr, cdim), lambda i, c: (i, 0))],
            out_specs=pl.BlockSpec((tr, cdim), lambda i, c: (i, 0))),
        out_shape=SDS((m, cdim), BF16), compiler_params=_params(("parallel",)))(c_arr, g, r)


def sum_slots(own, r, place_arr, *, name, tr=256):
    _, m, cdim = r.shape
    tr = next(cand for cand in (tr, 128, 64, 32, 16, 8) if m % cand == 0)

    def body(p_ref, own_ref, r_ref, o_ref):
        o_ref[...] = ((own_ref[...].astype(F32) + r_ref[0].astype(F32)) + r_ref[1].astype(F32)) + r_ref[2].astype(F32)

    return pl.pallas_call(
        body, name=name,
        grid_spec=pltpu.PrefetchScalarGridSpec(
            num_scalar_prefetch=1, grid=(m // tr,),
            in_specs=[pl.BlockSpec((None, tr, cdim), lambda i, p: (p[0], i, 0)),
                      pl.BlockSpec((N_CHIP - 1, tr, cdim), lambda i, p: (0, i, 0))],
            out_specs=pl.BlockSpec((None, tr, cdim), lambda i, p: (p[1], i, 0))),
        out_shape=SDS((N_LAYER, m, cdim), F32), compiler_params=_params(("parallel",)))(place_arr, own, r)


SEM_SPEC = pl.BlockSpec(memory_space=pltpu.SEMAPHORE)
ANY_SPEC = pl.BlockSpec(memory_space=pl.ANY)
DATAFLOW = pltpu.SideEffectType.DATAFLOW_SIDE_EFFECTING


def _in_hbm(a):
    return pltpu.with_memory_space_constraint(a, pltpu.HBM)


def _my_rows(ref_rows, c, mine=True):
    half = ref_rows // 2
    start = (c if mine else 1 - c) * half
    return pl.ds(pl.multiple_of(start, 8), half)


def place_layer_shard(w, layer, chip_arr, dtype, after, *, name):
    _, r, c = w.shape
    tr = next(cand for cand in (256, 128, 64, 32, 16, 8, r) if r % cand == 0)

    def body(chip_ref, w_ref, after_ref, o_ref):
        o_ref[...] = w_ref[...].astype(dtype)

    return pl.pallas_call(
        body, name=name,
        grid_spec=pltpu.PrefetchScalarGridSpec(
            num_scalar_prefetch=1, grid=(r // tr,),
            in_specs=[pl.BlockSpec((None, tr, c), lambda i, chip: (layer, i, 0)), ANY_SPEC],
            out_specs=pl.BlockSpec((None, tr, c), lambda i, chip: (chip[0], i, 0))),
        out_shape=SDS((N_CHIP, r, c), dtype), compiler_params=_params(("parallel",)))(chip_arr, w, after)


def _gather_pieces(refs, n_split, c, me, cids):
    mine, theirs = [], []
    for k, ref in enumerate(refs):
        if k < n_split:
            rows = _my_rows(ref.shape[1], c)
            mine.append(ref.at[me, rows])
            theirs.append([ref.at[cid, rows] for cid in cids])
        else:
            mine.append(ref.at[me])
            theirs.append([ref.at[cid] for cid in cids])
    return mine, theirs


def gather_start(bufs, n_split, after, *, name):
    n = len(bufs)

    def body(*refs):
        ins, send, recv, token = refs[:n], refs[n + 1], refs[n + 2], refs[-1]
        x, y, c, me, sib, chips, cids = _place()
        mine, _ = _gather_pieces(ins, n_split, c, me, cids)
        for k in range(n):
            for j, chip in enumerate(chips):
                _remote(mine[k], mine[k], send.at[3 * k + j], recv.at[3 * k + j], (*chip, c)).start()
        token[...] = jnp.zeros_like(token)

    out = pl.pallas_call(
        body, name=name, in_specs=[HBM_SPEC] * n + [ANY_SPEC],
        out_specs=[SEM_SPEC, SEM_SPEC] + [HBM_SPEC] * n + [pl.BlockSpec(memory_space=pltpu.VMEM)],
        out_shape=[pltpu.SemaphoreType.DMA((3 * n,)), pltpu.SemaphoreType.DMA((3 * n,))]
        + [pltpu.HBM(b.shape, b.dtype) for b in bufs] + [SDS((8, 128), F32)],
        input_output_aliases={k: k + 2 for k in range(n)},
        compiler_params=pltpu.CompilerParams(has_side_effects=DATAFLOW))(*[_in_hbm(b) for b in bufs], after)
    return out[0], out[1], list(out[2:2 + n]), out[-1]


def gather_wait(send, recv, bufs, n_split, after, *, name):
    n = len(bufs)

    def body(*refs):
        ins, send_ref, recv_ref = refs[:n], refs[n], refs[n + 1]
        x, y, c, me, sib, chips, cids = _place()
        mine, theirs = _gather_pieces(ins, n_split, c, me, cids)
        for k in range(n):
            for j in range(3):
                _remote(mine[k], mine[k], send_ref.at[3 * k + j], recv_ref.at[3 * k + j], sib).wait_send()
                _remote(theirs[k][j], theirs[k][j], send_ref.at[3 * k + j], recv_ref.at[3 * k + j], sib).wait_recv()

    return list(pl.pallas_call(
        body, name=name, in_specs=[HBM_SPEC] * n + [SEM_SPEC, SEM_SPEC, ANY_SPEC], out_specs=[HBM_SPEC] * n,
        out_shape=[pltpu.HBM(b.shape, b.dtype) for b in bufs], input_output_aliases={k: k for k in range(n)},
        compiler_params=pltpu.CompilerParams(has_side_effects=DATAFLOW))(*bufs, send, recv, after))


def gather_forward(bufs, *, name):
    n = len(bufs)

    def body(*refs):
        outs, (send, recv) = refs[n:2 * n], refs[2 * n:]
        x, y, c, me, sib, chips, cids = _place()
        cps = []
        for k in range(n):
            for j in range(3):
                piece = outs[k].at[cids[j], _my_rows(outs[k].shape[1], c)]
                cps.append(_remote(piece, piece, send.at[3 * k + j], recv.at[3 * k + j], sib))
                cps[-1].start()
        for k in range(n):
            for j in range(3):
                piece = outs[k].at[cids[j], _my_rows(outs[k].shape[1], c, mine=False)]
                _remote(piece, piece, send.at[3 * k + j], recv.at[3 * k + j], sib).wait_recv()
        for cp in cps:
            cp.wait_send()

    return list(pl.pallas_call(
        body, name=name, in_specs=[HBM_SPEC] * n, out_specs=[HBM_SPEC] * n, out_shape=[SDS(b.shape, b.dtype) for b in bufs],
        input_output_aliases={k: k for k in range(n)}, scratch_shapes=[pltpu.SemaphoreType.DMA((3 * n,))] * 2)(*bufs))


def reduce_exchange(gs, *, name):
    n = len(gs)

    def body(*refs):
        ins, outs, (send, recv) = refs[:n], refs[n:2 * n], refs[2 * n:]
        x, y, c, me, sib, chips, cids = _place()
        cps = [_remote(ins[k].at[pl.ds(0, N_CHIP), _my_rows(ins[k].shape[1], c, mine=False)], outs[k], send.at[k],
                       recv.at[k], sib) for k in range(n)]
        for cp in cps:
            cp.start()
        for cp in cps:
            cp.wait()

    return list(pl.pallas_call(
        body, name=name, in_specs=[HBM_SPEC] * n, out_specs=[HBM_SPEC] * n,
        out_shape=[SDS((N_CHIP, g.shape[1] // 2, g.shape[2]), g.dtype) for g in gs],
        scratch_shapes=[pltpu.SemaphoreType.DMA((n,))] * 2)(*gs))


def _half_tile(half):
    return next(cand for cand in (256, 176, 128, 64, 32, 16) if half % cand == 0)


def reduce_add(g, r, c_arr, *, name):
    _, rows, cdim = g.shape
    half = rows // 2
    tr = _half_tile(half)

    def body(c_ref, g_ref, r_ref, o_ref):
        o_ref[...] = (g_ref[...] + r_ref[...]).astype(o_ref.dtype)

    return pl.pallas_call(
        body, name=name,
        grid_spec=pltpu.PrefetchScalarGridSpec(
            num_scalar_prefetch=1, grid=(N_CHIP, half // tr),
            in_specs=[pl.BlockSpec((None, tr, cdim), lambda j, i, c: (j, c[0] * (half // tr) + i, 0)),
                      pl.BlockSpec((None, tr, cdim), lambda j, i, c: (j, i, 0))],
            out_specs=pl.BlockSpec((None, tr, cdim), lambda j, i, c: (j, i, 0))),
        out_shape=SDS((N_CHIP, half, cdim), BF16), compiler_params=_params(("parallel", "parallel")))(c_arr, g, r)


def reduce_start(ss, *, name):
    n = len(ss)

    def body(*refs):
        ins, lands, send, recv, token = refs[:n], refs[n:2 * n], refs[2 * n], refs[2 * n + 1], refs[-1]
        x, y, c, me, sib, chips, cids = _place()
        for k in range(n):
            for j, chip in enumerate(chips):
                _remote(ins[k].at[cids[j]], lands[k].at[j], send.at[3 * k + j], recv.at[3 * k + j], (*chip, c)).start()
        token[...] = jnp.zeros_like(token)

    lands = [_in_hbm(lax.empty((N_CHIP - 1,) + s.shape[1:], s.dtype)) for s in ss]
    out = pl.pallas_call(
        body, name=name, in_specs=[HBM_SPEC] * (2 * n),
        out_specs=[SEM_SPEC, SEM_SPEC] + [HBM_SPEC] * (2 * n) + [pl.BlockSpec(memory_space=pltpu.VMEM)],
        out_shape=[pltpu.SemaphoreType.DMA((3 * n,)), pltpu.SemaphoreType.DMA((3 * n,))]
        + [pltpu.HBM(b.shape, b.dtype) for b in list(ss) + lands] + [SDS((8, 128), F32)],
        input_output_aliases={k: k + 2 for k in range(2 * n)},
        compiler_params=pltpu.CompilerParams(has_side_effects=DATAFLOW))(*[_in_hbm(s) for s in ss], *lands)
    return out[0], out[1], list(out[2:2 + n]), list(out[2 + n:2 + 2 * n]), out[-1]


def reduce_wait(send, recv, ss, lands, after, *, name):
    n = len(ss)

    def body(*refs):
        ins, land_refs, send_ref, recv_ref = refs[:n], refs[n:2 * n], refs[2 * n], refs[2 * n + 1]
        x, y, c, me, sib, chips, cids = _place()
        for k in range(n):
            for j in range(3):
                _remote(ins[k].at[cids[j]], land_refs[k].at[j], send_ref.at[3 * k + j], recv_ref.at[3 * k + j],
                        sib).wait_send()
                _remote(ins[k].at[cids[j]], land_refs[k].at[j], send_ref.at[3 * k + j], recv_ref.at[3 * k + j],
                        sib).wait_recv()

    out = pl.pallas_call(
        body, name=name, in_specs=[HBM_SPEC] * (2 * n) + [SEM_SPEC, SEM_SPEC, ANY_SPEC], out_specs=[HBM_SPEC] * (2 * n),
        out_shape=[pltpu.HBM(b.shape, b.dtype) for b in list(ss) + list(lands)],
        input_output_aliases={k: k for k in range(2 * n)},
        compiler_params=pltpu.CompilerParams(has_side_effects=DATAFLOW))(*ss, *lands, send, recv, after)
    return list(out[:n]), list(out[n:])


def reduce_sum(own, land, place_arr, layer, acc, *, name):
    _, half, cdim = land.shape
    tr = _half_tile(half)

    def body(p_ref, own_ref, land_ref, *rest):
        o_ref = rest[-1]
        o_ref[...] = ((own_ref[...].astype(F32) + land_ref[0].astype(F32)) + land_ref[1].astype(F32)) + land_ref[2].astype(F32)

    in_specs = [pl.BlockSpec((None, tr, cdim), lambda i, p: (p[0], i, 0)),
                pl.BlockSpec((N_CHIP - 1, tr, cdim), lambda i, p: (0, i, 0))]
    args = [place_arr, own, land]
    if acc is not None:
        in_specs.append(ANY_SPEC)
        args.append(acc)
    return pl.pallas_call(
        body, name=name,
        grid_spec=pltpu.PrefetchScalarGridSpec(
            num_scalar_prefetch=1, grid=(half // tr,), in_specs=in_specs,
            out_specs=pl.BlockSpec((None, tr, cdim), lambda i, p: (layer, p[1] * (half // tr) + i, 0))),
        out_shape=SDS((N_LAYER, 2 * half, cdim), F32), input_output_aliases={} if acc is None else {3: 0},
        compiler_params=_params(("parallel",)))(*args)


def reduce_share(fs, *, name):
    n = len(fs)

    def body(*refs):
        outs, (send, recv) = refs[n:2 * n], refs[2 * n:]
        x, y, c, me, sib, chips, cids = _place()
        cps = []
        for k in range(n):
            piece = outs[k].at[pl.ds(0, N_LAYER), _my_rows(outs[k].shape[1], c)]
            cps.append(_remote(piece, piece, send.at[k], recv.at[k], sib))
            cps[-1].start()
        for k in range(n):
            theirs = outs[k].at[pl.ds(0, N_LAYER), _my_rows(outs[k].shape[1], c, mine=False)]
            _remote(theirs, theirs, send.at[k], recv.at[k], sib).wait_recv()
        for cp in cps:
            cp.wait_send()

    return list(pl.pallas_call(
        body, name=name, in_specs=[HBM_SPEC] * n, out_specs=[HBM_SPEC] * n, out_shape=[SDS(f.shape, f.dtype) for f in fs],
        input_output_aliases={k: k for k in range(n)}, scratch_shapes=[pltpu.SemaphoreType.DMA((n,))] * 2)(*fs))


WEIGHTS = ("ffn1_norm", "ffn1_w_gate", "ffn1_w_up", "ffn1_w_down", "mix_norm", "w_in", "lru_conv_w", "lru_conv_b", "lru_w_a",
           "lru_b_a", "lru_w_x", "lru_b_x", "lru_lambda", "attn_sinks", "rel_bias", "dn_conv_w", "dn_a_log", "dn_dt_bias",
           "dn_norm", "w_out", "ffn2_norm", "ffn2_w_gate", "ffn2_w_up", "ffn2_w_down", "ple_norm", "ple_w_gate",
           "ple_w_proj", "final_norm")
CONV_SHARDED = ("lru_conv_w", "dn_conv_w")
FFN_TRANSPOSED = ("ffn1_w_gate", "ffn1_w_up", "ffn2_w_gate", "ffn2_w_up")
SMALL = tuple(k for k in WEIGHTS if k not in SHARDED)


def _pack(arrs):
    blocks = []
    for a in arrs:
        v = a.reshape(-1)
        blocks.append(jnp.pad(v, (0, -v.shape[0] % 1024)).reshape(-1, 128))
    return jnp.concatenate(blocks, axis=0)


def _unpack(buf, shapes):
    out, off = [], 0
    for s in shapes:
        n = int(np.prod(s))
        rows = 8 * -(-n // 1024)
        out.append(buf[off:off + rows].reshape(-1)[:n].reshape(s))
        off += rows
    return out


def _chip_cols(a):
    n, l, r, c = a.shape
    return a.transpose(1, 2, 0, 3).reshape(l, r, n * c)


def _chip_rows(a):
    n, l, r, c = a.shape
    return a.transpose(1, 0, 2, 3).reshape(l, n * r, c)


def kernel(x, p, ffn1_norm, ffn1_w_gate, ffn1_w_up, ffn1_w_down, mix_norm, w_in, lru_conv_w, lru_conv_b, lru_w_a, lru_b_a, lru_w_x, lru_b_x, lru_lambda, attn_sinks, rel_bias, dn_conv_w, dn_a_log, dn_dt_bias, dn_norm, w_out, ffn2_norm, ffn2_w_gate, ffn2_w_up, ffn2_w_down, ple_norm, ple_w_gate, ple_w_proj, final_norm, loss_target, m_ffn1_norm, m_ffn1_w_gate, m_ffn1_w_up, m_ffn1_w_down, m_mix_norm, m_w_in, m_lru_conv_w, m_lru_conv_b, m_lru_w_a, m_lru_b_a, m_lru_w_x, m_lru_b_x, m_lru_lambda, m_attn_sinks, m_rel_bias, m_dn_conv_w, m_dn_a_log, m_dn_dt_bias, m_dn_norm, m_w_out, m_ffn2_norm, m_ffn2_w_gate, m_ffn2_w_up, m_ffn2_w_down, m_ple_norm, m_ple_w_gate, m_ple_w_proj, m_final_norm, v_ffn1_norm, v_ffn1_w_gate, v_ffn1_w_up, v_ffn1_w_down, v_mix_norm, v_w_in, v_lru_conv_w, v_lru_conv_b, v_lru_w_a, v_lru_b_a, v_lru_w_x, v_lru_b_x, v_lru_lambda, v_attn_sinks, v_rel_bias, v_dn_conv_w, v_dn_a_log, v_dn_dt_bias, v_dn_norm, v_w_out, v_ffn2_norm, v_ffn2_w_gate, v_ffn2_w_up, v_ffn2_w_down, v_ple_norm, v_ple_w_gate, v_ple_w_proj, v_final_norm):
    given = dict(locals())
    stored = lambda k, a: jnp.swapaxes(a, 1, 2) if k in FFN_TRANSPOSED else a
    ws = {k: stored(k, given[k]) for k in WEIGHTS}
    ms = {k: stored(k, given["m_" + k]) for k in WEIGHTS}
    vs = {k: stored(k, given["v_" + k]) for k in WEIGHTS}
    nb, seq, d = x.shape
    t = nb * seq
    cx, cy, cc = lax.axis_index("x"), lax.axis_index("y"), lax.axis_index("c")
    chip = 2 * cx + cy

    chip_arr = chip.astype(jnp.int32).reshape(1)
    c_arr = cc.astype(jnp.int32).reshape(1)
    place_arr = jnp.stack([chip, cc]).astype(jnp.int32)
    groups = [(l, part) for l in range(N_LAYER) for part in range(len(WEIGHT_PARTS))]
    placed, started = {}, {}

    def place_group(i, after):
        l, part = groups[i]
        for k in WEIGHT_PARTS[part]:
            placed[l, k] = place_layer_shard(ws[k], l, chip_arr, F32 if k in CONV_SHARDED else BF16, after,
                                             name=f"place_{k}_{l}")

    def start_group(i, after):
        l, part = groups[i]
        ks = WEIGHT_PARTS[part]
        n_split = sum(k in SHARDED for k in ks)
        started[i] = (ks, n_split) + gather_start([placed[l, k] for k in ks], n_split, after, name=f"gather_start_{l}_{part}")

    place_group(0, jnp.zeros((8, 128), F32))
    start_group(0, jnp.zeros((8, 128), F32))
    for i in range(1, len(groups)):
        place_group(i, started[0][-1])

    def layer_weights(l, part, h):
        i = groups.index((l, part))
        ks, n_split, send, recv, bufs, _ = started[i]
        bufs = gather_wait(send, recv, bufs, n_split, h, name=f"gather_wait_{l}_{part}")
        tie = jnp.zeros((8, 128), F32)
        for nxt in [j for j in range(i + 1, len(groups)) if j not in started and groups[j][0] == groups[min(i + 1, len(groups) - 1)][0]]:
            start_group(nxt, bufs[0] if nxt == i + 1 else started[nxt - 1][-1])
            tie = started[nxt][-1]
        wl = dict(zip(ks, gather_forward(bufs[:n_split], name=f"gather_forward_{l}_{part}") + bufs[n_split:]))
        for k in ("w_in", "ple_w_proj", "lru_conv_w", "dn_conv_w"):
            if k in wl:
                wl[k] = wl[k].transpose(1, 0, 2).reshape(wl[k].shape[1], -1)
        for k in ("w_out", "ple_w_gate"):
            if k in wl:
                wl[k] = wl[k].reshape(-1, wl[k].shape[-1])
        if "w_in" in wl:
            wl["w_in"] = jnp.pad(wl["w_in"], ((0, 0), (0, D_IN_PAD - D_IN)))
        wl[f"tie{part}"] = tie
        return wl

    pending, finished, tokens = [], {k: None for k in SHARDED}, []

    def finish_reduce(after):
        ks, send, recv, sums, lands, l, part = pending.pop(0)
        sums, lands = reduce_wait(send, recv, sums, lands, after, name=f"reduce_wait_{l}_{part}")
        for k, s, land in zip(ks, sums, lands):
            finished[k] = reduce_sum(s, land, place_arr, l, finished[k], name=f"reduce_sum_{k}_{l}")

    def layer_grads(l, part, g, dh):
        ks = GRAD_PARTS[part]
        gs = [g[k] for k in ks]
        theirs = reduce_exchange(gs, name=f"reduce_exchange_{l}_{part}")
        sums = [reduce_add(a, b, c_arr, name=f"reduce_add_{k}_{l}") for k, a, b in zip(ks, gs, theirs)]
        send, recv, sums, lands, token = reduce_start(sums, name=f"reduce_start_{l}_{part}")
        pending.append((ks, send, recv, sums, lands, l, part))
        while len(pending) > 2:
            finish_reduce(dh)
        tokens.append(token)
        return token

    small_w = {k: ws[k] for k in SMALL if k not in CONV_SHARDED}
    bmap = jnp.asarray(_rel_bucket_map())
    loss, gx, grads = local_step(x.reshape(t, d), p.reshape(N_LAYER, t, PLE_DIM), loss_target.reshape(t, d), small_w,
                                 layer_weights, layer_grads, bmap, nb, seq)
    g_out, delta, new_m, new_v = {}, {}, {}, {}

    small_shapes = [grads[k].shape for k in SMALL]
    g_small = dict(zip(SMALL, _unpack(allreduce_small(_pack([grads[k] for k in SMALL]), name="allreduce_small"), small_shapes)))
    for k in CONV_SHARDED:
        width = ws[k].shape[-1]
        g_small[k] = lax.dynamic_slice_in_dim(g_small[k], chip * width, width, axis=2)
    g_out.update(g_small)
    shapes = [ws[k].shape for k in SMALL]
    tie = tokens[-1][0:1, 0:1]
    res = adamw(_pack([ws[k] for k in SMALL]) + tie, *[_pack([src[k] for k in SMALL]) for src in (g_out, ms, vs)],
                name="adamw_small")
    for dst, r in zip((delta, new_m, new_v), res):
        dst.update(zip(SMALL, _unpack(r, shapes)))

    after = res[0]
    for part, ks in enumerate(GRAD_PARTS):
        while pending and pending[0][0] == ks:
            finish_reduce(after)
        g_out.update(zip(ks, reduce_share([finished[k] for k in ks], name=f"reduce_share_{part}")))
        for k in ks:
            two_d = lambda a: a.reshape(-1, a.shape[-1])
            res = adamw(two_d(ws[k]), two_d(g_out[k]), two_d(ms[k]), two_d(vs[k]), name=f"adamw_{k}")
            delta[k], new_m[k], new_v[k] = (r.reshape(ws[k].shape) for r in res)
        after = res[0]

    total = lax.psum(loss[0, 0], ("x", "y", "c"))
    return (total, gx.reshape(nb, seq, d), *[stored(k, out[k]) for out in (g_out, delta, new_m, new_v) for k in WEIGHTS])
```

```python
import functools
import math

import numpy as np
import jax
import jax.numpy as jnp
from jax import lax
from jax.experimental import pallas as pl
from jax.experimental.pallas import tpu as pltpu

F32 = jnp.float32
BF16 = jnp.bfloat16

EPS = 1e-6
D_MODEL = 1024
D_FF = 2816
N_CHIP = 4
FF_BLK = D_FF // N_CHIP
HEAD = 64
LRU_W = 256
ATT_W = 512
ATT_HEADS = 8
KV_HEADS = 2
ATT_GROUP = 4
BLOCK_Q = 128
DN_HEADS = 4
DN_CHUNK = 64
D_IN = 2312
D_IN_PAD = 2560
PLE_DIM = 256
REL_BUCKETS = 32
LRU_C = 8.0
N_LAYER = 2

ADAM_LR, ADAM_B1, ADAM_B2, ADAM_EPS, ADAM_WD, ADAM_STEP = 0.001, 0.9, 0.999, 1e-08, 0.01, 10

VMEM_LIMIT = 56 << 20
MESH = pl.DeviceIdType.MESH
SDS = jax.ShapeDtypeStruct


def _dot(a, b, ca=1, cb=0, hi=False):
    dims = (((ca,), (cb,)), ((), ()))
    one = lambda u, v: lax.dot_general(u, v, dims, preferred_element_type=F32)
    a_hi, b_hi = a.astype(BF16), b.astype(BF16)
    if not hi:
        return one(a_hi, b_hi)
    a_lo = (a - a_hi.astype(F32)).astype(BF16)
    b_lo = (b - b_hi.astype(F32)).astype(BF16)
    return one(a_hi, b_hi) + (one(a_hi, b_lo) + one(a_lo, b_hi))


def _nn(a, b, hi=False):
    return _dot(a, b, 1, 0, hi)


def _nt(a, b, hi=False):
    return _dot(a, b, 1, 1, hi)


def _tn(a, b, hi=False):
    return _dot(a, b, 0, 0, hi)


def _sigmoid(x):
    return jax.nn.sigmoid(x)


def _softplus(x):
    return jnp.maximum(x, 0.0) + jnp.log1p(jnp.exp(-jnp.abs(x)))


def _neg_expm1(z):
    series = -z * (1.0 + z * (0.5 + z * (1.0 / 6.0 + z * (1.0 / 24.0 + z * (1.0 / 120.0)))))
    return jnp.where(z > -0.05, series, 1.0 - jnp.exp(z))


_GELU_C = math.sqrt(2.0 / math.pi)


def _gelu(x):
    t = jnp.tanh(_GELU_C * (x + 0.044715 * x * x * x))
    return 0.5 * x * (1.0 + t), t


def _gelu_grad(x, t):
    return 0.5 * (1.0 + t) + 0.5 * x * (1.0 - t * t) * _GELU_C * (1.0 + 3.0 * 0.044715 * x * x)


def _rms_fwd(h, g):
    r = lax.rsqrt(jnp.mean(h * h, axis=-1, keepdims=True) + EPS)
    xh = h * r
    return xh * g, xh, r


def _rms_bwd(dn, xh, r, g):
    dxh = dn * g
    dh = r * (dxh - xh * jnp.mean(dxh * xh, axis=-1, keepdims=True))
    return dh, jnp.sum(dn * xh, axis=0, keepdims=True)


def _shift_down(x, d, fill=0.0):
    row = lax.broadcasted_iota(jnp.int32, x.shape, 0)
    return jnp.where(row >= d, pltpu.roll(x, d, 0), fill)


def _shift_up(x, d, fill=0.0):
    n = x.shape[0]
    row = lax.broadcasted_iota(jnp.int32, x.shape, 0)
    return jnp.where(row < n - d, pltpu.roll(x, n - d, 0), fill)


def _conv_fwd(x, w):
    y = x * w[3]
    for k in range(3):
        y = y + _shift_down(x, 3 - k) * w[k]
    return y


def _conv_bwd(dy, x, w):
    dx = dy * w[3]
    rows = [None] * 4
    rows[3] = jnp.sum(dy * x, axis=0, keepdims=True)
    for k in range(3):
        dx = dx + _shift_up(dy, 3 - k) * w[k]
        rows[k] = jnp.sum(dy * _shift_down(x, 3 - k), axis=0, keepdims=True)
    r4 = lax.broadcasted_iota(jnp.int32, (4, x.shape[1]), 0)
    dw = jnp.zeros((4, x.shape[1]), F32)
    for k in range(4):
        dw = jnp.where(r4 == k, rows[k], dw)
    return dx, dw


FFN_SPLIT = 2


def _interleave(gens):
    pending = list(gens)
    while pending:
        for g in list(pending):
            if next(g, StopIteration) is StopIteration:
                pending.remove(g)


def _params(sem=None, vmem=VMEM_LIMIT):
    return pltpu.CompilerParams(dimension_semantics=sem, vmem_limit_bytes=vmem)


def _whole(shape):
    nd = len(shape)
    return pl.BlockSpec(shape, lambda *_: (0,) * nd)


def matmul(a, b, *, name, ta=False, tb=False, residual=None, out_dtype=F32, tm=512, tn=512, tk=512):
    m, k = (a.shape[1], a.shape[0]) if ta else a.shape
    n = b.shape[0] if tb else b.shape[1]
    tm, tn, tk = min(tm, m), min(tn, n), min(tk, k)
    assert m % tm == 0 and n % tn == 0 and k % tk == 0, (m, n, k, tm, tn, tk)
    nk = k // tk

    def body(*refs):
        if residual is None:
            a_ref, b_ref, o_ref, acc = refs
        else:
            a_ref, b_ref, r_ref, o_ref, acc = refs
        kk = pl.program_id(2)

        @pl.when(kk == 0)
        def _():
            acc[...] = jnp.zeros_like(acc)

        acc[...] += _dot(a_ref[...], b_ref[...], 0 if ta else 1, 1 if tb else 0)

        @pl.when(kk == nk - 1)
        def _():
            out = acc[...]
            if residual is not None:
                out = out + r_ref[...]
            o_ref[...] = out.astype(out_dtype)

    a_spec = pl.BlockSpec((tk, tm), lambda i, j, kk: (kk, i)) if ta else pl.BlockSpec((tm, tk), lambda i, j, kk: (i, kk))
    b_spec = pl.BlockSpec((tn, tk), lambda i, j, kk: (j, kk)) if tb else pl.BlockSpec((tk, tn), lambda i, j, kk: (kk, j))
    o_spec = pl.BlockSpec((tm, tn), lambda i, j, kk: (i, j))
    in_specs, args = [a_spec, b_spec], [a, b]
    if residual is not None:
        in_specs.append(o_spec)
        args.append(residual)
    return pl.pallas_call(
        body, name=name, grid=(m // tm, n // tn, nk), in_specs=in_specs, out_specs=o_spec,
        out_shape=SDS((m, n), out_dtype), scratch_shapes=[pltpu.VMEM((tm, tn), F32)],
        compiler_params=_params(("parallel", "parallel", "arbitrary")))(*args)


def norm_matmul(h, gain, w, *, name, tm=512, tn=512):
    t, d = h.shape
    tm = min(tm, t)
    n = w.shape[1]
    assert t % tm == 0 and n % tn == 0

    def body(h_ref, g_ref, w_ref, u_ref, n_ref):
        @pl.when(pl.program_id(1) == 0)
        def _():
            n_ref[...] = _rms_fwd(h_ref[...], g_ref[...])[0].astype(BF16)

        u_ref[...] = _nn(n_ref[...], w_ref[...])

    return pl.pallas_call(
        body, name=name, grid=(t // tm, n // tn),
        in_specs=[pl.BlockSpec((tm, d), lambda i, j: (i, 0)), _whole((1, d)), pl.BlockSpec((d, tn), lambda i, j: (0, j))],
        out_specs=[pl.BlockSpec((tm, tn), lambda i, j: (i, j)), pl.BlockSpec((tm, d), lambda i, j: (i, 0))],
        out_shape=[SDS((t, n), F32), SDS((t, d), BF16)],
        compiler_params=_params(("parallel", "arbitrary")))(h, gain, w)


def rms_bwd(h, gain, dn, dres, *, name, tm=512):
    t, d = h.shape
    tm = min(tm, t)

    def body(h_ref, g_ref, dn_ref, dr_ref, dh_ref, dg_ref):
        @pl.when(pl.program_id(0) == 0)
        def _():
            dg_ref[...] = jnp.zeros_like(dg_ref)

        g = g_ref[...]
        _, xh, r = _rms_fwd(h_ref[...], g)
        dh, dg = _rms_bwd(dn_ref[...], xh, r, g)
        dh_ref[...] = dr_ref[...] + dh
        dg_ref[...] += dg

    row = pl.BlockSpec((tm, d), lambda i: (i, 0))
    return pl.pallas_call(
        body, name=name, grid=(t // tm,), in_specs=[row, _whole((1, d)), row, row],
        out_specs=[row, _whole((1, d))], out_shape=[SDS((t, d), F32), SDS((1, d), F32)],
        compiler_params=_params(("arbitrary",)))(h, gain, dn, dres)


def ffn_fwd(h, gain, wg, wu, wd, *, name, tm=512):
    t, d = h.shape
    tm = min(tm, t)

    def body(h_ref, g_ref, wg_ref, wu_ref, wd_ref, o_ref, n_ref, a_ref, b_ref, acc):
        j = pl.program_id(1)

        @pl.when(j == 0)
        def _():
            n_ref[...] = _rms_fwd(h_ref[...], g_ref[...])[0].astype(BF16)
            acc[...] = jnp.zeros_like(acc)

        def part(rows):
            n = n_ref[rows, :]
            a = _nt(n, wg_ref[...])
            b = _nt(n, wu_ref[...])
            yield
            a_ref[rows, :] = a.astype(BF16)
            b_ref[rows, :] = b.astype(BF16)
            acc[rows, :] += _nn(a * _sigmoid(a) * b, wd_ref[...])

        _interleave([part(pl.ds(k * (tm // FFN_SPLIT), tm // FFN_SPLIT)) for k in range(FFN_SPLIT)])

        @pl.when(j == N_CHIP - 1)
        def _():
            o_ref[...] = h_ref[...] + 0.5 * acc[...]

    row = pl.BlockSpec((tm, d), lambda i, j: (i, 0))
    blk = pl.BlockSpec((None, tm, FF_BLK), lambda i, j: (j, i, 0))
    wspec = pl.BlockSpec((None, FF_BLK, d), lambda i, j: (j, 0, 0))
    act = SDS((N_CHIP, t, FF_BLK), BF16)
    return pl.pallas_call(
        body, name=name, grid=(t // tm, N_CHIP), in_specs=[row, _whole((1, d)), wspec, wspec, wspec],
        out_specs=[row, row, blk, blk], out_shape=[SDS((t, d), F32), SDS((t, d), BF16), act, act],
        scratch_shapes=[pltpu.VMEM((tm, d), F32)],
        compiler_params=_params(("parallel", "arbitrary")))(h, gain, wg, wu, wd)


def ffn_bwd_act(h, gain, dout, a, b, wg, wu, wd, *, name, tm=512):
    t, d = h.shape
    tm = min(tm, t)

    def body(h_ref, g_ref, do_ref, a_ref, b_ref, wg_ref, wu_ref, wd_ref, dh_ref, da_ref, db_ref, s_ref, dg_ref, dn_acc):
        i, j = pl.program_id(0), pl.program_id(1)

        @pl.when((i == 0) & (j == 0))
        def _():
            dg_ref[...] = jnp.zeros_like(dg_ref)

        @pl.when(j == 0)
        def _():
            dn_acc[...] = jnp.zeros_like(dn_acc)

        def part(rows):
            ds = _nt(0.5 * do_ref[rows, :], wd_ref[...])
            yield
            a = a_ref[rows, :].astype(F32)
            b = b_ref[rows, :].astype(F32)
            sig = _sigmoid(a)
            sa = a * sig
            db = ds * sa
            da = ds * b * (sig * (1.0 + a * (1.0 - sig)))
            s_ref[rows, :] = (sa * b).astype(BF16)
            da_ref[rows, :] = da.astype(BF16)
            db_ref[rows, :] = db.astype(BF16)
            yield
            dn_acc[rows, :] += _nn(da, wg_ref[...]) + _nn(db, wu_ref[...])

        _interleave([part(pl.ds(k * (tm // FFN_SPLIT), tm // FFN_SPLIT)) for k in range(FFN_SPLIT)])

        @pl.when(j == N_CHIP - 1)
        def _():
            g = g_ref[...]
            _, xh, r = _rms_fwd(h_ref[...], g)
            dh, dg = _rms_bwd(dn_acc[...], xh, r, g)
            dh_ref[...] = do_ref[...] + dh
            dg_ref[...] += dg

    row = pl.BlockSpec((tm, d), lambda i, j: (i, 0))
    blk = pl.BlockSpec((None, tm, FF_BLK), lambda i, j: (j, i, 0))
    wspec = pl.BlockSpec((None, FF_BLK, d), lambda i, j: (j, 0, 0))
    act = SDS((N_CHIP, t, FF_BLK), BF16)
    return pl.pallas_call(
        body, name=name, grid=(t // tm, N_CHIP), in_specs=[row, _whole((1, d)), row, blk, blk, wspec, wspec, wspec],
        out_specs=[row, blk, blk, blk, _whole((1, d))],
        out_shape=[SDS((t, d), F32), act, act, act, SDS((1, d), F32)],
        scratch_shapes=[pltpu.VMEM((tm, d), F32)],
        compiler_params=_params(("arbitrary", "arbitrary")))(h, gain, dout, a, b, wg, wu, wd)


def ffn_bwd_w(n, da, db, s, dout, *, name, tk=512):
    t, d = n.shape
    tk = min(tk, t)

    def body(n_ref, da_ref, db_ref, s_ref, do_ref, dwg_ref, dwu_ref, dwd_ref):
        @pl.when(pl.program_id(1) == 0)
        def _():
            dwg_ref[...] = jnp.zeros_like(dwg_ref)
            dwu_ref[...] = jnp.zeros_like(dwu_ref)
            dwd_ref[...] = jnp.zeros_like(dwd_ref)

        nn = n_ref[...]
        dwg_ref[...] += _tn(da_ref[...], nn)
        dwu_ref[...] += _tn(db_ref[...], nn)
        dwd_ref[...] += _tn(s_ref[...], 0.5 * do_ref[...])

    row = pl.BlockSpec((tk, d), lambda j, kk: (kk, 0))
    blk = pl.BlockSpec((None, tk, FF_BLK), lambda j, kk: (j, kk, 0))
    return pl.pallas_call(
        body, name=name, grid=(N_CHIP, t // tk), in_specs=[row, blk, blk, blk, row],
        out_specs=[pl.BlockSpec((None, FF_BLK, d), lambda j, kk: (j, 0, 0)),
                   pl.BlockSpec((None, FF_BLK, d), lambda j, kk: (j, 0, 0)),
                   pl.BlockSpec((None, FF_BLK, d), lambda j, kk: (j, 0, 0))],
        out_shape=[SDS((N_CHIP, FF_BLK, d), F32)] * 3,
        compiler_params=_params(("parallel", "arbitrary")))(n, da, db, s, dout)


def ple_fwd(h, gain, wpg, pl_in, wpp, *, name, tm=512):
    t, d = h.shape
    tm = min(tm, t)
    pd = pl_in.shape[1]

    def body(h_ref, g_ref, wpg_ref, p_ref, wpp_ref, o_ref):
        hh = h_ref[...]
        n = _rms_fwd(hh, g_ref[...])[0]
        gate = _sigmoid(_nn(n, wpg_ref[...]))
        o_ref[...] = hh + gate * _nn(p_ref[...], wpp_ref[...])

    row = pl.BlockSpec((tm, d), lambda i: (i, 0))
    return pl.pallas_call(
        body, name=name, grid=(t // tm,),
        in_specs=[row, _whole((1, d)), _whole((d, d)), pl.BlockSpec((tm, pd), lambda i: (i, 0)), _whole((pd, d))],
        out_specs=row, out_shape=SDS((t, d), F32), compiler_params=_params(("parallel",)))(h, gain, wpg, pl_in, wpp)


def ple_bwd(h, gain, wpg, pl_in, wpp, dout, *, name, tm=512):
    t, d = h.shape
    tm = min(tm, t)
    pd = pl_in.shape[1]

    def body(h_ref, g_ref, wpg_ref, p_ref, wpp_ref, do_ref, dh_ref, n_ref, dga_ref, dpp_ref, dg_ref):
        @pl.when(pl.program_id(0) == 0)
        def _():
            dg_ref[...] = jnp.zeros_like(dg_ref)

        g = g_ref[...]
        n, xh, r = _rms_fwd(h_ref[...], g)
        gate = _sigmoid(_nn(n, wpg_ref[...]))
        pp = _nn(p_ref[...], wpp_ref[...])
        do = do_ref[...]
        dga = do * pp * gate * (1.0 - gate)
        dh, dg = _rms_bwd(_nt(dga, wpg_ref[...]), xh, r, g)
        dh_ref[...] = do + dh
        n_ref[...] = n.astype(BF16)
        dga_ref[...] = dga.astype(BF16)
        dpp_ref[...] = (do * gate).astype(BF16)
        dg_ref[...] += dg

    row = pl.BlockSpec((tm, d), lambda i: (i, 0))
    return pl.pallas_call(
        body, name=name, grid=(t // tm,),
        in_specs=[row, _whole((1, d)), _whole((d, d)), pl.BlockSpec((tm, pd), lambda i: (i, 0)), _whole((pd, d)), row],
        out_specs=[row, row, row, row, _whole((1, d))],
        out_shape=[SDS((t, d), F32), SDS((t, d), BF16), SDS((t, d), BF16), SDS((t, d), BF16), SDS((1, d), F32)],
        compiler_params=_params(("arbitrary",)))(h, gain, wpg, pl_in, wpp, dout)


def loss_head(h, gain, target, *, name, tm=512):
    t, d = h.shape
    tm = min(tm, t)

    def body(h_ref, g_ref, t_ref, dh_ref, dg_ref, l_ref):
        @pl.when(pl.program_id(0) == 0)
        def _():
            dg_ref[...] = jnp.zeros_like(dg_ref)
            l_ref[...] = jnp.zeros_like(l_ref)

        g = g_ref[...]
        y, xh, r = _rms_fwd(h_ref[...], g)
        err = y - t_ref[...]
        l_ref[...] += 0.5 * jnp.sum(jnp.mean(err * err, axis=-1, keepdims=True), axis=0, keepdims=True)
        dh, dg = _rms_bwd(err * (1.0 / d), xh, r, g)
        dh_ref[...] = dh
        dg_ref[...] += dg

    row = pl.BlockSpec((tm, d), lambda i: (i, 0))
    return pl.pallas_call(
        body, name=name, grid=(t // tm,), in_specs=[row, _whole((1, d)), row],
        out_specs=[row, _whole((1, d)), _whole((1, 1))],
        out_shape=[SDS((t, d), F32), SDS((1, d), F32), SDS((1, 1), F32)],
        compiler_params=_params(("arbitrary",)))(h, gain, target)


def adamw(w, g, m, v, *, name):
    r, c = w.shape
    tr = r
    for cand in (512, 256, 128, 64, 32, 16, 8):
        if r % cand == 0:
            tr = cand
            break

    def body(w_ref, g_ref, m_ref, v_ref, d_ref, nm_ref, nv_ref):
        gg = g_ref[...]
        mm = ADAM_B1 * m_ref[...] + (1.0 - ADAM_B1) * gg
        vv = ADAM_B2 * v_ref[...] + (1.0 - ADAM_B2) * (gg * gg)
        m_hat = mm / (1.0 - ADAM_B1 ** ADAM_STEP)
        v_hat = vv / (1.0 - ADAM_B2 ** ADAM_STEP)
        d_ref[...] = -ADAM_LR * (m_hat / (jnp.sqrt(v_hat) + ADAM_EPS) + ADAM_WD * w_ref[...])
        nm_ref[...] = mm
        nv_ref[...] = vv

    blk = pl.BlockSpec((tr, c), lambda i: (i, 0))
    out = SDS((r, c), F32)
    return pl.pallas_call(body, name=name, grid=(r // tr,), in_specs=[blk] * 4, out_specs=[blk] * 3,
                          out_shape=[out, out, out], compiler_params=_params(("parallel",)))(w, g, m, v)


def _scan_fwd(a, b):
    d = 1
    while d < a.shape[0]:
        b = a * _shift_down(b, d, 0.0) + b
        a = a * _shift_down(a, d, 1.0)
        d *= 2
    return b


def _scan_rev(a, b):
    d = 1
    while d < a.shape[0]:
        b = a * _shift_up(b, d, 0.0) + b
        a = a * _shift_up(a, d, 1.0)
        d *= 2
    return b


LRU_HALF = 128


def _lru_in_specs(seq):
    half = LRU_W // LRU_HALF
    vec = pl.BlockSpec((1, LRU_HALF), lambda j, b: (0, j))
    mat = pl.BlockSpec((LRU_HALF, LRU_HALF), lambda j, b: (j, j))
    return [pl.BlockSpec((seq, LRU_HALF), lambda j, b: (b, j)), pl.BlockSpec((seq, LRU_HALF), lambda j, b: (b, half + j)),
            pl.BlockSpec((4, LRU_HALF), lambda j, b: (0, j)), vec, mat, vec, mat, vec, vec]


def _lru_math(x_ref, gate_ref, cw_ref, cb_ref, wa_ref, ba_ref, wx_ref, bx_ref, lam_ref):
    x = x_ref[...]
    gate = gate_ref[...]
    cw =[cw_ref[k:k + 1, :] for k in range(4)]
    xr = _conv_fwd(x, cw) + cb_ref[...]
    r = _sigmoid(_nn(xr, wa_ref[...]) + ba_ref[...])
    i = _sigmoid(_nn(xr, wx_ref[...]) + bx_ref[...])
    sp = _softplus(-lam_ref[...])
    log_a = -LRU_C * r * sp
    a = jnp.exp(log_a)
    mult = jnp.sqrt(_neg_expm1(2.0 * log_a))
    gi = i * xr
    h = _scan_fwd(a, mult * gi)
    gl, tg = _gelu(gate)
    return dict(x=x, gate=gate, cw=cw, xr=xr, r=r, i=i, sp=sp, a=a, mult=mult, gi=gi, h=h, gl=gl, tg=tg)


def lru_fwd(u, cw, cb, wa, ba, wx, bx, lam, *, seq, name):
    t = u.shape[0]

    def body(x_ref, gate_ref, cw_ref, cb_ref, wa_ref, ba_ref, wx_ref, bx_ref, lam_ref, y_ref):
        f = _lru_math(x_ref, gate_ref, cw_ref, cb_ref, wa_ref, ba_ref, wx_ref, bx_ref, lam_ref)
        y_ref[...] = f["gl"] * f["h"]

    return pl.pallas_call(
        body, name=name, grid=(LRU_W // LRU_HALF, t // seq), in_specs=_lru_in_specs(seq),
        out_specs=pl.BlockSpec((seq, LRU_HALF), lambda j, b: (b, j)), out_shape=SDS((t, LRU_W), F32),
        compiler_params=_params(("parallel", "parallel")))(u, u, cw, cb, wa, ba, wx, bx, lam)


def lru_bwd(u, cw, cb, wa, ba, wx, bx, lam, dy, *, seq, name):
    t = u.shape[0]

    def body(x_ref, gate_ref, cw_ref, cb_ref, wa_ref, ba_ref, wx_ref, bx_ref, lam_ref, dy_ref,
             dx_ref, dgate_ref, dcw_ref, dwa_ref, dwx_ref, dv_ref):
        @pl.when(pl.program_id(1) == 0)
        def _():
            dcw_ref[...] = jnp.zeros_like(dcw_ref)
            dwa_ref[...] = jnp.zeros_like(dwa_ref)
            dwx_ref[...] = jnp.zeros_like(dwx_ref)
            dv_ref[...] = jnp.zeros_like(dv_ref)

        f = _lru_math(x_ref, gate_ref, cw_ref, cb_ref, wa_ref, ba_ref, wx_ref, bx_ref, lam_ref)
        dy = dy_ref[...]
        a, h, xr, r, i, mult, gi, sp = f["a"], f["h"], f["xr"], f["r"], f["i"], f["mult"], f["gi"], f["sp"]
        dgate_ref[...] = dy * h * _gelu_grad(f["gate"], f["tg"])
        lamb = _scan_rev(_shift_up(a, 1, 0.0), dy * f["gl"])
        da = lamb * _shift_down(h, 1)
        dlog_a = da * a - (lamb * gi) * (a * a) / mult
        dgi = lamb * mult
        dra = dlog_a * (-LRU_C * sp) * r * (1.0 - r)
        dia = dgi * xr * i * (1.0 - i)
        dsp = jnp.sum(dlog_a * (-LRU_C * r), axis=0, keepdims=True)
        dlam = -dsp * _sigmoid(-lam_ref[...])
        dxr = dgi * i + _nt(dra, wa_ref[...]) + _nt(dia, wx_ref[...])
        dx, dcw = _conv_bwd(dxr, f["x"], f["cw"])
        dx_ref[...] = dx
        dcw_ref[...] += dcw
        dwa_ref[...] += _tn(xr, dra)
        dwx_ref[...] += _tn(xr, dia)
        rows = [jnp.sum(dxr, axis=0, keepdims=True), jnp.sum(dra, axis=0, keepdims=True),
                jnp.sum(dia, axis=0, keepdims=True), dlam]
        r8 = lax.broadcasted_iota(jnp.int32, (8, LRU_HALF), 0)
        acc = jnp.zeros((8, LRU_HALF), F32)
        for k, row in enumerate(rows):
            acc = jnp.where(r8 == k, row, acc)
        dv_ref[...] += acc

    nhalf = LRU_W // LRU_HALF
    col = pl.BlockSpec((seq, LRU_HALF), lambda j, b: (b, j))
    mat = pl.BlockSpec((None, LRU_HALF, LRU_HALF), lambda j, b: (j, 0, 0))
    return pl.pallas_call(
        body, name=name, grid=(nhalf, t // seq), in_specs=_lru_in_specs(seq) + [col],
        out_specs=[col, col, pl.BlockSpec((4, LRU_HALF), lambda j, b: (0, j)), mat, mat,
                   pl.BlockSpec((8, LRU_HALF), lambda j, b: (0, j))],
        out_shape=[SDS((t, LRU_W), F32), SDS((t, LRU_W), F32), SDS((4, LRU_W), F32),
                   SDS((nhalf, LRU_HALF, LRU_HALF), F32), SDS((nhalf, LRU_HALF, LRU_HALF), F32), SDS((8, LRU_W), F32)],
        compiler_params=_params(("arbitrary", "arbitrary")))(u, u, cw, cb, wa, ba, wx, bx, lam, dy)


NEG = -1e30


def _rel_bucket_map():
    dist = (np.arange(BLOCK_Q)[:, None] - np.arange(BLOCK_Q)[None, :]) % BLOCK_Q
    max_exact = REL_BUCKETS // 2
    large = max_exact + (np.log(np.maximum(dist, 1).astype(np.float32) / max_exact)
                         / math.log(BLOCK_Q / max_exact) * (REL_BUCKETS - max_exact)).astype(np.int32)
    large = np.minimum(large, REL_BUCKETS - 1)
    return np.where(dist < max_exact, dist, large).astype(np.int32)


def relbias_fwd(rel_bias, bmap, *, name):
    def body(rb_ref, bm_ref, o_ref):
        bm = bm_ref[...]
        for h in range(ATT_HEADS):
            acc = jnp.zeros((BLOCK_Q, BLOCK_Q), F32)
            for b in range(REL_BUCKETS):
                acc = jnp.where(bm == b, rb_ref[b, h], acc)
            o_ref[h] = acc

    return pl.pallas_call(
        body, name=name, in_specs=[pl.BlockSpec(memory_space=pltpu.SMEM), pl.BlockSpec(memory_space=pltpu.VMEM)],
        out_specs=pl.BlockSpec(memory_space=pltpu.VMEM), out_shape=SDS((ATT_HEADS, BLOCK_Q, BLOCK_Q), F32))(rel_bias, bmap)


def relbias_bwd(dbias, bmap, *, name):
    def body(db_ref, bm_ref, o_ref):
        bm = bm_ref[...]
        row = lax.broadcasted_iota(jnp.int32, (REL_BUCKETS, 128), 0)
        col = lax.broadcasted_iota(jnp.int32, (REL_BUCKETS, 128), 1)
        acc = jnp.zeros((REL_BUCKETS, 128), F32)
        for h in range(ATT_HEADS):
            d = db_ref[h]
            for b in range(REL_BUCKETS):
                s = jnp.sum(jnp.sum(jnp.where(bm == b, d, 0.0), axis=1, keepdims=True), axis=0, keepdims=True)
                acc = jnp.where((row == b) & (col == h), s, acc)
        o_ref[...] = acc

    return pl.pallas_call(body, name=name, out_shape=SDS((REL_BUCKETS, 128), F32))(dbias, bmap)


def _attn_probs(q_ref, k_ref, v_ref, b_ref, s_ref, n):
    rows = ATT_GROUP * BLOCK_Q
    qs = q_ref[...].reshape(rows, HEAD) * (HEAD ** -0.5)
    prev = pl.multiple_of(jnp.maximum(n - 1, 0) * BLOCK_Q, BLOCK_Q)
    cur = pl.multiple_of(n * BLOCK_Q, BLOCK_Q)
    kp, kc = k_ref[pl.ds(prev, BLOCK_Q), :], k_ref[pl.ds(cur, BLOCK_Q), :]
    vp, vc = v_ref[pl.ds(prev, BLOCK_Q), :], v_ref[pl.ds(cur, BLOCK_Q), :]
    bias = b_ref[...].reshape(rows, BLOCK_Q)
    i = lax.broadcasted_iota(jnp.int32, (rows, BLOCK_Q), 0) & (BLOCK_Q - 1)
    j = lax.broadcasted_iota(jnp.int32, (rows, BLOCK_Q), 1)
    s_p = jnp.where((j > i) & (n > 0), _nt(qs, kp) + bias, NEG)
    s_c = jnp.where(j <= i, _nt(qs, kc) + bias, NEG)
    sink = s_ref[...]
    m = jnp.maximum(jnp.maximum(jnp.max(s_p, axis=-1, keepdims=True), jnp.max(s_c, axis=-1, keepdims=True)), sink)
    e_p, e_c, e_s = jnp.exp(s_p - m), jnp.exp(s_c - m), jnp.exp(sink - m)
    inv = 1.0 / (jnp.sum(e_p, axis=-1, keepdims=True) + jnp.sum(e_c, axis=-1, keepdims=True) + e_s)
    return e_p * inv, e_c * inv, e_s * inv, qs, kp, kc, vp, vc, prev, cur


def _attn_specs(seq):
    qspec = pl.BlockSpec((None, ATT_GROUP, BLOCK_Q, HEAD), lambda g, b, n: (b, g, n, 0))
    kvspec = pl.BlockSpec((None, None, seq, HEAD), lambda g, b, n: (b, g, 0, 0))
    bspec = pl.BlockSpec((ATT_GROUP, BLOCK_Q, BLOCK_Q), lambda g, b, n: (g, 0, 0))
    sspec = pl.BlockSpec((ATT_GROUP * BLOCK_Q, 1), lambda g, b, n: (g, 0))
    return qspec, kvspec, bspec, sspec


def attn_fwd(q, k, v, bias, sink_rows, *, name):
    nb, _, seq, _ = q.shape

    def body(q_ref, k_ref, v_ref, b_ref, s_ref, o_ref):
        p_p, p_c, _, _, _, _, vp, vc, _, _ = _attn_probs(q_ref, k_ref, v_ref, b_ref, s_ref, pl.program_id(2))
        o_ref[...] = (_nn(p_p, vp) + _nn(p_c, vc)).reshape(ATT_GROUP, BLOCK_Q, HEAD)

    qspec, kvspec, bspec, sspec = _attn_specs(seq)
    return pl.pallas_call(
        body, name=name, grid=(KV_HEADS, nb, seq // BLOCK_Q), in_specs=[qspec, kvspec, kvspec, bspec, sspec],
        out_specs=qspec, out_shape=SDS(q.shape, F32),
        compiler_params=_params(("parallel", "parallel", "arbitrary")))(q, k, v, bias, sink_rows)


def attn_bwd(q, k, v, bias, sink_rows, do, *, name):
    nb, _, seq, _ = q.shape

    def body(q_ref, k_ref, v_ref, b_ref, s_ref, do_ref, dq_ref, dk_ref, dv_ref, db_ref, ds_ref):
        b, n = pl.program_id(1), pl.program_id(2)

        @pl.when((b == 0) & (n == 0))
        def _():
            db_ref[...] = jnp.zeros_like(db_ref)
            ds_ref[...] = jnp.zeros_like(ds_ref)

        @pl.when(n == 0)
        def _():
            dk_ref[...] = jnp.zeros_like(dk_ref)
            dv_ref[...] = jnp.zeros_like(dv_ref)

        p_p, p_c, p_s, qs, kp, kc, vp, vc, prev, cur = _attn_probs(q_ref, k_ref, v_ref, b_ref, s_ref, n)
        do = do_ref[...].reshape(ATT_GROUP * BLOCK_Q, HEAD)
        dp_p, dp_c = _nt(do, vp), _nt(do, vc)
        delta = jnp.sum(p_p * dp_p, axis=-1, keepdims=True) + jnp.sum(p_c * dp_c, axis=-1, keepdims=True)
        ds_p, ds_c = p_p * (dp_p - delta), p_c * (dp_c - delta)
        dq_ref[...] = ((_nn(ds_p, kp) + _nn(ds_c, kc)) * (HEAD ** -0.5)).reshape(ATT_GROUP, BLOCK_Q, HEAD)
        dk_ref[pl.ds(prev, BLOCK_Q), :] += _tn(ds_p, qs)
        dk_ref[pl.ds(cur, BLOCK_Q), :] += _tn(ds_c, qs)
        dv_ref[pl.ds(prev, BLOCK_Q), :] += _tn(p_p, do)
        dv_ref[pl.ds(cur, BLOCK_Q), :] += _tn(p_c, do)
        db_ref[...] += (ds_p + ds_c).reshape(ATT_GROUP, BLOCK_Q, BLOCK_Q)
        ds_ref[...] += -p_s * delta

    qspec, kvspec, bspec, sspec = _attn_specs(seq)
    return pl.pallas_call(
        body, name=name, grid=(KV_HEADS, nb, seq // BLOCK_Q), in_specs=[qspec, kvspec, kvspec, bspec, sspec, qspec],
        out_specs=[qspec, kvspec, kvspec, bspec, sspec],
        out_shape=[SDS(q.shape, F32), SDS(k.shape, F32), SDS(v.shape, F32),
                   SDS((ATT_HEADS, BLOCK_Q, BLOCK_Q), F32), SDS((ATT_HEADS * BLOCK_Q, 1), F32)],
        compiler_params=_params(("arbitrary", "arbitrary", "arbitrary")))(q, k, v, bias, sink_rows, do)


def _iota2(shape, axis):
    return lax.broadcasted_iota(jnp.int32, shape, axis)


def _col_to_row(col):
    c = col.shape[0]
    eye = _iota2((c, c), 0) == _iota2((c, c), 1)
    return jnp.sum(jnp.where(eye, jnp.broadcast_to(col, (c, c)), 0.0), axis=0, keepdims=True)


def _row_to_col(row):
    c = row.shape[1]
    eye = _iota2((c, c), 0) == _iota2((c, c), 1)
    return jnp.sum(jnp.where(eye, jnp.broadcast_to(row, (c, c)), 0.0), axis=1, keepdims=True)


def _last_row(col):
    c = col.shape[0]
    return jnp.sum(jnp.where(_iota2((c, 1), 0) == c - 1, col, 0.0), axis=0, keepdims=True)


def _chunk_cumsum(x):
    pos = _iota2(x.shape, 0) & (DN_CHUNK - 1)
    d = 1
    while d < DN_CHUNK:
        x = x + jnp.where(pos >= d, pltpu.roll(x, d, 0), 0.0)
        d *= 2
    return x


def _chunk_rev_cumsum(x):
    n = x.shape[0]
    pos = _iota2(x.shape, 0) & (DN_CHUNK - 1)
    d = 1
    while d < DN_CHUNK:
        x = x + jnp.where(pos < DN_CHUNK - d, pltpu.roll(x, n - d, 0), 0.0)
        d *= 2
    return x


def _tri_inv(low):
    c = low.shape[0]
    eye = (_iota2((c, c), 0) == _iota2((c, c), 1)).astype(F32)
    m = -low
    p = eye + m
    steps = int(math.log2(c)) - 1
    for _ in range(steps):
        m = _nn(m, m, hi=True)
        p = p + _nn(p, m, hi=True)
    return p


_DN_SCALE = (HEAD ** -0.5, 1.0, None)


def _dn_act(c, scale):
    sig = _sigmoid(c)
    a = c * sig
    if scale is None:
        return a, sig, None, None
    r = lax.rsqrt(jnp.sum(a * a, axis=-1, keepdims=True) + EPS)
    return a * r * scale, sig, a * r, r


def _dn_gates(ba_ref, hs_ref):
    beta = _sigmoid(ba_ref[0])
    sp_arg = ba_ref[1] + hs_ref[1]
    a_exp = jnp.exp(hs_ref[0])
    g = -a_exp * _softplus(sp_arg)
    return beta, g, sp_arg, a_exp


def _dn_inputs(pre_ref, cw_ref, ba_ref, hs_ref, act_sc, b_sc, gc_sc, c_sc=None):
    for idx in range(3):
        c = _conv_fwd(pre_ref[idx], [cw_ref[idx, k:k + 1, :] for k in range(4)])
        if c_sc is not None:
            c_sc[idx] = c
        act_sc[idx] = _dn_act(c, _DN_SCALE[idx])[0]
    beta, g, _, _ = _dn_gates(ba_ref, hs_ref)
    b_sc[...] = beta
    gc_sc[...] = _chunk_cumsum(g)


def _dn_chunk_math(q, k, v, b, gcc):
    c = q.shape[0]
    tril = _iota2((c, c), 0) >= _iota2((c, c), 1)
    strict = _iota2((c, c), 0) > _iota2((c, c), 1)
    eg = jnp.exp(gcc)
    kb, vb = k * b, v * b
    kbg = kb * eg
    dm = jnp.exp(jnp.where(tril, jnp.broadcast_to(gcc, (c, c)) - _col_to_row(gcc), NEG))
    kk = _nt(kb, k)
    t = _tri_inv(jnp.where(strict, kk * dm, 0.0))
    glast = _last_row(gcc)
    ekd = jnp.exp(glast - gcc)
    qk = _nt(q, k)
    return dict(tril=tril, strict=strict, eg=eg, kb=kb, vb=vb, kbg=kbg, dm=dm, kk=kk, t=t, glast=glast, ekd=ekd,
                kd=k * ekd, qk=qk, amat=jnp.where(tril, qk * dm, 0.0), qg=q * eg,
                egl=jnp.broadcast_to(jnp.exp(glast), (c, 1)))


DN_UNROLL = 4


def _chunk_loop(nc, chunk):
    u = math.gcd(nc, DN_UNROLL)

    def step(i, carry):
        for j in range(u):
            chunk(i * u + j)
        return carry

    lax.fori_loop(0, nc // u, step, 0)


def _dn_specs(seq):
    s64 = lambda lead: pl.BlockSpec((lead, None, None, seq, HEAD), lambda b, h: (0, b, h, 0, 0))
    s1 = lambda lead: pl.BlockSpec((lead, None, None, seq, 1), lambda b, h: (0, b, h, 0, 0))
    one64 = pl.BlockSpec((None, None, seq, HEAD), lambda b, h: (b, h, 0, 0))
    one1 = pl.BlockSpec((None, None, seq, 1), lambda b, h: (b, h, 0, 0))
    cw = pl.BlockSpec((None, 3, 4, HEAD), lambda b, h: (h, 0, 0, 0))
    hs = pl.BlockSpec((None, 2, 1, 1), lambda b, h: (h, 0, 0, 0))
    return s64, s1, one64, one1, cw, hs


def dn_prep(pre, cw, ba, hs, *, name):
    _, nb, nh, seq, _ = pre.shape
    nc = seq // DN_CHUNK

    def body(pre_ref, cw_ref, ba_ref, hs_ref, loc_ref, egl_ref, act_sc, b_sc, gc_sc):
        _dn_inputs(pre_ref, cw_ref, ba_ref, hs_ref, act_sc, b_sc, gc_sc)

        def chunk(c):
            rows = pl.ds(pl.multiple_of(c * DN_CHUNK, DN_CHUNK), DN_CHUNK)
            m = _dn_chunk_math(act_sc[0, rows, :], act_sc[1, rows, :], act_sc[2, rows, :], b_sc[rows, :], gc_sc[rows, :])
            loc_ref[0, rows, :] = m["qg"]
            loc_ref[1, rows, :] = m["kd"]
            loc_ref[2, rows, :] = _nn(m["t"], m["vb"])
            loc_ref[3, rows, :] = _nn(m["t"], m["kbg"])
            loc_ref[4, rows, :] = m["amat"]
            egl_ref[rows, :] = m["egl"]

        _chunk_loop(nc, chunk)

    s64, s1, one64, one1, cwspec, hsspec = _dn_specs(seq)
    return pl.pallas_call(
        body, name=name, grid=(nb, nh), in_specs=[s64(3), cwspec, s1(2), hsspec], out_specs=[s64(5), one1],
        out_shape=[SDS((5, nb, nh, seq, HEAD), F32), SDS((nb, nh, seq, 1), F32)],
        scratch_shapes=[pltpu.VMEM((3, seq, HEAD), F32)] + [pltpu.VMEM((seq, 1), F32)] * 2,
        compiler_params=_params(("parallel", "parallel")))(pre, cw, ba, hs)


def _gated_norm(o, z, gn):
    r = lax.rsqrt(jnp.mean(o * o, axis=-1, keepdims=True) + EPS)
    sig = _sigmoid(z)
    return o * r, sig, r


def dn_scan(loc, egl, z, gn, *, name):
    _, nb, nh, seq, _ = loc.shape
    nc = seq // DN_CHUNK

    def body(loc_ref, egl_ref, z_ref, gn_ref, y_ref, o_ref, vn_ref, st_ref):
        gn = gn_ref[...]

        def step(c, state):
            rows = pl.ds(pl.multiple_of(c * DN_CHUNK, DN_CHUNK), DN_CHUNK)
            st_ref[rows, :] = state
            vn = loc_ref[2, rows, :] - _nn(loc_ref[3, rows, :], state)
            o = _nn(loc_ref[0, rows, :], state) + _nn(loc_ref[4, rows, :], vn)
            vn_ref[rows, :] = vn
            o_ref[rows, :] = o
            zz = z_ref[rows, :]
            on, sig, _ = _gated_norm(o, zz, gn)
            y_ref[rows, :] = on * gn * (zz * sig)
            return state * egl_ref[rows, :] + _tn(loc_ref[1, rows, :], vn)

        lax.fori_loop(0, nc, step, jnp.zeros((HEAD, HEAD), F32))

    s64, s1, one64, one1, cwspec, hsspec = _dn_specs(seq)
    out = SDS((nb, nh, seq, HEAD), F32)
    return pl.pallas_call(
        body, name=name, grid=(nb, nh), in_specs=[s64(5), one1, one64, _whole((1, HEAD))],
        out_specs=[one64] * 4, out_shape=[out] * 4,
        compiler_params=_params(("parallel", "parallel")))(loc, egl, z, gn)


def dn_scan_bwd(loc, egl, z, gn, o, vn, states, dy, *, name):
    _, nb, nh, seq, _ = loc.shape
    nc = seq // DN_CHUNK

    def body(loc_ref, egl_ref, z_ref, gn_ref, o_ref, vn_ref, st_ref, dy_ref, dloc_ref, degl_ref, dz_ref, dgn_ref):
        @pl.when((pl.program_id(0) == 0) & (pl.program_id(1) == 0))
        def _():
            dgn_ref[...] = jnp.zeros_like(dgn_ref)

        gn = gn_ref[...]
        tril = _iota2((DN_CHUNK, DN_CHUNK), 0) >= _iota2((DN_CHUNK, DN_CHUNK), 1)

        def step(i, carry):
            ds, dgn = carry
            rows = pl.ds(pl.multiple_of((nc - 1 - i) * DN_CHUNK, DN_CHUNK), DN_CHUNK)
            dy, zz, oo = dy_ref[rows, :], z_ref[rows, :], o_ref[rows, :]
            on, sig, r = _gated_norm(oo, zz, gn)
            sz = zz * sig
            dz_ref[rows, :] = dy * on * gn * (sig * (1.0 + zz * (1.0 - sig)))
            dgn = dgn + jnp.sum(dy * on * sz, axis=0, keepdims=True)
            don = dy * gn * sz
            do = r * (don - on * jnp.mean(don * on, axis=-1, keepdims=True))
            state, vnew = st_ref[rows, :], vn_ref[rows, :]
            qg, kd, w, amat = loc_ref[0, rows, :], loc_ref[1, rows, :], loc_ref[3, rows, :], loc_ref[4, rows, :]
            dvn = _tn(amat, do) + _nn(kd, ds)
            dloc_ref[0, rows, :] = _nt(do, state)
            dloc_ref[1, rows, :] = _nt(vnew, ds)
            dloc_ref[2, rows, :] = dvn
            dloc_ref[3, rows, :] = -_nt(dvn, state)
            dloc_ref[4, rows, :] = jnp.where(tril, _nt(do, vnew), 0.0)
            degl = jnp.sum(jnp.sum(state * ds, axis=1, keepdims=True), axis=0, keepdims=True)
            degl_ref[rows, :] = jnp.broadcast_to(degl, (DN_CHUNK, 1))
            return ds * egl_ref[rows, :] + _tn(qg, do) - _tn(w, dvn), dgn

        _, dgn = lax.fori_loop(0, nc, step, (jnp.zeros((HEAD, HEAD), F32), jnp.zeros((1, HEAD), F32)))
        dgn_ref[...] += dgn

    s64, s1, one64, one1, cwspec, hsspec = _dn_specs(seq)
    return pl.pallas_call(
        body, name=name, grid=(nb, nh),
        in_specs=[s64(5), one1, one64, _whole((1, HEAD)), one64, one64, one64, one64],
        out_specs=[s64(5), one1, one64, _whole((1, HEAD))],
        out_shape=[SDS((5, nb, nh, seq, HEAD), F32), SDS((nb, nh, seq, 1), F32), SDS((nb, nh, seq, HEAD), F32),
                   SDS((1, HEAD), F32)],
        compiler_params=_params(("arbitrary", "arbitrary")))(loc, egl, z, gn, o, vn, states, dy)


def dn_prep_bwd(pre, cw, ba, hs, dloc, degl, *, name):
    _, nb, nh, seq, _ = pre.shape
    nc = seq // DN_CHUNK

    def body(pre_ref, cw_ref, ba_ref, hs_ref, dloc_ref, degl_ref, dpre_ref, dba_ref, dcw_ref, dhs_ref,
             act_sc, b_sc, gc_sc, c_sc):
        @pl.when(pl.program_id(1) == 0)
        def _():
            dcw_ref[...] = jnp.zeros_like(dcw_ref)
            dhs_ref[...] = jnp.zeros_like(dhs_ref)

        _dn_inputs(pre_ref, cw_ref, ba_ref, hs_ref, act_sc, b_sc, gc_sc, c_sc)

        def chunk(c):
            rows = pl.ds(pl.multiple_of(c * DN_CHUNK, DN_CHUNK), DN_CHUNK)
            q, k, v, b, gcc = act_sc[0, rows, :], act_sc[1, rows, :], act_sc[2, rows, :], b_sc[rows, :], gc_sc[rows, :]
            m = _dn_chunk_math(q, k, v, b, gcc)
            dqg, dkd, du, dw, da = (dloc_ref[x, rows, :] for x in range(5))
            t, dm, eg = m["t"], m["dm"], m["eg"]
            dt = _nt(du, m["vb"]) + _nt(dw, m["kbg"])
            dvb, dkbg = _tn(t, du), _tn(t, dw)
            dl = jnp.where(m["strict"], -_tn(t, _nt(dt, t, hi=True), hi=True), 0.0)
            dkk = dl * dm
            dqk = da * dm
            dd = dl * m["kk"] + da * m["qk"]
            dkb = _nn(dkk, k) + dkbg * eg
            dq = _nn(dqk, k) + dqg * eg
            dk = _tn(dkk, m["kb"]) + _tn(dqk, q) + dkd * m["ekd"] + dkb * b
            db = jnp.sum(dkb * k, axis=-1, keepdims=True) + jnp.sum(dvb * v, axis=-1, keepdims=True)
            mx = jnp.where(m["tril"], dd * dm, 0.0)
            tk = jnp.sum(dkd * m["kd"], axis=-1, keepdims=True)
            dgc = (jnp.sum(mx, axis=-1, keepdims=True) - _row_to_col(jnp.sum(mx, axis=0, keepdims=True))
                   + jnp.sum(dqg * m["qg"], axis=-1, keepdims=True) + jnp.sum(dkbg * m["kbg"], axis=-1, keepdims=True) - tk)
            dglast = jnp.sum(tk, axis=0, keepdims=True) + _last_row(degl_ref[rows, :]) * jnp.exp(m["glast"])
            act_sc[0, rows, :] = dq
            act_sc[1, rows, :] = dk
            act_sc[2, rows, :] = dvb * b
            b_sc[rows, :] = db
            gc_sc[rows, :] = dgc + jnp.where(_iota2((DN_CHUNK, 1), 0) == DN_CHUNK - 1, dglast, 0.0)

        _chunk_loop(nc, chunk)

        beta, g, sp_arg, a_exp = _dn_gates(ba_ref, hs_ref)
        dg = _chunk_rev_cumsum(gc_sc[...])
        dal = dg * (-a_exp) * _sigmoid(sp_arg)
        dba_ref[0] = b_sc[...] * beta * (1.0 - beta)
        dba_ref[1] = dal
        dhs_ref[0] += jnp.sum(dg * g, axis=0, keepdims=True)
        dhs_ref[1] += jnp.sum(dal, axis=0, keepdims=True)
        for idx in range(3):
            c = c_sc[idx]
            _, sig, hat, r = _dn_act(c, _DN_SCALE[idx])
            da_ = act_sc[idx]
            if _DN_SCALE[idx] is not None:
                da_ = da_ * _DN_SCALE[idx]
                da_ = r * (da_ - hat * jnp.sum(da_ * hat, axis=-1, keepdims=True))
            dx, dcw = _conv_bwd(da_ * (sig * (1.0 + c * (1.0 - sig))), pre_ref[idx],
                                [cw_ref[idx, k:k + 1, :] for k in range(4)])
            dpre_ref[idx] = dx
            dcw_ref[idx] += dcw

    s64, s1, one64, one1, cwspec, hsspec = _dn_specs(seq)
    swap = lambda spec: pl.BlockSpec(spec.block_shape, lambda h, b, _f=spec.index_map: _f(b, h))
    return pl.pallas_call(
        body, name=name, grid=(nh, nb),
        in_specs=[swap(s64(3)), swap(cwspec), swap(s1(2)), swap(hsspec), swap(s64(5)), swap(one1)],
        out_specs=[swap(s64(3)), swap(s1(2)), swap(cwspec), swap(hsspec)],
        out_shape=[SDS((3, nb, nh, seq, HEAD), F32), SDS((2, nb, nh, seq, 1), F32), SDS((nh, 3, 4, HEAD), F32),
                   SDS((nh, 2, 1, 1), F32)],
        scratch_shapes=[pltpu.VMEM((3, seq, HEAD), F32)] + [pltpu.VMEM((seq, 1), F32)] * 2 + [pltpu.VMEM((3, seq, HEAD), F32)],
        compiler_params=_params(("arbitrary", "arbitrary")))(pre, cw, ba, hs, dloc, degl)


COL_Q, COL_K, COL_V = 512 // 128, 1024 // 128, 1152 // 128
COL_DNQ, COL_DNK, COL_DNV, COL_DNZ, COL_BA = 1280 // 128, 1536 // 128, 1792 // 128, 2048 // 128, 2304 // 128


def _lane_a(shape):
    return _iota2(shape, 1) < HEAD


def _bd(x):
    la = _lane_a(x.shape)
    return jnp.concatenate([jnp.where(la, x, 0.0), jnp.where(la, 0.0, x)], axis=0)


def _fold(m):
    return m[:HEAD] + m[HEAD:]


def _bd_mask():
    return (_iota2((2 * HEAD, 2 * HEAD), 0) < HEAD) == (_iota2((2 * HEAD, 2 * HEAD), 1) < HEAD)


def _pk_nn(x, y, hi=False):
    return _nn(x, _bd(y), hi)


def _pk_nt(u, v, hi=False):
    return _nt(u, _bd(v), hi)


def _pk_tn(x, y, hi=False):
    return _fold(jnp.where(_bd_mask(), _tn(x, y, hi), 0.0))


def _half_sum(x):
    la = _lane_a(x.shape)
    return jnp.where(la, jnp.sum(jnp.where(la, x, 0.0), axis=-1, keepdims=True),
                     jnp.sum(jnp.where(la, 0.0, x), axis=-1, keepdims=True))


def _lane_col(x, idx):
    return jnp.sum(jnp.where(_iota2(x.shape, 1) == idx, x, 0.0), axis=-1, keepdims=True)


def _row0(x):
    return jnp.max(x, axis=0, keepdims=True)


def _dup_kv(x, g):
    la = _lane_a(x.shape)
    rolled = pltpu.roll(x, HEAD, 1)
    return jnp.where(la, x, rolled) if g == 0 else jnp.where(la, rolled, x)


def _stack_heads(ref, g):
    la = _lane_a((BLOCK_Q, 2 * HEAD))
    parts = []
    for hh in range(ATT_GROUP):
        pair = ref[:, pl.ds(2 * HEAD * (2 * g + hh // 2), 2 * HEAD)]
        parts.append(jnp.where(la if hh % 2 == 0 else ~la, pair, 0.0))
    return jnp.concatenate(parts, axis=0)


def _unstack_heads(stack, ref, g):
    la = _lane_a((BLOCK_Q, 2 * HEAD))
    for j in range(2):
        top = stack[2 * j * BLOCK_Q:(2 * j + 1) * BLOCK_Q]
        bot = stack[(2 * j + 1) * BLOCK_Q:(2 * j + 2) * BLOCK_Q]
        ref[:, pl.ds(2 * HEAD * (2 * g + j), 2 * HEAD)] = jnp.where(la, top, bot)


def _swa_probs(q_ref, k_ref, v_ref, b_ref, s_ref, n, g):
    rows = ATT_GROUP * BLOCK_Q
    prev = pl.multiple_of(jnp.maximum(n - 1, 0) * BLOCK_Q, BLOCK_Q)
    cur = pl.multiple_of(n * BLOCK_Q, BLOCK_Q)
    kp, kc = _dup_kv(k_ref[pl.ds(prev, BLOCK_Q), :], g), _dup_kv(k_ref[pl.ds(cur, BLOCK_Q), :], g)
    vp, vc = _dup_kv(v_ref[pl.ds(prev, BLOCK_Q), :], g), _dup_kv(v_ref[pl.ds(cur, BLOCK_Q), :], g)
    qs = _stack_heads(q_ref, g) * (HEAD ** -0.5)
    bias = b_ref[pl.ds(ATT_GROUP * g, ATT_GROUP)].reshape(rows, BLOCK_Q)
    i = _iota2((rows, BLOCK_Q), 0) & (BLOCK_Q - 1)
    j = _iota2((rows, BLOCK_Q), 1)
    s_p = jnp.where((j > i) & (n > 0), _nt(qs, kp) + bias, NEG)
    s_c = jnp.where(j <= i, _nt(qs, kc) + bias, NEG)
    sink = s_ref[pl.ds(rows * g, rows), :]
    m = jnp.maximum(jnp.maximum(jnp.max(s_p, axis=-1, keepdims=True), jnp.max(s_c, axis=-1, keepdims=True)), sink)
    e_p, e_c, e_s = jnp.exp(s_p - m), jnp.exp(s_c - m), jnp.exp(sink - m)
    inv = 1.0 / (jnp.sum(e_p, axis=-1, keepdims=True) + jnp.sum(e_c, axis=-1, keepdims=True) + e_s)
    return e_p * inv, e_c * inv, e_s * inv, qs, kp, kc, vp, vc, prev, cur


def _swa_specs(seq):
    nblk = seq // BLOCK_Q
    qspec = pl.BlockSpec((BLOCK_Q, ATT_W), lambda b, n: (b * nblk + n, COL_Q * 128 // ATT_W))
    kspec = pl.BlockSpec((seq, 2 * HEAD), lambda b, n: (b, COL_K))
    vspec = pl.BlockSpec((seq, 2 * HEAD), lambda b, n: (b, COL_V))
    ospec = pl.BlockSpec((BLOCK_Q, ATT_W), lambda b, n: (b * nblk + n, 0))
    kvout = pl.BlockSpec((seq, 2 * HEAD), lambda b, n: (b, 0))
    return qspec, kspec, vspec, ospec, kvout, _whole((ATT_HEADS, BLOCK_Q, BLOCK_Q)), _whole((ATT_HEADS * BLOCK_Q, 1))


def swa_fwd(u, bias, sink_rows, *, seq, name):
    t = u.shape[0]

    def body(q_ref, k_ref, v_ref, b_ref, s_ref, o_ref):
        for g in range(KV_HEADS):
            p_p, p_c, _, _, _, _, vp, vc, _, _ = _swa_probs(q_ref, k_ref, v_ref, b_ref, s_ref, pl.program_id(1), g)
            _unstack_heads(_nn(p_p, vp) + _nn(p_c, vc), o_ref, g)

    qspec, kspec, vspec, ospec, kvout, bspec, sspec = _swa_specs(seq)
    return pl.pallas_call(
        body, name=name, grid=(t // seq, seq // BLOCK_Q), in_specs=[qspec, kspec, vspec, bspec, sspec], out_specs=ospec,
        out_shape=SDS((t, ATT_W), F32), compiler_params=_params(("parallel", "arbitrary")))(u, u, u, bias, sink_rows)


def swa_bwd(u, bias, sink_rows, do, *, seq, name):
    t = u.shape[0]

    def body(q_ref, k_ref, v_ref, b_ref, s_ref, do_ref, dq_ref, dk_ref, dv_ref, db_ref, ds_ref):
        b, n = pl.program_id(0), pl.program_id(1)

        @pl.when((b == 0) & (n == 0))
        def _():
            db_ref[...] = jnp.zeros_like(db_ref)
            ds_ref[...] = jnp.zeros_like(ds_ref)

        @pl.when(n == 0)
        def _():
            dk_ref[...] = jnp.zeros_like(dk_ref)
            dv_ref[...] = jnp.zeros_like(dv_ref)

        la = _lane_a((BLOCK_Q, 2 * HEAD))
        for g in range(KV_HEADS):
            p_p, p_c, p_s, qs, kp, kc, vp, vc, prev, cur = _swa_probs(q_ref, k_ref, v_ref, b_ref, s_ref, n, g)
            do = _stack_heads(do_ref, g)
            dp_p, dp_c = _nt(do, vp), _nt(do, vc)
            delta = jnp.sum(p_p * dp_p, axis=-1, keepdims=True) + jnp.sum(p_c * dp_c, axis=-1, keepdims=True)
            ds_p, ds_c = p_p * (dp_p - delta), p_c * (dp_c - delta)
            _unstack_heads((_nn(ds_p, kp) + _nn(ds_c, kc)) * (HEAD ** -0.5), dq_ref, g)
            mine = la if g == 0 else ~la

            def to_head(x):
                return jnp.where(mine, x + pltpu.roll(x, HEAD, 1), 0.0)

            dk_ref[pl.ds(prev, BLOCK_Q), :] += to_head(_tn(ds_p, qs))
            dk_ref[pl.ds(cur, BLOCK_Q), :] += to_head(_tn(ds_c, qs))
            dv_ref[pl.ds(prev, BLOCK_Q), :] += to_head(_tn(p_p, do))
            dv_ref[pl.ds(cur, BLOCK_Q), :] += to_head(_tn(p_c, do))
            db_ref[pl.ds(ATT_GROUP * g, ATT_GROUP)] += (ds_p + ds_c).reshape(ATT_GROUP, BLOCK_Q, BLOCK_Q)
            rows = ATT_GROUP * BLOCK_Q
            ds_ref[pl.ds(rows * g, rows), :] += -p_s * delta

    qspec, kspec, vspec, ospec, kvout, bspec, sspec = _swa_specs(seq)
    return pl.pallas_call(
        body, name=name, grid=(t // seq, seq // BLOCK_Q), in_specs=[qspec, kspec, vspec, bspec, sspec, ospec],
        out_specs=[ospec, kvout, kvout, bspec, sspec],
        out_shape=[SDS((t, ATT_W), F32), SDS((t, 2 * HEAD), F32), SDS((t, 2 * HEAD), F32),
                   SDS((ATT_HEADS, BLOCK_Q, BLOCK_Q), F32), SDS((ATT_HEADS * BLOCK_Q, 1), F32)],
        compiler_params=_params(("arbitrary", "arbitrary")))(u, u, u, bias, sink_rows, do)


def _gdn_gates(ba_ref, alog_ref, dt_ref, hp):
    blk = ba_ref[...]
    beta_blk = _sigmoid(blk)
    sp_arg = blk + dt_ref[...]
    a_exp = jnp.exp(alog_ref[...])
    g_blk = -a_exp * _softplus(sp_arg)
    la = _lane_a(blk.shape)
    ha = 2 * hp
    beta = jnp.where(la, _lane_col(beta_blk, ha), _lane_col(beta_blk, ha + 1))
    g = jnp.where(la, _lane_col(g_blk, DN_HEADS + ha), _lane_col(g_blk, DN_HEADS + ha + 1))
    return beta, g, beta_blk, sp_arg, a_exp, g_blk


def _gdn_act(c, scale):
    sig = _sigmoid(c)
    a = c * sig
    if scale is None:
        return a, sig, None, None
    r = lax.rsqrt(_half_sum(a * a) + EPS)
    return a * r * scale, sig, a * r, r


def _gdn_inputs(pre_refs, cw_refs, ba_ref, alog_ref, dt_ref, hp, act_sc, b_sc, gc_sc, c_sc=None):
    for idx in range(3):
        c = _conv_fwd(pre_refs[idx][...], [cw_refs[idx][k:k + 1, :] for k in range(4)])
        if c_sc is not None:
            c_sc[idx] = c
        act_sc[idx] = _gdn_act(c, _DN_SCALE[idx])[0]
    beta, g = _gdn_gates(ba_ref, alog_ref, dt_ref, hp)[:2]
    b_sc[...] = beta
    gc_sc[...] = _chunk_cumsum(g)


def _gdn_chunk(q, k, v, b, gcc):
    shape = q.shape
    row, lm = _iota2(shape, 0), _iota2(shape, 1) & (HEAD - 1)
    tril, strict, eye = row >= lm, row > lm, row == lm
    eg = jnp.exp(gcc)
    kb, vb = k * b, v * b
    kbg = kb * eg
    grow = jnp.sum(jnp.where(eye, gcc, 0.0), axis=0, keepdims=True)
    dm = jnp.exp(jnp.where(tril, gcc - grow, NEG))
    kk = _pk_nt(kb, k)
    glast = jnp.sum(jnp.where(row == DN_CHUNK - 1, gcc, 0.0), axis=0, keepdims=True)
    ekd = jnp.exp(glast - gcc)
    qk = _pk_nt(q, k)
    return dict(q=q, k=k, v=v, b=b, tril=tril, strict=strict, eye=eye, row=row, eg=eg, kb=kb, vb=vb, kbg=kbg, dm=dm, kk=kk,
                low=jnp.where(strict, kk * dm, 0.0), glast=glast, ekd=ekd, kd=k * ekd, qk=qk,
                amat=jnp.where(tril, qk * dm, 0.0), qg=q * eg, egl=jnp.broadcast_to(jnp.exp(glast), shape))


def _tri_inv_many(chunks):
    ms = [-m["low"] for m in chunks]
    ts = [m["eye"].astype(F32) + x for m, x in zip(chunks, ms)]
    for _ in range(int(math.log2(HEAD)) - 1):
        ms = [_pk_nn(x, x, hi=True) for x in ms]
        ts = [t + _pk_nn(t, x, hi=True) for t, x in zip(ts, ms)]
    return ts


def _gdn_chunk_loop(nc, act_sc, b_sc, gc_sc, finish):
    u = math.gcd(nc, DN_UNROLL)

    def step(i, carry):
        rows = [pl.ds(pl.multiple_of((i * u + j) * DN_CHUNK, DN_CHUNK), DN_CHUNK) for j in range(u)]
        chunks = [_gdn_chunk(act_sc[0, r, :], act_sc[1, r, :], act_sc[2, r, :], b_sc[r, :], gc_sc[r, :]) for r in rows]
        pending = [finish(r, m, t) for r, m, t in zip(rows, chunks, _tri_inv_many(chunks))]
        pending = [g for g in pending if g is not None]
        while pending:
            for g in list(pending):
                if next(g, StopIteration) is StopIteration:
                    pending.remove(g)
        return carry

    lax.fori_loop(0, nc // u, step, 0)


def _gdn_in_specs(seq):
    u_at = lambda col: pl.BlockSpec((seq, 2 * HEAD), lambda b, hp, _c=col: (b, _c + hp))
    cw_at = lambda col: pl.BlockSpec((4, 2 * HEAD), lambda b, hp, _c=col: (0, _c + hp))
    row = pl.BlockSpec((1, 2 * HEAD), lambda b, hp: (0, 0))
    ba = pl.BlockSpec((seq, 2 * HEAD), lambda b, hp: (b, COL_BA))
    return [u_at(COL_DNQ), u_at(COL_DNK), u_at(COL_DNV), ba, cw_at(0), cw_at(2), cw_at(4), row, row]


def _pair(seq, lead=None):
    if lead is None:
        return pl.BlockSpec((seq, 2 * HEAD), lambda b, hp: (b, hp))
    return pl.BlockSpec((lead, seq, 2 * HEAD), lambda b, hp: (0, b, hp))


def _swap(spec):
    return pl.BlockSpec(spec.block_shape, lambda hp, b, _f=spec.index_map: _f(b, hp))


def gdn_prep(u, cw, alog_row, dt_row, *, seq, name):
    t = u.shape[0]
    nc = seq // DN_CHUNK

    def body(q_ref, k_ref, v_ref, ba_ref, cq_ref, ck_ref, cv_ref, alog_ref, dt_ref, loc_ref, egl_ref, act_sc, b_sc, gc_sc):
        _gdn_inputs((q_ref, k_ref, v_ref), (cq_ref, ck_ref, cv_ref), ba_ref, alog_ref, dt_ref, pl.program_id(1),
                    act_sc, b_sc, gc_sc)

        def finish(rows, m, t):
            loc_ref[0, rows, :] = m["qg"]
            loc_ref[1, rows, :] = m["kd"]
            loc_ref[2, rows, :] = _pk_nn(t, m["vb"])
            loc_ref[3, rows, :] = _pk_nn(t, m["kbg"])
            loc_ref[4, rows, :] = m["amat"]
            egl_ref[rows, :] = m["egl"]

        _gdn_chunk_loop(nc, act_sc, b_sc, gc_sc, finish)

    return pl.pallas_call(
        body, name=name, grid=(t // seq, DN_HEADS // 2), in_specs=_gdn_in_specs(seq), out_specs=[_pair(seq, 5), _pair(seq)],
        out_shape=[SDS((5, t, DN_HEADS * HEAD), F32), SDS((t, DN_HEADS * HEAD), F32)],
        scratch_shapes=[pltpu.VMEM((3, seq, 2 * HEAD), F32)] + [pltpu.VMEM((seq, 2 * HEAD), F32)] * 2,
        compiler_params=_params(("parallel", "parallel")))(u, u, u, u, cw, cw, cw, alog_row, dt_row)


def _gated_norm2(o, z, gn):
    r = lax.rsqrt(_half_sum(o * o) * (1.0 / HEAD) + EPS)
    return o * r, _sigmoid(z), r


def gdn_scan(loc, egl, u, gn, *, seq, name):
    t = u.shape[0]
    nc = seq // DN_CHUNK

    def body(loc_ref, egl_ref, z_ref, gn_ref, y_ref, o_ref, vn_ref, st_ref):
        gn = gn_ref[...]
        bdm = _bd_mask()

        def step(c, state):
            rows = pl.ds(pl.multiple_of(c * DN_CHUNK, DN_CHUNK), DN_CHUNK)
            st_ref[rows, :] = _fold(state)
            vn = loc_ref[2, rows, :] - _nn(loc_ref[3, rows, :], state)
            o = _nn(loc_ref[0, rows, :], state) + _pk_nn(loc_ref[4, rows, :], vn)
            vn_ref[rows, :] = vn
            o_ref[rows, :] = o
            zz = z_ref[rows, :]
            on, sig, _ = _gated_norm2(o, zz, gn)
            y_ref[rows, :] = on * gn * (zz * sig)
            return state * _row0(egl_ref[rows, :]) + jnp.where(bdm, _tn(loc_ref[1, rows, :], vn), 0.0)

        lax.fori_loop(0, nc, step, jnp.zeros((2 * HEAD, 2 * HEAD), F32))

    zspec = pl.BlockSpec((seq, 2 * HEAD), lambda b, hp: (b, COL_DNZ + hp))
    out = SDS((t, DN_HEADS * HEAD), F32)
    return pl.pallas_call(
        body, name=name, grid=(t // seq, DN_HEADS // 2), in_specs=[_pair(seq, 5), _pair(seq), zspec, _whole((1, 2 * HEAD))],
        out_specs=[_pair(seq)] * 4, out_shape=[out] * 4,
        compiler_params=_params(("parallel", "parallel")))(loc, egl, u, gn)


def gdn_scan_bwd(loc, egl, u, gn, o, vn, states, dy, *, seq, name):
    t = u.shape[0]
    nc = seq // DN_CHUNK

    def body(loc_ref, egl_ref, z_ref, gn_ref, o_ref, vn_ref, st_ref, dy_ref, dloc_ref, degl_ref, dz_ref, dgn_ref):
        @pl.when((pl.program_id(0) == 0) & (pl.program_id(1) == 0))
        def _():
            dgn_ref[...] = jnp.zeros_like(dgn_ref)

        gn = gn_ref[...]
        bdm = _bd_mask()
        shape = (DN_CHUNK, 2 * HEAD)
        tril = _iota2(shape, 0) >= (_iota2(shape, 1) & (HEAD - 1))

        def step(i, carry):
            ds, dgn = carry
            rows = pl.ds(pl.multiple_of((nc - 1 - i) * DN_CHUNK, DN_CHUNK), DN_CHUNK)
            dy, zz, oo = dy_ref[rows, :], z_ref[rows, :], o_ref[rows, :]
            on, sig, r = _gated_norm2(oo, zz, gn)
            sz = zz * sig
            dz_ref[rows, :] = dy * on * gn * (sig * (1.0 + zz * (1.0 - sig)))
            dgn = dgn + jnp.sum(dy * on * sz, axis=0, keepdims=True)
            don = dy * gn * sz
            do = r * (don - on * _half_sum(don * on) * (1.0 / HEAD))
            state, vnew = _bd(st_ref[rows, :]), vn_ref[rows, :]
            qg, kd, w, amat = loc_ref[0, rows, :], loc_ref[1, rows, :], loc_ref[3, rows, :], loc_ref[4, rows, :]
            dvn = _pk_tn(amat, do) + _nn(kd, ds)
            dloc_ref[0, rows, :] = _nt(do, state)
            dloc_ref[1, rows, :] = _nt(vnew, ds)
            dloc_ref[2, rows, :] = dvn
            dloc_ref[3, rows, :] = -_nt(dvn, state)
            dloc_ref[4, rows, :] = jnp.where(tril, _pk_nt(do, vnew), 0.0)
            degl = _half_sum(jnp.sum(state * ds, axis=0, keepdims=True))
            degl_ref[rows, :] = jnp.broadcast_to(degl, shape)
            grow = jnp.where(bdm, _tn(qg, do) - _tn(w, dvn), 0.0)
            return ds * _row0(egl_ref[rows, :]) + grow, dgn

        _, dgn = lax.fori_loop(0, nc, step, (jnp.zeros((2 * HEAD, 2 * HEAD), F32), jnp.zeros((1, 2 * HEAD), F32)))
        dgn_ref[...] += dgn

    zspec = pl.BlockSpec((seq, 2 * HEAD), lambda b, hp: (b, COL_DNZ + hp))
    one = _pair(seq)
    out = SDS((t, DN_HEADS * HEAD), F32)
    return pl.pallas_call(
        body, name=name, grid=(t // seq, DN_HEADS // 2),
        in_specs=[_pair(seq, 5), one, zspec, _whole((1, 2 * HEAD)), one, one, one, one],
        out_specs=[_pair(seq, 5), one, one, _whole((1, 2 * HEAD))],
        out_shape=[SDS((5, t, DN_HEADS * HEAD), F32), out, out, SDS((1, 2 * HEAD), F32)],
        compiler_params=_params(("arbitrary", "arbitrary")))(loc, egl, u, gn, o, vn, states, dy)


def gdn_prep_bwd(u, cw, alog_row, dt_row, dloc, degl, *, seq, name):
    t = u.shape[0]
    nc = seq // DN_CHUNK

    def body(q_ref, k_ref, v_ref, ba_ref, cq_ref, ck_ref, cv_ref, alog_ref, dt_ref, dloc_ref, degl_ref,
             dqkv_ref, dba_ref, dcw_ref, dhs_ref, act_sc, b_sc, gc_sc, c_sc):
        hp = pl.program_id(0)

        @pl.when(pl.program_id(1) == 0)
        def _():
            dcw_ref[...] = jnp.zeros_like(dcw_ref)
            dhs_ref[...] = jnp.zeros_like(dhs_ref)

        pre_refs, cw_refs = (q_ref, k_ref, v_ref), (cq_ref, ck_ref, cv_ref)
        _gdn_inputs(pre_refs, cw_refs, ba_ref, alog_ref, dt_ref, hp, act_sc, b_sc, gc_sc, c_sc)

        def finish(rows, m, tt):
            q, k, v, b = m["q"], m["k"], m["v"], m["b"]
            dqg, dkd, du, dw, da = (dloc_ref[x, rows, :] for x in range(5))
            dm, eg = m["dm"], m["eg"]
            dt = _pk_nt(du, m["vb"]) + _pk_nt(dw, m["kbg"])
            dvb, dkbg = _pk_tn(tt, du), _pk_tn(tt, dw)
            yield
            dtt = _pk_nt(dt, tt, hi=True)
            yield
            dl = jnp.where(m["strict"], -_pk_tn(tt, dtt, hi=True), 0.0)
            yield
            dkk = dl * dm
            dqk = da * dm
            dd = dl * m["kk"] + da * m["qk"]
            dkb = _pk_nn(dkk, k) + dkbg * eg
            dq = _pk_nn(dqk, k) + dqg * eg
            yield
            dk = _pk_tn(dkk, m["kb"]) + _pk_tn(dqk, q) + dkd * m["ekd"] + dkb * b
            db = _half_sum(dkb * k + dvb * v)
            yield
            mx = jnp.where(m["tril"], dd * dm, 0.0)
            tk = _half_sum(dkd * m["kd"])
            colsum = jnp.where(m["eye"], jnp.broadcast_to(jnp.sum(mx, axis=0, keepdims=True), mx.shape), 0.0)
            dgc = _half_sum(mx) - _half_sum(colsum) + _half_sum(dqg * m["qg"] + dkbg * m["kbg"]) - tk
            dglast = jnp.sum(tk, axis=0, keepdims=True) + _row0(degl_ref[rows, :]) * jnp.exp(m["glast"])
            act_sc[0, rows, :] = dq
            act_sc[1, rows, :] = dk
            act_sc[2, rows, :] = dvb * b
            b_sc[rows, :] = db
            gc_sc[rows, :] = dgc + jnp.where(m["row"] == DN_CHUNK - 1, dglast, 0.0)

        _gdn_chunk_loop(nc, act_sc, b_sc, gc_sc, finish)

        beta, g, beta_blk, sp_arg, a_exp, g_blk = _gdn_gates(ba_ref, alog_ref, dt_ref, hp)
        dg = _chunk_rev_cumsum(gc_sc[...])
        lane = _iota2(beta_blk.shape, 1)
        ha = 2 * hp
        db = b_sc[...]
        at = lambda idx, x_a, x_b: (jnp.where(lane == idx, _lane_col(x_a, 0), 0.0)
                                    + jnp.where(lane == idx + 1, _lane_col(x_b, HEAD), 0.0))
        dg_blk = at(DN_HEADS + ha, dg, dg)
        dal = dg_blk * (-a_exp) * _sigmoid(sp_arg)
        dba_ref[...] = at(ha, db, db) * beta_blk * (1.0 - beta_blk) + dal
        dhs_ref[0:1, :] += jnp.sum(dg_blk * g_blk, axis=0, keepdims=True)
        dhs_ref[1:2, :] += jnp.sum(dal, axis=0, keepdims=True)
        for idx in range(3):
            c = c_sc[idx]
            _, sig, hat, r = _gdn_act(c, _DN_SCALE[idx])
            da_ = act_sc[idx]
            if _DN_SCALE[idx] is not None:
                da_ = da_ * _DN_SCALE[idx]
                da_ = r * (da_ - hat * _half_sum(da_ * hat))
            dx, dcw = _conv_bwd(da_ * (sig * (1.0 + c * (1.0 - sig))), pre_refs[idx][...],
                                [cw_refs[idx][k:k + 1, :] for k in range(4)])
            dqkv_ref[idx] = dx
            dcw_ref[idx] += dcw

    pair = DN_HEADS // 2
    in_specs = [_swap(s) for s in _gdn_in_specs(seq)] + [_swap(_pair(seq, 5)), _swap(_pair(seq))]
    return pl.pallas_call(
        body, name=name, grid=(pair, t // seq), in_specs=in_specs,
        out_specs=[_swap(_pair(seq, 3)), pl.BlockSpec((None, seq, 2 * HEAD), lambda hp, b: (hp, b, 0)),
                   pl.BlockSpec((3, 4, 2 * HEAD), lambda hp, b: (0, 0, hp)),
                   pl.BlockSpec((None, 2, 2 * HEAD), lambda hp, b: (hp, 0, 0))],
        out_shape=[SDS((3, t, DN_HEADS * HEAD), F32), SDS((pair, t, 2 * HEAD), F32), SDS((3, 4, DN_HEADS * HEAD), F32),
                   SDS((pair, 2, 2 * HEAD), F32)],
        scratch_shapes=[pltpu.VMEM((3, seq, 2 * HEAD), F32)] + [pltpu.VMEM((seq, 2 * HEAD), F32)] * 2
        + [pltpu.VMEM((3, seq, 2 * HEAD), F32)],
        compiler_params=_params(("arbitrary", "arbitrary")))(u, u, u, u, cw, cw, cw, alog_row, dt_row, dloc, degl)


def mix_out(y_lru, o, y_dn, w_out, h, *, name, tm=512):
    t, d = h.shape
    tm = min(tm, t)

    def body(a_ref, b_ref, c_ref, w_ref, h_ref, o_ref, y_ref):
        y_ref[:, 0:LRU_W] = a_ref[...].astype(BF16)
        y_ref[:, LRU_W:LRU_W + ATT_W] = b_ref[...].astype(BF16)
        y_ref[:, LRU_W + ATT_W:] = c_ref[...].astype(BF16)
        o_ref[...] = h_ref[...] + _nn(y_ref[...], w_ref[...])

    rows = lambda width: pl.BlockSpec((tm, width), lambda i: (i, 0))
    return pl.pallas_call(
        body, name=name, grid=(t // tm,), in_specs=[rows(LRU_W), rows(ATT_W), rows(LRU_W), _whole((d, d)), rows(d)],
        out_specs=[rows(d), rows(d)], out_shape=[SDS((t, d), F32), SDS((t, d), BF16)],
        compiler_params=_params(("parallel",)))(y_lru, o, y_dn, w_out, h)


def mix_out_bwd(dout, w_out, *, name, tm=512):
    t, d = dout.shape
    tm = min(tm, t)

    def body(d_ref, w_ref, a_ref, b_ref, c_ref):
        dy = _nt(d_ref[...], w_ref[...])
        a_ref[...] = dy[:, 0:LRU_W]
        b_ref[...] = dy[:, LRU_W:LRU_W + ATT_W]
        c_ref[...] = dy[:, LRU_W + ATT_W:]

    rows = lambda width: pl.BlockSpec((tm, width), lambda i: (i, 0))
    return pl.pallas_call(
        body, name=name, grid=(t // tm,), in_specs=[rows(d), _whole((d, d))], out_specs=[rows(LRU_W), rows(ATT_W), rows(LRU_W)],
        out_shape=[SDS((t, LRU_W), F32), SDS((t, ATT_W), F32), SDS((t, LRU_W), F32)],
        compiler_params=_params(("parallel",)))(dout, w_out)


def mix_in_bwd(h, gain, dout, w_in, dx, dgate, dq, dk, dv, dqkv, dz, dba, *, name, tm=512):
    t, d = h.shape
    tm = min(tm, t)

    def body(h_ref, g_ref, do_ref, w_ref, dx_ref, dgate_ref, dq_ref, dk_ref, dv_ref, dqkv_ref, dz_ref, dba_ref,
             dh_ref, dg_ref, du_ref):
        @pl.when(pl.program_id(0) == 0)
        def _():
            dg_ref[...] = jnp.zeros_like(dg_ref)

        off = 0
        for piece in (dx_ref[...], dgate_ref[...], dq_ref[...], dk_ref[...], dv_ref[...], dqkv_ref[0], dqkv_ref[1],
                      dqkv_ref[2], dz_ref[...], dba_ref[0] + dba_ref[1]):
            du_ref[:, off:off + piece.shape[1]] = piece.astype(BF16)
            off += piece.shape[1]
        du_ref[:, off:] = jnp.zeros((tm, D_IN_PAD - off), BF16)
        g = g_ref[...]
        _, xh, r = _rms_fwd(h_ref[...], g)
        dh, dg = _rms_bwd(_nt(du_ref[...], w_ref[...]), xh, r, g)
        dh_ref[...] = do_ref[...] + dh
        dg_ref[...] += dg

    rows = lambda width: pl.BlockSpec((tm, width), lambda i: (i, 0))
    return pl.pallas_call(
        body, name=name, grid=(t // tm,),
        in_specs=[rows(d), _whole((1, d)), rows(d), _whole((d, D_IN_PAD)), rows(LRU_W), rows(LRU_W), rows(ATT_W),
                  rows(2 * HEAD), rows(2 * HEAD), pl.BlockSpec((3, tm, DN_HEADS * HEAD), lambda i: (0, i, 0)),
                  rows(DN_HEADS * HEAD), pl.BlockSpec((2, tm, 2 * HEAD), lambda i: (0, i, 0))],
        out_specs=[rows(d), _whole((1, d)), rows(D_IN_PAD)],
        out_shape=[SDS((t, d), F32), SDS((1, d), F32), SDS((t, D_IN_PAD), BF16)],
        compiler_params=_params(("arbitrary",)))(h, gain, dout, w_in, dx, dgate, dq, dk, dv, dqkv, dz, dba)


def _block_diag(w):
    out = jnp.zeros((LRU_W, LRU_W), w.dtype)
    for h in range(LRU_W // HEAD):
        out = lax.dynamic_update_slice(out, w[h], (h * HEAD, h * HEAD))
    return out


def _diag_blocks(w):
    per = LRU_HALF // HEAD
    return jnp.stack([w[h // per, (h % per) * HEAD:(h % per + 1) * HEAD, (h % per) * HEAD:(h % per + 1) * HEAD]
                      for h in range(LRU_W // HEAD)])


def layer_params(w, wl, l, bias):
    row = lambda a: a[l].reshape(1, -1)
    return dict(
        ffn1_norm=row(w["ffn1_norm"]), ffn1=(wl["ffn1_w_gate"], wl["ffn1_w_up"], wl["ffn1_w_down"]),
        mix_norm=row(w["mix_norm"]) + wl["tie1"][0:1, 0:1], w_in=wl["w_in"],
        lru=(wl["lru_conv_w"], row(w["lru_conv_b"]), _block_diag(w["lru_w_a"][l]), row(w["lru_b_a"]),
             _block_diag(w["lru_w_x"][l]), row(w["lru_b_x"]), row(w["lru_lambda"])),
        bias=bias, sink_rows=jnp.repeat(w["attn_sinks"][l], BLOCK_Q).reshape(ATT_HEADS * BLOCK_Q, 1),
        dn_cw=wl["dn_conv_w"], dn_alog=_ba_row(w["dn_a_log"][l]), dn_dt=_ba_row(w["dn_dt_bias"][l]),
        dn_norm=jnp.tile(row(w["dn_norm"]), (1, 2)), w_out=wl["w_out"],
        ffn2_norm=row(w["ffn2_norm"]), ffn2=(wl["ffn2_w_gate"], wl["ffn2_w_up"], wl["ffn2_w_down"]),
        ple_norm=row(w["ple_norm"]), ple_w_gate=wl["ple_w_gate"], ple_w_proj=wl["ple_w_proj"])


def _ba_row(per_head):
    return jnp.pad(per_head, (DN_HEADS, 2 * HEAD - 2 * DN_HEADS)).reshape(1, 2 * HEAD)


def mixer_fwd(h, p, nb, seq, tag):
    u, n = norm_matmul(h, p["mix_norm"], p["w_in"], name=f"mix_in_{tag}")
    y_lru = lru_fwd(u, *p["lru"], seq=seq, name=f"lru_fwd_{tag}")
    o = swa_fwd(u, p["bias"], p["sink_rows"], seq=seq, name=f"swa_fwd_{tag}")
    loc, egl = gdn_prep(u, p["dn_cw"], p["dn_alog"], p["dn_dt"], seq=seq, name=f"gdn_prep_{tag}")
    y_dn, o_raw, vn, st = gdn_scan(loc, egl, u, p["dn_norm"], seq=seq, name=f"gdn_scan_{tag}")
    out, ycat = mix_out(y_lru, o, y_dn, p["w_out"], h, name=f"mix_out_{tag}")
    return out, dict(h=h, u=u, n=n, loc=loc, egl=egl, o_raw=o_raw, vn=vn, st=st, ycat=ycat)


def mixer_bwd(dout, s, p, nb, seq, tag):
    u = s["u"]
    dy_lru, do, dy_dn = mix_out_bwd(dout, p["w_out"], name=f"mix_out_dx_{tag}")
    g = {"w_out": matmul(s["ycat"], dout, ta=True, tm=1024, name=f"mix_out_dw_{tag}")}
    dx, dgate, dcw, dwa, dwx, dvec = lru_bwd(u, *p["lru"], dy_lru, seq=seq, name=f"lru_bwd_{tag}")
    g.update(lru_conv_w=dcw, lru_conv_b=dvec[0], lru_w_a=_diag_blocks(dwa), lru_b_a=dvec[1], lru_w_x=_diag_blocks(dwx),
             lru_b_x=dvec[2], lru_lambda=dvec[3])
    dq, dk, dv, dbias, dsink = swa_bwd(u, p["bias"], p["sink_rows"], do, seq=seq, name=f"swa_bwd_{tag}")
    g.update(attn_sinks=dsink.reshape(ATT_HEADS, BLOCK_Q).sum(axis=1), bias=dbias)
    dloc, degl, dz, dgn = gdn_scan_bwd(s["loc"], s["egl"], u, p["dn_norm"], s["o_raw"], s["vn"], s["st"], dy_dn, seq=seq,
                                       name=f"gdn_scan_bwd_{tag}")
    dqkv, dba, dcw3, dhs = gdn_prep_bwd(u, p["dn_cw"], p["dn_alog"], p["dn_dt"], dloc, degl, seq=seq,
                                        name=f"gdn_prep_bwd_{tag}")
    dhs = dhs.sum(axis=0)[:, DN_HEADS:2 * DN_HEADS]
    g.update(dn_conv_w=dcw3.transpose(1, 0, 2).reshape(4, 3 * DN_HEADS * HEAD), dn_a_log=dhs[0], dn_dt_bias=dhs[1],
             dn_norm=dgn[0, :HEAD] + dgn[0, HEAD:])
    dh, dgain, du = mix_in_bwd(s["h"], p["mix_norm"], dout, p["w_in"], dx, dgate, dq, dk, dv, dqkv, dz, dba,
                               name=f"mix_in_bwd_{tag}")
    g["w_in"] = matmul(s["n"], du, ta=True, tm=1024, tn=640, name=f"mix_in_dw_{tag}")
    g["mix_norm"] = dgain[0]
    return dh, g


SHARDED = ("ffn1_w_gate", "ffn1_w_up", "ffn1_w_down", "w_in", "w_out", "ffn2_w_gate", "ffn2_w_up", "ffn2_w_down",
           "ple_w_gate", "ple_w_proj")
PER_LAYER_SMALL = ("ffn1_norm", "mix_norm", "lru_conv_w", "lru_conv_b", "lru_w_a", "lru_b_a", "lru_w_x", "lru_b_x",
                   "lru_lambda", "attn_sinks", "dn_conv_w", "dn_a_log", "dn_dt_bias", "dn_norm", "ffn2_norm", "ple_norm")


GRAD_PARTS = (("ple_w_gate", "ple_w_proj", "ffn2_w_gate", "ffn2_w_up", "ffn2_w_down"), ("w_in", "w_out"),
              ("ffn1_w_gate", "ffn1_w_up", "ffn1_w_down"))
WEIGHT_PARTS = (("ffn1_w_gate", "ffn1_w_up", "ffn1_w_down"),
                ("w_in", "w_out", "ffn2_w_gate", "ffn2_w_up", "ffn2_w_down", "ple_w_gate", "ple_w_proj", "lru_conv_w",
                 "dn_conv_w"))


def _col_shards(a):
    r, c = a.shape
    return a.reshape(r, N_CHIP, c // N_CHIP).transpose(1, 0, 2)


def local_step(x, p, target, w, layer_weights, layer_grads, bmap, nb, seq):
    bias = relbias_fwd(w["rel_bias"], bmap, name="relbias_fwd")
    h, saved = x, []
    for l in range(N_LAYER):
        wl = layer_weights(l, 0, h)
        s = dict(h0=h)
        h, *s["ffn1"] = ffn_fwd(h, w["ffn1_norm"][l].reshape(1, -1) + wl["tie0"][0:1, 0:1], wl["ffn1_w_gate"],
                                wl["ffn1_w_up"], wl["ffn1_w_down"], name=f"ffn1_fwd_{l}")
        wl.update(layer_weights(l, 1, h))
        pr = layer_params(w, wl, l, bias)
        h, s["mix"] = mixer_fwd(h, pr, nb, seq, l)
        s["h2"] = h
        h, *s["ffn2"] = ffn_fwd(h, pr["ffn2_norm"], *pr["ffn2"], name=f"ffn2_fwd_{l}")
        s["h3"] = h
        h = ple_fwd(h, pr["ple_norm"], pr["ple_w_gate"], p[l], pr["ple_w_proj"], name=f"ple_fwd_{l}")
        saved.append((pr, s))
    dh, dgf, loss = loss_head(h, w["final_norm"].reshape(1, -1), target, name="loss_head")

    per_layer, dbias, token = [None] * N_LAYER, None, None
    for l in reversed(range(N_LAYER)):
        pr, s = saved[l]
        g = {}
        dout = dh
        ple_norm = pr["ple_norm"] if token is None else pr["ple_norm"] + token[0:1, 0:1]
        dh, n, dga, dpp, dg = ple_bwd(s["h3"], ple_norm, pr["ple_w_gate"], p[l], pr["ple_w_proj"], dout, name=f"ple_bwd_{l}")
        g["ple_norm"] = dg[0]
        g["ple_w_gate"] = matmul(n, dga, ta=True, tm=1024, name=f"ple_dwg_{l}").reshape(N_CHIP, -1, D_MODEL)
        g["ple_w_proj"] = _col_shards(matmul(p[l], dpp, ta=True, name=f"ple_dwp_{l}"))
        for nm, hin in (("ffn2", s["h2"]), ("ffn1", s["h0"])):
            if nm == "ffn1":
                lru = list(pr["lru"])
                lru[1] = lru[1] + token[0:1, 0:1]
                dh, gm = mixer_bwd(dh, s["mix"], dict(pr, lru=tuple(lru)), nb, seq, l)
                dbias = gm.pop("bias") if dbias is None else dbias + gm.pop("bias")
                gm["w_in"] = _col_shards(gm["w_in"][:, :D_IN])
                gm["w_out"] = gm["w_out"].reshape(N_CHIP, -1, D_MODEL)
                g.update(gm)
                token = layer_grads(l, 1, {k: g.pop(k) for k in GRAD_PARTS[1]}, dh)
            dout = dh
            n, a, b = s[nm]
            dh, da, db, sact, dg = ffn_bwd_act(hin, pr[nm + "_norm"] + token[0:1, 0:1] if nm == "ffn1" else pr[nm + "_norm"],
                                               dout, a, b, *pr[nm], name=f"{nm}_bwd_act_{l}")
            g[nm + "_norm"] = dg[0]
            g[nm + "_w_gate"], g[nm + "_w_up"], g[nm + "_w_down"] = ffn_bwd_w(n, da, db, sact, dout, name=f"{nm}_bwd_w_{l}")
            part = 0 if nm == "ffn2" else 2
            token = layer_grads(l, part, {k: g.pop(k) for k in GRAD_PARTS[part]}, dh)
        per_layer[l] = g
    grads = {k: jnp.stack([per_layer[l][k] for l in range(N_LAYER)]) for k in PER_LAYER_SMALL}
    grads["rel_bias"] = relbias_bwd(dbias, bmap, name="relbias_bwd")[:, :ATT_HEADS]
    grads["final_norm"] = dgf[0]
    return loss, dh, grads


HBM_SPEC = pl.BlockSpec(memory_space=pltpu.HBM)


def _place():
    x, y, c = lax.axis_index("x"), lax.axis_index("y"), lax.axis_index("c")
    chips = [(1 - x, y), (x, 1 - y), (1 - x, 1 - y)]
    return x, y, c, 2 * x + y, (x, y, 1 - c), chips, [2 * cx + cy for cx, cy in chips]


def _remote(src, dst, send_sem, recv_sem, to):
    return pltpu.make_async_remote_copy(src_ref=src, dst_ref=dst, send_sem=send_sem, recv_sem=recv_sem, device_id=to,
                                        device_id_type=MESH)


def place_shard(w, chip_arr, dtype, *, name):
    nl, r, c = w.shape
    tr = next(cand for cand in (256, 128, 64, 32, 16, 8, r) if r % cand == 0)

    def body(chip_ref, w_ref, o_ref):
        o_ref[...] = w_ref[...].astype(dtype)

    return pl.pallas_call(
        body, name=name,
        grid_spec=pltpu.PrefetchScalarGridSpec(
            num_scalar_prefetch=1, grid=(nl, r // tr),
            in_specs=[pl.BlockSpec((None, tr, c), lambda l, i, chip: (l, i, 0))],
            out_specs=pl.BlockSpec((None, None, tr, c), lambda l, i, chip: (chip[0], l, i, 0))),
        out_shape=SDS((N_CHIP, nl, r, c), dtype), compiler_params=_params(("parallel", "parallel")))(chip_arr, w)


def allgather_shards(shards, *, name):
    n = len(shards)

    def body(*refs):
        outs = refs[n:2 * n]
        send, recv, fsend, frecv = refs[2 * n:]
        x, y, c, me, sib, chips, cids = _place()
        first, passed = [], []
        for k in range(n):
            for j, chip in enumerate(chips):
                mine = outs[k].at[me, c]
                first.append(_remote(mine, mine, send.at[3 * k + j], recv.at[3 * k + j], (*chip, c)))
                first[-1].start()
        for k in range(n):
            for j in range(3):
                piece = outs[k].at[cids[j], c]
                _remote(piece, piece, send.at[3 * k + j], recv.at[3 * k + j], sib).wait_recv()
                passed.append(_remote(piece, piece, fsend.at[3 * k + j], frecv.at[3 * k + j], sib))
                passed[-1].start()
        for k in range(n):
            for j in range(3):
                piece = outs[k].at[cids[j], 1 - c]
                _remote(piece, piece, fsend.at[3 * k + j], frecv.at[3 * k + j], sib).wait_recv()
        for cp in first + passed:
            cp.wait_send()

    return pl.pallas_call(
        body, name=name, in_specs=[HBM_SPEC] * n, out_specs=[HBM_SPEC] * n,
        out_shape=[SDS(s.shape, s.dtype) for s in shards], input_output_aliases={k: k for k in range(n)},
        scratch_shapes=[pltpu.SemaphoreType.DMA((3 * n,))] * 4)(*shards)


def exchange_layers(gs, *, name):
    n = len(gs)

    def body(*refs):
        ins, outs, (send, recv) = refs[:n], refs[n:2 * n], refs[2 * n:]
        x, y, c, me, sib, chips, cids = _place()
        cps = [_remote(ins[k].at[1 - c], outs[k], send.at[k], recv.at[k], sib) for k in range(n)]
        for cp in cps:
            cp.start()
        for cp in cps:
            cp.wait()

    return pl.pallas_call(
        body, name=name, in_specs=[HBM_SPEC] * n, out_specs=[HBM_SPEC] * n,
        out_shape=[SDS(g.shape[1:], g.dtype) for g in gs], scratch_shapes=[pltpu.SemaphoreType.DMA((n,))] * 2)(*gs)


def reduce_to_shards(ss, *, name):
    n = len(ss)

    def body(*refs):
        ins, outs, (send, recv) = refs[:n], refs[n:2 * n], refs[2 * n:]
        x, y, c, me, sib, chips, cids = _place()
        cps = []
        for k in range(n):
            for j, chip in enumerate(chips):
                cps.append(_remote(ins[k].at[cids[j]], outs[k].at[j], send.at[3 * k + j], recv.at[3 * k + j], (*chip, c)))
                cps[-1].start()
        for k in range(n):
            for j in range(3):
                slot = outs[k].at[j]
                _remote(slot, slot, send.at[3 * k + j], recv.at[3 * k + j], sib).wait_recv()
        for cp in cps:
            cp.wait_send()

    return pl.pallas_call(
        body, name=name, in_specs=[HBM_SPEC] * n, out_specs=[HBM_SPEC] * n,
        out_shape=[SDS((N_CHIP - 1,) + s.shape[1:], s.dtype) for s in ss],
        scratch_shapes=[pltpu.SemaphoreType.DMA((3 * n,))] * 2)(*ss)


def share_layers(fs, *, name):
    n = len(fs)

    def body(*refs):
        outs, (send, recv) = refs[n:2 * n], refs[2 * n:]
        x, y, c, me, sib, chips, cids = _place()
        cps = [_remote(outs[k].at[c], outs[k].at[c], send.at[k], recv.at[k], sib) for k in range(n)]
        for cp in cps:
            cp.start()
        for k in range(n):
            theirs = outs[k].at[1 - c]
            _remote(theirs, theirs, send.at[k], recv.at[k], sib).wait_recv()
        for cp in cps:
            cp.wait_send()

    return pl.pallas_call(
        body, name=name, in_specs=[HBM_SPEC] * n, out_specs=[HBM_SPEC] * n, out_shape=[SDS(f.shape, f.dtype) for f in fs],
        input_output_aliases={k: k for k in range(n)}, scratch_shapes=[pltpu.SemaphoreType.DMA((n,))] * 2)(*fs)


N_DEV = 8


def allreduce_small(buf, *, name):
    rows = buf.shape[0]

    def body(in_ref, out_ref, gath, send, recv):
        x, y, c = lax.axis_index("x"), lax.axis_index("y"), lax.axis_index("c")
        mine = 4 * x + 2 * y + c
        gath[mine] = in_ref[...]
        cps = []
        for k in range(1, N_DEV):
            to = (x ^ (k >> 2), y ^ ((k >> 1) & 1), c ^ (k & 1))
            cps.append(_remote(in_ref, gath.at[mine], send.at[k - 1], recv.at[k - 1], to))
            cps[-1].start()
        for k in range(1, N_DEV):
            theirs = gath.at[4 * (x ^ (k >> 2)) + 2 * (y ^ ((k >> 1) & 1)) + (c ^ (k & 1))]
            _remote(theirs, theirs, send.at[k - 1], recv.at[k - 1], (x, y, c)).wait_recv()
        for cp in cps:
            cp.wait_send()
        acc = gath[0]
        for d in range(1, N_DEV):
            acc = acc + gath[d]
        out_ref[...] = acc

    vm = pl.BlockSpec(memory_space=pltpu.VMEM)
    return pl.pallas_call(
        body, name=name, in_specs=[vm], out_specs=vm, out_shape=SDS(buf.shape, F32),
        scratch_shapes=[pltpu.VMEM((N_DEV, rows, 128), F32), pltpu.SemaphoreType.DMA((N_DEV - 1,)),
                        pltpu.SemaphoreType.DMA((N_DEV - 1,))])(buf)


def add_sibling(g, r, c_arr, *, name, tr=256):
    _, m, cdim = g.shape
    assert m % tr == 0

    def body(c_ref, g_ref, r_ref, o_ref):
        o_ref[...] = (g_ref[...] + r_ref[...]).astype(o_ref.dtype)

    return pl.pallas_call(
        body, name=name,
        grid_spec=pltpu.PrefetchScalarGridSpec(
            num_scalar_prefetch=1, grid=(m // tr,),
            in_specs=[pl.BlockSpec((None, tr, cdim), lambda i, c: (c[0], i, 0)), pl.BlockSpec((tr, cdim), lambda i, c: (i, 0))],
            out_specs=pl.BlockSpec((tr, cdim), lambda i, c: (i, 0))),
        out_shape=SDS((m, cdim), BF16), compiler_params=_params(("parallel",)))(c_arr, g, r)


def sum_slots(own, r, place_arr, *, name, tr=256):
    _, m, cdim = r.shape
    tr = next(cand for cand in (tr, 128, 64, 32, 16, 8) if m % cand == 0)

    def body(p_ref, own_ref, r_ref, o_ref):
        o_ref[...] = ((own_ref[...].astype(F32) + r_ref[0].astype(F32)) + r_ref[1].astype(F32)) + r_ref[2].astype(F32)

    return pl.pallas_call(
        body, name=name,
        grid_spec=pltpu.PrefetchScalarGridSpec(
            num_scalar_prefetch=1, grid=(m // tr,),
            in_specs=[pl.BlockSpec((None, tr, cdim), lambda i, p: (p[0], i, 0)),
                      pl.BlockSpec((N_CHIP - 1, tr, cdim), lambda i, p: (0, i, 0))],
            out_specs=pl.BlockSpec((None, tr, cdim), lambda i, p: (p[1], i, 0))),
        out_shape=SDS((N_LAYER, m, cdim), F32), compiler_params=_params(("parallel",)))(place_arr, own, r)


SEM_SPEC = pl.BlockSpec(memory_space=pltpu.SEMAPHORE)
ANY_SPEC = pl.BlockSpec(memory_space=pl.ANY)
DATAFLOW = pltpu.SideEffectType.DATAFLOW_SIDE_EFFECTING


def _in_hbm(a):
    return pltpu.with_memory_space_constraint(a, pltpu.HBM)


def _my_rows(ref_rows, c, mine=True):
    half = ref_rows // 2
    start = (c if mine else 1 - c) * half
    return pl.ds(pl.multiple_of(start, 8), half)


def place_layer_shard(w, layer, chip_arr, dtype, after, *, name):
    _, r, c = w.shape
    tr = next(cand for cand in (256, 128, 64, 32, 16, 8, r) if r % cand == 0)

    def body(chip_ref, w_ref, after_ref, o_ref):
        o_ref[...] = w_ref[...].astype(dtype)

    return pl.pallas_call(
        body, name=name,
        grid_spec=pltpu.PrefetchScalarGridSpec(
            num_scalar_prefetch=1, grid=(r // tr,),
            in_specs=[pl.BlockSpec((None, tr, c), lambda i, chip: (layer, i, 0)), ANY_SPEC],
            out_specs=pl.BlockSpec((None, tr, c), lambda i, chip: (chip[0], i, 0))),
        out_shape=SDS((N_CHIP, r, c), dtype), compiler_params=_params(("parallel",)))(chip_arr, w, after)


def _gather_pieces(refs, n_split, c, me, cids):
    mine, theirs = [], []
    for k, ref in enumerate(refs):
        if k < n_split:
            rows = _my_rows(ref.shape[1], c)
            mine.append(ref.at[me, rows])
            theirs.append([ref.at[cid, rows] for cid in cids])
        else:
            mine.append(ref.at[me])
            theirs.append([ref.at[cid] for cid in cids])
    return mine, theirs


def gather_start(bufs, n_split, after, *, name):
    n = len(bufs)

    def body(*refs):
        ins, send, recv, token = refs[:n], refs[n + 1], refs[n + 2], refs[-1]
        x, y, c, me, sib, chips, cids = _place()
        mine, _ = _gather_pieces(ins, n_split, c, me, cids)
        for k in range(n):
            for j, chip in enumerate(chips):
                _remote(mine[k], mine[k], send.at[3 * k + j], recv.at[3 * k + j], (*chip, c)).start()
        token[...] = jnp.zeros_like(token)

    out = pl.pallas_call(
        body, name=name, in_specs=[HBM_SPEC] * n + [ANY_SPEC],
        out_specs=[SEM_SPEC, SEM_SPEC] + [HBM_SPEC] * n + [pl.BlockSpec(memory_space=pltpu.VMEM)],
        out_shape=[pltpu.SemaphoreType.DMA((3 * n,)), pltpu.SemaphoreType.DMA((3 * n,))]
        + [pltpu.HBM(b.shape, b.dtype) for b in bufs] + [SDS((8, 128), F32)],
        input_output_aliases={k: k + 2 for k in range(n)},
        compiler_params=pltpu.CompilerParams(has_side_effects=DATAFLOW))(*[_in_hbm(b) for b in bufs], after)
    return out[0], out[1], list(out[2:2 + n]), out[-1]


def gather_wait(send, recv, bufs, n_split, after, *, name):
    n = len(bufs)

    def body(*refs):
        ins, send_ref, recv_ref = refs[:n], refs[n], refs[n + 1]
        x, y, c, me, sib, chips, cids = _place()
        mine, theirs = _gather_pieces(ins, n_split, c, me, cids)
        for k in range(n):
            for j in range(3):
                _remote(mine[k], mine[k], send_ref.at[3 * k + j], recv_ref.at[3 * k + j], sib).wait_send()
                _remote(theirs[k][j], theirs[k][j], send_ref.at[3 * k + j], recv_ref.at[3 * k + j], sib).wait_recv()

    return list(pl.pallas_call(
        body, name=name, in_specs=[HBM_SPEC] * n + [SEM_SPEC, SEM_SPEC, ANY_SPEC], out_specs=[HBM_SPEC] * n,
        out_shape=[pltpu.HBM(b.shape, b.dtype) for b in bufs], input_output_aliases={k: k for k in range(n)},
        compiler_params=pltpu.CompilerParams(has_side_effects=DATAFLOW))(*bufs, send, recv, after))


def gather_forward(bufs, *, name):
    n = len(bufs)

    def body(*refs):
        outs, (send, recv) = refs[n:2 * n], refs[2 * n:]
        x, y, c, me, sib, chips, cids = _place()
        cps = []
        for k in range(n):
            for j in range(3):
                piece = outs[k].at[cids[j], _my_rows(outs[k].shape[1], c)]
                cps.append(_remote(piece, piece, send.at[3 * k + j], recv.at[3 * k + j], sib))
                cps[-1].start()
        for k in range(n):
            for j in range(3):
                piece = outs[k].at[cids[j], _my_rows(outs[k].shape[1], c, mine=False)]
                _remote(piece, piece, send.at[3 * k + j], recv.at[3 * k + j], sib).wait_recv()
        for cp in cps:
            cp.wait_send()

    return list(pl.pallas_call(
        body, name=name, in_specs=[HBM_SPEC] * n, out_specs=[HBM_SPEC] * n, out_shape=[SDS(b.shape, b.dtype) for b in bufs],
        input_output_aliases={k: k for k in range(n)}, scratch_shapes=[pltpu.SemaphoreType.DMA((3 * n,))] * 2)(*bufs))


def reduce_exchange(gs, *, name):
    n = len(gs)

    def body(*refs):
        ins, outs, (send, recv) = refs[:n], refs[n:2 * n], refs[2 * n:]
        x, y, c, me, sib, chips, cids = _place()
        cps = [_remote(ins[k].at[pl.ds(0, N_CHIP), _my_rows(ins[k].shape[1], c, mine=False)], outs[k], send.at[k],
                       recv.at[k], sib) for k in range(n)]
        for cp in cps:
            cp.start()
        for cp in cps:
            cp.wait()

    return list(pl.pallas_call(
        body, name=name, in_specs=[HBM_SPEC] * n, out_specs=[HBM_SPEC] * n,
        out_shape=[SDS((N_CHIP, g.shape[1] // 2, g.shape[2]), g.dtype) for g in gs],
        scratch_shapes=[pltpu.SemaphoreType.DMA((n,))] * 2)(*gs))


def _half_tile(half):
    return next(cand for cand in (256, 176, 128, 64, 32, 16) if half % cand == 0)


def reduce_add(g, r, c_arr, *, name):
    _, rows, cdim = g.shape
    half = rows // 2
    tr = _half_tile(half)

    def body(c_ref, g_ref, r_ref, o_ref):
        o_ref[...] = (g_ref[...] + r_ref[...]).astype(o_ref.dtype)

    return pl.pallas_call(
        body, name=name,
        grid_spec=pltpu.PrefetchScalarGridSpec(
            num_scalar_prefetch=1, grid=(N_CHIP, half // tr),
            in_specs=[pl.BlockSpec((None, tr, cdim), lambda j, i, c: (j, c[0] * (half // tr) + i, 0)),
                      pl.BlockSpec((None, tr, cdim), lambda j, i, c: (j, i, 0))],
            out_specs=pl.BlockSpec((None, tr, cdim), lambda j, i, c: (j, i, 0))),
        out_shape=SDS((N_CHIP, half, cdim), BF16), compiler_params=_params(("parallel", "parallel")))(c_arr, g, r)


def reduce_start(ss, *, name):
    n = len(ss)

    def body(*refs):
        ins, lands, send, recv, token = refs[:n], refs[n:2 * n], refs[2 * n], refs[2 * n + 1], refs[-1]
        x, y, c, me, sib, chips, cids = _place()
        for k in range(n):
            for j, chip in enumerate(chips):
                _remote(ins[k].at[cids[j]], lands[k].at[j], send.at[3 * k + j], recv.at[3 * k + j], (*chip, c)).start()
        token[...] = jnp.zeros_like(token)

    lands = [_in_hbm(lax.empty((N_CHIP - 1,) + s.shape[1:], s.dtype)) for s in ss]
    out = pl.pallas_call(
        body, name=name, in_specs=[HBM_SPEC] * (2 * n),
        out_specs=[SEM_SPEC, SEM_SPEC] + [HBM_SPEC] * (2 * n) + [pl.BlockSpec(memory_space=pltpu.VMEM)],
        out_shape=[pltpu.SemaphoreType.DMA((3 * n,)), pltpu.SemaphoreType.DMA((3 * n,))]
        + [pltpu.HBM(b.shape, b.dtype) for b in list(ss) + lands] + [SDS((8, 128), F32)],
        input_output_aliases={k: k + 2 for k in range(2 * n)},
        compiler_params=pltpu.CompilerParams(has_side_effects=DATAFLOW))(*[_in_hbm(s) for s in ss], *lands)
    return out[0], out[1], list(out[2:2 + n]), list(out[2 + n:2 + 2 * n]), out[-1]


def reduce_wait(send, recv, ss, lands, after, *, name):
    n = len(ss)

    def body(*refs):
        ins, land_refs, send_ref, recv_ref = refs[:n], refs[n:2 * n], refs[2 * n], refs[2 * n + 1]
        x, y, c, me, sib, chips, cids = _place()
        for k in range(n):
            for j in range(3):
                _remote(ins[k].at[cids[j]], land_refs[k].at[j], send_ref.at[3 * k + j], recv_ref.at[3 * k + j],
                        sib).wait_send()
                _remote(ins[k].at[cids[j]], land_refs[k].at[j], send_ref.at[3 * k + j], recv_ref.at[3 * k + j],
                        sib).wait_recv()

    out = pl.pallas_call(
        body, name=name, in_specs=[HBM_SPEC] * (2 * n) + [SEM_SPEC, SEM_SPEC, ANY_SPEC], out_specs=[HBM_SPEC] * (2 * n),
        out_shape=[pltpu.HBM(b.shape, b.dtype) for b in list(ss) + list(lands)],
        input_output_aliases={k: k for k in range(2 * n)},
        compiler_params=pltpu.CompilerParams(has_side_effects=DATAFLOW))(*ss, *lands, send, recv, after)
    return list(out[:n]), list(out[n:])


def reduce_sum(own, land, place_arr, layer, acc, *, name):
    _, half, cdim = land.shape
    tr = _half_tile(half)

    def body(p_ref, own_ref, land_ref, *rest):
        o_ref = rest[-1]
        o_ref[...] = ((own_ref[...].astype(F32) + land_ref[0].astype(F32)) + land_ref[1].astype(F32)) + land_ref[2].astype(F32)

    in_specs = [pl.BlockSpec((None, tr, cdim), lambda i, p: (p[0], i, 0)),
                pl.BlockSpec((N_CHIP - 1, tr, cdim), lambda i, p: (0, i, 0))]
    args = [place_arr, own, land]
    if acc is not None:
        in_specs.append(ANY_SPEC)
        args.append(acc)
    return pl.pallas_call(
        body, name=name,
        grid_spec=pltpu.PrefetchScalarGridSpec(
            num_scalar_prefetch=1, grid=(half // tr,), in_specs=in_specs,
            out_specs=pl.BlockSpec((None, tr, cdim), lambda i, p: (layer, p[1] * (half // tr) + i, 0))),
        out_shape=SDS((N_LAYER, 2 * half, cdim), F32), input_output_aliases={} if acc is None else {3: 0},
        compiler_params=_params(("parallel",)))(*args)


def reduce_share(fs, *, name):
    n = len(fs)

    def body(*refs):
        outs, (send, recv) = refs[n:2 * n], refs[2 * n:]
        x, y, c, me, sib, chips, cids = _place()
        cps = []
        for k in range(n):
            piece = outs[k].at[pl.ds(0, N_LAYER), _my_rows(outs[k].shape[1], c)]
            cps.append(_remote(piece, piece, send.at[k], recv.at[k], sib))
            cps[-1].start()
        for k in range(n):
            theirs = outs[k].at[pl.ds(0, N_LAYER), _my_rows(outs[k].shape[1], c, mine=False)]
            _remote(theirs, theirs, send.at[k], recv.at[k], sib).wait_recv()
        for cp in cps:
            cp.wait_send()

    return list(pl.pallas_call(
        body, name=name, in_specs=[HBM_SPEC] * n, out_specs=[HBM_SPEC] * n, out_shape=[SDS(f.shape, f.dtype) for f in fs],
        input_output_aliases={k: k for k in range(n)}, scratch_shapes=[pltpu.SemaphoreType.DMA((n,))] * 2)(*fs))


WEIGHTS = ("ffn1_norm", "ffn1_w_gate", "ffn1_w_up", "ffn1_w_down", "mix_norm", "w_in", "lru_conv_w", "lru_conv_b", "lru_w_a",
           "lru_b_a", "lru_w_x", "lru_b_x", "lru_lambda", "attn_sinks", "rel_bias", "dn_conv_w", "dn_a_log", "dn_dt_bias",
           "dn_norm", "w_out", "ffn2_norm", "ffn2_w_gate", "ffn2_w_up", "ffn2_w_down", "ple_norm", "ple_w_gate",
           "ple_w_proj", "final_norm")
CONV_SHARDED = ("lru_conv_w", "dn_conv_w")
FFN_TRANSPOSED = ("ffn1_w_gate", "ffn1_w_up", "ffn2_w_gate", "ffn2_w_up")
SMALL = tuple(k for k in WEIGHTS if k not in SHARDED)


def _pack(arrs):
    blocks = []
    for a in arrs:
        v = a.reshape(-1)
        blocks.append(jnp.pad(v, (0, -v.shape[0] % 1024)).reshape(-1, 128))
    return jnp.concatenate(blocks, axis=0)


def _unpack(buf, shapes):
    out, off = [], 0
    for s in shapes:
        n = int(np.prod(s))
        rows = 8 * -(-n // 1024)
        out.append(buf[off:off + rows].reshape(-1)[:n].reshape(s))
        off += rows
    return out


def _chip_cols(a):
    n, l, r, c = a.shape
    return a.transpose(1, 2, 0, 3).reshape(l, r, n * c)


def _chip_rows(a):
    n, l, r, c = a.shape
    return a.transpose(1, 0, 2, 3).reshape(l, n * r, c)


def kernel(x, p, ffn1_norm, ffn1_w_gate, ffn1_w_up, ffn1_w_down, mix_norm, w_in, lru_conv_w, lru_conv_b, lru_w_a, lru_b_a, lru_w_x, lru_b_x, lru_lambda, attn_sinks, rel_bias, dn_conv_w, dn_a_log, dn_dt_bias, dn_norm, w_out, ffn2_norm, ffn2_w_gate, ffn2_w_up, ffn2_w_down, ple_norm, ple_w_gate, ple_w_proj, final_norm, loss_target, m_ffn1_norm, m_ffn1_w_gate, m_ffn1_w_up, m_ffn1_w_down, m_mix_norm, m_w_in, m_lru_conv_w, m_lru_conv_b, m_lru_w_a, m_lru_b_a, m_lru_w_x, m_lru_b_x, m_lru_lambda, m_attn_sinks, m_rel_bias, m_dn_conv_w, m_dn_a_log, m_dn_dt_bias, m_dn_norm, m_w_out, m_ffn2_norm, m_ffn2_w_gate, m_ffn2_w_up, m_ffn2_w_down, m_ple_norm, m_ple_w_gate, m_ple_w_proj, m_final_norm, v_ffn1_norm, v_ffn1_w_gate, v_ffn1_w_up, v_ffn1_w_down, v_mix_norm, v_w_in, v_lru_conv_w, v_lru_conv_b, v_lru_w_a, v_lru_b_a, v_lru_w_x, v_lru_b_x, v_lru_lambda, v_attn_sinks, v_rel_bias, v_dn_conv_w, v_dn_a_log, v_dn_dt_bias, v_dn_norm, v_w_out, v_ffn2_norm, v_ffn2_w_gate, v_ffn2_w_up, v_ffn2_w_down, v_ple_norm, v_ple_w_gate, v_ple_w_proj, v_final_norm):
    given = dict(locals())
    stored = lambda k, a: jnp.swapaxes(a, 1, 2) if k in FFN_TRANSPOSED else a
    ws = {k: stored(k, given[k]) for k in WEIGHTS}
    ms = {k: stored(k, given["m_" + k]) for k in WEIGHTS}
    vs = {k: stored(k, given["v_" + k]) for k in WEIGHTS}
    nb, seq, d = x.shape
    t = nb * seq
    cx, cy, cc = lax.axis_index("x"), lax.axis_index("y"), lax.axis_index("c")
    chip = 2 * cx + cy

    chip_arr = chip.astype(jnp.int32).reshape(1)
    c_arr = cc.astype(jnp.int32).reshape(1)
    place_arr = jnp.stack([chip, cc]).astype(jnp.int32)
    groups = [(l, part) for l in range(N_LAYER) for part in range(len(WEIGHT_PARTS))]
    placed, started = {}, {}

    def place_group(i, after):
        l, part = groups[i]
        for k in WEIGHT_PARTS[part]:
            placed[l, k] = place_layer_shard(ws[k], l, chip_arr, F32 if k in CONV_SHARDED else BF16, after,
                                             name=f"place_{k}_{l}")

    def start_group(i, after):
        l, part = groups[i]
        ks = WEIGHT_PARTS[part]
        n_split = sum(k in SHARDED for k in ks)
        started[i] = (ks, n_split) + gather_start([placed[l, k] for k in ks], n_split, after, name=f"gather_start_{l}_{part}")

    place_group(0, jnp.zeros((8, 128), F32))
    start_group(0, jnp.zeros((8, 128), F32))
    for i in range(1, len(groups)):
        place_group(i, started[0][-1])

    def layer_weights(l, part, h):
        i = groups.index((l, part))
        ks, n_split, send, recv, bufs, _ = started[i]
        bufs = gather_wait(send, recv, bufs, n_split, h, name=f"gather_wait_{l}_{part}")
        tie = jnp.zeros((8, 128), F32)
        for nxt in [j for j in range(i + 1, len(groups)) if j not in started and groups[j][0] == groups[min(i + 1, len(groups) - 1)][0]]:
            start_group(nxt, bufs[0] if nxt == i + 1 else started[nxt - 1][-1])
            tie = started[nxt][-1]
        wl = dict(zip(ks, gather_forward(bufs[:n_split], name=f"gather_forward_{l}_{part}") + bufs[n_split:]))
        for k in ("w_in", "ple_w_proj", "lru_conv_w", "dn_conv_w"):
            if k in wl:
                wl[k] = wl[k].transpose(1, 0, 2).reshape(wl[k].shape[1], -1)
        for k in ("w_out", "ple_w_gate"):
            if k in wl:
                wl[k] = wl[k].reshape(-1, wl[k].shape[-1])
        if "w_in" in wl:
            wl["w_in"] = jnp.pad(wl["w_in"], ((0, 0), (0, D_IN_PAD - D_IN)))
        wl[f"tie{part}"] = tie
        return wl

    pending, finished, tokens = [], {k: None for k in SHARDED}, []

    def finish_reduce(after):
        ks, send, recv, sums, lands, l, part = pending.pop(0)
        sums, lands = reduce_wait(send, recv, sums, lands, after, name=f"reduce_wait_{l}_{part}")
        for k, s, land in zip(ks, sums, lands):
            finished[k] = reduce_sum(s, land, place_arr, l, finished[k], name=f"reduce_sum_{k}_{l}")

    def layer_grads(l, part, g, dh):
        ks = GRAD_PARTS[part]
        gs = [g[k] for k in ks]
        theirs = reduce_exchange(gs, name=f"reduce_exchange_{l}_{part}")
        sums = [reduce_add(a, b, c_arr, name=f"reduce_add_{k}_{l}") for k, a, b in zip(ks, gs, theirs)]
        send, recv, sums, lands, token = reduce_start(sums, name=f"reduce_start_{l}_{part}")
        pending.append((ks, send, recv, sums, lands, l, part))
        while len(pending) > 2:
            finish_reduce(dh)
        tokens.append(token)
        return token

    small_w = {k: ws[k] for k in SMALL if k not in CONV_SHARDED}
    bmap = jnp.asarray(_rel_bucket_map())
    loss, gx, grads = local_step(x.reshape(t, d), p.reshape(N_LAYER, t, PLE_DIM), loss_target.reshape(t, d), small_w,
                                 layer_weights, layer_grads, bmap, nb, seq)
    g_out, delta, new_m, new_v = {}, {}, {}, {}

    small_shapes = [grads[k].shape for k in SMALL]
    g_small = dict(zip(SMALL, _unpack(allreduce_small(_pack([grads[k] for k in SMALL]), name="allreduce_small"), small_shapes)))
    for k in CONV_SHARDED:
        width = ws[k].shape[-1]
        g_small[k] = lax.dynamic_slice_in_dim(g_small[k], chip * width, width, axis=2)
    g_out.update(g_small)
    shapes = [ws[k].shape for k in SMALL]
    tie = tokens[-1][0:1, 0:1]
    res = adamw(_pack([ws[k] for k in SMALL]) + tie, *[_pack([src[k] for k in SMALL]) for src in (g_out, ms, vs)],
                name="adamw_small")
    for dst, r in zip((delta, new_m, new_v), res):
        dst.update(zip(SMALL, _unpack(r, shapes)))

    after = res[0]
    for part, ks in enumerate(GRAD_PARTS):
        while pending and pending[0][0] == ks:
            finish_reduce(after)
        g_out.update(zip(ks, reduce_share([finished[k] for k in ks], name=f"reduce_share_{part}")))
        for k in ks:
            two_d = lambda a: a.reshape(-1, a.shape[-1])
            res = adamw(two_d(ws[k]), two_d(g_out[k]), two_d(ms[k]), two_d(vs[k]), name=f"adamw_{k}")
            delta[k], new_m[k], new_v[k] = (r.reshape(ws[k].shape) for r in res)
        after = res[0]

    total = lax.psum(loss[0, 0], ("x", "y", "c"))
    return (total, gx.reshape(nb, seq, d), *[stored(k, out[k]) for out in (g_out, delta, new_m, new_v) for k in WEIGHTS])
```

```python
import math

import numpy as np
import jax
import jax.numpy as jnp
from jax import lax
from jax.experimental import pallas as pl
from jax.experimental.pallas import tpu as pltpu

F32 = jnp.float32
BF16 = jnp.bfloat16

EPS = 1e-6
D_MODEL = 1024
D_FF = 2816
N_CHIP = 4
FF_BLK = D_FF // N_CHIP
HEAD = 64
LRU_W = 256
ATT_W = 512
ATT_HEADS = 8
KV_HEADS = 2
ATT_GROUP = 4
BLOCK_Q = 128
DN_HEADS = 4
DN_CHUNK = 64
D_IN = 2312
D_IN_PAD = 2560
PLE_DIM = 256
REL_BUCKETS = 32
LRU_C = 8.0
N_LAYER = 2

ADAM_LR, ADAM_B1, ADAM_B2, ADAM_EPS, ADAM_WD, ADAM_STEP = 0.001, 0.9, 0.999, 1e-08, 0.01, 10

VMEM_LIMIT = 56 << 20
MESH = pl.DeviceIdType.MESH
SDS = jax.ShapeDtypeStruct


def _dot(a, b, ca=1, cb=0, hi=False):
    dims = (((ca,), (cb,)), ((), ()))
    one = lambda u, v: lax.dot_general(u, v, dims, preferred_element_type=F32)
    a_hi, b_hi = a.astype(BF16), b.astype(BF16)
    if not hi:
        return one(a_hi, b_hi)
    a_lo = (a - a_hi.astype(F32)).astype(BF16)
    b_lo = (b - b_hi.astype(F32)).astype(BF16)
    return one(a_hi, b_hi) + (one(a_hi, b_lo) + one(a_lo, b_hi))


def _nn(a, b, hi=False):
    return _dot(a, b, 1, 0, hi)


def _nt(a, b, hi=False):
    return _dot(a, b, 1, 1, hi)


def _tn(a, b, hi=False):
    return _dot(a, b, 0, 0, hi)


def _sigmoid(x):
    return jax.nn.sigmoid(x)


def _softplus(x):
    return jnp.maximum(x, 0.0) + jnp.log1p(jnp.exp(-jnp.abs(x)))


def _neg_expm1(z):
    series = -z * (1.0 + z * (0.5 + z * (1.0 / 6.0 + z * (1.0 / 24.0 + z * (1.0 / 120.0)))))
    return jnp.where(z > -0.05, series, 1.0 - jnp.exp(z))


_GELU_C = math.sqrt(2.0 / math.pi)


def _gelu(x):
    t = jnp.tanh(_GELU_C * (x + 0.044715 * x * x * x))
    return 0.5 * x * (1.0 + t), t


def _gelu_grad(x, t):
    return 0.5 * (1.0 + t) + 0.5 * x * (1.0 - t * t) * _GELU_C * (1.0 + 3.0 * 0.044715 * x * x)


def _rms_fwd(h, g):
    r = lax.rsqrt(jnp.mean(h * h, axis=-1, keepdims=True) + EPS)
    xh = h * r
    return xh * g, xh, r


def _rms_bwd(dn, xh, r, g):
    dxh = dn * g
    dh = r * (dxh - xh * jnp.mean(dxh * xh, axis=-1, keepdims=True))
    return dh, jnp.sum(dn * xh, axis=0, keepdims=True)


def _shift_down(x, d, fill=0.0):
    row = lax.broadcasted_iota(jnp.int32, x.shape, 0)
    return jnp.where(row >= d, pltpu.roll(x, d, 0), fill)


def _shift_up(x, d, fill=0.0):
    n = x.shape[0]
    row = lax.broadcasted_iota(jnp.int32, x.shape, 0)
    return jnp.where(row < n - d, pltpu.roll(x, n - d, 0), fill)


def _conv_fwd(x, w):
    y = x * w[3]
    for k in range(3):
        y = y + _shift_down(x, 3 - k) * w[k]
    return y


def _conv_bwd(dy, x, w):
    dx = dy * w[3]
    rows = [None] * 4
    rows[3] = jnp.sum(dy * x, axis=0, keepdims=True)
    for k in range(3):
        dx = dx + _shift_up(dy, 3 - k) * w[k]
        rows[k] = jnp.sum(dy * _shift_down(x, 3 - k), axis=0, keepdims=True)
    r4 = lax.broadcasted_iota(jnp.int32, (4, x.shape[1]), 0)
    dw = jnp.zeros((4, x.shape[1]), F32)
    for k in range(4):
        dw = jnp.where(r4 == k, rows[k], dw)
    return dx, dw


FFN_SPLIT = 2


def _interleave(gens):
    pending = list(gens)
    while pending:
        for g in list(pending):
            if next(g, StopIteration) is StopIteration:
                pending.remove(g)


def _params(sem=None, vmem=VMEM_LIMIT):
    return pltpu.CompilerParams(dimension_semantics=sem, vmem_limit_bytes=vmem)


def _whole(shape):
    nd = len(shape)
    return pl.BlockSpec(shape, lambda *_: (0,) * nd)


def matmul(a, b, *, name, ta=False, tb=False, residual=None, out_dtype=F32, tm=512, tn=512, tk=512):
    m, k = (a.shape[1], a.shape[0]) if ta else a.shape
    n = b.shape[0] if tb else b.shape[1]
    tm, tn, tk = min(tm, m), min(tn, n), min(tk, k)
    assert m % tm == 0 and n % tn == 0 and k % tk == 0, (m, n, k, tm, tn, tk)
    nk = k // tk

    def body(*refs):
        if residual is None:
            a_ref, b_ref, o_ref, acc = refs
        else:
            a_ref, b_ref, r_ref, o_ref, acc = refs
        kk = pl.program_id(2)

        @pl.when(kk == 0)
        def _():
            acc[...] = jnp.zeros_like(acc)

        acc[...] += _dot(a_ref[...], b_ref[...], 0 if ta else 1, 1 if tb else 0)

        @pl.when(kk == nk - 1)
        def _():
            out = acc[...]
            if residual is not None:
                out = out + r_ref[...]
            o_ref[...] = out.astype(out_dtype)

    a_spec = pl.BlockSpec((tk, tm), lambda i, j, kk: (kk, i)) if ta else pl.BlockSpec((tm, tk), lambda i, j, kk: (i, kk))
    b_spec = pl.BlockSpec((tn, tk), lambda i, j, kk: (j, kk)) if tb else pl.BlockSpec((tk, tn), lambda i, j, kk: (kk, j))
    o_spec = pl.BlockSpec((tm, tn), lambda i, j, kk: (i, j))
    in_specs, args = [a_spec, b_spec], [a, b]
    if residual is not None:
        in_specs.append(o_spec)
        args.append(residual)
    return pl.pallas_call(
        body, name=name, grid=(m // tm, n // tn, nk), in_specs=in_specs, out_specs=o_spec,
        out_shape=SDS((m, n), out_dtype), scratch_shapes=[pltpu.VMEM((tm, tn), F32)],
        compiler_params=_params(("parallel", "parallel", "arbitrary")))(*args)


def norm_matmul(h, gain, w, *, name, tm=512, tn=512):
    t, d = h.shape
    tm = min(tm, t)
    n = w.shape[1]
    assert t % tm == 0 and n % tn == 0

    def body(h_ref, g_ref, w_ref, u_ref, n_ref):
        @pl.when(pl.program_id(1) == 0)
        def _():
            n_ref[...] = _rms_fwd(h_ref[...], g_ref[...])[0].astype(BF16)

        u_ref[...] = _nn(n_ref[...], w_ref[...])

    return pl.pallas_call(
        body, name=name, grid=(t // tm, n // tn),
        in_specs=[pl.BlockSpec((tm, d), lambda i, j: (i, 0)), _whole((1, d)), pl.BlockSpec((d, tn), lambda i, j: (0, j))],
        out_specs=[pl.BlockSpec((tm, tn), lambda i, j: (i, j)), pl.BlockSpec((tm, d), lambda i, j: (i, 0))],
        out_shape=[SDS((t, n), F32), SDS((t, d), BF16)],
        compiler_params=_params(("parallel", "arbitrary")))(h, gain, w)


def ffn_fwd(h, gain, wg, wu, wd, *, name, tm=512):
    t, d = h.shape
    tm = min(tm, t)

    def body(h_ref, g_ref, wg_ref, wu_ref, wd_ref, o_ref, n_ref, a_ref, b_ref, acc):
        j = pl.program_id(1)

        @pl.when(j == 0)
        def _():
            n_ref[...] = _rms_fwd(h_ref[...], g_ref[...])[0].astype(BF16)
            acc[...] = jnp.zeros_like(acc)

        def part(rows):
            n = n_ref[rows, :]
            a = _nt(n, wg_ref[...])
            b = _nt(n, wu_ref[...])
            yield
            a_ref[rows, :] = a.astype(BF16)
            b_ref[rows, :] = b.astype(BF16)
            acc[rows, :] += _nn(a * _sigmoid(a) * b, wd_ref[...])

        _interleave([part(pl.ds(k * (tm // FFN_SPLIT), tm // FFN_SPLIT)) for k in range(FFN_SPLIT)])

        @pl.when(j == N_CHIP - 1)
        def _():
            o_ref[...] = h_ref[...] + 0.5 * acc[...]

    row = pl.BlockSpec((tm, d), lambda i, j: (i, 0))
    blk = pl.BlockSpec((None, tm, FF_BLK), lambda i, j: (j, i, 0))
    wspec = pl.BlockSpec((None, FF_BLK, d), lambda i, j: (j, 0, 0))
    act = SDS((N_CHIP, t, FF_BLK), BF16)
    return pl.pallas_call(
        body, name=name, grid=(t // tm, N_CHIP), in_specs=[row, _whole((1, d)), wspec, wspec, wspec],
        out_specs=[row, row, blk, blk], out_shape=[SDS((t, d), F32), SDS((t, d), BF16), act, act],
        scratch_shapes=[pltpu.VMEM((tm, d), F32)],
        compiler_params=_params(("parallel", "arbitrary")))(h, gain, wg, wu, wd)


def ffn_bwd_act(h, gain, dout, a, b, wg, wu, wd, *, name, tm=512):
    t, d = h.shape
    tm = min(tm, t)

    def body(h_ref, g_ref, do_ref, a_ref, b_ref, wg_ref, wu_ref, wd_ref, dh_ref, da_ref, db_ref, s_ref, dg_ref, dn_acc):
        i, j = pl.program_id(0), pl.program_id(1)

        @pl.when((i == 0) & (j == 0))
        def _():
            dg_ref[...] = jnp.zeros_like(dg_ref)

        @pl.when(j == 0)
        def _():
            dn_acc[...] = jnp.zeros_like(dn_acc)

        def part(rows):
            ds = _nt(0.5 * do_ref[rows, :], wd_ref[...])
            yield
            a = a_ref[rows, :].astype(F32)
            b = b_ref[rows, :].astype(F32)
            sig = _sigmoid(a)
            sa = a * sig
            db = ds * sa
            da = ds * b * (sig * (1.0 + a * (1.0 - sig)))
            s_ref[rows, :] = (sa * b).astype(BF16)
            da_ref[rows, :] = da.astype(BF16)
            db_ref[rows, :] = db.astype(BF16)
            yield
            dn_acc[rows, :] += _nn(da, wg_ref[...]) + _nn(db, wu_ref[...])

        _interleave([part(pl.ds(k * (tm // FFN_SPLIT), tm // FFN_SPLIT)) for k in range(FFN_SPLIT)])

        @pl.when(j == N_CHIP - 1)
        def _():
            g = g_ref[...]
            _, xh, r = _rms_fwd(h_ref[...], g)
            dh, dg = _rms_bwd(dn_acc[...], xh, r, g)
            dh_ref[...] = do_ref[...] + dh
            dg_ref[...] += dg

    row = pl.BlockSpec((tm, d), lambda i, j: (i, 0))
    blk = pl.BlockSpec((None, tm, FF_BLK), lambda i, j: (j, i, 0))
    wspec = pl.BlockSpec((None, FF_BLK, d), lambda i, j: (j, 0, 0))
    act = SDS((N_CHIP, t, FF_BLK), BF16)
    return pl.pallas_call(
        body, name=name, grid=(t // tm, N_CHIP), in_specs=[row, _whole((1, d)), row, blk, blk, wspec, wspec, wspec],
        out_specs=[row, blk, blk, blk, _whole((1, d))],
        out_shape=[SDS((t, d), F32), act, act, act, SDS((1, d), F32)],
        scratch_shapes=[pltpu.VMEM((tm, d), F32)],
        compiler_params=_params(("arbitrary", "arbitrary")))(h, gain, dout, a, b, wg, wu, wd)


def ffn_bwd_w(n, da, db, s, dout, *, name, tk=512):
    t, d = n.shape
    tk = min(tk, t)

    def body(n_ref, da_ref, db_ref, s_ref, do_ref, dwg_ref, dwu_ref, dwd_ref):
        @pl.when(pl.program_id(1) == 0)
        def _():
            dwg_ref[...] = jnp.zeros_like(dwg_ref)
            dwu_ref[...] = jnp.zeros_like(dwu_ref)
            dwd_ref[...] = jnp.zeros_like(dwd_ref)

        nn = n_ref[...]
        dwg_ref[...] += _tn(da_ref[...], nn)
        dwu_ref[...] += _tn(db_ref[...], nn)
        dwd_ref[...] += _tn(s_ref[...], 0.5 * do_ref[...])

    row = pl.BlockSpec((tk, d), lambda j, kk: (kk, 0))
    blk = pl.BlockSpec((None, tk, FF_BLK), lambda j, kk: (j, kk, 0))
    return pl.pallas_call(
        body, name=name, grid=(N_CHIP, t // tk), in_specs=[row, blk, blk, blk, row],
        out_specs=[pl.BlockSpec((None, FF_BLK, d), lambda j, kk: (j, 0, 0)),
                   pl.BlockSpec((None, FF_BLK, d), lambda j, kk: (j, 0, 0)),
                   pl.BlockSpec((None, FF_BLK, d), lambda j, kk: (j, 0, 0))],
        out_shape=[SDS((N_CHIP, FF_BLK, d), F32)] * 3,
        compiler_params=_params(("parallel", "arbitrary")))(n, da, db, s, dout)


def ple_fwd(h, gain, wpg, pl_in, wpp, *, name, tm=512):
    t, d = h.shape
    tm = min(tm, t)
    pd = pl_in.shape[1]

    def body(h_ref, g_ref, wpg_ref, p_ref, wpp_ref, o_ref):
        hh = h_ref[...]
        n = _rms_fwd(hh, g_ref[...])[0]
        gate = _sigmoid(_nn(n, wpg_ref[...]))
        o_ref[...] = hh + gate * _nn(p_ref[...], wpp_ref[...])

    row = pl.BlockSpec((tm, d), lambda i: (i, 0))
    return pl.pallas_call(
        body, name=name, grid=(t // tm,),
        in_specs=[row, _whole((1, d)), _whole((d, d)), pl.BlockSpec((tm, pd), lambda i: (i, 0)), _whole((pd, d))],
        out_specs=row, out_shape=SDS((t, d), F32), compiler_params=_params(("parallel",)))(h, gain, wpg, pl_in, wpp)


def ple_bwd(h, gain, wpg, pl_in, wpp, dout, *, name, tm=512):
    t, d = h.shape
    tm = min(tm, t)
    pd = pl_in.shape[1]

    def body(h_ref, g_ref, wpg_ref, p_ref, wpp_ref, do_ref, dh_ref, n_ref, dga_ref, dpp_ref, dg_ref):
        @pl.when(pl.program_id(0) == 0)
        def _():
            dg_ref[...] = jnp.zeros_like(dg_ref)

        g = g_ref[...]
        n, xh, r = _rms_fwd(h_ref[...], g)
        gate = _sigmoid(_nn(n, wpg_ref[...]))
        pp = _nn(p_ref[...], wpp_ref[...])
        do = do_ref[...]
        dga = do * pp * gate * (1.0 - gate)
        dh, dg = _rms_bwd(_nt(dga, wpg_ref[...]), xh, r, g)
        dh_ref[...] = do + dh
        n_ref[...] = n.astype(BF16)
        dga_ref[...] = dga.astype(BF16)
        dpp_ref[...] = (do * gate).astype(BF16)
        dg_ref[...] += dg

    row = pl.BlockSpec((tm, d), lambda i: (i, 0))
    return pl.pallas_call(
        body, name=name, grid=(t // tm,),
        in_specs=[row, _whole((1, d)), _whole((d, d)), pl.BlockSpec((tm, pd), lambda i: (i, 0)), _whole((pd, d)), row],
        out_specs=[row, row, row, row, _whole((1, d))],
        out_shape=[SDS((t, d), F32), SDS((t, d), BF16), SDS((t, d), BF16), SDS((t, d), BF16), SDS((1, d), F32)],
        compiler_params=_params(("arbitrary",)))(h, gain, wpg, pl_in, wpp, dout)


def loss_head(h, gain, target, *, name, tm=512):
    t, d = h.shape
    tm = min(tm, t)

    def body(h_ref, g_ref, t_ref, dh_ref, dg_ref, l_ref):
        @pl.when(pl.program_id(0) == 0)
        def _():
            dg_ref[...] = jnp.zeros_like(dg_ref)
            l_ref[...] = jnp.zeros_like(l_ref)

        g = g_ref[...]
        y, xh, r = _rms_fwd(h_ref[...], g)
        err = y - t_ref[...]
        l_ref[...] += 0.5 * jnp.sum(jnp.mean(err * err, axis=-1, keepdims=True), axis=0, keepdims=True)
        dh, dg = _rms_bwd(err * (1.0 / d), xh, r, g)
        dh_ref[...] = dh
        dg_ref[...] += dg

    row = pl.BlockSpec((tm, d), lambda i: (i, 0))
    return pl.pallas_call(
        body, name=name, grid=(t // tm,), in_specs=[row, _whole((1, d)), row],
        out_specs=[row, _whole((1, d)), _whole((1, 1))],
        out_shape=[SDS((t, d), F32), SDS((1, d), F32), SDS((1, 1), F32)],
        compiler_params=_params(("arbitrary",)))(h, gain, target)


def adamw(w, g, m, v, *, name):
    r, c = w.shape
    tr = r
    for cand in (512, 256, 128, 64, 32, 16, 8):
        if r % cand == 0:
            tr = cand
            break

    def body(w_ref, g_ref, m_ref, v_ref, d_ref, nm_ref, nv_ref):
        gg = g_ref[...]
        mm = ADAM_B1 * m_ref[...] + (1.0 - ADAM_B1) * gg
        vv = ADAM_B2 * v_ref[...] + (1.0 - ADAM_B2) * (gg * gg)
        m_hat = mm / (1.0 - ADAM_B1 ** ADAM_STEP)
        v_hat = vv / (1.0 - ADAM_B2 ** ADAM_STEP)
        d_ref[...] = -ADAM_LR * (m_hat / (jnp.sqrt(v_hat) + ADAM_EPS) + ADAM_WD * w_ref[...])
        nm_ref[...] = mm
        nv_ref[...] = vv

    blk = pl.BlockSpec((tr, c), lambda i: (i, 0))
    out = SDS((r, c), F32)
    return pl.pallas_call(body, name=name, grid=(r // tr,), in_specs=[blk] * 4, out_specs=[blk] * 3,
                          out_shape=[out, out, out], compiler_params=_params(("parallel",)))(w, g, m, v)


def _scan_fwd(a, b):
    d = 1
    while d < a.shape[0]:
        b = a * _shift_down(b, d, 0.0) + b
        a = a * _shift_down(a, d, 1.0)
        d *= 2
    return b


def _scan_rev(a, b):
    d = 1
    while d < a.shape[0]:
        b = a * _shift_up(b, d, 0.0) + b
        a = a * _shift_up(a, d, 1.0)
        d *= 2
    return b


LRU_HALF = 128


def _lru_in_specs(seq):
    half = LRU_W // LRU_HALF
    vec = pl.BlockSpec((1, LRU_HALF), lambda j, b: (0, j))
    mat = pl.BlockSpec((LRU_HALF, LRU_HALF), lambda j, b: (j, j))
    return [pl.BlockSpec((seq, LRU_HALF), lambda j, b: (b, j)), pl.BlockSpec((seq, LRU_HALF), lambda j, b: (b, half + j)),
            pl.BlockSpec((4, LRU_HALF), lambda j, b: (0, j)), vec, mat, vec, mat, vec, vec]


def _lru_math(x_ref, gate_ref, cw_ref, cb_ref, wa_ref, ba_ref, wx_ref, bx_ref, lam_ref):
    x = x_ref[...]
    gate = gate_ref[...]
    cw =[cw_ref[k:k + 1, :] for k in range(4)]
    xr = _conv_fwd(x, cw) + cb_ref[...]
    r = _sigmoid(_nn(xr, wa_ref[...]) + ba_ref[...])
    i = _sigmoid(_nn(xr, wx_ref[...]) + bx_ref[...])
    sp = _softplus(-lam_ref[...])
    log_a = -LRU_C * r * sp
    a = jnp.exp(log_a)
    mult = jnp.sqrt(_neg_expm1(2.0 * log_a))
    gi = i * xr
    h = _scan_fwd(a, mult * gi)
    gl, tg = _gelu(gate)
    return dict(x=x, gate=gate, cw=cw, xr=xr, r=r, i=i, sp=sp, a=a, mult=mult, gi=gi, h=h, gl=gl, tg=tg)


def lru_fwd(u, cw, cb, wa, ba, wx, bx, lam, *, seq, name):
    t = u.shape[0]

    def body(x_ref, gate_ref, cw_ref, cb_ref, wa_ref, ba_ref, wx_ref, bx_ref, lam_ref, y_ref):
        f = _lru_math(x_ref, gate_ref, cw_ref, cb_ref, wa_ref, ba_ref, wx_ref, bx_ref, lam_ref)
        y_ref[...] = f["gl"] * f["h"]

    return pl.pallas_call(
        body, name=name, grid=(LRU_W // LRU_HALF, t // seq), in_specs=_lru_in_specs(seq),
        out_specs=pl.BlockSpec((seq, LRU_HALF), lambda j, b: (b, j)), out_shape=SDS((t, LRU_W), F32),
        compiler_params=_params(("parallel", "parallel")))(u, u, cw, cb, wa, ba, wx, bx, lam)


def lru_bwd(u, cw, cb, wa, ba, wx, bx, lam, dy, *, seq, name):
    t = u.shape[0]

    def body(x_ref, gate_ref, cw_ref, cb_ref, wa_ref, ba_ref, wx_ref, bx_ref, lam_ref, dy_ref,
             dx_ref, dgate_ref, dcw_ref, dwa_ref, dwx_ref, dv_ref):
        @pl.when(pl.program_id(1) == 0)
        def _():
            dcw_ref[...] = jnp.zeros_like(dcw_ref)
            dwa_ref[...] = jnp.zeros_like(dwa_ref)
            dwx_ref[...] = jnp.zeros_like(dwx_ref)
            dv_ref[...] = jnp.zeros_like(dv_ref)

        f = _lru_math(x_ref, gate_ref, cw_ref, cb_ref, wa_ref, ba_ref, wx_ref, bx_ref, lam_ref)
        dy = dy_ref[...]
        a, h, xr, r, i, mult, gi, sp = f["a"], f["h"], f["xr"], f["r"], f["i"], f["mult"], f["gi"], f["sp"]
        dgate_ref[...] = dy * h * _gelu_grad(f["gate"], f["tg"])
        lamb = _scan_rev(_shift_up(a, 1, 0.0), dy * f["gl"])
        da = lamb * _shift_down(h, 1)
        dlog_a = da * a - (lamb * gi) * (a * a) / mult
        dgi = lamb * mult
        dra = dlog_a * (-LRU_C * sp) * r * (1.0 - r)
        dia = dgi * xr * i * (1.0 - i)
        dsp = jnp.sum(dlog_a * (-LRU_C * r), axis=0, keepdims=True)
        dlam = -dsp * _sigmoid(-lam_ref[...])
        dxr = dgi * i + _nt(dra, wa_ref[...]) + _nt(dia, wx_ref[...])
        dx, dcw = _conv_bwd(dxr, f["x"], f["cw"])
        dx_ref[...] = dx
        dcw_ref[...] += dcw
        dwa_ref[...] += _tn(xr, dra)
        dwx_ref[...] += _tn(xr, dia)
        rows = [jnp.sum(dxr, axis=0, keepdims=True), jnp.sum(dra, axis=0, keepdims=True),
                jnp.sum(dia, axis=0, keepdims=True), dlam]
        r8 = lax.broadcasted_iota(jnp.int32, (8, LRU_HALF), 0)
        acc = jnp.zeros((8, LRU_HALF), F32)
        for k, row in enumerate(rows):
            acc = jnp.where(r8 == k, row, acc)
        dv_ref[...] += acc

    nhalf = LRU_W // LRU_HALF
    col = pl.BlockSpec((seq, LRU_HALF), lambda j, b: (b, j))
    mat = pl.BlockSpec((None, LRU_HALF, LRU_HALF), lambda j, b: (j, 0, 0))
    return pl.pallas_call(
        body, name=name, grid=(nhalf, t // seq), in_specs=_lru_in_specs(seq) + [col],
        out_specs=[col, col, pl.BlockSpec((4, LRU_HALF), lambda j, b: (0, j)), mat, mat,
                   pl.BlockSpec((8, LRU_HALF), lambda j, b: (0, j))],
        out_shape=[SDS((t, LRU_W), F32), SDS((t, LRU_W), F32), SDS((4, LRU_W), F32),
                   SDS((nhalf, LRU_HALF, LRU_HALF), F32), SDS((nhalf, LRU_HALF, LRU_HALF), F32), SDS((8, LRU_W), F32)],
        compiler_params=_params(("arbitrary", "arbitrary")))(u, u, cw, cb, wa, ba, wx, bx, lam, dy)


NEG = -1e30


def _rel_bucket_map():
    dist = (np.arange(BLOCK_Q)[:, None] - np.arange(BLOCK_Q)[None, :]) % BLOCK_Q
    max_exact = REL_BUCKETS // 2
    large = max_exact + (np.log(np.maximum(dist, 1).astype(np.float32) / max_exact)
                         / math.log(BLOCK_Q / max_exact) * (REL_BUCKETS - max_exact)).astype(np.int32)
    large = np.minimum(large, REL_BUCKETS - 1)
    return np.where(dist < max_exact, dist, large).astype(np.int32)


def relbias_fwd(rel_bias, bmap, *, name):
    def body(rb_ref, bm_ref, o_ref):
        bm = bm_ref[...]
        for h in range(ATT_HEADS):
            acc = jnp.zeros((BLOCK_Q, BLOCK_Q), F32)
            for b in range(REL_BUCKETS):
                acc = jnp.where(bm == b, rb_ref[b, h], acc)
            o_ref[h] = acc

    return pl.pallas_call(
        body, name=name, in_specs=[pl.BlockSpec(memory_space=pltpu.SMEM), pl.BlockSpec(memory_space=pltpu.VMEM)],
        out_specs=pl.BlockSpec(memory_space=pltpu.VMEM), out_shape=SDS((ATT_HEADS, BLOCK_Q, BLOCK_Q), F32))(rel_bias, bmap)


def relbias_bwd(dbias, bmap, *, name):
    def body(db_ref, bm_ref, o_ref):
        bm = bm_ref[...]
        row = lax.broadcasted_iota(jnp.int32, (REL_BUCKETS, 128), 0)
        col = lax.broadcasted_iota(jnp.int32, (REL_BUCKETS, 128), 1)
        acc = jnp.zeros((REL_BUCKETS, 128), F32)
        for h in range(ATT_HEADS):
            d = db_ref[h]
            for b in range(REL_BUCKETS):
                s = jnp.sum(jnp.sum(jnp.where(bm == b, d, 0.0), axis=1, keepdims=True), axis=0, keepdims=True)
                acc = jnp.where((row == b) & (col == h), s, acc)
        o_ref[...] = acc

    return pl.pallas_call(body, name=name, out_shape=SDS((REL_BUCKETS, 128), F32))(dbias, bmap)


def _iota2(shape, axis):
    return lax.broadcasted_iota(jnp.int32, shape, axis)


def _chunk_cumsum(x):
    pos = _iota2(x.shape, 0) & (DN_CHUNK - 1)
    d = 1
    while d < DN_CHUNK:
        x = x + jnp.where(pos >= d, pltpu.roll(x, d, 0), 0.0)
        d *= 2
    return x


def _chunk_rev_cumsum(x):
    n = x.shape[0]
    pos = _iota2(x.shape, 0) & (DN_CHUNK - 1)
    d = 1
    while d < DN_CHUNK:
        x = x + jnp.where(pos < DN_CHUNK - d, pltpu.roll(x, n - d, 0), 0.0)
        d *= 2
    return x


_DN_SCALE = (HEAD ** -0.5, 1.0, None)
DN_UNROLL = 4


COL_Q, COL_K, COL_V = 512 // 128, 1024 // 128, 1152 // 128
COL_DNQ, COL_DNK, COL_DNV, COL_DNZ, COL_BA = 1280 // 128, 1536 // 128, 1792 // 128, 2048 // 128, 2304 // 128


def _lane_a(shape):
    return _iota2(shape, 1) < HEAD


def _bd(x):
    la = _lane_a(x.shape)
    return jnp.concatenate([jnp.where(la, x, 0.0), jnp.where(la, 0.0, x)], axis=0)


def _fold(m):
    return m[:HEAD] + m[HEAD:]


def _bd_mask():
    return (_iota2((2 * HEAD, 2 * HEAD), 0) < HEAD) == (_iota2((2 * HEAD, 2 * HEAD), 1) < HEAD)


def _pk_nn(x, y, hi=False):
    return _nn(x, _bd(y), hi)


def _pk_nt(u, v, hi=False):
    return _nt(u, _bd(v), hi)


def _pk_tn(x, y, hi=False):
    return _fold(jnp.where(_bd_mask(), _tn(x, y, hi), 0.0))


def _half_sum(x):
    la = _lane_a(x.shape)
    return jnp.where(la, jnp.sum(jnp.where(la, x, 0.0), axis=-1, keepdims=True),
                     jnp.sum(jnp.where(la, 0.0, x), axis=-1, keepdims=True))


def _lane_col(x, idx):
    return jnp.sum(jnp.where(_iota2(x.shape, 1) == idx, x, 0.0), axis=-1, keepdims=True)


def _row0(x):
    return jnp.max(x, axis=0, keepdims=True)


def _dup_kv(x, g):
    la = _lane_a(x.shape)
    rolled = pltpu.roll(x, HEAD, 1)
    return jnp.where(la, x, rolled) if g == 0 else jnp.where(la, rolled, x)


def _stack_heads(ref, g):
    la = _lane_a((BLOCK_Q, 2 * HEAD))
    parts = []
    for hh in range(ATT_GROUP):
        pair = ref[:, pl.ds(2 * HEAD * (2 * g + hh // 2), 2 * HEAD)]
        parts.append(jnp.where(la if hh % 2 == 0 else ~la, pair, 0.0))
    return jnp.concatenate(parts, axis=0)


def _unstack_heads(stack, ref, g):
    la = _lane_a((BLOCK_Q, 2 * HEAD))
    for j in range(2):
        top = stack[2 * j * BLOCK_Q:(2 * j + 1) * BLOCK_Q]
        bot = stack[(2 * j + 1) * BLOCK_Q:(2 * j + 2) * BLOCK_Q]
        ref[:, pl.ds(2 * HEAD * (2 * g + j), 2 * HEAD)] = jnp.where(la, top, bot)


def _swa_probs(q_ref, k_ref, v_ref, b_ref, s_ref, n, g):
    rows = ATT_GROUP * BLOCK_Q
    prev = pl.multiple_of(jnp.maximum(n - 1, 0) * BLOCK_Q, BLOCK_Q)
    cur = pl.multiple_of(n * BLOCK_Q, BLOCK_Q)
    kp, kc = _dup_kv(k_ref[pl.ds(prev, BLOCK_Q), :], g), _dup_kv(k_ref[pl.ds(cur, BLOCK_Q), :], g)
    vp, vc = _dup_kv(v_ref[pl.ds(prev, BLOCK_Q), :], g), _dup_kv(v_ref[pl.ds(cur, BLOCK_Q), :], g)
    qs = _stack_heads(q_ref, g) * (HEAD ** -0.5)
    bias = b_ref[pl.ds(ATT_GROUP * g, ATT_GROUP)].reshape(rows, BLOCK_Q)
    i = _iota2((rows, BLOCK_Q), 0) & (BLOCK_Q - 1)
    j = _iota2((rows, BLOCK_Q), 1)
    s_p = jnp.where((j > i) & (n > 0), _nt(qs, kp) + bias, NEG)
    s_c = jnp.where(j <= i, _nt(qs, kc) + bias, NEG)
    sink = s_ref[pl.ds(rows * g, rows), :]
    m = jnp.maximum(jnp.maximum(jnp.max(s_p, axis=-1, keepdims=True), jnp.max(s_c, axis=-1, keepdims=True)), sink)
    e_p, e_c, e_s = jnp.exp(s_p - m), jnp.exp(s_c - m), jnp.exp(sink - m)
    inv = 1.0 / (jnp.sum(e_p, axis=-1, keepdims=True) + jnp.sum(e_c, axis=-1, keepdims=True) + e_s)
    return e_p * inv, e_c * inv, e_s * inv, qs, kp, kc, vp, vc, prev, cur


def _swa_specs(seq):
    nblk = seq // BLOCK_Q
    qspec = pl.BlockSpec((BLOCK_Q, ATT_W), lambda b, n: (b * nblk + n, COL_Q * 128 // ATT_W))
    kspec = pl.BlockSpec((seq, 2 * HEAD), lambda b, n: (b, COL_K))
    vspec = pl.BlockSpec((seq, 2 * HEAD), lambda b, n: (b, COL_V))
    ospec = pl.BlockSpec((BLOCK_Q, ATT_W), lambda b, n: (b * nblk + n, 0))
    kvout = pl.BlockSpec((seq, 2 * HEAD), lambda b, n: (b, 0))
    return qspec, kspec, vspec, ospec, kvout, _whole((ATT_HEADS, BLOCK_Q, BLOCK_Q)), _whole((ATT_HEADS * BLOCK_Q, 1))


def swa_fwd(u, bias, sink_rows, *, seq, name):
    t = u.shape[0]

    def body(q_ref, k_ref, v_ref, b_ref, s_ref, o_ref):
        for g in range(KV_HEADS):
            p_p, p_c, _, _, _, _, vp, vc, _, _ = _swa_probs(q_ref, k_ref, v_ref, b_ref, s_ref, pl.program_id(1), g)
            _unstack_heads(_nn(p_p, vp) + _nn(p_c, vc), o_ref, g)

    qspec, kspec, vspec, ospec, kvout, bspec, sspec = _swa_specs(seq)
    return pl.pallas_call(
        body, name=name, grid=(t // seq, seq // BLOCK_Q), in_specs=[qspec, kspec, vspec, bspec, sspec], out_specs=ospec,
        out_shape=SDS((t, ATT_W), F32), compiler_params=_params(("parallel", "arbitrary")))(u, u, u, bias, sink_rows)


def swa_bwd(u, bias, sink_rows, do, *, seq, name):
    t = u.shape[0]

    def body(q_ref, k_ref, v_ref, b_ref, s_ref, do_ref, dq_ref, dk_ref, dv_ref, db_ref, ds_ref):
        b, n = pl.program_id(0), pl.program_id(1)

        @pl.when((b == 0) & (n == 0))
        def _():
            db_ref[...] = jnp.zeros_like(db_ref)
            ds_ref[...] = jnp.zeros_like(ds_ref)

        @pl.when(n == 0)
        def _():
            dk_ref[...] = jnp.zeros_like(dk_ref)
            dv_ref[...] = jnp.zeros_like(dv_ref)

        la = _lane_a((BLOCK_Q, 2 * HEAD))
        for g in range(KV_HEADS):
            p_p, p_c, p_s, qs, kp, kc, vp, vc, prev, cur = _swa_probs(q_ref, k_ref, v_ref, b_ref, s_ref, n, g)
            do = _stack_heads(do_ref, g)
            dp_p, dp_c = _nt(do, vp), _nt(do, vc)
            delta = jnp.sum(p_p * dp_p, axis=-1, keepdims=True) + jnp.sum(p_c * dp_c, axis=-1, keepdims=True)
            ds_p, ds_c = p_p * (dp_p - delta), p_c * (dp_c - delta)
            _unstack_heads((_nn(ds_p, kp) + _nn(ds_c, kc)) * (HEAD ** -0.5), dq_ref, g)
            mine = la if g == 0 else ~la

            def to_head(x):
                return jnp.where(mine, x + pltpu.roll(x, HEAD, 1), 0.0)

            dk_ref[pl.ds(prev, BLOCK_Q), :] += to_head(_tn(ds_p, qs))
            dk_ref[pl.ds(cur, BLOCK_Q), :] += to_head(_tn(ds_c, qs))
            dv_ref[pl.ds(prev, BLOCK_Q), :] += to_head(_tn(p_p, do))
            dv_ref[pl.ds(cur, BLOCK_Q), :] += to_head(_tn(p_c, do))
            db_ref[pl.ds(ATT_GROUP * g, ATT_GROUP)] += (ds_p + ds_c).reshape(ATT_GROUP, BLOCK_Q, BLOCK_Q)
            rows = ATT_GROUP * BLOCK_Q
            ds_ref[pl.ds(rows * g, rows), :] += -p_s * delta

    qspec, kspec, vspec, ospec, kvout, bspec, sspec = _swa_specs(seq)
    return pl.pallas_call(
        body, name=name, grid=(t // seq, seq // BLOCK_Q), in_specs=[qspec, kspec, vspec, bspec, sspec, ospec],
        out_specs=[ospec, kvout, kvout, bspec, sspec],
        out_shape=[SDS((t, ATT_W), F32), SDS((t, 2 * HEAD), F32), SDS((t, 2 * HEAD), F32),
                   SDS((ATT_HEADS, BLOCK_Q, BLOCK_Q), F32), SDS((ATT_HEADS * BLOCK_Q, 1), F32)],
        compiler_params=_params(("arbitrary", "arbitrary")))(u, u, u, bias, sink_rows, do)


def _gdn_gates(ba_ref, alog_ref, dt_ref, hp):
    blk = ba_ref[...]
    beta_blk = _sigmoid(blk)
    sp_arg = blk + dt_ref[...]
    a_exp = jnp.exp(alog_ref[...])
    g_blk = -a_exp * _softplus(sp_arg)
    la = _lane_a(blk.shape)
    ha = 2 * hp
    beta = jnp.where(la, _lane_col(beta_blk, ha), _lane_col(beta_blk, ha + 1))
    g = jnp.where(la, _lane_col(g_blk, DN_HEADS + ha), _lane_col(g_blk, DN_HEADS + ha + 1))
    return beta, g, beta_blk, sp_arg, a_exp, g_blk


def _gdn_act(c, scale):
    sig = _sigmoid(c)
    a = c * sig
    if scale is None:
        return a, sig, None, None
    r = lax.rsqrt(_half_sum(a * a) + EPS)
    return a * r * scale, sig, a * r, r


def _gdn_inputs(pre_refs, cw_refs, ba_ref, alog_ref, dt_ref, hp, act_sc, b_sc, gc_sc, c_sc=None):
    for idx in range(3):
        c = _conv_fwd(pre_refs[idx][...], [cw_refs[idx][k:k + 1, :] for k in range(4)])
        if c_sc is not None:
            c_sc[idx] = c
        act_sc[idx] = _gdn_act(c, _DN_SCALE[idx])[0]
    beta, g = _gdn_gates(ba_ref, alog_ref, dt_ref, hp)[:2]
    b_sc[...] = beta
    gc_sc[...] = _chunk_cumsum(g)


def _gdn_chunk(q, k, v, b, gcc):
    shape = q.shape
    row, lm = _iota2(shape, 0), _iota2(shape, 1) & (HEAD - 1)
    tril, strict, eye = row >= lm, row > lm, row == lm
    eg = jnp.exp(gcc)
    kb, vb = k * b, v * b
    kbg = kb * eg
    grow = jnp.sum(jnp.where(eye, gcc, 0.0), axis=0, keepdims=True)
    dm = jnp.exp(jnp.where(tril, gcc - grow, NEG))
    kk = _pk_nt(kb, k)
    glast = jnp.sum(jnp.where(row == DN_CHUNK - 1, gcc, 0.0), axis=0, keepdims=True)
    ekd = jnp.exp(glast - gcc)
    qk = _pk_nt(q, k)
    return dict(q=q, k=k, v=v, b=b, tril=tril, strict=strict, eye=eye, row=row, eg=eg, kb=kb, vb=vb, kbg=kbg, dm=dm, kk=kk,
                low=jnp.where(strict, kk * dm, 0.0), glast=glast, ekd=ekd, kd=k * ekd, qk=qk,
                amat=jnp.where(tril, qk * dm, 0.0), qg=q * eg, egl=jnp.broadcast_to(jnp.exp(glast), shape))


def _tri_inv_many(chunks):
    ms = [-m["low"] for m in chunks]
    ts = [m["eye"].astype(F32) + x for m, x in zip(chunks, ms)]
    for _ in range(int(math.log2(HEAD)) - 1):
        ms = [_pk_nn(x, x, hi=True) for x in ms]
        ts = [t + _pk_nn(t, x, hi=True) for t, x in zip(ts, ms)]
    return ts


def _gdn_chunk_loop(nc, act_sc, b_sc, gc_sc, finish):
    u = math.gcd(nc, DN_UNROLL)

    def step(i, carry):
        rows = [pl.ds(pl.multiple_of((i * u + j) * DN_CHUNK, DN_CHUNK), DN_CHUNK) for j in range(u)]
        chunks = [_gdn_chunk(act_sc[0, r, :], act_sc[1, r, :], act_sc[2, r, :], b_sc[r, :], gc_sc[r, :]) for r in rows]
        pending = [finish(r, m, t) for r, m, t in zip(rows, chunks, _tri_inv_many(chunks))]
        pending = [g for g in pending if g is not None]
        while pending:
            for g in list(pending):
                if next(g, StopIteration) is StopIteration:
                    pending.remove(g)
        return carry

    lax.fori_loop(0, nc // u, step, 0)


def _gdn_in_specs(seq):
    u_at = lambda col: pl.BlockSpec((seq, 2 * HEAD), lambda b, hp, _c=col: (b, _c + hp))
    cw_at = lambda col: pl.BlockSpec((4, 2 * HEAD), lambda b, hp, _c=col: (0, _c + hp))
    row = pl.BlockSpec((1, 2 * HEAD), lambda b, hp: (0, 0))
    ba = pl.BlockSpec((seq, 2 * HEAD), lambda b, hp: (b, COL_BA))
    return [u_at(COL_DNQ), u_at(COL_DNK), u_at(COL_DNV), ba, cw_at(0), cw_at(2), cw_at(4), row, row]


def _pair(seq, lead=None):
    if lead is None:
        return pl.BlockSpec((seq, 2 * HEAD), lambda b, hp: (b, hp))
    return pl.BlockSpec((lead, seq, 2 * HEAD), lambda b, hp: (0, b, hp))


def _swap(spec):
    return pl.BlockSpec(spec.block_shape, lambda hp, b, _f=spec.index_map: _f(b, hp))


def gdn_prep(u, cw, alog_row, dt_row, *, seq, name):
    t = u.shape[0]
    nc = seq // DN_CHUNK

    def body(q_ref, k_ref, v_ref, ba_ref, cq_ref, ck_ref, cv_ref, alog_ref, dt_ref, loc_ref, egl_ref, act_sc, b_sc, gc_sc):
        _gdn_inputs((q_ref, k_ref, v_ref), (cq_ref, ck_ref, cv_ref), ba_ref, alog_ref, dt_ref, pl.program_id(1),
                    act_sc, b_sc, gc_sc)

        def finish(rows, m, t):
            loc_ref[0, rows, :] = m["qg"]
            loc_ref[1, rows, :] = m["kd"]
            loc_ref[2, rows, :] = _pk_nn(t, m["vb"])
            loc_ref[3, rows, :] = _pk_nn(t, m["kbg"])
            loc_ref[4, rows, :] = m["amat"]
            egl_ref[rows, :] = m["egl"]

        _gdn_chunk_loop(nc, act_sc, b_sc, gc_sc, finish)

    return pl.pallas_call(
        body, name=name, grid=(t // seq, DN_HEADS // 2), in_specs=_gdn_in_specs(seq), out_specs=[_pair(seq, 5), _pair(seq)],
        out_shape=[SDS((5, t, DN_HEADS * HEAD), F32), SDS((t, DN_HEADS * HEAD), F32)],
        scratch_shapes=[pltpu.VMEM((3, seq, 2 * HEAD), F32)] + [pltpu.VMEM((seq, 2 * HEAD), F32)] * 2,
        compiler_params=_params(("parallel", "parallel")))(u, u, u, u, cw, cw, cw, alog_row, dt_row)


def _gated_norm2(o, z, gn):
    r = lax.rsqrt(_half_sum(o * o) * (1.0 / HEAD) + EPS)
    return o * r, _sigmoid(z), r


def gdn_scan(loc, egl, u, gn, *, seq, name):
    t = u.shape[0]
    nc = seq // DN_CHUNK

    def body(loc_ref, egl_ref, z_ref, gn_ref, y_ref, o_ref, vn_ref, st_ref):
        gn = gn_ref[...]
        bdm = _bd_mask()

        def step(c, state):
            rows = pl.ds(pl.multiple_of(c * DN_CHUNK, DN_CHUNK), DN_CHUNK)
            st_ref[rows, :] = _fold(state)
            vn = loc_ref[2, rows, :] - _nn(loc_ref[3, rows, :], state)
            o = _nn(loc_ref[0, rows, :], state) + _pk_nn(loc_ref[4, rows, :], vn)
            vn_ref[rows, :] = vn
            o_ref[rows, :] = o
            zz = z_ref[rows, :]
            on, sig, _ = _gated_norm2(o, zz, gn)
            y_ref[rows, :] = on * gn * (zz * sig)
            return state * _row0(egl_ref[rows, :]) + jnp.where(bdm, _tn(loc_ref[1, rows, :], vn), 0.0)

        lax.fori_loop(0, nc, step, jnp.zeros((2 * HEAD, 2 * HEAD), F32))

    zspec = pl.BlockSpec((seq, 2 * HEAD), lambda b, hp: (b, COL_DNZ + hp))
    out = SDS((t, DN_HEADS * HEAD), F32)
    return pl.pallas_call(
        body, name=name, grid=(t // seq, DN_HEADS // 2), in_specs=[_pair(seq, 5), _pair(seq), zspec, _whole((1, 2 * HEAD))],
        out_specs=[_pair(seq)] * 4, out_shape=[out] * 4,
        compiler_params=_params(("parallel", "parallel")))(loc, egl, u, gn)


def gdn_scan_bwd(loc, egl, u, gn, o, vn, states, dy, *, seq, name):
    t = u.shape[0]
    nc = seq // DN_CHUNK

    def body(loc_ref, egl_ref, z_ref, gn_ref, o_ref, vn_ref, st_ref, dy_ref, dloc_ref, degl_ref, dz_ref, dgn_ref):
        @pl.when((pl.program_id(0) == 0) & (pl.program_id(1) == 0))
        def _():
            dgn_ref[...] = jnp.zeros_like(dgn_ref)

        gn = gn_ref[...]
        bdm = _bd_mask()
        shape = (DN_CHUNK, 2 * HEAD)
        tril = _iota2(shape, 0) >= (_iota2(shape, 1) & (HEAD - 1))

        def step(i, carry):
            ds, dgn = carry
            rows = pl.ds(pl.multiple_of((nc - 1 - i) * DN_CHUNK, DN_CHUNK), DN_CHUNK)
            dy, zz, oo = dy_ref[rows, :], z_ref[rows, :], o_ref[rows, :]
            on, sig, r = _gated_norm2(oo, zz, gn)
            sz = zz * sig
            dz_ref[rows, :] = dy * on * gn * (sig * (1.0 + zz * (1.0 - sig)))
            dgn = dgn + jnp.sum(dy * on * sz, axis=0, keepdims=True)
            don = dy * gn * sz
            do = r * (don - on * _half_sum(don * on) * (1.0 / HEAD))
            state, vnew = _bd(st_ref[rows, :]), vn_ref[rows, :]
            qg, kd, w, amat = loc_ref[0, rows, :], loc_ref[1, rows, :], loc_ref[3, rows, :], loc_ref[4, rows, :]
            dvn = _pk_tn(amat, do) + _nn(kd, ds)
            dloc_ref[0, rows, :] = _nt(do, state)
            dloc_ref[1, rows, :] = _nt(vnew, ds)
            dloc_ref[2, rows, :] = dvn
            dloc_ref[3, rows, :] = -_nt(dvn, state)
            dloc_ref[4, rows, :] = jnp.where(tril, _pk_nt(do, vnew), 0.0)
            degl = _half_sum(jnp.sum(state * ds, axis=0, keepdims=True))
            degl_ref[rows, :] = jnp.broadcast_to(degl, shape)
            grow = jnp.where(bdm, _tn(qg, do) - _tn(w, dvn), 0.0)
            return ds * _row0(egl_ref[rows, :]) + grow, dgn

        _, dgn = lax.fori_loop(0, nc, step, (jnp.zeros((2 * HEAD, 2 * HEAD), F32), jnp.zeros((1, 2 * HEAD), F32)))
        dgn_ref[...] += dgn

    zspec = pl.BlockSpec((seq, 2 * HEAD), lambda b, hp: (b, COL_DNZ + hp))
    one = _pair(seq)
    out = SDS((t, DN_HEADS * HEAD), F32)
    return pl.pallas_call(
        body, name=name, grid=(t // seq, DN_HEADS // 2),
        in_specs=[_pair(seq, 5), one, zspec, _whole((1, 2 * HEAD)), one, one, one, one],
        out_specs=[_pair(seq, 5), one, one, _whole((1, 2 * HEAD))],
        out_shape=[SDS((5, t, DN_HEADS * HEAD), F32), out, out, SDS((1, 2 * HEAD), F32)],
        compiler_params=_params(("arbitrary", "arbitrary")))(loc, egl, u, gn, o, vn, states, dy)


def gdn_prep_bwd(u, cw, alog_row, dt_row, dloc, degl, *, seq, name):
    t = u.shape[0]
    nc = seq // DN_CHUNK

    def body(q_ref, k_ref, v_ref, ba_ref, cq_ref, ck_ref, cv_ref, alog_ref, dt_ref, dloc_ref, degl_ref,
             dqkv_ref, dba_ref, dcw_ref, dhs_ref, act_sc, b_sc, gc_sc, c_sc):
        hp = pl.program_id(0)

        @pl.when(pl.program_id(1) == 0)
        def _():
            dcw_ref[...] = jnp.zeros_like(dcw_ref)
            dhs_ref[...] = jnp.zeros_like(dhs_ref)

        pre_refs, cw_refs = (q_ref, k_ref, v_ref), (cq_ref, ck_ref, cv_ref)
        _gdn_inputs(pre_refs, cw_refs, ba_ref, alog_ref, dt_ref, hp, act_sc, b_sc, gc_sc, c_sc)

        def finish(rows, m, tt):
            q, k, v, b = m["q"], m["k"], m["v"], m["b"]
            dqg, dkd, du, dw, da = (dloc_ref[x, rows, :] for x in range(5))
            dm, eg = m["dm"], m["eg"]
            dt = _pk_nt(du, m["vb"]) + _pk_nt(dw, m["kbg"])
            dvb, dkbg = _pk_tn(tt, du), _pk_tn(tt, dw)
            yield
            dtt = _pk_nt(dt, tt, hi=True)
            yield
            dl = jnp.where(m["strict"], -_pk_tn(tt, dtt, hi=True), 0.0)
            yield
            dkk = dl * dm
            dqk = da * dm
            dd = dl * m["kk"] + da * m["qk"]
            dkb = _pk_nn(dkk, k) + dkbg * eg
            dq = _pk_nn(dqk, k) + dqg * eg
            yield
            dk = _pk_tn(dkk, m["kb"]) + _pk_tn(dqk, q) + dkd * m["ekd"] + dkb * b
            db = _half_sum(dkb * k + dvb * v)
            yield
            mx = jnp.where(m["tril"], dd * dm, 0.0)
            tk = _half_sum(dkd * m["kd"])
            colsum = jnp.where(m["eye"], jnp.broadcast_to(jnp.sum(mx, axis=0, keepdims=True), mx.shape), 0.0)
            dgc = _half_sum(mx) - _half_sum(colsum) + _half_sum(dqg * m["qg"] + dkbg * m["kbg"]) - tk
            dglast = jnp.sum(tk, axis=0, keepdims=True) + _row0(degl_ref[rows, :]) * jnp.exp(m["glast"])
            act_sc[0, rows, :] = dq
            act_sc[1, rows, :] = dk
            act_sc[2, rows, :] = dvb * b
            b_sc[rows, :] = db
            gc_sc[rows, :] = dgc + jnp.where(m["row"] == DN_CHUNK - 1, dglast, 0.0)

        _gdn_chunk_loop(nc, act_sc, b_sc, gc_sc, finish)

        beta, g, beta_blk, sp_arg, a_exp, g_blk = _gdn_gates(ba_ref, alog_ref, dt_ref, hp)
        dg = _chunk_rev_cumsum(gc_sc[...])
        lane = _iota2(beta_blk.shape, 1)
        ha = 2 * hp
        db = b_sc[...]
        at = lambda idx, x_a, x_b: (jnp.where(lane == idx, _lane_col(x_a, 0), 0.0)
                                    + jnp.where(lane == idx + 1, _lane_col(x_b, HEAD), 0.0))
        dg_blk = at(DN_HEADS + ha, dg, dg)
        dal = dg_blk * (-a_exp) * _sigmoid(sp_arg)
        dba_ref[...] = at(ha, db, db) * beta_blk * (1.0 - beta_blk) + dal
        dhs_ref[0:1, :] += jnp.sum(dg_blk * g_blk, axis=0, keepdims=True)
        dhs_ref[1:2, :] += jnp.sum(dal, axis=0, keepdims=True)
        for idx in range(3):
            c = c_sc[idx]
            _, sig, hat, r = _gdn_act(c, _DN_SCALE[idx])
            da_ = act_sc[idx]
            if _DN_SCALE[idx] is not None:
                da_ = da_ * _DN_SCALE[idx]
                da_ = r * (da_ - hat * _half_sum(da_ * hat))
            dx, dcw = _conv_bwd(da_ * (sig * (1.0 + c * (1.0 - sig))), pre_refs[idx][...],
                                [cw_refs[idx][k:k + 1, :] for k in range(4)])
            dqkv_ref[idx] = dx
            dcw_ref[idx] += dcw

    pair = DN_HEADS // 2
    in_specs = [_swap(s) for s in _gdn_in_specs(seq)] + [_swap(_pair(seq, 5)), _swap(_pair(seq))]
    return pl.pallas_call(
        body, name=name, grid=(pair, t // seq), in_specs=in_specs,
        out_specs=[_swap(_pair(seq, 3)), pl.BlockSpec((None, seq, 2 * HEAD), lambda hp, b: (hp, b, 0)),
                   pl.BlockSpec((3, 4, 2 * HEAD), lambda hp, b: (0, 0, hp)),
                   pl.BlockSpec((None, 2, 2 * HEAD), lambda hp, b: (hp, 0, 0))],
        out_shape=[SDS((3, t, DN_HEADS * HEAD), F32), SDS((pair, t, 2 * HEAD), F32), SDS((3, 4, DN_HEADS * HEAD), F32),
                   SDS((pair, 2, 2 * HEAD), F32)],
        scratch_shapes=[pltpu.VMEM((3, seq, 2 * HEAD), F32)] + [pltpu.VMEM((seq, 2 * HEAD), F32)] * 2
        + [pltpu.VMEM((3, seq, 2 * HEAD), F32)],
        compiler_params=_params(("arbitrary", "arbitrary")))(u, u, u, u, cw, cw, cw, alog_row, dt_row, dloc, degl)


def mix_out(y_lru, o, y_dn, w_out, h, *, name, tm=512):
    t, d = h.shape
    tm = min(tm, t)

    def body(a_ref, b_ref, c_ref, w_ref, h_ref, o_ref, y_ref):
        y_ref[:, 0:LRU_W] = a_ref[...].astype(BF16)
        y_ref[:, LRU_W:LRU_W + ATT_W] = b_ref[...].astype(BF16)
        y_ref[:, LRU_W + ATT_W:] = c_ref[...].astype(BF16)
        o_ref[...] = h_ref[...] + _nn(y_ref[...], w_ref[...])

    rows = lambda width: pl.BlockSpec((tm, width), lambda i: (i, 0))
    return pl.pallas_call(
        body, name=name, grid=(t // tm,), in_specs=[rows(LRU_W), rows(ATT_W), rows(LRU_W), _whole((d, d)), rows(d)],
        out_specs=[rows(d), rows(d)], out_shape=[SDS((t, d), F32), SDS((t, d), BF16)],
        compiler_params=_params(("parallel",)))(y_lru, o, y_dn, w_out, h)


def mix_out_bwd(dout, w_out, *, name, tm=512):
    t, d = dout.shape
    tm = min(tm, t)

    def body(d_ref, w_ref, a_ref, b_ref, c_ref):
        dy = _nt(d_ref[...], w_ref[...])
        a_ref[...] = dy[:, 0:LRU_W]
        b_ref[...] = dy[:, LRU_W:LRU_W + ATT_W]
        c_ref[...] = dy[:, LRU_W + ATT_W:]

    rows = lambda width: pl.BlockSpec((tm, width), lambda i: (i, 0))
    return pl.pallas_call(
        body, name=name, grid=(t // tm,), in_specs=[rows(d), _whole((d, d))], out_specs=[rows(LRU_W), rows(ATT_W), rows(LRU_W)],
        out_shape=[SDS((t, LRU_W), F32), SDS((t, ATT_W), F32), SDS((t, LRU_W), F32)],
        compiler_params=_params(("parallel",)))(dout, w_out)


def mix_in_bwd(h, gain, dout, w_in, dx, dgate, dq, dk, dv, dqkv, dz, dba, *, name, tm=512):
    t, d = h.shape
    tm = min(tm, t)

    def body(h_ref, g_ref, do_ref, w_ref, dx_ref, dgate_ref, dq_ref, dk_ref, dv_ref, dqkv_ref, dz_ref, dba_ref,
             dh_ref, dg_ref, du_ref):
        @pl.when(pl.program_id(0) == 0)
        def _():
            dg_ref[...] = jnp.zeros_like(dg_ref)

        off = 0
        for piece in (dx_ref[...], dgate_ref[...], dq_ref[...], dk_ref[...], dv_ref[...], dqkv_ref[0], dqkv_ref[1],
                      dqkv_ref[2], dz_ref[...], dba_ref[0] + dba_ref[1]):
            du_ref[:, off:off + piece.shape[1]] = piece.astype(BF16)
            off += piece.shape[1]
        du_ref[:, off:] = jnp.zeros((tm, D_IN_PAD - off), BF16)
        g = g_ref[...]
        _, xh, r = _rms_fwd(h_ref[...], g)
        dh, dg = _rms_bwd(_nt(du_ref[...], w_ref[...]), xh, r, g)
        dh_ref[...] = do_ref[...] + dh
        dg_ref[...] += dg

    rows = lambda width: pl.BlockSpec((tm, width), lambda i: (i, 0))
    return pl.pallas_call(
        body, name=name, grid=(t // tm,),
        in_specs=[rows(d), _whole((1, d)), rows(d), _whole((d, D_IN_PAD)), rows(LRU_W), rows(LRU_W), rows(ATT_W),
                  rows(2 * HEAD), rows(2 * HEAD), pl.BlockSpec((3, tm, DN_HEADS * HEAD), lambda i: (0, i, 0)),
                  rows(DN_HEADS * HEAD), pl.BlockSpec((2, tm, 2 * HEAD), lambda i: (0, i, 0))],
        out_specs=[rows(d), _whole((1, d)), rows(D_IN_PAD)],
        out_shape=[SDS((t, d), F32), SDS((1, d), F32), SDS((t, D_IN_PAD), BF16)],
        compiler_params=_params(("arbitrary",)))(h, gain, dout, w_in, dx, dgate, dq, dk, dv, dqkv, dz, dba)


def _block_diag(w):
    out = jnp.zeros((LRU_W, LRU_W), w.dtype)
    for h in range(LRU_W // HEAD):
        out = lax.dynamic_update_slice(out, w[h], (h * HEAD, h * HEAD))
    return out


def _diag_blocks(w):
    per = LRU_HALF // HEAD
    return jnp.stack([w[h // per, (h % per) * HEAD:(h % per + 1) * HEAD, (h % per) * HEAD:(h % per + 1) * HEAD]
                      for h in range(LRU_W // HEAD)])


def layer_params(w, wl, l, bias):
    row = lambda a: a[l].reshape(1, -1)
    return dict(
        ffn1_norm=row(w["ffn1_norm"]), ffn1=(wl["ffn1_w_gate"], wl["ffn1_w_up"], wl["ffn1_w_down"]),
        mix_norm=row(w["mix_norm"]) + wl["tie1"][0:1, 0:1], w_in=wl["w_in"],
        lru=(wl["lru_conv_w"], row(w["lru_conv_b"]), _block_diag(w["lru_w_a"][l]), row(w["lru_b_a"]),
             _block_diag(w["lru_w_x"][l]), row(w["lru_b_x"]), row(w["lru_lambda"])),
        bias=bias, sink_rows=jnp.repeat(w["attn_sinks"][l], BLOCK_Q).reshape(ATT_HEADS * BLOCK_Q, 1),
        dn_cw=wl["dn_conv_w"], dn_alog=_ba_row(w["dn_a_log"][l]), dn_dt=_ba_row(w["dn_dt_bias"][l]),
        dn_norm=jnp.tile(row(w["dn_norm"]), (1, 2)), w_out=wl["w_out"],
        ffn2_norm=row(w["ffn2_norm"]), ffn2=(wl["ffn2_w_gate"], wl["ffn2_w_up"], wl["ffn2_w_down"]),
        ple_norm=row(w["ple_norm"]), ple_w_gate=wl["ple_w_gate"], ple_w_proj=wl["ple_w_proj"])


def _ba_row(per_head):
    return jnp.pad(per_head, (DN_HEADS, 2 * HEAD - 2 * DN_HEADS)).reshape(1, 2 * HEAD)


def mixer_fwd(h, p, nb, seq, tag):
    u, n = norm_matmul(h, p["mix_norm"], p["w_in"], name=f"mix_in_{tag}")
    y_lru = lru_fwd(u, *p["lru"], seq=seq, name=f"lru_fwd_{tag}")
    o = swa_fwd(u, p["bias"], p["sink_rows"], seq=seq, name=f"swa_fwd_{tag}")
    loc, egl = gdn_prep(u, p["dn_cw"], p["dn_alog"], p["dn_dt"], seq=seq, name=f"gdn_prep_{tag}")
    y_dn, o_raw, vn, st = gdn_scan(loc, egl, u, p["dn_norm"], seq=seq, name=f"gdn_scan_{tag}")
    out, ycat = mix_out(y_lru, o, y_dn, p["w_out"], h, name=f"mix_out_{tag}")
    return out, dict(h=h, u=u, n=n, loc=loc, egl=egl, o_raw=o_raw, vn=vn, st=st, ycat=ycat)


def mixer_bwd(dout, s, p, nb, seq, tag):
    u = s["u"]
    dy_lru, do, dy_dn = mix_out_bwd(dout, p["w_out"], name=f"mix_out_dx_{tag}")
    g = {"w_out": matmul(s["ycat"], dout, ta=True, tm=1024, name=f"mix_out_dw_{tag}")}
    dx, dgate, dcw, dwa, dwx, dvec = lru_bwd(u, *p["lru"], dy_lru, seq=seq, name=f"lru_bwd_{tag}")
    g.update(lru_conv_w=dcw, lru_conv_b=dvec[0], lru_w_a=_diag_blocks(dwa), lru_b_a=dvec[1], lru_w_x=_diag_blocks(dwx),
             lru_b_x=dvec[2], lru_lambda=dvec[3])
    dq, dk, dv, dbias, dsink = swa_bwd(u, p["bias"], p["sink_rows"], do, seq=seq, name=f"swa_bwd_{tag}")
    g.update(attn_sinks=dsink.reshape(ATT_HEADS, BLOCK_Q).sum(axis=1), bias=dbias)
    dloc, degl, dz, dgn = gdn_scan_bwd(s["loc"], s["egl"], u, p["dn_norm"], s["o_raw"], s["vn"], s["st"], dy_dn, seq=seq,
                                       name=f"gdn_scan_bwd_{tag}")
    dqkv, dba, dcw3, dhs = gdn_prep_bwd(u, p["dn_cw"], p["dn_alog"], p["dn_dt"], dloc, degl, seq=seq,
                                        name=f"gdn_prep_bwd_{tag}")
    dhs = dhs.sum(axis=0)[:, DN_HEADS:2 * DN_HEADS]
    g.update(dn_conv_w=dcw3.transpose(1, 0, 2).reshape(4, 3 * DN_HEADS * HEAD), dn_a_log=dhs[0], dn_dt_bias=dhs[1],
             dn_norm=dgn[0, :HEAD] + dgn[0, HEAD:])
    dh, dgain, du = mix_in_bwd(s["h"], p["mix_norm"], dout, p["w_in"], dx, dgate, dq, dk, dv, dqkv, dz, dba,
                               name=f"mix_in_bwd_{tag}")
    g["w_in"] = matmul(s["n"], du, ta=True, tm=1024, tn=640, name=f"mix_in_dw_{tag}")
    g["mix_norm"] = dgain[0]
    return dh, g


SHARDED = ("ffn1_w_gate", "ffn1_w_up", "ffn1_w_down", "w_in", "w_out", "ffn2_w_gate", "ffn2_w_up", "ffn2_w_down",
           "ple_w_gate", "ple_w_proj")
PER_LAYER_SMALL = ("ffn1_norm", "mix_norm", "lru_conv_w", "lru_conv_b", "lru_w_a", "lru_b_a", "lru_w_x", "lru_b_x",
                   "lru_lambda", "attn_sinks", "dn_conv_w", "dn_a_log", "dn_dt_bias", "dn_norm", "ffn2_norm", "ple_norm")


GRAD_PARTS = (("ple_w_gate", "ple_w_proj", "ffn2_w_gate", "ffn2_w_up", "ffn2_w_down"), ("w_in", "w_out"),
              ("ffn1_w_gate", "ffn1_w_up", "ffn1_w_down"))
WEIGHT_PARTS = (("ffn1_w_gate", "ffn1_w_up", "ffn1_w_down"),
                ("w_in", "w_out", "ffn2_w_gate", "ffn2_w_up", "ffn2_w_down", "ple_w_gate", "ple_w_proj", "lru_conv_w",
                 "dn_conv_w"))


def _col_shards(a):
    r, c = a.shape
    return a.reshape(r, N_CHIP, c // N_CHIP).transpose(1, 0, 2)


def local_step(x, p, target, w, layer_weights, layer_grads, bmap, nb, seq):
    bias = relbias_fwd(w["rel_bias"], bmap, name="relbias_fwd")
    h, saved = x, []
    for l in range(N_LAYER):
        wl = layer_weights(l, 0, h)
        s = dict(h0=h)
        h, *s["ffn1"] = ffn_fwd(h, w["ffn1_norm"][l].reshape(1, -1) + wl["tie0"][0:1, 0:1], wl["ffn1_w_gate"],
                                wl["ffn1_w_up"], wl["ffn1_w_down"], name=f"ffn1_fwd_{l}")
        wl.update(layer_weights(l, 1, h))
        pr = layer_params(w, wl, l, bias)
        h, s["mix"] = mixer_fwd(h, pr, nb, seq, l)
        s["h2"] = h
        h, *s["ffn2"] = ffn_fwd(h, pr["ffn2_norm"], *pr["ffn2"], name=f"ffn2_fwd_{l}")
        s["h3"] = h
        h = ple_fwd(h, pr["ple_norm"], pr["ple_w_gate"], p[l], pr["ple_w_proj"], name=f"ple_fwd_{l}")
        saved.append((pr, s))
    dh, dgf, loss = loss_head(h, w["final_norm"].reshape(1, -1), target, name="loss_head")

    per_layer, dbias, token = [None] * N_LAYER, None, None
    for l in reversed(range(N_LAYER)):
        pr, s = saved[l]
        g = {}
        dout = dh
        ple_norm = pr["ple_norm"] if token is None else pr["ple_norm"] + token[0:1, 0:1]
        dh, n, dga, dpp, dg = ple_bwd(s["h3"], ple_norm, pr["ple_w_gate"], p[l], pr["ple_w_proj"], dout, name=f"ple_bwd_{l}")
        g["ple_norm"] = dg[0]
        g["ple_w_gate"] = matmul(n, dga, ta=True, tm=1024, name=f"ple_dwg_{l}").reshape(N_CHIP, -1, D_MODEL)
        g["ple_w_proj"] = _col_shards(matmul(p[l], dpp, ta=True, name=f"ple_dwp_{l}"))
        for nm, hin in (("ffn2", s["h2"]), ("ffn1", s["h0"])):
            if nm == "ffn1":
                lru = list(pr["lru"])
                lru[1] = lru[1] + token[0:1, 0:1]
                dh, gm = mixer_bwd(dh, s["mix"], dict(pr, lru=tuple(lru)), nb, seq, l)
                dbias = gm.pop("bias") if dbias is None else dbias + gm.pop("bias")
                gm["w_in"] = _col_shards(gm["w_in"][:, :D_IN])
                gm["w_out"] = gm["w_out"].reshape(N_CHIP, -1, D_MODEL)
                g.update(gm)
                token = layer_grads(l, 1, {k: g.pop(k) for k in GRAD_PARTS[1]}, dh)
            dout = dh
            n, a, b = s[nm]
            dh, da, db, sact, dg = ffn_bwd_act(hin, pr[nm + "_norm"] + token[0:1, 0:1] if nm == "ffn1" else pr[nm + "_norm"],
                                               dout, a, b, *pr[nm], name=f"{nm}_bwd_act_{l}")
            g[nm + "_norm"] = dg[0]
            g[nm + "_w_gate"], g[nm + "_w_up"], g[nm + "_w_down"] = ffn_bwd_w(n, da, db, sact, dout, name=f"{nm}_bwd_w_{l}")
            part = 0 if nm == "ffn2" else 2
            token = layer_grads(l, part, {k: g.pop(k) for k in GRAD_PARTS[part]}, dh)
        per_layer[l] = g
    grads = {k: jnp.stack([per_layer[l][k] for l in range(N_LAYER)]) for k in PER_LAYER_SMALL}
    grads["rel_bias"] = relbias_bwd(dbias, bmap, name="relbias_bwd")[:, :ATT_HEADS]
    grads["final_norm"] = dgf[0]
    return loss, dh, grads


HBM_SPEC = pl.BlockSpec(memory_space=pltpu.HBM)


def _place():
    x, y, c = lax.axis_index("x"), lax.axis_index("y"), lax.axis_index("c")
    chips = [(1 - x, y), (x, 1 - y), (1 - x, 1 - y)]
    return x, y, c, 2 * x + y, (x, y, 1 - c), chips, [2 * cx + cy for cx, cy in chips]


def _remote(src, dst, send_sem, recv_sem, to):
    return pltpu.make_async_remote_copy(src_ref=src, dst_ref=dst, send_sem=send_sem, recv_sem=recv_sem, device_id=to,
                                        device_id_type=MESH)


N_DEV = 8


def allreduce_small(buf, *, name):
    rows = buf.shape[0]

    def body(in_ref, out_ref, gath, send, recv):
        x, y, c = lax.axis_index("x"), lax.axis_index("y"), lax.axis_index("c")
        mine = 4 * x + 2 * y + c
        gath[mine] = in_ref[...]
        cps = []
        for k in range(1, N_DEV):
            to = (x ^ (k >> 2), y ^ ((k >> 1) & 1), c ^ (k & 1))
            cps.append(_remote(in_ref, gath.at[mine], send.at[k - 1], recv.at[k - 1], to))
            cps[-1].start()
        for k in range(1, N_DEV):
            theirs = gath.at[4 * (x ^ (k >> 2)) + 2 * (y ^ ((k >> 1) & 1)) + (c ^ (k & 1))]
            _remote(theirs, theirs, send.at[k - 1], recv.at[k - 1], (x, y, c)).wait_recv()
        for cp in cps:
            cp.wait_send()
        acc = gath[0]
        for d in range(1, N_DEV):
            acc = acc + gath[d]
        out_ref[...] = acc

    vm = pl.BlockSpec(memory_space=pltpu.VMEM)
    return pl.pallas_call(
        body, name=name, in_specs=[vm], out_specs=vm, out_shape=SDS(buf.shape, F32),
        scratch_shapes=[pltpu.VMEM((N_DEV, rows, 128), F32), pltpu.SemaphoreType.DMA((N_DEV - 1,)),
                        pltpu.SemaphoreType.DMA((N_DEV - 1,))])(buf)


SEM_SPEC = pl.BlockSpec(memory_space=pltpu.SEMAPHORE)
ANY_SPEC = pl.BlockSpec(memory_space=pl.ANY)
DATAFLOW = pltpu.SideEffectType.DATAFLOW_SIDE_EFFECTING


def _in_hbm(a):
    return pltpu.with_memory_space_constraint(a, pltpu.HBM)


def _my_rows(ref_rows, c, mine=True):
    half = ref_rows // 2
    start = (c if mine else 1 - c) * half
    return pl.ds(pl.multiple_of(start, 8), half)


def place_layer_shard(w, layer, chip_arr, dtype, after, *, name):
    _, r, c = w.shape
    tr = next(cand for cand in (256, 128, 64, 32, 16, 8, r) if r % cand == 0)

    def body(chip_ref, w_ref, after_ref, o_ref):
        o_ref[...] = w_ref[...].astype(dtype)

    return pl.pallas_call(
        body, name=name,
        grid_spec=pltpu.PrefetchScalarGridSpec(
            num_scalar_prefetch=1, grid=(r // tr,),
            in_specs=[pl.BlockSpec((None, tr, c), lambda i, chip: (layer, i, 0)), ANY_SPEC],
            out_specs=pl.BlockSpec((None, tr, c), lambda i, chip: (chip[0], i, 0))),
        out_shape=SDS((N_CHIP, r, c), dtype), compiler_params=_params(("parallel",)))(chip_arr, w, after)


def _gather_pieces(refs, n_split, c, me, cids):
    mine, theirs = [], []
    for k, ref in enumerate(refs):
        if k < n_split:
            rows = _my_rows(ref.shape[1], c)
            mine.append(ref.at[me, rows])
            theirs.append([ref.at[cid, rows] for cid in cids])
        else:
            mine.append(ref.at[me])
            theirs.append([ref.at[cid] for cid in cids])
    return mine, theirs


def gather_start(bufs, n_split, after, *, name):
    n = len(bufs)

    def body(*refs):
        ins, send, recv, token = refs[:n], refs[n + 1], refs[n + 2], refs[-1]
        x, y, c, me, sib, chips, cids = _place()
        mine, _ = _gather_pieces(ins, n_split, c, me, cids)
        for k in range(n):
            for j, chip in enumerate(chips):
                _remote(mine[k], mine[k], send.at[3 * k + j], recv.at[3 * k + j], (*chip, c)).start()
        token[...] = jnp.zeros_like(token)

    out = pl.pallas_call(
        body, name=name, in_specs=[HBM_SPEC] * n + [ANY_SPEC],
        out_specs=[SEM_SPEC, SEM_SPEC] + [HBM_SPEC] * n + [pl.BlockSpec(memory_space=pltpu.VMEM)],
        out_shape=[pltpu.SemaphoreType.DMA((3 * n,)), pltpu.SemaphoreType.DMA((3 * n,))]
        + [pltpu.HBM(b.shape, b.dtype) for b in bufs] + [SDS((8, 128), F32)],
        input_output_aliases={k: k + 2 for k in range(n)},
        compiler_params=pltpu.CompilerParams(has_side_effects=DATAFLOW))(*[_in_hbm(b) for b in bufs], after)
    return out[0], out[1], list(out[2:2 + n]), out[-1]


def gather_wait(send, recv, bufs, n_split, after, *, name):
    n = len(bufs)

    def body(*refs):
        ins, send_ref, recv_ref = refs[:n], refs[n], refs[n + 1]
        x, y, c, me, sib, chips, cids = _place()
        mine, theirs = _gather_pieces(ins, n_split, c, me, cids)
        for k in range(n):
            for j in range(3):
                _remote(mine[k], mine[k], send_ref.at[3 * k + j], recv_ref.at[3 * k + j], sib).wait_send()
                _remote(theirs[k][j], theirs[k][j], send_ref.at[3 * k + j], recv_ref.at[3 * k + j], sib).wait_recv()

    return list(pl.pallas_call(
        body, name=name, in_specs=[HBM_SPEC] * n + [SEM_SPEC, SEM_SPEC, ANY_SPEC], out_specs=[HBM_SPEC] * n,
        out_shape=[pltpu.HBM(b.shape, b.dtype) for b in bufs], input_output_aliases={k: k for k in range(n)},
        compiler_params=pltpu.CompilerParams(has_side_effects=DATAFLOW))(*bufs, send, recv, after))


def gather_forward(bufs, *, name):
    n = len(bufs)

    def body(*refs):
        outs, (send, recv) = refs[n:2 * n], refs[2 * n:]
        x, y, c, me, sib, chips, cids = _place()
        cps = []
        for k in range(n):
            for j in range(3):
                piece = outs[k].at[cids[j], _my_rows(outs[k].shape[1], c)]
                cps.append(_remote(piece, piece, send.at[3 * k + j], recv.at[3 * k + j], sib))
                cps[-1].start()
        for k in range(n):
            for j in range(3):
                piece = outs[k].at[cids[j], _my_rows(outs[k].shape[1], c, mine=False)]
                _remote(piece, piece, send.at[3 * k + j], recv.at[3 * k + j], sib).wait_recv()
        for cp in cps:
            cp.wait_send()

    return list(pl.pallas_call(
        body, name=name, in_specs=[HBM_SPEC] * n, out_specs=[HBM_SPEC] * n, out_shape=[SDS(b.shape, b.dtype) for b in bufs],
        input_output_aliases={k: k for k in range(n)}, scratch_shapes=[pltpu.SemaphoreType.DMA((3 * n,))] * 2)(*bufs))


def _exchange_copies(ins, lands, send, recv, c, sib):
    return [_remote(ins[k].at[pl.ds(0, N_CHIP), _my_rows(ins[k].shape[1], c, mine=False)], lands[k], send.at[k],
                    recv.at[k], sib) for k in range(len(ins))]


def exchange_start(gs, *, name):
    n = len(gs)

    def body(*refs):
        ins, lands, send, recv, token = refs[:n], refs[n:2 * n], refs[2 * n], refs[2 * n + 1], refs[-1]
        x, y, c, me, sib, chips, cids = _place()
        for cp in _exchange_copies(ins, lands, send, recv, c, sib):
            cp.start()
        token[...] = jnp.zeros_like(token)

    lands = [_in_hbm(lax.empty((N_CHIP, g.shape[1] // 2, g.shape[2]), g.dtype)) for g in gs]
    out = pl.pallas_call(
        body, name=name, in_specs=[HBM_SPEC] * (2 * n),
        out_specs=[SEM_SPEC, SEM_SPEC] + [HBM_SPEC] * (2 * n) + [pl.BlockSpec(memory_space=pltpu.VMEM)],
        out_shape=[pltpu.SemaphoreType.DMA((n,)), pltpu.SemaphoreType.DMA((n,))]
        + [pltpu.HBM(b.shape, b.dtype) for b in list(gs) + lands] + [SDS((8, 128), F32)],
        input_output_aliases={k: k + 2 for k in range(2 * n)},
        compiler_params=pltpu.CompilerParams(has_side_effects=DATAFLOW))(*[_in_hbm(g) for g in gs], *lands)
    return out[0], out[1], list(out[2:2 + n]), list(out[2 + n:2 + 2 * n]), out[-1]


def exchange_wait(send, recv, gs, lands, after, *, name):
    n = len(gs)

    def body(*refs):
        ins, land_refs, send_ref, recv_ref = refs[:n], refs[n:2 * n], refs[2 * n], refs[2 * n + 1]
        x, y, c, me, sib, chips, cids = _place()
        for cp in _exchange_copies(ins, land_refs, send_ref, recv_ref, c, sib):
            cp.wait_send()
            cp.wait_recv()

    out = pl.pallas_call(
        body, name=name, in_specs=[HBM_SPEC] * (2 * n) + [SEM_SPEC, SEM_SPEC, ANY_SPEC], out_specs=[HBM_SPEC] * (2 * n),
        out_shape=[pltpu.HBM(b.shape, b.dtype) for b in list(gs) + list(lands)],
        input_output_aliases={k: k for k in range(2 * n)},
        compiler_params=pltpu.CompilerParams(has_side_effects=DATAFLOW))(*gs, *lands, send, recv, after)
    return list(out[:n]), list(out[n:])


def _half_tile(half):
    return next(cand for cand in (256, 176, 128, 64, 32, 16) if half % cand == 0)


def reduce_add(g, r, c_arr, *, name):
    _, rows, cdim = g.shape
    half = rows // 2
    tr = _half_tile(half)

    def body(c_ref, g_ref, r_ref, o_ref):
        o_ref[...] = (g_ref[...] + r_ref[...]).astype(o_ref.dtype)

    return pl.pallas_call(
        body, name=name,
        grid_spec=pltpu.PrefetchScalarGridSpec(
            num_scalar_prefetch=1, grid=(N_CHIP, half // tr),
            in_specs=[pl.BlockSpec((None, tr, cdim), lambda j, i, c: (j, c[0] * (half // tr) + i, 0)),
                      pl.BlockSpec((None, tr, cdim), lambda j, i, c: (j, i, 0))],
            out_specs=pl.BlockSpec((None, tr, cdim), lambda j, i, c: (j, i, 0))),
        out_shape=SDS((N_CHIP, half, cdim), BF16), compiler_params=_params(("parallel", "parallel")))(c_arr, g, r)


def reduce_start(ss, *, name):
    n = len(ss)

    def body(*refs):
        ins, lands, send, recv, token = refs[:n], refs[n:2 * n], refs[2 * n], refs[2 * n + 1], refs[-1]
        x, y, c, me, sib, chips, cids = _place()
        for k in range(n):
            for j, chip in enumerate(chips):
                _remote(ins[k].at[cids[j]], lands[k].at[j], send.at[3 * k + j], recv.at[3 * k + j], (*chip, c)).start()
        token[...] = jnp.zeros_like(token)

    lands = [_in_hbm(lax.empty((N_CHIP - 1,) + s.shape[1:], s.dtype)) for s in ss]
    out = pl.pallas_call(
        body, name=name, in_specs=[HBM_SPEC] * (2 * n),
        out_specs=[SEM_SPEC, SEM_SPEC] + [HBM_SPEC] * (2 * n) + [pl.BlockSpec(memory_space=pltpu.VMEM)],
        out_shape=[pltpu.SemaphoreType.DMA((3 * n,)), pltpu.SemaphoreType.DMA((3 * n,))]
        + [pltpu.HBM(b.shape, b.dtype) for b in list(ss) + lands] + [SDS((8, 128), F32)],
        input_output_aliases={k: k + 2 for k in range(2 * n)},
        compiler_params=pltpu.CompilerParams(has_side_effects=DATAFLOW))(*[_in_hbm(s) for s in ss], *lands)
    return out[0], out[1], list(out[2:2 + n]), list(out[2 + n:2 + 2 * n]), out[-1]


def reduce_wait(send, recv, ss, lands, after, *, name):
    n = len(ss)

    def body(*refs):
        ins, land_refs, send_ref, recv_ref = refs[:n], refs[n:2 * n], refs[2 * n], refs[2 * n + 1]
        x, y, c, me, sib, chips, cids = _place()
        for k in range(n):
            for j in range(3):
                _remote(ins[k].at[cids[j]], land_refs[k].at[j], send_ref.at[3 * k + j], recv_ref.at[3 * k + j],
                        sib).wait_send()
                _remote(ins[k].at[cids[j]], land_refs[k].at[j], send_ref.at[3 * k + j], recv_ref.at[3 * k + j],
                        sib).wait_recv()

    out = pl.pallas_call(
        body, name=name, in_specs=[HBM_SPEC] * (2 * n) + [SEM_SPEC, SEM_SPEC, ANY_SPEC], out_specs=[HBM_SPEC] * (2 * n),
        out_shape=[pltpu.HBM(b.shape, b.dtype) for b in list(ss) + list(lands)],
        input_output_aliases={k: k for k in range(2 * n)},
        compiler_params=pltpu.CompilerParams(has_side_effects=DATAFLOW))(*ss, *lands, send, recv, after)
    return list(out[:n]), list(out[n:])


def reduce_sum(own, land, place_arr, layer, acc, *, name):
    _, half, cdim = land.shape
    tr = _half_tile(half)

    def body(p_ref, own_ref, land_ref, *rest):
        o_ref = rest[-1]
        o_ref[...] = ((own_ref[...].astype(F32) + land_ref[0].astype(F32)) + land_ref[1].astype(F32)) + land_ref[2].astype(F32)

    in_specs = [pl.BlockSpec((None, tr, cdim), lambda i, p: (p[0], i, 0)),
                pl.BlockSpec((N_CHIP - 1, tr, cdim), lambda i, p: (0, i, 0))]
    args = [place_arr, own, land]
    if acc is not None:
        in_specs.append(ANY_SPEC)
        args.append(acc)
    return pl.pallas_call(
        body, name=name,
        grid_spec=pltpu.PrefetchScalarGridSpec(
            num_scalar_prefetch=1, grid=(half // tr,), in_specs=in_specs,
            out_specs=pl.BlockSpec((None, tr, cdim), lambda i, p: (layer, p[1] * (half // tr) + i, 0))),
        out_shape=SDS((N_LAYER, 2 * half, cdim), F32), input_output_aliases={} if acc is None else {3: 0},
        compiler_params=_params(("parallel",)))(*args)


def reduce_share(fs, *, name):
    n = len(fs)

    def body(*refs):
        outs, (send, recv) = refs[n:2 * n], refs[2 * n:]
        x, y, c, me, sib, chips, cids = _place()
        cps = []
        for k in range(n):
            piece = outs[k].at[pl.ds(0, N_LAYER), _my_rows(outs[k].shape[1], c)]
            cps.append(_remote(piece, piece, send.at[k], recv.at[k], sib))
            cps[-1].start()
        for k in range(n):
            theirs = outs[k].at[pl.ds(0, N_LAYER), _my_rows(outs[k].shape[1], c, mine=False)]
            _remote(theirs, theirs, send.at[k], recv.at[k], sib).wait_recv()
        for cp in cps:
            cp.wait_send()

    return list(pl.pallas_call(
        body, name=name, in_specs=[HBM_SPEC] * n, out_specs=[HBM_SPEC] * n, out_shape=[SDS(f.shape, f.dtype) for f in fs],
        input_output_aliases={k: k for k in range(n)}, scratch_shapes=[pltpu.SemaphoreType.DMA((n,))] * 2)(*fs))


WEIGHTS = ("ffn1_norm", "ffn1_w_gate", "ffn1_w_up", "ffn1_w_down", "mix_norm", "w_in", "lru_conv_w", "lru_conv_b", "lru_w_a",
           "lru_b_a", "lru_w_x", "lru_b_x", "lru_lambda", "attn_sinks", "rel_bias", "dn_conv_w", "dn_a_log", "dn_dt_bias",
           "dn_norm", "w_out", "ffn2_norm", "ffn2_w_gate", "ffn2_w_up", "ffn2_w_down", "ple_norm", "ple_w_gate",
           "ple_w_proj", "final_norm")
CONV_SHARDED = ("lru_conv_w", "dn_conv_w")
FFN_TRANSPOSED = ("ffn1_w_gate", "ffn1_w_up", "ffn2_w_gate", "ffn2_w_up")
SMALL = tuple(k for k in WEIGHTS if k not in SHARDED)


def _pack(arrs):
    blocks = []
    for a in arrs:
        v = a.reshape(-1)
        blocks.append(jnp.pad(v, (0, -v.shape[0] % 1024)).reshape(-1, 128))
    return jnp.concatenate(blocks, axis=0)


def _unpack(buf, shapes):
    out, off = [], 0
    for s in shapes:
        n = int(np.prod(s))
        rows = 8 * -(-n // 1024)
        out.append(buf[off:off + rows].reshape(-1)[:n].reshape(s))
        off += rows
    return out


def kernel(x, p, ffn1_norm, ffn1_w_gate, ffn1_w_up, ffn1_w_down, mix_norm, w_in, lru_conv_w, lru_conv_b, lru_w_a, lru_b_a, lru_w_x, lru_b_x, lru_lambda, attn_sinks, rel_bias, dn_conv_w, dn_a_log, dn_dt_bias, dn_norm, w_out, ffn2_norm, ffn2_w_gate, ffn2_w_up, ffn2_w_down, ple_norm, ple_w_gate, ple_w_proj, final_norm, loss_target, m_ffn1_norm, m_ffn1_w_gate, m_ffn1_w_up, m_ffn1_w_down, m_mix_norm, m_w_in, m_lru_conv_w, m_lru_conv_b, m_lru_w_a, m_lru_b_a, m_lru_w_x, m_lru_b_x, m_lru_lambda, m_attn_sinks, m_rel_bias, m_dn_conv_w, m_dn_a_log, m_dn_dt_bias, m_dn_norm, m_w_out, m_ffn2_norm, m_ffn2_w_gate, m_ffn2_w_up, m_ffn2_w_down, m_ple_norm, m_ple_w_gate, m_ple_w_proj, m_final_norm, v_ffn1_norm, v_ffn1_w_gate, v_ffn1_w_up, v_ffn1_w_down, v_mix_norm, v_w_in, v_lru_conv_w, v_lru_conv_b, v_lru_w_a, v_lru_b_a, v_lru_w_x, v_lru_b_x, v_lru_lambda, v_attn_sinks, v_rel_bias, v_dn_conv_w, v_dn_a_log, v_dn_dt_bias, v_dn_norm, v_w_out, v_ffn2_norm, v_ffn2_w_gate, v_ffn2_w_up, v_ffn2_w_down, v_ple_norm, v_ple_w_gate, v_ple_w_proj, v_final_norm):
    given = dict(locals())
    stored = lambda k, a: jnp.swapaxes(a, 1, 2) if k in FFN_TRANSPOSED else a
    ws = {k: stored(k, given[k]) for k in WEIGHTS}
    ms = {k: stored(k, given["m_" + k]) for k in WEIGHTS}
    vs = {k: stored(k, given["v_" + k]) for k in WEIGHTS}
    nb, seq, d = x.shape
    t = nb * seq
    cx, cy, cc = lax.axis_index("x"), lax.axis_index("y"), lax.axis_index("c")
    chip = 2 * cx + cy

    chip_arr = chip.astype(jnp.int32).reshape(1)
    c_arr = cc.astype(jnp.int32).reshape(1)
    place_arr = jnp.stack([chip, cc]).astype(jnp.int32)
    groups = [(l, part) for l in range(N_LAYER) for part in range(len(WEIGHT_PARTS))]
    placed, started = {}, {}

    def place_group(i, after):
        l, part = groups[i]
        for k in WEIGHT_PARTS[part]:
            placed[l, k] = place_layer_shard(ws[k], l, chip_arr, F32 if k in CONV_SHARDED else BF16, after,
                                             name=f"place_{k}_{l}")

    def start_group(i, after):
        l, part = groups[i]
        ks = WEIGHT_PARTS[part]
        n_split = sum(k in SHARDED for k in ks)
        started[i] = (ks, n_split) + gather_start([placed[l, k] for k in ks], n_split, after, name=f"gather_start_{l}_{part}")

    place_group(0, jnp.zeros((8, 128), F32))
    start_group(0, jnp.zeros((8, 128), F32))
    for i in range(1, len(groups)):
        place_group(i, started[0][-1])

    def layer_weights(l, part, h):
        i = groups.index((l, part))
        ks, n_split, send, recv, bufs, _ = started[i]
        bufs = gather_wait(send, recv, bufs, n_split, h, name=f"gather_wait_{l}_{part}")
        tie = jnp.zeros((8, 128), F32)
        for nxt in [j for j in range(i + 1, len(groups)) if j not in started and groups[j][0] == groups[min(i + 1, len(groups) - 1)][0]]:
            start_group(nxt, bufs[0] if nxt == i + 1 else started[nxt - 1][-1])
            tie = started[nxt][-1]
        wl = dict(zip(ks, gather_forward(bufs[:n_split], name=f"gather_forward_{l}_{part}") + bufs[n_split:]))
        for k in ("w_in", "ple_w_proj", "lru_conv_w", "dn_conv_w"):
            if k in wl:
                wl[k] = wl[k].transpose(1, 0, 2).reshape(wl[k].shape[1], -1)
        for k in ("w_out", "ple_w_gate"):
            if k in wl:
                wl[k] = wl[k].reshape(-1, wl[k].shape[-1])
        if "w_in" in wl:
            wl["w_in"] = jnp.pad(wl["w_in"], ((0, 0), (0, D_IN_PAD - D_IN)))
        wl[f"tie{part}"] = tie
        return wl

    pending, finished, tokens = [], {k: None for k in SHARDED}, []

    def finish_reduce(after):
        ks, send, recv, sums, lands, l, part = pending.pop(0)
        sums, lands = reduce_wait(send, recv, sums, lands, after, name=f"reduce_wait_{l}_{part}")
        for k, s, land in zip(ks, sums, lands):
            finished[k] = reduce_sum(s, land, place_arr, l, finished[k], name=f"reduce_sum_{k}_{l}")

    swapping = []

    def start_reduce(after):
        ks, send, recv, gs, theirs, l, part = swapping.pop(0)
        gs, theirs = exchange_wait(send, recv, gs, theirs, after, name=f"exchange_wait_{l}_{part}")
        sums = [reduce_add(a, b, c_arr, name=f"reduce_add_{k}_{l}") for k, a, b in zip(ks, gs, theirs)]
        send, recv, sums, lands, token = reduce_start(sums, name=f"reduce_start_{l}_{part}")
        pending.append((ks, send, recv, sums, lands, l, part))
        tokens.append(token)
        return token

    def layer_grads(l, part, g, dh):
        ks = GRAD_PARTS[part]
        send, recv, gs, theirs, token = exchange_start([g[k] for k in ks], name=f"exchange_start_{l}_{part}")
        swapping.append((ks, send, recv, gs, theirs, l, part))
        if len(swapping) > 1:
            token = token + start_reduce(dh)
        while len(pending) > 2:
            finish_reduce(dh)
        return token

    small_w = {k: ws[k] for k in SMALL if k not in CONV_SHARDED}
    bmap = jnp.asarray(_rel_bucket_map())
    loss, gx, grads = local_step(x.reshape(t, d), p.reshape(N_LAYER, t, PLE_DIM), loss_target.reshape(t, d), small_w,
                                 layer_weights, layer_grads, bmap, nb, seq)
    while swapping:
        start_reduce(gx)
    g_out, delta, new_m, new_v = {}, {}, {}, {}

    small_shapes = [grads[k].shape for k in SMALL]
    g_small = dict(zip(SMALL, _unpack(allreduce_small(_pack([grads[k] for k in SMALL]), name="allreduce_small"), small_shapes)))
    for k in CONV_SHARDED:
        width = ws[k].shape[-1]
        g_small[k] = lax.dynamic_slice_in_dim(g_small[k], chip * width, width, axis=2)
    g_out.update(g_small)
    shapes = [ws[k].shape for k in SMALL]
    tie = tokens[-1][0:1, 0:1]
    res = adamw(_pack([ws[k] for k in SMALL]) + tie, *[_pack([src[k] for k in SMALL]) for src in (g_out, ms, vs)],
                name="adamw_small")
    for dst, r in zip((delta, new_m, new_v), res):
        dst.update(zip(SMALL, _unpack(r, shapes)))

    after = res[0]
    for part, ks in enumerate(GRAD_PARTS):
        while pending and pending[0][0] == ks:
            finish_reduce(after)
        g_out.update(zip(ks, reduce_share([finished[k] for k in ks], name=f"reduce_share_{part}")))
        for k in ks:
            two_d = lambda a: a.reshape(-1, a.shape[-1])
            res = adamw(two_d(ws[k]), two_d(g_out[k]), two_d(ms[k]), two_d(vs[k]), name=f"adamw_{k}")
            delta[k], new_m[k], new_v[k] = (r.reshape(ws[k].shape) for r in res)
        after = res[0]

    total = lax.psum(loss[0, 0], ("x", "y", "c"))
    return (total, gx.reshape(nb, seq, d), *[stored(k, out[k]) for out in (g_out, delta, new_m, new_v) for k in WEIGHTS])
```

```python
import math

import numpy as np
import jax
import jax.numpy as jnp
from jax import lax
from jax.experimental import pallas as pl
from jax.experimental.pallas import tpu as pltpu

F32 = jnp.float32
BF16 = jnp.bfloat16

EPS = 1e-6
D_MODEL = 1024
D_FF = 2816
N_CHIP = 4
FF_BLK = D_FF // N_CHIP
HEAD = 64
LRU_W = 256
ATT_W = 512
ATT_HEADS = 8
KV_HEADS = 2
ATT_GROUP = 4
BLOCK_Q = 128
DN_HEADS = 4
DN_CHUNK = 64
D_IN = 2312
D_IN_PAD = 2560
PLE_DIM = 256
REL_BUCKETS = 32
LRU_C = 8.0
N_LAYER = 2

ADAM_LR, ADAM_B1, ADAM_B2, ADAM_EPS, ADAM_WD, ADAM_STEP = 0.001, 0.9, 0.999, 1e-08, 0.01, 10

VMEM_LIMIT = 56 << 20
MESH = pl.DeviceIdType.MESH
SDS = jax.ShapeDtypeStruct


def _dot(a, b, ca=1, cb=0, hi=False):
    dims = (((ca,), (cb,)), ((), ()))
    one = lambda u, v: lax.dot_general(u, v, dims, preferred_element_type=F32)
    a_hi, b_hi = a.astype(BF16), b.astype(BF16)
    if not hi:
        return one(a_hi, b_hi)
    a_lo = (a - a_hi.astype(F32)).astype(BF16)
    b_lo = (b - b_hi.astype(F32)).astype(BF16)
    return one(a_hi, b_hi) + (one(a_hi, b_lo) + one(a_lo, b_hi))


def _nn(a, b, hi=False):
    return _dot(a, b, 1, 0, hi)


def _nt(a, b, hi=False):
    return _dot(a, b, 1, 1, hi)


def _tn(a, b, hi=False):
    return _dot(a, b, 0, 0, hi)


def _sigmoid(x):
    return jax.nn.sigmoid(x)


def _softplus(x):
    return jnp.maximum(x, 0.0) + jnp.log1p(jnp.exp(-jnp.abs(x)))


def _neg_expm1(z):
    series = -z * (1.0 + z * (0.5 + z * (1.0 / 6.0 + z * (1.0 / 24.0 + z * (1.0 / 120.0)))))
    return jnp.where(z > -0.05, series, 1.0 - jnp.exp(z))


_GELU_C = math.sqrt(2.0 / math.pi)


def _gelu(x):
    t = jnp.tanh(_GELU_C * (x + 0.044715 * x * x * x))
    return 0.5 * x * (1.0 + t), t


def _gelu_grad(x, t):
    return 0.5 * (1.0 + t) + 0.5 * x * (1.0 - t * t) * _GELU_C * (1.0 + 3.0 * 0.044715 * x * x)


def _rms_fwd(h, g):
    r = lax.rsqrt(jnp.mean(h * h, axis=-1, keepdims=True) + EPS)
    xh = h * r
    return xh * g, xh, r


def _rms_bwd(dn, xh, r, g):
    dxh = dn * g
    dh = r * (dxh - xh * jnp.mean(dxh * xh, axis=-1, keepdims=True))
    return dh, jnp.sum(dn * xh, axis=0, keepdims=True)


def _shift_down(x, d, fill=0.0):
    row = lax.broadcasted_iota(jnp.int32, x.shape, 0)
    return jnp.where(row >= d, pltpu.roll(x, d, 0), fill)


def _shift_up(x, d, fill=0.0):
    n = x.shape[0]
    row = lax.broadcasted_iota(jnp.int32, x.shape, 0)
    return jnp.where(row < n - d, pltpu.roll(x, n - d, 0), fill)


def _conv_fwd(x, w):
    y = x * w[3]
    for k in range(3):
        y = y + _shift_down(x, 3 - k) * w[k]
    return y


def _conv_bwd(dy, x, w):
    dx = dy * w[3]
    rows = [None] * 4
    rows[3] = jnp.sum(dy * x, axis=0, keepdims=True)
    for k in range(3):
        dx = dx + _shift_up(dy, 3 - k) * w[k]
        rows[k] = jnp.sum(dy * _shift_down(x, 3 - k), axis=0, keepdims=True)
    r4 = lax.broadcasted_iota(jnp.int32, (4, x.shape[1]), 0)
    dw = jnp.zeros((4, x.shape[1]), F32)
    for k in range(4):
        dw = jnp.where(r4 == k, rows[k], dw)
    return dx, dw


FFN_SPLIT = 2


def _interleave(gens):
    pending = list(gens)
    while pending:
        for g in list(pending):
            if next(g, StopIteration) is StopIteration:
                pending.remove(g)


def _params(sem=None, vmem=VMEM_LIMIT):
    return pltpu.CompilerParams(dimension_semantics=sem, vmem_limit_bytes=vmem)


def _whole(shape):
    nd = len(shape)
    return pl.BlockSpec(shape, lambda *_: (0,) * nd)


def matmul(a, b, *, name, ta=False, tb=False, residual=None, out_dtype=F32, tm=512, tn=512, tk=512):
    m, k = (a.shape[1], a.shape[0]) if ta else a.shape
    n = b.shape[0] if tb else b.shape[1]
    tm, tn, tk = min(tm, m), min(tn, n), min(tk, k)
    assert m % tm == 0 and n % tn == 0 and k % tk == 0, (m, n, k, tm, tn, tk)
    nk = k // tk

    def body(*refs):
        if residual is None:
            a_ref, b_ref, o_ref, acc = refs
        else:
            a_ref, b_ref, r_ref, o_ref, acc = refs
        kk = pl.program_id(2)

        @pl.when(kk == 0)
        def _():
            acc[...] = jnp.zeros_like(acc)

        acc[...] += _dot(a_ref[...], b_ref[...], 0 if ta else 1, 1 if tb else 0)

        @pl.when(kk == nk - 1)
        def _():
            out = acc[...]
            if residual is not None:
                out = out + r_ref[...]
            o_ref[...] = out.astype(out_dtype)

    a_spec = pl.BlockSpec((tk, tm), lambda i, j, kk: (kk, i)) if ta else pl.BlockSpec((tm, tk), lambda i, j, kk: (i, kk))
    b_spec = pl.BlockSpec((tn, tk), lambda i, j, kk: (j, kk)) if tb else pl.BlockSpec((tk, tn), lambda i, j, kk: (kk, j))
    o_spec = pl.BlockSpec((tm, tn), lambda i, j, kk: (i, j))
    in_specs, args = [a_spec, b_spec], [a, b]
    if residual is not None:
        in_specs.append(o_spec)
        args.append(residual)
    return pl.pallas_call(
        body, name=name, grid=(m // tm, n // tn, nk), in_specs=in_specs, out_specs=o_spec,
        out_shape=SDS((m, n), out_dtype), scratch_shapes=[pltpu.VMEM((tm, tn), F32)],
        compiler_params=_params(("parallel", "parallel", "arbitrary")))(*args)


def norm_matmul(h, gain, w, *, name, tm=512, tn=512):
    t, d = h.shape
    tm = min(tm, t)
    n = w.shape[1]
    assert t % tm == 0 and n % tn == 0

    def body(h_ref, g_ref, w_ref, u_ref, n_ref):
        @pl.when(pl.program_id(1) == 0)
        def _():
            n_ref[...] = _rms_fwd(h_ref[...], g_ref[...])[0].astype(BF16)

        u_ref[...] = _nn(n_ref[...], w_ref[...])

    return pl.pallas_call(
        body, name=name, grid=(t // tm, n // tn),
        in_specs=[pl.BlockSpec((tm, d), lambda i, j: (i, 0)), _whole((1, d)), pl.BlockSpec((d, tn), lambda i, j: (0, j))],
        out_specs=[pl.BlockSpec((tm, tn), lambda i, j: (i, j)), pl.BlockSpec((tm, d), lambda i, j: (i, 0))],
        out_shape=[SDS((t, n), F32), SDS((t, d), BF16)],
        compiler_params=_params(("parallel", "arbitrary")))(h, gain, w)


def ffn_fwd(h, gain, wg, wu, wd, *, name, tm=1024):
    t, d = h.shape
    tm = min(tm, t)

    def body(h_ref, g_ref, wg_ref, wu_ref, wd_ref, o_ref, n_ref, a_ref, b_ref, acc):
        j = pl.program_id(1)

        @pl.when(j == 0)
        def _():
            n_ref[...] = _rms_fwd(h_ref[...], g_ref[...])[0].astype(BF16)
            acc[...] = jnp.zeros_like(acc)

        def part(rows):
            n = n_ref[rows, :]
            a = _nt(n, wg_ref[...])
            b = _nt(n, wu_ref[...])
            yield
            a_ref[rows, :] = a.astype(BF16)
            b_ref[rows, :] = b.astype(BF16)
            acc[rows, :] += _nn(a * _sigmoid(a) * b, wd_ref[...])

        _interleave([part(pl.ds(k * (tm // FFN_SPLIT), tm // FFN_SPLIT)) for k in range(FFN_SPLIT)])

        @pl.when(j == N_CHIP - 1)
        def _():
            o_ref[...] = h_ref[...] + 0.5 * acc[...]

    row = pl.BlockSpec((tm, d), lambda i, j: (i, 0))
    blk = pl.BlockSpec((None, tm, FF_BLK), lambda i, j: (j, i, 0))
    wspec = pl.BlockSpec((None, FF_BLK, d), lambda i, j: (j, 0, 0))
    act = SDS((N_CHIP, t, FF_BLK), BF16)
    return pl.pallas_call(
        body, name=name, grid=(t // tm, N_CHIP), in_specs=[row, _whole((1, d)), wspec, wspec, wspec],
        out_specs=[row, row, blk, blk], out_shape=[SDS((t, d), F32), SDS((t, d), BF16), act, act],
        scratch_shapes=[pltpu.VMEM((tm, d), F32)],
        compiler_params=_params(("parallel", "arbitrary")))(h, gain, wg, wu, wd)


def ffn_bwd_act(h, gain, dout, a, b, wg, wu, wd, *, name, tm=512):
    t, d = h.shape
    tm = min(tm, t)

    def body(h_ref, g_ref, do_ref, a_ref, b_ref, wg_ref, wu_ref, wd_ref, dh_ref, da_ref, db_ref, s_ref, dg_ref, dn_acc):
        i, j = pl.program_id(0), pl.program_id(1)

        @pl.when((i == 0) & (j == 0))
        def _():
            dg_ref[...] = jnp.zeros_like(dg_ref)

        @pl.when(j == 0)
        def _():
            dn_acc[...] = jnp.zeros_like(dn_acc)

        def part(rows):
            ds = _nt(0.5 * do_ref[rows, :], wd_ref[...])
            yield
            a = a_ref[rows, :].astype(F32)
            b = b_ref[rows, :].astype(F32)
            sig = _sigmoid(a)
            sa = a * sig
            db = ds * sa
            da = ds * b * (sig * (1.0 + a * (1.0 - sig)))
            s_ref[rows, :] = (sa * b).astype(BF16)
            da_ref[rows, :] = da.astype(BF16)
            db_ref[rows, :] = db.astype(BF16)
            yield
            dn_acc[rows, :] += _nn(da, wg_ref[...]) + _nn(db, wu_ref[...])

        _interleave([part(pl.ds(k * (tm // FFN_SPLIT), tm // FFN_SPLIT)) for k in range(FFN_SPLIT)])

        @pl.when(j == N_CHIP - 1)
        def _():
            g = g_ref[...]
            _, xh, r = _rms_fwd(h_ref[...], g)
            dh, dg = _rms_bwd(dn_acc[...], xh, r, g)
            dh_ref[...] = do_ref[...] + dh
            dg_ref[...] += dg

    row = pl.BlockSpec((tm, d), lambda i, j: (i, 0))
    blk = pl.BlockSpec((None, tm, FF_BLK), lambda i, j: (j, i, 0))
    wspec = pl.BlockSpec((None, FF_BLK, d), lambda i, j: (j, 0, 0))
    act = SDS((N_CHIP, t, FF_BLK), BF16)
    return pl.pallas_call(
        body, name=name, grid=(t // tm, N_CHIP), in_specs=[row, _whole((1, d)), row, blk, blk, wspec, wspec, wspec],
        out_specs=[row, blk, blk, blk, _whole((1, d))],
        out_shape=[SDS((t, d), F32), act, act, act, SDS((1, d), F32)],
        scratch_shapes=[pltpu.VMEM((tm, d), F32)],
        compiler_params=_params(("arbitrary", "arbitrary")))(h, gain, dout, a, b, wg, wu, wd)


def ffn_bwd_w(n, da, db, s, dout, *, name, tk=1024):
    t, d = n.shape
    tk = min(tk, t)

    def body(n_ref, da_ref, db_ref, s_ref, do_ref, dwg_ref, dwu_ref, dwd_ref):
        @pl.when(pl.program_id(1) == 0)
        def _():
            dwg_ref[...] = jnp.zeros_like(dwg_ref)
            dwu_ref[...] = jnp.zeros_like(dwu_ref)
            dwd_ref[...] = jnp.zeros_like(dwd_ref)

        nn = n_ref[...]
        dwg_ref[...] += _tn(da_ref[...], nn)
        dwu_ref[...] += _tn(db_ref[...], nn)
        dwd_ref[...] += _tn(s_ref[...], 0.5 * do_ref[...])

    row = pl.BlockSpec((tk, d), lambda j, kk: (kk, 0))
    blk = pl.BlockSpec((None, tk, FF_BLK), lambda j, kk: (j, kk, 0))
    return pl.pallas_call(
        body, name=name, grid=(N_CHIP, t // tk), in_specs=[row, blk, blk, blk, row],
        out_specs=[pl.BlockSpec((None, FF_BLK, d), lambda j, kk: (j, 0, 0)),
                   pl.BlockSpec((None, FF_BLK, d), lambda j, kk: (j, 0, 0)),
                   pl.BlockSpec((None, FF_BLK, d), lambda j, kk: (j, 0, 0))],
        out_shape=[SDS((N_CHIP, FF_BLK, d), F32)] * 3,
        compiler_params=_params(("parallel", "arbitrary")))(n, da, db, s, dout)


def ple_fwd(h, gain, wpg, pl_in, wpp, *, name, tm=512):
    t, d = h.shape
    tm = min(tm, t)
    pd = pl_in.shape[1]

    def body(h_ref, g_ref, wpg_ref, p_ref, wpp_ref, o_ref):
        hh = h_ref[...]
        n = _rms_fwd(hh, g_ref[...])[0]
        gate = _sigmoid(_nn(n, wpg_ref[...]))
        o_ref[...] = hh + gate * _nn(p_ref[...], wpp_ref[...])

    row = pl.BlockSpec((tm, d), lambda i: (i, 0))
    return pl.pallas_call(
        body, name=name, grid=(t // tm,),
        in_specs=[row, _whole((1, d)), _whole((d, d)), pl.BlockSpec((tm, pd), lambda i: (i, 0)), _whole((pd, d))],
        out_specs=row, out_shape=SDS((t, d), F32), compiler_params=_params(("parallel",)))(h, gain, wpg, pl_in, wpp)


def ple_bwd(h, gain, wpg, pl_in, wpp, dout, *, name, tm=512):
    t, d = h.shape
    tm = min(tm, t)
    pd = pl_in.shape[1]

    def body(h_ref, g_ref, wpg_ref, p_ref, wpp_ref, do_ref, dh_ref, n_ref, dga_ref, dpp_ref, dg_ref):
        @pl.when(pl.program_id(0) == 0)
        def _():
            dg_ref[...] = jnp.zeros_like(dg_ref)

        g = g_ref[...]
        n, xh, r = _rms_fwd(h_ref[...], g)
        gate = _sigmoid(_nn(n, wpg_ref[...]))
        pp = _nn(p_ref[...], wpp_ref[...])
        do = do_ref[...]
        dga = do * pp * gate * (1.0 - gate)
        dh, dg = _rms_bwd(_nt(dga, wpg_ref[...]), xh, r, g)
        dh_ref[...] = do + dh
        n_ref[...] = n.astype(BF16)
        dga_ref[...] = dga.astype(BF16)
        dpp_ref[...] = (do * gate).astype(BF16)
        dg_ref[...] += dg

    row = pl.BlockSpec((tm, d), lambda i: (i, 0))
    return pl.pallas_call(
        body, name=name, grid=(t // tm,),
        in_specs=[row, _whole((1, d)), _whole((d, d)), pl.BlockSpec((tm, pd), lambda i: (i, 0)), _whole((pd, d)), row],
        out_specs=[row, row, row, row, _whole((1, d))],
        out_shape=[SDS((t, d), F32), SDS((t, d), BF16), SDS((t, d), BF16), SDS((t, d), BF16), SDS((1, d), F32)],
        compiler_params=_params(("arbitrary",)))(h, gain, wpg, pl_in, wpp, dout)


def loss_head(h, gain, target, *, name, tm=512):
    t, d = h.shape
    tm = min(tm, t)

    def body(h_ref, g_ref, t_ref, dh_ref, dg_ref, l_ref):
        @pl.when(pl.program_id(0) == 0)
        def _():
            dg_ref[...] = jnp.zeros_like(dg_ref)
            l_ref[...] = jnp.zeros_like(l_ref)

        g = g_ref[...]
        y, xh, r = _rms_fwd(h_ref[...], g)
        err = y - t_ref[...]
        l_ref[...] += 0.5 * jnp.sum(jnp.mean(err * err, axis=-1, keepdims=True), axis=0, keepdims=True)
        dh, dg = _rms_bwd(err * (1.0 / d), xh, r, g)
        dh_ref[...] = dh
        dg_ref[...] += dg

    row = pl.BlockSpec((tm, d), lambda i: (i, 0))
    return pl.pallas_call(
        body, name=name, grid=(t // tm,), in_specs=[row, _whole((1, d)), row],
        out_specs=[row, _whole((1, d)), _whole((1, 1))],
        out_shape=[SDS((t, d), F32), SDS((1, d), F32), SDS((1, 1), F32)],
        compiler_params=_params(("arbitrary",)))(h, gain, target)


def adamw(w, g, m, v, *, name):
    r, c = w.shape
    tr = r
    for cand in (704, 512, 352, 256, 128, 64, 32, 16, 8):
        if r % cand == 0:
            tr = cand
            break

    def body(w_ref, g_ref, m_ref, v_ref, d_ref, nm_ref, nv_ref):
        gg = g_ref[...]
        mm = ADAM_B1 * m_ref[...] + (1.0 - ADAM_B1) * gg
        vv = ADAM_B2 * v_ref[...] + (1.0 - ADAM_B2) * (gg * gg)
        m_hat = mm / (1.0 - ADAM_B1 ** ADAM_STEP)
        v_hat = vv / (1.0 - ADAM_B2 ** ADAM_STEP)
        d_ref[...] = -ADAM_LR * (m_hat / (jnp.sqrt(v_hat) + ADAM_EPS) + ADAM_WD * w_ref[...])
        nm_ref[...] = mm
        nv_ref[...] = vv

    blk = pl.BlockSpec((tr, c), lambda i: (i, 0))
    out = SDS((r, c), F32)
    return pl.pallas_call(body, name=name, grid=(r // tr,), in_specs=[blk] * 4, out_specs=[blk] * 3,
                          out_shape=[out, out, out], compiler_params=_params(("parallel",)))(w, g, m, v)


def _scan_fwd(a, b):
    d = 1
    while d < a.shape[0]:
        b = a * _shift_down(b, d, 0.0) + b
        a = a * _shift_down(a, d, 1.0)
        d *= 2
    return b


def _scan_rev(a, b):
    d = 1
    while d < a.shape[0]:
        b = a * _shift_up(b, d, 0.0) + b
        a = a * _shift_up(a, d, 1.0)
        d *= 2
    return b


LRU_HALF = 128


def _lru_in_specs(seq):
    half = LRU_W // LRU_HALF
    vec = pl.BlockSpec((1, LRU_HALF), lambda j, b: (0, j))
    mat = pl.BlockSpec((LRU_HALF, LRU_HALF), lambda j, b: (j, j))
    return [pl.BlockSpec((seq, LRU_HALF), lambda j, b: (b, j)), pl.BlockSpec((seq, LRU_HALF), lambda j, b: (b, half + j)),
            pl.BlockSpec((4, LRU_HALF), lambda j, b: (0, j)), vec, mat, vec, mat, vec, vec]


def _lru_math(x_ref, gate_ref, cw_ref, cb_ref, wa_ref, ba_ref, wx_ref, bx_ref, lam_ref):
    x = x_ref[...]
    gate = gate_ref[...]
    cw =[cw_ref[k:k + 1, :] for k in range(4)]
    xr = _conv_fwd(x, cw) + cb_ref[...]
    r = _sigmoid(_nn(xr, wa_ref[...]) + ba_ref[...])
    i = _sigmoid(_nn(xr, wx_ref[...]) + bx_ref[...])
    sp = _softplus(-lam_ref[...])
    log_a = -LRU_C * r * sp
    a = jnp.exp(log_a)
    mult = jnp.sqrt(_neg_expm1(2.0 * log_a))
    gi = i * xr
    h = _scan_fwd(a, mult * gi)
    gl, tg = _gelu(gate)
    return dict(x=x, gate=gate, cw=cw, xr=xr, r=r, i=i, sp=sp, a=a, mult=mult, gi=gi, h=h, gl=gl, tg=tg)


def lru_fwd(u, cw, cb, wa, ba, wx, bx, lam, *, seq, name):
    t = u.shape[0]

    def body(x_ref, gate_ref, cw_ref, cb_ref, wa_ref, ba_ref, wx_ref, bx_ref, lam_ref, y_ref):
        f = _lru_math(x_ref, gate_ref, cw_ref, cb_ref, wa_ref, ba_ref, wx_ref, bx_ref, lam_ref)
        y_ref[...] = f["gl"] * f["h"]

    return pl.pallas_call(
        body, name=name, grid=(LRU_W // LRU_HALF, t // seq), in_specs=_lru_in_specs(seq),
        out_specs=pl.BlockSpec((seq, LRU_HALF), lambda j, b: (b, j)), out_shape=SDS((t, LRU_W), F32),
        compiler_params=_params(("parallel", "parallel")))(u, u, cw, cb, wa, ba, wx, bx, lam)


def lru_bwd(u, cw, cb, wa, ba, wx, bx, lam, dy, *, seq, name):
    t = u.shape[0]

    def body(x_ref, gate_ref, cw_ref, cb_ref, wa_ref, ba_ref, wx_ref, bx_ref, lam_ref, dy_ref,
             dx_ref, dgate_ref, dcw_ref, dwa_ref, dwx_ref, dv_ref):
        @pl.when(pl.program_id(1) == 0)
        def _():
            dcw_ref[...] = jnp.zeros_like(dcw_ref)
            dwa_ref[...] = jnp.zeros_like(dwa_ref)
            dwx_ref[...] = jnp.zeros_like(dwx_ref)
            dv_ref[...] = jnp.zeros_like(dv_ref)

        f = _lru_math(x_ref, gate_ref, cw_ref, cb_ref, wa_ref, ba_ref, wx_ref, bx_ref, lam_ref)
        dy = dy_ref[...]
        a, h, xr, r, i, mult, gi, sp = f["a"], f["h"], f["xr"], f["r"], f["i"], f["mult"], f["gi"], f["sp"]
        dgate_ref[...] = dy * h * _gelu_grad(f["gate"], f["tg"])
        lamb = _scan_rev(_shift_up(a, 1, 0.0), dy * f["gl"])
        da = lamb * _shift_down(h, 1)
        dlog_a = da * a - (lamb * gi) * (a * a) / mult
        dgi = lamb * mult
        dra = dlog_a * (-LRU_C * sp) * r * (1.0 - r)
        dia = dgi * xr * i * (1.0 - i)
        dsp = jnp.sum(dlog_a * (-LRU_C * r), axis=0, keepdims=True)
        dlam = -dsp * _sigmoid(-lam_ref[...])
        dxr = dgi * i + _nt(dra, wa_ref[...]) + _nt(dia, wx_ref[...])
        dx, dcw = _conv_bwd(dxr, f["x"], f["cw"])
        dx_ref[...] = dx
        dcw_ref[...] += dcw
        dwa_ref[...] += _tn(xr, dra)
        dwx_ref[...] += _tn(xr, dia)
        rows = [jnp.sum(dxr, axis=0, keepdims=True), jnp.sum(dra, axis=0, keepdims=True),
                jnp.sum(dia, axis=0, keepdims=True), dlam]
        r8 = lax.broadcasted_iota(jnp.int32, (8, LRU_HALF), 0)
        acc = jnp.zeros((8, LRU_HALF), F32)
        for k, row in enumerate(rows):
            acc = jnp.where(r8 == k, row, acc)
        dv_ref[...] += acc

    nhalf = LRU_W // LRU_HALF
    col = pl.BlockSpec((seq, LRU_HALF), lambda j, b: (b, j))
    mat = pl.BlockSpec((None, LRU_HALF, LRU_HALF), lambda j, b: (j, 0, 0))
    return pl.pallas_call(
        body, name=name, grid=(nhalf, t // seq), in_specs=_lru_in_specs(seq) + [col],
        out_specs=[col, col, pl.BlockSpec((4, LRU_HALF), lambda j, b: (0, j)), mat, mat,
                   pl.BlockSpec((8, LRU_HALF), lambda j, b: (0, j))],
        out_shape=[SDS((t, LRU_W), F32), SDS((t, LRU_W), F32), SDS((4, LRU_W), F32),
                   SDS((nhalf, LRU_HALF, LRU_HALF), F32), SDS((nhalf, LRU_HALF, LRU_HALF), F32), SDS((8, LRU_W), F32)],
        compiler_params=_params(("arbitrary", "arbitrary")))(u, u, cw, cb, wa, ba, wx, bx, lam, dy)


NEG = -1e30


def _rel_bucket_map():
    dist = (np.arange(BLOCK_Q)[:, None] - np.arange(BLOCK_Q)[None, :]) % BLOCK_Q
    max_exact = REL_BUCKETS // 2
    large = max_exact + (np.log(np.maximum(dist, 1).astype(np.float32) / max_exact)
                         / math.log(BLOCK_Q / max_exact) * (REL_BUCKETS - max_exact)).astype(np.int32)
    large = np.minimum(large, REL_BUCKETS - 1)
    return np.where(dist < max_exact, dist, large).astype(np.int32)


def relbias_fwd(rel_bias, bmap, *, name):
    def body(rb_ref, bm_ref, o_ref):
        bm = bm_ref[...]
        for h in range(ATT_HEADS):
            acc = jnp.zeros((BLOCK_Q, BLOCK_Q), F32)
            for b in range(REL_BUCKETS):
                acc = jnp.where(bm == b, rb_ref[b, h], acc)
            o_ref[h] = acc

    return pl.pallas_call(
        body, name=name, in_specs=[pl.BlockSpec(memory_space=pltpu.SMEM), pl.BlockSpec(memory_space=pltpu.VMEM)],
        out_specs=pl.BlockSpec(memory_space=pltpu.VMEM), out_shape=SDS((ATT_HEADS, BLOCK_Q, BLOCK_Q), F32))(rel_bias, bmap)


def relbias_bwd(dbias, bmap, *, name):
    def body(db_ref, bm_ref, o_ref):
        bm = bm_ref[...]
        row = lax.broadcasted_iota(jnp.int32, (REL_BUCKETS, 128), 0)
        col = lax.broadcasted_iota(jnp.int32, (REL_BUCKETS, 128), 1)
        acc = jnp.zeros((REL_BUCKETS, 128), F32)
        for h in range(ATT_HEADS):
            d = db_ref[h]
            for b in range(REL_BUCKETS):
                s = jnp.sum(jnp.sum(jnp.where(bm == b, d, 0.0), axis=1, keepdims=True), axis=0, keepdims=True)
                acc = jnp.where((row == b) & (col == h), s, acc)
        o_ref[...] = acc

    return pl.pallas_call(body, name=name, out_shape=SDS((REL_BUCKETS, 128), F32))(dbias, bmap)


def _iota2(shape, axis):
    return lax.broadcasted_iota(jnp.int32, shape, axis)


def _chunk_cumsum(x):
    pos = _iota2(x.shape, 0) & (DN_CHUNK - 1)
    d = 1
    while d < DN_CHUNK:
        x = x + jnp.where(pos >= d, pltpu.roll(x, d, 0), 0.0)
        d *= 2
    return x


def _chunk_rev_cumsum(x):
    n = x.shape[0]
    pos = _iota2(x.shape, 0) & (DN_CHUNK - 1)
    d = 1
    while d < DN_CHUNK:
        x = x + jnp.where(pos < DN_CHUNK - d, pltpu.roll(x, n - d, 0), 0.0)
        d *= 2
    return x


_DN_SCALE = (HEAD ** -0.5, 1.0, None)
DN_UNROLL = 4


COL_Q, COL_K, COL_V = 512 // 128, 1024 // 128, 1152 // 128
COL_DNQ, COL_DNK, COL_DNV, COL_DNZ, COL_BA = 1280 // 128, 1536 // 128, 1792 // 128, 2048 // 128, 2304 // 128


def _lane_a(shape):
    return _iota2(shape, 1) < HEAD


def _bd(x):
    la = _lane_a(x.shape)
    return jnp.concatenate([jnp.where(la, x, 0.0), jnp.where(la, 0.0, x)], axis=0)


def _fold(m):
    return m[:HEAD] + m[HEAD:]


def _bd_mask():
    return (_iota2((2 * HEAD, 2 * HEAD), 0) < HEAD) == (_iota2((2 * HEAD, 2 * HEAD), 1) < HEAD)


def _pk_nn(x, y, hi=False):
    return _nn(x, _bd(y), hi)


def _pk_nt(u, v, hi=False):
    return _nt(u, _bd(v), hi)


def _pk_tn(x, y, hi=False):
    return _fold(jnp.where(_bd_mask(), _tn(x, y, hi), 0.0))


def _half_sum(x):
    la = _lane_a(x.shape)
    return jnp.where(la, jnp.sum(jnp.where(la, x, 0.0), axis=-1, keepdims=True),
                     jnp.sum(jnp.where(la, 0.0, x), axis=-1, keepdims=True))


def _lane_col(x, idx):
    return jnp.sum(jnp.where(_iota2(x.shape, 1) == idx, x, 0.0), axis=-1, keepdims=True)


def _row0(x):
    return jnp.max(x, axis=0, keepdims=True)


def _dup_kv(x, g):
    la = _lane_a(x.shape)
    rolled = pltpu.roll(x, HEAD, 1)
    return jnp.where(la, x, rolled) if g == 0 else jnp.where(la, rolled, x)


def _stack_heads(ref, g):
    la = _lane_a((BLOCK_Q, 2 * HEAD))
    parts = []
    for hh in range(ATT_GROUP):
        pair = ref[:, pl.ds(2 * HEAD * (2 * g + hh // 2), 2 * HEAD)]
        parts.append(jnp.where(la if hh % 2 == 0 else ~la, pair, 0.0))
    return jnp.concatenate(parts, axis=0)


def _unstack_heads(stack, ref, g):
    la = _lane_a((BLOCK_Q, 2 * HEAD))
    for j in range(2):
        top = stack[2 * j * BLOCK_Q:(2 * j + 1) * BLOCK_Q]
        bot = stack[(2 * j + 1) * BLOCK_Q:(2 * j + 2) * BLOCK_Q]
        ref[:, pl.ds(2 * HEAD * (2 * g + j), 2 * HEAD)] = jnp.where(la, top, bot)


def _swa_probs(q_ref, k_ref, v_ref, b_ref, s_ref, n, g):
    rows = ATT_GROUP * BLOCK_Q
    prev = pl.multiple_of(jnp.maximum(n - 1, 0) * BLOCK_Q, BLOCK_Q)
    cur = pl.multiple_of(n * BLOCK_Q, BLOCK_Q)
    kp, kc = _dup_kv(k_ref[pl.ds(prev, BLOCK_Q), :], g), _dup_kv(k_ref[pl.ds(cur, BLOCK_Q), :], g)
    vp, vc = _dup_kv(v_ref[pl.ds(prev, BLOCK_Q), :], g), _dup_kv(v_ref[pl.ds(cur, BLOCK_Q), :], g)
    qs = _stack_heads(q_ref, g) * (HEAD ** -0.5)
    bias = b_ref[pl.ds(ATT_GROUP * g, ATT_GROUP)].reshape(rows, BLOCK_Q)
    i = _iota2((rows, BLOCK_Q), 0) & (BLOCK_Q - 1)
    j = _iota2((rows, BLOCK_Q), 1)
    s_p = jnp.where((j > i) & (n > 0), _nt(qs, kp) + bias, NEG)
    s_c = jnp.where(j <= i, _nt(qs, kc) + bias, NEG)
    sink = s_ref[pl.ds(rows * g, rows), :]
    m = jnp.maximum(jnp.maximum(jnp.max(s_p, axis=-1, keepdims=True), jnp.max(s_c, axis=-1, keepdims=True)), sink)
    e_p, e_c, e_s = jnp.exp(s_p - m), jnp.exp(s_c - m), jnp.exp(sink - m)
    inv = 1.0 / (jnp.sum(e_p, axis=-1, keepdims=True) + jnp.sum(e_c, axis=-1, keepdims=True) + e_s)
    return e_p * inv, e_c * inv, e_s * inv, qs, kp, kc, vp, vc, prev, cur


def _swa_specs(seq):
    nblk = seq // BLOCK_Q
    qspec = pl.BlockSpec((BLOCK_Q, ATT_W), lambda b, n: (b * nblk + n, COL_Q * 128 // ATT_W))
    kspec = pl.BlockSpec((seq, 2 * HEAD), lambda b, n: (b, COL_K))
    vspec = pl.BlockSpec((seq, 2 * HEAD), lambda b, n: (b, COL_V))
    ospec = pl.BlockSpec((BLOCK_Q, ATT_W), lambda b, n: (b * nblk + n, 0))
    kvout = pl.BlockSpec((seq, 2 * HEAD), lambda b, n: (b, 0))
    return qspec, kspec, vspec, ospec, kvout, _whole((ATT_HEADS, BLOCK_Q, BLOCK_Q)), _whole((ATT_HEADS * BLOCK_Q, 1))


def swa_fwd(u, bias, sink_rows, *, seq, name):
    t = u.shape[0]

    def body(q_ref, k_ref, v_ref, b_ref, s_ref, o_ref):
        for g in range(KV_HEADS):
            p_p, p_c, _, _, _, _, vp, vc, _, _ = _swa_probs(q_ref, k_ref, v_ref, b_ref, s_ref, pl.program_id(1), g)
            _unstack_heads(_nn(p_p, vp) + _nn(p_c, vc), o_ref, g)

    qspec, kspec, vspec, ospec, kvout, bspec, sspec = _swa_specs(seq)
    return pl.pallas_call(
        body, name=name, grid=(t // seq, seq // BLOCK_Q), in_specs=[qspec, kspec, vspec, bspec, sspec], out_specs=ospec,
        out_shape=SDS((t, ATT_W), F32), compiler_params=_params(("parallel", "arbitrary")))(u, u, u, bias, sink_rows)


def swa_bwd(u, bias, sink_rows, do, *, seq, name):
    t = u.shape[0]

    def body(q_ref, k_ref, v_ref, b_ref, s_ref, do_ref, dq_ref, dk_ref, dv_ref, db_ref, ds_ref):
        b, n = pl.program_id(0), pl.program_id(1)

        @pl.when((b == 0) & (n == 0))
        def _():
            db_ref[...] = jnp.zeros_like(db_ref)
            ds_ref[...] = jnp.zeros_like(ds_ref)

        @pl.when(n == 0)
        def _():
            dk_ref[...] = jnp.zeros_like(dk_ref)
            dv_ref[...] = jnp.zeros_like(dv_ref)

        la = _lane_a((BLOCK_Q, 2 * HEAD))
        for g in range(KV_HEADS):
            p_p, p_c, p_s, qs, kp, kc, vp, vc, prev, cur = _swa_probs(q_ref, k_ref, v_ref, b_ref, s_ref, n, g)
            do = _stack_heads(do_ref, g)
            dp_p, dp_c = _nt(do, vp), _nt(do, vc)
            delta = jnp.sum(p_p * dp_p, axis=-1, keepdims=True) + jnp.sum(p_c * dp_c, axis=-1, keepdims=True)
            ds_p, ds_c = p_p * (dp_p - delta), p_c * (dp_c - delta)
            _unstack_heads((_nn(ds_p, kp) + _nn(ds_c, kc)) * (HEAD ** -0.5), dq_ref, g)
            mine = la if g == 0 else ~la

            def to_head(x):
                return jnp.where(mine, x + pltpu.roll(x, HEAD, 1), 0.0)

            dk_ref[pl.ds(prev, BLOCK_Q), :] += to_head(_tn(ds_p, qs))
            dk_ref[pl.ds(cur, BLOCK_Q), :] += to_head(_tn(ds_c, qs))
            dv_ref[pl.ds(prev, BLOCK_Q), :] += to_head(_tn(p_p, do))
            dv_ref[pl.ds(cur, BLOCK_Q), :] += to_head(_tn(p_c, do))
            db_ref[pl.ds(ATT_GROUP * g, ATT_GROUP)] += (ds_p + ds_c).reshape(ATT_GROUP, BLOCK_Q, BLOCK_Q)
            rows = ATT_GROUP * BLOCK_Q
            ds_ref[pl.ds(rows * g, rows), :] += -p_s * delta

    qspec, kspec, vspec, ospec, kvout, bspec, sspec = _swa_specs(seq)
    return pl.pallas_call(
        body, name=name, grid=(t // seq, seq // BLOCK_Q), in_specs=[qspec, kspec, vspec, bspec, sspec, ospec],
        out_specs=[ospec, kvout, kvout, bspec, sspec],
        out_shape=[SDS((t, ATT_W), F32), SDS((t, 2 * HEAD), F32), SDS((t, 2 * HEAD), F32),
                   SDS((ATT_HEADS, BLOCK_Q, BLOCK_Q), F32), SDS((ATT_HEADS * BLOCK_Q, 1), F32)],
        compiler_params=_params(("arbitrary", "arbitrary")))(u, u, u, bias, sink_rows, do)


def _gdn_gates(ba_ref, alog_ref, dt_ref, hp):
    blk = ba_ref[...]
    beta_blk = _sigmoid(blk)
    sp_arg = blk + dt_ref[...]
    a_exp = jnp.exp(alog_ref[...])
    g_blk = -a_exp * _softplus(sp_arg)
    la = _lane_a(blk.shape)
    ha = 2 * hp
    beta = jnp.where(la, _lane_col(beta_blk, ha), _lane_col(beta_blk, ha + 1))
    g = jnp.where(la, _lane_col(g_blk, DN_HEADS + ha), _lane_col(g_blk, DN_HEADS + ha + 1))
    return beta, g, beta_blk, sp_arg, a_exp, g_blk


def _gdn_act(c, scale):
    sig = _sigmoid(c)
    a = c * sig
    if scale is None:
        return a, sig, None, None
    r = lax.rsqrt(_half_sum(a * a) + EPS)
    return a * r * scale, sig, a * r, r


def _gdn_inputs(pre_refs, cw_refs, ba_ref, alog_ref, dt_ref, hp, act_sc, b_sc, gc_sc, c_sc=None):
    for idx in range(3):
        c = _conv_fwd(pre_refs[idx][...], [cw_refs[idx][k:k + 1, :] for k in range(4)])
        if c_sc is not None:
            c_sc[idx] = c
        act_sc[idx] = _gdn_act(c, _DN_SCALE[idx])[0]
    beta, g = _gdn_gates(ba_ref, alog_ref, dt_ref, hp)[:2]
    b_sc[...] = beta
    gc_sc[...] = _chunk_cumsum(g)


def _gdn_chunk(q, k, v, b, gcc):
    shape = q.shape
    row, lm = _iota2(shape, 0), _iota2(shape, 1) & (HEAD - 1)
    tril, strict, eye = row >= lm, row > lm, row == lm
    eg = jnp.exp(gcc)
    kb, vb = k * b, v * b
    kbg = kb * eg
    grow = jnp.sum(jnp.where(eye, gcc, 0.0), axis=0, keepdims=True)
    dm = jnp.exp(jnp.where(tril, gcc - grow, NEG))
    kk = _pk_nt(kb, k)
    glast = jnp.sum(jnp.where(row == DN_CHUNK - 1, gcc, 0.0), axis=0, keepdims=True)
    ekd = jnp.exp(glast - gcc)
    qk = _pk_nt(q, k)
    return dict(q=q, k=k, v=v, b=b, tril=tril, strict=strict, eye=eye, row=row, eg=eg, kb=kb, vb=vb, kbg=kbg, dm=dm, kk=kk,
                low=jnp.where(strict, kk * dm, 0.0), glast=glast, ekd=ekd, kd=k * ekd, qk=qk,
                amat=jnp.where(tril, qk * dm, 0.0), qg=q * eg, egl=jnp.broadcast_to(jnp.exp(glast), shape))


def _tri_inv_many(chunks):
    ms = [-m["low"] for m in chunks]
    ts = [m["eye"].astype(F32) + x for m, x in zip(chunks, ms)]
    for _ in range(int(math.log2(HEAD)) - 1):
        ms = [_pk_nn(x, x, hi=True) for x in ms]
        ts = [t + _pk_nn(t, x, hi=True) for t, x in zip(ts, ms)]
    return ts


def _gdn_chunk_loop(nc, act_sc, b_sc, gc_sc, finish):
    u = math.gcd(nc, DN_UNROLL)

    def step(i, carry):
        rows = [pl.ds(pl.multiple_of((i * u + j) * DN_CHUNK, DN_CHUNK), DN_CHUNK) for j in range(u)]
        chunks = [_gdn_chunk(act_sc[0, r, :], act_sc[1, r, :], act_sc[2, r, :], b_sc[r, :], gc_sc[r, :]) for r in rows]
        pending = [finish(r, m, t) for r, m, t in zip(rows, chunks, _tri_inv_many(chunks))]
        pending = [g for g in pending if g is not None]
        while pending:
            for g in list(pending):
                if next(g, StopIteration) is StopIteration:
                    pending.remove(g)
        return carry

    lax.fori_loop(0, nc // u, step, 0)


def _gdn_in_specs(seq):
    u_at = lambda col: pl.BlockSpec((seq, 2 * HEAD), lambda b, hp, _c=col: (b, _c + hp))
    cw_at = lambda col: pl.BlockSpec((4, 2 * HEAD), lambda b, hp, _c=col: (0, _c + hp))
    row = pl.BlockSpec((1, 2 * HEAD), lambda b, hp: (0, 0))
    ba = pl.BlockSpec((seq, 2 * HEAD), lambda b, hp: (b, COL_BA))
    return [u_at(COL_DNQ), u_at(COL_DNK), u_at(COL_DNV), ba, cw_at(0), cw_at(2), cw_at(4), row, row]


def _pair(seq, lead=None):
    if lead is None:
        return pl.BlockSpec((seq, 2 * HEAD), lambda b, hp: (b, hp))
    return pl.BlockSpec((lead, seq, 2 * HEAD), lambda b, hp: (0, b, hp))


def _swap(spec):
    return pl.BlockSpec(spec.block_shape, lambda hp, b, _f=spec.index_map: _f(b, hp))


def gdn_prep(u, cw, alog_row, dt_row, *, seq, name):
    t = u.shape[0]
    nc = seq // DN_CHUNK

    def body(q_ref, k_ref, v_ref, ba_ref, cq_ref, ck_ref, cv_ref, alog_ref, dt_ref, loc_ref, egl_ref, act_sc, b_sc, gc_sc):
        _gdn_inputs((q_ref, k_ref, v_ref), (cq_ref, ck_ref, cv_ref), ba_ref, alog_ref, dt_ref, pl.program_id(1),
                    act_sc, b_sc, gc_sc)

        def finish(rows, m, t):
            loc_ref[0, rows, :] = m["qg"]
            loc_ref[1, rows, :] = m["kd"]
            loc_ref[2, rows, :] = _pk_nn(t, m["vb"])
            loc_ref[3, rows, :] = _pk_nn(t, m["kbg"])
            loc_ref[4, rows, :] = m["amat"]
            egl_ref[rows, :] = m["egl"]

        _gdn_chunk_loop(nc, act_sc, b_sc, gc_sc, finish)

    return pl.pallas_call(
        body, name=name, grid=(t // seq, DN_HEADS // 2), in_specs=_gdn_in_specs(seq), out_specs=[_pair(seq, 5), _pair(seq)],
        out_shape=[SDS((5, t, DN_HEADS * HEAD), F32), SDS((t, DN_HEADS * HEAD), F32)],
        scratch_shapes=[pltpu.VMEM((3, seq, 2 * HEAD), F32)] + [pltpu.VMEM((seq, 2 * HEAD), F32)] * 2,
        compiler_params=_params(("parallel", "parallel")))(u, u, u, u, cw, cw, cw, alog_row, dt_row)


def _gated_norm2(o, z, gn):
    r = lax.rsqrt(_half_sum(o * o) * (1.0 / HEAD) + EPS)
    return o * r, _sigmoid(z), r


def gdn_scan(loc, egl, u, gn, *, seq, name):
    t = u.shape[0]
    nc = seq // DN_CHUNK

    npair = DN_HEADS // 2

    def body(loc_ref, egl_ref, z_ref, gn_ref, y_ref, o_ref, vn_ref, st_ref):
        gn = gn_ref[...]
        bdm = _bd_mask()

        def step(c, states):
            rows = pl.ds(pl.multiple_of(c * DN_CHUNK, DN_CHUNK), DN_CHUNK)
            new = [None] * npair

            def pair(hp):
                lanes = pl.ds(hp * 2 * HEAD, 2 * HEAD)
                state = states[hp]
                st_ref[rows, lanes] = _fold(state)
                vn = loc_ref[2, rows, lanes] - _nn(loc_ref[3, rows, lanes], state)
                yield
                o = _nn(loc_ref[0, rows, lanes], state) + _pk_nn(loc_ref[4, rows, lanes], vn)
                new[hp] = state * _row0(egl_ref[rows, lanes]) + jnp.where(bdm, _tn(loc_ref[1, rows, lanes], vn), 0.0)
                yield
                vn_ref[rows, lanes] = vn
                o_ref[rows, lanes] = o
                zz = z_ref[rows, lanes]
                on, sig, _ = _gated_norm2(o, zz, gn)
                y_ref[rows, lanes] = on * gn * (zz * sig)

            _interleave([pair(hp) for hp in range(npair)])
            return tuple(new)

        lax.fori_loop(0, nc, step, tuple(jnp.zeros((2 * HEAD, 2 * HEAD), F32) for _ in range(npair)))

    width = DN_HEADS * HEAD
    rows = pl.BlockSpec((seq, width), lambda b: (b, 0))
    out = SDS((t, width), F32)
    return pl.pallas_call(
        body, name=name, grid=(t // seq,),
        in_specs=[pl.BlockSpec((5, seq, width), lambda b: (0, b, 0)), rows,
                  pl.BlockSpec((seq, width), lambda b: (b, COL_DNZ * 2 * HEAD // width)), _whole((1, 2 * HEAD))],
        out_specs=[rows] * 4, out_shape=[out] * 4, compiler_params=_params(("parallel",)))(loc, egl, u, gn)


def gdn_scan_bwd(loc, egl, u, gn, o, vn, states, dy, *, seq, name):
    t = u.shape[0]
    nc = seq // DN_CHUNK

    def body(loc_ref, egl_ref, z_ref, gn_ref, o_ref, vn_ref, st_ref, dy_ref, dloc_ref, degl_ref, dz_ref, dgn_ref):
        @pl.when((pl.program_id(0) == 0) & (pl.program_id(1) == 0))
        def _():
            dgn_ref[...] = jnp.zeros_like(dgn_ref)

        gn = gn_ref[...]
        bdm = _bd_mask()
        shape = (DN_CHUNK, 2 * HEAD)
        tril = _iota2(shape, 0) >= (_iota2(shape, 1) & (HEAD - 1))

        def step(i, carry):
            ds, dgn = carry
            rows = pl.ds(pl.multiple_of((nc - 1 - i) * DN_CHUNK, DN_CHUNK), DN_CHUNK)
            dy, zz, oo = dy_ref[rows, :], z_ref[rows, :], o_ref[rows, :]
            on, sig, r = _gated_norm2(oo, zz, gn)
            sz = zz * sig
            dz_ref[rows, :] = dy * on * gn * (sig * (1.0 + zz * (1.0 - sig)))
            dgn = dgn + jnp.sum(dy * on * sz, axis=0, keepdims=True)
            don = dy * gn * sz
            do = r * (don - on * _half_sum(don * on) * (1.0 / HEAD))
            state, vnew = _bd(st_ref[rows, :]), vn_ref[rows, :]
            qg, kd, w, amat = loc_ref[0, rows, :], loc_ref[1, rows, :], loc_ref[3, rows, :], loc_ref[4, rows, :]
            dvn = _pk_tn(amat, do) + _nn(kd, ds)
            dloc_ref[0, rows, :] = _nt(do, state)
            dloc_ref[1, rows, :] = _nt(vnew, ds)
            dloc_ref[2, rows, :] = dvn
            dloc_ref[3, rows, :] = -_nt(dvn, state)
            dloc_ref[4, rows, :] = jnp.where(tril, _pk_nt(do, vnew), 0.0)
            degl = _half_sum(jnp.sum(state * ds, axis=0, keepdims=True))
            degl_ref[rows, :] = jnp.broadcast_to(degl, shape)
            grow = jnp.where(bdm, _tn(qg, do) - _tn(w, dvn), 0.0)
            return ds * _row0(egl_ref[rows, :]) + grow, dgn

        _, dgn = lax.fori_loop(0, nc, step, (jnp.zeros((2 * HEAD, 2 * HEAD), F32), jnp.zeros((1, 2 * HEAD), F32)))
        dgn_ref[...] += dgn

    zspec = pl.BlockSpec((seq, 2 * HEAD), lambda b, hp: (b, COL_DNZ + hp))
    one = _pair(seq)
    out = SDS((t, DN_HEADS * HEAD), F32)
    return pl.pallas_call(
        body, name=name, grid=(t // seq, DN_HEADS // 2),
        in_specs=[_pair(seq, 5), one, zspec, _whole((1, 2 * HEAD)), one, one, one, one],
        out_specs=[_pair(seq, 5), one, one, _whole((1, 2 * HEAD))],
        out_shape=[SDS((5, t, DN_HEADS * HEAD), F32), out, out, SDS((1, 2 * HEAD), F32)],
        compiler_params=_params(("arbitrary", "arbitrary")))(loc, egl, u, gn, o, vn, states, dy)


def gdn_prep_bwd(u, cw, alog_row, dt_row, dloc, degl, *, seq, name):
    t = u.shape[0]
    nc = seq // DN_CHUNK

    def body(q_ref, k_ref, v_ref, ba_ref, cq_ref, ck_ref, cv_ref, alog_ref, dt_ref, dloc_ref, degl_ref,
             dqkv_ref, dba_ref, dcw_ref, dhs_ref, act_sc, b_sc, gc_sc, c_sc):
        hp = pl.program_id(0)

        @pl.when(pl.program_id(1) == 0)
        def _():
            dcw_ref[...] = jnp.zeros_like(dcw_ref)
            dhs_ref[...] = jnp.zeros_like(dhs_ref)

        pre_refs, cw_refs = (q_ref, k_ref, v_ref), (cq_ref, ck_ref, cv_ref)
        _gdn_inputs(pre_refs, cw_refs, ba_ref, alog_ref, dt_ref, hp, act_sc, b_sc, gc_sc, c_sc)

        def finish(rows, m, tt):
            q, k, v, b = m["q"], m["k"], m["v"], m["b"]
            dqg, dkd, du, dw, da = (dloc_ref[x, rows, :] for x in range(5))
            dm, eg = m["dm"], m["eg"]
            dt = _pk_nt(du, m["vb"]) + _pk_nt(dw, m["kbg"])
            dvb, dkbg = _pk_tn(tt, du), _pk_tn(tt, dw)
            yield
            dtt = _pk_nt(dt, tt, hi=True)
            yield
            dl = jnp.where(m["strict"], -_pk_tn(tt, dtt, hi=True), 0.0)
            yield
            dkk = dl * dm
            dqk = da * dm
            dd = dl * m["kk"] + da * m["qk"]
            dkb = _pk_nn(dkk, k) + dkbg * eg
            dq = _pk_nn(dqk, k) + dqg * eg
            yield
            dk = _pk_tn(dkk, m["kb"]) + _pk_tn(dqk, q) + dkd * m["ekd"] + dkb * b
            db = _half_sum(dkb * k + dvb * v)
            yield
            mx = jnp.where(m["tril"], dd * dm, 0.0)
            tk = _half_sum(dkd * m["kd"])
            colsum = jnp.where(m["eye"], jnp.broadcast_to(jnp.sum(mx, axis=0, keepdims=True), mx.shape), 0.0)
            dgc = _half_sum(mx) - _half_sum(colsum) + _half_sum(dqg * m["qg"] + dkbg * m["kbg"]) - tk
            dglast = jnp.sum(tk, axis=0, keepdims=True) + _row0(degl_ref[rows, :]) * jnp.exp(m["glast"])
            act_sc[0, rows, :] = dq
            act_sc[1, rows, :] = dk
            act_sc[2, rows, :] = dvb * b
            b_sc[rows, :] = db
            gc_sc[rows, :] = dgc + jnp.where(m["row"] == DN_CHUNK - 1, dglast, 0.0)

        _gdn_chunk_loop(nc, act_sc, b_sc, gc_sc, finish)

        beta, g, beta_blk, sp_arg, a_exp, g_blk = _gdn_gates(ba_ref, alog_ref, dt_ref, hp)
        dg = _chunk_rev_cumsum(gc_sc[...])
        lane = _iota2(beta_blk.shape, 1)
        ha = 2 * hp
        db = b_sc[...]
        at = lambda idx, x_a, x_b: (jnp.where(lane == idx, _lane_col(x_a, 0), 0.0)
                                    + jnp.where(lane == idx + 1, _lane_col(x_b, HEAD), 0.0))
        dg_blk = at(DN_HEADS + ha, dg, dg)
        dal = dg_blk * (-a_exp) * _sigmoid(sp_arg)
        dba_ref[...] = at(ha, db, db) * beta_blk * (1.0 - beta_blk) + dal
        dhs_ref[0:1, :] += jnp.sum(dg_blk * g_blk, axis=0, keepdims=True)
        dhs_ref[1:2, :] += jnp.sum(dal, axis=0, keepdims=True)
        for idx in range(3):
            c = c_sc[idx]
            _, sig, hat, r = _gdn_act(c, _DN_SCALE[idx])
            da_ = act_sc[idx]
            if _DN_SCALE[idx] is not None:
                da_ = da_ * _DN_SCALE[idx]
                da_ = r * (da_ - hat * _half_sum(da_ * hat))
            dx, dcw = _conv_bwd(da_ * (sig * (1.0 + c * (1.0 - sig))), pre_refs[idx][...],
                                [cw_refs[idx][k:k + 1, :] for k in range(4)])
            dqkv_ref[idx] = dx
            dcw_ref[idx] += dcw

    pair = DN_HEADS // 2
    in_specs = [_swap(s) for s in _gdn_in_specs(seq)] + [_swap(_pair(seq, 5)), _swap(_pair(seq))]
    return pl.pallas_call(
        body, name=name, grid=(pair, t // seq), in_specs=in_specs,
        out_specs=[_swap(_pair(seq, 3)), pl.BlockSpec((None, seq, 2 * HEAD), lambda hp, b: (hp, b, 0)),
                   pl.BlockSpec((3, 4, 2 * HEAD), lambda hp, b: (0, 0, hp)),
                   pl.BlockSpec((None, 2, 2 * HEAD), lambda hp, b: (hp, 0, 0))],
        out_shape=[SDS((3, t, DN_HEADS * HEAD), F32), SDS((pair, t, 2 * HEAD), F32), SDS((3, 4, DN_HEADS * HEAD), F32),
                   SDS((pair, 2, 2 * HEAD), F32)],
        scratch_shapes=[pltpu.VMEM((3, seq, 2 * HEAD), F32)] + [pltpu.VMEM((seq, 2 * HEAD), F32)] * 2
        + [pltpu.VMEM((3, seq, 2 * HEAD), F32)],
        compiler_params=_params(("arbitrary", "arbitrary")))(u, u, u, u, cw, cw, cw, alog_row, dt_row, dloc, degl)


def mix_out(y_lru, o, y_dn, w_out, h, *, name, tm=512):
    t, d = h.shape
    tm = min(tm, t)

    def body(a_ref, b_ref, c_ref, w_ref, h_ref, o_ref, y_ref):
        y_ref[:, 0:LRU_W] = a_ref[...].astype(BF16)
        y_ref[:, LRU_W:LRU_W + ATT_W] = b_ref[...].astype(BF16)
        y_ref[:, LRU_W + ATT_W:] = c_ref[...].astype(BF16)
        o_ref[...] = h_ref[...] + _nn(y_ref[...], w_ref[...])

    rows = lambda width: pl.BlockSpec((tm, width), lambda i: (i, 0))
    return pl.pallas_call(
        body, name=name, grid=(t // tm,), in_specs=[rows(LRU_W), rows(ATT_W), rows(LRU_W), _whole((d, d)), rows(d)],
        out_specs=[rows(d), rows(d)], out_shape=[SDS((t, d), F32), SDS((t, d), BF16)],
        compiler_params=_params(("parallel",)))(y_lru, o, y_dn, w_out, h)


def mix_out_bwd(dout, w_out, *, name, tm=512):
    t, d = dout.shape
    tm = min(tm, t)

    def body(d_ref, w_ref, a_ref, b_ref, c_ref):
        dy = _nt(d_ref[...], w_ref[...])
        a_ref[...] = dy[:, 0:LRU_W]
        b_ref[...] = dy[:, LRU_W:LRU_W + ATT_W]
        c_ref[...] = dy[:, LRU_W + ATT_W:]

    rows = lambda width: pl.BlockSpec((tm, width), lambda i: (i, 0))
    return pl.pallas_call(
        body, name=name, grid=(t // tm,), in_specs=[rows(d), _whole((d, d))], out_specs=[rows(LRU_W), rows(ATT_W), rows(LRU_W)],
        out_shape=[SDS((t, LRU_W), F32), SDS((t, ATT_W), F32), SDS((t, LRU_W), F32)],
        compiler_params=_params(("parallel",)))(dout, w_out)


def mix_in_bwd(h, gain, dout, w_in, dx, dgate, dq, dk, dv, dqkv, dz, dba, *, name, tm=512):
    t, d = h.shape
    tm = min(tm, t)

    def body(h_ref, g_ref, do_ref, w_ref, dx_ref, dgate_ref, dq_ref, dk_ref, dv_ref, dqkv_ref, dz_ref, dba_ref,
             dh_ref, dg_ref, du_ref):
        @pl.when(pl.program_id(0) == 0)
        def _():
            dg_ref[...] = jnp.zeros_like(dg_ref)

        off = 0
        for piece in (dx_ref[...], dgate_ref[...], dq_ref[...], dk_ref[...], dv_ref[...], dqkv_ref[0], dqkv_ref[1],
                      dqkv_ref[2], dz_ref[...], dba_ref[0] + dba_ref[1]):
            du_ref[:, off:off + piece.shape[1]] = piece.astype(BF16)
            off += piece.shape[1]
        du_ref[:, off:] = jnp.zeros((tm, D_IN_PAD - off), BF16)
        g = g_ref[...]
        _, xh, r = _rms_fwd(h_ref[...], g)
        dh, dg = _rms_bwd(_nt(du_ref[...], w_ref[...]), xh, r, g)
        dh_ref[...] = do_ref[...] + dh
        dg_ref[...] += dg

    rows = lambda width: pl.BlockSpec((tm, width), lambda i: (i, 0))
    return pl.pallas_call(
        body, name=name, grid=(t // tm,),
        in_specs=[rows(d), _whole((1, d)), rows(d), _whole((d, D_IN_PAD)), rows(LRU_W), rows(LRU_W), rows(ATT_W),
                  rows(2 * HEAD), rows(2 * HEAD), pl.BlockSpec((3, tm, DN_HEADS * HEAD), lambda i: (0, i, 0)),
                  rows(DN_HEADS * HEAD), pl.BlockSpec((2, tm, 2 * HEAD), lambda i: (0, i, 0))],
        out_specs=[rows(d), _whole((1, d)), rows(D_IN_PAD)],
        out_shape=[SDS((t, d), F32), SDS((1, d), F32), SDS((t, D_IN_PAD), BF16)],
        compiler_params=_params(("arbitrary",)))(h, gain, dout, w_in, dx, dgate, dq, dk, dv, dqkv, dz, dba)


def _block_diag(w):
    out = jnp.zeros((LRU_W, LRU_W), w.dtype)
    for h in range(LRU_W // HEAD):
        out = lax.dynamic_update_slice(out, w[h], (h * HEAD, h * HEAD))
    return out


def _diag_blocks(w):
    per = LRU_HALF // HEAD
    return jnp.stack([w[h // per, (h % per) * HEAD:(h % per + 1) * HEAD, (h % per) * HEAD:(h % per + 1) * HEAD]
                      for h in range(LRU_W // HEAD)])


def layer_params(w, wl, l, bias):
    row = lambda a: a[l].reshape(1, -1)
    return dict(
        ffn1_norm=row(w["ffn1_norm"]), ffn1=(wl["ffn1_w_gate"], wl["ffn1_w_up"], wl["ffn1_w_down"]),
        mix_norm=row(w["mix_norm"]) + wl["tie1"][0:1, 0:1], w_in=wl["w_in"],
        lru=(wl["lru_conv_w"], row(w["lru_conv_b"]), _block_diag(w["lru_w_a"][l]), row(w["lru_b_a"]),
             _block_diag(w["lru_w_x"][l]), row(w["lru_b_x"]), row(w["lru_lambda"])),
        bias=bias, sink_rows=jnp.repeat(w["attn_sinks"][l], BLOCK_Q).reshape(ATT_HEADS * BLOCK_Q, 1),
        dn_cw=wl["dn_conv_w"], dn_alog=_ba_row(w["dn_a_log"][l]), dn_dt=_ba_row(w["dn_dt_bias"][l]),
        dn_norm=jnp.tile(row(w["dn_norm"]), (1, 2)), w_out=wl["w_out"],
        ffn2_norm=row(w["ffn2_norm"]), ffn2=(wl["ffn2_w_gate"], wl["ffn2_w_up"], wl["ffn2_w_down"]),
        ple_norm=row(w["ple_norm"]), ple_w_gate=wl["ple_w_gate"], ple_w_proj=wl["ple_w_proj"])


def _ba_row(per_head):
    return jnp.pad(per_head, (DN_HEADS, 2 * HEAD - 2 * DN_HEADS)).reshape(1, 2 * HEAD)


def mixer_fwd(h, p, nb, seq, tag):
    u, n = norm_matmul(h, p["mix_norm"], p["w_in"], tn=D_IN_PAD // 2, name=f"mix_in_{tag}")
    y_lru = lru_fwd(u, *p["lru"], seq=seq, name=f"lru_fwd_{tag}")
    o = swa_fwd(u, p["bias"], p["sink_rows"], seq=seq, name=f"swa_fwd_{tag}")
    loc, egl = gdn_prep(u, p["dn_cw"], p["dn_alog"], p["dn_dt"], seq=seq, name=f"gdn_prep_{tag}")
    y_dn, o_raw, vn, st = gdn_scan(loc, egl, u, p["dn_norm"], seq=seq, name=f"gdn_scan_{tag}")
    out, ycat = mix_out(y_lru, o, y_dn, p["w_out"], h, name=f"mix_out_{tag}")
    return out, dict(h=h, u=u, n=n, loc=loc, egl=egl, o_raw=o_raw, vn=vn, st=st, ycat=ycat)


def mixer_bwd(dout, s, p, nb, seq, tag):
    u = s["u"]
    dy_lru, do, dy_dn = mix_out_bwd(dout, p["w_out"], name=f"mix_out_dx_{tag}")
    g = {"w_out": matmul(s["ycat"], dout, ta=True, tm=1024, name=f"mix_out_dw_{tag}")}
    dx, dgate, dcw, dwa, dwx, dvec = lru_bwd(u, *p["lru"], dy_lru, seq=seq, name=f"lru_bwd_{tag}")
    g.update(lru_conv_w=dcw, lru_conv_b=dvec[0], lru_w_a=_diag_blocks(dwa), lru_b_a=dvec[1], lru_w_x=_diag_blocks(dwx),
             lru_b_x=dvec[2], lru_lambda=dvec[3])
    dq, dk, dv, dbias, dsink = swa_bwd(u, p["bias"], p["sink_rows"], do, seq=seq, name=f"swa_bwd_{tag}")
    g.update(attn_sinks=dsink.reshape(ATT_HEADS, BLOCK_Q).sum(axis=1), bias=dbias)
    dloc, degl, dz, dgn = gdn_scan_bwd(s["loc"], s["egl"], u, p["dn_norm"], s["o_raw"], s["vn"], s["st"], dy_dn, seq=seq,
                                       name=f"gdn_scan_bwd_{tag}")
    dqkv, dba, dcw3, dhs = gdn_prep_bwd(u, p["dn_cw"], p["dn_alog"], p["dn_dt"], dloc, degl, seq=seq,
                                        name=f"gdn_prep_bwd_{tag}")
    dhs = dhs.sum(axis=0)[:, DN_HEADS:2 * DN_HEADS]
    g.update(dn_conv_w=dcw3.transpose(1, 0, 2).reshape(4, 3 * DN_HEADS * HEAD), dn_a_log=dhs[0], dn_dt_bias=dhs[1],
             dn_norm=dgn[0, :HEAD] + dgn[0, HEAD:])
    dh, dgain, du = mix_in_bwd(s["h"], p["mix_norm"], dout, p["w_in"], dx, dgate, dq, dk, dv, dqkv, dz, dba,
                               name=f"mix_in_bwd_{tag}")
    g["w_in"] = matmul(s["n"], du, ta=True, tm=1024, tn=640, name=f"mix_in_dw_{tag}")
    g["mix_norm"] = dgain[0]
    return dh, g


SHARDED = ("ffn1_w_gate", "ffn1_w_up", "ffn1_w_down", "w_in", "w_out", "ffn2_w_gate", "ffn2_w_up", "ffn2_w_down",
           "ple_w_gate", "ple_w_proj")
PER_LAYER_SMALL = ("ffn1_norm", "mix_norm", "lru_conv_w", "lru_conv_b", "lru_w_a", "lru_b_a", "lru_w_x", "lru_b_x",
                   "lru_lambda", "attn_sinks", "dn_conv_w", "dn_a_log", "dn_dt_bias", "dn_norm", "ffn2_norm", "ple_norm")


GRAD_PARTS = (("ple_w_gate", "ple_w_proj", "ffn2_w_gate", "ffn2_w_up", "ffn2_w_down"), ("w_in", "w_out"),
              ("ffn1_w_gate", "ffn1_w_up", "ffn1_w_down"))
WEIGHT_PARTS = (("ffn1_w_gate", "ffn1_w_up", "ffn1_w_down"),
                ("w_in", "w_out", "ffn2_w_gate", "ffn2_w_up", "ffn2_w_down", "ple_w_gate", "ple_w_proj", "lru_conv_w",
                 "dn_conv_w"))


def _col_shards(a):
    r, c = a.shape
    return a.reshape(r, N_CHIP, c // N_CHIP).transpose(1, 0, 2)


def local_step(x, p, target, w, layer_weights, layer_grads, bmap, nb, seq):
    bias = relbias_fwd(w["rel_bias"], bmap, name="relbias_fwd")
    h, saved = x, []
    for l in range(N_LAYER):
        wl = layer_weights(l, 0, h)
        s = dict(h0=h)
        h, *s["ffn1"] = ffn_fwd(h, w["ffn1_norm"][l].reshape(1, -1) + wl["tie0"][0:1, 0:1], wl["ffn1_w_gate"],
                                wl["ffn1_w_up"], wl["ffn1_w_down"], name=f"ffn1_fwd_{l}")
        wl.update(layer_weights(l, 1, h))
        pr = layer_params(w, wl, l, bias)
        h, s["mix"] = mixer_fwd(h, pr, nb, seq, l)
        s["h2"] = h
        h, *s["ffn2"] = ffn_fwd(h, pr["ffn2_norm"], *pr["ffn2"], name=f"ffn2_fwd_{l}")
        s["h3"] = h
        h = ple_fwd(h, pr["ple_norm"], pr["ple_w_gate"], p[l], pr["ple_w_proj"], name=f"ple_fwd_{l}")
        saved.append((pr, s))
    dh, dgf, loss = loss_head(h, w["final_norm"].reshape(1, -1), target, name="loss_head")

    per_layer, dbias, token = [None] * N_LAYER, None, None
    for l in reversed(range(N_LAYER)):
        pr, s = saved[l]
        g = {}
        dout = dh
        ple_norm = pr["ple_norm"] if token is None else pr["ple_norm"] + token[0:1, 0:1]
        dh, n, dga, dpp, dg = ple_bwd(s["h3"], ple_norm, pr["ple_w_gate"], p[l], pr["ple_w_proj"], dout, name=f"ple_bwd_{l}")
        g["ple_norm"] = dg[0]
        g["ple_w_gate"] = matmul(n, dga, ta=True, tm=1024, name=f"ple_dwg_{l}").reshape(N_CHIP, -1, D_MODEL)
        g["ple_w_proj"] = _col_shards(matmul(p[l], dpp, ta=True, name=f"ple_dwp_{l}"))
        for nm, hin in (("ffn2", s["h2"]), ("ffn1", s["h0"])):
            if nm == "ffn1":
                lru = list(pr["lru"])
                lru[1] = lru[1] + token[0:1, 0:1]
                dh, gm = mixer_bwd(dh, s["mix"], dict(pr, lru=tuple(lru)), nb, seq, l)
                dbias = gm.pop("bias") if dbias is None else dbias + gm.pop("bias")
                gm["w_in"] = _col_shards(gm["w_in"][:, :D_IN])
                gm["w_out"] = gm["w_out"].reshape(N_CHIP, -1, D_MODEL)
                g.update(gm)
                token = layer_grads(l, 1, {k: g.pop(k) for k in GRAD_PARTS[1]}, dh)
            dout = dh
            n, a, b = s[nm]
            dh, da, db, sact, dg = ffn_bwd_act(hin, pr[nm + "_norm"] + token[0:1, 0:1] if nm == "ffn1" else pr[nm + "_norm"],
                                               dout, a, b, *pr[nm], name=f"{nm}_bwd_act_{l}")
            g[nm + "_norm"] = dg[0]
            g[nm + "_w_gate"], g[nm + "_w_up"], g[nm + "_w_down"] = ffn_bwd_w(n, da, db, sact, dout, name=f"{nm}_bwd_w_{l}")
            part = 0 if nm == "ffn2" else 2
            token = layer_grads(l, part, {k: g.pop(k) for k in GRAD_PARTS[part]}, dh)
        per_layer[l] = g
    grads = {k: jnp.stack([per_layer[l][k] for l in range(N_LAYER)]) for k in PER_LAYER_SMALL}
    grads["rel_bias"] = relbias_bwd(dbias, bmap, name="relbias_bwd")[:, :ATT_HEADS]
    grads["final_norm"] = dgf[0]
    return loss, dh, grads


HBM_SPEC = pl.BlockSpec(memory_space=pltpu.HBM)


def _place():
    x, y, c = lax.axis_index("x"), lax.axis_index("y"), lax.axis_index("c")
    chips = [(1 - x, y), (x, 1 - y), (1 - x, 1 - y)]
    return x, y, c, 2 * x + y, (x, y, 1 - c), chips, [2 * cx + cy for cx, cy in chips]


def _remote(src, dst, send_sem, recv_sem, to):
    return pltpu.make_async_remote_copy(src_ref=src, dst_ref=dst, send_sem=send_sem, recv_sem=recv_sem, device_id=to,
                                        device_id_type=MESH)


N_DEV = 8


def allreduce_small(buf, *, name):
    rows = buf.shape[0]

    def body(in_ref, out_ref, gath, send, recv):
        x, y, c = lax.axis_index("x"), lax.axis_index("y"), lax.axis_index("c")
        mine = 4 * x + 2 * y + c
        gath[mine] = in_ref[...]
        cps = []
        for k in range(1, N_DEV):
            to = (x ^ (k >> 2), y ^ ((k >> 1) & 1), c ^ (k & 1))
            cps.append(_remote(in_ref, gath.at[mine], send.at[k - 1], recv.at[k - 1], to))
            cps[-1].start()
        for k in range(1, N_DEV):
            theirs = gath.at[4 * (x ^ (k >> 2)) + 2 * (y ^ ((k >> 1) & 1)) + (c ^ (k & 1))]
            _remote(theirs, theirs, send.at[k - 1], recv.at[k - 1], (x, y, c)).wait_recv()
        for cp in cps:
            cp.wait_send()
        acc = gath[0]
        for d in range(1, N_DEV):
            acc = acc + gath[d]
        out_ref[...] = acc

    vm = pl.BlockSpec(memory_space=pltpu.VMEM)
    return pl.pallas_call(
        body, name=name, in_specs=[vm], out_specs=vm, out_shape=SDS(buf.shape, F32),
        scratch_shapes=[pltpu.VMEM((N_DEV, rows, 128), F32), pltpu.SemaphoreType.DMA((N_DEV - 1,)),
                        pltpu.SemaphoreType.DMA((N_DEV - 1,))])(buf)


SEM_SPEC = pl.BlockSpec(memory_space=pltpu.SEMAPHORE)
ANY_SPEC = pl.BlockSpec(memory_space=pl.ANY)
DATAFLOW = pltpu.SideEffectType.DATAFLOW_SIDE_EFFECTING


def _in_hbm(a):
    return pltpu.with_memory_space_constraint(a, pltpu.HBM)


def _my_rows(ref_rows, c, mine=True):
    half = ref_rows // 2
    start = (c if mine else 1 - c) * half
    return pl.ds(pl.multiple_of(start, 8), half)


def place_layer_shard(w, layer, chip_arr, dtype, after, *, name):
    _, r, c = w.shape
    tr = next(cand for cand in (256, 128, 64, 32, 16, 8, r) if r % cand == 0)

    def body(chip_ref, w_ref, after_ref, o_ref):
        o_ref[...] = w_ref[...].astype(dtype)

    return pl.pallas_call(
        body, name=name,
        grid_spec=pltpu.PrefetchScalarGridSpec(
            num_scalar_prefetch=1, grid=(r // tr,),
            in_specs=[pl.BlockSpec((None, tr, c), lambda i, chip: (layer, i, 0)), ANY_SPEC],
            out_specs=pl.BlockSpec((None, tr, c), lambda i, chip: (chip[0], i, 0))),
        out_shape=SDS((N_CHIP, r, c), dtype), compiler_params=_params(("parallel",)))(chip_arr, w, after)


def _gather_pieces(refs, n_split, c, me, cids):
    mine, theirs = [], []
    for k, ref in enumerate(refs):
        if k < n_split:
            rows = _my_rows(ref.shape[1], c)
            mine.append(ref.at[me, rows])
            theirs.append([ref.at[cid, rows] for cid in cids])
        else:
            mine.append(ref.at[me])
            theirs.append([ref.at[cid] for cid in cids])
    return mine, theirs


def gather_start(bufs, n_split, after, *, name):
    n = len(bufs)

    def body(*refs):
        ins, send, recv, token = refs[:n], refs[n + 1], refs[n + 2], refs[-1]
        x, y, c, me, sib, chips, cids = _place()
        mine, _ = _gather_pieces(ins, n_split, c, me, cids)
        for k in range(n):
            for j, chip in enumerate(chips):
                _remote(mine[k], mine[k], send.at[3 * k + j], recv.at[3 * k + j], (*chip, c)).start()
        token[...] = jnp.zeros_like(token)

    out = pl.pallas_call(
        body, name=name, in_specs=[HBM_SPEC] * n + [ANY_SPEC],
        out_specs=[SEM_SPEC, SEM_SPEC] + [HBM_SPEC] * n + [pl.BlockSpec(memory_space=pltpu.VMEM)],
        out_shape=[pltpu.SemaphoreType.DMA((3 * n,)), pltpu.SemaphoreType.DMA((3 * n,))]
        + [pltpu.HBM(b.shape, b.dtype) for b in bufs] + [SDS((8, 128), F32)],
        input_output_aliases={k: k + 2 for k in range(n)},
        compiler_params=pltpu.CompilerParams(has_side_effects=DATAFLOW))(*[_in_hbm(b) for b in bufs], after)
    return out[0], out[1], list(out[2:2 + n]), out[-1]


def gather_wait(send, recv, bufs, n_split, after, *, name):
    n = len(bufs)

    def body(*refs):
        ins, send_ref, recv_ref = refs[:n], refs[n], refs[n + 1]
        x, y, c, me, sib, chips, cids = _place()
        mine, theirs = _gather_pieces(ins, n_split, c, me, cids)
        for k in range(n):
            for j in range(3):
                _remote(mine[k], mine[k], send_ref.at[3 * k + j], recv_ref.at[3 * k + j], sib).wait_send()
                _remote(theirs[k][j], theirs[k][j], send_ref.at[3 * k + j], recv_ref.at[3 * k + j], sib).wait_recv()

    return list(pl.pallas_call(
        body, name=name, in_specs=[HBM_SPEC] * n + [SEM_SPEC, SEM_SPEC, ANY_SPEC], out_specs=[HBM_SPEC] * n,
        out_shape=[pltpu.HBM(b.shape, b.dtype) for b in bufs], input_output_aliases={k: k for k in range(n)},
        compiler_params=pltpu.CompilerParams(has_side_effects=DATAFLOW))(*bufs, send, recv, after))


def gather_forward(bufs, *, name):
    n = len(bufs)

    def body(*refs):
        outs, (send, recv) = refs[n:2 * n], refs[2 * n:]
        x, y, c, me, sib, chips, cids = _place()
        cps = []
        for k in range(n):
            for j in range(3):
                piece = outs[k].at[cids[j], _my_rows(outs[k].shape[1], c)]
                cps.append(_remote(piece, piece, send.at[3 * k + j], recv.at[3 * k + j], sib))
                cps[-1].start()
        for k in range(n):
            for j in range(3):
                piece = outs[k].at[cids[j], _my_rows(outs[k].shape[1], c, mine=False)]
                _remote(piece, piece, send.at[3 * k + j], recv.at[3 * k + j], sib).wait_recv()
        for cp in cps:
            cp.wait_send()

    return list(pl.pallas_call(
        body, name=name, in_specs=[HBM_SPEC] * n, out_specs=[HBM_SPEC] * n, out_shape=[SDS(b.shape, b.dtype) for b in bufs],
        input_output_aliases={k: k for k in range(n)}, scratch_shapes=[pltpu.SemaphoreType.DMA((3 * n,))] * 2)(*bufs))


def _exchange_copies(ins, lands, send, recv, c, sib):
    return [_remote(ins[k].at[pl.ds(0, N_CHIP), _my_rows(ins[k].shape[1], c, mine=False)], lands[k], send.at[k],
                    recv.at[k], sib) for k in range(len(ins))]


def exchange_start(gs, *, name):
    n = len(gs)

    def body(*refs):
        ins, lands, send, recv, token = refs[:n], refs[n:2 * n], refs[2 * n], refs[2 * n + 1], refs[-1]
        x, y, c, me, sib, chips, cids = _place()
        for cp in _exchange_copies(ins, lands, send, recv, c, sib):
            cp.start()
        token[...] = jnp.zeros_like(token)

    lands = [_in_hbm(lax.empty((N_CHIP, g.shape[1] // 2, g.shape[2]), g.dtype)) for g in gs]
    out = pl.pallas_call(
        body, name=name, in_specs=[HBM_SPEC] * (2 * n),
        out_specs=[SEM_SPEC, SEM_SPEC] + [HBM_SPEC] * (2 * n) + [pl.BlockSpec(memory_space=pltpu.VMEM)],
        out_shape=[pltpu.SemaphoreType.DMA((n,)), pltpu.SemaphoreType.DMA((n,))]
        + [pltpu.HBM(b.shape, b.dtype) for b in list(gs) + lands] + [SDS((8, 128), F32)],
        input_output_aliases={k: k + 2 for k in range(2 * n)},
        compiler_params=pltpu.CompilerParams(has_side_effects=DATAFLOW))(*[_in_hbm(g) for g in gs], *lands)
    return out[0], out[1], list(out[2:2 + n]), list(out[2 + n:2 + 2 * n]), out[-1]


def exchange_wait(send, recv, gs, lands, after, *, name):
    n = len(gs)

    def body(*refs):
        ins, land_refs, send_ref, recv_ref = refs[:n], refs[n:2 * n], refs[2 * n], refs[2 * n + 1]
        x, y, c, me, sib, chips, cids = _place()
        for cp in _exchange_copies(ins, land_refs, send_ref, recv_ref, c, sib):
            cp.wait_send()
            cp.wait_recv()

    out = pl.pallas_call(
        body, name=name, in_specs=[HBM_SPEC] * (2 * n) + [SEM_SPEC, SEM_SPEC, ANY_SPEC], out_specs=[HBM_SPEC] * (2 * n),
        out_shape=[pltpu.HBM(b.shape, b.dtype) for b in list(gs) + list(lands)],
        input_output_aliases={k: k for k in range(2 * n)},
        compiler_params=pltpu.CompilerParams(has_side_effects=DATAFLOW))(*gs, *lands, send, recv, after)
    return list(out[:n]), list(out[n:])


def _half_tile(half):
    return next(cand for cand in (256, 176, 128, 64, 32, 16) if half % cand == 0)


def reduce_add(g, r, c_arr, *, name):
    _, rows, cdim = g.shape
    half = rows // 2
    tr = _half_tile(half)

    def body(c_ref, g_ref, r_ref, o_ref):
        o_ref[...] = (g_ref[...] + r_ref[...]).astype(o_ref.dtype)

    return pl.pallas_call(
        body, name=name,
        grid_spec=pltpu.PrefetchScalarGridSpec(
            num_scalar_prefetch=1, grid=(N_CHIP, half // tr),
            in_specs=[pl.BlockSpec((None, tr, cdim), lambda j, i, c: (j, c[0] * (half // tr) + i, 0)),
                      pl.BlockSpec((None, tr, cdim), lambda j, i, c: (j, i, 0))],
            out_specs=pl.BlockSpec((None, tr, cdim), lambda j, i, c: (j, i, 0))),
        out_shape=SDS((N_CHIP, half, cdim), BF16), compiler_params=_params(("parallel", "parallel")))(c_arr, g, r)


def reduce_start(ss, *, name):
    n = len(ss)

    def body(*refs):
        ins, lands, send, recv, token = refs[:n], refs[n:2 * n], refs[2 * n], refs[2 * n + 1], refs[-1]
        x, y, c, me, sib, chips, cids = _place()
        for k in range(n):
            for j, chip in enumerate(chips):
                _remote(ins[k].at[cids[j]], lands[k].at[j], send.at[3 * k + j], recv.at[3 * k + j], (*chip, c)).start()
        token[...] = jnp.zeros_like(token)

    lands = [_in_hbm(lax.empty((N_CHIP - 1,) + s.shape[1:], s.dtype)) for s in ss]
    out = pl.pallas_call(
        body, name=name, in_specs=[HBM_SPEC] * (2 * n),
        out_specs=[SEM_SPEC, SEM_SPEC] + [HBM_SPEC] * (2 * n) + [pl.BlockSpec(memory_space=pltpu.VMEM)],
        out_shape=[pltpu.SemaphoreType.DMA((3 * n,)), pltpu.SemaphoreType.DMA((3 * n,))]
        + [pltpu.HBM(b.shape, b.dtype) for b in list(ss) + lands] + [SDS((8, 128), F32)],
        input_output_aliases={k: k + 2 for k in range(2 * n)},
        compiler_params=pltpu.CompilerParams(has_side_effects=DATAFLOW))(*[_in_hbm(s) for s in ss], *lands)
    return out[0], out[1], list(out[2:2 + n]), list(out[2 + n:2 + 2 * n]), out[-1]


def reduce_wait(send, recv, ss, lands, after, *, name):
    n = len(ss)

    def body(*refs):
        ins, land_refs, send_ref, recv_ref = refs[:n], refs[n:2 * n], refs[2 * n], refs[2 * n + 1]
        x, y, c, me, sib, chips, cids = _place()
        for k in range(n):
            for j in range(3):
                _remote(ins[k].at[cids[j]], land_refs[k].at[j], send_ref.at[3 * k + j], recv_ref.at[3 * k + j],
                        sib).wait_send()
                _remote(ins[k].at[cids[j]], land_refs[k].at[j], send_ref.at[3 * k + j], recv_ref.at[3 * k + j],
                        sib).wait_recv()

    out = pl.pallas_call(
        body, name=name, in_specs=[HBM_SPEC] * (2 * n) + [SEM_SPEC, SEM_SPEC, ANY_SPEC], out_specs=[HBM_SPEC] * (2 * n),
        out_shape=[pltpu.HBM(b.shape, b.dtype) for b in list(ss) + list(lands)],
        input_output_aliases={k: k for k in range(2 * n)},
        compiler_params=pltpu.CompilerParams(has_side_effects=DATAFLOW))(*ss, *lands, send, recv, after)
    return list(out[:n]), list(out[n:])


def reduce_sum(own, land, place_arr, layer, acc, *, name):
    _, half, cdim = land.shape
    tr = _half_tile(half)

    def body(p_ref, own_ref, land_ref, *rest):
        o_ref = rest[-1]
        o_ref[...] = ((own_ref[...].astype(F32) + land_ref[0].astype(F32)) + land_ref[1].astype(F32)) + land_ref[2].astype(F32)

    in_specs = [pl.BlockSpec((None, tr, cdim), lambda i, p: (p[0], i, 0)),
                pl.BlockSpec((N_CHIP - 1, tr, cdim), lambda i, p: (0, i, 0))]
    args = [place_arr, own, land]
    if acc is not None:
        in_specs.append(ANY_SPEC)
        args.append(acc)
    return pl.pallas_call(
        body, name=name,
        grid_spec=pltpu.PrefetchScalarGridSpec(
            num_scalar_prefetch=1, grid=(half // tr,), in_specs=in_specs,
            out_specs=pl.BlockSpec((None, tr, cdim), lambda i, p: (layer, p[1] * (half // tr) + i, 0))),
        out_shape=SDS((N_LAYER, 2 * half, cdim), F32), input_output_aliases={} if acc is None else {3: 0},
        compiler_params=_params(("parallel",)))(*args)


def reduce_share(fs, *, name):
    n = len(fs)

    def body(*refs):
        outs, (send, recv) = refs[n:2 * n], refs[2 * n:]
        x, y, c, me, sib, chips, cids = _place()
        cps = []
        for k in range(n):
            piece = outs[k].at[pl.ds(0, N_LAYER), _my_rows(outs[k].shape[1], c)]
            cps.append(_remote(piece, piece, send.at[k], recv.at[k], sib))
            cps[-1].start()
        for k in range(n):
            theirs = outs[k].at[pl.ds(0, N_LAYER), _my_rows(outs[k].shape[1], c, mine=False)]
            _remote(theirs, theirs, send.at[k], recv.at[k], sib).wait_recv()
        for cp in cps:
            cp.wait_send()

    return list(pl.pallas_call(
        body, name=name, in_specs=[HBM_SPEC] * n, out_specs=[HBM_SPEC] * n, out_shape=[SDS(f.shape, f.dtype) for f in fs],
        input_output_aliases={k: k for k in range(n)}, scratch_shapes=[pltpu.SemaphoreType.DMA((n,))] * 2)(*fs))


WEIGHTS = ("ffn1_norm", "ffn1_w_gate", "ffn1_w_up", "ffn1_w_down", "mix_norm", "w_in", "lru_conv_w", "lru_conv_b", "lru_w_a",
           "lru_b_a", "lru_w_x", "lru_b_x", "lru_lambda", "attn_sinks", "rel_bias", "dn_conv_w", "dn_a_log", "dn_dt_bias",
           "dn_norm", "w_out", "ffn2_norm", "ffn2_w_gate", "ffn2_w_up", "ffn2_w_down", "ple_norm", "ple_w_gate",
           "ple_w_proj", "final_norm")
CONV_SHARDED = ("lru_conv_w", "dn_conv_w")
FFN_TRANSPOSED = ("ffn1_w_gate", "ffn1_w_up", "ffn2_w_gate", "ffn2_w_up")
SMALL = tuple(k for k in WEIGHTS if k not in SHARDED)


def _pack(arrs):
    blocks = []
    for a in arrs:
        v = a.reshape(-1)
        blocks.append(jnp.pad(v, (0, -v.shape[0] % 1024)).reshape(-1, 128))
    return jnp.concatenate(blocks, axis=0)


def _unpack(buf, shapes):
    out, off = [], 0
    for s in shapes:
        n = int(np.prod(s))
        rows = 8 * -(-n // 1024)
        out.append(buf[off:off + rows].reshape(-1)[:n].reshape(s))
        off += rows
    return out


def kernel(x, p, ffn1_norm, ffn1_w_gate, ffn1_w_up, ffn1_w_down, mix_norm, w_in, lru_conv_w, lru_conv_b, lru_w_a, lru_b_a, lru_w_x, lru_b_x, lru_lambda, attn_sinks, rel_bias, dn_conv_w, dn_a_log, dn_dt_bias, dn_norm, w_out, ffn2_norm, ffn2_w_gate, ffn2_w_up, ffn2_w_down, ple_norm, ple_w_gate, ple_w_proj, final_norm, loss_target, m_ffn1_norm, m_ffn1_w_gate, m_ffn1_w_up, m_ffn1_w_down, m_mix_norm, m_w_in, m_lru_conv_w, m_lru_conv_b, m_lru_w_a, m_lru_b_a, m_lru_w_x, m_lru_b_x, m_lru_lambda, m_attn_sinks, m_rel_bias, m_dn_conv_w, m_dn_a_log, m_dn_dt_bias, m_dn_norm, m_w_out, m_ffn2_norm, m_ffn2_w_gate, m_ffn2_w_up, m_ffn2_w_down, m_ple_norm, m_ple_w_gate, m_ple_w_proj, m_final_norm, v_ffn1_norm, v_ffn1_w_gate, v_ffn1_w_up, v_ffn1_w_down, v_mix_norm, v_w_in, v_lru_conv_w, v_lru_conv_b, v_lru_w_a, v_lru_b_a, v_lru_w_x, v_lru_b_x, v_lru_lambda, v_attn_sinks, v_rel_bias, v_dn_conv_w, v_dn_a_log, v_dn_dt_bias, v_dn_norm, v_w_out, v_ffn2_norm, v_ffn2_w_gate, v_ffn2_w_up, v_ffn2_w_down, v_ple_norm, v_ple_w_gate, v_ple_w_proj, v_final_norm):
    given = dict(locals())
    stored = lambda k, a: jnp.swapaxes(a, 1, 2) if k in FFN_TRANSPOSED else a
    ws = {k: stored(k, given[k]) for k in WEIGHTS}
    ms = {k: stored(k, given["m_" + k]) for k in WEIGHTS}
    vs = {k: stored(k, given["v_" + k]) for k in WEIGHTS}
    nb, seq, d = x.shape
    t = nb * seq
    cx, cy, cc = lax.axis_index("x"), lax.axis_index("y"), lax.axis_index("c")
    chip = 2 * cx + cy

    chip_arr = chip.astype(jnp.int32).reshape(1)
    c_arr = cc.astype(jnp.int32).reshape(1)
    place_arr = jnp.stack([chip, cc]).astype(jnp.int32)
    groups = [(l, part) for l in range(N_LAYER) for part in range(len(WEIGHT_PARTS))]
    placed, started = {}, {}

    def place_group(i, after):
        l, part = groups[i]
        for k in WEIGHT_PARTS[part]:
            placed[l, k] = place_layer_shard(ws[k], l, chip_arr, F32 if k in CONV_SHARDED else BF16, after,
                                             name=f"place_{k}_{l}")

    def start_group(i, after):
        l, part = groups[i]
        ks = WEIGHT_PARTS[part]
        n_split = sum(k in SHARDED for k in ks)
        started[i] = (ks, n_split) + gather_start([placed[l, k] for k in ks], n_split, after, name=f"gather_start_{l}_{part}")

    place_group(0, jnp.zeros((8, 128), F32))
    start_group(0, jnp.zeros((8, 128), F32))
    for i in range(1, len(groups)):
        place_group(i, started[0][-1])

    def layer_weights(l, part, h):
        i = groups.index((l, part))
        ks, n_split, send, recv, bufs, _ = started[i]
        bufs = gather_wait(send, recv, bufs, n_split, h, name=f"gather_wait_{l}_{part}")
        tie = jnp.zeros((8, 128), F32)
        for nxt in [j for j in range(i + 1, len(groups)) if j not in started and groups[j][0] == groups[min(i + 1, len(groups) - 1)][0]]:
            start_group(nxt, bufs[0] if nxt == i + 1 else started[nxt - 1][-1])
            tie = started[nxt][-1]
        wl = dict(zip(ks, gather_forward(bufs[:n_split], name=f"gather_forward_{l}_{part}") + bufs[n_split:]))
        for k in ("w_in", "ple_w_proj", "lru_conv_w", "dn_conv_w"):
            if k in wl:
                wl[k] = wl[k].transpose(1, 0, 2).reshape(wl[k].shape[1], -1)
        for k in ("w_out", "ple_w_gate"):
            if k in wl:
                wl[k] = wl[k].reshape(-1, wl[k].shape[-1])
        if "w_in" in wl:
            wl["w_in"] = jnp.pad(wl["w_in"], ((0, 0), (0, D_IN_PAD - D_IN)))
        wl[f"tie{part}"] = tie
        return wl

    pending, finished, tokens = [], {k: None for k in SHARDED}, []

    def finish_reduce(after):
        ks, send, recv, sums, lands, l, part = pending.pop(0)
        sums, lands = reduce_wait(send, recv, sums, lands, after, name=f"reduce_wait_{l}_{part}")
        for k, s, land in zip(ks, sums, lands):
            finished[k] = reduce_sum(s, land, place_arr, l, finished[k], name=f"reduce_sum_{k}_{l}")

    swapping = []

    def start_reduce(after):
        ks, send, recv, gs, theirs, l, part = swapping.pop(0)
        gs, theirs = exchange_wait(send, recv, gs, theirs, after, name=f"exchange_wait_{l}_{part}")
        sums = [reduce_add(a, b, c_arr, name=f"reduce_add_{k}_{l}") for k, a, b in zip(ks, gs, theirs)]
        send, recv, sums, lands, token = reduce_start(sums, name=f"reduce_start_{l}_{part}")
        pending.append((ks, send, recv, sums, lands, l, part))
        tokens.append(token)
        return token

    def layer_grads(l, part, g, dh):
        ks = GRAD_PARTS[part]
        send, recv, gs, theirs, token = exchange_start([g[k] for k in ks], name=f"exchange_start_{l}_{part}")
        swapping.append((ks, send, recv, gs, theirs, l, part))
        if len(swapping) > 1:
            token = token + start_reduce(dh)
        while len(pending) > 2:
            finish_reduce(dh)
        return token

    small_w = {k: ws[k] for k in SMALL if k not in CONV_SHARDED}
    bmap = jnp.asarray(_rel_bucket_map())
    loss, gx, grads = local_step(x.reshape(t, d), p.reshape(N_LAYER, t, PLE_DIM), loss_target.reshape(t, d), small_w,
                                 layer_weights, layer_grads, bmap, nb, seq)
    while swapping:
        start_reduce(gx)
    g_out, delta, new_m, new_v = {}, {}, {}, {}

    small_shapes = [grads[k].shape for k in SMALL]
    g_small = dict(zip(SMALL, _unpack(allreduce_small(_pack([grads[k] for k in SMALL]), name="allreduce_small"), small_shapes)))
    for k in CONV_SHARDED:
        width = ws[k].shape[-1]
        g_small[k] = lax.dynamic_slice_in_dim(g_small[k], chip * width, width, axis=2)
    g_out.update(g_small)
    shapes = [ws[k].shape for k in SMALL]
    tie = tokens[-1][0:1, 0:1]
    res = adamw(_pack([ws[k] for k in SMALL]) + tie, *[_pack([src[k] for k in SMALL]) for src in (g_out, ms, vs)],
                name="adamw_small")
    for dst, r in zip((delta, new_m, new_v), res):
        dst.update(zip(SMALL, _unpack(r, shapes)))

    after = res[0]
    for part, ks in enumerate(GRAD_PARTS):
        while pending and pending[0][0] == ks:
            finish_reduce(after)
        g_out.update(zip(ks, reduce_share([finished[k] for k in ks], name=f"reduce_share_{part}")))
        for k in ks:
            two_d = lambda a: a.reshape(-1, a.shape[-1])
            res = adamw(two_d(ws[k]), two_d(g_out[k]), two_d(ms[k]), two_d(vs[k]), name=f"adamw_{k}")
            delta[k], new_m[k], new_v[k] = (r.reshape(ws[k].shape) for r in res)
        after = res[0]

    total = lax.psum(loss[0, 0], ("x", "y", "c"))
    return (total, gx.reshape(nb, seq, d), *[stored(k, out[k]) for out in (g_out, delta, new_m, new_v) for k in WEIGHTS])
```

```python
import math

import numpy as np
import jax
import jax.numpy as jnp
from jax import lax
from jax.experimental import pallas as pl
from jax.experimental.pallas import tpu as pltpu

F32 = jnp.float32
BF16 = jnp.bfloat16

EPS = 1e-6
D_MODEL = 1024
D_FF = 2816
N_CHIP = 4
FF_BLK = D_FF // N_CHIP
HEAD = 64
LRU_W = 256
ATT_W = 512
ATT_HEADS = 8
KV_HEADS = 2
ATT_GROUP = 4
BLOCK_Q = 128
DN_HEADS = 4
DN_CHUNK = 64
D_IN = 2312
D_IN_PAD = 2560
PLE_DIM = 256
REL_BUCKETS = 32
LRU_C = 8.0
N_LAYER = 2

ADAM_LR, ADAM_B1, ADAM_B2, ADAM_EPS, ADAM_WD, ADAM_STEP = 0.001, 0.9, 0.999, 1e-08, 0.01, 10

VMEM_LIMIT = 56 << 20
MESH = pl.DeviceIdType.MESH
SDS = jax.ShapeDtypeStruct


def _dot(a, b, ca=1, cb=0, hi=False):
    dims = (((ca,), (cb,)), ((), ()))
    one = lambda u, v: lax.dot_general(u, v, dims, preferred_element_type=F32)
    a_hi, b_hi = a.astype(BF16), b.astype(BF16)
    if not hi:
        return one(a_hi, b_hi)
    a_lo = (a - a_hi.astype(F32)).astype(BF16)
    b_lo = (b - b_hi.astype(F32)).astype(BF16)
    return one(a_hi, b_hi) + (one(a_hi, b_lo) + one(a_lo, b_hi))


def _nn(a, b, hi=False):
    return _dot(a, b, 1, 0, hi)


def _nt(a, b, hi=False):
    return _dot(a, b, 1, 1, hi)


def _tn(a, b, hi=False):
    return _dot(a, b, 0, 0, hi)


def _sigmoid(x):
    return jax.nn.sigmoid(x)


def _softplus(x):
    return jnp.maximum(x, 0.0) + jnp.log1p(jnp.exp(-jnp.abs(x)))


def _neg_expm1(z):
    series = -z * (1.0 + z * (0.5 + z * (1.0 / 6.0 + z * (1.0 / 24.0 + z * (1.0 / 120.0)))))
    return jnp.where(z > -0.05, series, 1.0 - jnp.exp(z))


_GELU_C = math.sqrt(2.0 / math.pi)


def _gelu(x):
    t = jnp.tanh(_GELU_C * (x + 0.044715 * x * x * x))
    return 0.5 * x * (1.0 + t), t


def _gelu_grad(x, t):
    return 0.5 * (1.0 + t) + 0.5 * x * (1.0 - t * t) * _GELU_C * (1.0 + 3.0 * 0.044715 * x * x)


def _rms_fwd(h, g):
    r = lax.rsqrt(jnp.mean(h * h, axis=-1, keepdims=True) + EPS)
    xh = h * r
    return xh * g, xh, r


def _rms_bwd(dn, xh, r, g):
    dxh = dn * g
    dh = r * (dxh - xh * jnp.mean(dxh * xh, axis=-1, keepdims=True))
    return dh, jnp.sum(dn * xh, axis=0, keepdims=True)


def _shift_down(x, d, fill=0.0):
    row = lax.broadcasted_iota(jnp.int32, x.shape, 0)
    return jnp.where(row >= d, pltpu.roll(x, d, 0), fill)


def _shift_up(x, d, fill=0.0):
    n = x.shape[0]
    row = lax.broadcasted_iota(jnp.int32, x.shape, 0)
    return jnp.where(row < n - d, pltpu.roll(x, n - d, 0), fill)


def _conv_fwd(x, w):
    y = x * w[3]
    for k in range(3):
        y = y + _shift_down(x, 3 - k) * w[k]
    return y


def _conv_bwd(dy, x, w):
    dx = dy * w[3]
    rows = [None] * 4
    rows[3] = jnp.sum(dy * x, axis=0, keepdims=True)
    for k in range(3):
        dx = dx + _shift_up(dy, 3 - k) * w[k]
        rows[k] = jnp.sum(dy * _shift_down(x, 3 - k), axis=0, keepdims=True)
    r4 = lax.broadcasted_iota(jnp.int32, (4, x.shape[1]), 0)
    dw = jnp.zeros((4, x.shape[1]), F32)
    for k in range(4):
        dw = jnp.where(r4 == k, rows[k], dw)
    return dx, dw


FFN_SPLIT = 2


def _interleave(gens):
    pending = list(gens)
    while pending:
        for g in list(pending):
            if next(g, StopIteration) is StopIteration:
                pending.remove(g)


def _params(sem=None, vmem=VMEM_LIMIT):
    return pltpu.CompilerParams(dimension_semantics=sem, vmem_limit_bytes=vmem)


def _whole(shape):
    nd = len(shape)
    return pl.BlockSpec(shape, lambda *_: (0,) * nd)


def matmul(a, b, *, name, ta=False, tb=False, residual=None, out_dtype=F32, tm=512, tn=512, tk=512):
    m, k = (a.shape[1], a.shape[0]) if ta else a.shape
    n = b.shape[0] if tb else b.shape[1]
    tm, tn, tk = min(tm, m), min(tn, n), min(tk, k)
    assert m % tm == 0 and n % tn == 0 and k % tk == 0, (m, n, k, tm, tn, tk)
    nk = k // tk

    def body(*refs):
        if residual is None:
            a_ref, b_ref, o_ref, acc = refs
        else:
            a_ref, b_ref, r_ref, o_ref, acc = refs
        kk = pl.program_id(2)

        @pl.when(kk == 0)
        def _():
            acc[...] = jnp.zeros_like(acc)

        acc[...] += _dot(a_ref[...], b_ref[...], 0 if ta else 1, 1 if tb else 0)

        @pl.when(kk == nk - 1)
        def _():
            out = acc[...]
            if residual is not None:
                out = out + r_ref[...]
            o_ref[...] = out.astype(out_dtype)

    a_spec = pl.BlockSpec((tk, tm), lambda i, j, kk: (kk, i)) if ta else pl.BlockSpec((tm, tk), lambda i, j, kk: (i, kk))
    b_spec = pl.BlockSpec((tn, tk), lambda i, j, kk: (j, kk)) if tb else pl.BlockSpec((tk, tn), lambda i, j, kk: (kk, j))
    o_spec = pl.BlockSpec((tm, tn), lambda i, j, kk: (i, j))
    in_specs, args = [a_spec, b_spec], [a, b]
    if residual is not None:
        in_specs.append(o_spec)
        args.append(residual)
    return pl.pallas_call(
        body, name=name, grid=(m // tm, n // tn, nk), in_specs=in_specs, out_specs=o_spec,
        out_shape=SDS((m, n), out_dtype), scratch_shapes=[pltpu.VMEM((tm, tn), F32)],
        compiler_params=_params(("parallel", "parallel", "arbitrary")))(*args)


def norm_matmul(h, gain, w, *, name, tm=512, tn=512):
    t, d = h.shape
    tm = min(tm, t)
    n = w.shape[1]
    assert t % tm == 0 and n % tn == 0

    def body(h_ref, g_ref, w_ref, u_ref, n_ref):
        @pl.when(pl.program_id(1) == 0)
        def _():
            n_ref[...] = _rms_fwd(h_ref[...], g_ref[...])[0].astype(BF16)

        u_ref[...] = _nn(n_ref[...], w_ref[...])

    return pl.pallas_call(
        body, name=name, grid=(t // tm, n // tn),
        in_specs=[pl.BlockSpec((tm, d), lambda i, j: (i, 0)), _whole((1, d)), pl.BlockSpec((d, tn), lambda i, j: (0, j))],
        out_specs=[pl.BlockSpec((tm, tn), lambda i, j: (i, j)), pl.BlockSpec((tm, d), lambda i, j: (i, 0))],
        out_shape=[SDS((t, n), F32), SDS((t, d), BF16)],
        compiler_params=_params(("parallel", "arbitrary")))(h, gain, w)


def ffn_fwd(h, gain, wg, wu, wd, *, name, tm=1024):
    t, d = h.shape
    tm = min(tm, t)

    def body(h_ref, g_ref, wg_ref, wu_ref, wd_ref, o_ref, n_ref, a_ref, b_ref, acc):
        j = pl.program_id(1)

        @pl.when(j == 0)
        def _():
            n_ref[...] = _rms_fwd(h_ref[...], g_ref[...])[0].astype(BF16)
            acc[...] = jnp.zeros_like(acc)

        def part(rows):
            n = n_ref[rows, :]
            a = _nt(n, wg_ref[...])
            b = _nt(n, wu_ref[...])
            yield
            a_ref[rows, :] = a.astype(BF16)
            b_ref[rows, :] = b.astype(BF16)
            acc[rows, :] += _nn(a * _sigmoid(a) * b, wd_ref[...])

        _interleave([part(pl.ds(k * (tm // FFN_SPLIT), tm // FFN_SPLIT)) for k in range(FFN_SPLIT)])

        @pl.when(j == N_CHIP - 1)
        def _():
            o_ref[...] = h_ref[...] + 0.5 * acc[...]

    row = pl.BlockSpec((tm, d), lambda i, j: (i, 0))
    blk = pl.BlockSpec((None, tm, FF_BLK), lambda i, j: (j, i, 0))
    wspec = pl.BlockSpec((None, FF_BLK, d), lambda i, j: (j, 0, 0))
    act = SDS((N_CHIP, t, FF_BLK), BF16)
    return pl.pallas_call(
        body, name=name, grid=(t // tm, N_CHIP), in_specs=[row, _whole((1, d)), wspec, wspec, wspec],
        out_specs=[row, row, blk, blk], out_shape=[SDS((t, d), F32), SDS((t, d), BF16), act, act],
        scratch_shapes=[pltpu.VMEM((tm, d), F32)],
        compiler_params=_params(("parallel", "arbitrary")))(h, gain, wg, wu, wd)


def ffn_bwd_act(h, gain, dout, a, b, wg, wu, wd, *, name, tm=512):
    t, d = h.shape
    tm = min(tm, t)

    def body(h_ref, g_ref, do_ref, a_ref, b_ref, wg_ref, wu_ref, wd_ref, dh_ref, da_ref, db_ref, s_ref, dg_ref, dn_acc):
        i, j = pl.program_id(0), pl.program_id(1)

        @pl.when((i == 0) & (j == 0))
        def _():
            dg_ref[...] = jnp.zeros_like(dg_ref)

        @pl.when(j == 0)
        def _():
            dn_acc[...] = jnp.zeros_like(dn_acc)

        def part(rows):
            ds = _nt(0.5 * do_ref[rows, :], wd_ref[...])
            yield
            a = a_ref[rows, :].astype(F32)
            b = b_ref[rows, :].astype(F32)
            sig = _sigmoid(a)
            sa = a * sig
            db = ds * sa
            da = ds * b * (sig * (1.0 + a * (1.0 - sig)))
            s_ref[rows, :] = (sa * b).astype(BF16)
            da_ref[rows, :] = da.astype(BF16)
            db_ref[rows, :] = db.astype(BF16)
            yield
            dn_acc[rows, :] += _nn(da, wg_ref[...]) + _nn(db, wu_ref[...])

        _interleave([part(pl.ds(k * (tm // FFN_SPLIT), tm // FFN_SPLIT)) for k in range(FFN_SPLIT)])

        @pl.when(j == N_CHIP - 1)
        def _():
            g = g_ref[...]
            _, xh, r = _rms_fwd(h_ref[...], g)
            dh, dg = _rms_bwd(dn_acc[...], xh, r, g)
            dh_ref[...] = do_ref[...] + dh
            dg_ref[...] += dg

    row = pl.BlockSpec((tm, d), lambda i, j: (i, 0))
    blk = pl.BlockSpec((None, tm, FF_BLK), lambda i, j: (j, i, 0))
    wspec = pl.BlockSpec((None, FF_BLK, d), lambda i, j: (j, 0, 0))
    act = SDS((N_CHIP, t, FF_BLK), BF16)
    return pl.pallas_call(
        body, name=name, grid=(t // tm, N_CHIP), in_specs=[row, _whole((1, d)), row, blk, blk, wspec, wspec, wspec],
        out_specs=[row, blk, blk, blk, _whole((1, d))],
        out_shape=[SDS((t, d), F32), act, act, act, SDS((1, d), F32)],
        scratch_shapes=[pltpu.VMEM((tm, d), F32)],
        compiler_params=_params(("arbitrary", "arbitrary")))(h, gain, dout, a, b, wg, wu, wd)


def ffn_bwd_w(n, da, db, s, dout, *, name, tk=1024):
    t, d = n.shape
    tk = min(tk, t)

    def body(n_ref, da_ref, db_ref, s_ref, do_ref, dwg_ref, dwu_ref, dwd_ref):
        @pl.when(pl.program_id(1) == 0)
        def _():
            dwg_ref[...] = jnp.zeros_like(dwg_ref)
            dwu_ref[...] = jnp.zeros_like(dwu_ref)
            dwd_ref[...] = jnp.zeros_like(dwd_ref)

        nn = n_ref[...]
        dwg_ref[...] += _tn(da_ref[...], nn)
        dwu_ref[...] += _tn(db_ref[...], nn)
        dwd_ref[...] += _tn(s_ref[...], 0.5 * do_ref[...])

    row = pl.BlockSpec((tk, d), lambda j, kk: (kk, 0))
    blk = pl.BlockSpec((None, tk, FF_BLK), lambda j, kk: (j, kk, 0))
    return pl.pallas_call(
        body, name=name, grid=(N_CHIP, t // tk), in_specs=[row, blk, blk, blk, row],
        out_specs=[pl.BlockSpec((None, FF_BLK, d), lambda j, kk: (j, 0, 0)),
                   pl.BlockSpec((None, FF_BLK, d), lambda j, kk: (j, 0, 0)),
                   pl.BlockSpec((None, FF_BLK, d), lambda j, kk: (j, 0, 0))],
        out_shape=[SDS((N_CHIP, FF_BLK, d), F32)] * 3,
        compiler_params=_params(("parallel", "arbitrary")))(n, da, db, s, dout)


def ple_fwd(h, gain, wpg, pl_in, wpp, *, name, tm=512):
    t, d = h.shape
    tm = min(tm, t)
    pd = pl_in.shape[1]

    def body(h_ref, g_ref, wpg_ref, p_ref, wpp_ref, o_ref):
        hh = h_ref[...]
        n = _rms_fwd(hh, g_ref[...])[0]
        gate = _sigmoid(_nn(n, wpg_ref[...]))
        o_ref[...] = hh + gate * _nn(p_ref[...], wpp_ref[...])

    row = pl.BlockSpec((tm, d), lambda i: (i, 0))
    return pl.pallas_call(
        body, name=name, grid=(t // tm,),
        in_specs=[row, _whole((1, d)), _whole((d, d)), pl.BlockSpec((tm, pd), lambda i: (i, 0)), _whole((pd, d))],
        out_specs=row, out_shape=SDS((t, d), F32), compiler_params=_params(("parallel",)))(h, gain, wpg, pl_in, wpp)


def ple_bwd(h, gain, wpg, pl_in, wpp, dout, *, name, tm=512):
    t, d = h.shape
    tm = min(tm, t)
    pd = pl_in.shape[1]

    def body(h_ref, g_ref, wpg_ref, p_ref, wpp_ref, do_ref, dh_ref, n_ref, dga_ref, dpp_ref, dg_ref):
        @pl.when(pl.program_id(0) == 0)
        def _():
            dg_ref[...] = jnp.zeros_like(dg_ref)

        g = g_ref[...]
        n, xh, r = _rms_fwd(h_ref[...], g)
        gate = _sigmoid(_nn(n, wpg_ref[...]))
        pp = _nn(p_ref[...], wpp_ref[...])
        do = do_ref[...]
        dga = do * pp * gate * (1.0 - gate)
        dh, dg = _rms_bwd(_nt(dga, wpg_ref[...]), xh, r, g)
        dh_ref[...] = do + dh
        n_ref[...] = n.astype(BF16)
        dga_ref[...] = dga.astype(BF16)
        dpp_ref[...] = (do * gate).astype(BF16)
        dg_ref[...] += dg

    row = pl.BlockSpec((tm, d), lambda i: (i, 0))
    return pl.pallas_call(
        body, name=name, grid=(t // tm,),
        in_specs=[row, _whole((1, d)), _whole((d, d)), pl.BlockSpec((tm, pd), lambda i: (i, 0)), _whole((pd, d)), row],
        out_specs=[row, row, row, row, _whole((1, d))],
        out_shape=[SDS((t, d), F32), SDS((t, d), BF16), SDS((t, d), BF16), SDS((t, d), BF16), SDS((1, d), F32)],
        compiler_params=_params(("arbitrary",)))(h, gain, wpg, pl_in, wpp, dout)


def loss_head(h, gain, target, *, name, tm=512):
    t, d = h.shape
    tm = min(tm, t)

    def body(h_ref, g_ref, t_ref, dh_ref, dg_ref, l_ref):
        @pl.when(pl.program_id(0) == 0)
        def _():
            dg_ref[...] = jnp.zeros_like(dg_ref)
            l_ref[...] = jnp.zeros_like(l_ref)

        g = g_ref[...]
        y, xh, r = _rms_fwd(h_ref[...], g)
        err = y - t_ref[...]
        l_ref[...] += 0.5 * jnp.sum(jnp.mean(err * err, axis=-1, keepdims=True), axis=0, keepdims=True)
        dh, dg = _rms_bwd(err * (1.0 / d), xh, r, g)
        dh_ref[...] = dh
        dg_ref[...] += dg

    row = pl.BlockSpec((tm, d), lambda i: (i, 0))
    return pl.pallas_call(
        body, name=name, grid=(t // tm,), in_specs=[row, _whole((1, d)), row],
        out_specs=[row, _whole((1, d)), _whole((1, 1))],
        out_shape=[SDS((t, d), F32), SDS((1, d), F32), SDS((1, 1), F32)],
        compiler_params=_params(("arbitrary",)))(h, gain, target)


def adamw(ws, gs, ms, vs, *, name):
    n = len(ws)
    r, c = ws[0].shape
    budget = (24 << 20) // (2 * 7 * n * c * 4)
    tr = next((cand for cand in (704, 512, 352, 256, 176, 128, 64, 32, 16, 8) if r % cand == 0 and cand <= budget), r)

    def body(*refs):
        for w_ref, g_ref, m_ref, v_ref, d_ref, nm_ref, nv_ref in zip(*[refs[i * n:(i + 1) * n] for i in range(7)]):
            gg = g_ref[...]
            mm = ADAM_B1 * m_ref[...] + (1.0 - ADAM_B1) * gg
            vv = ADAM_B2 * v_ref[...] + (1.0 - ADAM_B2) * (gg * gg)
            m_hat = mm / (1.0 - ADAM_B1 ** ADAM_STEP)
            v_hat = vv / (1.0 - ADAM_B2 ** ADAM_STEP)
            d_ref[...] = -ADAM_LR * (m_hat / (jnp.sqrt(v_hat) + ADAM_EPS) + ADAM_WD * w_ref[...])
            nm_ref[...] = mm
            nv_ref[...] = vv

    blk = pl.BlockSpec((tr, c), lambda i: (i, 0))
    out = pl.pallas_call(body, name=name, grid=(r // tr,), in_specs=[blk] * (4 * n), out_specs=[blk] * (3 * n),
                         out_shape=[SDS((r, c), F32)] * (3 * n), compiler_params=_params(("parallel",)))(*ws, *gs, *ms, *vs)
    return list(out[:n]), list(out[n:2 * n]), list(out[2 * n:])


def _scan_fwd(a, b):
    d = 1
    while d < a.shape[0]:
        b = a * _shift_down(b, d, 0.0) + b
        a = a * _shift_down(a, d, 1.0)
        d *= 2
    return b


def _scan_rev(a, b):
    d = 1
    while d < a.shape[0]:
        b = a * _shift_up(b, d, 0.0) + b
        a = a * _shift_up(a, d, 1.0)
        d *= 2
    return b


LRU_HALF = 128


def _lru_in_specs(seq):
    half = LRU_W // LRU_HALF
    vec = pl.BlockSpec((1, LRU_HALF), lambda j, b: (0, j))
    mat = pl.BlockSpec((LRU_HALF, LRU_HALF), lambda j, b: (j, j))
    return [pl.BlockSpec((seq, LRU_HALF), lambda j, b: (b, j)), pl.BlockSpec((seq, LRU_HALF), lambda j, b: (b, half + j)),
            pl.BlockSpec((4, LRU_HALF), lambda j, b: (0, j)), vec, mat, vec, mat, vec, vec]


def _lru_math(x_ref, gate_ref, cw_ref, cb_ref, wa_ref, ba_ref, wx_ref, bx_ref, lam_ref):
    x = x_ref[...]
    gate = gate_ref[...]
    cw =[cw_ref[k:k + 1, :] for k in range(4)]
    xr = _conv_fwd(x, cw) + cb_ref[...]
    r = _sigmoid(_nn(xr, wa_ref[...]) + ba_ref[...])
    i = _sigmoid(_nn(xr, wx_ref[...]) + bx_ref[...])
    sp = _softplus(-lam_ref[...])
    log_a = -LRU_C * r * sp
    a = jnp.exp(log_a)
    mult = jnp.sqrt(_neg_expm1(2.0 * log_a))
    gi = i * xr
    h = _scan_fwd(a, mult * gi)
    gl, tg = _gelu(gate)
    return dict(x=x, gate=gate, cw=cw, xr=xr, r=r, i=i, sp=sp, a=a, mult=mult, gi=gi, h=h, gl=gl, tg=tg)


def lru_fwd(u, cw, cb, wa, ba, wx, bx, lam, *, seq, name):
    t = u.shape[0]

    def body(x_ref, gate_ref, cw_ref, cb_ref, wa_ref, ba_ref, wx_ref, bx_ref, lam_ref, y_ref):
        f = _lru_math(x_ref, gate_ref, cw_ref, cb_ref, wa_ref, ba_ref, wx_ref, bx_ref, lam_ref)
        y_ref[...] = f["gl"] * f["h"]

    return pl.pallas_call(
        body, name=name, grid=(LRU_W // LRU_HALF, t // seq), in_specs=_lru_in_specs(seq),
        out_specs=pl.BlockSpec((seq, LRU_HALF), lambda j, b: (b, j)), out_shape=SDS((t, LRU_W), F32),
        compiler_params=_params(("parallel", "parallel")))(u, u, cw, cb, wa, ba, wx, bx, lam)


def lru_bwd(u, cw, cb, wa, ba, wx, bx, lam, dy, *, seq, name):
    t = u.shape[0]

    def body(x_ref, gate_ref, cw_ref, cb_ref, wa_ref, ba_ref, wx_ref, bx_ref, lam_ref, dy_ref,
             dx_ref, dgate_ref, dcw_ref, dwa_ref, dwx_ref, dv_ref):
        @pl.when(pl.program_id(1) == 0)
        def _():
            dcw_ref[...] = jnp.zeros_like(dcw_ref)
            dwa_ref[...] = jnp.zeros_like(dwa_ref)
            dwx_ref[...] = jnp.zeros_like(dwx_ref)
            dv_ref[...] = jnp.zeros_like(dv_ref)

        f = _lru_math(x_ref, gate_ref, cw_ref, cb_ref, wa_ref, ba_ref, wx_ref, bx_ref, lam_ref)
        dy = dy_ref[...]
        a, h, xr, r, i, mult, gi, sp = f["a"], f["h"], f["xr"], f["r"], f["i"], f["mult"], f["gi"], f["sp"]
        dgate_ref[...] = dy * h * _gelu_grad(f["gate"], f["tg"])
        lamb = _scan_rev(_shift_up(a, 1, 0.0), dy * f["gl"])
        da = lamb * _shift_down(h, 1)
        dlog_a = da * a - (lamb * gi) * (a * a) / mult
        dgi = lamb * mult
        dra = dlog_a * (-LRU_C * sp) * r * (1.0 - r)
        dia = dgi * xr * i * (1.0 - i)
        dsp = jnp.sum(dlog_a * (-LRU_C * r), axis=0, keepdims=True)
        dlam = -dsp * _sigmoid(-lam_ref[...])
        dxr = dgi * i + _nt(dra, wa_ref[...]) + _nt(dia, wx_ref[...])
        dx, dcw = _conv_bwd(dxr, f["x"], f["cw"])
        dx_ref[...] = dx
        dcw_ref[...] += dcw
        dwa_ref[...] += _tn(xr, dra)
        dwx_ref[...] += _tn(xr, dia)
        rows = [jnp.sum(dxr, axis=0, keepdims=True), jnp.sum(dra, axis=0, keepdims=True),
                jnp.sum(dia, axis=0, keepdims=True), dlam]
        r8 = lax.broadcasted_iota(jnp.int32, (8, LRU_HALF), 0)
        acc = jnp.zeros((8, LRU_HALF), F32)
        for k, row in enumerate(rows):
            acc = jnp.where(r8 == k, row, acc)
        dv_ref[...] += acc

    nhalf = LRU_W // LRU_HALF
    col = pl.BlockSpec((seq, LRU_HALF), lambda j, b: (b, j))
    mat = pl.BlockSpec((None, LRU_HALF, LRU_HALF), lambda j, b: (j, 0, 0))
    return pl.pallas_call(
        body, name=name, grid=(nhalf, t // seq), in_specs=_lru_in_specs(seq) + [col],
        out_specs=[col, col, pl.BlockSpec((4, LRU_HALF), lambda j, b: (0, j)), mat, mat,
                   pl.BlockSpec((8, LRU_HALF), lambda j, b: (0, j))],
        out_shape=[SDS((t, LRU_W), F32), SDS((t, LRU_W), F32), SDS((4, LRU_W), F32),
                   SDS((nhalf, LRU_HALF, LRU_HALF), F32), SDS((nhalf, LRU_HALF, LRU_HALF), F32), SDS((8, LRU_W), F32)],
        compiler_params=_params(("arbitrary", "arbitrary")))(u, u, cw, cb, wa, ba, wx, bx, lam, dy)


NEG = -1e30


def _rel_bucket_map():
    dist = (np.arange(BLOCK_Q)[:, None] - np.arange(BLOCK_Q)[None, :]) % BLOCK_Q
    max_exact = REL_BUCKETS // 2
    large = max_exact + (np.log(np.maximum(dist, 1).astype(np.float32) / max_exact)
                         / math.log(BLOCK_Q / max_exact) * (REL_BUCKETS - max_exact)).astype(np.int32)
    large = np.minimum(large, REL_BUCKETS - 1)
    return np.where(dist < max_exact, dist, large).astype(np.int32)


def relbias_fwd(rel_bias, bmap, *, name):
    def body(rb_ref, bm_ref, o_ref):
        bm = bm_ref[...]
        for h in range(ATT_HEADS):
            acc = jnp.zeros((BLOCK_Q, BLOCK_Q), F32)
            for b in range(REL_BUCKETS):
                acc = jnp.where(bm == b, rb_ref[b, h], acc)
            o_ref[h] = acc

    return pl.pallas_call(
        body, name=name, in_specs=[pl.BlockSpec(memory_space=pltpu.SMEM), pl.BlockSpec(memory_space=pltpu.VMEM)],
        out_specs=pl.BlockSpec(memory_space=pltpu.VMEM), out_shape=SDS((ATT_HEADS, BLOCK_Q, BLOCK_Q), F32))(rel_bias, bmap)


def relbias_bwd(dbias, bmap, *, name):
    def body(db_ref, bm_ref, o_ref):
        bm = bm_ref[...]
        row = lax.broadcasted_iota(jnp.int32, (REL_BUCKETS, 128), 0)
        col = lax.broadcasted_iota(jnp.int32, (REL_BUCKETS, 128), 1)
        acc = jnp.zeros((REL_BUCKETS, 128), F32)
        for h in range(ATT_HEADS):
            d = db_ref[h]
            for b in range(REL_BUCKETS):
                s = jnp.sum(jnp.sum(jnp.where(bm == b, d, 0.0), axis=1, keepdims=True), axis=0, keepdims=True)
                acc = jnp.where((row == b) & (col == h), s, acc)
        o_ref[...] = acc

    return pl.pallas_call(body, name=name, out_shape=SDS((REL_BUCKETS, 128), F32))(dbias, bmap)


def _iota2(shape, axis):
    return lax.broadcasted_iota(jnp.int32, shape, axis)


def _chunk_cumsum(x):
    pos = _iota2(x.shape, 0) & (DN_CHUNK - 1)
    d = 1
    while d < DN_CHUNK:
        x = x + jnp.where(pos >= d, pltpu.roll(x, d, 0), 0.0)
        d *= 2
    return x


def _chunk_rev_cumsum(x):
    n = x.shape[0]
    pos = _iota2(x.shape, 0) & (DN_CHUNK - 1)
    d = 1
    while d < DN_CHUNK:
        x = x + jnp.where(pos < DN_CHUNK - d, pltpu.roll(x, n - d, 0), 0.0)
        d *= 2
    return x


_DN_SCALE = (HEAD ** -0.5, 1.0, None)
DN_UNROLL = 4


COL_Q, COL_K, COL_V = 512 // 128, 1024 // 128, 1152 // 128
COL_DNQ, COL_DNK, COL_DNV, COL_DNZ, COL_BA = 1280 // 128, 1536 // 128, 1792 // 128, 2048 // 128, 2304 // 128


def _lane_a(shape):
    return _iota2(shape, 1) < HEAD


def _bd(x):
    la = _lane_a(x.shape)
    return jnp.concatenate([jnp.where(la, x, 0.0), jnp.where(la, 0.0, x)], axis=0)


def _fold(m):
    return m[:HEAD] + m[HEAD:]


def _bd_mask():
    return (_iota2((2 * HEAD, 2 * HEAD), 0) < HEAD) == (_iota2((2 * HEAD, 2 * HEAD), 1) < HEAD)


def _pk_nn(x, y, hi=False):
    return _nn(x, _bd(y), hi)


def _pk_nt(u, v, hi=False):
    return _nt(u, _bd(v), hi)


def _pk_tn(x, y, hi=False):
    return _fold(jnp.where(_bd_mask(), _tn(x, y, hi), 0.0))


def _half_sum(x):
    la = _lane_a(x.shape)
    return jnp.where(la, jnp.sum(jnp.where(la, x, 0.0), axis=-1, keepdims=True),
                     jnp.sum(jnp.where(la, 0.0, x), axis=-1, keepdims=True))


def _lane_col(x, idx):
    return jnp.sum(jnp.where(_iota2(x.shape, 1) == idx, x, 0.0), axis=-1, keepdims=True)


def _row0(x):
    return jnp.max(x, axis=0, keepdims=True)


def _dup_kv(x, g):
    la = _lane_a(x.shape)
    rolled = pltpu.roll(x, HEAD, 1)
    return jnp.where(la, x, rolled) if g == 0 else jnp.where(la, rolled, x)


def _stack_heads(ref, g):
    la = _lane_a((BLOCK_Q, 2 * HEAD))
    parts = []
    for hh in range(ATT_GROUP):
        pair = ref[:, pl.ds(2 * HEAD * (2 * g + hh // 2), 2 * HEAD)]
        parts.append(jnp.where(la if hh % 2 == 0 else ~la, pair, 0.0))
    return jnp.concatenate(parts, axis=0)


def _unstack_heads(stack, ref, g):
    la = _lane_a((BLOCK_Q, 2 * HEAD))
    for j in range(2):
        top = stack[2 * j * BLOCK_Q:(2 * j + 1) * BLOCK_Q]
        bot = stack[(2 * j + 1) * BLOCK_Q:(2 * j + 2) * BLOCK_Q]
        ref[:, pl.ds(2 * HEAD * (2 * g + j), 2 * HEAD)] = jnp.where(la, top, bot)


def _swa_probs(q_ref, k_ref, v_ref, b_ref, s_ref, n, g):
    rows = ATT_GROUP * BLOCK_Q
    prev = pl.multiple_of(jnp.maximum(n - 1, 0) * BLOCK_Q, BLOCK_Q)
    cur = pl.multiple_of(n * BLOCK_Q, BLOCK_Q)
    kp, kc = _dup_kv(k_ref[pl.ds(prev, BLOCK_Q), :], g), _dup_kv(k_ref[pl.ds(cur, BLOCK_Q), :], g)
    vp, vc = _dup_kv(v_ref[pl.ds(prev, BLOCK_Q), :], g), _dup_kv(v_ref[pl.ds(cur, BLOCK_Q), :], g)
    qs = _stack_heads(q_ref, g) * (HEAD ** -0.5)
    bias = b_ref[pl.ds(ATT_GROUP * g, ATT_GROUP)].reshape(rows, BLOCK_Q)
    i = _iota2((rows, BLOCK_Q), 0) & (BLOCK_Q - 1)
    j = _iota2((rows, BLOCK_Q), 1)
    s_p = jnp.where((j > i) & (n > 0), _nt(qs, kp) + bias, NEG)
    s_c = jnp.where(j <= i, _nt(qs, kc) + bias, NEG)
    sink = s_ref[pl.ds(rows * g, rows), :]
    m = jnp.maximum(jnp.maximum(jnp.max(s_p, axis=-1, keepdims=True), jnp.max(s_c, axis=-1, keepdims=True)), sink)
    e_p, e_c, e_s = jnp.exp(s_p - m), jnp.exp(s_c - m), jnp.exp(sink - m)
    inv = 1.0 / (jnp.sum(e_p, axis=-1, keepdims=True) + jnp.sum(e_c, axis=-1, keepdims=True) + e_s)
    return e_p * inv, e_c * inv, e_s * inv, qs, kp, kc, vp, vc, prev, cur


def _swa_specs(seq):
    nblk = seq // BLOCK_Q
    qspec = pl.BlockSpec((BLOCK_Q, ATT_W), lambda b, n: (b * nblk + n, COL_Q * 128 // ATT_W))
    kspec = pl.BlockSpec((seq, 2 * HEAD), lambda b, n: (b, COL_K))
    vspec = pl.BlockSpec((seq, 2 * HEAD), lambda b, n: (b, COL_V))
    ospec = pl.BlockSpec((BLOCK_Q, ATT_W), lambda b, n: (b * nblk + n, 0))
    kvout = pl.BlockSpec((seq, 2 * HEAD), lambda b, n: (b, 0))
    return qspec, kspec, vspec, ospec, kvout, _whole((ATT_HEADS, BLOCK_Q, BLOCK_Q)), _whole((ATT_HEADS * BLOCK_Q, 1))


def swa_fwd(u, bias, sink_rows, *, seq, name):
    t = u.shape[0]

    def body(q_ref, k_ref, v_ref, b_ref, s_ref, o_ref):
        for g in range(KV_HEADS):
            p_p, p_c, _, _, _, _, vp, vc, _, _ = _swa_probs(q_ref, k_ref, v_ref, b_ref, s_ref, pl.program_id(1), g)
            _unstack_heads(_nn(p_p, vp) + _nn(p_c, vc), o_ref, g)

    qspec, kspec, vspec, ospec, kvout, bspec, sspec = _swa_specs(seq)
    return pl.pallas_call(
        body, name=name, grid=(t // seq, seq // BLOCK_Q), in_specs=[qspec, kspec, vspec, bspec, sspec], out_specs=ospec,
        out_shape=SDS((t, ATT_W), F32), compiler_params=_params(("parallel", "arbitrary")))(u, u, u, bias, sink_rows)


def swa_bwd(u, bias, sink_rows, do, *, seq, name):
    t = u.shape[0]

    def body(q_ref, k_ref, v_ref, b_ref, s_ref, do_ref, dq_ref, dk_ref, dv_ref, db_ref, ds_ref):
        b, n = pl.program_id(0), pl.program_id(1)

        @pl.when((b == 0) & (n == 0))
        def _():
            db_ref[...] = jnp.zeros_like(db_ref)
            ds_ref[...] = jnp.zeros_like(ds_ref)

        @pl.when(n == 0)
        def _():
            dk_ref[...] = jnp.zeros_like(dk_ref)
            dv_ref[...] = jnp.zeros_like(dv_ref)

        la = _lane_a((BLOCK_Q, 2 * HEAD))
        for g in range(KV_HEADS):
            p_p, p_c, p_s, qs, kp, kc, vp, vc, prev, cur = _swa_probs(q_ref, k_ref, v_ref, b_ref, s_ref, n, g)
            do = _stack_heads(do_ref, g)
            dp_p, dp_c = _nt(do, vp), _nt(do, vc)
            delta = jnp.sum(p_p * dp_p, axis=-1, keepdims=True) + jnp.sum(p_c * dp_c, axis=-1, keepdims=True)
            ds_p, ds_c = p_p * (dp_p - delta), p_c * (dp_c - delta)
            _unstack_heads((_nn(ds_p, kp) + _nn(ds_c, kc)) * (HEAD ** -0.5), dq_ref, g)
            mine = la if g == 0 else ~la

            def to_head(x):
                return jnp.where(mine, x + pltpu.roll(x, HEAD, 1), 0.0)

            dk_ref[pl.ds(prev, BLOCK_Q), :] += to_head(_tn(ds_p, qs))
            dk_ref[pl.ds(cur, BLOCK_Q), :] += to_head(_tn(ds_c, qs))
            dv_ref[pl.ds(prev, BLOCK_Q), :] += to_head(_tn(p_p, do))
            dv_ref[pl.ds(cur, BLOCK_Q), :] += to_head(_tn(p_c, do))
            db_ref[pl.ds(ATT_GROUP * g, ATT_GROUP)] += (ds_p + ds_c).reshape(ATT_GROUP, BLOCK_Q, BLOCK_Q)
            rows = ATT_GROUP * BLOCK_Q
            ds_ref[pl.ds(rows * g, rows), :] += -p_s * delta

    qspec, kspec, vspec, ospec, kvout, bspec, sspec = _swa_specs(seq)
    return pl.pallas_call(
        body, name=name, grid=(t // seq, seq // BLOCK_Q), in_specs=[qspec, kspec, vspec, bspec, sspec, ospec],
        out_specs=[ospec, kvout, kvout, bspec, sspec],
        out_shape=[SDS((t, ATT_W), F32), SDS((t, 2 * HEAD), F32), SDS((t, 2 * HEAD), F32),
                   SDS((ATT_HEADS, BLOCK_Q, BLOCK_Q), F32), SDS((ATT_HEADS * BLOCK_Q, 1), F32)],
        compiler_params=_params(("arbitrary", "arbitrary")))(u, u, u, bias, sink_rows, do)


def _gdn_gates(ba_ref, alog_ref, dt_ref, hp):
    blk = ba_ref[...]
    beta_blk = _sigmoid(blk)
    sp_arg = blk + dt_ref[...]
    a_exp = jnp.exp(alog_ref[...])
    g_blk = -a_exp * _softplus(sp_arg)
    la = _lane_a(blk.shape)
    ha = 2 * hp
    beta = jnp.where(la, _lane_col(beta_blk, ha), _lane_col(beta_blk, ha + 1))
    g = jnp.where(la, _lane_col(g_blk, DN_HEADS + ha), _lane_col(g_blk, DN_HEADS + ha + 1))
    return beta, g, beta_blk, sp_arg, a_exp, g_blk


def _gdn_act(c, scale):
    sig = _sigmoid(c)
    a = c * sig
    if scale is None:
        return a, sig, None, None
    r = lax.rsqrt(_half_sum(a * a) + EPS)
    return a * r * scale, sig, a * r, r


def _gdn_inputs(pre_refs, cw_refs, ba_ref, alog_ref, dt_ref, hp, act_sc, b_sc, gc_sc, c_sc=None):
    for idx in range(3):
        c = _conv_fwd(pre_refs[idx][...], [cw_refs[idx][k:k + 1, :] for k in range(4)])
        if c_sc is not None:
            c_sc[idx] = c
        act_sc[idx] = _gdn_act(c, _DN_SCALE[idx])[0]
    beta, g = _gdn_gates(ba_ref, alog_ref, dt_ref, hp)[:2]
    b_sc[...] = beta
    gc_sc[...] = _chunk_cumsum(g)


def _gdn_chunk(q, k, v, b, gcc):
    shape = q.shape
    row, lm = _iota2(shape, 0), _iota2(shape, 1) & (HEAD - 1)
    tril, strict, eye = row >= lm, row > lm, row == lm
    eg = jnp.exp(gcc)
    kb, vb = k * b, v * b
    kbg = kb * eg
    grow = jnp.sum(jnp.where(eye, gcc, 0.0), axis=0, keepdims=True)
    dm = jnp.exp(jnp.where(tril, gcc - grow, NEG))
    kk = _pk_nt(kb, k)
    glast = jnp.sum(jnp.where(row == DN_CHUNK - 1, gcc, 0.0), axis=0, keepdims=True)
    ekd = jnp.exp(glast - gcc)
    qk = _pk_nt(q, k)
    return dict(q=q, k=k, v=v, b=b, tril=tril, strict=strict, eye=eye, row=row, eg=eg, kb=kb, vb=vb, kbg=kbg, dm=dm, kk=kk,
                low=jnp.where(strict, kk * dm, 0.0), glast=glast, ekd=ekd, kd=k * ekd, qk=qk,
                amat=jnp.where(tril, qk * dm, 0.0), qg=q * eg, egl=jnp.broadcast_to(jnp.exp(glast), shape))


def _tri_inv_many(chunks):
    ms = [-m["low"] for m in chunks]
    ts = [m["eye"].astype(F32) + x for m, x in zip(chunks, ms)]
    for _ in range(int(math.log2(HEAD)) - 1):
        ms = [_pk_nn(x, x, hi=True) for x in ms]
        ts = [t + _pk_nn(t, x, hi=True) for t, x in zip(ts, ms)]
    return ts


def _gdn_chunk_loop(nc, act_sc, b_sc, gc_sc, finish):
    u = math.gcd(nc, DN_UNROLL)

    def step(i, carry):
        rows = [pl.ds(pl.multiple_of((i * u + j) * DN_CHUNK, DN_CHUNK), DN_CHUNK) for j in range(u)]
        chunks = [_gdn_chunk(act_sc[0, r, :], act_sc[1, r, :], act_sc[2, r, :], b_sc[r, :], gc_sc[r, :]) for r in rows]
        pending = [finish(r, m, t) for r, m, t in zip(rows, chunks, _tri_inv_many(chunks))]
        pending = [g for g in pending if g is not None]
        while pending:
            for g in list(pending):
                if next(g, StopIteration) is StopIteration:
                    pending.remove(g)
        return carry

    lax.fori_loop(0, nc // u, step, 0)


def _gdn_in_specs(seq):
    u_at = lambda col: pl.BlockSpec((seq, 2 * HEAD), lambda b, hp, _c=col: (b, _c + hp))
    cw_at = lambda col: pl.BlockSpec((4, 2 * HEAD), lambda b, hp, _c=col: (0, _c + hp))
    row = pl.BlockSpec((1, 2 * HEAD), lambda b, hp: (0, 0))
    ba = pl.BlockSpec((seq, 2 * HEAD), lambda b, hp: (b, COL_BA))
    return [u_at(COL_DNQ), u_at(COL_DNK), u_at(COL_DNV), ba, cw_at(0), cw_at(2), cw_at(4), row, row]


def _pair(seq, lead=None):
    if lead is None:
        return pl.BlockSpec((seq, 2 * HEAD), lambda b, hp: (b, hp))
    return pl.BlockSpec((lead, seq, 2 * HEAD), lambda b, hp: (0, b, hp))


def _swap(spec):
    return pl.BlockSpec(spec.block_shape, lambda hp, b, _f=spec.index_map: _f(b, hp))


def gdn_prep(u, cw, alog_row, dt_row, *, seq, name):
    t = u.shape[0]
    nc = seq // DN_CHUNK

    def body(q_ref, k_ref, v_ref, ba_ref, cq_ref, ck_ref, cv_ref, alog_ref, dt_ref, loc_ref, egl_ref, act_sc, b_sc, gc_sc):
        _gdn_inputs((q_ref, k_ref, v_ref), (cq_ref, ck_ref, cv_ref), ba_ref, alog_ref, dt_ref, pl.program_id(1),
                    act_sc, b_sc, gc_sc)

        def finish(rows, m, t):
            loc_ref[0, rows, :] = m["qg"]
            loc_ref[1, rows, :] = m["kd"]
            loc_ref[2, rows, :] = _pk_nn(t, m["vb"])
            loc_ref[3, rows, :] = _pk_nn(t, m["kbg"])
            loc_ref[4, rows, :] = m["amat"]
            egl_ref[rows, :] = m["egl"]

        _gdn_chunk_loop(nc, act_sc, b_sc, gc_sc, finish)

    return pl.pallas_call(
        body, name=name, grid=(t // seq, DN_HEADS // 2), in_specs=_gdn_in_specs(seq), out_specs=[_pair(seq, 5), _pair(seq)],
        out_shape=[SDS((5, t, DN_HEADS * HEAD), F32), SDS((t, DN_HEADS * HEAD), F32)],
        scratch_shapes=[pltpu.VMEM((3, seq, 2 * HEAD), F32)] + [pltpu.VMEM((seq, 2 * HEAD), F32)] * 2,
        compiler_params=_params(("parallel", "parallel")))(u, u, u, u, cw, cw, cw, alog_row, dt_row)


def _gated_norm2(o, z, gn):
    r = lax.rsqrt(_half_sum(o * o) * (1.0 / HEAD) + EPS)
    return o * r, _sigmoid(z), r


def gdn_scan(loc, egl, u, gn, *, seq, name):
    t = u.shape[0]
    nc = seq // DN_CHUNK

    npair = DN_HEADS // 2

    def body(loc_ref, egl_ref, z_ref, gn_ref, y_ref, o_ref, vn_ref, st_ref):
        gn = gn_ref[...]
        bdm = _bd_mask()

        def step(c, states):
            rows = pl.ds(pl.multiple_of(c * DN_CHUNK, DN_CHUNK), DN_CHUNK)
            new = [None] * npair

            def pair(hp):
                lanes = pl.ds(hp * 2 * HEAD, 2 * HEAD)
                state = states[hp]
                st_ref[rows, lanes] = _fold(state)
                vn = loc_ref[2, rows, lanes] - _nn(loc_ref[3, rows, lanes], state)
                yield
                o = _nn(loc_ref[0, rows, lanes], state) + _pk_nn(loc_ref[4, rows, lanes], vn)
                new[hp] = state * _row0(egl_ref[rows, lanes]) + jnp.where(bdm, _tn(loc_ref[1, rows, lanes], vn), 0.0)
                yield
                vn_ref[rows, lanes] = vn
                o_ref[rows, lanes] = o
                zz = z_ref[rows, lanes]
                on, sig, _ = _gated_norm2(o, zz, gn)
                y_ref[rows, lanes] = on * gn * (zz * sig)

            _interleave([pair(hp) for hp in range(npair)])
            return tuple(new)

        lax.fori_loop(0, nc, step, tuple(jnp.zeros((2 * HEAD, 2 * HEAD), F32) for _ in range(npair)))

    width = DN_HEADS * HEAD
    rows = pl.BlockSpec((seq, width), lambda b: (b, 0))
    out = SDS((t, width), F32)
    return pl.pallas_call(
        body, name=name, grid=(t // seq,),
        in_specs=[pl.BlockSpec((5, seq, width), lambda b: (0, b, 0)), rows,
                  pl.BlockSpec((seq, width), lambda b: (b, COL_DNZ * 2 * HEAD // width)), _whole((1, 2 * HEAD))],
        out_specs=[rows] * 4, out_shape=[out] * 4, compiler_params=_params(("parallel",)))(loc, egl, u, gn)


def gdn_scan_bwd(loc, egl, u, gn, o, vn, states, dy, *, seq, name):
    t = u.shape[0]
    nc = seq // DN_CHUNK

    def body(loc_ref, egl_ref, z_ref, gn_ref, o_ref, vn_ref, st_ref, dy_ref, dloc_ref, degl_ref, dz_ref, dgn_ref):
        @pl.when((pl.program_id(0) == 0) & (pl.program_id(1) == 0))
        def _():
            dgn_ref[...] = jnp.zeros_like(dgn_ref)

        gn = gn_ref[...]
        bdm = _bd_mask()
        shape = (DN_CHUNK, 2 * HEAD)
        tril = _iota2(shape, 0) >= (_iota2(shape, 1) & (HEAD - 1))

        def step(i, carry):
            ds, dgn = carry
            rows = pl.ds(pl.multiple_of((nc - 1 - i) * DN_CHUNK, DN_CHUNK), DN_CHUNK)
            dy, zz, oo = dy_ref[rows, :], z_ref[rows, :], o_ref[rows, :]
            on, sig, r = _gated_norm2(oo, zz, gn)
            sz = zz * sig
            dz_ref[rows, :] = dy * on * gn * (sig * (1.0 + zz * (1.0 - sig)))
            dgn = dgn + jnp.sum(dy * on * sz, axis=0, keepdims=True)
            don = dy * gn * sz
            do = r * (don - on * _half_sum(don * on) * (1.0 / HEAD))
            state, vnew = _bd(st_ref[rows, :]), vn_ref[rows, :]
            qg, kd, w, amat = loc_ref[0, rows, :], loc_ref[1, rows, :], loc_ref[3, rows, :], loc_ref[4, rows, :]
            dvn = _pk_tn(amat, do) + _nn(kd, ds)
            dloc_ref[0, rows, :] = _nt(do, state)
            dloc_ref[1, rows, :] = _nt(vnew, ds)
            dloc_ref[2, rows, :] = dvn
            dloc_ref[3, rows, :] = -_nt(dvn, state)
            dloc_ref[4, rows, :] = jnp.where(tril, _pk_nt(do, vnew), 0.0)
            degl = _half_sum(jnp.sum(state * ds, axis=0, keepdims=True))
            degl_ref[rows, :] = jnp.broadcast_to(degl, shape)
            grow = jnp.where(bdm, _tn(qg, do) - _tn(w, dvn), 0.0)
            return ds * _row0(egl_ref[rows, :]) + grow, dgn

        _, dgn = lax.fori_loop(0, nc, step, (jnp.zeros((2 * HEAD, 2 * HEAD), F32), jnp.zeros((1, 2 * HEAD), F32)))
        dgn_ref[...] += dgn

    zspec = pl.BlockSpec((seq, 2 * HEAD), lambda b, hp: (b, COL_DNZ + hp))
    one = _pair(seq)
    out = SDS((t, DN_HEADS * HEAD), F32)
    return pl.pallas_call(
        body, name=name, grid=(t // seq, DN_HEADS // 2),
        in_specs=[_pair(seq, 5), one, zspec, _whole((1, 2 * HEAD)), one, one, one, one],
        out_specs=[_pair(seq, 5), one, one, _whole((1, 2 * HEAD))],
        out_shape=[SDS((5, t, DN_HEADS * HEAD), F32), out, out, SDS((1, 2 * HEAD), F32)],
        compiler_params=_params(("arbitrary", "arbitrary")))(loc, egl, u, gn, o, vn, states, dy)


def gdn_prep_bwd(u, cw, alog_row, dt_row, dloc, degl, *, seq, name):
    t = u.shape[0]
    nc = seq // DN_CHUNK

    def body(q_ref, k_ref, v_ref, ba_ref, cq_ref, ck_ref, cv_ref, alog_ref, dt_ref, dloc_ref, degl_ref,
             dqkv_ref, dba_ref, dcw_ref, dhs_ref, act_sc, b_sc, gc_sc, c_sc):
        hp = pl.program_id(0)

        @pl.when(pl.program_id(1) == 0)
        def _():
            dcw_ref[...] = jnp.zeros_like(dcw_ref)
            dhs_ref[...] = jnp.zeros_like(dhs_ref)

        pre_refs, cw_refs = (q_ref, k_ref, v_ref), (cq_ref, ck_ref, cv_ref)
        _gdn_inputs(pre_refs, cw_refs, ba_ref, alog_ref, dt_ref, hp, act_sc, b_sc, gc_sc, c_sc)

        def finish(rows, m, tt):
            q, k, v, b = m["q"], m["k"], m["v"], m["b"]
            dqg, dkd, du, dw, da = (dloc_ref[x, rows, :] for x in range(5))
            dm, eg = m["dm"], m["eg"]
            dt = _pk_nt(du, m["vb"]) + _pk_nt(dw, m["kbg"])
            dvb, dkbg = _pk_tn(tt, du), _pk_tn(tt, dw)
            yield
            dtt = _pk_nt(dt, tt, hi=True)
            yield
            dl = jnp.where(m["strict"], -_pk_tn(tt, dtt, hi=True), 0.0)
            yield
            dkk = dl * dm
            dqk = da * dm
            dd = dl * m["kk"] + da * m["qk"]
            dkb = _pk_nn(dkk, k) + dkbg * eg
            dq = _pk_nn(dqk, k) + dqg * eg
            yield
            dk = _pk_tn(dkk, m["kb"]) + _pk_tn(dqk, q) + dkd * m["ekd"] + dkb * b
            db = _half_sum(dkb * k + dvb * v)
            yield
            mx = jnp.where(m["tril"], dd * dm, 0.0)
            tk = _half_sum(dkd * m["kd"])
            colsum = jnp.where(m["eye"], jnp.broadcast_to(jnp.sum(mx, axis=0, keepdims=True), mx.shape), 0.0)
            dgc = _half_sum(mx) - _half_sum(colsum) + _half_sum(dqg * m["qg"] + dkbg * m["kbg"]) - tk
            dglast = jnp.sum(tk, axis=0, keepdims=True) + _row0(degl_ref[rows, :]) * jnp.exp(m["glast"])
            act_sc[0, rows, :] = dq
            act_sc[1, rows, :] = dk
            act_sc[2, rows, :] = dvb * b
            b_sc[rows, :] = db
            gc_sc[rows, :] = dgc + jnp.where(m["row"] == DN_CHUNK - 1, dglast, 0.0)

        _gdn_chunk_loop(nc, act_sc, b_sc, gc_sc, finish)

        beta, g, beta_blk, sp_arg, a_exp, g_blk = _gdn_gates(ba_ref, alog_ref, dt_ref, hp)
        dg = _chunk_rev_cumsum(gc_sc[...])
        lane = _iota2(beta_blk.shape, 1)
        ha = 2 * hp
        db = b_sc[...]
        at = lambda idx, x_a, x_b: (jnp.where(lane == idx, _lane_col(x_a, 0), 0.0)
                                    + jnp.where(lane == idx + 1, _lane_col(x_b, HEAD), 0.0))
        dg_blk = at(DN_HEADS + ha, dg, dg)
        dal = dg_blk * (-a_exp) * _sigmoid(sp_arg)
        dba_ref[...] = at(ha, db, db) * beta_blk * (1.0 - beta_blk) + dal
        dhs_ref[0:1, :] += jnp.sum(dg_blk * g_blk, axis=0, keepdims=True)
        dhs_ref[1:2, :] += jnp.sum(dal, axis=0, keepdims=True)
        for idx in range(3):
            c = c_sc[idx]
            _, sig, hat, r = _gdn_act(c, _DN_SCALE[idx])
            da_ = act_sc[idx]
            if _DN_SCALE[idx] is not None:
                da_ = da_ * _DN_SCALE[idx]
                da_ = r * (da_ - hat * _half_sum(da_ * hat))
            dx, dcw = _conv_bwd(da_ * (sig * (1.0 + c * (1.0 - sig))), pre_refs[idx][...],
                                [cw_refs[idx][k:k + 1, :] for k in range(4)])
            dqkv_ref[idx] = dx
            dcw_ref[idx] += dcw

    pair = DN_HEADS // 2
    in_specs = [_swap(s) for s in _gdn_in_specs(seq)] + [_swap(_pair(seq, 5)), _swap(_pair(seq))]
    return pl.pallas_call(
        body, name=name, grid=(pair, t // seq), in_specs=in_specs,
        out_specs=[_swap(_pair(seq, 3)), pl.BlockSpec((None, seq, 2 * HEAD), lambda hp, b: (hp, b, 0)),
                   pl.BlockSpec((3, 4, 2 * HEAD), lambda hp, b: (0, 0, hp)),
                   pl.BlockSpec((None, 2, 2 * HEAD), lambda hp, b: (hp, 0, 0))],
        out_shape=[SDS((3, t, DN_HEADS * HEAD), F32), SDS((pair, t, 2 * HEAD), F32), SDS((3, 4, DN_HEADS * HEAD), F32),
                   SDS((pair, 2, 2 * HEAD), F32)],
        scratch_shapes=[pltpu.VMEM((3, seq, 2 * HEAD), F32)] + [pltpu.VMEM((seq, 2 * HEAD), F32)] * 2
        + [pltpu.VMEM((3, seq, 2 * HEAD), F32)],
        compiler_params=_params(("arbitrary", "arbitrary")))(u, u, u, u, cw, cw, cw, alog_row, dt_row, dloc, degl)


def mix_out(y_lru, o, y_dn, w_out, h, *, name, tm=512):
    t, d = h.shape
    tm = min(tm, t)

    def body(a_ref, b_ref, c_ref, w_ref, h_ref, o_ref, y_ref):
        y_ref[:, 0:LRU_W] = a_ref[...].astype(BF16)
        y_ref[:, LRU_W:LRU_W + ATT_W] = b_ref[...].astype(BF16)
        y_ref[:, LRU_W + ATT_W:] = c_ref[...].astype(BF16)
        o_ref[...] = h_ref[...] + _nn(y_ref[...], w_ref[...])

    rows = lambda width: pl.BlockSpec((tm, width), lambda i: (i, 0))
    return pl.pallas_call(
        body, name=name, grid=(t // tm,), in_specs=[rows(LRU_W), rows(ATT_W), rows(LRU_W), _whole((d, d)), rows(d)],
        out_specs=[rows(d), rows(d)], out_shape=[SDS((t, d), F32), SDS((t, d), BF16)],
        compiler_params=_params(("parallel",)))(y_lru, o, y_dn, w_out, h)


def mix_out_bwd(dout, w_out, *, name, tm=512):
    t, d = dout.shape
    tm = min(tm, t)

    def body(d_ref, w_ref, a_ref, b_ref, c_ref):
        dy = _nt(d_ref[...], w_ref[...])
        a_ref[...] = dy[:, 0:LRU_W]
        b_ref[...] = dy[:, LRU_W:LRU_W + ATT_W]
        c_ref[...] = dy[:, LRU_W + ATT_W:]

    rows = lambda width: pl.BlockSpec((tm, width), lambda i: (i, 0))
    return pl.pallas_call(
        body, name=name, grid=(t // tm,), in_specs=[rows(d), _whole((d, d))], out_specs=[rows(LRU_W), rows(ATT_W), rows(LRU_W)],
        out_shape=[SDS((t, LRU_W), F32), SDS((t, ATT_W), F32), SDS((t, LRU_W), F32)],
        compiler_params=_params(("parallel",)))(dout, w_out)


def mix_in_bwd(h, gain, dout, w_in, dx, dgate, dq, dk, dv, dqkv, dz, dba, *, name, tm=512):
    t, d = h.shape
    tm = min(tm, t)

    def body(h_ref, g_ref, do_ref, w_ref, dx_ref, dgate_ref, dq_ref, dk_ref, dv_ref, dqkv_ref, dz_ref, dba_ref,
             dh_ref, dg_ref, du_ref):
        @pl.when(pl.program_id(0) == 0)
        def _():
            dg_ref[...] = jnp.zeros_like(dg_ref)

        off = 0
        for piece in (dx_ref[...], dgate_ref[...], dq_ref[...], dk_ref[...], dv_ref[...], dqkv_ref[0], dqkv_ref[1],
                      dqkv_ref[2], dz_ref[...], dba_ref[0] + dba_ref[1]):
            du_ref[:, off:off + piece.shape[1]] = piece.astype(BF16)
            off += piece.shape[1]
        du_ref[:, off:] = jnp.zeros((tm, D_IN_PAD - off), BF16)
        g = g_ref[...]
        _, xh, r = _rms_fwd(h_ref[...], g)
        dh, dg = _rms_bwd(_nt(du_ref[...], w_ref[...]), xh, r, g)
        dh_ref[...] = do_ref[...] + dh
        dg_ref[...] += dg

    rows = lambda width: pl.BlockSpec((tm, width), lambda i: (i, 0))
    return pl.pallas_call(
        body, name=name, grid=(t // tm,),
        in_specs=[rows(d), _whole((1, d)), rows(d), _whole((d, D_IN_PAD)), rows(LRU_W), rows(LRU_W), rows(ATT_W),
                  rows(2 * HEAD), rows(2 * HEAD), pl.BlockSpec((3, tm, DN_HEADS * HEAD), lambda i: (0, i, 0)),
                  rows(DN_HEADS * HEAD), pl.BlockSpec((2, tm, 2 * HEAD), lambda i: (0, i, 0))],
        out_specs=[rows(d), _whole((1, d)), rows(D_IN_PAD)],
        out_shape=[SDS((t, d), F32), SDS((1, d), F32), SDS((t, D_IN_PAD), BF16)],
        compiler_params=_params(("arbitrary",)))(h, gain, dout, w_in, dx, dgate, dq, dk, dv, dqkv, dz, dba)


def _block_diag(w):
    out = jnp.zeros((LRU_W, LRU_W), w.dtype)
    for h in range(LRU_W // HEAD):
        out = lax.dynamic_update_slice(out, w[h], (h * HEAD, h * HEAD))
    return out


def _diag_blocks(w):
    per = LRU_HALF // HEAD
    return jnp.stack([w[h // per, (h % per) * HEAD:(h % per + 1) * HEAD, (h % per) * HEAD:(h % per + 1) * HEAD]
                      for h in range(LRU_W // HEAD)])


def layer_params(w, wl, l, bias):
    row = lambda a: a[l].reshape(1, -1)
    return dict(
        ffn1_norm=row(w["ffn1_norm"]), ffn1=(wl["ffn1_w_gate"], wl["ffn1_w_up"], wl["ffn1_w_down"]),
        mix_norm=row(w["mix_norm"]) + wl["tie1"][0:1, 0:1], w_in=wl["w_in"],
        lru=(wl["lru_conv_w"], row(w["lru_conv_b"]), _block_diag(w["lru_w_a"][l]), row(w["lru_b_a"]),
             _block_diag(w["lru_w_x"][l]), row(w["lru_b_x"]), row(w["lru_lambda"])),
        bias=bias, sink_rows=jnp.repeat(w["attn_sinks"][l], BLOCK_Q).reshape(ATT_HEADS * BLOCK_Q, 1),
        dn_cw=wl["dn_conv_w"], dn_alog=_ba_row(w["dn_a_log"][l]), dn_dt=_ba_row(w["dn_dt_bias"][l]),
        dn_norm=jnp.tile(row(w["dn_norm"]), (1, 2)), w_out=wl["w_out"],
        ffn2_norm=row(w["ffn2_norm"]), ffn2=(wl["ffn2_w_gate"], wl["ffn2_w_up"], wl["ffn2_w_down"]),
        ple_norm=row(w["ple_norm"]), ple_w_gate=wl["ple_w_gate"], ple_w_proj=wl["ple_w_proj"])


def _ba_row(per_head):
    return jnp.pad(per_head, (DN_HEADS, 2 * HEAD - 2 * DN_HEADS)).reshape(1, 2 * HEAD)


def mixer_fwd(h, p, nb, seq, tag):
    u, n = norm_matmul(h, p["mix_norm"], p["w_in"], tn=D_IN_PAD // 2, name=f"mix_in_{tag}")
    y_lru = lru_fwd(u, *p["lru"], seq=seq, name=f"lru_fwd_{tag}")
    o = swa_fwd(u, p["bias"], p["sink_rows"], seq=seq, name=f"swa_fwd_{tag}")
    loc, egl = gdn_prep(u, p["dn_cw"], p["dn_alog"], p["dn_dt"], seq=seq, name=f"gdn_prep_{tag}")
    y_dn, o_raw, vn, st = gdn_scan(loc, egl, u, p["dn_norm"], seq=seq, name=f"gdn_scan_{tag}")
    out, ycat = mix_out(y_lru, o, y_dn, p["w_out"], h, name=f"mix_out_{tag}")
    return out, dict(h=h, u=u, n=n, loc=loc, egl=egl, o_raw=o_raw, vn=vn, st=st, ycat=ycat)


def mixer_bwd(dout, s, p, nb, seq, tag):
    u = s["u"]
    dy_lru, do, dy_dn = mix_out_bwd(dout, p["w_out"], name=f"mix_out_dx_{tag}")
    g = {"w_out": matmul(s["ycat"], dout, ta=True, tm=1024, name=f"mix_out_dw_{tag}")}
    dx, dgate, dcw, dwa, dwx, dvec = lru_bwd(u, *p["lru"], dy_lru, seq=seq, name=f"lru_bwd_{tag}")
    g.update(lru_conv_w=dcw, lru_conv_b=dvec[0], lru_w_a=_diag_blocks(dwa), lru_b_a=dvec[1], lru_w_x=_diag_blocks(dwx),
             lru_b_x=dvec[2], lru_lambda=dvec[3])
    dq, dk, dv, dbias, dsink = swa_bwd(u, p["bias"], p["sink_rows"], do, seq=seq, name=f"swa_bwd_{tag}")
    g.update(attn_sinks=dsink.reshape(ATT_HEADS, BLOCK_Q).sum(axis=1), bias=dbias)
    dloc, degl, dz, dgn = gdn_scan_bwd(s["loc"], s["egl"], u, p["dn_norm"], s["o_raw"], s["vn"], s["st"], dy_dn, seq=seq,
                                       name=f"gdn_scan_bwd_{tag}")
    dqkv, dba, dcw3, dhs = gdn_prep_bwd(u, p["dn_cw"], p["dn_alog"], p["dn_dt"], dloc, degl, seq=seq,
                                        name=f"gdn_prep_bwd_{tag}")
    dhs = dhs.sum(axis=0)[:, DN_HEADS:2 * DN_HEADS]
    g.update(dn_conv_w=dcw3.transpose(1, 0, 2).reshape(4, 3 * DN_HEADS * HEAD), dn_a_log=dhs[0], dn_dt_bias=dhs[1],
             dn_norm=dgn[0, :HEAD] + dgn[0, HEAD:])
    dh, dgain, du = mix_in_bwd(s["h"], p["mix_norm"], dout, p["w_in"], dx, dgate, dq, dk, dv, dqkv, dz, dba,
                               name=f"mix_in_bwd_{tag}")
    g["w_in"] = matmul(s["n"], du, ta=True, tm=1024, tn=640, name=f"mix_in_dw_{tag}")
    g["mix_norm"] = dgain[0]
    return dh, g


SHARDED = ("ffn1_w_gate", "ffn1_w_up", "ffn1_w_down", "w_in", "w_out", "ffn2_w_gate", "ffn2_w_up", "ffn2_w_down",
           "ple_w_gate", "ple_w_proj")
PER_LAYER_SMALL = ("ffn1_norm", "mix_norm", "lru_conv_w", "lru_conv_b", "lru_w_a", "lru_b_a", "lru_w_x", "lru_b_x",
                   "lru_lambda", "attn_sinks", "dn_conv_w", "dn_a_log", "dn_dt_bias", "dn_norm", "ffn2_norm", "ple_norm")


GRAD_PARTS = (("ple_w_gate", "ple_w_proj", "ffn2_w_gate", "ffn2_w_up", "ffn2_w_down"), ("w_in", "w_out"),
              ("ffn1_w_gate", "ffn1_w_up", "ffn1_w_down"))
WEIGHT_PARTS = (("ffn1_w_gate", "ffn1_w_up", "ffn1_w_down"),
                ("w_in", "w_out", "ffn2_w_gate", "ffn2_w_up", "ffn2_w_down", "ple_w_gate", "ple_w_proj", "lru_conv_w",
                 "dn_conv_w"))


def _col_shards(a):
    r, c = a.shape
    return a.reshape(r, N_CHIP, c // N_CHIP).transpose(1, 0, 2)


def local_step(x, p, target, w, layer_weights, layer_grads, bmap, nb, seq):
    bias = relbias_fwd(w["rel_bias"], bmap, name="relbias_fwd")
    h, saved = x, []
    for l in range(N_LAYER):
        wl = layer_weights(l, 0, h)
        s = dict(h0=h)
        h, *s["ffn1"] = ffn_fwd(h, w["ffn1_norm"][l].reshape(1, -1) + wl["tie0"][0:1, 0:1], wl["ffn1_w_gate"],
                                wl["ffn1_w_up"], wl["ffn1_w_down"], name=f"ffn1_fwd_{l}")
        wl.update(layer_weights(l, 1, h))
        pr = layer_params(w, wl, l, bias)
        h, s["mix"] = mixer_fwd(h, pr, nb, seq, l)
        s["h2"] = h
        h, *s["ffn2"] = ffn_fwd(h, pr["ffn2_norm"], *pr["ffn2"], name=f"ffn2_fwd_{l}")
        s["h3"] = h
        h = ple_fwd(h, pr["ple_norm"], pr["ple_w_gate"], p[l], pr["ple_w_proj"], name=f"ple_fwd_{l}")
        saved.append((pr, s))
    dh, dgf, loss = loss_head(h, w["final_norm"].reshape(1, -1), target, name="loss_head")

    per_layer, dbias, token = [None] * N_LAYER, None, None
    for l in reversed(range(N_LAYER)):
        pr, s = saved[l]
        g = {}
        dout = dh
        ple_norm = pr["ple_norm"] if token is None else pr["ple_norm"] + token[0:1, 0:1]
        dh, n, dga, dpp, dg = ple_bwd(s["h3"], ple_norm, pr["ple_w_gate"], p[l], pr["ple_w_proj"], dout, name=f"ple_bwd_{l}")
        g["ple_norm"] = dg[0]
        g["ple_w_gate"] = matmul(n, dga, ta=True, tm=1024, name=f"ple_dwg_{l}").reshape(N_CHIP, -1, D_MODEL)
        g["ple_w_proj"] = _col_shards(matmul(p[l], dpp, ta=True, name=f"ple_dwp_{l}"))
        for nm, hin in (("ffn2", s["h2"]), ("ffn1", s["h0"])):
            if nm == "ffn1":
                lru = list(pr["lru"])
                lru[1] = lru[1] + token[0:1, 0:1]
                dh, gm = mixer_bwd(dh, s["mix"], dict(pr, lru=tuple(lru)), nb, seq, l)
                dbias = gm.pop("bias") if dbias is None else dbias + gm.pop("bias")
                gm["w_in"] = _col_shards(gm["w_in"][:, :D_IN])
                gm["w_out"] = gm["w_out"].reshape(N_CHIP, -1, D_MODEL)
                g.update(gm)
                token = layer_grads(l, 1, {k: g.pop(k) for k in GRAD_PARTS[1]}, dh)
            dout = dh
            n, a, b = s[nm]
            dh, da, db, sact, dg = ffn_bwd_act(hin, pr[nm + "_norm"] + token[0:1, 0:1] if nm == "ffn1" else pr[nm + "_norm"],
                                               dout, a, b, *pr[nm], name=f"{nm}_bwd_act_{l}")
            g[nm + "_norm"] = dg[0]
            g[nm + "_w_gate"], g[nm + "_w_up"], g[nm + "_w_down"] = ffn_bwd_w(n, da, db, sact, dout, name=f"{nm}_bwd_w_{l}")
            part = 0 if nm == "ffn2" else 2
            token = layer_grads(l, part, {k: g.pop(k) for k in GRAD_PARTS[part]}, dh)
        per_layer[l] = g
    grads = {k: jnp.stack([per_layer[l][k] for l in range(N_LAYER)]) for k in PER_LAYER_SMALL}
    grads["rel_bias"] = relbias_bwd(dbias, bmap, name="relbias_bwd")[:, :ATT_HEADS]
    grads["final_norm"] = dgf[0]
    return loss, dh, grads


HBM_SPEC = pl.BlockSpec(memory_space=pltpu.HBM)


def _place():
    x, y, c = lax.axis_index("x"), lax.axis_index("y"), lax.axis_index("c")
    chips = [(1 - x, y), (x, 1 - y), (1 - x, 1 - y)]
    return x, y, c, 2 * x + y, (x, y, 1 - c), chips, [2 * cx + cy for cx, cy in chips]


def _remote(src, dst, send_sem, recv_sem, to):
    return pltpu.make_async_remote_copy(src_ref=src, dst_ref=dst, send_sem=send_sem, recv_sem=recv_sem, device_id=to,
                                        device_id_type=MESH)


N_DEV = 8


def allreduce_small(buf, *, name):
    rows = buf.shape[0]

    def body(in_ref, out_ref, gath, send, recv):
        x, y, c = lax.axis_index("x"), lax.axis_index("y"), lax.axis_index("c")
        mine = 4 * x + 2 * y + c
        gath[mine] = in_ref[...]
        cps = []
        for k in range(1, N_DEV):
            to = (x ^ (k >> 2), y ^ ((k >> 1) & 1), c ^ (k & 1))
            cps.append(_remote(in_ref, gath.at[mine], send.at[k - 1], recv.at[k - 1], to))
            cps[-1].start()
        for k in range(1, N_DEV):
            theirs = gath.at[4 * (x ^ (k >> 2)) + 2 * (y ^ ((k >> 1) & 1)) + (c ^ (k & 1))]
            _remote(theirs, theirs, send.at[k - 1], recv.at[k - 1], (x, y, c)).wait_recv()
        for cp in cps:
            cp.wait_send()
        acc = gath[0]
        for d in range(1, N_DEV):
            acc = acc + gath[d]
        out_ref[...] = acc

    vm = pl.BlockSpec(memory_space=pltpu.VMEM)
    return pl.pallas_call(
        body, name=name, in_specs=[vm], out_specs=vm, out_shape=SDS(buf.shape, F32),
        scratch_shapes=[pltpu.VMEM((N_DEV, rows, 128), F32), pltpu.SemaphoreType.DMA((N_DEV - 1,)),
                        pltpu.SemaphoreType.DMA((N_DEV - 1,))])(buf)


SEM_SPEC = pl.BlockSpec(memory_space=pltpu.SEMAPHORE)
ANY_SPEC = pl.BlockSpec(memory_space=pl.ANY)
DATAFLOW = pltpu.SideEffectType.DATAFLOW_SIDE_EFFECTING


def _in_hbm(a):
    return pltpu.with_memory_space_constraint(a, pltpu.HBM)


def _my_rows(ref_rows, c, mine=True):
    half = ref_rows // 2
    start = (c if mine else 1 - c) * half
    return pl.ds(pl.multiple_of(start, 8), half)


def place_layer_shard(ws, layer, chip_arr, dtype, after, *, name):
    n = len(ws)
    _, r, c = ws[0].shape
    tr = next(cand for cand in (352, 256, 128, 64, 32, 16, 8, r) if r % cand == 0)

    def body(chip_ref, *refs):
        for w_ref, o_ref in zip(refs[:n], refs[n + 1:]):
            o_ref[...] = w_ref[...].astype(dtype)

    return list(pl.pallas_call(
        body, name=name,
        grid_spec=pltpu.PrefetchScalarGridSpec(
            num_scalar_prefetch=1, grid=(r // tr,),
            in_specs=[pl.BlockSpec((None, tr, c), lambda i, chip: (layer, i, 0))] * n + [ANY_SPEC],
            out_specs=[pl.BlockSpec((None, tr, c), lambda i, chip: (chip[0], i, 0))] * n),
        out_shape=[SDS((N_CHIP, r, c), dtype)] * n, compiler_params=_params(("parallel",)))(chip_arr, *ws, after))


def _gather_pieces(refs, n_split, c, me, cids):
    mine, theirs = [], []
    for k, ref in enumerate(refs):
        if k < n_split:
            rows = _my_rows(ref.shape[1], c)
            mine.append(ref.at[me, rows])
            theirs.append([ref.at[cid, rows] for cid in cids])
        else:
            mine.append(ref.at[me])
            theirs.append([ref.at[cid] for cid in cids])
    return mine, theirs


def gather_start(bufs, n_split, after, *, name):
    n = len(bufs)

    def body(*refs):
        ins, send, recv, token = refs[:n], refs[n + 1], refs[n + 2], refs[-1]
        x, y, c, me, sib, chips, cids = _place()
        mine, _ = _gather_pieces(ins, n_split, c, me, cids)
        for k in range(n):
            for j, chip in enumerate(chips):
                _remote(mine[k], mine[k], send.at[3 * k + j], recv.at[3 * k + j], (*chip, c)).start()
        token[...] = jnp.zeros_like(token)

    out = pl.pallas_call(
        body, name=name, in_specs=[HBM_SPEC] * n + [ANY_SPEC],
        out_specs=[SEM_SPEC, SEM_SPEC] + [HBM_SPEC] * n + [pl.BlockSpec(memory_space=pltpu.VMEM)],
        out_shape=[pltpu.SemaphoreType.DMA((3 * n,)), pltpu.SemaphoreType.DMA((3 * n,))]
        + [pltpu.HBM(b.shape, b.dtype) for b in bufs] + [SDS((8, 128), F32)],
        input_output_aliases={k: k + 2 for k in range(n)},
        compiler_params=pltpu.CompilerParams(has_side_effects=DATAFLOW))(*[_in_hbm(b) for b in bufs], after)
    return out[0], out[1], list(out[2:2 + n]), out[-1]


def gather_wait(send, recv, bufs, n_split, after, *, name):
    n = len(bufs)

    def body(*refs):
        ins, send_ref, recv_ref = refs[:n], refs[n], refs[n + 1]
        x, y, c, me, sib, chips, cids = _place()
        mine, theirs = _gather_pieces(ins, n_split, c, me, cids)
        for k in range(n):
            for j in range(3):
                _remote(mine[k], mine[k], send_ref.at[3 * k + j], recv_ref.at[3 * k + j], sib).wait_send()
                _remote(theirs[k][j], theirs[k][j], send_ref.at[3 * k + j], recv_ref.at[3 * k + j], sib).wait_recv()

    return list(pl.pallas_call(
        body, name=name, in_specs=[HBM_SPEC] * n + [SEM_SPEC, SEM_SPEC, ANY_SPEC], out_specs=[HBM_SPEC] * n,
        out_shape=[pltpu.HBM(b.shape, b.dtype) for b in bufs], input_output_aliases={k: k for k in range(n)},
        compiler_params=pltpu.CompilerParams(has_side_effects=DATAFLOW))(*bufs, send, recv, after))


def gather_forward(bufs, *, name):
    n = len(bufs)

    def body(*refs):
        outs, (send, recv) = refs[n:2 * n], refs[2 * n:]
        x, y, c, me, sib, chips, cids = _place()
        cps = []
        for k in range(n):
            for j in range(3):
                piece = outs[k].at[cids[j], _my_rows(outs[k].shape[1], c)]
                cps.append(_remote(piece, piece, send.at[3 * k + j], recv.at[3 * k + j], sib))
                cps[-1].start()
        for k in range(n):
            for j in range(3):
                piece = outs[k].at[cids[j], _my_rows(outs[k].shape[1], c, mine=False)]
                _remote(piece, piece, send.at[3 * k + j], recv.at[3 * k + j], sib).wait_recv()
        for cp in cps:
            cp.wait_send()

    return list(pl.pallas_call(
        body, name=name, in_specs=[HBM_SPEC] * n, out_specs=[HBM_SPEC] * n, out_shape=[SDS(b.shape, b.dtype) for b in bufs],
        input_output_aliases={k: k for k in range(n)}, scratch_shapes=[pltpu.SemaphoreType.DMA((3 * n,))] * 2)(*bufs))


def _exchange_copies(ins, lands, send, recv, c, sib):
    return [_remote(ins[k].at[pl.ds(0, N_CHIP), _my_rows(ins[k].shape[1], c, mine=False)], lands[k], send.at[k],
                    recv.at[k], sib) for k in range(len(ins))]


def exchange_start(gs, *, name):
    n = len(gs)

    def body(*refs):
        ins, lands, send, recv, token = refs[:n], refs[n:2 * n], refs[2 * n], refs[2 * n + 1], refs[-1]
        x, y, c, me, sib, chips, cids = _place()
        for cp in _exchange_copies(ins, lands, send, recv, c, sib):
            cp.start()
        token[...] = jnp.zeros_like(token)

    lands = [_in_hbm(lax.empty((N_CHIP, g.shape[1] // 2, g.shape[2]), g.dtype)) for g in gs]
    out = pl.pallas_call(
        body, name=name, in_specs=[HBM_SPEC] * (2 * n),
        out_specs=[SEM_SPEC, SEM_SPEC] + [HBM_SPEC] * (2 * n) + [pl.BlockSpec(memory_space=pltpu.VMEM)],
        out_shape=[pltpu.SemaphoreType.DMA((n,)), pltpu.SemaphoreType.DMA((n,))]
        + [pltpu.HBM(b.shape, b.dtype) for b in list(gs) + lands] + [SDS((8, 128), F32)],
        input_output_aliases={k: k + 2 for k in range(2 * n)},
        compiler_params=pltpu.CompilerParams(has_side_effects=DATAFLOW))(*[_in_hbm(g) for g in gs], *lands)
    return out[0], out[1], list(out[2:2 + n]), list(out[2 + n:2 + 2 * n]), out[-1]


def exchange_wait(send, recv, gs, lands, after, *, name):
    n = len(gs)

    def body(*refs):
        ins, land_refs, send_ref, recv_ref = refs[:n], refs[n:2 * n], refs[2 * n], refs[2 * n + 1]
        x, y, c, me, sib, chips, cids = _place()
        for cp in _exchange_copies(ins, land_refs, send_ref, recv_ref, c, sib):
            cp.wait_send()
            cp.wait_recv()

    out = pl.pallas_call(
        body, name=name, in_specs=[HBM_SPEC] * (2 * n) + [SEM_SPEC, SEM_SPEC, ANY_SPEC], out_specs=[HBM_SPEC] * (2 * n),
        out_shape=[pltpu.HBM(b.shape, b.dtype) for b in list(gs) + list(lands)],
        input_output_aliases={k: k for k in range(2 * n)},
        compiler_params=pltpu.CompilerParams(has_side_effects=DATAFLOW))(*gs, *lands, send, recv, after)
    return list(out[:n]), list(out[n:])


def _half_tile(half):
    return next(cand for cand in (256, 176, 128, 64, 32, 16) if half % cand == 0)


def _same_shape_runs(arrays):
    runs = []
    for i, a in enumerate(arrays):
        if runs and arrays[runs[-1][-1]].shape == a.shape:
            runs[-1].append(i)
        else:
            runs.append([i])
    return runs


def reduce_add(gs, rs, c_arr, *, name):
    n = len(gs)
    _, rows, cdim = gs[0].shape
    half = rows // 2
    tr = _half_tile(half)

    def body(c_ref, *refs):
        for g_ref, r_ref, o_ref in zip(refs[:n], refs[n:2 * n], refs[2 * n:]):
            o_ref[...] = (g_ref[...] + r_ref[...]).astype(o_ref.dtype)

    mine = pl.BlockSpec((None, tr, cdim), lambda j, i, c: (j, c[0] * (half // tr) + i, 0))
    blk = pl.BlockSpec((None, tr, cdim), lambda j, i, c: (j, i, 0))
    return list(pl.pallas_call(
        body, name=name,
        grid_spec=pltpu.PrefetchScalarGridSpec(
            num_scalar_prefetch=1, grid=(N_CHIP, half // tr), in_specs=[mine] * n + [blk] * n, out_specs=[blk] * n),
        out_shape=[SDS((N_CHIP, half, cdim), BF16)] * n, compiler_params=_params(("parallel", "parallel")))(c_arr, *gs, *rs))


def reduce_start(ss, *, name):
    n = len(ss)

    def body(*refs):
        ins, lands, send, recv, token = refs[:n], refs[n:2 * n], refs[2 * n], refs[2 * n + 1], refs[-1]
        x, y, c, me, sib, chips, cids = _place()
        for k in range(n):
            for j, chip in enumerate(chips):
                _remote(ins[k].at[cids[j]], lands[k].at[j], send.at[3 * k + j], recv.at[3 * k + j], (*chip, c)).start()
        token[...] = jnp.zeros_like(token)

    lands = [_in_hbm(lax.empty((N_CHIP - 1,) + s.shape[1:], s.dtype)) for s in ss]
    out = pl.pallas_call(
        body, name=name, in_specs=[HBM_SPEC] * (2 * n),
        out_specs=[SEM_SPEC, SEM_SPEC] + [HBM_SPEC] * (2 * n) + [pl.BlockSpec(memory_space=pltpu.VMEM)],
        out_shape=[pltpu.SemaphoreType.DMA((3 * n,)), pltpu.SemaphoreType.DMA((3 * n,))]
        + [pltpu.HBM(b.shape, b.dtype) for b in list(ss) + lands] + [SDS((8, 128), F32)],
        input_output_aliases={k: k + 2 for k in range(2 * n)},
        compiler_params=pltpu.CompilerParams(has_side_effects=DATAFLOW))(*[_in_hbm(s) for s in ss], *lands)
    return out[0], out[1], list(out[2:2 + n]), list(out[2 + n:2 + 2 * n]), out[-1]


def reduce_wait(send, recv, ss, lands, after, *, name):
    n = len(ss)

    def body(*refs):
        ins, land_refs, send_ref, recv_ref = refs[:n], refs[n:2 * n], refs[2 * n], refs[2 * n + 1]
        x, y, c, me, sib, chips, cids = _place()
        for k in range(n):
            for j in range(3):
                _remote(ins[k].at[cids[j]], land_refs[k].at[j], send_ref.at[3 * k + j], recv_ref.at[3 * k + j],
                        sib).wait_send()
                _remote(ins[k].at[cids[j]], land_refs[k].at[j], send_ref.at[3 * k + j], recv_ref.at[3 * k + j],
                        sib).wait_recv()

    out = pl.pallas_call(
        body, name=name, in_specs=[HBM_SPEC] * (2 * n) + [SEM_SPEC, SEM_SPEC, ANY_SPEC], out_specs=[HBM_SPEC] * (2 * n),
        out_shape=[pltpu.HBM(b.shape, b.dtype) for b in list(ss) + list(lands)],
        input_output_aliases={k: k for k in range(2 * n)},
        compiler_params=pltpu.CompilerParams(has_side_effects=DATAFLOW))(*ss, *lands, send, recv, after)
    return list(out[:n]), list(out[n:])


def reduce_sum(owns, lands, place_arr, layer, accs, *, name):
    n = len(owns)
    _, half, cdim = lands[0].shape
    tr = _half_tile(half)
    have = accs[0] is not None

    def body(p_ref, *refs):
        for own_ref, land_ref, o_ref in zip(refs[:n], refs[n:2 * n], refs[-n:]):
            o_ref[...] = (((own_ref[...].astype(F32) + land_ref[0].astype(F32)) + land_ref[1].astype(F32))
                          + land_ref[2].astype(F32))

    in_specs = ([pl.BlockSpec((None, tr, cdim), lambda i, p: (p[0], i, 0))] * n
                + [pl.BlockSpec((N_CHIP - 1, tr, cdim), lambda i, p: (0, i, 0))] * n + ([ANY_SPEC] * n if have else []))
    out_spec = pl.BlockSpec((None, tr, cdim), lambda i, p: (layer, p[1] * (half // tr) + i, 0))
    return list(pl.pallas_call(
        body, name=name,
        grid_spec=pltpu.PrefetchScalarGridSpec(
            num_scalar_prefetch=1, grid=(half // tr,), in_specs=in_specs, out_specs=[out_spec] * n),
        out_shape=[SDS((N_LAYER, 2 * half, cdim), F32)] * n,
        input_output_aliases={1 + 2 * n + i: i for i in range(n)} if have else {},
        compiler_params=_params(("parallel",)))(place_arr, *owns, *lands, *(accs if have else [])))


def reduce_share(fs, *, name):
    n = len(fs)

    def body(*refs):
        outs, (send, recv) = refs[n:2 * n], refs[2 * n:]
        x, y, c, me, sib, chips, cids = _place()
        cps = []
        for k in range(n):
            piece = outs[k].at[pl.ds(0, N_LAYER), _my_rows(outs[k].shape[1], c)]
            cps.append(_remote(piece, piece, send.at[k], recv.at[k], sib))
            cps[-1].start()
        for k in range(n):
            theirs = outs[k].at[pl.ds(0, N_LAYER), _my_rows(outs[k].shape[1], c, mine=False)]
            _remote(theirs, theirs, send.at[k], recv.at[k], sib).wait_recv()
        for cp in cps:
            cp.wait_send()

    return list(pl.pallas_call(
        body, name=name, in_specs=[HBM_SPEC] * n, out_specs=[HBM_SPEC] * n, out_shape=[SDS(f.shape, f.dtype) for f in fs],
        input_output_aliases={k: k for k in range(n)}, scratch_shapes=[pltpu.SemaphoreType.DMA((n,))] * 2)(*fs))


WEIGHTS = ("ffn1_norm", "ffn1_w_gate", "ffn1_w_up", "ffn1_w_down", "mix_norm", "w_in", "lru_conv_w", "lru_conv_b", "lru_w_a",
           "lru_b_a", "lru_w_x", "lru_b_x", "lru_lambda", "attn_sinks", "rel_bias", "dn_conv_w", "dn_a_log", "dn_dt_bias",
           "dn_norm", "w_out", "ffn2_norm", "ffn2_w_gate", "ffn2_w_up", "ffn2_w_down", "ple_norm", "ple_w_gate",
           "ple_w_proj", "final_norm")
CONV_SHARDED = ("lru_conv_w", "dn_conv_w")
FFN_TRANSPOSED = ("ffn1_w_gate", "ffn1_w_up", "ffn2_w_gate", "ffn2_w_up")
SMALL = tuple(k for k in WEIGHTS if k not in SHARDED)


def _pack(arrs):
    blocks = []
    for a in arrs:
        v = a.reshape(-1)
        blocks.append(jnp.pad(v, (0, -v.shape[0] % 1024)).reshape(-1, 128))
    return jnp.concatenate(blocks, axis=0)


def _unpack(buf, shapes):
    out, off = [], 0
    for s in shapes:
        n = int(np.prod(s))
        rows = 8 * -(-n // 1024)
        out.append(buf[off:off + rows].reshape(-1)[:n].reshape(s))
        off += rows
    return out


def kernel(x, p, ffn1_norm, ffn1_w_gate, ffn1_w_up, ffn1_w_down, mix_norm, w_in, lru_conv_w, lru_conv_b, lru_w_a, lru_b_a, lru_w_x, lru_b_x, lru_lambda, attn_sinks, rel_bias, dn_conv_w, dn_a_log, dn_dt_bias, dn_norm, w_out, ffn2_norm, ffn2_w_gate, ffn2_w_up, ffn2_w_down, ple_norm, ple_w_gate, ple_w_proj, final_norm, loss_target, m_ffn1_norm, m_ffn1_w_gate, m_ffn1_w_up, m_ffn1_w_down, m_mix_norm, m_w_in, m_lru_conv_w, m_lru_conv_b, m_lru_w_a, m_lru_b_a, m_lru_w_x, m_lru_b_x, m_lru_lambda, m_attn_sinks, m_rel_bias, m_dn_conv_w, m_dn_a_log, m_dn_dt_bias, m_dn_norm, m_w_out, m_ffn2_norm, m_ffn2_w_gate, m_ffn2_w_up, m_ffn2_w_down, m_ple_norm, m_ple_w_gate, m_ple_w_proj, m_final_norm, v_ffn1_norm, v_ffn1_w_gate, v_ffn1_w_up, v_ffn1_w_down, v_mix_norm, v_w_in, v_lru_conv_w, v_lru_conv_b, v_lru_w_a, v_lru_b_a, v_lru_w_x, v_lru_b_x, v_lru_lambda, v_attn_sinks, v_rel_bias, v_dn_conv_w, v_dn_a_log, v_dn_dt_bias, v_dn_norm, v_w_out, v_ffn2_norm, v_ffn2_w_gate, v_ffn2_w_up, v_ffn2_w_down, v_ple_norm, v_ple_w_gate, v_ple_w_proj, v_final_norm):
    given = dict(locals())
    stored = lambda k, a: jnp.swapaxes(a, 1, 2) if k in FFN_TRANSPOSED else a
    ws = {k: stored(k, given[k]) for k in WEIGHTS}
    ms = {k: stored(k, given["m_" + k]) for k in WEIGHTS}
    vs = {k: stored(k, given["v_" + k]) for k in WEIGHTS}
    nb, seq, d = x.shape
    t = nb * seq
    cx, cy, cc = lax.axis_index("x"), lax.axis_index("y"), lax.axis_index("c")
    chip = 2 * cx + cy

    chip_arr = chip.astype(jnp.int32).reshape(1)
    c_arr = cc.astype(jnp.int32).reshape(1)
    place_arr = jnp.stack([chip, cc]).astype(jnp.int32)
    groups = [(l, part) for l in range(N_LAYER) for part in range(len(WEIGHT_PARTS))]
    placed, started = {}, {}

    def place_group(i, after):
        l, part = groups[i]
        ks = WEIGHT_PARTS[part]
        for run in _same_shape_runs([ws[k] for k in ks]):
            outs = place_layer_shard([ws[ks[j]] for j in run], l, chip_arr, F32 if ks[run[0]] in CONV_SHARDED else BF16,
                                     after, name=f"place_{ks[run[0]]}_{l}")
            placed.update({(l, ks[j]): o for j, o in zip(run, outs)})

    def start_group(i, after):
        l, part = groups[i]
        ks = WEIGHT_PARTS[part]
        n_split = sum(k in SHARDED for k in ks)
        started[i] = (ks, n_split) + gather_start([placed[l, k] for k in ks], n_split, after, name=f"gather_start_{l}_{part}")

    place_group(0, jnp.zeros((8, 128), F32))
    start_group(0, jnp.zeros((8, 128), F32))
    for i in range(1, len(groups)):
        place_group(i, started[0][-1])

    def layer_weights(l, part, h):
        i = groups.index((l, part))
        ks, n_split, send, recv, bufs, _ = started[i]
        bufs = gather_wait(send, recv, bufs, n_split, h, name=f"gather_wait_{l}_{part}")
        tie = jnp.zeros((8, 128), F32)
        for nxt in [j for j in range(i + 1, len(groups)) if j not in started and groups[j][0] == groups[min(i + 1, len(groups) - 1)][0]]:
            start_group(nxt, bufs[0] if nxt == i + 1 else started[nxt - 1][-1])
            tie = started[nxt][-1]
        wl = dict(zip(ks, gather_forward(bufs[:n_split], name=f"gather_forward_{l}_{part}") + bufs[n_split:]))
        for k in ("w_in", "ple_w_proj", "lru_conv_w", "dn_conv_w"):
            if k in wl:
                wl[k] = wl[k].transpose(1, 0, 2).reshape(wl[k].shape[1], -1)
        for k in ("w_out", "ple_w_gate"):
            if k in wl:
                wl[k] = wl[k].reshape(-1, wl[k].shape[-1])
        if "w_in" in wl:
            wl["w_in"] = jnp.pad(wl["w_in"], ((0, 0), (0, D_IN_PAD - D_IN)))
        wl[f"tie{part}"] = tie
        return wl

    pending, finished, tokens = [], {k: None for k in SHARDED}, []

    def finish_reduce(after):
        ks, send, recv, sums, lands, l, part = pending.pop(0)
        sums, lands = reduce_wait(send, recv, sums, lands, after, name=f"reduce_wait_{l}_{part}")
        for run in _same_shape_runs(sums):
            outs = reduce_sum([sums[i] for i in run], [lands[i] for i in run], place_arr, l, [finished[ks[i]] for i in run],
                              name=f"reduce_sum_{ks[run[0]]}_{l}")
            finished.update({ks[i]: o for i, o in zip(run, outs)})

    swapping = []

    def start_reduce(after):
        ks, send, recv, gs, theirs, l, part = swapping.pop(0)
        gs, theirs = exchange_wait(send, recv, gs, theirs, after, name=f"exchange_wait_{l}_{part}")
        sums = [None] * len(ks)
        for run in _same_shape_runs(gs):
            outs = reduce_add([gs[i] for i in run], [theirs[i] for i in run], c_arr, name=f"reduce_add_{ks[run[0]]}_{l}")
            for i, o in zip(run, outs):
                sums[i] = o
        send, recv, sums, lands, token = reduce_start(sums, name=f"reduce_start_{l}_{part}")
        pending.append((ks, send, recv, sums, lands, l, part))
        tokens.append(token)
        return token

    def layer_grads(l, part, g, dh):
        ks = GRAD_PARTS[part]
        send, recv, gs, theirs, token = exchange_start([g[k] for k in ks], name=f"exchange_start_{l}_{part}")
        swapping.append((ks, send, recv, gs, theirs, l, part))
        if len(swapping) > 1:
            token = token + start_reduce(dh)
        while len(pending) > 2:
            finish_reduce(dh)
        return token

    small_w = {k: ws[k] for k in SMALL if k not in CONV_SHARDED}
    bmap = jnp.asarray(_rel_bucket_map())
    loss, gx, grads = local_step(x.reshape(t, d), p.reshape(N_LAYER, t, PLE_DIM), loss_target.reshape(t, d), small_w,
                                 layer_weights, layer_grads, bmap, nb, seq)
    while swapping:
        start_reduce(gx)
    g_out, delta, new_m, new_v = {}, {}, {}, {}

    small_shapes = [grads[k].shape for k in SMALL]
    g_small = dict(zip(SMALL, _unpack(allreduce_small(_pack([grads[k] for k in SMALL]), name="allreduce_small"), small_shapes)))
    for k in CONV_SHARDED:
        width = ws[k].shape[-1]
        g_small[k] = lax.dynamic_slice_in_dim(g_small[k], chip * width, width, axis=2)
    g_out.update(g_small)
    shapes = [ws[k].shape for k in SMALL]
    tie = tokens[-1][0:1, 0:1]
    res = adamw([_pack([ws[k] for k in SMALL]) + tie], *[[_pack([src[k] for k in SMALL])] for src in (g_out, ms, vs)],
                name="adamw_small")
    for dst, r in zip((delta, new_m, new_v), res):
        dst.update(zip(SMALL, _unpack(r[0], shapes)))

    after = res[0][0]
    two_d = lambda a: a.reshape(-1, a.shape[-1])
    for part, ks in enumerate(GRAD_PARTS):
        while pending and pending[0][0] == ks:
            finish_reduce(after)
        g_out.update(zip(ks, reduce_share([finished[k] for k in ks], name=f"reduce_share_{part}")))
        for run in _same_shape_runs([ws[k] for k in ks]):
            names = [ks[i] for i in run]
            res = adamw(*[[two_d(src[k]) for k in names] for src in (ws, g_out, ms, vs)], name=f"adamw_{names[0]}")
            for dst, rs in zip((delta, new_m, new_v), res):
                dst.update({k: r.reshape(ws[k].shape) for k, r in zip(names, rs)})
            after = res[0][0]

    total = lax.psum(loss[0, 0], ("x", "y", "c"))
    return (total, gx.reshape(nb, seq, d), *[stored(k, out[k]) for out in (g_out, delta, new_m, new_v) for k in WEIGHTS])
```

```python
import math

import numpy as np
import jax
import jax.numpy as jnp
from jax import lax
from jax.experimental import pallas as pl
from jax.experimental.pallas import tpu as pltpu

F32 = jnp.float32
BF16 = jnp.bfloat16

EPS = 1e-6
D_MODEL = 1024
D_FF = 2816
N_CHIP = 4
FF_BLK = D_FF // N_CHIP
HEAD = 64
LRU_W = 256
ATT_W = 512
ATT_HEADS = 8
KV_HEADS = 2
ATT_GROUP = 4
BLOCK_Q = 128
DN_HEADS = 4
DN_CHUNK = 64
D_IN = 2312
D_IN_PAD = 2560
PLE_DIM = 256
REL_BUCKETS = 32
LRU_C = 8.0
N_LAYER = 2

ADAM_LR, ADAM_B1, ADAM_B2, ADAM_EPS, ADAM_WD, ADAM_STEP = 0.001, 0.9, 0.999, 1e-08, 0.01, 10

VMEM_LIMIT = 56 << 20
MESH = pl.DeviceIdType.MESH
SDS = jax.ShapeDtypeStruct


def _dot(a, b, ca=1, cb=0, hi=False):
    dims = (((ca,), (cb,)), ((), ()))
    one = lambda u, v: lax.dot_general(u, v, dims, preferred_element_type=F32)
    a_hi, b_hi = a.astype(BF16), b.astype(BF16)
    if not hi:
        return one(a_hi, b_hi)
    a_lo = (a - a_hi.astype(F32)).astype(BF16)
    b_lo = (b - b_hi.astype(F32)).astype(BF16)
    return one(a_hi, b_hi) + (one(a_hi, b_lo) + one(a_lo, b_hi))


def _nn(a, b, hi=False):
    return _dot(a, b, 1, 0, hi)


def _nt(a, b, hi=False):
    return _dot(a, b, 1, 1, hi)


def _tn(a, b, hi=False):
    return _dot(a, b, 0, 0, hi)


def _sigmoid(x):
    return jax.nn.sigmoid(x)


def _softplus(x):
    return jnp.maximum(x, 0.0) + jnp.log1p(jnp.exp(-jnp.abs(x)))


def _neg_expm1(z):
    series = -z * (1.0 + z * (0.5 + z * (1.0 / 6.0 + z * (1.0 / 24.0 + z * (1.0 / 120.0)))))
    return jnp.where(z > -0.05, series, 1.0 - jnp.exp(z))


_GELU_C = math.sqrt(2.0 / math.pi)


def _gelu(x):
    t = jnp.tanh(_GELU_C * (x + 0.044715 * x * x * x))
    return 0.5 * x * (1.0 + t), t


def _gelu_grad(x, t):
    return 0.5 * (1.0 + t) + 0.5 * x * (1.0 - t * t) * _GELU_C * (1.0 + 3.0 * 0.044715 * x * x)


def _rms_fwd(h, g):
    r = lax.rsqrt(jnp.mean(h * h, axis=-1, keepdims=True) + EPS)
    xh = h * r
    return xh * g, xh, r


def _rms_bwd(dn, xh, r, g):
    dxh = dn * g
    dh = r * (dxh - xh * jnp.mean(dxh * xh, axis=-1, keepdims=True))
    return dh, jnp.sum(dn * xh, axis=0, keepdims=True)


def _shift_down(x, d, fill=0.0):
    row = lax.broadcasted_iota(jnp.int32, x.shape, 0)
    return jnp.where(row >= d, pltpu.roll(x, d, 0), fill)


def _shift_up(x, d, fill=0.0):
    n = x.shape[0]
    row = lax.broadcasted_iota(jnp.int32, x.shape, 0)
    return jnp.where(row < n - d, pltpu.roll(x, n - d, 0), fill)


def _conv_fwd(x, w):
    y = x * w[3]
    for k in range(3):
        y = y + _shift_down(x, 3 - k) * w[k]
    return y


def _conv_bwd(dy, x, w):
    dx = dy * w[3]
    rows = [None] * 4
    rows[3] = jnp.sum(dy * x, axis=0, keepdims=True)
    for k in range(3):
        dx = dx + _shift_up(dy, 3 - k) * w[k]
        rows[k] = jnp.sum(dy * _shift_down(x, 3 - k), axis=0, keepdims=True)
    r4 = lax.broadcasted_iota(jnp.int32, (4, x.shape[1]), 0)
    dw = jnp.zeros((4, x.shape[1]), F32)
    for k in range(4):
        dw = jnp.where(r4 == k, rows[k], dw)
    return dx, dw


FFN_SPLIT = 2


def _interleave(gens):
    pending = list(gens)
    while pending:
        for g in list(pending):
            if next(g, StopIteration) is StopIteration:
                pending.remove(g)


def _params(sem=None, vmem=VMEM_LIMIT):
    return pltpu.CompilerParams(dimension_semantics=sem, vmem_limit_bytes=vmem)


def _whole(shape):
    nd = len(shape)
    return pl.BlockSpec(shape, lambda *_: (0,) * nd)


def matmul(a, b, *, name, ta=False, tb=False, residual=None, out_dtype=F32, tm=512, tn=512, tk=512):
    m, k = (a.shape[1], a.shape[0]) if ta else a.shape
    n = b.shape[0] if tb else b.shape[1]
    tm, tn, tk = min(tm, m), min(tn, n), min(tk, k)
    assert m % tm == 0 and n % tn == 0 and k % tk == 0, (m, n, k, tm, tn, tk)
    nk = k // tk

    def body(*refs):
        if residual is None:
            a_ref, b_ref, o_ref, acc = refs
        else:
            a_ref, b_ref, r_ref, o_ref, acc = refs
        kk = pl.program_id(2)

        @pl.when(kk == 0)
        def _():
            acc[...] = jnp.zeros_like(acc)

        acc[...] += _dot(a_ref[...], b_ref[...], 0 if ta else 1, 1 if tb else 0)

        @pl.when(kk == nk - 1)
        def _():
            out = acc[...]
            if residual is not None:
                out = out + r_ref[...]
            o_ref[...] = out.astype(out_dtype)

    a_spec = pl.BlockSpec((tk, tm), lambda i, j, kk: (kk, i)) if ta else pl.BlockSpec((tm, tk), lambda i, j, kk: (i, kk))
    b_spec = pl.BlockSpec((tn, tk), lambda i, j, kk: (j, kk)) if tb else pl.BlockSpec((tk, tn), lambda i, j, kk: (kk, j))
    o_spec = pl.BlockSpec((tm, tn), lambda i, j, kk: (i, j))
    in_specs, args = [a_spec, b_spec], [a, b]
    if residual is not None:
        in_specs.append(o_spec)
        args.append(residual)
    return pl.pallas_call(
        body, name=name, grid=(m // tm, n // tn, nk), in_specs=in_specs, out_specs=o_spec,
        out_shape=SDS((m, n), out_dtype), scratch_shapes=[pltpu.VMEM((tm, tn), F32)],
        compiler_params=_params(("parallel", "parallel", "arbitrary")))(*args)


def norm_matmul(h, gain, w, *, name, tm=512, tn=512):
    t, d = h.shape
    tm = min(tm, t)
    n = w.shape[1]
    assert t % tm == 0 and n % tn == 0

    def body(h_ref, g_ref, w_ref, u_ref, n_ref):
        @pl.when(pl.program_id(1) == 0)
        def _():
            n_ref[...] = _rms_fwd(h_ref[...], g_ref[...])[0].astype(BF16)

        u_ref[...] = _nn(n_ref[...], w_ref[...])

    return pl.pallas_call(
        body, name=name, grid=(t // tm, n // tn),
        in_specs=[pl.BlockSpec((tm, d), lambda i, j: (i, 0)), _whole((1, d)), pl.BlockSpec((d, tn), lambda i, j: (0, j))],
        out_specs=[pl.BlockSpec((tm, tn), lambda i, j: (i, j)), pl.BlockSpec((tm, d), lambda i, j: (i, 0))],
        out_shape=[SDS((t, n), F32), SDS((t, d), BF16)],
        compiler_params=_params(("parallel", "arbitrary")))(h, gain, w)


def ffn_fwd(h, gain, wg, wu, wd, *, name, tm=1024):
    t, d = h.shape
    tm = min(tm, t)

    def body(h_ref, g_ref, wg_ref, wu_ref, wd_ref, o_ref, n_ref, a_ref, b_ref, acc):
        j = pl.program_id(1)

        @pl.when(j == 0)
        def _():
            n_ref[...] = _rms_fwd(h_ref[...], g_ref[...])[0].astype(BF16)
            acc[...] = jnp.zeros_like(acc)

        def part(rows):
            n = n_ref[rows, :]
            a = _nt(n, wg_ref[...])
            b = _nt(n, wu_ref[...])
            yield
            a_ref[rows, :] = a.astype(BF16)
            b_ref[rows, :] = b.astype(BF16)
            acc[rows, :] += _nn(a * _sigmoid(a) * b, wd_ref[...])

        _interleave([part(pl.ds(k * (tm // FFN_SPLIT), tm // FFN_SPLIT)) for k in range(FFN_SPLIT)])

        @pl.when(j == N_CHIP - 1)
        def _():
            o_ref[...] = h_ref[...] + 0.5 * acc[...]

    row = pl.BlockSpec((tm, d), lambda i, j: (i, 0))
    blk = pl.BlockSpec((None, tm, FF_BLK), lambda i, j: (j, i, 0))
    wspec = pl.BlockSpec((None, FF_BLK, d), lambda i, j: (j, 0, 0))
    act = SDS((N_CHIP, t, FF_BLK), BF16)
    return pl.pallas_call(
        body, name=name, grid=(t // tm, N_CHIP), in_specs=[row, _whole((1, d)), wspec, wspec, wspec],
        out_specs=[row, row, blk, blk], out_shape=[SDS((t, d), F32), SDS((t, d), BF16), act, act],
        scratch_shapes=[pltpu.VMEM((tm, d), F32)],
        compiler_params=_params(("parallel", "arbitrary")))(h, gain, wg, wu, wd)


def ffn_bwd_act(h, gain, dout, a, b, wg, wu, wd, *, name, tm=512):
    t, d = h.shape
    tm = min(tm, t)

    def body(h_ref, g_ref, do_ref, a_ref, b_ref, wg_ref, wu_ref, wd_ref, dh_ref, da_ref, db_ref, s_ref, dg_ref, dn_acc):
        i, j = pl.program_id(0), pl.program_id(1)

        @pl.when((i == 0) & (j == 0))
        def _():
            dg_ref[...] = jnp.zeros_like(dg_ref)

        @pl.when(j == 0)
        def _():
            dn_acc[...] = jnp.zeros_like(dn_acc)

        def part(rows):
            ds = _nt(0.5 * do_ref[rows, :], wd_ref[...])
            yield
            a = a_ref[rows, :].astype(F32)
            b = b_ref[rows, :].astype(F32)
            sig = _sigmoid(a)
            sa = a * sig
            db = ds * sa
            da = ds * b * (sig * (1.0 + a * (1.0 - sig)))
            s_ref[rows, :] = (sa * b).astype(BF16)
            da_ref[rows, :] = da.astype(BF16)
            db_ref[rows, :] = db.astype(BF16)
            yield
            dn_acc[rows, :] += _nn(da, wg_ref[...]) + _nn(db, wu_ref[...])

        _interleave([part(pl.ds(k * (tm // FFN_SPLIT), tm // FFN_SPLIT)) for k in range(FFN_SPLIT)])

        @pl.when(j == N_CHIP - 1)
        def _():
            g = g_ref[...]
            _, xh, r = _rms_fwd(h_ref[...], g)
            dh, dg = _rms_bwd(dn_acc[...], xh, r, g)
            dh_ref[...] = do_ref[...] + dh
            dg_ref[...] += dg

    row = pl.BlockSpec((tm, d), lambda i, j: (i, 0))
    blk = pl.BlockSpec((None, tm, FF_BLK), lambda i, j: (j, i, 0))
    wspec = pl.BlockSpec((None, FF_BLK, d), lambda i, j: (j, 0, 0))
    act = SDS((N_CHIP, t, FF_BLK), BF16)
    return pl.pallas_call(
        body, name=name, grid=(t // tm, N_CHIP), in_specs=[row, _whole((1, d)), row, blk, blk, wspec, wspec, wspec],
        out_specs=[row, blk, blk, blk, _whole((1, d))],
        out_shape=[SDS((t, d), F32), act, act, act, SDS((1, d), F32)],
        scratch_shapes=[pltpu.VMEM((tm, d), F32)],
        compiler_params=_params(("arbitrary", "arbitrary")))(h, gain, dout, a, b, wg, wu, wd)


def ffn_bwd_w(n, da, db, s, dout, *, name, tk=1024):
    t, d = n.shape
    tk = min(tk, t)

    def body(n_ref, da_ref, db_ref, s_ref, do_ref, dwg_ref, dwu_ref, dwd_ref):
        @pl.when(pl.program_id(1) == 0)
        def _():
            dwg_ref[...] = jnp.zeros_like(dwg_ref)
            dwu_ref[...] = jnp.zeros_like(dwu_ref)
            dwd_ref[...] = jnp.zeros_like(dwd_ref)

        nn = n_ref[...]
        dwg_ref[...] += _tn(da_ref[...], nn)
        dwu_ref[...] += _tn(db_ref[...], nn)
        dwd_ref[...] += _tn(s_ref[...], 0.5 * do_ref[...])

    row = pl.BlockSpec((tk, d), lambda j, kk: (kk, 0))
    blk = pl.BlockSpec((None, tk, FF_BLK), lambda j, kk: (j, kk, 0))
    return pl.pallas_call(
        body, name=name, grid=(N_CHIP, t // tk), in_specs=[row, blk, blk, blk, row],
        out_specs=[pl.BlockSpec((None, FF_BLK, d), lambda j, kk: (j, 0, 0)),
                   pl.BlockSpec((None, FF_BLK, d), lambda j, kk: (j, 0, 0)),
                   pl.BlockSpec((None, FF_BLK, d), lambda j, kk: (j, 0, 0))],
        out_shape=[SDS((N_CHIP, FF_BLK, d), F32)] * 3,
        compiler_params=_params(("parallel", "arbitrary")))(n, da, db, s, dout)


def ple_fwd(h, gain, wpg, pl_in, wpp, *, name, tm=512):
    t, d = h.shape
    tm = min(tm, t)
    pd = pl_in.shape[1]

    def body(h_ref, g_ref, wpg_ref, p_ref, wpp_ref, o_ref):
        hh = h_ref[...]
        n = _rms_fwd(hh, g_ref[...])[0]
        gate = _sigmoid(_nn(n, wpg_ref[...]))
        o_ref[...] = hh + gate * _nn(p_ref[...], wpp_ref[...])

    row = pl.BlockSpec((tm, d), lambda i: (i, 0))
    return pl.pallas_call(
        body, name=name, grid=(t // tm,),
        in_specs=[row, _whole((1, d)), _whole((d, d)), pl.BlockSpec((tm, pd), lambda i: (i, 0)), _whole((pd, d))],
        out_specs=row, out_shape=SDS((t, d), F32), compiler_params=_params(("parallel",)))(h, gain, wpg, pl_in, wpp)


def ple_bwd(h, gain, wpg, pl_in, wpp, dout, *, name, tm=512):
    t, d = h.shape
    tm = min(tm, t)
    pd = pl_in.shape[1]

    def body(h_ref, g_ref, wpg_ref, p_ref, wpp_ref, do_ref, dh_ref, n_ref, dga_ref, dpp_ref, dg_ref):
        @pl.when(pl.program_id(0) == 0)
        def _():
            dg_ref[...] = jnp.zeros_like(dg_ref)

        g = g_ref[...]
        n, xh, r = _rms_fwd(h_ref[...], g)
        gate = _sigmoid(_nn(n, wpg_ref[...]))
        pp = _nn(p_ref[...], wpp_ref[...])
        do = do_ref[...]
        dga = do * pp * gate * (1.0 - gate)
        dh, dg = _rms_bwd(_nt(dga, wpg_ref[...]), xh, r, g)
        dh_ref[...] = do + dh
        n_ref[...] = n.astype(BF16)
        dga_ref[...] = dga.astype(BF16)
        dpp_ref[...] = (do * gate).astype(BF16)
        dg_ref[...] += dg

    row = pl.BlockSpec((tm, d), lambda i: (i, 0))
    return pl.pallas_call(
        body, name=name, grid=(t // tm,),
        in_specs=[row, _whole((1, d)), _whole((d, d)), pl.BlockSpec((tm, pd), lambda i: (i, 0)), _whole((pd, d)), row],
        out_specs=[row, row, row, row, _whole((1, d))],
        out_shape=[SDS((t, d), F32), SDS((t, d), BF16), SDS((t, d), BF16), SDS((t, d), BF16), SDS((1, d), F32)],
        compiler_params=_params(("arbitrary",)))(h, gain, wpg, pl_in, wpp, dout)


def loss_head(h, gain, target, *, name, tm=512):
    t, d = h.shape
    tm = min(tm, t)

    def body(h_ref, g_ref, t_ref, dh_ref, dg_ref, l_ref):
        @pl.when(pl.program_id(0) == 0)
        def _():
            dg_ref[...] = jnp.zeros_like(dg_ref)
            l_ref[...] = jnp.zeros_like(l_ref)

        g = g_ref[...]
        y, xh, r = _rms_fwd(h_ref[...], g)
        err = y - t_ref[...]
        l_ref[...] += 0.5 * jnp.sum(jnp.mean(err * err, axis=-1, keepdims=True), axis=0, keepdims=True)
        dh, dg = _rms_bwd(err * (1.0 / d), xh, r, g)
        dh_ref[...] = dh
        dg_ref[...] += dg

    row = pl.BlockSpec((tm, d), lambda i: (i, 0))
    return pl.pallas_call(
        body, name=name, grid=(t // tm,), in_specs=[row, _whole((1, d)), row],
        out_specs=[row, _whole((1, d)), _whole((1, 1))],
        out_shape=[SDS((t, d), F32), SDS((1, d), F32), SDS((1, 1), F32)],
        compiler_params=_params(("arbitrary",)))(h, gain, target)


def adamw(ws, gs, ms, vs, *, name):
    n = len(ws)
    r, c = ws[0].shape
    budget = (24 << 20) // (2 * 7 * n * c * 4)
    tr = next((cand for cand in (704, 512, 352, 256, 176, 128, 64, 32, 16, 8) if r % cand == 0 and cand <= budget), r)

    def body(*refs):
        for w_ref, g_ref, m_ref, v_ref, d_ref, nm_ref, nv_ref in zip(*[refs[i * n:(i + 1) * n] for i in range(7)]):
            gg = g_ref[...]
            mm = ADAM_B1 * m_ref[...] + (1.0 - ADAM_B1) * gg
            vv = ADAM_B2 * v_ref[...] + (1.0 - ADAM_B2) * (gg * gg)
            m_hat = mm / (1.0 - ADAM_B1 ** ADAM_STEP)
            v_hat = vv / (1.0 - ADAM_B2 ** ADAM_STEP)
            d_ref[...] = -ADAM_LR * (m_hat / (jnp.sqrt(v_hat) + ADAM_EPS) + ADAM_WD * w_ref[...])
            nm_ref[...] = mm
            nv_ref[...] = vv

    blk = pl.BlockSpec((tr, c), lambda i: (i, 0))
    out = pl.pallas_call(body, name=name, grid=(r // tr,), in_specs=[blk] * (4 * n), out_specs=[blk] * (3 * n),
                         out_shape=[SDS((r, c), F32)] * (3 * n), compiler_params=_params(("parallel",)))(*ws, *gs, *ms, *vs)
    return list(out[:n]), list(out[n:2 * n]), list(out[2 * n:])


def _scan_fwd(a, b):
    d = 1
    while d < a.shape[0]:
        b = a * _shift_down(b, d, 0.0) + b
        a = a * _shift_down(a, d, 1.0)
        d *= 2
    return b


def _scan_rev(a, b):
    d = 1
    while d < a.shape[0]:
        b = a * _shift_up(b, d, 0.0) + b
        a = a * _shift_up(a, d, 1.0)
        d *= 2
    return b


LRU_HALF = 128


def _lru_in_specs(seq):
    half = LRU_W // LRU_HALF
    vec = pl.BlockSpec((1, LRU_HALF), lambda j, b: (0, j))
    mat = pl.BlockSpec((LRU_HALF, LRU_HALF), lambda j, b: (j, j))
    return [pl.BlockSpec((seq, LRU_HALF), lambda j, b: (b, j)), pl.BlockSpec((seq, LRU_HALF), lambda j, b: (b, half + j)),
            pl.BlockSpec((4, LRU_HALF), lambda j, b: (0, j)), vec, mat, vec, mat, vec, vec]


def _lru_math(x_ref, gate_ref, cw_ref, cb_ref, wa_ref, ba_ref, wx_ref, bx_ref, lam_ref):
    x = x_ref[...]
    gate = gate_ref[...]
    cw =[cw_ref[k:k + 1, :] for k in range(4)]
    xr = _conv_fwd(x, cw) + cb_ref[...]
    r = _sigmoid(_nn(xr, wa_ref[...]) + ba_ref[...])
    i = _sigmoid(_nn(xr, wx_ref[...]) + bx_ref[...])
    sp = _softplus(-lam_ref[...])
    log_a = -LRU_C * r * sp
    a = jnp.exp(log_a)
    mult = jnp.sqrt(_neg_expm1(2.0 * log_a))
    gi = i * xr
    h = _scan_fwd(a, mult * gi)
    gl, tg = _gelu(gate)
    return dict(x=x, gate=gate, cw=cw, xr=xr, r=r, i=i, sp=sp, a=a, mult=mult, gi=gi, h=h, gl=gl, tg=tg)


def lru_fwd(u, cw, cb, wa, ba, wx, bx, lam, *, seq, name):
    t = u.shape[0]

    def body(x_ref, gate_ref, cw_ref, cb_ref, wa_ref, ba_ref, wx_ref, bx_ref, lam_ref, y_ref):
        f = _lru_math(x_ref, gate_ref, cw_ref, cb_ref, wa_ref, ba_ref, wx_ref, bx_ref, lam_ref)
        y_ref[...] = f["gl"] * f["h"]

    return pl.pallas_call(
        body, name=name, grid=(LRU_W // LRU_HALF, t // seq), in_specs=_lru_in_specs(seq),
        out_specs=pl.BlockSpec((seq, LRU_HALF), lambda j, b: (b, j)), out_shape=SDS((t, LRU_W), F32),
        compiler_params=_params(("parallel", "parallel")))(u, u, cw, cb, wa, ba, wx, bx, lam)


def lru_bwd(u, cw, cb, wa, ba, wx, bx, lam, dy, *, seq, name):
    t = u.shape[0]

    def body(x_ref, gate_ref, cw_ref, cb_ref, wa_ref, ba_ref, wx_ref, bx_ref, lam_ref, dy_ref,
             dx_ref, dgate_ref, dcw_ref, dwa_ref, dwx_ref, dv_ref):
        @pl.when(pl.program_id(1) == 0)
        def _():
            dcw_ref[...] = jnp.zeros_like(dcw_ref)
            dwa_ref[...] = jnp.zeros_like(dwa_ref)
            dwx_ref[...] = jnp.zeros_like(dwx_ref)
            dv_ref[...] = jnp.zeros_like(dv_ref)

        f = _lru_math(x_ref, gate_ref, cw_ref, cb_ref, wa_ref, ba_ref, wx_ref, bx_ref, lam_ref)
        dy = dy_ref[...]
        a, h, xr, r, i, mult, gi, sp = f["a"], f["h"], f["xr"], f["r"], f["i"], f["mult"], f["gi"], f["sp"]
        dgate_ref[...] = dy * h * _gelu_grad(f["gate"], f["tg"])
        lamb = _scan_rev(_shift_up(a, 1, 0.0), dy * f["gl"])
        da = lamb * _shift_down(h, 1)
        dlog_a = da * a - (lamb * gi) * (a * a) / mult
        dgi = lamb * mult
        dra = dlog_a * (-LRU_C * sp) * r * (1.0 - r)
        dia = dgi * xr * i * (1.0 - i)
        dsp = jnp.sum(dlog_a * (-LRU_C * r), axis=0, keepdims=True)
        dlam = -dsp * _sigmoid(-lam_ref[...])
        dxr = dgi * i + _nt(dra, wa_ref[...]) + _nt(dia, wx_ref[...])
        dx, dcw = _conv_bwd(dxr, f["x"], f["cw"])
        dx_ref[...] = dx
        dcw_ref[...] += dcw
        dwa_ref[...] += _tn(xr, dra)
        dwx_ref[...] += _tn(xr, dia)
        rows = [jnp.sum(dxr, axis=0, keepdims=True), jnp.sum(dra, axis=0, keepdims=True),
                jnp.sum(dia, axis=0, keepdims=True), dlam]
        r8 = lax.broadcasted_iota(jnp.int32, (8, LRU_HALF), 0)
        acc = jnp.zeros((8, LRU_HALF), F32)
        for k, row in enumerate(rows):
            acc = jnp.where(r8 == k, row, acc)
        dv_ref[...] += acc

    nhalf = LRU_W // LRU_HALF
    col = pl.BlockSpec((seq, LRU_HALF), lambda j, b: (b, j))
    mat = pl.BlockSpec((None, LRU_HALF, LRU_HALF), lambda j, b: (j, 0, 0))
    return pl.pallas_call(
        body, name=name, grid=(nhalf, t // seq), in_specs=_lru_in_specs(seq) + [col],
        out_specs=[col, col, pl.BlockSpec((4, LRU_HALF), lambda j, b: (0, j)), mat, mat,
                   pl.BlockSpec((8, LRU_HALF), lambda j, b: (0, j))],
        out_shape=[SDS((t, LRU_W), F32), SDS((t, LRU_W), F32), SDS((4, LRU_W), F32),
                   SDS((nhalf, LRU_HALF, LRU_HALF), F32), SDS((nhalf, LRU_HALF, LRU_HALF), F32), SDS((8, LRU_W), F32)],
        compiler_params=_params(("arbitrary", "arbitrary")))(u, u, cw, cb, wa, ba, wx, bx, lam, dy)


NEG = -1e30


def _rel_bucket_map():
    dist = (np.arange(BLOCK_Q)[:, None] - np.arange(BLOCK_Q)[None, :]) % BLOCK_Q
    max_exact = REL_BUCKETS // 2
    large = max_exact + (np.log(np.maximum(dist, 1).astype(np.float32) / max_exact)
                         / math.log(BLOCK_Q / max_exact) * (REL_BUCKETS - max_exact)).astype(np.int32)
    large = np.minimum(large, REL_BUCKETS - 1)
    return np.where(dist < max_exact, dist, large).astype(np.int32)


def relbias_fwd(rel_bias, bmap, *, name):
    def body(rb_ref, bm_ref, o_ref):
        bm = bm_ref[...]
        for h in range(ATT_HEADS):
            acc = jnp.zeros((BLOCK_Q, BLOCK_Q), F32)
            for b in range(REL_BUCKETS):
                acc = jnp.where(bm == b, rb_ref[b, h], acc)
            o_ref[h] = acc

    return pl.pallas_call(
        body, name=name, in_specs=[pl.BlockSpec(memory_space=pltpu.SMEM), pl.BlockSpec(memory_space=pltpu.VMEM)],
        out_specs=pl.BlockSpec(memory_space=pltpu.VMEM), out_shape=SDS((ATT_HEADS, BLOCK_Q, BLOCK_Q), F32))(rel_bias, bmap)


def relbias_bwd(dbias, bmap, *, name):
    def body(db_ref, bm_ref, o_ref):
        bm = bm_ref[...]
        row = lax.broadcasted_iota(jnp.int32, (REL_BUCKETS, 128), 0)
        col = lax.broadcasted_iota(jnp.int32, (REL_BUCKETS, 128), 1)
        acc = jnp.zeros((REL_BUCKETS, 128), F32)
        for h in range(ATT_HEADS):
            d = db_ref[h]
            for b in range(REL_BUCKETS):
                s = jnp.sum(jnp.sum(jnp.where(bm == b, d, 0.0), axis=1, keepdims=True), axis=0, keepdims=True)
                acc = jnp.where((row == b) & (col == h), s, acc)
        o_ref[...] = acc

    return pl.pallas_call(body, name=name, out_shape=SDS((REL_BUCKETS, 128), F32))(dbias, bmap)


def _iota2(shape, axis):
    return lax.broadcasted_iota(jnp.int32, shape, axis)


def _chunk_cumsum(x):
    pos = _iota2(x.shape, 0) & (DN_CHUNK - 1)
    d = 1
    while d < DN_CHUNK:
        x = x + jnp.where(pos >= d, pltpu.roll(x, d, 0), 0.0)
        d *= 2
    return x


def _chunk_rev_cumsum(x):
    n = x.shape[0]
    pos = _iota2(x.shape, 0) & (DN_CHUNK - 1)
    d = 1
    while d < DN_CHUNK:
        x = x + jnp.where(pos < DN_CHUNK - d, pltpu.roll(x, n - d, 0), 0.0)
        d *= 2
    return x


_DN_SCALE = (HEAD ** -0.5, 1.0, None)
DN_UNROLL = 4


COL_Q, COL_K, COL_V = 512 // 128, 1024 // 128, 1152 // 128
COL_DNQ, COL_DNK, COL_DNV, COL_DNZ, COL_BA = 1280 // 128, 1536 // 128, 1792 // 128, 2048 // 128, 2304 // 128


def _lane_a(shape):
    return _iota2(shape, 1) < HEAD


def _bd(x):
    la = _lane_a(x.shape)
    return jnp.concatenate([jnp.where(la, x, 0.0), jnp.where(la, 0.0, x)], axis=0)


def _fold(m):
    return m[:HEAD] + m[HEAD:]


def _bd_mask():
    return (_iota2((2 * HEAD, 2 * HEAD), 0) < HEAD) == (_iota2((2 * HEAD, 2 * HEAD), 1) < HEAD)


def _pk_nn(x, y, hi=False):
    return _nn(x, _bd(y), hi)


def _pk_nt(u, v, hi=False):
    return _nt(u, _bd(v), hi)


def _pk_tn(x, y, hi=False):
    return _fold(jnp.where(_bd_mask(), _tn(x, y, hi), 0.0))


def _half_sum(x):
    la = _lane_a(x.shape)
    return jnp.where(la, jnp.sum(jnp.where(la, x, 0.0), axis=-1, keepdims=True),
                     jnp.sum(jnp.where(la, 0.0, x), axis=-1, keepdims=True))


def _lane_col(x, idx):
    return jnp.sum(jnp.where(_iota2(x.shape, 1) == idx, x, 0.0), axis=-1, keepdims=True)


def _row0(x):
    return jnp.max(x, axis=0, keepdims=True)


def _dup_kv(x, g):
    la = _lane_a(x.shape)
    rolled = pltpu.roll(x, HEAD, 1)
    return jnp.where(la, x, rolled) if g == 0 else jnp.where(la, rolled, x)


def _stack_heads(ref, g):
    la = _lane_a((BLOCK_Q, 2 * HEAD))
    parts = []
    for hh in range(ATT_GROUP):
        pair = ref[:, pl.ds(2 * HEAD * (2 * g + hh // 2), 2 * HEAD)]
        parts.append(jnp.where(la if hh % 2 == 0 else ~la, pair, 0.0))
    return jnp.concatenate(parts, axis=0)


def _unstack_heads(stack, ref, g):
    la = _lane_a((BLOCK_Q, 2 * HEAD))
    for j in range(2):
        top = stack[2 * j * BLOCK_Q:(2 * j + 1) * BLOCK_Q]
        bot = stack[(2 * j + 1) * BLOCK_Q:(2 * j + 2) * BLOCK_Q]
        ref[:, pl.ds(2 * HEAD * (2 * g + j), 2 * HEAD)] = jnp.where(la, top, bot)


def _swa_probs(q_ref, k_ref, v_ref, b_ref, s_ref, n, g):
    rows = ATT_GROUP * BLOCK_Q
    prev = pl.multiple_of(jnp.maximum(n - 1, 0) * BLOCK_Q, BLOCK_Q)
    cur = pl.multiple_of(n * BLOCK_Q, BLOCK_Q)
    kp, kc = _dup_kv(k_ref[pl.ds(prev, BLOCK_Q), :], g), _dup_kv(k_ref[pl.ds(cur, BLOCK_Q), :], g)
    vp, vc = _dup_kv(v_ref[pl.ds(prev, BLOCK_Q), :], g), _dup_kv(v_ref[pl.ds(cur, BLOCK_Q), :], g)
    qs = _stack_heads(q_ref, g) * (HEAD ** -0.5)
    bias = b_ref[pl.ds(ATT_GROUP * g, ATT_GROUP)].reshape(rows, BLOCK_Q)
    i = _iota2((rows, BLOCK_Q), 0) & (BLOCK_Q - 1)
    j = _iota2((rows, BLOCK_Q), 1)
    s_p = jnp.where((j > i) & (n > 0), _nt(qs, kp) + bias, NEG)
    s_c = jnp.where(j <= i, _nt(qs, kc) + bias, NEG)
    sink = s_ref[pl.ds(rows * g, rows), :]
    m = jnp.maximum(jnp.maximum(jnp.max(s_p, axis=-1, keepdims=True), jnp.max(s_c, axis=-1, keepdims=True)), sink)
    e_p, e_c, e_s = jnp.exp(s_p - m), jnp.exp(s_c - m), jnp.exp(sink - m)
    inv = 1.0 / (jnp.sum(e_p, axis=-1, keepdims=True) + jnp.sum(e_c, axis=-1, keepdims=True) + e_s)
    return e_p * inv, e_c * inv, e_s * inv, qs, kp, kc, vp, vc, prev, cur


def _swa_specs(seq):
    nblk = seq // BLOCK_Q
    qspec = pl.BlockSpec((BLOCK_Q, ATT_W), lambda b, n: (b * nblk + n, COL_Q * 128 // ATT_W))
    kspec = pl.BlockSpec((seq, 2 * HEAD), lambda b, n: (b, COL_K))
    vspec = pl.BlockSpec((seq, 2 * HEAD), lambda b, n: (b, COL_V))
    ospec = pl.BlockSpec((BLOCK_Q, ATT_W), lambda b, n: (b * nblk + n, 0))
    kvout = pl.BlockSpec((seq, 2 * HEAD), lambda b, n: (b, 0))
    return qspec, kspec, vspec, ospec, kvout, _whole((ATT_HEADS, BLOCK_Q, BLOCK_Q)), _whole((ATT_HEADS * BLOCK_Q, 1))


def swa_fwd(u, bias, sink_rows, *, seq, name):
    t = u.shape[0]

    def body(q_ref, k_ref, v_ref, b_ref, s_ref, o_ref):
        for g in range(KV_HEADS):
            p_p, p_c, _, _, _, _, vp, vc, _, _ = _swa_probs(q_ref, k_ref, v_ref, b_ref, s_ref, pl.program_id(1), g)
            _unstack_heads(_nn(p_p, vp) + _nn(p_c, vc), o_ref, g)

    qspec, kspec, vspec, ospec, kvout, bspec, sspec = _swa_specs(seq)
    return pl.pallas_call(
        body, name=name, grid=(t // seq, seq // BLOCK_Q), in_specs=[qspec, kspec, vspec, bspec, sspec], out_specs=ospec,
        out_shape=SDS((t, ATT_W), F32), compiler_params=_params(("parallel", "arbitrary")))(u, u, u, bias, sink_rows)


def swa_bwd(u, bias, sink_rows, do, *, seq, name):
    t = u.shape[0]

    def body(q_ref, k_ref, v_ref, b_ref, s_ref, do_ref, dq_ref, dk_ref, dv_ref, db_ref, ds_ref):
        b, n = pl.program_id(0), pl.program_id(1)

        @pl.when((b == 0) & (n == 0))
        def _():
            db_ref[...] = jnp.zeros_like(db_ref)
            ds_ref[...] = jnp.zeros_like(ds_ref)

        @pl.when(n == 0)
        def _():
            dk_ref[...] = jnp.zeros_like(dk_ref)
            dv_ref[...] = jnp.zeros_like(dv_ref)

        la = _lane_a((BLOCK_Q, 2 * HEAD))
        for g in range(KV_HEADS):
            p_p, p_c, p_s, qs, kp, kc, vp, vc, prev, cur = _swa_probs(q_ref, k_ref, v_ref, b_ref, s_ref, n, g)
            do = _stack_heads(do_ref, g)
            dp_p, dp_c = _nt(do, vp), _nt(do, vc)
            delta = jnp.sum(p_p * dp_p, axis=-1, keepdims=True) + jnp.sum(p_c * dp_c, axis=-1, keepdims=True)
            ds_p, ds_c = p_p * (dp_p - delta), p_c * (dp_c - delta)
            _unstack_heads((_nn(ds_p, kp) + _nn(ds_c, kc)) * (HEAD ** -0.5), dq_ref, g)
            mine = la if g == 0 else ~la

            def to_head(x):
                return jnp.where(mine, x + pltpu.roll(x, HEAD, 1), 0.0)

            dk_ref[pl.ds(prev, BLOCK_Q), :] += to_head(_tn(ds_p, qs))
            dk_ref[pl.ds(cur, BLOCK_Q), :] += to_head(_tn(ds_c, qs))
            dv_ref[pl.ds(prev, BLOCK_Q), :] += to_head(_tn(p_p, do))
            dv_ref[pl.ds(cur, BLOCK_Q), :] += to_head(_tn(p_c, do))
            db_ref[pl.ds(ATT_GROUP * g, ATT_GROUP)] += (ds_p + ds_c).reshape(ATT_GROUP, BLOCK_Q, BLOCK_Q)
            rows = ATT_GROUP * BLOCK_Q
            ds_ref[pl.ds(rows * g, rows), :] += -p_s * delta

    qspec, kspec, vspec, ospec, kvout, bspec, sspec = _swa_specs(seq)
    return pl.pallas_call(
        body, name=name, grid=(t // seq, seq // BLOCK_Q), in_specs=[qspec, kspec, vspec, bspec, sspec, ospec],
        out_specs=[ospec, kvout, kvout, bspec, sspec],
        out_shape=[SDS((t, ATT_W), F32), SDS((t, 2 * HEAD), F32), SDS((t, 2 * HEAD), F32),
                   SDS((ATT_HEADS, BLOCK_Q, BLOCK_Q), F32), SDS((ATT_HEADS * BLOCK_Q, 1), F32)],
        compiler_params=_params(("arbitrary", "arbitrary")))(u, u, u, bias, sink_rows, do)


def _gdn_gates(ba_ref, alog_ref, dt_ref, hp):
    blk = ba_ref[...]
    beta_blk = _sigmoid(blk)
    sp_arg = blk + dt_ref[...]
    a_exp = jnp.exp(alog_ref[...])
    g_blk = -a_exp * _softplus(sp_arg)
    la = _lane_a(blk.shape)
    ha = 2 * hp
    beta = jnp.where(la, _lane_col(beta_blk, ha), _lane_col(beta_blk, ha + 1))
    g = jnp.where(la, _lane_col(g_blk, DN_HEADS + ha), _lane_col(g_blk, DN_HEADS + ha + 1))
    return beta, g, beta_blk, sp_arg, a_exp, g_blk


def _gdn_act(c, scale):
    sig = _sigmoid(c)
    a = c * sig
    if scale is None:
        return a, sig, None, None
    r = lax.rsqrt(_half_sum(a * a) + EPS)
    return a * r * scale, sig, a * r, r


def _gdn_inputs(pre_refs, cw_refs, ba_ref, alog_ref, dt_ref, hp, act_sc, b_sc, gc_sc, c_sc=None):
    for idx in range(3):
        c = _conv_fwd(pre_refs[idx][...], [cw_refs[idx][k:k + 1, :] for k in range(4)])
        if c_sc is not None:
            c_sc[idx] = c
        act_sc[idx] = _gdn_act(c, _DN_SCALE[idx])[0]
    beta, g = _gdn_gates(ba_ref, alog_ref, dt_ref, hp)[:2]
    b_sc[...] = beta
    gc_sc[...] = _chunk_cumsum(g)


def _gdn_chunk(q, k, v, b, gcc):
    shape = q.shape
    row, lm = _iota2(shape, 0), _iota2(shape, 1) & (HEAD - 1)
    tril, strict, eye = row >= lm, row > lm, row == lm
    eg = jnp.exp(gcc)
    kb, vb = k * b, v * b
    kbg = kb * eg
    grow = jnp.sum(jnp.where(eye, gcc, 0.0), axis=0, keepdims=True)
    dm = jnp.exp(jnp.where(tril, gcc - grow, NEG))
    kk = _pk_nt(kb, k)
    glast = jnp.sum(jnp.where(row == DN_CHUNK - 1, gcc, 0.0), axis=0, keepdims=True)
    ekd = jnp.exp(glast - gcc)
    qk = _pk_nt(q, k)
    return dict(q=q, k=k, v=v, b=b, tril=tril, strict=strict, eye=eye, row=row, eg=eg, kb=kb, vb=vb, kbg=kbg, dm=dm, kk=kk,
                low=jnp.where(strict, kk * dm, 0.0), glast=glast, ekd=ekd, kd=k * ekd, qk=qk,
                amat=jnp.where(tril, qk * dm, 0.0), qg=q * eg, egl=jnp.broadcast_to(jnp.exp(glast), shape))


def _tri_inv_many(chunks):
    ms = [-m["low"] for m in chunks]
    ts = [m["eye"].astype(F32) + x for m, x in zip(chunks, ms)]
    for _ in range(int(math.log2(HEAD)) - 1):
        ms = [_pk_nn(x, x, hi=True) for x in ms]
        ts = [t + _pk_nn(t, x, hi=True) for t, x in zip(ts, ms)]
    return ts


def _gdn_chunk_loop(nc, act_sc, b_sc, gc_sc, finish):
    u = math.gcd(nc, DN_UNROLL)

    def step(i, carry):
        rows = [pl.ds(pl.multiple_of((i * u + j) * DN_CHUNK, DN_CHUNK), DN_CHUNK) for j in range(u)]
        chunks = [_gdn_chunk(act_sc[0, r, :], act_sc[1, r, :], act_sc[2, r, :], b_sc[r, :], gc_sc[r, :]) for r in rows]
        pending = [finish(r, m, t) for r, m, t in zip(rows, chunks, _tri_inv_many(chunks))]
        pending = [g for g in pending if g is not None]
        while pending:
            for g in list(pending):
                if next(g, StopIteration) is StopIteration:
                    pending.remove(g)
        return carry

    lax.fori_loop(0, nc // u, step, 0)


def _gdn_in_specs(seq):
    u_at = lambda col: pl.BlockSpec((seq, 2 * HEAD), lambda b, hp, _c=col: (b, _c + hp))
    cw_at = lambda col: pl.BlockSpec((4, 2 * HEAD), lambda b, hp, _c=col: (0, _c + hp))
    row = pl.BlockSpec((1, 2 * HEAD), lambda b, hp: (0, 0))
    ba = pl.BlockSpec((seq, 2 * HEAD), lambda b, hp: (b, COL_BA))
    return [u_at(COL_DNQ), u_at(COL_DNK), u_at(COL_DNV), ba, cw_at(0), cw_at(2), cw_at(4), row, row]


def _pair(seq, lead=None):
    if lead is None:
        return pl.BlockSpec((seq, 2 * HEAD), lambda b, hp: (b, hp))
    return pl.BlockSpec((lead, seq, 2 * HEAD), lambda b, hp: (0, b, hp))


def _swap(spec):
    return pl.BlockSpec(spec.block_shape, lambda hp, b, _f=spec.index_map: _f(b, hp))


def gdn_prep(u, cw, alog_row, dt_row, *, seq, name):
    t = u.shape[0]
    nc = seq // DN_CHUNK

    def body(q_ref, k_ref, v_ref, ba_ref, cq_ref, ck_ref, cv_ref, alog_ref, dt_ref, loc_ref, egl_ref, act_sc, b_sc, gc_sc):
        _gdn_inputs((q_ref, k_ref, v_ref), (cq_ref, ck_ref, cv_ref), ba_ref, alog_ref, dt_ref, pl.program_id(1),
                    act_sc, b_sc, gc_sc)

        def finish(rows, m, t):
            loc_ref[0, rows, :] = m["qg"]
            loc_ref[1, rows, :] = m["kd"]
            loc_ref[2, rows, :] = _pk_nn(t, m["vb"])
            loc_ref[3, rows, :] = _pk_nn(t, m["kbg"])
            loc_ref[4, rows, :] = m["amat"]
            egl_ref[rows, :] = m["egl"]

        _gdn_chunk_loop(nc, act_sc, b_sc, gc_sc, finish)

    return pl.pallas_call(
        body, name=name, grid=(t // seq, DN_HEADS // 2), in_specs=_gdn_in_specs(seq), out_specs=[_pair(seq, 5), _pair(seq)],
        out_shape=[SDS((5, t, DN_HEADS * HEAD), F32), SDS((t, DN_HEADS * HEAD), F32)],
        scratch_shapes=[pltpu.VMEM((3, seq, 2 * HEAD), F32)] + [pltpu.VMEM((seq, 2 * HEAD), F32)] * 2,
        compiler_params=_params(("parallel", "parallel")))(u, u, u, u, cw, cw, cw, alog_row, dt_row)


def _gated_norm2(o, z, gn):
    r = lax.rsqrt(_half_sum(o * o) * (1.0 / HEAD) + EPS)
    return o * r, _sigmoid(z), r


def gdn_scan(loc, egl, u, gn, *, seq, name):
    t = u.shape[0]
    nc = seq // DN_CHUNK

    npair = DN_HEADS // 2

    def body(loc_ref, egl_ref, z_ref, gn_ref, y_ref, o_ref, vn_ref, st_ref):
        gn = gn_ref[...]
        bdm = _bd_mask()

        def step(c, states):
            rows = pl.ds(pl.multiple_of(c * DN_CHUNK, DN_CHUNK), DN_CHUNK)
            new = [None] * npair

            def pair(hp):
                lanes = pl.ds(hp * 2 * HEAD, 2 * HEAD)
                state = states[hp]
                st_ref[rows, lanes] = _fold(state)
                vn = loc_ref[2, rows, lanes] - _nn(loc_ref[3, rows, lanes], state)
                yield
                o = _nn(loc_ref[0, rows, lanes], state) + _pk_nn(loc_ref[4, rows, lanes], vn)
                new[hp] = state * _row0(egl_ref[rows, lanes]) + jnp.where(bdm, _tn(loc_ref[1, rows, lanes], vn), 0.0)
                yield
                vn_ref[rows, lanes] = vn
                o_ref[rows, lanes] = o
                zz = z_ref[rows, lanes]
                on, sig, _ = _gated_norm2(o, zz, gn)
                y_ref[rows, lanes] = on * gn * (zz * sig)

            _interleave([pair(hp) for hp in range(npair)])
            return tuple(new)

        lax.fori_loop(0, nc, step, tuple(jnp.zeros((2 * HEAD, 2 * HEAD), F32) for _ in range(npair)))

    width = DN_HEADS * HEAD
    rows = pl.BlockSpec((seq, width), lambda b: (b, 0))
    out = SDS((t, width), F32)
    return pl.pallas_call(
        body, name=name, grid=(t // seq,),
        in_specs=[pl.BlockSpec((5, seq, width), lambda b: (0, b, 0)), rows,
                  pl.BlockSpec((seq, width), lambda b: (b, COL_DNZ * 2 * HEAD // width)), _whole((1, 2 * HEAD))],
        out_specs=[rows] * 4, out_shape=[out] * 4, compiler_params=_params(("parallel",)))(loc, egl, u, gn)


def gdn_scan_bwd(loc, egl, u, gn, o, vn, states, dy, *, seq, name):
    t = u.shape[0]
    nc = seq // DN_CHUNK

    npair = DN_HEADS // 2

    def body(loc_ref, egl_ref, z_ref, gn_ref, o_ref, vn_ref, st_ref, dy_ref, dloc_ref, degl_ref, dz_ref, dgn_ref):
        @pl.when(pl.program_id(0) == 0)
        def _():
            dgn_ref[...] = jnp.zeros_like(dgn_ref)

        gn = gn_ref[...]
        bdm = _bd_mask()
        shape = (DN_CHUNK, 2 * HEAD)
        tril = _iota2(shape, 0) >= (_iota2(shape, 1) & (HEAD - 1))

        def step(i, carry):
            rows = pl.ds(pl.multiple_of((nc - 1 - i) * DN_CHUNK, DN_CHUNK), DN_CHUNK)
            new = [None] * npair

            def pair(hp):
                lanes = pl.ds(hp * 2 * HEAD, 2 * HEAD)
                ds, dgn = carry[hp]
                dy, zz, oo = dy_ref[rows, lanes], z_ref[rows, lanes], o_ref[rows, lanes]
                on, sig, r = _gated_norm2(oo, zz, gn)
                sz = zz * sig
                dz_ref[rows, lanes] = dy * on * gn * (sig * (1.0 + zz * (1.0 - sig)))
                dgn = dgn + jnp.sum(dy * on * sz, axis=0, keepdims=True)
                don = dy * gn * sz
                do = r * (don - on * _half_sum(don * on) * (1.0 / HEAD))
                state, vnew = _bd(st_ref[rows, lanes]), vn_ref[rows, lanes]
                qg, kd, w, amat = (loc_ref[0, rows, lanes], loc_ref[1, rows, lanes], loc_ref[3, rows, lanes],
                                   loc_ref[4, rows, lanes])
                yield
                dvn = _pk_tn(amat, do) + _nn(kd, ds)
                dloc_ref[0, rows, lanes] = _nt(do, state)
                dloc_ref[1, rows, lanes] = _nt(vnew, ds)
                yield
                dloc_ref[2, rows, lanes] = dvn
                dloc_ref[3, rows, lanes] = -_nt(dvn, state)
                dloc_ref[4, rows, lanes] = jnp.where(tril, _pk_nt(do, vnew), 0.0)
                degl = _half_sum(jnp.sum(state * ds, axis=0, keepdims=True))
                degl_ref[rows, lanes] = jnp.broadcast_to(degl, shape)
                grow = jnp.where(bdm, _tn(qg, do) - _tn(w, dvn), 0.0)
                new[hp] = (ds * _row0(egl_ref[rows, lanes]) + grow, dgn)

            _interleave([pair(hp) for hp in range(npair)])
            return tuple(new)

        init = tuple((jnp.zeros((2 * HEAD, 2 * HEAD), F32), jnp.zeros((1, 2 * HEAD), F32)) for _ in range(npair))
        out = lax.fori_loop(0, nc, step, init)
        dgn_ref[...] += sum(dgn for _, dgn in out)

    width = DN_HEADS * HEAD
    once = pl.Buffered(1)
    rows = pl.BlockSpec((seq, width), lambda b: (b, 0), pipeline_mode=once)
    out_rows = pl.BlockSpec((seq, width), lambda b: (b, 0))
    out = SDS((t, width), F32)
    return pl.pallas_call(
        body, name=name, grid=(t // seq,),
        in_specs=[pl.BlockSpec((5, seq, width), lambda b: (0, b, 0), pipeline_mode=once), rows,
                  pl.BlockSpec((seq, width), lambda b: (b, COL_DNZ * 2 * HEAD // width), pipeline_mode=once),
                  _whole((1, 2 * HEAD)), rows, rows, rows, rows],
        out_specs=[pl.BlockSpec((5, seq, width), lambda b: (0, b, 0)), out_rows, out_rows, _whole((1, 2 * HEAD))],
        out_shape=[SDS((5, t, width), F32), out, out, SDS((1, 2 * HEAD), F32)],
        compiler_params=_params(("arbitrary",), vmem=60 << 20))(loc, egl, u, gn, o, vn, states, dy)


def gdn_prep_bwd(u, cw, alog_row, dt_row, dloc, degl, *, seq, name):
    t = u.shape[0]
    nc = seq // DN_CHUNK

    def body(q_ref, k_ref, v_ref, ba_ref, cq_ref, ck_ref, cv_ref, alog_ref, dt_ref, dloc_ref, degl_ref,
             dqkv_ref, dba_ref, dcw_ref, dhs_ref, act_sc, b_sc, gc_sc, c_sc):
        hp = pl.program_id(0)

        @pl.when(pl.program_id(1) == 0)
        def _():
            dcw_ref[...] = jnp.zeros_like(dcw_ref)
            dhs_ref[...] = jnp.zeros_like(dhs_ref)

        pre_refs, cw_refs = (q_ref, k_ref, v_ref), (cq_ref, ck_ref, cv_ref)
        _gdn_inputs(pre_refs, cw_refs, ba_ref, alog_ref, dt_ref, hp, act_sc, b_sc, gc_sc, c_sc)

        def finish(rows, m, tt):
            q, k, v, b = m["q"], m["k"], m["v"], m["b"]
            dqg, dkd, du, dw, da = (dloc_ref[x, rows, :] for x in range(5))
            dm, eg = m["dm"], m["eg"]
            dt = _pk_nt(du, m["vb"]) + _pk_nt(dw, m["kbg"])
            dvb, dkbg = _pk_tn(tt, du), _pk_tn(tt, dw)
            yield
            dtt = _pk_nt(dt, tt, hi=True)
            yield
            dl = jnp.where(m["strict"], -_pk_tn(tt, dtt, hi=True), 0.0)
            yield
            dkk = dl * dm
            dqk = da * dm
            dd = dl * m["kk"] + da * m["qk"]
            dkb = _pk_nn(dkk, k) + dkbg * eg
            dq = _pk_nn(dqk, k) + dqg * eg
            yield
            dk = _pk_tn(dkk, m["kb"]) + _pk_tn(dqk, q) + dkd * m["ekd"] + dkb * b
            db = _half_sum(dkb * k + dvb * v)
            yield
            mx = jnp.where(m["tril"], dd * dm, 0.0)
            tk = _half_sum(dkd * m["kd"])
            colsum = jnp.where(m["eye"], jnp.broadcast_to(jnp.sum(mx, axis=0, keepdims=True), mx.shape), 0.0)
            dgc = _half_sum(mx) - _half_sum(colsum) + _half_sum(dqg * m["qg"] + dkbg * m["kbg"]) - tk
            dglast = jnp.sum(tk, axis=0, keepdims=True) + _row0(degl_ref[rows, :]) * jnp.exp(m["glast"])
            act_sc[0, rows, :] = dq
            act_sc[1, rows, :] = dk
            act_sc[2, rows, :] = dvb * b
            b_sc[rows, :] = db
            gc_sc[rows, :] = dgc + jnp.where(m["row"] == DN_CHUNK - 1, dglast, 0.0)

        _gdn_chunk_loop(nc, act_sc, b_sc, gc_sc, finish)

        beta, g, beta_blk, sp_arg, a_exp, g_blk = _gdn_gates(ba_ref, alog_ref, dt_ref, hp)
        dg = _chunk_rev_cumsum(gc_sc[...])
        lane = _iota2(beta_blk.shape, 1)
        ha = 2 * hp
        db = b_sc[...]
        at = lambda idx, x_a, x_b: (jnp.where(lane == idx, _lane_col(x_a, 0), 0.0)
                                    + jnp.where(lane == idx + 1, _lane_col(x_b, HEAD), 0.0))
        dg_blk = at(DN_HEADS + ha, dg, dg)
        dal = dg_blk * (-a_exp) * _sigmoid(sp_arg)
        dba_ref[...] = at(ha, db, db) * beta_blk * (1.0 - beta_blk) + dal
        dhs_ref[0:1, :] += jnp.sum(dg_blk * g_blk, axis=0, keepdims=True)
        dhs_ref[1:2, :] += jnp.sum(dal, axis=0, keepdims=True)
        for idx in range(3):
            c = c_sc[idx]
            _, sig, hat, r = _gdn_act(c, _DN_SCALE[idx])
            da_ = act_sc[idx]
            if _DN_SCALE[idx] is not None:
                da_ = da_ * _DN_SCALE[idx]
                da_ = r * (da_ - hat * _half_sum(da_ * hat))
            dx, dcw = _conv_bwd(da_ * (sig * (1.0 + c * (1.0 - sig))), pre_refs[idx][...],
                                [cw_refs[idx][k:k + 1, :] for k in range(4)])
            dqkv_ref[idx] = dx
            dcw_ref[idx] += dcw

    pair = DN_HEADS // 2
    in_specs = [_swap(s) for s in _gdn_in_specs(seq)] + [_swap(_pair(seq, 5)), _swap(_pair(seq))]
    return pl.pallas_call(
        body, name=name, grid=(pair, t // seq), in_specs=in_specs,
        out_specs=[_swap(_pair(seq, 3)), pl.BlockSpec((None, seq, 2 * HEAD), lambda hp, b: (hp, b, 0)),
                   pl.BlockSpec((3, 4, 2 * HEAD), lambda hp, b: (0, 0, hp)),
                   pl.BlockSpec((None, 2, 2 * HEAD), lambda hp, b: (hp, 0, 0))],
        out_shape=[SDS((3, t, DN_HEADS * HEAD), F32), SDS((pair, t, 2 * HEAD), F32), SDS((3, 4, DN_HEADS * HEAD), F32),
                   SDS((pair, 2, 2 * HEAD), F32)],
        scratch_shapes=[pltpu.VMEM((3, seq, 2 * HEAD), F32)] + [pltpu.VMEM((seq, 2 * HEAD), F32)] * 2
        + [pltpu.VMEM((3, seq, 2 * HEAD), F32)],
        compiler_params=_params(("arbitrary", "arbitrary")))(u, u, u, u, cw, cw, cw, alog_row, dt_row, dloc, degl)


def mix_out(y_lru, o, y_dn, w_out, h, *, name, tm=512):
    t, d = h.shape
    tm = min(tm, t)

    def body(a_ref, b_ref, c_ref, w_ref, h_ref, o_ref, y_ref):
        y_ref[:, 0:LRU_W] = a_ref[...].astype(BF16)
        y_ref[:, LRU_W:LRU_W + ATT_W] = b_ref[...].astype(BF16)
        y_ref[:, LRU_W + ATT_W:] = c_ref[...].astype(BF16)
        o_ref[...] = h_ref[...] + _nn(y_ref[...], w_ref[...])

    rows = lambda width: pl.BlockSpec((tm, width), lambda i: (i, 0))
    return pl.pallas_call(
        body, name=name, grid=(t // tm,), in_specs=[rows(LRU_W), rows(ATT_W), rows(LRU_W), _whole((d, d)), rows(d)],
        out_specs=[rows(d), rows(d)], out_shape=[SDS((t, d), F32), SDS((t, d), BF16)],
        compiler_params=_params(("parallel",)))(y_lru, o, y_dn, w_out, h)


def mix_out_bwd(dout, w_out, *, name, tm=512):
    t, d = dout.shape
    tm = min(tm, t)

    def body(d_ref, w_ref, a_ref, b_ref, c_ref):
        dy = _nt(d_ref[...], w_ref[...])
        a_ref[...] = dy[:, 0:LRU_W]
        b_ref[...] = dy[:, LRU_W:LRU_W + ATT_W]
        c_ref[...] = dy[:, LRU_W + ATT_W:]

    rows = lambda width: pl.BlockSpec((tm, width), lambda i: (i, 0))
    return pl.pallas_call(
        body, name=name, grid=(t // tm,), in_specs=[rows(d), _whole((d, d))], out_specs=[rows(LRU_W), rows(ATT_W), rows(LRU_W)],
        out_shape=[SDS((t, LRU_W), F32), SDS((t, ATT_W), F32), SDS((t, LRU_W), F32)],
        compiler_params=_params(("parallel",)))(dout, w_out)


def mix_in_bwd(h, gain, dout, w_in, dx, dgate, dq, dk, dv, dqkv, dz, dba, *, name, tm=512):
    t, d = h.shape
    tm = min(tm, t)

    def body(h_ref, g_ref, do_ref, w_ref, dx_ref, dgate_ref, dq_ref, dk_ref, dv_ref, dqkv_ref, dz_ref, dba_ref,
             dh_ref, dg_ref, du_ref):
        @pl.when(pl.program_id(0) == 0)
        def _():
            dg_ref[...] = jnp.zeros_like(dg_ref)

        off = 0
        for piece in (dx_ref[...], dgate_ref[...], dq_ref[...], dk_ref[...], dv_ref[...], dqkv_ref[0], dqkv_ref[1],
                      dqkv_ref[2], dz_ref[...], dba_ref[0] + dba_ref[1]):
            du_ref[:, off:off + piece.shape[1]] = piece.astype(BF16)
            off += piece.shape[1]
        du_ref[:, off:] = jnp.zeros((tm, D_IN_PAD - off), BF16)
        g = g_ref[...]
        _, xh, r = _rms_fwd(h_ref[...], g)
        dh, dg = _rms_bwd(_nt(du_ref[...], w_ref[...]), xh, r, g)
        dh_ref[...] = do_ref[...] + dh
        dg_ref[...] += dg

    rows = lambda width: pl.BlockSpec((tm, width), lambda i: (i, 0))
    return pl.pallas_call(
        body, name=name, grid=(t // tm,),
        in_specs=[rows(d), _whole((1, d)), rows(d), _whole((d, D_IN_PAD)), rows(LRU_W), rows(LRU_W), rows(ATT_W),
                  rows(2 * HEAD), rows(2 * HEAD), pl.BlockSpec((3, tm, DN_HEADS * HEAD), lambda i: (0, i, 0)),
                  rows(DN_HEADS * HEAD), pl.BlockSpec((2, tm, 2 * HEAD), lambda i: (0, i, 0))],
        out_specs=[rows(d), _whole((1, d)), rows(D_IN_PAD)],
        out_shape=[SDS((t, d), F32), SDS((1, d), F32), SDS((t, D_IN_PAD), BF16)],
        compiler_params=_params(("arbitrary",)))(h, gain, dout, w_in, dx, dgate, dq, dk, dv, dqkv, dz, dba)


def _block_diag(w):
    out = jnp.zeros((LRU_W, LRU_W), w.dtype)
    for h in range(LRU_W // HEAD):
        out = lax.dynamic_update_slice(out, w[h], (h * HEAD, h * HEAD))
    return out


def _diag_blocks(w):
    per = LRU_HALF // HEAD
    return jnp.stack([w[h // per, (h % per) * HEAD:(h % per + 1) * HEAD, (h % per) * HEAD:(h % per + 1) * HEAD]
                      for h in range(LRU_W // HEAD)])


def layer_params(w, wl, l, bias):
    row = lambda a: a[l].reshape(1, -1)
    return dict(
        ffn1_norm=row(w["ffn1_norm"]), ffn1=(wl["ffn1_w_gate"], wl["ffn1_w_up"], wl["ffn1_w_down"]),
        mix_norm=row(w["mix_norm"]) + wl["tie1"][0:1, 0:1], w_in=wl["w_in"],
        lru=(wl["lru_conv_w"], row(w["lru_conv_b"]), _block_diag(w["lru_w_a"][l]), row(w["lru_b_a"]),
             _block_diag(w["lru_w_x"][l]), row(w["lru_b_x"]), row(w["lru_lambda"])),
        bias=bias, sink_rows=jnp.repeat(w["attn_sinks"][l], BLOCK_Q).reshape(ATT_HEADS * BLOCK_Q, 1),
        dn_cw=wl["dn_conv_w"], dn_alog=_ba_row(w["dn_a_log"][l]), dn_dt=_ba_row(w["dn_dt_bias"][l]),
        dn_norm=jnp.tile(row(w["dn_norm"]), (1, 2)), w_out=wl["w_out"],
        ffn2_norm=row(w["ffn2_norm"]), ffn2=(wl["ffn2_w_gate"], wl["ffn2_w_up"], wl["ffn2_w_down"]),
        ple_norm=row(w["ple_norm"]), ple_w_gate=wl["ple_w_gate"], ple_w_proj=wl["ple_w_proj"])


def _ba_row(per_head):
    return jnp.pad(per_head, (DN_HEADS, 2 * HEAD - 2 * DN_HEADS)).reshape(1, 2 * HEAD)


def mixer_fwd(h, p, nb, seq, tag):
    u, n = norm_matmul(h, p["mix_norm"], p["w_in"], tn=D_IN_PAD // 2, name=f"mix_in_{tag}")
    y_lru = lru_fwd(u, *p["lru"], seq=seq, name=f"lru_fwd_{tag}")
    o = swa_fwd(u, p["bias"], p["sink_rows"], seq=seq, name=f"swa_fwd_{tag}")
    loc, egl = gdn_prep(u, p["dn_cw"], p["dn_alog"], p["dn_dt"], seq=seq, name=f"gdn_prep_{tag}")
    y_dn, o_raw, vn, st = gdn_scan(loc, egl, u, p["dn_norm"], seq=seq, name=f"gdn_scan_{tag}")
    out, ycat = mix_out(y_lru, o, y_dn, p["w_out"], h, name=f"mix_out_{tag}")
    return out, dict(h=h, u=u, n=n, loc=loc, egl=egl, o_raw=o_raw, vn=vn, st=st, ycat=ycat)


def mixer_bwd(dout, s, p, nb, seq, tag):
    u = s["u"]
    dy_lru, do, dy_dn = mix_out_bwd(dout, p["w_out"], name=f"mix_out_dx_{tag}")
    g = {"w_out": matmul(s["ycat"], dout, ta=True, tm=1024, name=f"mix_out_dw_{tag}")}
    dx, dgate, dcw, dwa, dwx, dvec = lru_bwd(u, *p["lru"], dy_lru, seq=seq, name=f"lru_bwd_{tag}")
    g.update(lru_conv_w=dcw, lru_conv_b=dvec[0], lru_w_a=_diag_blocks(dwa), lru_b_a=dvec[1], lru_w_x=_diag_blocks(dwx),
             lru_b_x=dvec[2], lru_lambda=dvec[3])
    dq, dk, dv, dbias, dsink = swa_bwd(u, p["bias"], p["sink_rows"], do, seq=seq, name=f"swa_bwd_{tag}")
    g.update(attn_sinks=dsink.reshape(ATT_HEADS, BLOCK_Q).sum(axis=1), bias=dbias)
    dloc, degl, dz, dgn = gdn_scan_bwd(s["loc"], s["egl"], u, p["dn_norm"], s["o_raw"], s["vn"], s["st"], dy_dn, seq=seq,
                                       name=f"gdn_scan_bwd_{tag}")
    dqkv, dba, dcw3, dhs = gdn_prep_bwd(u, p["dn_cw"], p["dn_alog"], p["dn_dt"], dloc, degl, seq=seq,
                                        name=f"gdn_prep_bwd_{tag}")
    dhs = dhs.sum(axis=0)[:, DN_HEADS:2 * DN_HEADS]
    g.update(dn_conv_w=dcw3.transpose(1, 0, 2).reshape(4, 3 * DN_HEADS * HEAD), dn_a_log=dhs[0], dn_dt_bias=dhs[1],
             dn_norm=dgn[0, :HEAD] + dgn[0, HEAD:])
    dh, dgain, du = mix_in_bwd(s["h"], p["mix_norm"], dout, p["w_in"], dx, dgate, dq, dk, dv, dqkv, dz, dba,
                               name=f"mix_in_bwd_{tag}")
    g["w_in"] = matmul(s["n"], du, ta=True, tm=1024, tn=640, name=f"mix_in_dw_{tag}")
    g["mix_norm"] = dgain[0]
    return dh, g


SHARDED = ("ffn1_w_gate", "ffn1_w_up", "ffn1_w_down", "w_in", "w_out", "ffn2_w_gate", "ffn2_w_up", "ffn2_w_down",
           "ple_w_gate", "ple_w_proj")
PER_LAYER_SMALL = ("ffn1_norm", "mix_norm", "lru_conv_w", "lru_conv_b", "lru_w_a", "lru_b_a", "lru_w_x", "lru_b_x",
                   "lru_lambda", "attn_sinks", "dn_conv_w", "dn_a_log", "dn_dt_bias", "dn_norm", "ffn2_norm", "ple_norm")


GRAD_PARTS = (("ple_w_gate", "ple_w_proj", "ffn2_w_gate", "ffn2_w_up", "ffn2_w_down"), ("w_in", "w_out"),
              ("ffn1_w_gate", "ffn1_w_up", "ffn1_w_down"))
WEIGHT_PARTS = (("ffn1_w_gate", "ffn1_w_up", "ffn1_w_down"),
                ("w_in", "w_out", "ffn2_w_gate", "ffn2_w_up", "ffn2_w_down", "ple_w_gate", "ple_w_proj", "lru_conv_w",
                 "dn_conv_w"))


def _col_shards(a):
    r, c = a.shape
    return a.reshape(r, N_CHIP, c // N_CHIP).transpose(1, 0, 2)


def local_step(x, p, target, w, layer_weights, layer_grads, bmap, nb, seq):
    bias = relbias_fwd(w["rel_bias"], bmap, name="relbias_fwd")
    h, saved = x, []
    for l in range(N_LAYER):
        wl = layer_weights(l, 0, h)
        s = dict(h0=h)
        h, *s["ffn1"] = ffn_fwd(h, w["ffn1_norm"][l].reshape(1, -1) + wl["tie0"][0:1, 0:1], wl["ffn1_w_gate"],
                                wl["ffn1_w_up"], wl["ffn1_w_down"], name=f"ffn1_fwd_{l}")
        wl.update(layer_weights(l, 1, h))
        pr = layer_params(w, wl, l, bias)
        h, s["mix"] = mixer_fwd(h, pr, nb, seq, l)
        s["h2"] = h
        h, *s["ffn2"] = ffn_fwd(h, pr["ffn2_norm"], *pr["ffn2"], name=f"ffn2_fwd_{l}")
        s["h3"] = h
        h = ple_fwd(h, pr["ple_norm"], pr["ple_w_gate"], p[l], pr["ple_w_proj"], name=f"ple_fwd_{l}")
        saved.append((pr, s))
    dh, dgf, loss = loss_head(h, w["final_norm"].reshape(1, -1), target, name="loss_head")

    per_layer, dbias, token = [None] * N_LAYER, None, None
    for l in reversed(range(N_LAYER)):
        pr, s = saved[l]
        g = {}
        dout = dh
        ple_norm = pr["ple_norm"] if token is None else pr["ple_norm"] + token[0:1, 0:1]
        dh, n, dga, dpp, dg = ple_bwd(s["h3"], ple_norm, pr["ple_w_gate"], p[l], pr["ple_w_proj"], dout, name=f"ple_bwd_{l}")
        g["ple_norm"] = dg[0]
        g["ple_w_gate"] = matmul(n, dga, ta=True, tm=1024, name=f"ple_dwg_{l}").reshape(N_CHIP, -1, D_MODEL)
        g["ple_w_proj"] = _col_shards(matmul(p[l], dpp, ta=True, name=f"ple_dwp_{l}"))
        for nm, hin in (("ffn2", s["h2"]), ("ffn1", s["h0"])):
            if nm == "ffn1":
                lru = list(pr["lru"])
                lru[1] = lru[1] + token[0:1, 0:1]
                dh, gm = mixer_bwd(dh, s["mix"], dict(pr, lru=tuple(lru)), nb, seq, l)
                dbias = gm.pop("bias") if dbias is None else dbias + gm.pop("bias")
                gm["w_in"] = _col_shards(gm["w_in"][:, :D_IN])
                gm["w_out"] = gm["w_out"].reshape(N_CHIP, -1, D_MODEL)
                g.update(gm)
                token = layer_grads(l, 1, {k: g.pop(k) for k in GRAD_PARTS[1]}, dh)
            dout = dh
            n, a, b = s[nm]
            dh, da, db, sact, dg = ffn_bwd_act(hin, pr[nm + "_norm"] + token[0:1, 0:1] if nm == "ffn1" else pr[nm + "_norm"],
                                               dout, a, b, *pr[nm], name=f"{nm}_bwd_act_{l}")
            g[nm + "_norm"] = dg[0]
            g[nm + "_w_gate"], g[nm + "_w_up"], g[nm + "_w_down"] = ffn_bwd_w(n, da, db, sact, dout, name=f"{nm}_bwd_w_{l}")
            part = 0 if nm == "ffn2" else 2
            token = layer_grads(l, part, {k: g.pop(k) for k in GRAD_PARTS[part]}, dh)
        per_layer[l] = g
    grads = {k: jnp.stack([per_layer[l][k] for l in range(N_LAYER)]) for k in PER_LAYER_SMALL}
    grads["rel_bias"] = relbias_bwd(dbias, bmap, name="relbias_bwd")[:, :ATT_HEADS]
    grads["final_norm"] = dgf[0]
    return loss, dh, grads


HBM_SPEC = pl.BlockSpec(memory_space=pltpu.HBM)


def _place():
    x, y, c = lax.axis_index("x"), lax.axis_index("y"), lax.axis_index("c")
    chips = [(1 - x, y), (x, 1 - y), (1 - x, 1 - y)]
    return x, y, c, 2 * x + y, (x, y, 1 - c), chips, [2 * cx + cy for cx, cy in chips]


def _remote(src, dst, send_sem, recv_sem, to):
    return pltpu.make_async_remote_copy(src_ref=src, dst_ref=dst, send_sem=send_sem, recv_sem=recv_sem, device_id=to,
                                        device_id_type=MESH)


N_DEV = 8


def allreduce_small(buf, *, name):
    rows = buf.shape[0]

    def body(in_ref, out_ref, gath, send, recv):
        x, y, c = lax.axis_index("x"), lax.axis_index("y"), lax.axis_index("c")
        mine = 4 * x + 2 * y + c
        gath[mine] = in_ref[...]
        cps = []
        for k in range(1, N_DEV):
            to = (x ^ (k >> 2), y ^ ((k >> 1) & 1), c ^ (k & 1))
            cps.append(_remote(in_ref, gath.at[mine], send.at[k - 1], recv.at[k - 1], to))
            cps[-1].start()
        for k in range(1, N_DEV):
            theirs = gath.at[4 * (x ^ (k >> 2)) + 2 * (y ^ ((k >> 1) & 1)) + (c ^ (k & 1))]
            _remote(theirs, theirs, send.at[k - 1], recv.at[k - 1], (x, y, c)).wait_recv()
        for cp in cps:
            cp.wait_send()
        acc = gath[0]
        for d in range(1, N_DEV):
            acc = acc + gath[d]
        out_ref[...] = acc

    vm = pl.BlockSpec(memory_space=pltpu.VMEM)
    return pl.pallas_call(
        body, name=name, in_specs=[vm], out_specs=vm, out_shape=SDS(buf.shape, F32),
        scratch_shapes=[pltpu.VMEM((N_DEV, rows, 128), F32), pltpu.SemaphoreType.DMA((N_DEV - 1,)),
                        pltpu.SemaphoreType.DMA((N_DEV - 1,))])(buf)


SEM_SPEC = pl.BlockSpec(memory_space=pltpu.SEMAPHORE)
ANY_SPEC = pl.BlockSpec(memory_space=pl.ANY)
DATAFLOW = pltpu.SideEffectType.DATAFLOW_SIDE_EFFECTING


def _in_hbm(a):
    return pltpu.with_memory_space_constraint(a, pltpu.HBM)


def _my_rows(ref_rows, c, mine=True):
    half = ref_rows // 2
    start = (c if mine else 1 - c) * half
    return pl.ds(pl.multiple_of(start, 8), half)


def place_layer_shard(ws, layer, chip_arr, dtype, after, *, name):
    n = len(ws)
    _, r, c = ws[0].shape
    tr = next(cand for cand in (352, 256, 128, 64, 32, 16, 8, r) if r % cand == 0)

    def body(chip_ref, *refs):
        for w_ref, o_ref in zip(refs[:n], refs[n + 1:]):
            o_ref[...] = w_ref[...].astype(dtype)

    return list(pl.pallas_call(
        body, name=name,
        grid_spec=pltpu.PrefetchScalarGridSpec(
            num_scalar_prefetch=1, grid=(r // tr,),
            in_specs=[pl.BlockSpec((None, tr, c), lambda i, chip: (layer, i, 0))] * n + [ANY_SPEC],
            out_specs=[pl.BlockSpec((None, tr, c), lambda i, chip: (chip[0], i, 0))] * n),
        out_shape=[SDS((N_CHIP, r, c), dtype)] * n, compiler_params=_params(("parallel",)))(chip_arr, *ws, after))


def _gather_pieces(refs, n_split, c, me, cids):
    mine, theirs = [], []
    for k, ref in enumerate(refs):
        if k < n_split:
            rows = _my_rows(ref.shape[1], c)
            mine.append(ref.at[me, rows])
            theirs.append([ref.at[cid, rows] for cid in cids])
        else:
            mine.append(ref.at[me])
            theirs.append([ref.at[cid] for cid in cids])
    return mine, theirs


def gather_start(bufs, n_split, after, *, name):
    n = len(bufs)

    def body(*refs):
        ins, send, recv, token = refs[:n], refs[n + 1], refs[n + 2], refs[-1]
        x, y, c, me, sib, chips, cids = _place()
        mine, _ = _gather_pieces(ins, n_split, c, me, cids)
        for k in range(n):
            for j, chip in enumerate(chips):
                _remote(mine[k], mine[k], send.at[3 * k + j], recv.at[3 * k + j], (*chip, c)).start()
        token[...] = jnp.zeros_like(token)

    out = pl.pallas_call(
        body, name=name, in_specs=[HBM_SPEC] * n + [ANY_SPEC],
        out_specs=[SEM_SPEC, SEM_SPEC] + [HBM_SPEC] * n + [pl.BlockSpec(memory_space=pltpu.VMEM)],
        out_shape=[pltpu.SemaphoreType.DMA((3 * n,)), pltpu.SemaphoreType.DMA((3 * n,))]
        + [pltpu.HBM(b.shape, b.dtype) for b in bufs] + [SDS((8, 128), F32)],
        input_output_aliases={k: k + 2 for k in range(n)},
        compiler_params=pltpu.CompilerParams(has_side_effects=DATAFLOW))(*[_in_hbm(b) for b in bufs], after)
    return out[0], out[1], list(out[2:2 + n]), out[-1]


def gather_wait(send, recv, bufs, n_split, after, *, name):
    n = len(bufs)

    def body(*refs):
        ins, send_ref, recv_ref = refs[:n], refs[n], refs[n + 1]
        x, y, c, me, sib, chips, cids = _place()
        mine, theirs = _gather_pieces(ins, n_split, c, me, cids)
        for k in range(n):
            for j in range(3):
                _remote(mine[k], mine[k], send_ref.at[3 * k + j], recv_ref.at[3 * k + j], sib).wait_send()
                _remote(theirs[k][j], theirs[k][j], send_ref.at[3 * k + j], recv_ref.at[3 * k + j], sib).wait_recv()

    return list(pl.pallas_call(
        body, name=name, in_specs=[HBM_SPEC] * n + [SEM_SPEC, SEM_SPEC, ANY_SPEC], out_specs=[HBM_SPEC] * n,
        out_shape=[pltpu.HBM(b.shape, b.dtype) for b in bufs], input_output_aliases={k: k for k in range(n)},
        compiler_params=pltpu.CompilerParams(has_side_effects=DATAFLOW))(*bufs, send, recv, after))


def gather_forward(bufs, *, name):
    n = len(bufs)

    def body(*refs):
        outs, (send, recv) = refs[n:2 * n], refs[2 * n:]
        x, y, c, me, sib, chips, cids = _place()
        cps = []
        for k in range(n):
            for j in range(3):
                piece = outs[k].at[cids[j], _my_rows(outs[k].shape[1], c)]
                cps.append(_remote(piece, piece, send.at[3 * k + j], recv.at[3 * k + j], sib))
                cps[-1].start()
        for k in range(n):
            for j in range(3):
                piece = outs[k].at[cids[j], _my_rows(outs[k].shape[1], c, mine=False)]
                _remote(piece, piece, send.at[3 * k + j], recv.at[3 * k + j], sib).wait_recv()
        for cp in cps:
            cp.wait_send()

    return list(pl.pallas_call(
        body, name=name, in_specs=[HBM_SPEC] * n, out_specs=[HBM_SPEC] * n, out_shape=[SDS(b.shape, b.dtype) for b in bufs],
        input_output_aliases={k: k for k in range(n)}, scratch_shapes=[pltpu.SemaphoreType.DMA((3 * n,))] * 2)(*bufs))


def _exchange_copies(ins, lands, send, recv, c, sib):
    return [_remote(ins[k].at[pl.ds(0, N_CHIP), _my_rows(ins[k].shape[1], c, mine=False)], lands[k], send.at[k],
                    recv.at[k], sib) for k in range(len(ins))]


def exchange_start(gs, *, name):
    n = len(gs)

    def body(*refs):
        ins, lands, send, recv, token = refs[:n], refs[n:2 * n], refs[2 * n], refs[2 * n + 1], refs[-1]
        x, y, c, me, sib, chips, cids = _place()
        for cp in _exchange_copies(ins, lands, send, recv, c, sib):
            cp.start()
        token[...] = jnp.zeros_like(token)

    lands = [_in_hbm(lax.empty((N_CHIP, g.shape[1] // 2, g.shape[2]), g.dtype)) for g in gs]
    out = pl.pallas_call(
        body, name=name, in_specs=[HBM_SPEC] * (2 * n),
        out_specs=[SEM_SPEC, SEM_SPEC] + [HBM_SPEC] * (2 * n) + [pl.BlockSpec(memory_space=pltpu.VMEM)],
        out_shape=[pltpu.SemaphoreType.DMA((n,)), pltpu.SemaphoreType.DMA((n,))]
        + [pltpu.HBM(b.shape, b.dtype) for b in list(gs) + lands] + [SDS((8, 128), F32)],
        input_output_aliases={k: k + 2 for k in range(2 * n)},
        compiler_params=pltpu.CompilerParams(has_side_effects=DATAFLOW))(*[_in_hbm(g) for g in gs], *lands)
    return out[0], out[1], list(out[2:2 + n]), list(out[2 + n:2 + 2 * n]), out[-1]


def exchange_wait(send, recv, gs, lands, after, *, name):
    n = len(gs)

    def body(*refs):
        ins, land_refs, send_ref, recv_ref = refs[:n], refs[n:2 * n], refs[2 * n], refs[2 * n + 1]
        x, y, c, me, sib, chips, cids = _place()
        for cp in _exchange_copies(ins, land_refs, send_ref, recv_ref, c, sib):
            cp.wait_send()
            cp.wait_recv()

    out = pl.pallas_call(
        body, name=name, in_specs=[HBM_SPEC] * (2 * n) + [SEM_SPEC, SEM_SPEC, ANY_SPEC], out_specs=[HBM_SPEC] * (2 * n),
        out_shape=[pltpu.HBM(b.shape, b.dtype) for b in list(gs) + list(lands)],
        input_output_aliases={k: k for k in range(2 * n)},
        compiler_params=pltpu.CompilerParams(has_side_effects=DATAFLOW))(*gs, *lands, send, recv, after)
    return list(out[:n]), list(out[n:])


def _half_tile(half):
    return next(cand for cand in (256, 176, 128, 64, 32, 16) if half % cand == 0)


def _same_shape_runs(arrays):
    runs = []
    for i, a in enumerate(arrays):
        if runs and arrays[runs[-1][-1]].shape == a.shape:
            runs[-1].append(i)
        else:
            runs.append([i])
    return runs


def reduce_add(gs, rs, c_arr, *, name):
    n = len(gs)
    _, rows, cdim = gs[0].shape
    half = rows // 2
    tr = _half_tile(half)

    def body(c_ref, *refs):
        for g_ref, r_ref, o_ref in zip(refs[:n], refs[n:2 * n], refs[2 * n:]):
            o_ref[...] = (g_ref[...] + r_ref[...]).astype(o_ref.dtype)

    mine = pl.BlockSpec((None, tr, cdim), lambda j, i, c: (j, c[0] * (half // tr) + i, 0))
    blk = pl.BlockSpec((None, tr, cdim), lambda j, i, c: (j, i, 0))
    return list(pl.pallas_call(
        body, name=name,
        grid_spec=pltpu.PrefetchScalarGridSpec(
            num_scalar_prefetch=1, grid=(N_CHIP, half // tr), in_specs=[mine] * n + [blk] * n, out_specs=[blk] * n),
        out_shape=[SDS((N_CHIP, half, cdim), BF16)] * n, compiler_params=_params(("parallel", "parallel")))(c_arr, *gs, *rs))


def reduce_start(ss, *, name):
    n = len(ss)

    def body(*refs):
        ins, lands, send, recv, token = refs[:n], refs[n:2 * n], refs[2 * n], refs[2 * n + 1], refs[-1]
        x, y, c, me, sib, chips, cids = _place()
        for k in range(n):
            for j, chip in enumerate(chips):
                _remote(ins[k].at[cids[j]], lands[k].at[j], send.at[3 * k + j], recv.at[3 * k + j], (*chip, c)).start()
        token[...] = jnp.zeros_like(token)

    lands = [_in_hbm(lax.empty((N_CHIP - 1,) + s.shape[1:], s.dtype)) for s in ss]
    out = pl.pallas_call(
        body, name=name, in_specs=[HBM_SPEC] * (2 * n),
        out_specs=[SEM_SPEC, SEM_SPEC] + [HBM_SPEC] * (2 * n) + [pl.BlockSpec(memory_space=pltpu.VMEM)],
        out_shape=[pltpu.SemaphoreType.DMA((3 * n,)), pltpu.SemaphoreType.DMA((3 * n,))]
        + [pltpu.HBM(b.shape, b.dtype) for b in list(ss) + lands] + [SDS((8, 128), F32)],
        input_output_aliases={k: k + 2 for k in range(2 * n)},
        compiler_params=pltpu.CompilerParams(has_side_effects=DATAFLOW))(*[_in_hbm(s) for s in ss], *lands)
    return out[0], out[1], list(out[2:2 + n]), list(out[2 + n:2 + 2 * n]), out[-1]


def reduce_wait(send, recv, ss, lands, after, *, name):
    n = len(ss)

    def body(*refs):
        ins, land_refs, send_ref, recv_ref = refs[:n], refs[n:2 * n], refs[2 * n], refs[2 * n + 1]
        x, y, c, me, sib, chips, cids = _place()
        for k in range(n):
            for j in range(3):
                _remote(ins[k].at[cids[j]], land_refs[k].at[j], send_ref.at[3 * k + j], recv_ref.at[3 * k + j],
                        sib).wait_send()
                _remote(ins[k].at[cids[j]], land_refs[k].at[j], send_ref.at[3 * k + j], recv_ref.at[3 * k + j],
                        sib).wait_recv()

    out = pl.pallas_call(
        body, name=name, in_specs=[HBM_SPEC] * (2 * n) + [SEM_SPEC, SEM_SPEC, ANY_SPEC], out_specs=[HBM_SPEC] * (2 * n),
        out_shape=[pltpu.HBM(b.shape, b.dtype) for b in list(ss) + list(lands)],
        input_output_aliases={k: k for k in range(2 * n)},
        compiler_params=pltpu.CompilerParams(has_side_effects=DATAFLOW))(*ss, *lands, send, recv, after)
    return list(out[:n]), list(out[n:])


def reduce_sum(owns, lands, place_arr, layer, accs, *, name):
    n = len(owns)
    _, half, cdim = lands[0].shape
    tr = _half_tile(half)
    have = accs[0] is not None

    def body(p_ref, *refs):
        for own_ref, land_ref, o_ref in zip(refs[:n], refs[n:2 * n], refs[-n:]):
            o_ref[...] = (((own_ref[...].astype(F32) + land_ref[0].astype(F32)) + land_ref[1].astype(F32))
                          + land_ref[2].astype(F32))

    in_specs = ([pl.BlockSpec((None, tr, cdim), lambda i, p: (p[0], i, 0))] * n
                + [pl.BlockSpec((N_CHIP - 1, tr, cdim), lambda i, p: (0, i, 0))] * n + ([ANY_SPEC] * n if have else []))
    out_spec = pl.BlockSpec((None, tr, cdim), lambda i, p: (layer, p[1] * (half // tr) + i, 0))
    return list(pl.pallas_call(
        body, name=name,
        grid_spec=pltpu.PrefetchScalarGridSpec(
            num_scalar_prefetch=1, grid=(half // tr,), in_specs=in_specs, out_specs=[out_spec] * n),
        out_shape=[SDS((N_LAYER, 2 * half, cdim), F32)] * n,
        input_output_aliases={1 + 2 * n + i: i for i in range(n)} if have else {},
        compiler_params=_params(("parallel",)))(place_arr, *owns, *lands, *(accs if have else [])))


def reduce_share(fs, *, name):
    n = len(fs)

    def body(*refs):
        outs, (send, recv) = refs[n:2 * n], refs[2 * n:]
        x, y, c, me, sib, chips, cids = _place()
        cps = []
        for k in range(n):
            piece = outs[k].at[pl.ds(0, N_LAYER), _my_rows(outs[k].shape[1], c)]
            cps.append(_remote(piece, piece, send.at[k], recv.at[k], sib))
            cps[-1].start()
        for k in range(n):
            theirs = outs[k].at[pl.ds(0, N_LAYER), _my_rows(outs[k].shape[1], c, mine=False)]
            _remote(theirs, theirs, send.at[k], recv.at[k], sib).wait_recv()
        for cp in cps:
            cp.wait_send()

    return list(pl.pallas_call(
        body, name=name, in_specs=[HBM_SPEC] * n, out_specs=[HBM_SPEC] * n, out_shape=[SDS(f.shape, f.dtype) for f in fs],
        input_output_aliases={k: k for k in range(n)}, scratch_shapes=[pltpu.SemaphoreType.DMA((n,))] * 2)(*fs))


WEIGHTS = ("ffn1_norm", "ffn1_w_gate", "ffn1_w_up", "ffn1_w_down", "mix_norm", "w_in", "lru_conv_w", "lru_conv_b", "lru_w_a",
           "lru_b_a", "lru_w_x", "lru_b_x", "lru_lambda", "attn_sinks", "rel_bias", "dn_conv_w", "dn_a_log", "dn_dt_bias",
           "dn_norm", "w_out", "ffn2_norm", "ffn2_w_gate", "ffn2_w_up", "ffn2_w_down", "ple_norm", "ple_w_gate",
           "ple_w_proj", "final_norm")
CONV_SHARDED = ("lru_conv_w", "dn_conv_w")
FFN_TRANSPOSED = ("ffn1_w_gate", "ffn1_w_up", "ffn2_w_gate", "ffn2_w_up")
SMALL = tuple(k for k in WEIGHTS if k not in SHARDED)


def _pack(arrs):
    blocks = []
    for a in arrs:
        v = a.reshape(-1)
        blocks.append(jnp.pad(v, (0, -v.shape[0] % 1024)).reshape(-1, 128))
    return jnp.concatenate(blocks, axis=0)


def _unpack(buf, shapes):
    out, off = [], 0
    for s in shapes:
        n = int(np.prod(s))
        rows = 8 * -(-n // 1024)
        out.append(buf[off:off + rows].reshape(-1)[:n].reshape(s))
        off += rows
    return out


def kernel(x, p, ffn1_norm, ffn1_w_gate, ffn1_w_up, ffn1_w_down, mix_norm, w_in, lru_conv_w, lru_conv_b, lru_w_a, lru_b_a, lru_w_x, lru_b_x, lru_lambda, attn_sinks, rel_bias, dn_conv_w, dn_a_log, dn_dt_bias, dn_norm, w_out, ffn2_norm, ffn2_w_gate, ffn2_w_up, ffn2_w_down, ple_norm, ple_w_gate, ple_w_proj, final_norm, loss_target, m_ffn1_norm, m_ffn1_w_gate, m_ffn1_w_up, m_ffn1_w_down, m_mix_norm, m_w_in, m_lru_conv_w, m_lru_conv_b, m_lru_w_a, m_lru_b_a, m_lru_w_x, m_lru_b_x, m_lru_lambda, m_attn_sinks, m_rel_bias, m_dn_conv_w, m_dn_a_log, m_dn_dt_bias, m_dn_norm, m_w_out, m_ffn2_norm, m_ffn2_w_gate, m_ffn2_w_up, m_ffn2_w_down, m_ple_norm, m_ple_w_gate, m_ple_w_proj, m_final_norm, v_ffn1_norm, v_ffn1_w_gate, v_ffn1_w_up, v_ffn1_w_down, v_mix_norm, v_w_in, v_lru_conv_w, v_lru_conv_b, v_lru_w_a, v_lru_b_a, v_lru_w_x, v_lru_b_x, v_lru_lambda, v_attn_sinks, v_rel_bias, v_dn_conv_w, v_dn_a_log, v_dn_dt_bias, v_dn_norm, v_w_out, v_ffn2_norm, v_ffn2_w_gate, v_ffn2_w_up, v_ffn2_w_down, v_ple_norm, v_ple_w_gate, v_ple_w_proj, v_final_norm):
    given = dict(locals())
    stored = lambda k, a: jnp.swapaxes(a, 1, 2) if k in FFN_TRANSPOSED else a
    ws = {k: stored(k, given[k]) for k in WEIGHTS}
    ms = {k: stored(k, given["m_" + k]) for k in WEIGHTS}
    vs = {k: stored(k, given["v_" + k]) for k in WEIGHTS}
    nb, seq, d = x.shape
    t = nb * seq
    cx, cy, cc = lax.axis_index("x"), lax.axis_index("y"), lax.axis_index("c")
    chip = 2 * cx + cy

    chip_arr = chip.astype(jnp.int32).reshape(1)
    c_arr = cc.astype(jnp.int32).reshape(1)
    place_arr = jnp.stack([chip, cc]).astype(jnp.int32)
    groups = [(l, part) for l in range(N_LAYER) for part in range(len(WEIGHT_PARTS))]
    placed, started = {}, {}

    def place_group(i, after):
        l, part = groups[i]
        ks = WEIGHT_PARTS[part]
        for run in _same_shape_runs([ws[k] for k in ks]):
            outs = place_layer_shard([ws[ks[j]] for j in run], l, chip_arr, F32 if ks[run[0]] in CONV_SHARDED else BF16,
                                     after, name=f"place_{ks[run[0]]}_{l}")
            placed.update({(l, ks[j]): o for j, o in zip(run, outs)})

    def start_group(i, after):
        l, part = groups[i]
        ks = WEIGHT_PARTS[part]
        n_split = sum(k in SHARDED for k in ks)
        started[i] = (ks, n_split) + gather_start([placed[l, k] for k in ks], n_split, after, name=f"gather_start_{l}_{part}")

    place_group(0, jnp.zeros((8, 128), F32))
    start_group(0, jnp.zeros((8, 128), F32))
    for i in range(1, len(groups)):
        place_group(i, started[0][-1])

    def layer_weights(l, part, h):
        i = groups.index((l, part))
        ks, n_split, send, recv, bufs, _ = started[i]
        bufs = gather_wait(send, recv, bufs, n_split, h, name=f"gather_wait_{l}_{part}")
        tie = jnp.zeros((8, 128), F32)
        for nxt in [j for j in range(i + 1, len(groups)) if j not in started and groups[j][0] == groups[min(i + 1, len(groups) - 1)][0]]:
            start_group(nxt, bufs[0] if nxt == i + 1 else started[nxt - 1][-1])
            tie = started[nxt][-1]
        wl = dict(zip(ks, gather_forward(bufs[:n_split], name=f"gather_forward_{l}_{part}") + bufs[n_split:]))
        for k in ("w_in", "ple_w_proj", "lru_conv_w", "dn_conv_w"):
            if k in wl:
                wl[k] = wl[k].transpose(1, 0, 2).reshape(wl[k].shape[1], -1)
        for k in ("w_out", "ple_w_gate"):
            if k in wl:
                wl[k] = wl[k].reshape(-1, wl[k].shape[-1])
        if "w_in" in wl:
            wl["w_in"] = jnp.pad(wl["w_in"], ((0, 0), (0, D_IN_PAD - D_IN)))
        wl[f"tie{part}"] = tie
        return wl

    pending, finished, tokens = [], {k: None for k in SHARDED}, []

    def finish_reduce(after):
        ks, send, recv, sums, lands, l, part = pending.pop(0)
        sums, lands = reduce_wait(send, recv, sums, lands, after, name=f"reduce_wait_{l}_{part}")
        for run in _same_shape_runs(sums):
            outs = reduce_sum([sums[i] for i in run], [lands[i] for i in run], place_arr, l, [finished[ks[i]] for i in run],
                              name=f"reduce_sum_{ks[run[0]]}_{l}")
            finished.update({ks[i]: o for i, o in zip(run, outs)})

    swapping = []

    def start_reduce(after):
        ks, send, recv, gs, theirs, l, part = swapping.pop(0)
        gs, theirs = exchange_wait(send, recv, gs, theirs, after, name=f"exchange_wait_{l}_{part}")
        sums = [None] * len(ks)
        for run in _same_shape_runs(gs):
            outs = reduce_add([gs[i] for i in run], [theirs[i] for i in run], c_arr, name=f"reduce_add_{ks[run[0]]}_{l}")
            for i, o in zip(run, outs):
                sums[i] = o
        send, recv, sums, lands, token = reduce_start(sums, name=f"reduce_start_{l}_{part}")
        pending.append((ks, send, recv, sums, lands, l, part))
        tokens.append(token)
        return token

    def layer_grads(l, part, g, dh):
        ks = GRAD_PARTS[part]
        send, recv, gs, theirs, token = exchange_start([g[k] for k in ks], name=f"exchange_start_{l}_{part}")
        swapping.append((ks, send, recv, gs, theirs, l, part))
        if len(swapping) > 1:
            token = token + start_reduce(dh)
        while len(pending) > 2:
            finish_reduce(dh)
        return token

    small_w = {k: ws[k] for k in SMALL if k not in CONV_SHARDED}
    bmap = jnp.asarray(_rel_bucket_map())
    loss, gx, grads = local_step(x.reshape(t, d), p.reshape(N_LAYER, t, PLE_DIM), loss_target.reshape(t, d), small_w,
                                 layer_weights, layer_grads, bmap, nb, seq)
    while swapping:
        start_reduce(gx)
    g_out, delta, new_m, new_v = {}, {}, {}, {}

    small_shapes = [grads[k].shape for k in SMALL]
    g_small = dict(zip(SMALL, _unpack(allreduce_small(_pack([grads[k] for k in SMALL]), name="allreduce_small"), small_shapes)))
    for k in CONV_SHARDED:
        width = ws[k].shape[-1]
        g_small[k] = lax.dynamic_slice_in_dim(g_small[k], chip * width, width, axis=2)
    g_out.update(g_small)
    shapes = [ws[k].shape for k in SMALL]
    tie = tokens[-1][0:1, 0:1]
    res = adamw([_pack([ws[k] for k in SMALL]) + tie], *[[_pack([src[k] for k in SMALL])] for src in (g_out, ms, vs)],
                name="adamw_small")
    for dst, r in zip((delta, new_m, new_v), res):
        dst.update(zip(SMALL, _unpack(r[0], shapes)))

    after = res[0][0]
    two_d = lambda a: a.reshape(-1, a.shape[-1])
    for part, ks in enumerate(GRAD_PARTS):
        while pending and pending[0][0] == ks:
            finish_reduce(after)
        g_out.update(zip(ks, reduce_share([finished[k] for k in ks], name=f"reduce_share_{part}")))
        for run in _same_shape_runs([ws[k] for k in ks]):
            names = [ks[i] for i in run]
            res = adamw(*[[two_d(src[k]) for k in names] for src in (ws, g_out, ms, vs)], name=f"adamw_{names[0]}")
            for dst, rs in zip((delta, new_m, new_v), res):
                dst.update({k: r.reshape(ws[k].shape) for k, r in zip(names, rs)})
            after = res[0][0]

    total = lax.psum(loss[0, 0], ("x", "y", "c"))
    return (total, gx.reshape(nb, seq, d), *[stored(k, out[k]) for out in (g_out, delta, new_m, new_v) for k in WEIGHTS])
```

```python
import math

import numpy as np
import jax
import jax.numpy as jnp
from jax import lax
from jax.experimental import pallas as pl
from jax.experimental.pallas import tpu as pltpu

F32 = jnp.float32
BF16 = jnp.bfloat16

EPS = 1e-6
D_MODEL = 1024
D_FF = 2816
N_CHIP = 4
FF_BLK = D_FF // N_CHIP
HEAD = 64
LRU_W = 256
ATT_W = 512
ATT_HEADS = 8
KV_HEADS = 2
ATT_GROUP = 4
BLOCK_Q = 128
DN_HEADS = 4
DN_CHUNK = 64
D_IN = 2312
D_IN_PAD = 2560
PLE_DIM = 256
REL_BUCKETS = 32
LRU_C = 8.0
N_LAYER = 2

ADAM_LR, ADAM_B1, ADAM_B2, ADAM_EPS, ADAM_WD, ADAM_STEP = 0.001, 0.9, 0.999, 1e-08, 0.01, 10

VMEM_LIMIT = 56 << 20
MESH = pl.DeviceIdType.MESH
SDS = jax.ShapeDtypeStruct


def _dot(a, b, ca=1, cb=0, hi=False):
    dims = (((ca,), (cb,)), ((), ()))
    one = lambda u, v: lax.dot_general(u, v, dims, preferred_element_type=F32)
    a_hi, b_hi = a.astype(BF16), b.astype(BF16)
    if not hi:
        return one(a_hi, b_hi)
    a_lo = (a - a_hi.astype(F32)).astype(BF16)
    b_lo = (b - b_hi.astype(F32)).astype(BF16)
    return one(a_hi, b_hi) + (one(a_hi, b_lo) + one(a_lo, b_hi))


def _nn(a, b, hi=False):
    return _dot(a, b, 1, 0, hi)


def _nt(a, b, hi=False):
    return _dot(a, b, 1, 1, hi)


def _tn(a, b, hi=False):
    return _dot(a, b, 0, 0, hi)


def _sigmoid(x):
    return jax.nn.sigmoid(x)


def _softplus(x):
    return jnp.maximum(x, 0.0) + jnp.log1p(jnp.exp(-jnp.abs(x)))


def _neg_expm1(z):
    series = -z * (1.0 + z * (0.5 + z * (1.0 / 6.0 + z * (1.0 / 24.0 + z * (1.0 / 120.0)))))
    return jnp.where(z > -0.05, series, 1.0 - jnp.exp(z))


_GELU_C = math.sqrt(2.0 / math.pi)


def _gelu(x):
    t = jnp.tanh(_GELU_C * (x + 0.044715 * x * x * x))
    return 0.5 * x * (1.0 + t), t


def _gelu_grad(x, t):
    return 0.5 * (1.0 + t) + 0.5 * x * (1.0 - t * t) * _GELU_C * (1.0 + 3.0 * 0.044715 * x * x)


def _rms_fwd(h, g):
    r = lax.rsqrt(jnp.mean(h * h, axis=-1, keepdims=True) + EPS)
    xh = h * r
    return xh * g, xh, r


def _rms_bwd(dn, xh, r, g):
    dxh = dn * g
    dh = r * (dxh - xh * jnp.mean(dxh * xh, axis=-1, keepdims=True))
    return dh, jnp.sum(dn * xh, axis=0, keepdims=True)


def _shift_down(x, d, fill=0.0):
    row = lax.broadcasted_iota(jnp.int32, x.shape, 0)
    return jnp.where(row >= d, pltpu.roll(x, d, 0), fill)


def _shift_up(x, d, fill=0.0):
    n = x.shape[0]
    row = lax.broadcasted_iota(jnp.int32, x.shape, 0)
    return jnp.where(row < n - d, pltpu.roll(x, n - d, 0), fill)


def _conv_fwd(x, w):
    y = x * w[3]
    for k in range(3):
        y = y + _shift_down(x, 3 - k) * w[k]
    return y


def _conv_bwd(dy, x, w):
    dx = dy * w[3]
    rows = [None] * 4
    rows[3] = jnp.sum(dy * x, axis=0, keepdims=True)
    for k in range(3):
        dx = dx + _shift_up(dy, 3 - k) * w[k]
        rows[k] = jnp.sum(dy * _shift_down(x, 3 - k), axis=0, keepdims=True)
    r4 = lax.broadcasted_iota(jnp.int32, (4, x.shape[1]), 0)
    dw = jnp.zeros((4, x.shape[1]), F32)
    for k in range(4):
        dw = jnp.where(r4 == k, rows[k], dw)
    return dx, dw


FFN_SPLIT = 2


def _interleave(gens):
    pending = list(gens)
    while pending:
        for g in list(pending):
            if next(g, StopIteration) is StopIteration:
                pending.remove(g)


def _params(sem=None, vmem=VMEM_LIMIT):
    return pltpu.CompilerParams(dimension_semantics=sem, vmem_limit_bytes=vmem)


def _whole(shape):
    nd = len(shape)
    return pl.BlockSpec(shape, lambda *_: (0,) * nd)


def matmul(a, b, *, name, ta=False, tb=False, residual=None, out_dtype=F32, tm=512, tn=512, tk=512):
    m, k = (a.shape[1], a.shape[0]) if ta else a.shape
    n = b.shape[0] if tb else b.shape[1]
    tm, tn, tk = min(tm, m), min(tn, n), min(tk, k)
    assert m % tm == 0 and n % tn == 0 and k % tk == 0, (m, n, k, tm, tn, tk)
    nk = k // tk

    def body(*refs):
        if residual is None:
            a_ref, b_ref, o_ref, acc = refs
        else:
            a_ref, b_ref, r_ref, o_ref, acc = refs
        kk = pl.program_id(2)

        @pl.when(kk == 0)
        def _():
            acc[...] = jnp.zeros_like(acc)

        acc[...] += _dot(a_ref[...], b_ref[...], 0 if ta else 1, 1 if tb else 0)

        @pl.when(kk == nk - 1)
        def _():
            out = acc[...]
            if residual is not None:
                out = out + r_ref[...]
            o_ref[...] = out.astype(out_dtype)

    a_spec = pl.BlockSpec((tk, tm), lambda i, j, kk: (kk, i)) if ta else pl.BlockSpec((tm, tk), lambda i, j, kk: (i, kk))
    b_spec = pl.BlockSpec((tn, tk), lambda i, j, kk: (j, kk)) if tb else pl.BlockSpec((tk, tn), lambda i, j, kk: (kk, j))
    o_spec = pl.BlockSpec((tm, tn), lambda i, j, kk: (i, j))
    in_specs, args = [a_spec, b_spec], [a, b]
    if residual is not None:
        in_specs.append(o_spec)
        args.append(residual)
    return pl.pallas_call(
        body, name=name, grid=(m // tm, n // tn, nk), in_specs=in_specs, out_specs=o_spec,
        out_shape=SDS((m, n), out_dtype), scratch_shapes=[pltpu.VMEM((tm, tn), F32)],
        compiler_params=_params(("parallel", "parallel", "arbitrary")))(*args)


def norm_matmul(h, gain, w, *, name, tm=512, tn=512):
    t, d = h.shape
    tm = min(tm, t)
    n = w.shape[1]
    assert t % tm == 0 and n % tn == 0

    def body(h_ref, g_ref, w_ref, u_ref, n_ref):
        @pl.when(pl.program_id(1) == 0)
        def _():
            n_ref[...] = _rms_fwd(h_ref[...], g_ref[...])[0].astype(BF16)

        u_ref[...] = _nn(n_ref[...], w_ref[...])

    return pl.pallas_call(
        body, name=name, grid=(t // tm, n // tn),
        in_specs=[pl.BlockSpec((tm, d), lambda i, j: (i, 0)), _whole((1, d)), pl.BlockSpec((d, tn), lambda i, j: (0, j))],
        out_specs=[pl.BlockSpec((tm, tn), lambda i, j: (i, j)), pl.BlockSpec((tm, d), lambda i, j: (i, 0))],
        out_shape=[SDS((t, n), F32), SDS((t, d), BF16)],
        compiler_params=_params(("parallel", "arbitrary")))(h, gain, w)


def ffn_fwd(h, gain, wg, wu, wd, *, name, tm=1024):
    t, d = h.shape
    tm = min(tm, t)

    def body(h_ref, g_ref, wg_ref, wu_ref, wd_ref, o_ref, n_ref, a_ref, b_ref, acc):
        j = pl.program_id(1)

        @pl.when(j == 0)
        def _():
            n_ref[...] = _rms_fwd(h_ref[...], g_ref[...])[0].astype(BF16)
            acc[...] = jnp.zeros_like(acc)

        def part(rows):
            n = n_ref[rows, :]
            a = _nt(n, wg_ref[...])
            b = _nt(n, wu_ref[...])
            yield
            a_ref[rows, :] = a.astype(BF16)
            b_ref[rows, :] = b.astype(BF16)
            acc[rows, :] += _nn(a * _sigmoid(a) * b, wd_ref[...])

        _interleave([part(pl.ds(k * (tm // FFN_SPLIT), tm // FFN_SPLIT)) for k in range(FFN_SPLIT)])

        @pl.when(j == N_CHIP - 1)
        def _():
            o_ref[...] = h_ref[...] + 0.5 * acc[...]

    row = pl.BlockSpec((tm, d), lambda i, j: (i, 0))
    blk = pl.BlockSpec((None, tm, FF_BLK), lambda i, j: (j, i, 0))
    wspec = pl.BlockSpec((None, FF_BLK, d), lambda i, j: (j, 0, 0))
    act = SDS((N_CHIP, t, FF_BLK), BF16)
    return pl.pallas_call(
        body, name=name, grid=(t // tm, N_CHIP), in_specs=[row, _whole((1, d)), wspec, wspec, wspec],
        out_specs=[row, row, blk, blk], out_shape=[SDS((t, d), F32), SDS((t, d), BF16), act, act],
        scratch_shapes=[pltpu.VMEM((tm, d), F32)],
        compiler_params=_params(("parallel", "arbitrary")))(h, gain, wg, wu, wd)


def ffn_bwd_act(h, gain, dout, a, b, wg, wu, wd, *, name, tm=512):
    t, d = h.shape
    tm = min(tm, t)

    def body(h_ref, g_ref, do_ref, a_ref, b_ref, wg_ref, wu_ref, wd_ref, dh_ref, da_ref, db_ref, s_ref, dg_ref, dn_acc):
        i, j = pl.program_id(0), pl.program_id(1)

        @pl.when((i == 0) & (j == 0))
        def _():
            dg_ref[...] = jnp.zeros_like(dg_ref)

        @pl.when(j == 0)
        def _():
            dn_acc[...] = jnp.zeros_like(dn_acc)

        def part(rows):
            ds = _nt(0.5 * do_ref[rows, :], wd_ref[...])
            yield
            a = a_ref[rows, :].astype(F32)
            b = b_ref[rows, :].astype(F32)
            sig = _sigmoid(a)
            sa = a * sig
            db = ds * sa
            da = ds * b * (sig * (1.0 + a * (1.0 - sig)))
            s_ref[rows, :] = (sa * b).astype(BF16)
            da_ref[rows, :] = da.astype(BF16)
            db_ref[rows, :] = db.astype(BF16)
            yield
            dn_acc[rows, :] += _nn(da, wg_ref[...]) + _nn(db, wu_ref[...])

        _interleave([part(pl.ds(k * (tm // FFN_SPLIT), tm // FFN_SPLIT)) for k in range(FFN_SPLIT)])

        @pl.when(j == N_CHIP - 1)
        def _():
            g = g_ref[...]
            _, xh, r = _rms_fwd(h_ref[...], g)
            dh, dg = _rms_bwd(dn_acc[...], xh, r, g)
            dh_ref[...] = do_ref[...] + dh
            dg_ref[...] += dg

    row = pl.BlockSpec((tm, d), lambda i, j: (i, 0))
    blk = pl.BlockSpec((None, tm, FF_BLK), lambda i, j: (j, i, 0))
    wspec = pl.BlockSpec((None, FF_BLK, d), lambda i, j: (j, 0, 0))
    act = SDS((N_CHIP, t, FF_BLK), BF16)
    return pl.pallas_call(
        body, name=name, grid=(t // tm, N_CHIP), in_specs=[row, _whole((1, d)), row, blk, blk, wspec, wspec, wspec],
        out_specs=[row, blk, blk, blk, _whole((1, d))],
        out_shape=[SDS((t, d), F32), act, act, act, SDS((1, d), F32)],
        scratch_shapes=[pltpu.VMEM((tm, d), F32)],
        compiler_params=_params(("arbitrary", "arbitrary")))(h, gain, dout, a, b, wg, wu, wd)


def ffn_bwd_w(n, da, db, s, dout, *, name, tk=1024):
    t, d = n.shape
    tk = min(tk, t)

    def body(n_ref, da_ref, db_ref, s_ref, do_ref, dwg_ref, dwu_ref, dwd_ref):
        @pl.when(pl.program_id(1) == 0)
        def _():
            dwg_ref[...] = jnp.zeros_like(dwg_ref)
            dwu_ref[...] = jnp.zeros_like(dwu_ref)
            dwd_ref[...] = jnp.zeros_like(dwd_ref)

        nn = n_ref[...]
        dwg_ref[...] += _tn(da_ref[...], nn)
        dwu_ref[...] += _tn(db_ref[...], nn)
        dwd_ref[...] += _tn(s_ref[...], 0.5 * do_ref[...])

    row = pl.BlockSpec((tk, d), lambda j, kk: (kk, 0))
    blk = pl.BlockSpec((None, tk, FF_BLK), lambda j, kk: (j, kk, 0))
    return pl.pallas_call(
        body, name=name, grid=(N_CHIP, t // tk), in_specs=[row, blk, blk, blk, row],
        out_specs=[pl.BlockSpec((None, FF_BLK, d), lambda j, kk: (j, 0, 0)),
                   pl.BlockSpec((None, FF_BLK, d), lambda j, kk: (j, 0, 0)),
                   pl.BlockSpec((None, FF_BLK, d), lambda j, kk: (j, 0, 0))],
        out_shape=[SDS((N_CHIP, FF_BLK, d), F32)] * 3,
        compiler_params=_params(("parallel", "arbitrary")))(n, da, db, s, dout)


def ple_fwd(h, gain, wpg, pl_in, wpp, *, name, tm=512):
    t, d = h.shape
    tm = min(tm, t)
    pd = pl_in.shape[1]

    def body(h_ref, g_ref, wpg_ref, p_ref, wpp_ref, o_ref):
        hh = h_ref[...]
        n = _rms_fwd(hh, g_ref[...])[0]
        gate = _sigmoid(_nn(n, wpg_ref[...]))
        o_ref[...] = hh + gate * _nn(p_ref[...], wpp_ref[...])

    row = pl.BlockSpec((tm, d), lambda i: (i, 0))
    return pl.pallas_call(
        body, name=name, grid=(t // tm,),
        in_specs=[row, _whole((1, d)), _whole((d, d)), pl.BlockSpec((tm, pd), lambda i: (i, 0)), _whole((pd, d))],
        out_specs=row, out_shape=SDS((t, d), F32), compiler_params=_params(("parallel",)))(h, gain, wpg, pl_in, wpp)


def ple_bwd(h, gain, wpg, pl_in, wpp, dout, *, name, tm=512):
    t, d = h.shape
    tm = min(tm, t)
    pd = pl_in.shape[1]

    def body(h_ref, g_ref, wpg_ref, p_ref, wpp_ref, do_ref, dh_ref, n_ref, dga_ref, dpp_ref, dg_ref):
        @pl.when(pl.program_id(0) == 0)
        def _():
            dg_ref[...] = jnp.zeros_like(dg_ref)

        g = g_ref[...]
        n, xh, r = _rms_fwd(h_ref[...], g)
        gate = _sigmoid(_nn(n, wpg_ref[...]))
        pp = _nn(p_ref[...], wpp_ref[...])
        do = do_ref[...]
        dga = do * pp * gate * (1.0 - gate)
        dh, dg = _rms_bwd(_nt(dga, wpg_ref[...]), xh, r, g)
        dh_ref[...] = do + dh
        n_ref[...] = n.astype(BF16)
        dga_ref[...] = dga.astype(BF16)
        dpp_ref[...] = (do * gate).astype(BF16)
        dg_ref[...] += dg

    row = pl.BlockSpec((tm, d), lambda i: (i, 0))
    return pl.pallas_call(
        body, name=name, grid=(t // tm,),
        in_specs=[row, _whole((1, d)), _whole((d, d)), pl.BlockSpec((tm, pd), lambda i: (i, 0)), _whole((pd, d)), row],
        out_specs=[row, row, row, row, _whole((1, d))],
        out_shape=[SDS((t, d), F32), SDS((t, d), BF16), SDS((t, d), BF16), SDS((t, d), BF16), SDS((1, d), F32)],
        compiler_params=_params(("arbitrary",)))(h, gain, wpg, pl_in, wpp, dout)


def loss_head(h, gain, target, *, name, tm=512):
    t, d = h.shape
    tm = min(tm, t)

    def body(h_ref, g_ref, t_ref, dh_ref, dg_ref, l_ref):
        @pl.when(pl.program_id(0) == 0)
        def _():
            dg_ref[...] = jnp.zeros_like(dg_ref)
            l_ref[...] = jnp.zeros_like(l_ref)

        g = g_ref[...]
        y, xh, r = _rms_fwd(h_ref[...], g)
        err = y - t_ref[...]
        l_ref[...] += 0.5 * jnp.sum(jnp.mean(err * err, axis=-1, keepdims=True), axis=0, keepdims=True)
        dh, dg = _rms_bwd(err * (1.0 / d), xh, r, g)
        dh_ref[...] = dh
        dg_ref[...] += dg

    row = pl.BlockSpec((tm, d), lambda i: (i, 0))
    return pl.pallas_call(
        body, name=name, grid=(t // tm,), in_specs=[row, _whole((1, d)), row],
        out_specs=[row, _whole((1, d)), _whole((1, 1))],
        out_shape=[SDS((t, d), F32), SDS((1, d), F32), SDS((1, 1), F32)],
        compiler_params=_params(("arbitrary",)))(h, gain, target)


def adamw(ws, gs, ms, vs, *, name):
    n = len(ws)
    r, c = ws[0].shape
    budget = (24 << 20) // (2 * 7 * n * c * 4)
    tr = next((cand for cand in (704, 512, 352, 256, 176, 128, 64, 32, 16, 8) if r % cand == 0 and cand <= budget), r)

    def body(*refs):
        for w_ref, g_ref, m_ref, v_ref, d_ref, nm_ref, nv_ref in zip(*[refs[i * n:(i + 1) * n] for i in range(7)]):
            gg = g_ref[...]
            mm = ADAM_B1 * m_ref[...] + (1.0 - ADAM_B1) * gg
            vv = ADAM_B2 * v_ref[...] + (1.0 - ADAM_B2) * (gg * gg)
            m_hat = mm / (1.0 - ADAM_B1 ** ADAM_STEP)
            v_hat = vv / (1.0 - ADAM_B2 ** ADAM_STEP)
            d_ref[...] = -ADAM_LR * (m_hat / (jnp.sqrt(v_hat) + ADAM_EPS) + ADAM_WD * w_ref[...])
            nm_ref[...] = mm
            nv_ref[...] = vv

    blk = pl.BlockSpec((tr, c), lambda i: (i, 0))
    out = pl.pallas_call(body, name=name, grid=(r // tr,), in_specs=[blk] * (4 * n), out_specs=[blk] * (3 * n),
                         out_shape=[SDS((r, c), F32)] * (3 * n), compiler_params=_params(("parallel",)))(*ws, *gs, *ms, *vs)
    return list(out[:n]), list(out[n:2 * n]), list(out[2 * n:])


def _scan_fwd(a, b):
    d = 1
    while d < a.shape[0]:
        b = a * _shift_down(b, d, 0.0) + b
        a = a * _shift_down(a, d, 1.0)
        d *= 2
    return b


def _scan_rev(a, b):
    d = 1
    while d < a.shape[0]:
        b = a * _shift_up(b, d, 0.0) + b
        a = a * _shift_up(a, d, 1.0)
        d *= 2
    return b


LRU_HALF = 128


def _lru_in_specs(seq):
    half = LRU_W // LRU_HALF
    vec = pl.BlockSpec((1, LRU_HALF), lambda j, b: (0, j))
    mat = pl.BlockSpec((LRU_HALF, LRU_HALF), lambda j, b: (j, j))
    return [pl.BlockSpec((seq, LRU_HALF), lambda j, b: (b, j)), pl.BlockSpec((seq, LRU_HALF), lambda j, b: (b, half + j)),
            pl.BlockSpec((4, LRU_HALF), lambda j, b: (0, j)), vec, mat, vec, mat, vec, vec]


def _lru_math(x_ref, gate_ref, cw_ref, cb_ref, wa_ref, ba_ref, wx_ref, bx_ref, lam_ref):
    x = x_ref[...]
    gate = gate_ref[...]
    cw =[cw_ref[k:k + 1, :] for k in range(4)]
    xr = _conv_fwd(x, cw) + cb_ref[...]
    r = _sigmoid(_nn(xr, wa_ref[...]) + ba_ref[...])
    i = _sigmoid(_nn(xr, wx_ref[...]) + bx_ref[...])
    sp = _softplus(-lam_ref[...])
    log_a = -LRU_C * r * sp
    a = jnp.exp(log_a)
    mult = jnp.sqrt(_neg_expm1(2.0 * log_a))
    gi = i * xr
    h = _scan_fwd(a, mult * gi)
    gl, tg = _gelu(gate)
    return dict(x=x, gate=gate, cw=cw, xr=xr, r=r, i=i, sp=sp, a=a, mult=mult, gi=gi, h=h, gl=gl, tg=tg)


def lru_fwd(u, cw, cb, wa, ba, wx, bx, lam, *, seq, name):
    t = u.shape[0]

    def body(x_ref, gate_ref, cw_ref, cb_ref, wa_ref, ba_ref, wx_ref, bx_ref, lam_ref, y_ref):
        f = _lru_math(x_ref, gate_ref, cw_ref, cb_ref, wa_ref, ba_ref, wx_ref, bx_ref, lam_ref)
        y_ref[...] = f["gl"] * f["h"]

    return pl.pallas_call(
        body, name=name, grid=(LRU_W // LRU_HALF, t // seq), in_specs=_lru_in_specs(seq),
        out_specs=pl.BlockSpec((seq, LRU_HALF), lambda j, b: (b, j)), out_shape=SDS((t, LRU_W), F32),
        compiler_params=_params(("parallel", "parallel")))(u, u, cw, cb, wa, ba, wx, bx, lam)


def lru_bwd(u, cw, cb, wa, ba, wx, bx, lam, dy, *, seq, name):
    t = u.shape[0]

    def body(x_ref, gate_ref, cw_ref, cb_ref, wa_ref, ba_ref, wx_ref, bx_ref, lam_ref, dy_ref,
             dx_ref, dgate_ref, dcw_ref, dwa_ref, dwx_ref, dv_ref):
        @pl.when(pl.program_id(1) == 0)
        def _():
            dcw_ref[...] = jnp.zeros_like(dcw_ref)
            dwa_ref[...] = jnp.zeros_like(dwa_ref)
            dwx_ref[...] = jnp.zeros_like(dwx_ref)
            dv_ref[...] = jnp.zeros_like(dv_ref)

        f = _lru_math(x_ref, gate_ref, cw_ref, cb_ref, wa_ref, ba_ref, wx_ref, bx_ref, lam_ref)
        dy = dy_ref[...]
        a, h, xr, r, i, mult, gi, sp = f["a"], f["h"], f["xr"], f["r"], f["i"], f["mult"], f["gi"], f["sp"]
        dgate_ref[...] = dy * h * _gelu_grad(f["gate"], f["tg"])
        lamb = _scan_rev(_shift_up(a, 1, 0.0), dy * f["gl"])
        da = lamb * _shift_down(h, 1)
        dlog_a = da * a - (lamb * gi) * (a * a) / mult
        dgi = lamb * mult
        dra = dlog_a * (-LRU_C * sp) * r * (1.0 - r)
        dia = dgi * xr * i * (1.0 - i)
        dsp = jnp.sum(dlog_a * (-LRU_C * r), axis=0, keepdims=True)
        dlam = -dsp * _sigmoid(-lam_ref[...])
        dxr = dgi * i + _nt(dra, wa_ref[...]) + _nt(dia, wx_ref[...])
        dx, dcw = _conv_bwd(dxr, f["x"], f["cw"])
        dx_ref[...] = dx
        dcw_ref[...] += dcw
        dwa_ref[...] += _tn(xr, dra)
        dwx_ref[...] += _tn(xr, dia)
        rows = [jnp.sum(dxr, axis=0, keepdims=True), jnp.sum(dra, axis=0, keepdims=True),
                jnp.sum(dia, axis=0, keepdims=True), dlam]
        r8 = lax.broadcasted_iota(jnp.int32, (8, LRU_HALF), 0)
        acc = jnp.zeros((8, LRU_HALF), F32)
        for k, row in enumerate(rows):
            acc = jnp.where(r8 == k, row, acc)
        dv_ref[...] += acc

    nhalf = LRU_W // LRU_HALF
    col = pl.BlockSpec((seq, LRU_HALF), lambda j, b: (b, j))
    mat = pl.BlockSpec((None, LRU_HALF, LRU_HALF), lambda j, b: (j, 0, 0))
    return pl.pallas_call(
        body, name=name, grid=(nhalf, t // seq), in_specs=_lru_in_specs(seq) + [col],
        out_specs=[col, col, pl.BlockSpec((4, LRU_HALF), lambda j, b: (0, j)), mat, mat,
                   pl.BlockSpec((8, LRU_HALF), lambda j, b: (0, j))],
        out_shape=[SDS((t, LRU_W), F32), SDS((t, LRU_W), F32), SDS((4, LRU_W), F32),
                   SDS((nhalf, LRU_HALF, LRU_HALF), F32), SDS((nhalf, LRU_HALF, LRU_HALF), F32), SDS((8, LRU_W), F32)],
        compiler_params=_params(("arbitrary", "arbitrary")))(u, u, cw, cb, wa, ba, wx, bx, lam, dy)


NEG = -1e30


def _rel_bucket_map():
    dist = (np.arange(BLOCK_Q)[:, None] - np.arange(BLOCK_Q)[None, :]) % BLOCK_Q
    max_exact = REL_BUCKETS // 2
    large = max_exact + (np.log(np.maximum(dist, 1).astype(np.float32) / max_exact)
                         / math.log(BLOCK_Q / max_exact) * (REL_BUCKETS - max_exact)).astype(np.int32)
    large = np.minimum(large, REL_BUCKETS - 1)
    return np.where(dist < max_exact, dist, large).astype(np.int32)


def relbias_fwd(rel_bias, bmap, *, name):
    def body(rb_ref, bm_ref, o_ref):
        bm = bm_ref[...]
        for h in range(ATT_HEADS):
            acc = jnp.zeros((BLOCK_Q, BLOCK_Q), F32)
            for b in range(REL_BUCKETS):
                acc = jnp.where(bm == b, rb_ref[b, h], acc)
            o_ref[h] = acc

    return pl.pallas_call(
        body, name=name, in_specs=[pl.BlockSpec(memory_space=pltpu.SMEM), pl.BlockSpec(memory_space=pltpu.VMEM)],
        out_specs=pl.BlockSpec(memory_space=pltpu.VMEM), out_shape=SDS((ATT_HEADS, BLOCK_Q, BLOCK_Q), F32))(rel_bias, bmap)


def relbias_bwd(dbias, bmap, *, name):
    def body(db_ref, bm_ref, o_ref):
        bm = bm_ref[...]
        row = lax.broadcasted_iota(jnp.int32, (REL_BUCKETS, 128), 0)
        col = lax.broadcasted_iota(jnp.int32, (REL_BUCKETS, 128), 1)
        acc = jnp.zeros((REL_BUCKETS, 128), F32)
        for h in range(ATT_HEADS):
            d = db_ref[h]
            for b in range(REL_BUCKETS):
                s = jnp.sum(jnp.sum(jnp.where(bm == b, d, 0.0), axis=1, keepdims=True), axis=0, keepdims=True)
                acc = jnp.where((row == b) & (col == h), s, acc)
        o_ref[...] = acc

    return pl.pallas_call(body, name=name, out_shape=SDS((REL_BUCKETS, 128), F32))(dbias, bmap)


def _iota2(shape, axis):
    return lax.broadcasted_iota(jnp.int32, shape, axis)


def _chunk_cumsum(x):
    pos = _iota2(x.shape, 0) & (DN_CHUNK - 1)
    d = 1
    while d < DN_CHUNK:
        x = x + jnp.where(pos >= d, pltpu.roll(x, d, 0), 0.0)
        d *= 2
    return x


def _chunk_rev_cumsum(x):
    n = x.shape[0]
    pos = _iota2(x.shape, 0) & (DN_CHUNK - 1)
    d = 1
    while d < DN_CHUNK:
        x = x + jnp.where(pos < DN_CHUNK - d, pltpu.roll(x, n - d, 0), 0.0)
        d *= 2
    return x


_DN_SCALE = (HEAD ** -0.5, 1.0, None)
DN_UNROLL = 8


COL_Q, COL_K, COL_V = 512 // 128, 1024 // 128, 1152 // 128
COL_DNQ, COL_DNK, COL_DNV, COL_DNZ, COL_BA = 1280 // 128, 1536 // 128, 1792 // 128, 2048 // 128, 2304 // 128


def _lane_a(shape):
    return _iota2(shape, 1) < HEAD


def _bd(x):
    la = _lane_a(x.shape)
    return jnp.concatenate([jnp.where(la, x, 0.0), jnp.where(la, 0.0, x)], axis=0)


def _fold(m):
    return m[:HEAD] + m[HEAD:]


def _bd_mask():
    return (_iota2((2 * HEAD, 2 * HEAD), 0) < HEAD) == (_iota2((2 * HEAD, 2 * HEAD), 1) < HEAD)


def _pk_nn(x, y, hi=False):
    return _nn(x, _bd(y), hi)


def _pk_nt(u, v, hi=False):
    return _nt(u, _bd(v), hi)


def _pk_tn(x, y, hi=False):
    return _fold(jnp.where(_bd_mask(), _tn(x, y, hi), 0.0))


def _half_sum(x):
    la = _lane_a(x.shape)
    return jnp.where(la, jnp.sum(jnp.where(la, x, 0.0), axis=-1, keepdims=True),
                     jnp.sum(jnp.where(la, 0.0, x), axis=-1, keepdims=True))


def _lane_col(x, idx):
    return jnp.sum(jnp.where(_iota2(x.shape, 1) == idx, x, 0.0), axis=-1, keepdims=True)


def _row0(x):
    return jnp.max(x, axis=0, keepdims=True)


def _dup_kv(x, g):
    la = _lane_a(x.shape)
    rolled = pltpu.roll(x, HEAD, 1)
    return jnp.where(la, x, rolled) if g == 0 else jnp.where(la, rolled, x)


def _stack_heads(ref, g):
    la = _lane_a((BLOCK_Q, 2 * HEAD))
    parts = []
    for hh in range(ATT_GROUP):
        pair = ref[:, pl.ds(2 * HEAD * (2 * g + hh // 2), 2 * HEAD)]
        parts.append(jnp.where(la if hh % 2 == 0 else ~la, pair, 0.0))
    return jnp.concatenate(parts, axis=0)


def _unstack_heads(stack, ref, g):
    la = _lane_a((BLOCK_Q, 2 * HEAD))
    for j in range(2):
        top = stack[2 * j * BLOCK_Q:(2 * j + 1) * BLOCK_Q]
        bot = stack[(2 * j + 1) * BLOCK_Q:(2 * j + 2) * BLOCK_Q]
        ref[:, pl.ds(2 * HEAD * (2 * g + j), 2 * HEAD)] = jnp.where(la, top, bot)


def _swa_probs(q_ref, k_ref, v_ref, b_ref, s_ref, n, g):
    rows = ATT_GROUP * BLOCK_Q
    prev = pl.multiple_of(jnp.maximum(n - 1, 0) * BLOCK_Q, BLOCK_Q)
    cur = pl.multiple_of(n * BLOCK_Q, BLOCK_Q)
    kp, kc = _dup_kv(k_ref[pl.ds(prev, BLOCK_Q), :], g), _dup_kv(k_ref[pl.ds(cur, BLOCK_Q), :], g)
    vp, vc = _dup_kv(v_ref[pl.ds(prev, BLOCK_Q), :], g), _dup_kv(v_ref[pl.ds(cur, BLOCK_Q), :], g)
    qs = _stack_heads(q_ref, g) * (HEAD ** -0.5)
    bias = b_ref[pl.ds(ATT_GROUP * g, ATT_GROUP)].reshape(rows, BLOCK_Q)
    i = _iota2((rows, BLOCK_Q), 0) & (BLOCK_Q - 1)
    j = _iota2((rows, BLOCK_Q), 1)
    s_p = jnp.where((j > i) & (n > 0), _nt(qs, kp) + bias, NEG)
    s_c = jnp.where(j <= i, _nt(qs, kc) + bias, NEG)
    sink = s_ref[pl.ds(rows * g, rows), :]
    m = jnp.maximum(jnp.maximum(jnp.max(s_p, axis=-1, keepdims=True), jnp.max(s_c, axis=-1, keepdims=True)), sink)
    e_p, e_c, e_s = jnp.exp(s_p - m), jnp.exp(s_c - m), jnp.exp(sink - m)
    inv = 1.0 / (jnp.sum(e_p, axis=-1, keepdims=True) + jnp.sum(e_c, axis=-1, keepdims=True) + e_s)
    return e_p * inv, e_c * inv, e_s * inv, qs, kp, kc, vp, vc, prev, cur


def _swa_specs(seq):
    nblk = seq // BLOCK_Q
    qspec = pl.BlockSpec((BLOCK_Q, ATT_W), lambda b, n: (b * nblk + n, COL_Q * 128 // ATT_W))
    kspec = pl.BlockSpec((seq, 2 * HEAD), lambda b, n: (b, COL_K))
    vspec = pl.BlockSpec((seq, 2 * HEAD), lambda b, n: (b, COL_V))
    ospec = pl.BlockSpec((BLOCK_Q, ATT_W), lambda b, n: (b * nblk + n, 0))
    kvout = pl.BlockSpec((seq, 2 * HEAD), lambda b, n: (b, 0))
    return qspec, kspec, vspec, ospec, kvout, _whole((ATT_HEADS, BLOCK_Q, BLOCK_Q)), _whole((ATT_HEADS * BLOCK_Q, 1))


def swa_fwd(u, bias, sink_rows, *, seq, name):
    t = u.shape[0]

    def body(q_ref, k_ref, v_ref, b_ref, s_ref, o_ref):
        for g in range(KV_HEADS):
            p_p, p_c, _, _, _, _, vp, vc, _, _ = _swa_probs(q_ref, k_ref, v_ref, b_ref, s_ref, pl.program_id(1), g)
            _unstack_heads(_nn(p_p, vp) + _nn(p_c, vc), o_ref, g)

    qspec, kspec, vspec, ospec, kvout, bspec, sspec = _swa_specs(seq)
    return pl.pallas_call(
        body, name=name, grid=(t // seq, seq // BLOCK_Q), in_specs=[qspec, kspec, vspec, bspec, sspec], out_specs=ospec,
        out_shape=SDS((t, ATT_W), F32), compiler_params=_params(("parallel", "arbitrary")))(u, u, u, bias, sink_rows)


def swa_bwd(u, bias, sink_rows, do, *, seq, name):
    t = u.shape[0]

    def body(q_ref, k_ref, v_ref, b_ref, s_ref, do_ref, dq_ref, dk_ref, dv_ref, db_ref, ds_ref):
        b, n = pl.program_id(0), pl.program_id(1)

        @pl.when((b == 0) & (n == 0))
        def _():
            db_ref[...] = jnp.zeros_like(db_ref)
            ds_ref[...] = jnp.zeros_like(ds_ref)

        @pl.when(n == 0)
        def _():
            dk_ref[...] = jnp.zeros_like(dk_ref)
            dv_ref[...] = jnp.zeros_like(dv_ref)

        la = _lane_a((BLOCK_Q, 2 * HEAD))
        for g in range(KV_HEADS):
            p_p, p_c, p_s, qs, kp, kc, vp, vc, prev, cur = _swa_probs(q_ref, k_ref, v_ref, b_ref, s_ref, n, g)
            do = _stack_heads(do_ref, g)
            dp_p, dp_c = _nt(do, vp), _nt(do, vc)
            delta = jnp.sum(p_p * dp_p, axis=-1, keepdims=True) + jnp.sum(p_c * dp_c, axis=-1, keepdims=True)
            ds_p, ds_c = p_p * (dp_p - delta), p_c * (dp_c - delta)
            _unstack_heads((_nn(ds_p, kp) + _nn(ds_c, kc)) * (HEAD ** -0.5), dq_ref, g)
            mine = la if g == 0 else ~la

            def to_head(x):
                return jnp.where(mine, x + pltpu.roll(x, HEAD, 1), 0.0)

            dk_ref[pl.ds(prev, BLOCK_Q), :] += to_head(_tn(ds_p, qs))
            dk_ref[pl.ds(cur, BLOCK_Q), :] += to_head(_tn(ds_c, qs))
            dv_ref[pl.ds(prev, BLOCK_Q), :] += to_head(_tn(p_p, do))
            dv_ref[pl.ds(cur, BLOCK_Q), :] += to_head(_tn(p_c, do))
            db_ref[pl.ds(ATT_GROUP * g, ATT_GROUP)] += (ds_p + ds_c).reshape(ATT_GROUP, BLOCK_Q, BLOCK_Q)
            rows = ATT_GROUP * BLOCK_Q
            ds_ref[pl.ds(rows * g, rows), :] += -p_s * delta

    qspec, kspec, vspec, ospec, kvout, bspec, sspec = _swa_specs(seq)
    return pl.pallas_call(
        body, name=name, grid=(t // seq, seq // BLOCK_Q), in_specs=[qspec, kspec, vspec, bspec, sspec, ospec],
        out_specs=[ospec, kvout, kvout, bspec, sspec],
        out_shape=[SDS((t, ATT_W), F32), SDS((t, 2 * HEAD), F32), SDS((t, 2 * HEAD), F32),
                   SDS((ATT_HEADS, BLOCK_Q, BLOCK_Q), F32), SDS((ATT_HEADS * BLOCK_Q, 1), F32)],
        compiler_params=_params(("arbitrary", "arbitrary")))(u, u, u, bias, sink_rows, do)


def _gdn_gates(ba_ref, alog_ref, dt_ref, hp):
    blk = ba_ref[...]
    beta_blk = _sigmoid(blk)
    sp_arg = blk + dt_ref[...]
    a_exp = jnp.exp(alog_ref[...])
    g_blk = -a_exp * _softplus(sp_arg)
    la = _lane_a(blk.shape)
    ha = 2 * hp
    beta = jnp.where(la, _lane_col(beta_blk, ha), _lane_col(beta_blk, ha + 1))
    g = jnp.where(la, _lane_col(g_blk, DN_HEADS + ha), _lane_col(g_blk, DN_HEADS + ha + 1))
    return beta, g, beta_blk, sp_arg, a_exp, g_blk


def _gdn_act(c, scale):
    sig = _sigmoid(c)
    a = c * sig
    if scale is None:
        return a, sig, None, None
    r = lax.rsqrt(_half_sum(a * a) + EPS)
    return a * r * scale, sig, a * r, r


def _gdn_inputs(pre_refs, cw_refs, ba_ref, alog_ref, dt_ref, hp, act_sc, b_sc, gc_sc, c_sc=None):
    for idx in range(3):
        c = _conv_fwd(pre_refs[idx][...], [cw_refs[idx][k:k + 1, :] for k in range(4)])
        if c_sc is not None:
            c_sc[idx] = c
        act_sc[idx] = _gdn_act(c, _DN_SCALE[idx])[0]
    beta, g = _gdn_gates(ba_ref, alog_ref, dt_ref, hp)[:2]
    b_sc[...] = beta
    gc_sc[...] = _chunk_cumsum(g)


def _gdn_chunk(q, k, v, b, gcc):
    shape = q.shape
    row, lm = _iota2(shape, 0), _iota2(shape, 1) & (HEAD - 1)
    tril, strict, eye = row >= lm, row > lm, row == lm
    eg = jnp.exp(gcc)
    kb, vb = k * b, v * b
    kbg = kb * eg
    grow = jnp.sum(jnp.where(eye, gcc, 0.0), axis=0, keepdims=True)
    dm = jnp.exp(jnp.where(tril, gcc - grow, NEG))
    kk = _pk_nt(kb, k)
    glast = jnp.sum(jnp.where(row == DN_CHUNK - 1, gcc, 0.0), axis=0, keepdims=True)
    ekd = jnp.exp(glast - gcc)
    qk = _pk_nt(q, k)
    return dict(q=q, k=k, v=v, b=b, tril=tril, strict=strict, eye=eye, row=row, eg=eg, kb=kb, vb=vb, kbg=kbg, dm=dm, kk=kk,
                low=jnp.where(strict, kk * dm, 0.0), glast=glast, ekd=ekd, kd=k * ekd, qk=qk,
                amat=jnp.where(tril, qk * dm, 0.0), qg=q * eg, egl=jnp.broadcast_to(jnp.exp(glast), shape))


def _tri_inv_many(chunks):
    ms = [-m["low"] for m in chunks]
    ts = [m["eye"].astype(F32) + x for m, x in zip(chunks, ms)]
    for _ in range(int(math.log2(HEAD)) - 1):
        ms = [_pk_nn(x, x, hi=True) for x in ms]
        ts = [t + _pk_nn(t, x, hi=True) for t, x in zip(ts, ms)]
    return ts


def _gdn_chunk_loop(nc, act_sc, b_sc, gc_sc, finish):
    u = math.gcd(nc, DN_UNROLL)

    def step(i, carry):
        rows = [pl.ds(pl.multiple_of((i * u + j) * DN_CHUNK, DN_CHUNK), DN_CHUNK) for j in range(u)]
        chunks = [_gdn_chunk(act_sc[0, r, :], act_sc[1, r, :], act_sc[2, r, :], b_sc[r, :], gc_sc[r, :]) for r in rows]
        pending = [finish(r, m, t) for r, m, t in zip(rows, chunks, _tri_inv_many(chunks))]
        pending = [g for g in pending if g is not None]
        while pending:
            for g in list(pending):
                if next(g, StopIteration) is StopIteration:
                    pending.remove(g)
        return carry

    lax.fori_loop(0, nc // u, step, 0)


def _gdn_in_specs(seq):
    u_at = lambda col: pl.BlockSpec((seq, 2 * HEAD), lambda b, hp, _c=col: (b, _c + hp))
    cw_at = lambda col: pl.BlockSpec((4, 2 * HEAD), lambda b, hp, _c=col: (0, _c + hp))
    row = pl.BlockSpec((1, 2 * HEAD), lambda b, hp: (0, 0))
    ba = pl.BlockSpec((seq, 2 * HEAD), lambda b, hp: (b, COL_BA))
    return [u_at(COL_DNQ), u_at(COL_DNK), u_at(COL_DNV), ba, cw_at(0), cw_at(2), cw_at(4), row, row]


def _pair(seq, lead=None):
    if lead is None:
        return pl.BlockSpec((seq, 2 * HEAD), lambda b, hp: (b, hp))
    return pl.BlockSpec((lead, seq, 2 * HEAD), lambda b, hp: (0, b, hp))


def _swap(spec):
    return pl.BlockSpec(spec.block_shape, lambda hp, b, _f=spec.index_map: _f(b, hp))


def gdn_prep(u, cw, alog_row, dt_row, *, seq, name):
    t = u.shape[0]
    nc = seq // DN_CHUNK

    def body(q_ref, k_ref, v_ref, ba_ref, cq_ref, ck_ref, cv_ref, alog_ref, dt_ref, loc_ref, egl_ref, act_sc, b_sc, gc_sc):
        _gdn_inputs((q_ref, k_ref, v_ref), (cq_ref, ck_ref, cv_ref), ba_ref, alog_ref, dt_ref, pl.program_id(1),
                    act_sc, b_sc, gc_sc)

        def finish(rows, m, t):
            loc_ref[0, rows, :] = m["qg"]
            loc_ref[1, rows, :] = m["kd"]
            loc_ref[2, rows, :] = _pk_nn(t, m["vb"])
            loc_ref[3, rows, :] = _pk_nn(t, m["kbg"])
            loc_ref[4, rows, :] = m["amat"]
            egl_ref[rows, :] = m["egl"]

        _gdn_chunk_loop(nc, act_sc, b_sc, gc_sc, finish)

    return pl.pallas_call(
        body, name=name, grid=(t // seq, DN_HEADS // 2), in_specs=_gdn_in_specs(seq), out_specs=[_pair(seq, 5), _pair(seq)],
        out_shape=[SDS((5, t, DN_HEADS * HEAD), F32), SDS((t, DN_HEADS * HEAD), F32)],
        scratch_shapes=[pltpu.VMEM((3, seq, 2 * HEAD), F32)] + [pltpu.VMEM((seq, 2 * HEAD), F32)] * 2,
        compiler_params=_params(("parallel", "parallel")))(u, u, u, u, cw, cw, cw, alog_row, dt_row)


def _gated_norm2(o, z, gn):
    r = lax.rsqrt(_half_sum(o * o) * (1.0 / HEAD) + EPS)
    return o * r, _sigmoid(z), r


def gdn_scan(loc, egl, u, gn, *, seq, name):
    t = u.shape[0]
    nc = seq // DN_CHUNK

    npair = DN_HEADS // 2

    def body(loc_ref, egl_ref, z_ref, gn_ref, y_ref, o_ref, vn_ref, st_ref):
        gn = gn_ref[...]
        bdm = _bd_mask()

        def step(c, states):
            rows = pl.ds(pl.multiple_of(c * DN_CHUNK, DN_CHUNK), DN_CHUNK)
            new = [None] * npair

            def pair(hp):
                lanes = pl.ds(hp * 2 * HEAD, 2 * HEAD)
                state = states[hp]
                st_ref[rows, lanes] = _fold(state)
                vn = loc_ref[2, rows, lanes] - _nn(loc_ref[3, rows, lanes], state)
                yield
                o = _nn(loc_ref[0, rows, lanes], state) + _pk_nn(loc_ref[4, rows, lanes], vn)
                new[hp] = state * _row0(egl_ref[rows, lanes]) + jnp.where(bdm, _tn(loc_ref[1, rows, lanes], vn), 0.0)
                yield
                vn_ref[rows, lanes] = vn
                o_ref[rows, lanes] = o
                zz = z_ref[rows, lanes]
                on, sig, _ = _gated_norm2(o, zz, gn)
                y_ref[rows, lanes] = on * gn * (zz * sig)

            _interleave([pair(hp) for hp in range(npair)])
            return tuple(new)

        lax.fori_loop(0, nc, step, tuple(jnp.zeros((2 * HEAD, 2 * HEAD), F32) for _ in range(npair)))

    width = DN_HEADS * HEAD
    rows = pl.BlockSpec((seq, width), lambda b: (b, 0))
    out = SDS((t, width), F32)
    return pl.pallas_call(
        body, name=name, grid=(t // seq,),
        in_specs=[pl.BlockSpec((5, seq, width), lambda b: (0, b, 0)), rows,
                  pl.BlockSpec((seq, width), lambda b: (b, COL_DNZ * 2 * HEAD // width)), _whole((1, 2 * HEAD))],
        out_specs=[rows] * 4, out_shape=[out] * 4, compiler_params=_params(("parallel",)))(loc, egl, u, gn)


def gdn_scan_bwd(loc, egl, u, gn, o, vn, states, dy, *, seq, name):
    t = u.shape[0]
    nc = seq // DN_CHUNK

    npair = DN_HEADS // 2

    def body(loc_ref, egl_ref, z_ref, gn_ref, o_ref, vn_ref, st_ref, dy_ref, dloc_ref, degl_ref, dz_ref, dgn_ref):
        @pl.when(pl.program_id(0) == 0)
        def _():
            dgn_ref[...] = jnp.zeros_like(dgn_ref)

        gn = gn_ref[...]
        bdm = _bd_mask()
        shape = (DN_CHUNK, 2 * HEAD)
        tril = _iota2(shape, 0) >= (_iota2(shape, 1) & (HEAD - 1))

        def step(i, carry):
            rows = pl.ds(pl.multiple_of((nc - 1 - i) * DN_CHUNK, DN_CHUNK), DN_CHUNK)
            new = [None] * npair

            def pair(hp):
                lanes = pl.ds(hp * 2 * HEAD, 2 * HEAD)
                ds, dgn = carry[hp]
                dy, zz, oo = dy_ref[rows, lanes], z_ref[rows, lanes], o_ref[rows, lanes]
                on, sig, r = _gated_norm2(oo, zz, gn)
                sz = zz * sig
                dz_ref[rows, lanes] = dy * on * gn * (sig * (1.0 + zz * (1.0 - sig)))
                dgn = dgn + jnp.sum(dy * on * sz, axis=0, keepdims=True)
                don = dy * gn * sz
                do = r * (don - on * _half_sum(don * on) * (1.0 / HEAD))
                state, vnew = _bd(st_ref[rows, lanes]), vn_ref[rows, lanes]
                qg, kd, w, amat = (loc_ref[0, rows, lanes], loc_ref[1, rows, lanes], loc_ref[3, rows, lanes],
                                   loc_ref[4, rows, lanes])
                yield
                dvn = _pk_tn(amat, do) + _nn(kd, ds)
                dloc_ref[0, rows, lanes] = _nt(do, state)
                dloc_ref[1, rows, lanes] = _nt(vnew, ds)
                yield
                dloc_ref[2, rows, lanes] = dvn
                dloc_ref[3, rows, lanes] = -_nt(dvn, state)
                dloc_ref[4, rows, lanes] = jnp.where(tril, _pk_nt(do, vnew), 0.0)
                degl = _half_sum(jnp.sum(state * ds, axis=0, keepdims=True))
                degl_ref[rows, lanes] = jnp.broadcast_to(degl, shape)
                grow = jnp.where(bdm, _tn(qg, do) - _tn(w, dvn), 0.0)
                new[hp] = (ds * _row0(egl_ref[rows, lanes]) + grow, dgn)

            _interleave([pair(hp) for hp in range(npair)])
            return tuple(new)

        init = tuple((jnp.zeros((2 * HEAD, 2 * HEAD), F32), jnp.zeros((1, 2 * HEAD), F32)) for _ in range(npair))
        out = lax.fori_loop(0, nc, step, init)
        dgn_ref[...] += sum(dgn for _, dgn in out)

    width = DN_HEADS * HEAD
    once = pl.Buffered(1)
    rows = pl.BlockSpec((seq, width), lambda b: (b, 0), pipeline_mode=once)
    out_rows = pl.BlockSpec((seq, width), lambda b: (b, 0))
    out = SDS((t, width), F32)
    return pl.pallas_call(
        body, name=name, grid=(t // seq,),
        in_specs=[pl.BlockSpec((5, seq, width), lambda b: (0, b, 0), pipeline_mode=once), rows,
                  pl.BlockSpec((seq, width), lambda b: (b, COL_DNZ * 2 * HEAD // width), pipeline_mode=once),
                  _whole((1, 2 * HEAD)), rows, rows, rows, rows],
        out_specs=[pl.BlockSpec((5, seq, width), lambda b: (0, b, 0)), out_rows, out_rows, _whole((1, 2 * HEAD))],
        out_shape=[SDS((5, t, width), F32), out, out, SDS((1, 2 * HEAD), F32)],
        compiler_params=_params(("arbitrary",), vmem=60 << 20))(loc, egl, u, gn, o, vn, states, dy)


def gdn_prep_bwd(u, cw, alog_row, dt_row, dloc, degl, *, seq, name):
    t = u.shape[0]
    nc = seq // DN_CHUNK

    def body(q_ref, k_ref, v_ref, ba_ref, cq_ref, ck_ref, cv_ref, alog_ref, dt_ref, dloc_ref, degl_ref,
             dqkv_ref, dba_ref, dcw_ref, dhs_ref, act_sc, b_sc, gc_sc, c_sc):
        hp = pl.program_id(0)

        @pl.when(pl.program_id(1) == 0)
        def _():
            dcw_ref[...] = jnp.zeros_like(dcw_ref)
            dhs_ref[...] = jnp.zeros_like(dhs_ref)

        pre_refs, cw_refs = (q_ref, k_ref, v_ref), (cq_ref, ck_ref, cv_ref)
        _gdn_inputs(pre_refs, cw_refs, ba_ref, alog_ref, dt_ref, hp, act_sc, b_sc, gc_sc, c_sc)

        def finish(rows, m, tt):
            q, k, v, b = m["q"], m["k"], m["v"], m["b"]
            dqg, dkd, du, dw, da = (dloc_ref[x, rows, :] for x in range(5))
            dm, eg = m["dm"], m["eg"]
            dt = _pk_nt(du, m["vb"]) + _pk_nt(dw, m["kbg"])
            dvb, dkbg = _pk_tn(tt, du), _pk_tn(tt, dw)
            yield
            dtt = _pk_nt(dt, tt, hi=True)
            yield
            dl = jnp.where(m["strict"], -_pk_tn(tt, dtt, hi=True), 0.0)
            yield
            dkk = dl * dm
            dqk = da * dm
            dd = dl * m["kk"] + da * m["qk"]
            dkb = _pk_nn(dkk, k) + dkbg * eg
            dq = _pk_nn(dqk, k) + dqg * eg
            yield
            dk = _pk_tn(dkk, m["kb"]) + _pk_tn(dqk, q) + dkd * m["ekd"] + dkb * b
            db = _half_sum(dkb * k + dvb * v)
            yield
            mx = jnp.where(m["tril"], dd * dm, 0.0)
            tk = _half_sum(dkd * m["kd"])
            colsum = jnp.where(m["eye"], jnp.broadcast_to(jnp.sum(mx, axis=0, keepdims=True), mx.shape), 0.0)
            dgc = _half_sum(mx) - _half_sum(colsum) + _half_sum(dqg * m["qg"] + dkbg * m["kbg"]) - tk
            dglast = jnp.sum(tk, axis=0, keepdims=True) + _row0(degl_ref[rows, :]) * jnp.exp(m["glast"])
            act_sc[0, rows, :] = dq
            act_sc[1, rows, :] = dk
            act_sc[2, rows, :] = dvb * b
            b_sc[rows, :] = db
            gc_sc[rows, :] = dgc + jnp.where(m["row"] == DN_CHUNK - 1, dglast, 0.0)

        _gdn_chunk_loop(nc, act_sc, b_sc, gc_sc, finish)

        beta, g, beta_blk, sp_arg, a_exp, g_blk = _gdn_gates(ba_ref, alog_ref, dt_ref, hp)
        dg = _chunk_rev_cumsum(gc_sc[...])
        lane = _iota2(beta_blk.shape, 1)
        ha = 2 * hp
        db = b_sc[...]
        at = lambda idx, x_a, x_b: (jnp.where(lane == idx, _lane_col(x_a, 0), 0.0)
                                    + jnp.where(lane == idx + 1, _lane_col(x_b, HEAD), 0.0))
        dg_blk = at(DN_HEADS + ha, dg, dg)
        dal = dg_blk * (-a_exp) * _sigmoid(sp_arg)
        dba_ref[...] = at(ha, db, db) * beta_blk * (1.0 - beta_blk) + dal
        dhs_ref[0:1, :] += jnp.sum(dg_blk * g_blk, axis=0, keepdims=True)
        dhs_ref[1:2, :] += jnp.sum(dal, axis=0, keepdims=True)
        for idx in range(3):
            c = c_sc[idx]
            _, sig, hat, r = _gdn_act(c, _DN_SCALE[idx])
            da_ = act_sc[idx]
            if _DN_SCALE[idx] is not None:
                da_ = da_ * _DN_SCALE[idx]
                da_ = r * (da_ - hat * _half_sum(da_ * hat))
            dx, dcw = _conv_bwd(da_ * (sig * (1.0 + c * (1.0 - sig))), pre_refs[idx][...],
                                [cw_refs[idx][k:k + 1, :] for k in range(4)])
            dqkv_ref[idx] = dx
            dcw_ref[idx] += dcw

    pair = DN_HEADS // 2
    in_specs = [_swap(s) for s in _gdn_in_specs(seq)] + [_swap(_pair(seq, 5)), _swap(_pair(seq))]
    return pl.pallas_call(
        body, name=name, grid=(pair, t // seq), in_specs=in_specs,
        out_specs=[_swap(_pair(seq, 3)), pl.BlockSpec((None, seq, 2 * HEAD), lambda hp, b: (hp, b, 0)),
                   pl.BlockSpec((3, 4, 2 * HEAD), lambda hp, b: (0, 0, hp)),
                   pl.BlockSpec((None, 2, 2 * HEAD), lambda hp, b: (hp, 0, 0))],
        out_shape=[SDS((3, t, DN_HEADS * HEAD), F32), SDS((pair, t, 2 * HEAD), F32), SDS((3, 4, DN_HEADS * HEAD), F32),
                   SDS((pair, 2, 2 * HEAD), F32)],
        scratch_shapes=[pltpu.VMEM((3, seq, 2 * HEAD), F32)] + [pltpu.VMEM((seq, 2 * HEAD), F32)] * 2
        + [pltpu.VMEM((3, seq, 2 * HEAD), F32)],
        compiler_params=_params(("arbitrary", "arbitrary")))(u, u, u, u, cw, cw, cw, alog_row, dt_row, dloc, degl)


def mix_out(y_lru, o, y_dn, w_out, h, *, name, tm=512):
    t, d = h.shape
    tm = min(tm, t)

    def body(a_ref, b_ref, c_ref, w_ref, h_ref, o_ref, y_ref):
        y_ref[:, 0:LRU_W] = a_ref[...].astype(BF16)
        y_ref[:, LRU_W:LRU_W + ATT_W] = b_ref[...].astype(BF16)
        y_ref[:, LRU_W + ATT_W:] = c_ref[...].astype(BF16)
        o_ref[...] = h_ref[...] + _nn(y_ref[...], w_ref[...])

    rows = lambda width: pl.BlockSpec((tm, width), lambda i: (i, 0))
    return pl.pallas_call(
        body, name=name, grid=(t // tm,), in_specs=[rows(LRU_W), rows(ATT_W), rows(LRU_W), _whole((d, d)), rows(d)],
        out_specs=[rows(d), rows(d)], out_shape=[SDS((t, d), F32), SDS((t, d), BF16)],
        compiler_params=_params(("parallel",)))(y_lru, o, y_dn, w_out, h)


def mix_out_bwd(dout, w_out, *, name, tm=512):
    t, d = dout.shape
    tm = min(tm, t)

    def body(d_ref, w_ref, a_ref, b_ref, c_ref):
        dy = _nt(d_ref[...], w_ref[...])
        a_ref[...] = dy[:, 0:LRU_W]
        b_ref[...] = dy[:, LRU_W:LRU_W + ATT_W]
        c_ref[...] = dy[:, LRU_W + ATT_W:]

    rows = lambda width: pl.BlockSpec((tm, width), lambda i: (i, 0))
    return pl.pallas_call(
        body, name=name, grid=(t // tm,), in_specs=[rows(d), _whole((d, d))], out_specs=[rows(LRU_W), rows(ATT_W), rows(LRU_W)],
        out_shape=[SDS((t, LRU_W), F32), SDS((t, ATT_W), F32), SDS((t, LRU_W), F32)],
        compiler_params=_params(("parallel",)))(dout, w_out)


def mix_in_bwd(h, gain, dout, w_in, dx, dgate, dq, dk, dv, dqkv, dz, dba, *, name, tm=512):
    t, d = h.shape
    tm = min(tm, t)

    def body(h_ref, g_ref, do_ref, w_ref, dx_ref, dgate_ref, dq_ref, dk_ref, dv_ref, dqkv_ref, dz_ref, dba_ref,
             dh_ref, dg_ref, du_ref):
        @pl.when(pl.program_id(0) == 0)
        def _():
            dg_ref[...] = jnp.zeros_like(dg_ref)

        off = 0
        for piece in (dx_ref[...], dgate_ref[...], dq_ref[...], dk_ref[...], dv_ref[...], dqkv_ref[0], dqkv_ref[1],
                      dqkv_ref[2], dz_ref[...], dba_ref[0] + dba_ref[1]):
            du_ref[:, off:off + piece.shape[1]] = piece.astype(BF16)
            off += piece.shape[1]
        du_ref[:, off:] = jnp.zeros((tm, D_IN_PAD - off), BF16)
        g = g_ref[...]
        _, xh, r = _rms_fwd(h_ref[...], g)
        dh, dg = _rms_bwd(_nt(du_ref[...], w_ref[...]), xh, r, g)
        dh_ref[...] = do_ref[...] + dh
        dg_ref[...] += dg

    rows = lambda width: pl.BlockSpec((tm, width), lambda i: (i, 0))
    return pl.pallas_call(
        body, name=name, grid=(t // tm,),
        in_specs=[rows(d), _whole((1, d)), rows(d), _whole((d, D_IN_PAD)), rows(LRU_W), rows(LRU_W), rows(ATT_W),
                  rows(2 * HEAD), rows(2 * HEAD), pl.BlockSpec((3, tm, DN_HEADS * HEAD), lambda i: (0, i, 0)),
                  rows(DN_HEADS * HEAD), pl.BlockSpec((2, tm, 2 * HEAD), lambda i: (0, i, 0))],
        out_specs=[rows(d), _whole((1, d)), rows(D_IN_PAD)],
        out_shape=[SDS((t, d), F32), SDS((1, d), F32), SDS((t, D_IN_PAD), BF16)],
        compiler_params=_params(("arbitrary",)))(h, gain, dout, w_in, dx, dgate, dq, dk, dv, dqkv, dz, dba)


def _block_diag(w):
    out = jnp.zeros((LRU_W, LRU_W), w.dtype)
    for h in range(LRU_W // HEAD):
        out = lax.dynamic_update_slice(out, w[h], (h * HEAD, h * HEAD))
    return out


def _diag_blocks(w):
    per = LRU_HALF // HEAD
    return jnp.stack([w[h // per, (h % per) * HEAD:(h % per + 1) * HEAD, (h % per) * HEAD:(h % per + 1) * HEAD]
                      for h in range(LRU_W // HEAD)])


def layer_params(w, wl, l, bias):
    row = lambda a: a[l].reshape(1, -1)
    return dict(
        ffn1_norm=row(w["ffn1_norm"]), ffn1=(wl["ffn1_w_gate"], wl["ffn1_w_up"], wl["ffn1_w_down"]),
        mix_norm=row(w["mix_norm"]) + wl["tie1"][0:1, 0:1], w_in=wl["w_in"],
        lru=(wl["lru_conv_w"], row(w["lru_conv_b"]), _block_diag(w["lru_w_a"][l]), row(w["lru_b_a"]),
             _block_diag(w["lru_w_x"][l]), row(w["lru_b_x"]), row(w["lru_lambda"])),
        bias=bias, sink_rows=jnp.repeat(w["attn_sinks"][l], BLOCK_Q).reshape(ATT_HEADS * BLOCK_Q, 1),
        dn_cw=wl["dn_conv_w"], dn_alog=_ba_row(w["dn_a_log"][l]), dn_dt=_ba_row(w["dn_dt_bias"][l]),
        dn_norm=jnp.tile(row(w["dn_norm"]), (1, 2)), w_out=wl["w_out"],
        ffn2_norm=row(w["ffn2_norm"]), ffn2=(wl["ffn2_w_gate"], wl["ffn2_w_up"], wl["ffn2_w_down"]),
        ple_norm=row(w["ple_norm"]), ple_w_gate=wl["ple_w_gate"], ple_w_proj=wl["ple_w_proj"])


def _ba_row(per_head):
    return jnp.pad(per_head, (DN_HEADS, 2 * HEAD - 2 * DN_HEADS)).reshape(1, 2 * HEAD)


def mixer_fwd(h, p, nb, seq, tag):
    u, n = norm_matmul(h, p["mix_norm"], p["w_in"], tn=D_IN_PAD // 2, name=f"mix_in_{tag}")
    y_lru = lru_fwd(u, *p["lru"], seq=seq, name=f"lru_fwd_{tag}")
    o = swa_fwd(u, p["bias"], p["sink_rows"], seq=seq, name=f"swa_fwd_{tag}")
    loc, egl = gdn_prep(u, p["dn_cw"], p["dn_alog"], p["dn_dt"], seq=seq, name=f"gdn_prep_{tag}")
    y_dn, o_raw, vn, st = gdn_scan(loc, egl, u, p["dn_norm"], seq=seq, name=f"gdn_scan_{tag}")
    out, ycat = mix_out(y_lru, o, y_dn, p["w_out"], h, name=f"mix_out_{tag}")
    return out, dict(h=h, u=u, n=n, loc=loc, egl=egl, o_raw=o_raw, vn=vn, st=st, ycat=ycat)


def mixer_bwd(dout, s, p, nb, seq, tag):
    u = s["u"]
    dy_lru, do, dy_dn = mix_out_bwd(dout, p["w_out"], name=f"mix_out_dx_{tag}")
    g = {"w_out": matmul(s["ycat"], dout, ta=True, tm=1024, name=f"mix_out_dw_{tag}")}
    dx, dgate, dcw, dwa, dwx, dvec = lru_bwd(u, *p["lru"], dy_lru, seq=seq, name=f"lru_bwd_{tag}")
    g.update(lru_conv_w=dcw, lru_conv_b=dvec[0], lru_w_a=_diag_blocks(dwa), lru_b_a=dvec[1], lru_w_x=_diag_blocks(dwx),
             lru_b_x=dvec[2], lru_lambda=dvec[3])
    dq, dk, dv, dbias, dsink = swa_bwd(u, p["bias"], p["sink_rows"], do, seq=seq, name=f"swa_bwd_{tag}")
    g.update(attn_sinks=dsink.reshape(ATT_HEADS, BLOCK_Q).sum(axis=1), bias=dbias)
    dloc, degl, dz, dgn = gdn_scan_bwd(s["loc"], s["egl"], u, p["dn_norm"], s["o_raw"], s["vn"], s["st"], dy_dn, seq=seq,
                                       name=f"gdn_scan_bwd_{tag}")
    dqkv, dba, dcw3, dhs = gdn_prep_bwd(u, p["dn_cw"], p["dn_alog"], p["dn_dt"], dloc, degl, seq=seq,
                                        name=f"gdn_prep_bwd_{tag}")
    dhs = dhs.sum(axis=0)[:, DN_HEADS:2 * DN_HEADS]
    g.update(dn_conv_w=dcw3.transpose(1, 0, 2).reshape(4, 3 * DN_HEADS * HEAD), dn_a_log=dhs[0], dn_dt_bias=dhs[1],
             dn_norm=dgn[0, :HEAD] + dgn[0, HEAD:])
    dh, dgain, du = mix_in_bwd(s["h"], p["mix_norm"], dout, p["w_in"], dx, dgate, dq, dk, dv, dqkv, dz, dba,
                               name=f"mix_in_bwd_{tag}")
    g["w_in"] = matmul(s["n"], du, ta=True, tm=1024, tn=640, name=f"mix_in_dw_{tag}")
    g["mix_norm"] = dgain[0]
    return dh, g


SHARDED = ("ffn1_w_gate", "ffn1_w_up", "ffn1_w_down", "w_in", "w_out", "ffn2_w_gate", "ffn2_w_up", "ffn2_w_down",
           "ple_w_gate", "ple_w_proj")
PER_LAYER_SMALL = ("ffn1_norm", "mix_norm", "lru_conv_w", "lru_conv_b", "lru_w_a", "lru_b_a", "lru_w_x", "lru_b_x",
                   "lru_lambda", "attn_sinks", "dn_conv_w", "dn_a_log", "dn_dt_bias", "dn_norm", "ffn2_norm", "ple_norm")


GRAD_PARTS = (("ple_w_gate", "ple_w_proj", "ffn2_w_gate", "ffn2_w_up", "ffn2_w_down"), ("w_in", "w_out"),
              ("ffn1_w_gate", "ffn1_w_up", "ffn1_w_down"))
WEIGHT_PARTS = (("ffn1_w_gate", "ffn1_w_up", "ffn1_w_down"),
                ("w_in", "w_out", "ffn2_w_gate", "ffn2_w_up", "ffn2_w_down", "ple_w_gate", "ple_w_proj", "lru_conv_w",
                 "dn_conv_w"))


def _col_shards(a):
    r, c = a.shape
    return a.reshape(r, N_CHIP, c // N_CHIP).transpose(1, 0, 2)


def local_step(x, p, target, w, layer_weights, layer_grads, bmap, nb, seq):
    bias = relbias_fwd(w["rel_bias"], bmap, name="relbias_fwd")
    h, saved = x, []
    for l in range(N_LAYER):
        wl = layer_weights(l, 0, h)
        s = dict(h0=h)
        h, *s["ffn1"] = ffn_fwd(h, w["ffn1_norm"][l].reshape(1, -1) + wl["tie0"][0:1, 0:1], wl["ffn1_w_gate"],
                                wl["ffn1_w_up"], wl["ffn1_w_down"], name=f"ffn1_fwd_{l}")
        wl.update(layer_weights(l, 1, h))
        pr = layer_params(w, wl, l, bias)
        h, s["mix"] = mixer_fwd(h, pr, nb, seq, l)
        s["h2"] = h
        h, *s["ffn2"] = ffn_fwd(h, pr["ffn2_norm"], *pr["ffn2"], name=f"ffn2_fwd_{l}")
        s["h3"] = h
        h = ple_fwd(h, pr["ple_norm"], pr["ple_w_gate"], p[l], pr["ple_w_proj"], name=f"ple_fwd_{l}")
        saved.append((pr, s))
    dh, dgf, loss = loss_head(h, w["final_norm"].reshape(1, -1), target, name="loss_head")

    per_layer, dbias, token = [None] * N_LAYER, None, None
    for l in reversed(range(N_LAYER)):
        pr, s = saved[l]
        g = {}
        dout = dh
        ple_norm = pr["ple_norm"] if token is None else pr["ple_norm"] + token[0:1, 0:1]
        dh, n, dga, dpp, dg = ple_bwd(s["h3"], ple_norm, pr["ple_w_gate"], p[l], pr["ple_w_proj"], dout, name=f"ple_bwd_{l}")
        g["ple_norm"] = dg[0]
        g["ple_w_gate"] = matmul(n, dga, ta=True, tm=1024, name=f"ple_dwg_{l}").reshape(N_CHIP, -1, D_MODEL)
        g["ple_w_proj"] = _col_shards(matmul(p[l], dpp, ta=True, name=f"ple_dwp_{l}"))
        for nm, hin in (("ffn2", s["h2"]), ("ffn1", s["h0"])):
            if nm == "ffn1":
                lru = list(pr["lru"])
                lru[1] = lru[1] + token[0:1, 0:1]
                dh, gm = mixer_bwd(dh, s["mix"], dict(pr, lru=tuple(lru)), nb, seq, l)
                dbias = gm.pop("bias") if dbias is None else dbias + gm.pop("bias")
                gm["w_in"] = _col_shards(gm["w_in"][:, :D_IN])
                gm["w_out"] = gm["w_out"].reshape(N_CHIP, -1, D_MODEL)
                g.update(gm)
                token = layer_grads(l, 1, {k: g.pop(k) for k in GRAD_PARTS[1]}, dh)
            dout = dh
            n, a, b = s[nm]
            dh, da, db, sact, dg = ffn_bwd_act(hin, pr[nm + "_norm"] + token[0:1, 0:1] if nm == "ffn1" else pr[nm + "_norm"],
                                               dout, a, b, *pr[nm], name=f"{nm}_bwd_act_{l}")
            g[nm + "_norm"] = dg[0]
            g[nm + "_w_gate"], g[nm + "_w_up"], g[nm + "_w_down"] = ffn_bwd_w(n, da, db, sact, dout, name=f"{nm}_bwd_w_{l}")
            part = 0 if nm == "ffn2" else 2
            token = layer_grads(l, part, {k: g.pop(k) for k in GRAD_PARTS[part]}, dh)
        per_layer[l] = g
    grads = {k: jnp.stack([per_layer[l][k] for l in range(N_LAYER)]) for k in PER_LAYER_SMALL}
    grads["rel_bias"] = relbias_bwd(dbias, bmap, name="relbias_bwd")[:, :ATT_HEADS]
    grads["final_norm"] = dgf[0]
    return loss, dh, grads


HBM_SPEC = pl.BlockSpec(memory_space=pltpu.HBM)


def _place():
    x, y, c = lax.axis_index("x"), lax.axis_index("y"), lax.axis_index("c")
    chips = [(1 - x, y), (x, 1 - y), (1 - x, 1 - y)]
    return x, y, c, 2 * x + y, (x, y, 1 - c), chips, [2 * cx + cy for cx, cy in chips]


def _remote(src, dst, send_sem, recv_sem, to):
    return pltpu.make_async_remote_copy(src_ref=src, dst_ref=dst, send_sem=send_sem, recv_sem=recv_sem, device_id=to,
                                        device_id_type=MESH)


N_DEV = 8


def allreduce_small(buf, *, name):
    rows = buf.shape[0]

    def body(in_ref, out_ref, gath, send, recv):
        x, y, c = lax.axis_index("x"), lax.axis_index("y"), lax.axis_index("c")
        mine = 4 * x + 2 * y + c
        gath[mine] = in_ref[...]
        cps = []
        for k in range(1, N_DEV):
            to = (x ^ (k >> 2), y ^ ((k >> 1) & 1), c ^ (k & 1))
            cps.append(_remote(in_ref, gath.at[mine], send.at[k - 1], recv.at[k - 1], to))
            cps[-1].start()
        for k in range(1, N_DEV):
            theirs = gath.at[4 * (x ^ (k >> 2)) + 2 * (y ^ ((k >> 1) & 1)) + (c ^ (k & 1))]
            _remote(theirs, theirs, send.at[k - 1], recv.at[k - 1], (x, y, c)).wait_recv()
        for cp in cps:
            cp.wait_send()
        acc = gath[0]
        for d in range(1, N_DEV):
            acc = acc + gath[d]
        out_ref[...] = acc

    vm = pl.BlockSpec(memory_space=pltpu.VMEM)
    return pl.pallas_call(
        body, name=name, in_specs=[vm], out_specs=vm, out_shape=SDS(buf.shape, F32),
        scratch_shapes=[pltpu.VMEM((N_DEV, rows, 128), F32), pltpu.SemaphoreType.DMA((N_DEV - 1,)),
                        pltpu.SemaphoreType.DMA((N_DEV - 1,))])(buf)


SEM_SPEC = pl.BlockSpec(memory_space=pltpu.SEMAPHORE)
ANY_SPEC = pl.BlockSpec(memory_space=pl.ANY)
DATAFLOW = pltpu.SideEffectType.DATAFLOW_SIDE_EFFECTING


def _in_hbm(a):
    return pltpu.with_memory_space_constraint(a, pltpu.HBM)


def _my_rows(ref_rows, c, mine=True):
    half = ref_rows // 2
    start = (c if mine else 1 - c) * half
    return pl.ds(pl.multiple_of(start, 8), half)


def place_layer_shard(ws, layer, chip_arr, dtype, after, *, name):
    n = len(ws)
    _, r, c = ws[0].shape
    tr = next(cand for cand in (352, 256, 128, 64, 32, 16, 8, r) if r % cand == 0)

    def body(chip_ref, *refs):
        for w_ref, o_ref in zip(refs[:n], refs[n + 1:]):
            o_ref[...] = w_ref[...].astype(dtype)

    return list(pl.pallas_call(
        body, name=name,
        grid_spec=pltpu.PrefetchScalarGridSpec(
            num_scalar_prefetch=1, grid=(r // tr,),
            in_specs=[pl.BlockSpec((None, tr, c), lambda i, chip: (layer, i, 0))] * n + [ANY_SPEC],
            out_specs=[pl.BlockSpec((None, tr, c), lambda i, chip: (chip[0], i, 0))] * n),
        out_shape=[SDS((N_CHIP, r, c), dtype)] * n, compiler_params=_params(("parallel",)))(chip_arr, *ws, after))


def _gather_pieces(refs, n_split, c, me, cids):
    mine, theirs = [], []
    for k, ref in enumerate(refs):
        if k < n_split:
            rows = _my_rows(ref.shape[1], c)
            mine.append(ref.at[me, rows])
            theirs.append([ref.at[cid, rows] for cid in cids])
        else:
            mine.append(ref.at[me])
            theirs.append([ref.at[cid] for cid in cids])
    return mine, theirs


def gather_start(bufs, n_split, after, *, name):
    n = len(bufs)

    def body(*refs):
        ins, send, recv, token = refs[:n], refs[n + 1], refs[n + 2], refs[-1]
        x, y, c, me, sib, chips, cids = _place()
        mine, _ = _gather_pieces(ins, n_split, c, me, cids)
        for k in range(n):
            for j, chip in enumerate(chips):
                _remote(mine[k], mine[k], send.at[3 * k + j], recv.at[3 * k + j], (*chip, c)).start()
        token[...] = jnp.zeros_like(token)

    out = pl.pallas_call(
        body, name=name, in_specs=[HBM_SPEC] * n + [ANY_SPEC],
        out_specs=[SEM_SPEC, SEM_SPEC] + [HBM_SPEC] * n + [pl.BlockSpec(memory_space=pltpu.VMEM)],
        out_shape=[pltpu.SemaphoreType.DMA((3 * n,)), pltpu.SemaphoreType.DMA((3 * n,))]
        + [pltpu.HBM(b.shape, b.dtype) for b in bufs] + [SDS((8, 128), F32)],
        input_output_aliases={k: k + 2 for k in range(n)},
        compiler_params=pltpu.CompilerParams(has_side_effects=DATAFLOW))(*[_in_hbm(b) for b in bufs], after)
    return out[0], out[1], list(out[2:2 + n]), out[-1]


def gather_wait(send, recv, bufs, n_split, after, *, name):
    n = len(bufs)

    def body(*refs):
        ins, send_ref, recv_ref = refs[:n], refs[n], refs[n + 1]
        x, y, c, me, sib, chips, cids = _place()
        mine, theirs = _gather_pieces(ins, n_split, c, me, cids)
        for k in range(n):
            for j in range(3):
                _remote(mine[k], mine[k], send_ref.at[3 * k + j], recv_ref.at[3 * k + j], sib).wait_send()
                _remote(theirs[k][j], theirs[k][j], send_ref.at[3 * k + j], recv_ref.at[3 * k + j], sib).wait_recv()

    return list(pl.pallas_call(
        body, name=name, in_specs=[HBM_SPEC] * n + [SEM_SPEC, SEM_SPEC, ANY_SPEC], out_specs=[HBM_SPEC] * n,
        out_shape=[pltpu.HBM(b.shape, b.dtype) for b in bufs], input_output_aliases={k: k for k in range(n)},
        compiler_params=pltpu.CompilerParams(has_side_effects=DATAFLOW))(*bufs, send, recv, after))


def gather_forward(bufs, *, name):
    n = len(bufs)

    def body(*refs):
        outs, (send, recv) = refs[n:2 * n], refs[2 * n:]
        x, y, c, me, sib, chips, cids = _place()
        cps = []
        for k in range(n):
            for j in range(3):
                piece = outs[k].at[cids[j], _my_rows(outs[k].shape[1], c)]
                cps.append(_remote(piece, piece, send.at[3 * k + j], recv.at[3 * k + j], sib))
                cps[-1].start()
        for k in range(n):
            for j in range(3):
                piece = outs[k].at[cids[j], _my_rows(outs[k].shape[1], c, mine=False)]
                _remote(piece, piece, send.at[3 * k + j], recv.at[3 * k + j], sib).wait_recv()
        for cp in cps:
            cp.wait_send()

    return list(pl.pallas_call(
        body, name=name, in_specs=[HBM_SPEC] * n, out_specs=[HBM_SPEC] * n, out_shape=[SDS(b.shape, b.dtype) for b in bufs],
        input_output_aliases={k: k for k in range(n)}, scratch_shapes=[pltpu.SemaphoreType.DMA((3 * n,))] * 2)(*bufs))


def _exchange_copies(ins, lands, send, recv, c, sib):
    return [_remote(ins[k].at[pl.ds(0, N_CHIP), _my_rows(ins[k].shape[1], c, mine=False)], lands[k], send.at[k],
                    recv.at[k], sib) for k in range(len(ins))]


def exchange_start(gs, *, name):
    n = len(gs)

    def body(*refs):
        ins, lands, send, recv, token = refs[:n], refs[n:2 * n], refs[2 * n], refs[2 * n + 1], refs[-1]
        x, y, c, me, sib, chips, cids = _place()
        for cp in _exchange_copies(ins, lands, send, recv, c, sib):
            cp.start()
        token[...] = jnp.zeros_like(token)

    lands = [_in_hbm(lax.empty((N_CHIP, g.shape[1] // 2, g.shape[2]), g.dtype)) for g in gs]
    out = pl.pallas_call(
        body, name=name, in_specs=[HBM_SPEC] * (2 * n),
        out_specs=[SEM_SPEC, SEM_SPEC] + [HBM_SPEC] * (2 * n) + [pl.BlockSpec(memory_space=pltpu.VMEM)],
        out_shape=[pltpu.SemaphoreType.DMA((n,)), pltpu.SemaphoreType.DMA((n,))]
        + [pltpu.HBM(b.shape, b.dtype) for b in list(gs) + lands] + [SDS((8, 128), F32)],
        input_output_aliases={k: k + 2 for k in range(2 * n)},
        compiler_params=pltpu.CompilerParams(has_side_effects=DATAFLOW))(*[_in_hbm(g) for g in gs], *lands)
    return out[0], out[1], list(out[2:2 + n]), list(out[2 + n:2 + 2 * n]), out[-1]


def exchange_wait(send, recv, gs, lands, after, *, name):
    n = len(gs)

    def body(*refs):
        ins, land_refs, send_ref, recv_ref = refs[:n], refs[n:2 * n], refs[2 * n], refs[2 * n + 1]
        x, y, c, me, sib, chips, cids = _place()
        for cp in _exchange_copies(ins, land_refs, send_ref, recv_ref, c, sib):
            cp.wait_send()
            cp.wait_recv()

    out = pl.pallas_call(
        body, name=name, in_specs=[HBM_SPEC] * (2 * n) + [SEM_SPEC, SEM_SPEC, ANY_SPEC], out_specs=[HBM_SPEC] * (2 * n),
        out_shape=[pltpu.HBM(b.shape, b.dtype) for b in list(gs) + list(lands)],
        input_output_aliases={k: k for k in range(2 * n)},
        compiler_params=pltpu.CompilerParams(has_side_effects=DATAFLOW))(*gs, *lands, send, recv, after)
    return list(out[:n]), list(out[n:])


def _half_tile(half):
    return next(cand for cand in (256, 176, 128, 64, 32, 16) if half % cand == 0)


def _same_shape_runs(arrays):
    runs = []
    for i, a in enumerate(arrays):
        if runs and arrays[runs[-1][-1]].shape == a.shape:
            runs[-1].append(i)
        else:
            runs.append([i])
    return runs


def reduce_add(gs, rs, c_arr, *, name):
    n = len(gs)
    _, rows, cdim = gs[0].shape
    half = rows // 2
    tr = _half_tile(half)

    def body(c_ref, *refs):
        for g_ref, r_ref, o_ref in zip(refs[:n], refs[n:2 * n], refs[2 * n:]):
            o_ref[...] = (g_ref[...] + r_ref[...]).astype(o_ref.dtype)

    mine = pl.BlockSpec((None, tr, cdim), lambda j, i, c: (j, c[0] * (half // tr) + i, 0))
    blk = pl.BlockSpec((None, tr, cdim), lambda j, i, c: (j, i, 0))
    return list(pl.pallas_call(
        body, name=name,
        grid_spec=pltpu.PrefetchScalarGridSpec(
            num_scalar_prefetch=1, grid=(N_CHIP, half // tr), in_specs=[mine] * n + [blk] * n, out_specs=[blk] * n),
        out_shape=[SDS((N_CHIP, half, cdim), BF16)] * n, compiler_params=_params(("parallel", "parallel")))(c_arr, *gs, *rs))


def reduce_start(ss, *, name):
    n = len(ss)

    def body(*refs):
        ins, lands, send, recv, token = refs[:n], refs[n:2 * n], refs[2 * n], refs[2 * n + 1], refs[-1]
        x, y, c, me, sib, chips, cids = _place()
        for k in range(n):
            for j, chip in enumerate(chips):
                _remote(ins[k].at[cids[j]], lands[k].at[j], send.at[3 * k + j], recv.at[3 * k + j], (*chip, c)).start()
        token[...] = jnp.zeros_like(token)

    lands = [_in_hbm(lax.empty((N_CHIP - 1,) + s.shape[1:], s.dtype)) for s in ss]
    out = pl.pallas_call(
        body, name=name, in_specs=[HBM_SPEC] * (2 * n),
        out_specs=[SEM_SPEC, SEM_SPEC] + [HBM_SPEC] * (2 * n) + [pl.BlockSpec(memory_space=pltpu.VMEM)],
        out_shape=[pltpu.SemaphoreType.DMA((3 * n,)), pltpu.SemaphoreType.DMA((3 * n,))]
        + [pltpu.HBM(b.shape, b.dtype) for b in list(ss) + lands] + [SDS((8, 128), F32)],
        input_output_aliases={k: k + 2 for k in range(2 * n)},
        compiler_params=pltpu.CompilerParams(has_side_effects=DATAFLOW))(*[_in_hbm(s) for s in ss], *lands)
    return out[0], out[1], list(out[2:2 + n]), list(out[2 + n:2 + 2 * n]), out[-1]


def reduce_wait(send, recv, ss, lands, after, *, name):
    n = len(ss)

    def body(*refs):
        ins, land_refs, send_ref, recv_ref = refs[:n], refs[n:2 * n], refs[2 * n], refs[2 * n + 1]
        x, y, c, me, sib, chips, cids = _place()
        for k in range(n):
            for j in range(3):
                _remote(ins[k].at[cids[j]], land_refs[k].at[j], send_ref.at[3 * k + j], recv_ref.at[3 * k + j],
                        sib).wait_send()
                _remote(ins[k].at[cids[j]], land_refs[k].at[j], send_ref.at[3 * k + j], recv_ref.at[3 * k + j],
                        sib).wait_recv()

    out = pl.pallas_call(
        body, name=name, in_specs=[HBM_SPEC] * (2 * n) + [SEM_SPEC, SEM_SPEC, ANY_SPEC], out_specs=[HBM_SPEC] * (2 * n),
        out_shape=[pltpu.HBM(b.shape, b.dtype) for b in list(ss) + list(lands)],
        input_output_aliases={k: k for k in range(2 * n)},
        compiler_params=pltpu.CompilerParams(has_side_effects=DATAFLOW))(*ss, *lands, send, recv, after)
    return list(out[:n]), list(out[n:])


def reduce_sum(owns, lands, place_arr, layer, accs, *, name):
    n = len(owns)
    _, half, cdim = lands[0].shape
    tr = _half_tile(half)
    have = accs[0] is not None

    def body(p_ref, *refs):
        for own_ref, land_ref, o_ref in zip(refs[:n], refs[n:2 * n], refs[-n:]):
            o_ref[...] = (((own_ref[...].astype(F32) + land_ref[0].astype(F32)) + land_ref[1].astype(F32))
                          + land_ref[2].astype(F32))

    in_specs = ([pl.BlockSpec((None, tr, cdim), lambda i, p: (p[0], i, 0))] * n
                + [pl.BlockSpec((N_CHIP - 1, tr, cdim), lambda i, p: (0, i, 0))] * n + ([ANY_SPEC] * n if have else []))
    out_spec = pl.BlockSpec((None, tr, cdim), lambda i, p: (layer, p[1] * (half // tr) + i, 0))
    return list(pl.pallas_call(
        body, name=name,
        grid_spec=pltpu.PrefetchScalarGridSpec(
            num_scalar_prefetch=1, grid=(half // tr,), in_specs=in_specs, out_specs=[out_spec] * n),
        out_shape=[SDS((N_LAYER, 2 * half, cdim), F32)] * n,
        input_output_aliases={1 + 2 * n + i: i for i in range(n)} if have else {},
        compiler_params=_params(("parallel",)))(place_arr, *owns, *lands, *(accs if have else [])))


def reduce_share(fs, *, name):
    n = len(fs)

    def body(*refs):
        outs, (send, recv) = refs[n:2 * n], refs[2 * n:]
        x, y, c, me, sib, chips, cids = _place()
        cps = []
        for k in range(n):
            piece = outs[k].at[pl.ds(0, N_LAYER), _my_rows(outs[k].shape[1], c)]
            cps.append(_remote(piece, piece, send.at[k], recv.at[k], sib))
            cps[-1].start()
        for k in range(n):
            theirs = outs[k].at[pl.ds(0, N_LAYER), _my_rows(outs[k].shape[1], c, mine=False)]
            _remote(theirs, theirs, send.at[k], recv.at[k], sib).wait_recv()
        for cp in cps:
            cp.wait_send()

    return list(pl.pallas_call(
        body, name=name, in_specs=[HBM_SPEC] * n, out_specs=[HBM_SPEC] * n, out_shape=[SDS(f.shape, f.dtype) for f in fs],
        input_output_aliases={k: k for k in range(n)}, scratch_shapes=[pltpu.SemaphoreType.DMA((n,))] * 2)(*fs))


WEIGHTS = ("ffn1_norm", "ffn1_w_gate", "ffn1_w_up", "ffn1_w_down", "mix_norm", "w_in", "lru_conv_w", "lru_conv_b", "lru_w_a",
           "lru_b_a", "lru_w_x", "lru_b_x", "lru_lambda", "attn_sinks", "rel_bias", "dn_conv_w", "dn_a_log", "dn_dt_bias",
           "dn_norm", "w_out", "ffn2_norm", "ffn2_w_gate", "ffn2_w_up", "ffn2_w_down", "ple_norm", "ple_w_gate",
           "ple_w_proj", "final_norm")
CONV_SHARDED = ("lru_conv_w", "dn_conv_w")
FFN_TRANSPOSED = ("ffn1_w_gate", "ffn1_w_up", "ffn2_w_gate", "ffn2_w_up")
SMALL = tuple(k for k in WEIGHTS if k not in SHARDED)


def _pack(arrs):
    blocks = []
    for a in arrs:
        v = a.reshape(-1)
        blocks.append(jnp.pad(v, (0, -v.shape[0] % 1024)).reshape(-1, 128))
    return jnp.concatenate(blocks, axis=0)


def _unpack(buf, shapes):
    out, off = [], 0
    for s in shapes:
        n = int(np.prod(s))
        rows = 8 * -(-n // 1024)
        out.append(buf[off:off + rows].reshape(-1)[:n].reshape(s))
        off += rows
    return out


def kernel(x, p, ffn1_norm, ffn1_w_gate, ffn1_w_up, ffn1_w_down, mix_norm, w_in, lru_conv_w, lru_conv_b, lru_w_a, lru_b_a, lru_w_x, lru_b_x, lru_lambda, attn_sinks, rel_bias, dn_conv_w, dn_a_log, dn_dt_bias, dn_norm, w_out, ffn2_norm, ffn2_w_gate, ffn2_w_up, ffn2_w_down, ple_norm, ple_w_gate, ple_w_proj, final_norm, loss_target, m_ffn1_norm, m_ffn1_w_gate, m_ffn1_w_up, m_ffn1_w_down, m_mix_norm, m_w_in, m_lru_conv_w, m_lru_conv_b, m_lru_w_a, m_lru_b_a, m_lru_w_x, m_lru_b_x, m_lru_lambda, m_attn_sinks, m_rel_bias, m_dn_conv_w, m_dn_a_log, m_dn_dt_bias, m_dn_norm, m_w_out, m_ffn2_norm, m_ffn2_w_gate, m_ffn2_w_up, m_ffn2_w_down, m_ple_norm, m_ple_w_gate, m_ple_w_proj, m_final_norm, v_ffn1_norm, v_ffn1_w_gate, v_ffn1_w_up, v_ffn1_w_down, v_mix_norm, v_w_in, v_lru_conv_w, v_lru_conv_b, v_lru_w_a, v_lru_b_a, v_lru_w_x, v_lru_b_x, v_lru_lambda, v_attn_sinks, v_rel_bias, v_dn_conv_w, v_dn_a_log, v_dn_dt_bias, v_dn_norm, v_w_out, v_ffn2_norm, v_ffn2_w_gate, v_ffn2_w_up, v_ffn2_w_down, v_ple_norm, v_ple_w_gate, v_ple_w_proj, v_final_norm):
    given = dict(locals())
    stored = lambda k, a: jnp.swapaxes(a, 1, 2) if k in FFN_TRANSPOSED else a
    ws = {k: stored(k, given[k]) for k in WEIGHTS}
    ms = {k: stored(k, given["m_" + k]) for k in WEIGHTS}
    vs = {k: stored(k, given["v_" + k]) for k in WEIGHTS}
    nb, seq, d = x.shape
    t = nb * seq
    cx, cy, cc = lax.axis_index("x"), lax.axis_index("y"), lax.axis_index("c")
    chip = 2 * cx + cy

    chip_arr = chip.astype(jnp.int32).reshape(1)
    c_arr = cc.astype(jnp.int32).reshape(1)
    place_arr = jnp.stack([chip, cc]).astype(jnp.int32)
    groups = [(l, part) for l in range(N_LAYER) for part in range(len(WEIGHT_PARTS))]
    placed, started = {}, {}

    def place_group(i, after):
        l, part = groups[i]
        ks = WEIGHT_PARTS[part]
        for run in _same_shape_runs([ws[k] for k in ks]):
            outs = place_layer_shard([ws[ks[j]] for j in run], l, chip_arr, F32 if ks[run[0]] in CONV_SHARDED else BF16,
                                     after, name=f"place_{ks[run[0]]}_{l}")
            placed.update({(l, ks[j]): o for j, o in zip(run, outs)})

    def start_group(i, after):
        l, part = groups[i]
        ks = WEIGHT_PARTS[part]
        n_split = sum(k in SHARDED for k in ks)
        started[i] = (ks, n_split) + gather_start([placed[l, k] for k in ks], n_split, after, name=f"gather_start_{l}_{part}")

    place_group(0, jnp.zeros((8, 128), F32))
    start_group(0, jnp.zeros((8, 128), F32))
    for i in range(1, len(groups)):
        place_group(i, started[0][-1])

    def layer_weights(l, part, h):
        i = groups.index((l, part))
        ks, n_split, send, recv, bufs, _ = started[i]
        bufs = gather_wait(send, recv, bufs, n_split, h, name=f"gather_wait_{l}_{part}")
        tie = jnp.zeros((8, 128), F32)
        for nxt in [j for j in range(i + 1, len(groups)) if j not in started and groups[j][0] == groups[min(i + 1, len(groups) - 1)][0]]:
            start_group(nxt, bufs[0] if nxt == i + 1 else started[nxt - 1][-1])
            tie = started[nxt][-1]
        wl = dict(zip(ks, gather_forward(bufs[:n_split], name=f"gather_forward_{l}_{part}") + bufs[n_split:]))
        for k in ("w_in", "ple_w_proj", "lru_conv_w", "dn_conv_w"):
            if k in wl:
                wl[k] = wl[k].transpose(1, 0, 2).reshape(wl[k].shape[1], -1)
        for k in ("w_out", "ple_w_gate"):
            if k in wl:
                wl[k] = wl[k].reshape(-1, wl[k].shape[-1])
        if "w_in" in wl:
            wl["w_in"] = jnp.pad(wl["w_in"], ((0, 0), (0, D_IN_PAD - D_IN)))
        wl[f"tie{part}"] = tie
        return wl

    pending, finished, tokens = [], {k: None for k in SHARDED}, []

    def finish_reduce(after):
        ks, send, recv, sums, lands, l, part = pending.pop(0)
        sums, lands = reduce_wait(send, recv, sums, lands, after, name=f"reduce_wait_{l}_{part}")
        for run in _same_shape_runs(sums):
            outs = reduce_sum([sums[i] for i in run], [lands[i] for i in run], place_arr, l, [finished[ks[i]] for i in run],
                              name=f"reduce_sum_{ks[run[0]]}_{l}")
            finished.update({ks[i]: o for i, o in zip(run, outs)})

    swapping = []

    def start_reduce(after):
        ks, send, recv, gs, theirs, l, part = swapping.pop(0)
        gs, theirs = exchange_wait(send, recv, gs, theirs, after, name=f"exchange_wait_{l}_{part}")
        sums = [None] * len(ks)
        for run in _same_shape_runs(gs):
            outs = reduce_add([gs[i] for i in run], [theirs[i] for i in run], c_arr, name=f"reduce_add_{ks[run[0]]}_{l}")
            for i, o in zip(run, outs):
                sums[i] = o
        send, recv, sums, lands, token = reduce_start(sums, name=f"reduce_start_{l}_{part}")
        pending.append((ks, send, recv, sums, lands, l, part))
        tokens.append(token)
        return token

    def layer_grads(l, part, g, dh):
        ks = GRAD_PARTS[part]
        send, recv, gs, theirs, token = exchange_start([g[k] for k in ks], name=f"exchange_start_{l}_{part}")
        swapping.append((ks, send, recv, gs, theirs, l, part))
        if len(swapping) > 1:
            token = token + start_reduce(dh)
        while len(pending) > 2:
            finish_reduce(dh)
        return token

    small_w = {k: ws[k] for k in SMALL if k not in CONV_SHARDED}
    bmap = jnp.asarray(_rel_bucket_map())
    loss, gx, grads = local_step(x.reshape(t, d), p.reshape(N_LAYER, t, PLE_DIM), loss_target.reshape(t, d), small_w,
                                 layer_weights, layer_grads, bmap, nb, seq)
    while swapping:
        start_reduce(gx)
    g_out, delta, new_m, new_v = {}, {}, {}, {}

    small_shapes = [grads[k].shape for k in SMALL]
    g_small = dict(zip(SMALL, _unpack(allreduce_small(_pack([grads[k] for k in SMALL]), name="allreduce_small"), small_shapes)))
    for k in CONV_SHARDED:
        width = ws[k].shape[-1]
        g_small[k] = lax.dynamic_slice_in_dim(g_small[k], chip * width, width, axis=2)
    g_out.update(g_small)
    shapes = [ws[k].shape for k in SMALL]
    tie = tokens[-1][0:1, 0:1]
    res = adamw([_pack([ws[k] for k in SMALL]) + tie], *[[_pack([src[k] for k in SMALL])] for src in (g_out, ms, vs)],
                name="adamw_small")
    for dst, r in zip((delta, new_m, new_v), res):
        dst.update(zip(SMALL, _unpack(r[0], shapes)))

    after = res[0][0]
    two_d = lambda a: a.reshape(-1, a.shape[-1])
    for part, ks in enumerate(GRAD_PARTS):
        while pending and pending[0][0] == ks:
            finish_reduce(after)
        g_out.update(zip(ks, reduce_share([finished[k] for k in ks], name=f"reduce_share_{part}")))
        for run in _same_shape_runs([ws[k] for k in ks]):
            names = [ks[i] for i in run]
            res = adamw(*[[two_d(src[k]) for k in names] for src in (ws, g_out, ms, vs)], name=f"adamw_{names[0]}")
            for dst, rs in zip((delta, new_m, new_v), res):
                dst.update({k: r.reshape(ws[k].shape) for k, r in zip(names, rs)})
            after = res[0][0]

    total = lax.psum(loss[0, 0], ("x", "y", "c"))
    return (total, gx.reshape(nb, seq, d), *[stored(k, out[k]) for out in (g_out, delta, new_m, new_v) for k in WEIGHTS])
```

```python
import math

import numpy as np
import jax
import jax.numpy as jnp
from jax import lax
from jax.experimental import pallas as pl
from jax.experimental.pallas import tpu as pltpu

F32 = jnp.float32
BF16 = jnp.bfloat16

EPS = 1e-6
D_MODEL = 1024
D_FF = 2816
N_CHIP = 4
FF_BLK = D_FF // N_CHIP
HEAD = 64
LRU_W = 256
ATT_W = 512
ATT_HEADS = 8
KV_HEADS = 2
ATT_GROUP = 4
BLOCK_Q = 128
DN_HEADS = 4
DN_CHUNK = 64
D_IN = 2312
D_IN_PAD = 2560
PLE_DIM = 256
REL_BUCKETS = 32
LRU_C = 8.0
N_LAYER = 2

ADAM_LR, ADAM_B1, ADAM_B2, ADAM_EPS, ADAM_WD, ADAM_STEP = 0.001, 0.9, 0.999, 1e-08, 0.01, 10

VMEM_LIMIT = 56 << 20
MESH = pl.DeviceIdType.MESH
SDS = jax.ShapeDtypeStruct


def _dot(a, b, ca=1, cb=0, hi=False):
    dims = (((ca,), (cb,)), ((), ()))
    one = lambda u, v: lax.dot_general(u, v, dims, preferred_element_type=F32)
    a_hi, b_hi = a.astype(BF16), b.astype(BF16)
    if not hi:
        return one(a_hi, b_hi)
    a_lo = (a - a_hi.astype(F32)).astype(BF16)
    b_lo = (b - b_hi.astype(F32)).astype(BF16)
    return one(a_hi, b_hi) + (one(a_hi, b_lo) + one(a_lo, b_hi))


def _nn(a, b, hi=False):
    return _dot(a, b, 1, 0, hi)


def _nt(a, b, hi=False):
    return _dot(a, b, 1, 1, hi)


def _tn(a, b, hi=False):
    return _dot(a, b, 0, 0, hi)


def _sigmoid(x):
    return jax.nn.sigmoid(x)


def _softplus(x):
    return jnp.maximum(x, 0.0) + jnp.log1p(jnp.exp(-jnp.abs(x)))


def _neg_expm1(z):
    series = -z * (1.0 + z * (0.5 + z * (1.0 / 6.0 + z * (1.0 / 24.0 + z * (1.0 / 120.0)))))
    return jnp.where(z > -0.05, series, 1.0 - jnp.exp(z))


_GELU_C = math.sqrt(2.0 / math.pi)


def _gelu(x):
    t = jnp.tanh(_GELU_C * (x + 0.044715 * x * x * x))
    return 0.5 * x * (1.0 + t), t


def _gelu_grad(x, t):
    return 0.5 * (1.0 + t) + 0.5 * x * (1.0 - t * t) * _GELU_C * (1.0 + 3.0 * 0.044715 * x * x)


def _rms_fwd(h, g):
    r = lax.rsqrt(jnp.mean(h * h, axis=-1, keepdims=True) + EPS)
    xh = h * r
    return xh * g, xh, r


def _rms_bwd(dn, xh, r, g):
    dxh = dn * g
    dh = r * (dxh - xh * jnp.mean(dxh * xh, axis=-1, keepdims=True))
    return dh, jnp.sum(dn * xh, axis=0, keepdims=True)


def _shift_down(x, d, fill=0.0):
    row = lax.broadcasted_iota(jnp.int32, x.shape, 0)
    return jnp.where(row >= d, pltpu.roll(x, d, 0), fill)


def _shift_up(x, d, fill=0.0):
    n = x.shape[0]
    row = lax.broadcasted_iota(jnp.int32, x.shape, 0)
    return jnp.where(row < n - d, pltpu.roll(x, n - d, 0), fill)


def _conv_fwd(x, w):
    y = x * w[3]
    for k in range(3):
        y = y + _shift_down(x, 3 - k) * w[k]
    return y


def _conv_bwd(dy, x, w):
    dx = dy * w[3]
    rows = [None] * 4
    rows[3] = jnp.sum(dy * x, axis=0, keepdims=True)
    for k in range(3):
        dx = dx + _shift_up(dy, 3 - k) * w[k]
        rows[k] = jnp.sum(dy * _shift_down(x, 3 - k), axis=0, keepdims=True)
    r4 = lax.broadcasted_iota(jnp.int32, (4, x.shape[1]), 0)
    dw = jnp.zeros((4, x.shape[1]), F32)
    for k in range(4):
        dw = jnp.where(r4 == k, rows[k], dw)
    return dx, dw


FFN_SPLIT = 2


def _interleave(gens):
    pending = list(gens)
    while pending:
        for g in list(pending):
            if next(g, StopIteration) is StopIteration:
                pending.remove(g)


def _params(sem=None, vmem=VMEM_LIMIT):
    return pltpu.CompilerParams(dimension_semantics=sem, vmem_limit_bytes=vmem)


def _whole(shape):
    nd = len(shape)
    return pl.BlockSpec(shape, lambda *_: (0,) * nd)


def matmul(a, b, *, name, ta=False, tb=False, residual=None, out_dtype=F32, tm=512, tn=512, tk=512):
    m, k = (a.shape[1], a.shape[0]) if ta else a.shape
    n = b.shape[0] if tb else b.shape[1]
    tm, tn, tk = min(tm, m), min(tn, n), min(tk, k)
    assert m % tm == 0 and n % tn == 0 and k % tk == 0, (m, n, k, tm, tn, tk)
    nk = k // tk

    def body(*refs):
        if residual is None:
            a_ref, b_ref, o_ref, acc = refs
        else:
            a_ref, b_ref, r_ref, o_ref, acc = refs
        kk = pl.program_id(2)

        @pl.when(kk == 0)
        def _():
            acc[...] = jnp.zeros_like(acc)

        acc[...] += _dot(a_ref[...], b_ref[...], 0 if ta else 1, 1 if tb else 0)

        @pl.when(kk == nk - 1)
        def _():
            out = acc[...]
            if residual is not None:
                out = out + r_ref[...]
            o_ref[...] = out.astype(out_dtype)

    a_spec = pl.BlockSpec((tk, tm), lambda i, j, kk: (kk, i)) if ta else pl.BlockSpec((tm, tk), lambda i, j, kk: (i, kk))
    b_spec = pl.BlockSpec((tn, tk), lambda i, j, kk: (j, kk)) if tb else pl.BlockSpec((tk, tn), lambda i, j, kk: (kk, j))
    o_spec = pl.BlockSpec((tm, tn), lambda i, j, kk: (i, j))
    in_specs, args = [a_spec, b_spec], [a, b]
    if residual is not None:
        in_specs.append(o_spec)
        args.append(residual)
    return pl.pallas_call(
        body, name=name, grid=(m // tm, n // tn, nk), in_specs=in_specs, out_specs=o_spec,
        out_shape=SDS((m, n), out_dtype), scratch_shapes=[pltpu.VMEM((tm, tn), F32)],
        compiler_params=_params(("parallel", "parallel", "arbitrary")))(*args)


def norm_matmul(h, gain, w, *, name, tm=512, tn=512):
    t, d = h.shape
    tm = min(tm, t)
    n = w.shape[1]
    assert t % tm == 0 and n % tn == 0

    def body(h_ref, g_ref, w_ref, u_ref, n_ref):
        @pl.when(pl.program_id(1) == 0)
        def _():
            n_ref[...] = _rms_fwd(h_ref[...], g_ref[...])[0].astype(BF16)

        u_ref[...] = _nn(n_ref[...], w_ref[...])

    return pl.pallas_call(
        body, name=name, grid=(t // tm, n // tn),
        in_specs=[pl.BlockSpec((tm, d), lambda i, j: (i, 0)), _whole((1, d)), pl.BlockSpec((d, tn), lambda i, j: (0, j))],
        out_specs=[pl.BlockSpec((tm, tn), lambda i, j: (i, j)), pl.BlockSpec((tm, d), lambda i, j: (i, 0))],
        out_shape=[SDS((t, n), F32), SDS((t, d), BF16)],
        compiler_params=_params(("parallel", "arbitrary")))(h, gain, w)


def ffn_fwd(h, gain, wg, wu, wd, *, name, tm=1024):
    t, d = h.shape
    tm = min(tm, t)

    def body(h_ref, g_ref, wg_ref, wu_ref, wd_ref, o_ref, n_ref, a_ref, b_ref, acc):
        j = pl.program_id(1)

        @pl.when(j == 0)
        def _():
            n_ref[...] = _rms_fwd(h_ref[...], g_ref[...])[0].astype(BF16)
            acc[...] = jnp.zeros_like(acc)

        def part(rows):
            n = n_ref[rows, :]
            a = _nt(n, wg_ref[...])
            b = _nt(n, wu_ref[...])
            yield
            a_ref[rows, :] = a.astype(BF16)
            b_ref[rows, :] = b.astype(BF16)
            acc[rows, :] += _nn(a * _sigmoid(a) * b, wd_ref[...])

        _interleave([part(pl.ds(k * (tm // FFN_SPLIT), tm // FFN_SPLIT)) for k in range(FFN_SPLIT)])

        @pl.when(j == N_CHIP - 1)
        def _():
            o_ref[...] = h_ref[...] + 0.5 * acc[...]

    row = pl.BlockSpec((tm, d), lambda i, j: (i, 0))
    blk = pl.BlockSpec((None, tm, FF_BLK), lambda i, j: (j, i, 0))
    wspec = pl.BlockSpec((None, FF_BLK, d), lambda i, j: (j, 0, 0))
    act = SDS((N_CHIP, t, FF_BLK), BF16)
    return pl.pallas_call(
        body, name=name, grid=(t // tm, N_CHIP), in_specs=[row, _whole((1, d)), wspec, wspec, wspec],
        out_specs=[row, row, blk, blk], out_shape=[SDS((t, d), F32), SDS((t, d), BF16), act, act],
        scratch_shapes=[pltpu.VMEM((tm, d), F32)],
        compiler_params=_params(("parallel", "arbitrary")))(h, gain, wg, wu, wd)


def ffn_bwd_act(h, gain, dout, a, b, wg, wu, wd, *, name, tm=512):
    t, d = h.shape
    tm = min(tm, t)

    def body(h_ref, g_ref, do_ref, a_ref, b_ref, wg_ref, wu_ref, wd_ref, dh_ref, da_ref, db_ref, s_ref, dg_ref, dn_acc):
        i, j = pl.program_id(0), pl.program_id(1)

        @pl.when((i == 0) & (j == 0))
        def _():
            dg_ref[...] = jnp.zeros_like(dg_ref)

        @pl.when(j == 0)
        def _():
            dn_acc[...] = jnp.zeros_like(dn_acc)

        def part(rows):
            ds = _nt(0.5 * do_ref[rows, :], wd_ref[...])
            yield
            a = a_ref[rows, :].astype(F32)
            b = b_ref[rows, :].astype(F32)
            sig = _sigmoid(a)
            sa = a * sig
            db = ds * sa
            da = ds * b * (sig * (1.0 + a * (1.0 - sig)))
            s_ref[rows, :] = (sa * b).astype(BF16)
            da_ref[rows, :] = da.astype(BF16)
            db_ref[rows, :] = db.astype(BF16)
            yield
            dn_acc[rows, :] += _nn(da, wg_ref[...]) + _nn(db, wu_ref[...])

        _interleave([part(pl.ds(k * (tm // FFN_SPLIT), tm // FFN_SPLIT)) for k in range(FFN_SPLIT)])

        @pl.when(j == N_CHIP - 1)
        def _():
            g = g_ref[...]
            _, xh, r = _rms_fwd(h_ref[...], g)
            dh, dg = _rms_bwd(dn_acc[...], xh, r, g)
            dh_ref[...] = do_ref[...] + dh
            dg_ref[...] += dg

    row = pl.BlockSpec((tm, d), lambda i, j: (i, 0))
    blk = pl.BlockSpec((None, tm, FF_BLK), lambda i, j: (j, i, 0))
    wspec = pl.BlockSpec((None, FF_BLK, d), lambda i, j: (j, 0, 0))
    act = SDS((N_CHIP, t, FF_BLK), BF16)
    return pl.pallas_call(
        body, name=name, grid=(t // tm, N_CHIP), in_specs=[row, _whole((1, d)), row, blk, blk, wspec, wspec, wspec],
        out_specs=[row, blk, blk, blk, _whole((1, d))],
        out_shape=[SDS((t, d), F32), act, act, act, SDS((1, d), F32)],
        scratch_shapes=[pltpu.VMEM((tm, d), F32)],
        compiler_params=_params(("arbitrary", "arbitrary")))(h, gain, dout, a, b, wg, wu, wd)


def ffn_bwd_w(n, da, db, s, dout, *, name, tk=1024):
    t, d = n.shape
    tk = min(tk, t)

    def body(n_ref, da_ref, db_ref, s_ref, do_ref, dwg_ref, dwu_ref, dwd_ref):
        @pl.when(pl.program_id(1) == 0)
        def _():
            dwg_ref[...] = jnp.zeros_like(dwg_ref)
            dwu_ref[...] = jnp.zeros_like(dwu_ref)
            dwd_ref[...] = jnp.zeros_like(dwd_ref)

        nn = n_ref[...]
        dwg_ref[...] += _tn(da_ref[...], nn)
        dwu_ref[...] += _tn(db_ref[...], nn)
        dwd_ref[...] += _tn(s_ref[...], 0.5 * do_ref[...])

    row = pl.BlockSpec((tk, d), lambda j, kk: (kk, 0))
    blk = pl.BlockSpec((None, tk, FF_BLK), lambda j, kk: (j, kk, 0))
    return pl.pallas_call(
        body, name=name, grid=(N_CHIP, t // tk), in_specs=[row, blk, blk, blk, row],
        out_specs=[pl.BlockSpec((None, FF_BLK, d), lambda j, kk: (j, 0, 0)),
                   pl.BlockSpec((None, FF_BLK, d), lambda j, kk: (j, 0, 0)),
                   pl.BlockSpec((None, FF_BLK, d), lambda j, kk: (j, 0, 0))],
        out_shape=[SDS((N_CHIP, FF_BLK, d), F32)] * 3,
        compiler_params=_params(("parallel", "arbitrary")))(n, da, db, s, dout)


def ple_fwd(h, gain, wpg, pl_in, wpp, *, name, tm=512):
    t, d = h.shape
    tm = min(tm, t)
    pd = pl_in.shape[1]

    def body(h_ref, g_ref, wpg_ref, p_ref, wpp_ref, o_ref):
        hh = h_ref[...]
        n = _rms_fwd(hh, g_ref[...])[0]
        gate = _sigmoid(_nn(n, wpg_ref[...]))
        o_ref[...] = hh + gate * _nn(p_ref[...], wpp_ref[...])

    row = pl.BlockSpec((tm, d), lambda i: (i, 0))
    return pl.pallas_call(
        body, name=name, grid=(t // tm,),
        in_specs=[row, _whole((1, d)), _whole((d, d)), pl.BlockSpec((tm, pd), lambda i: (i, 0)), _whole((pd, d))],
        out_specs=row, out_shape=SDS((t, d), F32), compiler_params=_params(("parallel",)))(h, gain, wpg, pl_in, wpp)


def ple_bwd(h, gain, wpg, pl_in, wpp, dout, *, name, tm=512):
    t, d = h.shape
    tm = min(tm, t)
    pd = pl_in.shape[1]

    def body(h_ref, g_ref, wpg_ref, p_ref, wpp_ref, do_ref, dh_ref, n_ref, dga_ref, dpp_ref, dg_ref):
        @pl.when(pl.program_id(0) == 0)
        def _():
            dg_ref[...] = jnp.zeros_like(dg_ref)

        g = g_ref[...]
        n, xh, r = _rms_fwd(h_ref[...], g)
        gate = _sigmoid(_nn(n, wpg_ref[...]))
        pp = _nn(p_ref[...], wpp_ref[...])
        do = do_ref[...]
        dga = do * pp * gate * (1.0 - gate)
        dh, dg = _rms_bwd(_nt(dga, wpg_ref[...]), xh, r, g)
        dh_ref[...] = do + dh
        n_ref[...] = n.astype(BF16)
        dga_ref[...] = dga.astype(BF16)
        dpp_ref[...] = (do * gate).astype(BF16)
        dg_ref[...] += dg

    row = pl.BlockSpec((tm, d), lambda i: (i, 0))
    return pl.pallas_call(
        body, name=name, grid=(t // tm,),
        in_specs=[row, _whole((1, d)), _whole((d, d)), pl.BlockSpec((tm, pd), lambda i: (i, 0)), _whole((pd, d)), row],
        out_specs=[row, row, row, row, _whole((1, d))],
        out_shape=[SDS((t, d), F32), SDS((t, d), BF16), SDS((t, d), BF16), SDS((t, d), BF16), SDS((1, d), F32)],
        compiler_params=_params(("arbitrary",)))(h, gain, wpg, pl_in, wpp, dout)


def loss_head(h, gain, target, *, name, tm=512):
    t, d = h.shape
    tm = min(tm, t)

    def body(h_ref, g_ref, t_ref, dh_ref, dg_ref, l_ref):
        @pl.when(pl.program_id(0) == 0)
        def _():
            dg_ref[...] = jnp.zeros_like(dg_ref)
            l_ref[...] = jnp.zeros_like(l_ref)

        g = g_ref[...]
        y, xh, r = _rms_fwd(h_ref[...], g)
        err = y - t_ref[...]
        l_ref[...] += 0.5 * jnp.sum(jnp.mean(err * err, axis=-1, keepdims=True), axis=0, keepdims=True)
        dh, dg = _rms_bwd(err * (1.0 / d), xh, r, g)
        dh_ref[...] = dh
        dg_ref[...] += dg

    row = pl.BlockSpec((tm, d), lambda i: (i, 0))
    return pl.pallas_call(
        body, name=name, grid=(t // tm,), in_specs=[row, _whole((1, d)), row],
        out_specs=[row, _whole((1, d)), _whole((1, 1))],
        out_shape=[SDS((t, d), F32), SDS((1, d), F32), SDS((1, 1), F32)],
        compiler_params=_params(("arbitrary",)))(h, gain, target)


def adamw(ws, gs, ms, vs, *, name):
    n = len(ws)
    r, c = ws[0].shape
    budget = (24 << 20) // (2 * 7 * n * c * 4)
    tr = next((cand for cand in (704, 512, 352, 256, 176, 128, 64, 32, 16, 8) if r % cand == 0 and cand <= budget), r)

    def body(*refs):
        for w_ref, g_ref, m_ref, v_ref, d_ref, nm_ref, nv_ref in zip(*[refs[i * n:(i + 1) * n] for i in range(7)]):
            gg = g_ref[...]
            mm = ADAM_B1 * m_ref[...] + (1.0 - ADAM_B1) * gg
            vv = ADAM_B2 * v_ref[...] + (1.0 - ADAM_B2) * (gg * gg)
            m_hat = mm / (1.0 - ADAM_B1 ** ADAM_STEP)
            v_hat = vv / (1.0 - ADAM_B2 ** ADAM_STEP)
            d_ref[...] = -ADAM_LR * (m_hat / (jnp.sqrt(v_hat) + ADAM_EPS) + ADAM_WD * w_ref[...])
            nm_ref[...] = mm
            nv_ref[...] = vv

    blk = pl.BlockSpec((tr, c), lambda i: (i, 0))
    out = pl.pallas_call(body, name=name, grid=(r // tr,), in_specs=[blk] * (4 * n), out_specs=[blk] * (3 * n),
                         out_shape=[SDS((r, c), F32)] * (3 * n), compiler_params=_params(("parallel",)))(*ws, *gs, *ms, *vs)
    return list(out[:n]), list(out[n:2 * n]), list(out[2 * n:])


def _scan_fwd(a, b):
    d = 1
    while d < a.shape[0]:
        b = a * _shift_down(b, d, 0.0) + b
        a = a * _shift_down(a, d, 1.0)
        d *= 2
    return b


def _scan_rev(a, b):
    d = 1
    while d < a.shape[0]:
        b = a * _shift_up(b, d, 0.0) + b
        a = a * _shift_up(a, d, 1.0)
        d *= 2
    return b


LRU_HALF = 128


def _lru_in_specs(seq):
    half = LRU_W // LRU_HALF
    vec = pl.BlockSpec((1, LRU_HALF), lambda j, b: (0, j))
    mat = pl.BlockSpec((LRU_HALF, LRU_HALF), lambda j, b: (j, j))
    return [pl.BlockSpec((seq, LRU_HALF), lambda j, b: (b, j)), pl.BlockSpec((seq, LRU_HALF), lambda j, b: (b, half + j)),
            pl.BlockSpec((4, LRU_HALF), lambda j, b: (0, j)), vec, mat, vec, mat, vec, vec]


def _lru_math(x_ref, gate_ref, cw_ref, cb_ref, wa_ref, ba_ref, wx_ref, bx_ref, lam_ref):
    x = x_ref[...]
    gate = gate_ref[...]
    cw =[cw_ref[k:k + 1, :] for k in range(4)]
    xr = _conv_fwd(x, cw) + cb_ref[...]
    r = _sigmoid(_nn(xr, wa_ref[...]) + ba_ref[...])
    i = _sigmoid(_nn(xr, wx_ref[...]) + bx_ref[...])
    sp = _softplus(-lam_ref[...])
    log_a = -LRU_C * r * sp
    a = jnp.exp(log_a)
    mult = jnp.sqrt(_neg_expm1(2.0 * log_a))
    gi = i * xr
    h = _scan_fwd(a, mult * gi)
    gl, tg = _gelu(gate)
    return dict(x=x, gate=gate, cw=cw, xr=xr, r=r, i=i, sp=sp, a=a, mult=mult, gi=gi, h=h, gl=gl, tg=tg)


def lru_fwd(u, cw, cb, wa, ba, wx, bx, lam, *, seq, name):
    t = u.shape[0]

    def body(x_ref, gate_ref, cw_ref, cb_ref, wa_ref, ba_ref, wx_ref, bx_ref, lam_ref, y_ref):
        f = _lru_math(x_ref, gate_ref, cw_ref, cb_ref, wa_ref, ba_ref, wx_ref, bx_ref, lam_ref)
        y_ref[...] = f["gl"] * f["h"]

    return pl.pallas_call(
        body, name=name, grid=(LRU_W // LRU_HALF, t // seq), in_specs=_lru_in_specs(seq),
        out_specs=pl.BlockSpec((seq, LRU_HALF), lambda j, b: (b, j)), out_shape=SDS((t, LRU_W), F32),
        compiler_params=_params(("parallel", "parallel")))(u, u, cw, cb, wa, ba, wx, bx, lam)


def lru_bwd(u, cw, cb, wa, ba, wx, bx, lam, dy, *, seq, name):
    t = u.shape[0]

    def body(x_ref, gate_ref, cw_ref, cb_ref, wa_ref, ba_ref, wx_ref, bx_ref, lam_ref, dy_ref,
             dx_ref, dgate_ref, dcw_ref, dwa_ref, dwx_ref, dv_ref):
        @pl.when(pl.program_id(1) == 0)
        def _():
            dcw_ref[...] = jnp.zeros_like(dcw_ref)
            dwa_ref[...] = jnp.zeros_like(dwa_ref)
            dwx_ref[...] = jnp.zeros_like(dwx_ref)
            dv_ref[...] = jnp.zeros_like(dv_ref)

        f = _lru_math(x_ref, gate_ref, cw_ref, cb_ref, wa_ref, ba_ref, wx_ref, bx_ref, lam_ref)
        dy = dy_ref[...]
        a, h, xr, r, i, mult, gi, sp = f["a"], f["h"], f["xr"], f["r"], f["i"], f["mult"], f["gi"], f["sp"]
        dgate_ref[...] = dy * h * _gelu_grad(f["gate"], f["tg"])
        lamb = _scan_rev(_shift_up(a, 1, 0.0), dy * f["gl"])
        da = lamb * _shift_down(h, 1)
        dlog_a = da * a - (lamb * gi) * (a * a) / mult
        dgi = lamb * mult
        dra = dlog_a * (-LRU_C * sp) * r * (1.0 - r)
        dia = dgi * xr * i * (1.0 - i)
        dsp = jnp.sum(dlog_a * (-LRU_C * r), axis=0, keepdims=True)
        dlam = -dsp * _sigmoid(-lam_ref[...])
        dxr = dgi * i + _nt(dra, wa_ref[...]) + _nt(dia, wx_ref[...])
        dx, dcw = _conv_bwd(dxr, f["x"], f["cw"])
        dx_ref[...] = dx
        dcw_ref[...] += dcw
        dwa_ref[...] += _tn(xr, dra)
        dwx_ref[...] += _tn(xr, dia)
        rows = [jnp.sum(dxr, axis=0, keepdims=True), jnp.sum(dra, axis=0, keepdims=True),
                jnp.sum(dia, axis=0, keepdims=True), dlam]
        r8 = lax.broadcasted_iota(jnp.int32, (8, LRU_HALF), 0)
        acc = jnp.zeros((8, LRU_HALF), F32)
        for k, row in enumerate(rows):
            acc = jnp.where(r8 == k, row, acc)
        dv_ref[...] += acc

    nhalf = LRU_W // LRU_HALF
    col = pl.BlockSpec((seq, LRU_HALF), lambda j, b: (b, j))
    mat = pl.BlockSpec((None, LRU_HALF, LRU_HALF), lambda j, b: (j, 0, 0))
    return pl.pallas_call(
        body, name=name, grid=(nhalf, t // seq), in_specs=_lru_in_specs(seq) + [col],
        out_specs=[col, col, pl.BlockSpec((4, LRU_HALF), lambda j, b: (0, j)), mat, mat,
                   pl.BlockSpec((8, LRU_HALF), lambda j, b: (0, j))],
        out_shape=[SDS((t, LRU_W), F32), SDS((t, LRU_W), F32), SDS((4, LRU_W), F32),
                   SDS((nhalf, LRU_HALF, LRU_HALF), F32), SDS((nhalf, LRU_HALF, LRU_HALF), F32), SDS((8, LRU_W), F32)],
        compiler_params=_params(("arbitrary", "arbitrary")))(u, u, cw, cb, wa, ba, wx, bx, lam, dy)


NEG = -1e30


def _rel_bucket_map():
    dist = (np.arange(BLOCK_Q)[:, None] - np.arange(BLOCK_Q)[None, :]) % BLOCK_Q
    max_exact = REL_BUCKETS // 2
    large = max_exact + (np.log(np.maximum(dist, 1).astype(np.float32) / max_exact)
                         / math.log(BLOCK_Q / max_exact) * (REL_BUCKETS - max_exact)).astype(np.int32)
    large = np.minimum(large, REL_BUCKETS - 1)
    return np.where(dist < max_exact, dist, large).astype(np.int32)


def relbias_fwd(rel_bias, bmap, *, name):
    def body(rb_ref, bm_ref, o_ref):
        bm = bm_ref[...]
        for h in range(ATT_HEADS):
            acc = jnp.zeros((BLOCK_Q, BLOCK_Q), F32)
            for b in range(REL_BUCKETS):
                acc = jnp.where(bm == b, rb_ref[b, h], acc)
            o_ref[h] = acc

    return pl.pallas_call(
        body, name=name, in_specs=[pl.BlockSpec(memory_space=pltpu.SMEM), pl.BlockSpec(memory_space=pltpu.VMEM)],
        out_specs=pl.BlockSpec(memory_space=pltpu.VMEM), out_shape=SDS((ATT_HEADS, BLOCK_Q, BLOCK_Q), F32))(rel_bias, bmap)


def relbias_bwd(dbias, bmap, *, name):
    def body(db_ref, bm_ref, o_ref):
        bm = bm_ref[...]
        row = lax.broadcasted_iota(jnp.int32, (REL_BUCKETS, 128), 0)
        col = lax.broadcasted_iota(jnp.int32, (REL_BUCKETS, 128), 1)
        acc = jnp.zeros((REL_BUCKETS, 128), F32)
        for h in range(ATT_HEADS):
            d = db_ref[h]
            for b in range(REL_BUCKETS):
                s = jnp.sum(jnp.sum(jnp.where(bm == b, d, 0.0), axis=1, keepdims=True), axis=0, keepdims=True)
                acc = jnp.where((row == b) & (col == h), s, acc)
        o_ref[...] = acc

    return pl.pallas_call(body, name=name, out_shape=SDS((REL_BUCKETS, 128), F32))(dbias, bmap)


def _iota2(shape, axis):
    return lax.broadcasted_iota(jnp.int32, shape, axis)


def _chunk_cumsum(x):
    pos = _iota2(x.shape, 0) & (DN_CHUNK - 1)
    d = 1
    while d < DN_CHUNK:
        x = x + jnp.where(pos >= d, pltpu.roll(x, d, 0), 0.0)
        d *= 2
    return x


def _chunk_rev_cumsum(x):
    n = x.shape[0]
    pos = _iota2(x.shape, 0) & (DN_CHUNK - 1)
    d = 1
    while d < DN_CHUNK:
        x = x + jnp.where(pos < DN_CHUNK - d, pltpu.roll(x, n - d, 0), 0.0)
        d *= 2
    return x


_DN_SCALE = (HEAD ** -0.5, 1.0, None)
DN_UNROLL = 8


COL_Q, COL_K, COL_V = 512 // 128, 1024 // 128, 1152 // 128
COL_DNQ, COL_DNK, COL_DNV, COL_DNZ, COL_BA = 1280 // 128, 1536 // 128, 1792 // 128, 2048 // 128, 2304 // 128


def _lane_a(shape):
    return _iota2(shape, 1) < HEAD


def _bd(x):
    la = _lane_a(x.shape)
    return jnp.concatenate([jnp.where(la, x, 0.0), jnp.where(la, 0.0, x)], axis=0)


def _fold(m):
    return m[:HEAD] + m[HEAD:]


def _bd_mask():
    return (_iota2((2 * HEAD, 2 * HEAD), 0) < HEAD) == (_iota2((2 * HEAD, 2 * HEAD), 1) < HEAD)


def _pk_nn(x, y, hi=False):
    return _nn(x, _bd(y), hi)


def _pk_nt(u, v, hi=False):
    return _nt(u, _bd(v), hi)


def _pk_tn(x, y, hi=False):
    return _fold(jnp.where(_bd_mask(), _tn(x, y, hi), 0.0))


def _half_sum(x):
    la = _lane_a(x.shape)
    return jnp.where(la, jnp.sum(jnp.where(la, x, 0.0), axis=-1, keepdims=True),
                     jnp.sum(jnp.where(la, 0.0, x), axis=-1, keepdims=True))


def _lane_col(x, idx):
    return jnp.sum(jnp.where(_iota2(x.shape, 1) == idx, x, 0.0), axis=-1, keepdims=True)


def _row0(x):
    return jnp.max(x, axis=0, keepdims=True)


def _dup_kv(x, g):
    la = _lane_a(x.shape)
    rolled = pltpu.roll(x, HEAD, 1)
    return jnp.where(la, x, rolled) if g == 0 else jnp.where(la, rolled, x)


def _stack_heads(ref, g):
    la = _lane_a((BLOCK_Q, 2 * HEAD))
    parts = []
    for hh in range(ATT_GROUP):
        pair = ref[:, pl.ds(2 * HEAD * (2 * g + hh // 2), 2 * HEAD)]
        parts.append(jnp.where(la if hh % 2 == 0 else ~la, pair, 0.0))
    return jnp.concatenate(parts, axis=0)


def _unstack_heads(stack, ref, g):
    la = _lane_a((BLOCK_Q, 2 * HEAD))
    for j in range(2):
        top = stack[2 * j * BLOCK_Q:(2 * j + 1) * BLOCK_Q]
        bot = stack[(2 * j + 1) * BLOCK_Q:(2 * j + 2) * BLOCK_Q]
        ref[:, pl.ds(2 * HEAD * (2 * g + j), 2 * HEAD)] = jnp.where(la, top, bot)


def _swa_probs(q_ref, k_ref, v_ref, b_ref, s_ref, n, g):
    rows = ATT_GROUP * BLOCK_Q
    prev = pl.multiple_of(jnp.maximum(n - 1, 0) * BLOCK_Q, BLOCK_Q)
    cur = pl.multiple_of(n * BLOCK_Q, BLOCK_Q)
    kp, kc = _dup_kv(k_ref[pl.ds(prev, BLOCK_Q), :], g), _dup_kv(k_ref[pl.ds(cur, BLOCK_Q), :], g)
    vp, vc = _dup_kv(v_ref[pl.ds(prev, BLOCK_Q), :], g), _dup_kv(v_ref[pl.ds(cur, BLOCK_Q), :], g)
    qs = _stack_heads(q_ref, g) * (HEAD ** -0.5)
    bias = b_ref[pl.ds(ATT_GROUP * g, ATT_GROUP)].reshape(rows, BLOCK_Q)
    i = _iota2((rows, BLOCK_Q), 0) & (BLOCK_Q - 1)
    j = _iota2((rows, BLOCK_Q), 1)
    s_p = jnp.where((j > i) & (n > 0), _nt(qs, kp) + bias, NEG)
    s_c = jnp.where(j <= i, _nt(qs, kc) + bias, NEG)
    sink = s_ref[pl.ds(rows * g, rows), :]
    m = jnp.maximum(jnp.maximum(jnp.max(s_p, axis=-1, keepdims=True), jnp.max(s_c, axis=-1, keepdims=True)), sink)
    e_p, e_c, e_s = jnp.exp(s_p - m), jnp.exp(s_c - m), jnp.exp(sink - m)
    inv = 1.0 / (jnp.sum(e_p, axis=-1, keepdims=True) + jnp.sum(e_c, axis=-1, keepdims=True) + e_s)
    return e_p * inv, e_c * inv, e_s * inv, qs, kp, kc, vp, vc, prev, cur


def _swa_specs(seq):
    nblk = seq // BLOCK_Q
    qspec = pl.BlockSpec((BLOCK_Q, ATT_W), lambda b, n: (b * nblk + n, COL_Q * 128 // ATT_W))
    kspec = pl.BlockSpec((seq, 2 * HEAD), lambda b, n: (b, COL_K))
    vspec = pl.BlockSpec((seq, 2 * HEAD), lambda b, n: (b, COL_V))
    ospec = pl.BlockSpec((BLOCK_Q, ATT_W), lambda b, n: (b * nblk + n, 0))
    kvout = pl.BlockSpec((seq, 2 * HEAD), lambda b, n: (b, 0))
    return qspec, kspec, vspec, ospec, kvout, _whole((ATT_HEADS, BLOCK_Q, BLOCK_Q)), _whole((ATT_HEADS * BLOCK_Q, 1))


def swa_fwd(u, bias, sink_rows, *, seq, name):
    t = u.shape[0]

    def body(q_ref, k_ref, v_ref, b_ref, s_ref, o_ref):
        for g in range(KV_HEADS):
            p_p, p_c, _, _, _, _, vp, vc, _, _ = _swa_probs(q_ref, k_ref, v_ref, b_ref, s_ref, pl.program_id(1), g)
            _unstack_heads(_nn(p_p, vp) + _nn(p_c, vc), o_ref, g)

    qspec, kspec, vspec, ospec, kvout, bspec, sspec = _swa_specs(seq)
    return pl.pallas_call(
        body, name=name, grid=(t // seq, seq // BLOCK_Q), in_specs=[qspec, kspec, vspec, bspec, sspec], out_specs=ospec,
        out_shape=SDS((t, ATT_W), F32), compiler_params=_params(("parallel", "arbitrary")))(u, u, u, bias, sink_rows)


def swa_bwd(u, bias, sink_rows, do, *, seq, name):
    t = u.shape[0]

    def body(q_ref, k_ref, v_ref, b_ref, s_ref, do_ref, dq_ref, dk_ref, dv_ref, db_ref, ds_ref):
        b, n = pl.program_id(0), pl.program_id(1)

        @pl.when((b == 0) & (n == 0))
        def _():
            db_ref[...] = jnp.zeros_like(db_ref)
            ds_ref[...] = jnp.zeros_like(ds_ref)

        @pl.when(n == 0)
        def _():
            dk_ref[...] = jnp.zeros_like(dk_ref)
            dv_ref[...] = jnp.zeros_like(dv_ref)

        la = _lane_a((BLOCK_Q, 2 * HEAD))
        for g in range(KV_HEADS):
            p_p, p_c, p_s, qs, kp, kc, vp, vc, prev, cur = _swa_probs(q_ref, k_ref, v_ref, b_ref, s_ref, n, g)
            do = _stack_heads(do_ref, g)
            dp_p, dp_c = _nt(do, vp), _nt(do, vc)
            delta = jnp.sum(p_p * dp_p, axis=-1, keepdims=True) + jnp.sum(p_c * dp_c, axis=-1, keepdims=True)
            ds_p, ds_c = p_p * (dp_p - delta), p_c * (dp_c - delta)
            _unstack_heads((_nn(ds_p, kp) + _nn(ds_c, kc)) * (HEAD ** -0.5), dq_ref, g)
            mine = la if g == 0 else ~la

            def to_head(x):
                return jnp.where(mine, x + pltpu.roll(x, HEAD, 1), 0.0)

            dk_ref[pl.ds(prev, BLOCK_Q), :] += to_head(_tn(ds_p, qs))
            dk_ref[pl.ds(cur, BLOCK_Q), :] += to_head(_tn(ds_c, qs))
            dv_ref[pl.ds(prev, BLOCK_Q), :] += to_head(_tn(p_p, do))
            dv_ref[pl.ds(cur, BLOCK_Q), :] += to_head(_tn(p_c, do))
            db_ref[pl.ds(ATT_GROUP * g, ATT_GROUP)] += (ds_p + ds_c).reshape(ATT_GROUP, BLOCK_Q, BLOCK_Q)
            rows = ATT_GROUP * BLOCK_Q
            ds_ref[pl.ds(rows * g, rows), :] += -p_s * delta

    qspec, kspec, vspec, ospec, kvout, bspec, sspec = _swa_specs(seq)
    return pl.pallas_call(
        body, name=name, grid=(t // seq, seq // BLOCK_Q), in_specs=[qspec, kspec, vspec, bspec, sspec, ospec],
        out_specs=[ospec, kvout, kvout, bspec, sspec],
        out_shape=[SDS((t, ATT_W), F32), SDS((t, 2 * HEAD), F32), SDS((t, 2 * HEAD), F32),
                   SDS((ATT_HEADS, BLOCK_Q, BLOCK_Q), F32), SDS((ATT_HEADS * BLOCK_Q, 1), F32)],
        compiler_params=_params(("arbitrary", "arbitrary")))(u, u, u, bias, sink_rows, do)


def _gdn_gates(ba_ref, alog_ref, dt_ref, hp):
    blk = ba_ref[...]
    beta_blk = _sigmoid(blk)
    sp_arg = blk + dt_ref[...]
    a_exp = jnp.exp(alog_ref[...])
    g_blk = -a_exp * _softplus(sp_arg)
    la = _lane_a(blk.shape)
    ha = 2 * hp
    beta = jnp.where(la, _lane_col(beta_blk, ha), _lane_col(beta_blk, ha + 1))
    g = jnp.where(la, _lane_col(g_blk, DN_HEADS + ha), _lane_col(g_blk, DN_HEADS + ha + 1))
    return beta, g, beta_blk, sp_arg, a_exp, g_blk


def _gdn_act(c, scale):
    sig = _sigmoid(c)
    a = c * sig
    if scale is None:
        return a, sig, None, None
    r = lax.rsqrt(_half_sum(a * a) + EPS)
    return a * r * scale, sig, a * r, r


def _gdn_inputs(pre_refs, cw_refs, ba_ref, alog_ref, dt_ref, hp, act_sc, b_sc, gc_sc, c_sc=None):
    for idx in range(3):
        c = _conv_fwd(pre_refs[idx][...], [cw_refs[idx][k:k + 1, :] for k in range(4)])
        if c_sc is not None:
            c_sc[idx] = c
        act_sc[idx] = _gdn_act(c, _DN_SCALE[idx])[0]
    beta, g = _gdn_gates(ba_ref, alog_ref, dt_ref, hp)[:2]
    b_sc[...] = beta
    gc_sc[...] = _chunk_cumsum(g)


def _gdn_chunk(q, k, v, b, gcc):
    shape = q.shape
    row, lm = _iota2(shape, 0), _iota2(shape, 1) & (HEAD - 1)
    tril, strict, eye = row >= lm, row > lm, row == lm
    eg = jnp.exp(gcc)
    kb, vb = k * b, v * b
    kbg = kb * eg
    grow = jnp.sum(jnp.where(eye, gcc, 0.0), axis=0, keepdims=True)
    dm = jnp.exp(jnp.where(tril, gcc - grow, NEG))
    kk = _pk_nt(kb, k)
    glast = jnp.sum(jnp.where(row == DN_CHUNK - 1, gcc, 0.0), axis=0, keepdims=True)
    ekd = jnp.exp(glast - gcc)
    qk = _pk_nt(q, k)
    return dict(q=q, k=k, v=v, b=b, tril=tril, strict=strict, eye=eye, row=row, eg=eg, kb=kb, vb=vb, kbg=kbg, dm=dm, kk=kk,
                low=jnp.where(strict, kk * dm, 0.0), glast=glast, ekd=ekd, kd=k * ekd, qk=qk,
                amat=jnp.where(tril, qk * dm, 0.0), qg=q * eg, egl=jnp.broadcast_to(jnp.exp(glast), shape))


def _tri_inv_many(chunks):
    ms = [-m["low"] for m in chunks]
    ts = [m["eye"].astype(F32) + x for m, x in zip(chunks, ms)]
    for _ in range(int(math.log2(HEAD)) - 1):
        ms = [_pk_nn(x, x, hi=True) for x in ms]
        ts = [t + _pk_nn(t, x, hi=True) for t, x in zip(ts, ms)]
    return ts


def _gdn_chunk_loop(nc, act_sc, b_sc, gc_sc, finish):
    u = math.gcd(nc, DN_UNROLL)

    def step(i, carry):
        rows = [pl.ds(pl.multiple_of((i * u + j) * DN_CHUNK, DN_CHUNK), DN_CHUNK) for j in range(u)]
        chunks = [_gdn_chunk(act_sc[0, r, :], act_sc[1, r, :], act_sc[2, r, :], b_sc[r, :], gc_sc[r, :]) for r in rows]
        pending = [finish(r, m, t) for r, m, t in zip(rows, chunks, _tri_inv_many(chunks))]
        pending = [g for g in pending if g is not None]
        while pending:
            for g in list(pending):
                if next(g, StopIteration) is StopIteration:
                    pending.remove(g)
        return carry

    lax.fori_loop(0, nc // u, step, 0)


def _gdn_in_specs(seq):
    u_at = lambda col: pl.BlockSpec((seq, 2 * HEAD), lambda b, hp, _c=col: (b, _c + hp))
    cw_at = lambda col: pl.BlockSpec((4, 2 * HEAD), lambda b, hp, _c=col: (0, _c + hp))
    row = pl.BlockSpec((1, 2 * HEAD), lambda b, hp: (0, 0))
    ba = pl.BlockSpec((seq, 2 * HEAD), lambda b, hp: (b, COL_BA))
    return [u_at(COL_DNQ), u_at(COL_DNK), u_at(COL_DNV), ba, cw_at(0), cw_at(2), cw_at(4), row, row]


def _pair(seq, lead=None):
    if lead is None:
        return pl.BlockSpec((seq, 2 * HEAD), lambda b, hp: (b, hp))
    return pl.BlockSpec((lead, seq, 2 * HEAD), lambda b, hp: (0, b, hp))


def _swap(spec):
    return pl.BlockSpec(spec.block_shape, lambda hp, b, _f=spec.index_map: _f(b, hp))


def gdn_prep(u, cw, alog_row, dt_row, *, seq, name):
    t = u.shape[0]
    nc = seq // DN_CHUNK

    def body(q_ref, k_ref, v_ref, ba_ref, cq_ref, ck_ref, cv_ref, alog_ref, dt_ref, loc_ref, egl_ref, act_sc, b_sc, gc_sc):
        _gdn_inputs((q_ref, k_ref, v_ref), (cq_ref, ck_ref, cv_ref), ba_ref, alog_ref, dt_ref, pl.program_id(1),
                    act_sc, b_sc, gc_sc)

        def finish(rows, m, t):
            loc_ref[0, rows, :] = m["qg"]
            loc_ref[1, rows, :] = m["kd"]
            loc_ref[2, rows, :] = _pk_nn(t, m["vb"])
            loc_ref[3, rows, :] = _pk_nn(t, m["kbg"])
            loc_ref[4, rows, :] = m["amat"]
            egl_ref[rows, :] = m["egl"]

        _gdn_chunk_loop(nc, act_sc, b_sc, gc_sc, finish)

    return pl.pallas_call(
        body, name=name, grid=(t // seq, DN_HEADS // 2), in_specs=_gdn_in_specs(seq), out_specs=[_pair(seq, 5), _pair(seq)],
        out_shape=[SDS((5, t, DN_HEADS * HEAD), F32), SDS((t, DN_HEADS * HEAD), F32)],
        scratch_shapes=[pltpu.VMEM((3, seq, 2 * HEAD), F32)] + [pltpu.VMEM((seq, 2 * HEAD), F32)] * 2,
        compiler_params=_params(("parallel", "parallel")))(u, u, u, u, cw, cw, cw, alog_row, dt_row)


def _gated_norm2(o, z, gn):
    r = lax.rsqrt(_half_sum(o * o) * (1.0 / HEAD) + EPS)
    return o * r, _sigmoid(z), r


def gdn_scan(loc, egl, u, gn, *, seq, name):
    t = u.shape[0]
    nc = seq // DN_CHUNK

    npair = DN_HEADS // 2

    def body(loc_ref, egl_ref, z_ref, gn_ref, y_ref, o_ref, vn_ref, st_ref):
        gn = gn_ref[...]
        bdm = _bd_mask()

        def step(c, states):
            rows = pl.ds(pl.multiple_of(c * DN_CHUNK, DN_CHUNK), DN_CHUNK)
            new = [None] * npair

            def pair(hp):
                lanes = pl.ds(hp * 2 * HEAD, 2 * HEAD)
                state = states[hp]
                st_ref[rows, lanes] = _fold(state)
                vn = loc_ref[2, rows, lanes] - _nn(loc_ref[3, rows, lanes], state)
                yield
                o = _nn(loc_ref[0, rows, lanes], state) + _pk_nn(loc_ref[4, rows, lanes], vn)
                new[hp] = state * _row0(egl_ref[rows, lanes]) + jnp.where(bdm, _tn(loc_ref[1, rows, lanes], vn), 0.0)
                yield
                vn_ref[rows, lanes] = vn
                o_ref[rows, lanes] = o
                zz = z_ref[rows, lanes]
                on, sig, _ = _gated_norm2(o, zz, gn)
                y_ref[rows, lanes] = on * gn * (zz * sig)

            _interleave([pair(hp) for hp in range(npair)])
            return tuple(new)

        lax.fori_loop(0, nc, step, tuple(jnp.zeros((2 * HEAD, 2 * HEAD), F32) for _ in range(npair)))

    width = DN_HEADS * HEAD
    rows = pl.BlockSpec((seq, width), lambda b: (b, 0))
    out = SDS((t, width), F32)
    return pl.pallas_call(
        body, name=name, grid=(t // seq,),
        in_specs=[pl.BlockSpec((5, seq, width), lambda b: (0, b, 0)), rows,
                  pl.BlockSpec((seq, width), lambda b: (b, COL_DNZ * 2 * HEAD // width)), _whole((1, 2 * HEAD))],
        out_specs=[rows] * 4, out_shape=[out] * 4, compiler_params=_params(("parallel",)))(loc, egl, u, gn)


def gdn_scan_bwd(loc, egl, u, gn, o, vn, states, dy, *, seq, name):
    t = u.shape[0]
    nc = seq // DN_CHUNK

    npair = DN_HEADS // 2

    def body(loc_ref, egl_ref, z_ref, gn_ref, o_ref, vn_ref, st_ref, dy_ref, dloc_ref, degl_ref, dz_ref, dgn_ref):
        @pl.when(pl.program_id(0) == 0)
        def _():
            dgn_ref[...] = jnp.zeros_like(dgn_ref)

        gn = gn_ref[...]
        bdm = _bd_mask()
        shape = (DN_CHUNK, 2 * HEAD)
        tril = _iota2(shape, 0) >= (_iota2(shape, 1) & (HEAD - 1))

        def step(i, carry):
            rows = pl.ds(pl.multiple_of((nc - 1 - i) * DN_CHUNK, DN_CHUNK), DN_CHUNK)
            new = [None] * npair

            def pair(hp):
                lanes = pl.ds(hp * 2 * HEAD, 2 * HEAD)
                ds, dgn = carry[hp]
                dy, zz, oo = dy_ref[rows, lanes], z_ref[rows, lanes], o_ref[rows, lanes]
                on, sig, r = _gated_norm2(oo, zz, gn)
                sz = zz * sig
                dz_ref[rows, lanes] = dy * on * gn * (sig * (1.0 + zz * (1.0 - sig)))
                dgn = dgn + jnp.sum(dy * on * sz, axis=0, keepdims=True)
                don = dy * gn * sz
                do = r * (don - on * _half_sum(don * on) * (1.0 / HEAD))
                state, vnew = _bd(st_ref[rows, lanes]), vn_ref[rows, lanes]
                qg, kd, w, amat = (loc_ref[0, rows, lanes], loc_ref[1, rows, lanes], loc_ref[3, rows, lanes],
                                   loc_ref[4, rows, lanes])
                yield
                dvn = _pk_tn(amat, do) + _nn(kd, ds)
                dloc_ref[0, rows, lanes] = _nt(do, state)
                dloc_ref[1, rows, lanes] = _nt(vnew, ds)
                yield
                dloc_ref[2, rows, lanes] = dvn
                dloc_ref[3, rows, lanes] = -_nt(dvn, state)
                dloc_ref[4, rows, lanes] = jnp.where(tril, _pk_nt(do, vnew), 0.0)
                degl = _half_sum(jnp.sum(state * ds, axis=0, keepdims=True))
                degl_ref[rows, lanes] = jnp.broadcast_to(degl, shape)
                grow = jnp.where(bdm, _tn(qg, do) - _tn(w, dvn), 0.0)
                new[hp] = (ds * _row0(egl_ref[rows, lanes]) + grow, dgn)

            _interleave([pair(hp) for hp in range(npair)])
            return tuple(new)

        init = tuple((jnp.zeros((2 * HEAD, 2 * HEAD), F32), jnp.zeros((1, 2 * HEAD), F32)) for _ in range(npair))
        out = lax.fori_loop(0, nc, step, init)
        dgn_ref[...] += sum(dgn for _, dgn in out)

    width = DN_HEADS * HEAD
    once = pl.Buffered(1)
    rows = pl.BlockSpec((seq, width), lambda b: (b, 0), pipeline_mode=once)
    out_rows = pl.BlockSpec((seq, width), lambda b: (b, 0))
    out = SDS((t, width), F32)
    return pl.pallas_call(
        body, name=name, grid=(t // seq,),
        in_specs=[pl.BlockSpec((5, seq, width), lambda b: (0, b, 0), pipeline_mode=once), rows,
                  pl.BlockSpec((seq, width), lambda b: (b, COL_DNZ * 2 * HEAD // width), pipeline_mode=once),
                  _whole((1, 2 * HEAD)), rows, rows, rows, rows],
        out_specs=[pl.BlockSpec((5, seq, width), lambda b: (0, b, 0)), out_rows, out_rows, _whole((1, 2 * HEAD))],
        out_shape=[SDS((5, t, width), F32), out, out, SDS((1, 2 * HEAD), F32)],
        compiler_params=_params(("arbitrary",), vmem=60 << 20))(loc, egl, u, gn, o, vn, states, dy)


def gdn_prep_bwd(u, cw, alog_row, dt_row, dloc, degl, *, seq, name):
    t = u.shape[0]
    nc = seq // DN_CHUNK

    def body(q_ref, k_ref, v_ref, ba_ref, cq_ref, ck_ref, cv_ref, alog_ref, dt_ref, dloc_ref, degl_ref,
             dqkv_ref, dba_ref, dcw_ref, dhs_ref, act_sc, b_sc, gc_sc, c_sc):
        hp = pl.program_id(0)

        @pl.when(pl.program_id(1) == 0)
        def _():
            dcw_ref[...] = jnp.zeros_like(dcw_ref)
            dhs_ref[...] = jnp.zeros_like(dhs_ref)

        pre_refs, cw_refs = (q_ref, k_ref, v_ref), (cq_ref, ck_ref, cv_ref)
        _gdn_inputs(pre_refs, cw_refs, ba_ref, alog_ref, dt_ref, hp, act_sc, b_sc, gc_sc, c_sc)

        def finish(rows, m, tt):
            q, k, v, b = m["q"], m["k"], m["v"], m["b"]
            dqg, dkd, du, dw, da = (dloc_ref[x, rows, :] for x in range(5))
            dm, eg = m["dm"], m["eg"]
            dt = _pk_nt(du, m["vb"]) + _pk_nt(dw, m["kbg"])
            dvb, dkbg = _pk_tn(tt, du), _pk_tn(tt, dw)
            yield
            dtt = _pk_nt(dt, tt, hi=True)
            yield
            dl = jnp.where(m["strict"], -_pk_tn(tt, dtt, hi=True), 0.0)
            yield
            dkk = dl * dm
            dqk = da * dm
            dd = dl * m["kk"] + da * m["qk"]
            dkb = _pk_nn(dkk, k) + dkbg * eg
            dq = _pk_nn(dqk, k) + dqg * eg
            yield
            dk = _pk_tn(dkk, m["kb"]) + _pk_tn(dqk, q) + dkd * m["ekd"] + dkb * b
            db = _half_sum(dkb * k + dvb * v)
            yield
            mx = jnp.where(m["tril"], dd * dm, 0.0)
            tk = _half_sum(dkd * m["kd"])
            colsum = jnp.where(m["eye"], jnp.broadcast_to(jnp.sum(mx, axis=0, keepdims=True), mx.shape), 0.0)
            dgc = _half_sum(mx) - _half_sum(colsum) + _half_sum(dqg * m["qg"] + dkbg * m["kbg"]) - tk
            dglast = jnp.sum(tk, axis=0, keepdims=True) + _row0(degl_ref[rows, :]) * jnp.exp(m["glast"])
            act_sc[0, rows, :] = dq
            act_sc[1, rows, :] = dk
            act_sc[2, rows, :] = dvb * b
            b_sc[rows, :] = db
            gc_sc[rows, :] = dgc + jnp.where(m["row"] == DN_CHUNK - 1, dglast, 0.0)

        _gdn_chunk_loop(nc, act_sc, b_sc, gc_sc, finish)

        beta, g, beta_blk, sp_arg, a_exp, g_blk = _gdn_gates(ba_ref, alog_ref, dt_ref, hp)
        dg = _chunk_rev_cumsum(gc_sc[...])
        lane = _iota2(beta_blk.shape, 1)
        ha = 2 * hp
        db = b_sc[...]
        at = lambda idx, x_a, x_b: (jnp.where(lane == idx, _lane_col(x_a, 0), 0.0)
                                    + jnp.where(lane == idx + 1, _lane_col(x_b, HEAD), 0.0))
        dg_blk = at(DN_HEADS + ha, dg, dg)
        dal = dg_blk * (-a_exp) * _sigmoid(sp_arg)
        dba_ref[...] = at(ha, db, db) * beta_blk * (1.0 - beta_blk) + dal
        dhs_ref[0:1, :] += jnp.sum(dg_blk * g_blk, axis=0, keepdims=True)
        dhs_ref[1:2, :] += jnp.sum(dal, axis=0, keepdims=True)
        for idx in range(3):
            c = c_sc[idx]
            _, sig, hat, r = _gdn_act(c, _DN_SCALE[idx])
            da_ = act_sc[idx]
            if _DN_SCALE[idx] is not None:
                da_ = da_ * _DN_SCALE[idx]
                da_ = r * (da_ - hat * _half_sum(da_ * hat))
            dx, dcw = _conv_bwd(da_ * (sig * (1.0 + c * (1.0 - sig))), pre_refs[idx][...],
                                [cw_refs[idx][k:k + 1, :] for k in range(4)])
            dqkv_ref[idx] = dx
            dcw_ref[idx] += dcw

    pair = DN_HEADS // 2
    in_specs = [_swap(s) for s in _gdn_in_specs(seq)] + [_swap(_pair(seq, 5)), _swap(_pair(seq))]
    return pl.pallas_call(
        body, name=name, grid=(pair, t // seq), in_specs=in_specs,
        out_specs=[_swap(_pair(seq, 3)), pl.BlockSpec((None, seq, 2 * HEAD), lambda hp, b: (hp, b, 0)),
                   pl.BlockSpec((3, 4, 2 * HEAD), lambda hp, b: (0, 0, hp)),
                   pl.BlockSpec((None, 2, 2 * HEAD), lambda hp, b: (hp, 0, 0))],
        out_shape=[SDS((3, t, DN_HEADS * HEAD), F32), SDS((pair, t, 2 * HEAD), F32), SDS((3, 4, DN_HEADS * HEAD), F32),
                   SDS((pair, 2, 2 * HEAD), F32)],
        scratch_shapes=[pltpu.VMEM((3, seq, 2 * HEAD), F32)] + [pltpu.VMEM((seq, 2 * HEAD), F32)] * 2
        + [pltpu.VMEM((3, seq, 2 * HEAD), F32)],
        compiler_params=_params(("arbitrary", "arbitrary")))(u, u, u, u, cw, cw, cw, alog_row, dt_row, dloc, degl)


def mix_out(y_lru, o, y_dn, w_out, h, *, name, tm=512):
    t, d = h.shape
    tm = min(tm, t)

    def body(a_ref, b_ref, c_ref, w_ref, h_ref, o_ref, y_ref):
        y_ref[:, 0:LRU_W] = a_ref[...].astype(BF16)
        y_ref[:, LRU_W:LRU_W + ATT_W] = b_ref[...].astype(BF16)
        y_ref[:, LRU_W + ATT_W:] = c_ref[...].astype(BF16)
        o_ref[...] = h_ref[...] + _nn(y_ref[...], w_ref[...])

    rows = lambda width: pl.BlockSpec((tm, width), lambda i: (i, 0))
    return pl.pallas_call(
        body, name=name, grid=(t // tm,), in_specs=[rows(LRU_W), rows(ATT_W), rows(LRU_W), _whole((d, d)), rows(d)],
        out_specs=[rows(d), rows(d)], out_shape=[SDS((t, d), F32), SDS((t, d), BF16)],
        compiler_params=_params(("parallel",)))(y_lru, o, y_dn, w_out, h)


def mix_out_bwd(dout, w_out, *, name, tm=512):
    t, d = dout.shape
    tm = min(tm, t)

    def body(d_ref, w_ref, a_ref, b_ref, c_ref):
        dy = _nt(d_ref[...], w_ref[...])
        a_ref[...] = dy[:, 0:LRU_W]
        b_ref[...] = dy[:, LRU_W:LRU_W + ATT_W]
        c_ref[...] = dy[:, LRU_W + ATT_W:]

    rows = lambda width: pl.BlockSpec((tm, width), lambda i: (i, 0))
    return pl.pallas_call(
        body, name=name, grid=(t // tm,), in_specs=[rows(d), _whole((d, d))], out_specs=[rows(LRU_W), rows(ATT_W), rows(LRU_W)],
        out_shape=[SDS((t, LRU_W), F32), SDS((t, ATT_W), F32), SDS((t, LRU_W), F32)],
        compiler_params=_params(("parallel",)))(dout, w_out)


def mix_in_bwd(h, gain, dout, w_in, dx, dgate, dq, dk, dv, dqkv, dz, dba, *, name, tm=512):
    t, d = h.shape
    tm = min(tm, t)

    def body(h_ref, g_ref, do_ref, w_ref, dx_ref, dgate_ref, dq_ref, dk_ref, dv_ref, dqkv_ref, dz_ref, dba_ref,
             dh_ref, dg_ref, du_ref):
        @pl.when(pl.program_id(0) == 0)
        def _():
            dg_ref[...] = jnp.zeros_like(dg_ref)

        off = 0
        for piece in (dx_ref[...], dgate_ref[...], dq_ref[...], dk_ref[...], dv_ref[...], dqkv_ref[0], dqkv_ref[1],
                      dqkv_ref[2], dz_ref[...], dba_ref[0] + dba_ref[1]):
            du_ref[:, off:off + piece.shape[1]] = piece.astype(BF16)
            off += piece.shape[1]
        du_ref[:, off:] = jnp.zeros((tm, D_IN_PAD - off), BF16)
        g = g_ref[...]
        _, xh, r = _rms_fwd(h_ref[...], g)
        dh, dg = _rms_bwd(_nt(du_ref[...], w_ref[...]), xh, r, g)
        dh_ref[...] = do_ref[...] + dh
        dg_ref[...] += dg

    rows = lambda width: pl.BlockSpec((tm, width), lambda i: (i, 0))
    return pl.pallas_call(
        body, name=name, grid=(t // tm,),
        in_specs=[rows(d), _whole((1, d)), rows(d), _whole((d, D_IN_PAD)), rows(LRU_W), rows(LRU_W), rows(ATT_W),
                  rows(2 * HEAD), rows(2 * HEAD), pl.BlockSpec((3, tm, DN_HEADS * HEAD), lambda i: (0, i, 0)),
                  rows(DN_HEADS * HEAD), pl.BlockSpec((2, tm, 2 * HEAD), lambda i: (0, i, 0))],
        out_specs=[rows(d), _whole((1, d)), rows(D_IN_PAD)],
        out_shape=[SDS((t, d), F32), SDS((1, d), F32), SDS((t, D_IN_PAD), BF16)],
        compiler_params=_params(("arbitrary",)))(h, gain, dout, w_in, dx, dgate, dq, dk, dv, dqkv, dz, dba)


def _block_diag(w):
    out = jnp.zeros((LRU_W, LRU_W), w.dtype)
    for h in range(LRU_W // HEAD):
        out = lax.dynamic_update_slice(out, w[h], (h * HEAD, h * HEAD))
    return out


def _diag_blocks(w):
    per = LRU_HALF // HEAD
    return jnp.stack([w[h // per, (h % per) * HEAD:(h % per + 1) * HEAD, (h % per) * HEAD:(h % per + 1) * HEAD]
                      for h in range(LRU_W // HEAD)])


def layer_params(w, wl, l, bias):
    row = lambda a: a[l].reshape(1, -1)
    return dict(
        ffn1_norm=row(w["ffn1_norm"]), ffn1=(wl["ffn1_w_gate"], wl["ffn1_w_up"], wl["ffn1_w_down"]),
        mix_norm=row(w["mix_norm"]) + wl["tie1"][0:1, 0:1], w_in=wl["w_in"],
        lru=(wl["lru_conv_w"], row(w["lru_conv_b"]), _block_diag(w["lru_w_a"][l]), row(w["lru_b_a"]),
             _block_diag(w["lru_w_x"][l]), row(w["lru_b_x"]), row(w["lru_lambda"])),
        bias=bias, sink_rows=jnp.repeat(w["attn_sinks"][l], BLOCK_Q).reshape(ATT_HEADS * BLOCK_Q, 1),
        dn_cw=wl["dn_conv_w"], dn_alog=_ba_row(w["dn_a_log"][l]), dn_dt=_ba_row(w["dn_dt_bias"][l]),
        dn_norm=jnp.tile(row(w["dn_norm"]), (1, 2)), w_out=wl["w_out"],
        ffn2_norm=row(w["ffn2_norm"]), ffn2=(wl["ffn2_w_gate"], wl["ffn2_w_up"], wl["ffn2_w_down"]),
        ple_norm=row(w["ple_norm"]), ple_w_gate=wl["ple_w_gate"], ple_w_proj=wl["ple_w_proj"])


def _ba_row(per_head):
    return jnp.pad(per_head, (DN_HEADS, 2 * HEAD - 2 * DN_HEADS)).reshape(1, 2 * HEAD)


def mixer_fwd(h, p, nb, seq, tag):
    u, n = norm_matmul(h, p["mix_norm"], p["w_in"], tn=D_IN_PAD // 2, name=f"mix_in_{tag}")
    y_lru = lru_fwd(u, *p["lru"], seq=seq, name=f"lru_fwd_{tag}")
    o = swa_fwd(u, p["bias"], p["sink_rows"], seq=seq, name=f"swa_fwd_{tag}")
    loc, egl = gdn_prep(u, p["dn_cw"], p["dn_alog"], p["dn_dt"], seq=seq, name=f"gdn_prep_{tag}")
    y_dn, o_raw, vn, st = gdn_scan(loc, egl, u, p["dn_norm"], seq=seq, name=f"gdn_scan_{tag}")
    out, ycat = mix_out(y_lru, o, y_dn, p["w_out"], h, name=f"mix_out_{tag}")
    return out, dict(h=h, u=u, n=n, loc=loc, egl=egl, o_raw=o_raw, vn=vn, st=st, ycat=ycat)


def mixer_bwd(dout, s, p, nb, seq, tag):
    u = s["u"]
    dy_lru, do, dy_dn = mix_out_bwd(dout, p["w_out"], name=f"mix_out_dx_{tag}")
    g = {"w_out": matmul(s["ycat"], dout, ta=True, tm=1024, tk=1024, name=f"mix_out_dw_{tag}")}
    dx, dgate, dcw, dwa, dwx, dvec = lru_bwd(u, *p["lru"], dy_lru, seq=seq, name=f"lru_bwd_{tag}")
    g.update(lru_conv_w=dcw, lru_conv_b=dvec[0], lru_w_a=_diag_blocks(dwa), lru_b_a=dvec[1], lru_w_x=_diag_blocks(dwx),
             lru_b_x=dvec[2], lru_lambda=dvec[3])
    dq, dk, dv, dbias, dsink = swa_bwd(u, p["bias"], p["sink_rows"], do, seq=seq, name=f"swa_bwd_{tag}")
    g.update(attn_sinks=dsink.reshape(ATT_HEADS, BLOCK_Q).sum(axis=1), bias=dbias)
    dloc, degl, dz, dgn = gdn_scan_bwd(s["loc"], s["egl"], u, p["dn_norm"], s["o_raw"], s["vn"], s["st"], dy_dn, seq=seq,
                                       name=f"gdn_scan_bwd_{tag}")
    dqkv, dba, dcw3, dhs = gdn_prep_bwd(u, p["dn_cw"], p["dn_alog"], p["dn_dt"], dloc, degl, seq=seq,
                                        name=f"gdn_prep_bwd_{tag}")
    dhs = dhs.sum(axis=0)[:, DN_HEADS:2 * DN_HEADS]
    g.update(dn_conv_w=dcw3.transpose(1, 0, 2).reshape(4, 3 * DN_HEADS * HEAD), dn_a_log=dhs[0], dn_dt_bias=dhs[1],
             dn_norm=dgn[0, :HEAD] + dgn[0, HEAD:])
    dh, dgain, du = mix_in_bwd(s["h"], p["mix_norm"], dout, p["w_in"], dx, dgate, dq, dk, dv, dqkv, dz, dba,
                               name=f"mix_in_bwd_{tag}")
    g["w_in"] = matmul(s["n"], du, ta=True, tm=1024, tn=640, tk=1024, name=f"mix_in_dw_{tag}")
    g["mix_norm"] = dgain[0]
    return dh, g


SHARDED = ("ffn1_w_gate", "ffn1_w_up", "ffn1_w_down", "w_in", "w_out", "ffn2_w_gate", "ffn2_w_up", "ffn2_w_down",
           "ple_w_gate", "ple_w_proj")
PER_LAYER_SMALL = ("ffn1_norm", "mix_norm", "lru_conv_w", "lru_conv_b", "lru_w_a", "lru_b_a", "lru_w_x", "lru_b_x",
                   "lru_lambda", "attn_sinks", "dn_conv_w", "dn_a_log", "dn_dt_bias", "dn_norm", "ffn2_norm", "ple_norm")


GRAD_PARTS = (("ple_w_gate", "ple_w_proj", "ffn2_w_gate", "ffn2_w_up", "ffn2_w_down"), ("w_in", "w_out"),
              ("ffn1_w_gate", "ffn1_w_up", "ffn1_w_down"))
WEIGHT_PARTS = (("ffn1_w_gate", "ffn1_w_up", "ffn1_w_down"),
                ("w_in", "w_out", "ffn2_w_gate", "ffn2_w_up", "ffn2_w_down", "ple_w_gate", "ple_w_proj", "lru_conv_w",
                 "dn_conv_w"))


def _col_shards(a):
    r, c = a.shape
    return a.reshape(r, N_CHIP, c // N_CHIP).transpose(1, 0, 2)


def local_step(x, p, target, w, layer_weights, layer_grads, bmap, nb, seq):
    bias = relbias_fwd(w["rel_bias"], bmap, name="relbias_fwd")
    h, saved = x, []
    for l in range(N_LAYER):
        wl = layer_weights(l, 0, h)
        s = dict(h0=h)
        h, *s["ffn1"] = ffn_fwd(h, w["ffn1_norm"][l].reshape(1, -1) + wl["tie0"][0:1, 0:1], wl["ffn1_w_gate"],
                                wl["ffn1_w_up"], wl["ffn1_w_down"], name=f"ffn1_fwd_{l}")
        wl.update(layer_weights(l, 1, h))
        pr = layer_params(w, wl, l, bias)
        h, s["mix"] = mixer_fwd(h, pr, nb, seq, l)
        s["h2"] = h
        h, *s["ffn2"] = ffn_fwd(h, pr["ffn2_norm"], *pr["ffn2"], name=f"ffn2_fwd_{l}")
        s["h3"] = h
        h = ple_fwd(h, pr["ple_norm"], pr["ple_w_gate"], p[l], pr["ple_w_proj"], name=f"ple_fwd_{l}")
        saved.append((pr, s))
    dh, dgf, loss = loss_head(h, w["final_norm"].reshape(1, -1), target, name="loss_head")

    per_layer, dbias, token = [None] * N_LAYER, None, None
    for l in reversed(range(N_LAYER)):
        pr, s = saved[l]
        g = {}
        dout = dh
        ple_norm = pr["ple_norm"] if token is None else pr["ple_norm"] + token[0:1, 0:1]
        dh, n, dga, dpp, dg = ple_bwd(s["h3"], ple_norm, pr["ple_w_gate"], p[l], pr["ple_w_proj"], dout, name=f"ple_bwd_{l}")
        g["ple_norm"] = dg[0]
        g["ple_w_gate"] = matmul(n, dga, ta=True, tm=1024, tk=1024, name=f"ple_dwg_{l}").reshape(N_CHIP, -1, D_MODEL)
        g["ple_w_proj"] = _col_shards(matmul(p[l], dpp, ta=True, name=f"ple_dwp_{l}"))
        for nm, hin in (("ffn2", s["h2"]), ("ffn1", s["h0"])):
            if nm == "ffn1":
                lru = list(pr["lru"])
                lru[1] = lru[1] + token[0:1, 0:1]
                dh, gm = mixer_bwd(dh, s["mix"], dict(pr, lru=tuple(lru)), nb, seq, l)
                dbias = gm.pop("bias") if dbias is None else dbias + gm.pop("bias")
                gm["w_in"] = _col_shards(gm["w_in"][:, :D_IN])
                gm["w_out"] = gm["w_out"].reshape(N_CHIP, -1, D_MODEL)
                g.update(gm)
                token = layer_grads(l, 1, {k: g.pop(k) for k in GRAD_PARTS[1]}, dh)
            dout = dh
            n, a, b = s[nm]
            dh, da, db, sact, dg = ffn_bwd_act(hin, pr[nm + "_norm"] + token[0:1, 0:1] if nm == "ffn1" else pr[nm + "_norm"],
                                               dout, a, b, *pr[nm], name=f"{nm}_bwd_act_{l}")
            g[nm + "_norm"] = dg[0]
            g[nm + "_w_gate"], g[nm + "_w_up"], g[nm + "_w_down"] = ffn_bwd_w(n, da, db, sact, dout, name=f"{nm}_bwd_w_{l}")
            part = 0 if nm == "ffn2" else 2
            token = layer_grads(l, part, {k: g.pop(k) for k in GRAD_PARTS[part]}, dh)
        per_layer[l] = g
    grads = {k: jnp.stack([per_layer[l][k] for l in range(N_LAYER)]) for k in PER_LAYER_SMALL}
    grads["rel_bias"] = relbias_bwd(dbias, bmap, name="relbias_bwd")[:, :ATT_HEADS]
    grads["final_norm"] = dgf[0]
    return loss, dh, grads


HBM_SPEC = pl.BlockSpec(memory_space=pltpu.HBM)


def _place():
    x, y, c = lax.axis_index("x"), lax.axis_index("y"), lax.axis_index("c")
    chips = [(1 - x, y), (x, 1 - y), (1 - x, 1 - y)]
    return x, y, c, 2 * x + y, (x, y, 1 - c), chips, [2 * cx + cy for cx, cy in chips]


def _remote(src, dst, send_sem, recv_sem, to):
    return pltpu.make_async_remote_copy(src_ref=src, dst_ref=dst, send_sem=send_sem, recv_sem=recv_sem, device_id=to,
                                        device_id_type=MESH)


N_DEV = 8


def allreduce_small(buf, *, name):
    rows = buf.shape[0]

    def body(in_ref, out_ref, gath, send, recv):
        x, y, c = lax.axis_index("x"), lax.axis_index("y"), lax.axis_index("c")
        mine = 4 * x + 2 * y + c
        gath[mine] = in_ref[...]
        cps = []
        for k in range(1, N_DEV):
            to = (x ^ (k >> 2), y ^ ((k >> 1) & 1), c ^ (k & 1))
            cps.append(_remote(in_ref, gath.at[mine], send.at[k - 1], recv.at[k - 1], to))
            cps[-1].start()
        for k in range(1, N_DEV):
            theirs = gath.at[4 * (x ^ (k >> 2)) + 2 * (y ^ ((k >> 1) & 1)) + (c ^ (k & 1))]
            _remote(theirs, theirs, send.at[k - 1], recv.at[k - 1], (x, y, c)).wait_recv()
        for cp in cps:
            cp.wait_send()
        acc = gath[0]
        for d in range(1, N_DEV):
            acc = acc + gath[d]
        out_ref[...] = acc

    vm = pl.BlockSpec(memory_space=pltpu.VMEM)
    return pl.pallas_call(
        body, name=name, in_specs=[vm], out_specs=vm, out_shape=SDS(buf.shape, F32),
        scratch_shapes=[pltpu.VMEM((N_DEV, rows, 128), F32), pltpu.SemaphoreType.DMA((N_DEV - 1,)),
                        pltpu.SemaphoreType.DMA((N_DEV - 1,))])(buf)


SEM_SPEC = pl.BlockSpec(memory_space=pltpu.SEMAPHORE)
ANY_SPEC = pl.BlockSpec(memory_space=pl.ANY)
DATAFLOW = pltpu.SideEffectType.DATAFLOW_SIDE_EFFECTING


def _in_hbm(a):
    return pltpu.with_memory_space_constraint(a, pltpu.HBM)


def _my_rows(ref_rows, c, mine=True):
    half = ref_rows // 2
    start = (c if mine else 1 - c) * half
    return pl.ds(pl.multiple_of(start, 8), half)


def place_layer_shard(ws, layer, chip_arr, dtype, after, *, name):
    n = len(ws)
    _, r, c = ws[0].shape
    tr = next(cand for cand in (352, 256, 128, 64, 32, 16, 8, r) if r % cand == 0)

    def body(chip_ref, *refs):
        for w_ref, o_ref in zip(refs[:n], refs[n + 1:]):
            o_ref[...] = w_ref[...].astype(dtype)

    return list(pl.pallas_call(
        body, name=name,
        grid_spec=pltpu.PrefetchScalarGridSpec(
            num_scalar_prefetch=1, grid=(r // tr,),
            in_specs=[pl.BlockSpec((None, tr, c), lambda i, chip: (layer, i, 0))] * n + [ANY_SPEC],
            out_specs=[pl.BlockSpec((None, tr, c), lambda i, chip: (chip[0], i, 0))] * n),
        out_shape=[SDS((N_CHIP, r, c), dtype)] * n, compiler_params=_params(("parallel",)))(chip_arr, *ws, after))


def _gather_pieces(refs, n_split, c, me, cids):
    mine, theirs = [], []
    for k, ref in enumerate(refs):
        if k < n_split:
            rows = _my_rows(ref.shape[1], c)
            mine.append(ref.at[me, rows])
            theirs.append([ref.at[cid, rows] for cid in cids])
        else:
            mine.append(ref.at[me])
            theirs.append([ref.at[cid] for cid in cids])
    return mine, theirs


def gather_start(bufs, n_split, after, *, name):
    n = len(bufs)

    def body(*refs):
        ins, send, recv, token = refs[:n], refs[n + 1], refs[n + 2], refs[-1]
        x, y, c, me, sib, chips, cids = _place()
        mine, _ = _gather_pieces(ins, n_split, c, me, cids)
        for k in range(n):
            for j, chip in enumerate(chips):
                _remote(mine[k], mine[k], send.at[3 * k + j], recv.at[3 * k + j], (*chip, c)).start()
        token[...] = jnp.zeros_like(token)

    out = pl.pallas_call(
        body, name=name, in_specs=[HBM_SPEC] * n + [ANY_SPEC],
        out_specs=[SEM_SPEC, SEM_SPEC] + [HBM_SPEC] * n + [pl.BlockSpec(memory_space=pltpu.VMEM)],
        out_shape=[pltpu.SemaphoreType.DMA((3 * n,)), pltpu.SemaphoreType.DMA((3 * n,))]
        + [pltpu.HBM(b.shape, b.dtype) for b in bufs] + [SDS((8, 128), F32)],
        input_output_aliases={k: k + 2 for k in range(n)},
        compiler_params=pltpu.CompilerParams(has_side_effects=DATAFLOW))(*[_in_hbm(b) for b in bufs], after)
    return out[0], out[1], list(out[2:2 + n]), out[-1]


def gather_wait(send, recv, bufs, n_split, after, *, name):
    n = len(bufs)

    def body(*refs):
        ins, send_ref, recv_ref = refs[:n], refs[n], refs[n + 1]
        x, y, c, me, sib, chips, cids = _place()
        mine, theirs = _gather_pieces(ins, n_split, c, me, cids)
        for k in range(n):
            for j in range(3):
                _remote(mine[k], mine[k], send_ref.at[3 * k + j], recv_ref.at[3 * k + j], sib).wait_send()
                _remote(theirs[k][j], theirs[k][j], send_ref.at[3 * k + j], recv_ref.at[3 * k + j], sib).wait_recv()

    return list(pl.pallas_call(
        body, name=name, in_specs=[HBM_SPEC] * n + [SEM_SPEC, SEM_SPEC, ANY_SPEC], out_specs=[HBM_SPEC] * n,
        out_shape=[pltpu.HBM(b.shape, b.dtype) for b in bufs], input_output_aliases={k: k for k in range(n)},
        compiler_params=pltpu.CompilerParams(has_side_effects=DATAFLOW))(*bufs, send, recv, after))


def gather_forward(bufs, *, name):
    n = len(bufs)

    def body(*refs):
        outs, (send, recv) = refs[n:2 * n], refs[2 * n:]
        x, y, c, me, sib, chips, cids = _place()
        cps = []
        for k in range(n):
            for j in range(3):
                piece = outs[k].at[cids[j], _my_rows(outs[k].shape[1], c)]
                cps.append(_remote(piece, piece, send.at[3 * k + j], recv.at[3 * k + j], sib))
                cps[-1].start()
        for k in range(n):
            for j in range(3):
                piece = outs[k].at[cids[j], _my_rows(outs[k].shape[1], c, mine=False)]
                _remote(piece, piece, send.at[3 * k + j], recv.at[3 * k + j], sib).wait_recv()
        for cp in cps:
            cp.wait_send()

    return list(pl.pallas_call(
        body, name=name, in_specs=[HBM_SPEC] * n, out_specs=[HBM_SPEC] * n, out_shape=[SDS(b.shape, b.dtype) for b in bufs],
        input_output_aliases={k: k for k in range(n)}, scratch_shapes=[pltpu.SemaphoreType.DMA((3 * n,))] * 2)(*bufs))


def _exchange_copies(ins, lands, send, recv, c, sib):
    return [_remote(ins[k].at[pl.ds(0, N_CHIP), _my_rows(ins[k].shape[1], c, mine=False)], lands[k], send.at[k],
                    recv.at[k], sib) for k in range(len(ins))]


def exchange_start(gs, *, name):
    n = len(gs)

    def body(*refs):
        ins, lands, send, recv, token = refs[:n], refs[n:2 * n], refs[2 * n], refs[2 * n + 1], refs[-1]
        x, y, c, me, sib, chips, cids = _place()
        for cp in _exchange_copies(ins, lands, send, recv, c, sib):
            cp.start()
        token[...] = jnp.zeros_like(token)

    lands = [_in_hbm(lax.empty((N_CHIP, g.shape[1] // 2, g.shape[2]), g.dtype)) for g in gs]
    out = pl.pallas_call(
        body, name=name, in_specs=[HBM_SPEC] * (2 * n),
        out_specs=[SEM_SPEC, SEM_SPEC] + [HBM_SPEC] * (2 * n) + [pl.BlockSpec(memory_space=pltpu.VMEM)],
        out_shape=[pltpu.SemaphoreType.DMA((n,)), pltpu.SemaphoreType.DMA((n,))]
        + [pltpu.HBM(b.shape, b.dtype) for b in list(gs) + lands] + [SDS((8, 128), F32)],
        input_output_aliases={k: k + 2 for k in range(2 * n)},
        compiler_params=pltpu.CompilerParams(has_side_effects=DATAFLOW))(*[_in_hbm(g) for g in gs], *lands)
    return out[0], out[1], list(out[2:2 + n]), list(out[2 + n:2 + 2 * n]), out[-1]


def exchange_wait(send, recv, gs, lands, after, *, name):
    n = len(gs)

    def body(*refs):
        ins, land_refs, send_ref, recv_ref = refs[:n], refs[n:2 * n], refs[2 * n], refs[2 * n + 1]
        x, y, c, me, sib, chips, cids = _place()
        for cp in _exchange_copies(ins, land_refs, send_ref, recv_ref, c, sib):
            cp.wait_send()
            cp.wait_recv()

    out = pl.pallas_call(
        body, name=name, in_specs=[HBM_SPEC] * (2 * n) + [SEM_SPEC, SEM_SPEC, ANY_SPEC], out_specs=[HBM_SPEC] * (2 * n),
        out_shape=[pltpu.HBM(b.shape, b.dtype) for b in list(gs) + list(lands)],
        input_output_aliases={k: k for k in range(2 * n)},
        compiler_params=pltpu.CompilerParams(has_side_effects=DATAFLOW))(*gs, *lands, send, recv, after)
    return list(out[:n]), list(out[n:])


def _half_tile(half):
    return next(cand for cand in (256, 176, 128, 64, 32, 16) if half % cand == 0)


def _same_shape_runs(arrays):
    runs = []
    for i, a in enumerate(arrays):
        if runs and arrays[runs[-1][-1]].shape == a.shape:
            runs[-1].append(i)
        else:
            runs.append([i])
    return runs


def reduce_add(gs, rs, c_arr, *, name):
    n = len(gs)
    _, rows, cdim = gs[0].shape
    half = rows // 2
    tr = _half_tile(half)

    def body(c_ref, *refs):
        for g_ref, r_ref, o_ref in zip(refs[:n], refs[n:2 * n], refs[2 * n:]):
            o_ref[...] = (g_ref[...] + r_ref[...]).astype(o_ref.dtype)

    mine = pl.BlockSpec((None, tr, cdim), lambda j, i, c: (j, c[0] * (half // tr) + i, 0))
    blk = pl.BlockSpec((None, tr, cdim), lambda j, i, c: (j, i, 0))
    return list(pl.pallas_call(
        body, name=name,
        grid_spec=pltpu.PrefetchScalarGridSpec(
            num_scalar_prefetch=1, grid=(N_CHIP, half // tr), in_specs=[mine] * n + [blk] * n, out_specs=[blk] * n),
        out_shape=[SDS((N_CHIP, half, cdim), BF16)] * n, compiler_params=_params(("parallel", "parallel")))(c_arr, *gs, *rs))


def reduce_start(ss, *, name):
    n = len(ss)

    def body(*refs):
        ins, lands, send, recv, token = refs[:n], refs[n:2 * n], refs[2 * n], refs[2 * n + 1], refs[-1]
        x, y, c, me, sib, chips, cids = _place()
        for k in range(n):
            for j, chip in enumerate(chips):
                _remote(ins[k].at[cids[j]], lands[k].at[j], send.at[3 * k + j], recv.at[3 * k + j], (*chip, c)).start()
        token[...] = jnp.zeros_like(token)

    lands = [_in_hbm(lax.empty((N_CHIP - 1,) + s.shape[1:], s.dtype)) for s in ss]
    out = pl.pallas_call(
        body, name=name, in_specs=[HBM_SPEC] * (2 * n),
        out_specs=[SEM_SPEC, SEM_SPEC] + [HBM_SPEC] * (2 * n) + [pl.BlockSpec(memory_space=pltpu.VMEM)],
        out_shape=[pltpu.SemaphoreType.DMA((3 * n,)), pltpu.SemaphoreType.DMA((3 * n,))]
        + [pltpu.HBM(b.shape, b.dtype) for b in list(ss) + lands] + [SDS((8, 128), F32)],
        input_output_aliases={k: k + 2 for k in range(2 * n)},
        compiler_params=pltpu.CompilerParams(has_side_effects=DATAFLOW))(*[_in_hbm(s) for s in ss], *lands)
    return out[0], out[1], list(out[2:2 + n]), list(out[2 + n:2 + 2 * n]), out[-1]


def reduce_wait(send, recv, ss, lands, after, *, name):
    n = len(ss)

    def body(*refs):
        ins, land_refs, send_ref, recv_ref = refs[:n], refs[n:2 * n], refs[2 * n], refs[2 * n + 1]
        x, y, c, me, sib, chips, cids = _place()
        for k in range(n):
            for j in range(3):
                _remote(ins[k].at[cids[j]], land_refs[k].at[j], send_ref.at[3 * k + j], recv_ref.at[3 * k + j],
                        sib).wait_send()
                _remote(ins[k].at[cids[j]], land_refs[k].at[j], send_ref.at[3 * k + j], recv_ref.at[3 * k + j],
                        sib).wait_recv()

    out = pl.pallas_call(
        body, name=name, in_specs=[HBM_SPEC] * (2 * n) + [SEM_SPEC, SEM_SPEC, ANY_SPEC], out_specs=[HBM_SPEC] * (2 * n),
        out_shape=[pltpu.HBM(b.shape, b.dtype) for b in list(ss) + list(lands)],
        input_output_aliases={k: k for k in range(2 * n)},
        compiler_params=pltpu.CompilerParams(has_side_effects=DATAFLOW))(*ss, *lands, send, recv, after)
    return list(out[:n]), list(out[n:])


def reduce_sum(owns, lands, place_arr, layer, accs, *, name):
    n = len(owns)
    _, half, cdim = lands[0].shape
    tr = _half_tile(half)
    have = accs[0] is not None

    def body(p_ref, *refs):
        for own_ref, land_ref, o_ref in zip(refs[:n], refs[n:2 * n], refs[-n:]):
            o_ref[...] = (((own_ref[...].astype(F32) + land_ref[0].astype(F32)) + land_ref[1].astype(F32))
                          + land_ref[2].astype(F32))

    in_specs = ([pl.BlockSpec((None, tr, cdim), lambda i, p: (p[0], i, 0))] * n
                + [pl.BlockSpec((N_CHIP - 1, tr, cdim), lambda i, p: (0, i, 0))] * n + ([ANY_SPEC] * n if have else []))
    out_spec = pl.BlockSpec((None, tr, cdim), lambda i, p: (layer, p[1] * (half // tr) + i, 0))
    return list(pl.pallas_call(
        body, name=name,
        grid_spec=pltpu.PrefetchScalarGridSpec(
            num_scalar_prefetch=1, grid=(half // tr,), in_specs=in_specs, out_specs=[out_spec] * n),
        out_shape=[SDS((N_LAYER, 2 * half, cdim), F32)] * n,
        input_output_aliases={1 + 2 * n + i: i for i in range(n)} if have else {},
        compiler_params=_params(("parallel",)))(place_arr, *owns, *lands, *(accs if have else [])))


def reduce_share(fs, *, name):
    n = len(fs)

    def body(*refs):
        outs, (send, recv) = refs[n:2 * n], refs[2 * n:]
        x, y, c, me, sib, chips, cids = _place()
        cps = []
        for k in range(n):
            piece = outs[k].at[pl.ds(0, N_LAYER), _my_rows(outs[k].shape[1], c)]
            cps.append(_remote(piece, piece, send.at[k], recv.at[k], sib))
            cps[-1].start()
        for k in range(n):
            theirs = outs[k].at[pl.ds(0, N_LAYER), _my_rows(outs[k].shape[1], c, mine=False)]
            _remote(theirs, theirs, send.at[k], recv.at[k], sib).wait_recv()
        for cp in cps:
            cp.wait_send()

    return list(pl.pallas_call(
        body, name=name, in_specs=[HBM_SPEC] * n, out_specs=[HBM_SPEC] * n, out_shape=[SDS(f.shape, f.dtype) for f in fs],
        input_output_aliases={k: k for k in range(n)}, scratch_shapes=[pltpu.SemaphoreType.DMA((n,))] * 2)(*fs))


WEIGHTS = ("ffn1_norm", "ffn1_w_gate", "ffn1_w_up", "ffn1_w_down", "mix_norm", "w_in", "lru_conv_w", "lru_conv_b", "lru_w_a",
           "lru_b_a", "lru_w_x", "lru_b_x", "lru_lambda", "attn_sinks", "rel_bias", "dn_conv_w", "dn_a_log", "dn_dt_bias",
           "dn_norm", "w_out", "ffn2_norm", "ffn2_w_gate", "ffn2_w_up", "ffn2_w_down", "ple_norm", "ple_w_gate",
           "ple_w_proj", "final_norm")
CONV_SHARDED = ("lru_conv_w", "dn_conv_w")
FFN_TRANSPOSED = ("ffn1_w_gate", "ffn1_w_up", "ffn2_w_gate", "ffn2_w_up")
SMALL = tuple(k for k in WEIGHTS if k not in SHARDED)


def _pack(arrs):
    blocks = []
    for a in arrs:
        v = a.reshape(-1)
        blocks.append(jnp.pad(v, (0, -v.shape[0] % 1024)).reshape(-1, 128))
    return jnp.concatenate(blocks, axis=0)


def _unpack(buf, shapes):
    out, off = [], 0
    for s in shapes:
        n = int(np.prod(s))
        rows = 8 * -(-n // 1024)
        out.append(buf[off:off + rows].reshape(-1)[:n].reshape(s))
        off += rows
    return out


def kernel(x, p, ffn1_norm, ffn1_w_gate, ffn1_w_up, ffn1_w_down, mix_norm, w_in, lru_conv_w, lru_conv_b, lru_w_a, lru_b_a, lru_w_x, lru_b_x, lru_lambda, attn_sinks, rel_bias, dn_conv_w, dn_a_log, dn_dt_bias, dn_norm, w_out, ffn2_norm, ffn2_w_gate, ffn2_w_up, ffn2_w_down, ple_norm, ple_w_gate, ple_w_proj, final_norm, loss_target, m_ffn1_norm, m_ffn1_w_gate, m_ffn1_w_up, m_ffn1_w_down, m_mix_norm, m_w_in, m_lru_conv_w, m_lru_conv_b, m_lru_w_a, m_lru_b_a, m_lru_w_x, m_lru_b_x, m_lru_lambda, m_attn_sinks, m_rel_bias, m_dn_conv_w, m_dn_a_log, m_dn_dt_bias, m_dn_norm, m_w_out, m_ffn2_norm, m_ffn2_w_gate, m_ffn2_w_up, m_ffn2_w_down, m_ple_norm, m_ple_w_gate, m_ple_w_proj, m_final_norm, v_ffn1_norm, v_ffn1_w_gate, v_ffn1_w_up, v_ffn1_w_down, v_mix_norm, v_w_in, v_lru_conv_w, v_lru_conv_b, v_lru_w_a, v_lru_b_a, v_lru_w_x, v_lru_b_x, v_lru_lambda, v_attn_sinks, v_rel_bias, v_dn_conv_w, v_dn_a_log, v_dn_dt_bias, v_dn_norm, v_w_out, v_ffn2_norm, v_ffn2_w_gate, v_ffn2_w_up, v_ffn2_w_down, v_ple_norm, v_ple_w_gate, v_ple_w_proj, v_final_norm):
    given = dict(locals())
    stored = lambda k, a: jnp.swapaxes(a, 1, 2) if k in FFN_TRANSPOSED else a
    ws = {k: stored(k, given[k]) for k in WEIGHTS}
    ms = {k: stored(k, given["m_" + k]) for k in WEIGHTS}
    vs = {k: stored(k, given["v_" + k]) for k in WEIGHTS}
    nb, seq, d = x.shape
    t = nb * seq
    cx, cy, cc = lax.axis_index("x"), lax.axis_index("y"), lax.axis_index("c")
    chip = 2 * cx + cy

    chip_arr = chip.astype(jnp.int32).reshape(1)
    c_arr = cc.astype(jnp.int32).reshape(1)
    place_arr = jnp.stack([chip, cc]).astype(jnp.int32)
    groups = [(l, part) for l in range(N_LAYER) for part in range(len(WEIGHT_PARTS))]
    placed, started = {}, {}

    def place_group(i, after):
        l, part = groups[i]
        ks = WEIGHT_PARTS[part]
        for run in _same_shape_runs([ws[k] for k in ks]):
            outs = place_layer_shard([ws[ks[j]] for j in run], l, chip_arr, F32 if ks[run[0]] in CONV_SHARDED else BF16,
                                     after, name=f"place_{ks[run[0]]}_{l}")
            placed.update({(l, ks[j]): o for j, o in zip(run, outs)})

    def start_group(i, after):
        l, part = groups[i]
        ks = WEIGHT_PARTS[part]
        n_split = sum(k in SHARDED for k in ks)
        started[i] = (ks, n_split) + gather_start([placed[l, k] for k in ks], n_split, after, name=f"gather_start_{l}_{part}")

    place_group(0, jnp.zeros((8, 128), F32))
    start_group(0, jnp.zeros((8, 128), F32))
    for i in range(1, len(groups)):
        place_group(i, started[0][-1])

    def layer_weights(l, part, h):
        i = groups.index((l, part))
        ks, n_split, send, recv, bufs, _ = started[i]
        bufs = gather_wait(send, recv, bufs, n_split, h, name=f"gather_wait_{l}_{part}")
        tie = jnp.zeros((8, 128), F32)
        for nxt in [j for j in range(i + 1, len(groups)) if j not in started and groups[j][0] == groups[min(i + 1, len(groups) - 1)][0]]:
            start_group(nxt, bufs[0] if nxt == i + 1 else started[nxt - 1][-1])
            tie = started[nxt][-1]
        wl = dict(zip(ks, gather_forward(bufs[:n_split], name=f"gather_forward_{l}_{part}") + bufs[n_split:]))
        for k in ("w_in", "ple_w_proj", "lru_conv_w", "dn_conv_w"):
            if k in wl:
                wl[k] = wl[k].transpose(1, 0, 2).reshape(wl[k].shape[1], -1)
        for k in ("w_out", "ple_w_gate"):
            if k in wl:
                wl[k] = wl[k].reshape(-1, wl[k].shape[-1])
        if "w_in" in wl:
            wl["w_in"] = jnp.pad(wl["w_in"], ((0, 0), (0, D_IN_PAD - D_IN)))
        wl[f"tie{part}"] = tie
        return wl

    pending, finished, tokens = [], {k: None for k in SHARDED}, []

    def finish_reduce(after):
        ks, send, recv, sums, lands, l, part = pending.pop(0)
        sums, lands = reduce_wait(send, recv, sums, lands, after, name=f"reduce_wait_{l}_{part}")
        for run in _same_shape_runs(sums):
            outs = reduce_sum([sums[i] for i in run], [lands[i] for i in run], place_arr, l, [finished[ks[i]] for i in run],
                              name=f"reduce_sum_{ks[run[0]]}_{l}")
            finished.update({ks[i]: o for i, o in zip(run, outs)})

    swapping = []

    def start_reduce(after):
        ks, send, recv, gs, theirs, l, part = swapping.pop(0)
        gs, theirs = exchange_wait(send, recv, gs, theirs, after, name=f"exchange_wait_{l}_{part}")
        sums = [None] * len(ks)
        for run in _same_shape_runs(gs):
            outs = reduce_add([gs[i] for i in run], [theirs[i] for i in run], c_arr, name=f"reduce_add_{ks[run[0]]}_{l}")
            for i, o in zip(run, outs):
                sums[i] = o
        send, recv, sums, lands, token = reduce_start(sums, name=f"reduce_start_{l}_{part}")
        pending.append((ks, send, recv, sums, lands, l, part))
        tokens.append(token)
        return token

    def layer_grads(l, part, g, dh):
        ks = GRAD_PARTS[part]
        send, recv, gs, theirs, token = exchange_start([g[k] for k in ks], name=f"exchange_start_{l}_{part}")
        swapping.append((ks, send, recv, gs, theirs, l, part))
        if len(swapping) > 1:
            token = token + start_reduce(dh)
        while len(pending) > 2:
            finish_reduce(dh)
        return token

    small_w = {k: ws[k] for k in SMALL if k not in CONV_SHARDED}
    bmap = jnp.asarray(_rel_bucket_map())
    loss, gx, grads = local_step(x.reshape(t, d), p.reshape(N_LAYER, t, PLE_DIM), loss_target.reshape(t, d), small_w,
                                 layer_weights, layer_grads, bmap, nb, seq)
    while swapping:
        start_reduce(gx)
    g_out, delta, new_m, new_v = {}, {}, {}, {}

    small_shapes = [grads[k].shape for k in SMALL]
    g_small = dict(zip(SMALL, _unpack(allreduce_small(_pack([grads[k] for k in SMALL]), name="allreduce_small"), small_shapes)))
    for k in CONV_SHARDED:
        width = ws[k].shape[-1]
        g_small[k] = lax.dynamic_slice_in_dim(g_small[k], chip * width, width, axis=2)
    g_out.update(g_small)
    shapes = [ws[k].shape for k in SMALL]
    tie = tokens[-1][0:1, 0:1]
    res = adamw([_pack([ws[k] for k in SMALL]) + tie], *[[_pack([src[k] for k in SMALL])] for src in (g_out, ms, vs)],
                name="adamw_small")
    for dst, r in zip((delta, new_m, new_v), res):
        dst.update(zip(SMALL, _unpack(r[0], shapes)))

    after = res[0][0]
    two_d = lambda a: a.reshape(-1, a.shape[-1])
    for part, ks in enumerate(GRAD_PARTS):
        while pending and pending[0][0] == ks:
            finish_reduce(after)
        g_out.update(zip(ks, reduce_share([finished[k] for k in ks], name=f"reduce_share_{part}")))
        for run in _same_shape_runs([ws[k] for k in ks]):
            names = [ks[i] for i in run]
            res = adamw(*[[two_d(src[k]) for k in names] for src in (ws, g_out, ms, vs)], name=f"adamw_{names[0]}")
            for dst, rs in zip((delta, new_m, new_v), res):
                dst.update({k: r.reshape(ws[k].shape) for k, r in zip(names, rs)})
            after = res[0][0]

    total = lax.psum(loss[0, 0], ("x", "y", "c"))
    return (total, gx.reshape(nb, seq, d), *[stored(k, out[k]) for out in (g_out, delta, new_m, new_v) for k in WEIGHTS])
```
